```python
import jax, jax.numpy as jnp
from jax import lax
import numpy as np

D_MODEL = 1024
BATCH = 8
SEQ = 4096
DEPTH = 1

CHUNK = 64
LEFT_CHUNKS = 8
BAND = (LEFT_CHUNKS + 1) * CHUNK
ATT_HEADS = 8
ATT_HEAD_DIM = 64
ATT_WIDTH = ATT_HEADS * ATT_HEAD_DIM
MAX_REL = 128
LRU_WIDTH = D_MODEL
LRU_BLOCKS = 16
LRU_BLOCK = LRU_WIDTH // LRU_BLOCKS
CONV_WIDTH = 4
LRU_C = 8.0
D_FF = 2816
N_SUB = 3
EPS = 1e-6
PROJ_SIZES = (ATT_WIDTH, ATT_WIDTH, ATT_WIDTH, LRU_WIDTH, LRU_WIDTH, D_MODEL, D_MODEL)
PROJ_WIDTH = sum(PROJ_SIZES)

kernel_name = "hybrid_chunked_attn_rglru_macaron_block"


def rmsnorm(x, g):
    xf = x.astype(jnp.float32)
    y = xf * lax.rsqrt(jnp.mean(xf * xf, axis=-1, keepdims=True) + EPS)
    return (y * g.astype(jnp.float32)).astype(x.dtype)


def modulate(h, shift, scale):
    return h * (1 + scale[:, None, :]) + shift[:, None, :]


def swiglu(h, w_gu, w_down):
    g, u = jnp.split(h @ w_gu, 2, axis=-1)
    return (jax.nn.silu(g) * u) @ w_down


def chunked_attention(q, k, v, rel_bias):
    b, s, _ = q.shape
    nc = s // CHUNK
    q = q.reshape(b, nc, CHUNK, ATT_HEADS, ATT_HEAD_DIM)
    k = k.reshape(b, nc, CHUNK, ATT_HEADS, ATT_HEAD_DIM)
    v = v.reshape(b, nc, CHUNK, ATT_HEADS, ATT_HEAD_DIM)
    pad = ((0, 0), (LEFT_CHUNKS, 0), (0, 0), (0, 0), (0, 0))
    kp, vp = jnp.pad(k, pad), jnp.pad(v, pad)
    kb = jnp.concatenate([kp[:, j:j + nc] for j in range(LEFT_CHUNKS + 1)], axis=2)
    vb = jnp.concatenate([vp[:, j:j + nc] for j in range(LEFT_CHUNKS + 1)], axis=2)
    scores = jnp.einsum('bnqhd,bnkhd->bhnqk', q, kb).astype(jnp.float32) * (ATT_HEAD_DIM ** -0.5)
    qi = jnp.arange(CHUNK)[:, None]
    kj = jnp.arange(BAND)[None, :]
    rel = jnp.clip(qi - kj + LEFT_CHUNKS * CHUNK, -MAX_REL, MAX_REL) + MAX_REL
    bias = rel_bias.astype(jnp.float32)[:, rel]
    scores = scores + bias[None, :, None, :, :]
    valid = (jnp.arange(nc)[:, None] - LEFT_CHUNKS + jnp.arange(BAND)[None, :] // CHUNK) >= 0
    scores = jnp.where(valid[None, None, :, None, :], scores, jnp.finfo(jnp.float32).min)
    p = jax.nn.softmax(scores, axis=-1).astype(v.dtype)
    out = jnp.einsum('bhnqk,bnkhd->bnqhd', p, vb)
    return out.reshape(b, s, ATT_WIDTH)


def causal_depthwise_conv(x, w, bias):
    rhs = w[:, None, :]
    y = lax.conv_general_dilated(x, rhs, window_strides=(1,), padding=[(CONV_WIDTH - 1, 0)],
                                 dimension_numbers=('NWC', 'WIO', 'NWC'),
                                 feature_group_count=x.shape[-1])
    return y + bias


def block_diag_linear(x, w, bias):
    b, s, _ = x.shape
    xb = x.reshape(b, s, LRU_BLOCKS, LRU_BLOCK)
    return jnp.einsum('bsnk,nkj->bsnj', xb, w).reshape(b, s, LRU_WIDTH) + bias


def rg_lru(x, w_a, b_a, w_x, b_x, lam):
    r = jax.nn.sigmoid(block_diag_linear(x, w_a, b_a).astype(jnp.float32))
    i = jax.nn.sigmoid(block_diag_linear(x, w_x, b_x).astype(jnp.float32))
    log_a = -LRU_C * r * jax.nn.softplus(-lam.astype(jnp.float32))
    a = jnp.exp(log_a)
    mult = jnp.sqrt(-jnp.expm1(2.0 * log_a))
    u = mult * (i * x.astype(jnp.float32))

    def combine(left, right):
        a1, b1 = left
        a2, b2 = right
        return a1 * a2, a2 * b1 + b2

    _, h = lax.associative_scan(combine, (a, u), axis=1)
    return h.astype(x.dtype)


def mixer(h, w_in, rel_bias, conv_w, conv_b, lru_wa, lru_ba, lru_wx, lru_bx, lru_lambda,
          w_att_o, w_rec_o, w_out):
    proj = h @ w_in
    idx = [int(v) for v in np.cumsum(PROJ_SIZES)[:-1]]
    q, k, v, xr, yr, g_att, g_rec = jnp.split(proj, idx, axis=-1)
    att = chunked_attention(q, k, v, rel_bias) @ w_att_o
    xr = causal_depthwise_conv(xr, conv_w, conv_b)
    rec = (rg_lru(xr, lru_wa, lru_ba, lru_wx, lru_bx, lru_lambda) * jax.nn.gelu(yr)) @ w_rec_o
    merged = jax.nn.sigmoid(g_att) * att + jax.nn.sigmoid(g_rec) * rec
    return merged @ w_out


def sandwich(x, fn, g_pre, g_post, shift, scale, gate, res_w):
    h = modulate(rmsnorm(x, g_pre), shift, scale)
    y = rmsnorm(fn(h), g_post)
    return x + res_w * gate[:, None, :] * y


def _fwd_setup_inputs(seed: int = 0) -> dict:
    key = jax.random.key(seed)
    ks = jax.random.split(key, 24)
    L, D, F, W = DEPTH, D_MODEL, D_FF, LRU_WIDTH
    nrm = lambda k, shape, fan_in: jax.random.normal(k, shape, jnp.float32) * (fan_in ** -0.5)
    u = jax.random.uniform(ks[20], (L, W), jnp.float32, 0.9, 0.999)
    p = u ** (1.0 / LRU_C)
    lam = jnp.log(p) - jnp.log1p(-p)
    return {
        "x": jax.random.normal(ks[0], (BATCH, SEQ, D), jnp.float32),
        "c": jax.random.normal(ks[1], (BATCH, D), jnp.float32),
        "w_ada": nrm(ks[2], (L, D, N_SUB * 3 * D), D) * 0.5,
        "b_ada": 0.02 * jax.random.normal(ks[3], (L, N_SUB * 3 * D), jnp.float32),
        "norm_pre": 1.0 + 0.05 * jax.random.normal(ks[4], (L, N_SUB, D), jnp.float32),
        "norm_post": 1.0 + 0.05 * jax.random.normal(ks[5], (L, N_SUB, D), jnp.float32),
        "ffn1_w_gu": nrm(ks[6], (L, D, 2 * F), D),
        "ffn1_w_down": nrm(ks[7], (L, F, D), F),
        "w_in": nrm(ks[8], (L, D, PROJ_WIDTH), D),
        "rel_bias": 0.5 * jax.random.normal(ks[9], (L, ATT_HEADS, 2 * MAX_REL + 1), jnp.float32),
        "conv_w": nrm(ks[10], (L, CONV_WIDTH, W), CONV_WIDTH),
        "conv_b": 0.02 * jax.random.normal(ks[11], (L, W), jnp.float32),
        "lru_wa": nrm(ks[12], (L, LRU_BLOCKS, LRU_BLOCK, LRU_BLOCK), LRU_BLOCK),
        "lru_ba": 0.02 * jax.random.normal(ks[13], (L, W), jnp.float32),
        "lru_wx": nrm(ks[14], (L, LRU_BLOCKS, LRU_BLOCK, LRU_BLOCK), LRU_BLOCK),
        "lru_bx": 0.02 * jax.random.normal(ks[15], (L, W), jnp.float32),
        "lru_lambda": lam,
        "w_att_o": nrm(ks[16], (L, ATT_WIDTH, D), ATT_WIDTH),
        "w_rec_o": nrm(ks[17], (L, W, D), W),
        "w_out": nrm(ks[18], (L, D, D), D),
        "ffn2_w_gu": nrm(ks[19], (L, D, 2 * F), D),
        "ffn2_w_down": nrm(ks[21], (L, F, D), F),
    }


def _fwd_reference(x, c, w_ada, b_ada, norm_pre, norm_post, ffn1_w_gu, ffn1_w_down, w_in, rel_bias,
              conv_w, conv_b, lru_wa, lru_ba, lru_wx, lru_bx, lru_lambda, w_att_o, w_rec_o,
              w_out, ffn2_w_gu, ffn2_w_down):
    b = x.shape[0]
    c_act = jax.nn.silu(c)
    for l in range(DEPTH):
        mod = (c_act @ w_ada[l] + b_ada[l]).reshape(b, N_SUB, 3, D_MODEL)
        ffn1 = lambda h: swiglu(h, ffn1_w_gu[l], ffn1_w_down[l])
        mix = lambda h: mixer(h, w_in[l], rel_bias[l], conv_w[l], conv_b[l], lru_wa[l], lru_ba[l],
                              lru_wx[l], lru_bx[l], lru_lambda[l], w_att_o[l], w_rec_o[l], w_out[l])
        ffn2 = lambda h: swiglu(h, ffn2_w_gu[l], ffn2_w_down[l])
        x = sandwich(x, ffn1, norm_pre[l, 0], norm_post[l, 0], mod[:, 0, 0], mod[:, 0, 1], mod[:, 0, 2], 0.5)
        x = sandwich(x, mix, norm_pre[l, 1], norm_post[l, 1], mod[:, 1, 0], mod[:, 1, 1], mod[:, 1, 2], 1.0)
        x = sandwich(x, ffn2, norm_pre[l, 2], norm_post[l, 2], mod[:, 2, 0], mod[:, 2, 1], mod[:, 2, 2], 0.5)
    return x


import jax as _jax
import jax.numpy as _jnp

TWIN_FORMAT = 'train_step'
FWD_PARAMS = ['x', 'c', 'w_ada', 'b_ada', 'norm_pre', 'norm_post', 'ffn1_w_gu', 'ffn1_w_down', 'w_in', 'rel_bias', 'conv_w', 'conv_b', 'lru_wa', 'lru_ba', 'lru_wx', 'lru_bx', 'lru_lambda', 'w_att_o', 'w_rec_o', 'w_out', 'ffn2_w_gu', 'ffn2_w_down']
TWIN_WEIGHTS = ['w_ada', 'b_ada', 'norm_pre', 'norm_post', 'ffn1_w_gu', 'ffn1_w_down', 'w_in', 'rel_bias', 'conv_w', 'conv_b', 'lru_wa', 'lru_ba', 'lru_wx', 'lru_bx', 'lru_lambda', 'w_att_o', 'w_rec_o', 'w_out', 'ffn2_w_gu', 'ffn2_w_down']
TWIN_DIFF_INPUT = 'x'
TWIN_INPUTS = ['x', 'c', 'w_ada', 'b_ada', 'norm_pre', 'norm_post', 'ffn1_w_gu', 'ffn1_w_down', 'w_in', 'rel_bias', 'conv_w', 'conv_b', 'lru_wa', 'lru_ba', 'lru_wx', 'lru_bx', 'lru_lambda', 'w_att_o', 'w_rec_o', 'w_out', 'ffn2_w_gu', 'ffn2_w_down', 'loss_target', 'm_w_ada', 'm_b_ada', 'm_norm_pre', 'm_norm_post', 'm_ffn1_w_gu', 'm_ffn1_w_down', 'm_w_in', 'm_rel_bias', 'm_conv_w', 'm_conv_b', 'm_lru_wa', 'm_lru_ba', 'm_lru_wx', 'm_lru_bx', 'm_lru_lambda', 'm_w_att_o', 'm_w_rec_o', 'm_w_out', 'm_ffn2_w_gu', 'm_ffn2_w_down', 'v_w_ada', 'v_b_ada', 'v_norm_pre', 'v_norm_post', 'v_ffn1_w_gu', 'v_ffn1_w_down', 'v_w_in', 'v_rel_bias', 'v_conv_w', 'v_conv_b', 'v_lru_wa', 'v_lru_ba', 'v_lru_wx', 'v_lru_bx', 'v_lru_lambda', 'v_w_att_o', 'v_w_rec_o', 'v_w_out', 'v_ffn2_w_gu', 'v_ffn2_w_down']
TWIN_OUTPUTS = ['loss', 'grad_x', 'grad_w_ada', 'grad_b_ada', 'grad_norm_pre', 'grad_norm_post', 'grad_ffn1_w_gu', 'grad_ffn1_w_down', 'grad_w_in', 'grad_rel_bias', 'grad_conv_w', 'grad_conv_b', 'grad_lru_wa', 'grad_lru_ba', 'grad_lru_wx', 'grad_lru_bx', 'grad_lru_lambda', 'grad_w_att_o', 'grad_w_rec_o', 'grad_w_out', 'grad_ffn2_w_gu', 'grad_ffn2_w_down', 'delta_w_ada', 'delta_b_ada', 'delta_norm_pre', 'delta_norm_post', 'delta_ffn1_w_gu', 'delta_ffn1_w_down', 'delta_w_in', 'delta_rel_bias', 'delta_conv_w', 'delta_conv_b', 'delta_lru_wa', 'delta_lru_ba', 'delta_lru_wx', 'delta_lru_bx', 'delta_lru_lambda', 'delta_w_att_o', 'delta_w_rec_o', 'delta_w_out', 'delta_ffn2_w_gu', 'delta_ffn2_w_down', 'new_m_w_ada', 'new_m_b_ada', 'new_m_norm_pre', 'new_m_norm_post', 'new_m_ffn1_w_gu', 'new_m_ffn1_w_down', 'new_m_w_in', 'new_m_rel_bias', 'new_m_conv_w', 'new_m_conv_b', 'new_m_lru_wa', 'new_m_lru_ba', 'new_m_lru_wx', 'new_m_lru_bx', 'new_m_lru_lambda', 'new_m_w_att_o', 'new_m_w_rec_o', 'new_m_w_out', 'new_m_ffn2_w_gu', 'new_m_ffn2_w_down', 'new_v_w_ada', 'new_v_b_ada', 'new_v_norm_pre', 'new_v_norm_post', 'new_v_ffn1_w_gu', 'new_v_ffn1_w_down', 'new_v_w_in', 'new_v_rel_bias', 'new_v_conv_w', 'new_v_conv_b', 'new_v_lru_wa', 'new_v_lru_ba', 'new_v_lru_wx', 'new_v_lru_bx', 'new_v_lru_lambda', 'new_v_w_att_o', 'new_v_w_rec_o', 'new_v_w_out', 'new_v_ffn2_w_gu', 'new_v_ffn2_w_down']
TWIN_LEAF_KINDS = {'loss': 'loss', 'grad_x': 'grad_x', 'grad_w_ada': 'grad_w', 'grad_b_ada': 'grad_w', 'grad_norm_pre': 'grad_w', 'grad_norm_post': 'grad_w', 'grad_ffn1_w_gu': 'grad_w', 'grad_ffn1_w_down': 'grad_w', 'grad_w_in': 'grad_w', 'grad_rel_bias': 'grad_w', 'grad_conv_w': 'grad_w', 'grad_conv_b': 'grad_w', 'grad_lru_wa': 'grad_w', 'grad_lru_ba': 'grad_w', 'grad_lru_wx': 'grad_w', 'grad_lru_bx': 'grad_w', 'grad_lru_lambda': 'grad_w', 'grad_w_att_o': 'grad_w', 'grad_w_rec_o': 'grad_w', 'grad_w_out': 'grad_w', 'grad_ffn2_w_gu': 'grad_w', 'grad_ffn2_w_down': 'grad_w', 'delta_w_ada': 'delta_w', 'delta_b_ada': 'delta_w', 'delta_norm_pre': 'delta_w', 'delta_norm_post': 'delta_w', 'delta_ffn1_w_gu': 'delta_w', 'delta_ffn1_w_down': 'delta_w', 'delta_w_in': 'delta_w', 'delta_rel_bias': 'delta_w', 'delta_conv_w': 'delta_w', 'delta_conv_b': 'delta_w', 'delta_lru_wa': 'delta_w', 'delta_lru_ba': 'delta_w', 'delta_lru_wx': 'delta_w', 'delta_lru_bx': 'delta_w', 'delta_lru_lambda': 'delta_w', 'delta_w_att_o': 'delta_w', 'delta_w_rec_o': 'delta_w', 'delta_w_out': 'delta_w', 'delta_ffn2_w_gu': 'delta_w', 'delta_ffn2_w_down': 'delta_w', 'new_m_w_ada': 'new_m', 'new_m_b_ada': 'new_m', 'new_m_norm_pre': 'new_m', 'new_m_norm_post': 'new_m', 'new_m_ffn1_w_gu': 'new_m', 'new_m_ffn1_w_down': 'new_m', 'new_m_w_in': 'new_m', 'new_m_rel_bias': 'new_m', 'new_m_conv_w': 'new_m', 'new_m_conv_b': 'new_m', 'new_m_lru_wa': 'new_m', 'new_m_lru_ba': 'new_m', 'new_m_lru_wx': 'new_m', 'new_m_lru_bx': 'new_m', 'new_m_lru_lambda': 'new_m', 'new_m_w_att_o': 'new_m', 'new_m_w_rec_o': 'new_m', 'new_m_w_out': 'new_m', 'new_m_ffn2_w_gu': 'new_m', 'new_m_ffn2_w_down': 'new_m', 'new_v_w_ada': 'new_v', 'new_v_b_ada': 'new_v', 'new_v_norm_pre': 'new_v', 'new_v_norm_post': 'new_v', 'new_v_ffn1_w_gu': 'new_v', 'new_v_ffn1_w_down': 'new_v', 'new_v_w_in': 'new_v', 'new_v_rel_bias': 'new_v', 'new_v_conv_w': 'new_v', 'new_v_conv_b': 'new_v', 'new_v_lru_wa': 'new_v', 'new_v_lru_ba': 'new_v', 'new_v_lru_wx': 'new_v', 'new_v_lru_bx': 'new_v', 'new_v_lru_lambda': 'new_v', 'new_v_w_att_o': 'new_v', 'new_v_w_rec_o': 'new_v', 'new_v_w_out': 'new_v', 'new_v_ffn2_w_gu': 'new_v', 'new_v_ffn2_w_down': 'new_v'}


def _forward(args):
    return _fwd_reference(*[args[k] for k in FWD_PARAMS])


def _output_shape():
    def fwd():
        inp = _fwd_setup_inputs(0)
        return _fwd_reference(*[inp[k] for k in FWD_PARAMS])
    out = _jax.eval_shape(fwd)
    return out.shape, out.dtype

N_MICROBATCH = 1
ADAM_LR = 0.001
ADAM_B1 = 0.9
ADAM_B2 = 0.999
ADAM_EPS = 1e-08
ADAM_WD = 0.01
ADAM_STEP = 10
PER_EXAMPLE_BATCH_AXIS = {'x': 0, 'c': 0, 'loss_target': 0}
SHARED_INPUTS = []
_WEIGHT_DTYPES = {'w_ada': _jnp.float32, 'b_ada': _jnp.float32, 'norm_pre': _jnp.float32, 'norm_post': _jnp.float32, 'ffn1_w_gu': _jnp.float32, 'ffn1_w_down': _jnp.float32, 'w_in': _jnp.float32, 'rel_bias': _jnp.float32, 'conv_w': _jnp.float32, 'conv_b': _jnp.float32, 'lru_wa': _jnp.float32, 'lru_ba': _jnp.float32, 'lru_wx': _jnp.float32, 'lru_bx': _jnp.float32, 'lru_lambda': _jnp.float32, 'w_att_o': _jnp.float32, 'w_rec_o': _jnp.float32, 'w_out': _jnp.float32, 'ffn2_w_gu': _jnp.float32, 'ffn2_w_down': _jnp.float32}
MOMENT_SCALE = {'w_ada': 8.093201e-01, 'b_ada': 1.698298e+00, 'norm_pre': 7.537816e-02, 'norm_post': 2.039426e+00, 'ffn1_w_gu': 3.102225e-02, 'ffn1_w_down': 5.491056e-02, 'w_in': 1.262197e-01, 'rel_bias': 1.264469e-02, 'conv_w': 2.689845e-01, 'conv_b': 8.815520e-01, 'lru_wa': 3.146057e-02, 'lru_ba': 4.559030e-02, 'lru_wx': 6.887177e-02, 'lru_bx': 9.619753e-02, 'lru_lambda': 1.167633e-01, 'w_att_o': 1.056309e-01, 'w_rec_o': 2.944747e-01, 'w_out': 3.152854e-01, 'ffn2_w_gu': 2.956076e-02, 'ffn2_w_down': 5.288898e-02}


def _to_microbatches(a, axis):
    t = _jnp.moveaxis(a, axis, 0)
    t = t.reshape((N_MICROBATCH, t.shape[0] // N_MICROBATCH) + t.shape[1:])
    return _jnp.moveaxis(t, 1, axis + 1)


def setup_inputs(seed: int = 0) -> dict:
    inp = _fwd_setup_inputs(seed)
    key = _jax.random.fold_in(_jax.random.key(seed), 7919)
    shape, _ = _output_shape()
    out = dict(inp)
    out["loss_target"] = _jax.random.normal(_jax.random.fold_in(key, 0), shape, _jnp.float32)
    for i, name in enumerate(TWIN_WEIGHTS):
        w = inp[name].astype(_jnp.float32)
        if MOMENT_SCALE is None:
            s = _jnp.sqrt(_jnp.mean(_jnp.square(w)) + 1e-30)
        else:
            s = MOMENT_SCALE[name]
        km, kv = _jax.random.split(_jax.random.fold_in(key, i + 1))
        out[name] = w
        out["m_" + name] = s * _jax.random.normal(km, w.shape, _jnp.float32)
        out["v_" + name] = (s * s) * _jax.random.uniform(kv, w.shape, _jnp.float32, 0.5, 1.5)
    if N_MICROBATCH > 1:
        for name, axis in PER_EXAMPLE_BATCH_AXIS.items():
            out[name] = _to_microbatches(out[name], axis)
    return {'x': out['x'], 'c': out['c'], 'w_ada': out['w_ada'], 'b_ada': out['b_ada'], 'norm_pre': out['norm_pre'], 'norm_post': out['norm_post'], 'ffn1_w_gu': out['ffn1_w_gu'], 'ffn1_w_down': out['ffn1_w_down'], 'w_in': out['w_in'], 'rel_bias': out['rel_bias'], 'conv_w': out['conv_w'], 'conv_b': out['conv_b'], 'lru_wa': out['lru_wa'], 'lru_ba': out['lru_ba'], 'lru_wx': out['lru_wx'], 'lru_bx': out['lru_bx'], 'lru_lambda': out['lru_lambda'], 'w_att_o': out['w_att_o'], 'w_rec_o': out['w_rec_o'], 'w_out': out['w_out'], 'ffn2_w_gu': out['ffn2_w_gu'], 'ffn2_w_down': out['ffn2_w_down'], 'loss_target': out['loss_target'], 'm_w_ada': out['m_w_ada'], 'm_b_ada': out['m_b_ada'], 'm_norm_pre': out['m_norm_pre'], 'm_norm_post': out['m_norm_post'], 'm_ffn1_w_gu': out['m_ffn1_w_gu'], 'm_ffn1_w_down': out['m_ffn1_w_down'], 'm_w_in': out['m_w_in'], 'm_rel_bias': out['m_rel_bias'], 'm_conv_w': out['m_conv_w'], 'm_conv_b': out['m_conv_b'], 'm_lru_wa': out['m_lru_wa'], 'm_lru_ba': out['m_lru_ba'], 'm_lru_wx': out['m_lru_wx'], 'm_lru_bx': out['m_lru_bx'], 'm_lru_lambda': out['m_lru_lambda'], 'm_w_att_o': out['m_w_att_o'], 'm_w_rec_o': out['m_w_rec_o'], 'm_w_out': out['m_w_out'], 'm_ffn2_w_gu': out['m_ffn2_w_gu'], 'm_ffn2_w_down': out['m_ffn2_w_down'], 'v_w_ada': out['v_w_ada'], 'v_b_ada': out['v_b_ada'], 'v_norm_pre': out['v_norm_pre'], 'v_norm_post': out['v_norm_post'], 'v_ffn1_w_gu': out['v_ffn1_w_gu'], 'v_ffn1_w_down': out['v_ffn1_w_down'], 'v_w_in': out['v_w_in'], 'v_rel_bias': out['v_rel_bias'], 'v_conv_w': out['v_conv_w'], 'v_conv_b': out['v_conv_b'], 'v_lru_wa': out['v_lru_wa'], 'v_lru_ba': out['v_lru_ba'], 'v_lru_wx': out['v_lru_wx'], 'v_lru_bx': out['v_lru_bx'], 'v_lru_lambda': out['v_lru_lambda'], 'v_w_att_o': out['v_w_att_o'], 'v_w_rec_o': out['v_w_rec_o'], 'v_w_out': out['v_w_out'], 'v_ffn2_w_gu': out['v_ffn2_w_gu'], 'v_ffn2_w_down': out['v_ffn2_w_down']}


def _loss(weights, diff, rest, loss_target):
    with _jax.named_scope("forward"):
        args = {**rest, TWIN_DIFF_INPUT: diff, **{k: w.astype(_WEIGHT_DTYPES[k]) for k, w in weights.items()}}
        y = _forward(args)
    with _jax.named_scope("loss_head"):
        err = _jnp.square(y.astype(_jnp.float32) - loss_target)
        return 0.5 * _jnp.sum(_jnp.mean(err, axis=-1)) if err.ndim else 0.5 * err


def _adamw(w, g, m, v):
    m = ADAM_B1 * m + (1.0 - ADAM_B1) * g
    v = ADAM_B2 * v + (1.0 - ADAM_B2) * _jnp.square(g)
    m_hat = m / (1.0 - ADAM_B1 ** ADAM_STEP)
    v_hat = v / (1.0 - ADAM_B2 ** ADAM_STEP)
    delta = -ADAM_LR * (m_hat / (_jnp.sqrt(v_hat) + ADAM_EPS) + ADAM_WD * w)
    return delta, m, v


def reference(x, c, w_ada, b_ada, norm_pre, norm_post, ffn1_w_gu, ffn1_w_down, w_in, rel_bias, conv_w, conv_b, lru_wa, lru_ba, lru_wx, lru_bx, lru_lambda, w_att_o, w_rec_o, w_out, ffn2_w_gu, ffn2_w_down, loss_target, m_w_ada, m_b_ada, m_norm_pre, m_norm_post, m_ffn1_w_gu, m_ffn1_w_down, m_w_in, m_rel_bias, m_conv_w, m_conv_b, m_lru_wa, m_lru_ba, m_lru_wx, m_lru_bx, m_lru_lambda, m_w_att_o, m_w_rec_o, m_w_out, m_ffn2_w_gu, m_ffn2_w_down, v_w_ada, v_b_ada, v_norm_pre, v_norm_post, v_ffn1_w_gu, v_ffn1_w_down, v_w_in, v_rel_bias, v_conv_w, v_conv_b, v_lru_wa, v_lru_ba, v_lru_wx, v_lru_bx, v_lru_lambda, v_w_att_o, v_w_rec_o, v_w_out, v_ffn2_w_gu, v_ffn2_w_down):
    given = dict(x=x, c=c, w_ada=w_ada, b_ada=b_ada, norm_pre=norm_pre, norm_post=norm_post, ffn1_w_gu=ffn1_w_gu, ffn1_w_down=ffn1_w_down, w_in=w_in, rel_bias=rel_bias, conv_w=conv_w, conv_b=conv_b, lru_wa=lru_wa, lru_ba=lru_ba, lru_wx=lru_wx, lru_bx=lru_bx, lru_lambda=lru_lambda, w_att_o=w_att_o, w_rec_o=w_rec_o, w_out=w_out, ffn2_w_gu=ffn2_w_gu, ffn2_w_down=ffn2_w_down, loss_target=loss_target, m_w_ada=m_w_ada, m_b_ada=m_b_ada, m_norm_pre=m_norm_pre, m_norm_post=m_norm_post, m_ffn1_w_gu=m_ffn1_w_gu, m_ffn1_w_down=m_ffn1_w_down, m_w_in=m_w_in, m_rel_bias=m_rel_bias, m_conv_w=m_conv_w, m_conv_b=m_conv_b, m_lru_wa=m_lru_wa, m_lru_ba=m_lru_ba, m_lru_wx=m_lru_wx, m_lru_bx=m_lru_bx, m_lru_lambda=m_lru_lambda, m_w_att_o=m_w_att_o, m_w_rec_o=m_w_rec_o, m_w_out=m_w_out, m_ffn2_w_gu=m_ffn2_w_gu, m_ffn2_w_down=m_ffn2_w_down, v_w_ada=v_w_ada, v_b_ada=v_b_ada, v_norm_pre=v_norm_pre, v_norm_post=v_norm_post, v_ffn1_w_gu=v_ffn1_w_gu, v_ffn1_w_down=v_ffn1_w_down, v_w_in=v_w_in, v_rel_bias=v_rel_bias, v_conv_w=v_conv_w, v_conv_b=v_conv_b, v_lru_wa=v_lru_wa, v_lru_ba=v_lru_ba, v_lru_wx=v_lru_wx, v_lru_bx=v_lru_bx, v_lru_lambda=v_lru_lambda, v_w_att_o=v_w_att_o, v_w_rec_o=v_w_rec_o, v_w_out=v_w_out, v_ffn2_w_gu=v_ffn2_w_gu, v_ffn2_w_down=v_ffn2_w_down)
    weights = {n: given[n] for n in TWIN_WEIGHTS}
    shared = {n: given[n] for n in SHARED_INPUTS}
    per_example = {n: given[n] for n in ['x', 'c']}
    grad_fn = _jax.value_and_grad(_loss, argnums=(0, 1))

    def one_microbatch(ex, loss_target):
        ex = dict(ex)
        diff = ex.pop(TWIN_DIFF_INPUT)
        return grad_fn(weights, diff, {**shared, **ex}, loss_target)

    if N_MICROBATCH == 1:
        loss, (grad_w, grad_x) = one_microbatch(per_example, given["loss_target"])
    else:
        def body(carry, xs):
            loss_sum, grad_sum = carry
            l_k, (gw_k, gx_k) = one_microbatch(xs[0], xs[1])
            with _jax.named_scope("update"):
                return (loss_sum + l_k, _jax.tree.map(_jnp.add, grad_sum, gw_k)), gx_k

        init = (_jnp.zeros((), _jnp.float32), _jax.tree.map(_jnp.zeros_like, weights))
        (loss, grad_w), grad_x = _jax.lax.scan(body, init, (per_example, given["loss_target"]))
    with _jax.named_scope("update"):
        delta_w, new_m, new_v = {}, {}, {}
        for n in TWIN_WEIGHTS:
            delta_w[n], new_m[n], new_v[n] = _adamw(weights[n], grad_w[n], given["m_" + n], given["v_" + n])
    return (loss, grad_x, *[grad_w[n] for n in TWIN_WEIGHTS], *[delta_w[n] for n in TWIN_WEIGHTS],
            *[new_m[n] for n in TWIN_WEIGHTS], *[new_v[n] for n in TWIN_WEIGHTS])
```

```python
import functools

import jax
import jax.numpy as jnp
import numpy as np
from jax import lax
from jax.experimental import pallas as pl
from jax.experimental.pallas import tpu as pltpu

D_MODEL = 1024
D_FF = 2816
ATT_HEADS = 8
ATT_HEAD_DIM = 64
ATT_WIDTH = 512
CHUNK = 64
LEFT_CHUNKS = 8
MAX_REL = 128
LRU_WIDTH = 1024
LRU_BLOCKS = 16
LRU_BLOCK = 64
LRU_C = 8.0
EPS = 1e-6
PROJ_WIDTH = 5632
N_DEV = 8

ADAM_LR = 0.001
ADAM_B1 = 0.9
ADAM_B2 = 0.999
ADAM_EPS = 1e-08
ADAM_WD = 0.01
ADAM_STEP = 10

V7X_LANES = 128
V7X_SUBLANES = 8
V7X_VMEM_BYTES = 64 * 1024 * 1024
VMEM_LIMIT = V7X_VMEM_BYTES - 8 * 1024 * 1024

ATT_TQ = 256
NEG = -1e30
BF16 = jnp.bfloat16
F32 = jnp.float32
MESH = pl.DeviceIdType.MESH

OFF_Q = 4 * LRU_WIDTH
OFF_K = OFF_Q + ATT_WIDTH
OFF_V = OFF_K + ATT_WIDTH


def _cparams(**kw):
    return pltpu.CompilerParams(vmem_limit_bytes=VMEM_LIMIT, **kw)


def _pick(n, target, unit=V7X_LANES):
    best = None
    for t in range(unit, min(n, target) + 1, unit):
        if n % t == 0:
            best = t
    return n if best is None else best


_DIMS = {
    "nn": (((1,), (0,)), ((), ())),
    "nt": (((1,), (1,)), ((), ())),
    "tn": (((0,), (0,)), ((), ())),
}


def _matmul(a, b, mode, out_dtype, name, tm=1024, tn=512, tk=1408):
    if mode == "nn":
        (m, k), (k2, n) = a.shape, b.shape
    elif mode == "nt":
        (m, k), (n, k2) = a.shape, b.shape
    else:
        (k, m), (k2, n) = a.shape, b.shape
    assert k == k2, (a.shape, b.shape, mode)
    tm, tn, tk = _pick(m, tm), _pick(n, tn), _pick(k, tk)
    nk = k // tk
    dims = _DIMS[mode]

    def body(a_ref, b_ref, o_ref, *scratch):
        p = lax.dot_general(a_ref[...], b_ref[...], dims, preferred_element_type=F32)
        if nk == 1:
            o_ref[...] = p.astype(o_ref.dtype)
        else:
            acc = scratch[0]
            kk = pl.program_id(2)

            @pl.when(kk == 0)
            def _():
                acc[...] = p

            @pl.when(kk > 0)
            def _():
                acc[...] += p

            @pl.when(kk == nk - 1)
            def _():
                o_ref[...] = acc[...].astype(o_ref.dtype)

    if mode == "nn":
        a_spec = pl.BlockSpec((tm, tk), lambda i, j, kk: (i, kk))
        b_spec = pl.BlockSpec((tk, tn), lambda i, j, kk: (kk, j))
    elif mode == "nt":
        a_spec = pl.BlockSpec((tm, tk), lambda i, j, kk: (i, kk))
        b_spec = pl.BlockSpec((tn, tk), lambda i, j, kk: (j, kk))
    else:
        a_spec = pl.BlockSpec((tk, tm), lambda i, j, kk: (kk, i))
        b_spec = pl.BlockSpec((tk, tn), lambda i, j, kk: (kk, j))
    return pl.pallas_call(
        body,
        name=name,
        grid=(m // tm, n // tn, nk),
        in_specs=[a_spec, b_spec],
        out_specs=pl.BlockSpec((tm, tn), lambda i, j, kk: (i, j)),
        out_shape=jax.ShapeDtypeStruct((m, n), out_dtype),
        scratch_shapes=[pltpu.VMEM((tm, tn), F32)] if nk > 1 else [],
        compiler_params=_cparams(dimension_semantics=("parallel", "parallel", "arbitrary")),
    )(a, b)


def _rowwise(fn, name, params, tiles, outs, accs=(), ts=256, with_index=False):
    norm = []
    for t in tiles:
        if not isinstance(t, tuple):
            t = (t, t.shape[1], 0)
        norm.append(t if len(t) == 4 else (*t, None))
    s = norm[0][0].shape[0]
    ts = min(ts, s)
    assert s % ts == 0 and ts % V7X_SUBLANES == 0
    steps = s // ts
    halo_blocks = ts // V7X_SUBLANES
    n_p, n_t, n_o = len(params), len(norm), len(outs)

    def body(*refs):
        i = pl.program_id(0)
        vals = [r[...] for r in refs[: n_p + n_t]]
        res = fn(i, steps, *vals) if with_index else fn(*vals)
        if not isinstance(res, (tuple, list)):
            res = (res,)
        o_refs = refs[n_p + n_t : n_p + n_t + n_o]
        a_refs = refs[n_p + n_t + n_o :]
        for r, v in zip(o_refs, res[:n_o]):
            r[...] = v.astype(r.dtype)
        for r, v in zip(a_refs, res[n_o:]):
            _accumulate(r, v, i)

    in_specs = [pl.BlockSpec(p.shape, lambda i: (0, 0)) for p in params]
    for arr, w, cb, halo in norm:
        if halo is None:
            in_specs.append(pl.BlockSpec((ts, w), lambda i, cb=cb: (i, cb)))
        elif halo == "prev":
            in_specs.append(
                pl.BlockSpec((V7X_SUBLANES, w), lambda i, cb=cb: (jnp.maximum(i * halo_blocks - 1, 0), cb))
            )
        else:
            last = s // V7X_SUBLANES - 1
            in_specs.append(
                pl.BlockSpec((V7X_SUBLANES, w), lambda i, cb=cb: (jnp.minimum((i + 1) * halo_blocks, last), cb))
            )
    out_specs = [pl.BlockSpec((ts, w), lambda i: (i, 0)) for w, _ in outs]
    out_specs += [pl.BlockSpec(shape, lambda i: (0, 0)) for shape in accs]
    out_shape = [jax.ShapeDtypeStruct((s, w), dt) for w, dt in outs]
    out_shape += [jax.ShapeDtypeStruct(shape, F32) for shape in accs]
    res = pl.pallas_call(
        body,
        name=name,
        grid=(steps,),
        in_specs=in_specs,
        out_specs=out_specs,
        out_shape=out_shape,
        compiler_params=_cparams(dimension_semantics=("arbitrary",)),
    )(*params, *[t[0] for t in norm])
    return res


def _accumulate(ref, val, step):
    @pl.when(step == 0)
    def _():
        ref[...] = val

    @pl.when(step > 0)
    def _():
        ref[...] += val


def _sigmoid(z):
    return jax.nn.sigmoid(z)


def _silu(z):
    return z * _sigmoid(z)


def _gelu(z):
    return 0.5 * z * (1.0 + jnp.tanh(0.7978845608028654 * (z + 0.044715 * (z * z * z))))


def _pre_fn(g, shift, scale, x):
    r = lax.rsqrt(jnp.mean(x * x, axis=-1, keepdims=True) + EPS)
    return ((x * r) * g) * (1.0 + scale) + shift


def _post_fn(res_w, g, gate, f, x):
    r = lax.rsqrt(jnp.mean(f * f, axis=-1, keepdims=True) + EPS)
    return x + (res_w * gate) * ((f * r) * g)


def _swiglu_fn(gu):
    return _silu(gu[:, :D_FF]) * gu[:, D_FF:]


def _gates_fn(ba, bx, lam, pre, xc):
    ra = _sigmoid(pre[:, :LRU_WIDTH] + ba)
    ia = _sigmoid(pre[:, LRU_WIDTH:] + bx)
    softplus = jnp.maximum(-lam, 0.0) + jnp.log1p(jnp.exp(-jnp.abs(lam)))
    log_a = (-LRU_C) * ra * softplus
    a = jnp.exp(log_a)
    mult = jnp.sqrt(-jnp.tanh(log_a) * (a * a + 1.0))
    return a, mult * (ia * xc)


def _recin_fn(hs, yr):
    return hs * _gelu(yr)


def _merge_fn(att, rec, g_att, g_rec):
    return _sigmoid(g_att) * att + _sigmoid(g_rec) * rec


def _rowsum(v):
    return jnp.sum(v, axis=0, keepdims=True)


def _pre_fwd(x, g, shift, scale, name):
    (h,) = _rowwise(_pre_fn, name, [g, shift, scale], [x], [(D_MODEL, BF16)])
    return h


def _pre_bwd(x, g, shift, scale, dh, dres, name):
    def fn(g, shift, scale, x, dh, dres):
        _, vjp = jax.vjp(_pre_fn, g, shift, scale, x)
        dg, dshift, dscale, dx = vjp(dh)
        return dx + dres, dg, dshift, dscale

    row = (1, D_MODEL)
    return _rowwise(fn, name, [g, shift, scale], [x, dh, dres], [(D_MODEL, F32)], [row, row, row])


def _post_fwd(f, x, g, gate, res_w, name):
    (y,) = _rowwise(functools.partial(_post_fn, res_w), name, [g, gate], [f, x], [(D_MODEL, F32)])
    return y


def _post_bwd(f, g, gate, res_w, dy, name):
    def fn(g, gate, f, dy):
        _, vjp = jax.vjp(lambda g, gate, f: _post_fn(res_w, g, gate, f, 0.0), g, gate, f)
        dg, dgate, df = vjp(dy)
        return df, dg, dgate

    row = (1, D_MODEL)
    return _rowwise(fn, name, [g, gate], [f, dy], [(D_MODEL, BF16)], [row, row])


def _loss_stage(y, target, name):
    def fn(y, t):
        diff = y - t
        return diff * (1.0 / D_MODEL), _rowsum(diff * diff)

    return _rowwise(fn, name, [], [y, target], [(D_MODEL, F32)], [(1, D_MODEL)])


def _swiglu_fwd(gu, name):
    (a,) = _rowwise(_swiglu_fn, name, [], [gu], [(D_FF, BF16)], ts=128)
    return a


def _swiglu_bwd(gu, da, name):
    def fn(gu, da):
        _, vjp = jax.vjp(_swiglu_fn, gu)
        return vjp(da)[0]

    (dgu,) = _rowwise(fn, name, [], [gu, da], [(2 * D_FF, BF16)], ts=128)
    return dgu


def _shift_down(ext, j, rows):
    return pltpu.roll(ext, j, 0)[V7X_SUBLANES : V7X_SUBLANES + rows]


def _shift_up(ext, j, rows):
    return pltpu.roll(ext, ext.shape[0] - j, 0)[:rows] if j else ext[:rows]


def _conv_fwd(proj, w8, b, name):
    def fn(i, steps, w8, b, x, halo):
        halo = jnp.where(i > 0, halo, 0.0)
        ext = jnp.concatenate([halo, x], axis=0)
        acc = b + w8[3:4] * x
        for j in (1, 2, 3):
            acc = acc + w8[3 - j : 4 - j] * _shift_down(ext, j, x.shape[0])
        return acc, acc

    tiles = [(proj, LRU_WIDTH, 0), (proj, LRU_WIDTH, 0, "prev")]
    return _rowwise(fn, name, [w8, b], tiles, [(LRU_WIDTH, F32), (LRU_WIDTH, BF16)], with_index=True)


def _conv_bwd(proj, w8, d1, d2, name):
    def fn(i, steps, w8, x, halo, d1, d1n, d2, d2n):
        rows = x.shape[0]
        d = d1 + d2
        dn = jnp.where(i < steps - 1, d1n + d2n, 0.0)
        halo = jnp.where(i > 0, halo, 0.0)
        dext = jnp.concatenate([d, dn], axis=0)
        xext = jnp.concatenate([halo, x], axis=0)
        dx = w8[3:4] * d
        dw = [None] * 4
        dw[3] = _rowsum(d * x)
        for k in (1, 2, 3):
            dx = dx + w8[3 - k : 4 - k] * _shift_up(dext, k, rows)
            dw[3 - k] = _rowsum(d * _shift_down(xext, k, rows))
        dw8 = jnp.concatenate(dw + [jnp.zeros((4, LRU_WIDTH), F32)], axis=0)
        return dx, dw8, _rowsum(d)

    tiles = [(proj, LRU_WIDTH, 0), (proj, LRU_WIDTH, 0, "prev"), d1, (d1, LRU_WIDTH, 0, "next"),
             d2, (d2, LRU_WIDTH, 0, "next")]
    return _rowwise(fn, name, [w8], tiles, [(LRU_WIDTH, BF16)], [(8, LRU_WIDTH), (1, LRU_WIDTH)], with_index=True)


def _gates_fwd(pre, xc, ba, bx, lam, name):
    return _rowwise(_gates_fn, name, [ba, bx, lam], [pre, xc], [(LRU_WIDTH, F32), (LRU_WIDTH, F32)])


def _gates_bwd(pre, xc, ba, bx, lam, g, h_prev, name):
    def fn(ba, bx, lam, pre, xc, g, h_prev):
        _, vjp = jax.vjp(_gates_fn, ba, bx, lam, pre, xc)
        dba, dbx, dlam, dpre, dxc = vjp((g * h_prev, g))
        return dpre, dxc, dba, dbx, dlam

    row = (1, LRU_WIDTH)
    return _rowwise(fn, name, [ba, bx, lam], [pre, xc, g, h_prev],
                    [(2 * LRU_WIDTH, BF16), (LRU_WIDTH, F32)], [row, row, row])


SCAN_ROWS = 512


def _block_scan(a, b, row, reverse):
    for d in (1, 2, 4):
        if reverse:
            shift, keep = V7X_SUBLANES - d, row < V7X_SUBLANES - d
        else:
            shift, keep = d, row >= d
        a_s = pltpu.roll(a, shift, 0)
        b_s = pltpu.roll(b, shift, 0)
        b = jnp.where(keep, a * b_s + b, b)
        a = jnp.where(keep, a * a_s, a)
    return a, b


def _scan_fwd(a, u, name):
    s, w = a.shape
    ts = min(SCAN_ROWS, s)
    sub = ts // V7X_SUBLANES

    def body(a_ref, u_ref, h_ref, hp_ref, carry):
        @pl.when(pl.program_id(0) == 0)
        def _():
            carry[...] = jnp.zeros_like(carry)

        row = lax.broadcasted_iota(jnp.int32, (V7X_SUBLANES, w), 0)

        def step(j, c):
            rows = pl.ds(pl.multiple_of(j * V7X_SUBLANES, V7X_SUBLANES), V7X_SUBLANES)
            pa, pb = _block_scan(a_ref[rows, :], u_ref[rows, :], row, False)
            h = pb + pa * c
            h_ref[rows, :] = h
            hp_ref[rows, :] = jnp.where(row >= 1, pltpu.roll(h, 1, 0), c)
            return jnp.broadcast_to(h[V7X_SUBLANES - 1 :], (V7X_SUBLANES, w))

        carry[...] = lax.fori_loop(0, sub, step, carry[...])

    spec = pl.BlockSpec((ts, w), lambda i: (i, 0))
    return pl.pallas_call(
        body,
        name=name,
        grid=(s // ts,),
        in_specs=[spec, spec],
        out_specs=[spec, spec],
        out_shape=[jax.ShapeDtypeStruct((s, w), F32)] * 2,
        scratch_shapes=[pltpu.VMEM((V7X_SUBLANES, w), F32)],
        compiler_params=_cparams(dimension_semantics=("arbitrary",)),
    )(a, u)


def _scan_bwd(a, dh, name):
    s, w = a.shape
    ts = min(SCAN_ROWS, s)
    sub = ts // V7X_SUBLANES
    steps = s // ts

    def body(a_ref, d_ref, g_ref, carry):
        @pl.when(pl.program_id(0) == 0)
        def _():
            carry[...] = jnp.zeros_like(carry)

        row = lax.broadcasted_iota(jnp.int32, (V7X_SUBLANES, w), 0)

        def step(jj, c):
            j = sub - 1 - jj
            rows = pl.ds(pl.multiple_of(j * V7X_SUBLANES, V7X_SUBLANES), V7X_SUBLANES)
            av, dv = a_ref[rows, :], d_ref[rows, :]
            pa, pb = _block_scan(av, av * dv, row, True)
            big = pb + pa * c
            g_ref[rows, :] = dv + jnp.where(row < V7X_SUBLANES - 1, pltpu.roll(big, V7X_SUBLANES - 1, 0), c)
            return jnp.broadcast_to(big[:1], (V7X_SUBLANES, w))

        carry[...] = lax.fori_loop(0, sub, step, carry[...])

    spec = pl.BlockSpec((ts, w), lambda i: (steps - 1 - i, 0))
    return pl.pallas_call(
        body,
        name=name,
        grid=(steps,),
        in_specs=[spec, spec],
        out_specs=spec,
        out_shape=jax.ShapeDtypeStruct((s, w), F32),
        scratch_shapes=[pltpu.VMEM((V7X_SUBLANES, w), F32)],
        compiler_params=_cparams(dimension_semantics=("arbitrary",)),
    )(a, dh)


def _rel_index():
    i = np.arange(ATT_TQ)[:, None]
    j = np.arange(3 * ATT_TQ)[None, :]
    band = (j // CHUNK >= i // CHUNK) & (j // CHUNK <= i // CHUNK + LEFT_CHUNKS)
    return band


def _diag_onehot():
    d = np.arange(ATT_TQ + 3 * ATT_TQ - 1) - (3 * ATT_TQ - 1)
    idx = np.clip(d + LEFT_CHUNKS * CHUNK, -MAX_REL, MAX_REL) + MAX_REL
    return (idx[:, None] == np.arange(2 * MAX_REL + 1)[None, :]).astype(np.float32)


def _bias_tile(rel_bias):
    n_diag = 4 * ATT_TQ - 1
    per_diag = jnp.dot(rel_bias, jnp.asarray(_diag_onehot()).T, precision=lax.Precision.HIGHEST)
    flat = jnp.broadcast_to(per_diag[:, None, :], (ATT_HEADS, ATT_TQ, n_diag)).reshape(ATT_HEADS, ATT_TQ * n_diag)
    flat = jnp.pad(flat, ((0, 0), (0, ATT_TQ)))
    skew = flat.reshape(ATT_HEADS, ATT_TQ, n_diag + 1)[:, :, : 3 * ATT_TQ]
    tile = skew[:, :, ::-1]
    return jnp.where(jnp.asarray(_rel_index())[None], tile, NEG)


def _bias_grad(dbias):
    n_diag = 4 * ATT_TQ - 1
    flip = dbias[:, :, ::-1]
    padded = jnp.pad(flip, ((0, 0), (0, 0), (0, ATT_TQ))).reshape(ATT_HEADS, ATT_TQ * (n_diag + 1))
    skew = padded[:, : ATT_TQ * n_diag].reshape(ATT_HEADS, ATT_TQ, n_diag)
    per_diag = jnp.sum(skew, axis=1)
    return jnp.dot(per_diag, jnp.asarray(_diag_onehot()), precision=lax.Precision.HIGHEST)


def _attn_specs(nt):
    qb, kb, vb = OFF_Q // V7X_LANES, OFF_K // V7X_LANES, OFF_V // V7X_LANES
    blk = (ATT_TQ, V7X_LANES)

    def qmap(base):
        return lambda hp, m: (jnp.minimum(m, nt - 1), base + hp)

    def wmap(base, back):
        return lambda hp, m: (jnp.clip(m - back, 0, nt - 1), base + hp)

    specs = [pl.BlockSpec(blk, qmap(qb))]
    specs += [pl.BlockSpec(blk, wmap(kb, back)) for back in (2, 1, 0)]
    specs += [pl.BlockSpec(blk, wmap(vb, back)) for back in (2, 1, 0)]
    return specs


def _attn_probs(qh, kh, bias, ok):
    s = lax.dot_general(qh, kh, _DIMS["nt"], preferred_element_type=F32) * (ATT_HEAD_DIM**-0.5) + bias
    s = jnp.where(ok, s, NEG)
    e = jnp.exp(s - jnp.max(s, axis=-1, keepdims=True))
    return e / jnp.sum(e, axis=-1, keepdims=True)


def _attn_window(m, k0, k1, k2, v0, v1, v2):
    k = jnp.concatenate([k0[...], k1[...], k2[...]], axis=0).astype(BF16)
    v = jnp.concatenate([v0[...], v1[...], v2[...]], axis=0).astype(BF16)
    kpos = lax.broadcasted_iota(jnp.int32, (ATT_TQ, 3 * ATT_TQ), 1) + (m - 2) * ATT_TQ
    return k, v, kpos >= 0


def _attn_fwd(proj, bias, name):
    s = proj.shape[0]
    nt = s // ATT_TQ

    def body(q_ref, k0, k1, k2, v0, v1, v2, b_ref, o_ref):
        m = pl.program_id(1)
        k, v, ok = _attn_window(m, k0, k1, k2, v0, v1, v2)
        q = q_ref[...].astype(BF16)
        for hh in range(2):
            cols = slice(hh * ATT_HEAD_DIM, (hh + 1) * ATT_HEAD_DIM)
            p = _attn_probs(q[:, cols], k[:, cols], b_ref[hh], ok)
            o = jnp.dot(p.astype(BF16), v[:, cols], preferred_element_type=F32)
            o_ref[:, cols] = o.astype(o_ref.dtype)

    specs = _attn_specs(nt) + [pl.BlockSpec((2, ATT_TQ, 3 * ATT_TQ), lambda hp, m: (hp, 0, 0))]
    return pl.pallas_call(
        body,
        name=name,
        grid=(ATT_HEADS // 2, nt),
        in_specs=specs,
        out_specs=pl.BlockSpec((ATT_TQ, V7X_LANES), lambda hp, m: (m, hp)),
        out_shape=jax.ShapeDtypeStruct((s, ATT_WIDTH), BF16),
        compiler_params=_cparams(dimension_semantics=("parallel", "arbitrary")),
    )(proj, proj, proj, proj, proj, proj, proj, bias)


def _attn_bwd(proj, bias, do, name):
    s = proj.shape[0]
    nt = s // ATT_TQ
    win = 3 * ATT_TQ

    def body(q_ref, k0, k1, k2, v0, v1, v2, do_ref, b_ref, dq_ref, dk_ref, dv_ref, db_ref, dk_acc, dv_acc):
        m = pl.program_id(1)

        @pl.when(m == 0)
        def _():
            dk_acc[...] = jnp.zeros_like(dk_acc)
            dv_acc[...] = jnp.zeros_like(dv_acc)
            db_ref[...] = jnp.zeros_like(db_ref)

        @pl.when(m < nt)
        def _():
            k, v, ok = _attn_window(m, k0, k1, k2, v0, v1, v2)
            q = q_ref[...].astype(BF16)
            dout = do_ref[...]
            for hh in range(2):
                cols = slice(hh * ATT_HEAD_DIM, (hh + 1) * ATT_HEAD_DIM)
                qh, kh, vh, doh = q[:, cols], k[:, cols], v[:, cols], dout[:, cols]
                p = _attn_probs(qh, kh, b_ref[hh], ok)
                dvh = lax.dot_general(p.astype(BF16), doh, _DIMS["tn"], preferred_element_type=F32)
                dp = lax.dot_general(doh, vh, _DIMS["nt"], preferred_element_type=F32)
                ds = p * (dp - jnp.sum(dp * p, axis=-1, keepdims=True))
                db_ref[hh] += ds
                dsb = ds.astype(BF16)
                dqh = jnp.dot(dsb, kh, preferred_element_type=F32) * (ATT_HEAD_DIM**-0.5)
                dkh = lax.dot_general(dsb, qh, _DIMS["tn"], preferred_element_type=F32) * (ATT_HEAD_DIM**-0.5)
                dq_ref[:, cols] = dqh.astype(dq_ref.dtype)
                dk_acc[:, cols] += dkh
                dv_acc[:, cols] += dvh

        dk_ref[...] = dk_acc[:ATT_TQ].astype(dk_ref.dtype)
        dv_ref[...] = dv_acc[:ATT_TQ].astype(dv_ref.dtype)
        for acc in (dk_acc, dv_acc):
            rest = acc[ATT_TQ:]
            acc[: win - ATT_TQ] = rest
            acc[win - ATT_TQ :] = jnp.zeros((ATT_TQ, V7X_LANES), F32)

    blk = (ATT_TQ, V7X_LANES)
    specs = _attn_specs(nt)
    specs.append(pl.BlockSpec(blk, lambda hp, m: (jnp.minimum(m, nt - 1), hp)))
    specs.append(pl.BlockSpec((2, ATT_TQ, win), lambda hp, m: (hp, 0, 0)))
    done = lambda hp, m: (jnp.maximum(m - 2, 0), hp)
    out_specs = [
        pl.BlockSpec(blk, lambda hp, m: (jnp.minimum(m, nt - 1), hp)),
        pl.BlockSpec(blk, done),
        pl.BlockSpec(blk, done),
        pl.BlockSpec((2, ATT_TQ, win), lambda hp, m: (hp, 0, 0)),
    ]
    out_shape = [jax.ShapeDtypeStruct((s, ATT_WIDTH), BF16)] * 3
    out_shape.append(jax.ShapeDtypeStruct((ATT_HEADS, ATT_TQ, win), F32))
    return pl.pallas_call(
        body,
        name=name,
        grid=(ATT_HEADS // 2, nt + 2),
        in_specs=specs,
        out_specs=out_specs,
        out_shape=out_shape,
        scratch_shapes=[pltpu.VMEM((win, V7X_LANES), F32), pltpu.VMEM((win, V7X_LANES), F32)],
        compiler_params=_cparams(dimension_semantics=("arbitrary", "arbitrary")),
    )(proj, proj, proj, proj, proj, proj, proj, do, bias)


def _ada_fwd(c_all, w, name):
    def body(c_ref, w_ref, o_ref):
        act = _silu(c_ref[...]).astype(BF16)
        o_ref[...] = jnp.dot(act, w_ref[...].astype(BF16), preferred_element_type=F32)

    return pl.pallas_call(
        body, name=name, out_shape=jax.ShapeDtypeStruct((c_all.shape[0], w.shape[1]), F32), compiler_params=_cparams()
    )(c_all, w)


def _ada_bwd(c_all, dmod, name):
    def body(c_ref, d_ref, o_ref):
        act = _silu(c_ref[...])
        o_ref[...] = lax.dot_general(act, d_ref[...], _DIMS["tn"], preferred_element_type=F32,
                                     precision=lax.Precision.HIGHEST)

    return pl.pallas_call(
        body, name=name, out_shape=jax.ShapeDtypeStruct((c_all.shape[1], dmod.shape[1]), F32), compiler_params=_cparams()
    )(c_all, dmod)


def _adamw(g, w, m, v, name, rows=256):
    r, c = w.shape
    tr = _pick(r, rows, 16)
    parts = g.ndim == 3

    def body(g_ref, w_ref, m_ref, v_ref, go_ref, d_ref, mo_ref, vo_ref):
        if parts:
            grad = g_ref[0].astype(F32)
            for d in range(1, N_DEV):
                grad = grad + g_ref[d].astype(F32)
        else:
            grad = g_ref[...]
        m2 = ADAM_B1 * m_ref[...] + (1.0 - ADAM_B1) * grad
        v2 = ADAM_B2 * v_ref[...] + (1.0 - ADAM_B2) * (grad * grad)
        m_hat = m2 / (1.0 - ADAM_B1**ADAM_STEP)
        v_hat = v2 / (1.0 - ADAM_B2**ADAM_STEP)
        go_ref[...] = grad
        d_ref[...] = -ADAM_LR * (m_hat / (jnp.sqrt(v_hat) + ADAM_EPS) + ADAM_WD * w_ref[...])
        mo_ref[...] = m2
        vo_ref[...] = v2

    spec = pl.BlockSpec((tr, c), lambda i: (i, 0))
    g_spec = pl.BlockSpec((N_DEV, tr, c), lambda i: (0, i, 0)) if parts else spec
    return pl.pallas_call(
        body,
        name=name,
        grid=(r // tr,),
        in_specs=[g_spec, spec, spec, spec],
        out_specs=[spec] * 4,
        out_shape=[jax.ShapeDtypeStruct((r, c), F32)] * 4,
        compiler_params=_cparams(dimension_semantics=("parallel",)),
    )(g, w, m, v)


def _sum_parts(parts, name):
    def body(p_ref, o_ref):
        acc = p_ref[0]
        for d in range(1, N_DEV):
            acc = acc + p_ref[d]
        o_ref[...] = acc

    return pl.pallas_call(
        body, name=name, out_shape=jax.ShapeDtypeStruct(parts.shape[1:], F32), compiler_params=_cparams()
    )(parts)


def _place():
    x, y, c = lax.axis_index("x"), lax.axis_index("y"), lax.axis_index("c")
    return x, y, c


def _dev_index(p):
    return 4 * p[0] + 2 * p[1] + p[2]


def _allgather_vmem(shard, name):
    m_per, n = shard.shape

    def body(x_ref, out_ref, send_sems, recv_sems, local_sem):
        x, y, c = _place()
        me, sibling = (x, y, c), (x, y, 1 - c)
        chips = [(1 - x, y), (x, 1 - y), (1 - x, 1 - y)]

        def rows(p):
            return out_ref.at[pl.ds(_dev_index(p) * m_per, m_per), :]

        def copy(k, block, to, src=None):
            return pltpu.make_async_remote_copy(
                src_ref=rows(block) if src is None else src, dst_ref=rows(block),
                send_sem=send_sems.at[k], recv_sem=recv_sems.at[k], device_id=to, device_id_type=MESH)

        mine = pltpu.make_async_copy(x_ref, rows(me), local_sem)
        mine.start()
        first = [copy(0, me, sibling, src=x_ref)]
        first += [copy(1 + j, me, (*chip, c), src=x_ref) for j, chip in enumerate(chips)]
        for cp in first:
            cp.start()
        passed = [copy(4 + j, (*chip, c), sibling) for j, chip in enumerate(chips)]
        for j, chip in enumerate(chips):
            copy(1 + j, (*chip, c), me).wait_recv()
            passed[j].start()
        copy(0, sibling, me).wait_recv()
        for j, chip in enumerate(chips):
            copy(4 + j, (*chip, 1 - c), me).wait_recv()
        for cp in first + passed:
            cp.wait_send()
        mine.wait()

    return pl.pallas_call(
        body,
        name=name,
        out_shape=jax.ShapeDtypeStruct((N_DEV * m_per, n), shard.dtype),
        in_specs=[pl.BlockSpec(memory_space=pltpu.VMEM)],
        out_specs=pl.BlockSpec(memory_space=pltpu.VMEM),
        scratch_shapes=[pltpu.SemaphoreType.DMA((7,)), pltpu.SemaphoreType.DMA((7,)), pltpu.SemaphoreType.DMA],
        compiler_params=_cparams(),
    )(shard)


def _allgather_hbm(shards, name):
    n = len(shards)

    def body(*refs):
        ins, outs = refs[:n], refs[n : 2 * n]
        send_sems, recv_sems, local_sems = refs[2 * n :]
        x, y, c = _place()
        me, sibling = (x, y, c), (x, y, 1 - c)
        chips = [(1 - x, y), (x, 1 - y), (1 - x, 1 - y)]

        def copy(a, k, block, to, src=None):
            dst = outs[a].at[_dev_index(block)]
            return pltpu.make_async_remote_copy(
                src_ref=dst if src is None else src, dst_ref=dst,
                send_sem=send_sems.at[a * 7 + k], recv_sem=recv_sems.at[a * 7 + k], device_id=to, device_id_type=MESH)

        mine = [pltpu.make_async_copy(ins[a], outs[a].at[_dev_index(me)], local_sems.at[a]) for a in range(n)]
        for cp in mine:
            cp.start()
        first = []
        for a in range(n):
            first.append(copy(a, 0, me, sibling, src=ins[a]))
            first += [copy(a, 1 + j, me, (*chip, c), src=ins[a]) for j, chip in enumerate(chips)]
        for cp in first:
            cp.start()
        passed = []
        for j, chip in enumerate(chips):
            for a in range(n):
                copy(a, 1 + j, (*chip, c), me).wait_recv()
                cp = copy(a, 4 + j, (*chip, c), sibling)
                cp.start()
                passed.append(cp)
        for a in range(n):
            copy(a, 0, sibling, me).wait_recv()
        for j, chip in enumerate(chips):
            for a in range(n):
                copy(a, 4 + j, (*chip, 1 - c), me).wait_recv()
        for cp in first + passed:
            cp.wait_send()
        for cp in mine:
            cp.wait()

    any_spec = pl.BlockSpec(memory_space=pl.ANY)
    return pl.pallas_call(
        body,
        name=name,
        out_shape=[jax.ShapeDtypeStruct((N_DEV, *s.shape), s.dtype) for s in shards],
        in_specs=[any_spec] * n,
        out_specs=[any_spec] * n,
        scratch_shapes=[pltpu.SemaphoreType.DMA((7 * n,)), pltpu.SemaphoreType.DMA((7 * n,)),
                        pltpu.SemaphoreType.DMA((n,))],
        compiler_params=_cparams(),
    )(*shards)


def _exchange_hbm(bufs, name):
    n = len(bufs)

    def body(*refs):
        ins, outs = refs[:n], refs[n : 2 * n]
        send_sems, recv_sems, local_sems = refs[2 * n :]
        x, y, c = _place()
        me = _dev_index((x, y, c))
        mine = [pltpu.make_async_copy(ins[a].at[me], outs[a].at[me], local_sems.at[a]) for a in range(n)]
        for cp in mine:
            cp.start()
        def peer_of(k):
            return (1 - x if k & 4 else x, 1 - y if k & 2 else y, 1 - c if k & 1 else c)

        copies = []
        for k in range(1, N_DEV):
            peer = peer_of(k)
            for a in range(n):
                copies.append(pltpu.make_async_remote_copy(
                    src_ref=ins[a].at[_dev_index(peer)], dst_ref=outs[a].at[me],
                    send_sem=send_sems.at[a * 7 + k - 1], recv_sem=recv_sems.at[a * 7 + k - 1],
                    device_id=peer, device_id_type=MESH))
        for cp in copies:
            cp.start()
        for k in range(1, N_DEV):
            peer = peer_of(k)
            for a in range(n):
                pltpu.make_async_remote_copy(
                    src_ref=ins[a].at[me], dst_ref=outs[a].at[_dev_index(peer)],
                    send_sem=send_sems.at[a * 7 + k - 1], recv_sem=recv_sems.at[a * 7 + k - 1],
                    device_id=peer, device_id_type=MESH).wait_recv()
        for cp in copies:
            cp.wait_send()
        for cp in mine:
            cp.wait()

    any_spec = pl.BlockSpec(memory_space=pl.ANY)
    return pl.pallas_call(
        body,
        name=name,
        out_shape=[jax.ShapeDtypeStruct(b.shape, b.dtype) for b in bufs],
        in_specs=[any_spec] * n,
        out_specs=[any_spec] * n,
        scratch_shapes=[pltpu.SemaphoreType.DMA((7 * n,)), pltpu.SemaphoreType.DMA((7 * n,)),
                        pltpu.SemaphoreType.DMA((n,))],
        compiler_params=_cparams(),
    )(*bufs)


def _cols_full(g):
    return jnp.transpose(g, (1, 0, 2)).reshape(g.shape[1], -1)


def _rows_full(g):
    return g.reshape(-1, g.shape[2])


def _cols_parts(full):
    r = full.shape[0]
    return jnp.transpose(full.reshape(r, N_DEV, -1), (1, 0, 2)).astype(BF16)


def _rows_parts(full):
    return full.reshape(N_DEV, -1, full.shape[1]).astype(BF16)


def _block_diag(w):
    eye = jnp.eye(LRU_BLOCKS, dtype=w.dtype)
    return jnp.einsum("nkj,nm->nkmj", w, eye).reshape(LRU_WIDTH, LRU_WIDTH)


def _diag_blocks(full):
    r = full.reshape(LRU_BLOCKS, LRU_BLOCK, LRU_BLOCKS, LRU_BLOCK)
    idx = np.arange(LRU_BLOCKS)
    return r[idx, :, idx, :]


def _pad_rows(v, rows):
    flat = v.reshape(-1)
    return jnp.pad(flat, (0, rows * D_MODEL - flat.shape[0])).reshape(rows, D_MODEL)


def _my_cols(full, me, width):
    return lax.dynamic_slice_in_dim(full, me * width, width, axis=full.ndim - 1)


def kernel(x, c, w_ada, b_ada, norm_pre, norm_post, ffn1_w_gu, ffn1_w_down, w_in, rel_bias, conv_w, conv_b, lru_wa, lru_ba, lru_wx, lru_bx, lru_lambda, w_att_o, w_rec_o, w_out, ffn2_w_gu, ffn2_w_down, loss_target, m_w_ada, m_b_ada, m_norm_pre, m_norm_post, m_ffn1_w_gu, m_ffn1_w_down, m_w_in, m_rel_bias, m_conv_w, m_conv_b, m_lru_wa, m_lru_ba, m_lru_wx, m_lru_bx, m_lru_lambda, m_w_att_o, m_w_rec_o, m_w_out, m_ffn2_w_gu, m_ffn2_w_down, v_w_ada, v_b_ada, v_norm_pre, v_norm_post, v_ffn1_w_gu, v_ffn1_w_down, v_w_in, v_rel_bias, v_conv_w, v_conv_b, v_lru_wa, v_lru_ba, v_lru_wx, v_lru_bx, v_lru_lambda, v_w_att_o, v_w_rec_o, v_w_out, v_ffn2_w_gu, v_ffn2_w_down):
    weights = dict(w_ada=w_ada, b_ada=b_ada, norm_pre=norm_pre, norm_post=norm_post, ffn1_w_gu=ffn1_w_gu,
                   ffn1_w_down=ffn1_w_down, w_in=w_in, rel_bias=rel_bias, conv_w=conv_w, conv_b=conv_b,
                   lru_wa=lru_wa, lru_ba=lru_ba, lru_wx=lru_wx, lru_bx=lru_bx, lru_lambda=lru_lambda,
                   w_att_o=w_att_o, w_rec_o=w_rec_o, w_out=w_out, ffn2_w_gu=ffn2_w_gu, ffn2_w_down=ffn2_w_down)
    mom1 = dict(w_ada=m_w_ada, b_ada=m_b_ada, norm_pre=m_norm_pre, norm_post=m_norm_post, ffn1_w_gu=m_ffn1_w_gu,
                ffn1_w_down=m_ffn1_w_down, w_in=m_w_in, rel_bias=m_rel_bias, conv_w=m_conv_w, conv_b=m_conv_b,
                lru_wa=m_lru_wa, lru_ba=m_lru_ba, lru_wx=m_lru_wx, lru_bx=m_lru_bx, lru_lambda=m_lru_lambda,
                w_att_o=m_w_att_o, w_rec_o=m_w_rec_o, w_out=m_w_out, ffn2_w_gu=m_ffn2_w_gu, ffn2_w_down=m_ffn2_w_down)
    mom2 = dict(w_ada=v_w_ada, b_ada=v_b_ada, norm_pre=v_norm_pre, norm_post=v_norm_post, ffn1_w_gu=v_ffn1_w_gu,
                ffn1_w_down=v_ffn1_w_down, w_in=v_w_in, rel_bias=v_rel_bias, conv_w=v_conv_w, conv_b=v_conv_b,
                lru_wa=v_lru_wa, lru_ba=v_lru_ba, lru_wx=v_lru_wx, lru_bx=v_lru_bx, lru_lambda=v_lru_lambda,
                w_att_o=v_w_att_o, w_rec_o=v_w_rec_o, w_out=v_w_out, ffn2_w_gu=v_ffn2_w_gu, ffn2_w_down=v_ffn2_w_down)
    order = list(weights)
    big = ["ffn1_w_gu", "ffn1_w_down", "w_in", "w_att_o", "w_rec_o", "w_out", "ffn2_w_gu", "ffn2_w_down"]
    col_sharded = {"ffn1_w_gu", "w_in", "w_att_o", "ffn2_w_gu"}
    small = ["b_ada", "norm_pre", "norm_post", "rel_bias", "conv_w", "conv_b", "lru_wa", "lru_ba", "lru_wx",
             "lru_bx", "lru_lambda"]

    xi, yi, ci = _place()
    me = _dev_index((xi, yi, ci))
    x0 = x[0]
    target = loss_target[0]

    pack = jnp.concatenate([c.reshape(-1), norm_pre.reshape(-1), norm_post.reshape(-1), conv_w.reshape(-1)])
    pack = jnp.pad(pack, (0, 3072 - pack.shape[0])).reshape(8, 384)
    got = _allgather_vmem(pack, "gather_small_inputs").reshape(N_DEV, 3072)
    c_all = got[:, :1024]
    unshard = lambda blk, rows: jnp.transpose(blk.reshape(N_DEV, rows, 128), (1, 0, 2)).reshape(rows, D_MODEL)
    g_pre = unshard(got[:, 1024:1408], 3)
    g_post = unshard(got[:, 1408:1792], 3)
    conv_taps = unshard(got[:, 1792:2304], 4)
    conv_w8 = jnp.concatenate([conv_taps, jnp.zeros((4, LRU_WIDTH), F32)], axis=0)

    mod_cols = _ada_fwd(c_all, w_ada[0], "ada_fwd")
    mod_all = _allgather_vmem(mod_cols, "gather_mod").reshape(N_DEV, N_DEV, 1152)
    mod = lax.dynamic_index_in_dim(mod_all, me, axis=1, keepdims=False).reshape(1, -1) + b_ada
    mod = mod.reshape(3, 3, 1, D_MODEL)

    gathered = _allgather_hbm([weights[n][0].astype(BF16) for n in big], "gather_weights")
    full = {n: (_cols_full(g) if n in col_sharded else _rows_full(g)) for n, g in zip(big, gathered)}
    w_in_p = jnp.concatenate([full["w_in"][:, 3 * ATT_WIDTH :], full["w_in"][:, : 3 * ATT_WIDTH]], axis=1)
    w_bd = jnp.concatenate([_block_diag(lru_wa[0]), _block_diag(lru_wx[0])], axis=1).astype(BF16)
    bias = _bias_tile(rel_bias[0])

    res_w = (0.5, 1.0, 0.5)
    row = lambda v: v.reshape(1, -1)

    def ffn_fwd(xin, k, w_gu, w_down, tag):
        h = _pre_fwd(xin, row(g_pre[k]), mod[k, 0], mod[k, 1], f"{tag}_pre")
        gu = _matmul(h, w_gu, "nn", F32, f"{tag}_gu")
        a = _swiglu_fwd(gu, f"{tag}_act")
        f = _matmul(a, w_down, "nn", F32, f"{tag}_down")
        xout = _post_fwd(f, xin, row(g_post[k]), mod[k, 2], res_w[k], f"{tag}_post")
        return xout, (h, gu, a, f)

    x1, saved1 = ffn_fwd(x0, 0, full["ffn1_w_gu"], full["ffn1_w_down"], "ffn1")

    h2 = _pre_fwd(x1, row(g_pre[1]), mod[1, 0], mod[1, 1], "mix_pre")
    proj = _matmul(h2, w_in_p, "nn", F32, "mix_in")
    att_o = _attn_fwd(proj, bias, "attn_fwd")
    xc, xcb = _conv_fwd(proj, conv_w8, conv_b, "conv_fwd")
    pre = _matmul(xcb, w_bd, "nn", F32, "lru_gate_proj")
    a_t, u_t = _gates_fwd(pre, xc, lru_ba, lru_bx, lru_lambda, "lru_gates")
    hs, h_prev = _scan_fwd(a_t, u_t, "lru_scan")
    (rec_in,) = _rowwise(_recin_fn, "rec_in", [], [hs, (proj, LRU_WIDTH, 1)], [(LRU_WIDTH, BF16)])
    att = _matmul(att_o, full["w_att_o"], "nn", F32, "att_out")
    rec = _matmul(rec_in, full["w_rec_o"], "nn", F32, "rec_out")
    (merged,) = _rowwise(_merge_fn, "merge", [], [att, rec, (proj, LRU_WIDTH, 2), (proj, LRU_WIDTH, 3)],
                         [(D_MODEL, BF16)])
    f2 = _matmul(merged, full["w_out"], "nn", F32, "mix_out")
    x2 = _post_fwd(f2, x1, row(g_post[1]), mod[1, 2], res_w[1], "mix_post")

    x3, saved3 = ffn_fwd(x2, 2, full["ffn2_w_gu"], full["ffn2_w_down"], "ffn2")

    dy, sq = _loss_stage(x3, target, "loss")
    loss = lax.psum(0.5 * jnp.sum(sq) / D_MODEL, ("x", "y", "c"))

    grads = {}
    dmod = [[None] * 3 for _ in range(3)]
    d_pre, d_post = [None] * 3, [None] * 3

    def ffn_bwd(xin, k, w_gu, w_down, saved, dout, tag):
        h, gu, a, f = saved
        df, d_post[k], dmod[k][2] = _post_bwd(f, row(g_post[k]), mod[k, 2], res_w[k], dout, f"{tag}_post_bwd")
        da = _matmul(df, w_down, "nt", F32, f"{tag}_da")
        dw_down = _matmul(a, df, "tn", F32, f"{tag}_dw_down", tm=1408, tn=1024, tk=1024)
        dgu = _swiglu_bwd(gu, da, f"{tag}_act_bwd")
        dh = _matmul(dgu, w_gu, "nt", F32, f"{tag}_dh")
        dw_gu = _matmul(h, dgu, "tn", F32, f"{tag}_dw_gu", tm=1024, tn=1408, tk=1024)
        dx, d_pre[k], dmod[k][0], dmod[k][1] = _pre_bwd(xin, row(g_pre[k]), mod[k, 0], mod[k, 1], dh, dout,
                                                         f"{tag}_pre_bwd")
        return dx, dw_gu, dw_down

    dx2, grads["ffn2_w_gu"], grads["ffn2_w_down"] = ffn_bwd(x2, 2, full["ffn2_w_gu"], full["ffn2_w_down"], saved3,
                                                            dy, "ffn2")

    df2, d_post[1], dmod[1][2] = _post_bwd(f2, row(g_post[1]), mod[1, 2], res_w[1], dx2, "mix_post_bwd")
    dmerged = _matmul(df2, full["w_out"], "nt", F32, "mix_dmerged")
    grads["w_out"] = _matmul(merged, df2, "tn", F32, "mix_dw_out", tm=1024, tn=1024, tk=1024)

    def merge_bwd(att, rec, g_att, g_rec, dm):
        _, vjp = jax.vjp(_merge_fn, att, rec, g_att, g_rec)
        return vjp(dm)

    datt, drec, dg_att, dg_rec = _rowwise(
        merge_bwd, "merge_bwd", [], [att, rec, (proj, LRU_WIDTH, 2), (proj, LRU_WIDTH, 3), dmerged],
        [(D_MODEL, BF16)] * 4)
    datt_o = _matmul(datt, full["w_att_o"], "nt", BF16, "att_out_bwd")
    grads["w_att_o"] = _matmul(att_o, datt, "tn", F32, "dw_att_o", tm=512, tn=1024, tk=1024)
    drec_in = _matmul(drec, full["w_rec_o"], "nt", F32, "rec_out_bwd")
    grads["w_rec_o"] = _matmul(rec_in, drec, "tn", F32, "dw_rec_o", tm=1024, tn=1024, tk=1024)

    def recin_bwd(hs, yr, d):
        _, vjp = jax.vjp(_recin_fn, hs, yr)
        return vjp(d)

    dhs, dyr = _rowwise(recin_bwd, "rec_in_bwd", [], [hs, (proj, LRU_WIDTH, 1), drec_in],
                        [(LRU_WIDTH, F32), (LRU_WIDTH, BF16)])
    g_t = _scan_bwd(a_t, dhs, "lru_scan_bwd")
    dpre, dxc_direct, d_ba, d_bx, d_lam = _gates_bwd(pre, xc, lru_ba, lru_bx, lru_lambda, g_t, h_prev,
                                                     "lru_gates_bwd")
    dxc_mm = _matmul(dpre, w_bd, "nt", F32, "lru_gate_proj_bwd")
    dw_bd = _matmul(xcb, dpre, "tn", F32, "dw_lru_gate", tm=1024, tn=1024, tk=1024)
    dxr, d_conv_w8, d_conv_b = _conv_bwd(proj, conv_w8, dxc_direct, dxc_mm, "conv_bwd")
    dq, dk, dv, dbias = _attn_bwd(proj, bias, datt_o, "attn_bwd")
    dproj = jnp.concatenate([dxr, dyr, dg_att, dg_rec, dq, dk, dv], axis=1)
    dh2 = _matmul(dproj, w_in_p, "nt", F32, "mix_dh")
    dw_in_p = _matmul(h2, dproj, "tn", F32, "mix_dw_in", tm=1024, tn=1408, tk=1024)
    grads["w_in"] = jnp.concatenate([dw_in_p[:, OFF_Q:], dw_in_p[:, :OFF_Q]], axis=1)
    dx1, d_pre[1], dmod[1][0], dmod[1][1] = _pre_bwd(x1, row(g_pre[1]), mod[1, 0], mod[1, 1], dh2, dx2, "mix_pre_bwd")

    dx0, grads["ffn1_w_gu"], grads["ffn1_w_down"] = ffn_bwd(x0, 0, full["ffn1_w_gu"], full["ffn1_w_down"], saved1,
                                                            dx1, "ffn1")

    dmod_mine = jnp.concatenate([dmod[k][j] for k in range(3) for j in range(3)], axis=0)
    pieces = [
        dmod_mine,
        jnp.concatenate(d_pre, axis=0),
        jnp.concatenate(d_post, axis=0),
        d_conv_w8[:4],
        d_conv_b, d_ba, d_bx, d_lam,
        _pad_rows(_bias_grad(dbias), 3),
        _diag_blocks(dw_bd[:, :LRU_WIDTH]).reshape(64, D_MODEL),
        _diag_blocks(dw_bd[:, LRU_WIDTH:]).reshape(64, D_MODEL),
    ]
    small_rows = 160
    packed = jnp.concatenate(pieces, axis=0)
    packed = jnp.pad(packed, ((0, small_rows - packed.shape[0]), (0, 0)))
    parts = _allgather_vmem(packed, "gather_small_grads").reshape(N_DEV, small_rows, D_MODEL)
    total = _sum_parts(parts, "sum_small_grads")
    grads["b_ada"] = total[0:9].reshape(1, -1)
    grads["norm_pre"] = _my_cols(total[9:12], me, 128)
    grads["norm_post"] = _my_cols(total[12:15], me, 128)
    grads["conv_w"] = _my_cols(total[15:19], me, 128)
    grads["conv_b"] = total[19:20]
    grads["lru_ba"] = total[20:21]
    grads["lru_bx"] = total[21:22]
    grads["lru_lambda"] = total[22:23]
    grads["rel_bias"] = total[23:26].reshape(-1)[: ATT_HEADS * (2 * MAX_REL + 1)].reshape(ATT_HEADS, -1)
    grads["lru_wa"] = total[26:90].reshape(LRU_BLOCKS, LRU_BLOCK, LRU_BLOCK)
    grads["lru_wx"] = total[90:154].reshape(LRU_BLOCKS, LRU_BLOCK, LRU_BLOCK)
    dmod_all = parts[:, 0:9, :].reshape(N_DEV, 9 * D_MODEL)
    grads["w_ada"] = _ada_bwd(c_all, _my_cols(dmod_all, me, 1152), "ada_bwd")

    send = [(_cols_parts if n in col_sharded else _rows_parts)(grads[n]) for n in big]
    recv = _exchange_hbm(send, "exchange_grads")

    out_g, out_d, out_m, out_v = {}, {}, {}, {}
    for n, parts_n in zip(big, recv):
        res = _adamw(parts_n, weights[n][0], mom1[n][0], mom2[n][0], f"adamw_{n}")
        out_g[n], out_d[n], out_m[n], out_v[n] = [r.reshape(weights[n].shape) for r in res]
    res = _adamw(grads["w_ada"], w_ada[0], m_w_ada[0], v_w_ada[0], "adamw_w_ada")
    out_g["w_ada"], out_d["w_ada"], out_m["w_ada"], out_v["w_ada"] = [r.reshape(w_ada.shape) for r in res]

    sizes = [int(np.prod(weights[n].shape)) for n in small]
    tot = sum(sizes)
    rows_small = -(-tot // (16 * D_MODEL)) * 16
    flat = lambda arrs: jnp.pad(jnp.concatenate([a.reshape(-1) for a in arrs]),
                                (0, rows_small * D_MODEL - tot)).reshape(rows_small, D_MODEL)
    res = _adamw(flat([grads[n] for n in small]), flat([weights[n] for n in small]),
                 flat([mom1[n] for n in small]), flat([mom2[n] for n in small]), "adamw_small", rows=rows_small)
    offs = np.cumsum([0] + sizes)
    for dst, r in zip((out_g, out_d, out_m, out_v), res):
        rf = r.reshape(-1)
        for i, n in enumerate(small):
            dst[n] = rf[offs[i] : offs[i + 1]].reshape(weights[n].shape)

    return (loss, dx0[None], *[out_g[n] for n in order], *[out_d[n] for n in order],
            *[out_m[n] for n in order], *[out_v[n] for n in order])
```

```python
import functools

import jax
import jax.numpy as jnp
import numpy as np
from jax import lax
from jax.experimental import pallas as pl
from jax.experimental.pallas import tpu as pltpu

D_MODEL = 1024
D_FF = 2816
ATT_HEADS = 8
ATT_HEAD_DIM = 64
ATT_WIDTH = 512
CHUNK = 64
LEFT_CHUNKS = 8
MAX_REL = 128
LRU_WIDTH = 1024
LRU_BLOCKS = 16
LRU_BLOCK = 64
LRU_C = 8.0
EPS = 1e-6
PROJ_WIDTH = 5632
N_DEV = 8

ADAM_LR = 0.001
ADAM_B1 = 0.9
ADAM_B2 = 0.999
ADAM_EPS = 1e-08
ADAM_WD = 0.01
ADAM_STEP = 10

V7X_LANES = 128
V7X_SUBLANES = 8
V7X_VMEM_BYTES = 64 * 1024 * 1024
VMEM_LIMIT = V7X_VMEM_BYTES - 8 * 1024 * 1024

ATT_TQ = 256
NEG = -1e30
BF16 = jnp.bfloat16
F32 = jnp.float32
MESH = pl.DeviceIdType.MESH

OFF_Q = 4 * LRU_WIDTH
OFF_K = OFF_Q + ATT_WIDTH
OFF_V = OFF_K + ATT_WIDTH


def _cparams(**kw):
    return pltpu.CompilerParams(vmem_limit_bytes=VMEM_LIMIT, **kw)


def _pick(n, target, unit=V7X_LANES):
    best = None
    for t in range(unit, min(n, target) + 1, unit):
        if n % t == 0:
            best = t
    return n if best is None else best


_DIMS = {
    "nn": (((1,), (0,)), ((), ())),
    "nt": (((1,), (1,)), ((), ())),
    "tn": (((0,), (0,)), ((), ())),
}


def _matmul(a, b, mode, out_dtype, name, tm=1024, tn=512, tk=1408):
    if mode == "nn":
        (m, k), (k2, n) = a.shape, b.shape
    elif mode == "nt":
        (m, k), (n, k2) = a.shape, b.shape
    else:
        (k, m), (k2, n) = a.shape, b.shape
    assert k == k2, (a.shape, b.shape, mode)
    tm, tn, tk = _pick(m, tm), _pick(n, tn), _pick(k, tk)
    nk = k // tk
    dims = _DIMS[mode]

    def body(a_ref, b_ref, o_ref, *scratch):
        p = lax.dot_general(a_ref[...], b_ref[...], dims, preferred_element_type=F32)
        if nk == 1:
            o_ref[...] = p.astype(o_ref.dtype)
        else:
            acc = scratch[0]
            kk = pl.program_id(2)

            @pl.when(kk == 0)
            def _():
                acc[...] = p

            @pl.when(kk > 0)
            def _():
                acc[...] += p

            @pl.when(kk == nk - 1)
            def _():
                o_ref[...] = acc[...].astype(o_ref.dtype)

    if mode == "nn":
        a_spec = pl.BlockSpec((tm, tk), lambda i, j, kk: (i, kk))
        b_spec = pl.BlockSpec((tk, tn), lambda i, j, kk: (kk, j))
    elif mode == "nt":
        a_spec = pl.BlockSpec((tm, tk), lambda i, j, kk: (i, kk))
        b_spec = pl.BlockSpec((tn, tk), lambda i, j, kk: (j, kk))
    else:
        a_spec = pl.BlockSpec((tk, tm), lambda i, j, kk: (kk, i))
        b_spec = pl.BlockSpec((tk, tn), lambda i, j, kk: (kk, j))
    return pl.pallas_call(
        body,
        name=name,
        grid=(m // tm, n // tn, nk),
        in_specs=[a_spec, b_spec],
        out_specs=pl.BlockSpec((tm, tn), lambda i, j, kk: (i, j)),
        out_shape=jax.ShapeDtypeStruct((m, n), out_dtype),
        scratch_shapes=[pltpu.VMEM((tm, tn), F32)] if nk > 1 else [],
        compiler_params=_cparams(dimension_semantics=("parallel", "parallel", "arbitrary")),
    )(a, b)


def _rowwise(fn, name, params, tiles, outs, accs=(), ts=256, with_index=False):
    norm = []
    for t in tiles:
        if not isinstance(t, tuple):
            t = (t, t.shape[1], 0)
        norm.append(t if len(t) == 4 else (*t, None))
    s = norm[0][0].shape[0]
    ts = min(ts, s)
    assert s % ts == 0 and ts % V7X_SUBLANES == 0
    steps = s // ts
    halo_blocks = ts // V7X_SUBLANES
    n_p, n_t, n_o = len(params), len(norm), len(outs)

    def body(*refs):
        i = pl.program_id(0)
        vals = [r[...] for r in refs[: n_p + n_t]]
        res = fn(i, steps, *vals) if with_index else fn(*vals)
        if not isinstance(res, (tuple, list)):
            res = (res,)
        o_refs = refs[n_p + n_t : n_p + n_t + n_o]
        a_refs = refs[n_p + n_t + n_o :]
        for r, v in zip(o_refs, res[:n_o]):
            r[...] = v.astype(r.dtype)
        for r, v in zip(a_refs, res[n_o:]):
            _accumulate(r, v, i)

    in_specs = [pl.BlockSpec(p.shape, lambda i: (0, 0)) for p in params]
    for arr, w, cb, halo in norm:
        if halo is None:
            in_specs.append(pl.BlockSpec((ts, w), lambda i, cb=cb: (i, cb)))
        elif halo == "prev":
            in_specs.append(
                pl.BlockSpec((V7X_SUBLANES, w), lambda i, cb=cb: (jnp.maximum(i * halo_blocks - 1, 0), cb))
            )
        else:
            last = s // V7X_SUBLANES - 1
            in_specs.append(
                pl.BlockSpec((V7X_SUBLANES, w), lambda i, cb=cb: (jnp.minimum((i + 1) * halo_blocks, last), cb))
            )
    out_specs = [pl.BlockSpec((ts, w), lambda i: (i, 0)) for w, _ in outs]
    out_specs += [pl.BlockSpec(shape, lambda i: (0, 0)) for shape in accs]
    out_shape = [jax.ShapeDtypeStruct((s, w), dt) for w, dt in outs]
    out_shape += [jax.ShapeDtypeStruct(shape, F32) for shape in accs]
    res = pl.pallas_call(
        body,
        name=name,
        grid=(steps,),
        in_specs=in_specs,
        out_specs=out_specs,
        out_shape=out_shape,
        compiler_params=_cparams(dimension_semantics=("arbitrary",)),
    )(*params, *[t[0] for t in norm])
    return res


def _accumulate(ref, val, step):
    @pl.when(step == 0)
    def _():
        ref[...] = val

    @pl.when(step > 0)
    def _():
        ref[...] += val


def _sigmoid(z):
    return jax.nn.sigmoid(z)


def _silu(z):
    return z * _sigmoid(z)


def _gelu(z):
    return 0.5 * z * (1.0 + jnp.tanh(0.7978845608028654 * (z + 0.044715 * (z * z * z))))


def _pre_fn(g, shift, scale, x):
    r = lax.rsqrt(jnp.mean(x * x, axis=-1, keepdims=True) + EPS)
    return ((x * r) * g) * (1.0 + scale) + shift


def _post_fn(res_w, g, gate, f, x):
    r = lax.rsqrt(jnp.mean(f * f, axis=-1, keepdims=True) + EPS)
    return x + (res_w * gate) * ((f * r) * g)


def _swiglu_fn(gu):
    return _silu(gu[:, :D_FF]) * gu[:, D_FF:]


def _gates_fn(ba, bx, lam, pre, xc):
    ra = _sigmoid(pre[:, :LRU_WIDTH] + ba)
    ia = _sigmoid(pre[:, LRU_WIDTH:] + bx)
    softplus = jnp.maximum(-lam, 0.0) + jnp.log1p(jnp.exp(-jnp.abs(lam)))
    log_a = (-LRU_C) * ra * softplus
    a = jnp.exp(log_a)
    mult = jnp.sqrt(-jnp.tanh(log_a) * (a * a + 1.0))
    return a, mult * (ia * xc)


def _recin_fn(hs, yr):
    return hs * _gelu(yr)


def _merge_fn(att, rec, g_att, g_rec):
    return _sigmoid(g_att) * att + _sigmoid(g_rec) * rec


def _rowsum(v):
    return jnp.sum(v, axis=0, keepdims=True)


def _pre_fwd(x, g, shift, scale, name):
    (h,) = _rowwise(_pre_fn, name, [g, shift, scale], [x], [(D_MODEL, BF16)])
    return h


def _pre_bwd(x, g, shift, scale, dh, dres, name):
    def fn(g, shift, scale, x, dh, dres):
        _, vjp = jax.vjp(_pre_fn, g, shift, scale, x)
        dg, dshift, dscale, dx = vjp(dh)
        return dx + dres, dg, dshift, dscale

    row = (1, D_MODEL)
    return _rowwise(fn, name, [g, shift, scale], [x, dh, dres], [(D_MODEL, F32)], [row, row, row])


def _post_fwd(f, x, g, gate, res_w, name):
    (y,) = _rowwise(functools.partial(_post_fn, res_w), name, [g, gate], [f, x], [(D_MODEL, F32)])
    return y


def _post_bwd(f, g, gate, res_w, dy, name):
    def fn(g, gate, f, dy):
        _, vjp = jax.vjp(lambda g, gate, f: _post_fn(res_w, g, gate, f, 0.0), g, gate, f)
        dg, dgate, df = vjp(dy)
        return df, dg, dgate

    row = (1, D_MODEL)
    return _rowwise(fn, name, [g, gate], [f, dy], [(D_MODEL, BF16)], [row, row])


def _loss_stage(y, target, name):
    def fn(y, t):
        diff = y - t
        return diff * (1.0 / D_MODEL), _rowsum(diff * diff)

    return _rowwise(fn, name, [], [y, target], [(D_MODEL, F32)], [(1, D_MODEL)])


def _swiglu_fwd(gu, name):
    (a,) = _rowwise(_swiglu_fn, name, [], [gu], [(D_FF, BF16)], ts=128)
    return a


def _swiglu_bwd(gu, da, name):
    def fn(gu, da):
        _, vjp = jax.vjp(_swiglu_fn, gu)
        return vjp(da)[0]

    (dgu,) = _rowwise(fn, name, [], [gu, da], [(2 * D_FF, BF16)], ts=128)
    return dgu


def _shift_down(ext, j, rows):
    return pltpu.roll(ext, j, 0)[V7X_SUBLANES : V7X_SUBLANES + rows]


def _shift_up(ext, j, rows):
    return pltpu.roll(ext, ext.shape[0] - j, 0)[:rows] if j else ext[:rows]


def _conv_fwd(proj, w8, b, name):
    def fn(i, steps, w8, b, x, halo):
        halo = jnp.where(i > 0, halo, 0.0)
        ext = jnp.concatenate([halo, x], axis=0)
        acc = b + w8[3:4] * x
        for j in (1, 2, 3):
            acc = acc + w8[3 - j : 4 - j] * _shift_down(ext, j, x.shape[0])
        return acc, acc

    tiles = [(proj, LRU_WIDTH, 0), (proj, LRU_WIDTH, 0, "prev")]
    return _rowwise(fn, name, [w8, b], tiles, [(LRU_WIDTH, F32), (LRU_WIDTH, BF16)], with_index=True)


def _conv_bwd(proj, w8, d1, d2, name):
    def fn(i, steps, w8, x, halo, d1, d1n, d2, d2n):
        rows = x.shape[0]
        d = d1 + d2
        dn = jnp.where(i < steps - 1, d1n + d2n, 0.0)
        halo = jnp.where(i > 0, halo, 0.0)
        dext = jnp.concatenate([d, dn], axis=0)
        xext = jnp.concatenate([halo, x], axis=0)
        dx = w8[3:4] * d
        dw = [None] * 4
        dw[3] = _rowsum(d * x)
        for k in (1, 2, 3):
            dx = dx + w8[3 - k : 4 - k] * _shift_up(dext, k, rows)
            dw[3 - k] = _rowsum(d * _shift_down(xext, k, rows))
        dw8 = jnp.concatenate(dw + [jnp.zeros((4, LRU_WIDTH), F32)], axis=0)
        return dx, dw8, _rowsum(d)

    tiles = [(proj, LRU_WIDTH, 0), (proj, LRU_WIDTH, 0, "prev"), d1, (d1, LRU_WIDTH, 0, "next"),
             d2, (d2, LRU_WIDTH, 0, "next")]
    return _rowwise(fn, name, [w8], tiles, [(LRU_WIDTH, BF16)], [(8, LRU_WIDTH), (1, LRU_WIDTH)], with_index=True)


def _gates_fwd(pre, xc, ba, bx, lam, name):
    return _rowwise(_gates_fn, name, [ba, bx, lam], [pre, xc], [(LRU_WIDTH, F32), (LRU_WIDTH, F32)])


def _gates_bwd(pre, xc, ba, bx, lam, g, h_prev, name):
    def fn(ba, bx, lam, pre, xc, g, h_prev):
        _, vjp = jax.vjp(_gates_fn, ba, bx, lam, pre, xc)
        dba, dbx, dlam, dpre, dxc = vjp((g * h_prev, g))
        return dpre, dxc, dba, dbx, dlam

    row = (1, LRU_WIDTH)
    return _rowwise(fn, name, [ba, bx, lam], [pre, xc, g, h_prev],
                    [(2 * LRU_WIDTH, BF16), (LRU_WIDTH, F32)], [row, row, row])


SCAN_ROWS = 512


def _block_scan(a, b, row, reverse):
    for d in (1, 2, 4):
        if reverse:
            shift, keep = V7X_SUBLANES - d, row < V7X_SUBLANES - d
        else:
            shift, keep = d, row >= d
        a_s = pltpu.roll(a, shift, 0)
        b_s = pltpu.roll(b, shift, 0)
        b = jnp.where(keep, a * b_s + b, b)
        a = jnp.where(keep, a * a_s, a)
    return a, b


def _scan_fwd(a, u, name):
    s, w = a.shape
    ts = min(SCAN_ROWS, s)
    sub = ts // V7X_SUBLANES

    def body(a_ref, u_ref, h_ref, hp_ref, carry):
        @pl.when(pl.program_id(0) == 0)
        def _():
            carry[...] = jnp.zeros_like(carry)

        row = lax.broadcasted_iota(jnp.int32, (V7X_SUBLANES, w), 0)

        def step(j, c):
            rows = pl.ds(pl.multiple_of(j * V7X_SUBLANES, V7X_SUBLANES), V7X_SUBLANES)
            pa, pb = _block_scan(a_ref[rows, :], u_ref[rows, :], row, False)
            h = pb + pa * c
            h_ref[rows, :] = h
            hp_ref[rows, :] = jnp.where(row >= 1, pltpu.roll(h, 1, 0), c)
            return jnp.broadcast_to(h[V7X_SUBLANES - 1 :], (V7X_SUBLANES, w))

        carry[...] = lax.fori_loop(0, sub, step, carry[...])

    spec = pl.BlockSpec((ts, w), lambda i: (i, 0))
    return pl.pallas_call(
        body,
        name=name,
        grid=(s // ts,),
        in_specs=[spec, spec],
        out_specs=[spec, spec],
        out_shape=[jax.ShapeDtypeStruct((s, w), F32)] * 2,
        scratch_shapes=[pltpu.VMEM((V7X_SUBLANES, w), F32)],
        compiler_params=_cparams(dimension_semantics=("arbitrary",)),
    )(a, u)


def _scan_bwd(a, dh, name):
    s, w = a.shape
    ts = min(SCAN_ROWS, s)
    sub = ts // V7X_SUBLANES
    steps = s // ts

    def body(a_ref, d_ref, g_ref, carry):
        @pl.when(pl.program_id(0) == 0)
        def _():
            carry[...] = jnp.zeros_like(carry)

        row = lax.broadcasted_iota(jnp.int32, (V7X_SUBLANES, w), 0)

        def step(jj, c):
            j = sub - 1 - jj
            rows = pl.ds(pl.multiple_of(j * V7X_SUBLANES, V7X_SUBLANES), V7X_SUBLANES)
            av, dv = a_ref[rows, :], d_ref[rows, :]
            pa, pb = _block_scan(av, av * dv, row, True)
            big = pb + pa * c
            g_ref[rows, :] = dv + jnp.where(row < V7X_SUBLANES - 1, pltpu.roll(big, V7X_SUBLANES - 1, 0), c)
            return jnp.broadcast_to(big[:1], (V7X_SUBLANES, w))

        carry[...] = lax.fori_loop(0, sub, step, carry[...])

    spec = pl.BlockSpec((ts, w), lambda i: (steps - 1 - i, 0))
    return pl.pallas_call(
        body,
        name=name,
        grid=(steps,),
        in_specs=[spec, spec],
        out_specs=spec,
        out_shape=jax.ShapeDtypeStruct((s, w), F32),
        scratch_shapes=[pltpu.VMEM((V7X_SUBLANES, w), F32)],
        compiler_params=_cparams(dimension_semantics=("arbitrary",)),
    )(a, dh)


def _rel_index():
    i = np.arange(ATT_TQ)[:, None]
    j = np.arange(3 * ATT_TQ)[None, :]
    band = (j // CHUNK >= i // CHUNK) & (j // CHUNK <= i // CHUNK + LEFT_CHUNKS)
    return band


SKEW = 4 * ATT_TQ


def _skew_onehot():
    t = np.arange(SKEW)
    diag = np.where(t < 3 * ATT_TQ, -t, SKEW - t)
    idx = np.clip(diag + LEFT_CHUNKS * CHUNK, -MAX_REL, MAX_REL) + MAX_REL
    hit = (idx[:, None] == np.arange(2 * MAX_REL + 1)[None, :]) & (t[:, None] != 3 * ATT_TQ)
    return hit.astype(np.float32)


def _bias_tile(rel_bias):
    per_t = jnp.dot(rel_bias, jnp.asarray(_skew_onehot()).T, precision=lax.Precision.HIGHEST)
    flat = jnp.broadcast_to(per_t[:, None, :], (ATT_HEADS, ATT_TQ, SKEW)).reshape(ATT_HEADS, ATT_TQ * SKEW)
    tile = flat[:, : ATT_TQ * (SKEW - 1)].reshape(ATT_HEADS, ATT_TQ, SKEW - 1)[:, :, : 3 * ATT_TQ]
    return jnp.where(jnp.asarray(_rel_index())[None], tile, NEG)


def _bias_grad(dbias):
    flat = jnp.pad(dbias, ((0, 0), (0, 0), (0, SKEW - 1 - 3 * ATT_TQ))).reshape(ATT_HEADS, ATT_TQ * (SKEW - 1))
    per_t = jnp.sum(jnp.pad(flat, ((0, 0), (0, ATT_TQ))).reshape(ATT_HEADS, ATT_TQ, SKEW), axis=1)
    return jnp.dot(per_t, jnp.asarray(_skew_onehot()), precision=lax.Precision.HIGHEST)


def _attn_specs(nt):
    qb, kb, vb = OFF_Q // V7X_LANES, OFF_K // V7X_LANES, OFF_V // V7X_LANES
    blk = (ATT_TQ, V7X_LANES)

    def qmap(base):
        return lambda hp, m: (jnp.minimum(m, nt - 1), base + hp)

    def wmap(base, back):
        return lambda hp, m: (jnp.clip(m - back, 0, nt - 1), base + hp)

    specs = [pl.BlockSpec(blk, qmap(qb))]
    specs += [pl.BlockSpec(blk, wmap(kb, back)) for back in (2, 1, 0)]
    specs += [pl.BlockSpec(blk, wmap(vb, back)) for back in (2, 1, 0)]
    return specs


def _attn_probs(qh, kh, bias, ok):
    s = lax.dot_general(qh, kh, _DIMS["nt"], preferred_element_type=F32) * (ATT_HEAD_DIM**-0.5) + bias
    s = jnp.where(ok, s, NEG)
    e = jnp.exp(s - jnp.max(s, axis=-1, keepdims=True))
    return e / jnp.sum(e, axis=-1, keepdims=True)


def _attn_window(m, k0, k1, k2, v0, v1, v2):
    k = jnp.concatenate([k0[...], k1[...], k2[...]], axis=0).astype(BF16)
    v = jnp.concatenate([v0[...], v1[...], v2[...]], axis=0).astype(BF16)
    kpos = lax.broadcasted_iota(jnp.int32, (ATT_TQ, 3 * ATT_TQ), 1) + (m - 2) * ATT_TQ
    return k, v, kpos >= 0


def _attn_fwd(proj, bias, name):
    s = proj.shape[0]
    nt = s // ATT_TQ

    def body(q_ref, k0, k1, k2, v0, v1, v2, b_ref, o_ref):
        m = pl.program_id(1)
        k, v, ok = _attn_window(m, k0, k1, k2, v0, v1, v2)
        q = q_ref[...].astype(BF16)
        for hh in range(2):
            cols = slice(hh * ATT_HEAD_DIM, (hh + 1) * ATT_HEAD_DIM)
            p = _attn_probs(q[:, cols], k[:, cols], b_ref[hh], ok)
            o = jnp.dot(p.astype(BF16), v[:, cols], preferred_element_type=F32)
            o_ref[:, cols] = o.astype(o_ref.dtype)

    specs = _attn_specs(nt) + [pl.BlockSpec((2, ATT_TQ, 3 * ATT_TQ), lambda hp, m: (hp, 0, 0))]
    return pl.pallas_call(
        body,
        name=name,
        grid=(ATT_HEADS // 2, nt),
        in_specs=specs,
        out_specs=pl.BlockSpec((ATT_TQ, V7X_LANES), lambda hp, m: (m, hp)),
        out_shape=jax.ShapeDtypeStruct((s, ATT_WIDTH), BF16),
        compiler_params=_cparams(dimension_semantics=("parallel", "arbitrary")),
    )(proj, proj, proj, proj, proj, proj, proj, bias)


def _attn_bwd(proj, bias, do, name):
    s = proj.shape[0]
    nt = s // ATT_TQ
    win = 3 * ATT_TQ

    def body(q_ref, k0, k1, k2, v0, v1, v2, do_ref, b_ref, dq_ref, dk_ref, dv_ref, db_ref, dk_acc, dv_acc):
        m = pl.program_id(1)

        @pl.when(m == 0)
        def _():
            dk_acc[...] = jnp.zeros_like(dk_acc)
            dv_acc[...] = jnp.zeros_like(dv_acc)
            db_ref[...] = jnp.zeros_like(db_ref)

        @pl.when(m < nt)
        def _():
            k, v, ok = _attn_window(m, k0, k1, k2, v0, v1, v2)
            q = q_ref[...].astype(BF16)
            dout = do_ref[...]
            for hh in range(2):
                cols = slice(hh * ATT_HEAD_DIM, (hh + 1) * ATT_HEAD_DIM)
                qh, kh, vh, doh = q[:, cols], k[:, cols], v[:, cols], dout[:, cols]
                p = _attn_probs(qh, kh, b_ref[hh], ok)
                dvh = lax.dot_general(p.astype(BF16), doh, _DIMS["tn"], preferred_element_type=F32)
                dp = lax.dot_general(doh, vh, _DIMS["nt"], preferred_element_type=F32)
                ds = p * (dp - jnp.sum(dp * p, axis=-1, keepdims=True))
                db_ref[hh] += ds
                dsb = ds.astype(BF16)
                dqh = jnp.dot(dsb, kh, preferred_element_type=F32) * (ATT_HEAD_DIM**-0.5)
                dkh = lax.dot_general(dsb, qh, _DIMS["tn"], preferred_element_type=F32) * (ATT_HEAD_DIM**-0.5)
                dq_ref[:, cols] = dqh.astype(dq_ref.dtype)
                dk_acc[:, cols] += dkh
                dv_acc[:, cols] += dvh

        dk_ref[...] = dk_acc[:ATT_TQ].astype(dk_ref.dtype)
        dv_ref[...] = dv_acc[:ATT_TQ].astype(dv_ref.dtype)
        for acc in (dk_acc, dv_acc):
            rest = acc[ATT_TQ:]
            acc[: win - ATT_TQ] = rest
            acc[win - ATT_TQ :] = jnp.zeros((ATT_TQ, V7X_LANES), F32)

    blk = (ATT_TQ, V7X_LANES)
    specs = _attn_specs(nt)
    specs.append(pl.BlockSpec(blk, lambda hp, m: (jnp.minimum(m, nt - 1), hp)))
    specs.append(pl.BlockSpec((2, ATT_TQ, win), lambda hp, m: (hp, 0, 0)))
    done = lambda hp, m: (jnp.maximum(m - 2, 0), hp)
    out_specs = [
        pl.BlockSpec(blk, lambda hp, m: (jnp.minimum(m, nt - 1), hp)),
        pl.BlockSpec(blk, done),
        pl.BlockSpec(blk, done),
        pl.BlockSpec((2, ATT_TQ, win), lambda hp, m: (hp, 0, 0)),
    ]
    out_shape = [jax.ShapeDtypeStruct((s, ATT_WIDTH), BF16)] * 3
    out_shape.append(jax.ShapeDtypeStruct((ATT_HEADS, ATT_TQ, win), F32))
    return pl.pallas_call(
        body,
        name=name,
        grid=(ATT_HEADS // 2, nt + 2),
        in_specs=specs,
        out_specs=out_specs,
        out_shape=out_shape,
        scratch_shapes=[pltpu.VMEM((win, V7X_LANES), F32), pltpu.VMEM((win, V7X_LANES), F32)],
        compiler_params=_cparams(dimension_semantics=("arbitrary", "arbitrary")),
    )(proj, proj, proj, proj, proj, proj, proj, do, bias)


def _ada_fwd(c_all, w, name):
    def body(c_ref, w_ref, o_ref):
        act = _silu(c_ref[...]).astype(BF16)
        o_ref[...] = jnp.dot(act, w_ref[...].astype(BF16), preferred_element_type=F32)

    return pl.pallas_call(
        body, name=name, out_shape=jax.ShapeDtypeStruct((c_all.shape[0], w.shape[1]), F32), compiler_params=_cparams()
    )(c_all, w)


def _ada_bwd(c_all, dmod, name):
    def body(c_ref, d_ref, o_ref):
        act = _silu(c_ref[...])
        o_ref[...] = lax.dot_general(act, d_ref[...], _DIMS["tn"], preferred_element_type=F32,
                                     precision=lax.Precision.HIGHEST)

    return pl.pallas_call(
        body, name=name, out_shape=jax.ShapeDtypeStruct((c_all.shape[1], dmod.shape[1]), F32), compiler_params=_cparams()
    )(c_all, dmod)


def _adamw(g, w, m, v, name, rows=256):
    r, c = w.shape
    tr = _pick(r, rows, 16)
    parts = g.ndim == 3

    def body(g_ref, w_ref, m_ref, v_ref, go_ref, d_ref, mo_ref, vo_ref):
        if parts:
            grad = g_ref[0].astype(F32)
            for d in range(1, N_DEV):
                grad = grad + g_ref[d].astype(F32)
        else:
            grad = g_ref[...]
        m2 = ADAM_B1 * m_ref[...] + (1.0 - ADAM_B1) * grad
        v2 = ADAM_B2 * v_ref[...] + (1.0 - ADAM_B2) * (grad * grad)
        m_hat = m2 / (1.0 - ADAM_B1**ADAM_STEP)
        v_hat = v2 / (1.0 - ADAM_B2**ADAM_STEP)
        go_ref[...] = grad
        d_ref[...] = -ADAM_LR * (m_hat / (jnp.sqrt(v_hat) + ADAM_EPS) + ADAM_WD * w_ref[...])
        mo_ref[...] = m2
        vo_ref[...] = v2

    spec = pl.BlockSpec((tr, c), lambda i: (i, 0))
    g_spec = pl.BlockSpec((N_DEV, tr, c), lambda i: (0, i, 0)) if parts else spec
    return pl.pallas_call(
        body,
        name=name,
        grid=(r // tr,),
        in_specs=[g_spec, spec, spec, spec],
        out_specs=[spec] * 4,
        out_shape=[jax.ShapeDtypeStruct((r, c), F32)] * 4,
        compiler_params=_cparams(dimension_semantics=("parallel",)),
    )(g, w, m, v)


def _sum_parts(parts, name):
    def body(p_ref, o_ref):
        acc = p_ref[0]
        for d in range(1, N_DEV):
            acc = acc + p_ref[d]
        o_ref[...] = acc

    return pl.pallas_call(
        body, name=name, out_shape=jax.ShapeDtypeStruct(parts.shape[1:], F32), compiler_params=_cparams()
    )(parts)


def _place():
    x, y, c = lax.axis_index("x"), lax.axis_index("y"), lax.axis_index("c")
    return x, y, c


def _dev_index(p):
    return 4 * p[0] + 2 * p[1] + p[2]


def _allgather_vmem(shard, name):
    m_per, n = shard.shape

    def body(x_ref, out_ref, send_sems, recv_sems, local_sem):
        x, y, c = _place()
        me, sibling = (x, y, c), (x, y, 1 - c)
        chips = [(1 - x, y), (x, 1 - y), (1 - x, 1 - y)]

        def rows(p):
            return out_ref.at[pl.ds(_dev_index(p) * m_per, m_per), :]

        def copy(k, block, to, src=None):
            return pltpu.make_async_remote_copy(
                src_ref=rows(block) if src is None else src, dst_ref=rows(block),
                send_sem=send_sems.at[k], recv_sem=recv_sems.at[k], device_id=to, device_id_type=MESH)

        mine = pltpu.make_async_copy(x_ref, rows(me), local_sem)
        mine.start()
        first = [copy(0, me, sibling, src=x_ref)]
        first += [copy(1 + j, me, (*chip, c), src=x_ref) for j, chip in enumerate(chips)]
        for cp in first:
            cp.start()
        passed = [copy(4 + j, (*chip, c), sibling) for j, chip in enumerate(chips)]
        for j, chip in enumerate(chips):
            copy(1 + j, (*chip, c), me).wait_recv()
            passed[j].start()
        copy(0, sibling, me).wait_recv()
        for j, chip in enumerate(chips):
            copy(4 + j, (*chip, 1 - c), me).wait_recv()
        for cp in first + passed:
            cp.wait_send()
        mine.wait()

    return pl.pallas_call(
        body,
        name=name,
        out_shape=jax.ShapeDtypeStruct((N_DEV * m_per, n), shard.dtype),
        in_specs=[pl.BlockSpec(memory_space=pltpu.VMEM)],
        out_specs=pl.BlockSpec(memory_space=pltpu.VMEM),
        scratch_shapes=[pltpu.SemaphoreType.DMA((7,)), pltpu.SemaphoreType.DMA((7,)), pltpu.SemaphoreType.DMA],
        compiler_params=_cparams(),
    )(shard)


def _allgather_hbm(shards, name):
    n = len(shards)

    def body(*refs):
        ins, outs = refs[:n], refs[n : 2 * n]
        send_sems, recv_sems, local_sems = refs[2 * n :]
        x, y, c = _place()
        me, sibling = (x, y, c), (x, y, 1 - c)
        chips = [(1 - x, y), (x, 1 - y), (1 - x, 1 - y)]

        def copy(a, k, block, to, src=None):
            dst = outs[a].at[_dev_index(block)]
            return pltpu.make_async_remote_copy(
                src_ref=dst if src is None else src, dst_ref=dst,
                send_sem=send_sems.at[a * 7 + k], recv_sem=recv_sems.at[a * 7 + k], device_id=to, device_id_type=MESH)

        mine = [pltpu.make_async_copy(ins[a], outs[a].at[_dev_index(me)], local_sems.at[a]) for a in range(n)]
        for cp in mine:
            cp.start()
        first = []
        for a in range(n):
            first.append(copy(a, 0, me, sibling, src=ins[a]))
            first += [copy(a, 1 + j, me, (*chip, c), src=ins[a]) for j, chip in enumerate(chips)]
        for cp in first:
            cp.start()
        passed = []
        for j, chip in enumerate(chips):
            for a in range(n):
                copy(a, 1 + j, (*chip, c), me).wait_recv()
                cp = copy(a, 4 + j, (*chip, c), sibling)
                cp.start()
                passed.append(cp)
        for a in range(n):
            copy(a, 0, sibling, me).wait_recv()
        for j, chip in enumerate(chips):
            for a in range(n):
                copy(a, 4 + j, (*chip, 1 - c), me).wait_recv()
        for cp in first + passed:
            cp.wait_send()
        for cp in mine:
            cp.wait()

    any_spec = pl.BlockSpec(memory_space=pl.ANY)
    return pl.pallas_call(
        body,
        name=name,
        out_shape=[jax.ShapeDtypeStruct((N_DEV, *s.shape), s.dtype) for s in shards],
        in_specs=[any_spec] * n,
        out_specs=[any_spec] * n,
        scratch_shapes=[pltpu.SemaphoreType.DMA((7 * n,)), pltpu.SemaphoreType.DMA((7 * n,)),
                        pltpu.SemaphoreType.DMA((n,))],
        compiler_params=_cparams(),
    )(*shards)


def _exchange_hbm(bufs, name):
    n = len(bufs)

    def body(*refs):
        ins, outs = refs[:n], refs[n : 2 * n]
        send_sems, recv_sems, local_sems = refs[2 * n :]
        x, y, c = _place()
        me = _dev_index((x, y, c))
        mine = [pltpu.make_async_copy(ins[a].at[me], outs[a].at[me], local_sems.at[a]) for a in range(n)]
        for cp in mine:
            cp.start()
        def peer_of(k):
            return (1 - x if k & 4 else x, 1 - y if k & 2 else y, 1 - c if k & 1 else c)

        copies = []
        for k in range(1, N_DEV):
            peer = peer_of(k)
            for a in range(n):
                copies.append(pltpu.make_async_remote_copy(
                    src_ref=ins[a].at[_dev_index(peer)], dst_ref=outs[a].at[me],
                    send_sem=send_sems.at[a * 7 + k - 1], recv_sem=recv_sems.at[a * 7 + k - 1],
                    device_id=peer, device_id_type=MESH))
        for cp in copies:
            cp.start()
        for k in range(1, N_DEV):
            peer = peer_of(k)
            for a in range(n):
                pltpu.make_async_remote_copy(
                    src_ref=ins[a].at[me], dst_ref=outs[a].at[_dev_index(peer)],
                    send_sem=send_sems.at[a * 7 + k - 1], recv_sem=recv_sems.at[a * 7 + k - 1],
                    device_id=peer, device_id_type=MESH).wait_recv()
        for cp in copies:
            cp.wait_send()
        for cp in mine:
            cp.wait()

    any_spec = pl.BlockSpec(memory_space=pl.ANY)
    return pl.pallas_call(
        body,
        name=name,
        out_shape=[jax.ShapeDtypeStruct(b.shape, b.dtype) for b in bufs],
        in_specs=[any_spec] * n,
        out_specs=[any_spec] * n,
        scratch_shapes=[pltpu.SemaphoreType.DMA((7 * n,)), pltpu.SemaphoreType.DMA((7 * n,)),
                        pltpu.SemaphoreType.DMA((n,))],
        compiler_params=_cparams(),
    )(*bufs)


HBM_SPEC = pl.BlockSpec(memory_space=pltpu.HBM)
SEM_SPEC = pl.BlockSpec(memory_space=pltpu.SEMAPHORE)
EFFECT = pltpu.SideEffectType.DATAFLOW_SIDE_EFFECTING


def _peers(x, y, c):
    return [(1 - x if k & 4 else x, 1 - y if k & 2 else y, 1 - c if k & 1 else c) for k in range(1, N_DEV)]


def _push_start(groups, sliced, name):
    flat = [b for g in groups for b in g]
    n, ng = len(flat), len(groups)
    sizes = [len(g) for g in groups]
    lands = [lax.empty(b.shape if sliced else (N_DEV, *b.shape), b.dtype) for b in flat]

    def body(*refs):
        ins, lnd = refs[:n], refs[n : 2 * n]
        sems = refs[2 * n : 2 * n + 2 * ng]
        token = refs[-1]
        x, y, c = _place()
        me = _dev_index((x, y, c))
        first = 0
        for gi, size in enumerate(sizes):
            for k, peer in enumerate(_peers(x, y, c)):
                for j in range(first, first + size):
                    sem = (j - first) * 7 + k
                    pltpu.make_async_remote_copy(
                        src_ref=ins[j].at[_dev_index(peer)] if sliced else ins[j], dst_ref=lnd[j].at[me],
                        send_sem=sems[2 * gi].at[sem], recv_sem=sems[2 * gi + 1].at[sem],
                        device_id=peer, device_id_type=MESH).start()
            first += size
        token[...] = jnp.zeros_like(token)

    out_shape = []
    for size in sizes:
        out_shape += [pltpu.SemaphoreType.DMA((7 * size,)), pltpu.SemaphoreType.DMA((7 * size,))]
    out_shape += [pltpu.HBM(b.shape, b.dtype) for b in flat + lands]
    out_shape.append(jax.ShapeDtypeStruct((V7X_SUBLANES, V7X_LANES), F32))
    res = pl.pallas_call(
        body,
        name=name,
        out_shape=tuple(out_shape),
        in_specs=[HBM_SPEC] * (2 * n),
        out_specs=tuple([SEM_SPEC] * (2 * ng) + [HBM_SPEC] * (2 * n) + [pl.BlockSpec(memory_space=pltpu.VMEM)]),
        input_output_aliases={i: 2 * ng + i for i in range(2 * n)},
        compiler_params=pltpu.CompilerParams(has_side_effects=EFFECT),
    )(*[pltpu.with_memory_space_constraint(b, pltpu.HBM) for b in flat + lands])
    sems, thru, token = res[: 2 * ng], res[2 * ng : 2 * ng + 2 * n], res[-1]
    out, first = [], 0
    for gi, size in enumerate(sizes):
        out.append((sems[2 * gi], sems[2 * gi + 1], list(thru[first : first + size]),
                    list(thru[n + first : n + first + size])))
        first += size
    return out, token


def _push_wait(started, sliced, after, name):
    send_sems, recv_sems, bufs, lands = started
    n = len(bufs)

    def body(*refs):
        ins, lnd = refs[:n], refs[n : 2 * n]
        send_ref, recv_ref = refs[2 * n], refs[2 * n + 1]
        x, y, c = _place()
        for k, peer in enumerate(_peers(x, y, c)):
            for j in range(n):
                cp = pltpu.make_async_remote_copy(
                    src_ref=ins[j].at[_dev_index(peer)] if sliced else ins[j], dst_ref=lnd[j].at[_dev_index(peer)],
                    send_sem=send_ref.at[j * 7 + k], recv_sem=recv_ref.at[j * 7 + k],
                    device_id=peer, device_id_type=MESH)
                cp.wait_send()
                cp.wait_recv()

    res = pl.pallas_call(
        body,
        name=name,
        out_shape=tuple(pltpu.HBM(b.shape, b.dtype) for b in bufs + lands),
        in_specs=[HBM_SPEC] * (2 * n) + [SEM_SPEC, SEM_SPEC, pl.BlockSpec(memory_space=pl.ANY)],
        out_specs=tuple([HBM_SPEC] * (2 * n)),
        input_output_aliases={i: i for i in range(2 * n)},
        compiler_params=pltpu.CompilerParams(has_side_effects=EFFECT),
    )(*bufs, *lands, send_sems, recv_sems, after)
    return list(res[n:])


def _with_own_row(land, own, me):
    return lax.dynamic_update_index_in_dim(land, own, me, 0)


def _cols_full(g):
    return jnp.transpose(g, (1, 0, 2)).reshape(g.shape[1], -1)


def _rows_full(g):
    return g.reshape(-1, g.shape[2])


def _cols_parts(full):
    r = full.shape[0]
    return jnp.transpose(full.reshape(r, N_DEV, -1), (1, 0, 2)).astype(BF16)


def _rows_parts(full):
    return full.reshape(N_DEV, -1, full.shape[1]).astype(BF16)


def _block_diag(w):
    eye = jnp.eye(LRU_BLOCKS, dtype=w.dtype)
    return jnp.einsum("nkj,nm->nkmj", w, eye).reshape(LRU_WIDTH, LRU_WIDTH)


def _diag_blocks(full):
    r = full.reshape(LRU_BLOCKS, LRU_BLOCK, LRU_BLOCKS, LRU_BLOCK)
    idx = np.arange(LRU_BLOCKS)
    return r[idx, :, idx, :]


def _pad_rows(v, rows):
    flat = v.reshape(-1)
    return jnp.pad(flat, (0, rows * D_MODEL - flat.shape[0])).reshape(rows, D_MODEL)


def _my_cols(full, me, width):
    return lax.dynamic_slice_in_dim(full, me * width, width, axis=full.ndim - 1)


def kernel(x, c, w_ada, b_ada, norm_pre, norm_post, ffn1_w_gu, ffn1_w_down, w_in, rel_bias, conv_w, conv_b, lru_wa, lru_ba, lru_wx, lru_bx, lru_lambda, w_att_o, w_rec_o, w_out, ffn2_w_gu, ffn2_w_down, loss_target, m_w_ada, m_b_ada, m_norm_pre, m_norm_post, m_ffn1_w_gu, m_ffn1_w_down, m_w_in, m_rel_bias, m_conv_w, m_conv_b, m_lru_wa, m_lru_ba, m_lru_wx, m_lru_bx, m_lru_lambda, m_w_att_o, m_w_rec_o, m_w_out, m_ffn2_w_gu, m_ffn2_w_down, v_w_ada, v_b_ada, v_norm_pre, v_norm_post, v_ffn1_w_gu, v_ffn1_w_down, v_w_in, v_rel_bias, v_conv_w, v_conv_b, v_lru_wa, v_lru_ba, v_lru_wx, v_lru_bx, v_lru_lambda, v_w_att_o, v_w_rec_o, v_w_out, v_ffn2_w_gu, v_ffn2_w_down):
    weights = dict(w_ada=w_ada, b_ada=b_ada, norm_pre=norm_pre, norm_post=norm_post, ffn1_w_gu=ffn1_w_gu,
                   ffn1_w_down=ffn1_w_down, w_in=w_in, rel_bias=rel_bias, conv_w=conv_w, conv_b=conv_b,
                   lru_wa=lru_wa, lru_ba=lru_ba, lru_wx=lru_wx, lru_bx=lru_bx, lru_lambda=lru_lambda,
                   w_att_o=w_att_o, w_rec_o=w_rec_o, w_out=w_out, ffn2_w_gu=ffn2_w_gu, ffn2_w_down=ffn2_w_down)
    mom1 = dict(w_ada=m_w_ada, b_ada=m_b_ada, norm_pre=m_norm_pre, norm_post=m_norm_post, ffn1_w_gu=m_ffn1_w_gu,
                ffn1_w_down=m_ffn1_w_down, w_in=m_w_in, rel_bias=m_rel_bias, conv_w=m_conv_w, conv_b=m_conv_b,
                lru_wa=m_lru_wa, lru_ba=m_lru_ba, lru_wx=m_lru_wx, lru_bx=m_lru_bx, lru_lambda=m_lru_lambda,
                w_att_o=m_w_att_o, w_rec_o=m_w_rec_o, w_out=m_w_out, ffn2_w_gu=m_ffn2_w_gu, ffn2_w_down=m_ffn2_w_down)
    mom2 = dict(w_ada=v_w_ada, b_ada=v_b_ada, norm_pre=v_norm_pre, norm_post=v_norm_post, ffn1_w_gu=v_ffn1_w_gu,
                ffn1_w_down=v_ffn1_w_down, w_in=v_w_in, rel_bias=v_rel_bias, conv_w=v_conv_w, conv_b=v_conv_b,
                lru_wa=v_lru_wa, lru_ba=v_lru_ba, lru_wx=v_lru_wx, lru_bx=v_lru_bx, lru_lambda=v_lru_lambda,
                w_att_o=v_w_att_o, w_rec_o=v_w_rec_o, w_out=v_w_out, ffn2_w_gu=v_ffn2_w_gu, ffn2_w_down=v_ffn2_w_down)
    order = list(weights)
    big = ["ffn1_w_gu", "ffn1_w_down", "w_in", "w_att_o", "w_rec_o", "w_out", "ffn2_w_gu", "ffn2_w_down"]
    col_sharded = {"ffn1_w_gu", "w_in", "w_att_o", "ffn2_w_gu"}
    small = ["b_ada", "norm_pre", "norm_post", "rel_bias", "conv_w", "conv_b", "lru_wa", "lru_ba", "lru_wx",
             "lru_bx", "lru_lambda"]

    xi, yi, ci = _place()
    me = _dev_index((xi, yi, ci))
    x0 = x[0]
    target = loss_target[0]

    weight_groups = [["ffn1_w_gu", "ffn1_w_down"], ["w_in", "w_att_o", "w_rec_o", "w_out"],
                     ["ffn2_w_gu", "ffn2_w_down"]]
    shards = {n: weights[n][0].astype(BF16) for n in big}
    weights_started, started = _push_start([[shards[n] for n in g] for g in weight_groups], False,
                                           "gather_weights_start")

    def gathered_group(gi, after):
        lands = _push_wait(weights_started[gi], False, after, f"gather_weights_wait{gi}")
        out = {}
        for n, land in zip(weight_groups[gi], lands):
            g = _with_own_row(land, shards[n], me)
            out[n] = _cols_full(g) if n in col_sharded else _rows_full(g)
        return out

    pack = jnp.concatenate([c.reshape(-1), norm_pre.reshape(-1), norm_post.reshape(-1), conv_w.reshape(-1)])
    pack = jnp.pad(pack, (0, 3072 - pack.shape[0])).reshape(8, 384) + started[0, 0]
    got = _allgather_vmem(pack, "gather_small_inputs").reshape(N_DEV, 3072)
    c_all = got[:, :1024]
    unshard = lambda blk, rows: jnp.transpose(blk.reshape(N_DEV, rows, 128), (1, 0, 2)).reshape(rows, D_MODEL)
    g_pre = unshard(got[:, 1024:1408], 3)
    g_post = unshard(got[:, 1408:1792], 3)
    conv_taps = unshard(got[:, 1792:2304], 4)
    conv_w8 = jnp.concatenate([conv_taps, jnp.zeros((4, LRU_WIDTH), F32)], axis=0)

    mod_cols = _ada_fwd(c_all, w_ada[0], "ada_fwd")
    mod_all = _allgather_vmem(mod_cols, "gather_mod").reshape(N_DEV, N_DEV, 1152)
    mod = lax.dynamic_index_in_dim(mod_all, me, axis=1, keepdims=False).reshape(1, -1) + b_ada
    mod = mod.reshape(3, 3, 1, D_MODEL)

    w_bd = jnp.concatenate([_block_diag(lru_wa[0]), _block_diag(lru_wx[0])], axis=1).astype(BF16)
    bias = _bias_tile(rel_bias[0])

    res_w = (0.5, 1.0, 0.5)
    row = lambda v: v.reshape(1, -1)

    def ffn_fwd(xin, k, w_gu, w_down, tag):
        h = _pre_fwd(xin, row(g_pre[k]), mod[k, 0], mod[k, 1], f"{tag}_pre")
        gu = _matmul(h, w_gu, "nn", F32, f"{tag}_gu")
        a = _swiglu_fwd(gu, f"{tag}_act")
        f = _matmul(a, w_down, "nn", F32, f"{tag}_down")
        xout = _post_fwd(f, xin, row(g_post[k]), mod[k, 2], res_w[k], f"{tag}_post")
        return xout, (h, gu, a, f)

    full = gathered_group(0, mod)
    x1, saved1 = ffn_fwd(x0, 0, full["ffn1_w_gu"], full["ffn1_w_down"], "ffn1")

    full.update(gathered_group(1, x1))
    w_in_p = jnp.concatenate([full["w_in"][:, 3 * ATT_WIDTH :], full["w_in"][:, : 3 * ATT_WIDTH]], axis=1)
    h2 = _pre_fwd(x1, row(g_pre[1]), mod[1, 0], mod[1, 1], "mix_pre")
    proj = _matmul(h2, w_in_p, "nn", F32, "mix_in")
    att_o = _attn_fwd(proj, bias, "attn_fwd")
    xc, xcb = _conv_fwd(proj, conv_w8, conv_b, "conv_fwd")
    pre = _matmul(xcb, w_bd, "nn", F32, "lru_gate_proj")
    a_t, u_t = _gates_fwd(pre, xc, lru_ba, lru_bx, lru_lambda, "lru_gates")
    hs, h_prev = _scan_fwd(a_t, u_t, "lru_scan")
    (rec_in,) = _rowwise(_recin_fn, "rec_in", [], [hs, (proj, LRU_WIDTH, 1)], [(LRU_WIDTH, BF16)])
    att = _matmul(att_o, full["w_att_o"], "nn", F32, "att_out")
    rec = _matmul(rec_in, full["w_rec_o"], "nn", F32, "rec_out")
    (merged,) = _rowwise(_merge_fn, "merge", [], [att, rec, (proj, LRU_WIDTH, 2), (proj, LRU_WIDTH, 3)],
                         [(D_MODEL, BF16)])
    f2 = _matmul(merged, full["w_out"], "nn", F32, "mix_out")
    x2 = _post_fwd(f2, x1, row(g_post[1]), mod[1, 2], res_w[1], "mix_post")

    full.update(gathered_group(2, x2))
    x3, saved3 = ffn_fwd(x2, 2, full["ffn2_w_gu"], full["ffn2_w_down"], "ffn2")

    dy, sq = _loss_stage(x3, target, "loss")
    loss = lax.psum(0.5 * jnp.sum(sq) / D_MODEL, ("x", "y", "c"))

    grads = {}
    dmod = [[None] * 3 for _ in range(3)]
    d_pre, d_post = [None] * 3, [None] * 3

    def exchange_start(names, tag):
        send = [(_cols_parts if n in col_sharded else _rows_parts)(grads[n]) for n in names]
        (group,), token = _push_start([send], True, f"exchange_{tag}_start")
        return (names, send, group), token[0, 0]

    def exchange_finish(pending, after, tag):
        names, send, group = pending
        lands = _push_wait(group, True, after, f"exchange_{tag}_wait")
        res = None
        for n, land, mine in zip(names, lands, send):
            own = lax.dynamic_index_in_dim(mine, me, axis=0, keepdims=False)
            res = _adamw(_with_own_row(land, own, me), weights[n][0], mom1[n][0], mom2[n][0], f"adamw_{n}")
            out_g[n], out_d[n], out_m[n], out_v[n] = [r.reshape(weights[n].shape) for r in res]
        return res[0]

    out_g, out_d, out_m, out_v = {}, {}, {}, {}

    def ffn_bwd(xin, k, w_gu, w_down, saved, dout, gate, tag):
        h, gu, a, f = saved
        df, d_post[k], dmod[k][2] = _post_bwd(f, row(g_post[k]), gate, res_w[k], dout, f"{tag}_post_bwd")
        da = _matmul(df, w_down, "nt", F32, f"{tag}_da")
        dw_down = _matmul(a, df, "tn", F32, f"{tag}_dw_down", tm=1408, tn=1024, tk=1024)
        dgu = _swiglu_bwd(gu, da, f"{tag}_act_bwd")
        dh = _matmul(dgu, w_gu, "nt", F32, f"{tag}_dh")
        dw_gu = _matmul(h, dgu, "tn", F32, f"{tag}_dw_gu", tm=1024, tn=1408, tk=1024)
        dx, d_pre[k], dmod[k][0], dmod[k][1] = _pre_bwd(xin, row(g_pre[k]), mod[k, 0], mod[k, 1], dh, dout,
                                                         f"{tag}_pre_bwd")
        return dx, dw_gu, dw_down

    dx2, grads["ffn2_w_gu"], grads["ffn2_w_down"] = ffn_bwd(x2, 2, full["ffn2_w_gu"], full["ffn2_w_down"], saved3,
                                                            dy, mod[2, 2], "ffn2")
    pending_ffn2, started = exchange_start(weight_groups[2], "ffn2")

    df2, d_post[1], dmod[1][2] = _post_bwd(f2, row(g_post[1]), mod[1, 2] + started, res_w[1], dx2, "mix_post_bwd")
    dmerged = _matmul(df2, full["w_out"], "nt", F32, "mix_dmerged")
    grads["w_out"] = _matmul(merged, df2, "tn", F32, "mix_dw_out", tm=1024, tn=1024, tk=1024)

    def merge_bwd(att, rec, g_att, g_rec, dm):
        _, vjp = jax.vjp(_merge_fn, att, rec, g_att, g_rec)
        return vjp(dm)

    datt, drec, dg_att, dg_rec = _rowwise(
        merge_bwd, "merge_bwd", [], [att, rec, (proj, LRU_WIDTH, 2), (proj, LRU_WIDTH, 3), dmerged],
        [(D_MODEL, BF16)] * 4)
    datt_o = _matmul(datt, full["w_att_o"], "nt", BF16, "att_out_bwd")
    grads["w_att_o"] = _matmul(att_o, datt, "tn", F32, "dw_att_o", tm=512, tn=1024, tk=1024)
    drec_in = _matmul(drec, full["w_rec_o"], "nt", F32, "rec_out_bwd")
    grads["w_rec_o"] = _matmul(rec_in, drec, "tn", F32, "dw_rec_o", tm=1024, tn=1024, tk=1024)

    def recin_bwd(hs, yr, d):
        _, vjp = jax.vjp(_recin_fn, hs, yr)
        return vjp(d)

    dhs, dyr = _rowwise(recin_bwd, "rec_in_bwd", [], [hs, (proj, LRU_WIDTH, 1), drec_in],
                        [(LRU_WIDTH, F32), (LRU_WIDTH, BF16)])
    g_t = _scan_bwd(a_t, dhs, "lru_scan_bwd")
    dpre, dxc_direct, d_ba, d_bx, d_lam = _gates_bwd(pre, xc, lru_ba, lru_bx, lru_lambda, g_t, h_prev,
                                                     "lru_gates_bwd")
    dxc_mm = _matmul(dpre, w_bd, "nt", F32, "lru_gate_proj_bwd")
    dw_bd = _matmul(xcb, dpre, "tn", F32, "dw_lru_gate", tm=1024, tn=1024, tk=1024)
    dxr, d_conv_w8, d_conv_b = _conv_bwd(proj, conv_w8, dxc_direct, dxc_mm, "conv_bwd")
    dq, dk, dv, dbias = _attn_bwd(proj, bias, datt_o, "attn_bwd")
    dproj = jnp.concatenate([dxr, dyr, dg_att, dg_rec, dq, dk, dv], axis=1)
    dh2 = _matmul(dproj, w_in_p, "nt", F32, "mix_dh")
    dw_in_p = _matmul(h2, dproj, "tn", F32, "mix_dw_in", tm=1024, tn=1408, tk=1024)
    grads["w_in"] = jnp.concatenate([dw_in_p[:, OFF_Q:], dw_in_p[:, :OFF_Q]], axis=1)
    dx1, d_pre[1], dmod[1][0], dmod[1][1] = _pre_bwd(x1, row(g_pre[1]), mod[1, 0], mod[1, 1], dh2, dx2, "mix_pre_bwd")
    pending_mix, started = exchange_start(weight_groups[1], "mix")

    dx0, grads["ffn1_w_gu"], grads["ffn1_w_down"] = ffn_bwd(x0, 0, full["ffn1_w_gu"], full["ffn1_w_down"], saved1,
                                                            dx1, mod[0, 2] + started, "ffn1")
    pending_ffn1, started = exchange_start(weight_groups[0], "ffn1")

    dmod_mine = jnp.concatenate([dmod[k][j] for k in range(3) for j in range(3)], axis=0)
    pieces = [
        dmod_mine,
        jnp.concatenate(d_pre, axis=0),
        jnp.concatenate(d_post, axis=0),
        d_conv_w8[:4],
        d_conv_b, d_ba, d_bx, d_lam,
        _pad_rows(_bias_grad(dbias), 3),
        _diag_blocks(dw_bd[:, :LRU_WIDTH]).reshape(64, D_MODEL),
        _diag_blocks(dw_bd[:, LRU_WIDTH:]).reshape(64, D_MODEL),
    ]
    small_rows = 160
    packed = jnp.concatenate(pieces, axis=0)
    packed = jnp.pad(packed, ((0, small_rows - packed.shape[0]), (0, 0))) + started
    parts = _allgather_vmem(packed, "gather_small_grads").reshape(N_DEV, small_rows, D_MODEL)
    total = _sum_parts(parts, "sum_small_grads")
    grads["b_ada"] = total[0:9].reshape(1, -1)
    grads["norm_pre"] = _my_cols(total[9:12], me, 128)
    grads["norm_post"] = _my_cols(total[12:15], me, 128)
    grads["conv_w"] = _my_cols(total[15:19], me, 128)
    grads["conv_b"] = total[19:20]
    grads["lru_ba"] = total[20:21]
    grads["lru_bx"] = total[21:22]
    grads["lru_lambda"] = total[22:23]
    grads["rel_bias"] = total[23:26].reshape(-1)[: ATT_HEADS * (2 * MAX_REL + 1)].reshape(ATT_HEADS, -1)
    grads["lru_wa"] = total[26:90].reshape(LRU_BLOCKS, LRU_BLOCK, LRU_BLOCK)
    grads["lru_wx"] = total[90:154].reshape(LRU_BLOCKS, LRU_BLOCK, LRU_BLOCK)
    dmod_all = parts[:, 0:9, :].reshape(N_DEV, 9 * D_MODEL)
    grads["w_ada"] = _ada_bwd(c_all, _my_cols(dmod_all, me, 1152), "ada_bwd")

    res = _adamw(grads["w_ada"], w_ada[0], m_w_ada[0], v_w_ada[0], "adamw_w_ada")
    out_g["w_ada"], out_d["w_ada"], out_m["w_ada"], out_v["w_ada"] = [r.reshape(w_ada.shape) for r in res]

    sizes = [int(np.prod(weights[n].shape)) for n in small]
    tot = sum(sizes)
    rows_small = -(-tot // (16 * D_MODEL)) * 16
    flat = lambda arrs: jnp.pad(jnp.concatenate([a.reshape(-1) for a in arrs]),
                                (0, rows_small * D_MODEL - tot)).reshape(rows_small, D_MODEL)
    res = _adamw(flat([grads[n] for n in small]), flat([weights[n] for n in small]),
                 flat([mom1[n] for n in small]), flat([mom2[n] for n in small]), "adamw_small", rows=rows_small)
    offs = np.cumsum([0] + sizes)
    for dst, r in zip((out_g, out_d, out_m, out_v), res):
        rf = r.reshape(-1)
        for i, n in enumerate(small):
            dst[n] = rf[offs[i] : offs[i + 1]].reshape(weights[n].shape)

    done = exchange_finish(pending_ffn2, res[0], "ffn2")
    done = exchange_finish(pending_mix, done, "mix")
    exchange_finish(pending_ffn1, done, "ffn1")

    return (loss, dx0[None], *[out_g[n] for n in order], *[out_d[n] for n in order],
            *[out_m[n] for n in order], *[out_v[n] for n in order])
```

```python
import functools

import jax
import jax.numpy as jnp
import numpy as np
from jax import lax
from jax.experimental import pallas as pl
from jax.experimental.pallas import tpu as pltpu

D_MODEL = 1024
D_FF = 2816
ATT_HEADS = 8
ATT_HEAD_DIM = 64
ATT_WIDTH = 512
CHUNK = 64
LEFT_CHUNKS = 8
MAX_REL = 128
LRU_WIDTH = 1024
LRU_BLOCKS = 16
LRU_BLOCK = 64
LRU_C = 8.0
EPS = 1e-6
PROJ_WIDTH = 5632
N_DEV = 8

ADAM_LR = 0.001
ADAM_B1 = 0.9
ADAM_B2 = 0.999
ADAM_EPS = 1e-08
ADAM_WD = 0.01
ADAM_STEP = 10

V7X_LANES = 128
V7X_SUBLANES = 8
V7X_VMEM_BYTES = 64 * 1024 * 1024
VMEM_LIMIT = V7X_VMEM_BYTES - 8 * 1024 * 1024

ATT_TQ = 256
NEG = -1e30
BF16 = jnp.bfloat16
F32 = jnp.float32
MESH = pl.DeviceIdType.MESH

OFF_Q = 4 * LRU_WIDTH
OFF_K = OFF_Q + ATT_WIDTH
OFF_V = OFF_K + ATT_WIDTH


def _cparams(**kw):
    return pltpu.CompilerParams(vmem_limit_bytes=VMEM_LIMIT, **kw)


def _pick(n, target, unit=V7X_LANES):
    best = None
    for t in range(unit, min(n, target) + 1, unit):
        if n % t == 0:
            best = t
    return n if best is None else best


_DIMS = {
    "nn": (((1,), (0,)), ((), ())),
    "nt": (((1,), (1,)), ((), ())),
    "tn": (((0,), (0,)), ((), ())),
}


ANY_SPEC = pl.BlockSpec(memory_space=pl.ANY)


def _matmul(a, b, mode, out_dtype, name, tm=1024, tn=512, tk=1408, deps=()):
    n_deps = len(deps)
    if mode == "nn":
        (m, k), (k2, n) = a.shape, b.shape
    elif mode == "nt":
        (m, k), (n, k2) = a.shape, b.shape
    else:
        (k, m), (k2, n) = a.shape, b.shape
    assert k == k2, (a.shape, b.shape, mode)
    tm, tn, tk = _pick(m, tm), _pick(n, tn), _pick(k, tk)
    nk = k // tk
    dims = _DIMS[mode]

    def body(a_ref, b_ref, *rest):
        o_ref, scratch = rest[n_deps], rest[n_deps + 1 :]
        p = lax.dot_general(a_ref[...], b_ref[...], dims, preferred_element_type=F32)
        if nk == 1:
            o_ref[...] = p.astype(o_ref.dtype)
        else:
            acc = scratch[0]
            kk = pl.program_id(2)

            @pl.when(kk == 0)
            def _():
                acc[...] = p

            @pl.when(kk > 0)
            def _():
                acc[...] += p

            @pl.when(kk == nk - 1)
            def _():
                o_ref[...] = acc[...].astype(o_ref.dtype)

    if mode == "nn":
        a_spec = pl.BlockSpec((tm, tk), lambda i, j, kk: (i, kk))
        b_spec = pl.BlockSpec((tk, tn), lambda i, j, kk: (kk, j))
    elif mode == "nt":
        a_spec = pl.BlockSpec((tm, tk), lambda i, j, kk: (i, kk))
        b_spec = pl.BlockSpec((tn, tk), lambda i, j, kk: (j, kk))
    else:
        a_spec = pl.BlockSpec((tk, tm), lambda i, j, kk: (kk, i))
        b_spec = pl.BlockSpec((tk, tn), lambda i, j, kk: (kk, j))
    return pl.pallas_call(
        body,
        name=name,
        grid=(m // tm, n // tn, nk),
        in_specs=[a_spec, b_spec] + [ANY_SPEC] * n_deps,
        out_specs=pl.BlockSpec((tm, tn), lambda i, j, kk: (i, j)),
        out_shape=jax.ShapeDtypeStruct((m, n), out_dtype),
        scratch_shapes=[pltpu.VMEM((tm, tn), F32)] if nk > 1 else [],
        compiler_params=_cparams(dimension_semantics=("parallel", "parallel", "arbitrary")),
    )(a, b, *deps)


def _rowwise(fn, name, params, tiles, outs, accs=(), ts=256, with_index=False, deps=()):
    norm = []
    for t in tiles:
        if not isinstance(t, tuple):
            t = (t, t.shape[1], 0)
        norm.append(t if len(t) == 4 else (*t, None))
    s = norm[0][0].shape[0]
    ts = min(ts, s)
    assert s % ts == 0 and ts % V7X_SUBLANES == 0
    steps = s // ts
    halo_blocks = ts // V7X_SUBLANES
    n_p, n_t, n_o = len(params), len(norm), len(outs)

    def body(*refs):
        i = pl.program_id(0)
        vals = [r[...] for r in refs[: n_p + n_t]]
        res = fn(i, steps, *vals) if with_index else fn(*vals)
        if not isinstance(res, (tuple, list)):
            res = (res,)
        first_out = n_p + n_t + len(deps)
        o_refs = refs[first_out : first_out + n_o]
        a_refs = refs[first_out + n_o :]
        for r, v in zip(o_refs, res[:n_o]):
            r[...] = v.astype(r.dtype)
        for r, v in zip(a_refs, res[n_o:]):
            _accumulate(r, v, i)

    in_specs = [pl.BlockSpec(p.shape, lambda i: (0, 0)) for p in params]
    for arr, w, cb, halo in norm:
        if halo is None:
            in_specs.append(pl.BlockSpec((ts, w), lambda i, cb=cb: (i, cb)))
        elif halo == "prev":
            in_specs.append(
                pl.BlockSpec((V7X_SUBLANES, w), lambda i, cb=cb: (jnp.maximum(i * halo_blocks - 1, 0), cb))
            )
        else:
            last = s // V7X_SUBLANES - 1
            in_specs.append(
                pl.BlockSpec((V7X_SUBLANES, w), lambda i, cb=cb: (jnp.minimum((i + 1) * halo_blocks, last), cb))
            )
    in_specs += [ANY_SPEC] * len(deps)
    out_specs = [pl.BlockSpec((ts, w), lambda i: (i, 0)) for w, _ in outs]
    out_specs += [pl.BlockSpec(shape, lambda i: (0, 0)) for shape in accs]
    out_shape = [jax.ShapeDtypeStruct((s, w), dt) for w, dt in outs]
    out_shape += [jax.ShapeDtypeStruct(shape, F32) for shape in accs]
    res = pl.pallas_call(
        body,
        name=name,
        grid=(steps,),
        in_specs=in_specs,
        out_specs=out_specs,
        out_shape=out_shape,
        compiler_params=_cparams(dimension_semantics=("arbitrary",)),
    )(*params, *[t[0] for t in norm], *deps)
    return res


def _accumulate(ref, val, step):
    @pl.when(step == 0)
    def _():
        ref[...] = val

    @pl.when(step > 0)
    def _():
        ref[...] += val


def _sigmoid(z):
    return jax.nn.sigmoid(z)


def _silu(z):
    return z * _sigmoid(z)


def _gelu(z):
    return 0.5 * z * (1.0 + jnp.tanh(0.7978845608028654 * (z + 0.044715 * (z * z * z))))


def _pre_fn(g, shift, scale, x):
    r = lax.rsqrt(jnp.mean(x * x, axis=-1, keepdims=True) + EPS)
    return ((x * r) * g) * (1.0 + scale) + shift


def _post_fn(res_w, g, gate, f, x):
    r = lax.rsqrt(jnp.mean(f * f, axis=-1, keepdims=True) + EPS)
    return x + (res_w * gate) * ((f * r) * g)


def _swiglu_fn(gu):
    return _silu(gu[:, :D_FF]) * gu[:, D_FF:]


def _gates_fn(ba, bx, lam, pre, xc):
    ra = _sigmoid(pre[:, :LRU_WIDTH] + ba)
    ia = _sigmoid(pre[:, LRU_WIDTH:] + bx)
    softplus = jnp.maximum(-lam, 0.0) + jnp.log1p(jnp.exp(-jnp.abs(lam)))
    log_a = (-LRU_C) * ra * softplus
    a = jnp.exp(log_a)
    mult = jnp.sqrt(-jnp.tanh(log_a) * (a * a + 1.0))
    return a, mult * (ia * xc)


def _recin_fn(hs, yr):
    return hs * _gelu(yr)


def _merge_fn(att, rec, g_att, g_rec):
    return _sigmoid(g_att) * att + _sigmoid(g_rec) * rec


def _rowsum(v):
    return jnp.sum(v, axis=0, keepdims=True)


def _pre_fwd(x, g, shift, scale, name, deps=()):
    (h,) = _rowwise(_pre_fn, name, [g, shift, scale], [x], [(D_MODEL, BF16)], deps=deps)
    return h


def _pre_bwd(x, g, shift, scale, dh, dres, name):
    def fn(g, shift, scale, x, dh, dres):
        _, vjp = jax.vjp(_pre_fn, g, shift, scale, x)
        dg, dshift, dscale, dx = vjp(dh)
        return dx + dres, dg, dshift, dscale

    row = (1, D_MODEL)
    return _rowwise(fn, name, [g, shift, scale], [x, dh, dres], [(D_MODEL, F32)], [row, row, row])


def _post_fwd(f, x, g, gate, res_w, name):
    (y,) = _rowwise(functools.partial(_post_fn, res_w), name, [g, gate], [f, x], [(D_MODEL, F32)])
    return y


def _post_bwd(f, g, gate, res_w, dy, name, deps=()):
    def fn(g, gate, f, dy):
        _, vjp = jax.vjp(lambda g, gate, f: _post_fn(res_w, g, gate, f, 0.0), g, gate, f)
        dg, dgate, df = vjp(dy)
        return df, dg, dgate

    row = (1, D_MODEL)
    return _rowwise(fn, name, [g, gate], [f, dy], [(D_MODEL, BF16)], [row, row], deps=deps)


def _loss_stage(y, target, name):
    def fn(y, t):
        diff = y - t
        return diff * (1.0 / D_MODEL), _rowsum(diff * diff)

    return _rowwise(fn, name, [], [y, target], [(D_MODEL, F32)], [(1, D_MODEL)])


def _swiglu_fwd(gu, name):
    (a,) = _rowwise(_swiglu_fn, name, [], [gu], [(D_FF, BF16)], ts=128)
    return a


def _swiglu_bwd(gu, da, name, deps=()):
    def fn(gu, da):
        _, vjp = jax.vjp(_swiglu_fn, gu)
        return vjp(da)[0]

    (dgu,) = _rowwise(fn, name, [], [gu, da], [(2 * D_FF, BF16)], ts=128, deps=deps)
    return dgu


def _shift_down(ext, j, rows):
    return pltpu.roll(ext, j, 0)[V7X_SUBLANES : V7X_SUBLANES + rows]


def _shift_up(ext, j, rows):
    return pltpu.roll(ext, ext.shape[0] - j, 0)[:rows] if j else ext[:rows]


def _conv_fwd(proj, w8, b, name):
    def fn(i, steps, w8, b, x, halo):
        halo = jnp.where(i > 0, halo, 0.0)
        ext = jnp.concatenate([halo, x], axis=0)
        acc = b + w8[3:4] * x
        for j in (1, 2, 3):
            acc = acc + w8[3 - j : 4 - j] * _shift_down(ext, j, x.shape[0])
        return acc, acc

    tiles = [(proj, LRU_WIDTH, 0), (proj, LRU_WIDTH, 0, "prev")]
    return _rowwise(fn, name, [w8, b], tiles, [(LRU_WIDTH, F32), (LRU_WIDTH, BF16)], with_index=True)


def _conv_bwd(proj, w8, d1, d2, name):
    def fn(i, steps, w8, x, halo, d1, d1n, d2, d2n):
        rows = x.shape[0]
        d = d1 + d2
        dn = jnp.where(i < steps - 1, d1n + d2n, 0.0)
        halo = jnp.where(i > 0, halo, 0.0)
        dext = jnp.concatenate([d, dn], axis=0)
        xext = jnp.concatenate([halo, x], axis=0)
        dx = w8[3:4] * d
        dw = [None] * 4
        dw[3] = _rowsum(d * x)
        for k in (1, 2, 3):
            dx = dx + w8[3 - k : 4 - k] * _shift_up(dext, k, rows)
            dw[3 - k] = _rowsum(d * _shift_down(xext, k, rows))
        dw8 = jnp.concatenate(dw + [jnp.zeros((4, LRU_WIDTH), F32)], axis=0)
        return dx, dw8, _rowsum(d)

    tiles = [(proj, LRU_WIDTH, 0), (proj, LRU_WIDTH, 0, "prev"), d1, (d1, LRU_WIDTH, 0, "next"),
             d2, (d2, LRU_WIDTH, 0, "next")]
    return _rowwise(fn, name, [w8], tiles, [(LRU_WIDTH, BF16)], [(8, LRU_WIDTH), (1, LRU_WIDTH)], with_index=True)


def _gates_fwd(pre, xc, ba, bx, lam, name):
    return _rowwise(_gates_fn, name, [ba, bx, lam], [pre, xc], [(LRU_WIDTH, F32), (LRU_WIDTH, F32)])


def _gates_bwd(pre, xc, ba, bx, lam, g, h_prev, name):
    def fn(ba, bx, lam, pre, xc, g, h_prev):
        _, vjp = jax.vjp(_gates_fn, ba, bx, lam, pre, xc)
        dba, dbx, dlam, dpre, dxc = vjp((g * h_prev, g))
        return dpre, dxc, dba, dbx, dlam

    row = (1, LRU_WIDTH)
    return _rowwise(fn, name, [ba, bx, lam], [pre, xc, g, h_prev],
                    [(2 * LRU_WIDTH, BF16), (LRU_WIDTH, F32)], [row, row, row])


SCAN_ROWS = 512


def _block_scan(a, b, row, reverse):
    for d in (1, 2, 4):
        if reverse:
            shift, keep = V7X_SUBLANES - d, row < V7X_SUBLANES - d
        else:
            shift, keep = d, row >= d
        a_s = pltpu.roll(a, shift, 0)
        b_s = pltpu.roll(b, shift, 0)
        b = jnp.where(keep, a * b_s + b, b)
        a = jnp.where(keep, a * a_s, a)
    return a, b


def _scan_fwd(a, u, name):
    s, w = a.shape
    ts = min(SCAN_ROWS, s)
    sub = ts // V7X_SUBLANES

    def body(a_ref, u_ref, h_ref, hp_ref, carry):
        @pl.when(pl.program_id(0) == 0)
        def _():
            carry[...] = jnp.zeros_like(carry)

        row = lax.broadcasted_iota(jnp.int32, (V7X_SUBLANES, w), 0)

        def step(j, c):
            rows = pl.ds(pl.multiple_of(j * V7X_SUBLANES, V7X_SUBLANES), V7X_SUBLANES)
            pa, pb = _block_scan(a_ref[rows, :], u_ref[rows, :], row, False)
            h = pb + pa * c
            h_ref[rows, :] = h
            hp_ref[rows, :] = jnp.where(row >= 1, pltpu.roll(h, 1, 0), c)
            return jnp.broadcast_to(h[V7X_SUBLANES - 1 :], (V7X_SUBLANES, w))

        carry[...] = lax.fori_loop(0, sub, step, carry[...])

    spec = pl.BlockSpec((ts, w), lambda i: (i, 0))
    return pl.pallas_call(
        body,
        name=name,
        grid=(s // ts,),
        in_specs=[spec, spec],
        out_specs=[spec, spec],
        out_shape=[jax.ShapeDtypeStruct((s, w), F32)] * 2,
        scratch_shapes=[pltpu.VMEM((V7X_SUBLANES, w), F32)],
        compiler_params=_cparams(dimension_semantics=("arbitrary",)),
    )(a, u)


def _scan_bwd(a, dh, name):
    s, w = a.shape
    ts = min(SCAN_ROWS, s)
    sub = ts // V7X_SUBLANES
    steps = s // ts

    def body(a_ref, d_ref, g_ref, carry):
        @pl.when(pl.program_id(0) == 0)
        def _():
            carry[...] = jnp.zeros_like(carry)

        row = lax.broadcasted_iota(jnp.int32, (V7X_SUBLANES, w), 0)

        def step(jj, c):
            j = sub - 1 - jj
            rows = pl.ds(pl.multiple_of(j * V7X_SUBLANES, V7X_SUBLANES), V7X_SUBLANES)
            av, dv = a_ref[rows, :], d_ref[rows, :]
            pa, pb = _block_scan(av, av * dv, row, True)
            big = pb + pa * c
            g_ref[rows, :] = dv + jnp.where(row < V7X_SUBLANES - 1, pltpu.roll(big, V7X_SUBLANES - 1, 0), c)
            return jnp.broadcast_to(big[:1], (V7X_SUBLANES, w))

        carry[...] = lax.fori_loop(0, sub, step, carry[...])

    spec = pl.BlockSpec((ts, w), lambda i: (steps - 1 - i, 0))
    return pl.pallas_call(
        body,
        name=name,
        grid=(steps,),
        in_specs=[spec, spec],
        out_specs=spec,
        out_shape=jax.ShapeDtypeStruct((s, w), F32),
        scratch_shapes=[pltpu.VMEM((V7X_SUBLANES, w), F32)],
        compiler_params=_cparams(dimension_semantics=("arbitrary",)),
    )(a, dh)


def _rel_index():
    i = np.arange(ATT_TQ)[:, None]
    j = np.arange(3 * ATT_TQ)[None, :]
    band = (j // CHUNK >= i // CHUNK) & (j // CHUNK <= i // CHUNK + LEFT_CHUNKS)
    return band


SKEW = 4 * ATT_TQ


def _skew_onehot():
    t = np.arange(SKEW)
    diag = np.where(t < 3 * ATT_TQ, -t, SKEW - t)
    idx = np.clip(diag + LEFT_CHUNKS * CHUNK, -MAX_REL, MAX_REL) + MAX_REL
    hit = (idx[:, None] == np.arange(2 * MAX_REL + 1)[None, :]) & (t[:, None] != 3 * ATT_TQ)
    return hit.astype(np.float32)


def _bias_tile(rel_bias):
    per_t = jnp.dot(rel_bias, jnp.asarray(_skew_onehot()).T, precision=lax.Precision.HIGHEST)
    flat = jnp.broadcast_to(per_t[:, None, :], (ATT_HEADS, ATT_TQ, SKEW)).reshape(ATT_HEADS, ATT_TQ * SKEW)
    tile = flat[:, : ATT_TQ * (SKEW - 1)].reshape(ATT_HEADS, ATT_TQ, SKEW - 1)[:, :, : 3 * ATT_TQ]
    return jnp.where(jnp.asarray(_rel_index())[None], tile, NEG)


def _bias_grad(dbias):
    flat = jnp.pad(dbias, ((0, 0), (0, 0), (0, SKEW - 1 - 3 * ATT_TQ))).reshape(ATT_HEADS, ATT_TQ * (SKEW - 1))
    per_t = jnp.sum(jnp.pad(flat, ((0, 0), (0, ATT_TQ))).reshape(ATT_HEADS, ATT_TQ, SKEW), axis=1)
    return jnp.dot(per_t, jnp.asarray(_skew_onehot()), precision=lax.Precision.HIGHEST)


def _attn_specs(nt):
    qb, kb, vb = OFF_Q // V7X_LANES, OFF_K // V7X_LANES, OFF_V // V7X_LANES
    blk = (ATT_TQ, V7X_LANES)

    def qmap(base):
        return lambda hp, m: (jnp.minimum(m, nt - 1), base + hp)

    def wmap(base, back):
        return lambda hp, m: (jnp.clip(m - back, 0, nt - 1), base + hp)

    specs = [pl.BlockSpec(blk, qmap(qb))]
    specs += [pl.BlockSpec(blk, wmap(kb, back)) for back in (2, 1, 0)]
    specs += [pl.BlockSpec(blk, wmap(vb, back)) for back in (2, 1, 0)]
    return specs


def _attn_probs(qh, kh, bias, ok):
    s = lax.dot_general(qh, kh, _DIMS["nt"], preferred_element_type=F32) * (ATT_HEAD_DIM**-0.5) + bias
    s = jnp.where(ok, s, NEG)
    e = jnp.exp(s - jnp.max(s, axis=-1, keepdims=True))
    return e / jnp.sum(e, axis=-1, keepdims=True)


def _attn_window(m, k0, k1, k2, v0, v1, v2):
    k = jnp.concatenate([k0[...], k1[...], k2[...]], axis=0).astype(BF16)
    v = jnp.concatenate([v0[...], v1[...], v2[...]], axis=0).astype(BF16)
    kpos = lax.broadcasted_iota(jnp.int32, (ATT_TQ, 3 * ATT_TQ), 1) + (m - 2) * ATT_TQ
    return k, v, kpos >= 0


def _attn_fwd(proj, bias, name):
    s = proj.shape[0]
    nt = s // ATT_TQ

    def body(q_ref, k0, k1, k2, v0, v1, v2, b_ref, o_ref):
        m = pl.program_id(1)
        k, v, ok = _attn_window(m, k0, k1, k2, v0, v1, v2)
        q = q_ref[...].astype(BF16)
        for hh in range(2):
            cols = slice(hh * ATT_HEAD_DIM, (hh + 1) * ATT_HEAD_DIM)
            p = _attn_probs(q[:, cols], k[:, cols], b_ref[hh], ok)
            o = jnp.dot(p.astype(BF16), v[:, cols], preferred_element_type=F32)
            o_ref[:, cols] = o.astype(o_ref.dtype)

    specs = _attn_specs(nt) + [pl.BlockSpec((2, ATT_TQ, 3 * ATT_TQ), lambda hp, m: (hp, 0, 0))]
    return pl.pallas_call(
        body,
        name=name,
        grid=(ATT_HEADS // 2, nt),
        in_specs=specs,
        out_specs=pl.BlockSpec((ATT_TQ, V7X_LANES), lambda hp, m: (m, hp)),
        out_shape=jax.ShapeDtypeStruct((s, ATT_WIDTH), BF16),
        compiler_params=_cparams(dimension_semantics=("parallel", "arbitrary")),
    )(proj, proj, proj, proj, proj, proj, proj, bias)


def _attn_bwd(proj, bias, do, name):
    s = proj.shape[0]
    nt = s // ATT_TQ
    win = 3 * ATT_TQ

    def body(q_ref, k0, k1, k2, v0, v1, v2, do_ref, b_ref, dq_ref, dk_ref, dv_ref, db_ref, dk_acc, dv_acc):
        m = pl.program_id(1)

        @pl.when(m == 0)
        def _():
            dk_acc[...] = jnp.zeros_like(dk_acc)
            dv_acc[...] = jnp.zeros_like(dv_acc)
            db_ref[...] = jnp.zeros_like(db_ref)

        @pl.when(m < nt)
        def _():
            k, v, ok = _attn_window(m, k0, k1, k2, v0, v1, v2)
            q = q_ref[...].astype(BF16)
            dout = do_ref[...]
            for hh in range(2):
                cols = slice(hh * ATT_HEAD_DIM, (hh + 1) * ATT_HEAD_DIM)
                qh, kh, vh, doh = q[:, cols], k[:, cols], v[:, cols], dout[:, cols]
                p = _attn_probs(qh, kh, b_ref[hh], ok)
                dvh = lax.dot_general(p.astype(BF16), doh, _DIMS["tn"], preferred_element_type=F32)
                dp = lax.dot_general(doh, vh, _DIMS["nt"], preferred_element_type=F32)
                ds = p * (dp - jnp.sum(dp * p, axis=-1, keepdims=True))
                db_ref[hh] += ds
                dsb = ds.astype(BF16)
                dqh = jnp.dot(dsb, kh, preferred_element_type=F32) * (ATT_HEAD_DIM**-0.5)
                dkh = lax.dot_general(dsb, qh, _DIMS["tn"], preferred_element_type=F32) * (ATT_HEAD_DIM**-0.5)
                dq_ref[:, cols] = dqh.astype(dq_ref.dtype)
                dk_acc[:, cols] += dkh
                dv_acc[:, cols] += dvh

        dk_ref[...] = dk_acc[:ATT_TQ].astype(dk_ref.dtype)
        dv_ref[...] = dv_acc[:ATT_TQ].astype(dv_ref.dtype)
        for acc in (dk_acc, dv_acc):
            rest = acc[ATT_TQ:]
            acc[: win - ATT_TQ] = rest
            acc[win - ATT_TQ :] = jnp.zeros((ATT_TQ, V7X_LANES), F32)

    blk = (ATT_TQ, V7X_LANES)
    specs = _attn_specs(nt)
    specs.append(pl.BlockSpec(blk, lambda hp, m: (jnp.minimum(m, nt - 1), hp)))
    specs.append(pl.BlockSpec((2, ATT_TQ, win), lambda hp, m: (hp, 0, 0)))
    done = lambda hp, m: (jnp.maximum(m - 2, 0), hp)
    out_specs = [
        pl.BlockSpec(blk, lambda hp, m: (jnp.minimum(m, nt - 1), hp)),
        pl.BlockSpec(blk, done),
        pl.BlockSpec(blk, done),
        pl.BlockSpec((2, ATT_TQ, win), lambda hp, m: (hp, 0, 0)),
    ]
    out_shape = [jax.ShapeDtypeStruct((s, ATT_WIDTH), BF16)] * 3
    out_shape.append(jax.ShapeDtypeStruct((ATT_HEADS, ATT_TQ, win), F32))
    return pl.pallas_call(
        body,
        name=name,
        grid=(ATT_HEADS // 2, nt + 2),
        in_specs=specs,
        out_specs=out_specs,
        out_shape=out_shape,
        scratch_shapes=[pltpu.VMEM((win, V7X_LANES), F32), pltpu.VMEM((win, V7X_LANES), F32)],
        compiler_params=_cparams(dimension_semantics=("arbitrary", "arbitrary")),
    )(proj, proj, proj, proj, proj, proj, proj, do, bias)


def _ada_fwd(c_all, w, name):
    def body(c_ref, w_ref, o_ref):
        act = _silu(c_ref[...]).astype(BF16)
        o_ref[...] = jnp.dot(act, w_ref[...].astype(BF16), preferred_element_type=F32)

    return pl.pallas_call(
        body, name=name, out_shape=jax.ShapeDtypeStruct((c_all.shape[0], w.shape[1]), F32), compiler_params=_cparams()
    )(c_all, w)


def _ada_bwd(c_all, dmod, name):
    def body(c_ref, d_ref, o_ref):
        act = _silu(c_ref[...])
        o_ref[...] = lax.dot_general(act, d_ref[...], _DIMS["tn"], preferred_element_type=F32,
                                     precision=lax.Precision.HIGHEST)

    return pl.pallas_call(
        body, name=name, out_shape=jax.ShapeDtypeStruct((c_all.shape[1], dmod.shape[1]), F32), compiler_params=_cparams()
    )(c_all, dmod)


def _adamw(g, w, m, v, name, rows=256):
    r, c = w.shape
    tr = _pick(r, rows, 16)
    parts = g.ndim == 3

    def body(g_ref, w_ref, m_ref, v_ref, go_ref, d_ref, mo_ref, vo_ref):
        if parts:
            grad = g_ref[0].astype(F32)
            for d in range(1, N_DEV):
                grad = grad + g_ref[d].astype(F32)
        else:
            grad = g_ref[...]
        m2 = ADAM_B1 * m_ref[...] + (1.0 - ADAM_B1) * grad
        v2 = ADAM_B2 * v_ref[...] + (1.0 - ADAM_B2) * (grad * grad)
        m_hat = m2 / (1.0 - ADAM_B1**ADAM_STEP)
        v_hat = v2 / (1.0 - ADAM_B2**ADAM_STEP)
        go_ref[...] = grad
        d_ref[...] = -ADAM_LR * (m_hat / (jnp.sqrt(v_hat) + ADAM_EPS) + ADAM_WD * w_ref[...])
        mo_ref[...] = m2
        vo_ref[...] = v2

    spec = pl.BlockSpec((tr, c), lambda i: (i, 0))
    g_spec = pl.BlockSpec((N_DEV, tr, c), lambda i: (0, i, 0)) if parts else spec
    return pl.pallas_call(
        body,
        name=name,
        grid=(r // tr,),
        in_specs=[g_spec, spec, spec, spec],
        out_specs=[spec] * 4,
        out_shape=[jax.ShapeDtypeStruct((r, c), F32)] * 4,
        compiler_params=_cparams(dimension_semantics=("parallel",)),
    )(g, w, m, v)


def _sum_parts(parts, name):
    def body(p_ref, o_ref):
        acc = p_ref[0]
        for d in range(1, N_DEV):
            acc = acc + p_ref[d]
        o_ref[...] = acc

    return pl.pallas_call(
        body, name=name, out_shape=jax.ShapeDtypeStruct(parts.shape[1:], F32), compiler_params=_cparams()
    )(parts)


def _place():
    x, y, c = lax.axis_index("x"), lax.axis_index("y"), lax.axis_index("c")
    return x, y, c


def _dev_index(p):
    return 4 * p[0] + 2 * p[1] + p[2]


def _allgather_vmem(shard, name):
    m_per, n = shard.shape

    def body(x_ref, out_ref, send_sems, recv_sems, local_sem):
        x, y, c = _place()
        me, sibling = (x, y, c), (x, y, 1 - c)
        chips = [(1 - x, y), (x, 1 - y), (1 - x, 1 - y)]

        def rows(p):
            return out_ref.at[pl.ds(_dev_index(p) * m_per, m_per), :]

        def copy(k, block, to, src=None):
            return pltpu.make_async_remote_copy(
                src_ref=rows(block) if src is None else src, dst_ref=rows(block),
                send_sem=send_sems.at[k], recv_sem=recv_sems.at[k], device_id=to, device_id_type=MESH)

        mine = pltpu.make_async_copy(x_ref, rows(me), local_sem)
        mine.start()
        first = [copy(0, me, sibling, src=x_ref)]
        first += [copy(1 + j, me, (*chip, c), src=x_ref) for j, chip in enumerate(chips)]
        for cp in first:
            cp.start()
        passed = [copy(4 + j, (*chip, c), sibling) for j, chip in enumerate(chips)]
        for j, chip in enumerate(chips):
            copy(1 + j, (*chip, c), me).wait_recv()
            passed[j].start()
        copy(0, sibling, me).wait_recv()
        for j, chip in enumerate(chips):
            copy(4 + j, (*chip, 1 - c), me).wait_recv()
        for cp in first + passed:
            cp.wait_send()
        mine.wait()

    return pl.pallas_call(
        body,
        name=name,
        out_shape=jax.ShapeDtypeStruct((N_DEV * m_per, n), shard.dtype),
        in_specs=[pl.BlockSpec(memory_space=pltpu.VMEM)],
        out_specs=pl.BlockSpec(memory_space=pltpu.VMEM),
        scratch_shapes=[pltpu.SemaphoreType.DMA((7,)), pltpu.SemaphoreType.DMA((7,)), pltpu.SemaphoreType.DMA],
        compiler_params=_cparams(),
    )(shard)


def _allgather_hbm(shards, name):
    n = len(shards)

    def body(*refs):
        ins, outs = refs[:n], refs[n : 2 * n]
        send_sems, recv_sems, local_sems = refs[2 * n :]
        x, y, c = _place()
        me, sibling = (x, y, c), (x, y, 1 - c)
        chips = [(1 - x, y), (x, 1 - y), (1 - x, 1 - y)]

        def copy(a, k, block, to, src=None):
            dst = outs[a].at[_dev_index(block)]
            return pltpu.make_async_remote_copy(
                src_ref=dst if src is None else src, dst_ref=dst,
                send_sem=send_sems.at[a * 7 + k], recv_sem=recv_sems.at[a * 7 + k], device_id=to, device_id_type=MESH)

        mine = [pltpu.make_async_copy(ins[a], outs[a].at[_dev_index(me)], local_sems.at[a]) for a in range(n)]
        for cp in mine:
            cp.start()
        first = []
        for a in range(n):
            first.append(copy(a, 0, me, sibling, src=ins[a]))
            first += [copy(a, 1 + j, me, (*chip, c), src=ins[a]) for j, chip in enumerate(chips)]
        for cp in first:
            cp.start()
        passed = []
        for j, chip in enumerate(chips):
            for a in range(n):
                copy(a, 1 + j, (*chip, c), me).wait_recv()
                cp = copy(a, 4 + j, (*chip, c), sibling)
                cp.start()
                passed.append(cp)
        for a in range(n):
            copy(a, 0, sibling, me).wait_recv()
        for j, chip in enumerate(chips):
            for a in range(n):
                copy(a, 4 + j, (*chip, 1 - c), me).wait_recv()
        for cp in first + passed:
            cp.wait_send()
        for cp in mine:
            cp.wait()

    any_spec = pl.BlockSpec(memory_space=pl.ANY)
    return pl.pallas_call(
        body,
        name=name,
        out_shape=[jax.ShapeDtypeStruct((N_DEV, *s.shape), s.dtype) for s in shards],
        in_specs=[any_spec] * n,
        out_specs=[any_spec] * n,
        scratch_shapes=[pltpu.SemaphoreType.DMA((7 * n,)), pltpu.SemaphoreType.DMA((7 * n,)),
                        pltpu.SemaphoreType.DMA((n,))],
        compiler_params=_cparams(),
    )(*shards)


def _exchange_hbm(bufs, name):
    n = len(bufs)

    def body(*refs):
        ins, outs = refs[:n], refs[n : 2 * n]
        send_sems, recv_sems, local_sems = refs[2 * n :]
        x, y, c = _place()
        me = _dev_index((x, y, c))
        mine = [pltpu.make_async_copy(ins[a].at[me], outs[a].at[me], local_sems.at[a]) for a in range(n)]
        for cp in mine:
            cp.start()
        def peer_of(k):
            return (1 - x if k & 4 else x, 1 - y if k & 2 else y, 1 - c if k & 1 else c)

        copies = []
        for k in range(1, N_DEV):
            peer = peer_of(k)
            for a in range(n):
                copies.append(pltpu.make_async_remote_copy(
                    src_ref=ins[a].at[_dev_index(peer)], dst_ref=outs[a].at[me],
                    send_sem=send_sems.at[a * 7 + k - 1], recv_sem=recv_sems.at[a * 7 + k - 1],
                    device_id=peer, device_id_type=MESH))
        for cp in copies:
            cp.start()
        for k in range(1, N_DEV):
            peer = peer_of(k)
            for a in range(n):
                pltpu.make_async_remote_copy(
                    src_ref=ins[a].at[me], dst_ref=outs[a].at[_dev_index(peer)],
                    send_sem=send_sems.at[a * 7 + k - 1], recv_sem=recv_sems.at[a * 7 + k - 1],
                    device_id=peer, device_id_type=MESH).wait_recv()
        for cp in copies:
            cp.wait_send()
        for cp in mine:
            cp.wait()

    any_spec = pl.BlockSpec(memory_space=pl.ANY)
    return pl.pallas_call(
        body,
        name=name,
        out_shape=[jax.ShapeDtypeStruct(b.shape, b.dtype) for b in bufs],
        in_specs=[any_spec] * n,
        out_specs=[any_spec] * n,
        scratch_shapes=[pltpu.SemaphoreType.DMA((7 * n,)), pltpu.SemaphoreType.DMA((7 * n,)),
                        pltpu.SemaphoreType.DMA((n,))],
        compiler_params=_cparams(),
    )(*bufs)


HBM_SPEC = pl.BlockSpec(memory_space=pltpu.HBM)
SEM_SPEC = pl.BlockSpec(memory_space=pltpu.SEMAPHORE)
EFFECT = pltpu.SideEffectType.DATAFLOW_SIDE_EFFECTING


def _peers(x, y, c):
    return [(1 - x if k & 4 else x, 1 - y if k & 2 else y, 1 - c if k & 1 else c) for k in range(1, N_DEV)]


def _push_start(groups, sliced, name, after=()):
    flat = [b for g in groups for b in g]
    n, ng = len(flat), len(groups)
    sizes = [len(g) for g in groups]
    lands = [lax.empty(b.shape if sliced else (N_DEV, *b.shape), b.dtype) for b in flat]

    def body(*refs):
        ins, lnd = refs[:n], refs[n : 2 * n]
        sems = refs[2 * n + len(after) : 2 * n + len(after) + 2 * ng]
        token = refs[-1]
        x, y, c = _place()
        me = _dev_index((x, y, c))
        first = 0
        for gi, size in enumerate(sizes):
            for k, peer in enumerate(_peers(x, y, c)):
                for j in range(first, first + size):
                    sem = (j - first) * 7 + k
                    pltpu.make_async_remote_copy(
                        src_ref=ins[j].at[_dev_index(peer)] if sliced else ins[j], dst_ref=lnd[j].at[me],
                        send_sem=sems[2 * gi].at[sem], recv_sem=sems[2 * gi + 1].at[sem],
                        device_id=peer, device_id_type=MESH).start()
            first += size
        token[...] = jnp.zeros_like(token)

    out_shape = []
    for size in sizes:
        out_shape += [pltpu.SemaphoreType.DMA((7 * size,)), pltpu.SemaphoreType.DMA((7 * size,))]
    out_shape += [pltpu.HBM(b.shape, b.dtype) for b in flat + lands]
    out_shape.append(jax.ShapeDtypeStruct((V7X_SUBLANES, V7X_LANES), F32))
    res = pl.pallas_call(
        body,
        name=name,
        out_shape=tuple(out_shape),
        in_specs=[HBM_SPEC] * (2 * n) + [ANY_SPEC] * len(after),
        out_specs=tuple([SEM_SPEC] * (2 * ng) + [HBM_SPEC] * (2 * n) + [pl.BlockSpec(memory_space=pltpu.VMEM)]),
        input_output_aliases={i: 2 * ng + i for i in range(2 * n)},
        compiler_params=pltpu.CompilerParams(has_side_effects=EFFECT),
    )(*[pltpu.with_memory_space_constraint(b, pltpu.HBM) for b in flat + lands], *after)
    sems, thru, token = res[: 2 * ng], res[2 * ng : 2 * ng + 2 * n], res[-1]
    out, first = [], 0
    for gi, size in enumerate(sizes):
        out.append((sems[2 * gi], sems[2 * gi + 1], list(thru[first : first + size]),
                    list(thru[n + first : n + first + size])))
        first += size
    return out, token


def _push_wait(started, sliced, after, name):
    send_sems, recv_sems, bufs, lands = started
    n = len(bufs)

    def body(*refs):
        ins, lnd = refs[:n], refs[n : 2 * n]
        send_ref, recv_ref = refs[2 * n], refs[2 * n + 1]
        x, y, c = _place()
        for k, peer in enumerate(_peers(x, y, c)):
            for j in range(n):
                cp = pltpu.make_async_remote_copy(
                    src_ref=ins[j].at[_dev_index(peer)] if sliced else ins[j], dst_ref=lnd[j].at[_dev_index(peer)],
                    send_sem=send_ref.at[j * 7 + k], recv_sem=recv_ref.at[j * 7 + k],
                    device_id=peer, device_id_type=MESH)
                cp.wait_send()
                cp.wait_recv()

    res = pl.pallas_call(
        body,
        name=name,
        out_shape=tuple(pltpu.HBM(b.shape, b.dtype) for b in bufs + lands),
        in_specs=[HBM_SPEC] * (2 * n) + [SEM_SPEC, SEM_SPEC, pl.BlockSpec(memory_space=pl.ANY)],
        out_specs=tuple([HBM_SPEC] * (2 * n)),
        input_output_aliases={i: i for i in range(2 * n)},
        compiler_params=pltpu.CompilerParams(has_side_effects=EFFECT),
    )(*bufs, *lands, send_sems, recv_sems, after)
    return list(res[n:])


def _with_own_row(land, own, me):
    return lax.dynamic_update_index_in_dim(land, own, me, 0)


def _cols_full(g):
    return jnp.transpose(g, (1, 0, 2)).reshape(g.shape[1], -1)


def _rows_full(g):
    return g.reshape(-1, g.shape[2])


def _cols_parts(full):
    r = full.shape[0]
    return jnp.transpose(full.reshape(r, N_DEV, -1), (1, 0, 2)).astype(BF16)


def _rows_parts(full):
    return full.reshape(N_DEV, -1, full.shape[1]).astype(BF16)


def _block_diag(w):
    eye = jnp.eye(LRU_BLOCKS, dtype=w.dtype)
    return jnp.einsum("nkj,nm->nkmj", w, eye).reshape(LRU_WIDTH, LRU_WIDTH)


def _diag_blocks(full):
    r = full.reshape(LRU_BLOCKS, LRU_BLOCK, LRU_BLOCKS, LRU_BLOCK)
    idx = np.arange(LRU_BLOCKS)
    return r[idx, :, idx, :]


def _pad_rows(v, rows):
    flat = v.reshape(-1)
    return jnp.pad(flat, (0, rows * D_MODEL - flat.shape[0])).reshape(rows, D_MODEL)


def _my_cols(full, me, width):
    return lax.dynamic_slice_in_dim(full, me * width, width, axis=full.ndim - 1)


def kernel(x, c, w_ada, b_ada, norm_pre, norm_post, ffn1_w_gu, ffn1_w_down, w_in, rel_bias, conv_w, conv_b, lru_wa, lru_ba, lru_wx, lru_bx, lru_lambda, w_att_o, w_rec_o, w_out, ffn2_w_gu, ffn2_w_down, loss_target, m_w_ada, m_b_ada, m_norm_pre, m_norm_post, m_ffn1_w_gu, m_ffn1_w_down, m_w_in, m_rel_bias, m_conv_w, m_conv_b, m_lru_wa, m_lru_ba, m_lru_wx, m_lru_bx, m_lru_lambda, m_w_att_o, m_w_rec_o, m_w_out, m_ffn2_w_gu, m_ffn2_w_down, v_w_ada, v_b_ada, v_norm_pre, v_norm_post, v_ffn1_w_gu, v_ffn1_w_down, v_w_in, v_rel_bias, v_conv_w, v_conv_b, v_lru_wa, v_lru_ba, v_lru_wx, v_lru_bx, v_lru_lambda, v_w_att_o, v_w_rec_o, v_w_out, v_ffn2_w_gu, v_ffn2_w_down):
    weights = dict(w_ada=w_ada, b_ada=b_ada, norm_pre=norm_pre, norm_post=norm_post, ffn1_w_gu=ffn1_w_gu,
                   ffn1_w_down=ffn1_w_down, w_in=w_in, rel_bias=rel_bias, conv_w=conv_w, conv_b=conv_b,
                   lru_wa=lru_wa, lru_ba=lru_ba, lru_wx=lru_wx, lru_bx=lru_bx, lru_lambda=lru_lambda,
                   w_att_o=w_att_o, w_rec_o=w_rec_o, w_out=w_out, ffn2_w_gu=ffn2_w_gu, ffn2_w_down=ffn2_w_down)
    mom1 = dict(w_ada=m_w_ada, b_ada=m_b_ada, norm_pre=m_norm_pre, norm_post=m_norm_post, ffn1_w_gu=m_ffn1_w_gu,
                ffn1_w_down=m_ffn1_w_down, w_in=m_w_in, rel_bias=m_rel_bias, conv_w=m_conv_w, conv_b=m_conv_b,
                lru_wa=m_lru_wa, lru_ba=m_lru_ba, lru_wx=m_lru_wx, lru_bx=m_lru_bx, lru_lambda=m_lru_lambda,
                w_att_o=m_w_att_o, w_rec_o=m_w_rec_o, w_out=m_w_out, ffn2_w_gu=m_ffn2_w_gu, ffn2_w_down=m_ffn2_w_down)
    mom2 = dict(w_ada=v_w_ada, b_ada=v_b_ada, norm_pre=v_norm_pre, norm_post=v_norm_post, ffn1_w_gu=v_ffn1_w_gu,
                ffn1_w_down=v_ffn1_w_down, w_in=v_w_in, rel_bias=v_rel_bias, conv_w=v_conv_w, conv_b=v_conv_b,
                lru_wa=v_lru_wa, lru_ba=v_lru_ba, lru_wx=v_lru_wx, lru_bx=v_lru_bx, lru_lambda=v_lru_lambda,
                w_att_o=v_w_att_o, w_rec_o=v_w_rec_o, w_out=v_w_out, ffn2_w_gu=v_ffn2_w_gu, ffn2_w_down=v_ffn2_w_down)
    order = list(weights)
    big = ["ffn1_w_gu", "ffn1_w_down", "w_in", "w_att_o", "w_rec_o", "w_out", "ffn2_w_gu", "ffn2_w_down"]
    col_sharded = {"ffn1_w_gu", "w_in", "w_att_o", "ffn2_w_gu"}
    small = ["b_ada", "norm_pre", "norm_post", "rel_bias", "conv_w", "conv_b", "lru_wa", "lru_ba", "lru_wx",
             "lru_bx", "lru_lambda"]

    xi, yi, ci = _place()
    me = _dev_index((xi, yi, ci))
    x0 = x[0]
    target = loss_target[0]

    shards = {n: weights[n][0].astype(BF16) for n in big}
    full_of = lambda n, g: _cols_full(g) if n in col_sharded else _rows_full(g)

    pack = jnp.concatenate([c.reshape(-1), norm_pre.reshape(-1), norm_post.reshape(-1), conv_w.reshape(-1)])
    pack = jnp.pad(pack, (0, 3072 - pack.shape[0])).reshape(8, 384)
    got = _allgather_vmem(pack, "gather_small_inputs").reshape(N_DEV, 3072)
    c_all = got[:, :1024]
    unshard = lambda blk, rows: jnp.transpose(blk.reshape(N_DEV, rows, 128), (1, 0, 2)).reshape(rows, D_MODEL)
    g_pre = unshard(got[:, 1024:1408], 3)
    g_post = unshard(got[:, 1408:1792], 3)
    conv_taps = unshard(got[:, 1792:2304], 4)
    conv_w8 = jnp.concatenate([conv_taps, jnp.zeros((4, LRU_WIDTH), F32)], axis=0)

    mod_cols = _ada_fwd(c_all, w_ada[0], "ada_fwd")
    mod_all = _allgather_vmem(mod_cols, "gather_mod").reshape(N_DEV, N_DEV, 1152)
    mod = lax.dynamic_index_in_dim(mod_all, me, axis=1, keepdims=False).reshape(1, -1) + b_ada
    mod = mod.reshape(3, 3, 1, D_MODEL)

    w_bd = jnp.concatenate([_block_diag(lru_wa[0]), _block_diag(lru_wx[0])], axis=1).astype(BF16)
    bias = _bias_tile(rel_bias[0])

    res_w = (0.5, 1.0, 0.5)
    row = lambda v: v.reshape(1, -1)

    (w1_gu,) = _allgather_hbm([shards["ffn1_w_gu"]], "gather_ffn1_w_gu")
    weight_groups = [["ffn1_w_down"], ["w_in", "w_att_o", "w_rec_o", "w_out"], ["ffn2_w_gu", "ffn2_w_down"]]
    weights_started, started = _push_start([[shards[n] for n in g] for g in weight_groups], False,
                                           "gather_weights_start", after=(mod, w1_gu))
    full = {"ffn1_w_gu": _cols_full(w1_gu)}

    def gathered_group(gi, after):
        lands = _push_wait(weights_started[gi], False, after, f"gather_weights_wait{gi}")
        for n, land in zip(weight_groups[gi], lands):
            full[n] = full_of(n, _with_own_row(land, shards[n], me))

    def ffn_fwd(xin, k, gi, tag, deps=()):
        h = _pre_fwd(xin, row(g_pre[k]), mod[k, 0], mod[k, 1], f"{tag}_pre", deps=deps)
        gu = _matmul(h, full[f"{tag}_w_gu"], "nn", F32, f"{tag}_gu")
        if f"{tag}_w_down" not in full:
            gathered_group(gi, gu)
        a = _swiglu_fwd(gu, f"{tag}_act")
        f = _matmul(a, full[f"{tag}_w_down"], "nn", F32, f"{tag}_down")
        xout = _post_fwd(f, xin, row(g_post[k]), mod[k, 2], res_w[k], f"{tag}_post")
        return xout, (h, gu, a, f)

    x1, saved1 = ffn_fwd(x0, 0, 0, "ffn1", deps=(started,))

    gathered_group(1, x1)
    w_in_p = jnp.concatenate([full["w_in"][:, 3 * ATT_WIDTH :], full["w_in"][:, : 3 * ATT_WIDTH]], axis=1)
    h2 = _pre_fwd(x1, row(g_pre[1]), mod[1, 0], mod[1, 1], "mix_pre")
    proj = _matmul(h2, w_in_p, "nn", F32, "mix_in")
    att_o = _attn_fwd(proj, bias, "attn_fwd")
    xc, xcb = _conv_fwd(proj, conv_w8, conv_b, "conv_fwd")
    pre = _matmul(xcb, w_bd, "nn", F32, "lru_gate_proj")
    a_t, u_t = _gates_fwd(pre, xc, lru_ba, lru_bx, lru_lambda, "lru_gates")
    hs, h_prev = _scan_fwd(a_t, u_t, "lru_scan")
    (rec_in,) = _rowwise(_recin_fn, "rec_in", [], [hs, (proj, LRU_WIDTH, 1)], [(LRU_WIDTH, BF16)])
    att = _matmul(att_o, full["w_att_o"], "nn", F32, "att_out")
    rec = _matmul(rec_in, full["w_rec_o"], "nn", F32, "rec_out")
    (merged,) = _rowwise(_merge_fn, "merge", [], [att, rec, (proj, LRU_WIDTH, 2), (proj, LRU_WIDTH, 3)],
                         [(D_MODEL, BF16)])
    f2 = _matmul(merged, full["w_out"], "nn", F32, "mix_out")
    x2 = _post_fwd(f2, x1, row(g_post[1]), mod[1, 2], res_w[1], "mix_post")

    gathered_group(2, x2)
    x3, saved3 = ffn_fwd(x2, 2, 2, "ffn2")

    dy, sq = _loss_stage(x3, target, "loss")
    loss = lax.psum(0.5 * jnp.sum(sq) / D_MODEL, ("x", "y", "c"))

    grads = {}
    dmod = [[None] * 3 for _ in range(3)]
    d_pre, d_post = [None] * 3, [None] * 3

    pending = []

    def exchange_start(names, tag):
        send = [(_cols_parts if n in col_sharded else _rows_parts)(grads[n]) for n in names]
        (group,), token = _push_start([send], True, f"exchange_{tag}_start")
        pending.append((names, send, group, tag))
        return token

    def exchange_finish(names, send, group, tag, after):
        lands = _push_wait(group, True, after, f"exchange_{tag}_wait")
        res = None
        for n, land, mine in zip(names, lands, send):
            own = lax.dynamic_index_in_dim(mine, me, axis=0, keepdims=False)
            res = _adamw(_with_own_row(land, own, me), weights[n][0], mom1[n][0], mom2[n][0], f"adamw_{n}")
            out_g[n], out_d[n], out_m[n], out_v[n] = [r.reshape(weights[n].shape) for r in res]
        return res[0]

    out_g, out_d, out_m, out_v = {}, {}, {}, {}

    def ffn_bwd(xin, k, saved, dout, tag):
        h, gu, a, f = saved
        w_gu, w_down = f"{tag}_w_gu", f"{tag}_w_down"
        df, d_post[k], dmod[k][2] = _post_bwd(f, row(g_post[k]), mod[k, 2], res_w[k], dout, f"{tag}_post_bwd")
        da = _matmul(df, full[w_down], "nt", F32, f"{tag}_da", tn=1408)
        grads[w_down] = _matmul(a, df, "tn", F32, f"{tag}_dw_down", tm=1408, tn=1024, tk=1024)
        started = exchange_start([w_down], w_down)
        dgu = _swiglu_bwd(gu, da, f"{tag}_act_bwd", deps=(started,))
        grads[w_gu] = _matmul(h, dgu, "tn", F32, f"{tag}_dw_gu", tm=1024, tn=1408, tk=1024)
        started = exchange_start([w_gu], w_gu)
        dh = _matmul(dgu, full[w_gu], "nt", F32, f"{tag}_dh", deps=(started,))
        dx, d_pre[k], dmod[k][0], dmod[k][1] = _pre_bwd(xin, row(g_pre[k]), mod[k, 0], mod[k, 1], dh, dout,
                                                         f"{tag}_pre_bwd")
        return dx

    dx2 = ffn_bwd(x2, 2, saved3, dy, "ffn2")

    df2, d_post[1], dmod[1][2] = _post_bwd(f2, row(g_post[1]), mod[1, 2], res_w[1], dx2, "mix_post_bwd")
    dmerged = _matmul(df2, full["w_out"], "nt", F32, "mix_dmerged")
    grads["w_out"] = _matmul(merged, df2, "tn", F32, "mix_dw_out", tm=1024, tn=1024, tk=1024)

    def merge_bwd(att, rec, g_att, g_rec, dm):
        _, vjp = jax.vjp(_merge_fn, att, rec, g_att, g_rec)
        return vjp(dm)

    datt, drec, dg_att, dg_rec = _rowwise(
        merge_bwd, "merge_bwd", [], [att, rec, (proj, LRU_WIDTH, 2), (proj, LRU_WIDTH, 3), dmerged],
        [(D_MODEL, BF16)] * 4)
    datt_o = _matmul(datt, full["w_att_o"], "nt", BF16, "att_out_bwd")
    grads["w_att_o"] = _matmul(att_o, datt, "tn", F32, "dw_att_o", tm=512, tn=1024, tk=1024)
    drec_in = _matmul(drec, full["w_rec_o"], "nt", F32, "rec_out_bwd")
    grads["w_rec_o"] = _matmul(rec_in, drec, "tn", F32, "dw_rec_o", tm=1024, tn=1024, tk=1024)
    started = exchange_start(["w_out", "w_att_o", "w_rec_o"], "mix_out")

    def recin_bwd(hs, yr, d):
        _, vjp = jax.vjp(_recin_fn, hs, yr)
        return vjp(d)

    dhs, dyr = _rowwise(recin_bwd, "rec_in_bwd", [], [hs, (proj, LRU_WIDTH, 1), drec_in],
                        [(LRU_WIDTH, F32), (LRU_WIDTH, BF16)], deps=(started,))
    g_t = _scan_bwd(a_t, dhs, "lru_scan_bwd")
    dpre, dxc_direct, d_ba, d_bx, d_lam = _gates_bwd(pre, xc, lru_ba, lru_bx, lru_lambda, g_t, h_prev,
                                                     "lru_gates_bwd")
    dxc_mm = _matmul(dpre, w_bd, "nt", F32, "lru_gate_proj_bwd")
    dw_bd = _matmul(xcb, dpre, "tn", F32, "dw_lru_gate", tm=1024, tn=1024, tk=1024)
    dxr, d_conv_w8, d_conv_b = _conv_bwd(proj, conv_w8, dxc_direct, dxc_mm, "conv_bwd")
    dq, dk, dv, dbias = _attn_bwd(proj, bias, datt_o, "attn_bwd")
    dproj = jnp.concatenate([dxr, dyr, dg_att, dg_rec, dq, dk, dv], axis=1)
    dw_in_p = _matmul(h2, dproj, "tn", F32, "mix_dw_in", tm=1024, tn=1408, tk=1024)
    grads["w_in"] = jnp.concatenate([dw_in_p[:, OFF_Q:], dw_in_p[:, :OFF_Q]], axis=1)
    started = exchange_start(["w_in"], "w_in")
    dh2 = _matmul(dproj, w_in_p, "nt", F32, "mix_dh", deps=(started,))
    dx1, d_pre[1], dmod[1][0], dmod[1][1] = _pre_bwd(x1, row(g_pre[1]), mod[1, 0], mod[1, 1], dh2, dx2, "mix_pre_bwd")

    dx0 = ffn_bwd(x0, 0, saved1, dx1, "ffn1")

    dmod_mine = jnp.concatenate([dmod[k][j] for k in range(3) for j in range(3)], axis=0)
    pieces = [
        dmod_mine,
        jnp.concatenate(d_pre, axis=0),
        jnp.concatenate(d_post, axis=0),
        d_conv_w8[:4],
        d_conv_b, d_ba, d_bx, d_lam,
        _pad_rows(_bias_grad(dbias), 3),
        _diag_blocks(dw_bd[:, :LRU_WIDTH]).reshape(64, D_MODEL),
        _diag_blocks(dw_bd[:, LRU_WIDTH:]).reshape(64, D_MODEL),
    ]
    small_rows = 160
    packed = jnp.concatenate(pieces, axis=0)
    packed = jnp.pad(packed, ((0, small_rows - packed.shape[0]), (0, 0)))
    parts = _allgather_vmem(packed, "gather_small_grads").reshape(N_DEV, small_rows, D_MODEL)
    total = _sum_parts(parts, "sum_small_grads")
    grads["b_ada"] = total[0:9].reshape(1, -1)
    grads["norm_pre"] = _my_cols(total[9:12], me, 128)
    grads["norm_post"] = _my_cols(total[12:15], me, 128)
    grads["conv_w"] = _my_cols(total[15:19], me, 128)
    grads["conv_b"] = total[19:20]
    grads["lru_ba"] = total[20:21]
    grads["lru_bx"] = total[21:22]
    grads["lru_lambda"] = total[22:23]
    grads["rel_bias"] = total[23:26].reshape(-1)[: ATT_HEADS * (2 * MAX_REL + 1)].reshape(ATT_HEADS, -1)
    grads["lru_wa"] = total[26:90].reshape(LRU_BLOCKS, LRU_BLOCK, LRU_BLOCK)
    grads["lru_wx"] = total[90:154].reshape(LRU_BLOCKS, LRU_BLOCK, LRU_BLOCK)
    dmod_all = parts[:, 0:9, :].reshape(N_DEV, 9 * D_MODEL)
    grads["w_ada"] = _ada_bwd(c_all, _my_cols(dmod_all, me, 1152), "ada_bwd")

    res = _adamw(grads["w_ada"], w_ada[0], m_w_ada[0], v_w_ada[0], "adamw_w_ada")
    out_g["w_ada"], out_d["w_ada"], out_m["w_ada"], out_v["w_ada"] = [r.reshape(w_ada.shape) for r in res]

    sizes = [int(np.prod(weights[n].shape)) for n in small]
    tot = sum(sizes)
    rows_small = -(-tot // (16 * D_MODEL)) * 16
    flat = lambda arrs: jnp.pad(jnp.concatenate([a.reshape(-1) for a in arrs]),
                                (0, rows_small * D_MODEL - tot)).reshape(rows_small, D_MODEL)
    res = _adamw(flat([grads[n] for n in small]), flat([weights[n] for n in small]),
                 flat([mom1[n] for n in small]), flat([mom2[n] for n in small]), "adamw_small", rows=rows_small)
    offs = np.cumsum([0] + sizes)
    for dst, r in zip((out_g, out_d, out_m, out_v), res):
        rf = r.reshape(-1)
        for i, n in enumerate(small):
            dst[n] = rf[offs[i] : offs[i + 1]].reshape(weights[n].shape)

    done = res[0]
    for names, send, group, tag in pending:
        done = exchange_finish(names, send, group, tag, done)

    return (loss, dx0[None], *[out_g[n] for n in order], *[out_d[n] for n in order],
            *[out_m[n] for n in order], *[out_v[n] for n in order])
```

```python
import functools

import jax
import jax.numpy as jnp
import numpy as np
from jax import lax
from jax.experimental import pallas as pl
from jax.experimental.pallas import tpu as pltpu

D_MODEL = 1024
D_FF = 2816
ATT_HEADS = 8
ATT_HEAD_DIM = 64
ATT_WIDTH = 512
CHUNK = 64
LEFT_CHUNKS = 8
MAX_REL = 128
LRU_WIDTH = 1024
LRU_BLOCKS = 16
LRU_BLOCK = 64
LRU_C = 8.0
EPS = 1e-6
PROJ_WIDTH = 5632
N_DEV = 8

ADAM_LR = 0.001
ADAM_B1 = 0.9
ADAM_B2 = 0.999
ADAM_EPS = 1e-08
ADAM_WD = 0.01
ADAM_STEP = 10

V7X_LANES = 128
V7X_SUBLANES = 8
V7X_VMEM_BYTES = 64 * 1024 * 1024
VMEM_LIMIT = V7X_VMEM_BYTES - 8 * 1024 * 1024

ATT_TQ = 256
NEG = -1e30
BF16 = jnp.bfloat16
F32 = jnp.float32
MESH = pl.DeviceIdType.MESH

OFF_Q = 4 * LRU_WIDTH
OFF_K = OFF_Q + ATT_WIDTH
OFF_V = OFF_K + ATT_WIDTH


def _cparams(**kw):
    return pltpu.CompilerParams(vmem_limit_bytes=VMEM_LIMIT, **kw)


def _pick(n, target, unit=V7X_LANES):
    best = None
    for t in range(unit, min(n, target) + 1, unit):
        if n % t == 0:
            best = t
    return n if best is None else best


_DIMS = {
    "nn": (((1,), (0,)), ((), ())),
    "nt": (((1,), (1,)), ((), ())),
    "tn": (((0,), (0,)), ((), ())),
}


ANY_SPEC = pl.BlockSpec(memory_space=pl.ANY)


def _matmul(a, b, mode, out_dtype, name, tm=1024, tn=512, tk=1408, deps=()):
    n_deps = len(deps)
    if mode == "nn":
        (m, k), (k2, n) = a.shape, b.shape
    elif mode == "nt":
        (m, k), (n, k2) = a.shape, b.shape
    else:
        (k, m), (k2, n) = a.shape, b.shape
    assert k == k2, (a.shape, b.shape, mode)
    tm, tn, tk = _pick(m, tm), _pick(n, tn), _pick(k, tk)
    nk = k // tk
    dims = _DIMS[mode]

    def body(a_ref, b_ref, *rest):
        o_ref, scratch = rest[n_deps], rest[n_deps + 1 :]
        p = lax.dot_general(a_ref[...], b_ref[...], dims, preferred_element_type=F32)
        if nk == 1:
            o_ref[...] = p.astype(o_ref.dtype)
        else:
            acc = scratch[0]
            kk = pl.program_id(2)

            @pl.when(kk == 0)
            def _():
                acc[...] = p

            @pl.when(kk > 0)
            def _():
                acc[...] += p

            @pl.when(kk == nk - 1)
            def _():
                o_ref[...] = acc[...].astype(o_ref.dtype)

    if mode == "nn":
        a_spec = pl.BlockSpec((tm, tk), lambda i, j, kk: (i, kk))
        b_spec = pl.BlockSpec((tk, tn), lambda i, j, kk: (kk, j))
    elif mode == "nt":
        a_spec = pl.BlockSpec((tm, tk), lambda i, j, kk: (i, kk))
        b_spec = pl.BlockSpec((tn, tk), lambda i, j, kk: (j, kk))
    else:
        a_spec = pl.BlockSpec((tk, tm), lambda i, j, kk: (kk, i))
        b_spec = pl.BlockSpec((tk, tn), lambda i, j, kk: (kk, j))
    return pl.pallas_call(
        body,
        name=name,
        grid=(m // tm, n // tn, nk),
        in_specs=[a_spec, b_spec] + [ANY_SPEC] * n_deps,
        out_specs=pl.BlockSpec((tm, tn), lambda i, j, kk: (i, j)),
        out_shape=jax.ShapeDtypeStruct((m, n), out_dtype),
        scratch_shapes=[pltpu.VMEM((tm, tn), F32)] if nk > 1 else [],
        compiler_params=_cparams(dimension_semantics=("parallel", "parallel", "arbitrary")),
    )(a, b, *deps)


def _rowwise(fn, name, params, tiles, outs, accs=(), ts=256, with_index=False, deps=()):
    norm = []
    for t in tiles:
        if not isinstance(t, tuple):
            t = (t, t.shape[1], 0)
        norm.append(t if len(t) == 4 else (*t, None))
    s = norm[0][0].shape[0]
    ts = min(ts, s)
    assert s % ts == 0 and ts % V7X_SUBLANES == 0
    steps = s // ts
    halo_blocks = ts // V7X_SUBLANES
    n_p, n_t, n_o = len(params), len(norm), len(outs)

    def body(*refs):
        i = pl.program_id(0)
        vals = [r[...] for r in refs[: n_p + n_t]]
        res = fn(i, steps, *vals) if with_index else fn(*vals)
        if not isinstance(res, (tuple, list)):
            res = (res,)
        first_out = n_p + n_t + len(deps)
        o_refs = refs[first_out : first_out + n_o]
        a_refs = refs[first_out + n_o :]
        for r, v in zip(o_refs, res[:n_o]):
            r[...] = v.astype(r.dtype)
        for r, v in zip(a_refs, res[n_o:]):
            _accumulate(r, v, i)

    in_specs = [pl.BlockSpec(p.shape, lambda i: (0, 0)) for p in params]
    for arr, w, cb, halo in norm:
        if halo is None:
            in_specs.append(pl.BlockSpec((ts, w), lambda i, cb=cb: (i, cb)))
        elif halo == "prev":
            in_specs.append(
                pl.BlockSpec((V7X_SUBLANES, w), lambda i, cb=cb: (jnp.maximum(i * halo_blocks - 1, 0), cb))
            )
        else:
            last = s // V7X_SUBLANES - 1
            in_specs.append(
                pl.BlockSpec((V7X_SUBLANES, w), lambda i, cb=cb: (jnp.minimum((i + 1) * halo_blocks, last), cb))
            )
    in_specs += [ANY_SPEC] * len(deps)
    out_specs = [pl.BlockSpec((ts, w), lambda i: (i, 0)) for w, _ in outs]
    out_specs += [pl.BlockSpec(shape, lambda i: (0, 0)) for shape in accs]
    out_shape = [jax.ShapeDtypeStruct((s, w), dt) for w, dt in outs]
    out_shape += [jax.ShapeDtypeStruct(shape, F32) for shape in accs]
    res = pl.pallas_call(
        body,
        name=name,
        grid=(steps,),
        in_specs=in_specs,
        out_specs=out_specs,
        out_shape=out_shape,
        compiler_params=_cparams(dimension_semantics=("arbitrary",)),
    )(*params, *[t[0] for t in norm], *deps)
    return res


def _accumulate(ref, val, step):
    @pl.when(step == 0)
    def _():
        ref[...] = val

    @pl.when(step > 0)
    def _():
        ref[...] += val


def _sigmoid(z):
    return jax.nn.sigmoid(z)


def _silu(z):
    return z * _sigmoid(z)


def _gelu(z):
    return 0.5 * z * (1.0 + jnp.tanh(0.7978845608028654 * (z + 0.044715 * (z * z * z))))


def _pre_fn(g, shift, scale, x):
    r = lax.rsqrt(jnp.mean(x * x, axis=-1, keepdims=True) + EPS)
    return ((x * r) * g) * (1.0 + scale) + shift


def _post_fn(res_w, g, gate, f, x):
    r = lax.rsqrt(jnp.mean(f * f, axis=-1, keepdims=True) + EPS)
    return x + (res_w * gate) * ((f * r) * g)


def _swiglu_fn(gu):
    return _silu(gu[:, :D_FF]) * gu[:, D_FF:]


def _gates_fn(ba, bx, lam, pre, xc):
    ra = _sigmoid(pre[:, :LRU_WIDTH] + ba)
    ia = _sigmoid(pre[:, LRU_WIDTH:] + bx)
    softplus = jnp.maximum(-lam, 0.0) + jnp.log1p(jnp.exp(-jnp.abs(lam)))
    log_a = (-LRU_C) * ra * softplus
    a = jnp.exp(log_a)
    mult = jnp.sqrt(-jnp.tanh(log_a) * (a * a + 1.0))
    return a, mult * (ia * xc)


def _recin_fn(hs, yr):
    return hs * _gelu(yr)


def _merge_fn(att, rec, g_att, g_rec):
    return _sigmoid(g_att) * att + _sigmoid(g_rec) * rec


def _rowsum(v):
    return jnp.sum(v, axis=0, keepdims=True)


def _pre_fwd(x, g, shift, scale, name, deps=()):
    (h,) = _rowwise(_pre_fn, name, [g, shift, scale], [x], [(D_MODEL, BF16)], deps=deps)
    return h


def _pre_bwd(x, g, shift, scale, dh, dres, name):
    def fn(g, shift, scale, x, dh, dres):
        _, vjp = jax.vjp(_pre_fn, g, shift, scale, x)
        dg, dshift, dscale, dx = vjp(dh)
        return dx + dres, dg, dshift, dscale

    row = (1, D_MODEL)
    return _rowwise(fn, name, [g, shift, scale], [x, dh, dres], [(D_MODEL, F32)], [row, row, row])


def _post_fwd(f, x, g, gate, res_w, name):
    (y,) = _rowwise(functools.partial(_post_fn, res_w), name, [g, gate], [f, x], [(D_MODEL, F32)])
    return y


def _post_bwd(f, g, gate, res_w, dy, name, deps=()):
    def fn(g, gate, f, dy):
        _, vjp = jax.vjp(lambda g, gate, f: _post_fn(res_w, g, gate, f, 0.0), g, gate, f)
        dg, dgate, df = vjp(dy)
        return df, dg, dgate

    row = (1, D_MODEL)
    return _rowwise(fn, name, [g, gate], [f, dy], [(D_MODEL, BF16)], [row, row], deps=deps)


def _loss_stage(y, target, name):
    def fn(y, t):
        diff = y - t
        return diff * (1.0 / D_MODEL), _rowsum(diff * diff)

    return _rowwise(fn, name, [], [y, target], [(D_MODEL, F32)], [(1, D_MODEL)])


FFN_TM = 512
FFN_TF = 1408


def _glu_fn(g, u):
    return _silu(g) * u


def _ffn_up(h, w_gu, name):
    s = h.shape[0]
    tm = min(FFN_TM, s)
    nf = D_FF // FFN_TF

    def body(h_ref, wg_ref, wu_ref, a_ref, g_ref, u_ref):
        hv = h_ref[...]
        g = jnp.dot(hv, wg_ref[...], preferred_element_type=F32)
        u = jnp.dot(hv, wu_ref[...], preferred_element_type=F32)
        a_ref[...] = _glu_fn(g, u).astype(a_ref.dtype)
        g_ref[...] = g.astype(g_ref.dtype)
        u_ref[...] = u.astype(u_ref.dtype)

    out = pl.BlockSpec((tm, FFN_TF), lambda i, j: (i, j))
    return pl.pallas_call(
        body,
        name=name,
        grid=(s // tm, nf),
        in_specs=[pl.BlockSpec((tm, D_MODEL), lambda i, j: (i, 0)),
                  pl.BlockSpec((D_MODEL, FFN_TF), lambda i, j: (0, j)),
                  pl.BlockSpec((D_MODEL, FFN_TF), lambda i, j: (0, nf + j))],
        out_specs=[out, out, out],
        out_shape=[jax.ShapeDtypeStruct((s, D_FF), BF16)] * 3,
        compiler_params=_cparams(dimension_semantics=("parallel", "arbitrary")),
    )(h, w_gu, w_gu)


def _ffn_up_bwd(df, w_down, g, u, name, deps=()):
    s = df.shape[0]
    tm = min(FFN_TM, s)

    def body(df_ref, wd_ref, g_ref, u_ref, *rest):
        dg_ref, du_ref = rest[len(deps) :]
        da = lax.dot_general(df_ref[...], wd_ref[...], _DIMS["nt"], preferred_element_type=F32)
        _, vjp = jax.vjp(_glu_fn, g_ref[...].astype(F32), u_ref[...].astype(F32))
        dg, du = vjp(da)
        dg_ref[...] = dg.astype(dg_ref.dtype)
        du_ref[...] = du.astype(du_ref.dtype)

    blk = pl.BlockSpec((tm, FFN_TF), lambda i, j: (i, j))
    return pl.pallas_call(
        body,
        name=name,
        grid=(s // tm, D_FF // FFN_TF),
        in_specs=[pl.BlockSpec((tm, D_MODEL), lambda i, j: (i, 0)),
                  pl.BlockSpec((FFN_TF, D_MODEL), lambda i, j: (j, 0)), blk, blk] + [ANY_SPEC] * len(deps),
        out_specs=[blk, blk],
        out_shape=[jax.ShapeDtypeStruct((s, D_FF), BF16)] * 2,
        compiler_params=_cparams(dimension_semantics=("parallel", "arbitrary")),
    )(df, w_down, g, u, *deps)


def _ffn_dh(dg, du, w_gu, name, deps=()):
    s = dg.shape[0]
    tm, tn = min(1024, s), 512
    nk = D_FF // FFN_TF

    def body(dg_ref, du_ref, wg_ref, wu_ref, *rest):
        o_ref, acc = rest[len(deps) :]
        p = lax.dot_general(dg_ref[...], wg_ref[...], _DIMS["nt"], preferred_element_type=F32)
        p = p + lax.dot_general(du_ref[...], wu_ref[...], _DIMS["nt"], preferred_element_type=F32)
        kk = pl.program_id(2)

        @pl.when(kk == 0)
        def _():
            acc[...] = p

        @pl.when(kk > 0)
        def _():
            acc[...] += p

        @pl.when(kk == nk - 1)
        def _():
            o_ref[...] = acc[...]

    a_spec = pl.BlockSpec((tm, FFN_TF), lambda i, j, kk: (i, kk))
    return pl.pallas_call(
        body,
        name=name,
        grid=(s // tm, D_MODEL // tn, nk),
        in_specs=[a_spec, a_spec,
                  pl.BlockSpec((tn, FFN_TF), lambda i, j, kk: (j, kk)),
                  pl.BlockSpec((tn, FFN_TF), lambda i, j, kk: (j, nk + kk))] + [ANY_SPEC] * len(deps),
        out_specs=pl.BlockSpec((tm, tn), lambda i, j, kk: (i, j)),
        out_shape=jax.ShapeDtypeStruct((s, D_MODEL), F32),
        scratch_shapes=[pltpu.VMEM((tm, tn), F32)],
        compiler_params=_cparams(dimension_semantics=("parallel", "parallel", "arbitrary")),
    )(dg, du, w_gu, w_gu, *deps)


def _lru_diag_blocks(dw_bd, name):
    def body(w_ref, o_ref):
        for half in range(2):
            for n in range(LRU_BLOCKS):
                rows = slice(n * LRU_BLOCK, (n + 1) * LRU_BLOCK)
                cols = slice(half * LRU_WIDTH + n * LRU_BLOCK, half * LRU_WIDTH + (n + 1) * LRU_BLOCK)
                o_ref[half, rows, :] = w_ref[rows, cols]

    return pl.pallas_call(
        body, name=name, out_shape=jax.ShapeDtypeStruct((2, LRU_WIDTH, LRU_BLOCK), F32), compiler_params=_cparams()
    )(dw_bd)


def _swiglu_fwd(gu, name):
    (a,) = _rowwise(_swiglu_fn, name, [], [gu], [(D_FF, BF16)], ts=128)
    return a


def _swiglu_bwd(gu, da, name, deps=()):
    def fn(gu, da):
        _, vjp = jax.vjp(_swiglu_fn, gu)
        return vjp(da)[0]

    (dgu,) = _rowwise(fn, name, [], [gu, da], [(2 * D_FF, BF16)], ts=128, deps=deps)
    return dgu


def _shift_down(ext, j, rows):
    return pltpu.roll(ext, j, 0)[V7X_SUBLANES : V7X_SUBLANES + rows]


def _shift_up(ext, j, rows):
    return pltpu.roll(ext, ext.shape[0] - j, 0)[:rows] if j else ext[:rows]


def _conv_fwd(proj, w8, b, name):
    def fn(i, steps, w8, b, x, halo):
        halo = jnp.where(i > 0, halo, 0.0)
        ext = jnp.concatenate([halo, x], axis=0)
        acc = b + w8[3:4] * x
        for j in (1, 2, 3):
            acc = acc + w8[3 - j : 4 - j] * _shift_down(ext, j, x.shape[0])
        return acc, acc

    tiles = [(proj, LRU_WIDTH, 0), (proj, LRU_WIDTH, 0, "prev")]
    return _rowwise(fn, name, [w8, b], tiles, [(LRU_WIDTH, F32), (LRU_WIDTH, BF16)], with_index=True)


def _conv_bwd(proj, w8, d1, d2, name):
    def fn(i, steps, w8, x, halo, d1, d1n, d2, d2n):
        rows = x.shape[0]
        d = d1 + d2
        dn = jnp.where(i < steps - 1, d1n + d2n, 0.0)
        halo = jnp.where(i > 0, halo, 0.0)
        dext = jnp.concatenate([d, dn], axis=0)
        xext = jnp.concatenate([halo, x], axis=0)
        dx = w8[3:4] * d
        dw = [None] * 4
        dw[3] = _rowsum(d * x)
        for k in (1, 2, 3):
            dx = dx + w8[3 - k : 4 - k] * _shift_up(dext, k, rows)
            dw[3 - k] = _rowsum(d * _shift_down(xext, k, rows))
        dw8 = jnp.concatenate(dw + [jnp.zeros((4, LRU_WIDTH), F32)], axis=0)
        return dx, dw8, _rowsum(d)

    tiles = [(proj, LRU_WIDTH, 0), (proj, LRU_WIDTH, 0, "prev"), d1, (d1, LRU_WIDTH, 0, "next"),
             d2, (d2, LRU_WIDTH, 0, "next")]
    return _rowwise(fn, name, [w8], tiles, [(LRU_WIDTH, BF16)], [(8, LRU_WIDTH), (1, LRU_WIDTH)], with_index=True)


def _gates_fwd(pre, xc, ba, bx, lam, name):
    return _rowwise(_gates_fn, name, [ba, bx, lam], [pre, xc], [(LRU_WIDTH, F32), (LRU_WIDTH, F32)])


def _gates_bwd(pre, xc, ba, bx, lam, g, h_prev, name):
    def fn(ba, bx, lam, pre, xc, g, h_prev):
        _, vjp = jax.vjp(_gates_fn, ba, bx, lam, pre, xc)
        dba, dbx, dlam, dpre, dxc = vjp((g * h_prev, g))
        return dpre, dxc, dba, dbx, dlam

    row = (1, LRU_WIDTH)
    return _rowwise(fn, name, [ba, bx, lam], [pre, xc, g, h_prev],
                    [(2 * LRU_WIDTH, BF16), (LRU_WIDTH, F32)], [row, row, row])


SCAN_ROWS = 512


def _block_scan(a, b, row, reverse):
    for d in (1, 2, 4):
        if reverse:
            shift, keep = V7X_SUBLANES - d, row < V7X_SUBLANES - d
        else:
            shift, keep = d, row >= d
        a_s = pltpu.roll(a, shift, 0)
        b_s = pltpu.roll(b, shift, 0)
        b = jnp.where(keep, a * b_s + b, b)
        a = jnp.where(keep, a * a_s, a)
    return a, b


def _scan_fwd(a, u, name):
    s, w = a.shape
    ts = min(SCAN_ROWS, s)
    sub = ts // V7X_SUBLANES

    def body(a_ref, u_ref, h_ref, hp_ref, carry):
        @pl.when(pl.program_id(0) == 0)
        def _():
            carry[...] = jnp.zeros_like(carry)

        row = lax.broadcasted_iota(jnp.int32, (V7X_SUBLANES, w), 0)

        def step(j, c):
            rows = pl.ds(pl.multiple_of(j * V7X_SUBLANES, V7X_SUBLANES), V7X_SUBLANES)
            pa, pb = _block_scan(a_ref[rows, :], u_ref[rows, :], row, False)
            h = pb + pa * c
            h_ref[rows, :] = h
            hp_ref[rows, :] = jnp.where(row >= 1, pltpu.roll(h, 1, 0), c)
            return jnp.broadcast_to(h[V7X_SUBLANES - 1 :], (V7X_SUBLANES, w))

        carry[...] = lax.fori_loop(0, sub, step, carry[...])

    spec = pl.BlockSpec((ts, w), lambda i: (i, 0))
    return pl.pallas_call(
        body,
        name=name,
        grid=(s // ts,),
        in_specs=[spec, spec],
        out_specs=[spec, spec],
        out_shape=[jax.ShapeDtypeStruct((s, w), F32)] * 2,
        scratch_shapes=[pltpu.VMEM((V7X_SUBLANES, w), F32)],
        compiler_params=_cparams(dimension_semantics=("arbitrary",)),
    )(a, u)


def _scan_bwd(a, dh, name):
    s, w = a.shape
    ts = min(SCAN_ROWS, s)
    sub = ts // V7X_SUBLANES
    steps = s // ts

    def body(a_ref, d_ref, g_ref, carry):
        @pl.when(pl.program_id(0) == 0)
        def _():
            carry[...] = jnp.zeros_like(carry)

        row = lax.broadcasted_iota(jnp.int32, (V7X_SUBLANES, w), 0)

        def step(jj, c):
            j = sub - 1 - jj
            rows = pl.ds(pl.multiple_of(j * V7X_SUBLANES, V7X_SUBLANES), V7X_SUBLANES)
            av, dv = a_ref[rows, :], d_ref[rows, :]
            pa, pb = _block_scan(av, av * dv, row, True)
            big = pb + pa * c
            g_ref[rows, :] = dv + jnp.where(row < V7X_SUBLANES - 1, pltpu.roll(big, V7X_SUBLANES - 1, 0), c)
            return jnp.broadcast_to(big[:1], (V7X_SUBLANES, w))

        carry[...] = lax.fori_loop(0, sub, step, carry[...])

    spec = pl.BlockSpec((ts, w), lambda i: (steps - 1 - i, 0))
    return pl.pallas_call(
        body,
        name=name,
        grid=(steps,),
        in_specs=[spec, spec],
        out_specs=spec,
        out_shape=jax.ShapeDtypeStruct((s, w), F32),
        scratch_shapes=[pltpu.VMEM((V7X_SUBLANES, w), F32)],
        compiler_params=_cparams(dimension_semantics=("arbitrary",)),
    )(a, dh)


def _rel_index():
    i = np.arange(ATT_TQ)[:, None]
    j = np.arange(3 * ATT_TQ)[None, :]
    band = (j // CHUNK >= i // CHUNK) & (j // CHUNK <= i // CHUNK + LEFT_CHUNKS)
    return band


SKEW = 4 * ATT_TQ


def _skew_onehot():
    t = np.arange(SKEW)
    diag = np.where(t < 3 * ATT_TQ, -t, SKEW - t)
    idx = np.clip(diag + LEFT_CHUNKS * CHUNK, -MAX_REL, MAX_REL) + MAX_REL
    hit = (idx[:, None] == np.arange(2 * MAX_REL + 1)[None, :]) & (t[:, None] != 3 * ATT_TQ)
    return hit.astype(np.float32)


def _bias_tile(rel_bias):
    per_t = jnp.dot(rel_bias, jnp.asarray(_skew_onehot()).T, precision=lax.Precision.HIGHEST)
    flat = jnp.broadcast_to(per_t[:, None, :], (ATT_HEADS, ATT_TQ, SKEW)).reshape(ATT_HEADS, ATT_TQ * SKEW)
    tile = flat[:, : ATT_TQ * (SKEW - 1)].reshape(ATT_HEADS, ATT_TQ, SKEW - 1)[:, :, : 3 * ATT_TQ]
    return jnp.where(jnp.asarray(_rel_index())[None], tile, NEG)


def _bias_grad(dbias):
    flat = jnp.pad(dbias, ((0, 0), (0, 0), (0, SKEW - 1 - 3 * ATT_TQ))).reshape(ATT_HEADS, ATT_TQ * (SKEW - 1))
    per_t = jnp.sum(jnp.pad(flat, ((0, 0), (0, ATT_TQ))).reshape(ATT_HEADS, ATT_TQ, SKEW), axis=1)
    return jnp.dot(per_t, jnp.asarray(_skew_onehot()), precision=lax.Precision.HIGHEST)


def _attn_specs(nt):
    qb, kb, vb = OFF_Q // V7X_LANES, OFF_K // V7X_LANES, OFF_V // V7X_LANES
    blk = (ATT_TQ, V7X_LANES)

    def qmap(base):
        return lambda hp, m: (jnp.minimum(m, nt - 1), base + hp)

    def wmap(base, back):
        return lambda hp, m: (jnp.clip(m - back, 0, nt - 1), base + hp)

    specs = [pl.BlockSpec(blk, qmap(qb))]
    specs += [pl.BlockSpec(blk, wmap(kb, back)) for back in (2, 1, 0)]
    specs += [pl.BlockSpec(blk, wmap(vb, back)) for back in (2, 1, 0)]
    return specs


def _attn_probs(qh, kh, bias, ok):
    s = lax.dot_general(qh, kh, _DIMS["nt"], preferred_element_type=F32) * (ATT_HEAD_DIM**-0.5) + bias
    s = jnp.where(ok, s, NEG)
    e = jnp.exp(s - jnp.max(s, axis=-1, keepdims=True))
    return e / jnp.sum(e, axis=-1, keepdims=True)


def _attn_window(m, k0, k1, k2, v0, v1, v2):
    k = jnp.concatenate([k0[...], k1[...], k2[...]], axis=0).astype(BF16)
    v = jnp.concatenate([v0[...], v1[...], v2[...]], axis=0).astype(BF16)
    kpos = lax.broadcasted_iota(jnp.int32, (ATT_TQ, 3 * ATT_TQ), 1) + (m - 2) * ATT_TQ
    return k, v, kpos >= 0


def _attn_fwd(proj, bias, name):
    s = proj.shape[0]
    nt = s // ATT_TQ

    def body(q_ref, k0, k1, k2, v0, v1, v2, b_ref, o_ref):
        m = pl.program_id(1)
        k, v, ok = _attn_window(m, k0, k1, k2, v0, v1, v2)
        q = q_ref[...].astype(BF16)
        for hh in range(2):
            cols = slice(hh * ATT_HEAD_DIM, (hh + 1) * ATT_HEAD_DIM)
            p = _attn_probs(q[:, cols], k[:, cols], b_ref[hh], ok)
            o = jnp.dot(p.astype(BF16), v[:, cols], preferred_element_type=F32)
            o_ref[:, cols] = o.astype(o_ref.dtype)

    specs = _attn_specs(nt) + [pl.BlockSpec((2, ATT_TQ, 3 * ATT_TQ), lambda hp, m: (hp, 0, 0))]
    return pl.pallas_call(
        body,
        name=name,
        grid=(ATT_HEADS // 2, nt),
        in_specs=specs,
        out_specs=pl.BlockSpec((ATT_TQ, V7X_LANES), lambda hp, m: (m, hp)),
        out_shape=jax.ShapeDtypeStruct((s, ATT_WIDTH), BF16),
        compiler_params=_cparams(dimension_semantics=("parallel", "arbitrary")),
    )(proj, proj, proj, proj, proj, proj, proj, bias)


def _attn_bwd(proj, bias, do, name):
    s = proj.shape[0]
    nt = s // ATT_TQ
    win = 3 * ATT_TQ

    def body(q_ref, k0, k1, k2, v0, v1, v2, do_ref, b_ref, dq_ref, dk_ref, dv_ref, db_ref, dk_acc, dv_acc):
        m = pl.program_id(1)

        @pl.when(m == 0)
        def _():
            dk_acc[...] = jnp.zeros_like(dk_acc)
            dv_acc[...] = jnp.zeros_like(dv_acc)
            db_ref[...] = jnp.zeros_like(db_ref)

        @pl.when(m < nt)
        def _():
            k, v, ok = _attn_window(m, k0, k1, k2, v0, v1, v2)
            q = q_ref[...].astype(BF16)
            dout = do_ref[...]
            for hh in range(2):
                cols = slice(hh * ATT_HEAD_DIM, (hh + 1) * ATT_HEAD_DIM)
                qh, kh, vh, doh = q[:, cols], k[:, cols], v[:, cols], dout[:, cols]
                p = _attn_probs(qh, kh, b_ref[hh], ok)
                dvh = lax.dot_general(p.astype(BF16), doh, _DIMS["tn"], preferred_element_type=F32)
                dp = lax.dot_general(doh, vh, _DIMS["nt"], preferred_element_type=F32)
                ds = p * (dp - jnp.sum(dp * p, axis=-1, keepdims=True))
                db_ref[hh] += ds
                dsb = ds.astype(BF16)
                dqh = jnp.dot(dsb, kh, preferred_element_type=F32) * (ATT_HEAD_DIM**-0.5)
                dkh = lax.dot_general(dsb, qh, _DIMS["tn"], preferred_element_type=F32) * (ATT_HEAD_DIM**-0.5)
                dq_ref[:, cols] = dqh.astype(dq_ref.dtype)
                dk_acc[:, cols] += dkh
                dv_acc[:, cols] += dvh

        dk_ref[...] = dk_acc[:ATT_TQ].astype(dk_ref.dtype)
        dv_ref[...] = dv_acc[:ATT_TQ].astype(dv_ref.dtype)
        for acc in (dk_acc, dv_acc):
            rest = acc[ATT_TQ:]
            acc[: win - ATT_TQ] = rest
            acc[win - ATT_TQ :] = jnp.zeros((ATT_TQ, V7X_LANES), F32)

    blk = (ATT_TQ, V7X_LANES)
    specs = _attn_specs(nt)
    specs.append(pl.BlockSpec(blk, lambda hp, m: (jnp.minimum(m, nt - 1), hp)))
    specs.append(pl.BlockSpec((2, ATT_TQ, win), lambda hp, m: (hp, 0, 0)))
    done = lambda hp, m: (jnp.maximum(m - 2, 0), hp)
    out_specs = [
        pl.BlockSpec(blk, lambda hp, m: (jnp.minimum(m, nt - 1), hp)),
        pl.BlockSpec(blk, done),
        pl.BlockSpec(blk, done),
        pl.BlockSpec((2, ATT_TQ, win), lambda hp, m: (hp, 0, 0)),
    ]
    out_shape = [jax.ShapeDtypeStruct((s, ATT_WIDTH), BF16)] * 3
    out_shape.append(jax.ShapeDtypeStruct((ATT_HEADS, ATT_TQ, win), F32))
    return pl.pallas_call(
        body,
        name=name,
        grid=(ATT_HEADS // 2, nt + 2),
        in_specs=specs,
        out_specs=out_specs,
        out_shape=out_shape,
        scratch_shapes=[pltpu.VMEM((win, V7X_LANES), F32), pltpu.VMEM((win, V7X_LANES), F32)],
        compiler_params=_cparams(dimension_semantics=("arbitrary", "arbitrary")),
    )(proj, proj, proj, proj, proj, proj, proj, do, bias)


def _ada_fwd(c_all, w, name):
    def body(c_ref, w_ref, o_ref):
        act = _silu(c_ref[...]).astype(BF16)
        o_ref[...] = jnp.dot(act, w_ref[...].astype(BF16), preferred_element_type=F32)

    return pl.pallas_call(
        body, name=name, out_shape=jax.ShapeDtypeStruct((c_all.shape[0], w.shape[1]), F32), compiler_params=_cparams()
    )(c_all, w)


def _ada_bwd(c_all, dmod, name):
    def body(c_ref, d_ref, o_ref):
        act = _silu(c_ref[...])
        o_ref[...] = lax.dot_general(act, d_ref[...], _DIMS["tn"], preferred_element_type=F32,
                                     precision=lax.Precision.HIGHEST)

    return pl.pallas_call(
        body, name=name, out_shape=jax.ShapeDtypeStruct((c_all.shape[1], dmod.shape[1]), F32), compiler_params=_cparams()
    )(c_all, dmod)


def _adamw_parts(landed, sent, me, w, m, v, name, rows=256):
    r, c = w.shape
    tr = _pick(r, rows, 16)

    def body(me_ref, g_ref, own_ref, w_ref, m_ref, v_ref, go_ref, d_ref, mo_ref, vo_ref):
        mine = me_ref[0]
        grad = jnp.zeros((tr, c), F32)
        for d in range(N_DEV):
            grad = grad + jnp.where(mine == d, own_ref[0], g_ref[d]).astype(F32)
        _adamw_update(grad, w_ref, m_ref, v_ref, go_ref, d_ref, mo_ref, vo_ref)

    spec = pl.BlockSpec((tr, c), lambda i, me_ref: (i, 0))
    return pl.pallas_call(
        body,
        name=name,
        grid_spec=pltpu.PrefetchScalarGridSpec(
            num_scalar_prefetch=1,
            grid=(r // tr,),
            in_specs=[pl.BlockSpec((N_DEV, tr, c), lambda i, me_ref: (0, i, 0)),
                      pl.BlockSpec((1, tr, c), lambda i, me_ref: (me_ref[0], i, 0)), spec, spec, spec],
            out_specs=[spec] * 4,
        ),
        out_shape=[jax.ShapeDtypeStruct((r, c), F32)] * 4,
        compiler_params=_cparams(dimension_semantics=("parallel",)),
    )(me.reshape(1).astype(jnp.int32), landed, sent, w, m, v)


def _adamw_update(grad, w_ref, m_ref, v_ref, go_ref, d_ref, mo_ref, vo_ref):
    m2 = ADAM_B1 * m_ref[...] + (1.0 - ADAM_B1) * grad
    v2 = ADAM_B2 * v_ref[...] + (1.0 - ADAM_B2) * (grad * grad)
    m_hat = m2 / (1.0 - ADAM_B1**ADAM_STEP)
    v_hat = v2 / (1.0 - ADAM_B2**ADAM_STEP)
    go_ref[...] = grad
    d_ref[...] = -ADAM_LR * (m_hat / (jnp.sqrt(v_hat) + ADAM_EPS) + ADAM_WD * w_ref[...])
    mo_ref[...] = m2
    vo_ref[...] = v2


def _adamw(g, w, m, v, name, rows=256):
    r, c = w.shape
    tr = _pick(r, rows, 16)

    def body(g_ref, w_ref, m_ref, v_ref, go_ref, d_ref, mo_ref, vo_ref):
        _adamw_update(g_ref[...], w_ref, m_ref, v_ref, go_ref, d_ref, mo_ref, vo_ref)

    spec = pl.BlockSpec((tr, c), lambda i: (i, 0))
    return pl.pallas_call(
        body,
        name=name,
        grid=(r // tr,),
        in_specs=[spec, spec, spec, spec],
        out_specs=[spec] * 4,
        out_shape=[jax.ShapeDtypeStruct((r, c), F32)] * 4,
        compiler_params=_cparams(dimension_semantics=("parallel",)),
    )(g, w, m, v)


def _sum_parts(parts, name):
    def body(p_ref, o_ref):
        acc = p_ref[0]
        for d in range(1, N_DEV):
            acc = acc + p_ref[d]
        o_ref[...] = acc

    return pl.pallas_call(
        body, name=name, out_shape=jax.ShapeDtypeStruct(parts.shape[1:], F32), compiler_params=_cparams()
    )(parts)


def _place():
    x, y, c = lax.axis_index("x"), lax.axis_index("y"), lax.axis_index("c")
    return x, y, c


def _dev_index(p):
    return 4 * p[0] + 2 * p[1] + p[2]


def _allgather_vmem(shard, name):
    m_per, n = shard.shape

    def body(x_ref, out_ref, send_sems, recv_sems, local_sem):
        x, y, c = _place()
        me, sibling = (x, y, c), (x, y, 1 - c)
        chips = [(1 - x, y), (x, 1 - y), (1 - x, 1 - y)]

        def rows(p):
            return out_ref.at[pl.ds(_dev_index(p) * m_per, m_per), :]

        def copy(k, block, to, src=None):
            return pltpu.make_async_remote_copy(
                src_ref=rows(block) if src is None else src, dst_ref=rows(block),
                send_sem=send_sems.at[k], recv_sem=recv_sems.at[k], device_id=to, device_id_type=MESH)

        mine = pltpu.make_async_copy(x_ref, rows(me), local_sem)
        mine.start()
        first = [copy(0, me, sibling, src=x_ref)]
        first += [copy(1 + j, me, (*chip, c), src=x_ref) for j, chip in enumerate(chips)]
        for cp in first:
            cp.start()
        passed = [copy(4 + j, (*chip, c), sibling) for j, chip in enumerate(chips)]
        for j, chip in enumerate(chips):
            copy(1 + j, (*chip, c), me).wait_recv()
            passed[j].start()
        copy(0, sibling, me).wait_recv()
        for j, chip in enumerate(chips):
            copy(4 + j, (*chip, 1 - c), me).wait_recv()
        for cp in first + passed:
            cp.wait_send()
        mine.wait()

    return pl.pallas_call(
        body,
        name=name,
        out_shape=jax.ShapeDtypeStruct((N_DEV * m_per, n), shard.dtype),
        in_specs=[pl.BlockSpec(memory_space=pltpu.VMEM)],
        out_specs=pl.BlockSpec(memory_space=pltpu.VMEM),
        scratch_shapes=[pltpu.SemaphoreType.DMA((7,)), pltpu.SemaphoreType.DMA((7,)), pltpu.SemaphoreType.DMA],
        compiler_params=_cparams(),
    )(shard)


def _allgather_hbm(shards, name):
    n = len(shards)

    def body(*refs):
        ins, outs = refs[:n], refs[n : 2 * n]
        send_sems, recv_sems, local_sems = refs[2 * n :]
        x, y, c = _place()
        me, sibling = (x, y, c), (x, y, 1 - c)
        chips = [(1 - x, y), (x, 1 - y), (1 - x, 1 - y)]

        def copy(a, k, block, to, src=None):
            dst = outs[a].at[_dev_index(block)]
            return pltpu.make_async_remote_copy(
                src_ref=dst if src is None else src, dst_ref=dst,
                send_sem=send_sems.at[a * 7 + k], recv_sem=recv_sems.at[a * 7 + k], device_id=to, device_id_type=MESH)

        mine = [pltpu.make_async_copy(ins[a], outs[a].at[_dev_index(me)], local_sems.at[a]) for a in range(n)]
        for cp in mine:
            cp.start()
        first = []
        for a in range(n):
            first.append(copy(a, 0, me, sibling, src=ins[a]))
            first += [copy(a, 1 + j, me, (*chip, c), src=ins[a]) for j, chip in enumerate(chips)]
        for cp in first:
            cp.start()
        passed = []
        for j, chip in enumerate(chips):
            for a in range(n):
                copy(a, 1 + j, (*chip, c), me).wait_recv()
                cp = copy(a, 4 + j, (*chip, c), sibling)
                cp.start()
                passed.append(cp)
        for a in range(n):
            copy(a, 0, sibling, me).wait_recv()
        for j, chip in enumerate(chips):
            for a in range(n):
                copy(a, 4 + j, (*chip, 1 - c), me).wait_recv()
        for cp in first + passed:
            cp.wait_send()
        for cp in mine:
            cp.wait()

    any_spec = pl.BlockSpec(memory_space=pl.ANY)
    return pl.pallas_call(
        body,
        name=name,
        out_shape=[jax.ShapeDtypeStruct((N_DEV, *s.shape), s.dtype) for s in shards],
        in_specs=[any_spec] * n,
        out_specs=[any_spec] * n,
        scratch_shapes=[pltpu.SemaphoreType.DMA((7 * n,)), pltpu.SemaphoreType.DMA((7 * n,)),
                        pltpu.SemaphoreType.DMA((n,))],
        compiler_params=_cparams(),
    )(*shards)


def _exchange_hbm(bufs, name):
    n = len(bufs)

    def body(*refs):
        ins, outs = refs[:n], refs[n : 2 * n]
        send_sems, recv_sems, local_sems = refs[2 * n :]
        x, y, c = _place()
        me = _dev_index((x, y, c))
        mine = [pltpu.make_async_copy(ins[a].at[me], outs[a].at[me], local_sems.at[a]) for a in range(n)]
        for cp in mine:
            cp.start()
        def peer_of(k):
            return (1 - x if k & 4 else x, 1 - y if k & 2 else y, 1 - c if k & 1 else c)

        copies = []
        for k in range(1, N_DEV):
            peer = peer_of(k)
            for a in range(n):
                copies.append(pltpu.make_async_remote_copy(
                    src_ref=ins[a].at[_dev_index(peer)], dst_ref=outs[a].at[me],
                    send_sem=send_sems.at[a * 7 + k - 1], recv_sem=recv_sems.at[a * 7 + k - 1],
                    device_id=peer, device_id_type=MESH))
        for cp in copies:
            cp.start()
        for k in range(1, N_DEV):
            peer = peer_of(k)
            for a in range(n):
                pltpu.make_async_remote_copy(
                    src_ref=ins[a].at[me], dst_ref=outs[a].at[_dev_index(peer)],
                    send_sem=send_sems.at[a * 7 + k - 1], recv_sem=recv_sems.at[a * 7 + k - 1],
                    device_id=peer, device_id_type=MESH).wait_recv()
        for cp in copies:
            cp.wait_send()
        for cp in mine:
            cp.wait()

    any_spec = pl.BlockSpec(memory_space=pl.ANY)
    return pl.pallas_call(
        body,
        name=name,
        out_shape=[jax.ShapeDtypeStruct(b.shape, b.dtype) for b in bufs],
        in_specs=[any_spec] * n,
        out_specs=[any_spec] * n,
        scratch_shapes=[pltpu.SemaphoreType.DMA((7 * n,)), pltpu.SemaphoreType.DMA((7 * n,)),
                        pltpu.SemaphoreType.DMA((n,))],
        compiler_params=_cparams(),
    )(*bufs)


HBM_SPEC = pl.BlockSpec(memory_space=pltpu.HBM)
SEM_SPEC = pl.BlockSpec(memory_space=pltpu.SEMAPHORE)
EFFECT = pltpu.SideEffectType.DATAFLOW_SIDE_EFFECTING


def _peers(x, y, c):
    return [(1 - x if k & 4 else x, 1 - y if k & 2 else y, 1 - c if k & 1 else c) for k in range(1, N_DEV)]


def _push_start(groups, sliced, name, after=()):
    flat = [b for g in groups for b in g]
    n, ng = len(flat), len(groups)
    sizes = [len(g) for g in groups]
    lands = [lax.empty(b.shape if sliced else (N_DEV, *b.shape), b.dtype) for b in flat]

    def body(*refs):
        ins, lnd = refs[:n], refs[n : 2 * n]
        sems = refs[2 * n + len(after) : 2 * n + len(after) + 2 * ng]
        token = refs[-1]
        x, y, c = _place()
        me = _dev_index((x, y, c))
        first = 0
        for gi, size in enumerate(sizes):
            for k, peer in enumerate(_peers(x, y, c)):
                for j in range(first, first + size):
                    sem = (j - first) * 7 + k
                    pltpu.make_async_remote_copy(
                        src_ref=ins[j].at[_dev_index(peer)] if sliced else ins[j], dst_ref=lnd[j].at[me],
                        send_sem=sems[2 * gi].at[sem], recv_sem=sems[2 * gi + 1].at[sem],
                        device_id=peer, device_id_type=MESH).start()
            first += size
        token[...] = jnp.zeros_like(token)

    out_shape = []
    for size in sizes:
        out_shape += [pltpu.SemaphoreType.DMA((7 * size,)), pltpu.SemaphoreType.DMA((7 * size,))]
    out_shape += [pltpu.HBM(b.shape, b.dtype) for b in flat + lands]
    out_shape.append(jax.ShapeDtypeStruct((V7X_SUBLANES, V7X_LANES), F32))
    res = pl.pallas_call(
        body,
        name=name,
        out_shape=tuple(out_shape),
        in_specs=[HBM_SPEC] * (2 * n) + [ANY_SPEC] * len(after),
        out_specs=tuple([SEM_SPEC] * (2 * ng) + [HBM_SPEC] * (2 * n) + [pl.BlockSpec(memory_space=pltpu.VMEM)]),
        input_output_aliases={i: 2 * ng + i for i in range(2 * n)},
        compiler_params=pltpu.CompilerParams(has_side_effects=EFFECT),
    )(*[pltpu.with_memory_space_constraint(b, pltpu.HBM) for b in flat + lands], *after)
    sems, thru, token = res[: 2 * ng], res[2 * ng : 2 * ng + 2 * n], res[-1]
    out, first = [], 0
    for gi, size in enumerate(sizes):
        out.append((sems[2 * gi], sems[2 * gi + 1], list(thru[first : first + size]),
                    list(thru[n + first : n + first + size])))
        first += size
    return out, token


def _push_wait(started, sliced, after, name):
    send_sems, recv_sems, bufs, lands = started
    n = len(bufs)

    def body(*refs):
        ins, lnd = refs[:n], refs[n : 2 * n]
        send_ref, recv_ref = refs[2 * n], refs[2 * n + 1]
        x, y, c = _place()
        for k, peer in enumerate(_peers(x, y, c)):
            for j in range(n):
                cp = pltpu.make_async_remote_copy(
                    src_ref=ins[j].at[_dev_index(peer)] if sliced else ins[j], dst_ref=lnd[j].at[_dev_index(peer)],
                    send_sem=send_ref.at[j * 7 + k], recv_sem=recv_ref.at[j * 7 + k],
                    device_id=peer, device_id_type=MESH)
                cp.wait_send()
                cp.wait_recv()

    res = pl.pallas_call(
        body,
        name=name,
        out_shape=tuple(pltpu.HBM(b.shape, b.dtype) for b in bufs + lands),
        in_specs=[HBM_SPEC] * (2 * n) + [SEM_SPEC, SEM_SPEC, pl.BlockSpec(memory_space=pl.ANY)],
        out_specs=tuple([HBM_SPEC] * (2 * n)),
        input_output_aliases={i: i for i in range(2 * n)},
        compiler_params=pltpu.CompilerParams(has_side_effects=EFFECT),
    )(*bufs, *lands, send_sems, recv_sems, after)
    return list(res[n:])


def _with_own_row(land, own, me):
    return lax.dynamic_update_index_in_dim(land, own, me, 0)


def _cols_full(g):
    return jnp.transpose(g, (1, 0, 2)).reshape(g.shape[1], -1)


def _rows_full(g):
    return g.reshape(-1, g.shape[2])


def _cols_parts(full, n=N_DEV):
    r = full.shape[0]
    return jnp.transpose(full.reshape(r, n, -1), (1, 0, 2)).astype(BF16)


def _rows_parts(full):
    return full.reshape(N_DEV, -1, full.shape[1]).astype(BF16)


def _block_diag(w):
    eye = jnp.eye(LRU_BLOCKS, dtype=w.dtype)
    return jnp.einsum("nkj,nm->nkmj", w, eye).reshape(LRU_WIDTH, LRU_WIDTH)


def _pad_rows(v, rows):
    flat = v.reshape(-1)
    return jnp.pad(flat, (0, rows * D_MODEL - flat.shape[0])).reshape(rows, D_MODEL)


def _my_cols(full, me, width):
    return lax.dynamic_slice_in_dim(full, me * width, width, axis=full.ndim - 1)


def kernel(x, c, w_ada, b_ada, norm_pre, norm_post, ffn1_w_gu, ffn1_w_down, w_in, rel_bias, conv_w, conv_b, lru_wa, lru_ba, lru_wx, lru_bx, lru_lambda, w_att_o, w_rec_o, w_out, ffn2_w_gu, ffn2_w_down, loss_target, m_w_ada, m_b_ada, m_norm_pre, m_norm_post, m_ffn1_w_gu, m_ffn1_w_down, m_w_in, m_rel_bias, m_conv_w, m_conv_b, m_lru_wa, m_lru_ba, m_lru_wx, m_lru_bx, m_lru_lambda, m_w_att_o, m_w_rec_o, m_w_out, m_ffn2_w_gu, m_ffn2_w_down, v_w_ada, v_b_ada, v_norm_pre, v_norm_post, v_ffn1_w_gu, v_ffn1_w_down, v_w_in, v_rel_bias, v_conv_w, v_conv_b, v_lru_wa, v_lru_ba, v_lru_wx, v_lru_bx, v_lru_lambda, v_w_att_o, v_w_rec_o, v_w_out, v_ffn2_w_gu, v_ffn2_w_down):
    weights = dict(w_ada=w_ada, b_ada=b_ada, norm_pre=norm_pre, norm_post=norm_post, ffn1_w_gu=ffn1_w_gu,
                   ffn1_w_down=ffn1_w_down, w_in=w_in, rel_bias=rel_bias, conv_w=conv_w, conv_b=conv_b,
                   lru_wa=lru_wa, lru_ba=lru_ba, lru_wx=lru_wx, lru_bx=lru_bx, lru_lambda=lru_lambda,
                   w_att_o=w_att_o, w_rec_o=w_rec_o, w_out=w_out, ffn2_w_gu=ffn2_w_gu, ffn2_w_down=ffn2_w_down)
    mom1 = dict(w_ada=m_w_ada, b_ada=m_b_ada, norm_pre=m_norm_pre, norm_post=m_norm_post, ffn1_w_gu=m_ffn1_w_gu,
                ffn1_w_down=m_ffn1_w_down, w_in=m_w_in, rel_bias=m_rel_bias, conv_w=m_conv_w, conv_b=m_conv_b,
                lru_wa=m_lru_wa, lru_ba=m_lru_ba, lru_wx=m_lru_wx, lru_bx=m_lru_bx, lru_lambda=m_lru_lambda,
                w_att_o=m_w_att_o, w_rec_o=m_w_rec_o, w_out=m_w_out, ffn2_w_gu=m_ffn2_w_gu, ffn2_w_down=m_ffn2_w_down)
    mom2 = dict(w_ada=v_w_ada, b_ada=v_b_ada, norm_pre=v_norm_pre, norm_post=v_norm_post, ffn1_w_gu=v_ffn1_w_gu,
                ffn1_w_down=v_ffn1_w_down, w_in=v_w_in, rel_bias=v_rel_bias, conv_w=v_conv_w, conv_b=v_conv_b,
                lru_wa=v_lru_wa, lru_ba=v_lru_ba, lru_wx=v_lru_wx, lru_bx=v_lru_bx, lru_lambda=v_lru_lambda,
                w_att_o=v_w_att_o, w_rec_o=v_w_rec_o, w_out=v_w_out, ffn2_w_gu=v_ffn2_w_gu, ffn2_w_down=v_ffn2_w_down)
    order = list(weights)
    big = ["ffn1_w_gu", "ffn1_w_down", "w_in", "w_att_o", "w_rec_o", "w_out", "ffn2_w_gu", "ffn2_w_down"]
    col_sharded = {"ffn1_w_gu", "w_in", "w_att_o", "ffn2_w_gu"}
    small = ["b_ada", "norm_pre", "norm_post", "rel_bias", "conv_w", "conv_b", "lru_wa", "lru_ba", "lru_wx",
             "lru_bx", "lru_lambda"]

    xi, yi, ci = _place()
    me = _dev_index((xi, yi, ci))
    x0 = x[0]
    target = loss_target[0]

    shards = {n: weights[n][0].astype(BF16) for n in big}
    full_of = lambda n, g: _cols_full(g) if n in col_sharded else _rows_full(g)

    pack = jnp.concatenate([c.reshape(-1), norm_pre.reshape(-1), norm_post.reshape(-1), conv_w.reshape(-1)])
    pack = jnp.pad(pack, (0, 3072 - pack.shape[0])).reshape(8, 384)
    got = _allgather_vmem(pack, "gather_small_inputs").reshape(N_DEV, 3072)
    c_all = got[:, :1024]
    unshard = lambda blk, rows: jnp.transpose(blk.reshape(N_DEV, rows, 128), (1, 0, 2)).reshape(rows, D_MODEL)
    g_pre = unshard(got[:, 1024:1408], 3)
    g_post = unshard(got[:, 1408:1792], 3)
    conv_taps = unshard(got[:, 1792:2304], 4)
    conv_w8 = jnp.concatenate([conv_taps, jnp.zeros((4, LRU_WIDTH), F32)], axis=0)

    mod_cols = _ada_fwd(c_all, w_ada[0], "ada_fwd")
    mod_all = _allgather_vmem(mod_cols, "gather_mod").reshape(N_DEV, N_DEV, 1152)
    mod = lax.dynamic_index_in_dim(mod_all, me, axis=1, keepdims=False).reshape(1, -1) + b_ada
    mod = mod.reshape(3, 3, 1, D_MODEL)

    w_bd = jnp.concatenate([_block_diag(lru_wa[0]), _block_diag(lru_wx[0])], axis=1).astype(BF16)
    bias = _bias_tile(rel_bias[0])

    res_w = (0.5, 1.0, 0.5)
    row = lambda v: v.reshape(1, -1)

    (w1_gu,) = _allgather_hbm([shards["ffn1_w_gu"]], "gather_ffn1_w_gu")
    weight_groups = [["ffn1_w_down"], ["w_in", "w_att_o", "w_rec_o", "w_out"], ["ffn2_w_gu", "ffn2_w_down"]]
    weights_started, started = _push_start([[shards[n] for n in g] for g in weight_groups], False,
                                           "gather_weights_start", after=(mod, w1_gu))
    full = {"ffn1_w_gu": _cols_full(w1_gu)}

    def gathered_group(gi, after):
        lands = _push_wait(weights_started[gi], False, after, f"gather_weights_wait{gi}")
        for n, land in zip(weight_groups[gi], lands):
            full[n] = full_of(n, jnp.where(is_me, shards[n][None], land))

    is_me = (jnp.arange(N_DEV) == me)[:, None, None]

    def ffn_fwd(xin, k, gi, tag, deps=()):
        h = _pre_fwd(xin, row(g_pre[k]), mod[k, 0], mod[k, 1], f"{tag}_pre", deps=deps)
        a, g, u = _ffn_up(h, full[f"{tag}_w_gu"], f"{tag}_up")
        if f"{tag}_w_down" not in full:
            gathered_group(gi, a)
        f = _matmul(a, full[f"{tag}_w_down"], "nn", F32, f"{tag}_down")
        xout = _post_fwd(f, xin, row(g_post[k]), mod[k, 2], res_w[k], f"{tag}_post")
        return xout, (h, g, u, a, f)

    x1, saved1 = ffn_fwd(x0, 0, 0, "ffn1", deps=(started,))

    gathered_group(1, x1)
    w_in_p = jnp.concatenate([full["w_in"][:, 3 * ATT_WIDTH :], full["w_in"][:, : 3 * ATT_WIDTH]], axis=1)
    h2 = _pre_fwd(x1, row(g_pre[1]), mod[1, 0], mod[1, 1], "mix_pre")
    proj = _matmul(h2, w_in_p, "nn", F32, "mix_in")
    att_o = _attn_fwd(proj, bias, "attn_fwd")
    xc, xcb = _conv_fwd(proj, conv_w8, conv_b, "conv_fwd")
    pre = _matmul(xcb, w_bd, "nn", F32, "lru_gate_proj")
    a_t, u_t = _gates_fwd(pre, xc, lru_ba, lru_bx, lru_lambda, "lru_gates")
    hs, h_prev = _scan_fwd(a_t, u_t, "lru_scan")
    (rec_in,) = _rowwise(_recin_fn, "rec_in", [], [hs, (proj, LRU_WIDTH, 1)], [(LRU_WIDTH, BF16)])
    att = _matmul(att_o, full["w_att_o"], "nn", F32, "att_out")
    rec = _matmul(rec_in, full["w_rec_o"], "nn", F32, "rec_out")
    (merged,) = _rowwise(_merge_fn, "merge", [], [att, rec, (proj, LRU_WIDTH, 2), (proj, LRU_WIDTH, 3)],
                         [(D_MODEL, BF16)])
    f2 = _matmul(merged, full["w_out"], "nn", F32, "mix_out")
    x2 = _post_fwd(f2, x1, row(g_post[1]), mod[1, 2], res_w[1], "mix_post")

    gathered_group(2, x2)
    x3, saved3 = ffn_fwd(x2, 2, 2, "ffn2")

    dy, sq = _loss_stage(x3, target, "loss")
    loss = lax.psum(0.5 * jnp.sum(sq) / D_MODEL, ("x", "y", "c"))

    grads = {}
    dmod = [[None] * 3 for _ in range(3)]
    d_pre, d_post = [None] * 3, [None] * 3

    pending = []

    def exchange_start(names, tag):
        send = [grads[n] if grads[n].ndim == 3 else (_cols_parts if n in col_sharded else _rows_parts)(grads[n])
                for n in names]
        (group,), token = _push_start([send], True, f"exchange_{tag}_start")
        pending.append((names, send, group, tag))
        return token

    def exchange_finish(names, send, group, tag, after):
        lands = _push_wait(group, True, after, f"exchange_{tag}_wait")
        res = None
        for n, land, mine in zip(names, lands, send):
            res = _adamw_parts(land, mine, me, weights[n][0], mom1[n][0], mom2[n][0], f"adamw_{n}")
            out_g[n], out_d[n], out_m[n], out_v[n] = [r.reshape(weights[n].shape) for r in res]
        return res[0]

    out_g, out_d, out_m, out_v = {}, {}, {}, {}

    def ffn_bwd(xin, k, saved, dout, tag):
        h, g, u, a, f = saved
        w_gu, w_down = f"{tag}_w_gu", f"{tag}_w_down"
        df, d_post[k], dmod[k][2] = _post_bwd(f, row(g_post[k]), mod[k, 2], res_w[k], dout, f"{tag}_post_bwd")
        grads[w_down] = _matmul(a, df, "tn", BF16, f"{tag}_dw_down", tm=1408, tn=1024, tk=1024)
        started = exchange_start([w_down], w_down)
        dg, du = _ffn_up_bwd(df, full[w_down], g, u, f"{tag}_up_bwd", deps=(started,))
        dw_g = _matmul(h, dg, "tn", BF16, f"{tag}_dw_g", tm=1024, tn=1408, tk=1024)
        dw_u = _matmul(h, du, "tn", BF16, f"{tag}_dw_u", tm=1024, tn=1408, tk=1024)
        grads[w_gu] = jnp.concatenate([_cols_parts(dw_g, N_DEV // 2), _cols_parts(dw_u, N_DEV // 2)], axis=0)
        started = exchange_start([w_gu], w_gu)
        dh = _ffn_dh(dg, du, full[w_gu], f"{tag}_dh", deps=(started,))
        dx, d_pre[k], dmod[k][0], dmod[k][1] = _pre_bwd(xin, row(g_pre[k]), mod[k, 0], mod[k, 1], dh, dout,
                                                         f"{tag}_pre_bwd")
        return dx

    dx2 = ffn_bwd(x2, 2, saved3, dy, "ffn2")

    df2, d_post[1], dmod[1][2] = _post_bwd(f2, row(g_post[1]), mod[1, 2], res_w[1], dx2, "mix_post_bwd")
    dmerged = _matmul(df2, full["w_out"], "nt", F32, "mix_dmerged")
    grads["w_out"] = _matmul(merged, df2, "tn", BF16, "mix_dw_out", tm=1024, tn=1024, tk=1024)

    def merge_bwd(att, rec, g_att, g_rec, dm):
        _, vjp = jax.vjp(_merge_fn, att, rec, g_att, g_rec)
        return vjp(dm)

    datt, drec, dg_att, dg_rec = _rowwise(
        merge_bwd, "merge_bwd", [], [att, rec, (proj, LRU_WIDTH, 2), (proj, LRU_WIDTH, 3), dmerged],
        [(D_MODEL, BF16)] * 4)
    datt_o = _matmul(datt, full["w_att_o"], "nt", BF16, "att_out_bwd")
    grads["w_att_o"] = _matmul(att_o, datt, "tn", BF16, "dw_att_o", tm=512, tn=1024, tk=1024)
    drec_in = _matmul(drec, full["w_rec_o"], "nt", F32, "rec_out_bwd")
    grads["w_rec_o"] = _matmul(rec_in, drec, "tn", BF16, "dw_rec_o", tm=1024, tn=1024, tk=1024)
    started = exchange_start(["w_out", "w_att_o", "w_rec_o"], "mix_out")

    def recin_bwd(hs, yr, d):
        _, vjp = jax.vjp(_recin_fn, hs, yr)
        return vjp(d)

    dhs, dyr = _rowwise(recin_bwd, "rec_in_bwd", [], [hs, (proj, LRU_WIDTH, 1), drec_in],
                        [(LRU_WIDTH, F32), (LRU_WIDTH, BF16)], deps=(started,))
    g_t = _scan_bwd(a_t, dhs, "lru_scan_bwd")
    dpre, dxc_direct, d_ba, d_bx, d_lam = _gates_bwd(pre, xc, lru_ba, lru_bx, lru_lambda, g_t, h_prev,
                                                     "lru_gates_bwd")
    dxc_mm = _matmul(dpre, w_bd, "nt", F32, "lru_gate_proj_bwd")
    dw_bd = _matmul(xcb, dpre, "tn", F32, "dw_lru_gate", tm=1024, tn=1024, tk=1024)
    dxr, d_conv_w8, d_conv_b = _conv_bwd(proj, conv_w8, dxc_direct, dxc_mm, "conv_bwd")
    dq, dk, dv, dbias = _attn_bwd(proj, bias, datt_o, "attn_bwd")
    dproj = jnp.concatenate([dxr, dyr, dg_att, dg_rec, dq, dk, dv], axis=1)
    dw_in_p = _matmul(h2, dproj, "tn", BF16, "mix_dw_in", tm=1024, tn=1408, tk=1024)
    grads["w_in"] = jnp.concatenate([dw_in_p[:, OFF_Q:], dw_in_p[:, :OFF_Q]], axis=1)
    started = exchange_start(["w_in"], "w_in")
    dh2 = _matmul(dproj, w_in_p, "nt", F32, "mix_dh", deps=(started,))
    dx1, d_pre[1], dmod[1][0], dmod[1][1] = _pre_bwd(x1, row(g_pre[1]), mod[1, 0], mod[1, 1], dh2, dx2, "mix_pre_bwd")

    dx0 = ffn_bwd(x0, 0, saved1, dx1, "ffn1")

    dmod_mine = jnp.concatenate([dmod[k][j] for k in range(3) for j in range(3)], axis=0)
    pieces = [
        dmod_mine,
        jnp.concatenate(d_pre, axis=0),
        jnp.concatenate(d_post, axis=0),
        d_conv_w8[:4],
        d_conv_b, d_ba, d_bx, d_lam,
        _pad_rows(_bias_grad(dbias), 3),
        _lru_diag_blocks(dw_bd, "lru_diag_blocks").reshape(128, D_MODEL),
    ]
    small_rows = 160
    packed = jnp.concatenate(pieces, axis=0)
    packed = jnp.pad(packed, ((0, small_rows - packed.shape[0]), (0, 0)))
    parts = _allgather_vmem(packed, "gather_small_grads").reshape(N_DEV, small_rows, D_MODEL)
    total = _sum_parts(parts, "sum_small_grads")
    grads["b_ada"] = total[0:9].reshape(1, -1)
    grads["norm_pre"] = _my_cols(total[9:12], me, 128)
    grads["norm_post"] = _my_cols(total[12:15], me, 128)
    grads["conv_w"] = _my_cols(total[15:19], me, 128)
    grads["conv_b"] = total[19:20]
    grads["lru_ba"] = total[20:21]
    grads["lru_bx"] = total[21:22]
    grads["lru_lambda"] = total[22:23]
    grads["rel_bias"] = total[23:26].reshape(-1)[: ATT_HEADS * (2 * MAX_REL + 1)].reshape(ATT_HEADS, -1)
    grads["lru_wa"] = total[26:90].reshape(LRU_BLOCKS, LRU_BLOCK, LRU_BLOCK)
    grads["lru_wx"] = total[90:154].reshape(LRU_BLOCKS, LRU_BLOCK, LRU_BLOCK)
    dmod_all = parts[:, 0:9, :].reshape(N_DEV, 9 * D_MODEL)
    grads["w_ada"] = _ada_bwd(c_all, _my_cols(dmod_all, me, 1152), "ada_bwd")

    res = _adamw(grads["w_ada"], w_ada[0], m_w_ada[0], v_w_ada[0], "adamw_w_ada")
    out_g["w_ada"], out_d["w_ada"], out_m["w_ada"], out_v["w_ada"] = [r.reshape(w_ada.shape) for r in res]

    sizes = [int(np.prod(weights[n].shape)) for n in small]
    tot = sum(sizes)
    rows_small = -(-tot // (16 * D_MODEL)) * 16
    flat = lambda arrs: jnp.pad(jnp.concatenate([a.reshape(-1) for a in arrs]),
                                (0, rows_small * D_MODEL - tot)).reshape(rows_small, D_MODEL)
    res = _adamw(flat([grads[n] for n in small]), flat([weights[n] for n in small]),
                 flat([mom1[n] for n in small]), flat([mom2[n] for n in small]), "adamw_small", rows=rows_small)
    offs = np.cumsum([0] + sizes)
    for dst, r in zip((out_g, out_d, out_m, out_v), res):
        rf = r.reshape(-1)
        for i, n in enumerate(small):
            dst[n] = rf[offs[i] : offs[i + 1]].reshape(weights[n].shape)

    done = res[0]
    for names, send, group, tag in pending:
        done = exchange_finish(names, send, group, tag, done)

    return (loss, dx0[None], *[out_g[n] for n in order], *[out_d[n] for n in order],
            *[out_m[n] for n in order], *[out_v[n] for n in order])
```

```python
import functools

import jax
import jax.numpy as jnp
import numpy as np
from jax import lax
from jax.experimental import pallas as pl
from jax.experimental.pallas import tpu as pltpu

D_MODEL = 1024
D_FF = 2816
ATT_HEADS = 8
ATT_HEAD_DIM = 64
ATT_WIDTH = 512
CHUNK = 64
LEFT_CHUNKS = 8
MAX_REL = 128
LRU_WIDTH = 1024
LRU_BLOCKS = 16
LRU_BLOCK = 64
LRU_C = 8.0
EPS = 1e-6
PROJ_WIDTH = 5632
N_DEV = 8

ADAM_LR = 0.001
ADAM_B1 = 0.9
ADAM_B2 = 0.999
ADAM_EPS = 1e-08
ADAM_WD = 0.01
ADAM_STEP = 10

V7X_LANES = 128
V7X_SUBLANES = 8
V7X_VMEM_BYTES = 64 * 1024 * 1024
VMEM_LIMIT = V7X_VMEM_BYTES - 8 * 1024 * 1024

ATT_TQ = 256
NEG = -1e30
BF16 = jnp.bfloat16
F32 = jnp.float32
MESH = pl.DeviceIdType.MESH

OFF_Q = 4 * LRU_WIDTH
OFF_K = OFF_Q + ATT_WIDTH
OFF_V = OFF_K + ATT_WIDTH


def _cparams(**kw):
    return pltpu.CompilerParams(vmem_limit_bytes=VMEM_LIMIT, **kw)


def _pick(n, target, unit=V7X_LANES):
    best = None
    for t in range(unit, min(n, target) + 1, unit):
        if n % t == 0:
            best = t
    return n if best is None else best


_DIMS = {
    "nn": (((1,), (0,)), ((), ())),
    "nt": (((1,), (1,)), ((), ())),
    "tn": (((0,), (0,)), ((), ())),
}


ANY_SPEC = pl.BlockSpec(memory_space=pl.ANY)


def _matmul(a, b, mode, out_dtype, name, tm=1024, tn=512, tk=1408, deps=()):
    n_deps = len(deps)
    if mode == "nn":
        (m, k), (k2, n) = a.shape, b.shape
    elif mode == "nt":
        (m, k), (n, k2) = a.shape, b.shape
    else:
        (k, m), (k2, n) = a.shape, b.shape
    assert k == k2, (a.shape, b.shape, mode)
    tm, tn, tk = _pick(m, tm), _pick(n, tn), _pick(k, tk)
    nk = k // tk
    dims = _DIMS[mode]

    def body(a_ref, b_ref, *rest):
        o_ref, scratch = rest[n_deps], rest[n_deps + 1 :]
        p = lax.dot_general(a_ref[...], b_ref[...], dims, preferred_element_type=F32)
        if nk == 1:
            o_ref[...] = p.astype(o_ref.dtype)
        else:
            acc = scratch[0]
            kk = pl.program_id(2)

            @pl.when(kk == 0)
            def _():
                acc[...] = p

            @pl.when(kk > 0)
            def _():
                acc[...] += p

            @pl.when(kk == nk - 1)
            def _():
                o_ref[...] = acc[...].astype(o_ref.dtype)

    if mode == "nn":
        a_spec = pl.BlockSpec((tm, tk), lambda i, j, kk: (i, kk))
        b_spec = pl.BlockSpec((tk, tn), lambda i, j, kk: (kk, j))
    elif mode == "nt":
        a_spec = pl.BlockSpec((tm, tk), lambda i, j, kk: (i, kk))
        b_spec = pl.BlockSpec((tn, tk), lambda i, j, kk: (j, kk))
    else:
        a_spec = pl.BlockSpec((tk, tm), lambda i, j, kk: (kk, i))
        b_spec = pl.BlockSpec((tk, tn), lambda i, j, kk: (kk, j))
    return pl.pallas_call(
        body,
        name=name,
        grid=(m // tm, n // tn, nk),
        in_specs=[a_spec, b_spec] + [ANY_SPEC] * n_deps,
        out_specs=pl.BlockSpec((tm, tn), lambda i, j, kk: (i, j)),
        out_shape=jax.ShapeDtypeStruct((m, n), out_dtype),
        scratch_shapes=[pltpu.VMEM((tm, tn), F32)] if nk > 1 else [],
        compiler_params=_cparams(dimension_semantics=("parallel", "parallel", "arbitrary")),
    )(a, b, *deps)


def _rowwise(fn, name, params, tiles, outs, accs=(), ts=256, with_index=False, deps=()):
    norm = []
    for t in tiles:
        if not isinstance(t, tuple):
            t = (t, t.shape[1], 0)
        norm.append(t if len(t) == 4 else (*t, None))
    s = norm[0][0].shape[0]
    ts = min(ts, s)
    assert s % ts == 0 and ts % V7X_SUBLANES == 0
    steps = s // ts
    halo_blocks = ts // V7X_SUBLANES
    n_p, n_t, n_o = len(params), len(norm), len(outs)

    def body(*refs):
        i = pl.program_id(0)
        vals = [r[...] for r in refs[: n_p + n_t]]
        res = fn(i, steps, *vals) if with_index else fn(*vals)
        if not isinstance(res, (tuple, list)):
            res = (res,)
        first_out = n_p + n_t + len(deps)
        o_refs = refs[first_out : first_out + n_o]
        a_refs = refs[first_out + n_o :]
        for r, v in zip(o_refs, res[:n_o]):
            r[...] = v.astype(r.dtype)
        for r, v in zip(a_refs, res[n_o:]):
            _accumulate(r, v, i)

    in_specs = [pl.BlockSpec(p.shape, lambda i: (0, 0)) for p in params]
    for arr, w, cb, halo in norm:
        if halo is None:
            in_specs.append(pl.BlockSpec((ts, w), lambda i, cb=cb: (i, cb)))
        elif halo == "prev":
            in_specs.append(
                pl.BlockSpec((V7X_SUBLANES, w), lambda i, cb=cb: (jnp.maximum(i * halo_blocks - 1, 0), cb))
            )
        else:
            last = s // V7X_SUBLANES - 1
            in_specs.append(
                pl.BlockSpec((V7X_SUBLANES, w), lambda i, cb=cb: (jnp.minimum((i + 1) * halo_blocks, last), cb))
            )
    in_specs += [ANY_SPEC] * len(deps)
    out_specs = [pl.BlockSpec((ts, w), lambda i: (i, 0)) for w, _ in outs]
    out_specs += [pl.BlockSpec(shape, lambda i: (0, 0)) for shape in accs]
    out_shape = [jax.ShapeDtypeStruct((s, w), dt) for w, dt in outs]
    out_shape += [jax.ShapeDtypeStruct(shape, F32) for shape in accs]
    res = pl.pallas_call(
        body,
        name=name,
        grid=(steps,),
        in_specs=in_specs,
        out_specs=out_specs,
        out_shape=out_shape,
        compiler_params=_cparams(dimension_semantics=("arbitrary",)),
    )(*params, *[t[0] for t in norm], *deps)
    return res


def _accumulate(ref, val, step):
    @pl.when(step == 0)
    def _():
        ref[...] = val

    @pl.when(step > 0)
    def _():
        ref[...] += val


def _sigmoid(z):
    return jax.nn.sigmoid(z)


def _silu(z):
    return z * _sigmoid(z)


def _gelu(z):
    return 0.5 * z * (1.0 + jnp.tanh(0.7978845608028654 * (z + 0.044715 * (z * z * z))))


def _pre_fn(g, shift, scale, x):
    r = lax.rsqrt(jnp.mean(x * x, axis=-1, keepdims=True) + EPS)
    return ((x * r) * g) * (1.0 + scale) + shift


def _post_fn(res_w, g, gate, f, x):
    r = lax.rsqrt(jnp.mean(f * f, axis=-1, keepdims=True) + EPS)
    return x + (res_w * gate) * ((f * r) * g)


def _swiglu_fn(gu):
    return _silu(gu[:, :D_FF]) * gu[:, D_FF:]


def _gates_fn(ba, bx, lam, pre, xc):
    ra = _sigmoid(pre[:, :LRU_WIDTH] + ba)
    ia = _sigmoid(pre[:, LRU_WIDTH:] + bx)
    softplus = jnp.maximum(-lam, 0.0) + jnp.log1p(jnp.exp(-jnp.abs(lam)))
    log_a = (-LRU_C) * ra * softplus
    a = jnp.exp(log_a)
    mult = jnp.sqrt(-jnp.tanh(log_a) * (a * a + 1.0))
    return a, mult * (ia * xc)


def _recin_fn(hs, yr):
    return hs * _gelu(yr)


def _merge_fn(att, rec, g_att, g_rec):
    return _sigmoid(g_att) * att + _sigmoid(g_rec) * rec


def _rowsum(v):
    return jnp.sum(v, axis=0, keepdims=True)


def _pre_fwd(x, g, shift, scale, name, deps=()):
    (h,) = _rowwise(_pre_fn, name, [g, shift, scale], [x], [(D_MODEL, BF16)], deps=deps)
    return h


def _pre_bwd(x, g, shift, scale, dh, dres, name):
    def fn(g, shift, scale, x, dh, dres):
        _, vjp = jax.vjp(_pre_fn, g, shift, scale, x)
        dg, dshift, dscale, dx = vjp(dh)
        return dx + dres, dg, dshift, dscale

    row = (1, D_MODEL)
    return _rowwise(fn, name, [g, shift, scale], [x, dh, dres], [(D_MODEL, F32)], [row, row, row])


def _post_fwd(f, x, g, gate, res_w, name):
    (y,) = _rowwise(functools.partial(_post_fn, res_w), name, [g, gate], [f, x], [(D_MODEL, F32)])
    return y


def _post_bwd(f, g, gate, res_w, dy, name, deps=()):
    def fn(g, gate, f, dy):
        _, vjp = jax.vjp(lambda g, gate, f: _post_fn(res_w, g, gate, f, 0.0), g, gate, f)
        dg, dgate, df = vjp(dy)
        return df, dg, dgate

    row = (1, D_MODEL)
    return _rowwise(fn, name, [g, gate], [f, dy], [(D_MODEL, BF16)], [row, row], deps=deps)


def _loss_stage(y, target, name):
    def fn(y, t):
        diff = y - t
        return diff * (1.0 / D_MODEL), _rowsum(diff * diff)

    return _rowwise(fn, name, [], [y, target], [(D_MODEL, F32)], [(1, D_MODEL)])


FFN_TM = 512
FFN_TF = 1408


def _glu_fn(g, u):
    return _silu(g) * u


def _ffn_up(h, w_gu, name):
    s = h.shape[0]
    tm = min(FFN_TM, s)
    nf = D_FF // FFN_TF

    def body(h_ref, wg_ref, wu_ref, a_ref, g_ref, u_ref):
        hv = h_ref[...]
        g = jnp.dot(hv, wg_ref[...], preferred_element_type=F32)
        u = jnp.dot(hv, wu_ref[...], preferred_element_type=F32)
        a_ref[...] = _glu_fn(g, u).astype(a_ref.dtype)
        g_ref[...] = g.astype(g_ref.dtype)
        u_ref[...] = u.astype(u_ref.dtype)

    out = pl.BlockSpec((tm, FFN_TF), lambda i, j: (i, j))
    return pl.pallas_call(
        body,
        name=name,
        grid=(s // tm, nf),
        in_specs=[pl.BlockSpec((tm, D_MODEL), lambda i, j: (i, 0)),
                  pl.BlockSpec((D_MODEL, FFN_TF), lambda i, j: (0, j)),
                  pl.BlockSpec((D_MODEL, FFN_TF), lambda i, j: (0, nf + j))],
        out_specs=[out, out, out],
        out_shape=[jax.ShapeDtypeStruct((s, D_FF), BF16)] * 3,
        compiler_params=_cparams(dimension_semantics=("parallel", "arbitrary")),
    )(h, w_gu, w_gu)


def _ffn_up_bwd(df, w_down, g, u, name, deps=()):
    s = df.shape[0]
    tm = min(FFN_TM, s)

    def body(df_ref, wd_ref, g_ref, u_ref, *rest):
        dg_ref, du_ref = rest[len(deps) :]
        da = lax.dot_general(df_ref[...], wd_ref[...], _DIMS["nt"], preferred_element_type=F32)
        _, vjp = jax.vjp(_glu_fn, g_ref[...].astype(F32), u_ref[...].astype(F32))
        dg, du = vjp(da)
        dg_ref[...] = dg.astype(dg_ref.dtype)
        du_ref[...] = du.astype(du_ref.dtype)

    blk = pl.BlockSpec((tm, FFN_TF), lambda i, j: (i, j))
    return pl.pallas_call(
        body,
        name=name,
        grid=(s // tm, D_FF // FFN_TF),
        in_specs=[pl.BlockSpec((tm, D_MODEL), lambda i, j: (i, 0)),
                  pl.BlockSpec((FFN_TF, D_MODEL), lambda i, j: (j, 0)), blk, blk] + [ANY_SPEC] * len(deps),
        out_specs=[blk, blk],
        out_shape=[jax.ShapeDtypeStruct((s, D_FF), BF16)] * 2,
        compiler_params=_cparams(dimension_semantics=("parallel", "arbitrary")),
    )(df, w_down, g, u, *deps)


def _ffn_dh(dg, du, w_gu, name, deps=()):
    s = dg.shape[0]
    tm, tn = min(FFN_TM, s), 512

    def body(dg_ref, du_ref, wg_ref, wu_ref, *rest):
        o_ref = rest[len(deps)]
        p = lax.dot_general(dg_ref[...], wg_ref[...], _DIMS["nt"], preferred_element_type=F32)
        o_ref[...] = p + lax.dot_general(du_ref[...], wu_ref[...], _DIMS["nt"], preferred_element_type=F32)

    a_spec = pl.BlockSpec((tm, D_FF), lambda i, j: (i, 0))
    return pl.pallas_call(
        body,
        name=name,
        grid=(s // tm, D_MODEL // tn),
        in_specs=[a_spec, a_spec,
                  pl.BlockSpec((tn, D_FF), lambda i, j: (j, 0)),
                  pl.BlockSpec((tn, D_FF), lambda i, j: (j, 1))] + [ANY_SPEC] * len(deps),
        out_specs=pl.BlockSpec((tm, tn), lambda i, j: (i, j)),
        out_shape=jax.ShapeDtypeStruct((s, D_MODEL), F32),
        compiler_params=_cparams(dimension_semantics=("parallel", "arbitrary")),
    )(dg, du, w_gu, w_gu, *deps)


def _lru_diag_blocks(dw_bd, name):
    def body(w_ref, o_ref):
        for half in range(2):
            for n in range(LRU_BLOCKS):
                rows = slice(n * LRU_BLOCK, (n + 1) * LRU_BLOCK)
                cols = slice(half * LRU_WIDTH + n * LRU_BLOCK, half * LRU_WIDTH + (n + 1) * LRU_BLOCK)
                o_ref[half, rows, :] = w_ref[rows, cols]

    return pl.pallas_call(
        body, name=name, out_shape=jax.ShapeDtypeStruct((2, LRU_WIDTH, LRU_BLOCK), F32), compiler_params=_cparams()
    )(dw_bd)


def _swiglu_fwd(gu, name):
    (a,) = _rowwise(_swiglu_fn, name, [], [gu], [(D_FF, BF16)], ts=128)
    return a


def _swiglu_bwd(gu, da, name, deps=()):
    def fn(gu, da):
        _, vjp = jax.vjp(_swiglu_fn, gu)
        return vjp(da)[0]

    (dgu,) = _rowwise(fn, name, [], [gu, da], [(2 * D_FF, BF16)], ts=128, deps=deps)
    return dgu


def _shift_down(ext, j, rows):
    return pltpu.roll(ext, j, 0)[V7X_SUBLANES : V7X_SUBLANES + rows]


def _shift_up(ext, j, rows):
    return pltpu.roll(ext, ext.shape[0] - j, 0)[:rows] if j else ext[:rows]


def _conv_fwd(proj, w8, b, name):
    def fn(i, steps, w8, b, x, halo):
        halo = jnp.where(i > 0, halo, 0.0)
        ext = jnp.concatenate([halo, x], axis=0)
        acc = b + w8[3:4] * x
        for j in (1, 2, 3):
            acc = acc + w8[3 - j : 4 - j] * _shift_down(ext, j, x.shape[0])
        return acc, acc

    tiles = [(proj, LRU_WIDTH, 0), (proj, LRU_WIDTH, 0, "prev")]
    return _rowwise(fn, name, [w8, b], tiles, [(LRU_WIDTH, F32), (LRU_WIDTH, BF16)], with_index=True)


def _conv_bwd(proj, w8, d1, d2, name):
    def fn(i, steps, w8, x, halo, d1, d1n, d2, d2n):
        rows = x.shape[0]
        d = d1 + d2
        dn = jnp.where(i < steps - 1, d1n + d2n, 0.0)
        halo = jnp.where(i > 0, halo, 0.0)
        dext = jnp.concatenate([d, dn], axis=0)
        xext = jnp.concatenate([halo, x], axis=0)
        dx = w8[3:4] * d
        dw = [None] * 4
        dw[3] = _rowsum(d * x)
        for k in (1, 2, 3):
            dx = dx + w8[3 - k : 4 - k] * _shift_up(dext, k, rows)
            dw[3 - k] = _rowsum(d * _shift_down(xext, k, rows))
        dw8 = jnp.concatenate(dw + [jnp.zeros((4, LRU_WIDTH), F32)], axis=0)
        return dx, dw8, _rowsum(d)

    tiles = [(proj, LRU_WIDTH, 0), (proj, LRU_WIDTH, 0, "prev"), d1, (d1, LRU_WIDTH, 0, "next"),
             d2, (d2, LRU_WIDTH, 0, "next")]
    return _rowwise(fn, name, [w8], tiles, [(LRU_WIDTH, BF16)], [(8, LRU_WIDTH), (1, LRU_WIDTH)], with_index=True)


def _gates_fwd(pre, xc, ba, bx, lam, name):
    return _rowwise(_gates_fn, name, [ba, bx, lam], [pre, xc], [(LRU_WIDTH, F32), (LRU_WIDTH, F32)])


def _gates_bwd(pre, xc, ba, bx, lam, g, h_prev, name):
    def fn(ba, bx, lam, pre, xc, g, h_prev):
        _, vjp = jax.vjp(_gates_fn, ba, bx, lam, pre, xc)
        dba, dbx, dlam, dpre, dxc = vjp((g * h_prev, g))
        return dpre, dxc, dba, dbx, dlam

    row = (1, LRU_WIDTH)
    return _rowwise(fn, name, [ba, bx, lam], [pre, xc, g, h_prev],
                    [(2 * LRU_WIDTH, BF16), (LRU_WIDTH, F32)], [row, row, row])


SCAN_ROWS = 512


def _block_scan(a, b, row, reverse):
    for d in (1, 2, 4):
        if reverse:
            shift, keep = V7X_SUBLANES - d, row < V7X_SUBLANES - d
        else:
            shift, keep = d, row >= d
        a_s = pltpu.roll(a, shift, 0)
        b_s = pltpu.roll(b, shift, 0)
        b = jnp.where(keep, a * b_s + b, b)
        a = jnp.where(keep, a * a_s, a)
    return a, b


def _scan_fwd(a, u, name):
    s, w = a.shape
    ts = min(SCAN_ROWS, s)
    sub = ts // V7X_SUBLANES

    def body(a_ref, u_ref, h_ref, hp_ref, carry):
        @pl.when(pl.program_id(0) == 0)
        def _():
            carry[...] = jnp.zeros_like(carry)

        row = lax.broadcasted_iota(jnp.int32, (V7X_SUBLANES, w), 0)

        def step(j, c):
            rows = pl.ds(pl.multiple_of(j * V7X_SUBLANES, V7X_SUBLANES), V7X_SUBLANES)
            pa, pb = _block_scan(a_ref[rows, :], u_ref[rows, :], row, False)
            h = pb + pa * c
            h_ref[rows, :] = h
            hp_ref[rows, :] = jnp.where(row >= 1, pltpu.roll(h, 1, 0), c)
            return jnp.broadcast_to(h[V7X_SUBLANES - 1 :], (V7X_SUBLANES, w))

        carry[...] = lax.fori_loop(0, sub, step, carry[...])

    spec = pl.BlockSpec((ts, w), lambda i: (i, 0))
    return pl.pallas_call(
        body,
        name=name,
        grid=(s // ts,),
        in_specs=[spec, spec],
        out_specs=[spec, spec],
        out_shape=[jax.ShapeDtypeStruct((s, w), F32)] * 2,
        scratch_shapes=[pltpu.VMEM((V7X_SUBLANES, w), F32)],
        compiler_params=_cparams(dimension_semantics=("arbitrary",)),
    )(a, u)


def _scan_bwd(a, dh, name):
    s, w = a.shape
    ts = min(SCAN_ROWS, s)
    sub = ts // V7X_SUBLANES
    steps = s // ts

    def body(a_ref, d_ref, g_ref, carry):
        @pl.when(pl.program_id(0) == 0)
        def _():
            carry[...] = jnp.zeros_like(carry)

        row = lax.broadcasted_iota(jnp.int32, (V7X_SUBLANES, w), 0)

        def step(jj, c):
            j = sub - 1 - jj
            rows = pl.ds(pl.multiple_of(j * V7X_SUBLANES, V7X_SUBLANES), V7X_SUBLANES)
            av, dv = a_ref[rows, :], d_ref[rows, :]
            pa, pb = _block_scan(av, av * dv, row, True)
            big = pb + pa * c
            g_ref[rows, :] = dv + jnp.where(row < V7X_SUBLANES - 1, pltpu.roll(big, V7X_SUBLANES - 1, 0), c)
            return jnp.broadcast_to(big[:1], (V7X_SUBLANES, w))

        carry[...] = lax.fori_loop(0, sub, step, carry[...])

    spec = pl.BlockSpec((ts, w), lambda i: (steps - 1 - i, 0))
    return pl.pallas_call(
        body,
        name=name,
        grid=(steps,),
        in_specs=[spec, spec],
        out_specs=spec,
        out_shape=jax.ShapeDtypeStruct((s, w), F32),
        scratch_shapes=[pltpu.VMEM((V7X_SUBLANES, w), F32)],
        compiler_params=_cparams(dimension_semantics=("arbitrary",)),
    )(a, dh)


def _rel_index():
    i = np.arange(ATT_TQ)[:, None]
    j = np.arange(3 * ATT_TQ)[None, :]
    band = (j // CHUNK >= i // CHUNK) & (j // CHUNK <= i // CHUNK + LEFT_CHUNKS)
    return band


SKEW = 4 * ATT_TQ


def _skew_onehot():
    t = np.arange(SKEW)
    diag = np.where(t < 3 * ATT_TQ, -t, SKEW - t)
    idx = np.clip(diag + LEFT_CHUNKS * CHUNK, -MAX_REL, MAX_REL) + MAX_REL
    hit = (idx[:, None] == np.arange(2 * MAX_REL + 1)[None, :]) & (t[:, None] != 3 * ATT_TQ)
    return hit.astype(np.float32)


def _bias_tile(rel_bias):
    per_t = jnp.dot(rel_bias, jnp.asarray(_skew_onehot()).T, precision=lax.Precision.HIGHEST)
    flat = jnp.broadcast_to(per_t[:, None, :], (ATT_HEADS, ATT_TQ, SKEW)).reshape(ATT_HEADS, ATT_TQ * SKEW)
    tile = flat[:, : ATT_TQ * (SKEW - 1)].reshape(ATT_HEADS, ATT_TQ, SKEW - 1)[:, :, : 3 * ATT_TQ]
    first = (2 - np.arange(3))[:, None, None, None] * ATT_TQ
    seen = _rel_index()[None, None] & (np.arange(3 * ATT_TQ)[None, None, None, :] >= first)
    return jnp.where(jnp.asarray(seen), tile[None], NEG)


def _bias_grad(dbias):
    flat = jnp.pad(dbias, ((0, 0), (0, 0), (0, SKEW - 1 - 3 * ATT_TQ))).reshape(ATT_HEADS, ATT_TQ * (SKEW - 1))
    per_t = jnp.sum(jnp.pad(flat, ((0, 0), (0, ATT_TQ))).reshape(ATT_HEADS, ATT_TQ, SKEW), axis=1)
    return jnp.dot(per_t, jnp.asarray(_skew_onehot()), precision=lax.Precision.HIGHEST)


def _attn_specs(nt):
    qb, kb, vb = OFF_Q // V7X_LANES, OFF_K // V7X_LANES, OFF_V // V7X_LANES
    blk = (ATT_TQ, V7X_LANES)

    def qmap(base):
        return lambda hp, m: (jnp.minimum(m, nt - 1), base + hp)

    def wmap(base, back):
        return lambda hp, m: (jnp.clip(m - back, 0, nt - 1), base + hp)

    specs = [pl.BlockSpec(blk, qmap(qb))]
    specs += [pl.BlockSpec(blk, wmap(kb, back)) for back in (2, 1, 0)]
    specs += [pl.BlockSpec(blk, wmap(vb, back)) for back in (2, 1, 0)]
    return specs


ATT_SCALE = ATT_HEAD_DIM**-0.5


def _attn_exp(qh, kh, bias):
    s = lax.dot_general(qh, kh, _DIMS["nt"], preferred_element_type=F32) + bias
    e = jnp.exp(s - jnp.max(s, axis=-1, keepdims=True))
    return e, jnp.sum(e, axis=-1, keepdims=True)


def _attn_window(k0, k1, k2, v0, v1, v2):
    k = jnp.concatenate([k0[...], k1[...], k2[...]], axis=0).astype(BF16)
    v = jnp.concatenate([v0[...], v1[...], v2[...]], axis=0).astype(BF16)
    return k, v


def _bias_spec():
    return pl.BlockSpec((1, 2, ATT_TQ, 3 * ATT_TQ), lambda hp, m: (jnp.minimum(m, 2), hp, 0, 0))


def _attn_fwd(proj, bias, name):
    s = proj.shape[0]
    nt = s // ATT_TQ

    def body(q_ref, k0, k1, k2, v0, v1, v2, b_ref, o_ref):
        k, v = _attn_window(k0, k1, k2, v0, v1, v2)
        q = (q_ref[...] * ATT_SCALE).astype(BF16)
        for hh in range(2):
            cols = slice(hh * ATT_HEAD_DIM, (hh + 1) * ATT_HEAD_DIM)
            e, total = _attn_exp(q[:, cols], k[:, cols], b_ref[0, hh])
            o = jnp.dot(e.astype(BF16), v[:, cols], preferred_element_type=F32) / total
            o_ref[:, cols] = o.astype(o_ref.dtype)

    specs = _attn_specs(nt) + [_bias_spec()]
    return pl.pallas_call(
        body,
        name=name,
        grid=(ATT_HEADS // 2, nt),
        in_specs=specs,
        out_specs=pl.BlockSpec((ATT_TQ, V7X_LANES), lambda hp, m: (m, hp)),
        out_shape=jax.ShapeDtypeStruct((s, ATT_WIDTH), BF16),
        compiler_params=_cparams(dimension_semantics=("parallel", "arbitrary")),
    )(proj, proj, proj, proj, proj, proj, proj, bias)


def _attn_bwd(proj, bias, do, name):
    s = proj.shape[0]
    nt = s // ATT_TQ
    win = 3 * ATT_TQ

    def body(q_ref, k0, k1, k2, v0, v1, v2, do_ref, b_ref, dq_ref, dk_ref, dv_ref, db_ref, dk_acc, dv_acc):
        m = pl.program_id(1)

        @pl.when(m == 0)
        def _():
            dk_acc[...] = jnp.zeros_like(dk_acc)
            dv_acc[...] = jnp.zeros_like(dv_acc)
            db_ref[...] = jnp.zeros_like(db_ref)

        @pl.when(m < nt)
        def _():
            k, v = _attn_window(k0, k1, k2, v0, v1, v2)
            q = (q_ref[...] * ATT_SCALE).astype(BF16)
            dout = do_ref[...]
            for hh in range(2):
                cols = slice(hh * ATT_HEAD_DIM, (hh + 1) * ATT_HEAD_DIM)
                qh, kh, vh, doh = q[:, cols], k[:, cols], v[:, cols], dout[:, cols]
                e, total = _attn_exp(qh, kh, b_ref[0, hh])
                p = e / total
                dvh = lax.dot_general(p.astype(BF16), doh, _DIMS["tn"], preferred_element_type=F32)
                dp = lax.dot_general(doh, vh, _DIMS["nt"], preferred_element_type=F32)
                ds = p * (dp - jnp.sum(dp * p, axis=-1, keepdims=True))
                db_ref[hh] += ds
                dsb = ds.astype(BF16)
                dqh = jnp.dot(dsb, kh, preferred_element_type=F32) * ATT_SCALE
                dkh = lax.dot_general(dsb, qh, _DIMS["tn"], preferred_element_type=F32)
                dq_ref[:, cols] = dqh.astype(dq_ref.dtype)
                dk_acc[:, cols] += dkh
                dv_acc[:, cols] += dvh

        dk_ref[...] = dk_acc[:ATT_TQ].astype(dk_ref.dtype)
        dv_ref[...] = dv_acc[:ATT_TQ].astype(dv_ref.dtype)
        for acc in (dk_acc, dv_acc):
            rest = acc[ATT_TQ:]
            acc[: win - ATT_TQ] = rest
            acc[win - ATT_TQ :] = jnp.zeros((ATT_TQ, V7X_LANES), F32)

    blk = (ATT_TQ, V7X_LANES)
    specs = _attn_specs(nt)
    specs.append(pl.BlockSpec(blk, lambda hp, m: (jnp.minimum(m, nt - 1), hp)))
    specs.append(_bias_spec())
    done = lambda hp, m: (jnp.maximum(m - 2, 0), hp)
    out_specs = [
        pl.BlockSpec(blk, lambda hp, m: (jnp.minimum(m, nt - 1), hp)),
        pl.BlockSpec(blk, done),
        pl.BlockSpec(blk, done),
        pl.BlockSpec((2, ATT_TQ, win), lambda hp, m: (hp, 0, 0)),
    ]
    out_shape = [jax.ShapeDtypeStruct((s, ATT_WIDTH), BF16)] * 3
    out_shape.append(jax.ShapeDtypeStruct((ATT_HEADS, ATT_TQ, win), F32))
    return pl.pallas_call(
        body,
        name=name,
        grid=(ATT_HEADS // 2, nt + 2),
        in_specs=specs,
        out_specs=out_specs,
        out_shape=out_shape,
        scratch_shapes=[pltpu.VMEM((win, V7X_LANES), F32), pltpu.VMEM((win, V7X_LANES), F32)],
        compiler_params=_cparams(dimension_semantics=("arbitrary", "arbitrary")),
    )(proj, proj, proj, proj, proj, proj, proj, do, bias)


def _ada_fwd(c_all, w, name):
    def body(c_ref, w_ref, o_ref):
        act = _silu(c_ref[...]).astype(BF16)
        o_ref[...] = jnp.dot(act, w_ref[...].astype(BF16), preferred_element_type=F32)

    return pl.pallas_call(
        body, name=name, out_shape=jax.ShapeDtypeStruct((c_all.shape[0], w.shape[1]), F32), compiler_params=_cparams()
    )(c_all, w)


def _ada_bwd(c_all, dmod, name):
    def body(c_ref, d_ref, o_ref):
        act = _silu(c_ref[...])
        o_ref[...] = lax.dot_general(act, d_ref[...], _DIMS["tn"], preferred_element_type=F32,
                                     precision=lax.Precision.HIGHEST)

    return pl.pallas_call(
        body, name=name, out_shape=jax.ShapeDtypeStruct((c_all.shape[1], dmod.shape[1]), F32), compiler_params=_cparams()
    )(c_all, dmod)


def _adamw_parts(landed, sent, me, w, m, v, name, rows=256):
    r, c = w.shape
    tr = _pick(r, rows, 16)

    def body(me_ref, g_ref, own_ref, w_ref, m_ref, v_ref, go_ref, d_ref, mo_ref, vo_ref):
        mine = me_ref[0]
        grad = jnp.zeros((tr, c), F32)
        for d in range(N_DEV):
            grad = grad + jnp.where(mine == d, own_ref[0], g_ref[d]).astype(F32)
        _adamw_update(grad, w_ref, m_ref, v_ref, go_ref, d_ref, mo_ref, vo_ref)

    spec = pl.BlockSpec((tr, c), lambda i, me_ref: (i, 0))
    return pl.pallas_call(
        body,
        name=name,
        grid_spec=pltpu.PrefetchScalarGridSpec(
            num_scalar_prefetch=1,
            grid=(r // tr,),
            in_specs=[pl.BlockSpec((N_DEV, tr, c), lambda i, me_ref: (0, i, 0)),
                      pl.BlockSpec((1, tr, c), lambda i, me_ref: (me_ref[0], i, 0)), spec, spec, spec],
            out_specs=[spec] * 4,
        ),
        out_shape=[jax.ShapeDtypeStruct((r, c), F32)] * 4,
        compiler_params=_cparams(dimension_semantics=("parallel",)),
    )(me.reshape(1).astype(jnp.int32), landed, sent, w, m, v)


def _adamw_update(grad, w_ref, m_ref, v_ref, go_ref, d_ref, mo_ref, vo_ref):
    m2 = ADAM_B1 * m_ref[...] + (1.0 - ADAM_B1) * grad
    v2 = ADAM_B2 * v_ref[...] + (1.0 - ADAM_B2) * (grad * grad)
    m_hat = m2 / (1.0 - ADAM_B1**ADAM_STEP)
    v_hat = v2 / (1.0 - ADAM_B2**ADAM_STEP)
    go_ref[...] = grad
    d_ref[...] = -ADAM_LR * (m_hat / (jnp.sqrt(v_hat) + ADAM_EPS) + ADAM_WD * w_ref[...])
    mo_ref[...] = m2
    vo_ref[...] = v2


def _adamw(g, w, m, v, name, rows=256):
    r, c = w.shape
    tr = _pick(r, rows, 16)

    def body(g_ref, w_ref, m_ref, v_ref, go_ref, d_ref, mo_ref, vo_ref):
        _adamw_update(g_ref[...], w_ref, m_ref, v_ref, go_ref, d_ref, mo_ref, vo_ref)

    spec = pl.BlockSpec((tr, c), lambda i: (i, 0))
    return pl.pallas_call(
        body,
        name=name,
        grid=(r // tr,),
        in_specs=[spec, spec, spec, spec],
        out_specs=[spec] * 4,
        out_shape=[jax.ShapeDtypeStruct((r, c), F32)] * 4,
        compiler_params=_cparams(dimension_semantics=("parallel",)),
    )(g, w, m, v)


def _sum_parts(parts, name):
    def body(p_ref, o_ref):
        acc = p_ref[0]
        for d in range(1, N_DEV):
            acc = acc + p_ref[d]
        o_ref[...] = acc

    return pl.pallas_call(
        body, name=name, out_shape=jax.ShapeDtypeStruct(parts.shape[1:], F32), compiler_params=_cparams()
    )(parts)


def _place():
    x, y, c = lax.axis_index("x"), lax.axis_index("y"), lax.axis_index("c")
    return x, y, c


def _dev_index(p):
    return 4 * p[0] + 2 * p[1] + p[2]


def _allgather_vmem(shard, name):
    m_per, n = shard.shape

    def body(x_ref, out_ref, send_sems, recv_sems, local_sem):
        x, y, c = _place()
        me, sibling = (x, y, c), (x, y, 1 - c)
        chips = [(1 - x, y), (x, 1 - y), (1 - x, 1 - y)]

        def rows(p):
            return out_ref.at[pl.ds(_dev_index(p) * m_per, m_per), :]

        def copy(k, block, to, src=None):
            return pltpu.make_async_remote_copy(
                src_ref=rows(block) if src is None else src, dst_ref=rows(block),
                send_sem=send_sems.at[k], recv_sem=recv_sems.at[k], device_id=to, device_id_type=MESH)

        mine = pltpu.make_async_copy(x_ref, rows(me), local_sem)
        mine.start()
        first = [copy(0, me, sibling, src=x_ref)]
        first += [copy(1 + j, me, (*chip, c), src=x_ref) for j, chip in enumerate(chips)]
        for cp in first:
            cp.start()
        passed = [copy(4 + j, (*chip, c), sibling) for j, chip in enumerate(chips)]
        for j, chip in enumerate(chips):
            copy(1 + j, (*chip, c), me).wait_recv()
            passed[j].start()
        copy(0, sibling, me).wait_recv()
        for j, chip in enumerate(chips):
            copy(4 + j, (*chip, 1 - c), me).wait_recv()
        for cp in first + passed:
            cp.wait_send()
        mine.wait()

    return pl.pallas_call(
        body,
        name=name,
        out_shape=jax.ShapeDtypeStruct((N_DEV * m_per, n), shard.dtype),
        in_specs=[pl.BlockSpec(memory_space=pltpu.VMEM)],
        out_specs=pl.BlockSpec(memory_space=pltpu.VMEM),
        scratch_shapes=[pltpu.SemaphoreType.DMA((7,)), pltpu.SemaphoreType.DMA((7,)), pltpu.SemaphoreType.DMA],
        compiler_params=_cparams(),
    )(shard)


def _allgather_hbm(shards, name):
    n = len(shards)

    def body(*refs):
        ins, outs = refs[:n], refs[n : 2 * n]
        send_sems, recv_sems, local_sems = refs[2 * n :]
        x, y, c = _place()
        me, sibling = (x, y, c), (x, y, 1 - c)
        chips = [(1 - x, y), (x, 1 - y), (1 - x, 1 - y)]

        def copy(a, k, block, to, src=None):
            dst = outs[a].at[_dev_index(block)]
            return pltpu.make_async_remote_copy(
                src_ref=dst if src is None else src, dst_ref=dst,
                send_sem=send_sems.at[a * 7 + k], recv_sem=recv_sems.at[a * 7 + k], device_id=to, device_id_type=MESH)

        mine = [pltpu.make_async_copy(ins[a], outs[a].at[_dev_index(me)], local_sems.at[a]) for a in range(n)]
        for cp in mine:
            cp.start()
        first = []
        for a in range(n):
            first.append(copy(a, 0, me, sibling, src=ins[a]))
            first += [copy(a, 1 + j, me, (*chip, c), src=ins[a]) for j, chip in enumerate(chips)]
        for cp in first:
            cp.start()
        passed = []
        for j, chip in enumerate(chips):
            for a in range(n):
                copy(a, 1 + j, (*chip, c), me).wait_recv()
                cp = copy(a, 4 + j, (*chip, c), sibling)
                cp.start()
                passed.append(cp)
        for a in range(n):
            copy(a, 0, sibling, me).wait_recv()
        for j, chip in enumerate(chips):
            for a in range(n):
                copy(a, 4 + j, (*chip, 1 - c), me).wait_recv()
        for cp in first + passed:
            cp.wait_send()
        for cp in mine:
            cp.wait()

    any_spec = pl.BlockSpec(memory_space=pl.ANY)
    return pl.pallas_call(
        body,
        name=name,
        out_shape=[jax.ShapeDtypeStruct((N_DEV, *s.shape), s.dtype) for s in shards],
        in_specs=[any_spec] * n,
        out_specs=[any_spec] * n,
        scratch_shapes=[pltpu.SemaphoreType.DMA((7 * n,)), pltpu.SemaphoreType.DMA((7 * n,)),
                        pltpu.SemaphoreType.DMA((n,))],
        compiler_params=_cparams(),
    )(*shards)


def _exchange_hbm(bufs, name):
    n = len(bufs)

    def body(*refs):
        ins, outs = refs[:n], refs[n : 2 * n]
        send_sems, recv_sems, local_sems = refs[2 * n :]
        x, y, c = _place()
        me = _dev_index((x, y, c))
        mine = [pltpu.make_async_copy(ins[a].at[me], outs[a].at[me], local_sems.at[a]) for a in range(n)]
        for cp in mine:
            cp.start()
        def peer_of(k):
            return (1 - x if k & 4 else x, 1 - y if k & 2 else y, 1 - c if k & 1 else c)

        copies = []
        for k in range(1, N_DEV):
            peer = peer_of(k)
            for a in range(n):
                copies.append(pltpu.make_async_remote_copy(
                    src_ref=ins[a].at[_dev_index(peer)], dst_ref=outs[a].at[me],
                    send_sem=send_sems.at[a * 7 + k - 1], recv_sem=recv_sems.at[a * 7 + k - 1],
                    device_id=peer, device_id_type=MESH))
        for cp in copies:
            cp.start()
        for k in range(1, N_DEV):
            peer = peer_of(k)
            for a in range(n):
                pltpu.make_async_remote_copy(
                    src_ref=ins[a].at[me], dst_ref=outs[a].at[_dev_index(peer)],
                    send_sem=send_sems.at[a * 7 + k - 1], recv_sem=recv_sems.at[a * 7 + k - 1],
                    device_id=peer, device_id_type=MESH).wait_recv()
        for cp in copies:
            cp.wait_send()
        for cp in mine:
            cp.wait()

    any_spec = pl.BlockSpec(memory_space=pl.ANY)
    return pl.pallas_call(
        body,
        name=name,
        out_shape=[jax.ShapeDtypeStruct(b.shape, b.dtype) for b in bufs],
        in_specs=[any_spec] * n,
        out_specs=[any_spec] * n,
        scratch_shapes=[pltpu.SemaphoreType.DMA((7 * n,)), pltpu.SemaphoreType.DMA((7 * n,)),
                        pltpu.SemaphoreType.DMA((n,))],
        compiler_params=_cparams(),
    )(*bufs)


HBM_SPEC = pl.BlockSpec(memory_space=pltpu.HBM)
SEM_SPEC = pl.BlockSpec(memory_space=pltpu.SEMAPHORE)
EFFECT = pltpu.SideEffectType.DATAFLOW_SIDE_EFFECTING


def _peers(x, y, c):
    return [(1 - x if k & 4 else x, 1 - y if k & 2 else y, 1 - c if k & 1 else c) for k in range(1, N_DEV)]


def _push_start(groups, sliced, name, after=()):
    flat = [b for g in groups for b in g]
    n, ng = len(flat), len(groups)
    sizes = [len(g) for g in groups]
    lands = [lax.empty(b.shape if sliced else (N_DEV, *b.shape), b.dtype) for b in flat]

    def body(*refs):
        ins, lnd = refs[:n], refs[n : 2 * n]
        sems = refs[2 * n + len(after) : 2 * n + len(after) + 2 * ng]
        token = refs[-1]
        x, y, c = _place()
        me = _dev_index((x, y, c))
        first = 0
        for gi, size in enumerate(sizes):
            for k, peer in enumerate(_peers(x, y, c)):
                for j in range(first, first + size):
                    sem = (j - first) * 7 + k
                    pltpu.make_async_remote_copy(
                        src_ref=ins[j].at[_dev_index(peer)] if sliced else ins[j], dst_ref=lnd[j].at[me],
                        send_sem=sems[2 * gi].at[sem], recv_sem=sems[2 * gi + 1].at[sem],
                        device_id=peer, device_id_type=MESH).start()
            first += size
        token[...] = jnp.zeros_like(token)

    out_shape = []
    for size in sizes:
        out_shape += [pltpu.SemaphoreType.DMA((7 * size,)), pltpu.SemaphoreType.DMA((7 * size,))]
    out_shape += [pltpu.HBM(b.shape, b.dtype) for b in flat + lands]
    out_shape.append(jax.ShapeDtypeStruct((V7X_SUBLANES, V7X_LANES), F32))
    res = pl.pallas_call(
        body,
        name=name,
        out_shape=tuple(out_shape),
        in_specs=[HBM_SPEC] * (2 * n) + [ANY_SPEC] * len(after),
        out_specs=tuple([SEM_SPEC] * (2 * ng) + [HBM_SPEC] * (2 * n) + [pl.BlockSpec(memory_space=pltpu.VMEM)]),
        input_output_aliases={i: 2 * ng + i for i in range(2 * n)},
        compiler_params=pltpu.CompilerParams(has_side_effects=EFFECT),
    )(*[pltpu.with_memory_space_constraint(b, pltpu.HBM) for b in flat + lands], *after)
    sems, thru, token = res[: 2 * ng], res[2 * ng : 2 * ng + 2 * n], res[-1]
    out, first = [], 0
    for gi, size in enumerate(sizes):
        out.append((sems[2 * gi], sems[2 * gi + 1], list(thru[first : first + size]),
                    list(thru[n + first : n + first + size])))
        first += size
    return out, token


def _push_wait(started, sliced, after, name):
    send_sems, recv_sems, bufs, lands = started
    n = len(bufs)

    def body(*refs):
        ins, lnd = refs[:n], refs[n : 2 * n]
        send_ref, recv_ref = refs[2 * n], refs[2 * n + 1]
        x, y, c = _place()
        for k, peer in enumerate(_peers(x, y, c)):
            for j in range(n):
                cp = pltpu.make_async_remote_copy(
                    src_ref=ins[j].at[_dev_index(peer)] if sliced else ins[j], dst_ref=lnd[j].at[_dev_index(peer)],
                    send_sem=send_ref.at[j * 7 + k], recv_sem=recv_ref.at[j * 7 + k],
                    device_id=peer, device_id_type=MESH)
                cp.wait_send()
                cp.wait_recv()

    res = pl.pallas_call(
        body,
        name=name,
        out_shape=tuple(pltpu.HBM(b.shape, b.dtype) for b in bufs + lands),
        in_specs=[HBM_SPEC] * (2 * n) + [SEM_SPEC, SEM_SPEC, pl.BlockSpec(memory_space=pl.ANY)],
        out_specs=tuple([HBM_SPEC] * (2 * n)),
        input_output_aliases={i: i for i in range(2 * n)},
        compiler_params=pltpu.CompilerParams(has_side_effects=EFFECT),
    )(*bufs, *lands, send_sems, recv_sems, after)
    return list(res[n:])


def _with_own_row(land, own, me):
    return lax.dynamic_update_index_in_dim(land, own, me, 0)


def _cols_full(g):
    return jnp.transpose(g, (1, 0, 2)).reshape(g.shape[1], -1)


def _rows_full(g):
    return g.reshape(-1, g.shape[2])


def _cols_parts(full, n=N_DEV):
    r = full.shape[0]
    return jnp.transpose(full.reshape(r, n, -1), (1, 0, 2)).astype(BF16)


def _rows_parts(full):
    return full.reshape(N_DEV, -1, full.shape[1]).astype(BF16)


def _block_diag(w):
    eye = jnp.eye(LRU_BLOCKS, dtype=w.dtype)
    return jnp.einsum("nkj,nm->nkmj", w, eye).reshape(LRU_WIDTH, LRU_WIDTH)


def _pad_rows(v, rows):
    flat = v.reshape(-1)
    return jnp.pad(flat, (0, rows * D_MODEL - flat.shape[0])).reshape(rows, D_MODEL)


def _my_cols(full, me, width):
    return lax.dynamic_slice_in_dim(full, me * width, width, axis=full.ndim - 1)


def kernel(x, c, w_ada, b_ada, norm_pre, norm_post, ffn1_w_gu, ffn1_w_down, w_in, rel_bias, conv_w, conv_b, lru_wa, lru_ba, lru_wx, lru_bx, lru_lambda, w_att_o, w_rec_o, w_out, ffn2_w_gu, ffn2_w_down, loss_target, m_w_ada, m_b_ada, m_norm_pre, m_norm_post, m_ffn1_w_gu, m_ffn1_w_down, m_w_in, m_rel_bias, m_conv_w, m_conv_b, m_lru_wa, m_lru_ba, m_lru_wx, m_lru_bx, m_lru_lambda, m_w_att_o, m_w_rec_o, m_w_out, m_ffn2_w_gu, m_ffn2_w_down, v_w_ada, v_b_ada, v_norm_pre, v_norm_post, v_ffn1_w_gu, v_ffn1_w_down, v_w_in, v_rel_bias, v_conv_w, v_conv_b, v_lru_wa, v_lru_ba, v_lru_wx, v_lru_bx, v_lru_lambda, v_w_att_o, v_w_rec_o, v_w_out, v_ffn2_w_gu, v_ffn2_w_down):
    weights = dict(w_ada=w_ada, b_ada=b_ada, norm_pre=norm_pre, norm_post=norm_post, ffn1_w_gu=ffn1_w_gu,
                   ffn1_w_down=ffn1_w_down, w_in=w_in, rel_bias=rel_bias, conv_w=conv_w, conv_b=conv_b,
                   lru_wa=lru_wa, lru_ba=lru_ba, lru_wx=lru_wx, lru_bx=lru_bx, lru_lambda=lru_lambda,
                   w_att_o=w_att_o, w_rec_o=w_rec_o, w_out=w_out, ffn2_w_gu=ffn2_w_gu, ffn2_w_down=ffn2_w_down)
    mom1 = dict(w_ada=m_w_ada, b_ada=m_b_ada, norm_pre=m_norm_pre, norm_post=m_norm_post, ffn1_w_gu=m_ffn1_w_gu,
                ffn1_w_down=m_ffn1_w_down, w_in=m_w_in, rel_bias=m_rel_bias, conv_w=m_conv_w, conv_b=m_conv_b,
                lru_wa=m_lru_wa, lru_ba=m_lru_ba, lru_wx=m_lru_wx, lru_bx=m_lru_bx, lru_lambda=m_lru_lambda,
                w_att_o=m_w_att_o, w_rec_o=m_w_rec_o, w_out=m_w_out, ffn2_w_gu=m_ffn2_w_gu, ffn2_w_down=m_ffn2_w_down)
    mom2 = dict(w_ada=v_w_ada, b_ada=v_b_ada, norm_pre=v_norm_pre, norm_post=v_norm_post, ffn1_w_gu=v_ffn1_w_gu,
                ffn1_w_down=v_ffn1_w_down, w_in=v_w_in, rel_bias=v_rel_bias, conv_w=v_conv_w, conv_b=v_conv_b,
                lru_wa=v_lru_wa, lru_ba=v_lru_ba, lru_wx=v_lru_wx, lru_bx=v_lru_bx, lru_lambda=v_lru_lambda,
                w_att_o=v_w_att_o, w_rec_o=v_w_rec_o, w_out=v_w_out, ffn2_w_gu=v_ffn2_w_gu, ffn2_w_down=v_ffn2_w_down)
    order = list(weights)
    big = ["ffn1_w_gu", "ffn1_w_down", "w_in", "w_att_o", "w_rec_o", "w_out", "ffn2_w_gu", "ffn2_w_down"]
    col_sharded = {"ffn1_w_gu", "w_in", "w_att_o", "ffn2_w_gu"}
    small = ["b_ada", "norm_pre", "norm_post", "rel_bias", "conv_w", "conv_b", "lru_wa", "lru_ba", "lru_wx",
             "lru_bx", "lru_lambda"]

    xi, yi, ci = _place()
    me = _dev_index((xi, yi, ci))
    x0 = x[0]
    target = loss_target[0]

    shards = {n: weights[n][0].astype(BF16) for n in big}
    full_of = lambda n, g: _cols_full(g) if n in col_sharded else _rows_full(g)

    pack = jnp.concatenate([c.reshape(-1), norm_pre.reshape(-1), norm_post.reshape(-1), conv_w.reshape(-1)])
    pack = jnp.pad(pack, (0, 3072 - pack.shape[0])).reshape(8, 384)
    got = _allgather_vmem(pack, "gather_small_inputs").reshape(N_DEV, 3072)
    c_all = got[:, :1024]
    unshard = lambda blk, rows: jnp.transpose(blk.reshape(N_DEV, rows, 128), (1, 0, 2)).reshape(rows, D_MODEL)
    g_pre = unshard(got[:, 1024:1408], 3)
    g_post = unshard(got[:, 1408:1792], 3)
    conv_taps = unshard(got[:, 1792:2304], 4)
    conv_w8 = jnp.concatenate([conv_taps, jnp.zeros((4, LRU_WIDTH), F32)], axis=0)

    mod_cols = _ada_fwd(c_all, w_ada[0], "ada_fwd")
    mod_all = _allgather_vmem(mod_cols, "gather_mod").reshape(N_DEV, N_DEV, 1152)
    mod = lax.dynamic_index_in_dim(mod_all, me, axis=1, keepdims=False).reshape(1, -1) + b_ada
    mod = mod.reshape(3, 3, 1, D_MODEL)

    w_bd = jnp.concatenate([_block_diag(lru_wa[0]), _block_diag(lru_wx[0])], axis=1).astype(BF16)
    bias = _bias_tile(rel_bias[0])

    res_w = (0.5, 1.0, 0.5)
    row = lambda v: v.reshape(1, -1)

    (w1_gu,) = _allgather_hbm([shards["ffn1_w_gu"]], "gather_ffn1_w_gu")
    weight_groups = [["ffn1_w_down"], ["w_in"], ["w_att_o", "w_rec_o", "w_out"], ["ffn2_w_gu", "ffn2_w_down"]]
    weights_started, started = _push_start([[shards[n] for n in g] for g in weight_groups], False,
                                           "gather_weights_start", after=(mod, w1_gu))
    full = {"ffn1_w_gu": _cols_full(w1_gu)}

    def gathered_group(gi, after):
        lands = _push_wait(weights_started[gi], False, after, f"gather_weights_wait{gi}")
        for n, land in zip(weight_groups[gi], lands):
            full[n] = full_of(n, jnp.where(is_me, shards[n][None], land))

    is_me = (jnp.arange(N_DEV) == me)[:, None, None]

    def ffn_fwd(xin, k, gi, tag, deps=()):
        h = _pre_fwd(xin, row(g_pre[k]), mod[k, 0], mod[k, 1], f"{tag}_pre", deps=deps)
        a, g, u = _ffn_up(h, full[f"{tag}_w_gu"], f"{tag}_up")
        if f"{tag}_w_down" not in full:
            gathered_group(gi, a)
        f = _matmul(a, full[f"{tag}_w_down"], "nn", F32, f"{tag}_down", tm=512, tk=D_FF)
        xout = _post_fwd(f, xin, row(g_post[k]), mod[k, 2], res_w[k], f"{tag}_post")
        return xout, (h, g, u, a, f)

    x1, saved1 = ffn_fwd(x0, 0, 0, "ffn1", deps=(started,))

    gathered_group(1, x1)
    w_in_p = jnp.concatenate([full["w_in"][:, 3 * ATT_WIDTH :], full["w_in"][:, : 3 * ATT_WIDTH]], axis=1)
    h2 = _pre_fwd(x1, row(g_pre[1]), mod[1, 0], mod[1, 1], "mix_pre")
    proj = _matmul(h2, w_in_p, "nn", F32, "mix_in")
    att_o = _attn_fwd(proj, bias, "attn_fwd")
    gathered_group(2, att_o)
    xc, xcb = _conv_fwd(proj, conv_w8, conv_b, "conv_fwd")
    pre = _matmul(xcb, w_bd, "nn", F32, "lru_gate_proj")
    a_t, u_t = _gates_fwd(pre, xc, lru_ba, lru_bx, lru_lambda, "lru_gates")
    hs, h_prev = _scan_fwd(a_t, u_t, "lru_scan")
    (rec_in,) = _rowwise(_recin_fn, "rec_in", [], [hs, (proj, LRU_WIDTH, 1)], [(LRU_WIDTH, BF16)])
    att = _matmul(att_o, full["w_att_o"], "nn", F32, "att_out")
    rec = _matmul(rec_in, full["w_rec_o"], "nn", F32, "rec_out")
    (merged,) = _rowwise(_merge_fn, "merge", [], [att, rec, (proj, LRU_WIDTH, 2), (proj, LRU_WIDTH, 3)],
                         [(D_MODEL, BF16)])
    f2 = _matmul(merged, full["w_out"], "nn", F32, "mix_out")
    x2 = _post_fwd(f2, x1, row(g_post[1]), mod[1, 2], res_w[1], "mix_post")

    gathered_group(3, x2)
    x3, saved3 = ffn_fwd(x2, 2, 2, "ffn2")

    dy, sq = _loss_stage(x3, target, "loss")
    loss = lax.psum(0.5 * jnp.sum(sq) / D_MODEL, ("x", "y", "c"))

    grads = {}
    dmod = [[None] * 3 for _ in range(3)]
    d_pre, d_post = [None] * 3, [None] * 3

    pending = []

    def exchange_start(names, tag, after=()):
        send = [grads[n] if grads[n].ndim == 3 else (_cols_parts if n in col_sharded else _rows_parts)(grads[n])
                for n in names]
        (group,), token = _push_start([send], True, f"exchange_{tag}_start", after=after)
        pending.append((names, send, group, tag))
        return token

    def exchange_finish(names, send, group, tag, after):
        lands = _push_wait(group, True, after, f"exchange_{tag}_wait")
        res = None
        for n, land, mine in zip(names, lands, send):
            res = _adamw_parts(land, mine, me, weights[n][0], mom1[n][0], mom2[n][0], f"adamw_{n}")
            out_g[n], out_d[n], out_m[n], out_v[n] = [r.reshape(weights[n].shape) for r in res]
        return res[0]

    out_g, out_d, out_m, out_v = {}, {}, {}, {}

    def ffn_bwd(xin, k, saved, dout, tag):
        h, g, u, a, f = saved
        w_gu, w_down = f"{tag}_w_gu", f"{tag}_w_down"
        df, d_post[k], dmod[k][2] = _post_bwd(f, row(g_post[k]), mod[k, 2], res_w[k], dout, f"{tag}_post_bwd")
        grads[w_down] = _matmul(a, df, "tn", BF16, f"{tag}_dw_down", tm=1408, tn=1024, tk=1024)
        started = exchange_start([w_down], w_down)
        dg, du = _ffn_up_bwd(df, full[w_down], g, u, f"{tag}_up_bwd", deps=(started,))
        dw_g = _matmul(h, dg, "tn", BF16, f"{tag}_dw_g", tm=1024, tn=1408, tk=1024)
        dw_u = _matmul(h, du, "tn", BF16, f"{tag}_dw_u", tm=1024, tn=1408, tk=1024)
        grads[w_gu] = jnp.concatenate([_cols_parts(dw_g, N_DEV // 2), _cols_parts(dw_u, N_DEV // 2)], axis=0)
        started = exchange_start([w_gu], w_gu)
        dh = _ffn_dh(dg, du, full[w_gu], f"{tag}_dh", deps=(started,))
        dx, d_pre[k], dmod[k][0], dmod[k][1] = _pre_bwd(xin, row(g_pre[k]), mod[k, 0], mod[k, 1], dh, dout,
                                                         f"{tag}_pre_bwd")
        return dx

    dx2 = ffn_bwd(x2, 2, saved3, dy, "ffn2")

    df2, d_post[1], dmod[1][2] = _post_bwd(f2, row(g_post[1]), mod[1, 2], res_w[1], dx2, "mix_post_bwd")
    dmerged = _matmul(df2, full["w_out"], "nt", F32, "mix_dmerged")
    grads["w_out"] = _matmul(merged, df2, "tn", BF16, "mix_dw_out", tm=1024, tn=1024, tk=1024)

    def merge_bwd(att, rec, g_att, g_rec, dm):
        _, vjp = jax.vjp(_merge_fn, att, rec, g_att, g_rec)
        return vjp(dm)

    datt, drec, dg_att, dg_rec = _rowwise(
        merge_bwd, "merge_bwd", [], [att, rec, (proj, LRU_WIDTH, 2), (proj, LRU_WIDTH, 3), dmerged],
        [(D_MODEL, BF16)] * 4)
    datt_o = _matmul(datt, full["w_att_o"], "nt", BF16, "att_out_bwd")
    grads["w_att_o"] = _matmul(att_o, datt, "tn", BF16, "dw_att_o", tm=512, tn=1024, tk=1024)
    drec_in = _matmul(drec, full["w_rec_o"], "nt", F32, "rec_out_bwd")
    grads["w_rec_o"] = _matmul(rec_in, drec, "tn", BF16, "dw_rec_o", tm=1024, tn=1024, tk=1024)
    started = exchange_start(["w_out", "w_att_o", "w_rec_o"], "mix_out")

    def recin_bwd(hs, yr, d):
        _, vjp = jax.vjp(_recin_fn, hs, yr)
        return vjp(d)

    dhs, dyr = _rowwise(recin_bwd, "rec_in_bwd", [], [hs, (proj, LRU_WIDTH, 1), drec_in],
                        [(LRU_WIDTH, F32), (LRU_WIDTH, BF16)], deps=(started,))
    g_t = _scan_bwd(a_t, dhs, "lru_scan_bwd")
    dpre, dxc_direct, d_ba, d_bx, d_lam = _gates_bwd(pre, xc, lru_ba, lru_bx, lru_lambda, g_t, h_prev,
                                                     "lru_gates_bwd")
    dxc_mm = _matmul(dpre, w_bd, "nt", F32, "lru_gate_proj_bwd")
    dw_bd = _matmul(xcb, dpre, "tn", F32, "dw_lru_gate", tm=1024, tn=1024, tk=1024)
    dxr, d_conv_w8, d_conv_b = _conv_bwd(proj, conv_w8, dxc_direct, dxc_mm, "conv_bwd")
    dq, dk, dv, dbias = _attn_bwd(proj, bias, datt_o, "attn_bwd")
    dproj = jnp.concatenate([dxr, dyr, dg_att, dg_rec, dq, dk, dv], axis=1)
    dw_in_p = _matmul(h2, dproj, "tn", BF16, "mix_dw_in", tm=1024, tn=1408, tk=1024)
    grads["w_in"] = jnp.concatenate([dw_in_p[:, OFF_Q:], dw_in_p[:, :OFF_Q]], axis=1)
    pack_mix = jnp.concatenate([d_conv_w8[:4], d_conv_b, d_ba, d_bx, d_lam, _pad_rows(_bias_grad(dbias), 3),
                                _lru_diag_blocks(dw_bd, "lru_diag_blocks").reshape(128, D_MODEL),
                                jnp.zeros((5, D_MODEL), F32)], axis=0)
    (mix_started,), started = _push_start([[pack_mix]], False, "small_grads_mix_start")
    started = exchange_start(["w_in"], "w_in", after=(started,))
    dh2 = _matmul(dproj, w_in_p, "nt", F32, "mix_dh", deps=(started,))
    dx1, d_pre[1], dmod[1][0], dmod[1][1] = _pre_bwd(x1, row(g_pre[1]), mod[1, 0], mod[1, 1], dh2, dx2, "mix_pre_bwd")

    dx0 = ffn_bwd(x0, 0, saved1, dx1, "ffn1")

    dmod_mine = jnp.concatenate([dmod[k][j] for k in range(3) for j in range(3)], axis=0)
    pack_norm = jnp.concatenate([dmod_mine, *d_pre, *d_post, jnp.zeros((1, D_MODEL), F32)], axis=0)
    (norm_started,), _ = _push_start([[pack_norm]], False, "small_grads_norm_start")

    def summed(started, pack, tag):
        (land,) = _push_wait(started, False, pack, f"small_grads_{tag}_wait")
        parts = jnp.where(is_me, pack[None], land)
        return parts, _sum_parts(parts, f"small_grads_{tag}_sum")

    _, total = summed(mix_started, pack_mix, "mix")
    grads["conv_w"] = _my_cols(total[0:4], me, 128)
    grads["conv_b"] = total[4:5]
    grads["lru_ba"] = total[5:6]
    grads["lru_bx"] = total[6:7]
    grads["lru_lambda"] = total[7:8]
    grads["rel_bias"] = total[8:11].reshape(-1)[: ATT_HEADS * (2 * MAX_REL + 1)].reshape(ATT_HEADS, -1)
    grads["lru_wa"] = total[11:75].reshape(LRU_BLOCKS, LRU_BLOCK, LRU_BLOCK)
    grads["lru_wx"] = total[75:139].reshape(LRU_BLOCKS, LRU_BLOCK, LRU_BLOCK)
    parts, total = summed(norm_started, pack_norm, "norm")
    grads["b_ada"] = total[0:9].reshape(1, -1)
    grads["norm_pre"] = _my_cols(total[9:12], me, 128)
    grads["norm_post"] = _my_cols(total[12:15], me, 128)
    dmod_all = parts[:, 0:9, :].reshape(N_DEV, 9 * D_MODEL)
    grads["w_ada"] = _ada_bwd(c_all, _my_cols(dmod_all, me, 1152), "ada_bwd")

    res = _adamw(grads["w_ada"], w_ada[0], m_w_ada[0], v_w_ada[0], "adamw_w_ada")
    out_g["w_ada"], out_d["w_ada"], out_m["w_ada"], out_v["w_ada"] = [r.reshape(w_ada.shape) for r in res]

    sizes = [int(np.prod(weights[n].shape)) for n in small]
    tot = sum(sizes)
    rows_small = -(-tot // (16 * D_MODEL)) * 16
    flat = lambda arrs: jnp.pad(jnp.concatenate([a.reshape(-1) for a in arrs]),
                                (0, rows_small * D_MODEL - tot)).reshape(rows_small, D_MODEL)
    res = _adamw(flat([grads[n] for n in small]), flat([weights[n] for n in small]),
                 flat([mom1[n] for n in small]), flat([mom2[n] for n in small]), "adamw_small", rows=rows_small)
    offs = np.cumsum([0] + sizes)
    for dst, r in zip((out_g, out_d, out_m, out_v), res):
        rf = r.reshape(-1)
        for i, n in enumerate(small):
            dst[n] = rf[offs[i] : offs[i + 1]].reshape(weights[n].shape)

    done = res[0]
    for names, send, group, tag in pending:
        done = exchange_finish(names, send, group, tag, done)

    return (loss, dx0[None], *[out_g[n] for n in order], *[out_d[n] for n in order],
            *[out_m[n] for n in order], *[out_v[n] for n in order])
```

```python
import functools

import jax
import jax.numpy as jnp
import numpy as np
from jax import lax
from jax.experimental import pallas as pl
from jax.experimental.pallas import tpu as pltpu

D_MODEL = 1024
D_FF = 2816
ATT_HEADS = 8
ATT_HEAD_DIM = 64
ATT_WIDTH = 512
CHUNK = 64
LEFT_CHUNKS = 8
MAX_REL = 128
LRU_WIDTH = 1024
LRU_BLOCKS = 16
LRU_BLOCK = 64
LRU_C = 8.0
EPS = 1e-6
PROJ_WIDTH = 5632
N_DEV = 8

ADAM_LR = 0.001
ADAM_B1 = 0.9
ADAM_B2 = 0.999
ADAM_EPS = 1e-08
ADAM_WD = 0.01
ADAM_STEP = 10

V7X_LANES = 128
V7X_SUBLANES = 8
V7X_VMEM_BYTES = 64 * 1024 * 1024
VMEM_LIMIT = V7X_VMEM_BYTES - 8 * 1024 * 1024

ATT_TQ = 256
NEG = -1e30
BF16 = jnp.bfloat16
F32 = jnp.float32
MESH = pl.DeviceIdType.MESH

OFF_Q = 4 * LRU_WIDTH
OFF_K = OFF_Q + ATT_WIDTH
OFF_V = OFF_K + ATT_WIDTH


def _cparams(**kw):
    return pltpu.CompilerParams(vmem_limit_bytes=VMEM_LIMIT, **kw)


def _pick(n, target, unit=V7X_LANES):
    best = None
    for t in range(unit, min(n, target) + 1, unit):
        if n % t == 0:
            best = t
    return n if best is None else best


_DIMS = {
    "nn": (((1,), (0,)), ((), ())),
    "nt": (((1,), (1,)), ((), ())),
    "tn": (((0,), (0,)), ((), ())),
}


ANY_SPEC = pl.BlockSpec(memory_space=pl.ANY)


def _matmul(a, b, mode, out_dtype, name, tm=1024, tn=512, tk=1408, deps=(), b_shift=0):
    n_deps = len(deps)
    if mode == "nn":
        (m, k), (k2, n) = a.shape, b.shape
    elif mode == "nt":
        (m, k), (n, k2) = a.shape, b.shape
    else:
        (k, m), (k2, n) = a.shape, b.shape
    assert k == k2, (a.shape, b.shape, mode)
    tm, tn, tk = _pick(m, tm), _pick(n, tn), _pick(k, tk)
    nk = k // tk
    dims = _DIMS[mode]

    def body(a_ref, b_ref, *rest):
        o_ref, scratch = rest[n_deps], rest[n_deps + 1 :]
        p = lax.dot_general(a_ref[...], b_ref[...], dims, preferred_element_type=F32)
        if nk == 1:
            o_ref[...] = p.astype(o_ref.dtype)
        else:
            acc = scratch[0]
            kk = pl.program_id(2)

            @pl.when(kk == 0)
            def _():
                acc[...] = p

            @pl.when(kk > 0)
            def _():
                acc[...] += p

            @pl.when(kk == nk - 1)
            def _():
                o_ref[...] = acc[...].astype(o_ref.dtype)

    if mode == "nn":
        a_spec = pl.BlockSpec((tm, tk), lambda i, j, kk: (i, kk))
        b_spec = pl.BlockSpec((tk, tn), lambda i, j, kk: (kk, j))
    elif mode == "nt":
        a_spec = pl.BlockSpec((tm, tk), lambda i, j, kk: (i, kk))
        b_spec = pl.BlockSpec((tn, tk), lambda i, j, kk: ((j + b_shift) % (n // tn), kk))
    else:
        a_spec = pl.BlockSpec((tk, tm), lambda i, j, kk: (kk, i))
        b_spec = pl.BlockSpec((tk, tn), lambda i, j, kk: (kk, j))
    return pl.pallas_call(
        body,
        name=name,
        grid=(m // tm, n // tn, nk),
        in_specs=[a_spec, b_spec] + [ANY_SPEC] * n_deps,
        out_specs=pl.BlockSpec((tm, tn), lambda i, j, kk: (i, j)),
        out_shape=jax.ShapeDtypeStruct((m, n), out_dtype),
        scratch_shapes=[pltpu.VMEM((tm, tn), F32)] if nk > 1 else [],
        compiler_params=_cparams(dimension_semantics=("parallel", "parallel", "arbitrary")),
    )(a, b, *deps)


def _rowwise(fn, name, params, tiles, outs, accs=(), ts=256, with_index=False, deps=()):
    norm = []
    for t in tiles:
        if not isinstance(t, tuple):
            t = (t, t.shape[1], 0)
        norm.append(t if len(t) == 4 else (*t, None))
    s = norm[0][0].shape[0]
    ts = min(ts, s)
    assert s % ts == 0 and ts % V7X_SUBLANES == 0
    steps = s // ts
    halo_blocks = ts // V7X_SUBLANES
    n_p, n_t, n_o = len(params), len(norm), len(outs)

    def body(*refs):
        i = pl.program_id(0)
        vals = [r[...] for r in refs[: n_p + n_t]]
        res = fn(i, steps, *vals) if with_index else fn(*vals)
        if not isinstance(res, (tuple, list)):
            res = (res,)
        first_out = n_p + n_t + len(deps)
        o_refs = refs[first_out : first_out + n_o]
        a_refs = refs[first_out + n_o :]
        for r, v in zip(o_refs, res[:n_o]):
            r[...] = v.astype(r.dtype)
        for r, v in zip(a_refs, res[n_o:]):
            _accumulate(r, v, i)

    in_specs = [pl.BlockSpec(p.shape, lambda i: (0, 0)) for p in params]
    for arr, w, cb, halo in norm:
        if halo is None:
            in_specs.append(pl.BlockSpec((ts, w), lambda i, cb=cb: (i, cb)))
        elif halo == "prev":
            in_specs.append(
                pl.BlockSpec((V7X_SUBLANES, w), lambda i, cb=cb: (jnp.maximum(i * halo_blocks - 1, 0), cb))
            )
        else:
            last = s // V7X_SUBLANES - 1
            in_specs.append(
                pl.BlockSpec((V7X_SUBLANES, w), lambda i, cb=cb: (jnp.minimum((i + 1) * halo_blocks, last), cb))
            )
    in_specs += [ANY_SPEC] * len(deps)
    out_specs = [pl.BlockSpec((ts, w), lambda i: (i, 0)) for w, _ in outs]
    out_specs += [pl.BlockSpec(shape, lambda i: (0, 0)) for shape in accs]
    out_shape = [jax.ShapeDtypeStruct((s, w), dt) for w, dt in outs]
    out_shape += [jax.ShapeDtypeStruct(shape, F32) for shape in accs]
    res = pl.pallas_call(
        body,
        name=name,
        grid=(steps,),
        in_specs=in_specs,
        out_specs=out_specs,
        out_shape=out_shape,
        compiler_params=_cparams(dimension_semantics=("arbitrary",)),
    )(*params, *[t[0] for t in norm], *deps)
    return res


def _accumulate(ref, val, step):
    @pl.when(step == 0)
    def _():
        ref[...] = val

    @pl.when(step > 0)
    def _():
        ref[...] += val


def _sigmoid(z):
    return jax.nn.sigmoid(z)


def _silu(z):
    return z * _sigmoid(z)


def _gelu(z):
    return 0.5 * z * (1.0 + jnp.tanh(0.7978845608028654 * (z + 0.044715 * (z * z * z))))


def _pre_fn(g, shift, scale, x):
    r = lax.rsqrt(jnp.mean(x * x, axis=-1, keepdims=True) + EPS)
    return ((x * r) * g) * (1.0 + scale) + shift


def _post_fn(res_w, g, gate, f, x):
    r = lax.rsqrt(jnp.mean(f * f, axis=-1, keepdims=True) + EPS)
    return x + (res_w * gate) * ((f * r) * g)


def _swiglu_fn(gu):
    return _silu(gu[:, :D_FF]) * gu[:, D_FF:]


def _gates_fn(ba, bx, lam, pre, xc):
    ra = _sigmoid(pre[:, :LRU_WIDTH] + ba)
    ia = _sigmoid(pre[:, LRU_WIDTH:] + bx)
    softplus = jnp.maximum(-lam, 0.0) + jnp.log1p(jnp.exp(-jnp.abs(lam)))
    log_a = (-LRU_C) * ra * softplus
    a = jnp.exp(log_a)
    mult = jnp.sqrt(-jnp.tanh(log_a) * (a * a + 1.0))
    return a, mult * (ia * xc)


def _recin_fn(hs, yr):
    return hs * _gelu(yr)


def _merge_fn(att, rec, g_att, g_rec):
    return _sigmoid(g_att) * att + _sigmoid(g_rec) * rec


def _rowsum(v):
    return jnp.sum(v, axis=0, keepdims=True)


def _pre_fwd(x, g, shift, scale, name, deps=()):
    (h,) = _rowwise(_pre_fn, name, [g, shift, scale], [x], [(D_MODEL, BF16)], deps=deps)
    return h


def _pre_bwd(x, g, shift, scale, dh, dres, name):
    def fn(g, shift, scale, x, dh, dres):
        _, vjp = jax.vjp(_pre_fn, g, shift, scale, x)
        dg, dshift, dscale, dx = vjp(dh)
        return dx + dres, dg, dshift, dscale

    row = (1, D_MODEL)
    return _rowwise(fn, name, [g, shift, scale], [x, dh, dres], [(D_MODEL, F32)], [row, row, row])


def _post_fwd(f, x, g, gate, res_w, name):
    (y,) = _rowwise(functools.partial(_post_fn, res_w), name, [g, gate], [f, x], [(D_MODEL, F32)])
    return y


def _post_bwd(f, g, gate, res_w, dy, name, deps=()):
    def fn(g, gate, f, dy):
        _, vjp = jax.vjp(lambda g, gate, f: _post_fn(res_w, g, gate, f, 0.0), g, gate, f)
        dg, dgate, df = vjp(dy)
        return df, dg, dgate

    row = (1, D_MODEL)
    return _rowwise(fn, name, [g, gate], [f, dy], [(D_MODEL, BF16)], [row, row], deps=deps)


def _loss_stage(y, target, name):
    def fn(y, t):
        diff = y - t
        return diff * (1.0 / D_MODEL), _rowsum(diff * diff)

    return _rowwise(fn, name, [], [y, target], [(D_MODEL, F32)], [(1, D_MODEL)])


FFN_TM = 512
FFN_TF = 1408


def _glu_fn(g, u):
    return _silu(g) * u


def _ffn_up(h, w_gu_t, name):
    s = h.shape[0]
    tm = min(FFN_TM, s)
    nf = D_FF // FFN_TF

    def body(h_ref, wg_ref, wu_ref, a_ref, g_ref, u_ref):
        hv = h_ref[...]
        g = lax.dot_general(hv, wg_ref[...], _DIMS["nt"], preferred_element_type=F32)
        u = lax.dot_general(hv, wu_ref[...], _DIMS["nt"], preferred_element_type=F32)
        a_ref[...] = _glu_fn(g, u).astype(a_ref.dtype)
        g_ref[...] = g.astype(g_ref.dtype)
        u_ref[...] = u.astype(u_ref.dtype)

    out = pl.BlockSpec((tm, FFN_TF), lambda i, j: (i, j))
    return pl.pallas_call(
        body,
        name=name,
        grid=(s // tm, nf),
        in_specs=[pl.BlockSpec((tm, D_MODEL), lambda i, j: (i, 0)),
                  pl.BlockSpec((FFN_TF, D_MODEL), lambda i, j: (j, 0)),
                  pl.BlockSpec((FFN_TF, D_MODEL), lambda i, j: (nf + j, 0))],
        out_specs=[out, out, out],
        out_shape=[jax.ShapeDtypeStruct((s, D_FF), BF16)] * 3,
        compiler_params=_cparams(dimension_semantics=("parallel", "arbitrary")),
    )(h, w_gu_t, w_gu_t)


def _ffn_up_bwd(df, w_down, g, u, name, deps=()):
    s = df.shape[0]
    tm = min(FFN_TM, s)

    def body(df_ref, wd_ref, g_ref, u_ref, *rest):
        dg_ref, du_ref = rest[len(deps) :]
        da = lax.dot_general(df_ref[...], wd_ref[...], _DIMS["nt"], preferred_element_type=F32)
        _, vjp = jax.vjp(_glu_fn, g_ref[...].astype(F32), u_ref[...].astype(F32))
        dg, du = vjp(da)
        dg_ref[...] = dg.astype(dg_ref.dtype)
        du_ref[...] = du.astype(du_ref.dtype)

    blk = pl.BlockSpec((tm, FFN_TF), lambda i, j: (i, j))
    return pl.pallas_call(
        body,
        name=name,
        grid=(s // tm, D_FF // FFN_TF),
        in_specs=[pl.BlockSpec((tm, D_MODEL), lambda i, j: (i, 0)),
                  pl.BlockSpec((FFN_TF, D_MODEL), lambda i, j: (j, 0)), blk, blk] + [ANY_SPEC] * len(deps),
        out_specs=[blk, blk],
        out_shape=[jax.ShapeDtypeStruct((s, D_FF), BF16)] * 2,
        compiler_params=_cparams(dimension_semantics=("parallel", "arbitrary")),
    )(df, w_down, g, u, *deps)


def _ffn_dh(dg, du, w_gu_t, name, deps=()):
    s = dg.shape[0]
    tm, tn = min(FFN_TM, s), 512

    def body(dg_ref, du_ref, wg_ref, wu_ref, *rest):
        o_ref = rest[len(deps)]
        p = jnp.dot(dg_ref[...], wg_ref[...], preferred_element_type=F32)
        o_ref[...] = p + jnp.dot(du_ref[...], wu_ref[...], preferred_element_type=F32)

    a_spec = pl.BlockSpec((tm, D_FF), lambda i, j: (i, 0))
    return pl.pallas_call(
        body,
        name=name,
        grid=(s // tm, D_MODEL // tn),
        in_specs=[a_spec, a_spec,
                  pl.BlockSpec((D_FF, tn), lambda i, j: (0, j)),
                  pl.BlockSpec((D_FF, tn), lambda i, j: (1, j))] + [ANY_SPEC] * len(deps),
        out_specs=pl.BlockSpec((tm, tn), lambda i, j: (i, j)),
        out_shape=jax.ShapeDtypeStruct((s, D_MODEL), F32),
        compiler_params=_cparams(dimension_semantics=("parallel", "arbitrary")),
    )(dg, du, w_gu_t, w_gu_t, *deps)


def _lru_diag_blocks(dw_bd, name):
    def body(w_ref, o_ref):
        for half in range(2):
            for n in range(LRU_BLOCKS):
                rows = slice(n * LRU_BLOCK, (n + 1) * LRU_BLOCK)
                cols = slice(half * LRU_WIDTH + n * LRU_BLOCK, half * LRU_WIDTH + (n + 1) * LRU_BLOCK)
                o_ref[half, rows, :] = w_ref[rows, cols]

    return pl.pallas_call(
        body, name=name, out_shape=jax.ShapeDtypeStruct((2, LRU_WIDTH, LRU_BLOCK), F32), compiler_params=_cparams()
    )(dw_bd)


def _swiglu_fwd(gu, name):
    (a,) = _rowwise(_swiglu_fn, name, [], [gu], [(D_FF, BF16)], ts=128)
    return a


def _swiglu_bwd(gu, da, name, deps=()):
    def fn(gu, da):
        _, vjp = jax.vjp(_swiglu_fn, gu)
        return vjp(da)[0]

    (dgu,) = _rowwise(fn, name, [], [gu, da], [(2 * D_FF, BF16)], ts=128, deps=deps)
    return dgu


def _shift_down(ext, j, rows):
    return pltpu.roll(ext, j, 0)[V7X_SUBLANES : V7X_SUBLANES + rows]


def _shift_up(ext, j, rows):
    return pltpu.roll(ext, ext.shape[0] - j, 0)[:rows] if j else ext[:rows]


def _conv_fwd(proj, w8, b, name):
    def fn(i, steps, w8, b, x, halo):
        halo = jnp.where(i > 0, halo, 0.0)
        ext = jnp.concatenate([halo, x], axis=0)
        acc = b + w8[3:4] * x
        for j in (1, 2, 3):
            acc = acc + w8[3 - j : 4 - j] * _shift_down(ext, j, x.shape[0])
        return acc, acc

    tiles = [(proj, LRU_WIDTH, 0), (proj, LRU_WIDTH, 0, "prev")]
    return _rowwise(fn, name, [w8, b], tiles, [(LRU_WIDTH, F32), (LRU_WIDTH, BF16)], with_index=True)


def _conv_bwd(proj, w8, d1, d2, name):
    def fn(i, steps, w8, x, halo, d1, d1n, d2, d2n):
        rows = x.shape[0]
        d = d1 + d2
        dn = jnp.where(i < steps - 1, d1n + d2n, 0.0)
        halo = jnp.where(i > 0, halo, 0.0)
        dext = jnp.concatenate([d, dn], axis=0)
        xext = jnp.concatenate([halo, x], axis=0)
        dx = w8[3:4] * d
        dw = [None] * 4
        dw[3] = _rowsum(d * x)
        for k in (1, 2, 3):
            dx = dx + w8[3 - k : 4 - k] * _shift_up(dext, k, rows)
            dw[3 - k] = _rowsum(d * _shift_down(xext, k, rows))
        dw8 = jnp.concatenate(dw + [jnp.zeros((4, LRU_WIDTH), F32)], axis=0)
        return dx, dw8, _rowsum(d)

    tiles = [(proj, LRU_WIDTH, 0), (proj, LRU_WIDTH, 0, "prev"), d1, (d1, LRU_WIDTH, 0, "next"),
             d2, (d2, LRU_WIDTH, 0, "next")]
    return _rowwise(fn, name, [w8], tiles, [(LRU_WIDTH, BF16)], [(8, LRU_WIDTH), (1, LRU_WIDTH)], with_index=True)


def _gates_fwd(pre, xc, ba, bx, lam, name):
    return _rowwise(_gates_fn, name, [ba, bx, lam], [pre, xc], [(LRU_WIDTH, F32), (LRU_WIDTH, F32)])


def _gates_bwd(pre, xc, ba, bx, lam, g, h_prev, name):
    def fn(ba, bx, lam, pre, xc, g, h_prev):
        _, vjp = jax.vjp(_gates_fn, ba, bx, lam, pre, xc)
        dba, dbx, dlam, dpre, dxc = vjp((g * h_prev, g))
        return dpre, dxc, dba, dbx, dlam

    row = (1, LRU_WIDTH)
    return _rowwise(fn, name, [ba, bx, lam], [pre, xc, g, h_prev],
                    [(2 * LRU_WIDTH, BF16), (LRU_WIDTH, F32)], [row, row, row])


SCAN_ROWS = 512


def _block_scan(a, b, row, reverse):
    for d in (1, 2, 4):
        if reverse:
            shift, keep = V7X_SUBLANES - d, row < V7X_SUBLANES - d
        else:
            shift, keep = d, row >= d
        a_s = pltpu.roll(a, shift, 0)
        b_s = pltpu.roll(b, shift, 0)
        b = jnp.where(keep, a * b_s + b, b)
        a = jnp.where(keep, a * a_s, a)
    return a, b


def _scan_fwd(a, u, name):
    s, w = a.shape
    ts = min(SCAN_ROWS, s)
    sub = ts // V7X_SUBLANES

    def body(a_ref, u_ref, h_ref, hp_ref, carry):
        @pl.when(pl.program_id(0) == 0)
        def _():
            carry[...] = jnp.zeros_like(carry)

        row = lax.broadcasted_iota(jnp.int32, (V7X_SUBLANES, w), 0)

        def step(j, c):
            rows = pl.ds(pl.multiple_of(j * V7X_SUBLANES, V7X_SUBLANES), V7X_SUBLANES)
            pa, pb = _block_scan(a_ref[rows, :], u_ref[rows, :], row, False)
            h = pb + pa * c
            h_ref[rows, :] = h
            hp_ref[rows, :] = jnp.where(row >= 1, pltpu.roll(h, 1, 0), c)
            return jnp.broadcast_to(h[V7X_SUBLANES - 1 :], (V7X_SUBLANES, w))

        carry[...] = lax.fori_loop(0, sub, step, carry[...])

    spec = pl.BlockSpec((ts, w), lambda i: (i, 0))
    return pl.pallas_call(
        body,
        name=name,
        grid=(s // ts,),
        in_specs=[spec, spec],
        out_specs=[spec, spec],
        out_shape=[jax.ShapeDtypeStruct((s, w), F32)] * 2,
        scratch_shapes=[pltpu.VMEM((V7X_SUBLANES, w), F32)],
        compiler_params=_cparams(dimension_semantics=("arbitrary",)),
    )(a, u)


def _scan_bwd(a, dh, name):
    s, w = a.shape
    ts = min(SCAN_ROWS, s)
    sub = ts // V7X_SUBLANES
    steps = s // ts

    def body(a_ref, d_ref, g_ref, carry):
        @pl.when(pl.program_id(0) == 0)
        def _():
            carry[...] = jnp.zeros_like(carry)

        row = lax.broadcasted_iota(jnp.int32, (V7X_SUBLANES, w), 0)

        def step(jj, c):
            j = sub - 1 - jj
            rows = pl.ds(pl.multiple_of(j * V7X_SUBLANES, V7X_SUBLANES), V7X_SUBLANES)
            av, dv = a_ref[rows, :], d_ref[rows, :]
            pa, pb = _block_scan(av, av * dv, row, True)
            big = pb + pa * c
            g_ref[rows, :] = dv + jnp.where(row < V7X_SUBLANES - 1, pltpu.roll(big, V7X_SUBLANES - 1, 0), c)
            return jnp.broadcast_to(big[:1], (V7X_SUBLANES, w))

        carry[...] = lax.fori_loop(0, sub, step, carry[...])

    spec = pl.BlockSpec((ts, w), lambda i: (steps - 1 - i, 0))
    return pl.pallas_call(
        body,
        name=name,
        grid=(steps,),
        in_specs=[spec, spec],
        out_specs=spec,
        out_shape=jax.ShapeDtypeStruct((s, w), F32),
        scratch_shapes=[pltpu.VMEM((V7X_SUBLANES, w), F32)],
        compiler_params=_cparams(dimension_semantics=("arbitrary",)),
    )(a, dh)


def _rel_index():
    i = np.arange(ATT_TQ)[:, None]
    j = np.arange(3 * ATT_TQ)[None, :]
    band = (j // CHUNK >= i // CHUNK) & (j // CHUNK <= i // CHUNK + LEFT_CHUNKS)
    return band


SKEW = 4 * ATT_TQ


def _skew_onehot():
    t = np.arange(SKEW)
    diag = np.where(t < 3 * ATT_TQ, -t, SKEW - t)
    idx = np.clip(diag + LEFT_CHUNKS * CHUNK, -MAX_REL, MAX_REL) + MAX_REL
    hit = (idx[:, None] == np.arange(2 * MAX_REL + 1)[None, :]) & (t[:, None] != 3 * ATT_TQ)
    return hit.astype(np.float32)


def _bias_tile(rel_bias):
    per_t = jnp.dot(rel_bias, jnp.asarray(_skew_onehot()).T, precision=lax.Precision.HIGHEST)
    flat = jnp.broadcast_to(per_t[:, None, :], (ATT_HEADS, ATT_TQ, SKEW)).reshape(ATT_HEADS, ATT_TQ * SKEW)
    tile = flat[:, : ATT_TQ * (SKEW - 1)].reshape(ATT_HEADS, ATT_TQ, SKEW - 1)[:, :, : 3 * ATT_TQ]
    first = (2 - np.arange(3))[:, None, None, None] * ATT_TQ
    seen = _rel_index()[None, None] & (np.arange(3 * ATT_TQ)[None, None, None, :] >= first)
    return jnp.where(jnp.asarray(seen), tile[None], NEG)


def _bias_grad(dbias):
    flat = jnp.pad(dbias, ((0, 0), (0, 0), (0, SKEW - 1 - 3 * ATT_TQ))).reshape(ATT_HEADS, ATT_TQ * (SKEW - 1))
    per_t = jnp.sum(jnp.pad(flat, ((0, 0), (0, ATT_TQ))).reshape(ATT_HEADS, ATT_TQ, SKEW), axis=1)
    return jnp.dot(per_t, jnp.asarray(_skew_onehot()), precision=lax.Precision.HIGHEST)


def _attn_specs(nt):
    qb, kb, vb = OFF_Q // V7X_LANES, OFF_K // V7X_LANES, OFF_V // V7X_LANES
    blk = (ATT_TQ, V7X_LANES)

    def qmap(base):
        return lambda hp, m: (jnp.minimum(m, nt - 1), base + hp)

    def wmap(base, back):
        return lambda hp, m: (jnp.clip(m - back, 0, nt - 1), base + hp)

    specs = [pl.BlockSpec(blk, qmap(qb))]
    specs += [pl.BlockSpec(blk, wmap(kb, back)) for back in (2, 1, 0)]
    specs += [pl.BlockSpec(blk, wmap(vb, back)) for back in (2, 1, 0)]
    return specs


ATT_SCALE = ATT_HEAD_DIM**-0.5


def _attn_exp(qh, kh, bias):
    s = lax.dot_general(qh, kh, _DIMS["nt"], preferred_element_type=F32) + bias
    e = jnp.exp(s - jnp.max(s, axis=-1, keepdims=True))
    return e, jnp.sum(e, axis=-1, keepdims=True)


def _attn_window(k0, k1, k2, v0, v1, v2):
    k = jnp.concatenate([k0[...], k1[...], k2[...]], axis=0).astype(BF16)
    v = jnp.concatenate([v0[...], v1[...], v2[...]], axis=0).astype(BF16)
    return k, v


def _bias_spec():
    return pl.BlockSpec((1, 2, ATT_TQ, 3 * ATT_TQ), lambda hp, m: (jnp.minimum(m, 2), hp, 0, 0))


def _attn_fwd(proj, bias, name):
    s = proj.shape[0]
    nt = s // ATT_TQ

    def body(q_ref, k0, k1, k2, v0, v1, v2, b_ref, o_ref):
        k, v = _attn_window(k0, k1, k2, v0, v1, v2)
        q = (q_ref[...] * ATT_SCALE).astype(BF16)
        for hh in range(2):
            cols = slice(hh * ATT_HEAD_DIM, (hh + 1) * ATT_HEAD_DIM)
            e, total = _attn_exp(q[:, cols], k[:, cols], b_ref[0, hh])
            o = jnp.dot(e.astype(BF16), v[:, cols], preferred_element_type=F32) / total
            o_ref[:, cols] = o.astype(o_ref.dtype)

    specs = _attn_specs(nt) + [_bias_spec()]
    return pl.pallas_call(
        body,
        name=name,
        grid=(ATT_HEADS // 2, nt),
        in_specs=specs,
        out_specs=pl.BlockSpec((ATT_TQ, V7X_LANES), lambda hp, m: (m, hp)),
        out_shape=jax.ShapeDtypeStruct((s, ATT_WIDTH), BF16),
        compiler_params=_cparams(dimension_semantics=("parallel", "arbitrary")),
    )(proj, proj, proj, proj, proj, proj, proj, bias)


def _attn_bwd(proj, bias, do, name):
    s = proj.shape[0]
    nt = s // ATT_TQ
    win = 3 * ATT_TQ

    def body(q_ref, k0, k1, k2, v0, v1, v2, do_ref, b_ref, dq_ref, dk_ref, dv_ref, db_ref, dk_acc, dv_acc):
        m = pl.program_id(1)

        @pl.when(m == 0)
        def _():
            dk_acc[...] = jnp.zeros_like(dk_acc)
            dv_acc[...] = jnp.zeros_like(dv_acc)
            db_ref[...] = jnp.zeros_like(db_ref)

        @pl.when(m < nt)
        def _():
            k, v = _attn_window(k0, k1, k2, v0, v1, v2)
            q = (q_ref[...] * ATT_SCALE).astype(BF16)
            dout = do_ref[...]
            for hh in range(2):
                cols = slice(hh * ATT_HEAD_DIM, (hh + 1) * ATT_HEAD_DIM)
                qh, kh, vh, doh = q[:, cols], k[:, cols], v[:, cols], dout[:, cols]
                e, total = _attn_exp(qh, kh, b_ref[0, hh])
                p = e / total
                dvh = lax.dot_general(p.astype(BF16), doh, _DIMS["tn"], preferred_element_type=F32)
                dp = lax.dot_general(doh, vh, _DIMS["nt"], preferred_element_type=F32)
                ds = p * (dp - jnp.sum(dp * p, axis=-1, keepdims=True))
                db_ref[hh] += ds
                dsb = ds.astype(BF16)
                dqh = jnp.dot(dsb, kh, preferred_element_type=F32) * ATT_SCALE
                dkh = lax.dot_general(dsb, qh, _DIMS["tn"], preferred_element_type=F32)
                dq_ref[:, cols] = dqh.astype(dq_ref.dtype)
                dk_acc[:, cols] += dkh
                dv_acc[:, cols] += dvh

        dk_ref[...] = dk_acc[:ATT_TQ].astype(dk_ref.dtype)
        dv_ref[...] = dv_acc[:ATT_TQ].astype(dv_ref.dtype)
        for acc in (dk_acc, dv_acc):
            rest = acc[ATT_TQ:]
            acc[: win - ATT_TQ] = rest
            acc[win - ATT_TQ :] = jnp.zeros((ATT_TQ, V7X_LANES), F32)

    blk = (ATT_TQ, V7X_LANES)
    specs = _attn_specs(nt)
    specs.append(pl.BlockSpec(blk, lambda hp, m: (jnp.minimum(m, nt - 1), hp)))
    specs.append(_bias_spec())
    done = lambda hp, m: (jnp.maximum(m - 2, 0), hp)
    out_specs = [
        pl.BlockSpec(blk, lambda hp, m: (jnp.minimum(m, nt - 1), hp)),
        pl.BlockSpec(blk, done),
        pl.BlockSpec(blk, done),
        pl.BlockSpec((2, ATT_TQ, win), lambda hp, m: (hp, 0, 0)),
    ]
    out_shape = [jax.ShapeDtypeStruct((s, ATT_WIDTH), BF16)] * 3
    out_shape.append(jax.ShapeDtypeStruct((ATT_HEADS, ATT_TQ, win), F32))
    return pl.pallas_call(
        body,
        name=name,
        grid=(ATT_HEADS // 2, nt + 2),
        in_specs=specs,
        out_specs=out_specs,
        out_shape=out_shape,
        scratch_shapes=[pltpu.VMEM((win, V7X_LANES), F32), pltpu.VMEM((win, V7X_LANES), F32)],
        compiler_params=_cparams(dimension_semantics=("arbitrary", "arbitrary")),
    )(proj, proj, proj, proj, proj, proj, proj, do, bias)


def _ada_fwd(c_all, w, name):
    def body(c_ref, w_ref, o_ref):
        act = _silu(c_ref[...]).astype(BF16)
        o_ref[...] = jnp.dot(act, w_ref[...].astype(BF16), preferred_element_type=F32)

    return pl.pallas_call(
        body, name=name, out_shape=jax.ShapeDtypeStruct((c_all.shape[0], w.shape[1]), F32), compiler_params=_cparams()
    )(c_all, w)


def _ada_bwd(c_all, dmod, name):
    def body(c_ref, d_ref, o_ref):
        act = _silu(c_ref[...])
        o_ref[...] = lax.dot_general(act, d_ref[...], _DIMS["tn"], preferred_element_type=F32,
                                     precision=lax.Precision.HIGHEST)

    return pl.pallas_call(
        body, name=name, out_shape=jax.ShapeDtypeStruct((c_all.shape[1], dmod.shape[1]), F32), compiler_params=_cparams()
    )(c_all, dmod)


def _adamw_parts(landed, sent, me, w, m, v, name, rows=256):
    r, c = w.shape
    tr = _pick(r, rows, 16)

    def body(me_ref, g_ref, own_ref, w_ref, m_ref, v_ref, go_ref, d_ref, mo_ref, vo_ref):
        mine = me_ref[0]
        grad = jnp.zeros((tr, c), F32)
        for d in range(N_DEV):
            grad = grad + jnp.where(mine == d, own_ref[0], g_ref[d]).astype(F32)
        _adamw_update(grad, w_ref, m_ref, v_ref, go_ref, d_ref, mo_ref, vo_ref)

    spec = pl.BlockSpec((tr, c), lambda i, me_ref: (i, 0))
    return pl.pallas_call(
        body,
        name=name,
        grid_spec=pltpu.PrefetchScalarGridSpec(
            num_scalar_prefetch=1,
            grid=(r // tr,),
            in_specs=[pl.BlockSpec((N_DEV, tr, c), lambda i, me_ref: (0, i, 0)),
                      pl.BlockSpec((1, tr, c), lambda i, me_ref: (me_ref[0], i, 0)), spec, spec, spec],
            out_specs=[spec] * 4,
        ),
        out_shape=[jax.ShapeDtypeStruct((r, c), F32)] * 4,
        compiler_params=_cparams(dimension_semantics=("parallel",)),
    )(me.reshape(1).astype(jnp.int32), landed, sent, w, m, v)


def _adamw_update(grad, w_ref, m_ref, v_ref, go_ref, d_ref, mo_ref, vo_ref):
    m2 = ADAM_B1 * m_ref[...] + (1.0 - ADAM_B1) * grad
    v2 = ADAM_B2 * v_ref[...] + (1.0 - ADAM_B2) * (grad * grad)
    m_hat = m2 / (1.0 - ADAM_B1**ADAM_STEP)
    v_hat = v2 / (1.0 - ADAM_B2**ADAM_STEP)
    go_ref[...] = grad
    d_ref[...] = -ADAM_LR * (m_hat / (jnp.sqrt(v_hat) + ADAM_EPS) + ADAM_WD * w_ref[...])
    mo_ref[...] = m2
    vo_ref[...] = v2


def _adamw(g, w, m, v, name, rows=256):
    r, c = w.shape
    tr = _pick(r, rows, 16)

    def body(g_ref, w_ref, m_ref, v_ref, go_ref, d_ref, mo_ref, vo_ref):
        _adamw_update(g_ref[...], w_ref, m_ref, v_ref, go_ref, d_ref, mo_ref, vo_ref)

    spec = pl.BlockSpec((tr, c), lambda i: (i, 0))
    return pl.pallas_call(
        body,
        name=name,
        grid=(r // tr,),
        in_specs=[spec, spec, spec, spec],
        out_specs=[spec] * 4,
        out_shape=[jax.ShapeDtypeStruct((r, c), F32)] * 4,
        compiler_params=_cparams(dimension_semantics=("parallel",)),
    )(g, w, m, v)


def _sum_parts(parts, name):
    def body(p_ref, o_ref):
        acc = p_ref[0]
        for d in range(1, N_DEV):
            acc = acc + p_ref[d]
        o_ref[...] = acc

    return pl.pallas_call(
        body, name=name, out_shape=jax.ShapeDtypeStruct(parts.shape[1:], F32), compiler_params=_cparams()
    )(parts)


def _place():
    x, y, c = lax.axis_index("x"), lax.axis_index("y"), lax.axis_index("c")
    return x, y, c


def _dev_index(p):
    return 4 * p[0] + 2 * p[1] + p[2]


def _allgather_vmem(shard, name):
    m_per, n = shard.shape

    def body(x_ref, out_ref, send_sems, recv_sems, local_sem):
        x, y, c = _place()
        me, sibling = (x, y, c), (x, y, 1 - c)
        chips = [(1 - x, y), (x, 1 - y), (1 - x, 1 - y)]

        def rows(p):
            return out_ref.at[pl.ds(_dev_index(p) * m_per, m_per), :]

        def copy(k, block, to, src=None):
            return pltpu.make_async_remote_copy(
                src_ref=rows(block) if src is None else src, dst_ref=rows(block),
                send_sem=send_sems.at[k], recv_sem=recv_sems.at[k], device_id=to, device_id_type=MESH)

        mine = pltpu.make_async_copy(x_ref, rows(me), local_sem)
        mine.start()
        first = [copy(0, me, sibling, src=x_ref)]
        first += [copy(1 + j, me, (*chip, c), src=x_ref) for j, chip in enumerate(chips)]
        for cp in first:
            cp.start()
        passed = [copy(4 + j, (*chip, c), sibling) for j, chip in enumerate(chips)]
        for j, chip in enumerate(chips):
            copy(1 + j, (*chip, c), me).wait_recv()
            passed[j].start()
        copy(0, sibling, me).wait_recv()
        for j, chip in enumerate(chips):
            copy(4 + j, (*chip, 1 - c), me).wait_recv()
        for cp in first + passed:
            cp.wait_send()
        mine.wait()

    return pl.pallas_call(
        body,
        name=name,
        out_shape=jax.ShapeDtypeStruct((N_DEV * m_per, n), shard.dtype),
        in_specs=[pl.BlockSpec(memory_space=pltpu.VMEM)],
        out_specs=pl.BlockSpec(memory_space=pltpu.VMEM),
        scratch_shapes=[pltpu.SemaphoreType.DMA((7,)), pltpu.SemaphoreType.DMA((7,)), pltpu.SemaphoreType.DMA],
        compiler_params=_cparams(),
    )(shard)


def _allgather_hbm(shards, name):
    n = len(shards)

    def body(*refs):
        ins, outs = refs[:n], refs[n : 2 * n]
        send_sems, recv_sems, local_sems = refs[2 * n :]
        x, y, c = _place()
        me, sibling = (x, y, c), (x, y, 1 - c)
        chips = [(1 - x, y), (x, 1 - y), (1 - x, 1 - y)]

        def copy(a, k, block, to, src=None):
            dst = outs[a].at[_dev_index(block)]
            return pltpu.make_async_remote_copy(
                src_ref=dst if src is None else src, dst_ref=dst,
                send_sem=send_sems.at[a * 7 + k], recv_sem=recv_sems.at[a * 7 + k], device_id=to, device_id_type=MESH)

        mine = [pltpu.make_async_copy(ins[a], outs[a].at[_dev_index(me)], local_sems.at[a]) for a in range(n)]
        for cp in mine:
            cp.start()
        first = []
        for a in range(n):
            first.append(copy(a, 0, me, sibling, src=ins[a]))
            first += [copy(a, 1 + j, me, (*chip, c), src=ins[a]) for j, chip in enumerate(chips)]
        for cp in first:
            cp.start()
        passed = []
        for j, chip in enumerate(chips):
            for a in range(n):
                copy(a, 1 + j, (*chip, c), me).wait_recv()
                cp = copy(a, 4 + j, (*chip, c), sibling)
                cp.start()
                passed.append(cp)
        for a in range(n):
            copy(a, 0, sibling, me).wait_recv()
        for j, chip in enumerate(chips):
            for a in range(n):
                copy(a, 4 + j, (*chip, 1 - c), me).wait_recv()
        for cp in first + passed:
            cp.wait_send()
        for cp in mine:
            cp.wait()

    any_spec = pl.BlockSpec(memory_space=pl.ANY)
    return pl.pallas_call(
        body,
        name=name,
        out_shape=[jax.ShapeDtypeStruct((N_DEV, *s.shape), s.dtype) for s in shards],
        in_specs=[any_spec] * n,
        out_specs=[any_spec] * n,
        scratch_shapes=[pltpu.SemaphoreType.DMA((7 * n,)), pltpu.SemaphoreType.DMA((7 * n,)),
                        pltpu.SemaphoreType.DMA((n,))],
        compiler_params=_cparams(),
    )(*shards)


def _exchange_hbm(bufs, name):
    n = len(bufs)

    def body(*refs):
        ins, outs = refs[:n], refs[n : 2 * n]
        send_sems, recv_sems, local_sems = refs[2 * n :]
        x, y, c = _place()
        me = _dev_index((x, y, c))
        mine = [pltpu.make_async_copy(ins[a].at[me], outs[a].at[me], local_sems.at[a]) for a in range(n)]
        for cp in mine:
            cp.start()
        def peer_of(k):
            return (1 - x if k & 4 else x, 1 - y if k & 2 else y, 1 - c if k & 1 else c)

        copies = []
        for k in range(1, N_DEV):
            peer = peer_of(k)
            for a in range(n):
                copies.append(pltpu.make_async_remote_copy(
                    src_ref=ins[a].at[_dev_index(peer)], dst_ref=outs[a].at[me],
                    send_sem=send_sems.at[a * 7 + k - 1], recv_sem=recv_sems.at[a * 7 + k - 1],
                    device_id=peer, device_id_type=MESH))
        for cp in copies:
            cp.start()
        for k in range(1, N_DEV):
            peer = peer_of(k)
            for a in range(n):
                pltpu.make_async_remote_copy(
                    src_ref=ins[a].at[me], dst_ref=outs[a].at[_dev_index(peer)],
                    send_sem=send_sems.at[a * 7 + k - 1], recv_sem=recv_sems.at[a * 7 + k - 1],
                    device_id=peer, device_id_type=MESH).wait_recv()
        for cp in copies:
            cp.wait_send()
        for cp in mine:
            cp.wait()

    any_spec = pl.BlockSpec(memory_space=pl.ANY)
    return pl.pallas_call(
        body,
        name=name,
        out_shape=[jax.ShapeDtypeStruct(b.shape, b.dtype) for b in bufs],
        in_specs=[any_spec] * n,
        out_specs=[any_spec] * n,
        scratch_shapes=[pltpu.SemaphoreType.DMA((7 * n,)), pltpu.SemaphoreType.DMA((7 * n,)),
                        pltpu.SemaphoreType.DMA((n,))],
        compiler_params=_cparams(),
    )(*bufs)


HBM_SPEC = pl.BlockSpec(memory_space=pltpu.HBM)
SEM_SPEC = pl.BlockSpec(memory_space=pltpu.SEMAPHORE)
EFFECT = pltpu.SideEffectType.DATAFLOW_SIDE_EFFECTING


def _peers(x, y, c):
    return [(1 - x if k & 4 else x, 1 - y if k & 2 else y, 1 - c if k & 1 else c) for k in range(1, N_DEV)]


def _push_start(groups, sliced, name, after=()):
    flat = [b for g in groups for b in g]
    n, ng = len(flat), len(groups)
    sizes = [len(g) for g in groups]
    lands = [lax.empty(b.shape if sliced else (N_DEV, *b.shape), b.dtype) for b in flat]

    def body(*refs):
        ins, lnd = refs[:n], refs[n : 2 * n]
        sems = refs[2 * n + len(after) : 2 * n + len(after) + 2 * ng]
        token = refs[-1]
        x, y, c = _place()
        me = _dev_index((x, y, c))
        first = 0
        for gi, size in enumerate(sizes):
            for k, peer in enumerate(_peers(x, y, c)):
                for j in range(first, first + size):
                    sem = (j - first) * 7 + k
                    pltpu.make_async_remote_copy(
                        src_ref=ins[j].at[_dev_index(peer)] if sliced else ins[j], dst_ref=lnd[j].at[me],
                        send_sem=sems[2 * gi].at[sem], recv_sem=sems[2 * gi + 1].at[sem],
                        device_id=peer, device_id_type=MESH).start()
            first += size
        token[...] = jnp.zeros_like(token)

    out_shape = []
    for size in sizes:
        out_shape += [pltpu.SemaphoreType.DMA((7 * size,)), pltpu.SemaphoreType.DMA((7 * size,))]
    out_shape += [pltpu.HBM(b.shape, b.dtype) for b in flat + lands]
    out_shape.append(jax.ShapeDtypeStruct((V7X_SUBLANES, V7X_LANES), F32))
    res = pl.pallas_call(
        body,
        name=name,
        out_shape=tuple(out_shape),
        in_specs=[HBM_SPEC] * (2 * n) + [ANY_SPEC] * len(after),
        out_specs=tuple([SEM_SPEC] * (2 * ng) + [HBM_SPEC] * (2 * n) + [pl.BlockSpec(memory_space=pltpu.VMEM)]),
        input_output_aliases={i: 2 * ng + i for i in range(2 * n)},
        compiler_params=pltpu.CompilerParams(has_side_effects=EFFECT),
    )(*[pltpu.with_memory_space_constraint(b, pltpu.HBM) for b in flat + lands], *after)
    sems, thru, token = res[: 2 * ng], res[2 * ng : 2 * ng + 2 * n], res[-1]
    out, first = [], 0
    for gi, size in enumerate(sizes):
        out.append((sems[2 * gi], sems[2 * gi + 1], list(thru[first : first + size]),
                    list(thru[n + first : n + first + size])))
        first += size
    return out, token


def _push_wait(started, sliced, after, name):
    send_sems, recv_sems, bufs, lands = started
    n = len(bufs)

    def body(*refs):
        ins, lnd = refs[:n], refs[n : 2 * n]
        send_ref, recv_ref = refs[2 * n], refs[2 * n + 1]
        x, y, c = _place()
        for k, peer in enumerate(_peers(x, y, c)):
            for j in range(n):
                cp = pltpu.make_async_remote_copy(
                    src_ref=ins[j].at[_dev_index(peer)] if sliced else ins[j], dst_ref=lnd[j].at[_dev_index(peer)],
                    send_sem=send_ref.at[j * 7 + k], recv_sem=recv_ref.at[j * 7 + k],
                    device_id=peer, device_id_type=MESH)
                cp.wait_send()
                cp.wait_recv()

    res = pl.pallas_call(
        body,
        name=name,
        out_shape=tuple(pltpu.HBM(b.shape, b.dtype) for b in bufs + lands),
        in_specs=[HBM_SPEC] * (2 * n) + [SEM_SPEC, SEM_SPEC, pl.BlockSpec(memory_space=pl.ANY)],
        out_specs=tuple([HBM_SPEC] * (2 * n)),
        input_output_aliases={i: i for i in range(2 * n)},
        compiler_params=pltpu.CompilerParams(has_side_effects=EFFECT),
    )(*bufs, *lands, send_sems, recv_sems, after)
    return list(res[n:])


def _with_own_row(land, own, me):
    return lax.dynamic_update_index_in_dim(land, own, me, 0)


def _cols_full(g):
    return jnp.transpose(g, (1, 0, 2)).reshape(g.shape[1], -1)


def _rows_full(g):
    return g.reshape(-1, g.shape[2])


def _cols_parts(full, n=N_DEV):
    r = full.shape[0]
    return jnp.transpose(full.reshape(r, n, -1), (1, 0, 2)).astype(BF16)


def _rows_parts(full):
    return full.reshape(N_DEV, -1, full.shape[1]).astype(BF16)


def _block_diag(w):
    eye = jnp.eye(LRU_BLOCKS, dtype=w.dtype)
    return jnp.einsum("nkj,nm->nkmj", w, eye).reshape(LRU_WIDTH, LRU_WIDTH)


def _pad_rows(v, rows):
    flat = v.reshape(-1)
    return jnp.pad(flat, (0, rows * D_MODEL - flat.shape[0])).reshape(rows, D_MODEL)


def _my_cols(full, me, width):
    return lax.dynamic_slice_in_dim(full, me * width, width, axis=full.ndim - 1)


def kernel(x, c, w_ada, b_ada, norm_pre, norm_post, ffn1_w_gu, ffn1_w_down, w_in, rel_bias, conv_w, conv_b, lru_wa, lru_ba, lru_wx, lru_bx, lru_lambda, w_att_o, w_rec_o, w_out, ffn2_w_gu, ffn2_w_down, loss_target, m_w_ada, m_b_ada, m_norm_pre, m_norm_post, m_ffn1_w_gu, m_ffn1_w_down, m_w_in, m_rel_bias, m_conv_w, m_conv_b, m_lru_wa, m_lru_ba, m_lru_wx, m_lru_bx, m_lru_lambda, m_w_att_o, m_w_rec_o, m_w_out, m_ffn2_w_gu, m_ffn2_w_down, v_w_ada, v_b_ada, v_norm_pre, v_norm_post, v_ffn1_w_gu, v_ffn1_w_down, v_w_in, v_rel_bias, v_conv_w, v_conv_b, v_lru_wa, v_lru_ba, v_lru_wx, v_lru_bx, v_lru_lambda, v_w_att_o, v_w_rec_o, v_w_out, v_ffn2_w_gu, v_ffn2_w_down):
    weights = dict(w_ada=w_ada, b_ada=b_ada, norm_pre=norm_pre, norm_post=norm_post, ffn1_w_gu=ffn1_w_gu,
                   ffn1_w_down=ffn1_w_down, w_in=w_in, rel_bias=rel_bias, conv_w=conv_w, conv_b=conv_b,
                   lru_wa=lru_wa, lru_ba=lru_ba, lru_wx=lru_wx, lru_bx=lru_bx, lru_lambda=lru_lambda,
                   w_att_o=w_att_o, w_rec_o=w_rec_o, w_out=w_out, ffn2_w_gu=ffn2_w_gu, ffn2_w_down=ffn2_w_down)
    mom1 = dict(w_ada=m_w_ada, b_ada=m_b_ada, norm_pre=m_norm_pre, norm_post=m_norm_post, ffn1_w_gu=m_ffn1_w_gu,
                ffn1_w_down=m_ffn1_w_down, w_in=m_w_in, rel_bias=m_rel_bias, conv_w=m_conv_w, conv_b=m_conv_b,
                lru_wa=m_lru_wa, lru_ba=m_lru_ba, lru_wx=m_lru_wx, lru_bx=m_lru_bx, lru_lambda=m_lru_lambda,
                w_att_o=m_w_att_o, w_rec_o=m_w_rec_o, w_out=m_w_out, ffn2_w_gu=m_ffn2_w_gu, ffn2_w_down=m_ffn2_w_down)
    mom2 = dict(w_ada=v_w_ada, b_ada=v_b_ada, norm_pre=v_norm_pre, norm_post=v_norm_post, ffn1_w_gu=v_ffn1_w_gu,
                ffn1_w_down=v_ffn1_w_down, w_in=v_w_in, rel_bias=v_rel_bias, conv_w=v_conv_w, conv_b=v_conv_b,
                lru_wa=v_lru_wa, lru_ba=v_lru_ba, lru_wx=v_lru_wx, lru_bx=v_lru_bx, lru_lambda=v_lru_lambda,
                w_att_o=v_w_att_o, w_rec_o=v_w_rec_o, w_out=v_w_out, ffn2_w_gu=v_ffn2_w_gu, ffn2_w_down=v_ffn2_w_down)
    order = list(weights)
    big = ["ffn1_w_gu", "ffn1_w_down", "w_in", "w_att_o", "w_rec_o", "w_out", "ffn2_w_gu", "ffn2_w_down"]
    col_sharded = {"ffn1_w_gu", "w_in", "w_att_o", "ffn2_w_gu"}
    small = ["b_ada", "norm_pre", "norm_post", "rel_bias", "conv_w", "conv_b", "lru_wa", "lru_ba", "lru_wx",
             "lru_bx", "lru_lambda"]

    xi, yi, ci = _place()
    me = _dev_index((xi, yi, ci))
    x0 = x[0]
    target = loss_target[0]

    transposed = {"ffn1_w_gu", "w_in", "ffn2_w_gu"}
    local = lambda n, arr: jnp.transpose(arr[0]) if n in transposed else arr[0]
    shards = {n: local(n, weights[n]).astype(BF16) for n in big}
    full_of = lambda n, g: _cols_full(g) if n == "w_att_o" else _rows_full(g)

    pack = jnp.concatenate([c.reshape(-1), norm_pre.reshape(-1), norm_post.reshape(-1), conv_w.reshape(-1)])
    pack = jnp.pad(pack, (0, 3072 - pack.shape[0])).reshape(8, 384)
    got = _allgather_vmem(pack, "gather_small_inputs").reshape(N_DEV, 3072)
    c_all = got[:, :1024]
    unshard = lambda blk, rows: jnp.transpose(blk.reshape(N_DEV, rows, 128), (1, 0, 2)).reshape(rows, D_MODEL)
    g_pre = unshard(got[:, 1024:1408], 3)
    g_post = unshard(got[:, 1408:1792], 3)
    conv_taps = unshard(got[:, 1792:2304], 4)
    conv_w8 = jnp.concatenate([conv_taps, jnp.zeros((4, LRU_WIDTH), F32)], axis=0)

    mod_cols = _ada_fwd(c_all, w_ada[0], "ada_fwd")
    mod_all = _allgather_vmem(mod_cols, "gather_mod").reshape(N_DEV, N_DEV, 1152)
    mod = lax.dynamic_index_in_dim(mod_all, me, axis=1, keepdims=False).reshape(1, -1) + b_ada
    mod = mod.reshape(3, 3, 1, D_MODEL)

    w_bd = jnp.concatenate([_block_diag(lru_wa[0]), _block_diag(lru_wx[0])], axis=1).astype(BF16)
    bias = _bias_tile(rel_bias[0])

    res_w = (0.5, 1.0, 0.5)
    row = lambda v: v.reshape(1, -1)

    (w1_gu,) = _allgather_hbm([shards["ffn1_w_gu"]], "gather_ffn1_w_gu")
    weight_groups = [["ffn1_w_down"], ["w_in"], ["w_att_o", "w_rec_o", "w_out"], ["ffn2_w_gu", "ffn2_w_down"]]
    weights_started, started = _push_start([[shards[n] for n in g] for g in weight_groups], False,
                                           "gather_weights_start", after=(mod, w1_gu))
    full = {"ffn1_w_gu": _rows_full(w1_gu)}

    def gathered_group(gi, after):
        lands = _push_wait(weights_started[gi], False, after, f"gather_weights_wait{gi}")
        for n, land in zip(weight_groups[gi], lands):
            full[n] = full_of(n, jnp.where(is_me, shards[n][None], land))

    is_me = (jnp.arange(N_DEV) == me)[:, None, None]

    def ffn_fwd(xin, k, gi, tag, deps=()):
        h = _pre_fwd(xin, row(g_pre[k]), mod[k, 0], mod[k, 1], f"{tag}_pre", deps=deps)
        a, g, u = _ffn_up(h, full[f"{tag}_w_gu"], f"{tag}_up")
        if f"{tag}_w_down" not in full:
            gathered_group(gi, a)
        f = _matmul(a, full[f"{tag}_w_down"], "nn", F32, f"{tag}_down", tm=512, tk=D_FF)
        xout = _post_fwd(f, xin, row(g_post[k]), mod[k, 2], res_w[k], f"{tag}_post")
        return xout, (h, g, u, a, f)

    x1, saved1 = ffn_fwd(x0, 0, 0, "ffn1", deps=(started,))

    gathered_group(1, x1)
    h2 = _pre_fwd(x1, row(g_pre[1]), mod[1, 0], mod[1, 1], "mix_pre")
    proj = _matmul(h2, full["w_in"], "nt", F32, "mix_in", b_shift=3 * ATT_WIDTH // 512)
    att_o = _attn_fwd(proj, bias, "attn_fwd")
    gathered_group(2, att_o)
    xc, xcb = _conv_fwd(proj, conv_w8, conv_b, "conv_fwd")
    pre = _matmul(xcb, w_bd, "nn", F32, "lru_gate_proj")
    a_t, u_t = _gates_fwd(pre, xc, lru_ba, lru_bx, lru_lambda, "lru_gates")
    hs, h_prev = _scan_fwd(a_t, u_t, "lru_scan")
    (rec_in,) = _rowwise(_recin_fn, "rec_in", [], [hs, (proj, LRU_WIDTH, 1)], [(LRU_WIDTH, BF16)])
    att = _matmul(att_o, full["w_att_o"], "nn", F32, "att_out")
    rec = _matmul(rec_in, full["w_rec_o"], "nn", F32, "rec_out")
    (merged,) = _rowwise(_merge_fn, "merge", [], [att, rec, (proj, LRU_WIDTH, 2), (proj, LRU_WIDTH, 3)],
                         [(D_MODEL, BF16)])
    f2 = _matmul(merged, full["w_out"], "nn", F32, "mix_out")
    x2 = _post_fwd(f2, x1, row(g_post[1]), mod[1, 2], res_w[1], "mix_post")

    gathered_group(3, x2)
    x3, saved3 = ffn_fwd(x2, 2, 2, "ffn2")

    dy, sq = _loss_stage(x3, target, "loss")
    loss = lax.psum(0.5 * jnp.sum(sq) / D_MODEL, ("x", "y", "c"))

    grads = {}
    dmod = [[None] * 3 for _ in range(3)]
    d_pre, d_post = [None] * 3, [None] * 3

    pending = []

    def exchange_start(names, tag, after=()):
        send = [(_cols_parts if n == "w_att_o" else _rows_parts)(grads[n]) for n in names]
        (group,), token = _push_start([send], True, f"exchange_{tag}_start", after=after)
        pending.append((names, send, group, tag))
        return token

    def exchange_finish(names, send, group, tag, after):
        lands = _push_wait(group, True, after, f"exchange_{tag}_wait")
        res = None
        for n, land, mine in zip(names, lands, send):
            res = _adamw_parts(land, mine, me, local(n, weights[n]), local(n, mom1[n]), local(n, mom2[n]),
                               f"adamw_{n}")
            back = (lambda r: jnp.transpose(r)) if n in transposed else (lambda r: r)
            out_g[n], out_d[n], out_m[n], out_v[n] = [back(r).reshape(weights[n].shape) for r in res]
        return res[0]

    out_g, out_d, out_m, out_v = {}, {}, {}, {}

    def ffn_bwd(xin, k, saved, dout, tag):
        h, g, u, a, f = saved
        w_gu, w_down = f"{tag}_w_gu", f"{tag}_w_down"
        df, d_post[k], dmod[k][2] = _post_bwd(f, row(g_post[k]), mod[k, 2], res_w[k], dout, f"{tag}_post_bwd")
        grads[w_down] = _matmul(a, df, "tn", BF16, f"{tag}_dw_down", tm=1408, tn=1024, tk=1024)
        started = exchange_start([w_down], w_down)
        dg, du = _ffn_up_bwd(df, full[w_down], g, u, f"{tag}_up_bwd", deps=(started,))
        dw_g = _matmul(dg, h, "tn", BF16, f"{tag}_dw_g", tm=1408, tn=1024, tk=1024)
        dw_u = _matmul(du, h, "tn", BF16, f"{tag}_dw_u", tm=1408, tn=1024, tk=1024)
        grads[w_gu] = jnp.concatenate([dw_g, dw_u], axis=0)
        started = exchange_start([w_gu], w_gu)
        dh = _ffn_dh(dg, du, full[w_gu], f"{tag}_dh", deps=(started,))
        dx, d_pre[k], dmod[k][0], dmod[k][1] = _pre_bwd(xin, row(g_pre[k]), mod[k, 0], mod[k, 1], dh, dout,
                                                         f"{tag}_pre_bwd")
        return dx

    dx2 = ffn_bwd(x2, 2, saved3, dy, "ffn2")

    df2, d_post[1], dmod[1][2] = _post_bwd(f2, row(g_post[1]), mod[1, 2], res_w[1], dx2, "mix_post_bwd")
    dmerged = _matmul(df2, full["w_out"], "nt", F32, "mix_dmerged")
    grads["w_out"] = _matmul(merged, df2, "tn", BF16, "mix_dw_out", tm=1024, tn=1024, tk=1024)

    def merge_bwd(att, rec, g_att, g_rec, dm):
        _, vjp = jax.vjp(_merge_fn, att, rec, g_att, g_rec)
        return vjp(dm)

    datt, drec, dg_att, dg_rec = _rowwise(
        merge_bwd, "merge_bwd", [], [att, rec, (proj, LRU_WIDTH, 2), (proj, LRU_WIDTH, 3), dmerged],
        [(D_MODEL, BF16)] * 4)
    datt_o = _matmul(datt, full["w_att_o"], "nt", BF16, "att_out_bwd")
    grads["w_att_o"] = _matmul(att_o, datt, "tn", BF16, "dw_att_o", tm=512, tn=1024, tk=1024)
    drec_in = _matmul(drec, full["w_rec_o"], "nt", F32, "rec_out_bwd")
    grads["w_rec_o"] = _matmul(rec_in, drec, "tn", BF16, "dw_rec_o", tm=1024, tn=1024, tk=1024)
    started = exchange_start(["w_out", "w_att_o", "w_rec_o"], "mix_out")

    def recin_bwd(hs, yr, d):
        _, vjp = jax.vjp(_recin_fn, hs, yr)
        return vjp(d)

    dhs, dyr = _rowwise(recin_bwd, "rec_in_bwd", [], [hs, (proj, LRU_WIDTH, 1), drec_in],
                        [(LRU_WIDTH, F32), (LRU_WIDTH, BF16)], deps=(started,))
    g_t = _scan_bwd(a_t, dhs, "lru_scan_bwd")
    dpre, dxc_direct, d_ba, d_bx, d_lam = _gates_bwd(pre, xc, lru_ba, lru_bx, lru_lambda, g_t, h_prev,
                                                     "lru_gates_bwd")
    dxc_mm = _matmul(dpre, w_bd, "nt", F32, "lru_gate_proj_bwd")
    dw_bd = _matmul(xcb, dpre, "tn", F32, "dw_lru_gate", tm=1024, tn=1024, tk=1024)
    dxr, d_conv_w8, d_conv_b = _conv_bwd(proj, conv_w8, dxc_direct, dxc_mm, "conv_bwd")
    dq, dk, dv, dbias = _attn_bwd(proj, bias, datt_o, "attn_bwd")
    dproj = jnp.concatenate([dq, dk, dv, dxr, dyr, dg_att, dg_rec], axis=1)
    grads["w_in"] = _matmul(dproj, h2, "tn", BF16, "mix_dw_in", tm=1408, tn=1024, tk=1024)
    pack_mix = jnp.concatenate([d_conv_w8[:4], d_conv_b, d_ba, d_bx, d_lam, _pad_rows(_bias_grad(dbias), 3),
                                _lru_diag_blocks(dw_bd, "lru_diag_blocks").reshape(128, D_MODEL),
                                jnp.zeros((5, D_MODEL), F32)], axis=0)
    (mix_started,), started = _push_start([[pack_mix]], False, "small_grads_mix_start")
    started = exchange_start(["w_in"], "w_in", after=(started,))
    dh2 = _matmul(dproj, full["w_in"], "nn", F32, "mix_dh", deps=(started,))
    dx1, d_pre[1], dmod[1][0], dmod[1][1] = _pre_bwd(x1, row(g_pre[1]), mod[1, 0], mod[1, 1], dh2, dx2, "mix_pre_bwd")

    dx0 = ffn_bwd(x0, 0, saved1, dx1, "ffn1")

    dmod_mine = jnp.concatenate([dmod[k][j] for k in range(3) for j in range(3)], axis=0)
    pack_norm = jnp.concatenate([dmod_mine, *d_pre, *d_post, jnp.zeros((1, D_MODEL), F32)], axis=0)
    (norm_started,), _ = _push_start([[pack_norm]], False, "small_grads_norm_start")

    def summed(started, pack, after, tag):
        (land,) = _push_wait(started, False, after, f"small_grads_{tag}_wait")
        parts = jnp.where(is_me, pack[None], land)
        return parts, _sum_parts(parts, f"small_grads_{tag}_sum")

    done = dx0
    last = [p for p in pending if p[3].startswith("ffn1")]
    for names, send, group, tag in pending:
        if not tag.startswith("ffn1"):
            done = exchange_finish(names, send, group, tag, done)

    _, total = summed(mix_started, pack_mix, done, "mix")
    grads["conv_w"] = _my_cols(total[0:4], me, 128)
    grads["conv_b"] = total[4:5]
    grads["lru_ba"] = total[5:6]
    grads["lru_bx"] = total[6:7]
    grads["lru_lambda"] = total[7:8]
    grads["rel_bias"] = total[8:11].reshape(-1)[: ATT_HEADS * (2 * MAX_REL + 1)].reshape(ATT_HEADS, -1)
    grads["lru_wa"] = total[11:75].reshape(LRU_BLOCKS, LRU_BLOCK, LRU_BLOCK)
    grads["lru_wx"] = total[75:139].reshape(LRU_BLOCKS, LRU_BLOCK, LRU_BLOCK)
    parts, total = summed(norm_started, pack_norm, total, "norm")
    grads["b_ada"] = total[0:9].reshape(1, -1)
    grads["norm_pre"] = _my_cols(total[9:12], me, 128)
    grads["norm_post"] = _my_cols(total[12:15], me, 128)
    dmod_all = parts[:, 0:9, :].reshape(N_DEV, 9 * D_MODEL)
    grads["w_ada"] = _ada_bwd(c_all, _my_cols(dmod_all, me, 1152), "ada_bwd")

    res = _adamw(grads["w_ada"], w_ada[0], m_w_ada[0], v_w_ada[0], "adamw_w_ada")
    out_g["w_ada"], out_d["w_ada"], out_m["w_ada"], out_v["w_ada"] = [r.reshape(w_ada.shape) for r in res]

    sizes = [int(np.prod(weights[n].shape)) for n in small]
    tot = sum(sizes)
    rows_small = -(-tot // (16 * D_MODEL)) * 16
    flat = lambda arrs: jnp.pad(jnp.concatenate([a.reshape(-1) for a in arrs]),
                                (0, rows_small * D_MODEL - tot)).reshape(rows_small, D_MODEL)
    res = _adamw(flat([grads[n] for n in small]), flat([weights[n] for n in small]),
                 flat([mom1[n] for n in small]), flat([mom2[n] for n in small]), "adamw_small", rows=rows_small)
    offs = np.cumsum([0] + sizes)
    for dst, r in zip((out_g, out_d, out_m, out_v), res):
        rf = r.reshape(-1)
        for i, n in enumerate(small):
            dst[n] = rf[offs[i] : offs[i + 1]].reshape(weights[n].shape)

    done = res[0]
    for names, send, group, tag in last:
        done = exchange_finish(names, send, group, tag, done)

    return (loss, dx0[None], *[out_g[n] for n in order], *[out_d[n] for n in order],
            *[out_m[n] for n in order], *[out_v[n] for n in order])
```

```python
import functools

import jax
import jax.numpy as jnp
import numpy as np
from jax import lax
from jax.experimental import pallas as pl
from jax.experimental.pallas import tpu as pltpu

D_MODEL = 1024
D_FF = 2816
ATT_HEADS = 8
ATT_HEAD_DIM = 64
ATT_WIDTH = 512
CHUNK = 64
LEFT_CHUNKS = 8
MAX_REL = 128
LRU_WIDTH = 1024
LRU_BLOCKS = 16
LRU_BLOCK = 64
LRU_C = 8.0
EPS = 1e-6
PROJ_WIDTH = 5632
N_DEV = 8

ADAM_LR = 0.001
ADAM_B1 = 0.9
ADAM_B2 = 0.999
ADAM_EPS = 1e-08
ADAM_WD = 0.01
ADAM_STEP = 10

V7X_LANES = 128
V7X_SUBLANES = 8
V7X_VMEM_BYTES = 64 * 1024 * 1024
VMEM_LIMIT = V7X_VMEM_BYTES - 8 * 1024 * 1024

ATT_TQ = 256
NEG = -1e30
BF16 = jnp.bfloat16
F32 = jnp.float32
MESH = pl.DeviceIdType.MESH

OFF_Q = 4 * LRU_WIDTH
OFF_K = OFF_Q + ATT_WIDTH
OFF_V = OFF_K + ATT_WIDTH


def _cparams(**kw):
    return pltpu.CompilerParams(vmem_limit_bytes=VMEM_LIMIT, **kw)


def _pick(n, target, unit=V7X_LANES):
    best = None
    for t in range(unit, min(n, target) + 1, unit):
        if n % t == 0:
            best = t
    return n if best is None else best


_DIMS = {
    "nn": (((1,), (0,)), ((), ())),
    "nt": (((1,), (1,)), ((), ())),
    "tn": (((0,), (0,)), ((), ())),
}


ANY_SPEC = pl.BlockSpec(memory_space=pl.ANY)


def _matmul(a, b, mode, out_dtype, name, tm=1024, tn=512, tk=1408, deps=(), b_shift=0):
    n_deps = len(deps)
    if mode == "nn":
        (m, k), (k2, n) = a.shape, b.shape
    elif mode == "nt":
        (m, k), (n, k2) = a.shape, b.shape
    else:
        (k, m), (k2, n) = a.shape, b.shape
    assert k == k2, (a.shape, b.shape, mode)
    tm, tn, tk = _pick(m, tm), _pick(n, tn), _pick(k, tk)
    nk = k // tk
    dims = _DIMS[mode]

    def body(a_ref, b_ref, *rest):
        o_ref, scratch = rest[n_deps], rest[n_deps + 1 :]
        p = lax.dot_general(a_ref[...], b_ref[...], dims, preferred_element_type=F32)
        if nk == 1:
            o_ref[...] = p.astype(o_ref.dtype)
        else:
            acc = scratch[0]
            kk = pl.program_id(2)

            @pl.when(kk == 0)
            def _():
                acc[...] = p

            @pl.when(kk > 0)
            def _():
                acc[...] += p

            @pl.when(kk == nk - 1)
            def _():
                o_ref[...] = acc[...].astype(o_ref.dtype)

    if mode == "nn":
        a_spec = pl.BlockSpec((tm, tk), lambda i, j, kk: (i, kk))
        b_spec = pl.BlockSpec((tk, tn), lambda i, j, kk: (kk, j))
    elif mode == "nt":
        a_spec = pl.BlockSpec((tm, tk), lambda i, j, kk: (i, kk))
        b_spec = pl.BlockSpec((tn, tk), lambda i, j, kk: ((j + b_shift) % (n // tn), kk))
    else:
        a_spec = pl.BlockSpec((tk, tm), lambda i, j, kk: (kk, i))
        b_spec = pl.BlockSpec((tk, tn), lambda i, j, kk: (kk, j))
    return pl.pallas_call(
        body,
        name=name,
        grid=(m // tm, n // tn, nk),
        in_specs=[a_spec, b_spec] + [ANY_SPEC] * n_deps,
        out_specs=pl.BlockSpec((tm, tn), lambda i, j, kk: (i, j)),
        out_shape=jax.ShapeDtypeStruct((m, n), out_dtype),
        scratch_shapes=[pltpu.VMEM((tm, tn), F32)] if nk > 1 else [],
        compiler_params=_cparams(dimension_semantics=("parallel", "parallel", "arbitrary")),
    )(a, b, *deps)


def _rowwise(fn, name, params, tiles, outs, accs=(), ts=256, with_index=False, deps=()):
    norm = []
    for t in tiles:
        if not isinstance(t, tuple):
            t = (t, t.shape[1], 0)
        norm.append(t if len(t) == 4 else (*t, None))
    s = norm[0][0].shape[0]
    ts = min(ts, s)
    assert s % ts == 0 and ts % V7X_SUBLANES == 0
    steps = s // ts
    halo_blocks = ts // V7X_SUBLANES
    n_p, n_t, n_o = len(params), len(norm), len(outs)

    def body(*refs):
        i = pl.program_id(0)
        vals = [r[...] for r in refs[: n_p + n_t]]
        res = fn(i, steps, *vals) if with_index else fn(*vals)
        if not isinstance(res, (tuple, list)):
            res = (res,)
        first_out = n_p + n_t + len(deps)
        o_refs = refs[first_out : first_out + n_o]
        a_refs = refs[first_out + n_o :]
        for r, v in zip(o_refs, res[:n_o]):
            r[...] = v.astype(r.dtype)
        for r, v in zip(a_refs, res[n_o:]):
            _accumulate(r, v, i)

    in_specs = [pl.BlockSpec(p.shape, lambda i: (0, 0)) for p in params]
    for arr, w, cb, halo in norm:
        if halo is None:
            in_specs.append(pl.BlockSpec((ts, w), lambda i, cb=cb: (i, cb)))
        elif halo == "prev":
            in_specs.append(
                pl.BlockSpec((V7X_SUBLANES, w), lambda i, cb=cb: (jnp.maximum(i * halo_blocks - 1, 0), cb))
            )
        else:
            last = s // V7X_SUBLANES - 1
            in_specs.append(
                pl.BlockSpec((V7X_SUBLANES, w), lambda i, cb=cb: (jnp.minimum((i + 1) * halo_blocks, last), cb))
            )
    in_specs += [ANY_SPEC] * len(deps)
    out_specs = [pl.BlockSpec((ts, w), lambda i: (i, 0)) for w, _ in outs]
    out_specs += [pl.BlockSpec(shape, lambda i: (0, 0)) for shape in accs]
    out_shape = [jax.ShapeDtypeStruct((s, w), dt) for w, dt in outs]
    out_shape += [jax.ShapeDtypeStruct(shape, F32) for shape in accs]
    res = pl.pallas_call(
        body,
        name=name,
        grid=(steps,),
        in_specs=in_specs,
        out_specs=out_specs,
        out_shape=out_shape,
        compiler_params=_cparams(dimension_semantics=("arbitrary",)),
    )(*params, *[t[0] for t in norm], *deps)
    return res


def _accumulate(ref, val, step):
    @pl.when(step == 0)
    def _():
        ref[...] = val

    @pl.when(step > 0)
    def _():
        ref[...] += val


def _sigmoid(z):
    return jax.nn.sigmoid(z)


def _silu(z):
    return z * _sigmoid(z)


def _gelu(z):
    return 0.5 * z * (1.0 + jnp.tanh(0.7978845608028654 * (z + 0.044715 * (z * z * z))))


def _pre_fn(g, shift, scale, x):
    r = lax.rsqrt(jnp.mean(x * x, axis=-1, keepdims=True) + EPS)
    return ((x * r) * g) * (1.0 + scale) + shift


def _post_fn(res_w, g, gate, f, x):
    r = lax.rsqrt(jnp.mean(f * f, axis=-1, keepdims=True) + EPS)
    return x + (res_w * gate) * ((f * r) * g)


def _swiglu_fn(gu):
    return _silu(gu[:, :D_FF]) * gu[:, D_FF:]


def _gates_fn(ba, bx, lam, pre, xc):
    ra = _sigmoid(pre[:, :LRU_WIDTH] + ba)
    ia = _sigmoid(pre[:, LRU_WIDTH:] + bx)
    softplus = jnp.maximum(-lam, 0.0) + jnp.log1p(jnp.exp(-jnp.abs(lam)))
    log_a = (-LRU_C) * ra * softplus
    a = jnp.exp(log_a)
    mult = jnp.sqrt(-jnp.tanh(log_a) * (a * a + 1.0))
    return a, mult * (ia * xc)


def _recin_fn(hs, yr):
    return hs * _gelu(yr)


def _merge_fn(att, rec, g_att, g_rec):
    return _sigmoid(g_att) * att + _sigmoid(g_rec) * rec


def _rowsum(v):
    return jnp.sum(v, axis=0, keepdims=True)


def _pre_fwd(x, g, shift, scale, name, deps=()):
    (h,) = _rowwise(_pre_fn, name, [g, shift, scale], [x], [(D_MODEL, BF16)], deps=deps)
    return h


def _pre_bwd(x, g, shift, scale, dh, dres, name):
    def fn(g, shift, scale, x, dh, dres):
        _, vjp = jax.vjp(_pre_fn, g, shift, scale, x)
        dg, dshift, dscale, dx = vjp(dh)
        return dx + dres, dg, dshift, dscale

    row = (1, D_MODEL)
    return _rowwise(fn, name, [g, shift, scale], [x, dh, dres], [(D_MODEL, F32)], [row, row, row])


def _post_fwd(f, x, g, gate, res_w, name):
    (y,) = _rowwise(functools.partial(_post_fn, res_w), name, [g, gate], [f, x], [(D_MODEL, F32)])
    return y


def _post_bwd(f, g, gate, res_w, dy, name, deps=()):
    def fn(g, gate, f, dy):
        _, vjp = jax.vjp(lambda g, gate, f: _post_fn(res_w, g, gate, f, 0.0), g, gate, f)
        dg, dgate, df = vjp(dy)
        return df, dg, dgate

    row = (1, D_MODEL)
    return _rowwise(fn, name, [g, gate], [f, dy], [(D_MODEL, BF16)], [row, row], deps=deps)


def _loss_stage(y, target, name):
    def fn(y, t):
        diff = y - t
        return diff * (1.0 / D_MODEL), _rowsum(diff * diff)

    return _rowwise(fn, name, [], [y, target], [(D_MODEL, F32)], [(1, D_MODEL)])


FFN_TM = 512
FFN_TF = 1408


def _glu_fn(g, u):
    return _silu(g) * u


def _ffn_up(h, w_gu_t, name):
    s = h.shape[0]
    tm = min(FFN_TM, s)
    nf = D_FF // FFN_TF

    def body(h_ref, wg_ref, wu_ref, a_ref, g_ref, u_ref):
        hv = h_ref[...]
        g = lax.dot_general(hv, wg_ref[...], _DIMS["nt"], preferred_element_type=F32)
        u = lax.dot_general(hv, wu_ref[...], _DIMS["nt"], preferred_element_type=F32)
        a_ref[...] = _glu_fn(g, u).astype(a_ref.dtype)
        g_ref[...] = g.astype(g_ref.dtype)
        u_ref[...] = u.astype(u_ref.dtype)

    out = pl.BlockSpec((tm, FFN_TF), lambda i, j: (i, j))
    return pl.pallas_call(
        body,
        name=name,
        grid=(s // tm, nf),
        in_specs=[pl.BlockSpec((tm, D_MODEL), lambda i, j: (i, 0)),
                  pl.BlockSpec((FFN_TF, D_MODEL), lambda i, j: (j, 0)),
                  pl.BlockSpec((FFN_TF, D_MODEL), lambda i, j: (nf + j, 0))],
        out_specs=[out, out, out],
        out_shape=[jax.ShapeDtypeStruct((s, D_FF), BF16)] * 3,
        compiler_params=_cparams(dimension_semantics=("parallel", "arbitrary")),
    )(h, w_gu_t, w_gu_t)


def _ffn_up_bwd(df, w_down, g, u, name, deps=()):
    s = df.shape[0]
    tm = min(FFN_TM, s)

    def body(df_ref, wd_ref, g_ref, u_ref, *rest):
        dg_ref, du_ref = rest[len(deps) :]
        da = lax.dot_general(df_ref[...], wd_ref[...], _DIMS["nt"], preferred_element_type=F32)
        _, vjp = jax.vjp(_glu_fn, g_ref[...].astype(F32), u_ref[...].astype(F32))
        dg, du = vjp(da)
        dg_ref[...] = dg.astype(dg_ref.dtype)
        du_ref[...] = du.astype(du_ref.dtype)

    blk = pl.BlockSpec((tm, FFN_TF), lambda i, j: (i, j))
    return pl.pallas_call(
        body,
        name=name,
        grid=(s // tm, D_FF // FFN_TF),
        in_specs=[pl.BlockSpec((tm, D_MODEL), lambda i, j: (i, 0)),
                  pl.BlockSpec((FFN_TF, D_MODEL), lambda i, j: (j, 0)), blk, blk] + [ANY_SPEC] * len(deps),
        out_specs=[blk, blk],
        out_shape=[jax.ShapeDtypeStruct((s, D_FF), BF16)] * 2,
        compiler_params=_cparams(dimension_semantics=("parallel", "arbitrary")),
    )(df, w_down, g, u, *deps)


def _ffn_dh(dg, du, w_gu_t, name, deps=()):
    s = dg.shape[0]
    tm, tn = min(FFN_TM, s), 512

    def body(dg_ref, du_ref, wg_ref, wu_ref, *rest):
        o_ref = rest[len(deps)]
        p = jnp.dot(dg_ref[...], wg_ref[...], preferred_element_type=F32)
        o_ref[...] = p + jnp.dot(du_ref[...], wu_ref[...], preferred_element_type=F32)

    a_spec = pl.BlockSpec((tm, D_FF), lambda i, j: (i, 0))
    return pl.pallas_call(
        body,
        name=name,
        grid=(s // tm, D_MODEL // tn),
        in_specs=[a_spec, a_spec,
                  pl.BlockSpec((D_FF, tn), lambda i, j: (0, j)),
                  pl.BlockSpec((D_FF, tn), lambda i, j: (1, j))] + [ANY_SPEC] * len(deps),
        out_specs=pl.BlockSpec((tm, tn), lambda i, j: (i, j)),
        out_shape=jax.ShapeDtypeStruct((s, D_MODEL), F32),
        compiler_params=_cparams(dimension_semantics=("parallel", "arbitrary")),
    )(dg, du, w_gu_t, w_gu_t, *deps)


FUSE_TM = 256
ROW_SPEC2 = pl.BlockSpec((1, D_MODEL), lambda i, j: (0, 0))
ROW_SPEC1 = pl.BlockSpec((1, D_MODEL), lambda i: (0, 0))


def _pre_up(x, g, shift, scale, w_gu_t, name, deps=()):
    s = x.shape[0]
    tm = min(FFN_TM, s)
    nf = D_FF // FFN_TF
    nd = len(deps)

    def body(x_ref, g_ref, sh_ref, sc_ref, wg_ref, wu_ref, *rest):
        h_ref, a_ref, gg_ref, u_ref, h_s = rest[nd:]

        @pl.when(pl.program_id(1) == 0)
        def _():
            h = _pre_fn(g_ref[...], sh_ref[...], sc_ref[...], x_ref[...]).astype(BF16)
            h_s[...] = h
            h_ref[...] = h

        hv = h_s[...]
        gv = lax.dot_general(hv, wg_ref[...], _DIMS["nt"], preferred_element_type=F32)
        uv = lax.dot_general(hv, wu_ref[...], _DIMS["nt"], preferred_element_type=F32)
        a_ref[...] = _glu_fn(gv, uv).astype(a_ref.dtype)
        gg_ref[...] = gv.astype(gg_ref.dtype)
        u_ref[...] = uv.astype(u_ref.dtype)

    rows = pl.BlockSpec((tm, D_MODEL), lambda i, j: (i, 0))
    out = pl.BlockSpec((tm, FFN_TF), lambda i, j: (i, j))
    return pl.pallas_call(
        body,
        name=name,
        grid=(s // tm, nf),
        in_specs=[rows, ROW_SPEC2, ROW_SPEC2, ROW_SPEC2,
                  pl.BlockSpec((FFN_TF, D_MODEL), lambda i, j: (j, 0)),
                  pl.BlockSpec((FFN_TF, D_MODEL), lambda i, j: (nf + j, 0))] + [ANY_SPEC] * nd,
        out_specs=[rows, out, out, out],
        out_shape=[jax.ShapeDtypeStruct((s, D_MODEL), BF16)] + [jax.ShapeDtypeStruct((s, D_FF), BF16)] * 3,
        scratch_shapes=[pltpu.VMEM((tm, D_MODEL), BF16)],
        compiler_params=_cparams(dimension_semantics=("parallel", "arbitrary")),
    )(x, g, shift, scale, w_gu_t, w_gu_t, *deps)


def _pre_matmul(x, g, shift, scale, w_t, name, b_shift=0, tn=512):
    s = x.shape[0]
    n = w_t.shape[0]
    tm = min(FFN_TM, s)

    def body(x_ref, g_ref, sh_ref, sc_ref, w_ref, h_ref, o_ref, h_s):
        @pl.when(pl.program_id(1) == 0)
        def _():
            h = _pre_fn(g_ref[...], sh_ref[...], sc_ref[...], x_ref[...]).astype(BF16)
            h_s[...] = h
            h_ref[...] = h

        o_ref[...] = lax.dot_general(h_s[...], w_ref[...], _DIMS["nt"], preferred_element_type=F32)

    rows = pl.BlockSpec((tm, D_MODEL), lambda i, j: (i, 0))
    return pl.pallas_call(
        body,
        name=name,
        grid=(s // tm, n // tn),
        in_specs=[rows, ROW_SPEC2, ROW_SPEC2, ROW_SPEC2,
                  pl.BlockSpec((tn, D_MODEL), lambda i, j: ((j + b_shift) % (n // tn), 0))],
        out_specs=[rows, pl.BlockSpec((tm, tn), lambda i, j: (i, j))],
        out_shape=[jax.ShapeDtypeStruct((s, D_MODEL), BF16), jax.ShapeDtypeStruct((s, n), F32)],
        scratch_shapes=[pltpu.VMEM((tm, D_MODEL), BF16)],
        compiler_params=_cparams(dimension_semantics=("parallel", "arbitrary")),
    )(x, g, shift, scale, w_t)


def _matmul_post(a, w, x, g_post, gate, res_w, name):
    s, k = a.shape
    tm = min(FFN_TM, s)

    def body(a_ref, w_ref, x_ref, g_ref, gate_ref, f_ref, y_ref):
        f = jnp.dot(a_ref[...], w_ref[...], preferred_element_type=F32)
        f_ref[...] = f
        y_ref[...] = _post_fn(res_w, g_ref[...], gate_ref[...], f, x_ref[...])

    rows = pl.BlockSpec((tm, D_MODEL), lambda i: (i, 0))
    return pl.pallas_call(
        body,
        name=name,
        grid=(s // tm,),
        in_specs=[pl.BlockSpec((tm, k), lambda i: (i, 0)), pl.BlockSpec((k, D_MODEL), lambda i: (0, 0)), rows,
                  ROW_SPEC1, ROW_SPEC1],
        out_specs=[rows, rows],
        out_shape=[jax.ShapeDtypeStruct((s, D_MODEL), F32)] * 2,
        compiler_params=_cparams(dimension_semantics=("parallel",)),
    )(a, w, x, g_post, gate)


def _post_vjp(res_w, g, gate, f, dy):
    _, vjp = jax.vjp(lambda g, gate, f: _post_fn(res_w, g, gate, f, 0.0), g, gate, f)
    return vjp(dy)


def _post_bwd_up_bwd(f, dy, g_post, gate, res_w, w_down, g, u, name, deps=()):
    s = f.shape[0]
    tm = min(FUSE_TM, s)
    nd = len(deps)

    def body(f_ref, dy_ref, gp_ref, gate_ref, wd_ref, g_ref, u_ref, *rest):
        df_ref, dgu_ref, dgp_ref, dgate_ref, df_s = rest[nd:]
        i = pl.program_id(0)

        @pl.when(pl.program_id(1) == 0)
        def _():
            dgp, dgate, df = _post_vjp(res_w, gp_ref[...], gate_ref[...], f_ref[...], dy_ref[...])
            df_s[...] = df.astype(BF16)
            df_ref[...] = df_s[...]
            _accumulate(dgp_ref, dgp, i)
            _accumulate(dgate_ref, dgate, i)

        da = lax.dot_general(df_s[...], wd_ref[...], _DIMS["nt"], preferred_element_type=F32)
        _, vjp = jax.vjp(_glu_fn, g_ref[...].astype(F32), u_ref[...].astype(F32))
        dg, du = vjp(da)
        dgu_ref[0] = dg.astype(dgu_ref.dtype)
        dgu_ref[1] = du.astype(dgu_ref.dtype)

    rows = pl.BlockSpec((tm, D_MODEL), lambda i, j: (i, 0))
    blk = pl.BlockSpec((tm, FFN_TF), lambda i, j: (i, j))
    return pl.pallas_call(
        body,
        name=name,
        grid=(s // tm, D_FF // FFN_TF),
        in_specs=[rows, rows, ROW_SPEC2, ROW_SPEC2, pl.BlockSpec((FFN_TF, D_MODEL), lambda i, j: (j, 0)), blk,
                  blk] + [ANY_SPEC] * nd,
        out_specs=[rows, pl.BlockSpec((2, tm, FFN_TF), lambda i, j: (0, i, j)), ROW_SPEC2, ROW_SPEC2],
        out_shape=[jax.ShapeDtypeStruct((s, D_MODEL), BF16), jax.ShapeDtypeStruct((2, s, D_FF), BF16),
                   jax.ShapeDtypeStruct((1, D_MODEL), F32), jax.ShapeDtypeStruct((1, D_MODEL), F32)],
        scratch_shapes=[pltpu.VMEM((tm, D_MODEL), BF16)],
        compiler_params=_cparams(dimension_semantics=("arbitrary", "arbitrary")),
    )(f, dy, g_post, gate, w_down, g, u, *deps)


def _post_bwd_matmul(f, dy, g_post, gate, res_w, w, name):
    s = f.shape[0]
    n = w.shape[0]
    tm = min(FUSE_TM, s)

    def body(f_ref, dy_ref, gp_ref, gate_ref, w_ref, df_ref, o_ref, dgp_ref, dgate_ref):
        i = pl.program_id(0)
        dgp, dgate, df = _post_vjp(res_w, gp_ref[...], gate_ref[...], f_ref[...], dy_ref[...])
        dfb = df.astype(BF16)
        df_ref[...] = dfb
        _accumulate(dgp_ref, dgp, i)
        _accumulate(dgate_ref, dgate, i)
        o_ref[...] = lax.dot_general(dfb, w_ref[...], _DIMS["nt"], preferred_element_type=F32)

    rows = pl.BlockSpec((tm, D_MODEL), lambda i: (i, 0))
    return pl.pallas_call(
        body,
        name=name,
        grid=(s // tm,),
        in_specs=[rows, rows, ROW_SPEC1, ROW_SPEC1, pl.BlockSpec((n, D_MODEL), lambda i: (0, 0))],
        out_specs=[rows, pl.BlockSpec((tm, n), lambda i: (i, 0)), ROW_SPEC1, ROW_SPEC1],
        out_shape=[jax.ShapeDtypeStruct((s, D_MODEL), BF16), jax.ShapeDtypeStruct((s, n), F32),
                   jax.ShapeDtypeStruct((1, D_MODEL), F32), jax.ShapeDtypeStruct((1, D_MODEL), F32)],
        compiler_params=_cparams(dimension_semantics=("arbitrary",)),
    )(f, dy, g_post, gate, w)


def _matmul_pre_bwd(parts, w_t, x, dres, g, shift, scale, name, deps=()):
    s = x.shape[0]
    na, nd = len(parts), len(deps)
    ranges = [p[3] for p in parts]

    def body(*refs):
        a_refs = refs[:na]
        w_ref, x_ref, dres_ref, g_ref, sh_ref, sc_ref = refs[na : na + 6]
        dx_ref, dg_ref, dsh_ref, dsc_ref = refs[na + 6 + nd :]
        i = pl.program_id(0)
        dh = None
        for a_ref, (r0, r1) in zip(a_refs, ranges):
            p = jnp.dot(a_ref[...], w_ref[r0:r1, :], preferred_element_type=F32)
            dh = p if dh is None else dh + p
        _, vjp = jax.vjp(_pre_fn, g_ref[...], sh_ref[...], sc_ref[...], x_ref[...])
        dg, dsh, dsc, dx = vjp(dh)
        dx_ref[...] = dx + dres_ref[...]
        _accumulate(dg_ref, dg, i)
        _accumulate(dsh_ref, dsh, i)
        _accumulate(dsc_ref, dsc, i)

    tm = parts[0][1][-2]
    rows = pl.BlockSpec((tm, D_MODEL), lambda i: (i, 0))
    return pl.pallas_call(
        body,
        name=name,
        grid=(s // tm,),
        in_specs=[pl.BlockSpec(p[1], p[2]) for p in parts]
        + [pl.BlockSpec(w_t.shape, lambda i: (0, 0)), rows, rows, ROW_SPEC1, ROW_SPEC1, ROW_SPEC1]
        + [ANY_SPEC] * nd,
        out_specs=[rows, ROW_SPEC1, ROW_SPEC1, ROW_SPEC1],
        out_shape=[jax.ShapeDtypeStruct((s, D_MODEL), F32)] + [jax.ShapeDtypeStruct((1, D_MODEL), F32)] * 3,
        compiler_params=_cparams(dimension_semantics=("arbitrary",)),
    )(*[p[0] for p in parts], w_t, x, dres, g, shift, scale, *deps)


def _dw_gu(dgu, h, name, deps=()):
    s = h.shape[0]
    tk = min(1024, s)
    nk = s // tk
    half = D_FF // FFN_TF

    def body(a_ref, b_ref, *rest):
        o_ref, acc = rest[len(deps) :]
        kk = pl.program_id(1)
        p = lax.dot_general(a_ref[...], b_ref[...], _DIMS["tn"], preferred_element_type=F32)

        @pl.when(kk == 0)
        def _():
            acc[...] = p

        @pl.when(kk > 0)
        def _():
            acc[...] += p

        @pl.when(kk == nk - 1)
        def _():
            o_ref[...] = acc[...].astype(o_ref.dtype)

    return pl.pallas_call(
        body,
        name=name,
        grid=(2 * half, nk),
        in_specs=[pl.BlockSpec((None, tk, FFN_TF), lambda i, kk: (i // half, kk, i % half)),
                  pl.BlockSpec((tk, D_MODEL), lambda i, kk: (kk, 0))] + [ANY_SPEC] * len(deps),
        out_specs=pl.BlockSpec((FFN_TF, D_MODEL), lambda i, kk: (i, 0)),
        out_shape=jax.ShapeDtypeStruct((2 * D_FF, D_MODEL), BF16),
        scratch_shapes=[pltpu.VMEM((FFN_TF, D_MODEL), F32)],
        compiler_params=_cparams(dimension_semantics=("parallel", "arbitrary")),
    )(dgu, h, *deps)


def _lru_diag_blocks(dw_bd, name):
    def body(w_ref, o_ref):
        for half in range(2):
            for n in range(LRU_BLOCKS):
                rows = slice(n * LRU_BLOCK, (n + 1) * LRU_BLOCK)
                cols = slice(half * LRU_WIDTH + n * LRU_BLOCK, half * LRU_WIDTH + (n + 1) * LRU_BLOCK)
                o_ref[half, rows, :] = w_ref[rows, cols]

    return pl.pallas_call(
        body, name=name, out_shape=jax.ShapeDtypeStruct((2, LRU_WIDTH, LRU_BLOCK), F32), compiler_params=_cparams()
    )(dw_bd)


def _swiglu_fwd(gu, name):
    (a,) = _rowwise(_swiglu_fn, name, [], [gu], [(D_FF, BF16)], ts=128)
    return a


def _swiglu_bwd(gu, da, name, deps=()):
    def fn(gu, da):
        _, vjp = jax.vjp(_swiglu_fn, gu)
        return vjp(da)[0]

    (dgu,) = _rowwise(fn, name, [], [gu, da], [(2 * D_FF, BF16)], ts=128, deps=deps)
    return dgu


def _shift_down(ext, j, rows):
    return pltpu.roll(ext, j, 0)[V7X_SUBLANES : V7X_SUBLANES + rows]


def _shift_up(ext, j, rows):
    return pltpu.roll(ext, ext.shape[0] - j, 0)[:rows] if j else ext[:rows]


def _conv_fwd(proj, w8, b, name):
    def fn(i, steps, w8, b, x, halo):
        halo = jnp.where(i > 0, halo, 0.0)
        ext = jnp.concatenate([halo, x], axis=0)
        acc = b + w8[3:4] * x
        for j in (1, 2, 3):
            acc = acc + w8[3 - j : 4 - j] * _shift_down(ext, j, x.shape[0])
        return acc, acc

    tiles = [(proj, LRU_WIDTH, 0), (proj, LRU_WIDTH, 0, "prev")]
    return _rowwise(fn, name, [w8, b], tiles, [(LRU_WIDTH, F32), (LRU_WIDTH, BF16)], with_index=True)


def _conv_bwd(proj, w8, d1, d2, name):
    def fn(i, steps, w8, x, halo, d1, d1n, d2, d2n):
        rows = x.shape[0]
        d = d1 + d2
        dn = jnp.where(i < steps - 1, d1n + d2n, 0.0)
        halo = jnp.where(i > 0, halo, 0.0)
        dext = jnp.concatenate([d, dn], axis=0)
        xext = jnp.concatenate([halo, x], axis=0)
        dx = w8[3:4] * d
        dw = [None] * 4
        dw[3] = _rowsum(d * x)
        for k in (1, 2, 3):
            dx = dx + w8[3 - k : 4 - k] * _shift_up(dext, k, rows)
            dw[3 - k] = _rowsum(d * _shift_down(xext, k, rows))
        dw8 = jnp.concatenate(dw + [jnp.zeros((4, LRU_WIDTH), F32)], axis=0)
        return dx, dw8, _rowsum(d)

    tiles = [(proj, LRU_WIDTH, 0), (proj, LRU_WIDTH, 0, "prev"), d1, (d1, LRU_WIDTH, 0, "next"),
             d2, (d2, LRU_WIDTH, 0, "next")]
    return _rowwise(fn, name, [w8], tiles, [(LRU_WIDTH, BF16)], [(8, LRU_WIDTH), (1, LRU_WIDTH)], with_index=True)


def _gates_fwd(pre, xc, ba, bx, lam, name):
    return _rowwise(_gates_fn, name, [ba, bx, lam], [pre, xc], [(LRU_WIDTH, F32), (LRU_WIDTH, F32)])


def _gates_bwd(pre, xc, ba, bx, lam, g, h_prev, name):
    def fn(ba, bx, lam, pre, xc, g, h_prev):
        _, vjp = jax.vjp(_gates_fn, ba, bx, lam, pre, xc)
        dba, dbx, dlam, dpre, dxc = vjp((g * h_prev, g))
        return dpre, dxc, dba, dbx, dlam

    row = (1, LRU_WIDTH)
    return _rowwise(fn, name, [ba, bx, lam], [pre, xc, g, h_prev],
                    [(2 * LRU_WIDTH, BF16), (LRU_WIDTH, F32)], [row, row, row])


SCAN_ROWS = 512


def _block_scan(a, b, row, reverse):
    for d in (1, 2, 4):
        if reverse:
            shift, keep = V7X_SUBLANES - d, row < V7X_SUBLANES - d
        else:
            shift, keep = d, row >= d
        a_s = pltpu.roll(a, shift, 0)
        b_s = pltpu.roll(b, shift, 0)
        b = jnp.where(keep, a * b_s + b, b)
        a = jnp.where(keep, a * a_s, a)
    return a, b


def _scan_fwd(a, u, name):
    s, w = a.shape
    ts = min(SCAN_ROWS, s)
    sub = ts // V7X_SUBLANES

    def body(a_ref, u_ref, h_ref, hp_ref, carry):
        @pl.when(pl.program_id(0) == 0)
        def _():
            carry[...] = jnp.zeros_like(carry)

        row = lax.broadcasted_iota(jnp.int32, (V7X_SUBLANES, w), 0)

        def step(j, c):
            rows = pl.ds(pl.multiple_of(j * V7X_SUBLANES, V7X_SUBLANES), V7X_SUBLANES)
            pa, pb = _block_scan(a_ref[rows, :], u_ref[rows, :], row, False)
            h = pb + pa * c
            h_ref[rows, :] = h
            hp_ref[rows, :] = jnp.where(row >= 1, pltpu.roll(h, 1, 0), c)
            return jnp.broadcast_to(h[V7X_SUBLANES - 1 :], (V7X_SUBLANES, w))

        carry[...] = lax.fori_loop(0, sub, step, carry[...])

    spec = pl.BlockSpec((ts, w), lambda i: (i, 0))
    return pl.pallas_call(
        body,
        name=name,
        grid=(s // ts,),
        in_specs=[spec, spec],
        out_specs=[spec, spec],
        out_shape=[jax.ShapeDtypeStruct((s, w), F32)] * 2,
        scratch_shapes=[pltpu.VMEM((V7X_SUBLANES, w), F32)],
        compiler_params=_cparams(dimension_semantics=("arbitrary",)),
    )(a, u)


def _scan_bwd(a, dh, name):
    s, w = a.shape
    ts = min(SCAN_ROWS, s)
    sub = ts // V7X_SUBLANES
    steps = s // ts

    def body(a_ref, d_ref, g_ref, carry):
        @pl.when(pl.program_id(0) == 0)
        def _():
            carry[...] = jnp.zeros_like(carry)

        row = lax.broadcasted_iota(jnp.int32, (V7X_SUBLANES, w), 0)

        def step(jj, c):
            j = sub - 1 - jj
            rows = pl.ds(pl.multiple_of(j * V7X_SUBLANES, V7X_SUBLANES), V7X_SUBLANES)
            av, dv = a_ref[rows, :], d_ref[rows, :]
            pa, pb = _block_scan(av, av * dv, row, True)
            big = pb + pa * c
            g_ref[rows, :] = dv + jnp.where(row < V7X_SUBLANES - 1, pltpu.roll(big, V7X_SUBLANES - 1, 0), c)
            return jnp.broadcast_to(big[:1], (V7X_SUBLANES, w))

        carry[...] = lax.fori_loop(0, sub, step, carry[...])

    spec = pl.BlockSpec((ts, w), lambda i: (steps - 1 - i, 0))
    return pl.pallas_call(
        body,
        name=name,
        grid=(steps,),
        in_specs=[spec, spec],
        out_specs=spec,
        out_shape=jax.ShapeDtypeStruct((s, w), F32),
        scratch_shapes=[pltpu.VMEM((V7X_SUBLANES, w), F32)],
        compiler_params=_cparams(dimension_semantics=("arbitrary",)),
    )(a, dh)


def _rel_index():
    i = np.arange(ATT_TQ)[:, None]
    j = np.arange(3 * ATT_TQ)[None, :]
    band = (j // CHUNK >= i // CHUNK) & (j // CHUNK <= i // CHUNK + LEFT_CHUNKS)
    return band


SKEW = 4 * ATT_TQ


def _skew_onehot():
    t = np.arange(SKEW)
    diag = np.where(t < 3 * ATT_TQ, -t, SKEW - t)
    idx = np.clip(diag + LEFT_CHUNKS * CHUNK, -MAX_REL, MAX_REL) + MAX_REL
    hit = (idx[:, None] == np.arange(2 * MAX_REL + 1)[None, :]) & (t[:, None] != 3 * ATT_TQ)
    return hit.astype(np.float32)


def _bias_tile(rel_bias):
    per_t = jnp.dot(rel_bias, jnp.asarray(_skew_onehot()).T, precision=lax.Precision.HIGHEST)
    flat = jnp.broadcast_to(per_t[:, None, :], (ATT_HEADS, ATT_TQ, SKEW)).reshape(ATT_HEADS, ATT_TQ * SKEW)
    tile = flat[:, : ATT_TQ * (SKEW - 1)].reshape(ATT_HEADS, ATT_TQ, SKEW - 1)[:, :, : 3 * ATT_TQ]
    first = (2 - np.arange(3))[:, None, None, None] * ATT_TQ
    seen = _rel_index()[None, None] & (np.arange(3 * ATT_TQ)[None, None, None, :] >= first)
    return jnp.where(jnp.asarray(seen), tile[None], NEG)


def _bias_grad(dbias):
    flat = jnp.pad(dbias, ((0, 0), (0, 0), (0, SKEW - 1 - 3 * ATT_TQ))).reshape(ATT_HEADS, ATT_TQ * (SKEW - 1))
    per_t = jnp.sum(jnp.pad(flat, ((0, 0), (0, ATT_TQ))).reshape(ATT_HEADS, ATT_TQ, SKEW), axis=1)
    return jnp.dot(per_t, jnp.asarray(_skew_onehot()), precision=lax.Precision.HIGHEST)


def _attn_specs(nt):
    qb, kb, vb = OFF_Q // V7X_LANES, OFF_K // V7X_LANES, OFF_V // V7X_LANES
    blk = (ATT_TQ, V7X_LANES)

    def qmap(base):
        return lambda hp, m: (jnp.minimum(m, nt - 1), base + hp)

    def wmap(base, back):
        return lambda hp, m: (jnp.clip(m - back, 0, nt - 1), base + hp)

    specs = [pl.BlockSpec(blk, qmap(qb))]
    specs += [pl.BlockSpec(blk, wmap(kb, back)) for back in (2, 1, 0)]
    specs += [pl.BlockSpec(blk, wmap(vb, back)) for back in (2, 1, 0)]
    return specs


ATT_SCALE = ATT_HEAD_DIM**-0.5


def _attn_exp(qh, kh, bias):
    s = lax.dot_general(qh, kh, _DIMS["nt"], preferred_element_type=F32) + bias
    e = jnp.exp(s - jnp.max(s, axis=-1, keepdims=True))
    return e, jnp.sum(e, axis=-1, keepdims=True)


def _attn_window(k0, k1, k2, v0, v1, v2):
    k = jnp.concatenate([k0[...], k1[...], k2[...]], axis=0).astype(BF16)
    v = jnp.concatenate([v0[...], v1[...], v2[...]], axis=0).astype(BF16)
    return k, v


def _bias_spec():
    return pl.BlockSpec((1, 2, ATT_TQ, 3 * ATT_TQ), lambda hp, m: (jnp.minimum(m, 2), hp, 0, 0))


def _attn_fwd(proj, bias, name):
    s = proj.shape[0]
    nt = s // ATT_TQ

    def body(q_ref, k0, k1, k2, v0, v1, v2, b_ref, o_ref):
        k, v = _attn_window(k0, k1, k2, v0, v1, v2)
        q = (q_ref[...] * ATT_SCALE).astype(BF16)
        for hh in range(2):
            cols = slice(hh * ATT_HEAD_DIM, (hh + 1) * ATT_HEAD_DIM)
            e, total = _attn_exp(q[:, cols], k[:, cols], b_ref[0, hh])
            o = jnp.dot(e.astype(BF16), v[:, cols], preferred_element_type=F32) / total
            o_ref[:, cols] = o.astype(o_ref.dtype)

    specs = _attn_specs(nt) + [_bias_spec()]
    return pl.pallas_call(
        body,
        name=name,
        grid=(ATT_HEADS // 2, nt),
        in_specs=specs,
        out_specs=pl.BlockSpec((ATT_TQ, V7X_LANES), lambda hp, m: (m, hp)),
        out_shape=jax.ShapeDtypeStruct((s, ATT_WIDTH), BF16),
        compiler_params=_cparams(dimension_semantics=("parallel", "arbitrary")),
    )(proj, proj, proj, proj, proj, proj, proj, bias)


def _attn_bwd(proj, bias, do, name):
    s = proj.shape[0]
    nt = s // ATT_TQ
    win = 3 * ATT_TQ

    def body(q_ref, k0, k1, k2, v0, v1, v2, do_ref, b_ref, dq_ref, dk_ref, dv_ref, db_ref, dk_acc, dv_acc):
        m = pl.program_id(1)

        @pl.when(m == 0)
        def _():
            dk_acc[...] = jnp.zeros_like(dk_acc)
            dv_acc[...] = jnp.zeros_like(dv_acc)
            db_ref[...] = jnp.zeros_like(db_ref)

        @pl.when(m < nt)
        def _():
            k, v = _attn_window(k0, k1, k2, v0, v1, v2)
            q = (q_ref[...] * ATT_SCALE).astype(BF16)
            dout = do_ref[...]
            for hh in range(2):
                cols = slice(hh * ATT_HEAD_DIM, (hh + 1) * ATT_HEAD_DIM)
                qh, kh, vh, doh = q[:, cols], k[:, cols], v[:, cols], dout[:, cols]
                e, total = _attn_exp(qh, kh, b_ref[0, hh])
                p = e / total
                dvh = lax.dot_general(p.astype(BF16), doh, _DIMS["tn"], preferred_element_type=F32)
                dp = lax.dot_general(doh, vh, _DIMS["nt"], preferred_element_type=F32)
                ds = p * (dp - jnp.sum(dp * p, axis=-1, keepdims=True))
                db_ref[hh] += ds
                dsb = ds.astype(BF16)
                dqh = jnp.dot(dsb, kh, preferred_element_type=F32) * ATT_SCALE
                dkh = lax.dot_general(dsb, qh, _DIMS["tn"], preferred_element_type=F32)
                dq_ref[:, cols] = dqh.astype(dq_ref.dtype)
                dk_acc[:, cols] += dkh
                dv_acc[:, cols] += dvh

        dk_ref[...] = dk_acc[:ATT_TQ].astype(dk_ref.dtype)
        dv_ref[...] = dv_acc[:ATT_TQ].astype(dv_ref.dtype)
        for acc in (dk_acc, dv_acc):
            rest = acc[ATT_TQ:]
            acc[: win - ATT_TQ] = rest
            acc[win - ATT_TQ :] = jnp.zeros((ATT_TQ, V7X_LANES), F32)

    blk = (ATT_TQ, V7X_LANES)
    specs = _attn_specs(nt)
    specs.append(pl.BlockSpec(blk, lambda hp, m: (jnp.minimum(m, nt - 1), hp)))
    specs.append(_bias_spec())
    done = lambda hp, m: (jnp.maximum(m - 2, 0), hp)
    out_specs = [
        pl.BlockSpec(blk, lambda hp, m: (jnp.minimum(m, nt - 1), hp)),
        pl.BlockSpec(blk, done),
        pl.BlockSpec(blk, done),
        pl.BlockSpec((2, ATT_TQ, win), lambda hp, m: (hp, 0, 0)),
    ]
    out_shape = [jax.ShapeDtypeStruct((s, ATT_WIDTH), BF16)] * 3
    out_shape.append(jax.ShapeDtypeStruct((ATT_HEADS, ATT_TQ, win), F32))
    return pl.pallas_call(
        body,
        name=name,
        grid=(ATT_HEADS // 2, nt + 2),
        in_specs=specs,
        out_specs=out_specs,
        out_shape=out_shape,
        scratch_shapes=[pltpu.VMEM((win, V7X_LANES), F32), pltpu.VMEM((win, V7X_LANES), F32)],
        compiler_params=_cparams(dimension_semantics=("arbitrary", "arbitrary")),
    )(proj, proj, proj, proj, proj, proj, proj, do, bias)


def _ada_fwd(c_all, w, name):
    def body(c_ref, w_ref, o_ref):
        act = _silu(c_ref[...]).astype(BF16)
        o_ref[...] = jnp.dot(act, w_ref[...].astype(BF16), preferred_element_type=F32)

    return pl.pallas_call(
        body, name=name, out_shape=jax.ShapeDtypeStruct((c_all.shape[0], w.shape[1]), F32), compiler_params=_cparams()
    )(c_all, w)


def _ada_bwd(c_all, dmod, name):
    def body(c_ref, d_ref, o_ref):
        act = _silu(c_ref[...])
        o_ref[...] = lax.dot_general(act, d_ref[...], _DIMS["tn"], preferred_element_type=F32,
                                     precision=lax.Precision.HIGHEST)

    return pl.pallas_call(
        body, name=name, out_shape=jax.ShapeDtypeStruct((c_all.shape[1], dmod.shape[1]), F32), compiler_params=_cparams()
    )(c_all, dmod)


def _adamw_parts(landed, sent, me, w, m, v, name, rows=256):
    r, c = w.shape
    tr = _pick(r, rows, 16)

    def body(me_ref, g_ref, own_ref, w_ref, m_ref, v_ref, go_ref, d_ref, mo_ref, vo_ref):
        mine = me_ref[0]
        grad = jnp.zeros((tr, c), F32)
        for d in range(N_DEV):
            grad = grad + jnp.where(mine == d, own_ref[0], g_ref[d]).astype(F32)
        _adamw_update(grad, w_ref, m_ref, v_ref, go_ref, d_ref, mo_ref, vo_ref)

    spec = pl.BlockSpec((tr, c), lambda i, me_ref: (i, 0))
    return pl.pallas_call(
        body,
        name=name,
        grid_spec=pltpu.PrefetchScalarGridSpec(
            num_scalar_prefetch=1,
            grid=(r // tr,),
            in_specs=[pl.BlockSpec((N_DEV, tr, c), lambda i, me_ref: (0, i, 0)),
                      pl.BlockSpec((1, tr, c), lambda i, me_ref: (me_ref[0], i, 0)), spec, spec, spec],
            out_specs=[spec] * 4,
        ),
        out_shape=[jax.ShapeDtypeStruct((r, c), F32)] * 4,
        compiler_params=_cparams(dimension_semantics=("parallel",)),
    )(me.reshape(1).astype(jnp.int32), landed, sent, w, m, v)


def _adamw_update(grad, w_ref, m_ref, v_ref, go_ref, d_ref, mo_ref, vo_ref):
    m2 = ADAM_B1 * m_ref[...] + (1.0 - ADAM_B1) * grad
    v2 = ADAM_B2 * v_ref[...] + (1.0 - ADAM_B2) * (grad * grad)
    m_hat = m2 / (1.0 - ADAM_B1**ADAM_STEP)
    v_hat = v2 / (1.0 - ADAM_B2**ADAM_STEP)
    go_ref[...] = grad
    d_ref[...] = -ADAM_LR * (m_hat / (jnp.sqrt(v_hat) + ADAM_EPS) + ADAM_WD * w_ref[...])
    mo_ref[...] = m2
    vo_ref[...] = v2


def _adamw(g, w, m, v, name, rows=256):
    r, c = w.shape
    tr = _pick(r, rows, 16)

    def body(g_ref, w_ref, m_ref, v_ref, go_ref, d_ref, mo_ref, vo_ref):
        _adamw_update(g_ref[...], w_ref, m_ref, v_ref, go_ref, d_ref, mo_ref, vo_ref)

    spec = pl.BlockSpec((tr, c), lambda i: (i, 0))
    return pl.pallas_call(
        body,
        name=name,
        grid=(r // tr,),
        in_specs=[spec, spec, spec, spec],
        out_specs=[spec] * 4,
        out_shape=[jax.ShapeDtypeStruct((r, c), F32)] * 4,
        compiler_params=_cparams(dimension_semantics=("parallel",)),
    )(g, w, m, v)


def _sum_parts(parts, name):
    def body(p_ref, o_ref):
        acc = p_ref[0]
        for d in range(1, N_DEV):
            acc = acc + p_ref[d]
        o_ref[...] = acc

    return pl.pallas_call(
        body, name=name, out_shape=jax.ShapeDtypeStruct(parts.shape[1:], F32), compiler_params=_cparams()
    )(parts)


def _place():
    x, y, c = lax.axis_index("x"), lax.axis_index("y"), lax.axis_index("c")
    return x, y, c


def _dev_index(p):
    return 4 * p[0] + 2 * p[1] + p[2]


def _allgather_vmem(shard, name):
    m_per, n = shard.shape

    def body(x_ref, out_ref, send_sems, recv_sems, local_sem):
        x, y, c = _place()
        me, sibling = (x, y, c), (x, y, 1 - c)
        chips = [(1 - x, y), (x, 1 - y), (1 - x, 1 - y)]

        def rows(p):
            return out_ref.at[pl.ds(_dev_index(p) * m_per, m_per), :]

        def copy(k, block, to, src=None):
            return pltpu.make_async_remote_copy(
                src_ref=rows(block) if src is None else src, dst_ref=rows(block),
                send_sem=send_sems.at[k], recv_sem=recv_sems.at[k], device_id=to, device_id_type=MESH)

        mine = pltpu.make_async_copy(x_ref, rows(me), local_sem)
        mine.start()
        first = [copy(0, me, sibling, src=x_ref)]
        first += [copy(1 + j, me, (*chip, c), src=x_ref) for j, chip in enumerate(chips)]
        for cp in first:
            cp.start()
        passed = [copy(4 + j, (*chip, c), sibling) for j, chip in enumerate(chips)]
        for j, chip in enumerate(chips):
            copy(1 + j, (*chip, c), me).wait_recv()
            passed[j].start()
        copy(0, sibling, me).wait_recv()
        for j, chip in enumerate(chips):
            copy(4 + j, (*chip, 1 - c), me).wait_recv()
        for cp in first + passed:
            cp.wait_send()
        mine.wait()

    return pl.pallas_call(
        body,
        name=name,
        out_shape=jax.ShapeDtypeStruct((N_DEV * m_per, n), shard.dtype),
        in_specs=[pl.BlockSpec(memory_space=pltpu.VMEM)],
        out_specs=pl.BlockSpec(memory_space=pltpu.VMEM),
        scratch_shapes=[pltpu.SemaphoreType.DMA((7,)), pltpu.SemaphoreType.DMA((7,)), pltpu.SemaphoreType.DMA],
        compiler_params=_cparams(),
    )(shard)


def _allgather_hbm(shards, name):
    n = len(shards)

    def body(*refs):
        ins, outs = refs[:n], refs[n : 2 * n]
        send_sems, recv_sems, local_sems = refs[2 * n :]
        x, y, c = _place()
        me, sibling = (x, y, c), (x, y, 1 - c)
        chips = [(1 - x, y), (x, 1 - y), (1 - x, 1 - y)]

        def copy(a, k, block, to, src=None):
            dst = outs[a].at[_dev_index(block)]
            return pltpu.make_async_remote_copy(
                src_ref=dst if src is None else src, dst_ref=dst,
                send_sem=send_sems.at[a * 7 + k], recv_sem=recv_sems.at[a * 7 + k], device_id=to, device_id_type=MESH)

        mine = [pltpu.make_async_copy(ins[a], outs[a].at[_dev_index(me)], local_sems.at[a]) for a in range(n)]
        for cp in mine:
            cp.start()
        first = []
        for a in range(n):
            first.append(copy(a, 0, me, sibling, src=ins[a]))
            first += [copy(a, 1 + j, me, (*chip, c), src=ins[a]) for j, chip in enumerate(chips)]
        for cp in first:
            cp.start()
        passed = []
        for j, chip in enumerate(chips):
            for a in range(n):
                copy(a, 1 + j, (*chip, c), me).wait_recv()
                cp = copy(a, 4 + j, (*chip, c), sibling)
                cp.start()
                passed.append(cp)
        for a in range(n):
            copy(a, 0, sibling, me).wait_recv()
        for j, chip in enumerate(chips):
            for a in range(n):
                copy(a, 4 + j, (*chip, 1 - c), me).wait_recv()
        for cp in first + passed:
            cp.wait_send()
        for cp in mine:
            cp.wait()

    any_spec = pl.BlockSpec(memory_space=pl.ANY)
    return pl.pallas_call(
        body,
        name=name,
        out_shape=[jax.ShapeDtypeStruct((N_DEV, *s.shape), s.dtype) for s in shards],
        in_specs=[any_spec] * n,
        out_specs=[any_spec] * n,
        scratch_shapes=[pltpu.SemaphoreType.DMA((7 * n,)), pltpu.SemaphoreType.DMA((7 * n,)),
                        pltpu.SemaphoreType.DMA((n,))],
        compiler_params=_cparams(),
    )(*shards)


def _exchange_hbm(bufs, name):
    n = len(bufs)

    def body(*refs):
        ins, outs = refs[:n], refs[n : 2 * n]
        send_sems, recv_sems, local_sems = refs[2 * n :]
        x, y, c = _place()
        me = _dev_index((x, y, c))
        mine = [pltpu.make_async_copy(ins[a].at[me], outs[a].at[me], local_sems.at[a]) for a in range(n)]
        for cp in mine:
            cp.start()
        def peer_of(k):
            return (1 - x if k & 4 else x, 1 - y if k & 2 else y, 1 - c if k & 1 else c)

        copies = []
        for k in range(1, N_DEV):
            peer = peer_of(k)
            for a in range(n):
                copies.append(pltpu.make_async_remote_copy(
                    src_ref=ins[a].at[_dev_index(peer)], dst_ref=outs[a].at[me],
                    send_sem=send_sems.at[a * 7 + k - 1], recv_sem=recv_sems.at[a * 7 + k - 1],
                    device_id=peer, device_id_type=MESH))
        for cp in copies:
            cp.start()
        for k in range(1, N_DEV):
            peer = peer_of(k)
            for a in range(n):
                pltpu.make_async_remote_copy(
                    src_ref=ins[a].at[me], dst_ref=outs[a].at[_dev_index(peer)],
                    send_sem=send_sems.at[a * 7 + k - 1], recv_sem=recv_sems.at[a * 7 + k - 1],
                    device_id=peer, device_id_type=MESH).wait_recv()
        for cp in copies:
            cp.wait_send()
        for cp in mine:
            cp.wait()

    any_spec = pl.BlockSpec(memory_space=pl.ANY)
    return pl.pallas_call(
        body,
        name=name,
        out_shape=[jax.ShapeDtypeStruct(b.shape, b.dtype) for b in bufs],
        in_specs=[any_spec] * n,
        out_specs=[any_spec] * n,
        scratch_shapes=[pltpu.SemaphoreType.DMA((7 * n,)), pltpu.SemaphoreType.DMA((7 * n,)),
                        pltpu.SemaphoreType.DMA((n,))],
        compiler_params=_cparams(),
    )(*bufs)


HBM_SPEC = pl.BlockSpec(memory_space=pltpu.HBM)
SEM_SPEC = pl.BlockSpec(memory_space=pltpu.SEMAPHORE)
EFFECT = pltpu.SideEffectType.DATAFLOW_SIDE_EFFECTING


def _peers(x, y, c):
    return [(1 - x if k & 4 else x, 1 - y if k & 2 else y, 1 - c if k & 1 else c) for k in range(1, N_DEV)]


def _push_start(groups, sliced, name, after=()):
    flat = [b for g in groups for b in g]
    n, ng = len(flat), len(groups)
    sizes = [len(g) for g in groups]
    lands = [lax.empty(b.shape if sliced else (N_DEV, *b.shape), b.dtype) for b in flat]

    def body(*refs):
        ins, lnd = refs[:n], refs[n : 2 * n]
        sems = refs[2 * n + len(after) : 2 * n + len(after) + 2 * ng]
        token = refs[-1]
        x, y, c = _place()
        me = _dev_index((x, y, c))
        first = 0
        for gi, size in enumerate(sizes):
            for k, peer in enumerate(_peers(x, y, c)):
                for j in range(first, first + size):
                    sem = (j - first) * 7 + k
                    pltpu.make_async_remote_copy(
                        src_ref=ins[j].at[_dev_index(peer)] if sliced else ins[j], dst_ref=lnd[j].at[me],
                        send_sem=sems[2 * gi].at[sem], recv_sem=sems[2 * gi + 1].at[sem],
                        device_id=peer, device_id_type=MESH).start()
            first += size
        token[...] = jnp.zeros_like(token)

    out_shape = []
    for size in sizes:
        out_shape += [pltpu.SemaphoreType.DMA((7 * size,)), pltpu.SemaphoreType.DMA((7 * size,))]
    out_shape += [pltpu.HBM(b.shape, b.dtype) for b in flat + lands]
    out_shape.append(jax.ShapeDtypeStruct((V7X_SUBLANES, V7X_LANES), F32))
    res = pl.pallas_call(
        body,
        name=name,
        out_shape=tuple(out_shape),
        in_specs=[HBM_SPEC] * (2 * n) + [ANY_SPEC] * len(after),
        out_specs=tuple([SEM_SPEC] * (2 * ng) + [HBM_SPEC] * (2 * n) + [pl.BlockSpec(memory_space=pltpu.VMEM)]),
        input_output_aliases={i: 2 * ng + i for i in range(2 * n)},
        compiler_params=pltpu.CompilerParams(has_side_effects=EFFECT),
    )(*[pltpu.with_memory_space_constraint(b, pltpu.HBM) for b in flat + lands], *after)
    sems, thru, token = res[: 2 * ng], res[2 * ng : 2 * ng + 2 * n], res[-1]
    out, first = [], 0
    for gi, size in enumerate(sizes):
        out.append((sems[2 * gi], sems[2 * gi + 1], list(thru[first : first + size]),
                    list(thru[n + first : n + first + size])))
        first += size
    return out, token


def _push_wait(started, sliced, after, name):
    send_sems, recv_sems, bufs, lands = started
    n = len(bufs)

    def body(*refs):
        ins, lnd = refs[:n], refs[n : 2 * n]
        send_ref, recv_ref = refs[2 * n], refs[2 * n + 1]
        x, y, c = _place()
        for k, peer in enumerate(_peers(x, y, c)):
            for j in range(n):
                cp = pltpu.make_async_remote_copy(
                    src_ref=ins[j].at[_dev_index(peer)] if sliced else ins[j], dst_ref=lnd[j].at[_dev_index(peer)],
                    send_sem=send_ref.at[j * 7 + k], recv_sem=recv_ref.at[j * 7 + k],
                    device_id=peer, device_id_type=MESH)
                cp.wait_send()
                cp.wait_recv()

    res = pl.pallas_call(
        body,
        name=name,
        out_shape=tuple(pltpu.HBM(b.shape, b.dtype) for b in bufs + lands),
        in_specs=[HBM_SPEC] * (2 * n) + [SEM_SPEC, SEM_SPEC, pl.BlockSpec(memory_space=pl.ANY)],
        out_specs=tuple([HBM_SPEC] * (2 * n)),
        input_output_aliases={i: i for i in range(2 * n)},
        compiler_params=pltpu.CompilerParams(has_side_effects=EFFECT),
    )(*bufs, *lands, send_sems, recv_sems, after)
    return list(res[n:])


def _with_own_row(land, own, me):
    return lax.dynamic_update_index_in_dim(land, own, me, 0)


def _cols_full(g):
    return jnp.transpose(g, (1, 0, 2)).reshape(g.shape[1], -1)


def _rows_full(g):
    return g.reshape(-1, g.shape[2])


def _cols_parts(full, n=N_DEV):
    r = full.shape[0]
    return jnp.transpose(full.reshape(r, n, -1), (1, 0, 2)).astype(BF16)


def _rows_parts(full):
    return full.reshape(N_DEV, -1, full.shape[1]).astype(BF16)


def _block_diag(w):
    eye = jnp.eye(LRU_BLOCKS, dtype=w.dtype)
    return jnp.einsum("nkj,nm->nkmj", w, eye).reshape(LRU_WIDTH, LRU_WIDTH)


def _pad_rows(v, rows):
    flat = v.reshape(-1)
    return jnp.pad(flat, (0, rows * D_MODEL - flat.shape[0])).reshape(rows, D_MODEL)


def _my_cols(full, me, width):
    return lax.dynamic_slice_in_dim(full, me * width, width, axis=full.ndim - 1)


def kernel(x, c, w_ada, b_ada, norm_pre, norm_post, ffn1_w_gu, ffn1_w_down, w_in, rel_bias, conv_w, conv_b, lru_wa, lru_ba, lru_wx, lru_bx, lru_lambda, w_att_o, w_rec_o, w_out, ffn2_w_gu, ffn2_w_down, loss_target, m_w_ada, m_b_ada, m_norm_pre, m_norm_post, m_ffn1_w_gu, m_ffn1_w_down, m_w_in, m_rel_bias, m_conv_w, m_conv_b, m_lru_wa, m_lru_ba, m_lru_wx, m_lru_bx, m_lru_lambda, m_w_att_o, m_w_rec_o, m_w_out, m_ffn2_w_gu, m_ffn2_w_down, v_w_ada, v_b_ada, v_norm_pre, v_norm_post, v_ffn1_w_gu, v_ffn1_w_down, v_w_in, v_rel_bias, v_conv_w, v_conv_b, v_lru_wa, v_lru_ba, v_lru_wx, v_lru_bx, v_lru_lambda, v_w_att_o, v_w_rec_o, v_w_out, v_ffn2_w_gu, v_ffn2_w_down):
    weights = dict(w_ada=w_ada, b_ada=b_ada, norm_pre=norm_pre, norm_post=norm_post, ffn1_w_gu=ffn1_w_gu,
                   ffn1_w_down=ffn1_w_down, w_in=w_in, rel_bias=rel_bias, conv_w=conv_w, conv_b=conv_b,
                   lru_wa=lru_wa, lru_ba=lru_ba, lru_wx=lru_wx, lru_bx=lru_bx, lru_lambda=lru_lambda,
                   w_att_o=w_att_o, w_rec_o=w_rec_o, w_out=w_out, ffn2_w_gu=ffn2_w_gu, ffn2_w_down=ffn2_w_down)
    mom1 = dict(w_ada=m_w_ada, b_ada=m_b_ada, norm_pre=m_norm_pre, norm_post=m_norm_post, ffn1_w_gu=m_ffn1_w_gu,
                ffn1_w_down=m_ffn1_w_down, w_in=m_w_in, rel_bias=m_rel_bias, conv_w=m_conv_w, conv_b=m_conv_b,
                lru_wa=m_lru_wa, lru_ba=m_lru_ba, lru_wx=m_lru_wx, lru_bx=m_lru_bx, lru_lambda=m_lru_lambda,
                w_att_o=m_w_att_o, w_rec_o=m_w_rec_o, w_out=m_w_out, ffn2_w_gu=m_ffn2_w_gu, ffn2_w_down=m_ffn2_w_down)
    mom2 = dict(w_ada=v_w_ada, b_ada=v_b_ada, norm_pre=v_norm_pre, norm_post=v_norm_post, ffn1_w_gu=v_ffn1_w_gu,
                ffn1_w_down=v_ffn1_w_down, w_in=v_w_in, rel_bias=v_rel_bias, conv_w=v_conv_w, conv_b=v_conv_b,
                lru_wa=v_lru_wa, lru_ba=v_lru_ba, lru_wx=v_lru_wx, lru_bx=v_lru_bx, lru_lambda=v_lru_lambda,
                w_att_o=v_w_att_o, w_rec_o=v_w_rec_o, w_out=v_w_out, ffn2_w_gu=v_ffn2_w_gu, ffn2_w_down=v_ffn2_w_down)
    order = list(weights)
    big = ["ffn1_w_gu", "ffn1_w_down", "w_in", "w_att_o", "w_rec_o", "w_out", "ffn2_w_gu", "ffn2_w_down"]
    col_sharded = {"ffn1_w_gu", "w_in", "w_att_o", "ffn2_w_gu"}
    small = ["b_ada", "norm_pre", "norm_post", "rel_bias", "conv_w", "conv_b", "lru_wa", "lru_ba", "lru_wx",
             "lru_bx", "lru_lambda"]

    xi, yi, ci = _place()
    me = _dev_index((xi, yi, ci))
    x0 = x[0]
    target = loss_target[0]
    fuse_tm = min(FUSE_TM, x0.shape[0])

    transposed = {"ffn1_w_gu", "w_in", "ffn2_w_gu"}
    local = lambda n, arr: jnp.transpose(arr[0]) if n in transposed else arr[0]
    shards = {n: local(n, weights[n]).astype(BF16) for n in big}
    full_of = lambda n, g: _cols_full(g) if n == "w_att_o" else _rows_full(g)

    pack = jnp.concatenate([c.reshape(-1), norm_pre.reshape(-1), norm_post.reshape(-1), conv_w.reshape(-1)])
    pack = jnp.pad(pack, (0, 3072 - pack.shape[0])).reshape(8, 384)
    got = _allgather_vmem(pack, "gather_small_inputs").reshape(N_DEV, 3072)
    c_all = got[:, :1024]
    unshard = lambda blk, rows: jnp.transpose(blk.reshape(N_DEV, rows, 128), (1, 0, 2)).reshape(rows, D_MODEL)
    g_pre = unshard(got[:, 1024:1408], 3)
    g_post = unshard(got[:, 1408:1792], 3)
    conv_taps = unshard(got[:, 1792:2304], 4)
    conv_w8 = jnp.concatenate([conv_taps, jnp.zeros((4, LRU_WIDTH), F32)], axis=0)

    mod_cols = _ada_fwd(c_all, w_ada[0], "ada_fwd")
    mod_all = _allgather_vmem(mod_cols, "gather_mod").reshape(N_DEV, N_DEV, 1152)
    mod = lax.dynamic_index_in_dim(mod_all, me, axis=1, keepdims=False).reshape(1, -1) + b_ada
    mod = mod.reshape(3, 3, 1, D_MODEL)

    w_bd = jnp.concatenate([_block_diag(lru_wa[0]), _block_diag(lru_wx[0])], axis=1).astype(BF16)
    bias = _bias_tile(rel_bias[0])

    res_w = (0.5, 1.0, 0.5)
    row = lambda v: v.reshape(1, -1)

    (w1_gu,) = _allgather_hbm([shards["ffn1_w_gu"]], "gather_ffn1_w_gu")
    weight_groups = [["ffn1_w_down"], ["w_in"], ["w_att_o", "w_rec_o", "w_out"], ["ffn2_w_gu", "ffn2_w_down"]]
    weights_started, started = _push_start([[shards[n] for n in g] for g in weight_groups], False,
                                           "gather_weights_start", after=(mod, w1_gu))
    full = {"ffn1_w_gu": _rows_full(w1_gu)}

    def gathered_group(gi, after):
        lands = _push_wait(weights_started[gi], False, after, f"gather_weights_wait{gi}")
        for n, land in zip(weight_groups[gi], lands):
            full[n] = full_of(n, jnp.where(is_me, shards[n][None], land))

    is_me = (jnp.arange(N_DEV) == me)[:, None, None]

    def ffn_fwd(xin, k, gi, tag, deps=()):
        h, a, g, u = _pre_up(xin, row(g_pre[k]), mod[k, 0], mod[k, 1], full[f"{tag}_w_gu"], f"{tag}_up", deps=deps)
        if f"{tag}_w_down" not in full:
            gathered_group(gi, a)
        f, xout = _matmul_post(a, full[f"{tag}_w_down"], xin, row(g_post[k]), mod[k, 2], res_w[k], f"{tag}_down")
        return xout, (h, g, u, a, f)

    x1, saved1 = ffn_fwd(x0, 0, 0, "ffn1", deps=(started,))

    gathered_group(1, x1)
    h2, proj = _pre_matmul(x1, row(g_pre[1]), mod[1, 0], mod[1, 1], full["w_in"], "mix_in",
                           b_shift=3 * ATT_WIDTH // 512)
    att_o = _attn_fwd(proj, bias, "attn_fwd")
    gathered_group(2, att_o)
    xc, xcb = _conv_fwd(proj, conv_w8, conv_b, "conv_fwd")
    pre = _matmul(xcb, w_bd, "nn", F32, "lru_gate_proj")
    a_t, u_t = _gates_fwd(pre, xc, lru_ba, lru_bx, lru_lambda, "lru_gates")
    hs, h_prev = _scan_fwd(a_t, u_t, "lru_scan")
    (rec_in,) = _rowwise(_recin_fn, "rec_in", [], [hs, (proj, LRU_WIDTH, 1)], [(LRU_WIDTH, BF16)])
    att = _matmul(att_o, full["w_att_o"], "nn", F32, "att_out")
    rec = _matmul(rec_in, full["w_rec_o"], "nn", F32, "rec_out")
    (merged,) = _rowwise(_merge_fn, "merge", [], [att, rec, (proj, LRU_WIDTH, 2), (proj, LRU_WIDTH, 3)],
                         [(D_MODEL, BF16)])
    f2, x2 = _matmul_post(merged, full["w_out"], x1, row(g_post[1]), mod[1, 2], res_w[1], "mix_out")

    gathered_group(3, x2)
    x3, saved3 = ffn_fwd(x2, 2, 2, "ffn2")

    dy, sq = _loss_stage(x3, target, "loss")
    loss = lax.psum(0.5 * jnp.sum(sq) / D_MODEL, ("x", "y", "c"))

    grads = {}
    dmod = [[None] * 3 for _ in range(3)]
    d_pre, d_post = [None] * 3, [None] * 3

    pending = []

    def exchange_start(names, tag, after=()):
        send = [(_cols_parts if n == "w_att_o" else _rows_parts)(grads[n]) for n in names]
        (group,), token = _push_start([send], True, f"exchange_{tag}_start", after=after)
        pending.append((names, send, group, tag))
        return token

    def exchange_finish(names, send, group, tag, after):
        lands = _push_wait(group, True, after, f"exchange_{tag}_wait")
        res = None
        for n, land, mine in zip(names, lands, send):
            res = _adamw_parts(land, mine, me, local(n, weights[n]), local(n, mom1[n]), local(n, mom2[n]),
                               f"adamw_{n}")
            back = (lambda r: jnp.transpose(r)) if n in transposed else (lambda r: r)
            out_g[n], out_d[n], out_m[n], out_v[n] = [back(r).reshape(weights[n].shape) for r in res]
        return res[0]

    out_g, out_d, out_m, out_v = {}, {}, {}, {}

    def ffn_bwd(xin, k, saved, dout, tag):
        h, g, u, a, f = saved
        w_gu, w_down = f"{tag}_w_gu", f"{tag}_w_down"
        df, dgu, d_post[k], dmod[k][2] = _post_bwd_up_bwd(f, dout, row(g_post[k]), mod[k, 2], res_w[k], full[w_down],
                                                          g, u, f"{tag}_up_bwd")
        grads[w_down] = _matmul(a, df, "tn", BF16, f"{tag}_dw_down", tm=1408, tn=1024, tk=1024)
        started = exchange_start([w_down], w_down)
        grads[w_gu] = _dw_gu(dgu, h, f"{tag}_dw_gu", deps=(started,))
        started = exchange_start([w_gu], w_gu)
        halves = [(dgu, (None, fuse_tm, D_FF), lambda i, half=half: (half, i, 0), (half * D_FF, (half + 1) * D_FF))
                  for half in range(2)]
        dx, d_pre[k], dmod[k][0], dmod[k][1] = _matmul_pre_bwd(halves, full[w_gu], xin, dout, row(g_pre[k]),
                                                                mod[k, 0], mod[k, 1], f"{tag}_dh", deps=(started,))
        return dx

    dx2 = ffn_bwd(x2, 2, saved3, dy, "ffn2")

    df2, dmerged, d_post[1], dmod[1][2] = _post_bwd_matmul(f2, dx2, row(g_post[1]), mod[1, 2], res_w[1],
                                                           full["w_out"], "mix_dmerged")
    grads["w_out"] = _matmul(merged, df2, "tn", BF16, "mix_dw_out", tm=1024, tn=1024, tk=1024)

    def merge_bwd(att, rec, g_att, g_rec, dm):
        _, vjp = jax.vjp(_merge_fn, att, rec, g_att, g_rec)
        return vjp(dm)

    datt, drec, dg_att, dg_rec = _rowwise(
        merge_bwd, "merge_bwd", [], [att, rec, (proj, LRU_WIDTH, 2), (proj, LRU_WIDTH, 3), dmerged],
        [(D_MODEL, BF16)] * 4)
    datt_o = _matmul(datt, full["w_att_o"], "nt", BF16, "att_out_bwd")
    grads["w_att_o"] = _matmul(att_o, datt, "tn", BF16, "dw_att_o", tm=512, tn=1024, tk=1024)
    drec_in = _matmul(drec, full["w_rec_o"], "nt", F32, "rec_out_bwd")
    grads["w_rec_o"] = _matmul(rec_in, drec, "tn", BF16, "dw_rec_o", tm=1024, tn=1024, tk=1024)
    started = exchange_start(["w_out", "w_att_o", "w_rec_o"], "mix_out")

    def recin_bwd(hs, yr, d):
        _, vjp = jax.vjp(_recin_fn, hs, yr)
        return vjp(d)

    dhs, dyr = _rowwise(recin_bwd, "rec_in_bwd", [], [hs, (proj, LRU_WIDTH, 1), drec_in],
                        [(LRU_WIDTH, F32), (LRU_WIDTH, BF16)], deps=(started,))
    g_t = _scan_bwd(a_t, dhs, "lru_scan_bwd")
    dpre, dxc_direct, d_ba, d_bx, d_lam = _gates_bwd(pre, xc, lru_ba, lru_bx, lru_lambda, g_t, h_prev,
                                                     "lru_gates_bwd")
    dxc_mm = _matmul(dpre, w_bd, "nt", F32, "lru_gate_proj_bwd")
    dw_bd = _matmul(xcb, dpre, "tn", F32, "dw_lru_gate", tm=1024, tn=1024, tk=1024)
    dxr, d_conv_w8, d_conv_b = _conv_bwd(proj, conv_w8, dxc_direct, dxc_mm, "conv_bwd")
    dq, dk, dv, dbias = _attn_bwd(proj, bias, datt_o, "attn_bwd")
    dproj = jnp.concatenate([dq, dk, dv, dxr, dyr, dg_att, dg_rec], axis=1)
    grads["w_in"] = _matmul(dproj, h2, "tn", BF16, "mix_dw_in", tm=1408, tn=1024, tk=1024)
    pack_mix = jnp.concatenate([d_conv_w8[:4], d_conv_b, d_ba, d_bx, d_lam, _pad_rows(_bias_grad(dbias), 3),
                                _lru_diag_blocks(dw_bd, "lru_diag_blocks").reshape(128, D_MODEL),
                                jnp.zeros((5, D_MODEL), F32)], axis=0)
    (mix_started,), started = _push_start([[pack_mix]], False, "small_grads_mix_start")
    started = exchange_start(["w_in"], "w_in", after=(started,))
    whole = [(dproj, (fuse_tm, PROJ_WIDTH), lambda i: (i, 0), (0, PROJ_WIDTH))]
    dx1, d_pre[1], dmod[1][0], dmod[1][1] = _matmul_pre_bwd(whole, full["w_in"], x1, dx2, row(g_pre[1]), mod[1, 0],
                                                             mod[1, 1], "mix_dh", deps=(started,))

    dx0 = ffn_bwd(x0, 0, saved1, dx1, "ffn1")

    dmod_mine = jnp.concatenate([dmod[k][j] for k in range(3) for j in range(3)], axis=0)
    pack_norm = jnp.concatenate([dmod_mine, *d_pre, *d_post, jnp.zeros((1, D_MODEL), F32)], axis=0)
    (norm_started,), _ = _push_start([[pack_norm]], False, "small_grads_norm_start")

    def summed(started, pack, after, tag):
        (land,) = _push_wait(started, False, after, f"small_grads_{tag}_wait")
        parts = jnp.where(is_me, pack[None], land)
        return parts, _sum_parts(parts, f"small_grads_{tag}_sum")

    done = dx0
    last = [p for p in pending if p[3].startswith("ffn1")]
    for names, send, group, tag in pending:
        if not tag.startswith("ffn1"):
            done = exchange_finish(names, send, group, tag, done)

    _, total = summed(mix_started, pack_mix, done, "mix")
    grads["conv_w"] = _my_cols(total[0:4], me, 128)
    grads["conv_b"] = total[4:5]
    grads["lru_ba"] = total[5:6]
    grads["lru_bx"] = total[6:7]
    grads["lru_lambda"] = total[7:8]
    grads["rel_bias"] = total[8:11].reshape(-1)[: ATT_HEADS * (2 * MAX_REL + 1)].reshape(ATT_HEADS, -1)
    grads["lru_wa"] = total[11:75].reshape(LRU_BLOCKS, LRU_BLOCK, LRU_BLOCK)
    grads["lru_wx"] = total[75:139].reshape(LRU_BLOCKS, LRU_BLOCK, LRU_BLOCK)
    parts, total = summed(norm_started, pack_norm, total, "norm")
    grads["b_ada"] = total[0:9].reshape(1, -1)
    grads["norm_pre"] = _my_cols(total[9:12], me, 128)
    grads["norm_post"] = _my_cols(total[12:15], me, 128)
    dmod_all = parts[:, 0:9, :].reshape(N_DEV, 9 * D_MODEL)
    grads["w_ada"] = _ada_bwd(c_all, _my_cols(dmod_all, me, 1152), "ada_bwd")

    res = _adamw(grads["w_ada"], w_ada[0], m_w_ada[0], v_w_ada[0], "adamw_w_ada")
    out_g["w_ada"], out_d["w_ada"], out_m["w_ada"], out_v["w_ada"] = [r.reshape(w_ada.shape) for r in res]

    sizes = [int(np.prod(weights[n].shape)) for n in small]
    tot = sum(sizes)
    rows_small = -(-tot // (16 * D_MODEL)) * 16
    flat = lambda arrs: jnp.pad(jnp.concatenate([a.reshape(-1) for a in arrs]),
                                (0, rows_small * D_MODEL - tot)).reshape(rows_small, D_MODEL)
    res = _adamw(flat([grads[n] for n in small]), flat([weights[n] for n in small]),
                 flat([mom1[n] for n in small]), flat([mom2[n] for n in small]), "adamw_small", rows=rows_small)
    offs = np.cumsum([0] + sizes)
    for dst, r in zip((out_g, out_d, out_m, out_v), res):
        rf = r.reshape(-1)
        for i, n in enumerate(small):
            dst[n] = rf[offs[i] : offs[i + 1]].reshape(weights[n].shape)

    done = res[0]
    for names, send, group, tag in last:
        done = exchange_finish(names, send, group, tag, done)

    return (loss, dx0[None], *[out_g[n] for n in order], *[out_d[n] for n in order],
            *[out_m[n] for n in order], *[out_v[n] for n in order])
```

```python
import functools

import jax
import jax.numpy as jnp
import numpy as np
from jax import lax
from jax.experimental import pallas as pl
from jax.experimental.pallas import tpu as pltpu

D_MODEL = 1024
D_FF = 2816
ATT_HEADS = 8
ATT_HEAD_DIM = 64
ATT_WIDTH = 512
CHUNK = 64
LEFT_CHUNKS = 8
MAX_REL = 128
LRU_WIDTH = 1024
LRU_BLOCKS = 16
LRU_BLOCK = 64
LRU_C = 8.0
EPS = 1e-6
PROJ_WIDTH = 5632
N_DEV = 8

ADAM_LR = 0.001
ADAM_B1 = 0.9
ADAM_B2 = 0.999
ADAM_EPS = 1e-08
ADAM_WD = 0.01
ADAM_STEP = 10

V7X_LANES = 128
V7X_SUBLANES = 8
V7X_VMEM_BYTES = 64 * 1024 * 1024
VMEM_LIMIT = V7X_VMEM_BYTES - 8 * 1024 * 1024

ATT_TQ = 256
NEG = -1e30
BF16 = jnp.bfloat16
F32 = jnp.float32
MESH = pl.DeviceIdType.MESH

OFF_Q = 4 * LRU_WIDTH
OFF_K = OFF_Q + ATT_WIDTH
OFF_V = OFF_K + ATT_WIDTH


def _cparams(**kw):
    return pltpu.CompilerParams(vmem_limit_bytes=VMEM_LIMIT, **kw)


def _pick(n, target, unit=V7X_LANES):
    best = None
    for t in range(unit, min(n, target) + 1, unit):
        if n % t == 0:
            best = t
    return n if best is None else best


_DIMS = {
    "nn": (((1,), (0,)), ((), ())),
    "nt": (((1,), (1,)), ((), ())),
    "tn": (((0,), (0,)), ((), ())),
}


ANY_SPEC = pl.BlockSpec(memory_space=pl.ANY)


def _matmul(a, b, mode, out_dtype, name, tm=1024, tn=512, tk=1408, deps=(), b_shift=0):
    n_deps = len(deps)
    if mode == "nn":
        (m, k), (k2, n) = a.shape, b.shape
    elif mode == "nt":
        (m, k), (n, k2) = a.shape, b.shape
    else:
        (k, m), (k2, n) = a.shape, b.shape
    assert k == k2, (a.shape, b.shape, mode)
    tm, tn, tk = _pick(m, tm), _pick(n, tn), _pick(k, tk)
    nk = k // tk
    dims = _DIMS[mode]

    def body(a_ref, b_ref, *rest):
        o_ref, scratch = rest[n_deps], rest[n_deps + 1 :]
        p = lax.dot_general(a_ref[...], b_ref[...], dims, preferred_element_type=F32)
        if nk == 1:
            o_ref[...] = p.astype(o_ref.dtype)
        else:
            acc = scratch[0]
            kk = pl.program_id(2)

            @pl.when(kk == 0)
            def _():
                acc[...] = p

            @pl.when(kk > 0)
            def _():
                acc[...] += p

            @pl.when(kk == nk - 1)
            def _():
                o_ref[...] = acc[...].astype(o_ref.dtype)

    if mode == "nn":
        a_spec = pl.BlockSpec((tm, tk), lambda i, j, kk: (i, kk))
        b_spec = pl.BlockSpec((tk, tn), lambda i, j, kk: (kk, j))
    elif mode == "nt":
        a_spec = pl.BlockSpec((tm, tk), lambda i, j, kk: (i, kk))
        b_spec = pl.BlockSpec((tn, tk), lambda i, j, kk: ((j + b_shift) % (n // tn), kk))
    else:
        a_spec = pl.BlockSpec((tk, tm), lambda i, j, kk: (kk, i))
        b_spec = pl.BlockSpec((tk, tn), lambda i, j, kk: (kk, j))
    return pl.pallas_call(
        body,
        name=name,
        grid=(m // tm, n // tn, nk),
        in_specs=[a_spec, b_spec] + [ANY_SPEC] * n_deps,
        out_specs=pl.BlockSpec((tm, tn), lambda i, j, kk: (i, j)),
        out_shape=jax.ShapeDtypeStruct((m, n), out_dtype),
        scratch_shapes=[pltpu.VMEM((tm, tn), F32)] if nk > 1 else [],
        compiler_params=_cparams(dimension_semantics=("parallel", "parallel", "arbitrary")),
    )(a, b, *deps)


def _rowwise(fn, name, params, tiles, outs, accs=(), ts=256, with_index=False, deps=()):
    norm = []
    for t in tiles:
        if not isinstance(t, tuple):
            t = (t, t.shape[1], 0)
        norm.append(t if len(t) == 4 else (*t, None))
    s = norm[0][0].shape[0]
    ts = min(ts, s)
    assert s % ts == 0 and ts % V7X_SUBLANES == 0
    steps = s // ts
    halo_blocks = ts // V7X_SUBLANES
    n_p, n_t, n_o = len(params), len(norm), len(outs)

    def body(*refs):
        i = pl.program_id(0)
        vals = [r[...] for r in refs[: n_p + n_t]]
        res = fn(i, steps, *vals) if with_index else fn(*vals)
        if not isinstance(res, (tuple, list)):
            res = (res,)
        first_out = n_p + n_t + len(deps)
        o_refs = refs[first_out : first_out + n_o]
        a_refs = refs[first_out + n_o :]
        for r, v in zip(o_refs, res[:n_o]):
            r[...] = v.astype(r.dtype)
        for r, v in zip(a_refs, res[n_o:]):
            _accumulate(r, v, i)

    in_specs = [pl.BlockSpec(p.shape, lambda i: (0, 0)) for p in params]
    for arr, w, cb, halo in norm:
        if halo is None:
            in_specs.append(pl.BlockSpec((ts, w), lambda i, cb=cb: (i, cb)))
        elif halo == "prev":
            in_specs.append(
                pl.BlockSpec((V7X_SUBLANES, w), lambda i, cb=cb: (jnp.maximum(i * halo_blocks - 1, 0), cb))
            )
        else:
            last = s // V7X_SUBLANES - 1
            in_specs.append(
                pl.BlockSpec((V7X_SUBLANES, w), lambda i, cb=cb: (jnp.minimum((i + 1) * halo_blocks, last), cb))
            )
    in_specs += [ANY_SPEC] * len(deps)
    out_specs = [pl.BlockSpec((ts, w), lambda i: (i, 0)) for w, _ in outs]
    out_specs += [pl.BlockSpec(shape, lambda i: (0, 0)) for shape in accs]
    out_shape = [jax.ShapeDtypeStruct((s, w), dt) for w, dt in outs]
    out_shape += [jax.ShapeDtypeStruct(shape, F32) for shape in accs]
    res = pl.pallas_call(
        body,
        name=name,
        grid=(steps,),
        in_specs=in_specs,
        out_specs=out_specs,
        out_shape=out_shape,
        compiler_params=_cparams(dimension_semantics=("arbitrary",)),
    )(*params, *[t[0] for t in norm], *deps)
    return res


def _accumulate(ref, val, step):
    @pl.when(step == 0)
    def _():
        ref[...] = val

    @pl.when(step > 0)
    def _():
        ref[...] += val


def _sigmoid(z):
    return jax.nn.sigmoid(z)


def _silu(z):
    return z * _sigmoid(z)


def _gelu(z):
    return 0.5 * z * (1.0 + jnp.tanh(0.7978845608028654 * (z + 0.044715 * (z * z * z))))


def _pre_fn(g, shift, scale, x):
    r = lax.rsqrt(jnp.mean(x * x, axis=-1, keepdims=True) + EPS)
    return ((x * r) * g) * (1.0 + scale) + shift


def _post_fn(res_w, g, gate, f, x):
    r = lax.rsqrt(jnp.mean(f * f, axis=-1, keepdims=True) + EPS)
    return x + (res_w * gate) * ((f * r) * g)


def _swiglu_fn(gu):
    return _silu(gu[:, :D_FF]) * gu[:, D_FF:]


def _gates_fn(ba, bx, lam, pre, xc):
    ra = _sigmoid(pre[:, :LRU_WIDTH] + ba)
    ia = _sigmoid(pre[:, LRU_WIDTH:] + bx)
    softplus = jnp.maximum(-lam, 0.0) + jnp.log1p(jnp.exp(-jnp.abs(lam)))
    log_a = (-LRU_C) * ra * softplus
    a = jnp.exp(log_a)
    mult = jnp.sqrt(-jnp.tanh(log_a) * (a * a + 1.0))
    return a, mult * (ia * xc)


def _recin_fn(hs, yr):
    return hs * _gelu(yr)


def _merge_fn(att, rec, g_att, g_rec):
    return _sigmoid(g_att) * att + _sigmoid(g_rec) * rec


def _rowsum(v):
    return jnp.sum(v, axis=0, keepdims=True)


def _pre_fwd(x, g, shift, scale, name, deps=()):
    (h,) = _rowwise(_pre_fn, name, [g, shift, scale], [x], [(D_MODEL, BF16)], deps=deps)
    return h


def _pre_bwd(x, g, shift, scale, dh, dres, name):
    def fn(g, shift, scale, x, dh, dres):
        _, vjp = jax.vjp(_pre_fn, g, shift, scale, x)
        dg, dshift, dscale, dx = vjp(dh)
        return dx + dres, dg, dshift, dscale

    row = (1, D_MODEL)
    return _rowwise(fn, name, [g, shift, scale], [x, dh, dres], [(D_MODEL, F32)], [row, row, row])


def _post_fwd(f, x, g, gate, res_w, name):
    (y,) = _rowwise(functools.partial(_post_fn, res_w), name, [g, gate], [f, x], [(D_MODEL, F32)])
    return y


def _post_bwd(f, g, gate, res_w, dy, name, deps=()):
    def fn(g, gate, f, dy):
        _, vjp = jax.vjp(lambda g, gate, f: _post_fn(res_w, g, gate, f, 0.0), g, gate, f)
        dg, dgate, df = vjp(dy)
        return df, dg, dgate

    row = (1, D_MODEL)
    return _rowwise(fn, name, [g, gate], [f, dy], [(D_MODEL, BF16)], [row, row], deps=deps)


def _loss_stage(y, target, name):
    def fn(y, t):
        diff = y - t
        return diff * (1.0 / D_MODEL), _rowsum(diff * diff)

    return _rowwise(fn, name, [], [y, target], [(D_MODEL, F32)], [(1, D_MODEL)])


FFN_TM = 512
FFN_TF = 1408


def _glu_fn(g, u):
    return _silu(g) * u


def _ffn_up(h, w_gu_t, name):
    s = h.shape[0]
    tm = min(FFN_TM, s)
    nf = D_FF // FFN_TF

    def body(h_ref, wg_ref, wu_ref, a_ref, g_ref, u_ref):
        hv = h_ref[...]
        g = lax.dot_general(hv, wg_ref[...], _DIMS["nt"], preferred_element_type=F32)
        u = lax.dot_general(hv, wu_ref[...], _DIMS["nt"], preferred_element_type=F32)
        a_ref[...] = _glu_fn(g, u).astype(a_ref.dtype)
        g_ref[...] = g.astype(g_ref.dtype)
        u_ref[...] = u.astype(u_ref.dtype)

    out = pl.BlockSpec((tm, FFN_TF), lambda i, j: (i, j))
    return pl.pallas_call(
        body,
        name=name,
        grid=(s // tm, nf),
        in_specs=[pl.BlockSpec((tm, D_MODEL), lambda i, j: (i, 0)),
                  pl.BlockSpec((FFN_TF, D_MODEL), lambda i, j: (j, 0)),
                  pl.BlockSpec((FFN_TF, D_MODEL), lambda i, j: (nf + j, 0))],
        out_specs=[out, out, out],
        out_shape=[jax.ShapeDtypeStruct((s, D_FF), BF16)] * 3,
        compiler_params=_cparams(dimension_semantics=("parallel", "arbitrary")),
    )(h, w_gu_t, w_gu_t)


def _ffn_up_bwd(df, w_down, g, u, name, deps=()):
    s = df.shape[0]
    tm = min(FFN_TM, s)

    def body(df_ref, wd_ref, g_ref, u_ref, *rest):
        dg_ref, du_ref = rest[len(deps) :]
        da = lax.dot_general(df_ref[...], wd_ref[...], _DIMS["nt"], preferred_element_type=F32)
        _, vjp = jax.vjp(_glu_fn, g_ref[...].astype(F32), u_ref[...].astype(F32))
        dg, du = vjp(da)
        dg_ref[...] = dg.astype(dg_ref.dtype)
        du_ref[...] = du.astype(du_ref.dtype)

    blk = pl.BlockSpec((tm, FFN_TF), lambda i, j: (i, j))
    return pl.pallas_call(
        body,
        name=name,
        grid=(s // tm, D_FF // FFN_TF),
        in_specs=[pl.BlockSpec((tm, D_MODEL), lambda i, j: (i, 0)),
                  pl.BlockSpec((FFN_TF, D_MODEL), lambda i, j: (j, 0)), blk, blk] + [ANY_SPEC] * len(deps),
        out_specs=[blk, blk],
        out_shape=[jax.ShapeDtypeStruct((s, D_FF), BF16)] * 2,
        compiler_params=_cparams(dimension_semantics=("parallel", "arbitrary")),
    )(df, w_down, g, u, *deps)


def _ffn_dh(dg, du, w_gu_t, name, deps=()):
    s = dg.shape[0]
    tm, tn = min(FFN_TM, s), 512

    def body(dg_ref, du_ref, wg_ref, wu_ref, *rest):
        o_ref = rest[len(deps)]
        p = jnp.dot(dg_ref[...], wg_ref[...], preferred_element_type=F32)
        o_ref[...] = p + jnp.dot(du_ref[...], wu_ref[...], preferred_element_type=F32)

    a_spec = pl.BlockSpec((tm, D_FF), lambda i, j: (i, 0))
    return pl.pallas_call(
        body,
        name=name,
        grid=(s // tm, D_MODEL // tn),
        in_specs=[a_spec, a_spec,
                  pl.BlockSpec((D_FF, tn), lambda i, j: (0, j)),
                  pl.BlockSpec((D_FF, tn), lambda i, j: (1, j))] + [ANY_SPEC] * len(deps),
        out_specs=pl.BlockSpec((tm, tn), lambda i, j: (i, j)),
        out_shape=jax.ShapeDtypeStruct((s, D_MODEL), F32),
        compiler_params=_cparams(dimension_semantics=("parallel", "arbitrary")),
    )(dg, du, w_gu_t, w_gu_t, *deps)


FUSE_TM = 256
ROW_SPEC2 = pl.BlockSpec((1, D_MODEL), lambda i, j: (0, 0))
ROW_SPEC1 = pl.BlockSpec((1, D_MODEL), lambda i: (0, 0))


def _pre_up(x, g, shift, scale, w_gu_t, name, deps=()):
    s = x.shape[0]
    tm = min(FFN_TM, s)
    nf = D_FF // FFN_TF
    nd = len(deps)

    def body(x_ref, g_ref, sh_ref, sc_ref, wg_ref, wu_ref, *rest):
        h_ref, a_ref, gg_ref, u_ref, h_s = rest[nd:]

        @pl.when(pl.program_id(1) == 0)
        def _():
            h = _pre_fn(g_ref[...], sh_ref[...], sc_ref[...], x_ref[...]).astype(BF16)
            h_s[...] = h
            h_ref[...] = h

        hv = h_s[...]
        gv = lax.dot_general(hv, wg_ref[...], _DIMS["nt"], preferred_element_type=F32)
        uv = lax.dot_general(hv, wu_ref[...], _DIMS["nt"], preferred_element_type=F32)
        a_ref[...] = _glu_fn(gv, uv).astype(a_ref.dtype)
        gg_ref[...] = gv.astype(gg_ref.dtype)
        u_ref[...] = uv.astype(u_ref.dtype)

    rows = pl.BlockSpec((tm, D_MODEL), lambda i, j: (i, 0))
    out = pl.BlockSpec((tm, FFN_TF), lambda i, j: (i, j))
    return pl.pallas_call(
        body,
        name=name,
        grid=(s // tm, nf),
        in_specs=[rows, ROW_SPEC2, ROW_SPEC2, ROW_SPEC2,
                  pl.BlockSpec((FFN_TF, D_MODEL), lambda i, j: (j, 0)),
                  pl.BlockSpec((FFN_TF, D_MODEL), lambda i, j: (nf + j, 0))] + [ANY_SPEC] * nd,
        out_specs=[rows, out, out, out],
        out_shape=[jax.ShapeDtypeStruct((s, D_MODEL), BF16)] + [jax.ShapeDtypeStruct((s, D_FF), BF16)] * 3,
        scratch_shapes=[pltpu.VMEM((tm, D_MODEL), BF16)],
        compiler_params=_cparams(dimension_semantics=("parallel", "arbitrary")),
    )(x, g, shift, scale, w_gu_t, w_gu_t, *deps)


def _pre_matmul(x, g, shift, scale, w_t, name, b_shift=0, tn=512):
    s = x.shape[0]
    n = w_t.shape[0]
    tm = min(2 * FFN_TM, s)

    def body(x_ref, g_ref, sh_ref, sc_ref, w_ref, h_ref, o_ref, h_s):
        @pl.when(pl.program_id(1) == 0)
        def _():
            h = _pre_fn(g_ref[...], sh_ref[...], sc_ref[...], x_ref[...]).astype(BF16)
            h_s[...] = h
            h_ref[...] = h

        o_ref[...] = lax.dot_general(h_s[...], w_ref[...], _DIMS["nt"], preferred_element_type=F32)

    rows = pl.BlockSpec((tm, D_MODEL), lambda i, j: (i, 0))
    return pl.pallas_call(
        body,
        name=name,
        grid=(s // tm, n // tn),
        in_specs=[rows, ROW_SPEC2, ROW_SPEC2, ROW_SPEC2,
                  pl.BlockSpec((tn, D_MODEL), lambda i, j: ((j + b_shift) % (n // tn), 0))],
        out_specs=[rows, pl.BlockSpec((tm, tn), lambda i, j: (i, j))],
        out_shape=[jax.ShapeDtypeStruct((s, D_MODEL), BF16), jax.ShapeDtypeStruct((s, n), F32)],
        scratch_shapes=[pltpu.VMEM((tm, D_MODEL), BF16)],
        compiler_params=_cparams(dimension_semantics=("parallel", "arbitrary")),
    )(x, g, shift, scale, w_t)


def _matmul_post(a, w, x, g_post, gate, res_w, name):
    s, k = a.shape
    tm = min(FFN_TM, s)

    def body(a_ref, w_ref, x_ref, g_ref, gate_ref, f_ref, y_ref):
        f = jnp.dot(a_ref[...], w_ref[...], preferred_element_type=F32)
        f_ref[...] = f
        y_ref[...] = _post_fn(res_w, g_ref[...], gate_ref[...], f, x_ref[...])

    rows = pl.BlockSpec((tm, D_MODEL), lambda i: (i, 0))
    return pl.pallas_call(
        body,
        name=name,
        grid=(s // tm,),
        in_specs=[pl.BlockSpec((tm, k), lambda i: (i, 0)), pl.BlockSpec((k, D_MODEL), lambda i: (0, 0)), rows,
                  ROW_SPEC1, ROW_SPEC1],
        out_specs=[rows, rows],
        out_shape=[jax.ShapeDtypeStruct((s, D_MODEL), F32)] * 2,
        compiler_params=_cparams(dimension_semantics=("parallel",)),
    )(a, w, x, g_post, gate)


def _post_vjp(res_w, g, gate, f, dy):
    _, vjp = jax.vjp(lambda g, gate, f: _post_fn(res_w, g, gate, f, 0.0), g, gate, f)
    return vjp(dy)


def _post_bwd_up_bwd(f, dy, g_post, gate, res_w, w_down, g, u, name, deps=()):
    s = f.shape[0]
    tm = min(FFN_TM, s)
    nd = len(deps)

    def body(f_ref, dy_ref, gp_ref, gate_ref, wd_ref, g_ref, u_ref, *rest):
        df_ref, dgu_ref, dgp_ref, dgate_ref, df_s = rest[nd:]
        i = pl.program_id(0)

        @pl.when(pl.program_id(1) == 0)
        def _():
            dgp, dgate, df = _post_vjp(res_w, gp_ref[...], gate_ref[...], f_ref[...], dy_ref[...])
            df_s[...] = df.astype(BF16)
            df_ref[...] = df_s[...]
            _accumulate(dgp_ref, dgp, i)
            _accumulate(dgate_ref, dgate, i)

        da = lax.dot_general(df_s[...], wd_ref[...], _DIMS["nt"], preferred_element_type=F32)
        _, vjp = jax.vjp(_glu_fn, g_ref[...].astype(F32), u_ref[...].astype(F32))
        dg, du = vjp(da)
        dgu_ref[0] = dg.astype(dgu_ref.dtype)
        dgu_ref[1] = du.astype(dgu_ref.dtype)

    rows = pl.BlockSpec((tm, D_MODEL), lambda i, j: (i, 0))
    blk = pl.BlockSpec((tm, FFN_TF), lambda i, j: (i, j))
    return pl.pallas_call(
        body,
        name=name,
        grid=(s // tm, D_FF // FFN_TF),
        in_specs=[rows, rows, ROW_SPEC2, ROW_SPEC2, pl.BlockSpec((FFN_TF, D_MODEL), lambda i, j: (j, 0)), blk,
                  blk] + [ANY_SPEC] * nd,
        out_specs=[rows, pl.BlockSpec((2, tm, FFN_TF), lambda i, j: (0, i, j)), ROW_SPEC2, ROW_SPEC2],
        out_shape=[jax.ShapeDtypeStruct((s, D_MODEL), BF16), jax.ShapeDtypeStruct((2, s, D_FF), BF16),
                   jax.ShapeDtypeStruct((1, D_MODEL), F32), jax.ShapeDtypeStruct((1, D_MODEL), F32)],
        scratch_shapes=[pltpu.VMEM((tm, D_MODEL), BF16)],
        compiler_params=_cparams(dimension_semantics=("arbitrary", "arbitrary")),
    )(f, dy, g_post, gate, w_down, g, u, *deps)


def _post_bwd_matmul(f, dy, g_post, gate, res_w, w, name):
    s = f.shape[0]
    n = w.shape[0]
    tm = min(FUSE_TM, s)

    def body(f_ref, dy_ref, gp_ref, gate_ref, w_ref, df_ref, o_ref, dgp_ref, dgate_ref):
        i = pl.program_id(0)
        dgp, dgate, df = _post_vjp(res_w, gp_ref[...], gate_ref[...], f_ref[...], dy_ref[...])
        dfb = df.astype(BF16)
        df_ref[...] = dfb
        _accumulate(dgp_ref, dgp, i)
        _accumulate(dgate_ref, dgate, i)
        o_ref[...] = lax.dot_general(dfb, w_ref[...], _DIMS["nt"], preferred_element_type=F32)

    rows = pl.BlockSpec((tm, D_MODEL), lambda i: (i, 0))
    return pl.pallas_call(
        body,
        name=name,
        grid=(s // tm,),
        in_specs=[rows, rows, ROW_SPEC1, ROW_SPEC1, pl.BlockSpec((n, D_MODEL), lambda i: (0, 0))],
        out_specs=[rows, pl.BlockSpec((tm, n), lambda i: (i, 0)), ROW_SPEC1, ROW_SPEC1],
        out_shape=[jax.ShapeDtypeStruct((s, D_MODEL), BF16), jax.ShapeDtypeStruct((s, n), F32),
                   jax.ShapeDtypeStruct((1, D_MODEL), F32), jax.ShapeDtypeStruct((1, D_MODEL), F32)],
        compiler_params=_cparams(dimension_semantics=("arbitrary",)),
    )(f, dy, g_post, gate, w)


def _matmul_pre_bwd(parts, w_t, x, dres, g, shift, scale, name, deps=()):
    s = x.shape[0]
    na, nd = len(parts), len(deps)
    ranges = [p[3] for p in parts]

    def body(*refs):
        a_refs = refs[:na]
        w_ref, x_ref, dres_ref, g_ref, sh_ref, sc_ref = refs[na : na + 6]
        dx_ref, dg_ref, dsh_ref, dsc_ref = refs[na + 6 + nd :]
        i = pl.program_id(0)
        dh = None
        for a_ref, (r0, r1) in zip(a_refs, ranges):
            p = jnp.dot(a_ref[...], w_ref[r0:r1, :], preferred_element_type=F32)
            dh = p if dh is None else dh + p
        _, vjp = jax.vjp(_pre_fn, g_ref[...], sh_ref[...], sc_ref[...], x_ref[...])
        dg, dsh, dsc, dx = vjp(dh)
        dx_ref[...] = dx + dres_ref[...]
        _accumulate(dg_ref, dg, i)
        _accumulate(dsh_ref, dsh, i)
        _accumulate(dsc_ref, dsc, i)

    tm = parts[0][1][-2]
    rows = pl.BlockSpec((tm, D_MODEL), lambda i: (i, 0))
    return pl.pallas_call(
        body,
        name=name,
        grid=(s // tm,),
        in_specs=[pl.BlockSpec(p[1], p[2]) for p in parts]
        + [pl.BlockSpec(w_t.shape, lambda i: (0, 0)), rows, rows, ROW_SPEC1, ROW_SPEC1, ROW_SPEC1]
        + [ANY_SPEC] * nd,
        out_specs=[rows, ROW_SPEC1, ROW_SPEC1, ROW_SPEC1],
        out_shape=[jax.ShapeDtypeStruct((s, D_MODEL), F32)] + [jax.ShapeDtypeStruct((1, D_MODEL), F32)] * 3,
        compiler_params=_cparams(dimension_semantics=("arbitrary",)),
    )(*[p[0] for p in parts], w_t, x, dres, g, shift, scale, *deps)


def _dw_gu(dgu, h, name, deps=()):
    s = h.shape[0]
    tk = min(1024, s)
    nk = s // tk
    half = D_FF // FFN_TF

    def body(a_ref, b_ref, *rest):
        o_ref, acc = rest[len(deps) :]
        kk = pl.program_id(1)
        p = lax.dot_general(a_ref[...], b_ref[...], _DIMS["tn"], preferred_element_type=F32)

        @pl.when(kk == 0)
        def _():
            acc[...] = p

        @pl.when(kk > 0)
        def _():
            acc[...] += p

        @pl.when(kk == nk - 1)
        def _():
            o_ref[...] = acc[...].astype(o_ref.dtype)

    return pl.pallas_call(
        body,
        name=name,
        grid=(2 * half, nk),
        in_specs=[pl.BlockSpec((None, tk, FFN_TF), lambda i, kk: (i // half, kk, i % half)),
                  pl.BlockSpec((tk, D_MODEL), lambda i, kk: (kk, 0))] + [ANY_SPEC] * len(deps),
        out_specs=pl.BlockSpec((FFN_TF, D_MODEL), lambda i, kk: (i, 0)),
        out_shape=jax.ShapeDtypeStruct((2 * D_FF, D_MODEL), BF16),
        scratch_shapes=[pltpu.VMEM((FFN_TF, D_MODEL), F32)],
        compiler_params=_cparams(dimension_semantics=("parallel", "arbitrary")),
    )(dgu, h, *deps)


def _lru_diag_blocks(dw_bd, name):
    def body(w_ref, o_ref):
        for half in range(2):
            for n in range(LRU_BLOCKS):
                rows = slice(n * LRU_BLOCK, (n + 1) * LRU_BLOCK)
                cols = slice(half * LRU_WIDTH + n * LRU_BLOCK, half * LRU_WIDTH + (n + 1) * LRU_BLOCK)
                o_ref[half, rows, :] = w_ref[rows, cols]

    return pl.pallas_call(
        body, name=name, out_shape=jax.ShapeDtypeStruct((2, LRU_WIDTH, LRU_BLOCK), F32), compiler_params=_cparams()
    )(dw_bd)


def _swiglu_fwd(gu, name):
    (a,) = _rowwise(_swiglu_fn, name, [], [gu], [(D_FF, BF16)], ts=128)
    return a


def _swiglu_bwd(gu, da, name, deps=()):
    def fn(gu, da):
        _, vjp = jax.vjp(_swiglu_fn, gu)
        return vjp(da)[0]

    (dgu,) = _rowwise(fn, name, [], [gu, da], [(2 * D_FF, BF16)], ts=128, deps=deps)
    return dgu


def _shift_down(ext, j, rows):
    return pltpu.roll(ext, j, 0)[V7X_SUBLANES : V7X_SUBLANES + rows]


def _shift_up(ext, j, rows):
    return pltpu.roll(ext, ext.shape[0] - j, 0)[:rows] if j else ext[:rows]


LRU_SLAB = 256
N_SLABS = LRU_WIDTH // LRU_SLAB


def _slab_weights(wa, wx):
    per = LRU_SLAB // LRU_BLOCK
    eye = jnp.eye(per, dtype=wa.dtype)

    def diag(w):
        w4 = w.reshape(N_SLABS, per, LRU_BLOCK, LRU_BLOCK)
        return jnp.einsum("sbkj,bc->sbkcj", w4, eye).reshape(N_SLABS, LRU_SLAB, LRU_SLAB)

    return jnp.concatenate([diag(wa), diag(wx)], axis=2).reshape(LRU_WIDTH, 2 * LRU_SLAB).astype(BF16)


def _slab_cols(v, s):
    lo = s * LRU_SLAB
    return jnp.concatenate([v[:, lo : lo + LRU_SLAB], v[:, LRU_WIDTH + lo : LRU_WIDTH + lo + LRU_SLAB]], axis=1)


def _lru_front(proj, w8, b, w_slab, ba, bx, lam, name):
    def fn(i, steps, w8, b, w_slab, ba, bx, lam, x, halo):
        halo = jnp.where(i > 0, halo, 0.0)
        ext = jnp.concatenate([halo, x], axis=0)
        xc = b + w8[3:4] * x
        for j in (1, 2, 3):
            xc = xc + w8[3 - j : 4 - j] * _shift_down(ext, j, x.shape[0])
        xcb = xc.astype(BF16)
        prods = []
        for s in range(N_SLABS):
            rows = slice(s * LRU_SLAB, (s + 1) * LRU_SLAB)
            prods.append(jnp.dot(xcb[:, rows], w_slab[rows], preferred_element_type=F32))
        pre = jnp.concatenate([p[:, :LRU_SLAB] for p in prods] + [p[:, LRU_SLAB:] for p in prods], axis=1)
        a, u = _gates_fn(ba, bx, lam, pre, xc)
        return xc, pre, a, u

    tiles = [(proj, LRU_WIDTH, 0), (proj, LRU_WIDTH, 0, "prev")]
    outs = [(LRU_WIDTH, F32), (2 * LRU_WIDTH, F32), (LRU_WIDTH, F32), (LRU_WIDTH, F32)]
    return _rowwise(fn, name, [w8, b, w_slab, ba, bx, lam], tiles, outs, with_index=True)


def _lru_back(pre, xc, w_slab, ba, bx, lam, g, h_prev, name, deps=()):
    def fn(w_slab, ba, bx, lam, pre, xc, g, h_prev):
        _, vjp = jax.vjp(_gates_fn, ba, bx, lam, pre, xc)
        dba, dbx, dlam, dpre, dxc = vjp((g * h_prev, g))
        dpre = dpre.astype(BF16)
        back = []
        for s in range(N_SLABS):
            rows = slice(s * LRU_SLAB, (s + 1) * LRU_SLAB)
            back.append(lax.dot_general(_slab_cols(dpre, s), w_slab[rows], _DIMS["nt"], preferred_element_type=F32))
        return dpre, dxc + jnp.concatenate(back, axis=1), dba, dbx, dlam

    row = (1, LRU_WIDTH)
    return _rowwise(fn, name, [w_slab, ba, bx, lam], [pre, xc, g, h_prev],
                    [(2 * LRU_WIDTH, BF16), (LRU_WIDTH, F32)], [row, row, row], deps=deps)


def _lru_dw(xc, dpre, name):
    s = xc.shape[0]
    ts = min(512, s)
    steps = s // ts
    per = LRU_SLAB // LRU_BLOCK

    def body(x_ref, d_ref, o_ref, acc):
        i = pl.program_id(0)
        xcb = x_ref[...].astype(BF16)
        d = d_ref[...]
        for sl in range(N_SLABS):
            rows = slice(sl * LRU_SLAB, (sl + 1) * LRU_SLAB)
            p = lax.dot_general(xcb[:, rows], _slab_cols(d, sl), _DIMS["tn"], preferred_element_type=F32)

            @pl.when(i == 0)
            def _(p=p, rows=rows):
                acc[rows, :] = p

            @pl.when(i > 0)
            def _(p=p, rows=rows):
                acc[rows, :] += p

        @pl.when(i == steps - 1)
        def _():
            for half in range(2):
                for n in range(LRU_BLOCKS):
                    r0 = n * LRU_BLOCK
                    c0 = half * LRU_SLAB + (n % per) * LRU_BLOCK
                    o_ref[half, r0 : r0 + LRU_BLOCK, :] = acc[r0 : r0 + LRU_BLOCK, c0 : c0 + LRU_BLOCK]

    return pl.pallas_call(
        body,
        name=name,
        grid=(steps,),
        in_specs=[pl.BlockSpec((ts, LRU_WIDTH), lambda i: (i, 0)), pl.BlockSpec((ts, 2 * LRU_WIDTH), lambda i: (i, 0))],
        out_specs=pl.BlockSpec((2, LRU_WIDTH, LRU_BLOCK), lambda i: (0, 0, 0)),
        out_shape=jax.ShapeDtypeStruct((2, LRU_WIDTH, LRU_BLOCK), F32),
        scratch_shapes=[pltpu.VMEM((LRU_WIDTH, 2 * LRU_SLAB), F32)],
        compiler_params=_cparams(dimension_semantics=("arbitrary",)),
    )(xc, dpre)


def _conv_bwd(proj, w8, d1, name):
    def fn(i, steps, w8, x, halo, d, d1n):
        rows = x.shape[0]
        dn = jnp.where(i < steps - 1, d1n, 0.0)
        halo = jnp.where(i > 0, halo, 0.0)
        dext = jnp.concatenate([d, dn], axis=0)
        xext = jnp.concatenate([halo, x], axis=0)
        dx = w8[3:4] * d
        dw = [None] * 4
        dw[3] = _rowsum(d * x)
        for k in (1, 2, 3):
            dx = dx + w8[3 - k : 4 - k] * _shift_up(dext, k, rows)
            dw[3 - k] = _rowsum(d * _shift_down(xext, k, rows))
        dw8 = jnp.concatenate(dw + [jnp.zeros((4, LRU_WIDTH), F32)], axis=0)
        return dx, dw8, _rowsum(d)

    tiles = [(proj, LRU_WIDTH, 0), (proj, LRU_WIDTH, 0, "prev"), d1, (d1, LRU_WIDTH, 0, "next")]
    return _rowwise(fn, name, [w8], tiles, [(LRU_WIDTH, BF16)], [(8, LRU_WIDTH), (1, LRU_WIDTH)], with_index=True)


SCAN_ROWS = 512


def _block_scan(a, b, row, reverse):
    for d in (1, 2, 4):
        if reverse:
            shift, keep = V7X_SUBLANES - d, row < V7X_SUBLANES - d
        else:
            shift, keep = d, row >= d
        a_s = pltpu.roll(a, shift, 0)
        b_s = pltpu.roll(b, shift, 0)
        b = jnp.where(keep, a * b_s + b, b)
        a = jnp.where(keep, a * a_s, a)
    return a, b


def _scan_fwd(a, u, name):
    s, w = a.shape
    ts = min(SCAN_ROWS, s)
    sub = ts // V7X_SUBLANES

    def body(a_ref, u_ref, h_ref, hp_ref, carry):
        @pl.when(pl.program_id(0) == 0)
        def _():
            carry[...] = jnp.zeros_like(carry)

        row = lax.broadcasted_iota(jnp.int32, (V7X_SUBLANES, w), 0)

        def step(j, c):
            rows = pl.ds(pl.multiple_of(j * V7X_SUBLANES, V7X_SUBLANES), V7X_SUBLANES)
            pa, pb = _block_scan(a_ref[rows, :], u_ref[rows, :], row, False)
            h = pb + pa * c
            h_ref[rows, :] = h
            hp_ref[rows, :] = jnp.where(row >= 1, pltpu.roll(h, 1, 0), c)
            return jnp.broadcast_to(h[V7X_SUBLANES - 1 :], (V7X_SUBLANES, w))

        carry[...] = lax.fori_loop(0, sub, step, carry[...])

    spec = pl.BlockSpec((ts, w), lambda i: (i, 0))
    return pl.pallas_call(
        body,
        name=name,
        grid=(s // ts,),
        in_specs=[spec, spec],
        out_specs=[spec, spec],
        out_shape=[jax.ShapeDtypeStruct((s, w), F32)] * 2,
        scratch_shapes=[pltpu.VMEM((V7X_SUBLANES, w), F32)],
        compiler_params=_cparams(dimension_semantics=("arbitrary",)),
    )(a, u)


def _scan_bwd(a, dh, name):
    s, w = a.shape
    ts = min(SCAN_ROWS, s)
    sub = ts // V7X_SUBLANES
    steps = s // ts

    def body(a_ref, d_ref, g_ref, carry):
        @pl.when(pl.program_id(0) == 0)
        def _():
            carry[...] = jnp.zeros_like(carry)

        row = lax.broadcasted_iota(jnp.int32, (V7X_SUBLANES, w), 0)

        def step(jj, c):
            j = sub - 1 - jj
            rows = pl.ds(pl.multiple_of(j * V7X_SUBLANES, V7X_SUBLANES), V7X_SUBLANES)
            av, dv = a_ref[rows, :], d_ref[rows, :]
            pa, pb = _block_scan(av, av * dv, row, True)
            big = pb + pa * c
            g_ref[rows, :] = dv + jnp.where(row < V7X_SUBLANES - 1, pltpu.roll(big, V7X_SUBLANES - 1, 0), c)
            return jnp.broadcast_to(big[:1], (V7X_SUBLANES, w))

        carry[...] = lax.fori_loop(0, sub, step, carry[...])

    spec = pl.BlockSpec((ts, w), lambda i: (steps - 1 - i, 0))
    return pl.pallas_call(
        body,
        name=name,
        grid=(steps,),
        in_specs=[spec, spec],
        out_specs=spec,
        out_shape=jax.ShapeDtypeStruct((s, w), F32),
        scratch_shapes=[pltpu.VMEM((V7X_SUBLANES, w), F32)],
        compiler_params=_cparams(dimension_semantics=("arbitrary",)),
    )(a, dh)


def _rel_index():
    i = np.arange(ATT_TQ)[:, None]
    j = np.arange(3 * ATT_TQ)[None, :]
    band = (j // CHUNK >= i // CHUNK) & (j // CHUNK <= i // CHUNK + LEFT_CHUNKS)
    return band


SKEW = 4 * ATT_TQ


def _skew_onehot():
    t = np.arange(SKEW)
    diag = np.where(t < 3 * ATT_TQ, -t, SKEW - t)
    idx = np.clip(diag + LEFT_CHUNKS * CHUNK, -MAX_REL, MAX_REL) + MAX_REL
    hit = (idx[:, None] == np.arange(2 * MAX_REL + 1)[None, :]) & (t[:, None] != 3 * ATT_TQ)
    return hit.astype(np.float32)


def _bias_tile(rel_bias):
    per_t = jnp.dot(rel_bias, jnp.asarray(_skew_onehot()).T, precision=lax.Precision.HIGHEST)
    flat = jnp.broadcast_to(per_t[:, None, :], (ATT_HEADS, ATT_TQ, SKEW)).reshape(ATT_HEADS, ATT_TQ * SKEW)
    tile = flat[:, : ATT_TQ * (SKEW - 1)].reshape(ATT_HEADS, ATT_TQ, SKEW - 1)[:, :, : 3 * ATT_TQ]
    first = (2 - np.arange(3))[:, None, None, None] * ATT_TQ
    seen = _rel_index()[None, None] & (np.arange(3 * ATT_TQ)[None, None, None, :] >= first)
    return jnp.where(jnp.asarray(seen), tile[None], NEG)


def _bias_grad(dbias):
    flat = jnp.pad(dbias, ((0, 0), (0, 0), (0, SKEW - 1 - 3 * ATT_TQ))).reshape(ATT_HEADS, ATT_TQ * (SKEW - 1))
    per_t = jnp.sum(jnp.pad(flat, ((0, 0), (0, ATT_TQ))).reshape(ATT_HEADS, ATT_TQ, SKEW), axis=1)
    return jnp.dot(per_t, jnp.asarray(_skew_onehot()), precision=lax.Precision.HIGHEST)


def _attn_specs(nt):
    qb, kb, vb = OFF_Q // V7X_LANES, OFF_K // V7X_LANES, OFF_V // V7X_LANES
    blk = (ATT_TQ, V7X_LANES)

    def qmap(base):
        return lambda hp, m: (jnp.minimum(m, nt - 1), base + hp)

    def wmap(base, back):
        return lambda hp, m: (jnp.clip(m - back, 0, nt - 1), base + hp)

    specs = [pl.BlockSpec(blk, qmap(qb))]
    specs += [pl.BlockSpec(blk, wmap(kb, back)) for back in (2, 1, 0)]
    specs += [pl.BlockSpec(blk, wmap(vb, back)) for back in (2, 1, 0)]
    return specs


ATT_SCALE = ATT_HEAD_DIM**-0.5


def _attn_exp(qh, kh, bias):
    s = lax.dot_general(qh, kh, _DIMS["nt"], preferred_element_type=F32) + bias
    e = jnp.exp(s - jnp.max(s, axis=-1, keepdims=True))
    return e, jnp.sum(e, axis=-1, keepdims=True)


def _attn_window(k0, k1, k2, v0, v1, v2):
    k = jnp.concatenate([k0[...], k1[...], k2[...]], axis=0).astype(BF16)
    v = jnp.concatenate([v0[...], v1[...], v2[...]], axis=0).astype(BF16)
    return k, v


def _bias_spec():
    return pl.BlockSpec((1, 2, ATT_TQ, 3 * ATT_TQ), lambda hp, m: (jnp.minimum(m, 2), hp, 0, 0))


def _attn_fwd(proj, bias, name):
    s = proj.shape[0]
    nt = s // ATT_TQ

    def body(q_ref, k0, k1, k2, v0, v1, v2, b_ref, o_ref):
        k, v = _attn_window(k0, k1, k2, v0, v1, v2)
        q = (q_ref[...] * ATT_SCALE).astype(BF16)
        for hh in range(2):
            cols = slice(hh * ATT_HEAD_DIM, (hh + 1) * ATT_HEAD_DIM)
            e, total = _attn_exp(q[:, cols], k[:, cols], b_ref[0, hh])
            o = jnp.dot(e.astype(BF16), v[:, cols], preferred_element_type=F32) / total
            o_ref[:, cols] = o.astype(o_ref.dtype)

    specs = _attn_specs(nt) + [_bias_spec()]
    return pl.pallas_call(
        body,
        name=name,
        grid=(ATT_HEADS // 2, nt),
        in_specs=specs,
        out_specs=pl.BlockSpec((ATT_TQ, V7X_LANES), lambda hp, m: (m, hp)),
        out_shape=jax.ShapeDtypeStruct((s, ATT_WIDTH), BF16),
        compiler_params=_cparams(dimension_semantics=("parallel", "arbitrary")),
    )(proj, proj, proj, proj, proj, proj, proj, bias)


def _attn_bwd(proj, bias, do, name):
    s = proj.shape[0]
    nt = s // ATT_TQ
    win = 3 * ATT_TQ

    def body(q_ref, k0, k1, k2, v0, v1, v2, do_ref, b_ref, dq_ref, dk_ref, dv_ref, db_ref, dk_acc, dv_acc):
        m = pl.program_id(1)

        @pl.when(m == 0)
        def _():
            dk_acc[...] = jnp.zeros_like(dk_acc)
            dv_acc[...] = jnp.zeros_like(dv_acc)
            db_ref[...] = jnp.zeros_like(db_ref)

        @pl.when(m < nt)
        def _():
            k, v = _attn_window(k0, k1, k2, v0, v1, v2)
            q = (q_ref[...] * ATT_SCALE).astype(BF16)
            dout = do_ref[...]
            for hh in range(2):
                cols = slice(hh * ATT_HEAD_DIM, (hh + 1) * ATT_HEAD_DIM)
                qh, kh, vh, doh = q[:, cols], k[:, cols], v[:, cols], dout[:, cols]
                e, total = _attn_exp(qh, kh, b_ref[0, hh])
                p = e / total
                dvh = lax.dot_general(p.astype(BF16), doh, _DIMS["tn"], preferred_element_type=F32)
                dp = lax.dot_general(doh, vh, _DIMS["nt"], preferred_element_type=F32)
                ds = p * (dp - jnp.sum(dp * p, axis=-1, keepdims=True))
                db_ref[hh] += ds
                dsb = ds.astype(BF16)
                dqh = jnp.dot(dsb, kh, preferred_element_type=F32) * ATT_SCALE
                dkh = lax.dot_general(dsb, qh, _DIMS["tn"], preferred_element_type=F32)
                dq_ref[:, cols] = dqh.astype(dq_ref.dtype)
                dk_acc[:, cols] += dkh
                dv_acc[:, cols] += dvh

        dk_ref[...] = dk_acc[:ATT_TQ].astype(dk_ref.dtype)
        dv_ref[...] = dv_acc[:ATT_TQ].astype(dv_ref.dtype)
        for acc in (dk_acc, dv_acc):
            rest = acc[ATT_TQ:]
            acc[: win - ATT_TQ] = rest
            acc[win - ATT_TQ :] = jnp.zeros((ATT_TQ, V7X_LANES), F32)

    blk = (ATT_TQ, V7X_LANES)
    specs = _attn_specs(nt)
    specs.append(pl.BlockSpec(blk, lambda hp, m: (jnp.minimum(m, nt - 1), hp)))
    specs.append(_bias_spec())
    done = lambda hp, m: (jnp.maximum(m - 2, 0), hp)
    out_specs = [
        pl.BlockSpec(blk, lambda hp, m: (jnp.minimum(m, nt - 1), hp)),
        pl.BlockSpec(blk, done),
        pl.BlockSpec(blk, done),
        pl.BlockSpec((2, ATT_TQ, win), lambda hp, m: (hp, 0, 0)),
    ]
    out_shape = [jax.ShapeDtypeStruct((s, ATT_WIDTH), BF16)] * 3
    out_shape.append(jax.ShapeDtypeStruct((ATT_HEADS, ATT_TQ, win), F32))
    return pl.pallas_call(
        body,
        name=name,
        grid=(ATT_HEADS // 2, nt + 2),
        in_specs=specs,
        out_specs=out_specs,
        out_shape=out_shape,
        scratch_shapes=[pltpu.VMEM((win, V7X_LANES), F32), pltpu.VMEM((win, V7X_LANES), F32)],
        compiler_params=_cparams(dimension_semantics=("arbitrary", "arbitrary")),
    )(proj, proj, proj, proj, proj, proj, proj, do, bias)


def _ada_fwd(c_all, w, name):
    def body(c_ref, w_ref, o_ref):
        act = _silu(c_ref[...]).astype(BF16)
        o_ref[...] = jnp.dot(act, w_ref[...].astype(BF16), preferred_element_type=F32)

    return pl.pallas_call(
        body, name=name, out_shape=jax.ShapeDtypeStruct((c_all.shape[0], w.shape[1]), F32), compiler_params=_cparams()
    )(c_all, w)


def _ada_bwd(c_all, dmod, name):
    def body(c_ref, d_ref, o_ref):
        act = _silu(c_ref[...])
        o_ref[...] = lax.dot_general(act, d_ref[...], _DIMS["tn"], preferred_element_type=F32,
                                     precision=lax.Precision.HIGHEST)

    return pl.pallas_call(
        body, name=name, out_shape=jax.ShapeDtypeStruct((c_all.shape[1], dmod.shape[1]), F32), compiler_params=_cparams()
    )(c_all, dmod)


def _adamw_parts(landed, sent, me, w, m, v, name, rows=256):
    r, c = w.shape
    tr = _pick(r, rows, 16)

    def body(me_ref, g_ref, own_ref, w_ref, m_ref, v_ref, go_ref, d_ref, mo_ref, vo_ref):
        mine = me_ref[0]
        grad = jnp.zeros((tr, c), F32)
        for d in range(N_DEV):
            grad = grad + jnp.where(mine == d, own_ref[0], g_ref[d]).astype(F32)
        _adamw_update(grad, w_ref, m_ref, v_ref, go_ref, d_ref, mo_ref, vo_ref)

    spec = pl.BlockSpec((tr, c), lambda i, me_ref: (i, 0))
    return pl.pallas_call(
        body,
        name=name,
        grid_spec=pltpu.PrefetchScalarGridSpec(
            num_scalar_prefetch=1,
            grid=(r // tr,),
            in_specs=[pl.BlockSpec((N_DEV, tr, c), lambda i, me_ref: (0, i, 0)),
                      pl.BlockSpec((1, tr, c), lambda i, me_ref: (me_ref[0], i, 0)), spec, spec, spec],
            out_specs=[spec] * 4,
        ),
        out_shape=[jax.ShapeDtypeStruct((r, c), F32)] * 4,
        compiler_params=_cparams(dimension_semantics=("parallel",)),
    )(me.reshape(1).astype(jnp.int32), landed, sent, w, m, v)


def _adamw_update(grad, w_ref, m_ref, v_ref, go_ref, d_ref, mo_ref, vo_ref):
    m2 = ADAM_B1 * m_ref[...] + (1.0 - ADAM_B1) * grad
    v2 = ADAM_B2 * v_ref[...] + (1.0 - ADAM_B2) * (grad * grad)
    m_hat = m2 / (1.0 - ADAM_B1**ADAM_STEP)
    v_hat = v2 / (1.0 - ADAM_B2**ADAM_STEP)
    go_ref[...] = grad
    d_ref[...] = -ADAM_LR * (m_hat / (jnp.sqrt(v_hat) + ADAM_EPS) + ADAM_WD * w_ref[...])
    mo_ref[...] = m2
    vo_ref[...] = v2


def _adamw(g, w, m, v, name, rows=256):
    r, c = w.shape
    tr = _pick(r, rows, 16)

    def body(g_ref, w_ref, m_ref, v_ref, go_ref, d_ref, mo_ref, vo_ref):
        _adamw_update(g_ref[...], w_ref, m_ref, v_ref, go_ref, d_ref, mo_ref, vo_ref)

    spec = pl.BlockSpec((tr, c), lambda i: (i, 0))
    return pl.pallas_call(
        body,
        name=name,
        grid=(r // tr,),
        in_specs=[spec, spec, spec, spec],
        out_specs=[spec] * 4,
        out_shape=[jax.ShapeDtypeStruct((r, c), F32)] * 4,
        compiler_params=_cparams(dimension_semantics=("parallel",)),
    )(g, w, m, v)


def _sum_parts(parts, name):
    def body(p_ref, o_ref):
        acc = p_ref[0]
        for d in range(1, N_DEV):
            acc = acc + p_ref[d]
        o_ref[...] = acc

    return pl.pallas_call(
        body, name=name, out_shape=jax.ShapeDtypeStruct(parts.shape[1:], F32), compiler_params=_cparams()
    )(parts)


def _place():
    x, y, c = lax.axis_index("x"), lax.axis_index("y"), lax.axis_index("c")
    return x, y, c


def _dev_index(p):
    return 4 * p[0] + 2 * p[1] + p[2]


def _allgather_vmem(shard, name):
    m_per, n = shard.shape

    def body(x_ref, out_ref, send_sems, recv_sems, local_sem):
        x, y, c = _place()
        me, sibling = (x, y, c), (x, y, 1 - c)
        chips = [(1 - x, y), (x, 1 - y), (1 - x, 1 - y)]

        def rows(p):
            return out_ref.at[pl.ds(_dev_index(p) * m_per, m_per), :]

        def copy(k, block, to, src=None):
            return pltpu.make_async_remote_copy(
                src_ref=rows(block) if src is None else src, dst_ref=rows(block),
                send_sem=send_sems.at[k], recv_sem=recv_sems.at[k], device_id=to, device_id_type=MESH)

        mine = pltpu.make_async_copy(x_ref, rows(me), local_sem)
        mine.start()
        first = [copy(0, me, sibling, src=x_ref)]
        first += [copy(1 + j, me, (*chip, c), src=x_ref) for j, chip in enumerate(chips)]
        for cp in first:
            cp.start()
        passed = [copy(4 + j, (*chip, c), sibling) for j, chip in enumerate(chips)]
        for j, chip in enumerate(chips):
            copy(1 + j, (*chip, c), me).wait_recv()
            passed[j].start()
        copy(0, sibling, me).wait_recv()
        for j, chip in enumerate(chips):
            copy(4 + j, (*chip, 1 - c), me).wait_recv()
        for cp in first + passed:
            cp.wait_send()
        mine.wait()

    return pl.pallas_call(
        body,
        name=name,
        out_shape=jax.ShapeDtypeStruct((N_DEV * m_per, n), shard.dtype),
        in_specs=[pl.BlockSpec(memory_space=pltpu.VMEM)],
        out_specs=pl.BlockSpec(memory_space=pltpu.VMEM),
        scratch_shapes=[pltpu.SemaphoreType.DMA((7,)), pltpu.SemaphoreType.DMA((7,)), pltpu.SemaphoreType.DMA],
        compiler_params=_cparams(),
    )(shard)


def _allgather_hbm(shards, name):
    n = len(shards)

    def body(*refs):
        ins, outs = refs[:n], refs[n : 2 * n]
        send_sems, recv_sems, local_sems = refs[2 * n :]
        x, y, c = _place()
        me, sibling = (x, y, c), (x, y, 1 - c)
        chips = [(1 - x, y), (x, 1 - y), (1 - x, 1 - y)]

        def copy(a, k, block, to, src=None):
            dst = outs[a].at[_dev_index(block)]
            return pltpu.make_async_remote_copy(
                src_ref=dst if src is None else src, dst_ref=dst,
                send_sem=send_sems.at[a * 7 + k], recv_sem=recv_sems.at[a * 7 + k], device_id=to, device_id_type=MESH)

        mine = [pltpu.make_async_copy(ins[a], outs[a].at[_dev_index(me)], local_sems.at[a]) for a in range(n)]
        for cp in mine:
            cp.start()
        first = []
        for a in range(n):
            first.append(copy(a, 0, me, sibling, src=ins[a]))
            first += [copy(a, 1 + j, me, (*chip, c), src=ins[a]) for j, chip in enumerate(chips)]
        for cp in first:
            cp.start()
        passed = []
        for j, chip in enumerate(chips):
            for a in range(n):
                copy(a, 1 + j, (*chip, c), me).wait_recv()
                cp = copy(a, 4 + j, (*chip, c), sibling)
                cp.start()
                passed.append(cp)
        for a in range(n):
            copy(a, 0, sibling, me).wait_recv()
        for j, chip in enumerate(chips):
            for a in range(n):
                copy(a, 4 + j, (*chip, 1 - c), me).wait_recv()
        for cp in first + passed:
            cp.wait_send()
        for cp in mine:
            cp.wait()

    any_spec = pl.BlockSpec(memory_space=pl.ANY)
    return pl.pallas_call(
        body,
        name=name,
        out_shape=[jax.ShapeDtypeStruct((N_DEV, *s.shape), s.dtype) for s in shards],
        in_specs=[any_spec] * n,
        out_specs=[any_spec] * n,
        scratch_shapes=[pltpu.SemaphoreType.DMA((7 * n,)), pltpu.SemaphoreType.DMA((7 * n,)),
                        pltpu.SemaphoreType.DMA((n,))],
        compiler_params=_cparams(),
    )(*shards)


def _exchange_hbm(bufs, name):
    n = len(bufs)

    def body(*refs):
        ins, outs = refs[:n], refs[n : 2 * n]
        send_sems, recv_sems, local_sems = refs[2 * n :]
        x, y, c = _place()
        me = _dev_index((x, y, c))
        mine = [pltpu.make_async_copy(ins[a].at[me], outs[a].at[me], local_sems.at[a]) for a in range(n)]
        for cp in mine:
            cp.start()
        def peer_of(k):
            return (1 - x if k & 4 else x, 1 - y if k & 2 else y, 1 - c if k & 1 else c)

        copies = []
        for k in range(1, N_DEV):
            peer = peer_of(k)
            for a in range(n):
                copies.append(pltpu.make_async_remote_copy(
                    src_ref=ins[a].at[_dev_index(peer)], dst_ref=outs[a].at[me],
                    send_sem=send_sems.at[a * 7 + k - 1], recv_sem=recv_sems.at[a * 7 + k - 1],
                    device_id=peer, device_id_type=MESH))
        for cp in copies:
            cp.start()
        for k in range(1, N_DEV):
            peer = peer_of(k)
            for a in range(n):
                pltpu.make_async_remote_copy(
                    src_ref=ins[a].at[me], dst_ref=outs[a].at[_dev_index(peer)],
                    send_sem=send_sems.at[a * 7 + k - 1], recv_sem=recv_sems.at[a * 7 + k - 1],
                    device_id=peer, device_id_type=MESH).wait_recv()
        for cp in copies:
            cp.wait_send()
        for cp in mine:
            cp.wait()

    any_spec = pl.BlockSpec(memory_space=pl.ANY)
    return pl.pallas_call(
        body,
        name=name,
        out_shape=[jax.ShapeDtypeStruct(b.shape, b.dtype) for b in bufs],
        in_specs=[any_spec] * n,
        out_specs=[any_spec] * n,
        scratch_shapes=[pltpu.SemaphoreType.DMA((7 * n,)), pltpu.SemaphoreType.DMA((7 * n,)),
                        pltpu.SemaphoreType.DMA((n,))],
        compiler_params=_cparams(),
    )(*bufs)


HBM_SPEC = pl.BlockSpec(memory_space=pltpu.HBM)
SEM_SPEC = pl.BlockSpec(memory_space=pltpu.SEMAPHORE)
EFFECT = pltpu.SideEffectType.DATAFLOW_SIDE_EFFECTING


def _peers(x, y, c):
    return [(1 - x if k & 4 else x, 1 - y if k & 2 else y, 1 - c if k & 1 else c) for k in range(1, N_DEV)]


def _push_start(groups, sliced, name, after=()):
    flat = [b for g in groups for b in g]
    n, ng = len(flat), len(groups)
    sizes = [len(g) for g in groups]
    lands = [lax.empty(b.shape if sliced else (N_DEV, *b.shape), b.dtype) for b in flat]

    def body(*refs):
        ins, lnd = refs[:n], refs[n : 2 * n]
        sems = refs[2 * n + len(after) : 2 * n + len(after) + 2 * ng]
        token = refs[-1]
        x, y, c = _place()
        me = _dev_index((x, y, c))
        first = 0
        for gi, size in enumerate(sizes):
            for k, peer in enumerate(_peers(x, y, c)):
                for j in range(first, first + size):
                    sem = (j - first) * 7 + k
                    pltpu.make_async_remote_copy(
                        src_ref=ins[j].at[_dev_index(peer)] if sliced else ins[j], dst_ref=lnd[j].at[me],
                        send_sem=sems[2 * gi].at[sem], recv_sem=sems[2 * gi + 1].at[sem],
                        device_id=peer, device_id_type=MESH).start()
            first += size
        token[...] = jnp.zeros_like(token)

    out_shape = []
    for size in sizes:
        out_shape += [pltpu.SemaphoreType.DMA((7 * size,)), pltpu.SemaphoreType.DMA((7 * size,))]
    out_shape += [pltpu.HBM(b.shape, b.dtype) for b in flat + lands]
    out_shape.append(jax.ShapeDtypeStruct((V7X_SUBLANES, V7X_LANES), F32))
    res = pl.pallas_call(
        body,
        name=name,
        out_shape=tuple(out_shape),
        in_specs=[HBM_SPEC] * (2 * n) + [ANY_SPEC] * len(after),
        out_specs=tuple([SEM_SPEC] * (2 * ng) + [HBM_SPEC] * (2 * n) + [pl.BlockSpec(memory_space=pltpu.VMEM)]),
        input_output_aliases={i: 2 * ng + i for i in range(2 * n)},
        compiler_params=pltpu.CompilerParams(has_side_effects=EFFECT),
    )(*[pltpu.with_memory_space_constraint(b, pltpu.HBM) for b in flat + lands], *after)
    sems, thru, token = res[: 2 * ng], res[2 * ng : 2 * ng + 2 * n], res[-1]
    out, first = [], 0
    for gi, size in enumerate(sizes):
        out.append((sems[2 * gi], sems[2 * gi + 1], list(thru[first : first + size]),
                    list(thru[n + first : n + first + size])))
        first += size
    return out, token


def _push_wait(started, sliced, after, name):
    send_sems, recv_sems, bufs, lands = started
    n = len(bufs)

    def body(*refs):
        ins, lnd = refs[:n], refs[n : 2 * n]
        send_ref, recv_ref = refs[2 * n], refs[2 * n + 1]
        x, y, c = _place()
        for k, peer in enumerate(_peers(x, y, c)):
            for j in range(n):
                cp = pltpu.make_async_remote_copy(
                    src_ref=ins[j].at[_dev_index(peer)] if sliced else ins[j], dst_ref=lnd[j].at[_dev_index(peer)],
                    send_sem=send_ref.at[j * 7 + k], recv_sem=recv_ref.at[j * 7 + k],
                    device_id=peer, device_id_type=MESH)
                cp.wait_send()
                cp.wait_recv()

    res = pl.pallas_call(
        body,
        name=name,
        out_shape=tuple(pltpu.HBM(b.shape, b.dtype) for b in bufs + lands),
        in_specs=[HBM_SPEC] * (2 * n) + [SEM_SPEC, SEM_SPEC, pl.BlockSpec(memory_space=pl.ANY)],
        out_specs=tuple([HBM_SPEC] * (2 * n)),
        input_output_aliases={i: i for i in range(2 * n)},
        compiler_params=pltpu.CompilerParams(has_side_effects=EFFECT),
    )(*bufs, *lands, send_sems, recv_sems, after)
    return list(res[n:])


def _with_own_row(land, own, me):
    return lax.dynamic_update_index_in_dim(land, own, me, 0)


def _cols_full(g):
    return jnp.transpose(g, (1, 0, 2)).reshape(g.shape[1], -1)


def _rows_full(g):
    return g.reshape(-1, g.shape[2])


def _cols_parts(full, n=N_DEV):
    r = full.shape[0]
    return jnp.transpose(full.reshape(r, n, -1), (1, 0, 2)).astype(BF16)


def _rows_parts(full):
    return full.reshape(N_DEV, -1, full.shape[1]).astype(BF16)


def _block_diag(w):
    eye = jnp.eye(LRU_BLOCKS, dtype=w.dtype)
    return jnp.einsum("nkj,nm->nkmj", w, eye).reshape(LRU_WIDTH, LRU_WIDTH)


def _pad_rows(v, rows):
    flat = v.reshape(-1)
    return jnp.pad(flat, (0, rows * D_MODEL - flat.shape[0])).reshape(rows, D_MODEL)


def _my_cols(full, me, width):
    return lax.dynamic_slice_in_dim(full, me * width, width, axis=full.ndim - 1)


def kernel(x, c, w_ada, b_ada, norm_pre, norm_post, ffn1_w_gu, ffn1_w_down, w_in, rel_bias, conv_w, conv_b, lru_wa, lru_ba, lru_wx, lru_bx, lru_lambda, w_att_o, w_rec_o, w_out, ffn2_w_gu, ffn2_w_down, loss_target, m_w_ada, m_b_ada, m_norm_pre, m_norm_post, m_ffn1_w_gu, m_ffn1_w_down, m_w_in, m_rel_bias, m_conv_w, m_conv_b, m_lru_wa, m_lru_ba, m_lru_wx, m_lru_bx, m_lru_lambda, m_w_att_o, m_w_rec_o, m_w_out, m_ffn2_w_gu, m_ffn2_w_down, v_w_ada, v_b_ada, v_norm_pre, v_norm_post, v_ffn1_w_gu, v_ffn1_w_down, v_w_in, v_rel_bias, v_conv_w, v_conv_b, v_lru_wa, v_lru_ba, v_lru_wx, v_lru_bx, v_lru_lambda, v_w_att_o, v_w_rec_o, v_w_out, v_ffn2_w_gu, v_ffn2_w_down):
    weights = dict(w_ada=w_ada, b_ada=b_ada, norm_pre=norm_pre, norm_post=norm_post, ffn1_w_gu=ffn1_w_gu,
                   ffn1_w_down=ffn1_w_down, w_in=w_in, rel_bias=rel_bias, conv_w=conv_w, conv_b=conv_b,
                   lru_wa=lru_wa, lru_ba=lru_ba, lru_wx=lru_wx, lru_bx=lru_bx, lru_lambda=lru_lambda,
                   w_att_o=w_att_o, w_rec_o=w_rec_o, w_out=w_out, ffn2_w_gu=ffn2_w_gu, ffn2_w_down=ffn2_w_down)
    mom1 = dict(w_ada=m_w_ada, b_ada=m_b_ada, norm_pre=m_norm_pre, norm_post=m_norm_post, ffn1_w_gu=m_ffn1_w_gu,
                ffn1_w_down=m_ffn1_w_down, w_in=m_w_in, rel_bias=m_rel_bias, conv_w=m_conv_w, conv_b=m_conv_b,
                lru_wa=m_lru_wa, lru_ba=m_lru_ba, lru_wx=m_lru_wx, lru_bx=m_lru_bx, lru_lambda=m_lru_lambda,
                w_att_o=m_w_att_o, w_rec_o=m_w_rec_o, w_out=m_w_out, ffn2_w_gu=m_ffn2_w_gu, ffn2_w_down=m_ffn2_w_down)
    mom2 = dict(w_ada=v_w_ada, b_ada=v_b_ada, norm_pre=v_norm_pre, norm_post=v_norm_post, ffn1_w_gu=v_ffn1_w_gu,
                ffn1_w_down=v_ffn1_w_down, w_in=v_w_in, rel_bias=v_rel_bias, conv_w=v_conv_w, conv_b=v_conv_b,
                lru_wa=v_lru_wa, lru_ba=v_lru_ba, lru_wx=v_lru_wx, lru_bx=v_lru_bx, lru_lambda=v_lru_lambda,
                w_att_o=v_w_att_o, w_rec_o=v_w_rec_o, w_out=v_w_out, ffn2_w_gu=v_ffn2_w_gu, ffn2_w_down=v_ffn2_w_down)
    order = list(weights)
    big = ["ffn1_w_gu", "ffn1_w_down", "w_in", "w_att_o", "w_rec_o", "w_out", "ffn2_w_gu", "ffn2_w_down"]
    col_sharded = {"ffn1_w_gu", "w_in", "w_att_o", "ffn2_w_gu"}
    small = ["b_ada", "norm_pre", "norm_post", "rel_bias", "conv_w", "conv_b", "lru_wa", "lru_ba", "lru_wx",
             "lru_bx", "lru_lambda"]

    xi, yi, ci = _place()
    me = _dev_index((xi, yi, ci))
    x0 = x[0]
    target = loss_target[0]
    fuse_tm = min(FUSE_TM, x0.shape[0])

    transposed = {"ffn1_w_gu", "w_in", "ffn2_w_gu"}
    local = lambda n, arr: jnp.transpose(arr[0]) if n in transposed else arr[0]
    shards = {n: local(n, weights[n]).astype(BF16) for n in big}
    full_of = lambda n, g: _cols_full(g) if n == "w_att_o" else _rows_full(g)

    pack = jnp.concatenate([c.reshape(-1), norm_pre.reshape(-1), norm_post.reshape(-1), conv_w.reshape(-1)])
    pack = jnp.pad(pack, (0, 3072 - pack.shape[0])).reshape(8, 384)
    got = _allgather_vmem(pack, "gather_small_inputs").reshape(N_DEV, 3072)
    c_all = got[:, :1024]
    unshard = lambda blk, rows: jnp.transpose(blk.reshape(N_DEV, rows, 128), (1, 0, 2)).reshape(rows, D_MODEL)
    g_pre = unshard(got[:, 1024:1408], 3)
    g_post = unshard(got[:, 1408:1792], 3)
    conv_taps = unshard(got[:, 1792:2304], 4)
    conv_w8 = jnp.concatenate([conv_taps, jnp.zeros((4, LRU_WIDTH), F32)], axis=0)

    mod_cols = _ada_fwd(c_all, w_ada[0], "ada_fwd")
    mod_all = _allgather_vmem(mod_cols, "gather_mod").reshape(N_DEV, N_DEV, 1152)
    mod = lax.dynamic_index_in_dim(mod_all, me, axis=1, keepdims=False).reshape(1, -1) + b_ada
    mod = mod.reshape(3, 3, 1, D_MODEL)

    w_slab = _slab_weights(lru_wa[0], lru_wx[0])
    bias = _bias_tile(rel_bias[0])

    res_w = (0.5, 1.0, 0.5)
    row = lambda v: v.reshape(1, -1)

    (w1_gu,) = _allgather_hbm([shards["ffn1_w_gu"]], "gather_ffn1_w_gu")
    weight_groups = [["ffn1_w_down"], ["w_in"], ["w_att_o", "w_rec_o", "w_out"], ["ffn2_w_gu", "ffn2_w_down"]]
    weights_started, started = _push_start([[shards[n] for n in g] for g in weight_groups], False,
                                           "gather_weights_start", after=(mod, w1_gu))
    full = {"ffn1_w_gu": _rows_full(w1_gu)}

    def gathered_group(gi, after):
        lands = _push_wait(weights_started[gi], False, after, f"gather_weights_wait{gi}")
        for n, land in zip(weight_groups[gi], lands):
            full[n] = full_of(n, jnp.where(is_me, shards[n][None], land))

    is_me = (jnp.arange(N_DEV) == me)[:, None, None]

    def ffn_fwd(xin, k, gi, tag, deps=()):
        h, a, g, u = _pre_up(xin, row(g_pre[k]), mod[k, 0], mod[k, 1], full[f"{tag}_w_gu"], f"{tag}_up", deps=deps)
        if f"{tag}_w_down" not in full:
            gathered_group(gi, a)
        f, xout = _matmul_post(a, full[f"{tag}_w_down"], xin, row(g_post[k]), mod[k, 2], res_w[k], f"{tag}_down")
        return xout, (h, g, u, a, f)

    x1, saved1 = ffn_fwd(x0, 0, 0, "ffn1", deps=(started,))

    gathered_group(1, x1)
    h2, proj = _pre_matmul(x1, row(g_pre[1]), mod[1, 0], mod[1, 1], full["w_in"], "mix_in",
                           b_shift=3 * ATT_WIDTH // 512)
    att_o = _attn_fwd(proj, bias, "attn_fwd")
    gathered_group(2, att_o)
    xc, pre, a_t, u_t = _lru_front(proj, conv_w8, conv_b, w_slab, lru_ba, lru_bx, lru_lambda, "lru_front")
    hs, h_prev = _scan_fwd(a_t, u_t, "lru_scan")
    (rec_in,) = _rowwise(_recin_fn, "rec_in", [], [hs, (proj, LRU_WIDTH, 1)], [(LRU_WIDTH, BF16)])
    att = _matmul(att_o, full["w_att_o"], "nn", F32, "att_out")
    rec = _matmul(rec_in, full["w_rec_o"], "nn", F32, "rec_out")
    (merged,) = _rowwise(_merge_fn, "merge", [], [att, rec, (proj, LRU_WIDTH, 2), (proj, LRU_WIDTH, 3)],
                         [(D_MODEL, BF16)])
    f2, x2 = _matmul_post(merged, full["w_out"], x1, row(g_post[1]), mod[1, 2], res_w[1], "mix_out")

    gathered_group(3, x2)
    x3, saved3 = ffn_fwd(x2, 2, 2, "ffn2")

    dy, sq = _loss_stage(x3, target, "loss")
    loss = lax.psum(0.5 * jnp.sum(sq) / D_MODEL, ("x", "y", "c"))

    grads = {}
    dmod = [[None] * 3 for _ in range(3)]
    d_pre, d_post = [None] * 3, [None] * 3

    pending = []

    def exchange_start(names, tag, after=()):
        send = [(_cols_parts if n == "w_att_o" else _rows_parts)(grads[n]) for n in names]
        (group,), token = _push_start([send], True, f"exchange_{tag}_start", after=after)
        pending.append((names, send, group, tag))
        return token

    def exchange_finish(names, send, group, tag, after):
        lands = _push_wait(group, True, after, f"exchange_{tag}_wait")
        res = None
        for n, land, mine in zip(names, lands, send):
            res = _adamw_parts(land, mine, me, local(n, weights[n]), local(n, mom1[n]), local(n, mom2[n]),
                               f"adamw_{n}")
            back = (lambda r: jnp.transpose(r)) if n in transposed else (lambda r: r)
            out_g[n], out_d[n], out_m[n], out_v[n] = [back(r).reshape(weights[n].shape) for r in res]
        return res[0]

    out_g, out_d, out_m, out_v = {}, {}, {}, {}

    def ffn_bwd(xin, k, saved, dout, tag):
        h, g, u, a, f = saved
        w_gu, w_down = f"{tag}_w_gu", f"{tag}_w_down"
        df, dgu, d_post[k], dmod[k][2] = _post_bwd_up_bwd(f, dout, row(g_post[k]), mod[k, 2], res_w[k], full[w_down],
                                                          g, u, f"{tag}_up_bwd")
        grads[w_down] = _matmul(a, df, "tn", BF16, f"{tag}_dw_down", tm=1408, tn=1024, tk=1024)
        started = exchange_start([w_down], w_down)
        grads[w_gu] = _dw_gu(dgu, h, f"{tag}_dw_gu", deps=(started,))
        started = exchange_start([w_gu], w_gu)
        halves = [(dgu, (None, fuse_tm, D_FF), lambda i, half=half: (half, i, 0), (half * D_FF, (half + 1) * D_FF))
                  for half in range(2)]
        dx, d_pre[k], dmod[k][0], dmod[k][1] = _matmul_pre_bwd(halves, full[w_gu], xin, dout, row(g_pre[k]),
                                                                mod[k, 0], mod[k, 1], f"{tag}_dh", deps=(started,))
        return dx

    dx2 = ffn_bwd(x2, 2, saved3, dy, "ffn2")

    df2, dmerged, d_post[1], dmod[1][2] = _post_bwd_matmul(f2, dx2, row(g_post[1]), mod[1, 2], res_w[1],
                                                           full["w_out"], "mix_dmerged")
    grads["w_out"] = _matmul(merged, df2, "tn", BF16, "mix_dw_out", tm=1024, tn=1024, tk=1024)

    def merge_bwd(att, rec, g_att, g_rec, dm):
        _, vjp = jax.vjp(_merge_fn, att, rec, g_att, g_rec)
        return vjp(dm)

    datt, drec, dg_att, dg_rec = _rowwise(
        merge_bwd, "merge_bwd", [], [att, rec, (proj, LRU_WIDTH, 2), (proj, LRU_WIDTH, 3), dmerged],
        [(D_MODEL, BF16)] * 4)
    datt_o = _matmul(datt, full["w_att_o"], "nt", BF16, "att_out_bwd")
    grads["w_att_o"] = _matmul(att_o, datt, "tn", BF16, "dw_att_o", tm=512, tn=1024, tk=1024)
    drec_in = _matmul(drec, full["w_rec_o"], "nt", F32, "rec_out_bwd")
    grads["w_rec_o"] = _matmul(rec_in, drec, "tn", BF16, "dw_rec_o", tm=1024, tn=1024, tk=1024)
    started = exchange_start(["w_out", "w_att_o", "w_rec_o"], "mix_out")

    def recin_bwd(hs, yr, d):
        _, vjp = jax.vjp(_recin_fn, hs, yr)
        return vjp(d)

    dhs, dyr = _rowwise(recin_bwd, "rec_in_bwd", [], [hs, (proj, LRU_WIDTH, 1), drec_in],
                        [(LRU_WIDTH, F32), (LRU_WIDTH, BF16)], deps=(started,))
    g_t = _scan_bwd(a_t, dhs, "lru_scan_bwd")
    dpre, dxc, d_ba, d_bx, d_lam = _lru_back(pre, xc, w_slab, lru_ba, lru_bx, lru_lambda, g_t, h_prev, "lru_back")
    dxr, d_conv_w8, d_conv_b = _conv_bwd(proj, conv_w8, dxc, "conv_bwd")
    dq, dk, dv, dbias = _attn_bwd(proj, bias, datt_o, "attn_bwd")
    dproj = jnp.concatenate([dq, dk, dv, dxr, dyr, dg_att, dg_rec], axis=1)
    grads["w_in"] = _matmul(dproj, h2, "tn", BF16, "mix_dw_in", tm=1408, tn=1024, tk=1024)
    pack_mix = jnp.concatenate([d_conv_w8[:4], d_conv_b, d_ba, d_bx, d_lam, _pad_rows(_bias_grad(dbias), 3),
                                _lru_dw(xc, dpre, "lru_dw").reshape(128, D_MODEL),
                                jnp.zeros((5, D_MODEL), F32)], axis=0)
    (mix_started,), started = _push_start([[pack_mix]], False, "small_grads_mix_start")
    started = exchange_start(["w_in"], "w_in", after=(started,))
    whole = [(dproj, (fuse_tm, PROJ_WIDTH), lambda i: (i, 0), (0, PROJ_WIDTH))]
    dx1, d_pre[1], dmod[1][0], dmod[1][1] = _matmul_pre_bwd(whole, full["w_in"], x1, dx2, row(g_pre[1]), mod[1, 0],
                                                             mod[1, 1], "mix_dh", deps=(started,))

    dx0 = ffn_bwd(x0, 0, saved1, dx1, "ffn1")

    dmod_mine = jnp.concatenate([dmod[k][j] for k in range(3) for j in range(3)], axis=0)
    pack_norm = jnp.concatenate([dmod_mine, *d_pre, *d_post, jnp.zeros((1, D_MODEL), F32)], axis=0)
    (norm_started,), _ = _push_start([[pack_norm]], False, "small_grads_norm_start")

    def summed(started, pack, after, tag):
        (land,) = _push_wait(started, False, after, f"small_grads_{tag}_wait")
        parts = jnp.where(is_me, pack[None], land)
        return parts, _sum_parts(parts, f"small_grads_{tag}_sum")

    done = dx0
    last = [p for p in pending if p[3].startswith("ffn1")]
    for names, send, group, tag in pending:
        if not tag.startswith("ffn1"):
            done = exchange_finish(names, send, group, tag, done)

    _, total = summed(mix_started, pack_mix, done, "mix")
    grads["conv_w"] = _my_cols(total[0:4], me, 128)
    grads["conv_b"] = total[4:5]
    grads["lru_ba"] = total[5:6]
    grads["lru_bx"] = total[6:7]
    grads["lru_lambda"] = total[7:8]
    grads["rel_bias"] = total[8:11].reshape(-1)[: ATT_HEADS * (2 * MAX_REL + 1)].reshape(ATT_HEADS, -1)
    grads["lru_wa"] = total[11:75].reshape(LRU_BLOCKS, LRU_BLOCK, LRU_BLOCK)
    grads["lru_wx"] = total[75:139].reshape(LRU_BLOCKS, LRU_BLOCK, LRU_BLOCK)
    parts, total = summed(norm_started, pack_norm, total, "norm")
    grads["b_ada"] = total[0:9].reshape(1, -1)
    grads["norm_pre"] = _my_cols(total[9:12], me, 128)
    grads["norm_post"] = _my_cols(total[12:15], me, 128)
    dmod_all = parts[:, 0:9, :].reshape(N_DEV, 9 * D_MODEL)
    grads["w_ada"] = _ada_bwd(c_all, _my_cols(dmod_all, me, 1152), "ada_bwd")

    res = _adamw(grads["w_ada"], w_ada[0], m_w_ada[0], v_w_ada[0], "adamw_w_ada")
    out_g["w_ada"], out_d["w_ada"], out_m["w_ada"], out_v["w_ada"] = [r.reshape(w_ada.shape) for r in res]

    sizes = [int(np.prod(weights[n].shape)) for n in small]
    tot = sum(sizes)
    rows_small = -(-tot // (16 * D_MODEL)) * 16
    flat = lambda arrs: jnp.pad(jnp.concatenate([a.reshape(-1) for a in arrs]),
                                (0, rows_small * D_MODEL - tot)).reshape(rows_small, D_MODEL)
    res = _adamw(flat([grads[n] for n in small]), flat([weights[n] for n in small]),
                 flat([mom1[n] for n in small]), flat([mom2[n] for n in small]), "adamw_small", rows=rows_small)
    offs = np.cumsum([0] + sizes)
    for dst, r in zip((out_g, out_d, out_m, out_v), res):
        rf = r.reshape(-1)
        for i, n in enumerate(small):
            dst[n] = rf[offs[i] : offs[i + 1]].reshape(weights[n].shape)

    done = res[0]
    for names, send, group, tag in last:
        done = exchange_finish(names, send, group, tag, done)

    return (loss, dx0[None], *[out_g[n] for n in order], *[out_d[n] for n in order],
            *[out_m[n] for n in order], *[out_v[n] for n in order])
```

```python
import functools

import jax
import jax.numpy as jnp
import numpy as np
from jax import lax
from jax.experimental import pallas as pl
from jax.experimental.pallas import tpu as pltpu

D_MODEL = 1024
D_FF = 2816
ATT_HEADS = 8
ATT_HEAD_DIM = 64
ATT_WIDTH = 512
CHUNK = 64
LEFT_CHUNKS = 8
MAX_REL = 128
LRU_WIDTH = 1024
LRU_BLOCKS = 16
LRU_BLOCK = 64
LRU_C = 8.0
EPS = 1e-6
PROJ_WIDTH = 5632
N_DEV = 8

ADAM_LR = 0.001
ADAM_B1 = 0.9
ADAM_B2 = 0.999
ADAM_EPS = 1e-08
ADAM_WD = 0.01
ADAM_STEP = 10

V7X_LANES = 128
V7X_SUBLANES = 8
V7X_VMEM_BYTES = 64 * 1024 * 1024
VMEM_LIMIT = V7X_VMEM_BYTES - 8 * 1024 * 1024

ATT_TQ = 256
NEG = -1e30
BF16 = jnp.bfloat16
F32 = jnp.float32
MESH = pl.DeviceIdType.MESH

OFF_Q = 4 * LRU_WIDTH
OFF_K = OFF_Q + ATT_WIDTH
OFF_V = OFF_K + ATT_WIDTH


def _cparams(**kw):
    return pltpu.CompilerParams(vmem_limit_bytes=VMEM_LIMIT, **kw)


def _pick(n, target, unit=V7X_LANES):
    best = None
    for t in range(unit, min(n, target) + 1, unit):
        if n % t == 0:
            best = t
    return n if best is None else best


_DIMS = {
    "nn": (((1,), (0,)), ((), ())),
    "nt": (((1,), (1,)), ((), ())),
    "tn": (((0,), (0,)), ((), ())),
}


ANY_SPEC = pl.BlockSpec(memory_space=pl.ANY)


def _matmul(a, b, mode, out_dtype, name, tm=1024, tn=512, tk=1408, deps=(), b_shift=0):
    n_deps = len(deps)
    if mode == "nn":
        (m, k), (k2, n) = a.shape, b.shape
    elif mode == "nt":
        (m, k), (n, k2) = a.shape, b.shape
    else:
        (k, m), (k2, n) = a.shape, b.shape
    assert k == k2, (a.shape, b.shape, mode)
    tm, tn, tk = _pick(m, tm), _pick(n, tn), _pick(k, tk)
    nk = k // tk
    dims = _DIMS[mode]

    def body(a_ref, b_ref, *rest):
        o_ref, scratch = rest[n_deps], rest[n_deps + 1 :]
        p = lax.dot_general(a_ref[...], b_ref[...], dims, preferred_element_type=F32)
        if nk == 1:
            o_ref[...] = p.astype(o_ref.dtype)
        else:
            acc = scratch[0]
            kk = pl.program_id(2)

            @pl.when(kk == 0)
            def _():
                acc[...] = p

            @pl.when(kk > 0)
            def _():
                acc[...] += p

            @pl.when(kk == nk - 1)
            def _():
                o_ref[...] = acc[...].astype(o_ref.dtype)

    if mode == "nn":
        a_spec = pl.BlockSpec((tm, tk), lambda i, j, kk: (i, kk))
        b_spec = pl.BlockSpec((tk, tn), lambda i, j, kk: (kk, j))
    elif mode == "nt":
        a_spec = pl.BlockSpec((tm, tk), lambda i, j, kk: (i, kk))
        b_spec = pl.BlockSpec((tn, tk), lambda i, j, kk: ((j + b_shift) % (n // tn), kk))
    else:
        a_spec = pl.BlockSpec((tk, tm), lambda i, j, kk: (kk, i))
        b_spec = pl.BlockSpec((tk, tn), lambda i, j, kk: (kk, j))
    return pl.pallas_call(
        body,
        name=name,
        grid=(m // tm, n // tn, nk),
        in_specs=[a_spec, b_spec] + [ANY_SPEC] * n_deps,
        out_specs=pl.BlockSpec((tm, tn), lambda i, j, kk: (i, j)),
        out_shape=jax.ShapeDtypeStruct((m, n), out_dtype),
        scratch_shapes=[pltpu.VMEM((tm, tn), F32)] if nk > 1 else [],
        compiler_params=_cparams(dimension_semantics=("parallel", "parallel", "arbitrary")),
    )(a, b, *deps)


def _rowwise(fn, name, params, tiles, outs, accs=(), ts=256, with_index=False, deps=()):
    norm = []
    for t in tiles:
        if not isinstance(t, tuple):
            t = (t, t.shape[1], 0)
        norm.append(t if len(t) == 4 else (*t, None))
    s = norm[0][0].shape[0]
    ts = min(ts, s)
    assert s % ts == 0 and ts % V7X_SUBLANES == 0
    steps = s // ts
    halo_blocks = ts // V7X_SUBLANES
    n_p, n_t, n_o = len(params), len(norm), len(outs)

    def body(*refs):
        i = pl.program_id(0)
        vals = [r[...] for r in refs[: n_p + n_t]]
        res = fn(i, steps, *vals) if with_index else fn(*vals)
        if not isinstance(res, (tuple, list)):
            res = (res,)
        first_out = n_p + n_t + len(deps)
        o_refs = refs[first_out : first_out + n_o]
        a_refs = refs[first_out + n_o :]
        for r, v in zip(o_refs, res[:n_o]):
            r[...] = v.astype(r.dtype)
        for r, v in zip(a_refs, res[n_o:]):
            _accumulate(r, v, i)

    in_specs = [pl.BlockSpec(p.shape, lambda i: (0, 0)) for p in params]
    for arr, w, cb, halo in norm:
        if halo is None:
            in_specs.append(pl.BlockSpec((ts, w), lambda i, cb=cb: (i, cb)))
        elif halo == "prev":
            in_specs.append(
                pl.BlockSpec((V7X_SUBLANES, w), lambda i, cb=cb: (jnp.maximum(i * halo_blocks - 1, 0), cb))
            )
        else:
            last = s // V7X_SUBLANES - 1
            in_specs.append(
                pl.BlockSpec((V7X_SUBLANES, w), lambda i, cb=cb: (jnp.minimum((i + 1) * halo_blocks, last), cb))
            )
    in_specs += [ANY_SPEC] * len(deps)
    out_specs = [pl.BlockSpec((ts, w), lambda i: (i, 0)) for w, _ in outs]
    out_specs += [pl.BlockSpec(shape, lambda i: (0, 0)) for shape in accs]
    out_shape = [jax.ShapeDtypeStruct((s, w), dt) for w, dt in outs]
    out_shape += [jax.ShapeDtypeStruct(shape, F32) for shape in accs]
    res = pl.pallas_call(
        body,
        name=name,
        grid=(steps,),
        in_specs=in_specs,
        out_specs=out_specs,
        out_shape=out_shape,
        compiler_params=_cparams(dimension_semantics=("arbitrary",)),
    )(*params, *[t[0] for t in norm], *deps)
    return res


def _accumulate(ref, val, step):
    @pl.when(step == 0)
    def _():
        ref[...] = val

    @pl.when(step > 0)
    def _():
        ref[...] += val


def _sigmoid(z):
    return jax.nn.sigmoid(z)


def _silu(z):
    return z * _sigmoid(z)


def _gelu(z):
    return 0.5 * z * (1.0 + jnp.tanh(0.7978845608028654 * (z + 0.044715 * (z * z * z))))


def _pre_fn(g, shift, scale, x):
    r = lax.rsqrt(jnp.mean(x * x, axis=-1, keepdims=True) + EPS)
    return ((x * r) * g) * (1.0 + scale) + shift


def _post_fn(res_w, g, gate, f, x):
    r = lax.rsqrt(jnp.mean(f * f, axis=-1, keepdims=True) + EPS)
    return x + (res_w * gate) * ((f * r) * g)


def _swiglu_fn(gu):
    return _silu(gu[:, :D_FF]) * gu[:, D_FF:]


def _gates_fn(ba, bx, lam, pre, xc):
    ra = _sigmoid(pre[:, :LRU_WIDTH] + ba)
    ia = _sigmoid(pre[:, LRU_WIDTH:] + bx)
    softplus = jnp.maximum(-lam, 0.0) + jnp.log1p(jnp.exp(-jnp.abs(lam)))
    log_a = (-LRU_C) * ra * softplus
    a = jnp.exp(log_a)
    mult = jnp.sqrt(-jnp.tanh(log_a) * (a * a + 1.0))
    return a, mult * (ia * xc)


def _recin_fn(hs, yr):
    return hs * _gelu(yr)


def _merge_fn(att, rec, g_att, g_rec):
    return _sigmoid(g_att) * att + _sigmoid(g_rec) * rec


def _rowsum(v):
    return jnp.sum(v, axis=0, keepdims=True)


def _pre_fwd(x, g, shift, scale, name, deps=()):
    (h,) = _rowwise(_pre_fn, name, [g, shift, scale], [x], [(D_MODEL, BF16)], deps=deps)
    return h


def _pre_bwd(x, g, shift, scale, dh, dres, name):
    def fn(g, shift, scale, x, dh, dres):
        _, vjp = jax.vjp(_pre_fn, g, shift, scale, x)
        dg, dshift, dscale, dx = vjp(dh)
        return dx + dres, dg, dshift, dscale

    row = (1, D_MODEL)
    return _rowwise(fn, name, [g, shift, scale], [x, dh, dres], [(D_MODEL, F32)], [row, row, row])


def _post_fwd(f, x, g, gate, res_w, name):
    (y,) = _rowwise(functools.partial(_post_fn, res_w), name, [g, gate], [f, x], [(D_MODEL, F32)])
    return y


def _post_bwd(f, g, gate, res_w, dy, name, deps=()):
    def fn(g, gate, f, dy):
        _, vjp = jax.vjp(lambda g, gate, f: _post_fn(res_w, g, gate, f, 0.0), g, gate, f)
        dg, dgate, df = vjp(dy)
        return df, dg, dgate

    row = (1, D_MODEL)
    return _rowwise(fn, name, [g, gate], [f, dy], [(D_MODEL, BF16)], [row, row], deps=deps)


def _loss_stage(y, target, name):
    def fn(y, t):
        diff = y - t
        return diff * (1.0 / D_MODEL), _rowsum(diff * diff)

    return _rowwise(fn, name, [], [y, target], [(D_MODEL, F32)], [(1, D_MODEL)])


FFN_TM = 512
FFN_TF = 1408


def _glu_fn(g, u):
    return _silu(g) * u


def _ffn_up(h, w_gu_t, name):
    s = h.shape[0]
    tm = min(FFN_TM, s)
    nf = D_FF // FFN_TF

    def body(h_ref, wg_ref, wu_ref, a_ref, g_ref, u_ref):
        hv = h_ref[...]
        g = lax.dot_general(hv, wg_ref[...], _DIMS["nt"], preferred_element_type=F32)
        u = lax.dot_general(hv, wu_ref[...], _DIMS["nt"], preferred_element_type=F32)
        a_ref[...] = _glu_fn(g, u).astype(a_ref.dtype)
        g_ref[...] = g.astype(g_ref.dtype)
        u_ref[...] = u.astype(u_ref.dtype)

    out = pl.BlockSpec((tm, FFN_TF), lambda i, j: (i, j))
    return pl.pallas_call(
        body,
        name=name,
        grid=(s // tm, nf),
        in_specs=[pl.BlockSpec((tm, D_MODEL), lambda i, j: (i, 0)),
                  pl.BlockSpec((FFN_TF, D_MODEL), lambda i, j: (j, 0)),
                  pl.BlockSpec((FFN_TF, D_MODEL), lambda i, j: (nf + j, 0))],
        out_specs=[out, out, out],
        out_shape=[jax.ShapeDtypeStruct((s, D_FF), BF16)] * 3,
        compiler_params=_cparams(dimension_semantics=("parallel", "arbitrary")),
    )(h, w_gu_t, w_gu_t)


def _ffn_up_bwd(df, w_down, g, u, name, deps=()):
    s = df.shape[0]
    tm = min(FFN_TM, s)

    def body(df_ref, wd_ref, g_ref, u_ref, *rest):
        dg_ref, du_ref = rest[len(deps) :]
        da = lax.dot_general(df_ref[...], wd_ref[...], _DIMS["nt"], preferred_element_type=F32)
        _, vjp = jax.vjp(_glu_fn, g_ref[...].astype(F32), u_ref[...].astype(F32))
        dg, du = vjp(da)
        dg_ref[...] = dg.astype(dg_ref.dtype)
        du_ref[...] = du.astype(du_ref.dtype)

    blk = pl.BlockSpec((tm, FFN_TF), lambda i, j: (i, j))
    return pl.pallas_call(
        body,
        name=name,
        grid=(s // tm, D_FF // FFN_TF),
        in_specs=[pl.BlockSpec((tm, D_MODEL), lambda i, j: (i, 0)),
                  pl.BlockSpec((FFN_TF, D_MODEL), lambda i, j: (j, 0)), blk, blk] + [ANY_SPEC] * len(deps),
        out_specs=[blk, blk],
        out_shape=[jax.ShapeDtypeStruct((s, D_FF), BF16)] * 2,
        compiler_params=_cparams(dimension_semantics=("parallel", "arbitrary")),
    )(df, w_down, g, u, *deps)


def _ffn_dh(dg, du, w_gu_t, name, deps=()):
    s = dg.shape[0]
    tm, tn = min(FFN_TM, s), 512

    def body(dg_ref, du_ref, wg_ref, wu_ref, *rest):
        o_ref = rest[len(deps)]
        p = jnp.dot(dg_ref[...], wg_ref[...], preferred_element_type=F32)
        o_ref[...] = p + jnp.dot(du_ref[...], wu_ref[...], preferred_element_type=F32)

    a_spec = pl.BlockSpec((tm, D_FF), lambda i, j: (i, 0))
    return pl.pallas_call(
        body,
        name=name,
        grid=(s // tm, D_MODEL // tn),
        in_specs=[a_spec, a_spec,
                  pl.BlockSpec((D_FF, tn), lambda i, j: (0, j)),
                  pl.BlockSpec((D_FF, tn), lambda i, j: (1, j))] + [ANY_SPEC] * len(deps),
        out_specs=pl.BlockSpec((tm, tn), lambda i, j: (i, j)),
        out_shape=jax.ShapeDtypeStruct((s, D_MODEL), F32),
        compiler_params=_cparams(dimension_semantics=("parallel", "arbitrary")),
    )(dg, du, w_gu_t, w_gu_t, *deps)


FUSE_TM = 256
ROW_SPEC2 = pl.BlockSpec((1, D_MODEL), lambda i, j: (0, 0))
ROW_SPEC1 = pl.BlockSpec((1, D_MODEL), lambda i: (0, 0))
SUMS_SPEC1 = pl.BlockSpec((V7X_SUBLANES, D_MODEL), lambda i: (0, 0))
SUMS_SPEC2 = pl.BlockSpec((V7X_SUBLANES, D_MODEL), lambda i, j: (0, 0))
SUMS_SHAPE = jax.ShapeDtypeStruct((V7X_SUBLANES, D_MODEL), F32)


def _sum_rows(*rows):
    pad = jnp.zeros((V7X_SUBLANES - len(rows), rows[0].shape[1]), F32)
    return jnp.concatenate([*rows, pad], axis=0)


def _pre_up(x, g, shift, scale, w_gu_t, name, deps=()):
    s = x.shape[0]
    tm = min(FFN_TM, s)
    nf = D_FF // FFN_TF
    nd = len(deps)

    def body(x_ref, g_ref, sh_ref, sc_ref, wg_ref, wu_ref, *rest):
        h_ref, a_ref, gg_ref, u_ref, h_s = rest[nd:]

        @pl.when(pl.program_id(1) == 0)
        def _():
            h = _pre_fn(g_ref[...], sh_ref[...], sc_ref[...], x_ref[...]).astype(BF16)
            h_s[...] = h
            h_ref[...] = h

        hv = h_s[...]
        gv = lax.dot_general(hv, wg_ref[...], _DIMS["nt"], preferred_element_type=F32)
        uv = lax.dot_general(hv, wu_ref[...], _DIMS["nt"], preferred_element_type=F32)
        a_ref[...] = _glu_fn(gv, uv).astype(a_ref.dtype)
        gg_ref[...] = gv.astype(gg_ref.dtype)
        u_ref[...] = uv.astype(u_ref.dtype)

    rows = pl.BlockSpec((tm, D_MODEL), lambda i, j: (i, 0))
    out = pl.BlockSpec((tm, FFN_TF), lambda i, j: (i, j))
    return pl.pallas_call(
        body,
        name=name,
        grid=(s // tm, nf),
        in_specs=[rows, ROW_SPEC2, ROW_SPEC2, ROW_SPEC2,
                  pl.BlockSpec((FFN_TF, D_MODEL), lambda i, j: (j, 0)),
                  pl.BlockSpec((FFN_TF, D_MODEL), lambda i, j: (nf + j, 0))] + [ANY_SPEC] * nd,
        out_specs=[rows, out, out, out],
        out_shape=[jax.ShapeDtypeStruct((s, D_MODEL), BF16)] + [jax.ShapeDtypeStruct((s, D_FF), BF16)] * 3,
        scratch_shapes=[pltpu.VMEM((tm, D_MODEL), BF16)],
        compiler_params=_cparams(dimension_semantics=("parallel", "arbitrary")),
    )(x, g, shift, scale, w_gu_t, w_gu_t, *deps)


def _pre_matmul(x, g, shift, scale, w_t, name, b_shift=0, tn=512):
    s = x.shape[0]
    n = w_t.shape[0]
    tm = min(2 * FFN_TM, s)

    def body(x_ref, g_ref, sh_ref, sc_ref, w_ref, h_ref, o_ref, h_s):
        @pl.when(pl.program_id(1) == 0)
        def _():
            h = _pre_fn(g_ref[...], sh_ref[...], sc_ref[...], x_ref[...]).astype(BF16)
            h_s[...] = h
            h_ref[...] = h

        o_ref[...] = lax.dot_general(h_s[...], w_ref[...], _DIMS["nt"], preferred_element_type=F32)

    rows = pl.BlockSpec((tm, D_MODEL), lambda i, j: (i, 0))
    return pl.pallas_call(
        body,
        name=name,
        grid=(s // tm, n // tn),
        in_specs=[rows, ROW_SPEC2, ROW_SPEC2, ROW_SPEC2,
                  pl.BlockSpec((tn, D_MODEL), lambda i, j: ((j + b_shift) % (n // tn), 0))],
        out_specs=[rows, pl.BlockSpec((tm, tn), lambda i, j: (i, j))],
        out_shape=[jax.ShapeDtypeStruct((s, D_MODEL), BF16), jax.ShapeDtypeStruct((s, n), F32)],
        scratch_shapes=[pltpu.VMEM((tm, D_MODEL), BF16)],
        compiler_params=_cparams(dimension_semantics=("parallel", "arbitrary")),
    )(x, g, shift, scale, w_t)


def _matmul_post(a, w, x, g_post, gate, res_w, name, target=None):
    s, k = a.shape
    tm = min(FFN_TM, s)
    extra = [] if target is None else [target]

    def body(a_ref, w_ref, x_ref, g_ref, gate_ref, *rest):
        f = jnp.dot(a_ref[...], w_ref[...], preferred_element_type=F32)
        y = _post_fn(res_w, g_ref[...], gate_ref[...], f, x_ref[...])
        if target is None:
            f_ref, y_ref = rest
            y_ref[...] = y
        else:
            t_ref, f_ref, dy_ref, sq_ref = rest
            diff = y - t_ref[...]
            dy_ref[...] = diff * (1.0 / D_MODEL)
            _accumulate(sq_ref, _rowsum(diff * diff), pl.program_id(0))
        f_ref[...] = f

    rows = pl.BlockSpec((tm, D_MODEL), lambda i: (i, 0))
    out_specs, out_shape = [rows, rows], [jax.ShapeDtypeStruct((s, D_MODEL), F32)] * 2
    if target is not None:
        out_specs.append(ROW_SPEC1)
        out_shape.append(jax.ShapeDtypeStruct((1, D_MODEL), F32))
    return pl.pallas_call(
        body,
        name=name,
        grid=(s // tm,),
        in_specs=[pl.BlockSpec((tm, k), lambda i: (i, 0)), pl.BlockSpec((k, D_MODEL), lambda i: (0, 0)), rows,
                  ROW_SPEC1, ROW_SPEC1] + [rows] * len(extra),
        out_specs=out_specs,
        out_shape=out_shape,
        compiler_params=_cparams(dimension_semantics=("arbitrary",)),
    )(a, w, x, g_post, gate, *extra)


def _merge_matmul_post(att, rec, proj, w, x, g_post, gate, res_w, name):
    s = att.shape[0]
    tm = min(FUSE_TM, s)

    def body(att_ref, rec_ref, ga_ref, gr_ref, w_ref, x_ref, g_ref, gate_ref, m_ref, f_ref, y_ref):
        merged = _merge_fn(att_ref[...], rec_ref[...], ga_ref[...], gr_ref[...]).astype(BF16)
        m_ref[...] = merged
        f = jnp.dot(merged, w_ref[...], preferred_element_type=F32)
        f_ref[...] = f
        y_ref[...] = _post_fn(res_w, g_ref[...], gate_ref[...], f, x_ref[...])

    rows = pl.BlockSpec((tm, D_MODEL), lambda i: (i, 0))
    return pl.pallas_call(
        body,
        name=name,
        grid=(s // tm,),
        in_specs=[rows, rows, pl.BlockSpec((tm, D_MODEL), lambda i: (i, 2)), pl.BlockSpec((tm, D_MODEL), lambda i: (i, 3)),
                  pl.BlockSpec(w.shape, lambda i: (0, 0)), rows, ROW_SPEC1, ROW_SPEC1],
        out_specs=[rows, rows, rows],
        out_shape=[jax.ShapeDtypeStruct((s, D_MODEL), BF16)] + [jax.ShapeDtypeStruct((s, D_MODEL), F32)] * 2,
        compiler_params=_cparams(dimension_semantics=("parallel",)),
    )(att, rec, proj, proj, w, x, g_post, gate)


def _post_bwd_merge_bwd(f, dy, g_post, gate, res_w, w, att, rec, proj, name):
    s = f.shape[0]
    tm = min(FUSE_TM, s)

    def body(f_ref, dy_ref, gp_ref, gate_ref, w_ref, att_ref, rec_ref, ga_ref, gr_ref,
             df_ref, datt_ref, drec_ref, dga_ref, dgr_ref, sums_ref):
        i = pl.program_id(0)
        dgp, dgate, df = _post_vjp(res_w, gp_ref[...], gate_ref[...], f_ref[...], dy_ref[...])
        dfb = df.astype(BF16)
        df_ref[...] = dfb
        _accumulate(sums_ref, _sum_rows(dgp, dgate), i)
        dmerged = lax.dot_general(dfb, w_ref[...], _DIMS["nt"], preferred_element_type=F32)
        _, vjp = jax.vjp(_merge_fn, att_ref[...], rec_ref[...], ga_ref[...], gr_ref[...])
        for ref, val in zip((datt_ref, drec_ref, dga_ref, dgr_ref), vjp(dmerged)):
            ref[...] = val.astype(ref.dtype)

    rows = pl.BlockSpec((tm, D_MODEL), lambda i: (i, 0))
    return pl.pallas_call(
        body,
        name=name,
        grid=(s // tm,),
        in_specs=[rows, rows, ROW_SPEC1, ROW_SPEC1, pl.BlockSpec(w.shape, lambda i: (0, 0)), rows, rows,
                  pl.BlockSpec((tm, D_MODEL), lambda i: (i, 2)), pl.BlockSpec((tm, D_MODEL), lambda i: (i, 3))],
        out_specs=[rows] * 5 + [SUMS_SPEC1],
        out_shape=[jax.ShapeDtypeStruct((s, D_MODEL), BF16)] * 5 + [SUMS_SHAPE],
        compiler_params=_cparams(dimension_semantics=("arbitrary",)),
    )(f, dy, g_post, gate, w, att, rec, proj, proj)


def _matmul_recin_bwd(drec, w, hs, proj, name, deps=()):
    s = drec.shape[0]
    tm = min(FUSE_TM, s)
    nd = len(deps)

    def body(d_ref, w_ref, hs_ref, yr_ref, *rest):
        dhs_ref, dyr_ref = rest[nd:]
        d = lax.dot_general(d_ref[...], w_ref[...], _DIMS["nt"], preferred_element_type=F32)
        _, vjp = jax.vjp(_recin_fn, hs_ref[...], yr_ref[...])
        dhs, dyr = vjp(d)
        dhs_ref[...] = dhs
        dyr_ref[...] = dyr.astype(dyr_ref.dtype)

    rows = pl.BlockSpec((tm, D_MODEL), lambda i: (i, 0))
    return pl.pallas_call(
        body,
        name=name,
        grid=(s // tm,),
        in_specs=[rows, pl.BlockSpec(w.shape, lambda i: (0, 0)), rows,
                  pl.BlockSpec((tm, D_MODEL), lambda i: (i, 1))] + [ANY_SPEC] * nd,
        out_specs=[rows, rows],
        out_shape=[jax.ShapeDtypeStruct((s, D_MODEL), F32), jax.ShapeDtypeStruct((s, D_MODEL), BF16)],
        compiler_params=_cparams(dimension_semantics=("parallel",)),
    )(drec, w, hs, proj, *deps)


def _post_vjp(res_w, g, gate, f, dy):
    _, vjp = jax.vjp(lambda g, gate, f: _post_fn(res_w, g, gate, f, 0.0), g, gate, f)
    return vjp(dy)


def _post_bwd_up_bwd(f, dy, g_post, gate, res_w, w_down, g, u, name, deps=()):
    s = f.shape[0]
    tm = min(FFN_TM, s)
    nd = len(deps)

    def body(f_ref, dy_ref, gp_ref, gate_ref, wd_ref, g_ref, u_ref, *rest):
        df_ref, dgu_ref, sums_ref, df_s = rest[nd:]
        i = pl.program_id(0)

        @pl.when(pl.program_id(1) == 0)
        def _():
            dgp, dgate, df = _post_vjp(res_w, gp_ref[...], gate_ref[...], f_ref[...], dy_ref[...])
            df_s[...] = df.astype(BF16)
            df_ref[...] = df_s[...]
            _accumulate(sums_ref, _sum_rows(dgp, dgate), i)

        da = lax.dot_general(df_s[...], wd_ref[...], _DIMS["nt"], preferred_element_type=F32)
        _, vjp = jax.vjp(_glu_fn, g_ref[...].astype(F32), u_ref[...].astype(F32))
        dg, du = vjp(da)
        dgu_ref[0] = dg.astype(dgu_ref.dtype)
        dgu_ref[1] = du.astype(dgu_ref.dtype)

    rows = pl.BlockSpec((tm, D_MODEL), lambda i, j: (i, 0))
    blk = pl.BlockSpec((tm, FFN_TF), lambda i, j: (i, j))
    return pl.pallas_call(
        body,
        name=name,
        grid=(s // tm, D_FF // FFN_TF),
        in_specs=[rows, rows, ROW_SPEC2, ROW_SPEC2, pl.BlockSpec((FFN_TF, D_MODEL), lambda i, j: (j, 0)), blk,
                  blk] + [ANY_SPEC] * nd,
        out_specs=[rows, pl.BlockSpec((2, tm, FFN_TF), lambda i, j: (0, i, j)), SUMS_SPEC2],
        out_shape=[jax.ShapeDtypeStruct((s, D_MODEL), BF16), jax.ShapeDtypeStruct((2, s, D_FF), BF16), SUMS_SHAPE],
        scratch_shapes=[pltpu.VMEM((tm, D_MODEL), BF16)],
        compiler_params=_cparams(dimension_semantics=("arbitrary", "arbitrary")),
    )(f, dy, g_post, gate, w_down, g, u, *deps)


def _post_bwd_matmul(f, dy, g_post, gate, res_w, w, name):
    s = f.shape[0]
    n = w.shape[0]
    tm = min(FUSE_TM, s)

    def body(f_ref, dy_ref, gp_ref, gate_ref, w_ref, df_ref, o_ref, dgp_ref, dgate_ref):
        i = pl.program_id(0)
        dgp, dgate, df = _post_vjp(res_w, gp_ref[...], gate_ref[...], f_ref[...], dy_ref[...])
        dfb = df.astype(BF16)
        df_ref[...] = dfb
        _accumulate(dgp_ref, dgp, i)
        _accumulate(dgate_ref, dgate, i)
        o_ref[...] = lax.dot_general(dfb, w_ref[...], _DIMS["nt"], preferred_element_type=F32)

    rows = pl.BlockSpec((tm, D_MODEL), lambda i: (i, 0))
    return pl.pallas_call(
        body,
        name=name,
        grid=(s // tm,),
        in_specs=[rows, rows, ROW_SPEC1, ROW_SPEC1, pl.BlockSpec((n, D_MODEL), lambda i: (0, 0))],
        out_specs=[rows, pl.BlockSpec((tm, n), lambda i: (i, 0)), ROW_SPEC1, ROW_SPEC1],
        out_shape=[jax.ShapeDtypeStruct((s, D_MODEL), BF16), jax.ShapeDtypeStruct((s, n), F32),
                   jax.ShapeDtypeStruct((1, D_MODEL), F32), jax.ShapeDtypeStruct((1, D_MODEL), F32)],
        compiler_params=_cparams(dimension_semantics=("arbitrary",)),
    )(f, dy, g_post, gate, w)


def _matmul_pre_bwd(parts, w_t, x, dres, g, shift, scale, name, deps=()):
    s = x.shape[0]
    na, nd = len(parts), len(deps)
    ranges = [p[3] for p in parts]

    def body(*refs):
        a_refs = refs[:na]
        w_ref, x_ref, dres_ref, g_ref, sh_ref, sc_ref = refs[na : na + 6]
        dx_ref, sums_ref = refs[na + 6 + nd :]
        i = pl.program_id(0)
        dh = None
        for a_ref, (r0, r1) in zip(a_refs, ranges):
            p = jnp.dot(a_ref[...], w_ref[r0:r1, :], preferred_element_type=F32)
            dh = p if dh is None else dh + p
        _, vjp = jax.vjp(_pre_fn, g_ref[...], sh_ref[...], sc_ref[...], x_ref[...])
        dg, dsh, dsc, dx = vjp(dh)
        dx_ref[...] = dx + dres_ref[...]
        _accumulate(sums_ref, _sum_rows(dg, dsh, dsc), i)

    tm = parts[0][1][-2]
    rows = pl.BlockSpec((tm, D_MODEL), lambda i: (i, 0))
    return pl.pallas_call(
        body,
        name=name,
        grid=(s // tm,),
        in_specs=[pl.BlockSpec(p[1], p[2]) for p in parts]
        + [pl.BlockSpec(w_t.shape, lambda i: (0, 0)), rows, rows, ROW_SPEC1, ROW_SPEC1, ROW_SPEC1]
        + [ANY_SPEC] * nd,
        out_specs=[rows, SUMS_SPEC1],
        out_shape=[jax.ShapeDtypeStruct((s, D_MODEL), F32), SUMS_SHAPE],
        compiler_params=_cparams(dimension_semantics=("arbitrary",)),
    )(*[p[0] for p in parts], w_t, x, dres, g, shift, scale, *deps)


def _dw_gu(dgu, h, name, deps=()):
    s = h.shape[0]
    tk = min(1024, s)
    nk = s // tk
    half = D_FF // FFN_TF

    def body(a_ref, b_ref, *rest):
        o_ref, acc = rest[len(deps) :]
        kk = pl.program_id(1)
        p = lax.dot_general(a_ref[...], b_ref[...], _DIMS["tn"], preferred_element_type=F32)

        @pl.when(kk == 0)
        def _():
            acc[...] = p

        @pl.when(kk > 0)
        def _():
            acc[...] += p

        @pl.when(kk == nk - 1)
        def _():
            o_ref[...] = acc[...].astype(o_ref.dtype)

    return pl.pallas_call(
        body,
        name=name,
        grid=(2 * half, nk),
        in_specs=[pl.BlockSpec((None, tk, FFN_TF), lambda i, kk: (i // half, kk, i % half)),
                  pl.BlockSpec((tk, D_MODEL), lambda i, kk: (kk, 0))] + [ANY_SPEC] * len(deps),
        out_specs=pl.BlockSpec((FFN_TF, D_MODEL), lambda i, kk: (i, 0)),
        out_shape=jax.ShapeDtypeStruct((2 * D_FF, D_MODEL), BF16),
        scratch_shapes=[pltpu.VMEM((FFN_TF, D_MODEL), F32)],
        compiler_params=_cparams(dimension_semantics=("parallel", "arbitrary")),
    )(dgu, h, *deps)


def _lru_diag_blocks(dw_bd, name):
    def body(w_ref, o_ref):
        for half in range(2):
            for n in range(LRU_BLOCKS):
                rows = slice(n * LRU_BLOCK, (n + 1) * LRU_BLOCK)
                cols = slice(half * LRU_WIDTH + n * LRU_BLOCK, half * LRU_WIDTH + (n + 1) * LRU_BLOCK)
                o_ref[half, rows, :] = w_ref[rows, cols]

    return pl.pallas_call(
        body, name=name, out_shape=jax.ShapeDtypeStruct((2, LRU_WIDTH, LRU_BLOCK), F32), compiler_params=_cparams()
    )(dw_bd)


def _swiglu_fwd(gu, name):
    (a,) = _rowwise(_swiglu_fn, name, [], [gu], [(D_FF, BF16)], ts=128)
    return a


def _swiglu_bwd(gu, da, name, deps=()):
    def fn(gu, da):
        _, vjp = jax.vjp(_swiglu_fn, gu)
        return vjp(da)[0]

    (dgu,) = _rowwise(fn, name, [], [gu, da], [(2 * D_FF, BF16)], ts=128, deps=deps)
    return dgu


def _shift_down(ext, j, rows):
    return pltpu.roll(ext, j, 0)[V7X_SUBLANES : V7X_SUBLANES + rows]


def _shift_up(ext, j, rows):
    return pltpu.roll(ext, ext.shape[0] - j, 0)[:rows] if j else ext[:rows]


LRU_SLAB = 256
N_SLABS = LRU_WIDTH // LRU_SLAB


def _slab_weights(wa, wx):
    per = LRU_SLAB // LRU_BLOCK
    eye = jnp.eye(per, dtype=wa.dtype)

    def diag(w):
        w4 = w.reshape(N_SLABS, per, LRU_BLOCK, LRU_BLOCK)
        return jnp.einsum("sbkj,bc->sbkcj", w4, eye).reshape(N_SLABS, LRU_SLAB, LRU_SLAB)

    return jnp.concatenate([diag(wa), diag(wx)], axis=2).reshape(LRU_WIDTH, 2 * LRU_SLAB).astype(BF16)


def _slab_cols(v, s):
    lo = s * LRU_SLAB
    return jnp.concatenate([v[:, lo : lo + LRU_SLAB], v[:, LRU_WIDTH + lo : LRU_WIDTH + lo + LRU_SLAB]], axis=1)


def _lru_front(proj, w8, b, w_slab, ba, bx, lam, name):
    def fn(i, steps, w8, b, w_slab, ba, bx, lam, x, halo):
        halo = jnp.where(i > 0, halo, 0.0)
        ext = jnp.concatenate([halo, x], axis=0)
        xc = b + w8[3:4] * x
        for j in (1, 2, 3):
            xc = xc + w8[3 - j : 4 - j] * _shift_down(ext, j, x.shape[0])
        xcb = xc.astype(BF16)
        prods = []
        for s in range(N_SLABS):
            rows = slice(s * LRU_SLAB, (s + 1) * LRU_SLAB)
            prods.append(jnp.dot(xcb[:, rows], w_slab[rows], preferred_element_type=F32))
        pre = jnp.concatenate([p[:, :LRU_SLAB] for p in prods] + [p[:, LRU_SLAB:] for p in prods], axis=1)
        a, u = _gates_fn(ba, bx, lam, pre, xc)
        return xc, pre, a, u

    tiles = [(proj, LRU_WIDTH, 0), (proj, LRU_WIDTH, 0, "prev")]
    outs = [(LRU_WIDTH, F32), (2 * LRU_WIDTH, F32), (LRU_WIDTH, F32), (LRU_WIDTH, F32)]
    return _rowwise(fn, name, [w8, b, w_slab, ba, bx, lam], tiles, outs, with_index=True)


def _lru_back(pre, xc, w_slab, ba, bx, lam, g, h_prev, name, deps=()):
    def fn(w_slab, ba, bx, lam, pre, xc, g, h_prev):
        _, vjp = jax.vjp(_gates_fn, ba, bx, lam, pre, xc)
        dba, dbx, dlam, dpre, dxc = vjp((g * h_prev, g))
        dpre = dpre.astype(BF16)
        back = []
        for s in range(N_SLABS):
            rows = slice(s * LRU_SLAB, (s + 1) * LRU_SLAB)
            back.append(lax.dot_general(_slab_cols(dpre, s), w_slab[rows], _DIMS["nt"], preferred_element_type=F32))
        return dpre, dxc + jnp.concatenate(back, axis=1), _sum_rows(dba, dbx, dlam)

    return _rowwise(fn, name, [w_slab, ba, bx, lam], [pre, xc, g, h_prev],
                    [(2 * LRU_WIDTH, BF16), (LRU_WIDTH, F32)], [(V7X_SUBLANES, LRU_WIDTH)], deps=deps)


def _lru_dw(xc, dpre, name):
    s = xc.shape[0]
    ts = min(512, s)
    steps = s // ts
    per = LRU_SLAB // LRU_BLOCK

    def body(x_ref, d_ref, o_ref, acc):
        i = pl.program_id(0)
        xcb = x_ref[...].astype(BF16)
        d = d_ref[...]
        for sl in range(N_SLABS):
            rows = slice(sl * LRU_SLAB, (sl + 1) * LRU_SLAB)
            p = lax.dot_general(xcb[:, rows], _slab_cols(d, sl), _DIMS["tn"], preferred_element_type=F32)

            @pl.when(i == 0)
            def _(p=p, rows=rows):
                acc[rows, :] = p

            @pl.when(i > 0)
            def _(p=p, rows=rows):
                acc[rows, :] += p

        @pl.when(i == steps - 1)
        def _():
            for half in range(2):
                for n in range(LRU_BLOCKS):
                    r0 = n * LRU_BLOCK
                    c0 = half * LRU_SLAB + (n % per) * LRU_BLOCK
                    o_ref[half, r0 : r0 + LRU_BLOCK, :] = acc[r0 : r0 + LRU_BLOCK, c0 : c0 + LRU_BLOCK]

    return pl.pallas_call(
        body,
        name=name,
        grid=(steps,),
        in_specs=[pl.BlockSpec((ts, LRU_WIDTH), lambda i: (i, 0)), pl.BlockSpec((ts, 2 * LRU_WIDTH), lambda i: (i, 0))],
        out_specs=pl.BlockSpec((2, LRU_WIDTH, LRU_BLOCK), lambda i: (0, 0, 0)),
        out_shape=jax.ShapeDtypeStruct((2, LRU_WIDTH, LRU_BLOCK), F32),
        scratch_shapes=[pltpu.VMEM((LRU_WIDTH, 2 * LRU_SLAB), F32)],
        compiler_params=_cparams(dimension_semantics=("arbitrary",)),
    )(xc, dpre)


def _conv_bwd(proj, w8, d1, name):
    def fn(i, steps, w8, x, halo, d, d1n):
        rows = x.shape[0]
        dn = jnp.where(i < steps - 1, d1n, 0.0)
        halo = jnp.where(i > 0, halo, 0.0)
        dext = jnp.concatenate([d, dn], axis=0)
        xext = jnp.concatenate([halo, x], axis=0)
        dx = w8[3:4] * d
        dw = [None] * 4
        dw[3] = _rowsum(d * x)
        for k in (1, 2, 3):
            dx = dx + w8[3 - k : 4 - k] * _shift_up(dext, k, rows)
            dw[3 - k] = _rowsum(d * _shift_down(xext, k, rows))
        return dx, _sum_rows(*dw, _rowsum(d))

    tiles = [(proj, LRU_WIDTH, 0), (proj, LRU_WIDTH, 0, "prev"), d1, (d1, LRU_WIDTH, 0, "next")]
    return _rowwise(fn, name, [w8], tiles, [(LRU_WIDTH, BF16)], [(V7X_SUBLANES, LRU_WIDTH)], with_index=True)


SCAN_ROWS = 512


def _block_scan(a, b, row, reverse):
    for d in (1, 2, 4):
        if reverse:
            shift, keep = V7X_SUBLANES - d, row < V7X_SUBLANES - d
        else:
            shift, keep = d, row >= d
        a_s = pltpu.roll(a, shift, 0)
        b_s = pltpu.roll(b, shift, 0)
        b = jnp.where(keep, a * b_s + b, b)
        a = jnp.where(keep, a * a_s, a)
    return a, b


def _scan_fwd(a, u, proj, name):
    s, w = a.shape
    ts = min(SCAN_ROWS, s)
    sub = ts // V7X_SUBLANES

    def body(a_ref, u_ref, yr_ref, h_ref, hp_ref, rec_ref, carry):
        @pl.when(pl.program_id(0) == 0)
        def _():
            carry[...] = jnp.zeros_like(carry)

        row = lax.broadcasted_iota(jnp.int32, (V7X_SUBLANES, w), 0)

        def step(j, c):
            rows = pl.ds(pl.multiple_of(j * V7X_SUBLANES, V7X_SUBLANES), V7X_SUBLANES)
            pa, pb = _block_scan(a_ref[rows, :], u_ref[rows, :], row, False)
            h = pb + pa * c
            h_ref[rows, :] = h
            hp_ref[rows, :] = jnp.where(row >= 1, pltpu.roll(h, 1, 0), c)
            return jnp.broadcast_to(h[V7X_SUBLANES - 1 :], (V7X_SUBLANES, w))

        carry[...] = lax.fori_loop(0, sub, step, carry[...])
        rec_ref[...] = _recin_fn(h_ref[...], yr_ref[...]).astype(rec_ref.dtype)

    spec = pl.BlockSpec((ts, w), lambda i: (i, 0))
    return pl.pallas_call(
        body,
        name=name,
        grid=(s // ts,),
        in_specs=[spec, spec, pl.BlockSpec((ts, w), lambda i: (i, 1))],
        out_specs=[spec, spec, spec],
        out_shape=[jax.ShapeDtypeStruct((s, w), F32)] * 2 + [jax.ShapeDtypeStruct((s, w), BF16)],
        scratch_shapes=[pltpu.VMEM((V7X_SUBLANES, w), F32)],
        compiler_params=_cparams(dimension_semantics=("arbitrary",)),
    )(a, u, proj)


def _scan_bwd(a, dh, name):
    s, w = a.shape
    ts = min(SCAN_ROWS, s)
    sub = ts // V7X_SUBLANES
    steps = s // ts

    def body(a_ref, d_ref, g_ref, carry):
        @pl.when(pl.program_id(0) == 0)
        def _():
            carry[...] = jnp.zeros_like(carry)

        row = lax.broadcasted_iota(jnp.int32, (V7X_SUBLANES, w), 0)

        def step(jj, c):
            j = sub - 1 - jj
            rows = pl.ds(pl.multiple_of(j * V7X_SUBLANES, V7X_SUBLANES), V7X_SUBLANES)
            av, dv = a_ref[rows, :], d_ref[rows, :]
            pa, pb = _block_scan(av, av * dv, row, True)
            big = pb + pa * c
            g_ref[rows, :] = dv + jnp.where(row < V7X_SUBLANES - 1, pltpu.roll(big, V7X_SUBLANES - 1, 0), c)
            return jnp.broadcast_to(big[:1], (V7X_SUBLANES, w))

        carry[...] = lax.fori_loop(0, sub, step, carry[...])

    spec = pl.BlockSpec((ts, w), lambda i: (steps - 1 - i, 0))
    return pl.pallas_call(
        body,
        name=name,
        grid=(steps,),
        in_specs=[spec, spec],
        out_specs=spec,
        out_shape=jax.ShapeDtypeStruct((s, w), F32),
        scratch_shapes=[pltpu.VMEM((V7X_SUBLANES, w), F32)],
        compiler_params=_cparams(dimension_semantics=("arbitrary",)),
    )(a, dh)


def _rel_index():
    i = np.arange(ATT_TQ)[:, None]
    j = np.arange(3 * ATT_TQ)[None, :]
    band = (j // CHUNK >= i // CHUNK) & (j // CHUNK <= i // CHUNK + LEFT_CHUNKS)
    return band


SKEW = 4 * ATT_TQ


def _skew_onehot():
    t = np.arange(SKEW)
    diag = np.where(t < 3 * ATT_TQ, -t, SKEW - t)
    idx = np.clip(diag + LEFT_CHUNKS * CHUNK, -MAX_REL, MAX_REL) + MAX_REL
    hit = (idx[:, None] == np.arange(2 * MAX_REL + 1)[None, :]) & (t[:, None] != 3 * ATT_TQ)
    return hit.astype(np.float32)


def _bias_tile(rel_bias):
    per_t = jnp.dot(rel_bias, jnp.asarray(_skew_onehot()).T, precision=lax.Precision.HIGHEST)
    flat = jnp.broadcast_to(per_t[:, None, :], (ATT_HEADS, ATT_TQ, SKEW)).reshape(ATT_HEADS, ATT_TQ * SKEW)
    tile = flat[:, : ATT_TQ * (SKEW - 1)].reshape(ATT_HEADS, ATT_TQ, SKEW - 1)[:, :, : 3 * ATT_TQ]
    first = (2 - np.arange(3))[:, None, None, None] * ATT_TQ
    seen = _rel_index()[None, None] & (np.arange(3 * ATT_TQ)[None, None, None, :] >= first)
    return jnp.where(jnp.asarray(seen), tile[None], NEG)


def _bias_grad(dbias):
    flat = jnp.pad(dbias, ((0, 0), (0, 0), (0, SKEW - 1 - 3 * ATT_TQ))).reshape(ATT_HEADS, ATT_TQ * (SKEW - 1))
    per_t = jnp.sum(jnp.pad(flat, ((0, 0), (0, ATT_TQ))).reshape(ATT_HEADS, ATT_TQ, SKEW), axis=1)
    return jnp.dot(per_t, jnp.asarray(_skew_onehot()), precision=lax.Precision.HIGHEST)


def _attn_specs(nt):
    qb, kb, vb = OFF_Q // V7X_LANES, OFF_K // V7X_LANES, OFF_V // V7X_LANES
    blk = (ATT_TQ, V7X_LANES)

    def qmap(base):
        return lambda hp, m: (jnp.minimum(m, nt - 1), base + hp)

    def wmap(base, back):
        return lambda hp, m: (jnp.clip(m - back, 0, nt - 1), base + hp)

    specs = [pl.BlockSpec(blk, qmap(qb))]
    specs += [pl.BlockSpec(blk, wmap(kb, back)) for back in (2, 1, 0)]
    specs += [pl.BlockSpec(blk, wmap(vb, back)) for back in (2, 1, 0)]
    return specs


ATT_SCALE = ATT_HEAD_DIM**-0.5


def _attn_exp(qh, kh, bias):
    s = lax.dot_general(qh, kh, _DIMS["nt"], preferred_element_type=F32) + bias
    e = jnp.exp(s - jnp.max(s, axis=-1, keepdims=True))
    return e, jnp.sum(e, axis=-1, keepdims=True)


def _attn_window(k0, k1, k2, v0, v1, v2):
    k = jnp.concatenate([k0[...], k1[...], k2[...]], axis=0).astype(BF16)
    v = jnp.concatenate([v0[...], v1[...], v2[...]], axis=0).astype(BF16)
    return k, v


def _bias_spec():
    return pl.BlockSpec((1, 2, ATT_TQ, 3 * ATT_TQ), lambda hp, m: (jnp.minimum(m, 2), hp, 0, 0))


def _attn_fwd(proj, bias, name):
    s = proj.shape[0]
    nt = s // ATT_TQ

    def body(q_ref, k0, k1, k2, v0, v1, v2, b_ref, o_ref):
        k, v = _attn_window(k0, k1, k2, v0, v1, v2)
        q = (q_ref[...] * ATT_SCALE).astype(BF16)
        for hh in range(2):
            cols = slice(hh * ATT_HEAD_DIM, (hh + 1) * ATT_HEAD_DIM)
            e, total = _attn_exp(q[:, cols], k[:, cols], b_ref[0, hh])
            o = jnp.dot(e.astype(BF16), v[:, cols], preferred_element_type=F32) / total
            o_ref[:, cols] = o.astype(o_ref.dtype)

    specs = _attn_specs(nt) + [_bias_spec()]
    return pl.pallas_call(
        body,
        name=name,
        grid=(ATT_HEADS // 2, nt),
        in_specs=specs,
        out_specs=pl.BlockSpec((ATT_TQ, V7X_LANES), lambda hp, m: (m, hp)),
        out_shape=jax.ShapeDtypeStruct((s, ATT_WIDTH), BF16),
        compiler_params=_cparams(dimension_semantics=("parallel", "arbitrary")),
    )(proj, proj, proj, proj, proj, proj, proj, bias)


def _attn_bwd(proj, bias, do, name):
    s = proj.shape[0]
    nt = s // ATT_TQ
    win = 3 * ATT_TQ

    def body(q_ref, k0, k1, k2, v0, v1, v2, do_ref, b_ref, dq_ref, dk_ref, dv_ref, db_ref, dk_acc, dv_acc):
        m = pl.program_id(1)

        @pl.when(m == 0)
        def _():
            dk_acc[...] = jnp.zeros_like(dk_acc)
            dv_acc[...] = jnp.zeros_like(dv_acc)
            db_ref[...] = jnp.zeros_like(db_ref)

        @pl.when(m < nt)
        def _():
            k, v = _attn_window(k0, k1, k2, v0, v1, v2)
            q = (q_ref[...] * ATT_SCALE).astype(BF16)
            dout = do_ref[...]
            for hh in range(2):
                cols = slice(hh * ATT_HEAD_DIM, (hh + 1) * ATT_HEAD_DIM)
                qh, kh, vh, doh = q[:, cols], k[:, cols], v[:, cols], dout[:, cols]
                e, total = _attn_exp(qh, kh, b_ref[0, hh])
                p = e / total
                dvh = lax.dot_general(p.astype(BF16), doh, _DIMS["tn"], preferred_element_type=F32)
                dp = lax.dot_general(doh, vh, _DIMS["nt"], preferred_element_type=F32)
                ds = p * (dp - jnp.sum(dp * p, axis=-1, keepdims=True))
                db_ref[hh] += ds
                dsb = ds.astype(BF16)
                dqh = jnp.dot(dsb, kh, preferred_element_type=F32) * ATT_SCALE
                dkh = lax.dot_general(dsb, qh, _DIMS["tn"], preferred_element_type=F32)
                dq_ref[:, cols] = dqh.astype(dq_ref.dtype)
                dk_acc[:, cols] += dkh
                dv_acc[:, cols] += dvh

        dk_ref[...] = dk_acc[:ATT_TQ].astype(dk_ref.dtype)
        dv_ref[...] = dv_acc[:ATT_TQ].astype(dv_ref.dtype)
        for acc in (dk_acc, dv_acc):
            rest = acc[ATT_TQ:]
            acc[: win - ATT_TQ] = rest
            acc[win - ATT_TQ :] = jnp.zeros((ATT_TQ, V7X_LANES), F32)

    blk = (ATT_TQ, V7X_LANES)
    specs = _attn_specs(nt)
    specs.append(pl.BlockSpec(blk, lambda hp, m: (jnp.minimum(m, nt - 1), hp)))
    specs.append(_bias_spec())
    done = lambda hp, m: (jnp.maximum(m - 2, 0), hp)
    out_specs = [
        pl.BlockSpec(blk, lambda hp, m: (jnp.minimum(m, nt - 1), hp)),
        pl.BlockSpec(blk, done),
        pl.BlockSpec(blk, done),
        pl.BlockSpec((2, ATT_TQ, win), lambda hp, m: (hp, 0, 0)),
    ]
    out_shape = [jax.ShapeDtypeStruct((s, ATT_WIDTH), BF16)] * 3
    out_shape.append(jax.ShapeDtypeStruct((ATT_HEADS, ATT_TQ, win), F32))
    return pl.pallas_call(
        body,
        name=name,
        grid=(ATT_HEADS // 2, nt + 2),
        in_specs=specs,
        out_specs=out_specs,
        out_shape=out_shape,
        scratch_shapes=[pltpu.VMEM((win, V7X_LANES), F32), pltpu.VMEM((win, V7X_LANES), F32)],
        compiler_params=_cparams(dimension_semantics=("arbitrary", "arbitrary")),
    )(proj, proj, proj, proj, proj, proj, proj, do, bias)


def _ada_fwd(c_all, w, name):
    def body(c_ref, w_ref, o_ref):
        act = _silu(c_ref[...]).astype(BF16)
        o_ref[...] = jnp.dot(act, w_ref[...].astype(BF16), preferred_element_type=F32)

    return pl.pallas_call(
        body, name=name, out_shape=jax.ShapeDtypeStruct((c_all.shape[0], w.shape[1]), F32), compiler_params=_cparams()
    )(c_all, w)


def _ada_bwd(c_all, dmod, name):
    def body(c_ref, d_ref, o_ref):
        act = _silu(c_ref[...])
        o_ref[...] = lax.dot_general(act, d_ref[...], _DIMS["tn"], preferred_element_type=F32,
                                     precision=lax.Precision.HIGHEST)

    return pl.pallas_call(
        body, name=name, out_shape=jax.ShapeDtypeStruct((c_all.shape[1], dmod.shape[1]), F32), compiler_params=_cparams()
    )(c_all, dmod)


def _adamw_parts(landed, sent, me, w, m, v, name, rows=256):
    r, c = w.shape
    tr = _pick(r, rows, 16)

    def body(me_ref, g_ref, own_ref, w_ref, m_ref, v_ref, go_ref, d_ref, mo_ref, vo_ref):
        mine = me_ref[0]
        grad = jnp.zeros((tr, c), F32)
        for d in range(N_DEV):
            grad = grad + jnp.where(mine == d, own_ref[0], g_ref[d]).astype(F32)
        _adamw_update(grad, w_ref, m_ref, v_ref, go_ref, d_ref, mo_ref, vo_ref)

    spec = pl.BlockSpec((tr, c), lambda i, me_ref: (i, 0))
    return pl.pallas_call(
        body,
        name=name,
        grid_spec=pltpu.PrefetchScalarGridSpec(
            num_scalar_prefetch=1,
            grid=(r // tr,),
            in_specs=[pl.BlockSpec((N_DEV, tr, c), lambda i, me_ref: (0, i, 0)),
                      pl.BlockSpec((1, tr, c), lambda i, me_ref: (me_ref[0], i, 0)), spec, spec, spec],
            out_specs=[spec] * 4,
        ),
        out_shape=[jax.ShapeDtypeStruct((r, c), F32)] * 4,
        compiler_params=_cparams(dimension_semantics=("parallel",)),
    )(me.reshape(1).astype(jnp.int32), landed, sent, w, m, v)


def _adamw_update(grad, w_ref, m_ref, v_ref, go_ref, d_ref, mo_ref, vo_ref):
    m2 = ADAM_B1 * m_ref[...] + (1.0 - ADAM_B1) * grad
    v2 = ADAM_B2 * v_ref[...] + (1.0 - ADAM_B2) * (grad * grad)
    m_hat = m2 / (1.0 - ADAM_B1**ADAM_STEP)
    v_hat = v2 / (1.0 - ADAM_B2**ADAM_STEP)
    go_ref[...] = grad
    d_ref[...] = -ADAM_LR * (m_hat / (jnp.sqrt(v_hat) + ADAM_EPS) + ADAM_WD * w_ref[...])
    mo_ref[...] = m2
    vo_ref[...] = v2


def _adamw(g, w, m, v, name, rows=256):
    r, c = w.shape
    tr = _pick(r, rows, 16)

    def body(g_ref, w_ref, m_ref, v_ref, go_ref, d_ref, mo_ref, vo_ref):
        _adamw_update(g_ref[...], w_ref, m_ref, v_ref, go_ref, d_ref, mo_ref, vo_ref)

    spec = pl.BlockSpec((tr, c), lambda i: (i, 0))
    return pl.pallas_call(
        body,
        name=name,
        grid=(r // tr,),
        in_specs=[spec, spec, spec, spec],
        out_specs=[spec] * 4,
        out_shape=[jax.ShapeDtypeStruct((r, c), F32)] * 4,
        compiler_params=_cparams(dimension_semantics=("parallel",)),
    )(g, w, m, v)


def _sum_parts(parts, name):
    def body(p_ref, o_ref):
        acc = p_ref[0]
        for d in range(1, N_DEV):
            acc = acc + p_ref[d]
        o_ref[...] = acc

    return pl.pallas_call(
        body, name=name, out_shape=jax.ShapeDtypeStruct(parts.shape[1:], F32), compiler_params=_cparams()
    )(parts)


def _place():
    x, y, c = lax.axis_index("x"), lax.axis_index("y"), lax.axis_index("c")
    return x, y, c


def _dev_index(p):
    return 4 * p[0] + 2 * p[1] + p[2]


def _allgather_vmem(shard, name):
    m_per, n = shard.shape

    def body(x_ref, out_ref, send_sems, recv_sems, local_sem):
        x, y, c = _place()
        me, sibling = (x, y, c), (x, y, 1 - c)
        chips = [(1 - x, y), (x, 1 - y), (1 - x, 1 - y)]

        def rows(p):
            return out_ref.at[pl.ds(_dev_index(p) * m_per, m_per), :]

        def copy(k, block, to, src=None):
            return pltpu.make_async_remote_copy(
                src_ref=rows(block) if src is None else src, dst_ref=rows(block),
                send_sem=send_sems.at[k], recv_sem=recv_sems.at[k], device_id=to, device_id_type=MESH)

        mine = pltpu.make_async_copy(x_ref, rows(me), local_sem)
        mine.start()
        first = [copy(0, me, sibling, src=x_ref)]
        first += [copy(1 + j, me, (*chip, c), src=x_ref) for j, chip in enumerate(chips)]
        for cp in first:
            cp.start()
        passed = [copy(4 + j, (*chip, c), sibling) for j, chip in enumerate(chips)]
        for j, chip in enumerate(chips):
            copy(1 + j, (*chip, c), me).wait_recv()
            passed[j].start()
        copy(0, sibling, me).wait_recv()
        for j, chip in enumerate(chips):
            copy(4 + j, (*chip, 1 - c), me).wait_recv()
        for cp in first + passed:
            cp.wait_send()
        mine.wait()

    return pl.pallas_call(
        body,
        name=name,
        out_shape=jax.ShapeDtypeStruct((N_DEV * m_per, n), shard.dtype),
        in_specs=[pl.BlockSpec(memory_space=pltpu.VMEM)],
        out_specs=pl.BlockSpec(memory_space=pltpu.VMEM),
        scratch_shapes=[pltpu.SemaphoreType.DMA((7,)), pltpu.SemaphoreType.DMA((7,)), pltpu.SemaphoreType.DMA],
        compiler_params=_cparams(),
    )(shard)


def _allgather_hbm(shards, name):
    n = len(shards)

    def body(*refs):
        ins, outs = refs[:n], refs[n : 2 * n]
        send_sems, recv_sems, local_sems = refs[2 * n :]
        x, y, c = _place()
        me, sibling = (x, y, c), (x, y, 1 - c)
        chips = [(1 - x, y), (x, 1 - y), (1 - x, 1 - y)]

        def copy(a, k, block, to, src=None):
            dst = outs[a].at[_dev_index(block)]
            return pltpu.make_async_remote_copy(
                src_ref=dst if src is None else src, dst_ref=dst,
                send_sem=send_sems.at[a * 7 + k], recv_sem=recv_sems.at[a * 7 + k], device_id=to, device_id_type=MESH)

        mine = [pltpu.make_async_copy(ins[a], outs[a].at[_dev_index(me)], local_sems.at[a]) for a in range(n)]
        for cp in mine:
            cp.start()
        first = []
        for a in range(n):
            first.append(copy(a, 0, me, sibling, src=ins[a]))
            first += [copy(a, 1 + j, me, (*chip, c), src=ins[a]) for j, chip in enumerate(chips)]
        for cp in first:
            cp.start()
        passed = []
        for j, chip in enumerate(chips):
            for a in range(n):
                copy(a, 1 + j, (*chip, c), me).wait_recv()
                cp = copy(a, 4 + j, (*chip, c), sibling)
                cp.start()
                passed.append(cp)
        for a in range(n):
            copy(a, 0, sibling, me).wait_recv()
        for j, chip in enumerate(chips):
            for a in range(n):
                copy(a, 4 + j, (*chip, 1 - c), me).wait_recv()
        for cp in first + passed:
            cp.wait_send()
        for cp in mine:
            cp.wait()

    any_spec = pl.BlockSpec(memory_space=pl.ANY)
    return pl.pallas_call(
        body,
        name=name,
        out_shape=[jax.ShapeDtypeStruct((N_DEV, *s.shape), s.dtype) for s in shards],
        in_specs=[any_spec] * n,
        out_specs=[any_spec] * n,
        scratch_shapes=[pltpu.SemaphoreType.DMA((7 * n,)), pltpu.SemaphoreType.DMA((7 * n,)),
                        pltpu.SemaphoreType.DMA((n,))],
        compiler_params=_cparams(),
    )(*shards)


def _exchange_hbm(bufs, name):
    n = len(bufs)

    def body(*refs):
        ins, outs = refs[:n], refs[n : 2 * n]
        send_sems, recv_sems, local_sems = refs[2 * n :]
        x, y, c = _place()
        me = _dev_index((x, y, c))
        mine = [pltpu.make_async_copy(ins[a].at[me], outs[a].at[me], local_sems.at[a]) for a in range(n)]
        for cp in mine:
            cp.start()
        def peer_of(k):
            return (1 - x if k & 4 else x, 1 - y if k & 2 else y, 1 - c if k & 1 else c)

        copies = []
        for k in range(1, N_DEV):
            peer = peer_of(k)
            for a in range(n):
                copies.append(pltpu.make_async_remote_copy(
                    src_ref=ins[a].at[_dev_index(peer)], dst_ref=outs[a].at[me],
                    send_sem=send_sems.at[a * 7 + k - 1], recv_sem=recv_sems.at[a * 7 + k - 1],
                    device_id=peer, device_id_type=MESH))
        for cp in copies:
            cp.start()
        for k in range(1, N_DEV):
            peer = peer_of(k)
            for a in range(n):
                pltpu.make_async_remote_copy(
                    src_ref=ins[a].at[me], dst_ref=outs[a].at[_dev_index(peer)],
                    send_sem=send_sems.at[a * 7 + k - 1], recv_sem=recv_sems.at[a * 7 + k - 1],
                    device_id=peer, device_id_type=MESH).wait_recv()
        for cp in copies:
            cp.wait_send()
        for cp in mine:
            cp.wait()

    any_spec = pl.BlockSpec(memory_space=pl.ANY)
    return pl.pallas_call(
        body,
        name=name,
        out_shape=[jax.ShapeDtypeStruct(b.shape, b.dtype) for b in bufs],
        in_specs=[any_spec] * n,
        out_specs=[any_spec] * n,
        scratch_shapes=[pltpu.SemaphoreType.DMA((7 * n,)), pltpu.SemaphoreType.DMA((7 * n,)),
                        pltpu.SemaphoreType.DMA((n,))],
        compiler_params=_cparams(),
    )(*bufs)


HBM_SPEC = pl.BlockSpec(memory_space=pltpu.HBM)
SEM_SPEC = pl.BlockSpec(memory_space=pltpu.SEMAPHORE)
EFFECT = pltpu.SideEffectType.DATAFLOW_SIDE_EFFECTING


def _peers(x, y, c):
    return [(1 - x if k & 4 else x, 1 - y if k & 2 else y, 1 - c if k & 1 else c) for k in range(1, N_DEV)]


def _push_start(groups, sliced, name, after=()):
    flat = [b for g in groups for b in g]
    n, ng = len(flat), len(groups)
    sizes = [len(g) for g in groups]
    lands = [lax.empty(b.shape if sliced else (N_DEV, *b.shape), b.dtype) for b in flat]

    def body(*refs):
        ins, lnd = refs[:n], refs[n : 2 * n]
        sems = refs[2 * n + len(after) : 2 * n + len(after) + 2 * ng]
        token = refs[-1]
        x, y, c = _place()
        me = _dev_index((x, y, c))
        first = 0
        for gi, size in enumerate(sizes):
            for k, peer in enumerate(_peers(x, y, c)):
                for j in range(first, first + size):
                    sem = (j - first) * 7 + k
                    pltpu.make_async_remote_copy(
                        src_ref=ins[j].at[_dev_index(peer)] if sliced else ins[j], dst_ref=lnd[j].at[me],
                        send_sem=sems[2 * gi].at[sem], recv_sem=sems[2 * gi + 1].at[sem],
                        device_id=peer, device_id_type=MESH).start()
            first += size
        token[...] = jnp.zeros_like(token)

    out_shape = []
    for size in sizes:
        out_shape += [pltpu.SemaphoreType.DMA((7 * size,)), pltpu.SemaphoreType.DMA((7 * size,))]
    out_shape += [pltpu.HBM(b.shape, b.dtype) for b in flat + lands]
    out_shape.append(jax.ShapeDtypeStruct((V7X_SUBLANES, V7X_LANES), F32))
    res = pl.pallas_call(
        body,
        name=name,
        out_shape=tuple(out_shape),
        in_specs=[HBM_SPEC] * (2 * n) + [ANY_SPEC] * len(after),
        out_specs=tuple([SEM_SPEC] * (2 * ng) + [HBM_SPEC] * (2 * n) + [pl.BlockSpec(memory_space=pltpu.VMEM)]),
        input_output_aliases={i: 2 * ng + i for i in range(2 * n)},
        compiler_params=pltpu.CompilerParams(has_side_effects=EFFECT),
    )(*[pltpu.with_memory_space_constraint(b, pltpu.HBM) for b in flat + lands], *after)
    sems, thru, token = res[: 2 * ng], res[2 * ng : 2 * ng + 2 * n], res[-1]
    out, first = [], 0
    for gi, size in enumerate(sizes):
        out.append((sems[2 * gi], sems[2 * gi + 1], list(thru[first : first + size]),
                    list(thru[n + first : n + first + size])))
        first += size
    return out, token


def _push_wait(started, sliced, after, name):
    send_sems, recv_sems, bufs, lands = started
    n = len(bufs)

    def body(*refs):
        ins, lnd = refs[:n], refs[n : 2 * n]
        send_ref, recv_ref = refs[2 * n], refs[2 * n + 1]
        x, y, c = _place()
        for k, peer in enumerate(_peers(x, y, c)):
            for j in range(n):
                cp = pltpu.make_async_remote_copy(
                    src_ref=ins[j].at[_dev_index(peer)] if sliced else ins[j], dst_ref=lnd[j].at[_dev_index(peer)],
                    send_sem=send_ref.at[j * 7 + k], recv_sem=recv_ref.at[j * 7 + k],
                    device_id=peer, device_id_type=MESH)
                cp.wait_send()
                cp.wait_recv()

    res = pl.pallas_call(
        body,
        name=name,
        out_shape=tuple(pltpu.HBM(b.shape, b.dtype) for b in bufs + lands),
        in_specs=[HBM_SPEC] * (2 * n) + [SEM_SPEC, SEM_SPEC, pl.BlockSpec(memory_space=pl.ANY)],
        out_specs=tuple([HBM_SPEC] * (2 * n)),
        input_output_aliases={i: i for i in range(2 * n)},
        compiler_params=pltpu.CompilerParams(has_side_effects=EFFECT),
    )(*bufs, *lands, send_sems, recv_sems, after)
    return list(res[:n]), list(res[n:])


def _with_own_row(land, own, me):
    return lax.dynamic_update_index_in_dim(land, own, me, 0)


def _cols_full(g):
    return jnp.transpose(g, (1, 0, 2)).reshape(g.shape[1], -1)


def _rows_full(g):
    return g.reshape(-1, g.shape[2])


def _cols_parts(full, n=N_DEV):
    r = full.shape[0]
    return jnp.transpose(full.reshape(r, n, -1), (1, 0, 2)).astype(BF16)


def _rows_parts(full):
    return full.reshape(N_DEV, -1, full.shape[1]).astype(BF16)


def _block_diag(w):
    eye = jnp.eye(LRU_BLOCKS, dtype=w.dtype)
    return jnp.einsum("nkj,nm->nkmj", w, eye).reshape(LRU_WIDTH, LRU_WIDTH)


def _pad_rows(v, rows):
    flat = v.reshape(-1)
    return jnp.pad(flat, (0, rows * D_MODEL - flat.shape[0])).reshape(rows, D_MODEL)


def _my_cols(full, me, width):
    return lax.dynamic_slice_in_dim(full, me * width, width, axis=full.ndim - 1)


def kernel(x, c, w_ada, b_ada, norm_pre, norm_post, ffn1_w_gu, ffn1_w_down, w_in, rel_bias, conv_w, conv_b, lru_wa, lru_ba, lru_wx, lru_bx, lru_lambda, w_att_o, w_rec_o, w_out, ffn2_w_gu, ffn2_w_down, loss_target, m_w_ada, m_b_ada, m_norm_pre, m_norm_post, m_ffn1_w_gu, m_ffn1_w_down, m_w_in, m_rel_bias, m_conv_w, m_conv_b, m_lru_wa, m_lru_ba, m_lru_wx, m_lru_bx, m_lru_lambda, m_w_att_o, m_w_rec_o, m_w_out, m_ffn2_w_gu, m_ffn2_w_down, v_w_ada, v_b_ada, v_norm_pre, v_norm_post, v_ffn1_w_gu, v_ffn1_w_down, v_w_in, v_rel_bias, v_conv_w, v_conv_b, v_lru_wa, v_lru_ba, v_lru_wx, v_lru_bx, v_lru_lambda, v_w_att_o, v_w_rec_o, v_w_out, v_ffn2_w_gu, v_ffn2_w_down):
    weights = dict(w_ada=w_ada, b_ada=b_ada, norm_pre=norm_pre, norm_post=norm_post, ffn1_w_gu=ffn1_w_gu,
                   ffn1_w_down=ffn1_w_down, w_in=w_in, rel_bias=rel_bias, conv_w=conv_w, conv_b=conv_b,
                   lru_wa=lru_wa, lru_ba=lru_ba, lru_wx=lru_wx, lru_bx=lru_bx, lru_lambda=lru_lambda,
                   w_att_o=w_att_o, w_rec_o=w_rec_o, w_out=w_out, ffn2_w_gu=ffn2_w_gu, ffn2_w_down=ffn2_w_down)
    mom1 = dict(w_ada=m_w_ada, b_ada=m_b_ada, norm_pre=m_norm_pre, norm_post=m_norm_post, ffn1_w_gu=m_ffn1_w_gu,
                ffn1_w_down=m_ffn1_w_down, w_in=m_w_in, rel_bias=m_rel_bias, conv_w=m_conv_w, conv_b=m_conv_b,
                lru_wa=m_lru_wa, lru_ba=m_lru_ba, lru_wx=m_lru_wx, lru_bx=m_lru_bx, lru_lambda=m_lru_lambda,
                w_att_o=m_w_att_o, w_rec_o=m_w_rec_o, w_out=m_w_out, ffn2_w_gu=m_ffn2_w_gu, ffn2_w_down=m_ffn2_w_down)
    mom2 = dict(w_ada=v_w_ada, b_ada=v_b_ada, norm_pre=v_norm_pre, norm_post=v_norm_post, ffn1_w_gu=v_ffn1_w_gu,
                ffn1_w_down=v_ffn1_w_down, w_in=v_w_in, rel_bias=v_rel_bias, conv_w=v_conv_w, conv_b=v_conv_b,
                lru_wa=v_lru_wa, lru_ba=v_lru_ba, lru_wx=v_lru_wx, lru_bx=v_lru_bx, lru_lambda=v_lru_lambda,
                w_att_o=v_w_att_o, w_rec_o=v_w_rec_o, w_out=v_w_out, ffn2_w_gu=v_ffn2_w_gu, ffn2_w_down=v_ffn2_w_down)
    order = list(weights)
    big = ["ffn1_w_gu", "ffn1_w_down", "w_in", "w_att_o", "w_rec_o", "w_out", "ffn2_w_gu", "ffn2_w_down"]
    col_sharded = {"ffn1_w_gu", "w_in", "w_att_o", "ffn2_w_gu"}
    small = ["b_ada", "norm_pre", "norm_post", "rel_bias", "conv_w", "conv_b", "lru_wa", "lru_ba", "lru_wx",
             "lru_bx", "lru_lambda"]

    xi, yi, ci = _place()
    me = _dev_index((xi, yi, ci))
    x0 = x[0]
    target = loss_target[0]
    fuse_tm = min(FUSE_TM, x0.shape[0])

    transposed = {"ffn1_w_gu", "w_in", "ffn2_w_gu"}
    local = lambda n, arr: jnp.transpose(arr[0]) if n in transposed else arr[0]
    shards = {n: local(n, weights[n]).astype(BF16) for n in big}
    full_of = lambda n, g: _cols_full(g) if n == "w_att_o" else _rows_full(g)

    pack = jnp.concatenate([c.reshape(-1), norm_pre.reshape(-1), norm_post.reshape(-1), conv_w.reshape(-1)])
    pack = jnp.pad(pack, (0, 3072 - pack.shape[0])).reshape(8, 384)
    got = _allgather_vmem(pack, "gather_small_inputs").reshape(N_DEV, 3072)
    c_all = got[:, :1024]
    unshard = lambda blk, rows: jnp.transpose(blk.reshape(N_DEV, rows, 128), (1, 0, 2)).reshape(rows, D_MODEL)
    g_pre = unshard(got[:, 1024:1408], 3)
    g_post = unshard(got[:, 1408:1792], 3)
    conv_taps = unshard(got[:, 1792:2304], 4)
    conv_w8 = jnp.concatenate([conv_taps, jnp.zeros((4, LRU_WIDTH), F32)], axis=0)

    mod_cols = _ada_fwd(c_all, w_ada[0], "ada_fwd")
    mod_all = _allgather_vmem(mod_cols, "gather_mod").reshape(N_DEV, N_DEV, 1152)
    mod = lax.dynamic_index_in_dim(mod_all, me, axis=1, keepdims=False).reshape(1, -1) + b_ada
    mod = mod.reshape(3, 3, 1, D_MODEL)

    w_slab = _slab_weights(lru_wa[0], lru_wx[0])
    bias = _bias_tile(rel_bias[0])

    res_w = (0.5, 1.0, 0.5)
    row = lambda v: v.reshape(1, -1)

    (w1_gu,) = _allgather_hbm([shards["ffn1_w_gu"]], "gather_ffn1_w_gu")
    weight_groups = [["ffn1_w_down"], ["w_in"], ["w_att_o", "w_rec_o", "w_out"], ["ffn2_w_gu", "ffn2_w_down"]]
    weights_started, started = _push_start([[shards[n] for n in g] for g in weight_groups], False,
                                           "gather_weights_start", after=(mod, w1_gu))
    full = {"ffn1_w_gu": _rows_full(w1_gu)}

    def gathered_group(gi, after):
        sent, lands = _push_wait(weights_started[gi], False, after, f"gather_weights_wait{gi}")
        for n, own, land in zip(weight_groups[gi], sent, lands):
            full[n] = full_of(n, jnp.where(is_me, own[None], land))

    is_me = (jnp.arange(N_DEV) == me)[:, None, None]

    def ffn_fwd(xin, k, gi, tag, deps=(), target=None):
        h, a, g, u = _pre_up(xin, row(g_pre[k]), mod[k, 0], mod[k, 1], full[f"{tag}_w_gu"], f"{tag}_up", deps=deps)
        if f"{tag}_w_down" not in full:
            gathered_group(gi, a)
        f, *out = _matmul_post(a, full[f"{tag}_w_down"], xin, row(g_post[k]), mod[k, 2], res_w[k], f"{tag}_down",
                               target=target)
        return (out[0] if target is None else out), (h, g, u, a, f)

    x1, saved1 = ffn_fwd(x0, 0, 0, "ffn1", deps=(started,))

    gathered_group(1, x1)
    h2, proj = _pre_matmul(x1, row(g_pre[1]), mod[1, 0], mod[1, 1], full["w_in"], "mix_in",
                           b_shift=3 * ATT_WIDTH // 512)
    att_o = _attn_fwd(proj, bias, "attn_fwd")
    gathered_group(2, att_o)
    xc, pre, a_t, u_t = _lru_front(proj, conv_w8, conv_b, w_slab, lru_ba, lru_bx, lru_lambda, "lru_front")
    hs, h_prev, rec_in = _scan_fwd(a_t, u_t, proj, "lru_scan")
    att = _matmul(att_o, full["w_att_o"], "nn", F32, "att_out")
    rec = _matmul(rec_in, full["w_rec_o"], "nn", F32, "rec_out")
    merged, f2, x2 = _merge_matmul_post(att, rec, proj, full["w_out"], x1, row(g_post[1]), mod[1, 2], res_w[1],
                                        "mix_out")

    gathered_group(3, x2)
    (dy, sq), saved3 = ffn_fwd(x2, 2, 2, "ffn2", target=target)
    loss = lax.psum(0.5 * jnp.sum(sq) / D_MODEL, ("x", "y", "c"))

    grads = {}
    norm_sums = [None] * 6

    pending = []

    def exchange_start(names, tag, after=()):
        send = [(_cols_parts if n == "w_att_o" else _rows_parts)(grads[n]) for n in names]
        (group,), token = _push_start([send], True, f"exchange_{tag}_start", after=after)
        pending.append((names, send, group, tag))
        return token

    def exchange_finish(names, send, group, tag, after):
        sent, lands = _push_wait(group, True, after, f"exchange_{tag}_wait")
        res = None
        for n, land, mine in zip(names, lands, sent):
            res = _adamw_parts(land, mine, me, local(n, weights[n]), local(n, mom1[n]), local(n, mom2[n]),
                               f"adamw_{n}")
            back = (lambda r: jnp.transpose(r)) if n in transposed else (lambda r: r)
            out_g[n], out_d[n], out_m[n], out_v[n] = [back(r).reshape(weights[n].shape) for r in res]
        return res[0]

    out_g, out_d, out_m, out_v = {}, {}, {}, {}

    def ffn_bwd(xin, k, saved, dout, tag):
        h, g, u, a, f = saved
        w_gu, w_down = f"{tag}_w_gu", f"{tag}_w_down"
        df, dgu, norm_sums[2 * k + 1] = _post_bwd_up_bwd(f, dout, row(g_post[k]), mod[k, 2], res_w[k], full[w_down],
                                                          g, u, f"{tag}_up_bwd")
        grads[w_down] = _matmul(a, df, "tn", BF16, f"{tag}_dw_down", tm=1408, tn=1024, tk=1024)
        started = exchange_start([w_down], w_down)
        grads[w_gu] = _dw_gu(dgu, h, f"{tag}_dw_gu", deps=(started,))
        started = exchange_start([w_gu], w_gu)
        halves = [(dgu, (None, fuse_tm, D_FF), lambda i, half=half: (half, i, 0), (half * D_FF, (half + 1) * D_FF))
                  for half in range(2)]
        dx, norm_sums[2 * k] = _matmul_pre_bwd(halves, full[w_gu], xin, dout, row(g_pre[k]),
                                                                mod[k, 0], mod[k, 1], f"{tag}_dh", deps=(started,))
        return dx

    dx2 = ffn_bwd(x2, 2, saved3, dy, "ffn2")

    df2, datt, drec, dg_att, dg_rec, norm_sums[3] = _post_bwd_merge_bwd(
        f2, dx2, row(g_post[1]), mod[1, 2], res_w[1], full["w_out"], att, rec, proj, "mix_dmerged")
    grads["w_out"] = _matmul(merged, df2, "tn", BF16, "mix_dw_out", tm=1024, tn=1024, tk=1024)
    datt_o = _matmul(datt, full["w_att_o"], "nt", BF16, "att_out_bwd")
    grads["w_att_o"] = _matmul(att_o, datt, "tn", BF16, "dw_att_o", tm=512, tn=1024, tk=1024)
    grads["w_rec_o"] = _matmul(rec_in, drec, "tn", BF16, "dw_rec_o", tm=1024, tn=1024, tk=1024)
    started = exchange_start(["w_out", "w_att_o", "w_rec_o"], "mix_out")
    dhs, dyr = _matmul_recin_bwd(drec, full["w_rec_o"], hs, proj, "rec_out_bwd", deps=(started,))
    g_t = _scan_bwd(a_t, dhs, "lru_scan_bwd")
    dpre, dxc, lru_sums = _lru_back(pre, xc, w_slab, lru_ba, lru_bx, lru_lambda, g_t, h_prev, "lru_back")
    dxr, conv_sums = _conv_bwd(proj, conv_w8, dxc, "conv_bwd")
    dq, dk, dv, dbias = _attn_bwd(proj, bias, datt_o, "attn_bwd")
    dproj = jnp.concatenate([dq, dk, dv, dxr, dyr, dg_att, dg_rec], axis=1)
    grads["w_in"] = _matmul(dproj, h2, "tn", BF16, "mix_dw_in", tm=1408, tn=1024, tk=1024)
    pack_mix = jnp.concatenate([conv_sums, lru_sums, _pad_rows(_bias_grad(dbias), V7X_SUBLANES),
                                _lru_dw(xc, dpre, "lru_dw").reshape(128, D_MODEL)], axis=0)
    (mix_started,), started = _push_start([[pack_mix]], False, "small_grads_mix_start")
    started = exchange_start(["w_in"], "w_in", after=(started,))
    whole = [(dproj, (fuse_tm, PROJ_WIDTH), lambda i: (i, 0), (0, PROJ_WIDTH))]
    dx1, norm_sums[2] = _matmul_pre_bwd(whole, full["w_in"], x1, dx2, row(g_pre[1]), mod[1, 0],
                                                             mod[1, 1], "mix_dh", deps=(started,))

    dx0 = ffn_bwd(x0, 0, saved1, dx1, "ffn1")

    pack_norm = jnp.concatenate(norm_sums, axis=0)
    (norm_started,), _ = _push_start([[pack_norm]], False, "small_grads_norm_start")

    def summed(started, pack, after, tag):
        (own,), (land,) = _push_wait(started, False, after, f"small_grads_{tag}_wait")
        parts = jnp.where(is_me, own[None], land)
        return parts, _sum_parts(parts, f"small_grads_{tag}_sum")

    done = dx0
    last = [p for p in pending if p[3].startswith("ffn1")]
    for names, send, group, tag in pending:
        if not tag.startswith("ffn1"):
            done = exchange_finish(names, send, group, tag, done)

    _, total = summed(mix_started, pack_mix, done, "mix")
    grads["conv_w"] = _my_cols(total[0:4], me, 128)
    grads["conv_b"] = total[4:5]
    grads["lru_ba"] = total[8:9]
    grads["lru_bx"] = total[9:10]
    grads["lru_lambda"] = total[10:11]
    grads["rel_bias"] = total[16:19].reshape(-1)[: ATT_HEADS * (2 * MAX_REL + 1)].reshape(ATT_HEADS, -1)
    grads["lru_wa"] = total[24:88].reshape(LRU_BLOCKS, LRU_BLOCK, LRU_BLOCK)
    grads["lru_wx"] = total[88:152].reshape(LRU_BLOCKS, LRU_BLOCK, LRU_BLOCK)
    parts, total = summed(norm_started, pack_norm, total, "norm")
    by_sandwich = lambda v: v.reshape(*v.shape[:-2], 3, 2 * V7X_SUBLANES, D_MODEL)
    dmod_of = lambda v: jnp.concatenate([by_sandwich(v)[..., 1:3, :], by_sandwich(v)[..., 9:10, :]], axis=-2)
    grads["b_ada"] = dmod_of(total).reshape(1, -1)
    grads["norm_pre"] = _my_cols(by_sandwich(total)[:, 0, :], me, 128)
    grads["norm_post"] = _my_cols(by_sandwich(total)[:, V7X_SUBLANES, :], me, 128)
    dmod_all = dmod_of(parts).reshape(N_DEV, 9 * D_MODEL)
    grads["w_ada"] = _ada_bwd(c_all, _my_cols(dmod_all, me, 1152), "ada_bwd")

    res = _adamw(grads["w_ada"], w_ada[0], m_w_ada[0], v_w_ada[0], "adamw_w_ada")
    out_g["w_ada"], out_d["w_ada"], out_m["w_ada"], out_v["w_ada"] = [r.reshape(w_ada.shape) for r in res]

    sizes = [int(np.prod(weights[n].shape)) for n in small]
    tot = sum(sizes)
    rows_small = -(-tot // (16 * D_MODEL)) * 16
    flat = lambda arrs: jnp.pad(jnp.concatenate([a.reshape(-1) for a in arrs]),
                                (0, rows_small * D_MODEL - tot)).reshape(rows_small, D_MODEL)
    res = _adamw(flat([grads[n] for n in small]), flat([weights[n] for n in small]),
                 flat([mom1[n] for n in small]), flat([mom2[n] for n in small]), "adamw_small", rows=rows_small)
    offs = np.cumsum([0] + sizes)
    for dst, r in zip((out_g, out_d, out_m, out_v), res):
        rf = r.reshape(-1)
        for i, n in enumerate(small):
            dst[n] = rf[offs[i] : offs[i + 1]].reshape(weights[n].shape)

    done = res[0]
    for names, send, group, tag in last:
        done = exchange_finish(names, send, group, tag, done)

    return (loss, dx0[None], *[out_g[n] for n in order], *[out_d[n] for n in order],
            *[out_m[n] for n in order], *[out_v[n] for n in order])
```

```python
import functools

import jax
import jax.numpy as jnp
import numpy as np
from jax import lax
from jax.experimental import pallas as pl
from jax.experimental.pallas import tpu as pltpu

D_MODEL = 1024
D_FF = 2816
ATT_HEADS = 8
ATT_HEAD_DIM = 64
ATT_WIDTH = 512
CHUNK = 64
LEFT_CHUNKS = 8
MAX_REL = 128
LRU_WIDTH = 1024
LRU_BLOCKS = 16
LRU_BLOCK = 64
LRU_C = 8.0
EPS = 1e-6
PROJ_WIDTH = 5632
N_DEV = 8

ADAM_LR = 0.001
ADAM_B1 = 0.9
ADAM_B2 = 0.999
ADAM_EPS = 1e-08
ADAM_WD = 0.01
ADAM_STEP = 10

V7X_LANES = 128
V7X_SUBLANES = 8
V7X_VMEM_BYTES = 64 * 1024 * 1024
VMEM_LIMIT = V7X_VMEM_BYTES - 8 * 1024 * 1024

ATT_TQ = 256
NEG = -1e30
BF16 = jnp.bfloat16
F32 = jnp.float32
MESH = pl.DeviceIdType.MESH

OFF_Q = 4 * LRU_WIDTH
OFF_K = OFF_Q + ATT_WIDTH
OFF_V = OFF_K + ATT_WIDTH


def _cparams(**kw):
    return pltpu.CompilerParams(vmem_limit_bytes=VMEM_LIMIT, **kw)


def _pick(n, target, unit=V7X_LANES):
    best = None
    for t in range(unit, min(n, target) + 1, unit):
        if n % t == 0:
            best = t
    return n if best is None else best


_DIMS = {
    "nn": (((1,), (0,)), ((), ())),
    "nt": (((1,), (1,)), ((), ())),
    "tn": (((0,), (0,)), ((), ())),
}


ANY_SPEC = pl.BlockSpec(memory_space=pl.ANY)


def _matmul(a, b, mode, out_dtype, name, tm=1024, tn=512, tk=1408, deps=(), b_shift=0):
    n_deps = len(deps)
    if mode == "nn":
        (m, k), (k2, n) = a.shape, b.shape
    elif mode == "nt":
        (m, k), (n, k2) = a.shape, b.shape
    else:
        (k, m), (k2, n) = a.shape, b.shape
    assert k == k2, (a.shape, b.shape, mode)
    tm, tn, tk = _pick(m, tm), _pick(n, tn), _pick(k, tk)
    nk = k // tk
    dims = _DIMS[mode]

    def body(a_ref, b_ref, *rest):
        o_ref, scratch = rest[n_deps], rest[n_deps + 1 :]
        p = lax.dot_general(a_ref[...], b_ref[...], dims, preferred_element_type=F32)
        if nk == 1:
            o_ref[...] = p.astype(o_ref.dtype)
        else:
            acc = scratch[0]
            kk = pl.program_id(2)

            @pl.when(kk == 0)
            def _():
                acc[...] = p

            @pl.when(kk > 0)
            def _():
                acc[...] += p

            @pl.when(kk == nk - 1)
            def _():
                o_ref[...] = acc[...].astype(o_ref.dtype)

    if mode == "nn":
        a_spec = pl.BlockSpec((tm, tk), lambda i, j, kk: (i, kk))
        b_spec = pl.BlockSpec((tk, tn), lambda i, j, kk: (kk, j))
    elif mode == "nt":
        a_spec = pl.BlockSpec((tm, tk), lambda i, j, kk: (i, kk))
        b_spec = pl.BlockSpec((tn, tk), lambda i, j, kk: ((j + b_shift) % (n // tn), kk))
    else:
        a_spec = pl.BlockSpec((tk, tm), lambda i, j, kk: (kk, i))
        b_spec = pl.BlockSpec((tk, tn), lambda i, j, kk: (kk, j))
    return pl.pallas_call(
        body,
        name=name,
        grid=(m // tm, n // tn, nk),
        in_specs=[a_spec, b_spec] + [ANY_SPEC] * n_deps,
        out_specs=pl.BlockSpec((tm, tn), lambda i, j, kk: (i, j)),
        out_shape=jax.ShapeDtypeStruct((m, n), out_dtype),
        scratch_shapes=[pltpu.VMEM((tm, tn), F32)] if nk > 1 else [],
        compiler_params=_cparams(dimension_semantics=("parallel", "parallel", "arbitrary")),
    )(a, b, *deps)


def _rowwise(fn, name, params, tiles, outs, accs=(), ts=256, with_index=False, deps=()):
    norm = []
    for t in tiles:
        if not isinstance(t, tuple):
            t = (t, t.shape[1], 0)
        norm.append(t if len(t) == 4 else (*t, None))
    s = norm[0][0].shape[0]
    ts = min(ts, s)
    assert s % ts == 0 and ts % V7X_SUBLANES == 0
    steps = s // ts
    halo_blocks = ts // V7X_SUBLANES
    n_p, n_t, n_o = len(params), len(norm), len(outs)

    def body(*refs):
        i = pl.program_id(0)
        vals = [r[...] for r in refs[: n_p + n_t]]
        res = fn(i, steps, *vals) if with_index else fn(*vals)
        if not isinstance(res, (tuple, list)):
            res = (res,)
        first_out = n_p + n_t + len(deps)
        o_refs = refs[first_out : first_out + n_o]
        a_refs = refs[first_out + n_o :]
        for r, v in zip(o_refs, res[:n_o]):
            r[...] = v.astype(r.dtype)
        for r, v in zip(a_refs, res[n_o:]):
            _accumulate(r, v, i)

    in_specs = [pl.BlockSpec(p.shape, lambda i: (0, 0)) for p in params]
    for arr, w, cb, halo in norm:
        if halo is None:
            in_specs.append(pl.BlockSpec((ts, w), lambda i, cb=cb: (i, cb)))
        elif halo == "prev":
            in_specs.append(
                pl.BlockSpec((V7X_SUBLANES, w), lambda i, cb=cb: (jnp.maximum(i * halo_blocks - 1, 0), cb))
            )
        else:
            last = s // V7X_SUBLANES - 1
            in_specs.append(
                pl.BlockSpec((V7X_SUBLANES, w), lambda i, cb=cb: (jnp.minimum((i + 1) * halo_blocks, last), cb))
            )
    in_specs += [ANY_SPEC] * len(deps)
    out_specs = [pl.BlockSpec((ts, w), lambda i: (i, 0)) for w, _ in outs]
    out_specs += [pl.BlockSpec(shape, lambda i: (0, 0)) for shape in accs]
    out_shape = [jax.ShapeDtypeStruct((s, w), dt) for w, dt in outs]
    out_shape += [jax.ShapeDtypeStruct(shape, F32) for shape in accs]
    res = pl.pallas_call(
        body,
        name=name,
        grid=(steps,),
        in_specs=in_specs,
        out_specs=out_specs,
        out_shape=out_shape,
        compiler_params=_cparams(dimension_semantics=("arbitrary",)),
    )(*params, *[t[0] for t in norm], *deps)
    return res


def _accumulate(ref, val, step):
    @pl.when(step == 0)
    def _():
        ref[...] = val

    @pl.when(step > 0)
    def _():
        ref[...] += val


def _sigmoid(z):
    return jax.nn.sigmoid(z)


def _silu(z):
    return z * _sigmoid(z)


def _gelu(z):
    return 0.5 * z * (1.0 + jnp.tanh(0.7978845608028654 * (z + 0.044715 * (z * z * z))))


def _pre_fn(g, shift, scale, x):
    r = lax.rsqrt(jnp.mean(x * x, axis=-1, keepdims=True) + EPS)
    return ((x * r) * g) * (1.0 + scale) + shift


def _post_fn(res_w, g, gate, f, x):
    r = lax.rsqrt(jnp.mean(f * f, axis=-1, keepdims=True) + EPS)
    return x + (res_w * gate) * ((f * r) * g)


def _swiglu_fn(gu):
    return _silu(gu[:, :D_FF]) * gu[:, D_FF:]


def _gates_fn(ba, bx, lam, pre, xc):
    ra = _sigmoid(pre[:, :LRU_WIDTH] + ba)
    ia = _sigmoid(pre[:, LRU_WIDTH:] + bx)
    softplus = jnp.maximum(-lam, 0.0) + jnp.log1p(jnp.exp(-jnp.abs(lam)))
    log_a = (-LRU_C) * ra * softplus
    a = jnp.exp(log_a)
    mult = jnp.sqrt(-jnp.tanh(log_a) * (a * a + 1.0))
    return a, mult * (ia * xc)


def _recin_fn(hs, yr):
    return hs * _gelu(yr)


def _merge_fn(att, rec, g_att, g_rec):
    return _sigmoid(g_att) * att + _sigmoid(g_rec) * rec


def _rowsum(v):
    return jnp.sum(v, axis=0, keepdims=True)


def _pre_fwd(x, g, shift, scale, name, deps=()):
    (h,) = _rowwise(_pre_fn, name, [g, shift, scale], [x], [(D_MODEL, BF16)], deps=deps)
    return h


def _pre_bwd(x, g, shift, scale, dh, dres, name):
    def fn(g, shift, scale, x, dh, dres):
        _, vjp = jax.vjp(_pre_fn, g, shift, scale, x)
        dg, dshift, dscale, dx = vjp(dh)
        return dx + dres, dg, dshift, dscale

    row = (1, D_MODEL)
    return _rowwise(fn, name, [g, shift, scale], [x, dh, dres], [(D_MODEL, F32)], [row, row, row])


def _post_fwd(f, x, g, gate, res_w, name):
    (y,) = _rowwise(functools.partial(_post_fn, res_w), name, [g, gate], [f, x], [(D_MODEL, F32)])
    return y


def _post_bwd(f, g, gate, res_w, dy, name, deps=()):
    def fn(g, gate, f, dy):
        _, vjp = jax.vjp(lambda g, gate, f: _post_fn(res_w, g, gate, f, 0.0), g, gate, f)
        dg, dgate, df = vjp(dy)
        return df, dg, dgate

    row = (1, D_MODEL)
    return _rowwise(fn, name, [g, gate], [f, dy], [(D_MODEL, BF16)], [row, row], deps=deps)


def _loss_stage(y, target, name):
    def fn(y, t):
        diff = y - t
        return diff * (1.0 / D_MODEL), _rowsum(diff * diff)

    return _rowwise(fn, name, [], [y, target], [(D_MODEL, F32)], [(1, D_MODEL)])


FFN_TM = 512
FFN_TF = 1408


def _glu_fn(g, u):
    return _silu(g) * u


def _ffn_up(h, w_gu_t, name):
    s = h.shape[0]
    tm = min(FFN_TM, s)
    nf = D_FF // FFN_TF

    def body(h_ref, wg_ref, wu_ref, a_ref, g_ref, u_ref):
        hv = h_ref[...]
        g = lax.dot_general(hv, wg_ref[...], _DIMS["nt"], preferred_element_type=F32)
        u = lax.dot_general(hv, wu_ref[...], _DIMS["nt"], preferred_element_type=F32)
        a_ref[...] = _glu_fn(g, u).astype(a_ref.dtype)
        g_ref[...] = g.astype(g_ref.dtype)
        u_ref[...] = u.astype(u_ref.dtype)

    out = pl.BlockSpec((tm, FFN_TF), lambda i, j: (i, j))
    return pl.pallas_call(
        body,
        name=name,
        grid=(s // tm, nf),
        in_specs=[pl.BlockSpec((tm, D_MODEL), lambda i, j: (i, 0)),
                  pl.BlockSpec((FFN_TF, D_MODEL), lambda i, j: (j, 0)),
                  pl.BlockSpec((FFN_TF, D_MODEL), lambda i, j: (nf + j, 0))],
        out_specs=[out, out, out],
        out_shape=[jax.ShapeDtypeStruct((s, D_FF), BF16)] * 3,
        compiler_params=_cparams(dimension_semantics=("parallel", "arbitrary")),
    )(h, w_gu_t, w_gu_t)


def _ffn_up_bwd(df, w_down, g, u, name, deps=()):
    s = df.shape[0]
    tm = min(FFN_TM, s)

    def body(df_ref, wd_ref, g_ref, u_ref, *rest):
        dg_ref, du_ref = rest[len(deps) :]
        da = lax.dot_general(df_ref[...], wd_ref[...], _DIMS["nt"], preferred_element_type=F32)
        _, vjp = jax.vjp(_glu_fn, g_ref[...].astype(F32), u_ref[...].astype(F32))
        dg, du = vjp(da)
        dg_ref[...] = dg.astype(dg_ref.dtype)
        du_ref[...] = du.astype(du_ref.dtype)

    blk = pl.BlockSpec((tm, FFN_TF), lambda i, j: (i, j))
    return pl.pallas_call(
        body,
        name=name,
        grid=(s // tm, D_FF // FFN_TF),
        in_specs=[pl.BlockSpec((tm, D_MODEL), lambda i, j: (i, 0)),
                  pl.BlockSpec((FFN_TF, D_MODEL), lambda i, j: (j, 0)), blk, blk] + [ANY_SPEC] * len(deps),
        out_specs=[blk, blk],
        out_shape=[jax.ShapeDtypeStruct((s, D_FF), BF16)] * 2,
        compiler_params=_cparams(dimension_semantics=("parallel", "arbitrary")),
    )(df, w_down, g, u, *deps)


def _ffn_dh(dg, du, w_gu_t, name, deps=()):
    s = dg.shape[0]
    tm, tn = min(FFN_TM, s), 512

    def body(dg_ref, du_ref, wg_ref, wu_ref, *rest):
        o_ref = rest[len(deps)]
        p = jnp.dot(dg_ref[...], wg_ref[...], preferred_element_type=F32)
        o_ref[...] = p + jnp.dot(du_ref[...], wu_ref[...], preferred_element_type=F32)

    a_spec = pl.BlockSpec((tm, D_FF), lambda i, j: (i, 0))
    return pl.pallas_call(
        body,
        name=name,
        grid=(s // tm, D_MODEL // tn),
        in_specs=[a_spec, a_spec,
                  pl.BlockSpec((D_FF, tn), lambda i, j: (0, j)),
                  pl.BlockSpec((D_FF, tn), lambda i, j: (1, j))] + [ANY_SPEC] * len(deps),
        out_specs=pl.BlockSpec((tm, tn), lambda i, j: (i, j)),
        out_shape=jax.ShapeDtypeStruct((s, D_MODEL), F32),
        compiler_params=_cparams(dimension_semantics=("parallel", "arbitrary")),
    )(dg, du, w_gu_t, w_gu_t, *deps)


FUSE_TM = 256
ROW_SPEC2 = pl.BlockSpec((1, D_MODEL), lambda i, j: (0, 0))
ROW_SPEC1 = pl.BlockSpec((1, D_MODEL), lambda i: (0, 0))
SUMS_SPEC1 = pl.BlockSpec((V7X_SUBLANES, D_MODEL), lambda i: (0, 0))
SUMS_SPEC2 = pl.BlockSpec((V7X_SUBLANES, D_MODEL), lambda i, j: (0, 0))
SUMS_SHAPE = jax.ShapeDtypeStruct((V7X_SUBLANES, D_MODEL), F32)


def _sum_rows(*rows):
    pad = jnp.zeros((V7X_SUBLANES - len(rows), rows[0].shape[1]), F32)
    return jnp.concatenate([*rows, pad], axis=0)


def _pre_up(x, g, shift, scale, w_gu_t, name, deps=()):
    s = x.shape[0]
    tm = min(FFN_TM, s)
    nf = D_FF // FFN_TF
    nd = len(deps)

    def body(x_ref, g_ref, sh_ref, sc_ref, wg_ref, wu_ref, *rest):
        h_ref, a_ref, gg_ref, u_ref, h_s = rest[nd:]

        @pl.when(pl.program_id(1) == 0)
        def _():
            h = _pre_fn(g_ref[...], sh_ref[...], sc_ref[...], x_ref[...]).astype(BF16)
            h_s[...] = h
            h_ref[...] = h

        hv = h_s[...]
        gv = lax.dot_general(hv, wg_ref[...], _DIMS["nt"], preferred_element_type=F32)
        uv = lax.dot_general(hv, wu_ref[...], _DIMS["nt"], preferred_element_type=F32)
        a_ref[...] = _glu_fn(gv, uv).astype(a_ref.dtype)
        gg_ref[...] = gv.astype(gg_ref.dtype)
        u_ref[...] = uv.astype(u_ref.dtype)

    rows = pl.BlockSpec((tm, D_MODEL), lambda i, j: (i, 0))
    out = pl.BlockSpec((tm, FFN_TF), lambda i, j: (i, j))
    return pl.pallas_call(
        body,
        name=name,
        grid=(s // tm, nf),
        in_specs=[rows, ROW_SPEC2, ROW_SPEC2, ROW_SPEC2,
                  pl.BlockSpec((FFN_TF, D_MODEL), lambda i, j: (j, 0)),
                  pl.BlockSpec((FFN_TF, D_MODEL), lambda i, j: (nf + j, 0))] + [ANY_SPEC] * nd,
        out_specs=[rows, out, out, out],
        out_shape=[jax.ShapeDtypeStruct((s, D_MODEL), BF16)] + [jax.ShapeDtypeStruct((s, D_FF), BF16)] * 3,
        scratch_shapes=[pltpu.VMEM((tm, D_MODEL), BF16)],
        compiler_params=_cparams(dimension_semantics=("parallel", "arbitrary")),
    )(x, g, shift, scale, w_gu_t, w_gu_t, *deps)


def _pre_matmul(x, g, shift, scale, w_t, name, b_shift=0, tn=512):
    s = x.shape[0]
    n = w_t.shape[0]
    tm = min(2 * FFN_TM, s)

    def body(x_ref, g_ref, sh_ref, sc_ref, w_ref, h_ref, o_ref, h_s):
        @pl.when(pl.program_id(1) == 0)
        def _():
            h = _pre_fn(g_ref[...], sh_ref[...], sc_ref[...], x_ref[...]).astype(BF16)
            h_s[...] = h
            h_ref[...] = h

        o_ref[...] = lax.dot_general(h_s[...], w_ref[...], _DIMS["nt"], preferred_element_type=F32)

    rows = pl.BlockSpec((tm, D_MODEL), lambda i, j: (i, 0))
    return pl.pallas_call(
        body,
        name=name,
        grid=(s // tm, n // tn),
        in_specs=[rows, ROW_SPEC2, ROW_SPEC2, ROW_SPEC2,
                  pl.BlockSpec((tn, D_MODEL), lambda i, j: ((j + b_shift) % (n // tn), 0))],
        out_specs=[rows, pl.BlockSpec((tm, tn), lambda i, j: (i, j))],
        out_shape=[jax.ShapeDtypeStruct((s, D_MODEL), BF16), jax.ShapeDtypeStruct((s, n), F32)],
        scratch_shapes=[pltpu.VMEM((tm, D_MODEL), BF16)],
        compiler_params=_cparams(dimension_semantics=("parallel", "arbitrary")),
    )(x, g, shift, scale, w_t)


def _matmul_post(a, w, x, g_post, gate, res_w, name, target=None):
    s, k = a.shape
    tm = min(FFN_TM, s)
    extra = [] if target is None else [target]

    def body(a_ref, w_ref, x_ref, g_ref, gate_ref, *rest):
        f = jnp.dot(a_ref[...], w_ref[...], preferred_element_type=F32)
        y = _post_fn(res_w, g_ref[...], gate_ref[...], f, x_ref[...])
        if target is None:
            f_ref, y_ref = rest
            y_ref[...] = y
        else:
            t_ref, f_ref, dy_ref, sq_ref = rest
            diff = y - t_ref[...]
            dy_ref[...] = diff * (1.0 / D_MODEL)
            _accumulate(sq_ref, _rowsum(diff * diff), pl.program_id(0))
        f_ref[...] = f

    rows = pl.BlockSpec((tm, D_MODEL), lambda i: (i, 0))
    out_specs, out_shape = [rows, rows], [jax.ShapeDtypeStruct((s, D_MODEL), F32)] * 2
    if target is not None:
        out_specs.append(ROW_SPEC1)
        out_shape.append(jax.ShapeDtypeStruct((1, D_MODEL), F32))
    return pl.pallas_call(
        body,
        name=name,
        grid=(s // tm,),
        in_specs=[pl.BlockSpec((tm, k), lambda i: (i, 0)), pl.BlockSpec((k, D_MODEL), lambda i: (0, 0)), rows,
                  ROW_SPEC1, ROW_SPEC1] + [rows] * len(extra),
        out_specs=out_specs,
        out_shape=out_shape,
        compiler_params=_cparams(dimension_semantics=("arbitrary",)),
    )(a, w, x, g_post, gate, *extra)


def _merge_matmul_post(att, rec, proj, w, x, g_post, gate, res_w, name):
    s = att.shape[0]
    tm = min(FUSE_TM, s)

    def body(att_ref, rec_ref, ga_ref, gr_ref, w_ref, x_ref, g_ref, gate_ref, m_ref, f_ref, y_ref):
        merged = _merge_fn(att_ref[...], rec_ref[...], ga_ref[...], gr_ref[...]).astype(BF16)
        m_ref[...] = merged
        f = jnp.dot(merged, w_ref[...], preferred_element_type=F32)
        f_ref[...] = f
        y_ref[...] = _post_fn(res_w, g_ref[...], gate_ref[...], f, x_ref[...])

    rows = pl.BlockSpec((tm, D_MODEL), lambda i: (i, 0))
    return pl.pallas_call(
        body,
        name=name,
        grid=(s // tm,),
        in_specs=[rows, rows, pl.BlockSpec((tm, D_MODEL), lambda i: (i, 2)), pl.BlockSpec((tm, D_MODEL), lambda i: (i, 3)),
                  pl.BlockSpec(w.shape, lambda i: (0, 0)), rows, ROW_SPEC1, ROW_SPEC1],
        out_specs=[rows, rows, rows],
        out_shape=[jax.ShapeDtypeStruct((s, D_MODEL), BF16)] + [jax.ShapeDtypeStruct((s, D_MODEL), F32)] * 2,
        compiler_params=_cparams(dimension_semantics=("parallel",)),
    )(att, rec, proj, proj, w, x, g_post, gate)


def _post_bwd_merge_bwd(f, dy, g_post, gate, res_w, w, att, rec, proj, name):
    s = f.shape[0]
    tm = min(FUSE_TM, s)

    def body(f_ref, dy_ref, gp_ref, gate_ref, w_ref, att_ref, rec_ref, ga_ref, gr_ref,
             df_ref, datt_ref, drec_ref, dga_ref, dgr_ref, sums_ref):
        i = pl.program_id(0)
        dgp, dgate, df = _post_vjp(res_w, gp_ref[...], gate_ref[...], f_ref[...], dy_ref[...])
        dfb = df.astype(BF16)
        df_ref[...] = dfb
        _accumulate(sums_ref, _sum_rows(dgp, dgate), i)
        dmerged = lax.dot_general(dfb, w_ref[...], _DIMS["nt"], preferred_element_type=F32)
        _, vjp = jax.vjp(_merge_fn, att_ref[...], rec_ref[...], ga_ref[...], gr_ref[...])
        for ref, val in zip((datt_ref, drec_ref, dga_ref, dgr_ref), vjp(dmerged)):
            ref[...] = val.astype(ref.dtype)

    rows = pl.BlockSpec((tm, D_MODEL), lambda i: (i, 0))
    return pl.pallas_call(
        body,
        name=name,
        grid=(s // tm,),
        in_specs=[rows, rows, ROW_SPEC1, ROW_SPEC1, pl.BlockSpec(w.shape, lambda i: (0, 0)), rows, rows,
                  pl.BlockSpec((tm, D_MODEL), lambda i: (i, 2)), pl.BlockSpec((tm, D_MODEL), lambda i: (i, 3))],
        out_specs=[rows] * 5 + [SUMS_SPEC1],
        out_shape=[jax.ShapeDtypeStruct((s, D_MODEL), BF16)] * 5 + [SUMS_SHAPE],
        compiler_params=_cparams(dimension_semantics=("arbitrary",)),
    )(f, dy, g_post, gate, w, att, rec, proj, proj)


def _matmul_recin_bwd(drec, w, hs, proj, name, deps=()):
    s = drec.shape[0]
    tm = min(FUSE_TM, s)
    nd = len(deps)

    def body(d_ref, w_ref, hs_ref, yr_ref, *rest):
        dhs_ref, dyr_ref = rest[nd:]
        d = lax.dot_general(d_ref[...], w_ref[...], _DIMS["nt"], preferred_element_type=F32)
        _, vjp = jax.vjp(_recin_fn, hs_ref[...], yr_ref[...])
        dhs, dyr = vjp(d)
        dhs_ref[...] = dhs
        dyr_ref[...] = dyr.astype(dyr_ref.dtype)

    rows = pl.BlockSpec((tm, D_MODEL), lambda i: (i, 0))
    return pl.pallas_call(
        body,
        name=name,
        grid=(s // tm,),
        in_specs=[rows, pl.BlockSpec(w.shape, lambda i: (0, 0)), rows,
                  pl.BlockSpec((tm, D_MODEL), lambda i: (i, 1))] + [ANY_SPEC] * nd,
        out_specs=[rows, rows],
        out_shape=[jax.ShapeDtypeStruct((s, D_MODEL), F32), jax.ShapeDtypeStruct((s, D_MODEL), BF16)],
        compiler_params=_cparams(dimension_semantics=("parallel",)),
    )(drec, w, hs, proj, *deps)


def _post_vjp(res_w, g, gate, f, dy):
    _, vjp = jax.vjp(lambda g, gate, f: _post_fn(res_w, g, gate, f, 0.0), g, gate, f)
    return vjp(dy)


def _post_bwd_up_bwd(f, dy, g_post, gate, res_w, w_down, g, u, name, deps=()):
    s = f.shape[0]
    tm = min(FFN_TM, s)
    nd = len(deps)

    def body(f_ref, dy_ref, gp_ref, gate_ref, wd_ref, g_ref, u_ref, *rest):
        df_ref, dgu_ref, sums_ref, df_s = rest[nd:]
        i = pl.program_id(0)

        @pl.when(pl.program_id(1) == 0)
        def _():
            dgp, dgate, df = _post_vjp(res_w, gp_ref[...], gate_ref[...], f_ref[...], dy_ref[...])
            df_s[...] = df.astype(BF16)
            df_ref[...] = df_s[...]
            _accumulate(sums_ref, _sum_rows(dgp, dgate), i)

        da = lax.dot_general(df_s[...], wd_ref[...], _DIMS["nt"], preferred_element_type=F32)
        _, vjp = jax.vjp(_glu_fn, g_ref[...].astype(F32), u_ref[...].astype(F32))
        dg, du = vjp(da)
        dgu_ref[0] = dg.astype(dgu_ref.dtype)
        dgu_ref[1] = du.astype(dgu_ref.dtype)

    rows = pl.BlockSpec((tm, D_MODEL), lambda i, j: (i, 0))
    blk = pl.BlockSpec((tm, FFN_TF), lambda i, j: (i, j))
    return pl.pallas_call(
        body,
        name=name,
        grid=(s // tm, D_FF // FFN_TF),
        in_specs=[rows, rows, ROW_SPEC2, ROW_SPEC2, pl.BlockSpec((FFN_TF, D_MODEL), lambda i, j: (j, 0)), blk,
                  blk] + [ANY_SPEC] * nd,
        out_specs=[rows, pl.BlockSpec((2, tm, FFN_TF), lambda i, j: (0, i, j)), SUMS_SPEC2],
        out_shape=[jax.ShapeDtypeStruct((s, D_MODEL), BF16), jax.ShapeDtypeStruct((2, s, D_FF), BF16), SUMS_SHAPE],
        scratch_shapes=[pltpu.VMEM((tm, D_MODEL), BF16)],
        compiler_params=_cparams(dimension_semantics=("arbitrary", "arbitrary")),
    )(f, dy, g_post, gate, w_down, g, u, *deps)


def _post_bwd_matmul(f, dy, g_post, gate, res_w, w, name):
    s = f.shape[0]
    n = w.shape[0]
    tm = min(FUSE_TM, s)

    def body(f_ref, dy_ref, gp_ref, gate_ref, w_ref, df_ref, o_ref, dgp_ref, dgate_ref):
        i = pl.program_id(0)
        dgp, dgate, df = _post_vjp(res_w, gp_ref[...], gate_ref[...], f_ref[...], dy_ref[...])
        dfb = df.astype(BF16)
        df_ref[...] = dfb
        _accumulate(dgp_ref, dgp, i)
        _accumulate(dgate_ref, dgate, i)
        o_ref[...] = lax.dot_general(dfb, w_ref[...], _DIMS["nt"], preferred_element_type=F32)

    rows = pl.BlockSpec((tm, D_MODEL), lambda i: (i, 0))
    return pl.pallas_call(
        body,
        name=name,
        grid=(s // tm,),
        in_specs=[rows, rows, ROW_SPEC1, ROW_SPEC1, pl.BlockSpec((n, D_MODEL), lambda i: (0, 0))],
        out_specs=[rows, pl.BlockSpec((tm, n), lambda i: (i, 0)), ROW_SPEC1, ROW_SPEC1],
        out_shape=[jax.ShapeDtypeStruct((s, D_MODEL), BF16), jax.ShapeDtypeStruct((s, n), F32),
                   jax.ShapeDtypeStruct((1, D_MODEL), F32), jax.ShapeDtypeStruct((1, D_MODEL), F32)],
        compiler_params=_cparams(dimension_semantics=("arbitrary",)),
    )(f, dy, g_post, gate, w)


def _matmul_pre_bwd(parts, w_t, x, dres, g, shift, scale, name, deps=()):
    s = x.shape[0]
    na, nd = len(parts), len(deps)
    ranges = [p[3] for p in parts]

    def body(*refs):
        a_refs = refs[:na]
        w_ref, x_ref, dres_ref, g_ref, sh_ref, sc_ref = refs[na : na + 6]
        dx_ref, sums_ref = refs[na + 6 + nd :]
        i = pl.program_id(0)
        dh = None
        for a_ref, (r0, r1) in zip(a_refs, ranges):
            p = jnp.dot(a_ref[...], w_ref[r0:r1, :], preferred_element_type=F32)
            dh = p if dh is None else dh + p
        _, vjp = jax.vjp(_pre_fn, g_ref[...], sh_ref[...], sc_ref[...], x_ref[...])
        dg, dsh, dsc, dx = vjp(dh)
        dx_ref[...] = dx + dres_ref[...]
        _accumulate(sums_ref, _sum_rows(dg, dsh, dsc), i)

    tm = parts[0][1][-2]
    rows = pl.BlockSpec((tm, D_MODEL), lambda i: (i, 0))
    return pl.pallas_call(
        body,
        name=name,
        grid=(s // tm,),
        in_specs=[pl.BlockSpec(p[1], p[2]) for p in parts]
        + [pl.BlockSpec(w_t.shape, lambda i: (0, 0)), rows, rows, ROW_SPEC1, ROW_SPEC1, ROW_SPEC1]
        + [ANY_SPEC] * nd,
        out_specs=[rows, SUMS_SPEC1],
        out_shape=[jax.ShapeDtypeStruct((s, D_MODEL), F32), SUMS_SHAPE],
        compiler_params=_cparams(dimension_semantics=("arbitrary",)),
    )(*[p[0] for p in parts], w_t, x, dres, g, shift, scale, *deps)


def _dw_gu(dgu, h, name, deps=(), tk=1024):
    s = h.shape[0]
    tk = min(tk, s)
    nk = s // tk
    half = D_FF // FFN_TF

    def body(a_ref, b_ref, *rest):
        o_ref, acc = rest[len(deps) :]
        kk = pl.program_id(1)
        p = lax.dot_general(a_ref[...], b_ref[...], _DIMS["tn"], preferred_element_type=F32)

        @pl.when(kk == 0)
        def _():
            acc[...] = p

        @pl.when(kk > 0)
        def _():
            acc[...] += p

        @pl.when(kk == nk - 1)
        def _():
            o_ref[...] = acc[...].astype(o_ref.dtype)

    return pl.pallas_call(
        body,
        name=name,
        grid=(2 * half, nk),
        in_specs=[pl.BlockSpec((None, tk, FFN_TF), lambda i, kk: (i // half, kk, i % half)),
                  pl.BlockSpec((tk, D_MODEL), lambda i, kk: (kk, 0))] + [ANY_SPEC] * len(deps),
        out_specs=pl.BlockSpec((FFN_TF, D_MODEL), lambda i, kk: (i, 0)),
        out_shape=jax.ShapeDtypeStruct((2 * D_FF, D_MODEL), BF16),
        scratch_shapes=[pltpu.VMEM((FFN_TF, D_MODEL), F32)],
        compiler_params=_cparams(dimension_semantics=("parallel", "arbitrary")),
    )(dgu, h, *deps)


def _lru_diag_blocks(dw_bd, name):
    def body(w_ref, o_ref):
        for half in range(2):
            for n in range(LRU_BLOCKS):
                rows = slice(n * LRU_BLOCK, (n + 1) * LRU_BLOCK)
                cols = slice(half * LRU_WIDTH + n * LRU_BLOCK, half * LRU_WIDTH + (n + 1) * LRU_BLOCK)
                o_ref[half, rows, :] = w_ref[rows, cols]

    return pl.pallas_call(
        body, name=name, out_shape=jax.ShapeDtypeStruct((2, LRU_WIDTH, LRU_BLOCK), F32), compiler_params=_cparams()
    )(dw_bd)


def _swiglu_fwd(gu, name):
    (a,) = _rowwise(_swiglu_fn, name, [], [gu], [(D_FF, BF16)], ts=128)
    return a


def _swiglu_bwd(gu, da, name, deps=()):
    def fn(gu, da):
        _, vjp = jax.vjp(_swiglu_fn, gu)
        return vjp(da)[0]

    (dgu,) = _rowwise(fn, name, [], [gu, da], [(2 * D_FF, BF16)], ts=128, deps=deps)
    return dgu


def _shift_down(ext, j, rows):
    return pltpu.roll(ext, j, 0)[V7X_SUBLANES : V7X_SUBLANES + rows]


def _shift_up(ext, j, rows):
    return pltpu.roll(ext, ext.shape[0] - j, 0)[:rows] if j else ext[:rows]


LRU_SLAB = 256
N_SLABS = LRU_WIDTH // LRU_SLAB


def _slab_weights(wa, wx):
    per = LRU_SLAB // LRU_BLOCK
    eye = jnp.eye(per, dtype=wa.dtype)

    def diag(w):
        w4 = w.reshape(N_SLABS, per, LRU_BLOCK, LRU_BLOCK)
        return jnp.einsum("sbkj,bc->sbkcj", w4, eye).reshape(N_SLABS, LRU_SLAB, LRU_SLAB)

    return jnp.concatenate([diag(wa), diag(wx)], axis=2).reshape(LRU_WIDTH, 2 * LRU_SLAB).astype(BF16)


def _slab_cols(v, s):
    lo = s * LRU_SLAB
    return jnp.concatenate([v[:, lo : lo + LRU_SLAB], v[:, LRU_WIDTH + lo : LRU_WIDTH + lo + LRU_SLAB]], axis=1)


def _lru_front(proj, w8, b, w_slab, ba, bx, lam, name):
    def fn(i, steps, w8, b, w_slab, ba, bx, lam, x, halo):
        halo = jnp.where(i > 0, halo, 0.0)
        ext = jnp.concatenate([halo, x], axis=0)
        xc = b + w8[3:4] * x
        for j in (1, 2, 3):
            xc = xc + w8[3 - j : 4 - j] * _shift_down(ext, j, x.shape[0])
        xcb = xc.astype(BF16)
        prods = []
        for s in range(N_SLABS):
            rows = slice(s * LRU_SLAB, (s + 1) * LRU_SLAB)
            prods.append(jnp.dot(xcb[:, rows], w_slab[rows], preferred_element_type=F32))
        pre = jnp.concatenate([p[:, :LRU_SLAB] for p in prods] + [p[:, LRU_SLAB:] for p in prods], axis=1)
        a, u = _gates_fn(ba, bx, lam, pre, xc)
        return xc, pre, a, u

    tiles = [(proj, LRU_WIDTH, 0), (proj, LRU_WIDTH, 0, "prev")]
    outs = [(LRU_WIDTH, F32), (2 * LRU_WIDTH, F32), (LRU_WIDTH, F32), (LRU_WIDTH, F32)]
    return _rowwise(fn, name, [w8, b, w_slab, ba, bx, lam], tiles, outs, with_index=True)


def _lru_back(pre, xc, w_slab, ba, bx, lam, g, h_prev, name, deps=()):
    def fn(w_slab, ba, bx, lam, pre, xc, g, h_prev):
        _, vjp = jax.vjp(_gates_fn, ba, bx, lam, pre, xc)
        dba, dbx, dlam, dpre, dxc = vjp((g * h_prev, g))
        dpre = dpre.astype(BF16)
        back = []
        for s in range(N_SLABS):
            rows = slice(s * LRU_SLAB, (s + 1) * LRU_SLAB)
            back.append(lax.dot_general(_slab_cols(dpre, s), w_slab[rows], _DIMS["nt"], preferred_element_type=F32))
        return dpre, dxc + jnp.concatenate(back, axis=1), _sum_rows(dba, dbx, dlam)

    return _rowwise(fn, name, [w_slab, ba, bx, lam], [pre, xc, g, h_prev],
                    [(2 * LRU_WIDTH, BF16), (LRU_WIDTH, F32)], [(V7X_SUBLANES, LRU_WIDTH)], deps=deps)


def _lru_dw(xc, dpre, name):
    s = xc.shape[0]
    ts = min(512, s)
    steps = s // ts
    per = LRU_SLAB // LRU_BLOCK

    def body(x_ref, d_ref, o_ref, acc):
        i = pl.program_id(0)
        xcb = x_ref[...].astype(BF16)
        d = d_ref[...]
        for sl in range(N_SLABS):
            rows = slice(sl * LRU_SLAB, (sl + 1) * LRU_SLAB)
            p = lax.dot_general(xcb[:, rows], _slab_cols(d, sl), _DIMS["tn"], preferred_element_type=F32)

            @pl.when(i == 0)
            def _(p=p, rows=rows):
                acc[rows, :] = p

            @pl.when(i > 0)
            def _(p=p, rows=rows):
                acc[rows, :] += p

        @pl.when(i == steps - 1)
        def _():
            for half in range(2):
                for n in range(LRU_BLOCKS):
                    r0 = n * LRU_BLOCK
                    c0 = half * LRU_SLAB + (n % per) * LRU_BLOCK
                    o_ref[half, r0 : r0 + LRU_BLOCK, :] = acc[r0 : r0 + LRU_BLOCK, c0 : c0 + LRU_BLOCK]

    return pl.pallas_call(
        body,
        name=name,
        grid=(steps,),
        in_specs=[pl.BlockSpec((ts, LRU_WIDTH), lambda i: (i, 0)), pl.BlockSpec((ts, 2 * LRU_WIDTH), lambda i: (i, 0))],
        out_specs=pl.BlockSpec((2, LRU_WIDTH, LRU_BLOCK), lambda i: (0, 0, 0)),
        out_shape=jax.ShapeDtypeStruct((2, LRU_WIDTH, LRU_BLOCK), F32),
        scratch_shapes=[pltpu.VMEM((LRU_WIDTH, 2 * LRU_SLAB), F32)],
        compiler_params=_cparams(dimension_semantics=("arbitrary",)),
    )(xc, dpre)


def _conv_bwd(proj, w8, d1, name):
    def fn(i, steps, w8, x, halo, d, d1n):
        rows = x.shape[0]
        dn = jnp.where(i < steps - 1, d1n, 0.0)
        halo = jnp.where(i > 0, halo, 0.0)
        dext = jnp.concatenate([d, dn], axis=0)
        xext = jnp.concatenate([halo, x], axis=0)
        dx = w8[3:4] * d
        dw = [None] * 4
        dw[3] = _rowsum(d * x)
        for k in (1, 2, 3):
            dx = dx + w8[3 - k : 4 - k] * _shift_up(dext, k, rows)
            dw[3 - k] = _rowsum(d * _shift_down(xext, k, rows))
        return dx, _sum_rows(*dw, _rowsum(d))

    tiles = [(proj, LRU_WIDTH, 0), (proj, LRU_WIDTH, 0, "prev"), d1, (d1, LRU_WIDTH, 0, "next")]
    return _rowwise(fn, name, [w8], tiles, [(LRU_WIDTH, BF16)], [(V7X_SUBLANES, LRU_WIDTH)], with_index=True)


SCAN_ROWS = 512


def _block_scan(a, b, row, reverse):
    for d in (1, 2, 4):
        if reverse:
            shift, keep = V7X_SUBLANES - d, row < V7X_SUBLANES - d
        else:
            shift, keep = d, row >= d
        a_s = pltpu.roll(a, shift, 0)
        b_s = pltpu.roll(b, shift, 0)
        b = jnp.where(keep, a * b_s + b, b)
        a = jnp.where(keep, a * a_s, a)
    return a, b


def _scan_fwd(a, u, proj, name):
    s, w = a.shape
    ts = min(SCAN_ROWS, s)
    sub = ts // V7X_SUBLANES

    def body(a_ref, u_ref, yr_ref, h_ref, hp_ref, rec_ref, carry):
        @pl.when(pl.program_id(0) == 0)
        def _():
            carry[...] = jnp.zeros_like(carry)

        row = lax.broadcasted_iota(jnp.int32, (V7X_SUBLANES, w), 0)

        def step(j, c):
            rows = pl.ds(pl.multiple_of(j * V7X_SUBLANES, V7X_SUBLANES), V7X_SUBLANES)
            pa, pb = _block_scan(a_ref[rows, :], u_ref[rows, :], row, False)
            h = pb + pa * c
            h_ref[rows, :] = h
            hp_ref[rows, :] = jnp.where(row >= 1, pltpu.roll(h, 1, 0), c)
            return jnp.broadcast_to(h[V7X_SUBLANES - 1 :], (V7X_SUBLANES, w))

        carry[...] = lax.fori_loop(0, sub, step, carry[...])
        rec_ref[...] = _recin_fn(h_ref[...], yr_ref[...]).astype(rec_ref.dtype)

    spec = pl.BlockSpec((ts, w), lambda i: (i, 0))
    return pl.pallas_call(
        body,
        name=name,
        grid=(s // ts,),
        in_specs=[spec, spec, pl.BlockSpec((ts, w), lambda i: (i, 1))],
        out_specs=[spec, spec, spec],
        out_shape=[jax.ShapeDtypeStruct((s, w), F32)] * 2 + [jax.ShapeDtypeStruct((s, w), BF16)],
        scratch_shapes=[pltpu.VMEM((V7X_SUBLANES, w), F32)],
        compiler_params=_cparams(dimension_semantics=("arbitrary",)),
    )(a, u, proj)


def _scan_bwd(a, dh, name):
    s, w = a.shape
    ts = min(SCAN_ROWS, s)
    sub = ts // V7X_SUBLANES
    steps = s // ts

    def body(a_ref, d_ref, g_ref, carry):
        @pl.when(pl.program_id(0) == 0)
        def _():
            carry[...] = jnp.zeros_like(carry)

        row = lax.broadcasted_iota(jnp.int32, (V7X_SUBLANES, w), 0)

        def step(jj, c):
            j = sub - 1 - jj
            rows = pl.ds(pl.multiple_of(j * V7X_SUBLANES, V7X_SUBLANES), V7X_SUBLANES)
            av, dv = a_ref[rows, :], d_ref[rows, :]
            pa, pb = _block_scan(av, av * dv, row, True)
            big = pb + pa * c
            g_ref[rows, :] = dv + jnp.where(row < V7X_SUBLANES - 1, pltpu.roll(big, V7X_SUBLANES - 1, 0), c)
            return jnp.broadcast_to(big[:1], (V7X_SUBLANES, w))

        carry[...] = lax.fori_loop(0, sub, step, carry[...])

    spec = pl.BlockSpec((ts, w), lambda i: (steps - 1 - i, 0))
    return pl.pallas_call(
        body,
        name=name,
        grid=(steps,),
        in_specs=[spec, spec],
        out_specs=spec,
        out_shape=jax.ShapeDtypeStruct((s, w), F32),
        scratch_shapes=[pltpu.VMEM((V7X_SUBLANES, w), F32)],
        compiler_params=_cparams(dimension_semantics=("arbitrary",)),
    )(a, dh)


def _rel_index():
    i = np.arange(ATT_TQ)[:, None]
    j = np.arange(3 * ATT_TQ)[None, :]
    band = (j // CHUNK >= i // CHUNK) & (j // CHUNK <= i // CHUNK + LEFT_CHUNKS)
    return band


SKEW = 4 * ATT_TQ


def _skew_onehot():
    t = np.arange(SKEW)
    diag = np.where(t < 3 * ATT_TQ, -t, SKEW - t)
    idx = np.clip(diag + LEFT_CHUNKS * CHUNK, -MAX_REL, MAX_REL) + MAX_REL
    hit = (idx[:, None] == np.arange(2 * MAX_REL + 1)[None, :]) & (t[:, None] != 3 * ATT_TQ)
    return hit.astype(np.float32)


def _bias_tile(rel_bias):
    per_t = jnp.dot(rel_bias, jnp.asarray(_skew_onehot()).T, precision=lax.Precision.HIGHEST)
    flat = jnp.broadcast_to(per_t[:, None, :], (ATT_HEADS, ATT_TQ, SKEW)).reshape(ATT_HEADS, ATT_TQ * SKEW)
    tile = flat[:, : ATT_TQ * (SKEW - 1)].reshape(ATT_HEADS, ATT_TQ, SKEW - 1)[:, :, : 3 * ATT_TQ]
    first = (2 - np.arange(3))[:, None, None, None] * ATT_TQ
    seen = _rel_index()[None, None] & (np.arange(3 * ATT_TQ)[None, None, None, :] >= first)
    return jnp.where(jnp.asarray(seen), tile[None], NEG)


def _bias_grad(dbias):
    flat = jnp.pad(dbias, ((0, 0), (0, 0), (0, SKEW - 1 - 3 * ATT_TQ))).reshape(ATT_HEADS, ATT_TQ * (SKEW - 1))
    per_t = jnp.sum(jnp.pad(flat, ((0, 0), (0, ATT_TQ))).reshape(ATT_HEADS, ATT_TQ, SKEW), axis=1)
    return jnp.dot(per_t, jnp.asarray(_skew_onehot()), precision=lax.Precision.HIGHEST)


def _attn_specs(nt):
    qb, kb, vb = OFF_Q // V7X_LANES, OFF_K // V7X_LANES, OFF_V // V7X_LANES
    blk = (ATT_TQ, V7X_LANES)

    def qmap(base):
        return lambda hp, m: (jnp.minimum(m, nt - 1), base + hp)

    def wmap(base, back):
        return lambda hp, m: (jnp.clip(m - back, 0, nt - 1), base + hp)

    specs = [pl.BlockSpec(blk, qmap(qb))]
    specs += [pl.BlockSpec(blk, wmap(kb, back)) for back in (2, 1, 0)]
    specs += [pl.BlockSpec(blk, wmap(vb, back)) for back in (2, 1, 0)]
    return specs


ATT_SCALE = ATT_HEAD_DIM**-0.5


def _attn_exp(qh, kh, bias):
    s = lax.dot_general(qh, kh, _DIMS["nt"], preferred_element_type=F32) + bias
    e = jnp.exp(s - jnp.max(s, axis=-1, keepdims=True))
    return e, jnp.sum(e, axis=-1, keepdims=True)


def _attn_window(k0, k1, k2, v0, v1, v2):
    k = jnp.concatenate([k0[...], k1[...], k2[...]], axis=0).astype(BF16)
    v = jnp.concatenate([v0[...], v1[...], v2[...]], axis=0).astype(BF16)
    return k, v


def _bias_spec():
    return pl.BlockSpec((1, 2, ATT_TQ, 3 * ATT_TQ), lambda hp, m: (jnp.minimum(m, 2), hp, 0, 0))


def _attn_fwd(proj, bias, name):
    s = proj.shape[0]
    nt = s // ATT_TQ

    def body(q_ref, k0, k1, k2, v0, v1, v2, b_ref, o_ref):
        k, v = _attn_window(k0, k1, k2, v0, v1, v2)
        q = (q_ref[...] * ATT_SCALE).astype(BF16)
        for hh in range(2):
            cols = slice(hh * ATT_HEAD_DIM, (hh + 1) * ATT_HEAD_DIM)
            e, total = _attn_exp(q[:, cols], k[:, cols], b_ref[0, hh])
            o = jnp.dot(e.astype(BF16), v[:, cols], preferred_element_type=F32) / total
            o_ref[:, cols] = o.astype(o_ref.dtype)

    specs = _attn_specs(nt) + [_bias_spec()]
    return pl.pallas_call(
        body,
        name=name,
        grid=(ATT_HEADS // 2, nt),
        in_specs=specs,
        out_specs=pl.BlockSpec((ATT_TQ, V7X_LANES), lambda hp, m: (m, hp)),
        out_shape=jax.ShapeDtypeStruct((s, ATT_WIDTH), BF16),
        compiler_params=_cparams(dimension_semantics=("parallel", "arbitrary")),
    )(proj, proj, proj, proj, proj, proj, proj, bias)


def _attn_bwd(proj, bias, do, name):
    s = proj.shape[0]
    nt = s // ATT_TQ
    win = 3 * ATT_TQ

    def body(q_ref, k0, k1, k2, v0, v1, v2, do_ref, b_ref, dq_ref, dk_ref, dv_ref, db_ref, dk_acc, dv_acc):
        m = pl.program_id(1)

        @pl.when(m == 0)
        def _():
            dk_acc[...] = jnp.zeros_like(dk_acc)
            dv_acc[...] = jnp.zeros_like(dv_acc)
            db_ref[...] = jnp.zeros_like(db_ref)

        @pl.when(m < nt)
        def _():
            k, v = _attn_window(k0, k1, k2, v0, v1, v2)
            q = (q_ref[...] * ATT_SCALE).astype(BF16)
            dout = do_ref[...]
            for hh in range(2):
                cols = slice(hh * ATT_HEAD_DIM, (hh + 1) * ATT_HEAD_DIM)
                qh, kh, vh, doh = q[:, cols], k[:, cols], v[:, cols], dout[:, cols]
                e, total = _attn_exp(qh, kh, b_ref[0, hh])
                p = e / total
                dvh = lax.dot_general(p.astype(BF16), doh, _DIMS["tn"], preferred_element_type=F32)
                dp = lax.dot_general(doh, vh, _DIMS["nt"], preferred_element_type=F32)
                ds = p * (dp - jnp.sum(dp * p, axis=-1, keepdims=True))
                db_ref[hh] += ds
                dsb = ds.astype(BF16)
                dqh = jnp.dot(dsb, kh, preferred_element_type=F32) * ATT_SCALE
                dkh = lax.dot_general(dsb, qh, _DIMS["tn"], preferred_element_type=F32)
                dq_ref[:, cols] = dqh.astype(dq_ref.dtype)
                dk_acc[:, cols] += dkh
                dv_acc[:, cols] += dvh

        dk_ref[...] = dk_acc[:ATT_TQ].astype(dk_ref.dtype)
        dv_ref[...] = dv_acc[:ATT_TQ].astype(dv_ref.dtype)
        for acc in (dk_acc, dv_acc):
            rest = acc[ATT_TQ:]
            acc[: win - ATT_TQ] = rest
            acc[win - ATT_TQ :] = jnp.zeros((ATT_TQ, V7X_LANES), F32)

    blk = (ATT_TQ, V7X_LANES)
    specs = _attn_specs(nt)
    specs.append(pl.BlockSpec(blk, lambda hp, m: (jnp.minimum(m, nt - 1), hp)))
    specs.append(_bias_spec())
    done = lambda hp, m: (jnp.maximum(m - 2, 0), hp)
    out_specs = [
        pl.BlockSpec(blk, lambda hp, m: (jnp.minimum(m, nt - 1), hp)),
        pl.BlockSpec(blk, done),
        pl.BlockSpec(blk, done),
        pl.BlockSpec((2, ATT_TQ, win), lambda hp, m: (hp, 0, 0)),
    ]
    out_shape = [jax.ShapeDtypeStruct((s, ATT_WIDTH), BF16)] * 3
    out_shape.append(jax.ShapeDtypeStruct((ATT_HEADS, ATT_TQ, win), F32))
    return pl.pallas_call(
        body,
        name=name,
        grid=(ATT_HEADS // 2, nt + 2),
        in_specs=specs,
        out_specs=out_specs,
        out_shape=out_shape,
        scratch_shapes=[pltpu.VMEM((win, V7X_LANES), F32), pltpu.VMEM((win, V7X_LANES), F32)],
        compiler_params=_cparams(dimension_semantics=("arbitrary", "arbitrary")),
    )(proj, proj, proj, proj, proj, proj, proj, do, bias)


def _ada_fwd(c_all, w, name):
    def body(c_ref, w_ref, o_ref):
        act = _silu(c_ref[...]).astype(BF16)
        o_ref[...] = jnp.dot(act, w_ref[...].astype(BF16), preferred_element_type=F32)

    return pl.pallas_call(
        body, name=name, out_shape=jax.ShapeDtypeStruct((c_all.shape[0], w.shape[1]), F32), compiler_params=_cparams()
    )(c_all, w)


def _ada_bwd(c_all, dmod, name):
    def body(c_ref, d_ref, o_ref):
        act = _silu(c_ref[...])
        o_ref[...] = lax.dot_general(act, d_ref[...], _DIMS["tn"], preferred_element_type=F32,
                                     precision=lax.Precision.HIGHEST)

    return pl.pallas_call(
        body, name=name, out_shape=jax.ShapeDtypeStruct((c_all.shape[1], dmod.shape[1]), F32), compiler_params=_cparams()
    )(c_all, dmod)


def _adamw_parts(landed, sent, me, w, m, v, name, rows=256):
    r, c = w.shape
    tr = _pick(r, rows, 16)

    def body(me_ref, g_ref, own_ref, w_ref, m_ref, v_ref, go_ref, d_ref, mo_ref, vo_ref):
        mine = me_ref[0]
        grad = jnp.zeros((tr, c), F32)
        for d in range(N_DEV):
            grad = grad + jnp.where(mine == d, own_ref[0], g_ref[d]).astype(F32)
        _adamw_update(grad, w_ref, m_ref, v_ref, go_ref, d_ref, mo_ref, vo_ref)

    spec = pl.BlockSpec((tr, c), lambda i, me_ref: (i, 0))
    return pl.pallas_call(
        body,
        name=name,
        grid_spec=pltpu.PrefetchScalarGridSpec(
            num_scalar_prefetch=1,
            grid=(r // tr,),
            in_specs=[pl.BlockSpec((N_DEV, tr, c), lambda i, me_ref: (0, i, 0)),
                      pl.BlockSpec((1, tr, c), lambda i, me_ref: (me_ref[0], i, 0)), spec, spec, spec],
            out_specs=[spec] * 4,
        ),
        out_shape=[jax.ShapeDtypeStruct((r, c), F32)] * 4,
        compiler_params=_cparams(dimension_semantics=("parallel",)),
    )(me.reshape(1).astype(jnp.int32), landed, sent, w, m, v)


def _adamw_update(grad, w_ref, m_ref, v_ref, go_ref, d_ref, mo_ref, vo_ref):
    m2 = ADAM_B1 * m_ref[...] + (1.0 - ADAM_B1) * grad
    v2 = ADAM_B2 * v_ref[...] + (1.0 - ADAM_B2) * (grad * grad)
    m_hat = m2 / (1.0 - ADAM_B1**ADAM_STEP)
    v_hat = v2 / (1.0 - ADAM_B2**ADAM_STEP)
    go_ref[...] = grad
    d_ref[...] = -ADAM_LR * (m_hat / (jnp.sqrt(v_hat) + ADAM_EPS) + ADAM_WD * w_ref[...])
    mo_ref[...] = m2
    vo_ref[...] = v2


def _adamw(g, w, m, v, name, rows=256):
    r, c = w.shape
    tr = _pick(r, rows, 16)

    def body(g_ref, w_ref, m_ref, v_ref, go_ref, d_ref, mo_ref, vo_ref):
        _adamw_update(g_ref[...], w_ref, m_ref, v_ref, go_ref, d_ref, mo_ref, vo_ref)

    spec = pl.BlockSpec((tr, c), lambda i: (i, 0))
    return pl.pallas_call(
        body,
        name=name,
        grid=(r // tr,),
        in_specs=[spec, spec, spec, spec],
        out_specs=[spec] * 4,
        out_shape=[jax.ShapeDtypeStruct((r, c), F32)] * 4,
        compiler_params=_cparams(dimension_semantics=("parallel",)),
    )(g, w, m, v)


def _sum_parts(parts, name):
    def body(p_ref, o_ref):
        acc = p_ref[0]
        for d in range(1, N_DEV):
            acc = acc + p_ref[d]
        o_ref[...] = acc

    return pl.pallas_call(
        body, name=name, out_shape=jax.ShapeDtypeStruct(parts.shape[1:], F32), compiler_params=_cparams()
    )(parts)


def _place():
    x, y, c = lax.axis_index("x"), lax.axis_index("y"), lax.axis_index("c")
    return x, y, c


def _dev_index(p):
    return 4 * p[0] + 2 * p[1] + p[2]


def _allgather_vmem(shard, name):
    m_per, n = shard.shape

    def body(x_ref, out_ref, send_sems, recv_sems, local_sem):
        x, y, c = _place()
        me, sibling = (x, y, c), (x, y, 1 - c)
        chips = [(1 - x, y), (x, 1 - y), (1 - x, 1 - y)]

        def rows(p):
            return out_ref.at[pl.ds(_dev_index(p) * m_per, m_per), :]

        def copy(k, block, to, src=None):
            return pltpu.make_async_remote_copy(
                src_ref=rows(block) if src is None else src, dst_ref=rows(block),
                send_sem=send_sems.at[k], recv_sem=recv_sems.at[k], device_id=to, device_id_type=MESH)

        mine = pltpu.make_async_copy(x_ref, rows(me), local_sem)
        mine.start()
        first = [copy(0, me, sibling, src=x_ref)]
        first += [copy(1 + j, me, (*chip, c), src=x_ref) for j, chip in enumerate(chips)]
        for cp in first:
            cp.start()
        passed = [copy(4 + j, (*chip, c), sibling) for j, chip in enumerate(chips)]
        for j, chip in enumerate(chips):
            copy(1 + j, (*chip, c), me).wait_recv()
            passed[j].start()
        copy(0, sibling, me).wait_recv()
        for j, chip in enumerate(chips):
            copy(4 + j, (*chip, 1 - c), me).wait_recv()
        for cp in first + passed:
            cp.wait_send()
        mine.wait()

    return pl.pallas_call(
        body,
        name=name,
        out_shape=jax.ShapeDtypeStruct((N_DEV * m_per, n), shard.dtype),
        in_specs=[pl.BlockSpec(memory_space=pltpu.VMEM)],
        out_specs=pl.BlockSpec(memory_space=pltpu.VMEM),
        scratch_shapes=[pltpu.SemaphoreType.DMA((7,)), pltpu.SemaphoreType.DMA((7,)), pltpu.SemaphoreType.DMA],
        compiler_params=_cparams(),
    )(shard)


def _allgather_hbm(shards, name):
    n = len(shards)

    def body(*refs):
        ins, outs = refs[:n], refs[n : 2 * n]
        send_sems, recv_sems, local_sems = refs[2 * n :]
        x, y, c = _place()
        me, sibling = (x, y, c), (x, y, 1 - c)
        chips = [(1 - x, y), (x, 1 - y), (1 - x, 1 - y)]

        def copy(a, k, block, to, src=None):
            dst = outs[a].at[_dev_index(block)]
            return pltpu.make_async_remote_copy(
                src_ref=dst if src is None else src, dst_ref=dst,
                send_sem=send_sems.at[a * 7 + k], recv_sem=recv_sems.at[a * 7 + k], device_id=to, device_id_type=MESH)

        mine = [pltpu.make_async_copy(ins[a], outs[a].at[_dev_index(me)], local_sems.at[a]) for a in range(n)]
        for cp in mine:
            cp.start()
        first = []
        for a in range(n):
            first.append(copy(a, 0, me, sibling, src=ins[a]))
            first += [copy(a, 1 + j, me, (*chip, c), src=ins[a]) for j, chip in enumerate(chips)]
        for cp in first:
            cp.start()
        passed = []
        for j, chip in enumerate(chips):
            for a in range(n):
                copy(a, 1 + j, (*chip, c), me).wait_recv()
                cp = copy(a, 4 + j, (*chip, c), sibling)
                cp.start()
                passed.append(cp)
        for a in range(n):
            copy(a, 0, sibling, me).wait_recv()
        for j, chip in enumerate(chips):
            for a in range(n):
                copy(a, 4 + j, (*chip, 1 - c), me).wait_recv()
        for cp in first + passed:
            cp.wait_send()
        for cp in mine:
            cp.wait()

    any_spec = pl.BlockSpec(memory_space=pl.ANY)
    return pl.pallas_call(
        body,
        name=name,
        out_shape=[jax.ShapeDtypeStruct((N_DEV, *s.shape), s.dtype) for s in shards],
        in_specs=[any_spec] * n,
        out_specs=[any_spec] * n,
        scratch_shapes=[pltpu.SemaphoreType.DMA((7 * n,)), pltpu.SemaphoreType.DMA((7 * n,)),
                        pltpu.SemaphoreType.DMA((n,))],
        compiler_params=_cparams(),
    )(*shards)


def _exchange_hbm(bufs, name):
    n = len(bufs)

    def body(*refs):
        ins, outs = refs[:n], refs[n : 2 * n]
        send_sems, recv_sems, local_sems = refs[2 * n :]
        x, y, c = _place()
        me = _dev_index((x, y, c))
        mine = [pltpu.make_async_copy(ins[a].at[me], outs[a].at[me], local_sems.at[a]) for a in range(n)]
        for cp in mine:
            cp.start()
        def peer_of(k):
            return (1 - x if k & 4 else x, 1 - y if k & 2 else y, 1 - c if k & 1 else c)

        copies = []
        for k in range(1, N_DEV):
            peer = peer_of(k)
            for a in range(n):
                copies.append(pltpu.make_async_remote_copy(
                    src_ref=ins[a].at[_dev_index(peer)], dst_ref=outs[a].at[me],
                    send_sem=send_sems.at[a * 7 + k - 1], recv_sem=recv_sems.at[a * 7 + k - 1],
                    device_id=peer, device_id_type=MESH))
        for cp in copies:
            cp.start()
        for k in range(1, N_DEV):
            peer = peer_of(k)
            for a in range(n):
                pltpu.make_async_remote_copy(
                    src_ref=ins[a].at[me], dst_ref=outs[a].at[_dev_index(peer)],
                    send_sem=send_sems.at[a * 7 + k - 1], recv_sem=recv_sems.at[a * 7 + k - 1],
                    device_id=peer, device_id_type=MESH).wait_recv()
        for cp in copies:
            cp.wait_send()
        for cp in mine:
            cp.wait()

    any_spec = pl.BlockSpec(memory_space=pl.ANY)
    return pl.pallas_call(
        body,
        name=name,
        out_shape=[jax.ShapeDtypeStruct(b.shape, b.dtype) for b in bufs],
        in_specs=[any_spec] * n,
        out_specs=[any_spec] * n,
        scratch_shapes=[pltpu.SemaphoreType.DMA((7 * n,)), pltpu.SemaphoreType.DMA((7 * n,)),
                        pltpu.SemaphoreType.DMA((n,))],
        compiler_params=_cparams(),
    )(*bufs)


HBM_SPEC = pl.BlockSpec(memory_space=pltpu.HBM)
SEM_SPEC = pl.BlockSpec(memory_space=pltpu.SEMAPHORE)
EFFECT = pltpu.SideEffectType.DATAFLOW_SIDE_EFFECTING


def _peers(x, y, c):
    return [(1 - x if k & 4 else x, 1 - y if k & 2 else y, 1 - c if k & 1 else c) for k in range(1, N_DEV)]


def _push_peers(mode, x, y, c):
    if mode == "all":
        return _peers(x, y, c)
    return [(x, y, 1 - c), (1 - x, y, c), (x, 1 - y, c), (1 - x, 1 - y, c)]


def _push_start(groups, sliced, name, after=(), modes=None):
    flat = [b for g in groups for b in g]
    n, ng = len(flat), len(groups)
    sizes = [len(g) for g in groups]
    modes = modes or ["all"] * ng
    fan = [len(_push_peers(m, 0, 0, 0)) for m in modes]
    lands = [lax.empty(b.shape if sliced else (N_DEV, *b.shape), b.dtype) for b in flat]

    def body(*refs):
        ins, lnd = refs[:n], refs[n : 2 * n]
        sems = refs[2 * n + len(after) : 2 * n + len(after) + 2 * ng]
        token = refs[-1]
        x, y, c = _place()
        me = _dev_index((x, y, c))
        first = 0
        for gi, size in enumerate(sizes):
            for k, peer in enumerate(_push_peers(modes[gi], x, y, c)):
                for j in range(first, first + size):
                    sem = (j - first) * fan[gi] + k
                    pltpu.make_async_remote_copy(
                        src_ref=ins[j].at[_dev_index(peer)] if sliced else ins[j], dst_ref=lnd[j].at[me],
                        send_sem=sems[2 * gi].at[sem], recv_sem=sems[2 * gi + 1].at[sem],
                        device_id=peer, device_id_type=MESH).start()
            first += size
        token[...] = jnp.zeros_like(token)

    out_shape = []
    for size, width in zip(sizes, fan):
        out_shape += [pltpu.SemaphoreType.DMA((width * size,)), pltpu.SemaphoreType.DMA((width * size,))]
    out_shape += [pltpu.HBM(b.shape, b.dtype) for b in flat + lands]
    out_shape.append(jax.ShapeDtypeStruct((V7X_SUBLANES, V7X_LANES), F32))
    res = pl.pallas_call(
        body,
        name=name,
        out_shape=tuple(out_shape),
        in_specs=[HBM_SPEC] * (2 * n) + [ANY_SPEC] * len(after),
        out_specs=tuple([SEM_SPEC] * (2 * ng) + [HBM_SPEC] * (2 * n) + [pl.BlockSpec(memory_space=pltpu.VMEM)]),
        input_output_aliases={i: 2 * ng + i for i in range(2 * n)},
        compiler_params=pltpu.CompilerParams(has_side_effects=EFFECT),
    )(*[pltpu.with_memory_space_constraint(b, pltpu.HBM) for b in flat + lands], *after)
    sems, thru, token = res[: 2 * ng], res[2 * ng : 2 * ng + 2 * n], res[-1]
    out, first = [], 0
    for gi, size in enumerate(sizes):
        out.append((sems[2 * gi], sems[2 * gi + 1], list(thru[first : first + size]),
                    list(thru[n + first : n + first + size])))
        first += size
    return out, token


def _push_wait(started, sliced, after, name, mode="all"):
    send_sems, recv_sems, bufs, lands = started
    n = len(bufs)
    fan = len(_push_peers(mode, 0, 0, 0))

    def body(*refs):
        ins, lnd = refs[:n], refs[n : 2 * n]
        send_ref, recv_ref = refs[2 * n], refs[2 * n + 1]
        x, y, c = _place()
        for k, peer in enumerate(_push_peers(mode, x, y, c)):
            for j in range(n):
                cp = pltpu.make_async_remote_copy(
                    src_ref=ins[j].at[_dev_index(peer)] if sliced else ins[j], dst_ref=lnd[j].at[_dev_index(peer)],
                    send_sem=send_ref.at[j * fan + k], recv_sem=recv_ref.at[j * fan + k],
                    device_id=peer, device_id_type=MESH)
                cp.wait_send()
                cp.wait_recv()

    res = pl.pallas_call(
        body,
        name=name,
        out_shape=tuple(pltpu.HBM(b.shape, b.dtype) for b in bufs + lands),
        in_specs=[HBM_SPEC] * (2 * n) + [SEM_SPEC, SEM_SPEC, pl.BlockSpec(memory_space=pl.ANY)],
        out_specs=tuple([HBM_SPEC] * (2 * n)),
        input_output_aliases={i: i for i in range(2 * n)},
        compiler_params=pltpu.CompilerParams(has_side_effects=EFFECT),
    )(*bufs, *lands, send_sems, recv_sems, after)
    return list(res[:n]), list(res[n:])


def _forward_copies(lnd, send_ref, recv_ref, incoming):
    x, y, c = _place()
    copies = []
    for k, chip in enumerate([(1 - x, y), (x, 1 - y), (1 - x, 1 - y)]):
        mine, theirs = _dev_index((*chip, c)), _dev_index((*chip, 1 - c))
        for j, ref in enumerate(lnd):
            copies.append(pltpu.make_async_remote_copy(
                src_ref=ref.at[mine], dst_ref=ref.at[theirs if incoming else mine],
                send_sem=send_ref.at[j * 3 + k], recv_sem=recv_ref.at[j * 3 + k],
                device_id=(x, y, 1 - c), device_id_type=MESH))
    return copies


def _forward_start(lands, name):
    n = len(lands)

    def body(*refs):
        for cp in _forward_copies(refs[:n], refs[n], refs[n + 1], False):
            cp.start()

    res = pl.pallas_call(
        body,
        name=name,
        out_shape=(pltpu.SemaphoreType.DMA((3 * n,)), pltpu.SemaphoreType.DMA((3 * n,)),
                   *[pltpu.HBM(b.shape, b.dtype) for b in lands]),
        in_specs=[HBM_SPEC] * n,
        out_specs=(SEM_SPEC, SEM_SPEC, *[HBM_SPEC] * n),
        input_output_aliases={i: 2 + i for i in range(n)},
        compiler_params=pltpu.CompilerParams(has_side_effects=EFFECT),
    )(*[pltpu.with_memory_space_constraint(b, pltpu.HBM) for b in lands])
    return res[0], res[1], list(res[2:])


def _forward_wait(started, after, name):
    send_sems, recv_sems, lands = started
    n = len(lands)

    def body(*refs):
        for cp in _forward_copies(refs[:n], refs[n], refs[n + 1], True):
            cp.wait_send()
            cp.wait_recv()

    res = pl.pallas_call(
        body,
        name=name,
        out_shape=tuple(pltpu.HBM(b.shape, b.dtype) for b in lands),
        in_specs=[HBM_SPEC] * n + [SEM_SPEC, SEM_SPEC, pl.BlockSpec(memory_space=pl.ANY)],
        out_specs=tuple([HBM_SPEC] * n),
        input_output_aliases={i: i for i in range(n)},
        compiler_params=pltpu.CompilerParams(has_side_effects=EFFECT),
    )(*lands, send_sems, recv_sems, after)
    return list(res)


def _cols_full(g):
    return jnp.transpose(g, (1, 0, 2)).reshape(g.shape[1], -1)


def _rows_full(g):
    return g.reshape(-1, g.shape[2])


def _cols_parts(full, n=N_DEV):
    r = full.shape[0]
    return jnp.transpose(full.reshape(r, n, -1), (1, 0, 2)).astype(BF16)


def _rows_parts(full):
    return full.reshape(N_DEV, -1, full.shape[1]).astype(BF16)


def _block_diag(w):
    eye = jnp.eye(LRU_BLOCKS, dtype=w.dtype)
    return jnp.einsum("nkj,nm->nkmj", w, eye).reshape(LRU_WIDTH, LRU_WIDTH)


def _pad_rows(v, rows):
    flat = v.reshape(-1)
    return jnp.pad(flat, (0, rows * D_MODEL - flat.shape[0])).reshape(rows, D_MODEL)


def _my_cols(full, me, width):
    return lax.dynamic_slice_in_dim(full, me * width, width, axis=full.ndim - 1)


def kernel(x, c, w_ada, b_ada, norm_pre, norm_post, ffn1_w_gu, ffn1_w_down, w_in, rel_bias, conv_w, conv_b, lru_wa, lru_ba, lru_wx, lru_bx, lru_lambda, w_att_o, w_rec_o, w_out, ffn2_w_gu, ffn2_w_down, loss_target, m_w_ada, m_b_ada, m_norm_pre, m_norm_post, m_ffn1_w_gu, m_ffn1_w_down, m_w_in, m_rel_bias, m_conv_w, m_conv_b, m_lru_wa, m_lru_ba, m_lru_wx, m_lru_bx, m_lru_lambda, m_w_att_o, m_w_rec_o, m_w_out, m_ffn2_w_gu, m_ffn2_w_down, v_w_ada, v_b_ada, v_norm_pre, v_norm_post, v_ffn1_w_gu, v_ffn1_w_down, v_w_in, v_rel_bias, v_conv_w, v_conv_b, v_lru_wa, v_lru_ba, v_lru_wx, v_lru_bx, v_lru_lambda, v_w_att_o, v_w_rec_o, v_w_out, v_ffn2_w_gu, v_ffn2_w_down):
    weights = dict(w_ada=w_ada, b_ada=b_ada, norm_pre=norm_pre, norm_post=norm_post, ffn1_w_gu=ffn1_w_gu,
                   ffn1_w_down=ffn1_w_down, w_in=w_in, rel_bias=rel_bias, conv_w=conv_w, conv_b=conv_b,
                   lru_wa=lru_wa, lru_ba=lru_ba, lru_wx=lru_wx, lru_bx=lru_bx, lru_lambda=lru_lambda,
                   w_att_o=w_att_o, w_rec_o=w_rec_o, w_out=w_out, ffn2_w_gu=ffn2_w_gu, ffn2_w_down=ffn2_w_down)
    mom1 = dict(w_ada=m_w_ada, b_ada=m_b_ada, norm_pre=m_norm_pre, norm_post=m_norm_post, ffn1_w_gu=m_ffn1_w_gu,
                ffn1_w_down=m_ffn1_w_down, w_in=m_w_in, rel_bias=m_rel_bias, conv_w=m_conv_w, conv_b=m_conv_b,
                lru_wa=m_lru_wa, lru_ba=m_lru_ba, lru_wx=m_lru_wx, lru_bx=m_lru_bx, lru_lambda=m_lru_lambda,
                w_att_o=m_w_att_o, w_rec_o=m_w_rec_o, w_out=m_w_out, ffn2_w_gu=m_ffn2_w_gu, ffn2_w_down=m_ffn2_w_down)
    mom2 = dict(w_ada=v_w_ada, b_ada=v_b_ada, norm_pre=v_norm_pre, norm_post=v_norm_post, ffn1_w_gu=v_ffn1_w_gu,
                ffn1_w_down=v_ffn1_w_down, w_in=v_w_in, rel_bias=v_rel_bias, conv_w=v_conv_w, conv_b=v_conv_b,
                lru_wa=v_lru_wa, lru_ba=v_lru_ba, lru_wx=v_lru_wx, lru_bx=v_lru_bx, lru_lambda=v_lru_lambda,
                w_att_o=v_w_att_o, w_rec_o=v_w_rec_o, w_out=v_w_out, ffn2_w_gu=v_ffn2_w_gu, ffn2_w_down=v_ffn2_w_down)
    order = list(weights)
    big = ["ffn1_w_gu", "ffn1_w_down", "w_in", "w_att_o", "w_rec_o", "w_out", "ffn2_w_gu", "ffn2_w_down"]
    col_sharded = {"ffn1_w_gu", "w_in", "w_att_o", "ffn2_w_gu"}
    small = ["b_ada", "norm_pre", "norm_post", "rel_bias", "conv_w", "conv_b", "lru_wa", "lru_ba", "lru_wx",
             "lru_bx", "lru_lambda"]

    xi, yi, ci = _place()
    me = _dev_index((xi, yi, ci))
    x0 = x[0]
    target = loss_target[0]
    fuse_tm = min(FUSE_TM, x0.shape[0])

    transposed = {"ffn1_w_gu", "w_in", "ffn2_w_gu"}
    local = lambda n, arr: jnp.transpose(arr[0]) if n in transposed else arr[0]
    shards = {n: local(n, weights[n]).astype(BF16) for n in big}
    full_of = lambda n, g: _cols_full(g) if n == "w_att_o" else _rows_full(g)

    pack = jnp.concatenate([c.reshape(-1), norm_pre.reshape(-1), norm_post.reshape(-1), conv_w.reshape(-1)])
    pack = jnp.pad(pack, (0, 3072 - pack.shape[0])).reshape(8, 384)
    got = _allgather_vmem(pack, "gather_small_inputs").reshape(N_DEV, 3072)
    c_all = got[:, :1024]
    unshard = lambda blk, rows: jnp.transpose(blk.reshape(N_DEV, rows, 128), (1, 0, 2)).reshape(rows, D_MODEL)
    g_pre = unshard(got[:, 1024:1408], 3)
    g_post = unshard(got[:, 1408:1792], 3)
    conv_taps = unshard(got[:, 1792:2304], 4)
    conv_w8 = jnp.concatenate([conv_taps, jnp.zeros((4, LRU_WIDTH), F32)], axis=0)

    mod_cols = _ada_fwd(c_all, w_ada[0], "ada_fwd")
    mod_all = _allgather_vmem(mod_cols, "gather_mod").reshape(N_DEV, N_DEV, 1152)
    mod = lax.dynamic_index_in_dim(mod_all, me, axis=1, keepdims=False).reshape(1, -1) + b_ada
    mod = mod.reshape(3, 3, 1, D_MODEL)

    w_slab = _slab_weights(lru_wa[0], lru_wx[0])
    bias = _bias_tile(rel_bias[0])

    res_w = (0.5, 1.0, 0.5)
    row = lambda v: v.reshape(1, -1)

    (w1_gu,) = _allgather_hbm([shards["ffn1_w_gu"]], "gather_ffn1_w_gu")
    weight_groups = [["ffn1_w_down"], ["w_in"], ["w_att_o", "w_rec_o", "w_out"], ["ffn2_w_gu", "ffn2_w_down"]]
    weight_modes = ["all", "chip", "all", "all"]
    weights_started, started = _push_start([[shards[n] for n in g] for g in weight_groups], False,
                                           "gather_weights_start", after=(mod, w1_gu), modes=weight_modes)
    full = {"ffn1_w_gu": _rows_full(w1_gu)}

    def gathered_group(gi, after):
        sent, lands = _push_wait(weights_started[gi], False, after, f"gather_weights_wait{gi}", mode=weight_modes[gi])
        if weight_modes[gi] == "chip":
            lands = _forward_wait(_forward_start(lands, f"gather_weights_forward{gi}"), sent[0],
                                  f"gather_weights_forward_wait{gi}")
        for n, own, land in zip(weight_groups[gi], sent, lands):
            full[n] = full_of(n, jnp.where(is_me, own[None], land))

    is_me = (jnp.arange(N_DEV) == me)[:, None, None]

    def ffn_fwd(xin, k, gi, tag, deps=(), target=None):
        h, a, g, u = _pre_up(xin, row(g_pre[k]), mod[k, 0], mod[k, 1], full[f"{tag}_w_gu"], f"{tag}_up", deps=deps)
        if f"{tag}_w_down" not in full:
            gathered_group(gi, a)
        f, *out = _matmul_post(a, full[f"{tag}_w_down"], xin, row(g_post[k]), mod[k, 2], res_w[k], f"{tag}_down",
                               target=target)
        return (out[0] if target is None else out), (h, g, u, a, f)

    x1, saved1 = ffn_fwd(x0, 0, 0, "ffn1", deps=(started,))

    gathered_group(1, x1)
    h2, proj = _pre_matmul(x1, row(g_pre[1]), mod[1, 0], mod[1, 1], full["w_in"], "mix_in",
                           b_shift=3 * ATT_WIDTH // 512)
    att_o = _attn_fwd(proj, bias, "attn_fwd")
    gathered_group(2, att_o)
    xc, pre, a_t, u_t = _lru_front(proj, conv_w8, conv_b, w_slab, lru_ba, lru_bx, lru_lambda, "lru_front")
    hs, h_prev, rec_in = _scan_fwd(a_t, u_t, proj, "lru_scan")
    att = _matmul(att_o, full["w_att_o"], "nn", F32, "att_out")
    rec = _matmul(rec_in, full["w_rec_o"], "nn", F32, "rec_out")
    merged, f2, x2 = _merge_matmul_post(att, rec, proj, full["w_out"], x1, row(g_post[1]), mod[1, 2], res_w[1],
                                        "mix_out")

    gathered_group(3, x2)
    (dy, sq), saved3 = ffn_fwd(x2, 2, 2, "ffn2", target=target)
    loss = lax.psum(0.5 * jnp.sum(sq) / D_MODEL, ("x", "y", "c"))

    grads = {}
    norm_sums = [None] * 6

    pending = []

    def exchange_start(names, tag, after=()):
        send = [(_cols_parts if n == "w_att_o" else _rows_parts)(grads[n]) for n in names]
        (group,), token = _push_start([send], True, f"exchange_{tag}_start", after=after)
        pending.append((names, send, group, tag))
        return token

    def exchange_finish(names, send, group, tag, after):
        sent, lands = _push_wait(group, True, after, f"exchange_{tag}_wait")
        res = None
        for n, land, mine in zip(names, lands, sent):
            res = _adamw_parts(land, mine, me, local(n, weights[n]), local(n, mom1[n]), local(n, mom2[n]),
                               f"adamw_{n}")
            back = (lambda r: jnp.transpose(r)) if n in transposed else (lambda r: r)
            out_g[n], out_d[n], out_m[n], out_v[n] = [back(r).reshape(weights[n].shape) for r in res]
        return res[0]

    out_g, out_d, out_m, out_v = {}, {}, {}, {}

    def ffn_bwd(xin, k, saved, dout, tag):
        h, g, u, a, f = saved
        w_gu, w_down = f"{tag}_w_gu", f"{tag}_w_down"
        trial = 2 if k == 2 and xin.shape[0] >= 2048 else 1
        df, dgu, norm_sums[2 * k + 1] = _post_bwd_up_bwd(f, dout, row(g_post[k]), mod[k, 2], res_w[k], full[w_down],
                                                          g, u, f"{tag}_up_bwd")
        grads[w_down] = _matmul(a, df, "tn", BF16, f"{tag}_dw_down", tm=1408, tn=1024, tk=1024 * trial)
        started = exchange_start([w_down], w_down)
        grads[w_gu] = _dw_gu(dgu, h, f"{tag}_dw_gu", deps=(started,), tk=1024 * trial)
        started = exchange_start([w_gu], w_gu)
        halves = [(dgu, (None, fuse_tm * trial, D_FF), lambda i, half=half: (half, i, 0),
                   (half * D_FF, (half + 1) * D_FF))
                  for half in range(2)]
        dx, norm_sums[2 * k] = _matmul_pre_bwd(halves, full[w_gu], xin, dout, row(g_pre[k]),
                                                                mod[k, 0], mod[k, 1], f"{tag}_dh", deps=(started,))
        return dx

    dx2 = ffn_bwd(x2, 2, saved3, dy, "ffn2")

    df2, datt, drec, dg_att, dg_rec, norm_sums[3] = _post_bwd_merge_bwd(
        f2, dx2, row(g_post[1]), mod[1, 2], res_w[1], full["w_out"], att, rec, proj, "mix_dmerged")
    grads["w_out"] = _matmul(merged, df2, "tn", BF16, "mix_dw_out", tm=1024, tn=1024, tk=1024)
    datt_o = _matmul(datt, full["w_att_o"], "nt", BF16, "att_out_bwd")
    grads["w_att_o"] = _matmul(att_o, datt, "tn", BF16, "dw_att_o", tm=512, tn=1024, tk=1024)
    grads["w_rec_o"] = _matmul(rec_in, drec, "tn", BF16, "dw_rec_o", tm=1024, tn=1024, tk=1024)
    started = exchange_start(["w_out", "w_att_o", "w_rec_o"], "mix_out")
    dhs, dyr = _matmul_recin_bwd(drec, full["w_rec_o"], hs, proj, "rec_out_bwd", deps=(started,))
    g_t = _scan_bwd(a_t, dhs, "lru_scan_bwd")
    dpre, dxc, lru_sums = _lru_back(pre, xc, w_slab, lru_ba, lru_bx, lru_lambda, g_t, h_prev, "lru_back")
    dxr, conv_sums = _conv_bwd(proj, conv_w8, dxc, "conv_bwd")
    dq, dk, dv, dbias = _attn_bwd(proj, bias, datt_o, "attn_bwd")
    dproj = jnp.concatenate([dq, dk, dv, dxr, dyr, dg_att, dg_rec], axis=1)
    grads["w_in"] = _matmul(dproj, h2, "tn", BF16, "mix_dw_in", tm=1408, tn=1024, tk=1024)
    pack_mix = jnp.concatenate([conv_sums, lru_sums, _pad_rows(_bias_grad(dbias), V7X_SUBLANES),
                                _lru_dw(xc, dpre, "lru_dw").reshape(128, D_MODEL)], axis=0)
    (mix_started,), started = _push_start([[pack_mix]], False, "small_grads_mix_start")
    started = exchange_start(["w_in"], "w_in", after=(started,))
    whole = [(dproj, (fuse_tm, PROJ_WIDTH), lambda i: (i, 0), (0, PROJ_WIDTH))]
    dx1, norm_sums[2] = _matmul_pre_bwd(whole, full["w_in"], x1, dx2, row(g_pre[1]), mod[1, 0],
                                                             mod[1, 1], "mix_dh", deps=(started,))

    dx0 = ffn_bwd(x0, 0, saved1, dx1, "ffn1")

    pack_norm = jnp.concatenate(norm_sums, axis=0)
    (norm_started,), _ = _push_start([[pack_norm]], False, "small_grads_norm_start")

    def summed(started, pack, after, tag):
        (own,), (land,) = _push_wait(started, False, after, f"small_grads_{tag}_wait")
        parts = jnp.where(is_me, own[None], land)
        return parts, _sum_parts(parts, f"small_grads_{tag}_sum")

    done = dx0
    last = [p for p in pending if p[3].startswith("ffn1")]
    for names, send, group, tag in pending:
        if not tag.startswith("ffn1"):
            done = exchange_finish(names, send, group, tag, done)

    _, total = summed(mix_started, pack_mix, done, "mix")
    grads["conv_w"] = _my_cols(total[0:4], me, 128)
    grads["conv_b"] = total[4:5]
    grads["lru_ba"] = total[8:9]
    grads["lru_bx"] = total[9:10]
    grads["lru_lambda"] = total[10:11]
    grads["rel_bias"] = total[16:19].reshape(-1)[: ATT_HEADS * (2 * MAX_REL + 1)].reshape(ATT_HEADS, -1)
    grads["lru_wa"] = total[24:88].reshape(LRU_BLOCKS, LRU_BLOCK, LRU_BLOCK)
    grads["lru_wx"] = total[88:152].reshape(LRU_BLOCKS, LRU_BLOCK, LRU_BLOCK)
    parts, total = summed(norm_started, pack_norm, total, "norm")
    by_sandwich = lambda v: v.reshape(*v.shape[:-2], 3, 2 * V7X_SUBLANES, D_MODEL)
    dmod_of = lambda v: jnp.concatenate([by_sandwich(v)[..., 1:3, :], by_sandwich(v)[..., 9:10, :]], axis=-2)
    grads["b_ada"] = dmod_of(total).reshape(1, -1)
    grads["norm_pre"] = _my_cols(by_sandwich(total)[:, 0, :], me, 128)
    grads["norm_post"] = _my_cols(by_sandwich(total)[:, V7X_SUBLANES, :], me, 128)
    dmod_all = dmod_of(parts).reshape(N_DEV, 9 * D_MODEL)
    grads["w_ada"] = _ada_bwd(c_all, _my_cols(dmod_all, me, 1152), "ada_bwd")

    res = _adamw(grads["w_ada"], w_ada[0], m_w_ada[0], v_w_ada[0], "adamw_w_ada")
    out_g["w_ada"], out_d["w_ada"], out_m["w_ada"], out_v["w_ada"] = [r.reshape(w_ada.shape) for r in res]

    sizes = [int(np.prod(weights[n].shape)) for n in small]
    tot = sum(sizes)
    rows_small = -(-tot // (16 * D_MODEL)) * 16
    flat = lambda arrs: jnp.pad(jnp.concatenate([a.reshape(-1) for a in arrs]),
                                (0, rows_small * D_MODEL - tot)).reshape(rows_small, D_MODEL)
    res = _adamw(flat([grads[n] for n in small]), flat([weights[n] for n in small]),
                 flat([mom1[n] for n in small]), flat([mom2[n] for n in small]), "adamw_small", rows=rows_small)
    offs = np.cumsum([0] + sizes)
    for dst, r in zip((out_g, out_d, out_m, out_v), res):
        rf = r.reshape(-1)
        for i, n in enumerate(small):
            dst[n] = rf[offs[i] : offs[i + 1]].reshape(weights[n].shape)

    done = res[0]
    for names, send, group, tag in last:
        done = exchange_finish(names, send, group, tag, done)

    return (loss, dx0[None], *[out_g[n] for n in order], *[out_d[n] for n in order],
            *[out_m[n] for n in order], *[out_v[n] for n in order])
```

```python
import functools

import jax
import jax.numpy as jnp
import numpy as np
from jax import lax
from jax.experimental import pallas as pl
from jax.experimental.pallas import tpu as pltpu

D_MODEL = 1024
D_FF = 2816
ATT_HEADS = 8
ATT_HEAD_DIM = 64
ATT_WIDTH = 512
CHUNK = 64
LEFT_CHUNKS = 8
MAX_REL = 128
LRU_WIDTH = 1024
LRU_BLOCKS = 16
LRU_BLOCK = 64
LRU_C = 8.0
EPS = 1e-6
PROJ_WIDTH = 5632
N_DEV = 8

ADAM_LR = 0.001
ADAM_B1 = 0.9
ADAM_B2 = 0.999
ADAM_EPS = 1e-08
ADAM_WD = 0.01
ADAM_STEP = 10

V7X_LANES = 128
V7X_SUBLANES = 8
V7X_VMEM_BYTES = 64 * 1024 * 1024
VMEM_LIMIT = V7X_VMEM_BYTES - 8 * 1024 * 1024

ATT_TQ = 256
NEG = -1e30
BF16 = jnp.bfloat16
F32 = jnp.float32
MESH = pl.DeviceIdType.MESH

OFF_Q = 4 * LRU_WIDTH
OFF_K = OFF_Q + ATT_WIDTH
OFF_V = OFF_K + ATT_WIDTH


def _cparams(**kw):
    return pltpu.CompilerParams(vmem_limit_bytes=VMEM_LIMIT, **kw)


def _pick(n, target, unit=V7X_LANES):
    best = None
    for t in range(unit, min(n, target) + 1, unit):
        if n % t == 0:
            best = t
    return n if best is None else best


_DIMS = {
    "nn": (((1,), (0,)), ((), ())),
    "nt": (((1,), (1,)), ((), ())),
    "tn": (((0,), (0,)), ((), ())),
}


ANY_SPEC = pl.BlockSpec(memory_space=pl.ANY)


def _matmul(a, b, mode, out_dtype, name, tm=1024, tn=512, tk=1408, deps=(), b_shift=0):
    n_deps = len(deps)
    if mode == "nn":
        (m, k), (k2, n) = a.shape, b.shape
    elif mode == "nt":
        (m, k), (n, k2) = a.shape, b.shape
    else:
        (k, m), (k2, n) = a.shape, b.shape
    assert k == k2, (a.shape, b.shape, mode)
    tm, tn, tk = _pick(m, tm), _pick(n, tn), _pick(k, tk)
    nk = k // tk
    dims = _DIMS[mode]

    def body(a_ref, b_ref, *rest):
        o_ref, scratch = rest[n_deps], rest[n_deps + 1 :]
        p = lax.dot_general(a_ref[...], b_ref[...], dims, preferred_element_type=F32)
        if nk == 1:
            o_ref[...] = p.astype(o_ref.dtype)
        else:
            acc = scratch[0]
            kk = pl.program_id(2)

            @pl.when(kk == 0)
            def _():
                acc[...] = p

            @pl.when(kk > 0)
            def _():
                acc[...] += p

            @pl.when(kk == nk - 1)
            def _():
                o_ref[...] = acc[...].astype(o_ref.dtype)

    if mode == "nn":
        a_spec = pl.BlockSpec((tm, tk), lambda i, j, kk: (i, kk))
        b_spec = pl.BlockSpec((tk, tn), lambda i, j, kk: (kk, j))
    elif mode == "nt":
        a_spec = pl.BlockSpec((tm, tk), lambda i, j, kk: (i, kk))
        b_spec = pl.BlockSpec((tn, tk), lambda i, j, kk: ((j + b_shift) % (n // tn), kk))
    else:
        a_spec = pl.BlockSpec((tk, tm), lambda i, j, kk: (kk, i))
        b_spec = pl.BlockSpec((tk, tn), lambda i, j, kk: (kk, j))
    return pl.pallas_call(
        body,
        name=name,
        grid=(m // tm, n // tn, nk),
        in_specs=[a_spec, b_spec] + [ANY_SPEC] * n_deps,
        out_specs=pl.BlockSpec((tm, tn), lambda i, j, kk: (i, j)),
        out_shape=jax.ShapeDtypeStruct((m, n), out_dtype),
        scratch_shapes=[pltpu.VMEM((tm, tn), F32)] if nk > 1 else [],
        compiler_params=_cparams(dimension_semantics=("parallel", "parallel", "arbitrary")),
    )(a, b, *deps)


def _rowwise(fn, name, params, tiles, outs, accs=(), ts=256, with_index=False, deps=()):
    norm = []
    for t in tiles:
        if not isinstance(t, tuple):
            t = (t, t.shape[1], 0)
        norm.append(t if len(t) == 4 else (*t, None))
    s = norm[0][0].shape[0]
    ts = min(ts, s)
    assert s % ts == 0 and ts % V7X_SUBLANES == 0
    steps = s // ts
    halo_blocks = ts // V7X_SUBLANES
    n_p, n_t, n_o = len(params), len(norm), len(outs)

    def body(*refs):
        i = pl.program_id(0)
        vals = [r[...] for r in refs[: n_p + n_t]]
        res = fn(i, steps, *vals) if with_index else fn(*vals)
        if not isinstance(res, (tuple, list)):
            res = (res,)
        first_out = n_p + n_t + len(deps)
        o_refs = refs[first_out : first_out + n_o]
        a_refs = refs[first_out + n_o :]
        for r, v in zip(o_refs, res[:n_o]):
            r[...] = v.astype(r.dtype)
        for r, v in zip(a_refs, res[n_o:]):
            _accumulate(r, v, i)

    in_specs = [pl.BlockSpec(p.shape, lambda i: (0, 0)) for p in params]
    for arr, w, cb, halo in norm:
        if halo is None:
            in_specs.append(pl.BlockSpec((ts, w), lambda i, cb=cb: (i, cb)))
        elif halo == "prev":
            in_specs.append(
                pl.BlockSpec((V7X_SUBLANES, w), lambda i, cb=cb: (jnp.maximum(i * halo_blocks - 1, 0), cb))
            )
        else:
            last = s // V7X_SUBLANES - 1
            in_specs.append(
                pl.BlockSpec((V7X_SUBLANES, w), lambda i, cb=cb: (jnp.minimum((i + 1) * halo_blocks, last), cb))
            )
    in_specs += [ANY_SPEC] * len(deps)
    out_specs = [pl.BlockSpec((ts, w), lambda i: (i, 0)) for w, _ in outs]
    out_specs += [pl.BlockSpec(shape, lambda i: (0, 0)) for shape in accs]
    out_shape = [jax.ShapeDtypeStruct((s, w), dt) for w, dt in outs]
    out_shape += [jax.ShapeDtypeStruct(shape, F32) for shape in accs]
    res = pl.pallas_call(
        body,
        name=name,
        grid=(steps,),
        in_specs=in_specs,
        out_specs=out_specs,
        out_shape=out_shape,
        compiler_params=_cparams(dimension_semantics=("arbitrary",)),
    )(*params, *[t[0] for t in norm], *deps)
    return res


def _accumulate(ref, val, step):
    @pl.when(step == 0)
    def _():
        ref[...] = val

    @pl.when(step > 0)
    def _():
        ref[...] += val


def _sigmoid(z):
    return jax.nn.sigmoid(z)


def _silu(z):
    return z * _sigmoid(z)


def _gelu(z):
    return 0.5 * z * (1.0 + jnp.tanh(0.7978845608028654 * (z + 0.044715 * (z * z * z))))


def _pre_fn(g, shift, scale, x):
    r = lax.rsqrt(jnp.mean(x * x, axis=-1, keepdims=True) + EPS)
    return ((x * r) * g) * (1.0 + scale) + shift


def _post_fn(res_w, g, gate, f, x):
    r = lax.rsqrt(jnp.mean(f * f, axis=-1, keepdims=True) + EPS)
    return x + (res_w * gate) * ((f * r) * g)


def _swiglu_fn(gu):
    return _silu(gu[:, :D_FF]) * gu[:, D_FF:]


def _gates_fn(ba, bx, lam, pre, xc):
    ra = _sigmoid(pre[:, :LRU_WIDTH] + ba)
    ia = _sigmoid(pre[:, LRU_WIDTH:] + bx)
    softplus = jnp.maximum(-lam, 0.0) + jnp.log1p(jnp.exp(-jnp.abs(lam)))
    log_a = (-LRU_C) * ra * softplus
    a = jnp.exp(log_a)
    mult = jnp.sqrt(-jnp.tanh(log_a) * (a * a + 1.0))
    return a, mult * (ia * xc)


def _recin_fn(hs, yr):
    return hs * _gelu(yr)


def _merge_fn(att, rec, g_att, g_rec):
    return _sigmoid(g_att) * att + _sigmoid(g_rec) * rec


def _rowsum(v):
    return jnp.sum(v, axis=0, keepdims=True)


def _pre_fwd(x, g, shift, scale, name, deps=()):
    (h,) = _rowwise(_pre_fn, name, [g, shift, scale], [x], [(D_MODEL, BF16)], deps=deps)
    return h


def _pre_bwd(x, g, shift, scale, dh, dres, name):
    def fn(g, shift, scale, x, dh, dres):
        _, vjp = jax.vjp(_pre_fn, g, shift, scale, x)
        dg, dshift, dscale, dx = vjp(dh)
        return dx + dres, dg, dshift, dscale

    row = (1, D_MODEL)
    return _rowwise(fn, name, [g, shift, scale], [x, dh, dres], [(D_MODEL, F32)], [row, row, row])


def _post_fwd(f, x, g, gate, res_w, name):
    (y,) = _rowwise(functools.partial(_post_fn, res_w), name, [g, gate], [f, x], [(D_MODEL, F32)])
    return y


def _post_bwd(f, g, gate, res_w, dy, name, deps=()):
    def fn(g, gate, f, dy):
        _, vjp = jax.vjp(lambda g, gate, f: _post_fn(res_w, g, gate, f, 0.0), g, gate, f)
        dg, dgate, df = vjp(dy)
        return df, dg, dgate

    row = (1, D_MODEL)
    return _rowwise(fn, name, [g, gate], [f, dy], [(D_MODEL, BF16)], [row, row], deps=deps)


def _loss_stage(y, target, name):
    def fn(y, t):
        diff = y - t
        return diff * (1.0 / D_MODEL), _rowsum(diff * diff)

    return _rowwise(fn, name, [], [y, target], [(D_MODEL, F32)], [(1, D_MODEL)])


FFN_TM = 512
FFN_TF = 1408


def _glu_fn(g, u):
    return _silu(g) * u


def _ffn_up(h, w_gu_t, name):
    s = h.shape[0]
    tm = min(FFN_TM, s)
    nf = D_FF // FFN_TF

    def body(h_ref, wg_ref, wu_ref, a_ref, g_ref, u_ref):
        hv = h_ref[...]
        g = lax.dot_general(hv, wg_ref[...], _DIMS["nt"], preferred_element_type=F32)
        u = lax.dot_general(hv, wu_ref[...], _DIMS["nt"], preferred_element_type=F32)
        a_ref[...] = _glu_fn(g, u).astype(a_ref.dtype)
        g_ref[...] = g.astype(g_ref.dtype)
        u_ref[...] = u.astype(u_ref.dtype)

    out = pl.BlockSpec((tm, FFN_TF), lambda i, j: (i, j))
    return pl.pallas_call(
        body,
        name=name,
        grid=(s // tm, nf),
        in_specs=[pl.BlockSpec((tm, D_MODEL), lambda i, j: (i, 0)),
                  pl.BlockSpec((FFN_TF, D_MODEL), lambda i, j: (j, 0)),
                  pl.BlockSpec((FFN_TF, D_MODEL), lambda i, j: (nf + j, 0))],
        out_specs=[out, out, out],
        out_shape=[jax.ShapeDtypeStruct((s, D_FF), BF16)] * 3,
        compiler_params=_cparams(dimension_semantics=("parallel", "arbitrary")),
    )(h, w_gu_t, w_gu_t)


def _ffn_up_bwd(df, w_down, g, u, name, deps=()):
    s = df.shape[0]
    tm = min(FFN_TM, s)

    def body(df_ref, wd_ref, g_ref, u_ref, *rest):
        dg_ref, du_ref = rest[len(deps) :]
        da = lax.dot_general(df_ref[...], wd_ref[...], _DIMS["nt"], preferred_element_type=F32)
        _, vjp = jax.vjp(_glu_fn, g_ref[...].astype(F32), u_ref[...].astype(F32))
        dg, du = vjp(da)
        dg_ref[...] = dg.astype(dg_ref.dtype)
        du_ref[...] = du.astype(du_ref.dtype)

    blk = pl.BlockSpec((tm, FFN_TF), lambda i, j: (i, j))
    return pl.pallas_call(
        body,
        name=name,
        grid=(s // tm, D_FF // FFN_TF),
        in_specs=[pl.BlockSpec((tm, D_MODEL), lambda i, j: (i, 0)),
                  pl.BlockSpec((FFN_TF, D_MODEL), lambda i, j: (j, 0)), blk, blk] + [ANY_SPEC] * len(deps),
        out_specs=[blk, blk],
        out_shape=[jax.ShapeDtypeStruct((s, D_FF), BF16)] * 2,
        compiler_params=_cparams(dimension_semantics=("parallel", "arbitrary")),
    )(df, w_down, g, u, *deps)


def _ffn_dh(dg, du, w_gu_t, name, deps=()):
    s = dg.shape[0]
    tm, tn = min(FFN_TM, s), 512

    def body(dg_ref, du_ref, wg_ref, wu_ref, *rest):
        o_ref = rest[len(deps)]
        p = jnp.dot(dg_ref[...], wg_ref[...], preferred_element_type=F32)
        o_ref[...] = p + jnp.dot(du_ref[...], wu_ref[...], preferred_element_type=F32)

    a_spec = pl.BlockSpec((tm, D_FF), lambda i, j: (i, 0))
    return pl.pallas_call(
        body,
        name=name,
        grid=(s // tm, D_MODEL // tn),
        in_specs=[a_spec, a_spec,
                  pl.BlockSpec((D_FF, tn), lambda i, j: (0, j)),
                  pl.BlockSpec((D_FF, tn), lambda i, j: (1, j))] + [ANY_SPEC] * len(deps),
        out_specs=pl.BlockSpec((tm, tn), lambda i, j: (i, j)),
        out_shape=jax.ShapeDtypeStruct((s, D_MODEL), F32),
        compiler_params=_cparams(dimension_semantics=("parallel", "arbitrary")),
    )(dg, du, w_gu_t, w_gu_t, *deps)


FUSE_TM = 256
DW_TK = 2048
FFN_CHUNKS = (0, 512, 1024, FFN_TF)
ROW_SPEC2 = pl.BlockSpec((1, D_MODEL), lambda i, j: (0, 0))
ROW_SPEC1 = pl.BlockSpec((1, D_MODEL), lambda i: (0, 0))
SUMS_SPEC1 = pl.BlockSpec((V7X_SUBLANES, D_MODEL), lambda i: (0, 0))
SUMS_SPEC2 = pl.BlockSpec((V7X_SUBLANES, D_MODEL), lambda i, j: (0, 0))
SUMS_SHAPE = jax.ShapeDtypeStruct((V7X_SUBLANES, D_MODEL), F32)


def _sum_rows(*rows):
    pad = jnp.zeros((V7X_SUBLANES - len(rows), rows[0].shape[1]), F32)
    return jnp.concatenate([*rows, pad], axis=0)


def _pre_up(x, g, shift, scale, w_gu_t, name, deps=()):
    s = x.shape[0]
    tm = min(FFN_TM, s)
    nf = D_FF // FFN_TF
    nd = len(deps)

    def body(x_ref, g_ref, sh_ref, sc_ref, wg_ref, wu_ref, *rest):
        h_ref, a_ref, gg_ref, u_ref, h_s = rest[nd:]

        @pl.when(pl.program_id(1) == 0)
        def _():
            h = _pre_fn(g_ref[...], sh_ref[...], sc_ref[...], x_ref[...]).astype(BF16)
            h_s[...] = h
            h_ref[...] = h

        hv = h_s[...]
        gv = lax.dot_general(hv, wg_ref[...], _DIMS["nt"], preferred_element_type=F32)
        uv = lax.dot_general(hv, wu_ref[...], _DIMS["nt"], preferred_element_type=F32)
        a_ref[...] = _glu_fn(gv, uv).astype(a_ref.dtype)
        gg_ref[...] = gv.astype(gg_ref.dtype)
        u_ref[...] = uv.astype(u_ref.dtype)

    rows = pl.BlockSpec((tm, D_MODEL), lambda i, j: (i, 0))
    out = pl.BlockSpec((tm, FFN_TF), lambda i, j: (i, j))
    return pl.pallas_call(
        body,
        name=name,
        grid=(s // tm, nf),
        in_specs=[rows, ROW_SPEC2, ROW_SPEC2, ROW_SPEC2,
                  pl.BlockSpec((FFN_TF, D_MODEL), lambda i, j: (j, 0)),
                  pl.BlockSpec((FFN_TF, D_MODEL), lambda i, j: (nf + j, 0))] + [ANY_SPEC] * nd,
        out_specs=[rows, out, out, out],
        out_shape=[jax.ShapeDtypeStruct((s, D_MODEL), BF16)] + [jax.ShapeDtypeStruct((s, D_FF), BF16)] * 3,
        scratch_shapes=[pltpu.VMEM((tm, D_MODEL), BF16)],
        compiler_params=_cparams(dimension_semantics=("parallel", "arbitrary")),
    )(x, g, shift, scale, w_gu_t, w_gu_t, *deps)


def _pre_matmul(x, g, shift, scale, w_t, name, b_shift=0, tn=512):
    s = x.shape[0]
    n = w_t.shape[0]
    tm = min(2 * FFN_TM, s)

    def body(x_ref, g_ref, sh_ref, sc_ref, w_ref, h_ref, o_ref, h_s):
        @pl.when(pl.program_id(1) == 0)
        def _():
            h = _pre_fn(g_ref[...], sh_ref[...], sc_ref[...], x_ref[...]).astype(BF16)
            h_s[...] = h
            h_ref[...] = h

        o_ref[...] = lax.dot_general(h_s[...], w_ref[...], _DIMS["nt"], preferred_element_type=F32)

    rows = pl.BlockSpec((tm, D_MODEL), lambda i, j: (i, 0))
    return pl.pallas_call(
        body,
        name=name,
        grid=(s // tm, n // tn),
        in_specs=[rows, ROW_SPEC2, ROW_SPEC2, ROW_SPEC2,
                  pl.BlockSpec((tn, D_MODEL), lambda i, j: ((j + b_shift) % (n // tn), 0))],
        out_specs=[rows, pl.BlockSpec((tm, tn), lambda i, j: (i, j))],
        out_shape=[jax.ShapeDtypeStruct((s, D_MODEL), BF16), jax.ShapeDtypeStruct((s, n), F32)],
        scratch_shapes=[pltpu.VMEM((tm, D_MODEL), BF16)],
        compiler_params=_cparams(dimension_semantics=("parallel", "arbitrary")),
    )(x, g, shift, scale, w_t)


def _matmul_post(a, w, x, g_post, gate, res_w, name, target=None):
    s, k = a.shape
    tm = min(FFN_TM, s)
    extra = [] if target is None else [target]

    def body(a_ref, w_ref, x_ref, g_ref, gate_ref, *rest):
        f = jnp.dot(a_ref[...], w_ref[...], preferred_element_type=F32)
        y = _post_fn(res_w, g_ref[...], gate_ref[...], f, x_ref[...])
        if target is None:
            f_ref, y_ref = rest
            y_ref[...] = y
        else:
            t_ref, f_ref, dy_ref, sq_ref = rest
            diff = y - t_ref[...]
            dy_ref[...] = diff * (1.0 / D_MODEL)
            _accumulate(sq_ref, _rowsum(diff * diff), pl.program_id(0))
        f_ref[...] = f

    rows = pl.BlockSpec((tm, D_MODEL), lambda i: (i, 0))
    out_specs, out_shape = [rows, rows], [jax.ShapeDtypeStruct((s, D_MODEL), F32)] * 2
    if target is not None:
        out_specs.append(ROW_SPEC1)
        out_shape.append(jax.ShapeDtypeStruct((1, D_MODEL), F32))
    return pl.pallas_call(
        body,
        name=name,
        grid=(s // tm,),
        in_specs=[pl.BlockSpec((tm, k), lambda i: (i, 0)), pl.BlockSpec((k, D_MODEL), lambda i: (0, 0)), rows,
                  ROW_SPEC1, ROW_SPEC1] + [rows] * len(extra),
        out_specs=out_specs,
        out_shape=out_shape,
        compiler_params=_cparams(dimension_semantics=("arbitrary",)),
    )(a, w, x, g_post, gate, *extra)


def _merge_matmul_post(att, rec, proj, w, x, g_post, gate, res_w, name):
    s = att.shape[0]
    tm = min(FUSE_TM, s)

    def body(att_ref, rec_ref, ga_ref, gr_ref, w_ref, x_ref, g_ref, gate_ref, m_ref, f_ref, y_ref):
        merged = _merge_fn(att_ref[...], rec_ref[...], ga_ref[...], gr_ref[...]).astype(BF16)
        m_ref[...] = merged
        f = jnp.dot(merged, w_ref[...], preferred_element_type=F32)
        f_ref[...] = f
        y_ref[...] = _post_fn(res_w, g_ref[...], gate_ref[...], f, x_ref[...])

    rows = pl.BlockSpec((tm, D_MODEL), lambda i: (i, 0))
    return pl.pallas_call(
        body,
        name=name,
        grid=(s // tm,),
        in_specs=[rows, rows, pl.BlockSpec((tm, D_MODEL), lambda i: (i, 2)), pl.BlockSpec((tm, D_MODEL), lambda i: (i, 3)),
                  pl.BlockSpec(w.shape, lambda i: (0, 0)), rows, ROW_SPEC1, ROW_SPEC1],
        out_specs=[rows, rows, rows],
        out_shape=[jax.ShapeDtypeStruct((s, D_MODEL), BF16)] + [jax.ShapeDtypeStruct((s, D_MODEL), F32)] * 2,
        compiler_params=_cparams(dimension_semantics=("parallel",)),
    )(att, rec, proj, proj, w, x, g_post, gate)


def _post_bwd_merge_bwd(f, dy, g_post, gate, res_w, w, att, rec, proj, name):
    s = f.shape[0]
    tm = min(FUSE_TM, s)

    def body(f_ref, dy_ref, gp_ref, gate_ref, w_ref, att_ref, rec_ref, ga_ref, gr_ref,
             df_ref, datt_ref, drec_ref, dga_ref, dgr_ref, sums_ref):
        i = pl.program_id(0)
        dgp, dgate, df = _post_vjp(res_w, gp_ref[...], gate_ref[...], f_ref[...], dy_ref[...])
        dfb = df.astype(BF16)
        df_ref[...] = dfb
        _accumulate(sums_ref, _sum_rows(dgp, dgate), i)
        dmerged = lax.dot_general(dfb, w_ref[...], _DIMS["nt"], preferred_element_type=F32)
        _, vjp = jax.vjp(_merge_fn, att_ref[...], rec_ref[...], ga_ref[...], gr_ref[...])
        for ref, val in zip((datt_ref, drec_ref, dga_ref, dgr_ref), vjp(dmerged)):
            ref[...] = val.astype(ref.dtype)

    rows = pl.BlockSpec((tm, D_MODEL), lambda i: (i, 0))
    return pl.pallas_call(
        body,
        name=name,
        grid=(s // tm,),
        in_specs=[rows, rows, ROW_SPEC1, ROW_SPEC1, pl.BlockSpec(w.shape, lambda i: (0, 0)), rows, rows,
                  pl.BlockSpec((tm, D_MODEL), lambda i: (i, 2)), pl.BlockSpec((tm, D_MODEL), lambda i: (i, 3))],
        out_specs=[rows] * 5 + [SUMS_SPEC1],
        out_shape=[jax.ShapeDtypeStruct((s, D_MODEL), BF16)] * 5 + [SUMS_SHAPE],
        compiler_params=_cparams(dimension_semantics=("arbitrary",)),
    )(f, dy, g_post, gate, w, att, rec, proj, proj)


def _matmul_recin_bwd(drec, w, hs, proj, name, deps=()):
    s = drec.shape[0]
    tm = min(FUSE_TM, s)
    nd = len(deps)

    def body(d_ref, w_ref, hs_ref, yr_ref, *rest):
        dhs_ref, dyr_ref = rest[nd:]
        d = lax.dot_general(d_ref[...], w_ref[...], _DIMS["nt"], preferred_element_type=F32)
        _, vjp = jax.vjp(_recin_fn, hs_ref[...], yr_ref[...])
        dhs, dyr = vjp(d)
        dhs_ref[...] = dhs
        dyr_ref[...] = dyr.astype(dyr_ref.dtype)

    rows = pl.BlockSpec((tm, D_MODEL), lambda i: (i, 0))
    return pl.pallas_call(
        body,
        name=name,
        grid=(s // tm,),
        in_specs=[rows, pl.BlockSpec(w.shape, lambda i: (0, 0)), rows,
                  pl.BlockSpec((tm, D_MODEL), lambda i: (i, 1))] + [ANY_SPEC] * nd,
        out_specs=[rows, rows],
        out_shape=[jax.ShapeDtypeStruct((s, D_MODEL), F32), jax.ShapeDtypeStruct((s, D_MODEL), BF16)],
        compiler_params=_cparams(dimension_semantics=("parallel",)),
    )(drec, w, hs, proj, *deps)


def _post_vjp(res_w, g, gate, f, dy):
    _, vjp = jax.vjp(lambda g, gate, f: _post_fn(res_w, g, gate, f, 0.0), g, gate, f)
    return vjp(dy)


def _post_bwd_up_bwd(f, dy, g_post, gate, res_w, w_down, g, u, name, deps=(), chunked=False):
    s = f.shape[0]
    tm = min(FFN_TM, s)
    nd = len(deps)

    def body(f_ref, dy_ref, gp_ref, gate_ref, wd_ref, g_ref, u_ref, *rest):
        df_ref, dgu_ref, sums_ref, df_s = rest[nd:]
        i = pl.program_id(0)

        @pl.when(pl.program_id(1) == 0)
        def _():
            dgp, dgate, df = _post_vjp(res_w, gp_ref[...], gate_ref[...], f_ref[...], dy_ref[...])
            df_s[...] = df.astype(BF16)
            df_ref[...] = df_s[...]
            _accumulate(sums_ref, _sum_rows(dgp, dgate), i)

        bounds = FFN_CHUNKS if chunked else (0, FFN_TF)
        for lo, hi in zip(bounds[:-1], bounds[1:]):
            da = lax.dot_general(df_s[...], wd_ref[lo:hi, :], _DIMS["nt"], preferred_element_type=F32)
            _, vjp = jax.vjp(_glu_fn, g_ref[:, lo:hi].astype(F32), u_ref[:, lo:hi].astype(F32))
            dg, du = vjp(da)
            dgu_ref[0, :, lo:hi] = dg.astype(dgu_ref.dtype)
            dgu_ref[1, :, lo:hi] = du.astype(dgu_ref.dtype)

    rows = pl.BlockSpec((tm, D_MODEL), lambda i, j: (i, 0))
    blk = pl.BlockSpec((tm, FFN_TF), lambda i, j: (i, j))
    return pl.pallas_call(
        body,
        name=name,
        grid=(s // tm, D_FF // FFN_TF),
        in_specs=[rows, rows, ROW_SPEC2, ROW_SPEC2, pl.BlockSpec((FFN_TF, D_MODEL), lambda i, j: (j, 0)), blk,
                  blk] + [ANY_SPEC] * nd,
        out_specs=[rows, pl.BlockSpec((2, tm, FFN_TF), lambda i, j: (0, i, j)), SUMS_SPEC2],
        out_shape=[jax.ShapeDtypeStruct((s, D_MODEL), BF16), jax.ShapeDtypeStruct((2, s, D_FF), BF16), SUMS_SHAPE],
        scratch_shapes=[pltpu.VMEM((tm, D_MODEL), BF16)],
        compiler_params=_cparams(dimension_semantics=("arbitrary", "arbitrary")),
    )(f, dy, g_post, gate, w_down, g, u, *deps)


def _post_bwd_matmul(f, dy, g_post, gate, res_w, w, name):
    s = f.shape[0]
    n = w.shape[0]
    tm = min(FUSE_TM, s)

    def body(f_ref, dy_ref, gp_ref, gate_ref, w_ref, df_ref, o_ref, dgp_ref, dgate_ref):
        i = pl.program_id(0)
        dgp, dgate, df = _post_vjp(res_w, gp_ref[...], gate_ref[...], f_ref[...], dy_ref[...])
        dfb = df.astype(BF16)
        df_ref[...] = dfb
        _accumulate(dgp_ref, dgp, i)
        _accumulate(dgate_ref, dgate, i)
        o_ref[...] = lax.dot_general(dfb, w_ref[...], _DIMS["nt"], preferred_element_type=F32)

    rows = pl.BlockSpec((tm, D_MODEL), lambda i: (i, 0))
    return pl.pallas_call(
        body,
        name=name,
        grid=(s // tm,),
        in_specs=[rows, rows, ROW_SPEC1, ROW_SPEC1, pl.BlockSpec((n, D_MODEL), lambda i: (0, 0))],
        out_specs=[rows, pl.BlockSpec((tm, n), lambda i: (i, 0)), ROW_SPEC1, ROW_SPEC1],
        out_shape=[jax.ShapeDtypeStruct((s, D_MODEL), BF16), jax.ShapeDtypeStruct((s, n), F32),
                   jax.ShapeDtypeStruct((1, D_MODEL), F32), jax.ShapeDtypeStruct((1, D_MODEL), F32)],
        compiler_params=_cparams(dimension_semantics=("arbitrary",)),
    )(f, dy, g_post, gate, w)


def _matmul_pre_bwd(parts, w_t, x, dres, g, shift, scale, name, deps=()):
    s = x.shape[0]
    na, nd = len(parts), len(deps)
    ranges = [p[3] for p in parts]

    def body(*refs):
        a_refs = refs[:na]
        w_ref, x_ref, dres_ref, g_ref, sh_ref, sc_ref = refs[na : na + 6]
        dx_ref, sums_ref = refs[na + 6 + nd :]
        i = pl.program_id(0)
        dh = None
        for a_ref, (r0, r1) in zip(a_refs, ranges):
            p = jnp.dot(a_ref[...], w_ref[r0:r1, :], preferred_element_type=F32)
            dh = p if dh is None else dh + p
        _, vjp = jax.vjp(_pre_fn, g_ref[...], sh_ref[...], sc_ref[...], x_ref[...])
        dg, dsh, dsc, dx = vjp(dh)
        dx_ref[...] = dx + dres_ref[...]
        _accumulate(sums_ref, _sum_rows(dg, dsh, dsc), i)

    tm = parts[0][1][-2]
    rows = pl.BlockSpec((tm, D_MODEL), lambda i: (i, 0))
    return pl.pallas_call(
        body,
        name=name,
        grid=(s // tm,),
        in_specs=[pl.BlockSpec(p[1], p[2]) for p in parts]
        + [pl.BlockSpec(w_t.shape, lambda i: (0, 0)), rows, rows, ROW_SPEC1, ROW_SPEC1, ROW_SPEC1]
        + [ANY_SPEC] * nd,
        out_specs=[rows, SUMS_SPEC1],
        out_shape=[jax.ShapeDtypeStruct((s, D_MODEL), F32), SUMS_SHAPE],
        compiler_params=_cparams(dimension_semantics=("arbitrary",)),
    )(*[p[0] for p in parts], w_t, x, dres, g, shift, scale, *deps)


def _dw_gu(dgu, h, name, deps=(), tk=DW_TK):
    s = h.shape[0]
    tk = min(tk, s)
    nk = s // tk
    half = D_FF // FFN_TF

    def body(a_ref, b_ref, *rest):
        o_ref, acc = rest[len(deps) :]
        kk = pl.program_id(1)
        p = lax.dot_general(a_ref[...], b_ref[...], _DIMS["tn"], preferred_element_type=F32)

        @pl.when(kk == 0)
        def _():
            acc[...] = p

        @pl.when(kk > 0)
        def _():
            acc[...] += p

        @pl.when(kk == nk - 1)
        def _():
            o_ref[...] = acc[...].astype(o_ref.dtype)

    return pl.pallas_call(
        body,
        name=name,
        grid=(2 * half, nk),
        in_specs=[pl.BlockSpec((None, tk, FFN_TF), lambda i, kk: (i // half, kk, i % half)),
                  pl.BlockSpec((tk, D_MODEL), lambda i, kk: (kk, 0))] + [ANY_SPEC] * len(deps),
        out_specs=pl.BlockSpec((FFN_TF, D_MODEL), lambda i, kk: (i, 0)),
        out_shape=jax.ShapeDtypeStruct((2 * D_FF, D_MODEL), BF16),
        scratch_shapes=[pltpu.VMEM((FFN_TF, D_MODEL), F32)],
        compiler_params=_cparams(dimension_semantics=("parallel", "arbitrary")),
    )(dgu, h, *deps)


def _lru_diag_blocks(dw_bd, name):
    def body(w_ref, o_ref):
        for half in range(2):
            for n in range(LRU_BLOCKS):
                rows = slice(n * LRU_BLOCK, (n + 1) * LRU_BLOCK)
                cols = slice(half * LRU_WIDTH + n * LRU_BLOCK, half * LRU_WIDTH + (n + 1) * LRU_BLOCK)
                o_ref[half, rows, :] = w_ref[rows, cols]

    return pl.pallas_call(
        body, name=name, out_shape=jax.ShapeDtypeStruct((2, LRU_WIDTH, LRU_BLOCK), F32), compiler_params=_cparams()
    )(dw_bd)


def _swiglu_fwd(gu, name):
    (a,) = _rowwise(_swiglu_fn, name, [], [gu], [(D_FF, BF16)], ts=128)
    return a


def _swiglu_bwd(gu, da, name, deps=()):
    def fn(gu, da):
        _, vjp = jax.vjp(_swiglu_fn, gu)
        return vjp(da)[0]

    (dgu,) = _rowwise(fn, name, [], [gu, da], [(2 * D_FF, BF16)], ts=128, deps=deps)
    return dgu


def _shift_down(ext, j, rows):
    return pltpu.roll(ext, j, 0)[V7X_SUBLANES : V7X_SUBLANES + rows]


def _shift_up(ext, j, rows):
    return pltpu.roll(ext, ext.shape[0] - j, 0)[:rows] if j else ext[:rows]


LRU_SLAB = 256
N_SLABS = LRU_WIDTH // LRU_SLAB


def _slab_weights(wa, wx):
    per = LRU_SLAB // LRU_BLOCK
    eye = jnp.eye(per, dtype=wa.dtype)

    def diag(w):
        w4 = w.reshape(N_SLABS, per, LRU_BLOCK, LRU_BLOCK)
        return jnp.einsum("sbkj,bc->sbkcj", w4, eye).reshape(N_SLABS, LRU_SLAB, LRU_SLAB)

    return jnp.concatenate([diag(wa), diag(wx)], axis=2).reshape(LRU_WIDTH, 2 * LRU_SLAB).astype(BF16)


def _slab_cols(v, s):
    lo = s * LRU_SLAB
    return jnp.concatenate([v[:, lo : lo + LRU_SLAB], v[:, LRU_WIDTH + lo : LRU_WIDTH + lo + LRU_SLAB]], axis=1)


def _lru_front(proj, w8, b, w_slab, ba, bx, lam, name):
    def fn(i, steps, w8, b, w_slab, ba, bx, lam, x, halo):
        halo = jnp.where(i > 0, halo, 0.0)
        ext = jnp.concatenate([halo, x], axis=0)
        xc = b + w8[3:4] * x
        for j in (1, 2, 3):
            xc = xc + w8[3 - j : 4 - j] * _shift_down(ext, j, x.shape[0])
        xcb = xc.astype(BF16)
        prods = []
        for s in range(N_SLABS):
            rows = slice(s * LRU_SLAB, (s + 1) * LRU_SLAB)
            prods.append(jnp.dot(xcb[:, rows], w_slab[rows], preferred_element_type=F32))
        pre = jnp.concatenate([p[:, :LRU_SLAB] for p in prods] + [p[:, LRU_SLAB:] for p in prods], axis=1)
        a, u = _gates_fn(ba, bx, lam, pre, xc)
        return xc, pre, a, u

    tiles = [(proj, LRU_WIDTH, 0), (proj, LRU_WIDTH, 0, "prev")]
    outs = [(LRU_WIDTH, F32), (2 * LRU_WIDTH, F32), (LRU_WIDTH, F32), (LRU_WIDTH, F32)]
    return _rowwise(fn, name, [w8, b, w_slab, ba, bx, lam], tiles, outs, with_index=True)


def _lru_back(pre, xc, w_slab, ba, bx, lam, g, h_prev, name, deps=()):
    def fn(w_slab, ba, bx, lam, pre, xc, g, h_prev):
        _, vjp = jax.vjp(_gates_fn, ba, bx, lam, pre, xc)
        dba, dbx, dlam, dpre, dxc = vjp((g * h_prev, g))
        dpre = dpre.astype(BF16)
        back = []
        for s in range(N_SLABS):
            rows = slice(s * LRU_SLAB, (s + 1) * LRU_SLAB)
            back.append(lax.dot_general(_slab_cols(dpre, s), w_slab[rows], _DIMS["nt"], preferred_element_type=F32))
        return dpre, dxc + jnp.concatenate(back, axis=1), _sum_rows(dba, dbx, dlam)

    return _rowwise(fn, name, [w_slab, ba, bx, lam], [pre, xc, g, h_prev],
                    [(2 * LRU_WIDTH, BF16), (LRU_WIDTH, F32)], [(V7X_SUBLANES, LRU_WIDTH)], deps=deps)


def _lru_dw(xc, dpre, name):
    s = xc.shape[0]
    ts = min(512, s)
    steps = s // ts
    per = LRU_SLAB // LRU_BLOCK

    def body(x_ref, d_ref, o_ref, acc):
        i = pl.program_id(0)
        xcb = x_ref[...].astype(BF16)
        d = d_ref[...]
        for sl in range(N_SLABS):
            rows = slice(sl * LRU_SLAB, (sl + 1) * LRU_SLAB)
            p = lax.dot_general(xcb[:, rows], _slab_cols(d, sl), _DIMS["tn"], preferred_element_type=F32)

            @pl.when(i == 0)
            def _(p=p, rows=rows):
                acc[rows, :] = p

            @pl.when(i > 0)
            def _(p=p, rows=rows):
                acc[rows, :] += p

        @pl.when(i == steps - 1)
        def _():
            for half in range(2):
                for n in range(LRU_BLOCKS):
                    r0 = n * LRU_BLOCK
                    c0 = half * LRU_SLAB + (n % per) * LRU_BLOCK
                    o_ref[half, r0 : r0 + LRU_BLOCK, :] = acc[r0 : r0 + LRU_BLOCK, c0 : c0 + LRU_BLOCK]

    return pl.pallas_call(
        body,
        name=name,
        grid=(steps,),
        in_specs=[pl.BlockSpec((ts, LRU_WIDTH), lambda i: (i, 0)), pl.BlockSpec((ts, 2 * LRU_WIDTH), lambda i: (i, 0))],
        out_specs=pl.BlockSpec((2, LRU_WIDTH, LRU_BLOCK), lambda i: (0, 0, 0)),
        out_shape=jax.ShapeDtypeStruct((2, LRU_WIDTH, LRU_BLOCK), F32),
        scratch_shapes=[pltpu.VMEM((LRU_WIDTH, 2 * LRU_SLAB), F32)],
        compiler_params=_cparams(dimension_semantics=("arbitrary",)),
    )(xc, dpre)


def _conv_bwd(proj, w8, d1, name):
    def fn(i, steps, w8, x, halo, d, d1n):
        rows = x.shape[0]
        dn = jnp.where(i < steps - 1, d1n, 0.0)
        halo = jnp.where(i > 0, halo, 0.0)
        dext = jnp.concatenate([d, dn], axis=0)
        xext = jnp.concatenate([halo, x], axis=0)
        dx = w8[3:4] * d
        dw = [None] * 4
        dw[3] = _rowsum(d * x)
        for k in (1, 2, 3):
            dx = dx + w8[3 - k : 4 - k] * _shift_up(dext, k, rows)
            dw[3 - k] = _rowsum(d * _shift_down(xext, k, rows))
        return dx, _sum_rows(*dw, _rowsum(d))

    tiles = [(proj, LRU_WIDTH, 0), (proj, LRU_WIDTH, 0, "prev"), d1, (d1, LRU_WIDTH, 0, "next")]
    return _rowwise(fn, name, [w8], tiles, [(LRU_WIDTH, BF16)], [(V7X_SUBLANES, LRU_WIDTH)], with_index=True)


SCAN_ROWS = 512


def _block_scan(a, b, row, reverse):
    for d in (1, 2, 4):
        if reverse:
            shift, keep = V7X_SUBLANES - d, row < V7X_SUBLANES - d
        else:
            shift, keep = d, row >= d
        a_s = pltpu.roll(a, shift, 0)
        b_s = pltpu.roll(b, shift, 0)
        b = jnp.where(keep, a * b_s + b, b)
        a = jnp.where(keep, a * a_s, a)
    return a, b


def _scan_fwd(a, u, proj, name):
    s, w = a.shape
    ts = min(SCAN_ROWS, s)
    sub = ts // V7X_SUBLANES

    def body(a_ref, u_ref, yr_ref, h_ref, hp_ref, rec_ref, carry):
        @pl.when(pl.program_id(0) == 0)
        def _():
            carry[...] = jnp.zeros_like(carry)

        row = lax.broadcasted_iota(jnp.int32, (V7X_SUBLANES, w), 0)

        def step(j, c):
            rows = pl.ds(pl.multiple_of(j * V7X_SUBLANES, V7X_SUBLANES), V7X_SUBLANES)
            pa, pb = _block_scan(a_ref[rows, :], u_ref[rows, :], row, False)
            h = pb + pa * c
            h_ref[rows, :] = h
            hp_ref[rows, :] = jnp.where(row >= 1, pltpu.roll(h, 1, 0), c)
            return jnp.broadcast_to(h[V7X_SUBLANES - 1 :], (V7X_SUBLANES, w))

        carry[...] = lax.fori_loop(0, sub, step, carry[...])
        rec_ref[...] = _recin_fn(h_ref[...], yr_ref[...]).astype(rec_ref.dtype)

    spec = pl.BlockSpec((ts, w), lambda i: (i, 0))
    return pl.pallas_call(
        body,
        name=name,
        grid=(s // ts,),
        in_specs=[spec, spec, pl.BlockSpec((ts, w), lambda i: (i, 1))],
        out_specs=[spec, spec, spec],
        out_shape=[jax.ShapeDtypeStruct((s, w), F32)] * 2 + [jax.ShapeDtypeStruct((s, w), BF16)],
        scratch_shapes=[pltpu.VMEM((V7X_SUBLANES, w), F32)],
        compiler_params=_cparams(dimension_semantics=("arbitrary",)),
    )(a, u, proj)


def _scan_bwd(a, dh, name):
    s, w = a.shape
    ts = min(SCAN_ROWS, s)
    sub = ts // V7X_SUBLANES
    steps = s // ts

    def body(a_ref, d_ref, g_ref, carry):
        @pl.when(pl.program_id(0) == 0)
        def _():
            carry[...] = jnp.zeros_like(carry)

        row = lax.broadcasted_iota(jnp.int32, (V7X_SUBLANES, w), 0)

        def step(jj, c):
            j = sub - 1 - jj
            rows = pl.ds(pl.multiple_of(j * V7X_SUBLANES, V7X_SUBLANES), V7X_SUBLANES)
            av, dv = a_ref[rows, :], d_ref[rows, :]
            pa, pb = _block_scan(av, av * dv, row, True)
            big = pb + pa * c
            g_ref[rows, :] = dv + jnp.where(row < V7X_SUBLANES - 1, pltpu.roll(big, V7X_SUBLANES - 1, 0), c)
            return jnp.broadcast_to(big[:1], (V7X_SUBLANES, w))

        carry[...] = lax.fori_loop(0, sub, step, carry[...])

    spec = pl.BlockSpec((ts, w), lambda i: (steps - 1 - i, 0))
    return pl.pallas_call(
        body,
        name=name,
        grid=(steps,),
        in_specs=[spec, spec],
        out_specs=spec,
        out_shape=jax.ShapeDtypeStruct((s, w), F32),
        scratch_shapes=[pltpu.VMEM((V7X_SUBLANES, w), F32)],
        compiler_params=_cparams(dimension_semantics=("arbitrary",)),
    )(a, dh)


def _rel_index():
    i = np.arange(ATT_TQ)[:, None]
    j = np.arange(3 * ATT_TQ)[None, :]
    band = (j // CHUNK >= i // CHUNK) & (j // CHUNK <= i // CHUNK + LEFT_CHUNKS)
    return band


SKEW = 4 * ATT_TQ


def _skew_onehot():
    t = np.arange(SKEW)
    diag = np.where(t < 3 * ATT_TQ, -t, SKEW - t)
    idx = np.clip(diag + LEFT_CHUNKS * CHUNK, -MAX_REL, MAX_REL) + MAX_REL
    hit = (idx[:, None] == np.arange(2 * MAX_REL + 1)[None, :]) & (t[:, None] != 3 * ATT_TQ)
    return hit.astype(np.float32)


def _bias_tile(rel_bias):
    per_t = jnp.dot(rel_bias, jnp.asarray(_skew_onehot()).T, precision=lax.Precision.HIGHEST)
    flat = jnp.broadcast_to(per_t[:, None, :], (ATT_HEADS, ATT_TQ, SKEW)).reshape(ATT_HEADS, ATT_TQ * SKEW)
    tile = flat[:, : ATT_TQ * (SKEW - 1)].reshape(ATT_HEADS, ATT_TQ, SKEW - 1)[:, :, : 3 * ATT_TQ]
    first = (2 - np.arange(3))[:, None, None, None] * ATT_TQ
    seen = _rel_index()[None, None] & (np.arange(3 * ATT_TQ)[None, None, None, :] >= first)
    return jnp.where(jnp.asarray(seen), tile[None], NEG)


def _bias_grad(dbias):
    flat = jnp.pad(dbias, ((0, 0), (0, 0), (0, SKEW - 1 - 3 * ATT_TQ))).reshape(ATT_HEADS, ATT_TQ * (SKEW - 1))
    per_t = jnp.sum(jnp.pad(flat, ((0, 0), (0, ATT_TQ))).reshape(ATT_HEADS, ATT_TQ, SKEW), axis=1)
    return jnp.dot(per_t, jnp.asarray(_skew_onehot()), precision=lax.Precision.HIGHEST)


def _attn_specs(nt):
    qb, kb, vb = OFF_Q // V7X_LANES, OFF_K // V7X_LANES, OFF_V // V7X_LANES
    blk = (ATT_TQ, V7X_LANES)

    def qmap(base):
        return lambda hp, m: (jnp.minimum(m, nt - 1), base + hp)

    def wmap(base, back):
        return lambda hp, m: (jnp.clip(m - back, 0, nt - 1), base + hp)

    specs = [pl.BlockSpec(blk, qmap(qb))]
    specs += [pl.BlockSpec(blk, wmap(kb, back)) for back in (2, 1, 0)]
    specs += [pl.BlockSpec(blk, wmap(vb, back)) for back in (2, 1, 0)]
    return specs


ATT_SCALE = ATT_HEAD_DIM**-0.5


def _attn_exp(qh, kh, bias):
    s = lax.dot_general(qh, kh, _DIMS["nt"], preferred_element_type=F32) + bias
    e = jnp.exp(s - jnp.max(s, axis=-1, keepdims=True))
    return e, jnp.sum(e, axis=-1, keepdims=True)


def _attn_window(k0, k1, k2, v0, v1, v2):
    k = jnp.concatenate([k0[...], k1[...], k2[...]], axis=0).astype(BF16)
    v = jnp.concatenate([v0[...], v1[...], v2[...]], axis=0).astype(BF16)
    return k, v


def _bias_spec():
    return pl.BlockSpec((1, 2, ATT_TQ, 3 * ATT_TQ), lambda hp, m: (jnp.minimum(m, 2), hp, 0, 0))


def _attn_fwd(proj, bias, name):
    s = proj.shape[0]
    nt = s // ATT_TQ

    def body(q_ref, k0, k1, k2, v0, v1, v2, b_ref, o_ref):
        k, v = _attn_window(k0, k1, k2, v0, v1, v2)
        q = (q_ref[...] * ATT_SCALE).astype(BF16)
        for hh in range(2):
            cols = slice(hh * ATT_HEAD_DIM, (hh + 1) * ATT_HEAD_DIM)
            e, total = _attn_exp(q[:, cols], k[:, cols], b_ref[0, hh])
            o = jnp.dot(e.astype(BF16), v[:, cols], preferred_element_type=F32) / total
            o_ref[:, cols] = o.astype(o_ref.dtype)

    specs = _attn_specs(nt) + [_bias_spec()]
    return pl.pallas_call(
        body,
        name=name,
        grid=(ATT_HEADS // 2, nt),
        in_specs=specs,
        out_specs=pl.BlockSpec((ATT_TQ, V7X_LANES), lambda hp, m: (m, hp)),
        out_shape=jax.ShapeDtypeStruct((s, ATT_WIDTH), BF16),
        compiler_params=_cparams(dimension_semantics=("parallel", "arbitrary")),
    )(proj, proj, proj, proj, proj, proj, proj, bias)


def _attn_bwd(proj, bias, do, name):
    s = proj.shape[0]
    nt = s // ATT_TQ
    win = 3 * ATT_TQ

    def body(q_ref, k0, k1, k2, v0, v1, v2, do_ref, b_ref, dq_ref, dk_ref, dv_ref, db_ref, dk_acc, dv_acc):
        m = pl.program_id(1)

        @pl.when(m == 0)
        def _():
            dk_acc[...] = jnp.zeros_like(dk_acc)
            dv_acc[...] = jnp.zeros_like(dv_acc)
            db_ref[...] = jnp.zeros_like(db_ref)

        @pl.when(m < nt)
        def _():
            k, v = _attn_window(k0, k1, k2, v0, v1, v2)
            q = (q_ref[...] * ATT_SCALE).astype(BF16)
            dout = do_ref[...]
            for hh in range(2):
                cols = slice(hh * ATT_HEAD_DIM, (hh + 1) * ATT_HEAD_DIM)
                qh, kh, vh, doh = q[:, cols], k[:, cols], v[:, cols], dout[:, cols]
                e, total = _attn_exp(qh, kh, b_ref[0, hh])
                p = e / total
                dvh = lax.dot_general(p.astype(BF16), doh, _DIMS["tn"], preferred_element_type=F32)
                dp = lax.dot_general(doh, vh, _DIMS["nt"], preferred_element_type=F32)
                ds = p * (dp - jnp.sum(dp * p, axis=-1, keepdims=True))
                db_ref[hh] += ds
                dsb = ds.astype(BF16)
                dqh = jnp.dot(dsb, kh, preferred_element_type=F32) * ATT_SCALE
                dkh = lax.dot_general(dsb, qh, _DIMS["tn"], preferred_element_type=F32)
                dq_ref[:, cols] = dqh.astype(dq_ref.dtype)
                dk_acc[:, cols] += dkh
                dv_acc[:, cols] += dvh

        dk_ref[...] = dk_acc[:ATT_TQ].astype(dk_ref.dtype)
        dv_ref[...] = dv_acc[:ATT_TQ].astype(dv_ref.dtype)
        for acc in (dk_acc, dv_acc):
            rest = acc[ATT_TQ:]
            acc[: win - ATT_TQ] = rest
            acc[win - ATT_TQ :] = jnp.zeros((ATT_TQ, V7X_LANES), F32)

    blk = (ATT_TQ, V7X_LANES)
    specs = _attn_specs(nt)
    specs.append(pl.BlockSpec(blk, lambda hp, m: (jnp.minimum(m, nt - 1), hp)))
    specs.append(_bias_spec())
    done = lambda hp, m: (jnp.maximum(m - 2, 0), hp)
    out_specs = [
        pl.BlockSpec(blk, lambda hp, m: (jnp.minimum(m, nt - 1), hp)),
        pl.BlockSpec(blk, done),
        pl.BlockSpec(blk, done),
        pl.BlockSpec((2, ATT_TQ, win), lambda hp, m: (hp, 0, 0)),
    ]
    out_shape = [jax.ShapeDtypeStruct((s, ATT_WIDTH), BF16)] * 3
    out_shape.append(jax.ShapeDtypeStruct((ATT_HEADS, ATT_TQ, win), F32))
    return pl.pallas_call(
        body,
        name=name,
        grid=(ATT_HEADS // 2, nt + 2),
        in_specs=specs,
        out_specs=out_specs,
        out_shape=out_shape,
        scratch_shapes=[pltpu.VMEM((win, V7X_LANES), F32), pltpu.VMEM((win, V7X_LANES), F32)],
        compiler_params=_cparams(dimension_semantics=("arbitrary", "arbitrary")),
    )(proj, proj, proj, proj, proj, proj, proj, do, bias)


def _ada_fwd(c_all, w, name):
    def body(c_ref, w_ref, o_ref):
        act = _silu(c_ref[...]).astype(BF16)
        o_ref[...] = jnp.dot(act, w_ref[...].astype(BF16), preferred_element_type=F32)

    return pl.pallas_call(
        body, name=name, out_shape=jax.ShapeDtypeStruct((c_all.shape[0], w.shape[1]), F32), compiler_params=_cparams()
    )(c_all, w)


def _ada_bwd(c_all, dmod, name):
    def body(c_ref, d_ref, o_ref):
        act = _silu(c_ref[...])
        o_ref[...] = lax.dot_general(act, d_ref[...], _DIMS["tn"], preferred_element_type=F32,
                                     precision=lax.Precision.HIGHEST)

    return pl.pallas_call(
        body, name=name, out_shape=jax.ShapeDtypeStruct((c_all.shape[1], dmod.shape[1]), F32), compiler_params=_cparams()
    )(c_all, dmod)


def _adamw_parts(landed, sent, me, w, m, v, name, rows=256):
    r, c = w.shape
    tr = _pick(r, rows, 16)

    def body(me_ref, g_ref, own_ref, w_ref, m_ref, v_ref, go_ref, d_ref, mo_ref, vo_ref):
        mine = me_ref[0]
        grad = jnp.zeros((tr, c), F32)
        for d in range(N_DEV):
            grad = grad + jnp.where(mine == d, own_ref[0], g_ref[d]).astype(F32)
        _adamw_update(grad, w_ref, m_ref, v_ref, go_ref, d_ref, mo_ref, vo_ref)

    spec = pl.BlockSpec((tr, c), lambda i, me_ref: (i, 0))
    return pl.pallas_call(
        body,
        name=name,
        grid_spec=pltpu.PrefetchScalarGridSpec(
            num_scalar_prefetch=1,
            grid=(r // tr,),
            in_specs=[pl.BlockSpec((N_DEV, tr, c), lambda i, me_ref: (0, i, 0)),
                      pl.BlockSpec((1, tr, c), lambda i, me_ref: (me_ref[0], i, 0)), spec, spec, spec],
            out_specs=[spec] * 4,
        ),
        out_shape=[jax.ShapeDtypeStruct((r, c), F32)] * 4,
        compiler_params=_cparams(dimension_semantics=("parallel",)),
    )(me.reshape(1).astype(jnp.int32), landed, sent, w, m, v)


def _adamw_update(grad, w_ref, m_ref, v_ref, go_ref, d_ref, mo_ref, vo_ref):
    m2 = ADAM_B1 * m_ref[...] + (1.0 - ADAM_B1) * grad
    v2 = ADAM_B2 * v_ref[...] + (1.0 - ADAM_B2) * (grad * grad)
    m_hat = m2 / (1.0 - ADAM_B1**ADAM_STEP)
    v_hat = v2 / (1.0 - ADAM_B2**ADAM_STEP)
    go_ref[...] = grad
    d_ref[...] = -ADAM_LR * (m_hat / (jnp.sqrt(v_hat) + ADAM_EPS) + ADAM_WD * w_ref[...])
    mo_ref[...] = m2
    vo_ref[...] = v2


def _adamw(g, w, m, v, name, rows=256):
    r, c = w.shape
    tr = _pick(r, rows, 16)

    def body(g_ref, w_ref, m_ref, v_ref, go_ref, d_ref, mo_ref, vo_ref):
        _adamw_update(g_ref[...], w_ref, m_ref, v_ref, go_ref, d_ref, mo_ref, vo_ref)

    spec = pl.BlockSpec((tr, c), lambda i: (i, 0))
    return pl.pallas_call(
        body,
        name=name,
        grid=(r // tr,),
        in_specs=[spec, spec, spec, spec],
        out_specs=[spec] * 4,
        out_shape=[jax.ShapeDtypeStruct((r, c), F32)] * 4,
        compiler_params=_cparams(dimension_semantics=("parallel",)),
    )(g, w, m, v)


def _sum_parts(parts, name):
    def body(p_ref, o_ref):
        acc = p_ref[0]
        for d in range(1, N_DEV):
            acc = acc + p_ref[d]
        o_ref[...] = acc

    return pl.pallas_call(
        body, name=name, out_shape=jax.ShapeDtypeStruct(parts.shape[1:], F32), compiler_params=_cparams()
    )(parts)


def _place():
    x, y, c = lax.axis_index("x"), lax.axis_index("y"), lax.axis_index("c")
    return x, y, c


def _dev_index(p):
    return 4 * p[0] + 2 * p[1] + p[2]


def _allgather_vmem(shard, name):
    m_per, n = shard.shape

    def body(x_ref, out_ref, send_sems, recv_sems, local_sem):
        x, y, c = _place()
        me, sibling = (x, y, c), (x, y, 1 - c)
        chips = [(1 - x, y), (x, 1 - y), (1 - x, 1 - y)]

        def rows(p):
            return out_ref.at[pl.ds(_dev_index(p) * m_per, m_per), :]

        def copy(k, block, to, src=None):
            return pltpu.make_async_remote_copy(
                src_ref=rows(block) if src is None else src, dst_ref=rows(block),
                send_sem=send_sems.at[k], recv_sem=recv_sems.at[k], device_id=to, device_id_type=MESH)

        mine = pltpu.make_async_copy(x_ref, rows(me), local_sem)
        mine.start()
        first = [copy(0, me, sibling, src=x_ref)]
        first += [copy(1 + j, me, (*chip, c), src=x_ref) for j, chip in enumerate(chips)]
        for cp in first:
            cp.start()
        passed = [copy(4 + j, (*chip, c), sibling) for j, chip in enumerate(chips)]
        for j, chip in enumerate(chips):
            copy(1 + j, (*chip, c), me).wait_recv()
            passed[j].start()
        copy(0, sibling, me).wait_recv()
        for j, chip in enumerate(chips):
            copy(4 + j, (*chip, 1 - c), me).wait_recv()
        for cp in first + passed:
            cp.wait_send()
        mine.wait()

    return pl.pallas_call(
        body,
        name=name,
        out_shape=jax.ShapeDtypeStruct((N_DEV * m_per, n), shard.dtype),
        in_specs=[pl.BlockSpec(memory_space=pltpu.VMEM)],
        out_specs=pl.BlockSpec(memory_space=pltpu.VMEM),
        scratch_shapes=[pltpu.SemaphoreType.DMA((7,)), pltpu.SemaphoreType.DMA((7,)), pltpu.SemaphoreType.DMA],
        compiler_params=_cparams(),
    )(shard)


def _allgather_hbm(shards, name):
    n = len(shards)

    def body(*refs):
        ins, outs = refs[:n], refs[n : 2 * n]
        send_sems, recv_sems, local_sems = refs[2 * n :]
        x, y, c = _place()
        me, sibling = (x, y, c), (x, y, 1 - c)
        chips = [(1 - x, y), (x, 1 - y), (1 - x, 1 - y)]

        def copy(a, k, block, to, src=None):
            dst = outs[a].at[_dev_index(block)]
            return pltpu.make_async_remote_copy(
                src_ref=dst if src is None else src, dst_ref=dst,
                send_sem=send_sems.at[a * 7 + k], recv_sem=recv_sems.at[a * 7 + k], device_id=to, device_id_type=MESH)

        mine = [pltpu.make_async_copy(ins[a], outs[a].at[_dev_index(me)], local_sems.at[a]) for a in range(n)]
        for cp in mine:
            cp.start()
        first = []
        for a in range(n):
            first.append(copy(a, 0, me, sibling, src=ins[a]))
            first += [copy(a, 1 + j, me, (*chip, c), src=ins[a]) for j, chip in enumerate(chips)]
        for cp in first:
            cp.start()
        passed = []
        for j, chip in enumerate(chips):
            for a in range(n):
                copy(a, 1 + j, (*chip, c), me).wait_recv()
                cp = copy(a, 4 + j, (*chip, c), sibling)
                cp.start()
                passed.append(cp)
        for a in range(n):
            copy(a, 0, sibling, me).wait_recv()
        for j, chip in enumerate(chips):
            for a in range(n):
                copy(a, 4 + j, (*chip, 1 - c), me).wait_recv()
        for cp in first + passed:
            cp.wait_send()
        for cp in mine:
            cp.wait()

    any_spec = pl.BlockSpec(memory_space=pl.ANY)
    return pl.pallas_call(
        body,
        name=name,
        out_shape=[jax.ShapeDtypeStruct((N_DEV, *s.shape), s.dtype) for s in shards],
        in_specs=[any_spec] * n,
        out_specs=[any_spec] * n,
        scratch_shapes=[pltpu.SemaphoreType.DMA((7 * n,)), pltpu.SemaphoreType.DMA((7 * n,)),
                        pltpu.SemaphoreType.DMA((n,))],
        compiler_params=_cparams(),
    )(*shards)


def _exchange_hbm(bufs, name):
    n = len(bufs)

    def body(*refs):
        ins, outs = refs[:n], refs[n : 2 * n]
        send_sems, recv_sems, local_sems = refs[2 * n :]
        x, y, c = _place()
        me = _dev_index((x, y, c))
        mine = [pltpu.make_async_copy(ins[a].at[me], outs[a].at[me], local_sems.at[a]) for a in range(n)]
        for cp in mine:
            cp.start()
        def peer_of(k):
            return (1 - x if k & 4 else x, 1 - y if k & 2 else y, 1 - c if k & 1 else c)

        copies = []
        for k in range(1, N_DEV):
            peer = peer_of(k)
            for a in range(n):
                copies.append(pltpu.make_async_remote_copy(
                    src_ref=ins[a].at[_dev_index(peer)], dst_ref=outs[a].at[me],
                    send_sem=send_sems.at[a * 7 + k - 1], recv_sem=recv_sems.at[a * 7 + k - 1],
                    device_id=peer, device_id_type=MESH))
        for cp in copies:
            cp.start()
        for k in range(1, N_DEV):
            peer = peer_of(k)
            for a in range(n):
                pltpu.make_async_remote_copy(
                    src_ref=ins[a].at[me], dst_ref=outs[a].at[_dev_index(peer)],
                    send_sem=send_sems.at[a * 7 + k - 1], recv_sem=recv_sems.at[a * 7 + k - 1],
                    device_id=peer, device_id_type=MESH).wait_recv()
        for cp in copies:
            cp.wait_send()
        for cp in mine:
            cp.wait()

    any_spec = pl.BlockSpec(memory_space=pl.ANY)
    return pl.pallas_call(
        body,
        name=name,
        out_shape=[jax.ShapeDtypeStruct(b.shape, b.dtype) for b in bufs],
        in_specs=[any_spec] * n,
        out_specs=[any_spec] * n,
        scratch_shapes=[pltpu.SemaphoreType.DMA((7 * n,)), pltpu.SemaphoreType.DMA((7 * n,)),
                        pltpu.SemaphoreType.DMA((n,))],
        compiler_params=_cparams(),
    )(*bufs)


HBM_SPEC = pl.BlockSpec(memory_space=pltpu.HBM)
SEM_SPEC = pl.BlockSpec(memory_space=pltpu.SEMAPHORE)
EFFECT = pltpu.SideEffectType.DATAFLOW_SIDE_EFFECTING


def _peers(x, y, c):
    return [(1 - x if k & 4 else x, 1 - y if k & 2 else y, 1 - c if k & 1 else c) for k in range(1, N_DEV)]


def _push_peers(mode, x, y, c):
    if mode == "all":
        return _peers(x, y, c)
    return [(x, y, 1 - c), (1 - x, y, c), (x, 1 - y, c), (1 - x, 1 - y, c)]


def _push_start(groups, sliced, name, after=(), modes=None):
    flat = [b for g in groups for b in g]
    n, ng = len(flat), len(groups)
    sizes = [len(g) for g in groups]
    modes = modes or ["all"] * ng
    fan = [len(_push_peers(m, 0, 0, 0)) for m in modes]
    lands = [lax.empty(b.shape if sliced else (N_DEV, *b.shape), b.dtype) for b in flat]

    def body(*refs):
        ins, lnd = refs[:n], refs[n : 2 * n]
        sems = refs[2 * n + len(after) : 2 * n + len(after) + 2 * ng]
        token = refs[-1]
        x, y, c = _place()
        me = _dev_index((x, y, c))
        first = 0
        for gi, size in enumerate(sizes):
            for k, peer in enumerate(_push_peers(modes[gi], x, y, c)):
                for j in range(first, first + size):
                    sem = (j - first) * fan[gi] + k
                    pltpu.make_async_remote_copy(
                        src_ref=ins[j].at[_dev_index(peer)] if sliced else ins[j], dst_ref=lnd[j].at[me],
                        send_sem=sems[2 * gi].at[sem], recv_sem=sems[2 * gi + 1].at[sem],
                        device_id=peer, device_id_type=MESH).start()
            first += size
        token[...] = jnp.zeros_like(token)

    out_shape = []
    for size, width in zip(sizes, fan):
        out_shape += [pltpu.SemaphoreType.DMA((width * size,)), pltpu.SemaphoreType.DMA((width * size,))]
    out_shape += [pltpu.HBM(b.shape, b.dtype) for b in flat + lands]
    out_shape.append(jax.ShapeDtypeStruct((V7X_SUBLANES, V7X_LANES), F32))
    res = pl.pallas_call(
        body,
        name=name,
        out_shape=tuple(out_shape),
        in_specs=[HBM_SPEC] * (2 * n) + [ANY_SPEC] * len(after),
        out_specs=tuple([SEM_SPEC] * (2 * ng) + [HBM_SPEC] * (2 * n) + [pl.BlockSpec(memory_space=pltpu.VMEM)]),
        input_output_aliases={i: 2 * ng + i for i in range(2 * n)},
        compiler_params=pltpu.CompilerParams(has_side_effects=EFFECT),
    )(*[pltpu.with_memory_space_constraint(b, pltpu.HBM) for b in flat + lands], *after)
    sems, thru, token = res[: 2 * ng], res[2 * ng : 2 * ng + 2 * n], res[-1]
    out, first = [], 0
    for gi, size in enumerate(sizes):
        out.append((sems[2 * gi], sems[2 * gi + 1], list(thru[first : first + size]),
                    list(thru[n + first : n + first + size])))
        first += size
    return out, token


def _push_wait(started, sliced, after, name, mode="all"):
    send_sems, recv_sems, bufs, lands = started
    n = len(bufs)
    fan = len(_push_peers(mode, 0, 0, 0))

    def body(*refs):
        ins, lnd = refs[:n], refs[n : 2 * n]
        send_ref, recv_ref = refs[2 * n], refs[2 * n + 1]
        x, y, c = _place()
        for k, peer in enumerate(_push_peers(mode, x, y, c)):
            for j in range(n):
                cp = pltpu.make_async_remote_copy(
                    src_ref=ins[j].at[_dev_index(peer)] if sliced else ins[j], dst_ref=lnd[j].at[_dev_index(peer)],
                    send_sem=send_ref.at[j * fan + k], recv_sem=recv_ref.at[j * fan + k],
                    device_id=peer, device_id_type=MESH)
                cp.wait_send()
                cp.wait_recv()
        if not sliced:
            for j in range(n):
                pltpu.sync_copy(ins[j], lnd[j].at[_dev_index((x, y, c))])

    res = pl.pallas_call(
        body,
        name=name,
        out_shape=tuple(pltpu.HBM(b.shape, b.dtype) for b in bufs + lands),
        in_specs=[HBM_SPEC] * (2 * n) + [SEM_SPEC, SEM_SPEC, pl.BlockSpec(memory_space=pl.ANY)],
        out_specs=tuple([HBM_SPEC] * (2 * n)),
        input_output_aliases={i: i for i in range(2 * n)},
        compiler_params=pltpu.CompilerParams(has_side_effects=EFFECT),
    )(*bufs, *lands, send_sems, recv_sems, after)
    return list(res[:n]), list(res[n:])


def _forward_copies(lnd, send_ref, recv_ref, incoming):
    x, y, c = _place()
    copies = []
    for k, chip in enumerate([(1 - x, y), (x, 1 - y), (1 - x, 1 - y)]):
        mine, theirs = _dev_index((*chip, c)), _dev_index((*chip, 1 - c))
        for j, ref in enumerate(lnd):
            copies.append(pltpu.make_async_remote_copy(
                src_ref=ref.at[mine], dst_ref=ref.at[theirs if incoming else mine],
                send_sem=send_ref.at[j * 3 + k], recv_sem=recv_ref.at[j * 3 + k],
                device_id=(x, y, 1 - c), device_id_type=MESH))
    return copies


def _forward_start(lands, name):
    n = len(lands)

    def body(*refs):
        for cp in _forward_copies(refs[:n], refs[n], refs[n + 1], False):
            cp.start()

    res = pl.pallas_call(
        body,
        name=name,
        out_shape=(pltpu.SemaphoreType.DMA((3 * n,)), pltpu.SemaphoreType.DMA((3 * n,)),
                   *[pltpu.HBM(b.shape, b.dtype) for b in lands]),
        in_specs=[HBM_SPEC] * n,
        out_specs=(SEM_SPEC, SEM_SPEC, *[HBM_SPEC] * n),
        input_output_aliases={i: 2 + i for i in range(n)},
        compiler_params=pltpu.CompilerParams(has_side_effects=EFFECT),
    )(*[pltpu.with_memory_space_constraint(b, pltpu.HBM) for b in lands])
    return res[0], res[1], list(res[2:])


def _forward_wait(started, after, name):
    send_sems, recv_sems, lands = started
    n = len(lands)

    def body(*refs):
        for cp in _forward_copies(refs[:n], refs[n], refs[n + 1], True):
            cp.wait_send()
            cp.wait_recv()

    res = pl.pallas_call(
        body,
        name=name,
        out_shape=tuple(pltpu.HBM(b.shape, b.dtype) for b in lands),
        in_specs=[HBM_SPEC] * n + [SEM_SPEC, SEM_SPEC, pl.BlockSpec(memory_space=pl.ANY)],
        out_specs=tuple([HBM_SPEC] * n),
        input_output_aliases={i: i for i in range(n)},
        compiler_params=pltpu.CompilerParams(has_side_effects=EFFECT),
    )(*lands, send_sems, recv_sems, after)
    return list(res)


def _cols_full(g):
    return jnp.transpose(g, (1, 0, 2)).reshape(g.shape[1], -1)


def _rows_full(g):
    return g.reshape(-1, g.shape[2])


def _cols_parts(full, n=N_DEV):
    r = full.shape[0]
    return jnp.transpose(full.reshape(r, n, -1), (1, 0, 2)).astype(BF16)


def _rows_parts(full):
    return full.reshape(N_DEV, -1, full.shape[1]).astype(BF16)


def _block_diag(w):
    eye = jnp.eye(LRU_BLOCKS, dtype=w.dtype)
    return jnp.einsum("nkj,nm->nkmj", w, eye).reshape(LRU_WIDTH, LRU_WIDTH)


def _pad_rows(v, rows):
    flat = v.reshape(-1)
    return jnp.pad(flat, (0, rows * D_MODEL - flat.shape[0])).reshape(rows, D_MODEL)


def _my_cols(full, me, width):
    return lax.dynamic_slice_in_dim(full, me * width, width, axis=full.ndim - 1)


def kernel(x, c, w_ada, b_ada, norm_pre, norm_post, ffn1_w_gu, ffn1_w_down, w_in, rel_bias, conv_w, conv_b, lru_wa, lru_ba, lru_wx, lru_bx, lru_lambda, w_att_o, w_rec_o, w_out, ffn2_w_gu, ffn2_w_down, loss_target, m_w_ada, m_b_ada, m_norm_pre, m_norm_post, m_ffn1_w_gu, m_ffn1_w_down, m_w_in, m_rel_bias, m_conv_w, m_conv_b, m_lru_wa, m_lru_ba, m_lru_wx, m_lru_bx, m_lru_lambda, m_w_att_o, m_w_rec_o, m_w_out, m_ffn2_w_gu, m_ffn2_w_down, v_w_ada, v_b_ada, v_norm_pre, v_norm_post, v_ffn1_w_gu, v_ffn1_w_down, v_w_in, v_rel_bias, v_conv_w, v_conv_b, v_lru_wa, v_lru_ba, v_lru_wx, v_lru_bx, v_lru_lambda, v_w_att_o, v_w_rec_o, v_w_out, v_ffn2_w_gu, v_ffn2_w_down):
    weights = dict(w_ada=w_ada, b_ada=b_ada, norm_pre=norm_pre, norm_post=norm_post, ffn1_w_gu=ffn1_w_gu,
                   ffn1_w_down=ffn1_w_down, w_in=w_in, rel_bias=rel_bias, conv_w=conv_w, conv_b=conv_b,
                   lru_wa=lru_wa, lru_ba=lru_ba, lru_wx=lru_wx, lru_bx=lru_bx, lru_lambda=lru_lambda,
                   w_att_o=w_att_o, w_rec_o=w_rec_o, w_out=w_out, ffn2_w_gu=ffn2_w_gu, ffn2_w_down=ffn2_w_down)
    mom1 = dict(w_ada=m_w_ada, b_ada=m_b_ada, norm_pre=m_norm_pre, norm_post=m_norm_post, ffn1_w_gu=m_ffn1_w_gu,
                ffn1_w_down=m_ffn1_w_down, w_in=m_w_in, rel_bias=m_rel_bias, conv_w=m_conv_w, conv_b=m_conv_b,
                lru_wa=m_lru_wa, lru_ba=m_lru_ba, lru_wx=m_lru_wx, lru_bx=m_lru_bx, lru_lambda=m_lru_lambda,
                w_att_o=m_w_att_o, w_rec_o=m_w_rec_o, w_out=m_w_out, ffn2_w_gu=m_ffn2_w_gu, ffn2_w_down=m_ffn2_w_down)
    mom2 = dict(w_ada=v_w_ada, b_ada=v_b_ada, norm_pre=v_norm_pre, norm_post=v_norm_post, ffn1_w_gu=v_ffn1_w_gu,
                ffn1_w_down=v_ffn1_w_down, w_in=v_w_in, rel_bias=v_rel_bias, conv_w=v_conv_w, conv_b=v_conv_b,
                lru_wa=v_lru_wa, lru_ba=v_lru_ba, lru_wx=v_lru_wx, lru_bx=v_lru_bx, lru_lambda=v_lru_lambda,
                w_att_o=v_w_att_o, w_rec_o=v_w_rec_o, w_out=v_w_out, ffn2_w_gu=v_ffn2_w_gu, ffn2_w_down=v_ffn2_w_down)
    order = list(weights)
    big = ["ffn1_w_gu", "ffn1_w_down", "w_in", "w_att_o", "w_rec_o", "w_out", "ffn2_w_gu", "ffn2_w_down"]
    col_sharded = {"ffn1_w_gu", "w_in", "w_att_o", "ffn2_w_gu"}
    small = ["b_ada", "norm_pre", "norm_post", "rel_bias", "conv_w", "conv_b", "lru_wa", "lru_ba", "lru_wx",
             "lru_bx", "lru_lambda"]

    xi, yi, ci = _place()
    me = _dev_index((xi, yi, ci))
    x0 = x[0]
    target = loss_target[0]
    fuse_tm = min(FUSE_TM, x0.shape[0])

    transposed = {"ffn1_w_gu", "w_in", "ffn2_w_gu"}
    local = lambda n, arr: jnp.transpose(arr[0]) if n in transposed else arr[0]
    shards = {n: local(n, weights[n]).astype(BF16) for n in big}
    full_of = lambda n, g: _cols_full(g) if n == "w_att_o" else _rows_full(g)

    pack = jnp.concatenate([c.reshape(-1), norm_pre.reshape(-1), norm_post.reshape(-1), conv_w.reshape(-1)])
    pack = jnp.pad(pack, (0, 3072 - pack.shape[0])).reshape(8, 384)
    got = _allgather_vmem(pack, "gather_small_inputs").reshape(N_DEV, 3072)
    c_all = got[:, :1024]
    unshard = lambda blk, rows: jnp.transpose(blk.reshape(N_DEV, rows, 128), (1, 0, 2)).reshape(rows, D_MODEL)
    g_pre = unshard(got[:, 1024:1408], 3)
    g_post = unshard(got[:, 1408:1792], 3)
    conv_taps = unshard(got[:, 1792:2304], 4)
    conv_w8 = jnp.concatenate([conv_taps, jnp.zeros((4, LRU_WIDTH), F32)], axis=0)

    mod_cols = _ada_fwd(c_all, w_ada[0], "ada_fwd")
    mod_all = _allgather_vmem(mod_cols, "gather_mod").reshape(N_DEV, N_DEV, 1152)
    mod = lax.dynamic_index_in_dim(mod_all, me, axis=1, keepdims=False).reshape(1, -1) + b_ada
    mod = mod.reshape(3, 3, 1, D_MODEL)

    w_slab = _slab_weights(lru_wa[0], lru_wx[0])
    bias = _bias_tile(rel_bias[0])

    res_w = (0.5, 1.0, 0.5)
    row = lambda v: v.reshape(1, -1)

    (w1_gu,) = _allgather_hbm([shards["ffn1_w_gu"]], "gather_ffn1_w_gu")
    weight_groups = [["ffn1_w_down"], ["w_in"], ["w_att_o", "w_rec_o", "w_out"], ["ffn2_w_gu", "ffn2_w_down"]]
    weight_modes = ["all", "chip", "all", "all"]
    weights_started, started = _push_start([[shards[n] for n in g] for g in weight_groups], False,
                                           "gather_weights_start", after=(mod, w1_gu), modes=weight_modes)
    full = {"ffn1_w_gu": _rows_full(w1_gu)}

    def gathered_group(gi, after):
        sent, lands = _push_wait(weights_started[gi], False, after, f"gather_weights_wait{gi}", mode=weight_modes[gi])
        if weight_modes[gi] == "chip":
            lands = _forward_wait(_forward_start(lands, f"gather_weights_forward{gi}"), sent[0],
                                  f"gather_weights_forward_wait{gi}")
        for n, land in zip(weight_groups[gi], lands):
            full[n] = full_of(n, land)

    def ffn_fwd(xin, k, gi, tag, deps=(), target=None):
        h, a, g, u = _pre_up(xin, row(g_pre[k]), mod[k, 0], mod[k, 1], full[f"{tag}_w_gu"], f"{tag}_up", deps=deps)
        if f"{tag}_w_down" not in full:
            gathered_group(gi, a)
        f, *out = _matmul_post(a, full[f"{tag}_w_down"], xin, row(g_post[k]), mod[k, 2], res_w[k], f"{tag}_down",
                               target=target)
        return (out[0] if target is None else out), (h, g, u, a, f)

    x1, saved1 = ffn_fwd(x0, 0, 0, "ffn1", deps=(started,))

    gathered_group(1, x1)
    h2, proj = _pre_matmul(x1, row(g_pre[1]), mod[1, 0], mod[1, 1], full["w_in"], "mix_in",
                           b_shift=3 * ATT_WIDTH // 512)
    att_o = _attn_fwd(proj, bias, "attn_fwd")
    gathered_group(2, att_o)
    xc, pre, a_t, u_t = _lru_front(proj, conv_w8, conv_b, w_slab, lru_ba, lru_bx, lru_lambda, "lru_front")
    hs, h_prev, rec_in = _scan_fwd(a_t, u_t, proj, "lru_scan")
    att = _matmul(att_o, full["w_att_o"], "nn", F32, "att_out")
    rec = _matmul(rec_in, full["w_rec_o"], "nn", F32, "rec_out")
    merged, f2, x2 = _merge_matmul_post(att, rec, proj, full["w_out"], x1, row(g_post[1]), mod[1, 2], res_w[1],
                                        "mix_out")

    gathered_group(3, x2)
    (dy, sq), saved3 = ffn_fwd(x2, 2, 2, "ffn2", target=target)
    loss = lax.psum(0.5 * jnp.sum(sq) / D_MODEL, ("x", "y", "c"))

    grads = {}
    norm_sums = [None] * 6

    pending = []

    def exchange_start(names, tag, after=()):
        send = [(_cols_parts if n == "w_att_o" else _rows_parts)(grads[n]) for n in names]
        (group,), token = _push_start([send], True, f"exchange_{tag}_start", after=after)
        pending.append((names, send, group, tag))
        return token

    def exchange_finish(names, send, group, tag, after):
        sent, lands = _push_wait(group, True, after, f"exchange_{tag}_wait")
        res = None
        for n, land, mine in zip(names, lands, sent):
            res = _adamw_parts(land, mine, me, local(n, weights[n]), local(n, mom1[n]), local(n, mom2[n]),
                               f"adamw_{n}")
            back = (lambda r: jnp.transpose(r)) if n in transposed else (lambda r: r)
            out_g[n], out_d[n], out_m[n], out_v[n] = [back(r).reshape(weights[n].shape) for r in res]
        return res[0]

    out_g, out_d, out_m, out_v = {}, {}, {}, {}

    def ffn_bwd(xin, k, saved, dout, tag):
        h, g, u, a, f = saved
        w_gu, w_down = f"{tag}_w_gu", f"{tag}_w_down"
        chunked = k == 2
        df, dgu, norm_sums[2 * k + 1] = _post_bwd_up_bwd(f, dout, row(g_post[k]), mod[k, 2], res_w[k], full[w_down],
                                                          g, u, f"{tag}_up_bwd", chunked=chunked)
        grads[w_down] = _matmul(a, df, "tn", BF16, f"{tag}_dw_down", tm=1408, tn=1024, tk=DW_TK)
        started = exchange_start([w_down], w_down)
        grads[w_gu] = _dw_gu(dgu, h, f"{tag}_dw_gu", deps=(started,))
        started = exchange_start([w_gu], w_gu)
        halves = [(dgu, (None, fuse_tm, D_FF), lambda i, half=half: (half, i, 0), (half * D_FF, (half + 1) * D_FF))
                  for half in range(2)]
        dx, norm_sums[2 * k] = _matmul_pre_bwd(halves, full[w_gu], xin, dout, row(g_pre[k]),
                                                                mod[k, 0], mod[k, 1], f"{tag}_dh", deps=(started,))
        return dx

    dx2 = ffn_bwd(x2, 2, saved3, dy, "ffn2")

    df2, datt, drec, dg_att, dg_rec, norm_sums[3] = _post_bwd_merge_bwd(
        f2, dx2, row(g_post[1]), mod[1, 2], res_w[1], full["w_out"], att, rec, proj, "mix_dmerged")
    grads["w_out"] = _matmul(merged, df2, "tn", BF16, "mix_dw_out", tm=1024, tn=1024, tk=DW_TK)
    datt_o = _matmul(datt, full["w_att_o"], "nt", BF16, "att_out_bwd")
    grads["w_att_o"] = _matmul(att_o, datt, "tn", BF16, "dw_att_o", tm=512, tn=1024, tk=DW_TK)
    grads["w_rec_o"] = _matmul(rec_in, drec, "tn", BF16, "dw_rec_o", tm=1024, tn=1024, tk=DW_TK)
    started = exchange_start(["w_out", "w_att_o", "w_rec_o"], "mix_out")
    dhs, dyr = _matmul_recin_bwd(drec, full["w_rec_o"], hs, proj, "rec_out_bwd", deps=(started,))
    g_t = _scan_bwd(a_t, dhs, "lru_scan_bwd")
    dpre, dxc, lru_sums = _lru_back(pre, xc, w_slab, lru_ba, lru_bx, lru_lambda, g_t, h_prev, "lru_back")
    dxr, conv_sums = _conv_bwd(proj, conv_w8, dxc, "conv_bwd")
    dq, dk, dv, dbias = _attn_bwd(proj, bias, datt_o, "attn_bwd")
    dproj = jnp.concatenate([dq, dk, dv, dxr, dyr, dg_att, dg_rec], axis=1)
    grads["w_in"] = _matmul(dproj, h2, "tn", BF16, "mix_dw_in", tm=1408, tn=1024, tk=DW_TK)
    pack_mix = jnp.concatenate([conv_sums, lru_sums, _pad_rows(_bias_grad(dbias), V7X_SUBLANES),
                                _lru_dw(xc, dpre, "lru_dw").reshape(128, D_MODEL)], axis=0)
    (mix_started,), started = _push_start([[pack_mix]], False, "small_grads_mix_start")
    started = exchange_start(["w_in"], "w_in", after=(started,))
    whole = [(dproj, (fuse_tm, PROJ_WIDTH), lambda i: (i, 0), (0, PROJ_WIDTH))]
    dx1, norm_sums[2] = _matmul_pre_bwd(whole, full["w_in"], x1, dx2, row(g_pre[1]), mod[1, 0],
                                                             mod[1, 1], "mix_dh", deps=(started,))

    dx0 = ffn_bwd(x0, 0, saved1, dx1, "ffn1")

    pack_norm = jnp.concatenate(norm_sums, axis=0)
    (norm_started,), _ = _push_start([[pack_norm]], False, "small_grads_norm_start")

    def summed(started, pack, after, tag):
        _, (parts,) = _push_wait(started, False, after, f"small_grads_{tag}_wait")
        return parts, _sum_parts(parts, f"small_grads_{tag}_sum")

    done = dx0
    last = [p for p in pending if p[3].startswith("ffn1")]
    for names, send, group, tag in pending:
        if not tag.startswith("ffn1"):
            done = exchange_finish(names, send, group, tag, done)

    _, total = summed(mix_started, pack_mix, done, "mix")
    grads["conv_w"] = _my_cols(total[0:4], me, 128)
    grads["conv_b"] = total[4:5]
    grads["lru_ba"] = total[8:9]
    grads["lru_bx"] = total[9:10]
    grads["lru_lambda"] = total[10:11]
    grads["rel_bias"] = total[16:19].reshape(-1)[: ATT_HEADS * (2 * MAX_REL + 1)].reshape(ATT_HEADS, -1)
    grads["lru_wa"] = total[24:88].reshape(LRU_BLOCKS, LRU_BLOCK, LRU_BLOCK)
    grads["lru_wx"] = total[88:152].reshape(LRU_BLOCKS, LRU_BLOCK, LRU_BLOCK)
    parts, total = summed(norm_started, pack_norm, total, "norm")
    by_sandwich = lambda v: v.reshape(*v.shape[:-2], 3, 2 * V7X_SUBLANES, D_MODEL)
    dmod_of = lambda v: jnp.concatenate([by_sandwich(v)[..., 1:3, :], by_sandwich(v)[..., 9:10, :]], axis=-2)
    grads["b_ada"] = dmod_of(total).reshape(1, -1)
    grads["norm_pre"] = _my_cols(by_sandwich(total)[:, 0, :], me, 128)
    grads["norm_post"] = _my_cols(by_sandwich(total)[:, V7X_SUBLANES, :], me, 128)
    dmod_all = dmod_of(parts).reshape(N_DEV, 9 * D_MODEL)
    grads["w_ada"] = _ada_bwd(c_all, _my_cols(dmod_all, me, 1152), "ada_bwd")

    res = _adamw(grads["w_ada"], w_ada[0], m_w_ada[0], v_w_ada[0], "adamw_w_ada")
    out_g["w_ada"], out_d["w_ada"], out_m["w_ada"], out_v["w_ada"] = [r.reshape(w_ada.shape) for r in res]

    sizes = [int(np.prod(weights[n].shape)) for n in small]
    tot = sum(sizes)
    rows_small = -(-tot // (16 * D_MODEL)) * 16
    flat = lambda arrs: jnp.pad(jnp.concatenate([a.reshape(-1) for a in arrs]),
                                (0, rows_small * D_MODEL - tot)).reshape(rows_small, D_MODEL)
    res = _adamw(flat([grads[n] for n in small]), flat([weights[n] for n in small]),
                 flat([mom1[n] for n in small]), flat([mom2[n] for n in small]), "adamw_small", rows=rows_small)
    offs = np.cumsum([0] + sizes)
    for dst, r in zip((out_g, out_d, out_m, out_v), res):
        rf = r.reshape(-1)
        for i, n in enumerate(small):
            dst[n] = rf[offs[i] : offs[i + 1]].reshape(weights[n].shape)

    done = res[0]
    for names, send, group, tag in last:
        done = exchange_finish(names, send, group, tag, done)

    return (loss, dx0[None], *[out_g[n] for n in order], *[out_d[n] for n in order],
            *[out_m[n] for n in order], *[out_v[n] for n in order])
```

```python
import functools

import jax
import jax.numpy as jnp
import numpy as np
from jax import lax
from jax.experimental import pallas as pl
from jax.experimental.pallas import tpu as pltpu

D_MODEL = 1024
D_FF = 2816
ATT_HEADS = 8
ATT_HEAD_DIM = 64
ATT_WIDTH = 512
CHUNK = 64
LEFT_CHUNKS = 8
MAX_REL = 128
LRU_WIDTH = 1024
LRU_BLOCKS = 16
LRU_BLOCK = 64
LRU_C = 8.0
EPS = 1e-6
PROJ_WIDTH = 5632
N_DEV = 8

ADAM_LR = 0.001
ADAM_B1 = 0.9
ADAM_B2 = 0.999
ADAM_EPS = 1e-08
ADAM_WD = 0.01
ADAM_STEP = 10

V7X_LANES = 128
V7X_SUBLANES = 8
V7X_VMEM_BYTES = 64 * 1024 * 1024
VMEM_LIMIT = V7X_VMEM_BYTES - 8 * 1024 * 1024

ATT_TQ = 256
NEG = -1e30
BF16 = jnp.bfloat16
F32 = jnp.float32
MESH = pl.DeviceIdType.MESH

OFF_Q = 4 * LRU_WIDTH
OFF_K = OFF_Q + ATT_WIDTH
OFF_V = OFF_K + ATT_WIDTH


def _cparams(**kw):
    return pltpu.CompilerParams(vmem_limit_bytes=VMEM_LIMIT, **kw)


def _pick(n, target, unit=V7X_LANES):
    best = None
    for t in range(unit, min(n, target) + 1, unit):
        if n % t == 0:
            best = t
    return n if best is None else best


_DIMS = {
    "nn": (((1,), (0,)), ((), ())),
    "nt": (((1,), (1,)), ((), ())),
    "tn": (((0,), (0,)), ((), ())),
}


ANY_SPEC = pl.BlockSpec(memory_space=pl.ANY)


def _matmul(a, b, mode, out_dtype, name, tm=1024, tn=512, tk=1408, deps=(), b_shift=0):
    n_deps = len(deps)
    if mode == "nn":
        (m, k), (k2, n) = a.shape, b.shape
    elif mode == "nt":
        (m, k), (n, k2) = a.shape, b.shape
    else:
        (k, m), (k2, n) = a.shape, b.shape
    assert k == k2, (a.shape, b.shape, mode)
    tm, tn, tk = _pick(m, tm), _pick(n, tn), _pick(k, tk)
    nk = k // tk
    dims = _DIMS[mode]

    def body(a_ref, b_ref, *rest):
        o_ref, scratch = rest[n_deps], rest[n_deps + 1 :]
        p = lax.dot_general(a_ref[...], b_ref[...], dims, preferred_element_type=F32)
        if nk == 1:
            o_ref[...] = p.astype(o_ref.dtype)
        else:
            acc = scratch[0]
            kk = pl.program_id(2)

            @pl.when(kk == 0)
            def _():
                acc[...] = p

            @pl.when(kk > 0)
            def _():
                acc[...] += p

            @pl.when(kk == nk - 1)
            def _():
                o_ref[...] = acc[...].astype(o_ref.dtype)

    if mode == "nn":
        a_spec = pl.BlockSpec((tm, tk), lambda i, j, kk: (i, kk))
        b_spec = pl.BlockSpec((tk, tn), lambda i, j, kk: (kk, j))
    elif mode == "nt":
        a_spec = pl.BlockSpec((tm, tk), lambda i, j, kk: (i, kk))
        b_spec = pl.BlockSpec((tn, tk), lambda i, j, kk: ((j + b_shift) % (n // tn), kk))
    else:
        a_spec = pl.BlockSpec((tk, tm), lambda i, j, kk: (kk, i))
        b_spec = pl.BlockSpec((tk, tn), lambda i, j, kk: (kk, j))
    return pl.pallas_call(
        body,
        name=name,
        grid=(m // tm, n // tn, nk),
        in_specs=[a_spec, b_spec] + [ANY_SPEC] * n_deps,
        out_specs=pl.BlockSpec((tm, tn), lambda i, j, kk: (i, j)),
        out_shape=jax.ShapeDtypeStruct((m, n), out_dtype),
        scratch_shapes=[pltpu.VMEM((tm, tn), F32)] if nk > 1 else [],
        compiler_params=_cparams(dimension_semantics=("parallel", "parallel", "arbitrary")),
    )(a, b, *deps)


def _rowwise(fn, name, params, tiles, outs, accs=(), ts=256, with_index=False, deps=()):
    norm = []
    for t in tiles:
        if not isinstance(t, tuple):
            t = (t, t.shape[1], 0)
        norm.append(t if len(t) == 4 else (*t, None))
    s = norm[0][0].shape[0]
    ts = min(ts, s)
    assert s % ts == 0 and ts % V7X_SUBLANES == 0
    steps = s // ts
    halo_blocks = ts // V7X_SUBLANES
    n_p, n_t, n_o = len(params), len(norm), len(outs)

    def body(*refs):
        i = pl.program_id(0)
        vals = [r[...] for r in refs[: n_p + n_t]]
        res = fn(i, steps, *vals) if with_index else fn(*vals)
        if not isinstance(res, (tuple, list)):
            res = (res,)
        first_out = n_p + n_t + len(deps)
        o_refs = refs[first_out : first_out + n_o]
        a_refs = refs[first_out + n_o :]
        for r, v in zip(o_refs, res[:n_o]):
            r[...] = v.astype(r.dtype)
        for r, v in zip(a_refs, res[n_o:]):
            _accumulate(r, v, i)

    in_specs = [pl.BlockSpec(p.shape, lambda i: (0, 0)) for p in params]
    for arr, w, cb, halo in norm:
        if halo is None:
            in_specs.append(pl.BlockSpec((ts, w), lambda i, cb=cb: (i, cb)))
        elif halo == "prev":
            in_specs.append(
                pl.BlockSpec((V7X_SUBLANES, w), lambda i, cb=cb: (jnp.maximum(i * halo_blocks - 1, 0), cb))
            )
        else:
            last = s // V7X_SUBLANES - 1
            in_specs.append(
                pl.BlockSpec((V7X_SUBLANES, w), lambda i, cb=cb: (jnp.minimum((i + 1) * halo_blocks, last), cb))
            )
    in_specs += [ANY_SPEC] * len(deps)
    out_specs = [pl.BlockSpec((ts, w), lambda i: (i, 0)) for w, _ in outs]
    out_specs += [pl.BlockSpec(shape, lambda i: (0, 0)) for shape in accs]
    out_shape = [jax.ShapeDtypeStruct((s, w), dt) for w, dt in outs]
    out_shape += [jax.ShapeDtypeStruct(shape, F32) for shape in accs]
    res = pl.pallas_call(
        body,
        name=name,
        grid=(steps,),
        in_specs=in_specs,
        out_specs=out_specs,
        out_shape=out_shape,
        compiler_params=_cparams(dimension_semantics=("arbitrary",)),
    )(*params, *[t[0] for t in norm], *deps)
    return res


def _accumulate(ref, val, step):
    @pl.when(step == 0)
    def _():
        ref[...] = val

    @pl.when(step > 0)
    def _():
        ref[...] += val


def _sigmoid(z):
    return jax.nn.sigmoid(z)


def _silu(z):
    return z * _sigmoid(z)


def _gelu(z):
    return 0.5 * z * (1.0 + jnp.tanh(0.7978845608028654 * (z + 0.044715 * (z * z * z))))


def _pre_fn(g, shift, scale, x):
    r = lax.rsqrt(jnp.mean(x * x, axis=-1, keepdims=True) + EPS)
    return ((x * r) * g) * (1.0 + scale) + shift


def _post_fn(res_w, g, gate, f, x):
    r = lax.rsqrt(jnp.mean(f * f, axis=-1, keepdims=True) + EPS)
    return x + (res_w * gate) * ((f * r) * g)


def _swiglu_fn(gu):
    return _silu(gu[:, :D_FF]) * gu[:, D_FF:]


def _gates_fn(ba, bx, lam, pre, xc):
    ra = _sigmoid(pre[:, :LRU_WIDTH] + ba)
    ia = _sigmoid(pre[:, LRU_WIDTH:] + bx)
    softplus = jnp.maximum(-lam, 0.0) + jnp.log1p(jnp.exp(-jnp.abs(lam)))
    log_a = (-LRU_C) * ra * softplus
    a = jnp.exp(log_a)
    mult = jnp.sqrt(-jnp.tanh(log_a) * (a * a + 1.0))
    return a, mult * (ia * xc)


def _recin_fn(hs, yr):
    return hs * _gelu(yr)


def _merge_fn(att, rec, g_att, g_rec):
    return _sigmoid(g_att) * att + _sigmoid(g_rec) * rec


def _rowsum(v):
    return jnp.sum(v, axis=0, keepdims=True)


def _pre_fwd(x, g, shift, scale, name, deps=()):
    (h,) = _rowwise(_pre_fn, name, [g, shift, scale], [x], [(D_MODEL, BF16)], deps=deps)
    return h


def _pre_bwd(x, g, shift, scale, dh, dres, name):
    def fn(g, shift, scale, x, dh, dres):
        _, vjp = jax.vjp(_pre_fn, g, shift, scale, x)
        dg, dshift, dscale, dx = vjp(dh)
        return dx + dres, dg, dshift, dscale

    row = (1, D_MODEL)
    return _rowwise(fn, name, [g, shift, scale], [x, dh, dres], [(D_MODEL, F32)], [row, row, row])


def _post_fwd(f, x, g, gate, res_w, name):
    (y,) = _rowwise(functools.partial(_post_fn, res_w), name, [g, gate], [f, x], [(D_MODEL, F32)])
    return y


def _post_bwd(f, g, gate, res_w, dy, name, deps=()):
    def fn(g, gate, f, dy):
        _, vjp = jax.vjp(lambda g, gate, f: _post_fn(res_w, g, gate, f, 0.0), g, gate, f)
        dg, dgate, df = vjp(dy)
        return df, dg, dgate

    row = (1, D_MODEL)
    return _rowwise(fn, name, [g, gate], [f, dy], [(D_MODEL, BF16)], [row, row], deps=deps)


def _loss_stage(y, target, name):
    def fn(y, t):
        diff = y - t
        return diff * (1.0 / D_MODEL), _rowsum(diff * diff)

    return _rowwise(fn, name, [], [y, target], [(D_MODEL, F32)], [(1, D_MODEL)])


FFN_TM = 512
FFN_TF = 1408


def _glu_fn(g, u):
    return _silu(g) * u


def _ffn_up(h, w_gu_t, name):
    s = h.shape[0]
    tm = min(FFN_TM, s)
    nf = D_FF // FFN_TF

    def body(h_ref, wg_ref, wu_ref, a_ref, g_ref, u_ref):
        hv = h_ref[...]
        g = lax.dot_general(hv, wg_ref[...], _DIMS["nt"], preferred_element_type=F32)
        u = lax.dot_general(hv, wu_ref[...], _DIMS["nt"], preferred_element_type=F32)
        a_ref[...] = _glu_fn(g, u).astype(a_ref.dtype)
        g_ref[...] = g.astype(g_ref.dtype)
        u_ref[...] = u.astype(u_ref.dtype)

    out = pl.BlockSpec((tm, FFN_TF), lambda i, j: (i, j))
    return pl.pallas_call(
        body,
        name=name,
        grid=(s // tm, nf),
        in_specs=[pl.BlockSpec((tm, D_MODEL), lambda i, j: (i, 0)),
                  pl.BlockSpec((FFN_TF, D_MODEL), lambda i, j: (j, 0)),
                  pl.BlockSpec((FFN_TF, D_MODEL), lambda i, j: (nf + j, 0))],
        out_specs=[out, out, out],
        out_shape=[jax.ShapeDtypeStruct((s, D_FF), BF16)] * 3,
        compiler_params=_cparams(dimension_semantics=("parallel", "arbitrary")),
    )(h, w_gu_t, w_gu_t)


def _ffn_up_bwd(df, w_down, g, u, name, deps=()):
    s = df.shape[0]
    tm = min(FFN_TM, s)

    def body(df_ref, wd_ref, g_ref, u_ref, *rest):
        dg_ref, du_ref = rest[len(deps) :]
        da = lax.dot_general(df_ref[...], wd_ref[...], _DIMS["nt"], preferred_element_type=F32)
        _, vjp = jax.vjp(_glu_fn, g_ref[...].astype(F32), u_ref[...].astype(F32))
        dg, du = vjp(da)
        dg_ref[...] = dg.astype(dg_ref.dtype)
        du_ref[...] = du.astype(du_ref.dtype)

    blk = pl.BlockSpec((tm, FFN_TF), lambda i, j: (i, j))
    return pl.pallas_call(
        body,
        name=name,
        grid=(s // tm, D_FF // FFN_TF),
        in_specs=[pl.BlockSpec((tm, D_MODEL), lambda i, j: (i, 0)),
                  pl.BlockSpec((FFN_TF, D_MODEL), lambda i, j: (j, 0)), blk, blk] + [ANY_SPEC] * len(deps),
        out_specs=[blk, blk],
        out_shape=[jax.ShapeDtypeStruct((s, D_FF), BF16)] * 2,
        compiler_params=_cparams(dimension_semantics=("parallel", "arbitrary")),
    )(df, w_down, g, u, *deps)


def _ffn_dh(dg, du, w_gu_t, name, deps=()):
    s = dg.shape[0]
    tm, tn = min(FFN_TM, s), 512

    def body(dg_ref, du_ref, wg_ref, wu_ref, *rest):
        o_ref = rest[len(deps)]
        p = jnp.dot(dg_ref[...], wg_ref[...], preferred_element_type=F32)
        o_ref[...] = p + jnp.dot(du_ref[...], wu_ref[...], preferred_element_type=F32)

    a_spec = pl.BlockSpec((tm, D_FF), lambda i, j: (i, 0))
    return pl.pallas_call(
        body,
        name=name,
        grid=(s // tm, D_MODEL // tn),
        in_specs=[a_spec, a_spec,
                  pl.BlockSpec((D_FF, tn), lambda i, j: (0, j)),
                  pl.BlockSpec((D_FF, tn), lambda i, j: (1, j))] + [ANY_SPEC] * len(deps),
        out_specs=pl.BlockSpec((tm, tn), lambda i, j: (i, j)),
        out_shape=jax.ShapeDtypeStruct((s, D_MODEL), F32),
        compiler_params=_cparams(dimension_semantics=("parallel", "arbitrary")),
    )(dg, du, w_gu_t, w_gu_t, *deps)


FUSE_TM = 256
DW_TK = 2048
ROW_SPEC2 = pl.BlockSpec((1, D_MODEL), lambda i, j: (0, 0))
ROW_SPEC1 = pl.BlockSpec((1, D_MODEL), lambda i: (0, 0))
SUMS_SPEC1 = pl.BlockSpec((V7X_SUBLANES, D_MODEL), lambda i: (0, 0))
SUMS_SPEC2 = pl.BlockSpec((V7X_SUBLANES, D_MODEL), lambda i, j: (0, 0))
SUMS_SHAPE = jax.ShapeDtypeStruct((V7X_SUBLANES, D_MODEL), F32)


def _sum_rows(*rows):
    pad = jnp.zeros((V7X_SUBLANES - len(rows), rows[0].shape[1]), F32)
    return jnp.concatenate([*rows, pad], axis=0)


def _pre_up(x, g, shift, scale, w_gu_t, name, deps=()):
    s = x.shape[0]
    tm = min(FFN_TM, s)
    nf = D_FF // FFN_TF
    nd = len(deps)

    def body(x_ref, g_ref, sh_ref, sc_ref, wg_ref, wu_ref, *rest):
        h_ref, a_ref, gg_ref, u_ref, h_s = rest[nd:]

        @pl.when(pl.program_id(1) == 0)
        def _():
            h = _pre_fn(g_ref[...], sh_ref[...], sc_ref[...], x_ref[...]).astype(BF16)
            h_s[...] = h
            h_ref[...] = h

        hv = h_s[...]
        gv = lax.dot_general(hv, wg_ref[...], _DIMS["nt"], preferred_element_type=F32)
        uv = lax.dot_general(hv, wu_ref[...], _DIMS["nt"], preferred_element_type=F32)
        a_ref[...] = _glu_fn(gv, uv).astype(a_ref.dtype)
        gg_ref[...] = gv.astype(gg_ref.dtype)
        u_ref[...] = uv.astype(u_ref.dtype)

    rows = pl.BlockSpec((tm, D_MODEL), lambda i, j: (i, 0))
    out = pl.BlockSpec((tm, FFN_TF), lambda i, j: (i, j))
    return pl.pallas_call(
        body,
        name=name,
        grid=(s // tm, nf),
        in_specs=[rows, ROW_SPEC2, ROW_SPEC2, ROW_SPEC2,
                  pl.BlockSpec((FFN_TF, D_MODEL), lambda i, j: (j, 0)),
                  pl.BlockSpec((FFN_TF, D_MODEL), lambda i, j: (nf + j, 0))] + [ANY_SPEC] * nd,
        out_specs=[rows, out, out, out],
        out_shape=[jax.ShapeDtypeStruct((s, D_MODEL), BF16)] + [jax.ShapeDtypeStruct((s, D_FF), BF16)] * 3,
        scratch_shapes=[pltpu.VMEM((tm, D_MODEL), BF16)],
        compiler_params=_cparams(dimension_semantics=("parallel", "arbitrary")),
    )(x, g, shift, scale, w_gu_t, w_gu_t, *deps)


def _pre_matmul(x, g, shift, scale, w_t, name, b_shift=0, tn=512):
    s = x.shape[0]
    n = w_t.shape[0]
    tm = min(2 * FFN_TM, s)

    def body(x_ref, g_ref, sh_ref, sc_ref, w_ref, h_ref, o_ref, h_s):
        @pl.when(pl.program_id(1) == 0)
        def _():
            h = _pre_fn(g_ref[...], sh_ref[...], sc_ref[...], x_ref[...]).astype(BF16)
            h_s[...] = h
            h_ref[...] = h

        o_ref[...] = lax.dot_general(h_s[...], w_ref[...], _DIMS["nt"], preferred_element_type=F32)

    rows = pl.BlockSpec((tm, D_MODEL), lambda i, j: (i, 0))
    return pl.pallas_call(
        body,
        name=name,
        grid=(s // tm, n // tn),
        in_specs=[rows, ROW_SPEC2, ROW_SPEC2, ROW_SPEC2,
                  pl.BlockSpec((tn, D_MODEL), lambda i, j: ((j + b_shift) % (n // tn), 0))],
        out_specs=[rows, pl.BlockSpec((tm, tn), lambda i, j: (i, j))],
        out_shape=[jax.ShapeDtypeStruct((s, D_MODEL), BF16), jax.ShapeDtypeStruct((s, n), F32)],
        scratch_shapes=[pltpu.VMEM((tm, D_MODEL), BF16)],
        compiler_params=_cparams(dimension_semantics=("parallel", "arbitrary")),
    )(x, g, shift, scale, w_t)


def _matmul_post(a, w, x, g_post, gate, res_w, name, target=None):
    s, k = a.shape
    tm = min(FFN_TM, s)
    extra = [] if target is None else [target]

    def body(a_ref, w_ref, x_ref, g_ref, gate_ref, *rest):
        f = jnp.dot(a_ref[...], w_ref[...], preferred_element_type=F32)
        y = _post_fn(res_w, g_ref[...], gate_ref[...], f, x_ref[...])
        if target is None:
            f_ref, y_ref = rest
            y_ref[...] = y
        else:
            t_ref, f_ref, dy_ref, sq_ref = rest
            diff = y - t_ref[...]
            dy_ref[...] = diff * (1.0 / D_MODEL)
            _accumulate(sq_ref, _rowsum(diff * diff), pl.program_id(0))
        f_ref[...] = f

    rows = pl.BlockSpec((tm, D_MODEL), lambda i: (i, 0))
    out_specs, out_shape = [rows, rows], [jax.ShapeDtypeStruct((s, D_MODEL), F32)] * 2
    if target is not None:
        out_specs.append(ROW_SPEC1)
        out_shape.append(jax.ShapeDtypeStruct((1, D_MODEL), F32))
    return pl.pallas_call(
        body,
        name=name,
        grid=(s // tm,),
        in_specs=[pl.BlockSpec((tm, k), lambda i: (i, 0)), pl.BlockSpec((k, D_MODEL), lambda i: (0, 0)), rows,
                  ROW_SPEC1, ROW_SPEC1] + [rows] * len(extra),
        out_specs=out_specs,
        out_shape=out_shape,
        compiler_params=_cparams(dimension_semantics=("arbitrary",)),
    )(a, w, x, g_post, gate, *extra)


def _merge_matmul_post(att, rec, proj, w, x, g_post, gate, res_w, name):
    s = att.shape[0]
    tm = min(FUSE_TM, s)

    def body(att_ref, rec_ref, ga_ref, gr_ref, w_ref, x_ref, g_ref, gate_ref, m_ref, f_ref, y_ref):
        merged = _merge_fn(att_ref[...], rec_ref[...], ga_ref[...], gr_ref[...]).astype(BF16)
        m_ref[...] = merged
        f = jnp.dot(merged, w_ref[...], preferred_element_type=F32)
        f_ref[...] = f
        y_ref[...] = _post_fn(res_w, g_ref[...], gate_ref[...], f, x_ref[...])

    rows = pl.BlockSpec((tm, D_MODEL), lambda i: (i, 0))
    return pl.pallas_call(
        body,
        name=name,
        grid=(s // tm,),
        in_specs=[rows, rows, pl.BlockSpec((tm, D_MODEL), lambda i: (i, 2)), pl.BlockSpec((tm, D_MODEL), lambda i: (i, 3)),
                  pl.BlockSpec(w.shape, lambda i: (0, 0)), rows, ROW_SPEC1, ROW_SPEC1],
        out_specs=[rows, rows, rows],
        out_shape=[jax.ShapeDtypeStruct((s, D_MODEL), BF16)] + [jax.ShapeDtypeStruct((s, D_MODEL), F32)] * 2,
        compiler_params=_cparams(dimension_semantics=("parallel",)),
    )(att, rec, proj, proj, w, x, g_post, gate)


def _post_bwd_merge_bwd(f, dy, g_post, gate, res_w, w, att, rec, proj, name):
    s = f.shape[0]
    tm = min(FUSE_TM, s)

    def body(f_ref, dy_ref, gp_ref, gate_ref, w_ref, att_ref, rec_ref, ga_ref, gr_ref,
             df_ref, datt_ref, drec_ref, dga_ref, dgr_ref, sums_ref):
        i = pl.program_id(0)
        dgp, dgate, df = _post_vjp(res_w, gp_ref[...], gate_ref[...], f_ref[...], dy_ref[...])
        dfb = df.astype(BF16)
        df_ref[...] = dfb
        _accumulate(sums_ref, _sum_rows(dgp, dgate), i)
        dmerged = lax.dot_general(dfb, w_ref[...], _DIMS["nt"], preferred_element_type=F32)
        _, vjp = jax.vjp(_merge_fn, att_ref[...], rec_ref[...], ga_ref[...], gr_ref[...])
        for ref, val in zip((datt_ref, drec_ref, dga_ref, dgr_ref), vjp(dmerged)):
            ref[...] = val.astype(ref.dtype)

    rows = pl.BlockSpec((tm, D_MODEL), lambda i: (i, 0))
    return pl.pallas_call(
        body,
        name=name,
        grid=(s // tm,),
        in_specs=[rows, rows, ROW_SPEC1, ROW_SPEC1, pl.BlockSpec(w.shape, lambda i: (0, 0)), rows, rows,
                  pl.BlockSpec((tm, D_MODEL), lambda i: (i, 2)), pl.BlockSpec((tm, D_MODEL), lambda i: (i, 3))],
        out_specs=[rows] * 5 + [SUMS_SPEC1],
        out_shape=[jax.ShapeDtypeStruct((s, D_MODEL), BF16)] * 5 + [SUMS_SHAPE],
        compiler_params=_cparams(dimension_semantics=("arbitrary",)),
    )(f, dy, g_post, gate, w, att, rec, proj, proj)


def _matmul_recin_bwd(drec, w, hs, proj, name, deps=()):
    s = drec.shape[0]
    tm = min(FUSE_TM, s)
    nd = len(deps)

    def body(d_ref, w_ref, hs_ref, yr_ref, *rest):
        dhs_ref, dyr_ref = rest[nd:]
        d = lax.dot_general(d_ref[...], w_ref[...], _DIMS["nt"], preferred_element_type=F32)
        _, vjp = jax.vjp(_recin_fn, hs_ref[...], yr_ref[...])
        dhs, dyr = vjp(d)
        dhs_ref[...] = dhs
        dyr_ref[...] = dyr.astype(dyr_ref.dtype)

    rows = pl.BlockSpec((tm, D_MODEL), lambda i: (i, 0))
    return pl.pallas_call(
        body,
        name=name,
        grid=(s // tm,),
        in_specs=[rows, pl.BlockSpec(w.shape, lambda i: (0, 0)), rows,
                  pl.BlockSpec((tm, D_MODEL), lambda i: (i, 1))] + [ANY_SPEC] * nd,
        out_specs=[rows, rows],
        out_shape=[jax.ShapeDtypeStruct((s, D_MODEL), F32), jax.ShapeDtypeStruct((s, D_MODEL), BF16)],
        compiler_params=_cparams(dimension_semantics=("parallel",)),
    )(drec, w, hs, proj, *deps)


def _post_vjp(res_w, g, gate, f, dy):
    _, vjp = jax.vjp(lambda g, gate, f: _post_fn(res_w, g, gate, f, 0.0), g, gate, f)
    return vjp(dy)


def _post_bwd_up_bwd(f, dy, g_post, gate, res_w, w_down, g, u, name, deps=()):
    s = f.shape[0]
    tm = min(FFN_TM, s)
    nd = len(deps)

    def body(f_ref, dy_ref, gp_ref, gate_ref, wd_ref, g_ref, u_ref, *rest):
        df_ref, dgu_ref, sums_ref, df_s = rest[nd:]
        i = pl.program_id(0)

        @pl.when(pl.program_id(1) == 0)
        def _():
            dgp, dgate, df = _post_vjp(res_w, gp_ref[...], gate_ref[...], f_ref[...], dy_ref[...])
            df_s[...] = df.astype(BF16)
            df_ref[...] = df_s[...]
            _accumulate(sums_ref, _sum_rows(dgp, dgate), i)

        da = lax.dot_general(df_s[...], wd_ref[...], _DIMS["nt"], preferred_element_type=F32)
        _, vjp = jax.vjp(_glu_fn, g_ref[...].astype(F32), u_ref[...].astype(F32))
        dg, du = vjp(da)
        dgu_ref[0] = dg.astype(dgu_ref.dtype)
        dgu_ref[1] = du.astype(dgu_ref.dtype)

    rows = pl.BlockSpec((tm, D_MODEL), lambda i, j: (i, 0))
    blk = pl.BlockSpec((tm, FFN_TF), lambda i, j: (i, j))
    return pl.pallas_call(
        body,
        name=name,
        grid=(s // tm, D_FF // FFN_TF),
        in_specs=[rows, rows, ROW_SPEC2, ROW_SPEC2, pl.BlockSpec((FFN_TF, D_MODEL), lambda i, j: (j, 0)), blk,
                  blk] + [ANY_SPEC] * nd,
        out_specs=[rows, pl.BlockSpec((2, tm, FFN_TF), lambda i, j: (0, i, j)), SUMS_SPEC2],
        out_shape=[jax.ShapeDtypeStruct((s, D_MODEL), BF16), jax.ShapeDtypeStruct((2, s, D_FF), BF16), SUMS_SHAPE],
        scratch_shapes=[pltpu.VMEM((tm, D_MODEL), BF16)],
        compiler_params=_cparams(dimension_semantics=("arbitrary", "arbitrary")),
    )(f, dy, g_post, gate, w_down, g, u, *deps)


def _post_bwd_matmul(f, dy, g_post, gate, res_w, w, name):
    s = f.shape[0]
    n = w.shape[0]
    tm = min(FUSE_TM, s)

    def body(f_ref, dy_ref, gp_ref, gate_ref, w_ref, df_ref, o_ref, dgp_ref, dgate_ref):
        i = pl.program_id(0)
        dgp, dgate, df = _post_vjp(res_w, gp_ref[...], gate_ref[...], f_ref[...], dy_ref[...])
        dfb = df.astype(BF16)
        df_ref[...] = dfb
        _accumulate(dgp_ref, dgp, i)
        _accumulate(dgate_ref, dgate, i)
        o_ref[...] = lax.dot_general(dfb, w_ref[...], _DIMS["nt"], preferred_element_type=F32)

    rows = pl.BlockSpec((tm, D_MODEL), lambda i: (i, 0))
    return pl.pallas_call(
        body,
        name=name,
        grid=(s // tm,),
        in_specs=[rows, rows, ROW_SPEC1, ROW_SPEC1, pl.BlockSpec((n, D_MODEL), lambda i: (0, 0))],
        out_specs=[rows, pl.BlockSpec((tm, n), lambda i: (i, 0)), ROW_SPEC1, ROW_SPEC1],
        out_shape=[jax.ShapeDtypeStruct((s, D_MODEL), BF16), jax.ShapeDtypeStruct((s, n), F32),
                   jax.ShapeDtypeStruct((1, D_MODEL), F32), jax.ShapeDtypeStruct((1, D_MODEL), F32)],
        compiler_params=_cparams(dimension_semantics=("arbitrary",)),
    )(f, dy, g_post, gate, w)


def _matmul_pre_bwd(parts, w_t, x, dres, g, shift, scale, name, deps=()):
    s = x.shape[0]
    na, nd = len(parts), len(deps)
    ranges = [p[3] for p in parts]

    def body(*refs):
        a_refs = refs[:na]
        w_ref, x_ref, dres_ref, g_ref, sh_ref, sc_ref = refs[na : na + 6]
        dx_ref, sums_ref = refs[na + 6 + nd :]
        i = pl.program_id(0)
        dh = None
        for a_ref, (r0, r1) in zip(a_refs, ranges):
            p = jnp.dot(a_ref[...], w_ref[r0:r1, :], preferred_element_type=F32)
            dh = p if dh is None else dh + p
        _, vjp = jax.vjp(_pre_fn, g_ref[...], sh_ref[...], sc_ref[...], x_ref[...])
        dg, dsh, dsc, dx = vjp(dh)
        dx_ref[...] = dx + dres_ref[...]
        _accumulate(sums_ref, _sum_rows(dg, dsh, dsc), i)

    tm = parts[0][1][-2]
    rows = pl.BlockSpec((tm, D_MODEL), lambda i: (i, 0))
    return pl.pallas_call(
        body,
        name=name,
        grid=(s // tm,),
        in_specs=[pl.BlockSpec(p[1], p[2]) for p in parts]
        + [pl.BlockSpec(w_t.shape, lambda i: (0, 0)), rows, rows, ROW_SPEC1, ROW_SPEC1, ROW_SPEC1]
        + [ANY_SPEC] * nd,
        out_specs=[rows, SUMS_SPEC1],
        out_shape=[jax.ShapeDtypeStruct((s, D_MODEL), F32), SUMS_SHAPE],
        compiler_params=_cparams(dimension_semantics=("arbitrary",)),
    )(*[p[0] for p in parts], w_t, x, dres, g, shift, scale, *deps)


def _dw_gu(dgu, h, name, deps=(), tk=DW_TK):
    s = h.shape[0]
    tk = min(tk, s)
    nk = s // tk
    half = D_FF // FFN_TF

    def body(a_ref, b_ref, *rest):
        o_ref, acc = rest[len(deps) :]
        kk = pl.program_id(1)
        p = lax.dot_general(a_ref[...], b_ref[...], _DIMS["tn"], preferred_element_type=F32)

        @pl.when(kk == 0)
        def _():
            acc[...] = p

        @pl.when(kk > 0)
        def _():
            acc[...] += p

        @pl.when(kk == nk - 1)
        def _():
            o_ref[...] = acc[...].astype(o_ref.dtype)

    return pl.pallas_call(
        body,
        name=name,
        grid=(2 * half, nk),
        in_specs=[pl.BlockSpec((None, tk, FFN_TF), lambda i, kk: (i // half, kk, i % half)),
                  pl.BlockSpec((tk, D_MODEL), lambda i, kk: (kk, 0))] + [ANY_SPEC] * len(deps),
        out_specs=pl.BlockSpec((FFN_TF, D_MODEL), lambda i, kk: (i, 0)),
        out_shape=jax.ShapeDtypeStruct((2 * D_FF, D_MODEL), BF16),
        scratch_shapes=[pltpu.VMEM((FFN_TF, D_MODEL), F32)],
        compiler_params=_cparams(dimension_semantics=("parallel", "arbitrary")),
    )(dgu, h, *deps)


def _lru_diag_blocks(dw_bd, name):
    def body(w_ref, o_ref):
        for half in range(2):
            for n in range(LRU_BLOCKS):
                rows = slice(n * LRU_BLOCK, (n + 1) * LRU_BLOCK)
                cols = slice(half * LRU_WIDTH + n * LRU_BLOCK, half * LRU_WIDTH + (n + 1) * LRU_BLOCK)
                o_ref[half, rows, :] = w_ref[rows, cols]

    return pl.pallas_call(
        body, name=name, out_shape=jax.ShapeDtypeStruct((2, LRU_WIDTH, LRU_BLOCK), F32), compiler_params=_cparams()
    )(dw_bd)


def _swiglu_fwd(gu, name):
    (a,) = _rowwise(_swiglu_fn, name, [], [gu], [(D_FF, BF16)], ts=128)
    return a


def _swiglu_bwd(gu, da, name, deps=()):
    def fn(gu, da):
        _, vjp = jax.vjp(_swiglu_fn, gu)
        return vjp(da)[0]

    (dgu,) = _rowwise(fn, name, [], [gu, da], [(2 * D_FF, BF16)], ts=128, deps=deps)
    return dgu


def _shift_down(ext, j, rows):
    return pltpu.roll(ext, j, 0)[V7X_SUBLANES : V7X_SUBLANES + rows]


def _shift_up(ext, j, rows):
    return pltpu.roll(ext, ext.shape[0] - j, 0)[:rows] if j else ext[:rows]


LRU_SLAB = 256
N_SLABS = LRU_WIDTH // LRU_SLAB


def _slab_weights(wa, wx):
    per = LRU_SLAB // LRU_BLOCK
    eye = jnp.eye(per, dtype=wa.dtype)

    def diag(w):
        w4 = w.reshape(N_SLABS, per, LRU_BLOCK, LRU_BLOCK)
        return jnp.einsum("sbkj,bc->sbkcj", w4, eye).reshape(N_SLABS, LRU_SLAB, LRU_SLAB)

    return jnp.concatenate([diag(wa), diag(wx)], axis=2).reshape(LRU_WIDTH, 2 * LRU_SLAB).astype(BF16)


def _slab_cols(v, s):
    lo = s * LRU_SLAB
    return jnp.concatenate([v[:, lo : lo + LRU_SLAB], v[:, LRU_WIDTH + lo : LRU_WIDTH + lo + LRU_SLAB]], axis=1)


def _lru_front(proj, w8, b, w_slab, ba, bx, lam, name):
    def fn(i, steps, w8, b, w_slab, ba, bx, lam, x, halo):
        halo = jnp.where(i > 0, halo, 0.0)
        ext = jnp.concatenate([halo, x], axis=0)
        xc = b + w8[3:4] * x
        for j in (1, 2, 3):
            xc = xc + w8[3 - j : 4 - j] * _shift_down(ext, j, x.shape[0])
        xcb = xc.astype(BF16)
        prods = []
        for s in range(N_SLABS):
            rows = slice(s * LRU_SLAB, (s + 1) * LRU_SLAB)
            prods.append(jnp.dot(xcb[:, rows], w_slab[rows], preferred_element_type=F32))
        pre = jnp.concatenate([p[:, :LRU_SLAB] for p in prods] + [p[:, LRU_SLAB:] for p in prods], axis=1)
        a, u = _gates_fn(ba, bx, lam, pre, xc)
        return xc, pre, a, u

    tiles = [(proj, LRU_WIDTH, 0), (proj, LRU_WIDTH, 0, "prev")]
    outs = [(LRU_WIDTH, F32), (2 * LRU_WIDTH, F32), (LRU_WIDTH, F32), (LRU_WIDTH, F32)]
    return _rowwise(fn, name, [w8, b, w_slab, ba, bx, lam], tiles, outs, with_index=True)


def _lru_back(pre, xc, w_slab, ba, bx, lam, g, h_prev, name, deps=()):
    def fn(w_slab, ba, bx, lam, pre, xc, g, h_prev):
        _, vjp = jax.vjp(_gates_fn, ba, bx, lam, pre, xc)
        dba, dbx, dlam, dpre, dxc = vjp((g * h_prev, g))
        dpre = dpre.astype(BF16)
        back = []
        for s in range(N_SLABS):
            rows = slice(s * LRU_SLAB, (s + 1) * LRU_SLAB)
            back.append(lax.dot_general(_slab_cols(dpre, s), w_slab[rows], _DIMS["nt"], preferred_element_type=F32))
        return dpre, dxc + jnp.concatenate(back, axis=1), _sum_rows(dba, dbx, dlam)

    return _rowwise(fn, name, [w_slab, ba, bx, lam], [pre, xc, g, h_prev],
                    [(2 * LRU_WIDTH, BF16), (LRU_WIDTH, F32)], [(V7X_SUBLANES, LRU_WIDTH)], deps=deps)


def _lru_dw(xc, dpre, name):
    s = xc.shape[0]
    ts = min(512, s)
    steps = s // ts
    per = LRU_SLAB // LRU_BLOCK

    def body(x_ref, d_ref, o_ref, acc):
        i = pl.program_id(0)
        xcb = x_ref[...].astype(BF16)
        d = d_ref[...]
        for sl in range(N_SLABS):
            rows = slice(sl * LRU_SLAB, (sl + 1) * LRU_SLAB)
            p = lax.dot_general(xcb[:, rows], _slab_cols(d, sl), _DIMS["tn"], preferred_element_type=F32)

            @pl.when(i == 0)
            def _(p=p, rows=rows):
                acc[rows, :] = p

            @pl.when(i > 0)
            def _(p=p, rows=rows):
                acc[rows, :] += p

        @pl.when(i == steps - 1)
        def _():
            for half in range(2):
                for n in range(LRU_BLOCKS):
                    r0 = n * LRU_BLOCK
                    c0 = half * LRU_SLAB + (n % per) * LRU_BLOCK
                    o_ref[half, r0 : r0 + LRU_BLOCK, :] = acc[r0 : r0 + LRU_BLOCK, c0 : c0 + LRU_BLOCK]

    return pl.pallas_call(
        body,
        name=name,
        grid=(steps,),
        in_specs=[pl.BlockSpec((ts, LRU_WIDTH), lambda i: (i, 0)), pl.BlockSpec((ts, 2 * LRU_WIDTH), lambda i: (i, 0))],
        out_specs=pl.BlockSpec((2, LRU_WIDTH, LRU_BLOCK), lambda i: (0, 0, 0)),
        out_shape=jax.ShapeDtypeStruct((2, LRU_WIDTH, LRU_BLOCK), F32),
        scratch_shapes=[pltpu.VMEM((LRU_WIDTH, 2 * LRU_SLAB), F32)],
        compiler_params=_cparams(dimension_semantics=("arbitrary",)),
    )(xc, dpre)


def _conv_bwd(proj, w8, d1, name):
    def fn(i, steps, w8, x, halo, d, d1n):
        rows = x.shape[0]
        dn = jnp.where(i < steps - 1, d1n, 0.0)
        halo = jnp.where(i > 0, halo, 0.0)
        dext = jnp.concatenate([d, dn], axis=0)
        xext = jnp.concatenate([halo, x], axis=0)
        dx = w8[3:4] * d
        dw = [None] * 4
        dw[3] = _rowsum(d * x)
        for k in (1, 2, 3):
            dx = dx + w8[3 - k : 4 - k] * _shift_up(dext, k, rows)
            dw[3 - k] = _rowsum(d * _shift_down(xext, k, rows))
        return dx, _sum_rows(*dw, _rowsum(d))

    tiles = [(proj, LRU_WIDTH, 0), (proj, LRU_WIDTH, 0, "prev"), d1, (d1, LRU_WIDTH, 0, "next")]
    return _rowwise(fn, name, [w8], tiles, [(LRU_WIDTH, BF16)], [(V7X_SUBLANES, LRU_WIDTH)], with_index=True)


SCAN_ROWS = 512


def _block_scan(a, b, row, reverse):
    for d in (1, 2, 4):
        if reverse:
            shift, keep = V7X_SUBLANES - d, row < V7X_SUBLANES - d
        else:
            shift, keep = d, row >= d
        a_s = pltpu.roll(a, shift, 0)
        b_s = pltpu.roll(b, shift, 0)
        b = jnp.where(keep, a * b_s + b, b)
        a = jnp.where(keep, a * a_s, a)
    return a, b


def _scan_fwd(a, u, proj, name):
    s, w = a.shape
    ts = min(SCAN_ROWS, s)
    sub = ts // V7X_SUBLANES

    def body(a_ref, u_ref, yr_ref, h_ref, hp_ref, rec_ref, carry):
        @pl.when(pl.program_id(0) == 0)
        def _():
            carry[...] = jnp.zeros_like(carry)

        row = lax.broadcasted_iota(jnp.int32, (V7X_SUBLANES, w), 0)

        def step(j, c):
            rows = pl.ds(pl.multiple_of(j * V7X_SUBLANES, V7X_SUBLANES), V7X_SUBLANES)
            pa, pb = _block_scan(a_ref[rows, :], u_ref[rows, :], row, False)
            h = pb + pa * c
            h_ref[rows, :] = h
            hp_ref[rows, :] = jnp.where(row >= 1, pltpu.roll(h, 1, 0), c)
            return jnp.broadcast_to(h[V7X_SUBLANES - 1 :], (V7X_SUBLANES, w))

        carry[...] = lax.fori_loop(0, sub, step, carry[...])
        rec_ref[...] = _recin_fn(h_ref[...], yr_ref[...]).astype(rec_ref.dtype)

    spec = pl.BlockSpec((ts, w), lambda i: (i, 0))
    return pl.pallas_call(
        body,
        name=name,
        grid=(s // ts,),
        in_specs=[spec, spec, pl.BlockSpec((ts, w), lambda i: (i, 1))],
        out_specs=[spec, spec, spec],
        out_shape=[jax.ShapeDtypeStruct((s, w), F32)] * 2 + [jax.ShapeDtypeStruct((s, w), BF16)],
        scratch_shapes=[pltpu.VMEM((V7X_SUBLANES, w), F32)],
        compiler_params=_cparams(dimension_semantics=("arbitrary",)),
    )(a, u, proj)


def _scan_bwd(a, dh, name):
    s, w = a.shape
    ts = min(SCAN_ROWS, s)
    sub = ts // V7X_SUBLANES
    steps = s // ts

    def body(a_ref, d_ref, g_ref, carry):
        @pl.when(pl.program_id(0) == 0)
        def _():
            carry[...] = jnp.zeros_like(carry)

        row = lax.broadcasted_iota(jnp.int32, (V7X_SUBLANES, w), 0)

        def step(jj, c):
            j = sub - 1 - jj
            rows = pl.ds(pl.multiple_of(j * V7X_SUBLANES, V7X_SUBLANES), V7X_SUBLANES)
            av, dv = a_ref[rows, :], d_ref[rows, :]
            pa, pb = _block_scan(av, av * dv, row, True)
            big = pb + pa * c
            g_ref[rows, :] = dv + jnp.where(row < V7X_SUBLANES - 1, pltpu.roll(big, V7X_SUBLANES - 1, 0), c)
            return jnp.broadcast_to(big[:1], (V7X_SUBLANES, w))

        carry[...] = lax.fori_loop(0, sub, step, carry[...])

    spec = pl.BlockSpec((ts, w), lambda i: (steps - 1 - i, 0))
    return pl.pallas_call(
        body,
        name=name,
        grid=(steps,),
        in_specs=[spec, spec],
        out_specs=spec,
        out_shape=jax.ShapeDtypeStruct((s, w), F32),
        scratch_shapes=[pltpu.VMEM((V7X_SUBLANES, w), F32)],
        compiler_params=_cparams(dimension_semantics=("arbitrary",)),
    )(a, dh)


def _rel_index():
    i = np.arange(ATT_TQ)[:, None]
    j = np.arange(3 * ATT_TQ)[None, :]
    band = (j // CHUNK >= i // CHUNK) & (j // CHUNK <= i // CHUNK + LEFT_CHUNKS)
    return band


SKEW = 4 * ATT_TQ


def _skew_onehot():
    t = np.arange(SKEW)
    diag = np.where(t < 3 * ATT_TQ, -t, SKEW - t)
    idx = np.clip(diag + LEFT_CHUNKS * CHUNK, -MAX_REL, MAX_REL) + MAX_REL
    hit = (idx[:, None] == np.arange(2 * MAX_REL + 1)[None, :]) & (t[:, None] != 3 * ATT_TQ)
    return hit.astype(np.float32)


def _bias_tile(rel_bias):
    per_t = jnp.dot(rel_bias, jnp.asarray(_skew_onehot()).T, precision=lax.Precision.HIGHEST)
    flat = jnp.broadcast_to(per_t[:, None, :], (ATT_HEADS, ATT_TQ, SKEW)).reshape(ATT_HEADS, ATT_TQ * SKEW)
    tile = flat[:, : ATT_TQ * (SKEW - 1)].reshape(ATT_HEADS, ATT_TQ, SKEW - 1)[:, :, : 3 * ATT_TQ]
    first = (2 - np.arange(3))[:, None, None, None] * ATT_TQ
    seen = _rel_index()[None, None] & (np.arange(3 * ATT_TQ)[None, None, None, :] >= first)
    return jnp.where(jnp.asarray(seen), tile[None], NEG)


def _bias_grad(dbias):
    flat = jnp.pad(dbias, ((0, 0), (0, 0), (0, SKEW - 1 - 3 * ATT_TQ))).reshape(ATT_HEADS, ATT_TQ * (SKEW - 1))
    per_t = jnp.sum(jnp.pad(flat, ((0, 0), (0, ATT_TQ))).reshape(ATT_HEADS, ATT_TQ, SKEW), axis=1)
    return jnp.dot(per_t, jnp.asarray(_skew_onehot()), precision=lax.Precision.HIGHEST)


ATT_STEP_HEADS = ATT_HEADS
ATT_STEP_COLS = ATT_STEP_HEADS * ATT_HEAD_DIM


def _attn_specs(nt):
    qb, kb, vb = OFF_Q // ATT_STEP_COLS, OFF_K // ATT_STEP_COLS, OFF_V // ATT_STEP_COLS
    blk = (ATT_TQ, ATT_STEP_COLS)

    def qmap(base):
        return lambda hp, m: (jnp.minimum(m, nt - 1), base + hp)

    def wmap(base, back):
        return lambda hp, m: (jnp.clip(m - back, 0, nt - 1), base + hp)

    specs = [pl.BlockSpec(blk, qmap(qb))]
    specs += [pl.BlockSpec(blk, wmap(kb, back)) for back in (2, 1, 0)]
    specs += [pl.BlockSpec(blk, wmap(vb, back)) for back in (2, 1, 0)]
    return specs


ATT_SCALE = ATT_HEAD_DIM**-0.5


def _attn_exp(qh, kh, bias):
    s = lax.dot_general(qh, kh, _DIMS["nt"], preferred_element_type=F32) + bias
    e = jnp.exp(s - jnp.max(s, axis=-1, keepdims=True))
    return e, jnp.sum(e, axis=-1, keepdims=True)


def _attn_window(k0, k1, k2, v0, v1, v2):
    k = jnp.concatenate([k0[...], k1[...], k2[...]], axis=0).astype(BF16)
    v = jnp.concatenate([v0[...], v1[...], v2[...]], axis=0).astype(BF16)
    return k, v


def _bias_spec():
    return pl.BlockSpec((1, ATT_STEP_HEADS, ATT_TQ, 3 * ATT_TQ), lambda hp, m: (jnp.minimum(m, 2), hp, 0, 0))


def _attn_fwd(proj, bias, name):
    s = proj.shape[0]
    nt = s // ATT_TQ

    def body(q_ref, k0, k1, k2, v0, v1, v2, b_ref, o_ref):
        k, v = _attn_window(k0, k1, k2, v0, v1, v2)
        q = (q_ref[...] * ATT_SCALE).astype(BF16)
        for hh in range(ATT_STEP_HEADS):
            cols = slice(hh * ATT_HEAD_DIM, (hh + 1) * ATT_HEAD_DIM)
            e, total = _attn_exp(q[:, cols], k[:, cols], b_ref[0, hh])
            o = jnp.dot(e.astype(BF16), v[:, cols], preferred_element_type=F32) / total
            o_ref[:, cols] = o.astype(o_ref.dtype)

    specs = _attn_specs(nt) + [_bias_spec()]
    return pl.pallas_call(
        body,
        name=name,
        grid=(ATT_HEADS // ATT_STEP_HEADS, nt),
        in_specs=specs,
        out_specs=pl.BlockSpec((ATT_TQ, ATT_STEP_COLS), lambda hp, m: (m, hp)),
        out_shape=jax.ShapeDtypeStruct((s, ATT_WIDTH), BF16),
        compiler_params=_cparams(dimension_semantics=("parallel", "arbitrary")),
    )(proj, proj, proj, proj, proj, proj, proj, bias)


def _attn_bwd(proj, bias, do, name):
    s = proj.shape[0]
    nt = s // ATT_TQ
    win = 3 * ATT_TQ

    def body(q_ref, k0, k1, k2, v0, v1, v2, do_ref, b_ref, dq_ref, dk_ref, dv_ref, db_ref, dk_acc, dv_acc):
        m = pl.program_id(1)

        @pl.when(m == 0)
        def _():
            dk_acc[...] = jnp.zeros_like(dk_acc)
            dv_acc[...] = jnp.zeros_like(dv_acc)
            db_ref[...] = jnp.zeros_like(db_ref)

        @pl.when(m < nt)
        def _():
            k, v = _attn_window(k0, k1, k2, v0, v1, v2)
            q = (q_ref[...] * ATT_SCALE).astype(BF16)
            dout = do_ref[...]
            for hh in range(ATT_STEP_HEADS):
                cols = slice(hh * ATT_HEAD_DIM, (hh + 1) * ATT_HEAD_DIM)
                qh, kh, vh, doh = q[:, cols], k[:, cols], v[:, cols], dout[:, cols]
                e, total = _attn_exp(qh, kh, b_ref[0, hh])
                p = e / total
                dvh = lax.dot_general(p.astype(BF16), doh, _DIMS["tn"], preferred_element_type=F32)
                dp = lax.dot_general(doh, vh, _DIMS["nt"], preferred_element_type=F32)
                ds = p * (dp - jnp.sum(dp * p, axis=-1, keepdims=True))
                db_ref[hh] += ds
                dsb = ds.astype(BF16)
                dqh = jnp.dot(dsb, kh, preferred_element_type=F32) * ATT_SCALE
                dkh = lax.dot_general(dsb, qh, _DIMS["tn"], preferred_element_type=F32)
                dq_ref[:, cols] = dqh.astype(dq_ref.dtype)
                dk_acc[:, cols] += dkh
                dv_acc[:, cols] += dvh

        dk_ref[...] = dk_acc[:ATT_TQ].astype(dk_ref.dtype)
        dv_ref[...] = dv_acc[:ATT_TQ].astype(dv_ref.dtype)
        for acc in (dk_acc, dv_acc):
            rest = acc[ATT_TQ:]
            acc[: win - ATT_TQ] = rest
            acc[win - ATT_TQ :] = jnp.zeros((ATT_TQ, ATT_STEP_COLS), F32)

    blk = (ATT_TQ, ATT_STEP_COLS)
    specs = _attn_specs(nt)
    specs.append(pl.BlockSpec(blk, lambda hp, m: (jnp.minimum(m, nt - 1), hp)))
    specs.append(_bias_spec())
    done = lambda hp, m: (jnp.maximum(m - 2, 0), hp)
    out_specs = [
        pl.BlockSpec(blk, lambda hp, m: (jnp.minimum(m, nt - 1), hp)),
        pl.BlockSpec(blk, done),
        pl.BlockSpec(blk, done),
        pl.BlockSpec((ATT_STEP_HEADS, ATT_TQ, win), lambda hp, m: (hp, 0, 0)),
    ]
    out_shape = [jax.ShapeDtypeStruct((s, ATT_WIDTH), BF16)] * 3
    out_shape.append(jax.ShapeDtypeStruct((ATT_HEADS, ATT_TQ, win), F32))
    return pl.pallas_call(
        body,
        name=name,
        grid=(ATT_HEADS // ATT_STEP_HEADS, nt + 2),
        in_specs=specs,
        out_specs=out_specs,
        out_shape=out_shape,
        scratch_shapes=[pltpu.VMEM((win, ATT_STEP_COLS), F32), pltpu.VMEM((win, ATT_STEP_COLS), F32)],
        compiler_params=_cparams(dimension_semantics=("arbitrary", "arbitrary")),
    )(proj, proj, proj, proj, proj, proj, proj, do, bias)


def _ada_fwd(c_all, w, name):
    def body(c_ref, w_ref, o_ref):
        act = _silu(c_ref[...]).astype(BF16)
        o_ref[...] = jnp.dot(act, w_ref[...].astype(BF16), preferred_element_type=F32)

    return pl.pallas_call(
        body, name=name, out_shape=jax.ShapeDtypeStruct((c_all.shape[0], w.shape[1]), F32), compiler_params=_cparams()
    )(c_all, w)


def _ada_bwd(c_all, dmod, name):
    def body(c_ref, d_ref, o_ref):
        act = _silu(c_ref[...])
        o_ref[...] = lax.dot_general(act, d_ref[...], _DIMS["tn"], preferred_element_type=F32,
                                     precision=lax.Precision.HIGHEST)

    return pl.pallas_call(
        body, name=name, out_shape=jax.ShapeDtypeStruct((c_all.shape[1], dmod.shape[1]), F32), compiler_params=_cparams()
    )(c_all, dmod)


def _adamw_parts(landed, sent, me, w, m, v, name, rows=256):
    r, c = w.shape
    tr = _pick(r, rows, 16)

    def body(me_ref, g_ref, own_ref, w_ref, m_ref, v_ref, go_ref, d_ref, mo_ref, vo_ref):
        mine = me_ref[0]
        grad = jnp.zeros((tr, c), F32)
        for d in range(N_DEV):
            grad = grad + jnp.where(mine == d, own_ref[0], g_ref[d]).astype(F32)
        _adamw_update(grad, w_ref, m_ref, v_ref, go_ref, d_ref, mo_ref, vo_ref)

    spec = pl.BlockSpec((tr, c), lambda i, me_ref: (i, 0))
    return pl.pallas_call(
        body,
        name=name,
        grid_spec=pltpu.PrefetchScalarGridSpec(
            num_scalar_prefetch=1,
            grid=(r // tr,),
            in_specs=[pl.BlockSpec((N_DEV, tr, c), lambda i, me_ref: (0, i, 0)),
                      pl.BlockSpec((1, tr, c), lambda i, me_ref: (me_ref[0], i, 0)), spec, spec, spec],
            out_specs=[spec] * 4,
        ),
        out_shape=[jax.ShapeDtypeStruct((r, c), F32)] * 4,
        compiler_params=_cparams(dimension_semantics=("parallel",)),
    )(me.reshape(1).astype(jnp.int32), landed, sent, w, m, v)


def _adamw_update(grad, w_ref, m_ref, v_ref, go_ref, d_ref, mo_ref, vo_ref):
    m2 = ADAM_B1 * m_ref[...] + (1.0 - ADAM_B1) * grad
    v2 = ADAM_B2 * v_ref[...] + (1.0 - ADAM_B2) * (grad * grad)
    m_hat = m2 / (1.0 - ADAM_B1**ADAM_STEP)
    v_hat = v2 / (1.0 - ADAM_B2**ADAM_STEP)
    go_ref[...] = grad
    d_ref[...] = -ADAM_LR * (m_hat / (jnp.sqrt(v_hat) + ADAM_EPS) + ADAM_WD * w_ref[...])
    mo_ref[...] = m2
    vo_ref[...] = v2


def _adamw(g, w, m, v, name, rows=256):
    r, c = w.shape
    tr = _pick(r, rows, 16)

    def body(g_ref, w_ref, m_ref, v_ref, go_ref, d_ref, mo_ref, vo_ref):
        _adamw_update(g_ref[...], w_ref, m_ref, v_ref, go_ref, d_ref, mo_ref, vo_ref)

    spec = pl.BlockSpec((tr, c), lambda i: (i, 0))
    return pl.pallas_call(
        body,
        name=name,
        grid=(r // tr,),
        in_specs=[spec, spec, spec, spec],
        out_specs=[spec] * 4,
        out_shape=[jax.ShapeDtypeStruct((r, c), F32)] * 4,
        compiler_params=_cparams(dimension_semantics=("parallel",)),
    )(g, w, m, v)


def _sum_parts(parts, name):
    def body(p_ref, o_ref):
        acc = p_ref[0]
        for d in range(1, N_DEV):
            acc = acc + p_ref[d]
        o_ref[...] = acc

    return pl.pallas_call(
        body, name=name, out_shape=jax.ShapeDtypeStruct(parts.shape[1:], F32), compiler_params=_cparams()
    )(parts)


def _place():
    x, y, c = lax.axis_index("x"), lax.axis_index("y"), lax.axis_index("c")
    return x, y, c


def _dev_index(p):
    return 4 * p[0] + 2 * p[1] + p[2]


def _allgather_vmem(shard, name):
    m_per, n = shard.shape

    def body(x_ref, out_ref, send_sems, recv_sems, local_sem):
        x, y, c = _place()
        me, sibling = (x, y, c), (x, y, 1 - c)
        chips = [(1 - x, y), (x, 1 - y), (1 - x, 1 - y)]

        def rows(p):
            return out_ref.at[pl.ds(_dev_index(p) * m_per, m_per), :]

        def copy(k, block, to, src=None):
            return pltpu.make_async_remote_copy(
                src_ref=rows(block) if src is None else src, dst_ref=rows(block),
                send_sem=send_sems.at[k], recv_sem=recv_sems.at[k], device_id=to, device_id_type=MESH)

        mine = pltpu.make_async_copy(x_ref, rows(me), local_sem)
        mine.start()
        first = [copy(0, me, sibling, src=x_ref)]
        first += [copy(1 + j, me, (*chip, c), src=x_ref) for j, chip in enumerate(chips)]
        for cp in first:
            cp.start()
        passed = [copy(4 + j, (*chip, c), sibling) for j, chip in enumerate(chips)]
        for j, chip in enumerate(chips):
            copy(1 + j, (*chip, c), me).wait_recv()
            passed[j].start()
        copy(0, sibling, me).wait_recv()
        for j, chip in enumerate(chips):
            copy(4 + j, (*chip, 1 - c), me).wait_recv()
        for cp in first + passed:
            cp.wait_send()
        mine.wait()

    return pl.pallas_call(
        body,
        name=name,
        out_shape=jax.ShapeDtypeStruct((N_DEV * m_per, n), shard.dtype),
        in_specs=[pl.BlockSpec(memory_space=pltpu.VMEM)],
        out_specs=pl.BlockSpec(memory_space=pltpu.VMEM),
        scratch_shapes=[pltpu.SemaphoreType.DMA((7,)), pltpu.SemaphoreType.DMA((7,)), pltpu.SemaphoreType.DMA],
        compiler_params=_cparams(),
    )(shard)


def _allgather_hbm(shards, name):
    n = len(shards)

    def body(*refs):
        ins, outs = refs[:n], refs[n : 2 * n]
        send_sems, recv_sems, local_sems = refs[2 * n :]
        x, y, c = _place()
        me, sibling = (x, y, c), (x, y, 1 - c)
        chips = [(1 - x, y), (x, 1 - y), (1 - x, 1 - y)]

        def copy(a, k, block, to, src=None):
            dst = outs[a].at[_dev_index(block)]
            return pltpu.make_async_remote_copy(
                src_ref=dst if src is None else src, dst_ref=dst,
                send_sem=send_sems.at[a * 7 + k], recv_sem=recv_sems.at[a * 7 + k], device_id=to, device_id_type=MESH)

        mine = [pltpu.make_async_copy(ins[a], outs[a].at[_dev_index(me)], local_sems.at[a]) for a in range(n)]
        for cp in mine:
            cp.start()
        first = []
        for a in range(n):
            first.append(copy(a, 0, me, sibling, src=ins[a]))
            first += [copy(a, 1 + j, me, (*chip, c), src=ins[a]) for j, chip in enumerate(chips)]
        for cp in first:
            cp.start()
        passed = []
        for j, chip in enumerate(chips):
            for a in range(n):
                copy(a, 1 + j, (*chip, c), me).wait_recv()
                cp = copy(a, 4 + j, (*chip, c), sibling)
                cp.start()
                passed.append(cp)
        for a in range(n):
            copy(a, 0, sibling, me).wait_recv()
        for j, chip in enumerate(chips):
            for a in range(n):
                copy(a, 4 + j, (*chip, 1 - c), me).wait_recv()
        for cp in first + passed:
            cp.wait_send()
        for cp in mine:
            cp.wait()

    any_spec = pl.BlockSpec(memory_space=pl.ANY)
    return pl.pallas_call(
        body,
        name=name,
        out_shape=[jax.ShapeDtypeStruct((N_DEV, *s.shape), s.dtype) for s in shards],
        in_specs=[any_spec] * n,
        out_specs=[any_spec] * n,
        scratch_shapes=[pltpu.SemaphoreType.DMA((7 * n,)), pltpu.SemaphoreType.DMA((7 * n,)),
                        pltpu.SemaphoreType.DMA((n,))],
        compiler_params=_cparams(),
    )(*shards)


def _exchange_hbm(bufs, name):
    n = len(bufs)

    def body(*refs):
        ins, outs = refs[:n], refs[n : 2 * n]
        send_sems, recv_sems, local_sems = refs[2 * n :]
        x, y, c = _place()
        me = _dev_index((x, y, c))
        mine = [pltpu.make_async_copy(ins[a].at[me], outs[a].at[me], local_sems.at[a]) for a in range(n)]
        for cp in mine:
            cp.start()
        def peer_of(k):
            return (1 - x if k & 4 else x, 1 - y if k & 2 else y, 1 - c if k & 1 else c)

        copies = []
        for k in range(1, N_DEV):
            peer = peer_of(k)
            for a in range(n):
                copies.append(pltpu.make_async_remote_copy(
                    src_ref=ins[a].at[_dev_index(peer)], dst_ref=outs[a].at[me],
                    send_sem=send_sems.at[a * 7 + k - 1], recv_sem=recv_sems.at[a * 7 + k - 1],
                    device_id=peer, device_id_type=MESH))
        for cp in copies:
            cp.start()
        for k in range(1, N_DEV):
            peer = peer_of(k)
            for a in range(n):
                pltpu.make_async_remote_copy(
                    src_ref=ins[a].at[me], dst_ref=outs[a].at[_dev_index(peer)],
                    send_sem=send_sems.at[a * 7 + k - 1], recv_sem=recv_sems.at[a * 7 + k - 1],
                    device_id=peer, device_id_type=MESH).wait_recv()
        for cp in copies:
            cp.wait_send()
        for cp in mine:
            cp.wait()

    any_spec = pl.BlockSpec(memory_space=pl.ANY)
    return pl.pallas_call(
        body,
        name=name,
        out_shape=[jax.ShapeDtypeStruct(b.shape, b.dtype) for b in bufs],
        in_specs=[any_spec] * n,
        out_specs=[any_spec] * n,
        scratch_shapes=[pltpu.SemaphoreType.DMA((7 * n,)), pltpu.SemaphoreType.DMA((7 * n,)),
                        pltpu.SemaphoreType.DMA((n,))],
        compiler_params=_cparams(),
    )(*bufs)


HBM_SPEC = pl.BlockSpec(memory_space=pltpu.HBM)
SEM_SPEC = pl.BlockSpec(memory_space=pltpu.SEMAPHORE)
EFFECT = pltpu.SideEffectType.DATAFLOW_SIDE_EFFECTING


def _peers(x, y, c):
    return [(1 - x if k & 4 else x, 1 - y if k & 2 else y, 1 - c if k & 1 else c) for k in range(1, N_DEV)]


def _push_peers(mode, x, y, c):
    if mode == "all":
        return _peers(x, y, c)
    return [(x, y, 1 - c), (1 - x, y, c), (x, 1 - y, c), (1 - x, 1 - y, c)]


def _push_start(groups, sliced, name, after=(), modes=None):
    flat = [b for g in groups for b in g]
    n, ng = len(flat), len(groups)
    sizes = [len(g) for g in groups]
    modes = modes or ["all"] * ng
    fan = [len(_push_peers(m, 0, 0, 0)) for m in modes]
    lands = [lax.empty(b.shape if sliced else (N_DEV, *b.shape), b.dtype) for b in flat]

    def body(*refs):
        ins, lnd = refs[:n], refs[n : 2 * n]
        sems = refs[2 * n + len(after) : 2 * n + len(after) + 2 * ng]
        token = refs[-1]
        x, y, c = _place()
        me = _dev_index((x, y, c))
        first = 0
        for gi, size in enumerate(sizes):
            for k, peer in enumerate(_push_peers(modes[gi], x, y, c)):
                for j in range(first, first + size):
                    sem = (j - first) * fan[gi] + k
                    pltpu.make_async_remote_copy(
                        src_ref=ins[j].at[_dev_index(peer)] if sliced else ins[j], dst_ref=lnd[j].at[me],
                        send_sem=sems[2 * gi].at[sem], recv_sem=sems[2 * gi + 1].at[sem],
                        device_id=peer, device_id_type=MESH).start()
            first += size
        token[...] = jnp.zeros_like(token)

    out_shape = []
    for size, width in zip(sizes, fan):
        out_shape += [pltpu.SemaphoreType.DMA((width * size,)), pltpu.SemaphoreType.DMA((width * size,))]
    out_shape += [pltpu.HBM(b.shape, b.dtype) for b in flat + lands]
    out_shape.append(jax.ShapeDtypeStruct((V7X_SUBLANES, V7X_LANES), F32))
    res = pl.pallas_call(
        body,
        name=name,
        out_shape=tuple(out_shape),
        in_specs=[HBM_SPEC] * (2 * n) + [ANY_SPEC] * len(after),
        out_specs=tuple([SEM_SPEC] * (2 * ng) + [HBM_SPEC] * (2 * n) + [pl.BlockSpec(memory_space=pltpu.VMEM)]),
        input_output_aliases={i: 2 * ng + i for i in range(2 * n)},
        compiler_params=pltpu.CompilerParams(has_side_effects=EFFECT),
    )(*[pltpu.with_memory_space_constraint(b, pltpu.HBM) for b in flat + lands], *after)
    sems, thru, token = res[: 2 * ng], res[2 * ng : 2 * ng + 2 * n], res[-1]
    out, first = [], 0
    for gi, size in enumerate(sizes):
        out.append((sems[2 * gi], sems[2 * gi + 1], list(thru[first : first + size]),
                    list(thru[n + first : n + first + size])))
        first += size
    return out, token


def _push_wait(started, sliced, after, name, mode="all"):
    send_sems, recv_sems, bufs, lands = started
    n = len(bufs)
    fan = len(_push_peers(mode, 0, 0, 0))

    def body(*refs):
        ins, lnd = refs[:n], refs[n : 2 * n]
        send_ref, recv_ref = refs[2 * n], refs[2 * n + 1]
        x, y, c = _place()
        for k, peer in enumerate(_push_peers(mode, x, y, c)):
            for j in range(n):
                cp = pltpu.make_async_remote_copy(
                    src_ref=ins[j].at[_dev_index(peer)] if sliced else ins[j], dst_ref=lnd[j].at[_dev_index(peer)],
                    send_sem=send_ref.at[j * fan + k], recv_sem=recv_ref.at[j * fan + k],
                    device_id=peer, device_id_type=MESH)
                cp.wait_send()
                cp.wait_recv()

    res = pl.pallas_call(
        body,
        name=name,
        out_shape=tuple(pltpu.HBM(b.shape, b.dtype) for b in bufs + lands),
        in_specs=[HBM_SPEC] * (2 * n) + [SEM_SPEC, SEM_SPEC, pl.BlockSpec(memory_space=pl.ANY)],
        out_specs=tuple([HBM_SPEC] * (2 * n)),
        input_output_aliases={i: i for i in range(2 * n)},
        compiler_params=pltpu.CompilerParams(has_side_effects=EFFECT),
    )(*bufs, *lands, send_sems, recv_sems, after)
    return list(res[:n]), list(res[n:])


def _forward_copies(lnd, send_ref, recv_ref, incoming):
    x, y, c = _place()
    copies = []
    for k, chip in enumerate([(1 - x, y), (x, 1 - y), (1 - x, 1 - y)]):
        mine, theirs = _dev_index((*chip, c)), _dev_index((*chip, 1 - c))
        for j, ref in enumerate(lnd):
            copies.append(pltpu.make_async_remote_copy(
                src_ref=ref.at[mine], dst_ref=ref.at[theirs if incoming else mine],
                send_sem=send_ref.at[j * 3 + k], recv_sem=recv_ref.at[j * 3 + k],
                device_id=(x, y, 1 - c), device_id_type=MESH))
    return copies


def _forward_start(lands, name):
    n = len(lands)

    def body(*refs):
        for cp in _forward_copies(refs[:n], refs[n], refs[n + 1], False):
            cp.start()

    res = pl.pallas_call(
        body,
        name=name,
        out_shape=(pltpu.SemaphoreType.DMA((3 * n,)), pltpu.SemaphoreType.DMA((3 * n,)),
                   *[pltpu.HBM(b.shape, b.dtype) for b in lands]),
        in_specs=[HBM_SPEC] * n,
        out_specs=(SEM_SPEC, SEM_SPEC, *[HBM_SPEC] * n),
        input_output_aliases={i: 2 + i for i in range(n)},
        compiler_params=pltpu.CompilerParams(has_side_effects=EFFECT),
    )(*[pltpu.with_memory_space_constraint(b, pltpu.HBM) for b in lands])
    return res[0], res[1], list(res[2:])


def _forward_wait(started, after, name):
    send_sems, recv_sems, lands = started
    n = len(lands)

    def body(*refs):
        for cp in _forward_copies(refs[:n], refs[n], refs[n + 1], True):
            cp.wait_send()
            cp.wait_recv()

    res = pl.pallas_call(
        body,
        name=name,
        out_shape=tuple(pltpu.HBM(b.shape, b.dtype) for b in lands),
        in_specs=[HBM_SPEC] * n + [SEM_SPEC, SEM_SPEC, pl.BlockSpec(memory_space=pl.ANY)],
        out_specs=tuple([HBM_SPEC] * n),
        input_output_aliases={i: i for i in range(n)},
        compiler_params=pltpu.CompilerParams(has_side_effects=EFFECT),
    )(*lands, send_sems, recv_sems, after)
    return list(res)


def _cols_full(g):
    return jnp.transpose(g, (1, 0, 2)).reshape(g.shape[1], -1)


def _rows_full(g):
    return g.reshape(-1, g.shape[2])


def _cols_parts(full, n=N_DEV):
    r = full.shape[0]
    return jnp.transpose(full.reshape(r, n, -1), (1, 0, 2)).astype(BF16)


def _rows_parts(full):
    return full.reshape(N_DEV, -1, full.shape[1]).astype(BF16)


def _block_diag(w):
    eye = jnp.eye(LRU_BLOCKS, dtype=w.dtype)
    return jnp.einsum("nkj,nm->nkmj", w, eye).reshape(LRU_WIDTH, LRU_WIDTH)


def _pad_rows(v, rows):
    flat = v.reshape(-1)
    return jnp.pad(flat, (0, rows * D_MODEL - flat.shape[0])).reshape(rows, D_MODEL)


def _my_cols(full, me, width):
    return lax.dynamic_slice_in_dim(full, me * width, width, axis=full.ndim - 1)


def kernel(x, c, w_ada, b_ada, norm_pre, norm_post, ffn1_w_gu, ffn1_w_down, w_in, rel_bias, conv_w, conv_b, lru_wa, lru_ba, lru_wx, lru_bx, lru_lambda, w_att_o, w_rec_o, w_out, ffn2_w_gu, ffn2_w_down, loss_target, m_w_ada, m_b_ada, m_norm_pre, m_norm_post, m_ffn1_w_gu, m_ffn1_w_down, m_w_in, m_rel_bias, m_conv_w, m_conv_b, m_lru_wa, m_lru_ba, m_lru_wx, m_lru_bx, m_lru_lambda, m_w_att_o, m_w_rec_o, m_w_out, m_ffn2_w_gu, m_ffn2_w_down, v_w_ada, v_b_ada, v_norm_pre, v_norm_post, v_ffn1_w_gu, v_ffn1_w_down, v_w_in, v_rel_bias, v_conv_w, v_conv_b, v_lru_wa, v_lru_ba, v_lru_wx, v_lru_bx, v_lru_lambda, v_w_att_o, v_w_rec_o, v_w_out, v_ffn2_w_gu, v_ffn2_w_down):
    weights = dict(w_ada=w_ada, b_ada=b_ada, norm_pre=norm_pre, norm_post=norm_post, ffn1_w_gu=ffn1_w_gu,
                   ffn1_w_down=ffn1_w_down, w_in=w_in, rel_bias=rel_bias, conv_w=conv_w, conv_b=conv_b,
                   lru_wa=lru_wa, lru_ba=lru_ba, lru_wx=lru_wx, lru_bx=lru_bx, lru_lambda=lru_lambda,
                   w_att_o=w_att_o, w_rec_o=w_rec_o, w_out=w_out, ffn2_w_gu=ffn2_w_gu, ffn2_w_down=ffn2_w_down)
    mom1 = dict(w_ada=m_w_ada, b_ada=m_b_ada, norm_pre=m_norm_pre, norm_post=m_norm_post, ffn1_w_gu=m_ffn1_w_gu,
                ffn1_w_down=m_ffn1_w_down, w_in=m_w_in, rel_bias=m_rel_bias, conv_w=m_conv_w, conv_b=m_conv_b,
                lru_wa=m_lru_wa, lru_ba=m_lru_ba, lru_wx=m_lru_wx, lru_bx=m_lru_bx, lru_lambda=m_lru_lambda,
                w_att_o=m_w_att_o, w_rec_o=m_w_rec_o, w_out=m_w_out, ffn2_w_gu=m_ffn2_w_gu, ffn2_w_down=m_ffn2_w_down)
    mom2 = dict(w_ada=v_w_ada, b_ada=v_b_ada, norm_pre=v_norm_pre, norm_post=v_norm_post, ffn1_w_gu=v_ffn1_w_gu,
                ffn1_w_down=v_ffn1_w_down, w_in=v_w_in, rel_bias=v_rel_bias, conv_w=v_conv_w, conv_b=v_conv_b,
                lru_wa=v_lru_wa, lru_ba=v_lru_ba, lru_wx=v_lru_wx, lru_bx=v_lru_bx, lru_lambda=v_lru_lambda,
                w_att_o=v_w_att_o, w_rec_o=v_w_rec_o, w_out=v_w_out, ffn2_w_gu=v_ffn2_w_gu, ffn2_w_down=v_ffn2_w_down)
    order = list(weights)
    big = ["ffn1_w_gu", "ffn1_w_down", "w_in", "w_att_o", "w_rec_o", "w_out", "ffn2_w_gu", "ffn2_w_down"]
    col_sharded = {"ffn1_w_gu", "w_in", "w_att_o", "ffn2_w_gu"}
    small = ["b_ada", "norm_pre", "norm_post", "rel_bias", "conv_w", "conv_b", "lru_wa", "lru_ba", "lru_wx",
             "lru_bx", "lru_lambda"]

    xi, yi, ci = _place()
    me = _dev_index((xi, yi, ci))
    x0 = x[0]
    target = loss_target[0]
    fuse_tm = min(FUSE_TM, x0.shape[0])

    transposed = {"ffn1_w_gu", "w_in", "ffn2_w_gu"}
    local = lambda n, arr: jnp.transpose(arr[0]) if n in transposed else arr[0]
    shards = {n: local(n, weights[n]).astype(BF16) for n in big}
    full_of = lambda n, g: _cols_full(g) if n == "w_att_o" else _rows_full(g)

    pack = jnp.concatenate([c.reshape(-1), norm_pre.reshape(-1), norm_post.reshape(-1), conv_w.reshape(-1)])
    pack = jnp.pad(pack, (0, 3072 - pack.shape[0])).reshape(8, 384)
    got = _allgather_vmem(pack, "gather_small_inputs").reshape(N_DEV, 3072)
    c_all = got[:, :1024]
    unshard = lambda blk, rows: jnp.transpose(blk.reshape(N_DEV, rows, 128), (1, 0, 2)).reshape(rows, D_MODEL)
    g_pre = unshard(got[:, 1024:1408], 3)
    g_post = unshard(got[:, 1408:1792], 3)
    conv_taps = unshard(got[:, 1792:2304], 4)
    conv_w8 = jnp.concatenate([conv_taps, jnp.zeros((4, LRU_WIDTH), F32)], axis=0)

    mod_cols = _ada_fwd(c_all, w_ada[0], "ada_fwd")
    mod_all = _allgather_vmem(mod_cols, "gather_mod").reshape(N_DEV, N_DEV, 1152)
    mod = lax.dynamic_index_in_dim(mod_all, me, axis=1, keepdims=False).reshape(1, -1) + b_ada
    mod = mod.reshape(3, 3, 1, D_MODEL)

    w_slab = _slab_weights(lru_wa[0], lru_wx[0])
    bias = _bias_tile(rel_bias[0])

    res_w = (0.5, 1.0, 0.5)
    row = lambda v: v.reshape(1, -1)

    (w1_gu,) = _allgather_hbm([shards["ffn1_w_gu"]], "gather_ffn1_w_gu")
    weight_groups = [["ffn1_w_down"], ["w_in"], ["w_att_o", "w_rec_o", "w_out"], ["ffn2_w_gu", "ffn2_w_down"]]
    weight_modes = ["all", "chip", "all", "all"]
    weights_started, started = _push_start([[shards[n] for n in g] for g in weight_groups], False,
                                           "gather_weights_start", after=(mod, w1_gu), modes=weight_modes)
    full = {"ffn1_w_gu": _rows_full(w1_gu)}

    def gathered_group(gi, after):
        sent, lands = _push_wait(weights_started[gi], False, after, f"gather_weights_wait{gi}", mode=weight_modes[gi])
        if weight_modes[gi] == "chip":
            lands = _forward_wait(_forward_start(lands, f"gather_weights_forward{gi}"), sent[0],
                                  f"gather_weights_forward_wait{gi}")
        for n, own, land in zip(weight_groups[gi], sent, lands):
            full[n] = full_of(n, jnp.where(is_me, own[None], land))

    is_me = (jnp.arange(N_DEV) == me)[:, None, None]

    def ffn_fwd(xin, k, gi, tag, deps=(), target=None):
        h, a, g, u = _pre_up(xin, row(g_pre[k]), mod[k, 0], mod[k, 1], full[f"{tag}_w_gu"], f"{tag}_up", deps=deps)
        if f"{tag}_w_down" not in full:
            gathered_group(gi, a)
        f, *out = _matmul_post(a, full[f"{tag}_w_down"], xin, row(g_post[k]), mod[k, 2], res_w[k], f"{tag}_down",
                               target=target)
        return (out[0] if target is None else out), (h, g, u, a, f)

    x1, saved1 = ffn_fwd(x0, 0, 0, "ffn1", deps=(started,))

    gathered_group(1, x1)
    h2, proj = _pre_matmul(x1, row(g_pre[1]), mod[1, 0], mod[1, 1], full["w_in"], "mix_in",
                           b_shift=3 * ATT_WIDTH // 512)
    att_o = _attn_fwd(proj, bias, "attn_fwd")
    gathered_group(2, att_o)
    xc, pre, a_t, u_t = _lru_front(proj, conv_w8, conv_b, w_slab, lru_ba, lru_bx, lru_lambda, "lru_front")
    hs, h_prev, rec_in = _scan_fwd(a_t, u_t, proj, "lru_scan")
    att = _matmul(att_o, full["w_att_o"], "nn", F32, "att_out")
    rec = _matmul(rec_in, full["w_rec_o"], "nn", F32, "rec_out")
    merged, f2, x2 = _merge_matmul_post(att, rec, proj, full["w_out"], x1, row(g_post[1]), mod[1, 2], res_w[1],
                                        "mix_out")

    gathered_group(3, x2)
    (dy, sq), saved3 = ffn_fwd(x2, 2, 2, "ffn2", target=target)
    loss = lax.psum(0.5 * jnp.sum(sq) / D_MODEL, ("x", "y", "c"))

    grads = {}
    norm_sums = [None] * 6

    pending = []

    def exchange_start(names, tag, after=()):
        send = [(_cols_parts if n == "w_att_o" else _rows_parts)(grads[n]) for n in names]
        (group,), token = _push_start([send], True, f"exchange_{tag}_start", after=after)
        pending.append((names, send, group, tag))
        return token

    def exchange_finish(names, send, group, tag, after):
        sent, lands = _push_wait(group, True, after, f"exchange_{tag}_wait")
        res = None
        for n, land, mine in zip(names, lands, sent):
            res = _adamw_parts(land, mine, me, local(n, weights[n]), local(n, mom1[n]), local(n, mom2[n]),
                               f"adamw_{n}")
            back = (lambda r: jnp.transpose(r)) if n in transposed else (lambda r: r)
            out_g[n], out_d[n], out_m[n], out_v[n] = [back(r).reshape(weights[n].shape) for r in res]
        return res[0]

    out_g, out_d, out_m, out_v = {}, {}, {}, {}

    def ffn_bwd(xin, k, saved, dout, tag):
        h, g, u, a, f = saved
        w_gu, w_down = f"{tag}_w_gu", f"{tag}_w_down"
        df, dgu, norm_sums[2 * k + 1] = _post_bwd_up_bwd(f, dout, row(g_post[k]), mod[k, 2], res_w[k], full[w_down],
                                                          g, u, f"{tag}_up_bwd")
        grads[w_down] = _matmul(a, df, "tn", BF16, f"{tag}_dw_down", tm=1408, tn=1024, tk=DW_TK)
        started = exchange_start([w_down], w_down)
        grads[w_gu] = _dw_gu(dgu, h, f"{tag}_dw_gu", deps=(started,))
        started = exchange_start([w_gu], w_gu)
        halves = [(dgu, (None, fuse_tm, D_FF), lambda i, half=half: (half, i, 0), (half * D_FF, (half + 1) * D_FF))
                  for half in range(2)]
        dx, norm_sums[2 * k] = _matmul_pre_bwd(halves, full[w_gu], xin, dout, row(g_pre[k]),
                                                                mod[k, 0], mod[k, 1], f"{tag}_dh", deps=(started,))
        return dx

    dx2 = ffn_bwd(x2, 2, saved3, dy, "ffn2")

    df2, datt, drec, dg_att, dg_rec, norm_sums[3] = _post_bwd_merge_bwd(
        f2, dx2, row(g_post[1]), mod[1, 2], res_w[1], full["w_out"], att, rec, proj, "mix_dmerged")
    grads["w_out"] = _matmul(merged, df2, "tn", BF16, "mix_dw_out", tm=1024, tn=1024, tk=DW_TK)
    datt_o = _matmul(datt, full["w_att_o"], "nt", BF16, "att_out_bwd")
    grads["w_att_o"] = _matmul(att_o, datt, "tn", BF16, "dw_att_o", tm=512, tn=1024, tk=DW_TK)
    grads["w_rec_o"] = _matmul(rec_in, drec, "tn", BF16, "dw_rec_o", tm=1024, tn=1024, tk=DW_TK)
    started = exchange_start(["w_out", "w_att_o", "w_rec_o"], "mix_out")
    dhs, dyr = _matmul_recin_bwd(drec, full["w_rec_o"], hs, proj, "rec_out_bwd", deps=(started,))
    g_t = _scan_bwd(a_t, dhs, "lru_scan_bwd")
    dpre, dxc, lru_sums = _lru_back(pre, xc, w_slab, lru_ba, lru_bx, lru_lambda, g_t, h_prev, "lru_back")
    dxr, conv_sums = _conv_bwd(proj, conv_w8, dxc, "conv_bwd")
    dq, dk, dv, dbias = _attn_bwd(proj, bias, datt_o, "attn_bwd")
    dproj = jnp.concatenate([dq, dk, dv, dxr, dyr, dg_att, dg_rec], axis=1)
    grads["w_in"] = _matmul(dproj, h2, "tn", BF16, "mix_dw_in", tm=1408, tn=1024, tk=DW_TK)
    pack_mix = jnp.concatenate([conv_sums, lru_sums, _pad_rows(_bias_grad(dbias), V7X_SUBLANES),
                                _lru_dw(xc, dpre, "lru_dw").reshape(128, D_MODEL)], axis=0)
    (mix_started,), started = _push_start([[pack_mix]], False, "small_grads_mix_start")
    started = exchange_start(["w_in"], "w_in", after=(started,))
    whole = [(dproj, (fuse_tm, PROJ_WIDTH), lambda i: (i, 0), (0, PROJ_WIDTH))]
    dx1, norm_sums[2] = _matmul_pre_bwd(whole, full["w_in"], x1, dx2, row(g_pre[1]), mod[1, 0],
                                                             mod[1, 1], "mix_dh", deps=(started,))

    dx0 = ffn_bwd(x0, 0, saved1, dx1, "ffn1")

    pack_norm = jnp.concatenate(norm_sums, axis=0)
    (norm_started,), _ = _push_start([[pack_norm]], False, "small_grads_norm_start")

    def summed(started, pack, after, tag):
        (own,), (land,) = _push_wait(started, False, after, f"small_grads_{tag}_wait")
        parts = jnp.where(is_me, own[None], land)
        return parts, _sum_parts(parts, f"small_grads_{tag}_sum")

    done = dx0
    last = [p for p in pending if p[3].startswith("ffn1")]
    for names, send, group, tag in pending:
        if not tag.startswith("ffn1"):
            done = exchange_finish(names, send, group, tag, done)

    _, total = summed(mix_started, pack_mix, done, "mix")
    grads["conv_w"] = _my_cols(total[0:4], me, 128)
    grads["conv_b"] = total[4:5]
    grads["lru_ba"] = total[8:9]
    grads["lru_bx"] = total[9:10]
    grads["lru_lambda"] = total[10:11]
    grads["rel_bias"] = total[16:19].reshape(-1)[: ATT_HEADS * (2 * MAX_REL + 1)].reshape(ATT_HEADS, -1)
    grads["lru_wa"] = total[24:88].reshape(LRU_BLOCKS, LRU_BLOCK, LRU_BLOCK)
    grads["lru_wx"] = total[88:152].reshape(LRU_BLOCKS, LRU_BLOCK, LRU_BLOCK)
    parts, total = summed(norm_started, pack_norm, total, "norm")
    by_sandwich = lambda v: v.reshape(*v.shape[:-2], 3, 2 * V7X_SUBLANES, D_MODEL)
    dmod_of = lambda v: jnp.concatenate([by_sandwich(v)[..., 1:3, :], by_sandwich(v)[..., 9:10, :]], axis=-2)
    grads["b_ada"] = dmod_of(total).reshape(1, -1)
    grads["norm_pre"] = _my_cols(by_sandwich(total)[:, 0, :], me, 128)
    grads["norm_post"] = _my_cols(by_sandwich(total)[:, V7X_SUBLANES, :], me, 128)
    dmod_all = dmod_of(parts).reshape(N_DEV, 9 * D_MODEL)
    grads["w_ada"] = _ada_bwd(c_all, _my_cols(dmod_all, me, 1152), "ada_bwd")

    res = _adamw(grads["w_ada"], w_ada[0], m_w_ada[0], v_w_ada[0], "adamw_w_ada")
    out_g["w_ada"], out_d["w_ada"], out_m["w_ada"], out_v["w_ada"] = [r.reshape(w_ada.shape) for r in res]

    sizes = [int(np.prod(weights[n].shape)) for n in small]
    tot = sum(sizes)
    rows_small = -(-tot // (16 * D_MODEL)) * 16
    flat = lambda arrs: jnp.pad(jnp.concatenate([a.reshape(-1) for a in arrs]),
                                (0, rows_small * D_MODEL - tot)).reshape(rows_small, D_MODEL)
    res = _adamw(flat([grads[n] for n in small]), flat([weights[n] for n in small]),
                 flat([mom1[n] for n in small]), flat([mom2[n] for n in small]), "adamw_small", rows=rows_small)
    offs = np.cumsum([0] + sizes)
    for dst, r in zip((out_g, out_d, out_m, out_v), res):
        rf = r.reshape(-1)
        for i, n in enumerate(small):
            dst[n] = rf[offs[i] : offs[i + 1]].reshape(weights[n].shape)

    done = res[0]
    for names, send, group, tag in last:
        done = exchange_finish(names, send, group, tag, done)

    return (loss, dx0[None], *[out_g[n] for n in order], *[out_d[n] for n in order],
            *[out_m[n] for n in order], *[out_v[n] for n in order])
```

```python
import functools

import jax
import jax.numpy as jnp
import numpy as np
from jax import lax
from jax.experimental import pallas as pl
from jax.experimental.pallas import tpu as pltpu

D_MODEL = 1024
D_FF = 2816
ATT_HEADS = 8
ATT_HEAD_DIM = 64
ATT_WIDTH = 512
CHUNK = 64
LEFT_CHUNKS = 8
MAX_REL = 128
LRU_WIDTH = 1024
LRU_BLOCKS = 16
LRU_BLOCK = 64
LRU_C = 8.0
EPS = 1e-6
PROJ_WIDTH = 5632
N_DEV = 8

ADAM_LR = 0.001
ADAM_B1 = 0.9
ADAM_B2 = 0.999
ADAM_EPS = 1e-08
ADAM_WD = 0.01
ADAM_STEP = 10

V7X_LANES = 128
V7X_SUBLANES = 8
V7X_VMEM_BYTES = 64 * 1024 * 1024
VMEM_LIMIT = V7X_VMEM_BYTES - 8 * 1024 * 1024

ATT_TQ = 256
NEG = -1e30
BF16 = jnp.bfloat16
F32 = jnp.float32
MESH = pl.DeviceIdType.MESH

OFF_Q = 4 * LRU_WIDTH
OFF_K = OFF_Q + ATT_WIDTH
OFF_V = OFF_K + ATT_WIDTH


def _cparams(**kw):
    return pltpu.CompilerParams(vmem_limit_bytes=VMEM_LIMIT, **kw)


def _pick(n, target, unit=V7X_LANES):
    best = None
    for t in range(unit, min(n, target) + 1, unit):
        if n % t == 0:
            best = t
    return n if best is None else best


_DIMS = {
    "nn": (((1,), (0,)), ((), ())),
    "nt": (((1,), (1,)), ((), ())),
    "tn": (((0,), (0,)), ((), ())),
}


ANY_SPEC = pl.BlockSpec(memory_space=pl.ANY)


def _matmul(a, b, mode, out_dtype, name, tm=1024, tn=512, tk=1408, deps=(), b_shift=0):
    n_deps = len(deps)
    if mode == "nn":
        (m, k), (k2, n) = a.shape, b.shape
    elif mode == "nt":
        (m, k), (n, k2) = a.shape, b.shape
    else:
        (k, m), (k2, n) = a.shape, b.shape
    assert k == k2, (a.shape, b.shape, mode)
    tm, tn, tk = _pick(m, tm), _pick(n, tn), _pick(k, tk)
    nk = k // tk
    dims = _DIMS[mode]

    def body(a_ref, b_ref, *rest):
        o_ref, scratch = rest[n_deps], rest[n_deps + 1 :]
        p = lax.dot_general(a_ref[...], b_ref[...], dims, preferred_element_type=F32)
        if nk == 1:
            o_ref[...] = p.astype(o_ref.dtype)
        else:
            acc = scratch[0]
            kk = pl.program_id(2)

            @pl.when(kk == 0)
            def _():
                acc[...] = p

            @pl.when(kk > 0)
            def _():
                acc[...] += p

            @pl.when(kk == nk - 1)
            def _():
                o_ref[...] = acc[...].astype(o_ref.dtype)

    if mode == "nn":
        a_spec = pl.BlockSpec((tm, tk), lambda i, j, kk: (i, kk))
        b_spec = pl.BlockSpec((tk, tn), lambda i, j, kk: (kk, j))
    elif mode == "nt":
        a_spec = pl.BlockSpec((tm, tk), lambda i, j, kk: (i, kk))
        b_spec = pl.BlockSpec((tn, tk), lambda i, j, kk: ((j + b_shift) % (n // tn), kk))
    else:
        a_spec = pl.BlockSpec((tk, tm), lambda i, j, kk: (kk, i))
        b_spec = pl.BlockSpec((tk, tn), lambda i, j, kk: (kk, j))
    return pl.pallas_call(
        body,
        name=name,
        grid=(m // tm, n // tn, nk),
        in_specs=[a_spec, b_spec] + [ANY_SPEC] * n_deps,
        out_specs=pl.BlockSpec((tm, tn), lambda i, j, kk: (i, j)),
        out_shape=jax.ShapeDtypeStruct((m, n), out_dtype),
        scratch_shapes=[pltpu.VMEM((tm, tn), F32)] if nk > 1 else [],
        compiler_params=_cparams(dimension_semantics=("parallel", "parallel", "arbitrary")),
    )(a, b, *deps)


def _rowwise(fn, name, params, tiles, outs, accs=(), ts=256, with_index=False, deps=()):
    norm = []
    for t in tiles:
        if not isinstance(t, tuple):
            t = (t, t.shape[1], 0)
        norm.append(t if len(t) == 4 else (*t, None))
    s = norm[0][0].shape[0]
    ts = min(ts, s)
    assert s % ts == 0 and ts % V7X_SUBLANES == 0
    steps = s // ts
    halo_blocks = ts // V7X_SUBLANES
    n_p, n_t, n_o = len(params), len(norm), len(outs)

    def body(*refs):
        i = pl.program_id(0)
        vals = [r[...] for r in refs[: n_p + n_t]]
        res = fn(i, steps, *vals) if with_index else fn(*vals)
        if not isinstance(res, (tuple, list)):
            res = (res,)
        first_out = n_p + n_t + len(deps)
        o_refs = refs[first_out : first_out + n_o]
        a_refs = refs[first_out + n_o :]
        for r, v in zip(o_refs, res[:n_o]):
            r[...] = v.astype(r.dtype)
        for r, v in zip(a_refs, res[n_o:]):
            _accumulate(r, v, i)

    in_specs = [pl.BlockSpec(p.shape, lambda i: (0, 0)) for p in params]
    for arr, w, cb, halo in norm:
        if halo is None:
            in_specs.append(pl.BlockSpec((ts, w), lambda i, cb=cb: (i, cb)))
        elif halo == "prev":
            in_specs.append(
                pl.BlockSpec((V7X_SUBLANES, w), lambda i, cb=cb: (jnp.maximum(i * halo_blocks - 1, 0), cb))
            )
        else:
            last = s // V7X_SUBLANES - 1
            in_specs.append(
                pl.BlockSpec((V7X_SUBLANES, w), lambda i, cb=cb: (jnp.minimum((i + 1) * halo_blocks, last), cb))
            )
    in_specs += [ANY_SPEC] * len(deps)
    out_specs = [pl.BlockSpec((ts, w), lambda i: (i, 0)) for w, _ in outs]
    out_specs += [pl.BlockSpec(shape, lambda i: (0, 0)) for shape in accs]
    out_shape = [jax.ShapeDtypeStruct((s, w), dt) for w, dt in outs]
    out_shape += [jax.ShapeDtypeStruct(shape, F32) for shape in accs]
    res = pl.pallas_call(
        body,
        name=name,
        grid=(steps,),
        in_specs=in_specs,
        out_specs=out_specs,
        out_shape=out_shape,
        compiler_params=_cparams(dimension_semantics=("arbitrary",)),
    )(*params, *[t[0] for t in norm], *deps)
    return res


def _accumulate(ref, val, step):
    @pl.when(step == 0)
    def _():
        ref[...] = val

    @pl.when(step > 0)
    def _():
        ref[...] += val


def _sigmoid(z):
    return jax.nn.sigmoid(z)


def _silu(z):
    return z * _sigmoid(z)


def _gelu(z):
    return 0.5 * z * (1.0 + jnp.tanh(0.7978845608028654 * (z + 0.044715 * (z * z * z))))


def _pre_fn(g, shift, scale, x):
    r = lax.rsqrt(jnp.mean(x * x, axis=-1, keepdims=True) + EPS)
    return ((x * r) * g) * (1.0 + scale) + shift


def _post_fn(res_w, g, gate, f, x):
    r = lax.rsqrt(jnp.mean(f * f, axis=-1, keepdims=True) + EPS)
    return x + (res_w * gate) * ((f * r) * g)


def _swiglu_fn(gu):
    return _silu(gu[:, :D_FF]) * gu[:, D_FF:]


def _gates_fn(ba, bx, lam, pre, xc):
    ra = _sigmoid(pre[:, :LRU_WIDTH] + ba)
    ia = _sigmoid(pre[:, LRU_WIDTH:] + bx)
    softplus = jnp.maximum(-lam, 0.0) + jnp.log1p(jnp.exp(-jnp.abs(lam)))
    log_a = (-LRU_C) * ra * softplus
    a = jnp.exp(log_a)
    mult = jnp.sqrt(-jnp.tanh(log_a) * (a * a + 1.0))
    return a, mult * (ia * xc)


def _recin_fn(hs, yr):
    return hs * _gelu(yr)


def _merge_fn(att, rec, g_att, g_rec):
    return _sigmoid(g_att) * att + _sigmoid(g_rec) * rec


def _rowsum(v):
    return jnp.sum(v, axis=0, keepdims=True)


def _pre_fwd(x, g, shift, scale, name, deps=()):
    (h,) = _rowwise(_pre_fn, name, [g, shift, scale], [x], [(D_MODEL, BF16)], deps=deps)
    return h


def _pre_bwd(x, g, shift, scale, dh, dres, name):
    def fn(g, shift, scale, x, dh, dres):
        _, vjp = jax.vjp(_pre_fn, g, shift, scale, x)
        dg, dshift, dscale, dx = vjp(dh)
        return dx + dres, dg, dshift, dscale

    row = (1, D_MODEL)
    return _rowwise(fn, name, [g, shift, scale], [x, dh, dres], [(D_MODEL, F32)], [row, row, row])


def _post_fwd(f, x, g, gate, res_w, name):
    (y,) = _rowwise(functools.partial(_post_fn, res_w), name, [g, gate], [f, x], [(D_MODEL, F32)])
    return y


def _post_bwd(f, g, gate, res_w, dy, name, deps=()):
    def fn(g, gate, f, dy):
        _, vjp = jax.vjp(lambda g, gate, f: _post_fn(res_w, g, gate, f, 0.0), g, gate, f)
        dg, dgate, df = vjp(dy)
        return df, dg, dgate

    row = (1, D_MODEL)
    return _rowwise(fn, name, [g, gate], [f, dy], [(D_MODEL, BF16)], [row, row], deps=deps)


def _loss_stage(y, target, name):
    def fn(y, t):
        diff = y - t
        return diff * (1.0 / D_MODEL), _rowsum(diff * diff)

    return _rowwise(fn, name, [], [y, target], [(D_MODEL, F32)], [(1, D_MODEL)])


FFN_TM = 512
FFN_TF = 1408


def _glu_fn(g, u):
    return _silu(g) * u


def _ffn_up(h, w_gu_t, name):
    s = h.shape[0]
    tm = min(FFN_TM, s)
    nf = D_FF // FFN_TF

    def body(h_ref, wg_ref, wu_ref, a_ref, g_ref, u_ref):
        hv = h_ref[...]
        g = lax.dot_general(hv, wg_ref[...], _DIMS["nt"], preferred_element_type=F32)
        u = lax.dot_general(hv, wu_ref[...], _DIMS["nt"], preferred_element_type=F32)
        a_ref[...] = _glu_fn(g, u).astype(a_ref.dtype)
        g_ref[...] = g.astype(g_ref.dtype)
        u_ref[...] = u.astype(u_ref.dtype)

    out = pl.BlockSpec((tm, FFN_TF), lambda i, j: (i, j))
    return pl.pallas_call(
        body,
        name=name,
        grid=(s // tm, nf),
        in_specs=[pl.BlockSpec((tm, D_MODEL), lambda i, j: (i, 0)),
                  pl.BlockSpec((FFN_TF, D_MODEL), lambda i, j: (j, 0)),
                  pl.BlockSpec((FFN_TF, D_MODEL), lambda i, j: (nf + j, 0))],
        out_specs=[out, out, out],
        out_shape=[jax.ShapeDtypeStruct((s, D_FF), BF16)] * 3,
        compiler_params=_cparams(dimension_semantics=("parallel", "arbitrary")),
    )(h, w_gu_t, w_gu_t)


def _ffn_up_bwd(df, w_down, g, u, name, deps=()):
    s = df.shape[0]
    tm = min(FFN_TM, s)

    def body(df_ref, wd_ref, g_ref, u_ref, *rest):
        dg_ref, du_ref = rest[len(deps) :]
        da = lax.dot_general(df_ref[...], wd_ref[...], _DIMS["nt"], preferred_element_type=F32)
        _, vjp = jax.vjp(_glu_fn, g_ref[...].astype(F32), u_ref[...].astype(F32))
        dg, du = vjp(da)
        dg_ref[...] = dg.astype(dg_ref.dtype)
        du_ref[...] = du.astype(du_ref.dtype)

    blk = pl.BlockSpec((tm, FFN_TF), lambda i, j: (i, j))
    return pl.pallas_call(
        body,
        name=name,
        grid=(s // tm, D_FF // FFN_TF),
        in_specs=[pl.BlockSpec((tm, D_MODEL), lambda i, j: (i, 0)),
                  pl.BlockSpec((FFN_TF, D_MODEL), lambda i, j: (j, 0)), blk, blk] + [ANY_SPEC] * len(deps),
        out_specs=[blk, blk],
        out_shape=[jax.ShapeDtypeStruct((s, D_FF), BF16)] * 2,
        compiler_params=_cparams(dimension_semantics=("parallel", "arbitrary")),
    )(df, w_down, g, u, *deps)


def _ffn_dh(dg, du, w_gu_t, name, deps=()):
    s = dg.shape[0]
    tm, tn = min(FFN_TM, s), 512

    def body(dg_ref, du_ref, wg_ref, wu_ref, *rest):
        o_ref = rest[len(deps)]
        p = jnp.dot(dg_ref[...], wg_ref[...], preferred_element_type=F32)
        o_ref[...] = p + jnp.dot(du_ref[...], wu_ref[...], preferred_element_type=F32)

    a_spec = pl.BlockSpec((tm, D_FF), lambda i, j: (i, 0))
    return pl.pallas_call(
        body,
        name=name,
        grid=(s // tm, D_MODEL // tn),
        in_specs=[a_spec, a_spec,
                  pl.BlockSpec((D_FF, tn), lambda i, j: (0, j)),
                  pl.BlockSpec((D_FF, tn), lambda i, j: (1, j))] + [ANY_SPEC] * len(deps),
        out_specs=pl.BlockSpec((tm, tn), lambda i, j: (i, j)),
        out_shape=jax.ShapeDtypeStruct((s, D_MODEL), F32),
        compiler_params=_cparams(dimension_semantics=("parallel", "arbitrary")),
    )(dg, du, w_gu_t, w_gu_t, *deps)


FUSE_TM = 256
DW_TK = 2048
ROW_SPEC2 = pl.BlockSpec((1, D_MODEL), lambda i, j: (0, 0))
ROW_SPEC1 = pl.BlockSpec((1, D_MODEL), lambda i: (0, 0))
SUMS_SPEC1 = pl.BlockSpec((V7X_SUBLANES, D_MODEL), lambda i: (0, 0))
SUMS_SPEC2 = pl.BlockSpec((V7X_SUBLANES, D_MODEL), lambda i, j: (0, 0))
SUMS_SHAPE = jax.ShapeDtypeStruct((V7X_SUBLANES, D_MODEL), F32)


def _sum_rows(*rows):
    pad = jnp.zeros((V7X_SUBLANES - len(rows), rows[0].shape[1]), F32)
    return jnp.concatenate([*rows, pad], axis=0)


def _pre_up(x, g, shift, scale, w_gu_t, name, deps=()):
    s = x.shape[0]
    tm = min(FFN_TM, s)
    nf = D_FF // FFN_TF
    nd = len(deps)

    def body(x_ref, g_ref, sh_ref, sc_ref, wg_ref, wu_ref, *rest):
        h_ref, a_ref, gg_ref, u_ref, h_s = rest[nd:]

        @pl.when(pl.program_id(1) == 0)
        def _():
            h = _pre_fn(g_ref[...], sh_ref[...], sc_ref[...], x_ref[...]).astype(BF16)
            h_s[...] = h
            h_ref[...] = h

        hv = h_s[...]
        gv = lax.dot_general(hv, wg_ref[...], _DIMS["nt"], preferred_element_type=F32)
        uv = lax.dot_general(hv, wu_ref[...], _DIMS["nt"], preferred_element_type=F32)
        a_ref[...] = _glu_fn(gv, uv).astype(a_ref.dtype)
        gg_ref[...] = gv.astype(gg_ref.dtype)
        u_ref[...] = uv.astype(u_ref.dtype)

    rows = pl.BlockSpec((tm, D_MODEL), lambda i, j: (i, 0))
    out = pl.BlockSpec((tm, FFN_TF), lambda i, j: (i, j))
    return pl.pallas_call(
        body,
        name=name,
        grid=(s // tm, nf),
        in_specs=[rows, ROW_SPEC2, ROW_SPEC2, ROW_SPEC2,
                  pl.BlockSpec((FFN_TF, D_MODEL), lambda i, j: (j, 0)),
                  pl.BlockSpec((FFN_TF, D_MODEL), lambda i, j: (nf + j, 0))] + [ANY_SPEC] * nd,
        out_specs=[rows, out, out, out],
        out_shape=[jax.ShapeDtypeStruct((s, D_MODEL), BF16)] + [jax.ShapeDtypeStruct((s, D_FF), BF16)] * 3,
        scratch_shapes=[pltpu.VMEM((tm, D_MODEL), BF16)],
        compiler_params=_cparams(dimension_semantics=("parallel", "arbitrary")),
    )(x, g, shift, scale, w_gu_t, w_gu_t, *deps)


def _pre_matmul(x, g, shift, scale, w_t, name, b_shift=0, tn=512):
    s = x.shape[0]
    n = w_t.shape[0]
    tm = min(2 * FFN_TM, s)

    def body(x_ref, g_ref, sh_ref, sc_ref, w_ref, h_ref, o_ref, h_s):
        @pl.when(pl.program_id(1) == 0)
        def _():
            h = _pre_fn(g_ref[...], sh_ref[...], sc_ref[...], x_ref[...]).astype(BF16)
            h_s[...] = h
            h_ref[...] = h

        o_ref[...] = lax.dot_general(h_s[...], w_ref[...], _DIMS["nt"], preferred_element_type=F32)

    rows = pl.BlockSpec((tm, D_MODEL), lambda i, j: (i, 0))
    return pl.pallas_call(
        body,
        name=name,
        grid=(s // tm, n // tn),
        in_specs=[rows, ROW_SPEC2, ROW_SPEC2, ROW_SPEC2,
                  pl.BlockSpec((tn, D_MODEL), lambda i, j: ((j + b_shift) % (n // tn), 0))],
        out_specs=[rows, pl.BlockSpec((tm, tn), lambda i, j: (i, j))],
        out_shape=[jax.ShapeDtypeStruct((s, D_MODEL), BF16), jax.ShapeDtypeStruct((s, n), F32)],
        scratch_shapes=[pltpu.VMEM((tm, D_MODEL), BF16)],
        compiler_params=_cparams(dimension_semantics=("parallel", "arbitrary")),
    )(x, g, shift, scale, w_t)


def _matmul_post(a, w, x, g_post, gate, res_w, name, target=None):
    s, k = a.shape
    tm = min(FFN_TM, s)
    extra = [] if target is None else [target]

    def body(a_ref, w_ref, x_ref, g_ref, gate_ref, *rest):
        f = jnp.dot(a_ref[...], w_ref[...], preferred_element_type=F32)
        y = _post_fn(res_w, g_ref[...], gate_ref[...], f, x_ref[...])
        if target is None:
            f_ref, y_ref = rest
            y_ref[...] = y
        else:
            t_ref, f_ref, dy_ref, sq_ref = rest
            diff = y - t_ref[...]
            dy_ref[...] = diff * (1.0 / D_MODEL)
            _accumulate(sq_ref, _rowsum(diff * diff), pl.program_id(0))
        f_ref[...] = f

    rows = pl.BlockSpec((tm, D_MODEL), lambda i: (i, 0))
    out_specs, out_shape = [rows, rows], [jax.ShapeDtypeStruct((s, D_MODEL), F32)] * 2
    if target is not None:
        out_specs.append(ROW_SPEC1)
        out_shape.append(jax.ShapeDtypeStruct((1, D_MODEL), F32))
    return pl.pallas_call(
        body,
        name=name,
        grid=(s // tm,),
        in_specs=[pl.BlockSpec((tm, k), lambda i: (i, 0)), pl.BlockSpec((k, D_MODEL), lambda i: (0, 0)), rows,
                  ROW_SPEC1, ROW_SPEC1] + [rows] * len(extra),
        out_specs=out_specs,
        out_shape=out_shape,
        compiler_params=_cparams(dimension_semantics=("arbitrary",)),
    )(a, w, x, g_post, gate, *extra)


def _merge_matmul_post(att, rec, proj, w, x, g_post, gate, res_w, name):
    s = att.shape[0]
    tm = min(FUSE_TM, s)

    def body(att_ref, rec_ref, ga_ref, gr_ref, w_ref, x_ref, g_ref, gate_ref, m_ref, f_ref, y_ref):
        merged = _merge_fn(att_ref[...], rec_ref[...], ga_ref[...], gr_ref[...]).astype(BF16)
        m_ref[...] = merged
        f = jnp.dot(merged, w_ref[...], preferred_element_type=F32)
        f_ref[...] = f
        y_ref[...] = _post_fn(res_w, g_ref[...], gate_ref[...], f, x_ref[...])

    rows = pl.BlockSpec((tm, D_MODEL), lambda i: (i, 0))
    return pl.pallas_call(
        body,
        name=name,
        grid=(s // tm,),
        in_specs=[rows, rows, pl.BlockSpec((tm, D_MODEL), lambda i: (i, 2)), pl.BlockSpec((tm, D_MODEL), lambda i: (i, 3)),
                  pl.BlockSpec(w.shape, lambda i: (0, 0)), rows, ROW_SPEC1, ROW_SPEC1],
        out_specs=[rows, rows, rows],
        out_shape=[jax.ShapeDtypeStruct((s, D_MODEL), BF16)] + [jax.ShapeDtypeStruct((s, D_MODEL), F32)] * 2,
        compiler_params=_cparams(dimension_semantics=("parallel",)),
    )(att, rec, proj, proj, w, x, g_post, gate)


def _post_bwd_merge_bwd(f, dy, g_post, gate, res_w, w, att, rec, proj, name):
    s = f.shape[0]
    tm = min(FUSE_TM, s)

    def body(f_ref, dy_ref, gp_ref, gate_ref, w_ref, att_ref, rec_ref, ga_ref, gr_ref,
             df_ref, datt_ref, drec_ref, dga_ref, dgr_ref, sums_ref):
        i = pl.program_id(0)
        dgp, dgate, df = _post_vjp(res_w, gp_ref[...], gate_ref[...], f_ref[...], dy_ref[...])
        dfb = df.astype(BF16)
        df_ref[...] = dfb
        _accumulate(sums_ref, _sum_rows(dgp, dgate), i)
        dmerged = lax.dot_general(dfb, w_ref[...], _DIMS["nt"], preferred_element_type=F32)
        _, vjp = jax.vjp(_merge_fn, att_ref[...], rec_ref[...], ga_ref[...], gr_ref[...])
        for ref, val in zip((datt_ref, drec_ref, dga_ref, dgr_ref), vjp(dmerged)):
            ref[...] = val.astype(ref.dtype)

    rows = pl.BlockSpec((tm, D_MODEL), lambda i: (i, 0))
    return pl.pallas_call(
        body,
        name=name,
        grid=(s // tm,),
        in_specs=[rows, rows, ROW_SPEC1, ROW_SPEC1, pl.BlockSpec(w.shape, lambda i: (0, 0)), rows, rows,
                  pl.BlockSpec((tm, D_MODEL), lambda i: (i, 2)), pl.BlockSpec((tm, D_MODEL), lambda i: (i, 3))],
        out_specs=[rows] * 5 + [SUMS_SPEC1],
        out_shape=[jax.ShapeDtypeStruct((s, D_MODEL), BF16)] * 5 + [SUMS_SHAPE],
        compiler_params=_cparams(dimension_semantics=("arbitrary",)),
    )(f, dy, g_post, gate, w, att, rec, proj, proj)


def _matmul_recin_bwd(drec, w, hs, proj, name, deps=()):
    s = drec.shape[0]
    tm = min(FUSE_TM, s)
    nd = len(deps)

    def body(d_ref, w_ref, hs_ref, yr_ref, *rest):
        dhs_ref, dyr_ref = rest[nd:]
        d = lax.dot_general(d_ref[...], w_ref[...], _DIMS["nt"], preferred_element_type=F32)
        _, vjp = jax.vjp(_recin_fn, hs_ref[...], yr_ref[...])
        dhs, dyr = vjp(d)
        dhs_ref[...] = dhs
        dyr_ref[...] = dyr.astype(dyr_ref.dtype)

    rows = pl.BlockSpec((tm, D_MODEL), lambda i: (i, 0))
    return pl.pallas_call(
        body,
        name=name,
        grid=(s // tm,),
        in_specs=[rows, pl.BlockSpec(w.shape, lambda i: (0, 0)), rows,
                  pl.BlockSpec((tm, D_MODEL), lambda i: (i, 1))] + [ANY_SPEC] * nd,
        out_specs=[rows, rows],
        out_shape=[jax.ShapeDtypeStruct((s, D_MODEL), F32), jax.ShapeDtypeStruct((s, D_MODEL), BF16)],
        compiler_params=_cparams(dimension_semantics=("parallel",)),
    )(drec, w, hs, proj, *deps)


def _post_vjp(res_w, g, gate, f, dy):
    _, vjp = jax.vjp(lambda g, gate, f: _post_fn(res_w, g, gate, f, 0.0), g, gate, f)
    return vjp(dy)


def _post_bwd_up_bwd(f, dy, g_post, gate, res_w, w_down, g, u, name, deps=()):
    s = f.shape[0]
    tm = min(FFN_TM, s)
    nd = len(deps)

    def body(f_ref, dy_ref, gp_ref, gate_ref, wd_ref, g_ref, u_ref, *rest):
        df_ref, dgu_ref, sums_ref, df_s = rest[nd:]
        i = pl.program_id(0)

        @pl.when(pl.program_id(1) == 0)
        def _():
            dgp, dgate, df = _post_vjp(res_w, gp_ref[...], gate_ref[...], f_ref[...], dy_ref[...])
            df_s[...] = df.astype(BF16)
            df_ref[...] = df_s[...]
            _accumulate(sums_ref, _sum_rows(dgp, dgate), i)

        da = lax.dot_general(df_s[...], wd_ref[...], _DIMS["nt"], preferred_element_type=F32)
        _, vjp = jax.vjp(_glu_fn, g_ref[...].astype(F32), u_ref[...].astype(F32))
        dg, du = vjp(da)
        dgu_ref[0] = dg.astype(dgu_ref.dtype)
        dgu_ref[1] = du.astype(dgu_ref.dtype)

    rows = pl.BlockSpec((tm, D_MODEL), lambda i, j: (i, 0))
    blk = pl.BlockSpec((tm, FFN_TF), lambda i, j: (i, j))
    return pl.pallas_call(
        body,
        name=name,
        grid=(s // tm, D_FF // FFN_TF),
        in_specs=[rows, rows, ROW_SPEC2, ROW_SPEC2, pl.BlockSpec((FFN_TF, D_MODEL), lambda i, j: (j, 0)), blk,
                  blk] + [ANY_SPEC] * nd,
        out_specs=[rows, pl.BlockSpec((2, tm, FFN_TF), lambda i, j: (0, i, j)), SUMS_SPEC2],
        out_shape=[jax.ShapeDtypeStruct((s, D_MODEL), BF16), jax.ShapeDtypeStruct((2, s, D_FF), BF16), SUMS_SHAPE],
        scratch_shapes=[pltpu.VMEM((tm, D_MODEL), BF16)],
        compiler_params=_cparams(dimension_semantics=("arbitrary", "arbitrary")),
    )(f, dy, g_post, gate, w_down, g, u, *deps)


def _post_bwd_matmul(f, dy, g_post, gate, res_w, w, name):
    s = f.shape[0]
    n = w.shape[0]
    tm = min(FUSE_TM, s)

    def body(f_ref, dy_ref, gp_ref, gate_ref, w_ref, df_ref, o_ref, dgp_ref, dgate_ref):
        i = pl.program_id(0)
        dgp, dgate, df = _post_vjp(res_w, gp_ref[...], gate_ref[...], f_ref[...], dy_ref[...])
        dfb = df.astype(BF16)
        df_ref[...] = dfb
        _accumulate(dgp_ref, dgp, i)
        _accumulate(dgate_ref, dgate, i)
        o_ref[...] = lax.dot_general(dfb, w_ref[...], _DIMS["nt"], preferred_element_type=F32)

    rows = pl.BlockSpec((tm, D_MODEL), lambda i: (i, 0))
    return pl.pallas_call(
        body,
        name=name,
        grid=(s // tm,),
        in_specs=[rows, rows, ROW_SPEC1, ROW_SPEC1, pl.BlockSpec((n, D_MODEL), lambda i: (0, 0))],
        out_specs=[rows, pl.BlockSpec((tm, n), lambda i: (i, 0)), ROW_SPEC1, ROW_SPEC1],
        out_shape=[jax.ShapeDtypeStruct((s, D_MODEL), BF16), jax.ShapeDtypeStruct((s, n), F32),
                   jax.ShapeDtypeStruct((1, D_MODEL), F32), jax.ShapeDtypeStruct((1, D_MODEL), F32)],
        compiler_params=_cparams(dimension_semantics=("arbitrary",)),
    )(f, dy, g_post, gate, w)


def _matmul_pre_bwd(parts, w_t, x, dres, g, shift, scale, name, deps=()):
    s = x.shape[0]
    na, nd = len(parts), len(deps)
    ranges = [p[3] for p in parts]

    def body(*refs):
        a_refs = refs[:na]
        w_ref, x_ref, dres_ref, g_ref, sh_ref, sc_ref = refs[na : na + 6]
        dx_ref, sums_ref = refs[na + 6 + nd :]
        i = pl.program_id(0)
        dh = None
        for a_ref, (r0, r1) in zip(a_refs, ranges):
            p = jnp.dot(a_ref[...], w_ref[r0:r1, :], preferred_element_type=F32)
            dh = p if dh is None else dh + p
        _, vjp = jax.vjp(_pre_fn, g_ref[...], sh_ref[...], sc_ref[...], x_ref[...])
        dg, dsh, dsc, dx = vjp(dh)
        dx_ref[...] = dx + dres_ref[...]
        _accumulate(sums_ref, _sum_rows(dg, dsh, dsc), i)

    tm = parts[0][1][-2]
    rows = pl.BlockSpec((tm, D_MODEL), lambda i: (i, 0))
    return pl.pallas_call(
        body,
        name=name,
        grid=(s // tm,),
        in_specs=[pl.BlockSpec(p[1], p[2]) for p in parts]
        + [pl.BlockSpec(w_t.shape, lambda i: (0, 0)), rows, rows, ROW_SPEC1, ROW_SPEC1, ROW_SPEC1]
        + [ANY_SPEC] * nd,
        out_specs=[rows, SUMS_SPEC1],
        out_shape=[jax.ShapeDtypeStruct((s, D_MODEL), F32), SUMS_SHAPE],
        compiler_params=_cparams(dimension_semantics=("arbitrary",)),
    )(*[p[0] for p in parts], w_t, x, dres, g, shift, scale, *deps)


def _dw_gu(dgu, h, name, deps=(), tk=DW_TK):
    s = h.shape[0]
    tk = min(tk, s)
    nk = s // tk
    half = D_FF // FFN_TF

    def body(a_ref, b_ref, *rest):
        o_ref, acc = rest[len(deps) :]
        kk = pl.program_id(1)
        p = lax.dot_general(a_ref[...], b_ref[...], _DIMS["tn"], preferred_element_type=F32)

        @pl.when(kk == 0)
        def _():
            acc[...] = p

        @pl.when(kk > 0)
        def _():
            acc[...] += p

        @pl.when(kk == nk - 1)
        def _():
            o_ref[...] = acc[...].astype(o_ref.dtype)

    return pl.pallas_call(
        body,
        name=name,
        grid=(2 * half, nk),
        in_specs=[pl.BlockSpec((None, tk, FFN_TF), lambda i, kk: (i // half, kk, i % half)),
                  pl.BlockSpec((tk, D_MODEL), lambda i, kk: (kk, 0))] + [ANY_SPEC] * len(deps),
        out_specs=pl.BlockSpec((FFN_TF, D_MODEL), lambda i, kk: (i, 0)),
        out_shape=jax.ShapeDtypeStruct((2 * D_FF, D_MODEL), BF16),
        scratch_shapes=[pltpu.VMEM((FFN_TF, D_MODEL), F32)],
        compiler_params=_cparams(dimension_semantics=("parallel", "arbitrary")),
    )(dgu, h, *deps)


def _lru_diag_blocks(dw_bd, name):
    def body(w_ref, o_ref):
        for half in range(2):
            for n in range(LRU_BLOCKS):
                rows = slice(n * LRU_BLOCK, (n + 1) * LRU_BLOCK)
                cols = slice(half * LRU_WIDTH + n * LRU_BLOCK, half * LRU_WIDTH + (n + 1) * LRU_BLOCK)
                o_ref[half, rows, :] = w_ref[rows, cols]

    return pl.pallas_call(
        body, name=name, out_shape=jax.ShapeDtypeStruct((2, LRU_WIDTH, LRU_BLOCK), F32), compiler_params=_cparams()
    )(dw_bd)


def _swiglu_fwd(gu, name):
    (a,) = _rowwise(_swiglu_fn, name, [], [gu], [(D_FF, BF16)], ts=128)
    return a


def _swiglu_bwd(gu, da, name, deps=()):
    def fn(gu, da):
        _, vjp = jax.vjp(_swiglu_fn, gu)
        return vjp(da)[0]

    (dgu,) = _rowwise(fn, name, [], [gu, da], [(2 * D_FF, BF16)], ts=128, deps=deps)
    return dgu


def _shift_down(ext, j, rows):
    return pltpu.roll(ext, j, 0)[V7X_SUBLANES : V7X_SUBLANES + rows]


def _shift_up(ext, j, rows):
    return pltpu.roll(ext, ext.shape[0] - j, 0)[:rows] if j else ext[:rows]


LRU_SLAB = 256
N_SLABS = LRU_WIDTH // LRU_SLAB


def _slab_weights(wa, wx):
    per = LRU_SLAB // LRU_BLOCK
    eye = jnp.eye(per, dtype=wa.dtype)

    def diag(w):
        w4 = w.reshape(N_SLABS, per, LRU_BLOCK, LRU_BLOCK)
        return jnp.einsum("sbkj,bc->sbkcj", w4, eye).reshape(N_SLABS, LRU_SLAB, LRU_SLAB)

    return jnp.concatenate([diag(wa), diag(wx)], axis=2).reshape(LRU_WIDTH, 2 * LRU_SLAB).astype(BF16)


def _slab_cols(v, s):
    lo = s * LRU_SLAB
    return jnp.concatenate([v[:, lo : lo + LRU_SLAB], v[:, LRU_WIDTH + lo : LRU_WIDTH + lo + LRU_SLAB]], axis=1)


def _lru_front(proj, w8, b, w_slab, ba, bx, lam, name):
    def fn(i, steps, w8, b, w_slab, ba, bx, lam, x, halo):
        halo = jnp.where(i > 0, halo, 0.0)
        ext = jnp.concatenate([halo, x], axis=0)
        xc = b + w8[3:4] * x
        for j in (1, 2, 3):
            xc = xc + w8[3 - j : 4 - j] * _shift_down(ext, j, x.shape[0])
        xcb = xc.astype(BF16)
        prods = []
        for s in range(N_SLABS):
            rows = slice(s * LRU_SLAB, (s + 1) * LRU_SLAB)
            prods.append(jnp.dot(xcb[:, rows], w_slab[rows], preferred_element_type=F32))
        pre = jnp.concatenate([p[:, :LRU_SLAB] for p in prods] + [p[:, LRU_SLAB:] for p in prods], axis=1)
        a, u = _gates_fn(ba, bx, lam, pre, xc)
        return xc, pre, a, u

    tiles = [(proj, LRU_WIDTH, 0), (proj, LRU_WIDTH, 0, "prev")]
    outs = [(LRU_WIDTH, F32), (2 * LRU_WIDTH, F32), (LRU_WIDTH, F32), (LRU_WIDTH, F32)]
    return _rowwise(fn, name, [w8, b, w_slab, ba, bx, lam], tiles, outs, with_index=True)


def _lru_back(pre, xc, w_slab, ba, bx, lam, g, h_prev, name, deps=()):
    def fn(w_slab, ba, bx, lam, pre, xc, g, h_prev):
        _, vjp = jax.vjp(_gates_fn, ba, bx, lam, pre, xc)
        dba, dbx, dlam, dpre, dxc = vjp((g * h_prev, g))
        dpre = dpre.astype(BF16)
        back = []
        for s in range(N_SLABS):
            rows = slice(s * LRU_SLAB, (s + 1) * LRU_SLAB)
            back.append(lax.dot_general(_slab_cols(dpre, s), w_slab[rows], _DIMS["nt"], preferred_element_type=F32))
        return dpre, dxc + jnp.concatenate(back, axis=1), _sum_rows(dba, dbx, dlam)

    return _rowwise(fn, name, [w_slab, ba, bx, lam], [pre, xc, g, h_prev],
                    [(2 * LRU_WIDTH, BF16), (LRU_WIDTH, F32)], [(V7X_SUBLANES, LRU_WIDTH)], deps=deps)


def _lru_dw(xc, dpre, name):
    s = xc.shape[0]
    ts = min(512, s)
    steps = s // ts
    per = LRU_SLAB // LRU_BLOCK

    def body(x_ref, d_ref, o_ref, acc):
        i = pl.program_id(0)
        xcb = x_ref[...].astype(BF16)
        d = d_ref[...]
        for sl in range(N_SLABS):
            rows = slice(sl * LRU_SLAB, (sl + 1) * LRU_SLAB)
            p = lax.dot_general(xcb[:, rows], _slab_cols(d, sl), _DIMS["tn"], preferred_element_type=F32)

            @pl.when(i == 0)
            def _(p=p, rows=rows):
                acc[rows, :] = p

            @pl.when(i > 0)
            def _(p=p, rows=rows):
                acc[rows, :] += p

        @pl.when(i == steps - 1)
        def _():
            for half in range(2):
                for n in range(LRU_BLOCKS):
                    r0 = n * LRU_BLOCK
                    c0 = half * LRU_SLAB + (n % per) * LRU_BLOCK
                    o_ref[half, r0 : r0 + LRU_BLOCK, :] = acc[r0 : r0 + LRU_BLOCK, c0 : c0 + LRU_BLOCK]

    return pl.pallas_call(
        body,
        name=name,
        grid=(steps,),
        in_specs=[pl.BlockSpec((ts, LRU_WIDTH), lambda i: (i, 0)), pl.BlockSpec((ts, 2 * LRU_WIDTH), lambda i: (i, 0))],
        out_specs=pl.BlockSpec((2, LRU_WIDTH, LRU_BLOCK), lambda i: (0, 0, 0)),
        out_shape=jax.ShapeDtypeStruct((2, LRU_WIDTH, LRU_BLOCK), F32),
        scratch_shapes=[pltpu.VMEM((LRU_WIDTH, 2 * LRU_SLAB), F32)],
        compiler_params=_cparams(dimension_semantics=("arbitrary",)),
    )(xc, dpre)


def _conv_bwd(proj, w8, d1, name):
    def fn(i, steps, w8, x, halo, d, d1n):
        rows = x.shape[0]
        dn = jnp.where(i < steps - 1, d1n, 0.0)
        halo = jnp.where(i > 0, halo, 0.0)
        dext = jnp.concatenate([d, dn], axis=0)
        xext = jnp.concatenate([halo, x], axis=0)
        dx = w8[3:4] * d
        dw = [None] * 4
        dw[3] = _rowsum(d * x)
        for k in (1, 2, 3):
            dx = dx + w8[3 - k : 4 - k] * _shift_up(dext, k, rows)
            dw[3 - k] = _rowsum(d * _shift_down(xext, k, rows))
        return dx, _sum_rows(*dw, _rowsum(d))

    tiles = [(proj, LRU_WIDTH, 0), (proj, LRU_WIDTH, 0, "prev"), d1, (d1, LRU_WIDTH, 0, "next")]
    return _rowwise(fn, name, [w8], tiles, [(LRU_WIDTH, BF16)], [(V7X_SUBLANES, LRU_WIDTH)], with_index=True)


SCAN_ROWS = 512


def _block_scan(a, b, row, reverse):
    for d in (1, 2, 4):
        if reverse:
            shift, keep = V7X_SUBLANES - d, row < V7X_SUBLANES - d
        else:
            shift, keep = d, row >= d
        a_s = pltpu.roll(a, shift, 0)
        b_s = pltpu.roll(b, shift, 0)
        b = jnp.where(keep, a * b_s + b, b)
        a = jnp.where(keep, a * a_s, a)
    return a, b


def _scan_fwd(a, u, proj, name):
    s, w = a.shape
    ts = min(SCAN_ROWS, s)
    sub = ts // V7X_SUBLANES

    def body(a_ref, u_ref, yr_ref, h_ref, hp_ref, rec_ref, carry):
        @pl.when(pl.program_id(0) == 0)
        def _():
            carry[...] = jnp.zeros_like(carry)

        row = lax.broadcasted_iota(jnp.int32, (V7X_SUBLANES, w), 0)

        def step(j, c):
            rows = pl.ds(pl.multiple_of(j * V7X_SUBLANES, V7X_SUBLANES), V7X_SUBLANES)
            pa, pb = _block_scan(a_ref[rows, :], u_ref[rows, :], row, False)
            h = pb + pa * c
            h_ref[rows, :] = h
            hp_ref[rows, :] = jnp.where(row >= 1, pltpu.roll(h, 1, 0), c)
            return jnp.broadcast_to(h[V7X_SUBLANES - 1 :], (V7X_SUBLANES, w))

        carry[...] = lax.fori_loop(0, sub, step, carry[...])
        rec_ref[...] = _recin_fn(h_ref[...], yr_ref[...]).astype(rec_ref.dtype)

    spec = pl.BlockSpec((ts, w), lambda i: (i, 0))
    return pl.pallas_call(
        body,
        name=name,
        grid=(s // ts,),
        in_specs=[spec, spec, pl.BlockSpec((ts, w), lambda i: (i, 1))],
        out_specs=[spec, spec, spec],
        out_shape=[jax.ShapeDtypeStruct((s, w), F32)] * 2 + [jax.ShapeDtypeStruct((s, w), BF16)],
        scratch_shapes=[pltpu.VMEM((V7X_SUBLANES, w), F32)],
        compiler_params=_cparams(dimension_semantics=("arbitrary",)),
    )(a, u, proj)


def _scan_bwd(a, dh, name):
    s, w = a.shape
    ts = min(SCAN_ROWS, s)
    sub = ts // V7X_SUBLANES
    steps = s // ts

    def body(a_ref, d_ref, g_ref, carry):
        @pl.when(pl.program_id(0) == 0)
        def _():
            carry[...] = jnp.zeros_like(carry)

        row = lax.broadcasted_iota(jnp.int32, (V7X_SUBLANES, w), 0)

        def step(jj, c):
            j = sub - 1 - jj
            rows = pl.ds(pl.multiple_of(j * V7X_SUBLANES, V7X_SUBLANES), V7X_SUBLANES)
            av, dv = a_ref[rows, :], d_ref[rows, :]
            pa, pb = _block_scan(av, av * dv, row, True)
            big = pb + pa * c
            g_ref[rows, :] = dv + jnp.where(row < V7X_SUBLANES - 1, pltpu.roll(big, V7X_SUBLANES - 1, 0), c)
            return jnp.broadcast_to(big[:1], (V7X_SUBLANES, w))

        carry[...] = lax.fori_loop(0, sub, step, carry[...])

    spec = pl.BlockSpec((ts, w), lambda i: (steps - 1 - i, 0))
    return pl.pallas_call(
        body,
        name=name,
        grid=(steps,),
        in_specs=[spec, spec],
        out_specs=spec,
        out_shape=jax.ShapeDtypeStruct((s, w), F32),
        scratch_shapes=[pltpu.VMEM((V7X_SUBLANES, w), F32)],
        compiler_params=_cparams(dimension_semantics=("arbitrary",)),
    )(a, dh)


def _rel_index():
    i = np.arange(ATT_TQ)[:, None]
    j = np.arange(3 * ATT_TQ)[None, :]
    band = (j // CHUNK >= i // CHUNK) & (j // CHUNK <= i // CHUNK + LEFT_CHUNKS)
    return band


SKEW = 4 * ATT_TQ


def _skew_onehot():
    t = np.arange(SKEW)
    diag = np.where(t < 3 * ATT_TQ, -t, SKEW - t)
    idx = np.clip(diag + LEFT_CHUNKS * CHUNK, -MAX_REL, MAX_REL) + MAX_REL
    hit = (idx[:, None] == np.arange(2 * MAX_REL + 1)[None, :]) & (t[:, None] != 3 * ATT_TQ)
    return hit.astype(np.float32)


def _bias_tile(rel_bias, name):
    per_t = jnp.dot(rel_bias, jnp.asarray(_skew_onehot()).T, precision=lax.Precision.HIGHEST)
    win = 3 * ATT_TQ

    def body(t_ref, o_ref):
        tile = pltpu.roll(jnp.broadcast_to(t_ref[0], (ATT_TQ, SKEW)), 0, 1, stride=1, stride_axis=0)[:, :win]
        qc = lax.broadcasted_iota(jnp.int32, (ATT_TQ, win), 0) // CHUNK
        kpos = lax.broadcasted_iota(jnp.int32, (ATT_TQ, win), 1)
        band = (kpos // CHUNK >= qc) & (kpos // CHUNK <= qc + LEFT_CHUNKS)
        for v in range(3):
            o_ref[v, 0] = jnp.where(band & (kpos >= (2 - v) * ATT_TQ), tile, NEG)

    return pl.pallas_call(
        body,
        name=name,
        grid=(ATT_HEADS,),
        in_specs=[pl.BlockSpec((1, 1, SKEW), lambda h: (h, 0, 0))],
        out_specs=pl.BlockSpec((3, 1, ATT_TQ, win), lambda h: (0, h, 0, 0)),
        out_shape=jax.ShapeDtypeStruct((3, ATT_HEADS, ATT_TQ, win), F32),
        compiler_params=_cparams(dimension_semantics=("parallel",)),
    )(per_t.reshape(ATT_HEADS, 1, SKEW))


def _bias_grad(dbias, name):
    win = 3 * ATT_TQ

    def body(d_ref, o_ref):
        d = jnp.concatenate([d_ref[0], jnp.zeros((ATT_TQ, SKEW - win), F32)], axis=1)
        r = lax.broadcasted_iota(jnp.int32, (ATT_TQ, ATT_TQ), 0)
        c = lax.broadcasted_iota(jnp.int32, (ATT_TQ, ATT_TQ), 1)
        flip = (r + c == ATT_TQ - 1).astype(F32)
        d = jnp.dot(flip, d, preferred_element_type=F32, precision=lax.Precision.HIGHEST)
        o_ref[0] = jnp.sum(pltpu.roll(d, SKEW - (ATT_TQ - 1), 1, stride=1, stride_axis=0), axis=0, keepdims=True)

    per_t = pl.pallas_call(
        body,
        name=name,
        grid=(ATT_HEADS,),
        in_specs=[pl.BlockSpec((1, ATT_TQ, win), lambda h: (h, 0, 0))],
        out_specs=pl.BlockSpec((1, 1, SKEW), lambda h: (h, 0, 0)),
        out_shape=jax.ShapeDtypeStruct((ATT_HEADS, 1, SKEW), F32),
        compiler_params=_cparams(dimension_semantics=("parallel",)),
    )(dbias)
    return jnp.dot(per_t.reshape(ATT_HEADS, SKEW), jnp.asarray(_skew_onehot()), precision=lax.Precision.HIGHEST)


ATT_STEP_HEADS = ATT_HEADS
ATT_STEP_COLS = ATT_STEP_HEADS * ATT_HEAD_DIM


def _attn_specs(nt):
    qb, kb, vb = OFF_Q // ATT_STEP_COLS, OFF_K // ATT_STEP_COLS, OFF_V // ATT_STEP_COLS
    blk = (ATT_TQ, ATT_STEP_COLS)

    def qmap(base):
        return lambda hp, m: (jnp.minimum(m, nt - 1), base + hp)

    def wmap(base, back):
        return lambda hp, m: (jnp.clip(m - back, 0, nt - 1), base + hp)

    specs = [pl.BlockSpec(blk, qmap(qb))]
    specs += [pl.BlockSpec(blk, wmap(kb, back)) for back in (2, 1, 0)]
    specs += [pl.BlockSpec(blk, wmap(vb, back)) for back in (2, 1, 0)]
    return specs


ATT_SCALE = ATT_HEAD_DIM**-0.5


def _attn_exp(qh, kh, bias):
    s = lax.dot_general(qh, kh, _DIMS["nt"], preferred_element_type=F32) + bias
    e = jnp.exp(s - jnp.max(s, axis=-1, keepdims=True))
    return e, jnp.sum(e, axis=-1, keepdims=True)


def _attn_window(k0, k1, k2, v0, v1, v2):
    k = jnp.concatenate([k0[...], k1[...], k2[...]], axis=0).astype(BF16)
    v = jnp.concatenate([v0[...], v1[...], v2[...]], axis=0).astype(BF16)
    return k, v


def _bias_spec():
    return pl.BlockSpec((1, ATT_STEP_HEADS, ATT_TQ, 3 * ATT_TQ), lambda hp, m: (jnp.minimum(m, 2), hp, 0, 0))


def _attn_fwd(proj, bias, name):
    s = proj.shape[0]
    nt = s // ATT_TQ

    def body(q_ref, k0, k1, k2, v0, v1, v2, b_ref, o_ref):
        k, v = _attn_window(k0, k1, k2, v0, v1, v2)
        q = (q_ref[...] * ATT_SCALE).astype(BF16)
        for hh in range(ATT_STEP_HEADS):
            cols = slice(hh * ATT_HEAD_DIM, (hh + 1) * ATT_HEAD_DIM)
            e, total = _attn_exp(q[:, cols], k[:, cols], b_ref[0, hh])
            o = jnp.dot(e.astype(BF16), v[:, cols], preferred_element_type=F32) / total
            o_ref[:, cols] = o.astype(o_ref.dtype)

    specs = _attn_specs(nt) + [_bias_spec()]
    return pl.pallas_call(
        body,
        name=name,
        grid=(ATT_HEADS // ATT_STEP_HEADS, nt),
        in_specs=specs,
        out_specs=pl.BlockSpec((ATT_TQ, ATT_STEP_COLS), lambda hp, m: (m, hp)),
        out_shape=jax.ShapeDtypeStruct((s, ATT_WIDTH), BF16),
        compiler_params=_cparams(dimension_semantics=("parallel", "arbitrary")),
    )(proj, proj, proj, proj, proj, proj, proj, bias)


def _attn_bwd(proj, bias, do, name):
    s = proj.shape[0]
    nt = s // ATT_TQ
    win = 3 * ATT_TQ

    def body(q_ref, k0, k1, k2, v0, v1, v2, do_ref, b_ref, dq_ref, dk_ref, dv_ref, db_ref, dk_acc, dv_acc):
        m = pl.program_id(1)

        @pl.when(m == 0)
        def _():
            dk_acc[...] = jnp.zeros_like(dk_acc)
            dv_acc[...] = jnp.zeros_like(dv_acc)
            db_ref[...] = jnp.zeros_like(db_ref)

        @pl.when(m < nt)
        def _():
            k, v = _attn_window(k0, k1, k2, v0, v1, v2)
            q = (q_ref[...] * ATT_SCALE).astype(BF16)
            dout = do_ref[...]
            for hh in range(ATT_STEP_HEADS):
                cols = slice(hh * ATT_HEAD_DIM, (hh + 1) * ATT_HEAD_DIM)
                qh, kh, vh, doh = q[:, cols], k[:, cols], v[:, cols], dout[:, cols]
                e, total = _attn_exp(qh, kh, b_ref[0, hh])
                p = e / total
                dvh = lax.dot_general(p.astype(BF16), doh, _DIMS["tn"], preferred_element_type=F32)
                dp = lax.dot_general(doh, vh, _DIMS["nt"], preferred_element_type=F32)
                ds = p * (dp - jnp.sum(dp * p, axis=-1, keepdims=True))
                db_ref[hh] += ds
                dsb = ds.astype(BF16)
                dqh = jnp.dot(dsb, kh, preferred_element_type=F32) * ATT_SCALE
                dkh = lax.dot_general(dsb, qh, _DIMS["tn"], preferred_element_type=F32)
                dq_ref[:, cols] = dqh.astype(dq_ref.dtype)
                dk_acc[:, cols] += dkh
                dv_acc[:, cols] += dvh

        dk_ref[...] = dk_acc[:ATT_TQ].astype(dk_ref.dtype)
        dv_ref[...] = dv_acc[:ATT_TQ].astype(dv_ref.dtype)
        for acc in (dk_acc, dv_acc):
            rest = acc[ATT_TQ:]
            acc[: win - ATT_TQ] = rest
            acc[win - ATT_TQ :] = jnp.zeros((ATT_TQ, ATT_STEP_COLS), F32)

    blk = (ATT_TQ, ATT_STEP_COLS)
    specs = _attn_specs(nt)
    specs.append(pl.BlockSpec(blk, lambda hp, m: (jnp.minimum(m, nt - 1), hp)))
    specs.append(_bias_spec())
    done = lambda hp, m: (jnp.maximum(m - 2, 0), hp)
    out_specs = [
        pl.BlockSpec(blk, lambda hp, m: (jnp.minimum(m, nt - 1), hp)),
        pl.BlockSpec(blk, done),
        pl.BlockSpec(blk, done),
        pl.BlockSpec((ATT_STEP_HEADS, ATT_TQ, win), lambda hp, m: (hp, 0, 0)),
    ]
    out_shape = [jax.ShapeDtypeStruct((s, ATT_WIDTH), BF16)] * 3
    out_shape.append(jax.ShapeDtypeStruct((ATT_HEADS, ATT_TQ, win), F32))
    return pl.pallas_call(
        body,
        name=name,
        grid=(ATT_HEADS // ATT_STEP_HEADS, nt + 2),
        in_specs=specs,
        out_specs=out_specs,
        out_shape=out_shape,
        scratch_shapes=[pltpu.VMEM((win, ATT_STEP_COLS), F32), pltpu.VMEM((win, ATT_STEP_COLS), F32)],
        compiler_params=_cparams(dimension_semantics=("arbitrary", "arbitrary")),
    )(proj, proj, proj, proj, proj, proj, proj, do, bias)


def _ada_fwd(c_all, w, name):
    def body(c_ref, w_ref, o_ref):
        act = _silu(c_ref[...]).astype(BF16)
        o_ref[...] = jnp.dot(act, w_ref[...].astype(BF16), preferred_element_type=F32)

    return pl.pallas_call(
        body, name=name, out_shape=jax.ShapeDtypeStruct((c_all.shape[0], w.shape[1]), F32), compiler_params=_cparams()
    )(c_all, w)


def _ada_bwd(c_all, dmod, name):
    def body(c_ref, d_ref, o_ref):
        act = _silu(c_ref[...])
        o_ref[...] = lax.dot_general(act, d_ref[...], _DIMS["tn"], preferred_element_type=F32,
                                     precision=lax.Precision.HIGHEST)

    return pl.pallas_call(
        body, name=name, out_shape=jax.ShapeDtypeStruct((c_all.shape[1], dmod.shape[1]), F32), compiler_params=_cparams()
    )(c_all, dmod)


def _adamw_parts(landed, sent, me, w, m, v, name, rows=256):
    r, c = w.shape
    tr = _pick(r, rows, 16)

    def body(me_ref, g_ref, own_ref, w_ref, m_ref, v_ref, go_ref, d_ref, mo_ref, vo_ref):
        mine = me_ref[0]
        grad = jnp.zeros((tr, c), F32)
        for d in range(N_DEV):
            grad = grad + jnp.where(mine == d, own_ref[0], g_ref[d]).astype(F32)
        _adamw_update(grad, w_ref, m_ref, v_ref, go_ref, d_ref, mo_ref, vo_ref)

    spec = pl.BlockSpec((tr, c), lambda i, me_ref: (i, 0))
    return pl.pallas_call(
        body,
        name=name,
        grid_spec=pltpu.PrefetchScalarGridSpec(
            num_scalar_prefetch=1,
            grid=(r // tr,),
            in_specs=[pl.BlockSpec((N_DEV, tr, c), lambda i, me_ref: (0, i, 0)),
                      pl.BlockSpec((1, tr, c), lambda i, me_ref: (me_ref[0], i, 0)), spec, spec, spec],
            out_specs=[spec] * 4,
        ),
        out_shape=[jax.ShapeDtypeStruct((r, c), F32)] * 4,
        compiler_params=_cparams(dimension_semantics=("parallel",)),
    )(me.reshape(1).astype(jnp.int32), landed, sent, w, m, v)


def _adamw_update(grad, w_ref, m_ref, v_ref, go_ref, d_ref, mo_ref, vo_ref):
    m2 = ADAM_B1 * m_ref[...] + (1.0 - ADAM_B1) * grad
    v2 = ADAM_B2 * v_ref[...] + (1.0 - ADAM_B2) * (grad * grad)
    m_hat = m2 / (1.0 - ADAM_B1**ADAM_STEP)
    v_hat = v2 / (1.0 - ADAM_B2**ADAM_STEP)
    go_ref[...] = grad
    d_ref[...] = -ADAM_LR * (m_hat / (jnp.sqrt(v_hat) + ADAM_EPS) + ADAM_WD * w_ref[...])
    mo_ref[...] = m2
    vo_ref[...] = v2


def _adamw(g, w, m, v, name, rows=256):
    r, c = w.shape
    tr = _pick(r, rows, 16)

    def body(g_ref, w_ref, m_ref, v_ref, go_ref, d_ref, mo_ref, vo_ref):
        _adamw_update(g_ref[...], w_ref, m_ref, v_ref, go_ref, d_ref, mo_ref, vo_ref)

    spec = pl.BlockSpec((tr, c), lambda i: (i, 0))
    return pl.pallas_call(
        body,
        name=name,
        grid=(r // tr,),
        in_specs=[spec, spec, spec, spec],
        out_specs=[spec] * 4,
        out_shape=[jax.ShapeDtypeStruct((r, c), F32)] * 4,
        compiler_params=_cparams(dimension_semantics=("parallel",)),
    )(g, w, m, v)


def _sum_parts(parts, name):
    def body(p_ref, o_ref):
        acc = p_ref[0]
        for d in range(1, N_DEV):
            acc = acc + p_ref[d]
        o_ref[...] = acc

    return pl.pallas_call(
        body, name=name, out_shape=jax.ShapeDtypeStruct(parts.shape[1:], F32), compiler_params=_cparams()
    )(parts)


def _place():
    x, y, c = lax.axis_index("x"), lax.axis_index("y"), lax.axis_index("c")
    return x, y, c


def _dev_index(p):
    return 4 * p[0] + 2 * p[1] + p[2]


def _allgather_vmem(shard, name):
    m_per, n = shard.shape

    def body(x_ref, out_ref, send_sems, recv_sems, local_sem):
        x, y, c = _place()
        me, sibling = (x, y, c), (x, y, 1 - c)
        chips = [(1 - x, y), (x, 1 - y), (1 - x, 1 - y)]

        def rows(p):
            return out_ref.at[pl.ds(_dev_index(p) * m_per, m_per), :]

        def copy(k, block, to, src=None):
            return pltpu.make_async_remote_copy(
                src_ref=rows(block) if src is None else src, dst_ref=rows(block),
                send_sem=send_sems.at[k], recv_sem=recv_sems.at[k], device_id=to, device_id_type=MESH)

        mine = pltpu.make_async_copy(x_ref, rows(me), local_sem)
        mine.start()
        first = [copy(0, me, sibling, src=x_ref)]
        first += [copy(1 + j, me, (*chip, c), src=x_ref) for j, chip in enumerate(chips)]
        for cp in first:
            cp.start()
        passed = [copy(4 + j, (*chip, c), sibling) for j, chip in enumerate(chips)]
        for j, chip in enumerate(chips):
            copy(1 + j, (*chip, c), me).wait_recv()
            passed[j].start()
        copy(0, sibling, me).wait_recv()
        for j, chip in enumerate(chips):
            copy(4 + j, (*chip, 1 - c), me).wait_recv()
        for cp in first + passed:
            cp.wait_send()
        mine.wait()

    return pl.pallas_call(
        body,
        name=name,
        out_shape=jax.ShapeDtypeStruct((N_DEV * m_per, n), shard.dtype),
        in_specs=[pl.BlockSpec(memory_space=pltpu.VMEM)],
        out_specs=pl.BlockSpec(memory_space=pltpu.VMEM),
        scratch_shapes=[pltpu.SemaphoreType.DMA((7,)), pltpu.SemaphoreType.DMA((7,)), pltpu.SemaphoreType.DMA],
        compiler_params=_cparams(),
    )(shard)


def _allgather_hbm(shards, name):
    n = len(shards)

    def body(*refs):
        ins, outs = refs[:n], refs[n : 2 * n]
        send_sems, recv_sems, local_sems = refs[2 * n :]
        x, y, c = _place()
        me, sibling = (x, y, c), (x, y, 1 - c)
        chips = [(1 - x, y), (x, 1 - y), (1 - x, 1 - y)]

        def copy(a, k, block, to, src=None):
            dst = outs[a].at[_dev_index(block)]
            return pltpu.make_async_remote_copy(
                src_ref=dst if src is None else src, dst_ref=dst,
                send_sem=send_sems.at[a * 7 + k], recv_sem=recv_sems.at[a * 7 + k], device_id=to, device_id_type=MESH)

        mine = [pltpu.make_async_copy(ins[a], outs[a].at[_dev_index(me)], local_sems.at[a]) for a in range(n)]
        for cp in mine:
            cp.start()
        first = []
        for a in range(n):
            first.append(copy(a, 0, me, sibling, src=ins[a]))
            first += [copy(a, 1 + j, me, (*chip, c), src=ins[a]) for j, chip in enumerate(chips)]
        for cp in first:
            cp.start()
        passed = []
        for j, chip in enumerate(chips):
            for a in range(n):
                copy(a, 1 + j, (*chip, c), me).wait_recv()
                cp = copy(a, 4 + j, (*chip, c), sibling)
                cp.start()
                passed.append(cp)
        for a in range(n):
            copy(a, 0, sibling, me).wait_recv()
        for j, chip in enumerate(chips):
            for a in range(n):
                copy(a, 4 + j, (*chip, 1 - c), me).wait_recv()
        for cp in first + passed:
            cp.wait_send()
        for cp in mine:
            cp.wait()

    any_spec = pl.BlockSpec(memory_space=pl.ANY)
    return pl.pallas_call(
        body,
        name=name,
        out_shape=[jax.ShapeDtypeStruct((N_DEV, *s.shape), s.dtype) for s in shards],
        in_specs=[any_spec] * n,
        out_specs=[any_spec] * n,
        scratch_shapes=[pltpu.SemaphoreType.DMA((7 * n,)), pltpu.SemaphoreType.DMA((7 * n,)),
                        pltpu.SemaphoreType.DMA((n,))],
        compiler_params=_cparams(),
    )(*shards)


def _exchange_hbm(bufs, name):
    n = len(bufs)

    def body(*refs):
        ins, outs = refs[:n], refs[n : 2 * n]
        send_sems, recv_sems, local_sems = refs[2 * n :]
        x, y, c = _place()
        me = _dev_index((x, y, c))
        mine = [pltpu.make_async_copy(ins[a].at[me], outs[a].at[me], local_sems.at[a]) for a in range(n)]
        for cp in mine:
            cp.start()
        def peer_of(k):
            return (1 - x if k & 4 else x, 1 - y if k & 2 else y, 1 - c if k & 1 else c)

        copies = []
        for k in range(1, N_DEV):
            peer = peer_of(k)
            for a in range(n):
                copies.append(pltpu.make_async_remote_copy(
                    src_ref=ins[a].at[_dev_index(peer)], dst_ref=outs[a].at[me],
                    send_sem=send_sems.at[a * 7 + k - 1], recv_sem=recv_sems.at[a * 7 + k - 1],
                    device_id=peer, device_id_type=MESH))
        for cp in copies:
            cp.start()
        for k in range(1, N_DEV):
            peer = peer_of(k)
            for a in range(n):
                pltpu.make_async_remote_copy(
                    src_ref=ins[a].at[me], dst_ref=outs[a].at[_dev_index(peer)],
                    send_sem=send_sems.at[a * 7 + k - 1], recv_sem=recv_sems.at[a * 7 + k - 1],
                    device_id=peer, device_id_type=MESH).wait_recv()
        for cp in copies:
            cp.wait_send()
        for cp in mine:
            cp.wait()

    any_spec = pl.BlockSpec(memory_space=pl.ANY)
    return pl.pallas_call(
        body,
        name=name,
        out_shape=[jax.ShapeDtypeStruct(b.shape, b.dtype) for b in bufs],
        in_specs=[any_spec] * n,
        out_specs=[any_spec] * n,
        scratch_shapes=[pltpu.SemaphoreType.DMA((7 * n,)), pltpu.SemaphoreType.DMA((7 * n,)),
                        pltpu.SemaphoreType.DMA((n,))],
        compiler_params=_cparams(),
    )(*bufs)


HBM_SPEC = pl.BlockSpec(memory_space=pltpu.HBM)
SEM_SPEC = pl.BlockSpec(memory_space=pltpu.SEMAPHORE)
EFFECT = pltpu.SideEffectType.DATAFLOW_SIDE_EFFECTING


def _peers(x, y, c):
    return [(1 - x if k & 4 else x, 1 - y if k & 2 else y, 1 - c if k & 1 else c) for k in range(1, N_DEV)]


def _push_peers(mode, x, y, c):
    if mode == "all":
        return _peers(x, y, c)
    return [(x, y, 1 - c), (1 - x, y, c), (x, 1 - y, c), (1 - x, 1 - y, c)]


def _push_start(groups, sliced, name, after=(), modes=None):
    flat = [b for g in groups for b in g]
    n, ng = len(flat), len(groups)
    sizes = [len(g) for g in groups]
    modes = modes or ["all"] * ng
    fan = [len(_push_peers(m, 0, 0, 0)) for m in modes]
    lands = [lax.empty(b.shape if sliced else (N_DEV, *b.shape), b.dtype) for b in flat]

    def body(*refs):
        ins, lnd = refs[:n], refs[n : 2 * n]
        sems = refs[2 * n + len(after) : 2 * n + len(after) + 2 * ng]
        token = refs[-1]
        x, y, c = _place()
        me = _dev_index((x, y, c))
        first = 0
        for gi, size in enumerate(sizes):
            for k, peer in enumerate(_push_peers(modes[gi], x, y, c)):
                for j in range(first, first + size):
                    sem = (j - first) * fan[gi] + k
                    pltpu.make_async_remote_copy(
                        src_ref=ins[j].at[_dev_index(peer)] if sliced else ins[j], dst_ref=lnd[j].at[me],
                        send_sem=sems[2 * gi].at[sem], recv_sem=sems[2 * gi + 1].at[sem],
                        device_id=peer, device_id_type=MESH).start()
            first += size
        token[...] = jnp.zeros_like(token)

    out_shape = []
    for size, width in zip(sizes, fan):
        out_shape += [pltpu.SemaphoreType.DMA((width * size,)), pltpu.SemaphoreType.DMA((width * size,))]
    out_shape += [pltpu.HBM(b.shape, b.dtype) for b in flat + lands]
    out_shape.append(jax.ShapeDtypeStruct((V7X_SUBLANES, V7X_LANES), F32))
    res = pl.pallas_call(
        body,
        name=name,
        out_shape=tuple(out_shape),
        in_specs=[HBM_SPEC] * (2 * n) + [ANY_SPEC] * len(after),
        out_specs=tuple([SEM_SPEC] * (2 * ng) + [HBM_SPEC] * (2 * n) + [pl.BlockSpec(memory_space=pltpu.VMEM)]),
        input_output_aliases={i: 2 * ng + i for i in range(2 * n)},
        compiler_params=pltpu.CompilerParams(has_side_effects=EFFECT),
    )(*[pltpu.with_memory_space_constraint(b, pltpu.HBM) for b in flat + lands], *after)
    sems, thru, token = res[: 2 * ng], res[2 * ng : 2 * ng + 2 * n], res[-1]
    out, first = [], 0
    for gi, size in enumerate(sizes):
        out.append((sems[2 * gi], sems[2 * gi + 1], list(thru[first : first + size]),
                    list(thru[n + first : n + first + size])))
        first += size
    return out, token


def _push_wait(started, sliced, after, name, mode="all"):
    send_sems, recv_sems, bufs, lands = started
    n = len(bufs)
    fan = len(_push_peers(mode, 0, 0, 0))

    def body(*refs):
        ins, lnd = refs[:n], refs[n : 2 * n]
        send_ref, recv_ref = refs[2 * n], refs[2 * n + 1]
        x, y, c = _place()
        for k, peer in enumerate(_push_peers(mode, x, y, c)):
            for j in range(n):
                cp = pltpu.make_async_remote_copy(
                    src_ref=ins[j].at[_dev_index(peer)] if sliced else ins[j], dst_ref=lnd[j].at[_dev_index(peer)],
                    send_sem=send_ref.at[j * fan + k], recv_sem=recv_ref.at[j * fan + k],
                    device_id=peer, device_id_type=MESH)
                cp.wait_send()
                cp.wait_recv()

    res = pl.pallas_call(
        body,
        name=name,
        out_shape=tuple(pltpu.HBM(b.shape, b.dtype) for b in bufs + lands),
        in_specs=[HBM_SPEC] * (2 * n) + [SEM_SPEC, SEM_SPEC, pl.BlockSpec(memory_space=pl.ANY)],
        out_specs=tuple([HBM_SPEC] * (2 * n)),
        input_output_aliases={i: i for i in range(2 * n)},
        compiler_params=pltpu.CompilerParams(has_side_effects=EFFECT),
    )(*bufs, *lands, send_sems, recv_sems, after)
    return list(res[:n]), list(res[n:])


def _forward_copies(lnd, send_ref, recv_ref, incoming):
    x, y, c = _place()
    copies = []
    for k, chip in enumerate([(1 - x, y), (x, 1 - y), (1 - x, 1 - y)]):
        mine, theirs = _dev_index((*chip, c)), _dev_index((*chip, 1 - c))
        for j, ref in enumerate(lnd):
            copies.append(pltpu.make_async_remote_copy(
                src_ref=ref.at[mine], dst_ref=ref.at[theirs if incoming else mine],
                send_sem=send_ref.at[j * 3 + k], recv_sem=recv_ref.at[j * 3 + k],
                device_id=(x, y, 1 - c), device_id_type=MESH))
    return copies


def _forward_start(lands, name):
    n = len(lands)

    def body(*refs):
        for cp in _forward_copies(refs[:n], refs[n], refs[n + 1], False):
            cp.start()

    res = pl.pallas_call(
        body,
        name=name,
        out_shape=(pltpu.SemaphoreType.DMA((3 * n,)), pltpu.SemaphoreType.DMA((3 * n,)),
                   *[pltpu.HBM(b.shape, b.dtype) for b in lands]),
        in_specs=[HBM_SPEC] * n,
        out_specs=(SEM_SPEC, SEM_SPEC, *[HBM_SPEC] * n),
        input_output_aliases={i: 2 + i for i in range(n)},
        compiler_params=pltpu.CompilerParams(has_side_effects=EFFECT),
    )(*[pltpu.with_memory_space_constraint(b, pltpu.HBM) for b in lands])
    return res[0], res[1], list(res[2:])


def _forward_wait(started, after, name):
    send_sems, recv_sems, lands = started
    n = len(lands)

    def body(*refs):
        for cp in _forward_copies(refs[:n], refs[n], refs[n + 1], True):
            cp.wait_send()
            cp.wait_recv()

    res = pl.pallas_call(
        body,
        name=name,
        out_shape=tuple(pltpu.HBM(b.shape, b.dtype) for b in lands),
        in_specs=[HBM_SPEC] * n + [SEM_SPEC, SEM_SPEC, pl.BlockSpec(memory_space=pl.ANY)],
        out_specs=tuple([HBM_SPEC] * n),
        input_output_aliases={i: i for i in range(n)},
        compiler_params=pltpu.CompilerParams(has_side_effects=EFFECT),
    )(*lands, send_sems, recv_sems, after)
    return list(res)


def _cols_full(g):
    return jnp.transpose(g, (1, 0, 2)).reshape(g.shape[1], -1)


def _rows_full(g):
    return g.reshape(-1, g.shape[2])


def _cols_parts(full, n=N_DEV):
    r = full.shape[0]
    return jnp.transpose(full.reshape(r, n, -1), (1, 0, 2)).astype(BF16)


def _rows_parts(full):
    return full.reshape(N_DEV, -1, full.shape[1]).astype(BF16)


def _block_diag(w):
    eye = jnp.eye(LRU_BLOCKS, dtype=w.dtype)
    return jnp.einsum("nkj,nm->nkmj", w, eye).reshape(LRU_WIDTH, LRU_WIDTH)


def _pad_rows(v, rows):
    flat = v.reshape(-1)
    return jnp.pad(flat, (0, rows * D_MODEL - flat.shape[0])).reshape(rows, D_MODEL)


def _my_cols(full, me, width):
    return lax.dynamic_slice_in_dim(full, me * width, width, axis=full.ndim - 1)


def kernel(x, c, w_ada, b_ada, norm_pre, norm_post, ffn1_w_gu, ffn1_w_down, w_in, rel_bias, conv_w, conv_b, lru_wa, lru_ba, lru_wx, lru_bx, lru_lambda, w_att_o, w_rec_o, w_out, ffn2_w_gu, ffn2_w_down, loss_target, m_w_ada, m_b_ada, m_norm_pre, m_norm_post, m_ffn1_w_gu, m_ffn1_w_down, m_w_in, m_rel_bias, m_conv_w, m_conv_b, m_lru_wa, m_lru_ba, m_lru_wx, m_lru_bx, m_lru_lambda, m_w_att_o, m_w_rec_o, m_w_out, m_ffn2_w_gu, m_ffn2_w_down, v_w_ada, v_b_ada, v_norm_pre, v_norm_post, v_ffn1_w_gu, v_ffn1_w_down, v_w_in, v_rel_bias, v_conv_w, v_conv_b, v_lru_wa, v_lru_ba, v_lru_wx, v_lru_bx, v_lru_lambda, v_w_att_o, v_w_rec_o, v_w_out, v_ffn2_w_gu, v_ffn2_w_down):
    weights = dict(w_ada=w_ada, b_ada=b_ada, norm_pre=norm_pre, norm_post=norm_post, ffn1_w_gu=ffn1_w_gu,
                   ffn1_w_down=ffn1_w_down, w_in=w_in, rel_bias=rel_bias, conv_w=conv_w, conv_b=conv_b,
                   lru_wa=lru_wa, lru_ba=lru_ba, lru_wx=lru_wx, lru_bx=lru_bx, lru_lambda=lru_lambda,
                   w_att_o=w_att_o, w_rec_o=w_rec_o, w_out=w_out, ffn2_w_gu=ffn2_w_gu, ffn2_w_down=ffn2_w_down)
    mom1 = dict(w_ada=m_w_ada, b_ada=m_b_ada, norm_pre=m_norm_pre, norm_post=m_norm_post, ffn1_w_gu=m_ffn1_w_gu,
                ffn1_w_down=m_ffn1_w_down, w_in=m_w_in, rel_bias=m_rel_bias, conv_w=m_conv_w, conv_b=m_conv_b,
                lru_wa=m_lru_wa, lru_ba=m_lru_ba, lru_wx=m_lru_wx, lru_bx=m_lru_bx, lru_lambda=m_lru_lambda,
                w_att_o=m_w_att_o, w_rec_o=m_w_rec_o, w_out=m_w_out, ffn2_w_gu=m_ffn2_w_gu, ffn2_w_down=m_ffn2_w_down)
    mom2 = dict(w_ada=v_w_ada, b_ada=v_b_ada, norm_pre=v_norm_pre, norm_post=v_norm_post, ffn1_w_gu=v_ffn1_w_gu,
                ffn1_w_down=v_ffn1_w_down, w_in=v_w_in, rel_bias=v_rel_bias, conv_w=v_conv_w, conv_b=v_conv_b,
                lru_wa=v_lru_wa, lru_ba=v_lru_ba, lru_wx=v_lru_wx, lru_bx=v_lru_bx, lru_lambda=v_lru_lambda,
                w_att_o=v_w_att_o, w_rec_o=v_w_rec_o, w_out=v_w_out, ffn2_w_gu=v_ffn2_w_gu, ffn2_w_down=v_ffn2_w_down)
    order = list(weights)
    big = ["ffn1_w_gu", "ffn1_w_down", "w_in", "w_att_o", "w_rec_o", "w_out", "ffn2_w_gu", "ffn2_w_down"]
    col_sharded = {"ffn1_w_gu", "w_in", "w_att_o", "ffn2_w_gu"}
    small = ["b_ada", "norm_pre", "norm_post", "rel_bias", "conv_w", "conv_b", "lru_wa", "lru_ba", "lru_wx",
             "lru_bx", "lru_lambda"]

    xi, yi, ci = _place()
    me = _dev_index((xi, yi, ci))
    x0 = x[0]
    target = loss_target[0]
    fuse_tm = min(FUSE_TM, x0.shape[0])

    transposed = {"ffn1_w_gu", "w_in", "ffn2_w_gu"}
    local = lambda n, arr: jnp.transpose(arr[0]) if n in transposed else arr[0]
    shards = {n: local(n, weights[n]).astype(BF16) for n in big}
    full_of = lambda n, g: _cols_full(g) if n == "w_att_o" else _rows_full(g)

    pack = jnp.concatenate([c.reshape(-1), norm_pre.reshape(-1), norm_post.reshape(-1), conv_w.reshape(-1)])
    pack = jnp.pad(pack, (0, 3072 - pack.shape[0])).reshape(8, 384)
    got = _allgather_vmem(pack, "gather_small_inputs").reshape(N_DEV, 3072)
    c_all = got[:, :1024]
    unshard = lambda blk, rows: jnp.transpose(blk.reshape(N_DEV, rows, 128), (1, 0, 2)).reshape(rows, D_MODEL)
    g_pre = unshard(got[:, 1024:1408], 3)
    g_post = unshard(got[:, 1408:1792], 3)
    conv_taps = unshard(got[:, 1792:2304], 4)
    conv_w8 = jnp.concatenate([conv_taps, jnp.zeros((4, LRU_WIDTH), F32)], axis=0)

    mod_cols = _ada_fwd(c_all, w_ada[0], "ada_fwd")
    mod_all = _allgather_vmem(mod_cols, "gather_mod").reshape(N_DEV, N_DEV, 1152)
    mod = lax.dynamic_index_in_dim(mod_all, me, axis=1, keepdims=False).reshape(1, -1) + b_ada
    mod = mod.reshape(3, 3, 1, D_MODEL)

    w_slab = _slab_weights(lru_wa[0], lru_wx[0])
    bias = _bias_tile(rel_bias[0], "bias_tile")

    res_w = (0.5, 1.0, 0.5)
    row = lambda v: v.reshape(1, -1)

    (w1_gu,) = _allgather_hbm([shards["ffn1_w_gu"]], "gather_ffn1_w_gu")
    weight_groups = [["ffn1_w_down"], ["w_in"], ["w_att_o", "w_rec_o", "w_out"], ["ffn2_w_gu", "ffn2_w_down"]]
    weight_modes = ["all", "chip", "all", "all"]
    weights_started, started = _push_start([[shards[n] for n in g] for g in weight_groups], False,
                                           "gather_weights_start", after=(mod, w1_gu), modes=weight_modes)
    full = {"ffn1_w_gu": _rows_full(w1_gu)}

    def gathered_group(gi, after):
        sent, lands = _push_wait(weights_started[gi], False, after, f"gather_weights_wait{gi}", mode=weight_modes[gi])
        if weight_modes[gi] == "chip":
            lands = _forward_wait(_forward_start(lands, f"gather_weights_forward{gi}"), sent[0],
                                  f"gather_weights_forward_wait{gi}")
        for n, own, land in zip(weight_groups[gi], sent, lands):
            full[n] = full_of(n, jnp.where(is_me, own[None], land))

    is_me = (jnp.arange(N_DEV) == me)[:, None, None]

    def ffn_fwd(xin, k, gi, tag, deps=(), target=None):
        h, a, g, u = _pre_up(xin, row(g_pre[k]), mod[k, 0], mod[k, 1], full[f"{tag}_w_gu"], f"{tag}_up", deps=deps)
        if f"{tag}_w_down" not in full:
            gathered_group(gi, a)
        f, *out = _matmul_post(a, full[f"{tag}_w_down"], xin, row(g_post[k]), mod[k, 2], res_w[k], f"{tag}_down",
                               target=target)
        return (out[0] if target is None else out), (h, g, u, a, f)

    x1, saved1 = ffn_fwd(x0, 0, 0, "ffn1", deps=(started,))

    gathered_group(1, x1)
    h2, proj = _pre_matmul(x1, row(g_pre[1]), mod[1, 0], mod[1, 1], full["w_in"], "mix_in",
                           b_shift=3 * ATT_WIDTH // 512)
    att_o = _attn_fwd(proj, bias, "attn_fwd")
    gathered_group(2, att_o)
    xc, pre, a_t, u_t = _lru_front(proj, conv_w8, conv_b, w_slab, lru_ba, lru_bx, lru_lambda, "lru_front")
    hs, h_prev, rec_in = _scan_fwd(a_t, u_t, proj, "lru_scan")
    att = _matmul(att_o, full["w_att_o"], "nn", F32, "att_out")
    rec = _matmul(rec_in, full["w_rec_o"], "nn", F32, "rec_out")
    merged, f2, x2 = _merge_matmul_post(att, rec, proj, full["w_out"], x1, row(g_post[1]), mod[1, 2], res_w[1],
                                        "mix_out")

    gathered_group(3, x2)
    (dy, sq), saved3 = ffn_fwd(x2, 2, 2, "ffn2", target=target)
    loss = lax.psum(0.5 * jnp.sum(sq) / D_MODEL, ("x", "y", "c"))

    grads = {}
    norm_sums = [None] * 6

    pending = []

    def exchange_start(names, tag, after=()):
        send = [(_cols_parts if n == "w_att_o" else _rows_parts)(grads[n]) for n in names]
        (group,), token = _push_start([send], True, f"exchange_{tag}_start", after=after)
        pending.append((names, send, group, tag))
        return token

    def exchange_finish(names, send, group, tag, after):
        sent, lands = _push_wait(group, True, after, f"exchange_{tag}_wait")
        res = None
        for n, land, mine in zip(names, lands, sent):
            res = _adamw_parts(land, mine, me, local(n, weights[n]), local(n, mom1[n]), local(n, mom2[n]),
                               f"adamw_{n}")
            back = (lambda r: jnp.transpose(r)) if n in transposed else (lambda r: r)
            out_g[n], out_d[n], out_m[n], out_v[n] = [back(r).reshape(weights[n].shape) for r in res]
        return res[0]

    out_g, out_d, out_m, out_v = {}, {}, {}, {}

    def ffn_bwd(xin, k, saved, dout, tag):
        h, g, u, a, f = saved
        w_gu, w_down = f"{tag}_w_gu", f"{tag}_w_down"
        df, dgu, norm_sums[2 * k + 1] = _post_bwd_up_bwd(f, dout, row(g_post[k]), mod[k, 2], res_w[k], full[w_down],
                                                          g, u, f"{tag}_up_bwd")
        grads[w_down] = _matmul(a, df, "tn", BF16, f"{tag}_dw_down", tm=1408, tn=1024, tk=DW_TK)
        started = exchange_start([w_down], w_down)
        grads[w_gu] = _dw_gu(dgu, h, f"{tag}_dw_gu", deps=(started,))
        started = exchange_start([w_gu], w_gu)
        halves = [(dgu, (None, fuse_tm, D_FF), lambda i, half=half: (half, i, 0), (half * D_FF, (half + 1) * D_FF))
                  for half in range(2)]
        dx, norm_sums[2 * k] = _matmul_pre_bwd(halves, full[w_gu], xin, dout, row(g_pre[k]),
                                                                mod[k, 0], mod[k, 1], f"{tag}_dh", deps=(started,))
        return dx

    dx2 = ffn_bwd(x2, 2, saved3, dy, "ffn2")

    df2, datt, drec, dg_att, dg_rec, norm_sums[3] = _post_bwd_merge_bwd(
        f2, dx2, row(g_post[1]), mod[1, 2], res_w[1], full["w_out"], att, rec, proj, "mix_dmerged")
    grads["w_out"] = _matmul(merged, df2, "tn", BF16, "mix_dw_out", tm=1024, tn=1024, tk=DW_TK)
    datt_o = _matmul(datt, full["w_att_o"], "nt", BF16, "att_out_bwd")
    grads["w_att_o"] = _matmul(att_o, datt, "tn", BF16, "dw_att_o", tm=512, tn=1024, tk=DW_TK)
    grads["w_rec_o"] = _matmul(rec_in, drec, "tn", BF16, "dw_rec_o", tm=1024, tn=1024, tk=DW_TK)
    started = exchange_start(["w_out", "w_att_o", "w_rec_o"], "mix_out")
    dhs, dyr = _matmul_recin_bwd(drec, full["w_rec_o"], hs, proj, "rec_out_bwd", deps=(started,))
    g_t = _scan_bwd(a_t, dhs, "lru_scan_bwd")
    dpre, dxc, lru_sums = _lru_back(pre, xc, w_slab, lru_ba, lru_bx, lru_lambda, g_t, h_prev, "lru_back")
    dxr, conv_sums = _conv_bwd(proj, conv_w8, dxc, "conv_bwd")
    dq, dk, dv, dbias = _attn_bwd(proj, bias, datt_o, "attn_bwd")
    dproj = jnp.concatenate([dq, dk, dv, dxr, dyr, dg_att, dg_rec], axis=1)
    grads["w_in"] = _matmul(dproj, h2, "tn", BF16, "mix_dw_in", tm=1408, tn=1024, tk=DW_TK)
    pack_mix = jnp.concatenate([conv_sums, lru_sums, _pad_rows(_bias_grad(dbias, "bias_grad"), V7X_SUBLANES),
                                _lru_dw(xc, dpre, "lru_dw").reshape(128, D_MODEL)], axis=0)
    (mix_started,), started = _push_start([[pack_mix]], False, "small_grads_mix_start")
    started = exchange_start(["w_in"], "w_in", after=(started,))
    whole = [(dproj, (fuse_tm, PROJ_WIDTH), lambda i: (i, 0), (0, PROJ_WIDTH))]
    dx1, norm_sums[2] = _matmul_pre_bwd(whole, full["w_in"], x1, dx2, row(g_pre[1]), mod[1, 0],
                                                             mod[1, 1], "mix_dh", deps=(started,))

    dx0 = ffn_bwd(x0, 0, saved1, dx1, "ffn1")

    pack_norm = jnp.concatenate(norm_sums, axis=0)
    (norm_started,), _ = _push_start([[pack_norm]], False, "small_grads_norm_start")

    def summed(started, pack, after, tag):
        (own,), (land,) = _push_wait(started, False, after, f"small_grads_{tag}_wait")
        parts = jnp.where(is_me, own[None], land)
        return parts, _sum_parts(parts, f"small_grads_{tag}_sum")

    done = dx0
    last = pending[-1:]
    for names, send, group, tag in pending[:-1]:
        done = exchange_finish(names, send, group, tag, done)

    _, total = summed(mix_started, pack_mix, done, "mix")
    grads["conv_w"] = _my_cols(total[0:4], me, 128)
    grads["conv_b"] = total[4:5]
    grads["lru_ba"] = total[8:9]
    grads["lru_bx"] = total[9:10]
    grads["lru_lambda"] = total[10:11]
    grads["rel_bias"] = total[16:19].reshape(-1)[: ATT_HEADS * (2 * MAX_REL + 1)].reshape(ATT_HEADS, -1)
    grads["lru_wa"] = total[24:88].reshape(LRU_BLOCKS, LRU_BLOCK, LRU_BLOCK)
    grads["lru_wx"] = total[88:152].reshape(LRU_BLOCKS, LRU_BLOCK, LRU_BLOCK)
    parts, total = summed(norm_started, pack_norm, total, "norm")
    by_sandwich = lambda v: v.reshape(*v.shape[:-2], 3, 2 * V7X_SUBLANES, D_MODEL)
    dmod_of = lambda v: jnp.concatenate([by_sandwich(v)[..., 1:3, :], by_sandwich(v)[..., 9:10, :]], axis=-2)
    grads["b_ada"] = dmod_of(total).reshape(1, -1)
    grads["norm_pre"] = _my_cols(by_sandwich(total)[:, 0, :], me, 128)
    grads["norm_post"] = _my_cols(by_sandwich(total)[:, V7X_SUBLANES, :], me, 128)
    dmod_all = dmod_of(parts).reshape(N_DEV, 9 * D_MODEL)
    grads["w_ada"] = _ada_bwd(c_all, _my_cols(dmod_all, me, 1152), "ada_bwd")

    res = _adamw(grads["w_ada"], w_ada[0], m_w_ada[0], v_w_ada[0], "adamw_w_ada")
    out_g["w_ada"], out_d["w_ada"], out_m["w_ada"], out_v["w_ada"] = [r.reshape(w_ada.shape) for r in res]

    sizes = [int(np.prod(weights[n].shape)) for n in small]
    tot = sum(sizes)
    rows_small = -(-tot // (16 * D_MODEL)) * 16
    flat = lambda arrs: jnp.pad(jnp.concatenate([a.reshape(-1) for a in arrs]),
                                (0, rows_small * D_MODEL - tot)).reshape(rows_small, D_MODEL)
    res = _adamw(flat([grads[n] for n in small]), flat([weights[n] for n in small]),
                 flat([mom1[n] for n in small]), flat([mom2[n] for n in small]), "adamw_small", rows=rows_small)
    offs = np.cumsum([0] + sizes)
    for dst, r in zip((out_g, out_d, out_m, out_v), res):
        rf = r.reshape(-1)
        for i, n in enumerate(small):
            dst[n] = rf[offs[i] : offs[i + 1]].reshape(weights[n].shape)

    done = res[0]
    for names, send, group, tag in last:
        done = exchange_finish(names, send, group, tag, done)

    return (loss, dx0[None], *[out_g[n] for n in order], *[out_d[n] for n in order],
            *[out_m[n] for n in order], *[out_v[n] for n in order])
```

```python
import functools

import jax
import jax.numpy as jnp
import numpy as np
from jax import lax
from jax.experimental import pallas as pl
from jax.experimental.pallas import tpu as pltpu

D_MODEL = 1024
D_FF = 2816
ATT_HEADS = 8
ATT_HEAD_DIM = 64
ATT_WIDTH = 512
CHUNK = 64
LEFT_CHUNKS = 8
MAX_REL = 128
LRU_WIDTH = 1024
LRU_BLOCKS = 16
LRU_BLOCK = 64
LRU_C = 8.0
EPS = 1e-6
PROJ_WIDTH = 5632
N_DEV = 8

ADAM_LR = 0.001
ADAM_B1 = 0.9
ADAM_B2 = 0.999
ADAM_EPS = 1e-08
ADAM_WD = 0.01
ADAM_STEP = 10

V7X_LANES = 128
V7X_SUBLANES = 8
V7X_VMEM_BYTES = 64 * 1024 * 1024
VMEM_LIMIT = V7X_VMEM_BYTES - 8 * 1024 * 1024

ATT_TQ = 256
NEG = -1e30
BF16 = jnp.bfloat16
F32 = jnp.float32
MESH = pl.DeviceIdType.MESH

OFF_Q = 4 * LRU_WIDTH
OFF_K = OFF_Q + ATT_WIDTH
OFF_V = OFF_K + ATT_WIDTH


def _cparams(**kw):
    return pltpu.CompilerParams(vmem_limit_bytes=VMEM_LIMIT, **kw)


def _pick(n, target, unit=V7X_LANES):
    best = None
    for t in range(unit, min(n, target) + 1, unit):
        if n % t == 0:
            best = t
    return n if best is None else best


_DIMS = {
    "nn": (((1,), (0,)), ((), ())),
    "nt": (((1,), (1,)), ((), ())),
    "tn": (((0,), (0,)), ((), ())),
}


ANY_SPEC = pl.BlockSpec(memory_space=pl.ANY)


def _matmul(a, b, mode, out_dtype, name, tm=1024, tn=512, tk=1408, deps=(), b_shift=0):
    n_deps = len(deps)
    if mode == "nn":
        (m, k), (k2, n) = a.shape, b.shape
    elif mode == "nt":
        (m, k), (n, k2) = a.shape, b.shape
    else:
        (k, m), (k2, n) = a.shape, b.shape
    assert k == k2, (a.shape, b.shape, mode)
    tm, tn, tk = _pick(m, tm), _pick(n, tn), _pick(k, tk)
    nk = k // tk
    dims = _DIMS[mode]

    def body(a_ref, b_ref, *rest):
        o_ref, scratch = rest[n_deps], rest[n_deps + 1 :]
        p = lax.dot_general(a_ref[...], b_ref[...], dims, preferred_element_type=F32)
        if nk == 1:
            o_ref[...] = p.astype(o_ref.dtype)
        else:
            acc = scratch[0]
            kk = pl.program_id(2)

            @pl.when(kk == 0)
            def _():
                acc[...] = p

            @pl.when(kk > 0)
            def _():
                acc[...] += p

            @pl.when(kk == nk - 1)
            def _():
                o_ref[...] = acc[...].astype(o_ref.dtype)

    if mode == "nn":
        a_spec = pl.BlockSpec((tm, tk), lambda i, j, kk: (i, kk))
        b_spec = pl.BlockSpec((tk, tn), lambda i, j, kk: (kk, j))
    elif mode == "nt":
        a_spec = pl.BlockSpec((tm, tk), lambda i, j, kk: (i, kk))
        b_spec = pl.BlockSpec((tn, tk), lambda i, j, kk: ((j + b_shift) % (n // tn), kk))
    else:
        a_spec = pl.BlockSpec((tk, tm), lambda i, j, kk: (kk, i))
        b_spec = pl.BlockSpec((tk, tn), lambda i, j, kk: (kk, j))
    return pl.pallas_call(
        body,
        name=name,
        grid=(m // tm, n // tn, nk),
        in_specs=[a_spec, b_spec] + [ANY_SPEC] * n_deps,
        out_specs=pl.BlockSpec((tm, tn), lambda i, j, kk: (i, j)),
        out_shape=jax.ShapeDtypeStruct((m, n), out_dtype),
        scratch_shapes=[pltpu.VMEM((tm, tn), F32)] if nk > 1 else [],
        compiler_params=_cparams(dimension_semantics=("parallel", "parallel", "arbitrary")),
    )(a, b, *deps)


def _rowwise(fn, name, params, tiles, outs, accs=(), ts=256, with_index=False, deps=()):
    norm = []
    for t in tiles:
        if not isinstance(t, tuple):
            t = (t, t.shape[1], 0)
        norm.append(t if len(t) == 4 else (*t, None))
    s = norm[0][0].shape[0]
    ts = min(ts, s)
    assert s % ts == 0 and ts % V7X_SUBLANES == 0
    steps = s // ts
    halo_blocks = ts // V7X_SUBLANES
    n_p, n_t, n_o = len(params), len(norm), len(outs)

    def body(*refs):
        i = pl.program_id(0)
        vals = [r[...] for r in refs[: n_p + n_t]]
        res = fn(i, steps, *vals) if with_index else fn(*vals)
        if not isinstance(res, (tuple, list)):
            res = (res,)
        first_out = n_p + n_t + len(deps)
        o_refs = refs[first_out : first_out + n_o]
        a_refs = refs[first_out + n_o :]
        for r, v in zip(o_refs, res[:n_o]):
            r[...] = v.astype(r.dtype)
        for r, v in zip(a_refs, res[n_o:]):
            _accumulate(r, v, i)

    in_specs = [pl.BlockSpec(p.shape, lambda i: (0, 0)) for p in params]
    for arr, w, cb, halo in norm:
        if halo is None:
            in_specs.append(pl.BlockSpec((ts, w), lambda i, cb=cb: (i, cb)))
        elif halo == "prev":
            in_specs.append(
                pl.BlockSpec((V7X_SUBLANES, w), lambda i, cb=cb: (jnp.maximum(i * halo_blocks - 1, 0), cb))
            )
        else:
            last = s // V7X_SUBLANES - 1
            in_specs.append(
                pl.BlockSpec((V7X_SUBLANES, w), lambda i, cb=cb: (jnp.minimum((i + 1) * halo_blocks, last), cb))
            )
    in_specs += [ANY_SPEC] * len(deps)
    out_specs = [pl.BlockSpec((ts, w), lambda i: (i, 0)) for w, _ in outs]
    out_specs += [pl.BlockSpec(shape, lambda i: (0, 0)) for shape in accs]
    out_shape = [jax.ShapeDtypeStruct((s, w), dt) for w, dt in outs]
    out_shape += [jax.ShapeDtypeStruct(shape, F32) for shape in accs]
    res = pl.pallas_call(
        body,
        name=name,
        grid=(steps,),
        in_specs=in_specs,
        out_specs=out_specs,
        out_shape=out_shape,
        compiler_params=_cparams(dimension_semantics=("arbitrary",)),
    )(*params, *[t[0] for t in norm], *deps)
    return res


def _accumulate(ref, val, step):
    @pl.when(step == 0)
    def _():
        ref[...] = val

    @pl.when(step > 0)
    def _():
        ref[...] += val


def _sigmoid(z):
    return jax.nn.sigmoid(z)


def _silu(z):
    return z * _sigmoid(z)


def _gelu(z):
    return 0.5 * z * (1.0 + jnp.tanh(0.7978845608028654 * (z + 0.044715 * (z * z * z))))


def _pre_fn(g, shift, scale, x):
    r = lax.rsqrt(jnp.mean(x * x, axis=-1, keepdims=True) + EPS)
    return ((x * r) * g) * (1.0 + scale) + shift


def _post_fn(res_w, g, gate, f, x):
    r = lax.rsqrt(jnp.mean(f * f, axis=-1, keepdims=True) + EPS)
    return x + (res_w * gate) * ((f * r) * g)


def _swiglu_fn(gu):
    return _silu(gu[:, :D_FF]) * gu[:, D_FF:]


def _gates_fn(ba, bx, lam, pre, xc):
    ra = _sigmoid(pre[:, :LRU_WIDTH] + ba)
    ia = _sigmoid(pre[:, LRU_WIDTH:] + bx)
    softplus = jnp.maximum(-lam, 0.0) + jnp.log1p(jnp.exp(-jnp.abs(lam)))
    log_a = (-LRU_C) * ra * softplus
    a = jnp.exp(log_a)
    mult = jnp.sqrt(-jnp.tanh(log_a) * (a * a + 1.0))
    return a, mult * (ia * xc)


def _recin_fn(hs, yr):
    return hs * _gelu(yr)


def _merge_fn(att, rec, g_att, g_rec):
    return _sigmoid(g_att) * att + _sigmoid(g_rec) * rec


def _rowsum(v):
    return jnp.sum(v, axis=0, keepdims=True)


def _pre_fwd(x, g, shift, scale, name, deps=()):
    (h,) = _rowwise(_pre_fn, name, [g, shift, scale], [x], [(D_MODEL, BF16)], deps=deps)
    return h


def _pre_bwd(x, g, shift, scale, dh, dres, name):
    def fn(g, shift, scale, x, dh, dres):
        _, vjp = jax.vjp(_pre_fn, g, shift, scale, x)
        dg, dshift, dscale, dx = vjp(dh)
        return dx + dres, dg, dshift, dscale

    row = (1, D_MODEL)
    return _rowwise(fn, name, [g, shift, scale], [x, dh, dres], [(D_MODEL, F32)], [row, row, row])


def _post_fwd(f, x, g, gate, res_w, name):
    (y,) = _rowwise(functools.partial(_post_fn, res_w), name, [g, gate], [f, x], [(D_MODEL, F32)])
    return y


def _post_bwd(f, g, gate, res_w, dy, name, deps=()):
    def fn(g, gate, f, dy):
        _, vjp = jax.vjp(lambda g, gate, f: _post_fn(res_w, g, gate, f, 0.0), g, gate, f)
        dg, dgate, df = vjp(dy)
        return df, dg, dgate

    row = (1, D_MODEL)
    return _rowwise(fn, name, [g, gate], [f, dy], [(D_MODEL, BF16)], [row, row], deps=deps)


def _loss_stage(y, target, name):
    def fn(y, t):
        diff = y - t
        return diff * (1.0 / D_MODEL), _rowsum(diff * diff)

    return _rowwise(fn, name, [], [y, target], [(D_MODEL, F32)], [(1, D_MODEL)])


FFN_TM = 512
FFN_TF = 1408


def _glu_fn(g, u):
    return _silu(g) * u


def _ffn_up(h, w_gu_t, name):
    s = h.shape[0]
    tm = min(FFN_TM, s)
    nf = D_FF // FFN_TF

    def body(h_ref, wg_ref, wu_ref, a_ref, g_ref, u_ref):
        hv = h_ref[...]
        g = lax.dot_general(hv, wg_ref[...], _DIMS["nt"], preferred_element_type=F32)
        u = lax.dot_general(hv, wu_ref[...], _DIMS["nt"], preferred_element_type=F32)
        a_ref[...] = _glu_fn(g, u).astype(a_ref.dtype)
        g_ref[...] = g.astype(g_ref.dtype)
        u_ref[...] = u.astype(u_ref.dtype)

    out = pl.BlockSpec((tm, FFN_TF), lambda i, j: (i, j))
    return pl.pallas_call(
        body,
        name=name,
        grid=(s // tm, nf),
        in_specs=[pl.BlockSpec((tm, D_MODEL), lambda i, j: (i, 0)),
                  pl.BlockSpec((FFN_TF, D_MODEL), lambda i, j: (j, 0)),
                  pl.BlockSpec((FFN_TF, D_MODEL), lambda i, j: (nf + j, 0))],
        out_specs=[out, out, out],
        out_shape=[jax.ShapeDtypeStruct((s, D_FF), BF16)] * 3,
        compiler_params=_cparams(dimension_semantics=("parallel", "arbitrary")),
    )(h, w_gu_t, w_gu_t)


def _ffn_up_bwd(df, w_down, g, u, name, deps=()):
    s = df.shape[0]
    tm = min(FFN_TM, s)

    def body(df_ref, wd_ref, g_ref, u_ref, *rest):
        dg_ref, du_ref = rest[len(deps) :]
        da = lax.dot_general(df_ref[...], wd_ref[...], _DIMS["nt"], preferred_element_type=F32)
        _, vjp = jax.vjp(_glu_fn, g_ref[...].astype(F32), u_ref[...].astype(F32))
        dg, du = vjp(da)
        dg_ref[...] = dg.astype(dg_ref.dtype)
        du_ref[...] = du.astype(du_ref.dtype)

    blk = pl.BlockSpec((tm, FFN_TF), lambda i, j: (i, j))
    return pl.pallas_call(
        body,
        name=name,
        grid=(s // tm, D_FF // FFN_TF),
        in_specs=[pl.BlockSpec((tm, D_MODEL), lambda i, j: (i, 0)),
                  pl.BlockSpec((FFN_TF, D_MODEL), lambda i, j: (j, 0)), blk, blk] + [ANY_SPEC] * len(deps),
        out_specs=[blk, blk],
        out_shape=[jax.ShapeDtypeStruct((s, D_FF), BF16)] * 2,
        compiler_params=_cparams(dimension_semantics=("parallel", "arbitrary")),
    )(df, w_down, g, u, *deps)


def _ffn_dh(dg, du, w_gu_t, name, deps=()):
    s = dg.shape[0]
    tm, tn = min(FFN_TM, s), 512

    def body(dg_ref, du_ref, wg_ref, wu_ref, *rest):
        o_ref = rest[len(deps)]
        p = jnp.dot(dg_ref[...], wg_ref[...], preferred_element_type=F32)
        o_ref[...] = p + jnp.dot(du_ref[...], wu_ref[...], preferred_element_type=F32)

    a_spec = pl.BlockSpec((tm, D_FF), lambda i, j: (i, 0))
    return pl.pallas_call(
        body,
        name=name,
        grid=(s // tm, D_MODEL // tn),
        in_specs=[a_spec, a_spec,
                  pl.BlockSpec((D_FF, tn), lambda i, j: (0, j)),
                  pl.BlockSpec((D_FF, tn), lambda i, j: (1, j))] + [ANY_SPEC] * len(deps),
        out_specs=pl.BlockSpec((tm, tn), lambda i, j: (i, j)),
        out_shape=jax.ShapeDtypeStruct((s, D_MODEL), F32),
        compiler_params=_cparams(dimension_semantics=("parallel", "arbitrary")),
    )(dg, du, w_gu_t, w_gu_t, *deps)


FUSE_TM = 256
DW_TK = 2048
ROW_SPEC2 = pl.BlockSpec((1, D_MODEL), lambda i, j: (0, 0))
ROW_SPEC1 = pl.BlockSpec((1, D_MODEL), lambda i: (0, 0))
SUMS_SPEC1 = pl.BlockSpec((V7X_SUBLANES, D_MODEL), lambda i: (0, 0))
SUMS_SPEC2 = pl.BlockSpec((V7X_SUBLANES, D_MODEL), lambda i, j: (0, 0))
SUMS_SHAPE = jax.ShapeDtypeStruct((V7X_SUBLANES, D_MODEL), F32)


def _sum_rows(*rows):
    pad = jnp.zeros((V7X_SUBLANES - len(rows), rows[0].shape[1]), F32)
    return jnp.concatenate([*rows, pad], axis=0)


def _pre_up(x, g, shift, scale, w_gu_t, name, deps=()):
    s = x.shape[0]
    tm = min(FFN_TM, s)
    nf = D_FF // FFN_TF
    nd = len(deps)

    def body(x_ref, g_ref, sh_ref, sc_ref, wg_ref, wu_ref, *rest):
        h_ref, a_ref, gg_ref, u_ref, h_s = rest[nd:]

        @pl.when(pl.program_id(1) == 0)
        def _():
            h = _pre_fn(g_ref[...], sh_ref[...], sc_ref[...], x_ref[...]).astype(BF16)
            h_s[...] = h
            h_ref[...] = h

        hv = h_s[...]
        gv = lax.dot_general(hv, wg_ref[...], _DIMS["nt"], preferred_element_type=F32)
        uv = lax.dot_general(hv, wu_ref[...], _DIMS["nt"], preferred_element_type=F32)
        a_ref[...] = _glu_fn(gv, uv).astype(a_ref.dtype)
        gg_ref[...] = gv.astype(gg_ref.dtype)
        u_ref[...] = uv.astype(u_ref.dtype)

    rows = pl.BlockSpec((tm, D_MODEL), lambda i, j: (i, 0))
    out = pl.BlockSpec((tm, FFN_TF), lambda i, j: (i, j))
    return pl.pallas_call(
        body,
        name=name,
        grid=(s // tm, nf),
        in_specs=[rows, ROW_SPEC2, ROW_SPEC2, ROW_SPEC2,
                  pl.BlockSpec((FFN_TF, D_MODEL), lambda i, j: (j, 0)),
                  pl.BlockSpec((FFN_TF, D_MODEL), lambda i, j: (nf + j, 0))] + [ANY_SPEC] * nd,
        out_specs=[rows, out, out, out],
        out_shape=[jax.ShapeDtypeStruct((s, D_MODEL), BF16)] + [jax.ShapeDtypeStruct((s, D_FF), BF16)] * 3,
        scratch_shapes=[pltpu.VMEM((tm, D_MODEL), BF16)],
        compiler_params=_cparams(dimension_semantics=("parallel", "arbitrary")),
    )(x, g, shift, scale, w_gu_t, w_gu_t, *deps)


def _pre_matmul(x, g, shift, scale, w_t, name, b_shift=0, tn=512):
    s = x.shape[0]
    n = w_t.shape[0]
    tm = min(2 * FFN_TM, s)

    def body(x_ref, g_ref, sh_ref, sc_ref, w_ref, h_ref, o_ref, h_s):
        @pl.when(pl.program_id(1) == 0)
        def _():
            h = _pre_fn(g_ref[...], sh_ref[...], sc_ref[...], x_ref[...]).astype(BF16)
            h_s[...] = h
            h_ref[...] = h

        o_ref[...] = lax.dot_general(h_s[...], w_ref[...], _DIMS["nt"], preferred_element_type=F32)

    rows = pl.BlockSpec((tm, D_MODEL), lambda i, j: (i, 0))
    return pl.pallas_call(
        body,
        name=name,
        grid=(s // tm, n // tn),
        in_specs=[rows, ROW_SPEC2, ROW_SPEC2, ROW_SPEC2,
                  pl.BlockSpec((tn, D_MODEL), lambda i, j: ((j + b_shift) % (n // tn), 0))],
        out_specs=[rows, pl.BlockSpec((tm, tn), lambda i, j: (i, j))],
        out_shape=[jax.ShapeDtypeStruct((s, D_MODEL), BF16), jax.ShapeDtypeStruct((s, n), F32)],
        scratch_shapes=[pltpu.VMEM((tm, D_MODEL), BF16)],
        compiler_params=_cparams(dimension_semantics=("parallel", "arbitrary")),
    )(x, g, shift, scale, w_t)


def _matmul_post(a, w, x, g_post, gate, res_w, name, target=None):
    s, k = a.shape
    tm = min(FFN_TM, s)
    extra = [] if target is None else [target]

    def body(a_ref, w_ref, x_ref, g_ref, gate_ref, *rest):
        f = jnp.dot(a_ref[...], w_ref[...], preferred_element_type=F32)
        y = _post_fn(res_w, g_ref[...], gate_ref[...], f, x_ref[...])
        if target is None:
            f_ref, y_ref = rest
            y_ref[...] = y
        else:
            t_ref, f_ref, dy_ref, sq_ref = rest
            diff = y - t_ref[...]
            dy_ref[...] = diff * (1.0 / D_MODEL)
            _accumulate(sq_ref, _rowsum(diff * diff), pl.program_id(0))
        f_ref[...] = f

    rows = pl.BlockSpec((tm, D_MODEL), lambda i: (i, 0))
    out_specs, out_shape = [rows, rows], [jax.ShapeDtypeStruct((s, D_MODEL), F32)] * 2
    if target is not None:
        out_specs.append(ROW_SPEC1)
        out_shape.append(jax.ShapeDtypeStruct((1, D_MODEL), F32))
    return pl.pallas_call(
        body,
        name=name,
        grid=(s // tm,),
        in_specs=[pl.BlockSpec((tm, k), lambda i: (i, 0)), pl.BlockSpec((k, D_MODEL), lambda i: (0, 0)), rows,
                  ROW_SPEC1, ROW_SPEC1] + [rows] * len(extra),
        out_specs=out_specs,
        out_shape=out_shape,
        compiler_params=_cparams(dimension_semantics=("arbitrary",)),
    )(a, w, x, g_post, gate, *extra)


def _merge_matmul_post(att, rec, proj, w, x, g_post, gate, res_w, name):
    s = att.shape[0]
    tm = min(FUSE_TM, s)

    def body(att_ref, rec_ref, ga_ref, gr_ref, w_ref, x_ref, g_ref, gate_ref, m_ref, f_ref, y_ref):
        merged = _merge_fn(att_ref[...], rec_ref[...], ga_ref[...], gr_ref[...]).astype(BF16)
        m_ref[...] = merged
        f = jnp.dot(merged, w_ref[...], preferred_element_type=F32)
        f_ref[...] = f
        y_ref[...] = _post_fn(res_w, g_ref[...], gate_ref[...], f, x_ref[...])

    rows = pl.BlockSpec((tm, D_MODEL), lambda i: (i, 0))
    return pl.pallas_call(
        body,
        name=name,
        grid=(s // tm,),
        in_specs=[rows, rows, pl.BlockSpec((tm, D_MODEL), lambda i: (i, 2)), pl.BlockSpec((tm, D_MODEL), lambda i: (i, 3)),
                  pl.BlockSpec(w.shape, lambda i: (0, 0)), rows, ROW_SPEC1, ROW_SPEC1],
        out_specs=[rows, rows, rows],
        out_shape=[jax.ShapeDtypeStruct((s, D_MODEL), BF16)] + [jax.ShapeDtypeStruct((s, D_MODEL), F32)] * 2,
        compiler_params=_cparams(dimension_semantics=("parallel",)),
    )(att, rec, proj, proj, w, x, g_post, gate)


def _post_bwd_merge_bwd(f, dy, g_post, gate, res_w, w, att, rec, proj, name):
    s = f.shape[0]
    tm = min(FUSE_TM, s)

    def body(f_ref, dy_ref, gp_ref, gate_ref, w_ref, att_ref, rec_ref, ga_ref, gr_ref,
             df_ref, datt_ref, drec_ref, dga_ref, dgr_ref, sums_ref):
        i = pl.program_id(0)
        dgp, dgate, df = _post_vjp(res_w, gp_ref[...], gate_ref[...], f_ref[...], dy_ref[...])
        dfb = df.astype(BF16)
        df_ref[...] = dfb
        _accumulate(sums_ref, _sum_rows(dgp, dgate), i)
        dmerged = lax.dot_general(dfb, w_ref[...], _DIMS["nt"], preferred_element_type=F32)
        _, vjp = jax.vjp(_merge_fn, att_ref[...], rec_ref[...], ga_ref[...], gr_ref[...])
        for ref, val in zip((datt_ref, drec_ref, dga_ref, dgr_ref), vjp(dmerged)):
            ref[...] = val.astype(ref.dtype)

    rows = pl.BlockSpec((tm, D_MODEL), lambda i: (i, 0))
    return pl.pallas_call(
        body,
        name=name,
        grid=(s // tm,),
        in_specs=[rows, rows, ROW_SPEC1, ROW_SPEC1, pl.BlockSpec(w.shape, lambda i: (0, 0)), rows, rows,
                  pl.BlockSpec((tm, D_MODEL), lambda i: (i, 2)), pl.BlockSpec((tm, D_MODEL), lambda i: (i, 3))],
        out_specs=[rows] * 5 + [SUMS_SPEC1],
        out_shape=[jax.ShapeDtypeStruct((s, D_MODEL), BF16)] * 5 + [SUMS_SHAPE],
        compiler_params=_cparams(dimension_semantics=("arbitrary",)),
    )(f, dy, g_post, gate, w, att, rec, proj, proj)


def _matmul_recin_bwd(drec, w, hs, proj, name, deps=()):
    s = drec.shape[0]
    tm = min(FUSE_TM, s)
    nd = len(deps)

    def body(d_ref, w_ref, hs_ref, yr_ref, *rest):
        dhs_ref, dyr_ref = rest[nd:]
        d = lax.dot_general(d_ref[...], w_ref[...], _DIMS["nt"], preferred_element_type=F32)
        _, vjp = jax.vjp(_recin_fn, hs_ref[...], yr_ref[...])
        dhs, dyr = vjp(d)
        dhs_ref[...] = dhs
        dyr_ref[...] = dyr.astype(dyr_ref.dtype)

    rows = pl.BlockSpec((tm, D_MODEL), lambda i: (i, 0))
    return pl.pallas_call(
        body,
        name=name,
        grid=(s // tm,),
        in_specs=[rows, pl.BlockSpec(w.shape, lambda i: (0, 0)), rows,
                  pl.BlockSpec((tm, D_MODEL), lambda i: (i, 1))] + [ANY_SPEC] * nd,
        out_specs=[rows, rows],
        out_shape=[jax.ShapeDtypeStruct((s, D_MODEL), F32), jax.ShapeDtypeStruct((s, D_MODEL), BF16)],
        compiler_params=_cparams(dimension_semantics=("parallel",)),
    )(drec, w, hs, proj, *deps)


def _post_vjp(res_w, g, gate, f, dy):
    _, vjp = jax.vjp(lambda g, gate, f: _post_fn(res_w, g, gate, f, 0.0), g, gate, f)
    return vjp(dy)


def _post_bwd_up_bwd(f, dy, g_post, gate, res_w, w_down, g, u, name, deps=()):
    s = f.shape[0]
    tm = min(FFN_TM, s)
    nd = len(deps)

    def body(f_ref, dy_ref, gp_ref, gate_ref, wd_ref, g_ref, u_ref, *rest):
        df_ref, dgu_ref, sums_ref, df_s = rest[nd:]
        i = pl.program_id(0)

        @pl.when(pl.program_id(1) == 0)
        def _():
            dgp, dgate, df = _post_vjp(res_w, gp_ref[...], gate_ref[...], f_ref[...], dy_ref[...])
            df_s[...] = df.astype(BF16)
            df_ref[...] = df_s[...]
            _accumulate(sums_ref, _sum_rows(dgp, dgate), i)

        da = lax.dot_general(df_s[...], wd_ref[...], _DIMS["nt"], preferred_element_type=F32)
        _, vjp = jax.vjp(_glu_fn, g_ref[...].astype(F32), u_ref[...].astype(F32))
        dg, du = vjp(da)
        dgu_ref[0] = dg.astype(dgu_ref.dtype)
        dgu_ref[1] = du.astype(dgu_ref.dtype)

    rows = pl.BlockSpec((tm, D_MODEL), lambda i, j: (i, 0))
    blk = pl.BlockSpec((tm, FFN_TF), lambda i, j: (i, j))
    return pl.pallas_call(
        body,
        name=name,
        grid=(s // tm, D_FF // FFN_TF),
        in_specs=[rows, rows, ROW_SPEC2, ROW_SPEC2, pl.BlockSpec((FFN_TF, D_MODEL), lambda i, j: (j, 0)), blk,
                  blk] + [ANY_SPEC] * nd,
        out_specs=[rows, pl.BlockSpec((2, tm, FFN_TF), lambda i, j: (0, i, j)), SUMS_SPEC2],
        out_shape=[jax.ShapeDtypeStruct((s, D_MODEL), BF16), jax.ShapeDtypeStruct((2, s, D_FF), BF16), SUMS_SHAPE],
        scratch_shapes=[pltpu.VMEM((tm, D_MODEL), BF16)],
        compiler_params=_cparams(dimension_semantics=("arbitrary", "arbitrary")),
    )(f, dy, g_post, gate, w_down, g, u, *deps)


def _post_bwd_matmul(f, dy, g_post, gate, res_w, w, name):
    s = f.shape[0]
    n = w.shape[0]
    tm = min(FUSE_TM, s)

    def body(f_ref, dy_ref, gp_ref, gate_ref, w_ref, df_ref, o_ref, dgp_ref, dgate_ref):
        i = pl.program_id(0)
        dgp, dgate, df = _post_vjp(res_w, gp_ref[...], gate_ref[...], f_ref[...], dy_ref[...])
        dfb = df.astype(BF16)
        df_ref[...] = dfb
        _accumulate(dgp_ref, dgp, i)
        _accumulate(dgate_ref, dgate, i)
        o_ref[...] = lax.dot_general(dfb, w_ref[...], _DIMS["nt"], preferred_element_type=F32)

    rows = pl.BlockSpec((tm, D_MODEL), lambda i: (i, 0))
    return pl.pallas_call(
        body,
        name=name,
        grid=(s // tm,),
        in_specs=[rows, rows, ROW_SPEC1, ROW_SPEC1, pl.BlockSpec((n, D_MODEL), lambda i: (0, 0))],
        out_specs=[rows, pl.BlockSpec((tm, n), lambda i: (i, 0)), ROW_SPEC1, ROW_SPEC1],
        out_shape=[jax.ShapeDtypeStruct((s, D_MODEL), BF16), jax.ShapeDtypeStruct((s, n), F32),
                   jax.ShapeDtypeStruct((1, D_MODEL), F32), jax.ShapeDtypeStruct((1, D_MODEL), F32)],
        compiler_params=_cparams(dimension_semantics=("arbitrary",)),
    )(f, dy, g_post, gate, w)


def _matmul_pre_bwd(parts, w_t, x, dres, g, shift, scale, name, deps=()):
    s = x.shape[0]
    na, nd = len(parts), len(deps)
    ranges = [p[3] for p in parts]

    def body(*refs):
        a_refs = refs[:na]
        w_ref, x_ref, dres_ref, g_ref, sh_ref, sc_ref = refs[na : na + 6]
        dx_ref, sums_ref = refs[na + 6 + nd :]
        i = pl.program_id(0)
        dh = None
        for a_ref, (r0, r1) in zip(a_refs, ranges):
            p = jnp.dot(a_ref[...], w_ref[r0:r1, :], preferred_element_type=F32)
            dh = p if dh is None else dh + p
        _, vjp = jax.vjp(_pre_fn, g_ref[...], sh_ref[...], sc_ref[...], x_ref[...])
        dg, dsh, dsc, dx = vjp(dh)
        dx_ref[...] = dx + dres_ref[...]
        _accumulate(sums_ref, _sum_rows(dg, dsh, dsc), i)

    tm = parts[0][1][-2]
    rows = pl.BlockSpec((tm, D_MODEL), lambda i: (i, 0))
    return pl.pallas_call(
        body,
        name=name,
        grid=(s // tm,),
        in_specs=[pl.BlockSpec(p[1], p[2]) for p in parts]
        + [pl.BlockSpec(w_t.shape, lambda i: (0, 0)), rows, rows, ROW_SPEC1, ROW_SPEC1, ROW_SPEC1]
        + [ANY_SPEC] * nd,
        out_specs=[rows, SUMS_SPEC1],
        out_shape=[jax.ShapeDtypeStruct((s, D_MODEL), F32), SUMS_SHAPE],
        compiler_params=_cparams(dimension_semantics=("arbitrary",)),
    )(*[p[0] for p in parts], w_t, x, dres, g, shift, scale, *deps)


def _dw_gu(dgu, h, name, deps=(), tk=DW_TK):
    s = h.shape[0]
    tk = min(tk, s)
    nk = s // tk
    half = D_FF // FFN_TF

    def body(a_ref, b_ref, *rest):
        o_ref, acc = rest[len(deps) :]
        kk = pl.program_id(1)
        p = lax.dot_general(a_ref[...], b_ref[...], _DIMS["tn"], preferred_element_type=F32)

        @pl.when(kk == 0)
        def _():
            acc[...] = p

        @pl.when(kk > 0)
        def _():
            acc[...] += p

        @pl.when(kk == nk - 1)
        def _():
            o_ref[...] = acc[...].astype(o_ref.dtype)

    return pl.pallas_call(
        body,
        name=name,
        grid=(2 * half, nk),
        in_specs=[pl.BlockSpec((None, tk, FFN_TF), lambda i, kk: (i // half, kk, i % half)),
                  pl.BlockSpec((tk, D_MODEL), lambda i, kk: (kk, 0))] + [ANY_SPEC] * len(deps),
        out_specs=pl.BlockSpec((FFN_TF, D_MODEL), lambda i, kk: (i, 0)),
        out_shape=jax.ShapeDtypeStruct((2 * D_FF, D_MODEL), BF16),
        scratch_shapes=[pltpu.VMEM((FFN_TF, D_MODEL), F32)],
        compiler_params=_cparams(dimension_semantics=("parallel", "arbitrary")),
    )(dgu, h, *deps)


def _lru_diag_blocks(dw_bd, name):
    def body(w_ref, o_ref):
        for half in range(2):
            for n in range(LRU_BLOCKS):
                rows = slice(n * LRU_BLOCK, (n + 1) * LRU_BLOCK)
                cols = slice(half * LRU_WIDTH + n * LRU_BLOCK, half * LRU_WIDTH + (n + 1) * LRU_BLOCK)
                o_ref[half, rows, :] = w_ref[rows, cols]

    return pl.pallas_call(
        body, name=name, out_shape=jax.ShapeDtypeStruct((2, LRU_WIDTH, LRU_BLOCK), F32), compiler_params=_cparams()
    )(dw_bd)


def _swiglu_fwd(gu, name):
    (a,) = _rowwise(_swiglu_fn, name, [], [gu], [(D_FF, BF16)], ts=128)
    return a


def _swiglu_bwd(gu, da, name, deps=()):
    def fn(gu, da):
        _, vjp = jax.vjp(_swiglu_fn, gu)
        return vjp(da)[0]

    (dgu,) = _rowwise(fn, name, [], [gu, da], [(2 * D_FF, BF16)], ts=128, deps=deps)
    return dgu


def _shift_down(ext, j, rows):
    return pltpu.roll(ext, j, 0)[V7X_SUBLANES : V7X_SUBLANES + rows]


def _shift_up(ext, j, rows):
    return pltpu.roll(ext, ext.shape[0] - j, 0)[:rows] if j else ext[:rows]


LRU_SLAB = 256
N_SLABS = LRU_WIDTH // LRU_SLAB


def _slab_weights(wa, wx):
    per = LRU_SLAB // LRU_BLOCK
    eye = jnp.eye(per, dtype=wa.dtype)

    def diag(w):
        w4 = w.reshape(N_SLABS, per, LRU_BLOCK, LRU_BLOCK)
        return jnp.einsum("sbkj,bc->sbkcj", w4, eye).reshape(N_SLABS, LRU_SLAB, LRU_SLAB)

    return jnp.concatenate([diag(wa), diag(wx)], axis=2).reshape(LRU_WIDTH, 2 * LRU_SLAB).astype(BF16)


def _slab_cols(v, s):
    lo = s * LRU_SLAB
    return jnp.concatenate([v[:, lo : lo + LRU_SLAB], v[:, LRU_WIDTH + lo : LRU_WIDTH + lo + LRU_SLAB]], axis=1)


def _lru_front(proj, w8, b, w_slab, ba, bx, lam, name):
    def fn(i, steps, w8, b, w_slab, ba, bx, lam, x, halo):
        halo = jnp.where(i > 0, halo, 0.0)
        ext = jnp.concatenate([halo, x], axis=0)
        xc = b + w8[3:4] * x
        for j in (1, 2, 3):
            xc = xc + w8[3 - j : 4 - j] * _shift_down(ext, j, x.shape[0])
        xcb = xc.astype(BF16)
        prods = []
        for s in range(N_SLABS):
            rows = slice(s * LRU_SLAB, (s + 1) * LRU_SLAB)
            prods.append(jnp.dot(xcb[:, rows], w_slab[rows], preferred_element_type=F32))
        pre = jnp.concatenate([p[:, :LRU_SLAB] for p in prods] + [p[:, LRU_SLAB:] for p in prods], axis=1)
        a, u = _gates_fn(ba, bx, lam, pre, xc)
        return xc, pre, a, u

    tiles = [(proj, LRU_WIDTH, 0), (proj, LRU_WIDTH, 0, "prev")]
    outs = [(LRU_WIDTH, F32), (2 * LRU_WIDTH, F32), (LRU_WIDTH, F32), (LRU_WIDTH, F32)]
    return _rowwise(fn, name, [w8, b, w_slab, ba, bx, lam], tiles, outs, with_index=True)


def _lru_back(pre, xc, w_slab, ba, bx, lam, g, h_prev, name, deps=()):
    def fn(w_slab, ba, bx, lam, pre, xc, g, h_prev):
        _, vjp = jax.vjp(_gates_fn, ba, bx, lam, pre, xc)
        dba, dbx, dlam, dpre, dxc = vjp((g * h_prev, g))
        dpre = dpre.astype(BF16)
        back = []
        for s in range(N_SLABS):
            rows = slice(s * LRU_SLAB, (s + 1) * LRU_SLAB)
            back.append(lax.dot_general(_slab_cols(dpre, s), w_slab[rows], _DIMS["nt"], preferred_element_type=F32))
        return dpre, dxc + jnp.concatenate(back, axis=1), _sum_rows(dba, dbx, dlam)

    return _rowwise(fn, name, [w_slab, ba, bx, lam], [pre, xc, g, h_prev],
                    [(2 * LRU_WIDTH, BF16), (LRU_WIDTH, F32)], [(V7X_SUBLANES, LRU_WIDTH)], deps=deps)


def _lru_dw(xc, dpre, name):
    s = xc.shape[0]
    ts = min(512, s)
    steps = s // ts
    per = LRU_SLAB // LRU_BLOCK

    def body(x_ref, d_ref, o_ref, acc):
        i = pl.program_id(0)
        xcb = x_ref[...].astype(BF16)
        d = d_ref[...]
        for sl in range(N_SLABS):
            rows = slice(sl * LRU_SLAB, (sl + 1) * LRU_SLAB)
            p = lax.dot_general(xcb[:, rows], _slab_cols(d, sl), _DIMS["tn"], preferred_element_type=F32)

            @pl.when(i == 0)
            def _(p=p, rows=rows):
                acc[rows, :] = p

            @pl.when(i > 0)
            def _(p=p, rows=rows):
                acc[rows, :] += p

        @pl.when(i == steps - 1)
        def _():
            for half in range(2):
                for n in range(LRU_BLOCKS):
                    r0 = n * LRU_BLOCK
                    c0 = half * LRU_SLAB + (n % per) * LRU_BLOCK
                    o_ref[half, r0 : r0 + LRU_BLOCK, :] = acc[r0 : r0 + LRU_BLOCK, c0 : c0 + LRU_BLOCK]

    return pl.pallas_call(
        body,
        name=name,
        grid=(steps,),
        in_specs=[pl.BlockSpec((ts, LRU_WIDTH), lambda i: (i, 0)), pl.BlockSpec((ts, 2 * LRU_WIDTH), lambda i: (i, 0))],
        out_specs=pl.BlockSpec((2, LRU_WIDTH, LRU_BLOCK), lambda i: (0, 0, 0)),
        out_shape=jax.ShapeDtypeStruct((2, LRU_WIDTH, LRU_BLOCK), F32),
        scratch_shapes=[pltpu.VMEM((LRU_WIDTH, 2 * LRU_SLAB), F32)],
        compiler_params=_cparams(dimension_semantics=("arbitrary",)),
    )(xc, dpre)


def _conv_bwd(proj, w8, d1, name):
    def fn(i, steps, w8, x, halo, d, d1n):
        rows = x.shape[0]
        dn = jnp.where(i < steps - 1, d1n, 0.0)
        halo = jnp.where(i > 0, halo, 0.0)
        dext = jnp.concatenate([d, dn], axis=0)
        xext = jnp.concatenate([halo, x], axis=0)
        dx = w8[3:4] * d
        dw = [None] * 4
        dw[3] = _rowsum(d * x)
        for k in (1, 2, 3):
            dx = dx + w8[3 - k : 4 - k] * _shift_up(dext, k, rows)
            dw[3 - k] = _rowsum(d * _shift_down(xext, k, rows))
        return dx, _sum_rows(*dw, _rowsum(d))

    tiles = [(proj, LRU_WIDTH, 0), (proj, LRU_WIDTH, 0, "prev"), d1, (d1, LRU_WIDTH, 0, "next")]
    return _rowwise(fn, name, [w8], tiles, [(LRU_WIDTH, BF16)], [(V7X_SUBLANES, LRU_WIDTH)], with_index=True)


SCAN_ROWS = 512


def _block_scan(a, b, row, reverse):
    for d in (1, 2, 4):
        if reverse:
            shift, keep = V7X_SUBLANES - d, row < V7X_SUBLANES - d
        else:
            shift, keep = d, row >= d
        a_s = pltpu.roll(a, shift, 0)
        b_s = pltpu.roll(b, shift, 0)
        b = jnp.where(keep, a * b_s + b, b)
        a = jnp.where(keep, a * a_s, a)
    return a, b


def _scan_fwd(a, u, proj, name):
    s, w = a.shape
    ts = min(SCAN_ROWS, s)
    sub = ts // V7X_SUBLANES

    def body(a_ref, u_ref, yr_ref, h_ref, hp_ref, rec_ref, carry):
        @pl.when(pl.program_id(0) == 0)
        def _():
            carry[...] = jnp.zeros_like(carry)

        row = lax.broadcasted_iota(jnp.int32, (V7X_SUBLANES, w), 0)

        def step(j, c):
            rows = pl.ds(pl.multiple_of(j * V7X_SUBLANES, V7X_SUBLANES), V7X_SUBLANES)
            pa, pb = _block_scan(a_ref[rows, :], u_ref[rows, :], row, False)
            h = pb + pa * c
            h_ref[rows, :] = h
            hp_ref[rows, :] = jnp.where(row >= 1, pltpu.roll(h, 1, 0), c)
            return jnp.broadcast_to(h[V7X_SUBLANES - 1 :], (V7X_SUBLANES, w))

        carry[...] = lax.fori_loop(0, sub, step, carry[...])
        rec_ref[...] = _recin_fn(h_ref[...], yr_ref[...]).astype(rec_ref.dtype)

    spec = pl.BlockSpec((ts, w), lambda i: (i, 0))
    return pl.pallas_call(
        body,
        name=name,
        grid=(s // ts,),
        in_specs=[spec, spec, pl.BlockSpec((ts, w), lambda i: (i, 1))],
        out_specs=[spec, spec, spec],
        out_shape=[jax.ShapeDtypeStruct((s, w), F32)] * 2 + [jax.ShapeDtypeStruct((s, w), BF16)],
        scratch_shapes=[pltpu.VMEM((V7X_SUBLANES, w), F32)],
        compiler_params=_cparams(dimension_semantics=("arbitrary",)),
    )(a, u, proj)


def _scan_bwd(a, dh, name):
    s, w = a.shape
    ts = min(SCAN_ROWS, s)
    sub = ts // V7X_SUBLANES
    steps = s // ts

    def body(a_ref, d_ref, g_ref, carry):
        @pl.when(pl.program_id(0) == 0)
        def _():
            carry[...] = jnp.zeros_like(carry)

        row = lax.broadcasted_iota(jnp.int32, (V7X_SUBLANES, w), 0)

        def step(jj, c):
            j = sub - 1 - jj
            rows = pl.ds(pl.multiple_of(j * V7X_SUBLANES, V7X_SUBLANES), V7X_SUBLANES)
            av, dv = a_ref[rows, :], d_ref[rows, :]
            pa, pb = _block_scan(av, av * dv, row, True)
            big = pb + pa * c
            g_ref[rows, :] = dv + jnp.where(row < V7X_SUBLANES - 1, pltpu.roll(big, V7X_SUBLANES - 1, 0), c)
            return jnp.broadcast_to(big[:1], (V7X_SUBLANES, w))

        carry[...] = lax.fori_loop(0, sub, step, carry[...])

    spec = pl.BlockSpec((ts, w), lambda i: (steps - 1 - i, 0))
    return pl.pallas_call(
        body,
        name=name,
        grid=(steps,),
        in_specs=[spec, spec],
        out_specs=spec,
        out_shape=jax.ShapeDtypeStruct((s, w), F32),
        scratch_shapes=[pltpu.VMEM((V7X_SUBLANES, w), F32)],
        compiler_params=_cparams(dimension_semantics=("arbitrary",)),
    )(a, dh)


def _rel_index():
    i = np.arange(ATT_TQ)[:, None]
    j = np.arange(3 * ATT_TQ)[None, :]
    band = (j // CHUNK >= i // CHUNK) & (j // CHUNK <= i // CHUNK + LEFT_CHUNKS)
    return band


SKEW = 4 * ATT_TQ


def _skew_onehot():
    t = np.arange(SKEW)
    diag = np.where(t < 3 * ATT_TQ, -t, SKEW - t)
    idx = np.clip(diag + LEFT_CHUNKS * CHUNK, -MAX_REL, MAX_REL) + MAX_REL
    hit = (idx[:, None] == np.arange(2 * MAX_REL + 1)[None, :]) & (t[:, None] != 3 * ATT_TQ)
    return hit.astype(np.float32)


def _bias_tile(rel_bias, name):
    per_t = jnp.dot(rel_bias, jnp.asarray(_skew_onehot()).T, precision=lax.Precision.HIGHEST)
    win = 3 * ATT_TQ

    def body(t_ref, o_ref):
        tile = pltpu.roll(jnp.broadcast_to(t_ref[0], (ATT_TQ, SKEW)), 0, 1, stride=1, stride_axis=0)[:, :win]
        qc = lax.broadcasted_iota(jnp.int32, (ATT_TQ, win), 0) // CHUNK
        kpos = lax.broadcasted_iota(jnp.int32, (ATT_TQ, win), 1)
        band = (kpos // CHUNK >= qc) & (kpos // CHUNK <= qc + LEFT_CHUNKS)
        for v in range(3):
            o_ref[v, 0] = jnp.where(band & (kpos >= (2 - v) * ATT_TQ), tile, NEG)

    return pl.pallas_call(
        body,
        name=name,
        grid=(ATT_HEADS,),
        in_specs=[pl.BlockSpec((1, 1, SKEW), lambda h: (h, 0, 0))],
        out_specs=pl.BlockSpec((3, 1, ATT_TQ, win), lambda h: (0, h, 0, 0)),
        out_shape=jax.ShapeDtypeStruct((3, ATT_HEADS, ATT_TQ, win), F32),
        compiler_params=_cparams(dimension_semantics=("parallel",)),
    )(per_t.reshape(ATT_HEADS, 1, SKEW))


def _bias_grad(dbias, name):
    win = 3 * ATT_TQ

    def body(d_ref, o_ref):
        d = jnp.concatenate([d_ref[0], jnp.zeros((ATT_TQ, SKEW - win), F32)], axis=1)
        r = lax.broadcasted_iota(jnp.int32, (ATT_TQ, ATT_TQ), 0)
        c = lax.broadcasted_iota(jnp.int32, (ATT_TQ, ATT_TQ), 1)
        flip = (r + c == ATT_TQ - 1).astype(F32)
        d = jnp.dot(flip, d, preferred_element_type=F32, precision=lax.Precision.HIGHEST)
        o_ref[0] = jnp.sum(pltpu.roll(d, SKEW - (ATT_TQ - 1), 1, stride=1, stride_axis=0), axis=0, keepdims=True)

    per_t = pl.pallas_call(
        body,
        name=name,
        grid=(ATT_HEADS,),
        in_specs=[pl.BlockSpec((1, ATT_TQ, win), lambda h: (h, 0, 0))],
        out_specs=pl.BlockSpec((1, 1, SKEW), lambda h: (h, 0, 0)),
        out_shape=jax.ShapeDtypeStruct((ATT_HEADS, 1, SKEW), F32),
        compiler_params=_cparams(dimension_semantics=("parallel",)),
    )(dbias)
    return jnp.dot(per_t.reshape(ATT_HEADS, SKEW), jnp.asarray(_skew_onehot()), precision=lax.Precision.HIGHEST)


ATT_STEP_HEADS = ATT_HEADS
ATT_STEP_COLS = ATT_STEP_HEADS * ATT_HEAD_DIM


def _attn_specs(nt):
    qb, kb, vb = OFF_Q // ATT_STEP_COLS, OFF_K // ATT_STEP_COLS, OFF_V // ATT_STEP_COLS
    blk = (ATT_TQ, ATT_STEP_COLS)

    def qmap(base):
        return lambda hp, m: (jnp.minimum(m, nt - 1), base + hp)

    def wmap(base, back):
        return lambda hp, m: (jnp.clip(m - back, 0, nt - 1), base + hp)

    specs = [pl.BlockSpec(blk, qmap(qb))]
    specs += [pl.BlockSpec(blk, wmap(kb, back)) for back in (2, 1, 0)]
    specs += [pl.BlockSpec(blk, wmap(vb, back)) for back in (2, 1, 0)]
    return specs


ATT_SCALE = ATT_HEAD_DIM**-0.5


def _attn_exp(qh, kh, bias):
    s = lax.dot_general(qh, kh, _DIMS["nt"], preferred_element_type=F32) + bias
    e = jnp.exp(s - jnp.max(s, axis=-1, keepdims=True))
    return e, jnp.sum(e, axis=-1, keepdims=True)


def _attn_window(k0, k1, k2, v0, v1, v2):
    k = jnp.concatenate([k0[...], k1[...], k2[...]], axis=0).astype(BF16)
    v = jnp.concatenate([v0[...], v1[...], v2[...]], axis=0).astype(BF16)
    return k, v


def _bias_spec():
    return pl.BlockSpec((1, ATT_STEP_HEADS, ATT_TQ, 3 * ATT_TQ), lambda hp, m: (jnp.minimum(m, 2), hp, 0, 0))


def _attn_fwd(proj, bias, name):
    s = proj.shape[0]
    nt = s // ATT_TQ

    def body(q_ref, k0, k1, k2, v0, v1, v2, b_ref, o_ref):
        k, v = _attn_window(k0, k1, k2, v0, v1, v2)
        q = (q_ref[...] * ATT_SCALE).astype(BF16)
        for hh in range(ATT_STEP_HEADS):
            cols = slice(hh * ATT_HEAD_DIM, (hh + 1) * ATT_HEAD_DIM)
            e, total = _attn_exp(q[:, cols], k[:, cols], b_ref[0, hh])
            o = jnp.dot(e.astype(BF16), v[:, cols], preferred_element_type=F32) / total
            o_ref[:, cols] = o.astype(o_ref.dtype)

    specs = _attn_specs(nt) + [_bias_spec()]
    return pl.pallas_call(
        body,
        name=name,
        grid=(ATT_HEADS // ATT_STEP_HEADS, nt),
        in_specs=specs,
        out_specs=pl.BlockSpec((ATT_TQ, ATT_STEP_COLS), lambda hp, m: (m, hp)),
        out_shape=jax.ShapeDtypeStruct((s, ATT_WIDTH), BF16),
        compiler_params=_cparams(dimension_semantics=("parallel", "arbitrary")),
    )(proj, proj, proj, proj, proj, proj, proj, bias)


def _attn_bwd(proj, bias, do, name):
    s = proj.shape[0]
    nt = s // ATT_TQ
    win = 3 * ATT_TQ

    def body(q_ref, k0, k1, k2, v0, v1, v2, do_ref, b_ref, dq_ref, dk_ref, dv_ref, db_ref, dk_acc, dv_acc):
        m = pl.program_id(1)

        @pl.when(m == 0)
        def _():
            dk_acc[...] = jnp.zeros_like(dk_acc)
            dv_acc[...] = jnp.zeros_like(dv_acc)
            db_ref[...] = jnp.zeros_like(db_ref)

        @pl.when(m < nt)
        def _():
            k, v = _attn_window(k0, k1, k2, v0, v1, v2)
            q = (q_ref[...] * ATT_SCALE).astype(BF16)
            dout = do_ref[...]
            for hh in range(ATT_STEP_HEADS):
                cols = slice(hh * ATT_HEAD_DIM, (hh + 1) * ATT_HEAD_DIM)
                qh, kh, vh, doh = q[:, cols], k[:, cols], v[:, cols], dout[:, cols]
                e, total = _attn_exp(qh, kh, b_ref[0, hh])
                p = e / total
                dvh = lax.dot_general(p.astype(BF16), doh, _DIMS["tn"], preferred_element_type=F32)
                dp = lax.dot_general(doh, vh, _DIMS["nt"], preferred_element_type=F32)
                ds = p * (dp - jnp.sum(dp * p, axis=-1, keepdims=True))
                db_ref[hh] += ds
                dsb = ds.astype(BF16)
                dqh = jnp.dot(dsb, kh, preferred_element_type=F32) * ATT_SCALE
                dkh = lax.dot_general(dsb, qh, _DIMS["tn"], preferred_element_type=F32)
                dq_ref[:, cols] = dqh.astype(dq_ref.dtype)
                dk_acc[:, cols] += dkh
                dv_acc[:, cols] += dvh

        dk_ref[...] = dk_acc[:ATT_TQ].astype(dk_ref.dtype)
        dv_ref[...] = dv_acc[:ATT_TQ].astype(dv_ref.dtype)
        for acc in (dk_acc, dv_acc):
            rest = acc[ATT_TQ:]
            acc[: win - ATT_TQ] = rest
            acc[win - ATT_TQ :] = jnp.zeros((ATT_TQ, ATT_STEP_COLS), F32)

    blk = (ATT_TQ, ATT_STEP_COLS)
    specs = _attn_specs(nt)
    specs.append(pl.BlockSpec(blk, lambda hp, m: (jnp.minimum(m, nt - 1), hp)))
    specs.append(_bias_spec())
    done = lambda hp, m: (jnp.maximum(m - 2, 0), hp)
    out_specs = [
        pl.BlockSpec(blk, lambda hp, m: (jnp.minimum(m, nt - 1), hp)),
        pl.BlockSpec(blk, done),
        pl.BlockSpec(blk, done),
        pl.BlockSpec((ATT_STEP_HEADS, ATT_TQ, win), lambda hp, m: (hp, 0, 0)),
    ]
    out_shape = [jax.ShapeDtypeStruct((s, ATT_WIDTH), BF16)] * 3
    out_shape.append(jax.ShapeDtypeStruct((ATT_HEADS, ATT_TQ, win), F32))
    return pl.pallas_call(
        body,
        name=name,
        grid=(ATT_HEADS // ATT_STEP_HEADS, nt + 2),
        in_specs=specs,
        out_specs=out_specs,
        out_shape=out_shape,
        scratch_shapes=[pltpu.VMEM((win, ATT_STEP_COLS), F32), pltpu.VMEM((win, ATT_STEP_COLS), F32)],
        compiler_params=_cparams(dimension_semantics=("arbitrary", "arbitrary")),
    )(proj, proj, proj, proj, proj, proj, proj, do, bias)


def _ada_fwd(c_all, w, name):
    def body(c_ref, w_ref, o_ref):
        act = _silu(c_ref[...]).astype(BF16)
        o_ref[...] = jnp.dot(act, w_ref[...].astype(BF16), preferred_element_type=F32)

    return pl.pallas_call(
        body, name=name, out_shape=jax.ShapeDtypeStruct((c_all.shape[0], w.shape[1]), F32), compiler_params=_cparams()
    )(c_all, w)


def _ada_bwd(c_all, dmod, name):
    def body(c_ref, d_ref, o_ref):
        act = _silu(c_ref[...])
        o_ref[...] = lax.dot_general(act, d_ref[...], _DIMS["tn"], preferred_element_type=F32,
                                     precision=lax.Precision.HIGHEST)

    return pl.pallas_call(
        body, name=name, out_shape=jax.ShapeDtypeStruct((c_all.shape[1], dmod.shape[1]), F32), compiler_params=_cparams()
    )(c_all, dmod)


def _adamw_parts(landed, sent, me, w, m, v, name, rows=256):
    r, c = w.shape
    tr = _pick(r, rows, 16)

    def body(me_ref, g_ref, own_ref, w_ref, m_ref, v_ref, go_ref, d_ref, mo_ref, vo_ref):
        mine = me_ref[0]
        grad = jnp.zeros((tr, c), F32)
        for d in range(N_DEV):
            grad = grad + jnp.where(mine == d, own_ref[0], g_ref[d]).astype(F32)
        _adamw_update(grad, w_ref, m_ref, v_ref, go_ref, d_ref, mo_ref, vo_ref)

    spec = pl.BlockSpec((tr, c), lambda i, me_ref: (i, 0))
    return pl.pallas_call(
        body,
        name=name,
        grid_spec=pltpu.PrefetchScalarGridSpec(
            num_scalar_prefetch=1,
            grid=(r // tr,),
            in_specs=[pl.BlockSpec((N_DEV, tr, c), lambda i, me_ref: (0, i, 0)),
                      pl.BlockSpec((1, tr, c), lambda i, me_ref: (me_ref[0], i, 0)), spec, spec, spec],
            out_specs=[spec] * 4,
        ),
        out_shape=[jax.ShapeDtypeStruct((r, c), F32)] * 4,
        compiler_params=_cparams(dimension_semantics=("parallel",)),
    )(me.reshape(1).astype(jnp.int32), landed, sent, w, m, v)


def _adamw_update(grad, w_ref, m_ref, v_ref, go_ref, d_ref, mo_ref, vo_ref):
    m2 = ADAM_B1 * m_ref[...] + (1.0 - ADAM_B1) * grad
    v2 = ADAM_B2 * v_ref[...] + (1.0 - ADAM_B2) * (grad * grad)
    m_hat = m2 / (1.0 - ADAM_B1**ADAM_STEP)
    v_hat = v2 / (1.0 - ADAM_B2**ADAM_STEP)
    go_ref[...] = grad
    d_ref[...] = -ADAM_LR * (m_hat / (jnp.sqrt(v_hat) + ADAM_EPS) + ADAM_WD * w_ref[...])
    mo_ref[...] = m2
    vo_ref[...] = v2


def _adamw(g, w, m, v, name, rows=256):
    r, c = w.shape
    tr = _pick(r, rows, 16)

    def body(g_ref, w_ref, m_ref, v_ref, go_ref, d_ref, mo_ref, vo_ref):
        _adamw_update(g_ref[...], w_ref, m_ref, v_ref, go_ref, d_ref, mo_ref, vo_ref)

    spec = pl.BlockSpec((tr, c), lambda i: (i, 0))
    return pl.pallas_call(
        body,
        name=name,
        grid=(r // tr,),
        in_specs=[spec, spec, spec, spec],
        out_specs=[spec] * 4,
        out_shape=[jax.ShapeDtypeStruct((r, c), F32)] * 4,
        compiler_params=_cparams(dimension_semantics=("parallel",)),
    )(g, w, m, v)


def _sum_parts(parts, name):
    def body(p_ref, o_ref):
        acc = p_ref[0]
        for d in range(1, N_DEV):
            acc = acc + p_ref[d]
        o_ref[...] = acc

    return pl.pallas_call(
        body, name=name, out_shape=jax.ShapeDtypeStruct(parts.shape[1:], F32), compiler_params=_cparams()
    )(parts)


def _place():
    x, y, c = lax.axis_index("x"), lax.axis_index("y"), lax.axis_index("c")
    return x, y, c


def _dev_index(p):
    return 4 * p[0] + 2 * p[1] + p[2]


def _allgather_vmem(shard, name):
    m_per, n = shard.shape

    def body(x_ref, out_ref, send_sems, recv_sems, local_sem):
        x, y, c = _place()
        me, sibling = (x, y, c), (x, y, 1 - c)
        chips = [(1 - x, y), (x, 1 - y), (1 - x, 1 - y)]

        def rows(p):
            return out_ref.at[pl.ds(_dev_index(p) * m_per, m_per), :]

        def copy(k, block, to, src=None):
            return pltpu.make_async_remote_copy(
                src_ref=rows(block) if src is None else src, dst_ref=rows(block),
                send_sem=send_sems.at[k], recv_sem=recv_sems.at[k], device_id=to, device_id_type=MESH)

        mine = pltpu.make_async_copy(x_ref, rows(me), local_sem)
        mine.start()
        first = [copy(0, me, sibling, src=x_ref)]
        first += [copy(1 + j, me, (*chip, c), src=x_ref) for j, chip in enumerate(chips)]
        for cp in first:
            cp.start()
        passed = [copy(4 + j, (*chip, c), sibling) for j, chip in enumerate(chips)]
        for j, chip in enumerate(chips):
            copy(1 + j, (*chip, c), me).wait_recv()
            passed[j].start()
        copy(0, sibling, me).wait_recv()
        for j, chip in enumerate(chips):
            copy(4 + j, (*chip, 1 - c), me).wait_recv()
        for cp in first + passed:
            cp.wait_send()
        mine.wait()

    return pl.pallas_call(
        body,
        name=name,
        out_shape=jax.ShapeDtypeStruct((N_DEV * m_per, n), shard.dtype),
        in_specs=[pl.BlockSpec(memory_space=pltpu.VMEM)],
        out_specs=pl.BlockSpec(memory_space=pltpu.VMEM),
        scratch_shapes=[pltpu.SemaphoreType.DMA((7,)), pltpu.SemaphoreType.DMA((7,)), pltpu.SemaphoreType.DMA],
        compiler_params=_cparams(),
    )(shard)


def _allgather_hbm(shards, name):
    n = len(shards)

    def body(*refs):
        ins, outs = refs[:n], refs[n : 2 * n]
        send_sems, recv_sems, local_sems = refs[2 * n :]
        x, y, c = _place()
        me, sibling = (x, y, c), (x, y, 1 - c)
        chips = [(1 - x, y), (x, 1 - y), (1 - x, 1 - y)]

        def copy(a, k, block, to, src=None):
            dst = outs[a].at[_dev_index(block)]
            return pltpu.make_async_remote_copy(
                src_ref=dst if src is None else src, dst_ref=dst,
                send_sem=send_sems.at[a * 7 + k], recv_sem=recv_sems.at[a * 7 + k], device_id=to, device_id_type=MESH)

        mine = [pltpu.make_async_copy(ins[a], outs[a].at[_dev_index(me)], local_sems.at[a]) for a in range(n)]
        for cp in mine:
            cp.start()
        first = []
        for a in range(n):
            first.append(copy(a, 0, me, sibling, src=ins[a]))
            first += [copy(a, 1 + j, me, (*chip, c), src=ins[a]) for j, chip in enumerate(chips)]
        for cp in first:
            cp.start()
        passed = []
        for j, chip in enumerate(chips):
            for a in range(n):
                copy(a, 1 + j, (*chip, c), me).wait_recv()
                cp = copy(a, 4 + j, (*chip, c), sibling)
                cp.start()
                passed.append(cp)
        for a in range(n):
            copy(a, 0, sibling, me).wait_recv()
        for j, chip in enumerate(chips):
            for a in range(n):
                copy(a, 4 + j, (*chip, 1 - c), me).wait_recv()
        for cp in first + passed:
            cp.wait_send()
        for cp in mine:
            cp.wait()

    any_spec = pl.BlockSpec(memory_space=pl.ANY)
    return pl.pallas_call(
        body,
        name=name,
        out_shape=[jax.ShapeDtypeStruct((N_DEV, *s.shape), s.dtype) for s in shards],
        in_specs=[any_spec] * n,
        out_specs=[any_spec] * n,
        scratch_shapes=[pltpu.SemaphoreType.DMA((7 * n,)), pltpu.SemaphoreType.DMA((7 * n,)),
                        pltpu.SemaphoreType.DMA((n,))],
        compiler_params=_cparams(),
    )(*shards)


def _exchange_hbm(bufs, name):
    n = len(bufs)

    def body(*refs):
        ins, outs = refs[:n], refs[n : 2 * n]
        send_sems, recv_sems, local_sems = refs[2 * n :]
        x, y, c = _place()
        me = _dev_index((x, y, c))
        mine = [pltpu.make_async_copy(ins[a].at[me], outs[a].at[me], local_sems.at[a]) for a in range(n)]
        for cp in mine:
            cp.start()
        def peer_of(k):
            return (1 - x if k & 4 else x, 1 - y if k & 2 else y, 1 - c if k & 1 else c)

        copies = []
        for k in range(1, N_DEV):
            peer = peer_of(k)
            for a in range(n):
                copies.append(pltpu.make_async_remote_copy(
                    src_ref=ins[a].at[_dev_index(peer)], dst_ref=outs[a].at[me],
                    send_sem=send_sems.at[a * 7 + k - 1], recv_sem=recv_sems.at[a * 7 + k - 1],
                    device_id=peer, device_id_type=MESH))
        for cp in copies:
            cp.start()
        for k in range(1, N_DEV):
            peer = peer_of(k)
            for a in range(n):
                pltpu.make_async_remote_copy(
                    src_ref=ins[a].at[me], dst_ref=outs[a].at[_dev_index(peer)],
                    send_sem=send_sems.at[a * 7 + k - 1], recv_sem=recv_sems.at[a * 7 + k - 1],
                    device_id=peer, device_id_type=MESH).wait_recv()
        for cp in copies:
            cp.wait_send()
        for cp in mine:
            cp.wait()

    any_spec = pl.BlockSpec(memory_space=pl.ANY)
    return pl.pallas_call(
        body,
        name=name,
        out_shape=[jax.ShapeDtypeStruct(b.shape, b.dtype) for b in bufs],
        in_specs=[any_spec] * n,
        out_specs=[any_spec] * n,
        scratch_shapes=[pltpu.SemaphoreType.DMA((7 * n,)), pltpu.SemaphoreType.DMA((7 * n,)),
                        pltpu.SemaphoreType.DMA((n,))],
        compiler_params=_cparams(),
    )(*bufs)


HBM_SPEC = pl.BlockSpec(memory_space=pltpu.HBM)
SEM_SPEC = pl.BlockSpec(memory_space=pltpu.SEMAPHORE)
EFFECT = pltpu.SideEffectType.DATAFLOW_SIDE_EFFECTING


def _peers(x, y, c):
    return [(1 - x if k & 4 else x, 1 - y if k & 2 else y, 1 - c if k & 1 else c) for k in range(1, N_DEV)]


def _push_peers(mode, x, y, c):
    if mode == "all":
        return _peers(x, y, c)
    return [(x, y, 1 - c), (1 - x, y, c), (x, 1 - y, c), (1 - x, 1 - y, c)]


def _push_start(groups, sliced, name, after=(), modes=None):
    flat = [b for g in groups for b in g]
    n, ng = len(flat), len(groups)
    sizes = [len(g) for g in groups]
    modes = modes or ["all"] * ng
    fan = [len(_push_peers(m, 0, 0, 0)) for m in modes]
    per = 2 if sliced else 3
    lands = [lax.empty(b.shape if sliced else (N_DEV, *b.shape), b.dtype) for b in flat]

    def body(*refs):
        ins, lnd = refs[:n], refs[n : 2 * n]
        sems = refs[2 * n + len(after) : 2 * n + len(after) + per * ng]
        token = refs[-1]
        x, y, c = _place()
        me = _dev_index((x, y, c))
        if not sliced:
            first = 0
            for gi, size in enumerate(sizes):
                for j in range(first, first + size):
                    pltpu.make_async_copy(ins[j], lnd[j].at[me], sems[per * gi + 2].at[j - first]).start()
                first += size
        first = 0
        for gi, size in enumerate(sizes):
            for k, peer in enumerate(_push_peers(modes[gi], x, y, c)):
                for j in range(first, first + size):
                    sem = (j - first) * fan[gi] + k
                    pltpu.make_async_remote_copy(
                        src_ref=ins[j].at[_dev_index(peer)] if sliced else ins[j], dst_ref=lnd[j].at[me],
                        send_sem=sems[per * gi].at[sem], recv_sem=sems[per * gi + 1].at[sem],
                        device_id=peer, device_id_type=MESH).start()
            first += size
        token[...] = jnp.zeros_like(token)

    out_shape = []
    for size, width in zip(sizes, fan):
        out_shape += [pltpu.SemaphoreType.DMA((width * size,)), pltpu.SemaphoreType.DMA((width * size,))]
        out_shape += [] if sliced else [pltpu.SemaphoreType.DMA((size,))]
    out_shape += [pltpu.HBM(b.shape, b.dtype) for b in flat + lands]
    out_shape.append(jax.ShapeDtypeStruct((V7X_SUBLANES, V7X_LANES), F32))
    res = pl.pallas_call(
        body,
        name=name,
        out_shape=tuple(out_shape),
        in_specs=[HBM_SPEC] * (2 * n) + [ANY_SPEC] * len(after),
        out_specs=tuple([SEM_SPEC] * (per * ng) + [HBM_SPEC] * (2 * n) + [pl.BlockSpec(memory_space=pltpu.VMEM)]),
        input_output_aliases={i: per * ng + i for i in range(2 * n)},
        compiler_params=pltpu.CompilerParams(has_side_effects=EFFECT),
    )(*[pltpu.with_memory_space_constraint(b, pltpu.HBM) for b in flat + lands], *after)
    sems, thru, token = res[: per * ng], res[per * ng : per * ng + 2 * n], res[-1]
    out, first = [], 0
    for gi, size in enumerate(sizes):
        out.append((sems[per * gi], sems[per * gi + 1], list(thru[first : first + size]),
                    list(thru[n + first : n + first + size]), None if sliced else sems[per * gi + 2]))
        first += size
    return out, token


def _push_wait(started, sliced, after, name, mode="all"):
    send_sems, recv_sems, bufs, lands, own_sems = started
    n = len(bufs)
    fan = len(_push_peers(mode, 0, 0, 0))
    own = [] if own_sems is None else [own_sems]

    def body(*refs):
        ins, lnd = refs[:n], refs[n : 2 * n]
        send_ref, recv_ref = refs[2 * n], refs[2 * n + 1]
        x, y, c = _place()
        for k, peer in enumerate(_push_peers(mode, x, y, c)):
            for j in range(n):
                cp = pltpu.make_async_remote_copy(
                    src_ref=ins[j].at[_dev_index(peer)] if sliced else ins[j], dst_ref=lnd[j].at[_dev_index(peer)],
                    send_sem=send_ref.at[j * fan + k], recv_sem=recv_ref.at[j * fan + k],
                    device_id=peer, device_id_type=MESH)
                cp.wait_send()
                cp.wait_recv()
        if own:
            for j in range(n):
                pltpu.make_async_copy(ins[j], lnd[j].at[_dev_index((x, y, c))], refs[2 * n + 2].at[j]).wait()

    res = pl.pallas_call(
        body,
        name=name,
        out_shape=tuple(pltpu.HBM(b.shape, b.dtype) for b in bufs + lands),
        in_specs=[HBM_SPEC] * (2 * n) + [SEM_SPEC] * (2 + len(own)) + [pl.BlockSpec(memory_space=pl.ANY)],
        out_specs=tuple([HBM_SPEC] * (2 * n)),
        input_output_aliases={i: i for i in range(2 * n)},
        compiler_params=pltpu.CompilerParams(has_side_effects=EFFECT),
    )(*bufs, *lands, send_sems, recv_sems, *own, after)
    return list(res[:n]), list(res[n:])


def _forward_copies(lnd, send_ref, recv_ref, incoming):
    x, y, c = _place()
    copies = []
    for k, chip in enumerate([(1 - x, y), (x, 1 - y), (1 - x, 1 - y)]):
        mine, theirs = _dev_index((*chip, c)), _dev_index((*chip, 1 - c))
        for j, ref in enumerate(lnd):
            copies.append(pltpu.make_async_remote_copy(
                src_ref=ref.at[mine], dst_ref=ref.at[theirs if incoming else mine],
                send_sem=send_ref.at[j * 3 + k], recv_sem=recv_ref.at[j * 3 + k],
                device_id=(x, y, 1 - c), device_id_type=MESH))
    return copies


def _forward_start(lands, name):
    n = len(lands)

    def body(*refs):
        for cp in _forward_copies(refs[:n], refs[n], refs[n + 1], False):
            cp.start()

    res = pl.pallas_call(
        body,
        name=name,
        out_shape=(pltpu.SemaphoreType.DMA((3 * n,)), pltpu.SemaphoreType.DMA((3 * n,)),
                   *[pltpu.HBM(b.shape, b.dtype) for b in lands]),
        in_specs=[HBM_SPEC] * n,
        out_specs=(SEM_SPEC, SEM_SPEC, *[HBM_SPEC] * n),
        input_output_aliases={i: 2 + i for i in range(n)},
        compiler_params=pltpu.CompilerParams(has_side_effects=EFFECT),
    )(*[pltpu.with_memory_space_constraint(b, pltpu.HBM) for b in lands])
    return res[0], res[1], list(res[2:])


def _forward_wait(started, after, name):
    send_sems, recv_sems, lands = started
    n = len(lands)

    def body(*refs):
        for cp in _forward_copies(refs[:n], refs[n], refs[n + 1], True):
            cp.wait_send()
            cp.wait_recv()

    res = pl.pallas_call(
        body,
        name=name,
        out_shape=tuple(pltpu.HBM(b.shape, b.dtype) for b in lands),
        in_specs=[HBM_SPEC] * n + [SEM_SPEC, SEM_SPEC, pl.BlockSpec(memory_space=pl.ANY)],
        out_specs=tuple([HBM_SPEC] * n),
        input_output_aliases={i: i for i in range(n)},
        compiler_params=pltpu.CompilerParams(has_side_effects=EFFECT),
    )(*lands, send_sems, recv_sems, after)
    return list(res)


def _cols_full(g):
    return jnp.transpose(g, (1, 0, 2)).reshape(g.shape[1], -1)


def _rows_full(g):
    return g.reshape(-1, g.shape[2])


def _cols_parts(full, n=N_DEV):
    r = full.shape[0]
    return jnp.transpose(full.reshape(r, n, -1), (1, 0, 2)).astype(BF16)


def _rows_parts(full):
    return full.reshape(N_DEV, -1, full.shape[1]).astype(BF16)


def _block_diag(w):
    eye = jnp.eye(LRU_BLOCKS, dtype=w.dtype)
    return jnp.einsum("nkj,nm->nkmj", w, eye).reshape(LRU_WIDTH, LRU_WIDTH)


def _pad_rows(v, rows):
    flat = v.reshape(-1)
    return jnp.pad(flat, (0, rows * D_MODEL - flat.shape[0])).reshape(rows, D_MODEL)


def _my_cols(full, me, width):
    return lax.dynamic_slice_in_dim(full, me * width, width, axis=full.ndim - 1)


def kernel(x, c, w_ada, b_ada, norm_pre, norm_post, ffn1_w_gu, ffn1_w_down, w_in, rel_bias, conv_w, conv_b, lru_wa, lru_ba, lru_wx, lru_bx, lru_lambda, w_att_o, w_rec_o, w_out, ffn2_w_gu, ffn2_w_down, loss_target, m_w_ada, m_b_ada, m_norm_pre, m_norm_post, m_ffn1_w_gu, m_ffn1_w_down, m_w_in, m_rel_bias, m_conv_w, m_conv_b, m_lru_wa, m_lru_ba, m_lru_wx, m_lru_bx, m_lru_lambda, m_w_att_o, m_w_rec_o, m_w_out, m_ffn2_w_gu, m_ffn2_w_down, v_w_ada, v_b_ada, v_norm_pre, v_norm_post, v_ffn1_w_gu, v_ffn1_w_down, v_w_in, v_rel_bias, v_conv_w, v_conv_b, v_lru_wa, v_lru_ba, v_lru_wx, v_lru_bx, v_lru_lambda, v_w_att_o, v_w_rec_o, v_w_out, v_ffn2_w_gu, v_ffn2_w_down):
    weights = dict(w_ada=w_ada, b_ada=b_ada, norm_pre=norm_pre, norm_post=norm_post, ffn1_w_gu=ffn1_w_gu,
                   ffn1_w_down=ffn1_w_down, w_in=w_in, rel_bias=rel_bias, conv_w=conv_w, conv_b=conv_b,
                   lru_wa=lru_wa, lru_ba=lru_ba, lru_wx=lru_wx, lru_bx=lru_bx, lru_lambda=lru_lambda,
                   w_att_o=w_att_o, w_rec_o=w_rec_o, w_out=w_out, ffn2_w_gu=ffn2_w_gu, ffn2_w_down=ffn2_w_down)
    mom1 = dict(w_ada=m_w_ada, b_ada=m_b_ada, norm_pre=m_norm_pre, norm_post=m_norm_post, ffn1_w_gu=m_ffn1_w_gu,
                ffn1_w_down=m_ffn1_w_down, w_in=m_w_in, rel_bias=m_rel_bias, conv_w=m_conv_w, conv_b=m_conv_b,
                lru_wa=m_lru_wa, lru_ba=m_lru_ba, lru_wx=m_lru_wx, lru_bx=m_lru_bx, lru_lambda=m_lru_lambda,
                w_att_o=m_w_att_o, w_rec_o=m_w_rec_o, w_out=m_w_out, ffn2_w_gu=m_ffn2_w_gu, ffn2_w_down=m_ffn2_w_down)
    mom2 = dict(w_ada=v_w_ada, b_ada=v_b_ada, norm_pre=v_norm_pre, norm_post=v_norm_post, ffn1_w_gu=v_ffn1_w_gu,
                ffn1_w_down=v_ffn1_w_down, w_in=v_w_in, rel_bias=v_rel_bias, conv_w=v_conv_w, conv_b=v_conv_b,
                lru_wa=v_lru_wa, lru_ba=v_lru_ba, lru_wx=v_lru_wx, lru_bx=v_lru_bx, lru_lambda=v_lru_lambda,
                w_att_o=v_w_att_o, w_rec_o=v_w_rec_o, w_out=v_w_out, ffn2_w_gu=v_ffn2_w_gu, ffn2_w_down=v_ffn2_w_down)
    order = list(weights)
    big = ["ffn1_w_gu", "ffn1_w_down", "w_in", "w_att_o", "w_rec_o", "w_out", "ffn2_w_gu", "ffn2_w_down"]
    col_sharded = {"ffn1_w_gu", "w_in", "w_att_o", "ffn2_w_gu"}
    small = ["b_ada", "norm_pre", "norm_post", "rel_bias", "conv_w", "conv_b", "lru_wa", "lru_ba", "lru_wx",
             "lru_bx", "lru_lambda"]

    xi, yi, ci = _place()
    me = _dev_index((xi, yi, ci))
    x0 = x[0]
    target = loss_target[0]
    fuse_tm = min(FUSE_TM, x0.shape[0])

    transposed = {"ffn1_w_gu", "w_in", "ffn2_w_gu"}
    local = lambda n, arr: jnp.transpose(arr[0]) if n in transposed else arr[0]
    shards = {n: local(n, weights[n]).astype(BF16) for n in big}
    full_of = lambda n, g: _cols_full(g) if n == "w_att_o" else _rows_full(g)

    pack = jnp.concatenate([c.reshape(-1), norm_pre.reshape(-1), norm_post.reshape(-1), conv_w.reshape(-1)])
    pack = jnp.pad(pack, (0, 3072 - pack.shape[0])).reshape(8, 384)
    got = _allgather_vmem(pack, "gather_small_inputs").reshape(N_DEV, 3072)
    c_all = got[:, :1024]
    unshard = lambda blk, rows: jnp.transpose(blk.reshape(N_DEV, rows, 128), (1, 0, 2)).reshape(rows, D_MODEL)
    g_pre = unshard(got[:, 1024:1408], 3)
    g_post = unshard(got[:, 1408:1792], 3)
    conv_taps = unshard(got[:, 1792:2304], 4)
    conv_w8 = jnp.concatenate([conv_taps, jnp.zeros((4, LRU_WIDTH), F32)], axis=0)

    mod_cols = _ada_fwd(c_all, w_ada[0], "ada_fwd")
    mod_all = _allgather_vmem(mod_cols, "gather_mod").reshape(N_DEV, N_DEV, 1152)
    mod = lax.dynamic_index_in_dim(mod_all, me, axis=1, keepdims=False).reshape(1, -1) + b_ada
    mod = mod.reshape(3, 3, 1, D_MODEL)

    w_slab = _slab_weights(lru_wa[0], lru_wx[0])
    bias = _bias_tile(rel_bias[0], "bias_tile")

    res_w = (0.5, 1.0, 0.5)
    row = lambda v: v.reshape(1, -1)

    (w1_gu,) = _allgather_hbm([shards["ffn1_w_gu"]], "gather_ffn1_w_gu")
    weight_groups = [["ffn1_w_down"], ["w_in"], ["w_att_o", "w_rec_o", "w_out"], ["ffn2_w_gu", "ffn2_w_down"]]
    weight_modes = ["all", "chip", "all", "all"]
    weights_started, started = _push_start([[shards[n] for n in g] for g in weight_groups], False,
                                           "gather_weights_start", after=(mod, w1_gu), modes=weight_modes)
    full = {"ffn1_w_gu": _rows_full(w1_gu)}

    def gathered_group(gi, after):
        sent, lands = _push_wait(weights_started[gi], False, after, f"gather_weights_wait{gi}", mode=weight_modes[gi])
        if weight_modes[gi] == "chip":
            lands = _forward_wait(_forward_start(lands, f"gather_weights_forward{gi}"), sent[0],
                                  f"gather_weights_forward_wait{gi}")
        for n, land in zip(weight_groups[gi], lands):
            full[n] = full_of(n, land)

    def ffn_fwd(xin, k, gi, tag, deps=(), target=None):
        h, a, g, u = _pre_up(xin, row(g_pre[k]), mod[k, 0], mod[k, 1], full[f"{tag}_w_gu"], f"{tag}_up", deps=deps)
        if f"{tag}_w_down" not in full:
            gathered_group(gi, a)
        f, *out = _matmul_post(a, full[f"{tag}_w_down"], xin, row(g_post[k]), mod[k, 2], res_w[k], f"{tag}_down",
                               target=target)
        return (out[0] if target is None else out), (h, g, u, a, f)

    x1, saved1 = ffn_fwd(x0, 0, 0, "ffn1", deps=(started,))

    gathered_group(1, x1)
    h2, proj = _pre_matmul(x1, row(g_pre[1]), mod[1, 0], mod[1, 1], full["w_in"], "mix_in",
                           b_shift=3 * ATT_WIDTH // 512)
    att_o = _attn_fwd(proj, bias, "attn_fwd")
    gathered_group(2, att_o)
    xc, pre, a_t, u_t = _lru_front(proj, conv_w8, conv_b, w_slab, lru_ba, lru_bx, lru_lambda, "lru_front")
    hs, h_prev, rec_in = _scan_fwd(a_t, u_t, proj, "lru_scan")
    att = _matmul(att_o, full["w_att_o"], "nn", F32, "att_out")
    rec = _matmul(rec_in, full["w_rec_o"], "nn", F32, "rec_out")
    merged, f2, x2 = _merge_matmul_post(att, rec, proj, full["w_out"], x1, row(g_post[1]), mod[1, 2], res_w[1],
                                        "mix_out")

    gathered_group(3, x2)
    (dy, sq), saved3 = ffn_fwd(x2, 2, 2, "ffn2", target=target)
    loss = lax.psum(0.5 * jnp.sum(sq) / D_MODEL, ("x", "y", "c"))

    grads = {}
    norm_sums = [None] * 6

    pending = []

    def exchange_start(names, tag, after=()):
        send = [(_cols_parts if n == "w_att_o" else _rows_parts)(grads[n]) for n in names]
        (group,), token = _push_start([send], True, f"exchange_{tag}_start", after=after)
        pending.append((names, send, group, tag))
        return token

    def exchange_finish(names, send, group, tag, after):
        sent, lands = _push_wait(group, True, after, f"exchange_{tag}_wait")
        res = None
        for n, land, mine in zip(names, lands, sent):
            res = _adamw_parts(land, mine, me, local(n, weights[n]), local(n, mom1[n]), local(n, mom2[n]),
                               f"adamw_{n}")
            back = (lambda r: jnp.transpose(r)) if n in transposed else (lambda r: r)
            out_g[n], out_d[n], out_m[n], out_v[n] = [back(r).reshape(weights[n].shape) for r in res]
        return res[0]

    out_g, out_d, out_m, out_v = {}, {}, {}, {}

    def ffn_bwd(xin, k, saved, dout, tag):
        h, g, u, a, f = saved
        w_gu, w_down = f"{tag}_w_gu", f"{tag}_w_down"
        df, dgu, norm_sums[2 * k + 1] = _post_bwd_up_bwd(f, dout, row(g_post[k]), mod[k, 2], res_w[k], full[w_down],
                                                          g, u, f"{tag}_up_bwd")
        grads[w_down] = _matmul(a, df, "tn", BF16, f"{tag}_dw_down", tm=1408, tn=1024, tk=DW_TK)
        started = exchange_start([w_down], w_down)
        grads[w_gu] = _dw_gu(dgu, h, f"{tag}_dw_gu", deps=(started,))
        started = exchange_start([w_gu], w_gu)
        halves = [(dgu, (None, fuse_tm, D_FF), lambda i, half=half: (half, i, 0), (half * D_FF, (half + 1) * D_FF))
                  for half in range(2)]
        dx, norm_sums[2 * k] = _matmul_pre_bwd(halves, full[w_gu], xin, dout, row(g_pre[k]),
                                                                mod[k, 0], mod[k, 1], f"{tag}_dh", deps=(started,))
        return dx

    dx2 = ffn_bwd(x2, 2, saved3, dy, "ffn2")

    df2, datt, drec, dg_att, dg_rec, norm_sums[3] = _post_bwd_merge_bwd(
        f2, dx2, row(g_post[1]), mod[1, 2], res_w[1], full["w_out"], att, rec, proj, "mix_dmerged")
    grads["w_out"] = _matmul(merged, df2, "tn", BF16, "mix_dw_out", tm=1024, tn=1024, tk=DW_TK)
    datt_o = _matmul(datt, full["w_att_o"], "nt", BF16, "att_out_bwd")
    grads["w_att_o"] = _matmul(att_o, datt, "tn", BF16, "dw_att_o", tm=512, tn=1024, tk=DW_TK)
    grads["w_rec_o"] = _matmul(rec_in, drec, "tn", BF16, "dw_rec_o", tm=1024, tn=1024, tk=DW_TK)
    started = exchange_start(["w_out", "w_att_o", "w_rec_o"], "mix_out")
    dhs, dyr = _matmul_recin_bwd(drec, full["w_rec_o"], hs, proj, "rec_out_bwd", deps=(started,))
    g_t = _scan_bwd(a_t, dhs, "lru_scan_bwd")
    dpre, dxc, lru_sums = _lru_back(pre, xc, w_slab, lru_ba, lru_bx, lru_lambda, g_t, h_prev, "lru_back")
    dxr, conv_sums = _conv_bwd(proj, conv_w8, dxc, "conv_bwd")
    dq, dk, dv, dbias = _attn_bwd(proj, bias, datt_o, "attn_bwd")
    dproj = jnp.concatenate([dq, dk, dv, dxr, dyr, dg_att, dg_rec], axis=1)
    grads["w_in"] = _matmul(dproj, h2, "tn", BF16, "mix_dw_in", tm=1408, tn=1024, tk=DW_TK)
    pack_mix = jnp.concatenate([conv_sums, lru_sums, _pad_rows(_bias_grad(dbias, "bias_grad"), V7X_SUBLANES),
                                _lru_dw(xc, dpre, "lru_dw").reshape(128, D_MODEL)], axis=0)
    (mix_started,), started = _push_start([[pack_mix]], False, "small_grads_mix_start")
    started = exchange_start(["w_in"], "w_in", after=(started,))
    whole = [(dproj, (fuse_tm, PROJ_WIDTH), lambda i: (i, 0), (0, PROJ_WIDTH))]
    dx1, norm_sums[2] = _matmul_pre_bwd(whole, full["w_in"], x1, dx2, row(g_pre[1]), mod[1, 0],
                                                             mod[1, 1], "mix_dh", deps=(started,))

    dx0 = ffn_bwd(x0, 0, saved1, dx1, "ffn1")

    pack_norm = jnp.concatenate(norm_sums, axis=0)
    (norm_started,), _ = _push_start([[pack_norm]], False, "small_grads_norm_start")

    def summed(started, pack, after, tag):
        _, (parts,) = _push_wait(started, False, after, f"small_grads_{tag}_wait")
        return parts, _sum_parts(parts, f"small_grads_{tag}_sum")

    done = dx0
    last = pending[-1:]
    for names, send, group, tag in pending[:-1]:
        done = exchange_finish(names, send, group, tag, done)

    _, total = summed(mix_started, pack_mix, done, "mix")
    grads["conv_w"] = _my_cols(total[0:4], me, 128)
    grads["conv_b"] = total[4:5]
    grads["lru_ba"] = total[8:9]
    grads["lru_bx"] = total[9:10]
    grads["lru_lambda"] = total[10:11]
    grads["rel_bias"] = total[16:19].reshape(-1)[: ATT_HEADS * (2 * MAX_REL + 1)].reshape(ATT_HEADS, -1)
    grads["lru_wa"] = total[24:88].reshape(LRU_BLOCKS, LRU_BLOCK, LRU_BLOCK)
    grads["lru_wx"] = total[88:152].reshape(LRU_BLOCKS, LRU_BLOCK, LRU_BLOCK)
    parts, total = summed(norm_started, pack_norm, total, "norm")
    by_sandwich = lambda v: v.reshape(*v.shape[:-2], 3, 2 * V7X_SUBLANES, D_MODEL)
    dmod_of = lambda v: jnp.concatenate([by_sandwich(v)[..., 1:3, :], by_sandwich(v)[..., 9:10, :]], axis=-2)
    grads["b_ada"] = dmod_of(total).reshape(1, -1)
    grads["norm_pre"] = _my_cols(by_sandwich(total)[:, 0, :], me, 128)
    grads["norm_post"] = _my_cols(by_sandwich(total)[:, V7X_SUBLANES, :], me, 128)
    dmod_all = dmod_of(parts).reshape(N_DEV, 9 * D_MODEL)
    grads["w_ada"] = _ada_bwd(c_all, _my_cols(dmod_all, me, 1152), "ada_bwd")

    res = _adamw(grads["w_ada"], w_ada[0], m_w_ada[0], v_w_ada[0], "adamw_w_ada")
    out_g["w_ada"], out_d["w_ada"], out_m["w_ada"], out_v["w_ada"] = [r.reshape(w_ada.shape) for r in res]

    sizes = [int(np.prod(weights[n].shape)) for n in small]
    tot = sum(sizes)
    rows_small = -(-tot // (16 * D_MODEL)) * 16
    flat = lambda arrs: jnp.pad(jnp.concatenate([a.reshape(-1) for a in arrs]),
                                (0, rows_small * D_MODEL - tot)).reshape(rows_small, D_MODEL)
    res = _adamw(flat([grads[n] for n in small]), flat([weights[n] for n in small]),
                 flat([mom1[n] for n in small]), flat([mom2[n] for n in small]), "adamw_small", rows=rows_small)
    offs = np.cumsum([0] + sizes)
    for dst, r in zip((out_g, out_d, out_m, out_v), res):
        rf = r.reshape(-1)
        for i, n in enumerate(small):
            dst[n] = rf[offs[i] : offs[i + 1]].reshape(weights[n].shape)

    done = res[0]
    for names, send, group, tag in last:
        done = exchange_finish(names, send, group, tag, done)

    return (loss, dx0[None], *[out_g[n] for n in order], *[out_d[n] for n in order],
            *[out_m[n] for n in order], *[out_v[n] for n in order])
```

```python
import functools

import jax
import jax.numpy as jnp
import numpy as np
from jax import lax
from jax.experimental import pallas as pl
from jax.experimental.pallas import tpu as pltpu

D_MODEL = 1024
D_FF = 2816
ATT_HEADS = 8
ATT_HEAD_DIM = 64
ATT_WIDTH = 512
CHUNK = 64
LEFT_CHUNKS = 8
MAX_REL = 128
LRU_WIDTH = 1024
LRU_BLOCKS = 16
LRU_BLOCK = 64
LRU_C = 8.0
EPS = 1e-6
PROJ_WIDTH = 5632
N_DEV = 8

ADAM_LR = 0.001
ADAM_B1 = 0.9
ADAM_B2 = 0.999
ADAM_EPS = 1e-08
ADAM_WD = 0.01
ADAM_STEP = 10

V7X_LANES = 128
V7X_SUBLANES = 8
V7X_VMEM_BYTES = 64 * 1024 * 1024
VMEM_LIMIT = V7X_VMEM_BYTES - 8 * 1024 * 1024

ATT_TQ = 256
NEG = -1e30
BF16 = jnp.bfloat16
F32 = jnp.float32
MESH = pl.DeviceIdType.MESH

OFF_Q = 4 * LRU_WIDTH
OFF_K = OFF_Q + ATT_WIDTH
OFF_V = OFF_K + ATT_WIDTH


def _cparams(**kw):
    return pltpu.CompilerParams(vmem_limit_bytes=VMEM_LIMIT, **kw)


def _pick(n, target, unit=V7X_LANES):
    best = None
    for t in range(unit, min(n, target) + 1, unit):
        if n % t == 0:
            best = t
    return n if best is None else best


_DIMS = {
    "nn": (((1,), (0,)), ((), ())),
    "nt": (((1,), (1,)), ((), ())),
    "tn": (((0,), (0,)), ((), ())),
}


ANY_SPEC = pl.BlockSpec(memory_space=pl.ANY)


def _matmul(a, b, mode, out_dtype, name, tm=1024, tn=512, tk=1408, deps=(), b_shift=0):
    n_deps = len(deps)
    if mode == "nn":
        (m, k), (k2, n) = a.shape, b.shape
    elif mode == "nt":
        (m, k), (n, k2) = a.shape, b.shape
    else:
        (k, m), (k2, n) = a.shape, b.shape
    assert k == k2, (a.shape, b.shape, mode)
    tm, tn, tk = _pick(m, tm), _pick(n, tn), _pick(k, tk)
    nk = k // tk
    dims = _DIMS[mode]

    def body(a_ref, b_ref, *rest):
        o_ref, scratch = rest[n_deps], rest[n_deps + 1 :]
        p = lax.dot_general(a_ref[...], b_ref[...], dims, preferred_element_type=F32)
        if nk == 1:
            o_ref[...] = p.astype(o_ref.dtype)
        else:
            acc = scratch[0]
            kk = pl.program_id(2)

            @pl.when(kk == 0)
            def _():
                acc[...] = p

            @pl.when(kk > 0)
            def _():
                acc[...] += p

            @pl.when(kk == nk - 1)
            def _():
                o_ref[...] = acc[...].astype(o_ref.dtype)

    if mode == "nn":
        a_spec = pl.BlockSpec((tm, tk), lambda i, j, kk: (i, kk))
        b_spec = pl.BlockSpec((tk, tn), lambda i, j, kk: (kk, j))
    elif mode == "nt":
        a_spec = pl.BlockSpec((tm, tk), lambda i, j, kk: (i, kk))
        b_spec = pl.BlockSpec((tn, tk), lambda i, j, kk: ((j + b_shift) % (n // tn), kk))
    else:
        a_spec = pl.BlockSpec((tk, tm), lambda i, j, kk: (kk, i))
        b_spec = pl.BlockSpec((tk, tn), lambda i, j, kk: (kk, j))
    return pl.pallas_call(
        body,
        name=name,
        grid=(m // tm, n // tn, nk),
        in_specs=[a_spec, b_spec] + [ANY_SPEC] * n_deps,
        out_specs=pl.BlockSpec((tm, tn), lambda i, j, kk: (i, j)),
        out_shape=jax.ShapeDtypeStruct((m, n), out_dtype),
        scratch_shapes=[pltpu.VMEM((tm, tn), F32)] if nk > 1 else [],
        compiler_params=_cparams(dimension_semantics=("parallel", "parallel", "arbitrary")),
    )(a, b, *deps)


def _rowwise(fn, name, params, tiles, outs, accs=(), ts=256, with_index=False, deps=()):
    norm = []
    for t in tiles:
        if not isinstance(t, tuple):
            t = (t, t.shape[1], 0)
        norm.append(t if len(t) == 4 else (*t, None))
    s = norm[0][0].shape[0]
    ts = min(ts, s)
    assert s % ts == 0 and ts % V7X_SUBLANES == 0
    steps = s // ts
    halo_blocks = ts // V7X_SUBLANES
    n_p, n_t, n_o = len(params), len(norm), len(outs)

    def body(*refs):
        i = pl.program_id(0)
        vals = [r[...] for r in refs[: n_p + n_t]]
        res = fn(i, steps, *vals) if with_index else fn(*vals)
        if not isinstance(res, (tuple, list)):
            res = (res,)
        first_out = n_p + n_t + len(deps)
        o_refs = refs[first_out : first_out + n_o]
        a_refs = refs[first_out + n_o :]
        for r, v in zip(o_refs, res[:n_o]):
            r[...] = v.astype(r.dtype)
        for r, v in zip(a_refs, res[n_o:]):
            _accumulate(r, v, i)

    in_specs = [pl.BlockSpec(p.shape, lambda i: (0, 0)) for p in params]
    for arr, w, cb, halo in norm:
        if halo is None:
            in_specs.append(pl.BlockSpec((ts, w), lambda i, cb=cb: (i, cb)))
        elif halo == "prev":
            in_specs.append(
                pl.BlockSpec((V7X_SUBLANES, w), lambda i, cb=cb: (jnp.maximum(i * halo_blocks - 1, 0), cb))
            )
        else:
            last = s // V7X_SUBLANES - 1
            in_specs.append(
                pl.BlockSpec((V7X_SUBLANES, w), lambda i, cb=cb: (jnp.minimum((i + 1) * halo_blocks, last), cb))
            )
    in_specs += [ANY_SPEC] * len(deps)
    out_specs = [pl.BlockSpec((ts, w), lambda i: (i, 0)) for w, _ in outs]
    out_specs += [pl.BlockSpec(shape, lambda i: (0, 0)) for shape in accs]
    out_shape = [jax.ShapeDtypeStruct((s, w), dt) for w, dt in outs]
    out_shape += [jax.ShapeDtypeStruct(shape, F32) for shape in accs]
    res = pl.pallas_call(
        body,
        name=name,
        grid=(steps,),
        in_specs=in_specs,
        out_specs=out_specs,
        out_shape=out_shape,
        compiler_params=_cparams(dimension_semantics=("arbitrary",)),
    )(*params, *[t[0] for t in norm], *deps)
    return res


def _accumulate(ref, val, step):
    @pl.when(step == 0)
    def _():
        ref[...] = val

    @pl.when(step > 0)
    def _():
        ref[...] += val


def _sigmoid(z):
    return jax.nn.sigmoid(z)


def _silu(z):
    return z * _sigmoid(z)


def _gelu(z):
    return 0.5 * z * (1.0 + jnp.tanh(0.7978845608028654 * (z + 0.044715 * (z * z * z))))


def _pre_fn(g, shift, scale, x):
    r = lax.rsqrt(jnp.mean(x * x, axis=-1, keepdims=True) + EPS)
    return ((x * r) * g) * (1.0 + scale) + shift


def _post_fn(res_w, g, gate, f, x):
    r = lax.rsqrt(jnp.mean(f * f, axis=-1, keepdims=True) + EPS)
    return x + (res_w * gate) * ((f * r) * g)


def _swiglu_fn(gu):
    return _silu(gu[:, :D_FF]) * gu[:, D_FF:]


def _gates_fn(ba, bx, lam, pre, xc):
    ra = _sigmoid(pre[:, :LRU_WIDTH] + ba)
    ia = _sigmoid(pre[:, LRU_WIDTH:] + bx)
    softplus = jnp.maximum(-lam, 0.0) + jnp.log1p(jnp.exp(-jnp.abs(lam)))
    log_a = (-LRU_C) * ra * softplus
    a = jnp.exp(log_a)
    mult = jnp.sqrt(-jnp.tanh(log_a) * (a * a + 1.0))
    return a, mult * (ia * xc)


def _recin_fn(hs, yr):
    return hs * _gelu(yr)


def _merge_fn(att, rec, g_att, g_rec):
    return _sigmoid(g_att) * att + _sigmoid(g_rec) * rec


def _rowsum(v):
    return jnp.sum(v, axis=0, keepdims=True)


def _pre_fwd(x, g, shift, scale, name, deps=()):
    (h,) = _rowwise(_pre_fn, name, [g, shift, scale], [x], [(D_MODEL, BF16)], deps=deps)
    return h


def _pre_bwd(x, g, shift, scale, dh, dres, name):
    def fn(g, shift, scale, x, dh, dres):
        _, vjp = jax.vjp(_pre_fn, g, shift, scale, x)
        dg, dshift, dscale, dx = vjp(dh)
        return dx + dres, dg, dshift, dscale

    row = (1, D_MODEL)
    return _rowwise(fn, name, [g, shift, scale], [x, dh, dres], [(D_MODEL, F32)], [row, row, row])


def _post_fwd(f, x, g, gate, res_w, name):
    (y,) = _rowwise(functools.partial(_post_fn, res_w), name, [g, gate], [f, x], [(D_MODEL, F32)])
    return y


def _post_bwd(f, g, gate, res_w, dy, name, deps=()):
    def fn(g, gate, f, dy):
        _, vjp = jax.vjp(lambda g, gate, f: _post_fn(res_w, g, gate, f, 0.0), g, gate, f)
        dg, dgate, df = vjp(dy)
        return df, dg, dgate

    row = (1, D_MODEL)
    return _rowwise(fn, name, [g, gate], [f, dy], [(D_MODEL, BF16)], [row, row], deps=deps)


def _loss_stage(y, target, name):
    def fn(y, t):
        diff = y - t
        return diff * (1.0 / D_MODEL), _rowsum(diff * diff)

    return _rowwise(fn, name, [], [y, target], [(D_MODEL, F32)], [(1, D_MODEL)])


FFN_TM = 512
FFN_TF = 1408


def _glu_fn(g, u):
    return _silu(g) * u


def _ffn_up(h, w_gu_t, name):
    s = h.shape[0]
    tm = min(FFN_TM, s)
    nf = D_FF // FFN_TF

    def body(h_ref, wg_ref, wu_ref, a_ref, g_ref, u_ref):
        hv = h_ref[...]
        g = lax.dot_general(hv, wg_ref[...], _DIMS["nt"], preferred_element_type=F32)
        u = lax.dot_general(hv, wu_ref[...], _DIMS["nt"], preferred_element_type=F32)
        a_ref[...] = _glu_fn(g, u).astype(a_ref.dtype)
        g_ref[...] = g.astype(g_ref.dtype)
        u_ref[...] = u.astype(u_ref.dtype)

    out = pl.BlockSpec((tm, FFN_TF), lambda i, j: (i, j))
    return pl.pallas_call(
        body,
        name=name,
        grid=(s // tm, nf),
        in_specs=[pl.BlockSpec((tm, D_MODEL), lambda i, j: (i, 0)),
                  pl.BlockSpec((FFN_TF, D_MODEL), lambda i, j: (j, 0)),
                  pl.BlockSpec((FFN_TF, D_MODEL), lambda i, j: (nf + j, 0))],
        out_specs=[out, out, out],
        out_shape=[jax.ShapeDtypeStruct((s, D_FF), BF16)] * 3,
        compiler_params=_cparams(dimension_semantics=("parallel", "arbitrary")),
    )(h, w_gu_t, w_gu_t)


def _ffn_up_bwd(df, w_down, g, u, name, deps=()):
    s = df.shape[0]
    tm = min(FFN_TM, s)

    def body(df_ref, wd_ref, g_ref, u_ref, *rest):
        dg_ref, du_ref = rest[len(deps) :]
        da = lax.dot_general(df_ref[...], wd_ref[...], _DIMS["nt"], preferred_element_type=F32)
        _, vjp = jax.vjp(_glu_fn, g_ref[...].astype(F32), u_ref[...].astype(F32))
        dg, du = vjp(da)
        dg_ref[...] = dg.astype(dg_ref.dtype)
        du_ref[...] = du.astype(du_ref.dtype)

    blk = pl.BlockSpec((tm, FFN_TF), lambda i, j: (i, j))
    return pl.pallas_call(
        body,
        name=name,
        grid=(s // tm, D_FF // FFN_TF),
        in_specs=[pl.BlockSpec((tm, D_MODEL), lambda i, j: (i, 0)),
                  pl.BlockSpec((FFN_TF, D_MODEL), lambda i, j: (j, 0)), blk, blk] + [ANY_SPEC] * len(deps),
        out_specs=[blk, blk],
        out_shape=[jax.ShapeDtypeStruct((s, D_FF), BF16)] * 2,
        compiler_params=_cparams(dimension_semantics=("parallel", "arbitrary")),
    )(df, w_down, g, u, *deps)


def _ffn_dh(dg, du, w_gu_t, name, deps=()):
    s = dg.shape[0]
    tm, tn = min(FFN_TM, s), 512

    def body(dg_ref, du_ref, wg_ref, wu_ref, *rest):
        o_ref = rest[len(deps)]
        p = jnp.dot(dg_ref[...], wg_ref[...], preferred_element_type=F32)
        o_ref[...] = p + jnp.dot(du_ref[...], wu_ref[...], preferred_element_type=F32)

    a_spec = pl.BlockSpec((tm, D_FF), lambda i, j: (i, 0))
    return pl.pallas_call(
        body,
        name=name,
        grid=(s // tm, D_MODEL // tn),
        in_specs=[a_spec, a_spec,
                  pl.BlockSpec((D_FF, tn), lambda i, j: (0, j)),
                  pl.BlockSpec((D_FF, tn), lambda i, j: (1, j))] + [ANY_SPEC] * len(deps),
        out_specs=pl.BlockSpec((tm, tn), lambda i, j: (i, j)),
        out_shape=jax.ShapeDtypeStruct((s, D_MODEL), F32),
        compiler_params=_cparams(dimension_semantics=("parallel", "arbitrary")),
    )(dg, du, w_gu_t, w_gu_t, *deps)


FUSE_TM = 256
DW_TK = 2048
ROW_SPEC2 = pl.BlockSpec((1, D_MODEL), lambda i, j: (0, 0))
ROW_SPEC1 = pl.BlockSpec((1, D_MODEL), lambda i: (0, 0))
SUMS_SPEC1 = pl.BlockSpec((V7X_SUBLANES, D_MODEL), lambda i: (0, 0))
SUMS_SPEC2 = pl.BlockSpec((V7X_SUBLANES, D_MODEL), lambda i, j: (0, 0))
SUMS_SHAPE = jax.ShapeDtypeStruct((V7X_SUBLANES, D_MODEL), F32)


def _sum_rows(*rows):
    pad = jnp.zeros((V7X_SUBLANES - len(rows), rows[0].shape[1]), F32)
    return jnp.concatenate([*rows, pad], axis=0)


def _pre_ahead(x_ref, xn_ref, g_ref, sh_ref, sc_ref, h_ref, h_s, chunk):
    i, j = pl.program_id(0), pl.program_id(1)
    pre = lambda rows: _pre_fn(g_ref[...], sh_ref[...], sc_ref[...], rows).astype(BF16)

    @pl.when((i == 0) & (j == 0))
    def _():
        h_s[0] = pre(x_ref[...])

    @pl.when(j == 0)
    def _():
        h_ref[...] = h_s[i % 2]

    last = x_ref.shape[0] // chunk - 1
    rows = pl.ds(pl.multiple_of(jnp.minimum(j, last) * chunk, chunk), chunk)
    h_s[(i + 1) % 2, rows, :] = pre(xn_ref[rows, :])
    return h_s[i % 2]


def _pre_up(x, g, shift, scale, w_gu_t, name, deps=()):
    s = x.shape[0]
    tm = min(FFN_TM, s)
    nf = D_FF // FFN_TF
    nd = len(deps)
    ni = s // tm

    def body(x_ref, xn_ref, g_ref, sh_ref, sc_ref, wg_ref, wu_ref, *rest):
        h_ref, a_ref, gg_ref, u_ref, h_s = rest[nd:]
        hv = _pre_ahead(x_ref, xn_ref, g_ref, sh_ref, sc_ref, h_ref, h_s, tm // nf)
        gv = lax.dot_general(hv, wg_ref[...], _DIMS["nt"], preferred_element_type=F32)
        uv = lax.dot_general(hv, wu_ref[...], _DIMS["nt"], preferred_element_type=F32)
        a_ref[...] = _glu_fn(gv, uv).astype(a_ref.dtype)
        gg_ref[...] = gv.astype(gg_ref.dtype)
        u_ref[...] = uv.astype(u_ref.dtype)

    rows = pl.BlockSpec((tm, D_MODEL), lambda i, j: (i, 0))
    ahead = pl.BlockSpec((tm, D_MODEL), lambda i, j: (jnp.minimum(i + 1, ni - 1), 0))
    out = pl.BlockSpec((tm, FFN_TF), lambda i, j: (i, j))
    return pl.pallas_call(
        body,
        name=name,
        grid=(ni, nf),
        in_specs=[rows, ahead, ROW_SPEC2, ROW_SPEC2, ROW_SPEC2,
                  pl.BlockSpec((FFN_TF, D_MODEL), lambda i, j: (j, 0)),
                  pl.BlockSpec((FFN_TF, D_MODEL), lambda i, j: (nf + j, 0))] + [ANY_SPEC] * nd,
        out_specs=[rows, out, out, out],
        out_shape=[jax.ShapeDtypeStruct((s, D_MODEL), BF16)] + [jax.ShapeDtypeStruct((s, D_FF), BF16)] * 3,
        scratch_shapes=[pltpu.VMEM((2, tm, D_MODEL), BF16)],
        compiler_params=_cparams(dimension_semantics=("arbitrary", "arbitrary")),
    )(x, x, g, shift, scale, w_gu_t, w_gu_t, *deps)


def _pre_matmul(x, g, shift, scale, w_t, name, b_shift=0, tn=512):
    s = x.shape[0]
    n = w_t.shape[0]
    tm = min(2 * FFN_TM, s)
    ni = s // tm

    def body(x_ref, xn_ref, g_ref, sh_ref, sc_ref, w_ref, h_ref, o_ref, h_s):
        hv = _pre_ahead(x_ref, xn_ref, g_ref, sh_ref, sc_ref, h_ref, h_s, tm // 8)
        o_ref[...] = lax.dot_general(hv, w_ref[...], _DIMS["nt"], preferred_element_type=F32)

    rows = pl.BlockSpec((tm, D_MODEL), lambda i, j: (i, 0))
    ahead = pl.BlockSpec((tm, D_MODEL), lambda i, j: (jnp.minimum(i + 1, ni - 1), 0))
    return pl.pallas_call(
        body,
        name=name,
        grid=(ni, n // tn),
        in_specs=[rows, ahead, ROW_SPEC2, ROW_SPEC2, ROW_SPEC2,
                  pl.BlockSpec((tn, D_MODEL), lambda i, j: ((j + b_shift) % (n // tn), 0))],
        out_specs=[rows, pl.BlockSpec((tm, tn), lambda i, j: (i, j))],
        out_shape=[jax.ShapeDtypeStruct((s, D_MODEL), BF16), jax.ShapeDtypeStruct((s, n), F32)],
        scratch_shapes=[pltpu.VMEM((2, tm, D_MODEL), BF16)],
        compiler_params=_cparams(dimension_semantics=("arbitrary", "arbitrary")),
    )(x, x, g, shift, scale, w_t)


def _matmul_post(a, w, x, g_post, gate, res_w, name, target=None):
    s, k = a.shape
    tm = min(FFN_TM, s)
    extra = [] if target is None else [target]

    def body(a_ref, w_ref, x_ref, g_ref, gate_ref, *rest):
        f = jnp.dot(a_ref[...], w_ref[...], preferred_element_type=F32)
        y = _post_fn(res_w, g_ref[...], gate_ref[...], f, x_ref[...])
        if target is None:
            f_ref, y_ref = rest
            y_ref[...] = y
        else:
            t_ref, f_ref, dy_ref, sq_ref = rest
            diff = y - t_ref[...]
            dy_ref[...] = diff * (1.0 / D_MODEL)
            _accumulate(sq_ref, _rowsum(diff * diff), pl.program_id(0))
        f_ref[...] = f

    rows = pl.BlockSpec((tm, D_MODEL), lambda i: (i, 0))
    out_specs, out_shape = [rows, rows], [jax.ShapeDtypeStruct((s, D_MODEL), F32)] * 2
    if target is not None:
        out_specs.append(ROW_SPEC1)
        out_shape.append(jax.ShapeDtypeStruct((1, D_MODEL), F32))
    return pl.pallas_call(
        body,
        name=name,
        grid=(s // tm,),
        in_specs=[pl.BlockSpec((tm, k), lambda i: (i, 0)), pl.BlockSpec((k, D_MODEL), lambda i: (0, 0)), rows,
                  ROW_SPEC1, ROW_SPEC1] + [rows] * len(extra),
        out_specs=out_specs,
        out_shape=out_shape,
        compiler_params=_cparams(dimension_semantics=("arbitrary",)),
    )(a, w, x, g_post, gate, *extra)


def _merge_matmul_post(att, rec, proj, w, x, g_post, gate, res_w, name):
    s = att.shape[0]
    tm = min(FUSE_TM, s)

    def body(att_ref, rec_ref, ga_ref, gr_ref, w_ref, x_ref, g_ref, gate_ref, m_ref, f_ref, y_ref):
        merged = _merge_fn(att_ref[...], rec_ref[...], ga_ref[...], gr_ref[...]).astype(BF16)
        m_ref[...] = merged
        f = jnp.dot(merged, w_ref[...], preferred_element_type=F32)
        f_ref[...] = f
        y_ref[...] = _post_fn(res_w, g_ref[...], gate_ref[...], f, x_ref[...])

    rows = pl.BlockSpec((tm, D_MODEL), lambda i: (i, 0))
    return pl.pallas_call(
        body,
        name=name,
        grid=(s // tm,),
        in_specs=[rows, rows, pl.BlockSpec((tm, D_MODEL), lambda i: (i, 2)), pl.BlockSpec((tm, D_MODEL), lambda i: (i, 3)),
                  pl.BlockSpec(w.shape, lambda i: (0, 0)), rows, ROW_SPEC1, ROW_SPEC1],
        out_specs=[rows, rows, rows],
        out_shape=[jax.ShapeDtypeStruct((s, D_MODEL), BF16)] + [jax.ShapeDtypeStruct((s, D_MODEL), F32)] * 2,
        compiler_params=_cparams(dimension_semantics=("parallel",)),
    )(att, rec, proj, proj, w, x, g_post, gate)


def _post_bwd_merge_bwd(f, dy, g_post, gate, res_w, w, att, rec, proj, name):
    s = f.shape[0]
    tm = min(FUSE_TM, s)

    def body(f_ref, dy_ref, gp_ref, gate_ref, w_ref, att_ref, rec_ref, ga_ref, gr_ref,
             df_ref, datt_ref, drec_ref, dga_ref, dgr_ref, sums_ref):
        i = pl.program_id(0)
        dgp, dgate, df = _post_vjp(res_w, gp_ref[...], gate_ref[...], f_ref[...], dy_ref[...])
        dfb = df.astype(BF16)
        df_ref[...] = dfb
        _accumulate(sums_ref, _sum_rows(dgp, dgate), i)
        dmerged = lax.dot_general(dfb, w_ref[...], _DIMS["nt"], preferred_element_type=F32)
        _, vjp = jax.vjp(_merge_fn, att_ref[...], rec_ref[...], ga_ref[...], gr_ref[...])
        for ref, val in zip((datt_ref, drec_ref, dga_ref, dgr_ref), vjp(dmerged)):
            ref[...] = val.astype(ref.dtype)

    rows = pl.BlockSpec((tm, D_MODEL), lambda i: (i, 0))
    return pl.pallas_call(
        body,
        name=name,
        grid=(s // tm,),
        in_specs=[rows, rows, ROW_SPEC1, ROW_SPEC1, pl.BlockSpec(w.shape, lambda i: (0, 0)), rows, rows,
                  pl.BlockSpec((tm, D_MODEL), lambda i: (i, 2)), pl.BlockSpec((tm, D_MODEL), lambda i: (i, 3))],
        out_specs=[rows] * 5 + [SUMS_SPEC1],
        out_shape=[jax.ShapeDtypeStruct((s, D_MODEL), BF16)] * 5 + [SUMS_SHAPE],
        compiler_params=_cparams(dimension_semantics=("arbitrary",)),
    )(f, dy, g_post, gate, w, att, rec, proj, proj)


def _matmul_recin_bwd(drec, w, hs, proj, name, deps=()):
    s = drec.shape[0]
    tm = min(FUSE_TM, s)
    nd = len(deps)

    def body(d_ref, w_ref, hs_ref, yr_ref, *rest):
        dhs_ref, dyr_ref = rest[nd:]
        d = lax.dot_general(d_ref[...], w_ref[...], _DIMS["nt"], preferred_element_type=F32)
        _, vjp = jax.vjp(_recin_fn, hs_ref[...], yr_ref[...])
        dhs, dyr = vjp(d)
        dhs_ref[...] = dhs
        dyr_ref[...] = dyr.astype(dyr_ref.dtype)

    rows = pl.BlockSpec((tm, D_MODEL), lambda i: (i, 0))
    return pl.pallas_call(
        body,
        name=name,
        grid=(s // tm,),
        in_specs=[rows, pl.BlockSpec(w.shape, lambda i: (0, 0)), rows,
                  pl.BlockSpec((tm, D_MODEL), lambda i: (i, 1))] + [ANY_SPEC] * nd,
        out_specs=[rows, rows],
        out_shape=[jax.ShapeDtypeStruct((s, D_MODEL), F32), jax.ShapeDtypeStruct((s, D_MODEL), BF16)],
        compiler_params=_cparams(dimension_semantics=("parallel",)),
    )(drec, w, hs, proj, *deps)


def _post_vjp(res_w, g, gate, f, dy):
    _, vjp = jax.vjp(lambda g, gate, f: _post_fn(res_w, g, gate, f, 0.0), g, gate, f)
    return vjp(dy)


def _post_bwd_up_bwd(f, dy, g_post, gate, res_w, w_down, g, u, name, deps=()):
    s = f.shape[0]
    tm = min(FFN_TM, s)
    nd = len(deps)

    def body(f_ref, dy_ref, gp_ref, gate_ref, wd_ref, g_ref, u_ref, *rest):
        df_ref, dgu_ref, sums_ref, df_s = rest[nd:]
        i = pl.program_id(0)

        @pl.when(pl.program_id(1) == 0)
        def _():
            dgp, dgate, df = _post_vjp(res_w, gp_ref[...], gate_ref[...], f_ref[...], dy_ref[...])
            df_s[...] = df.astype(BF16)
            df_ref[...] = df_s[...]
            _accumulate(sums_ref, _sum_rows(dgp, dgate), i)

        da = lax.dot_general(df_s[...], wd_ref[...], _DIMS["nt"], preferred_element_type=F32)
        _, vjp = jax.vjp(_glu_fn, g_ref[...].astype(F32), u_ref[...].astype(F32))
        dg, du = vjp(da)
        dgu_ref[0] = dg.astype(dgu_ref.dtype)
        dgu_ref[1] = du.astype(dgu_ref.dtype)

    rows = pl.BlockSpec((tm, D_MODEL), lambda i, j: (i, 0))
    blk = pl.BlockSpec((tm, FFN_TF), lambda i, j: (i, j))
    return pl.pallas_call(
        body,
        name=name,
        grid=(s // tm, D_FF // FFN_TF),
        in_specs=[rows, rows, ROW_SPEC2, ROW_SPEC2, pl.BlockSpec((FFN_TF, D_MODEL), lambda i, j: (j, 0)), blk,
                  blk] + [ANY_SPEC] * nd,
        out_specs=[rows, pl.BlockSpec((2, tm, FFN_TF), lambda i, j: (0, i, j)), SUMS_SPEC2],
        out_shape=[jax.ShapeDtypeStruct((s, D_MODEL), BF16), jax.ShapeDtypeStruct((2, s, D_FF), BF16), SUMS_SHAPE],
        scratch_shapes=[pltpu.VMEM((tm, D_MODEL), BF16)],
        compiler_params=_cparams(dimension_semantics=("arbitrary", "arbitrary")),
    )(f, dy, g_post, gate, w_down, g, u, *deps)


def _post_bwd_matmul(f, dy, g_post, gate, res_w, w, name):
    s = f.shape[0]
    n = w.shape[0]
    tm = min(FUSE_TM, s)

    def body(f_ref, dy_ref, gp_ref, gate_ref, w_ref, df_ref, o_ref, dgp_ref, dgate_ref):
        i = pl.program_id(0)
        dgp, dgate, df = _post_vjp(res_w, gp_ref[...], gate_ref[...], f_ref[...], dy_ref[...])
        dfb = df.astype(BF16)
        df_ref[...] = dfb
        _accumulate(dgp_ref, dgp, i)
        _accumulate(dgate_ref, dgate, i)
        o_ref[...] = lax.dot_general(dfb, w_ref[...], _DIMS["nt"], preferred_element_type=F32)

    rows = pl.BlockSpec((tm, D_MODEL), lambda i: (i, 0))
    return pl.pallas_call(
        body,
        name=name,
        grid=(s // tm,),
        in_specs=[rows, rows, ROW_SPEC1, ROW_SPEC1, pl.BlockSpec((n, D_MODEL), lambda i: (0, 0))],
        out_specs=[rows, pl.BlockSpec((tm, n), lambda i: (i, 0)), ROW_SPEC1, ROW_SPEC1],
        out_shape=[jax.ShapeDtypeStruct((s, D_MODEL), BF16), jax.ShapeDtypeStruct((s, n), F32),
                   jax.ShapeDtypeStruct((1, D_MODEL), F32), jax.ShapeDtypeStruct((1, D_MODEL), F32)],
        compiler_params=_cparams(dimension_semantics=("arbitrary",)),
    )(f, dy, g_post, gate, w)


def _matmul_pre_bwd(parts, w_t, x, dres, g, shift, scale, name, deps=()):
    s = x.shape[0]
    na, nd = len(parts), len(deps)
    ranges = [p[3] for p in parts]

    def body(*refs):
        a_refs = refs[:na]
        w_ref, x_ref, dres_ref, g_ref, sh_ref, sc_ref = refs[na : na + 6]
        dx_ref, sums_ref = refs[na + 6 + nd :]
        i = pl.program_id(0)
        dh = None
        for a_ref, (r0, r1) in zip(a_refs, ranges):
            p = jnp.dot(a_ref[...], w_ref[r0:r1, :], preferred_element_type=F32)
            dh = p if dh is None else dh + p
        _, vjp = jax.vjp(_pre_fn, g_ref[...], sh_ref[...], sc_ref[...], x_ref[...])
        dg, dsh, dsc, dx = vjp(dh)
        dx_ref[...] = dx + dres_ref[...]
        _accumulate(sums_ref, _sum_rows(dg, dsh, dsc), i)

    tm = parts[0][1][-2]
    rows = pl.BlockSpec((tm, D_MODEL), lambda i: (i, 0))
    return pl.pallas_call(
        body,
        name=name,
        grid=(s // tm,),
        in_specs=[pl.BlockSpec(p[1], p[2]) for p in parts]
        + [pl.BlockSpec(w_t.shape, lambda i: (0, 0)), rows, rows, ROW_SPEC1, ROW_SPEC1, ROW_SPEC1]
        + [ANY_SPEC] * nd,
        out_specs=[rows, SUMS_SPEC1],
        out_shape=[jax.ShapeDtypeStruct((s, D_MODEL), F32), SUMS_SHAPE],
        compiler_params=_cparams(dimension_semantics=("arbitrary",)),
    )(*[p[0] for p in parts], w_t, x, dres, g, shift, scale, *deps)


def _dw_gu(dgu, h, name, deps=(), tk=DW_TK):
    s = h.shape[0]
    tk = min(tk, s)
    nk = s // tk
    half = D_FF // FFN_TF

    def body(a_ref, b_ref, *rest):
        o_ref, acc = rest[len(deps) :]
        kk = pl.program_id(1)
        p = lax.dot_general(a_ref[...], b_ref[...], _DIMS["tn"], preferred_element_type=F32)

        @pl.when(kk == 0)
        def _():
            acc[...] = p

        @pl.when(kk > 0)
        def _():
            acc[...] += p

        @pl.when(kk == nk - 1)
        def _():
            o_ref[...] = acc[...].astype(o_ref.dtype)

    return pl.pallas_call(
        body,
        name=name,
        grid=(2 * half, nk),
        in_specs=[pl.BlockSpec((None, tk, FFN_TF), lambda i, kk: (i // half, kk, i % half)),
                  pl.BlockSpec((tk, D_MODEL), lambda i, kk: (kk, 0))] + [ANY_SPEC] * len(deps),
        out_specs=pl.BlockSpec((FFN_TF, D_MODEL), lambda i, kk: (i, 0)),
        out_shape=jax.ShapeDtypeStruct((2 * D_FF, D_MODEL), BF16),
        scratch_shapes=[pltpu.VMEM((FFN_TF, D_MODEL), F32)],
        compiler_params=_cparams(dimension_semantics=("parallel", "arbitrary")),
    )(dgu, h, *deps)


def _lru_diag_blocks(dw_bd, name):
    def body(w_ref, o_ref):
        for half in range(2):
            for n in range(LRU_BLOCKS):
                rows = slice(n * LRU_BLOCK, (n + 1) * LRU_BLOCK)
                cols = slice(half * LRU_WIDTH + n * LRU_BLOCK, half * LRU_WIDTH + (n + 1) * LRU_BLOCK)
                o_ref[half, rows, :] = w_ref[rows, cols]

    return pl.pallas_call(
        body, name=name, out_shape=jax.ShapeDtypeStruct((2, LRU_WIDTH, LRU_BLOCK), F32), compiler_params=_cparams()
    )(dw_bd)


def _swiglu_fwd(gu, name):
    (a,) = _rowwise(_swiglu_fn, name, [], [gu], [(D_FF, BF16)], ts=128)
    return a


def _swiglu_bwd(gu, da, name, deps=()):
    def fn(gu, da):
        _, vjp = jax.vjp(_swiglu_fn, gu)
        return vjp(da)[0]

    (dgu,) = _rowwise(fn, name, [], [gu, da], [(2 * D_FF, BF16)], ts=128, deps=deps)
    return dgu


def _shift_down(ext, j, rows):
    return pltpu.roll(ext, j, 0)[V7X_SUBLANES : V7X_SUBLANES + rows]


def _shift_up(ext, j, rows):
    return pltpu.roll(ext, ext.shape[0] - j, 0)[:rows] if j else ext[:rows]


LRU_SLAB = 256
N_SLABS = LRU_WIDTH // LRU_SLAB


def _slab_weights(wa, wx):
    per = LRU_SLAB // LRU_BLOCK
    eye = jnp.eye(per, dtype=wa.dtype)

    def diag(w):
        w4 = w.reshape(N_SLABS, per, LRU_BLOCK, LRU_BLOCK)
        return jnp.einsum("sbkj,bc->sbkcj", w4, eye).reshape(N_SLABS, LRU_SLAB, LRU_SLAB)

    return jnp.concatenate([diag(wa), diag(wx)], axis=2).reshape(LRU_WIDTH, 2 * LRU_SLAB).astype(BF16)


def _slab_cols(v, s):
    lo = s * LRU_SLAB
    return jnp.concatenate([v[:, lo : lo + LRU_SLAB], v[:, LRU_WIDTH + lo : LRU_WIDTH + lo + LRU_SLAB]], axis=1)


def _lru_front(proj, w8, b, w_slab, ba, bx, lam, name):
    def fn(i, steps, w8, b, w_slab, ba, bx, lam, x, halo):
        halo = jnp.where(i > 0, halo, 0.0)
        ext = jnp.concatenate([halo, x], axis=0)
        xc = b + w8[3:4] * x
        for j in (1, 2, 3):
            xc = xc + w8[3 - j : 4 - j] * _shift_down(ext, j, x.shape[0])
        xcb = xc.astype(BF16)
        prods = []
        for s in range(N_SLABS):
            rows = slice(s * LRU_SLAB, (s + 1) * LRU_SLAB)
            prods.append(jnp.dot(xcb[:, rows], w_slab[rows], preferred_element_type=F32))
        pre = jnp.concatenate([p[:, :LRU_SLAB] for p in prods] + [p[:, LRU_SLAB:] for p in prods], axis=1)
        a, u = _gates_fn(ba, bx, lam, pre, xc)
        return xc, pre, a, u

    tiles = [(proj, LRU_WIDTH, 0), (proj, LRU_WIDTH, 0, "prev")]
    outs = [(LRU_WIDTH, F32), (2 * LRU_WIDTH, F32), (LRU_WIDTH, F32), (LRU_WIDTH, F32)]
    return _rowwise(fn, name, [w8, b, w_slab, ba, bx, lam], tiles, outs, with_index=True)


def _lru_back(pre, xc, w_slab, ba, bx, lam, g, h_prev, name, deps=()):
    def fn(w_slab, ba, bx, lam, pre, xc, g, h_prev):
        _, vjp = jax.vjp(_gates_fn, ba, bx, lam, pre, xc)
        dba, dbx, dlam, dpre, dxc = vjp((g * h_prev, g))
        dpre = dpre.astype(BF16)
        back = []
        for s in range(N_SLABS):
            rows = slice(s * LRU_SLAB, (s + 1) * LRU_SLAB)
            back.append(lax.dot_general(_slab_cols(dpre, s), w_slab[rows], _DIMS["nt"], preferred_element_type=F32))
        return dpre, dxc + jnp.concatenate(back, axis=1), _sum_rows(dba, dbx, dlam)

    return _rowwise(fn, name, [w_slab, ba, bx, lam], [pre, xc, g, h_prev],
                    [(2 * LRU_WIDTH, BF16), (LRU_WIDTH, F32)], [(V7X_SUBLANES, LRU_WIDTH)], deps=deps)


def _lru_dw(xc, dpre, name):
    s = xc.shape[0]
    ts = min(512, s)
    steps = s // ts
    per = LRU_SLAB // LRU_BLOCK

    def body(x_ref, d_ref, o_ref, acc):
        i = pl.program_id(0)
        xcb = x_ref[...].astype(BF16)
        d = d_ref[...]
        for sl in range(N_SLABS):
            rows = slice(sl * LRU_SLAB, (sl + 1) * LRU_SLAB)
            p = lax.dot_general(xcb[:, rows], _slab_cols(d, sl), _DIMS["tn"], preferred_element_type=F32)

            @pl.when(i == 0)
            def _(p=p, rows=rows):
                acc[rows, :] = p

            @pl.when(i > 0)
            def _(p=p, rows=rows):
                acc[rows, :] += p

        @pl.when(i == steps - 1)
        def _():
            for half in range(2):
                for n in range(LRU_BLOCKS):
                    r0 = n * LRU_BLOCK
                    c0 = half * LRU_SLAB + (n % per) * LRU_BLOCK
                    o_ref[half, r0 : r0 + LRU_BLOCK, :] = acc[r0 : r0 + LRU_BLOCK, c0 : c0 + LRU_BLOCK]

    return pl.pallas_call(
        body,
        name=name,
        grid=(steps,),
        in_specs=[pl.BlockSpec((ts, LRU_WIDTH), lambda i: (i, 0)), pl.BlockSpec((ts, 2 * LRU_WIDTH), lambda i: (i, 0))],
        out_specs=pl.BlockSpec((2, LRU_WIDTH, LRU_BLOCK), lambda i: (0, 0, 0)),
        out_shape=jax.ShapeDtypeStruct((2, LRU_WIDTH, LRU_BLOCK), F32),
        scratch_shapes=[pltpu.VMEM((LRU_WIDTH, 2 * LRU_SLAB), F32)],
        compiler_params=_cparams(dimension_semantics=("arbitrary",)),
    )(xc, dpre)


def _conv_bwd(proj, w8, d1, name):
    def fn(i, steps, w8, x, halo, d, d1n):
        rows = x.shape[0]
        dn = jnp.where(i < steps - 1, d1n, 0.0)
        halo = jnp.where(i > 0, halo, 0.0)
        dext = jnp.concatenate([d, dn], axis=0)
        xext = jnp.concatenate([halo, x], axis=0)
        dx = w8[3:4] * d
        dw = [None] * 4
        dw[3] = _rowsum(d * x)
        for k in (1, 2, 3):
            dx = dx + w8[3 - k : 4 - k] * _shift_up(dext, k, rows)
            dw[3 - k] = _rowsum(d * _shift_down(xext, k, rows))
        return dx, _sum_rows(*dw, _rowsum(d))

    tiles = [(proj, LRU_WIDTH, 0), (proj, LRU_WIDTH, 0, "prev"), d1, (d1, LRU_WIDTH, 0, "next")]
    return _rowwise(fn, name, [w8], tiles, [(LRU_WIDTH, BF16)], [(V7X_SUBLANES, LRU_WIDTH)], with_index=True)


SCAN_ROWS = 512


def _block_scan(a, b, row, reverse):
    for d in (1, 2, 4):
        if reverse:
            shift, keep = V7X_SUBLANES - d, row < V7X_SUBLANES - d
        else:
            shift, keep = d, row >= d
        a_s = pltpu.roll(a, shift, 0)
        b_s = pltpu.roll(b, shift, 0)
        b = jnp.where(keep, a * b_s + b, b)
        a = jnp.where(keep, a * a_s, a)
    return a, b


def _scan_fwd(a, u, proj, name):
    s, w = a.shape
    ts = min(SCAN_ROWS, s)
    sub = ts // V7X_SUBLANES

    def body(a_ref, u_ref, yr_ref, h_ref, hp_ref, rec_ref, carry):
        @pl.when(pl.program_id(0) == 0)
        def _():
            carry[...] = jnp.zeros_like(carry)

        row = lax.broadcasted_iota(jnp.int32, (V7X_SUBLANES, w), 0)

        def step(j, c):
            rows = pl.ds(pl.multiple_of(j * V7X_SUBLANES, V7X_SUBLANES), V7X_SUBLANES)
            pa, pb = _block_scan(a_ref[rows, :], u_ref[rows, :], row, False)
            h = pb + pa * c
            h_ref[rows, :] = h
            hp_ref[rows, :] = jnp.where(row >= 1, pltpu.roll(h, 1, 0), c)
            return jnp.broadcast_to(h[V7X_SUBLANES - 1 :], (V7X_SUBLANES, w))

        carry[...] = lax.fori_loop(0, sub, step, carry[...])
        rec_ref[...] = _recin_fn(h_ref[...], yr_ref[...]).astype(rec_ref.dtype)

    spec = pl.BlockSpec((ts, w), lambda i: (i, 0))
    return pl.pallas_call(
        body,
        name=name,
        grid=(s // ts,),
        in_specs=[spec, spec, pl.BlockSpec((ts, w), lambda i: (i, 1))],
        out_specs=[spec, spec, spec],
        out_shape=[jax.ShapeDtypeStruct((s, w), F32)] * 2 + [jax.ShapeDtypeStruct((s, w), BF16)],
        scratch_shapes=[pltpu.VMEM((V7X_SUBLANES, w), F32)],
        compiler_params=_cparams(dimension_semantics=("arbitrary",)),
    )(a, u, proj)


def _scan_bwd(a, dh, name):
    s, w = a.shape
    ts = min(SCAN_ROWS, s)
    sub = ts // V7X_SUBLANES
    steps = s // ts

    def body(a_ref, d_ref, g_ref, carry):
        @pl.when(pl.program_id(0) == 0)
        def _():
            carry[...] = jnp.zeros_like(carry)

        row = lax.broadcasted_iota(jnp.int32, (V7X_SUBLANES, w), 0)

        def step(jj, c):
            j = sub - 1 - jj
            rows = pl.ds(pl.multiple_of(j * V7X_SUBLANES, V7X_SUBLANES), V7X_SUBLANES)
            av, dv = a_ref[rows, :], d_ref[rows, :]
            pa, pb = _block_scan(av, av * dv, row, True)
            big = pb + pa * c
            g_ref[rows, :] = dv + jnp.where(row < V7X_SUBLANES - 1, pltpu.roll(big, V7X_SUBLANES - 1, 0), c)
            return jnp.broadcast_to(big[:1], (V7X_SUBLANES, w))

        carry[...] = lax.fori_loop(0, sub, step, carry[...])

    spec = pl.BlockSpec((ts, w), lambda i: (steps - 1 - i, 0))
    return pl.pallas_call(
        body,
        name=name,
        grid=(steps,),
        in_specs=[spec, spec],
        out_specs=spec,
        out_shape=jax.ShapeDtypeStruct((s, w), F32),
        scratch_shapes=[pltpu.VMEM((V7X_SUBLANES, w), F32)],
        compiler_params=_cparams(dimension_semantics=("arbitrary",)),
    )(a, dh)


def _rel_index():
    i = np.arange(ATT_TQ)[:, None]
    j = np.arange(3 * ATT_TQ)[None, :]
    band = (j // CHUNK >= i // CHUNK) & (j // CHUNK <= i // CHUNK + LEFT_CHUNKS)
    return band


SKEW = 4 * ATT_TQ


def _skew_onehot():
    t = np.arange(SKEW)
    diag = np.where(t < 3 * ATT_TQ, -t, SKEW - t)
    idx = np.clip(diag + LEFT_CHUNKS * CHUNK, -MAX_REL, MAX_REL) + MAX_REL
    hit = (idx[:, None] == np.arange(2 * MAX_REL + 1)[None, :]) & (t[:, None] != 3 * ATT_TQ)
    return hit.astype(np.float32)


def _bias_tile(rel_bias, name):
    per_t = jnp.dot(rel_bias, jnp.asarray(_skew_onehot()).T, precision=lax.Precision.HIGHEST)
    win = 3 * ATT_TQ

    def body(t_ref, o_ref):
        tile = pltpu.roll(jnp.broadcast_to(t_ref[0], (ATT_TQ, SKEW)), 0, 1, stride=1, stride_axis=0)[:, :win]
        qc = lax.broadcasted_iota(jnp.int32, (ATT_TQ, win), 0) // CHUNK
        kpos = lax.broadcasted_iota(jnp.int32, (ATT_TQ, win), 1)
        band = (kpos // CHUNK >= qc) & (kpos // CHUNK <= qc + LEFT_CHUNKS)
        for v in range(3):
            o_ref[v, 0] = jnp.where(band & (kpos >= (2 - v) * ATT_TQ), tile, NEG)

    return pl.pallas_call(
        body,
        name=name,
        grid=(ATT_HEADS,),
        in_specs=[pl.BlockSpec((1, 1, SKEW), lambda h: (h, 0, 0))],
        out_specs=pl.BlockSpec((3, 1, ATT_TQ, win), lambda h: (0, h, 0, 0)),
        out_shape=jax.ShapeDtypeStruct((3, ATT_HEADS, ATT_TQ, win), F32),
        compiler_params=_cparams(dimension_semantics=("parallel",)),
    )(per_t.reshape(ATT_HEADS, 1, SKEW))


def _bias_grad(dbias, name):
    win = 3 * ATT_TQ

    def body(d_ref, o_ref):
        d = jnp.concatenate([d_ref[0], jnp.zeros((ATT_TQ, SKEW - win), F32)], axis=1)
        r = lax.broadcasted_iota(jnp.int32, (ATT_TQ, ATT_TQ), 0)
        c = lax.broadcasted_iota(jnp.int32, (ATT_TQ, ATT_TQ), 1)
        flip = (r + c == ATT_TQ - 1).astype(F32)
        d = jnp.dot(flip, d, preferred_element_type=F32, precision=lax.Precision.HIGHEST)
        o_ref[0] = jnp.sum(pltpu.roll(d, SKEW - (ATT_TQ - 1), 1, stride=1, stride_axis=0), axis=0, keepdims=True)

    per_t = pl.pallas_call(
        body,
        name=name,
        grid=(ATT_HEADS,),
        in_specs=[pl.BlockSpec((1, ATT_TQ, win), lambda h: (h, 0, 0))],
        out_specs=pl.BlockSpec((1, 1, SKEW), lambda h: (h, 0, 0)),
        out_shape=jax.ShapeDtypeStruct((ATT_HEADS, 1, SKEW), F32),
        compiler_params=_cparams(dimension_semantics=("parallel",)),
    )(dbias)
    return jnp.dot(per_t.reshape(ATT_HEADS, SKEW), jnp.asarray(_skew_onehot()), precision=lax.Precision.HIGHEST)


ATT_STEP_HEADS = ATT_HEADS
ATT_STEP_COLS = ATT_STEP_HEADS * ATT_HEAD_DIM


def _attn_specs(nt):
    qb, kb, vb = OFF_Q // ATT_STEP_COLS, OFF_K // ATT_STEP_COLS, OFF_V // ATT_STEP_COLS
    blk = (ATT_TQ, ATT_STEP_COLS)

    def qmap(base):
        return lambda hp, m: (jnp.minimum(m, nt - 1), base + hp)

    def wmap(base, back):
        return lambda hp, m: (jnp.clip(m - back, 0, nt - 1), base + hp)

    specs = [pl.BlockSpec(blk, qmap(qb))]
    specs += [pl.BlockSpec(blk, wmap(kb, back)) for back in (2, 1, 0)]
    specs += [pl.BlockSpec(blk, wmap(vb, back)) for back in (2, 1, 0)]
    return specs


ATT_SCALE = ATT_HEAD_DIM**-0.5


def _attn_exp(qh, kh, bias):
    s = lax.dot_general(qh, kh, _DIMS["nt"], preferred_element_type=F32) + bias
    e = jnp.exp(s - jnp.max(s, axis=-1, keepdims=True))
    return e, jnp.sum(e, axis=-1, keepdims=True)


def _attn_window(k0, k1, k2, v0, v1, v2):
    k = jnp.concatenate([k0[...], k1[...], k2[...]], axis=0).astype(BF16)
    v = jnp.concatenate([v0[...], v1[...], v2[...]], axis=0).astype(BF16)
    return k, v


def _bias_spec():
    return pl.BlockSpec((1, ATT_STEP_HEADS, ATT_TQ, 3 * ATT_TQ), lambda hp, m: (jnp.minimum(m, 2), hp, 0, 0))


def _attn_fwd(proj, bias, name):
    s = proj.shape[0]
    nt = s // ATT_TQ

    def body(q_ref, k0, k1, k2, v0, v1, v2, b_ref, o_ref):
        k, v = _attn_window(k0, k1, k2, v0, v1, v2)
        q = (q_ref[...] * ATT_SCALE).astype(BF16)
        for hh in range(ATT_STEP_HEADS):
            cols = slice(hh * ATT_HEAD_DIM, (hh + 1) * ATT_HEAD_DIM)
            e, total = _attn_exp(q[:, cols], k[:, cols], b_ref[0, hh])
            o = jnp.dot(e.astype(BF16), v[:, cols], preferred_element_type=F32) / total
            o_ref[:, cols] = o.astype(o_ref.dtype)

    specs = _attn_specs(nt) + [_bias_spec()]
    return pl.pallas_call(
        body,
        name=name,
        grid=(ATT_HEADS // ATT_STEP_HEADS, nt),
        in_specs=specs,
        out_specs=pl.BlockSpec((ATT_TQ, ATT_STEP_COLS), lambda hp, m: (m, hp)),
        out_shape=jax.ShapeDtypeStruct((s, ATT_WIDTH), BF16),
        compiler_params=_cparams(dimension_semantics=("parallel", "arbitrary")),
    )(proj, proj, proj, proj, proj, proj, proj, bias)


def _attn_bwd(proj, bias, do, name):
    s = proj.shape[0]
    nt = s // ATT_TQ
    win = 3 * ATT_TQ

    def body(q_ref, k0, k1, k2, v0, v1, v2, do_ref, b_ref, dq_ref, dk_ref, dv_ref, db_ref, dk_acc, dv_acc):
        m = pl.program_id(1)

        @pl.when(m == 0)
        def _():
            dk_acc[...] = jnp.zeros_like(dk_acc)
            dv_acc[...] = jnp.zeros_like(dv_acc)
            db_ref[...] = jnp.zeros_like(db_ref)

        @pl.when(m < nt)
        def _():
            k, v = _attn_window(k0, k1, k2, v0, v1, v2)
            q = (q_ref[...] * ATT_SCALE).astype(BF16)
            dout = do_ref[...]
            for hh in range(ATT_STEP_HEADS):
                cols = slice(hh * ATT_HEAD_DIM, (hh + 1) * ATT_HEAD_DIM)
                qh, kh, vh, doh = q[:, cols], k[:, cols], v[:, cols], dout[:, cols]
                e, total = _attn_exp(qh, kh, b_ref[0, hh])
                p = e / total
                dvh = lax.dot_general(p.astype(BF16), doh, _DIMS["tn"], preferred_element_type=F32)
                dp = lax.dot_general(doh, vh, _DIMS["nt"], preferred_element_type=F32)
                ds = p * (dp - jnp.sum(dp * p, axis=-1, keepdims=True))
                db_ref[hh] += ds
                dsb = ds.astype(BF16)
                dqh = jnp.dot(dsb, kh, preferred_element_type=F32) * ATT_SCALE
                dkh = lax.dot_general(dsb, qh, _DIMS["tn"], preferred_element_type=F32)
                dq_ref[:, cols] = dqh.astype(dq_ref.dtype)
                dk_acc[:, cols] += dkh
                dv_acc[:, cols] += dvh

        dk_ref[...] = dk_acc[:ATT_TQ].astype(dk_ref.dtype)
        dv_ref[...] = dv_acc[:ATT_TQ].astype(dv_ref.dtype)
        for acc in (dk_acc, dv_acc):
            rest = acc[ATT_TQ:]
            acc[: win - ATT_TQ] = rest
            acc[win - ATT_TQ :] = jnp.zeros((ATT_TQ, ATT_STEP_COLS), F32)

    blk = (ATT_TQ, ATT_STEP_COLS)
    specs = _attn_specs(nt)
    specs.append(pl.BlockSpec(blk, lambda hp, m: (jnp.minimum(m, nt - 1), hp)))
    specs.append(_bias_spec())
    done = lambda hp, m: (jnp.maximum(m - 2, 0), hp)
    out_specs = [
        pl.BlockSpec(blk, lambda hp, m: (jnp.minimum(m, nt - 1), hp)),
        pl.BlockSpec(blk, done),
        pl.BlockSpec(blk, done),
        pl.BlockSpec((ATT_STEP_HEADS, ATT_TQ, win), lambda hp, m: (hp, 0, 0)),
    ]
    out_shape = [jax.ShapeDtypeStruct((s, ATT_WIDTH), BF16)] * 3
    out_shape.append(jax.ShapeDtypeStruct((ATT_HEADS, ATT_TQ, win), F32))
    return pl.pallas_call(
        body,
        name=name,
        grid=(ATT_HEADS // ATT_STEP_HEADS, nt + 2),
        in_specs=specs,
        out_specs=out_specs,
        out_shape=out_shape,
        scratch_shapes=[pltpu.VMEM((win, ATT_STEP_COLS), F32), pltpu.VMEM((win, ATT_STEP_COLS), F32)],
        compiler_params=_cparams(dimension_semantics=("arbitrary", "arbitrary")),
    )(proj, proj, proj, proj, proj, proj, proj, do, bias)


def _ada_fwd(c_all, w, name):
    def body(c_ref, w_ref, o_ref):
        act = _silu(c_ref[...]).astype(BF16)
        o_ref[...] = jnp.dot(act, w_ref[...].astype(BF16), preferred_element_type=F32)

    return pl.pallas_call(
        body, name=name, out_shape=jax.ShapeDtypeStruct((c_all.shape[0], w.shape[1]), F32), compiler_params=_cparams()
    )(c_all, w)


def _ada_bwd(c_all, dmod, name):
    def body(c_ref, d_ref, o_ref):
        act = _silu(c_ref[...])
        o_ref[...] = lax.dot_general(act, d_ref[...], _DIMS["tn"], preferred_element_type=F32,
                                     precision=lax.Precision.HIGHEST)

    return pl.pallas_call(
        body, name=name, out_shape=jax.ShapeDtypeStruct((c_all.shape[1], dmod.shape[1]), F32), compiler_params=_cparams()
    )(c_all, dmod)


def _adamw_parts(landed, sent, me, w, m, v, name, rows=256):
    r, c = w.shape
    tr = _pick(r, rows, 16)

    def body(me_ref, g_ref, own_ref, w_ref, m_ref, v_ref, go_ref, d_ref, mo_ref, vo_ref):
        mine = me_ref[0]
        grad = jnp.zeros((tr, c), F32)
        for d in range(N_DEV):
            grad = grad + jnp.where(mine == d, own_ref[0], g_ref[d]).astype(F32)
        _adamw_update(grad, w_ref, m_ref, v_ref, go_ref, d_ref, mo_ref, vo_ref)

    spec = pl.BlockSpec((tr, c), lambda i, me_ref: (i, 0))
    return pl.pallas_call(
        body,
        name=name,
        grid_spec=pltpu.PrefetchScalarGridSpec(
            num_scalar_prefetch=1,
            grid=(r // tr,),
            in_specs=[pl.BlockSpec((N_DEV, tr, c), lambda i, me_ref: (0, i, 0)),
                      pl.BlockSpec((1, tr, c), lambda i, me_ref: (me_ref[0], i, 0)), spec, spec, spec],
            out_specs=[spec] * 4,
        ),
        out_shape=[jax.ShapeDtypeStruct((r, c), F32)] * 4,
        compiler_params=_cparams(dimension_semantics=("parallel",)),
    )(me.reshape(1).astype(jnp.int32), landed, sent, w, m, v)


def _adamw_update(grad, w_ref, m_ref, v_ref, go_ref, d_ref, mo_ref, vo_ref):
    m2 = ADAM_B1 * m_ref[...] + (1.0 - ADAM_B1) * grad
    v2 = ADAM_B2 * v_ref[...] + (1.0 - ADAM_B2) * (grad * grad)
    m_hat = m2 / (1.0 - ADAM_B1**ADAM_STEP)
    v_hat = v2 / (1.0 - ADAM_B2**ADAM_STEP)
    go_ref[...] = grad
    d_ref[...] = -ADAM_LR * (m_hat / (jnp.sqrt(v_hat) + ADAM_EPS) + ADAM_WD * w_ref[...])
    mo_ref[...] = m2
    vo_ref[...] = v2


def _adamw(g, w, m, v, name, rows=256):
    r, c = w.shape
    tr = _pick(r, rows, 16)

    def body(g_ref, w_ref, m_ref, v_ref, go_ref, d_ref, mo_ref, vo_ref):
        _adamw_update(g_ref[...], w_ref, m_ref, v_ref, go_ref, d_ref, mo_ref, vo_ref)

    spec = pl.BlockSpec((tr, c), lambda i: (i, 0))
    return pl.pallas_call(
        body,
        name=name,
        grid=(r // tr,),
        in_specs=[spec, spec, spec, spec],
        out_specs=[spec] * 4,
        out_shape=[jax.ShapeDtypeStruct((r, c), F32)] * 4,
        compiler_params=_cparams(dimension_semantics=("parallel",)),
    )(g, w, m, v)


def _sum_parts(parts, name):
    def body(p_ref, o_ref):
        acc = p_ref[0]
        for d in range(1, N_DEV):
            acc = acc + p_ref[d]
        o_ref[...] = acc

    return pl.pallas_call(
        body, name=name, out_shape=jax.ShapeDtypeStruct(parts.shape[1:], F32), compiler_params=_cparams()
    )(parts)


def _place():
    x, y, c = lax.axis_index("x"), lax.axis_index("y"), lax.axis_index("c")
    return x, y, c


def _dev_index(p):
    return 4 * p[0] + 2 * p[1] + p[2]


def _allgather_vmem(shard, name):
    m_per, n = shard.shape

    def body(x_ref, out_ref, send_sems, recv_sems, local_sem):
        x, y, c = _place()
        me, sibling = (x, y, c), (x, y, 1 - c)
        chips = [(1 - x, y), (x, 1 - y), (1 - x, 1 - y)]

        def rows(p):
            return out_ref.at[pl.ds(_dev_index(p) * m_per, m_per), :]

        def copy(k, block, to, src=None):
            return pltpu.make_async_remote_copy(
                src_ref=rows(block) if src is None else src, dst_ref=rows(block),
                send_sem=send_sems.at[k], recv_sem=recv_sems.at[k], device_id=to, device_id_type=MESH)

        mine = pltpu.make_async_copy(x_ref, rows(me), local_sem)
        mine.start()
        first = [copy(0, me, sibling, src=x_ref)]
        first += [copy(1 + j, me, (*chip, c), src=x_ref) for j, chip in enumerate(chips)]
        for cp in first:
            cp.start()
        passed = [copy(4 + j, (*chip, c), sibling) for j, chip in enumerate(chips)]
        for j, chip in enumerate(chips):
            copy(1 + j, (*chip, c), me).wait_recv()
            passed[j].start()
        copy(0, sibling, me).wait_recv()
        for j, chip in enumerate(chips):
            copy(4 + j, (*chip, 1 - c), me).wait_recv()
        for cp in first + passed:
            cp.wait_send()
        mine.wait()

    return pl.pallas_call(
        body,
        name=name,
        out_shape=jax.ShapeDtypeStruct((N_DEV * m_per, n), shard.dtype),
        in_specs=[pl.BlockSpec(memory_space=pltpu.VMEM)],
        out_specs=pl.BlockSpec(memory_space=pltpu.VMEM),
        scratch_shapes=[pltpu.SemaphoreType.DMA((7,)), pltpu.SemaphoreType.DMA((7,)), pltpu.SemaphoreType.DMA],
        compiler_params=_cparams(),
    )(shard)


def _allgather_hbm(shards, name):
    n = len(shards)

    def body(*refs):
        ins, outs = refs[:n], refs[n : 2 * n]
        send_sems, recv_sems, local_sems = refs[2 * n :]
        x, y, c = _place()
        me, sibling = (x, y, c), (x, y, 1 - c)
        chips = [(1 - x, y), (x, 1 - y), (1 - x, 1 - y)]

        def copy(a, k, block, to, src=None):
            dst = outs[a].at[_dev_index(block)]
            return pltpu.make_async_remote_copy(
                src_ref=dst if src is None else src, dst_ref=dst,
                send_sem=send_sems.at[a * 7 + k], recv_sem=recv_sems.at[a * 7 + k], device_id=to, device_id_type=MESH)

        mine = [pltpu.make_async_copy(ins[a], outs[a].at[_dev_index(me)], local_sems.at[a]) for a in range(n)]
        for cp in mine:
            cp.start()
        first = []
        for a in range(n):
            first.append(copy(a, 0, me, sibling, src=ins[a]))
            first += [copy(a, 1 + j, me, (*chip, c), src=ins[a]) for j, chip in enumerate(chips)]
        for cp in first:
            cp.start()
        passed = []
        for j, chip in enumerate(chips):
            for a in range(n):
                copy(a, 1 + j, (*chip, c), me).wait_recv()
                cp = copy(a, 4 + j, (*chip, c), sibling)
                cp.start()
                passed.append(cp)
        for a in range(n):
            copy(a, 0, sibling, me).wait_recv()
        for j, chip in enumerate(chips):
            for a in range(n):
                copy(a, 4 + j, (*chip, 1 - c), me).wait_recv()
        for cp in first + passed:
            cp.wait_send()
        for cp in mine:
            cp.wait()

    any_spec = pl.BlockSpec(memory_space=pl.ANY)
    return pl.pallas_call(
        body,
        name=name,
        out_shape=[jax.ShapeDtypeStruct((N_DEV, *s.shape), s.dtype) for s in shards],
        in_specs=[any_spec] * n,
        out_specs=[any_spec] * n,
        scratch_shapes=[pltpu.SemaphoreType.DMA((7 * n,)), pltpu.SemaphoreType.DMA((7 * n,)),
                        pltpu.SemaphoreType.DMA((n,))],
        compiler_params=_cparams(),
    )(*shards)


def _exchange_hbm(bufs, name):
    n = len(bufs)

    def body(*refs):
        ins, outs = refs[:n], refs[n : 2 * n]
        send_sems, recv_sems, local_sems = refs[2 * n :]
        x, y, c = _place()
        me = _dev_index((x, y, c))
        mine = [pltpu.make_async_copy(ins[a].at[me], outs[a].at[me], local_sems.at[a]) for a in range(n)]
        for cp in mine:
            cp.start()
        def peer_of(k):
            return (1 - x if k & 4 else x, 1 - y if k & 2 else y, 1 - c if k & 1 else c)

        copies = []
        for k in range(1, N_DEV):
            peer = peer_of(k)
            for a in range(n):
                copies.append(pltpu.make_async_remote_copy(
                    src_ref=ins[a].at[_dev_index(peer)], dst_ref=outs[a].at[me],
                    send_sem=send_sems.at[a * 7 + k - 1], recv_sem=recv_sems.at[a * 7 + k - 1],
                    device_id=peer, device_id_type=MESH))
        for cp in copies:
            cp.start()
        for k in range(1, N_DEV):
            peer = peer_of(k)
            for a in range(n):
                pltpu.make_async_remote_copy(
                    src_ref=ins[a].at[me], dst_ref=outs[a].at[_dev_index(peer)],
                    send_sem=send_sems.at[a * 7 + k - 1], recv_sem=recv_sems.at[a * 7 + k - 1],
                    device_id=peer, device_id_type=MESH).wait_recv()
        for cp in copies:
            cp.wait_send()
        for cp in mine:
            cp.wait()

    any_spec = pl.BlockSpec(memory_space=pl.ANY)
    return pl.pallas_call(
        body,
        name=name,
        out_shape=[jax.ShapeDtypeStruct(b.shape, b.dtype) for b in bufs],
        in_specs=[any_spec] * n,
        out_specs=[any_spec] * n,
        scratch_shapes=[pltpu.SemaphoreType.DMA((7 * n,)), pltpu.SemaphoreType.DMA((7 * n,)),
                        pltpu.SemaphoreType.DMA((n,))],
        compiler_params=_cparams(),
    )(*bufs)


HBM_SPEC = pl.BlockSpec(memory_space=pltpu.HBM)
SEM_SPEC = pl.BlockSpec(memory_space=pltpu.SEMAPHORE)
EFFECT = pltpu.SideEffectType.DATAFLOW_SIDE_EFFECTING


def _peers(x, y, c):
    return [(1 - x if k & 4 else x, 1 - y if k & 2 else y, 1 - c if k & 1 else c) for k in range(1, N_DEV)]


def _push_peers(mode, x, y, c):
    if mode == "all":
        return _peers(x, y, c)
    return [(x, y, 1 - c), (1 - x, y, c), (x, 1 - y, c), (1 - x, 1 - y, c)]


def _push_start(groups, sliced, name, after=(), modes=None):
    flat = [b for g in groups for b in g]
    n, ng = len(flat), len(groups)
    sizes = [len(g) for g in groups]
    modes = modes or ["all"] * ng
    fan = [len(_push_peers(m, 0, 0, 0)) for m in modes]
    per = 2 if sliced else 3
    lands = [lax.empty(b.shape if sliced else (N_DEV, *b.shape), b.dtype) for b in flat]

    def body(*refs):
        ins, lnd = refs[:n], refs[n : 2 * n]
        sems = refs[2 * n + len(after) : 2 * n + len(after) + per * ng]
        token = refs[-1]
        x, y, c = _place()
        me = _dev_index((x, y, c))
        if not sliced:
            first = 0
            for gi, size in enumerate(sizes):
                for j in range(first, first + size):
                    pltpu.make_async_copy(ins[j], lnd[j].at[me], sems[per * gi + 2].at[j - first]).start()
                first += size
        first = 0
        for gi, size in enumerate(sizes):
            for k, peer in enumerate(_push_peers(modes[gi], x, y, c)):
                for j in range(first, first + size):
                    sem = (j - first) * fan[gi] + k
                    pltpu.make_async_remote_copy(
                        src_ref=ins[j].at[_dev_index(peer)] if sliced else ins[j], dst_ref=lnd[j].at[me],
                        send_sem=sems[per * gi].at[sem], recv_sem=sems[per * gi + 1].at[sem],
                        device_id=peer, device_id_type=MESH).start()
            first += size
        token[...] = jnp.zeros_like(token)

    out_shape = []
    for size, width in zip(sizes, fan):
        out_shape += [pltpu.SemaphoreType.DMA((width * size,)), pltpu.SemaphoreType.DMA((width * size,))]
        out_shape += [] if sliced else [pltpu.SemaphoreType.DMA((size,))]
    out_shape += [pltpu.HBM(b.shape, b.dtype) for b in flat + lands]
    out_shape.append(jax.ShapeDtypeStruct((V7X_SUBLANES, V7X_LANES), F32))
    res = pl.pallas_call(
        body,
        name=name,
        out_shape=tuple(out_shape),
        in_specs=[HBM_SPEC] * (2 * n) + [ANY_SPEC] * len(after),
        out_specs=tuple([SEM_SPEC] * (per * ng) + [HBM_SPEC] * (2 * n) + [pl.BlockSpec(memory_space=pltpu.VMEM)]),
        input_output_aliases={i: per * ng + i for i in range(2 * n)},
        compiler_params=pltpu.CompilerParams(has_side_effects=EFFECT),
    )(*[pltpu.with_memory_space_constraint(b, pltpu.HBM) for b in flat + lands], *after)
    sems, thru, token = res[: per * ng], res[per * ng : per * ng + 2 * n], res[-1]
    out, first = [], 0
    for gi, size in enumerate(sizes):
        out.append((sems[per * gi], sems[per * gi + 1], list(thru[first : first + size]),
                    list(thru[n + first : n + first + size]), None if sliced else sems[per * gi + 2]))
        first += size
    return out, token


def _push_wait(started, sliced, after, name, mode="all"):
    send_sems, recv_sems, bufs, lands, own_sems = started
    n = len(bufs)
    fan = len(_push_peers(mode, 0, 0, 0))
    own = [] if own_sems is None else [own_sems]

    def body(*refs):
        ins, lnd = refs[:n], refs[n : 2 * n]
        send_ref, recv_ref = refs[2 * n], refs[2 * n + 1]
        x, y, c = _place()
        for k, peer in enumerate(_push_peers(mode, x, y, c)):
            for j in range(n):
                cp = pltpu.make_async_remote_copy(
                    src_ref=ins[j].at[_dev_index(peer)] if sliced else ins[j], dst_ref=lnd[j].at[_dev_index(peer)],
                    send_sem=send_ref.at[j * fan + k], recv_sem=recv_ref.at[j * fan + k],
                    device_id=peer, device_id_type=MESH)
                cp.wait_send()
                cp.wait_recv()
        if own:
            for j in range(n):
                pltpu.make_async_copy(ins[j], lnd[j].at[_dev_index((x, y, c))], refs[2 * n + 2].at[j]).wait()

    res = pl.pallas_call(
        body,
        name=name,
        out_shape=tuple(pltpu.HBM(b.shape, b.dtype) for b in bufs + lands),
        in_specs=[HBM_SPEC] * (2 * n) + [SEM_SPEC] * (2 + len(own)) + [pl.BlockSpec(memory_space=pl.ANY)],
        out_specs=tuple([HBM_SPEC] * (2 * n)),
        input_output_aliases={i: i for i in range(2 * n)},
        compiler_params=pltpu.CompilerParams(has_side_effects=EFFECT),
    )(*bufs, *lands, send_sems, recv_sems, *own, after)
    return list(res[:n]), list(res[n:])


def _forward_copies(lnd, send_ref, recv_ref, incoming):
    x, y, c = _place()
    copies = []
    for k, chip in enumerate([(1 - x, y), (x, 1 - y), (1 - x, 1 - y)]):
        mine, theirs = _dev_index((*chip, c)), _dev_index((*chip, 1 - c))
        for j, ref in enumerate(lnd):
            copies.append(pltpu.make_async_remote_copy(
                src_ref=ref.at[mine], dst_ref=ref.at[theirs if incoming else mine],
                send_sem=send_ref.at[j * 3 + k], recv_sem=recv_ref.at[j * 3 + k],
                device_id=(x, y, 1 - c), device_id_type=MESH))
    return copies


def _forward_start(lands, name):
    n = len(lands)

    def body(*refs):
        for cp in _forward_copies(refs[:n], refs[n], refs[n + 1], False):
            cp.start()

    res = pl.pallas_call(
        body,
        name=name,
        out_shape=(pltpu.SemaphoreType.DMA((3 * n,)), pltpu.SemaphoreType.DMA((3 * n,)),
                   *[pltpu.HBM(b.shape, b.dtype) for b in lands]),
        in_specs=[HBM_SPEC] * n,
        out_specs=(SEM_SPEC, SEM_SPEC, *[HBM_SPEC] * n),
        input_output_aliases={i: 2 + i for i in range(n)},
        compiler_params=pltpu.CompilerParams(has_side_effects=EFFECT),
    )(*[pltpu.with_memory_space_constraint(b, pltpu.HBM) for b in lands])
    return res[0], res[1], list(res[2:])


def _forward_wait(started, after, name):
    send_sems, recv_sems, lands = started
    n = len(lands)

    def body(*refs):
        for cp in _forward_copies(refs[:n], refs[n], refs[n + 1], True):
            cp.wait_send()
            cp.wait_recv()

    res = pl.pallas_call(
        body,
        name=name,
        out_shape=tuple(pltpu.HBM(b.shape, b.dtype) for b in lands),
        in_specs=[HBM_SPEC] * n + [SEM_SPEC, SEM_SPEC, pl.BlockSpec(memory_space=pl.ANY)],
        out_specs=tuple([HBM_SPEC] * n),
        input_output_aliases={i: i for i in range(n)},
        compiler_params=pltpu.CompilerParams(has_side_effects=EFFECT),
    )(*lands, send_sems, recv_sems, after)
    return list(res)


def _cols_full(g):
    return jnp.transpose(g, (1, 0, 2)).reshape(g.shape[1], -1)


def _rows_full(g):
    return g.reshape(-1, g.shape[2])


def _cols_parts(full, n=N_DEV):
    r = full.shape[0]
    return jnp.transpose(full.reshape(r, n, -1), (1, 0, 2)).astype(BF16)


def _rows_parts(full):
    return full.reshape(N_DEV, -1, full.shape[1]).astype(BF16)


def _block_diag(w):
    eye = jnp.eye(LRU_BLOCKS, dtype=w.dtype)
    return jnp.einsum("nkj,nm->nkmj", w, eye).reshape(LRU_WIDTH, LRU_WIDTH)


def _pad_rows(v, rows):
    flat = v.reshape(-1)
    return jnp.pad(flat, (0, rows * D_MODEL - flat.shape[0])).reshape(rows, D_MODEL)


def _my_cols(full, me, width):
    return lax.dynamic_slice_in_dim(full, me * width, width, axis=full.ndim - 1)


def kernel(x, c, w_ada, b_ada, norm_pre, norm_post, ffn1_w_gu, ffn1_w_down, w_in, rel_bias, conv_w, conv_b, lru_wa, lru_ba, lru_wx, lru_bx, lru_lambda, w_att_o, w_rec_o, w_out, ffn2_w_gu, ffn2_w_down, loss_target, m_w_ada, m_b_ada, m_norm_pre, m_norm_post, m_ffn1_w_gu, m_ffn1_w_down, m_w_in, m_rel_bias, m_conv_w, m_conv_b, m_lru_wa, m_lru_ba, m_lru_wx, m_lru_bx, m_lru_lambda, m_w_att_o, m_w_rec_o, m_w_out, m_ffn2_w_gu, m_ffn2_w_down, v_w_ada, v_b_ada, v_norm_pre, v_norm_post, v_ffn1_w_gu, v_ffn1_w_down, v_w_in, v_rel_bias, v_conv_w, v_conv_b, v_lru_wa, v_lru_ba, v_lru_wx, v_lru_bx, v_lru_lambda, v_w_att_o, v_w_rec_o, v_w_out, v_ffn2_w_gu, v_ffn2_w_down):
    weights = dict(w_ada=w_ada, b_ada=b_ada, norm_pre=norm_pre, norm_post=norm_post, ffn1_w_gu=ffn1_w_gu,
                   ffn1_w_down=ffn1_w_down, w_in=w_in, rel_bias=rel_bias, conv_w=conv_w, conv_b=conv_b,
                   lru_wa=lru_wa, lru_ba=lru_ba, lru_wx=lru_wx, lru_bx=lru_bx, lru_lambda=lru_lambda,
                   w_att_o=w_att_o, w_rec_o=w_rec_o, w_out=w_out, ffn2_w_gu=ffn2_w_gu, ffn2_w_down=ffn2_w_down)
    mom1 = dict(w_ada=m_w_ada, b_ada=m_b_ada, norm_pre=m_norm_pre, norm_post=m_norm_post, ffn1_w_gu=m_ffn1_w_gu,
                ffn1_w_down=m_ffn1_w_down, w_in=m_w_in, rel_bias=m_rel_bias, conv_w=m_conv_w, conv_b=m_conv_b,
                lru_wa=m_lru_wa, lru_ba=m_lru_ba, lru_wx=m_lru_wx, lru_bx=m_lru_bx, lru_lambda=m_lru_lambda,
                w_att_o=m_w_att_o, w_rec_o=m_w_rec_o, w_out=m_w_out, ffn2_w_gu=m_ffn2_w_gu, ffn2_w_down=m_ffn2_w_down)
    mom2 = dict(w_ada=v_w_ada, b_ada=v_b_ada, norm_pre=v_norm_pre, norm_post=v_norm_post, ffn1_w_gu=v_ffn1_w_gu,
                ffn1_w_down=v_ffn1_w_down, w_in=v_w_in, rel_bias=v_rel_bias, conv_w=v_conv_w, conv_b=v_conv_b,
                lru_wa=v_lru_wa, lru_ba=v_lru_ba, lru_wx=v_lru_wx, lru_bx=v_lru_bx, lru_lambda=v_lru_lambda,
                w_att_o=v_w_att_o, w_rec_o=v_w_rec_o, w_out=v_w_out, ffn2_w_gu=v_ffn2_w_gu, ffn2_w_down=v_ffn2_w_down)
    order = list(weights)
    big = ["ffn1_w_gu", "ffn1_w_down", "w_in", "w_att_o", "w_rec_o", "w_out", "ffn2_w_gu", "ffn2_w_down"]
    col_sharded = {"ffn1_w_gu", "w_in", "w_att_o", "ffn2_w_gu"}
    small = ["b_ada", "norm_pre", "norm_post", "rel_bias", "conv_w", "conv_b", "lru_wa", "lru_ba", "lru_wx",
             "lru_bx", "lru_lambda"]

    xi, yi, ci = _place()
    me = _dev_index((xi, yi, ci))
    x0 = x[0]
    target = loss_target[0]
    fuse_tm = min(FUSE_TM, x0.shape[0])

    transposed = {"ffn1_w_gu", "w_in", "ffn2_w_gu"}
    local = lambda n, arr: jnp.transpose(arr[0]) if n in transposed else arr[0]
    shards = {n: local(n, weights[n]).astype(BF16) for n in big}
    full_of = lambda n, g: _cols_full(g) if n == "w_att_o" else _rows_full(g)

    pack = jnp.concatenate([c.reshape(-1), norm_pre.reshape(-1), norm_post.reshape(-1), conv_w.reshape(-1)])
    pack = jnp.pad(pack, (0, 3072 - pack.shape[0])).reshape(8, 384)
    got = _allgather_vmem(pack, "gather_small_inputs").reshape(N_DEV, 3072)
    c_all = got[:, :1024]
    unshard = lambda blk, rows: jnp.transpose(blk.reshape(N_DEV, rows, 128), (1, 0, 2)).reshape(rows, D_MODEL)
    g_pre = unshard(got[:, 1024:1408], 3)
    g_post = unshard(got[:, 1408:1792], 3)
    conv_taps = unshard(got[:, 1792:2304], 4)
    conv_w8 = jnp.concatenate([conv_taps, jnp.zeros((4, LRU_WIDTH), F32)], axis=0)

    mod_cols = _ada_fwd(c_all, w_ada[0], "ada_fwd")
    mod_all = _allgather_vmem(mod_cols, "gather_mod").reshape(N_DEV, N_DEV, 1152)
    mod = lax.dynamic_index_in_dim(mod_all, me, axis=1, keepdims=False).reshape(1, -1) + b_ada
    mod = mod.reshape(3, 3, 1, D_MODEL)

    w_slab = _slab_weights(lru_wa[0], lru_wx[0])
    bias = _bias_tile(rel_bias[0], "bias_tile")

    res_w = (0.5, 1.0, 0.5)
    row = lambda v: v.reshape(1, -1)

    (w1_gu,) = _allgather_hbm([shards["ffn1_w_gu"]], "gather_ffn1_w_gu")
    weight_groups = [["ffn1_w_down"], ["w_in"], ["w_att_o", "w_rec_o", "w_out"], ["ffn2_w_gu", "ffn2_w_down"]]
    weight_modes = ["all", "chip", "all", "all"]
    weights_started, started = _push_start([[shards[n] for n in g] for g in weight_groups], False,
                                           "gather_weights_start", after=(mod, w1_gu), modes=weight_modes)
    full = {"ffn1_w_gu": _rows_full(w1_gu)}

    def gathered_group(gi, after):
        sent, lands = _push_wait(weights_started[gi], False, after, f"gather_weights_wait{gi}", mode=weight_modes[gi])
        if weight_modes[gi] == "chip":
            lands = _forward_wait(_forward_start(lands, f"gather_weights_forward{gi}"), sent[0],
                                  f"gather_weights_forward_wait{gi}")
        for n, land in zip(weight_groups[gi], lands):
            full[n] = full_of(n, land)

    def ffn_fwd(xin, k, gi, tag, deps=(), target=None):
        h, a, g, u = _pre_up(xin, row(g_pre[k]), mod[k, 0], mod[k, 1], full[f"{tag}_w_gu"], f"{tag}_up", deps=deps)
        if f"{tag}_w_down" not in full:
            gathered_group(gi, a)
        f, *out = _matmul_post(a, full[f"{tag}_w_down"], xin, row(g_post[k]), mod[k, 2], res_w[k], f"{tag}_down",
                               target=target)
        return (out[0] if target is None else out), (h, g, u, a, f)

    x1, saved1 = ffn_fwd(x0, 0, 0, "ffn1", deps=(started,))

    gathered_group(1, x1)
    h2, proj = _pre_matmul(x1, row(g_pre[1]), mod[1, 0], mod[1, 1], full["w_in"], "mix_in",
                           b_shift=3 * ATT_WIDTH // 512)
    att_o = _attn_fwd(proj, bias, "attn_fwd")
    gathered_group(2, att_o)
    xc, pre, a_t, u_t = _lru_front(proj, conv_w8, conv_b, w_slab, lru_ba, lru_bx, lru_lambda, "lru_front")
    hs, h_prev, rec_in = _scan_fwd(a_t, u_t, proj, "lru_scan")
    att = _matmul(att_o, full["w_att_o"], "nn", F32, "att_out")
    rec = _matmul(rec_in, full["w_rec_o"], "nn", F32, "rec_out")
    merged, f2, x2 = _merge_matmul_post(att, rec, proj, full["w_out"], x1, row(g_post[1]), mod[1, 2], res_w[1],
                                        "mix_out")

    gathered_group(3, x2)
    (dy, sq), saved3 = ffn_fwd(x2, 2, 2, "ffn2", target=target)
    loss = lax.psum(0.5 * jnp.sum(sq) / D_MODEL, ("x", "y", "c"))

    grads = {}
    norm_sums = [None] * 6

    pending = []

    def exchange_start(names, tag, after=()):
        send = [(_cols_parts if n == "w_att_o" else _rows_parts)(grads[n]) for n in names]
        (group,), token = _push_start([send], True, f"exchange_{tag}_start", after=after)
        pending.append((names, send, group, tag))
        return token

    def exchange_finish(names, send, group, tag, after):
        sent, lands = _push_wait(group, True, after, f"exchange_{tag}_wait")
        res = None
        for n, land, mine in zip(names, lands, sent):
            res = _adamw_parts(land, mine, me, local(n, weights[n]), local(n, mom1[n]), local(n, mom2[n]),
                               f"adamw_{n}")
            back = (lambda r: jnp.transpose(r)) if n in transposed else (lambda r: r)
            out_g[n], out_d[n], out_m[n], out_v[n] = [back(r).reshape(weights[n].shape) for r in res]
        return res[0]

    out_g, out_d, out_m, out_v = {}, {}, {}, {}

    def ffn_bwd(xin, k, saved, dout, tag):
        h, g, u, a, f = saved
        w_gu, w_down = f"{tag}_w_gu", f"{tag}_w_down"
        df, dgu, norm_sums[2 * k + 1] = _post_bwd_up_bwd(f, dout, row(g_post[k]), mod[k, 2], res_w[k], full[w_down],
                                                          g, u, f"{tag}_up_bwd")
        grads[w_down] = _matmul(a, df, "tn", BF16, f"{tag}_dw_down", tm=1408, tn=1024, tk=DW_TK)
        started = exchange_start([w_down], w_down)
        grads[w_gu] = _dw_gu(dgu, h, f"{tag}_dw_gu", deps=(started,))
        started = exchange_start([w_gu], w_gu)
        halves = [(dgu, (None, fuse_tm, D_FF), lambda i, half=half: (half, i, 0), (half * D_FF, (half + 1) * D_FF))
                  for half in range(2)]
        dx, norm_sums[2 * k] = _matmul_pre_bwd(halves, full[w_gu], xin, dout, row(g_pre[k]),
                                                                mod[k, 0], mod[k, 1], f"{tag}_dh", deps=(started,))
        return dx

    dx2 = ffn_bwd(x2, 2, saved3, dy, "ffn2")

    df2, datt, drec, dg_att, dg_rec, norm_sums[3] = _post_bwd_merge_bwd(
        f2, dx2, row(g_post[1]), mod[1, 2], res_w[1], full["w_out"], att, rec, proj, "mix_dmerged")
    grads["w_out"] = _matmul(merged, df2, "tn", BF16, "mix_dw_out", tm=1024, tn=1024, tk=DW_TK)
    datt_o = _matmul(datt, full["w_att_o"], "nt", BF16, "att_out_bwd")
    grads["w_att_o"] = _matmul(att_o, datt, "tn", BF16, "dw_att_o", tm=512, tn=1024, tk=DW_TK)
    grads["w_rec_o"] = _matmul(rec_in, drec, "tn", BF16, "dw_rec_o", tm=1024, tn=1024, tk=DW_TK)
    started = exchange_start(["w_out", "w_att_o", "w_rec_o"], "mix_out")
    dhs, dyr = _matmul_recin_bwd(drec, full["w_rec_o"], hs, proj, "rec_out_bwd", deps=(started,))
    g_t = _scan_bwd(a_t, dhs, "lru_scan_bwd")
    dpre, dxc, lru_sums = _lru_back(pre, xc, w_slab, lru_ba, lru_bx, lru_lambda, g_t, h_prev, "lru_back")
    dxr, conv_sums = _conv_bwd(proj, conv_w8, dxc, "conv_bwd")
    dq, dk, dv, dbias = _attn_bwd(proj, bias, datt_o, "attn_bwd")
    dproj = jnp.concatenate([dq, dk, dv, dxr, dyr, dg_att, dg_rec], axis=1)
    grads["w_in"] = _matmul(dproj, h2, "tn", BF16, "mix_dw_in", tm=1408, tn=1024, tk=DW_TK)
    pack_mix = jnp.concatenate([conv_sums, lru_sums, _pad_rows(_bias_grad(dbias, "bias_grad"), V7X_SUBLANES),
                                _lru_dw(xc, dpre, "lru_dw").reshape(128, D_MODEL)], axis=0)
    (mix_started,), started = _push_start([[pack_mix]], False, "small_grads_mix_start")
    started = exchange_start(["w_in"], "w_in", after=(started,))
    whole = [(dproj, (fuse_tm, PROJ_WIDTH), lambda i: (i, 0), (0, PROJ_WIDTH))]
    dx1, norm_sums[2] = _matmul_pre_bwd(whole, full["w_in"], x1, dx2, row(g_pre[1]), mod[1, 0],
                                                             mod[1, 1], "mix_dh", deps=(started,))

    dx0 = ffn_bwd(x0, 0, saved1, dx1, "ffn1")

    pack_norm = jnp.concatenate(norm_sums, axis=0)
    (norm_started,), _ = _push_start([[pack_norm]], False, "small_grads_norm_start")

    def summed(started, pack, after, tag):
        _, (parts,) = _push_wait(started, False, after, f"small_grads_{tag}_wait")
        return parts, _sum_parts(parts, f"small_grads_{tag}_sum")

    done = dx0
    last = pending[-1:]
    for names, send, group, tag in pending[:-1]:
        done = exchange_finish(names, send, group, tag, done)

    _, total = summed(mix_started, pack_mix, done, "mix")
    grads["conv_w"] = _my_cols(total[0:4], me, 128)
    grads["conv_b"] = total[4:5]
    grads["lru_ba"] = total[8:9]
    grads["lru_bx"] = total[9:10]
    grads["lru_lambda"] = total[10:11]
    grads["rel_bias"] = total[16:19].reshape(-1)[: ATT_HEADS * (2 * MAX_REL + 1)].reshape(ATT_HEADS, -1)
    grads["lru_wa"] = total[24:88].reshape(LRU_BLOCKS, LRU_BLOCK, LRU_BLOCK)
    grads["lru_wx"] = total[88:152].reshape(LRU_BLOCKS, LRU_BLOCK, LRU_BLOCK)
    parts, total = summed(norm_started, pack_norm, total, "norm")
    by_sandwich = lambda v: v.reshape(*v.shape[:-2], 3, 2 * V7X_SUBLANES, D_MODEL)
    dmod_of = lambda v: jnp.concatenate([by_sandwich(v)[..., 1:3, :], by_sandwich(v)[..., 9:10, :]], axis=-2)
    grads["b_ada"] = dmod_of(total).reshape(1, -1)
    grads["norm_pre"] = _my_cols(by_sandwich(total)[:, 0, :], me, 128)
    grads["norm_post"] = _my_cols(by_sandwich(total)[:, V7X_SUBLANES, :], me, 128)
    dmod_all = dmod_of(parts).reshape(N_DEV, 9 * D_MODEL)
    grads["w_ada"] = _ada_bwd(c_all, _my_cols(dmod_all, me, 1152), "ada_bwd")

    res = _adamw(grads["w_ada"], w_ada[0], m_w_ada[0], v_w_ada[0], "adamw_w_ada")
    out_g["w_ada"], out_d["w_ada"], out_m["w_ada"], out_v["w_ada"] = [r.reshape(w_ada.shape) for r in res]

    sizes = [int(np.prod(weights[n].shape)) for n in small]
    tot = sum(sizes)
    rows_small = -(-tot // (16 * D_MODEL)) * 16
    flat = lambda arrs: jnp.pad(jnp.concatenate([a.reshape(-1) for a in arrs]),
                                (0, rows_small * D_MODEL - tot)).reshape(rows_small, D_MODEL)
    res = _adamw(flat([grads[n] for n in small]), flat([weights[n] for n in small]),
                 flat([mom1[n] for n in small]), flat([mom2[n] for n in small]), "adamw_small", rows=rows_small)
    offs = np.cumsum([0] + sizes)
    for dst, r in zip((out_g, out_d, out_m, out_v), res):
        rf = r.reshape(-1)
        for i, n in enumerate(small):
            dst[n] = rf[offs[i] : offs[i + 1]].reshape(weights[n].shape)

    done = res[0]
    for names, send, group, tag in last:
        done = exchange_finish(names, send, group, tag, done)

    return (loss, dx0[None], *[out_g[n] for n in order], *[out_d[n] for n in order],
            *[out_m[n] for n in order], *[out_v[n] for n in order])
```

```python
import jax
import jax.numpy as jnp
import numpy as np
from jax import lax
from jax.experimental import pallas as pl
from jax.experimental.pallas import tpu as pltpu

D_MODEL = 1024
D_FF = 2816
ATT_HEADS = 8
ATT_HEAD_DIM = 64
ATT_WIDTH = 512
CHUNK = 64
LEFT_CHUNKS = 8
MAX_REL = 128
LRU_WIDTH = 1024
LRU_BLOCKS = 16
LRU_BLOCK = 64
LRU_C = 8.0
EPS = 1e-6
PROJ_WIDTH = 5632
N_DEV = 8

ADAM_LR = 0.001
ADAM_B1 = 0.9
ADAM_B2 = 0.999
ADAM_EPS = 1e-08
ADAM_WD = 0.01
ADAM_STEP = 10

V7X_LANES = 128
V7X_SUBLANES = 8
V7X_VMEM_BYTES = 64 * 1024 * 1024
VMEM_LIMIT = V7X_VMEM_BYTES - 8 * 1024 * 1024

ATT_TQ = 256
NEG = -1e30
BF16 = jnp.bfloat16
F32 = jnp.float32
MESH = pl.DeviceIdType.MESH

OFF_Q = 4 * LRU_WIDTH
OFF_K = OFF_Q + ATT_WIDTH
OFF_V = OFF_K + ATT_WIDTH


def _cparams(**kw):
    return pltpu.CompilerParams(vmem_limit_bytes=VMEM_LIMIT, **kw)


def _pick(n, target, unit=V7X_LANES):
    best = None
    for t in range(unit, min(n, target) + 1, unit):
        if n % t == 0:
            best = t
    return n if best is None else best


_DIMS = {
    "nn": (((1,), (0,)), ((), ())),
    "nt": (((1,), (1,)), ((), ())),
    "tn": (((0,), (0,)), ((), ())),
}


ANY_SPEC = pl.BlockSpec(memory_space=pl.ANY)


def _matmul(a, b, mode, out_dtype, name, tm=1024, tn=512, tk=1408, deps=(), b_shift=0):
    n_deps = len(deps)
    if mode == "nn":
        (m, k), (k2, n) = a.shape, b.shape
    elif mode == "nt":
        (m, k), (n, k2) = a.shape, b.shape
    else:
        (k, m), (k2, n) = a.shape, b.shape
    assert k == k2, (a.shape, b.shape, mode)
    tm, tn, tk = _pick(m, tm), _pick(n, tn), _pick(k, tk)
    nk = k // tk
    dims = _DIMS[mode]

    def body(a_ref, b_ref, *rest):
        o_ref, scratch = rest[n_deps], rest[n_deps + 1 :]
        p = lax.dot_general(a_ref[...], b_ref[...], dims, preferred_element_type=F32)
        if nk == 1:
            o_ref[...] = p.astype(o_ref.dtype)
        else:
            acc = scratch[0]
            kk = pl.program_id(2)

            @pl.when(kk == 0)
            def _():
                acc[...] = p

            @pl.when(kk > 0)
            def _():
                acc[...] += p

            @pl.when(kk == nk - 1)
            def _():
                o_ref[...] = acc[...].astype(o_ref.dtype)

    if mode == "nn":
        a_spec = pl.BlockSpec((tm, tk), lambda i, j, kk: (i, kk))
        b_spec = pl.BlockSpec((tk, tn), lambda i, j, kk: (kk, j))
    elif mode == "nt":
        a_spec = pl.BlockSpec((tm, tk), lambda i, j, kk: (i, kk))
        b_spec = pl.BlockSpec((tn, tk), lambda i, j, kk: ((j + b_shift) % (n // tn), kk))
    else:
        a_spec = pl.BlockSpec((tk, tm), lambda i, j, kk: (kk, i))
        b_spec = pl.BlockSpec((tk, tn), lambda i, j, kk: (kk, j))
    return pl.pallas_call(
        body,
        name=name,
        grid=(m // tm, n // tn, nk),
        in_specs=[a_spec, b_spec] + [ANY_SPEC] * n_deps,
        out_specs=pl.BlockSpec((tm, tn), lambda i, j, kk: (i, j)),
        out_shape=jax.ShapeDtypeStruct((m, n), out_dtype),
        scratch_shapes=[pltpu.VMEM((tm, tn), F32)] if nk > 1 else [],
        compiler_params=_cparams(dimension_semantics=("parallel", "parallel", "arbitrary")),
    )(a, b, *deps)


def _rowwise(fn, name, params, tiles, outs, accs=(), ts=256, with_index=False, deps=()):
    norm = []
    for t in tiles:
        if not isinstance(t, tuple):
            t = (t, t.shape[1], 0)
        norm.append(t if len(t) == 4 else (*t, None))
    s = norm[0][0].shape[0]
    ts = min(ts, s)
    assert s % ts == 0 and ts % V7X_SUBLANES == 0
    steps = s // ts
    halo_blocks = ts // V7X_SUBLANES
    n_p, n_t, n_o = len(params), len(norm), len(outs)

    def body(*refs):
        i = pl.program_id(0)
        vals = [r[...] for r in refs[: n_p + n_t]]
        res = fn(i, steps, *vals) if with_index else fn(*vals)
        if not isinstance(res, (tuple, list)):
            res = (res,)
        first_out = n_p + n_t + len(deps)
        o_refs = refs[first_out : first_out + n_o]
        a_refs = refs[first_out + n_o :]
        for r, v in zip(o_refs, res[:n_o]):
            r[...] = v.astype(r.dtype)
        for r, v in zip(a_refs, res[n_o:]):
            _accumulate(r, v, i)

    in_specs = [pl.BlockSpec(p.shape, lambda i: (0, 0)) for p in params]
    for arr, w, cb, halo in norm:
        if halo is None:
            in_specs.append(pl.BlockSpec((ts, w), lambda i, cb=cb: (i, cb)))
        elif halo == "prev":
            in_specs.append(
                pl.BlockSpec((V7X_SUBLANES, w), lambda i, cb=cb: (jnp.maximum(i * halo_blocks - 1, 0), cb))
            )
        else:
            last = s // V7X_SUBLANES - 1
            in_specs.append(
                pl.BlockSpec((V7X_SUBLANES, w), lambda i, cb=cb: (jnp.minimum((i + 1) * halo_blocks, last), cb))
            )
    in_specs += [ANY_SPEC] * len(deps)
    out_specs = [pl.BlockSpec((ts, w), lambda i: (i, 0)) for w, _ in outs]
    out_specs += [pl.BlockSpec(shape, lambda i: (0, 0)) for shape in accs]
    out_shape = [jax.ShapeDtypeStruct((s, w), dt) for w, dt in outs]
    out_shape += [jax.ShapeDtypeStruct(shape, F32) for shape in accs]
    res = pl.pallas_call(
        body,
        name=name,
        grid=(steps,),
        in_specs=in_specs,
        out_specs=out_specs,
        out_shape=out_shape,
        compiler_params=_cparams(dimension_semantics=("arbitrary",)),
    )(*params, *[t[0] for t in norm], *deps)
    return res


def _accumulate(ref, val, step):
    @pl.when(step == 0)
    def _():
        ref[...] = val

    @pl.when(step > 0)
    def _():
        ref[...] += val


def _sigmoid(z):
    return jax.nn.sigmoid(z)


def _silu(z):
    return z * _sigmoid(z)


def _gelu(z):
    return 0.5 * z * (1.0 + jnp.tanh(0.7978845608028654 * (z + 0.044715 * (z * z * z))))


def _pre_fn(g, shift, scale, x):
    r = lax.rsqrt(jnp.mean(x * x, axis=-1, keepdims=True) + EPS)
    return ((x * r) * g) * (1.0 + scale) + shift


def _post_fn(res_w, g, gate, f, x):
    r = lax.rsqrt(jnp.mean(f * f, axis=-1, keepdims=True) + EPS)
    return x + (res_w * gate) * ((f * r) * g)


def _gates_fn(ba, bx, lam, pre, xc):
    ra = _sigmoid(pre[:, :LRU_WIDTH] + ba)
    ia = _sigmoid(pre[:, LRU_WIDTH:] + bx)
    softplus = jnp.maximum(-lam, 0.0) + jnp.log1p(jnp.exp(-jnp.abs(lam)))
    log_a = (-LRU_C) * ra * softplus
    a = jnp.exp(log_a)
    mult = jnp.sqrt(-jnp.tanh(log_a) * (a * a + 1.0))
    return a, mult * (ia * xc)


def _recin_fn(hs, yr):
    return hs * _gelu(yr)


def _merge_fn(att, rec, g_att, g_rec):
    return _sigmoid(g_att) * att + _sigmoid(g_rec) * rec


def _rowsum(v):
    return jnp.sum(v, axis=0, keepdims=True)


FFN_TM = 512
FFN_TF = 1408


def _glu_fn(g, u):
    return _silu(g) * u


FUSE_TM = 256
DW_TK = 2048
ROW_SPEC2 = pl.BlockSpec((1, D_MODEL), lambda i, j: (0, 0))
ROW_SPEC1 = pl.BlockSpec((1, D_MODEL), lambda i: (0, 0))
SUMS_SPEC1 = pl.BlockSpec((V7X_SUBLANES, D_MODEL), lambda i: (0, 0))
SUMS_SPEC2 = pl.BlockSpec((V7X_SUBLANES, D_MODEL), lambda i, j: (0, 0))
SUMS_SHAPE = jax.ShapeDtypeStruct((V7X_SUBLANES, D_MODEL), F32)


def _sum_rows(*rows):
    pad = jnp.zeros((V7X_SUBLANES - len(rows), rows[0].shape[1]), F32)
    return jnp.concatenate([*rows, pad], axis=0)


def _pre_up(x, g, shift, scale, w_gu_t, name, deps=(), tm=FFN_TM):
    s = x.shape[0]
    tm = min(tm, s)
    nf = D_FF // FFN_TF
    nd = len(deps)

    def body(x_ref, g_ref, sh_ref, sc_ref, wg_ref, wu_ref, *rest):
        h_ref, a_ref, gg_ref, u_ref, h_s = rest[nd:]

        @pl.when(pl.program_id(1) == 0)
        def _():
            h = _pre_fn(g_ref[...], sh_ref[...], sc_ref[...], x_ref[...]).astype(BF16)
            h_s[...] = h
            h_ref[...] = h

        hv = h_s[...]
        gv = lax.dot_general(hv, wg_ref[...], _DIMS["nt"], preferred_element_type=F32)
        uv = lax.dot_general(hv, wu_ref[...], _DIMS["nt"], preferred_element_type=F32)
        a_ref[...] = _glu_fn(gv, uv).astype(a_ref.dtype)
        gg_ref[...] = gv.astype(gg_ref.dtype)
        u_ref[...] = uv.astype(u_ref.dtype)

    rows = pl.BlockSpec((tm, D_MODEL), lambda i, j: (i, 0))
    out = pl.BlockSpec((tm, FFN_TF), lambda i, j: (i, j))
    return pl.pallas_call(
        body,
        name=name,
        grid=(s // tm, nf),
        in_specs=[rows, ROW_SPEC2, ROW_SPEC2, ROW_SPEC2,
                  pl.BlockSpec((FFN_TF, D_MODEL), lambda i, j: (j, 0)),
                  pl.BlockSpec((FFN_TF, D_MODEL), lambda i, j: (nf + j, 0))] + [ANY_SPEC] * nd,
        out_specs=[rows, out, out, out],
        out_shape=[jax.ShapeDtypeStruct((s, D_MODEL), BF16)] + [jax.ShapeDtypeStruct((s, D_FF), BF16)] * 3,
        scratch_shapes=[pltpu.VMEM((tm, D_MODEL), BF16)],
        compiler_params=_cparams(dimension_semantics=("parallel", "arbitrary")),
    )(x, g, shift, scale, w_gu_t, w_gu_t, *deps)


def _pre_matmul(x, g, shift, scale, w_t, name, b_shift=0, tn=512):
    s = x.shape[0]
    n = w_t.shape[0]
    tm = min(2 * FFN_TM, s)

    def body(x_ref, g_ref, sh_ref, sc_ref, w_ref, h_ref, o_ref, h_s):
        @pl.when(pl.program_id(1) == 0)
        def _():
            h = _pre_fn(g_ref[...], sh_ref[...], sc_ref[...], x_ref[...]).astype(BF16)
            h_s[...] = h
            h_ref[...] = h

        o_ref[...] = lax.dot_general(h_s[...], w_ref[...], _DIMS["nt"], preferred_element_type=F32)

    rows = pl.BlockSpec((tm, D_MODEL), lambda i, j: (i, 0))
    return pl.pallas_call(
        body,
        name=name,
        grid=(s // tm, n // tn),
        in_specs=[rows, ROW_SPEC2, ROW_SPEC2, ROW_SPEC2,
                  pl.BlockSpec((tn, D_MODEL), lambda i, j: ((j + b_shift) % (n // tn), 0))],
        out_specs=[rows, pl.BlockSpec((tm, tn), lambda i, j: (i, j))],
        out_shape=[jax.ShapeDtypeStruct((s, D_MODEL), BF16), jax.ShapeDtypeStruct((s, n), F32)],
        scratch_shapes=[pltpu.VMEM((tm, D_MODEL), BF16)],
        compiler_params=_cparams(dimension_semantics=("parallel", "arbitrary")),
    )(x, g, shift, scale, w_t)


def _matmul_post(a, w, x, g_post, gate, res_w, name, target=None, tm=FFN_TM):
    s, k = a.shape
    tm = min(tm, s)
    extra = [] if target is None else [target]

    def body(a_ref, w_ref, x_ref, g_ref, gate_ref, *rest):
        f = jnp.dot(a_ref[...], w_ref[...], preferred_element_type=F32)
        y = _post_fn(res_w, g_ref[...], gate_ref[...], f, x_ref[...])
        if target is None:
            f_ref, y_ref = rest
            y_ref[...] = y
        else:
            t_ref, f_ref, dy_ref, sq_ref = rest
            diff = y - t_ref[...]
            dy_ref[...] = diff * (1.0 / D_MODEL)
            _accumulate(sq_ref, _rowsum(diff * diff), pl.program_id(0))
        f_ref[...] = f

    rows = pl.BlockSpec((tm, D_MODEL), lambda i: (i, 0))
    out_specs, out_shape = [rows, rows], [jax.ShapeDtypeStruct((s, D_MODEL), F32)] * 2
    if target is not None:
        out_specs.append(ROW_SPEC1)
        out_shape.append(jax.ShapeDtypeStruct((1, D_MODEL), F32))
    return pl.pallas_call(
        body,
        name=name,
        grid=(s // tm,),
        in_specs=[pl.BlockSpec((tm, k), lambda i: (i, 0)), pl.BlockSpec((k, D_MODEL), lambda i: (0, 0)), rows,
                  ROW_SPEC1, ROW_SPEC1] + [rows] * len(extra),
        out_specs=out_specs,
        out_shape=out_shape,
        compiler_params=_cparams(dimension_semantics=("arbitrary",)),
    )(a, w, x, g_post, gate, *extra)


def _merge_matmul_post(att, rec, proj, w, x, g_post, gate, res_w, name):
    s = att.shape[0]
    tm = min(FUSE_TM, s)

    def body(att_ref, rec_ref, ga_ref, gr_ref, w_ref, x_ref, g_ref, gate_ref, m_ref, f_ref, y_ref):
        merged = _merge_fn(att_ref[...], rec_ref[...], ga_ref[...], gr_ref[...]).astype(BF16)
        m_ref[...] = merged
        f = jnp.dot(merged, w_ref[...], preferred_element_type=F32)
        f_ref[...] = f
        y_ref[...] = _post_fn(res_w, g_ref[...], gate_ref[...], f, x_ref[...])

    rows = pl.BlockSpec((tm, D_MODEL), lambda i: (i, 0))
    return pl.pallas_call(
        body,
        name=name,
        grid=(s // tm,),
        in_specs=[rows, rows, pl.BlockSpec((tm, D_MODEL), lambda i: (i, 2)), pl.BlockSpec((tm, D_MODEL), lambda i: (i, 3)),
                  pl.BlockSpec(w.shape, lambda i: (0, 0)), rows, ROW_SPEC1, ROW_SPEC1],
        out_specs=[rows, rows, rows],
        out_shape=[jax.ShapeDtypeStruct((s, D_MODEL), BF16)] + [jax.ShapeDtypeStruct((s, D_MODEL), F32)] * 2,
        compiler_params=_cparams(dimension_semantics=("parallel",)),
    )(att, rec, proj, proj, w, x, g_post, gate)


def _post_bwd_merge_bwd(f, dy, g_post, gate, res_w, w, att, rec, proj, name):
    s = f.shape[0]
    tm = min(FUSE_TM, s)

    def body(f_ref, dy_ref, gp_ref, gate_ref, w_ref, att_ref, rec_ref, ga_ref, gr_ref,
             df_ref, datt_ref, drec_ref, dga_ref, dgr_ref, sums_ref):
        i = pl.program_id(0)
        dgp, dgate, df = _post_vjp(res_w, gp_ref[...], gate_ref[...], f_ref[...], dy_ref[...])
        dfb = df.astype(BF16)
        df_ref[...] = dfb
        _accumulate(sums_ref, _sum_rows(dgp, dgate), i)
        dmerged = lax.dot_general(dfb, w_ref[...], _DIMS["nt"], preferred_element_type=F32)
        _, vjp = jax.vjp(_merge_fn, att_ref[...], rec_ref[...], ga_ref[...], gr_ref[...])
        for ref, val in zip((datt_ref, drec_ref, dga_ref, dgr_ref), vjp(dmerged)):
            ref[...] = val.astype(ref.dtype)

    rows = pl.BlockSpec((tm, D_MODEL), lambda i: (i, 0))
    return pl.pallas_call(
        body,
        name=name,
        grid=(s // tm,),
        in_specs=[rows, rows, ROW_SPEC1, ROW_SPEC1, pl.BlockSpec(w.shape, lambda i: (0, 0)), rows, rows,
                  pl.BlockSpec((tm, D_MODEL), lambda i: (i, 2)), pl.BlockSpec((tm, D_MODEL), lambda i: (i, 3))],
        out_specs=[rows] * 5 + [SUMS_SPEC1],
        out_shape=[jax.ShapeDtypeStruct((s, D_MODEL), BF16)] * 5 + [SUMS_SHAPE],
        compiler_params=_cparams(dimension_semantics=("arbitrary",)),
    )(f, dy, g_post, gate, w, att, rec, proj, proj)


def _matmul_recin_bwd(drec, w, hs, proj, name, deps=()):
    s = drec.shape[0]
    tm = min(FUSE_TM, s)
    nd = len(deps)

    def body(d_ref, w_ref, hs_ref, yr_ref, *rest):
        dhs_ref, dyr_ref = rest[nd:]
        d = lax.dot_general(d_ref[...], w_ref[...], _DIMS["nt"], preferred_element_type=F32)
        _, vjp = jax.vjp(_recin_fn, hs_ref[...], yr_ref[...])
        dhs, dyr = vjp(d)
        dhs_ref[...] = dhs
        dyr_ref[...] = dyr.astype(dyr_ref.dtype)

    rows = pl.BlockSpec((tm, D_MODEL), lambda i: (i, 0))
    return pl.pallas_call(
        body,
        name=name,
        grid=(s // tm,),
        in_specs=[rows, pl.BlockSpec(w.shape, lambda i: (0, 0)), rows,
                  pl.BlockSpec((tm, D_MODEL), lambda i: (i, 1))] + [ANY_SPEC] * nd,
        out_specs=[rows, rows],
        out_shape=[jax.ShapeDtypeStruct((s, D_MODEL), F32), jax.ShapeDtypeStruct((s, D_MODEL), BF16)],
        compiler_params=_cparams(dimension_semantics=("parallel",)),
    )(drec, w, hs, proj, *deps)


def _post_vjp(res_w, g, gate, f, dy):
    _, vjp = jax.vjp(lambda g, gate, f: _post_fn(res_w, g, gate, f, 0.0), g, gate, f)
    return vjp(dy)


def _post_bwd_up_bwd(f, dy, g_post, gate, res_w, w_down, g, u, name, deps=()):
    s = f.shape[0]
    tm = min(FFN_TM, s)
    nd = len(deps)

    def body(f_ref, dy_ref, gp_ref, gate_ref, wd_ref, g_ref, u_ref, *rest):
        df_ref, dgu_ref, sums_ref, df_s = rest[nd:]
        i = pl.program_id(0)

        @pl.when(pl.program_id(1) == 0)
        def _():
            dgp, dgate, df = _post_vjp(res_w, gp_ref[...], gate_ref[...], f_ref[...], dy_ref[...])
            df_s[...] = df.astype(BF16)
            df_ref[...] = df_s[...]
            _accumulate(sums_ref, _sum_rows(dgp, dgate), i)

        da = lax.dot_general(df_s[...], wd_ref[...], _DIMS["nt"], preferred_element_type=F32)
        _, vjp = jax.vjp(_glu_fn, g_ref[...].astype(F32), u_ref[...].astype(F32))
        dg, du = vjp(da)
        dgu_ref[0] = dg.astype(dgu_ref.dtype)
        dgu_ref[1] = du.astype(dgu_ref.dtype)

    rows = pl.BlockSpec((tm, D_MODEL), lambda i, j: (i, 0))
    blk = pl.BlockSpec((tm, FFN_TF), lambda i, j: (i, j))
    return pl.pallas_call(
        body,
        name=name,
        grid=(s // tm, D_FF // FFN_TF),
        in_specs=[rows, rows, ROW_SPEC2, ROW_SPEC2, pl.BlockSpec((FFN_TF, D_MODEL), lambda i, j: (j, 0)), blk,
                  blk] + [ANY_SPEC] * nd,
        out_specs=[rows, pl.BlockSpec((2, tm, FFN_TF), lambda i, j: (0, i, j)), SUMS_SPEC2],
        out_shape=[jax.ShapeDtypeStruct((s, D_MODEL), BF16), jax.ShapeDtypeStruct((2, s, D_FF), BF16), SUMS_SHAPE],
        scratch_shapes=[pltpu.VMEM((tm, D_MODEL), BF16)],
        compiler_params=_cparams(dimension_semantics=("arbitrary", "arbitrary")),
    )(f, dy, g_post, gate, w_down, g, u, *deps)


def _matmul_pre_bwd(parts, w_t, x, dres, g, shift, scale, name, deps=()):
    s = x.shape[0]
    na, nd = len(parts), len(deps)
    ranges = [p[3] for p in parts]

    def body(*refs):
        a_refs = refs[:na]
        w_ref, x_ref, dres_ref, g_ref, sh_ref, sc_ref = refs[na : na + 6]
        dx_ref, sums_ref = refs[na + 6 + nd :]
        i = pl.program_id(0)
        dh = None
        for a_ref, (r0, r1) in zip(a_refs, ranges):
            p = jnp.dot(a_ref[...], w_ref[r0:r1, :], preferred_element_type=F32)
            dh = p if dh is None else dh + p
        _, vjp = jax.vjp(_pre_fn, g_ref[...], sh_ref[...], sc_ref[...], x_ref[...])
        dg, dsh, dsc, dx = vjp(dh)
        dx_ref[...] = dx + dres_ref[...]
        _accumulate(sums_ref, _sum_rows(dg, dsh, dsc), i)

    tm = parts[0][1][-2]
    rows = pl.BlockSpec((tm, D_MODEL), lambda i: (i, 0))
    return pl.pallas_call(
        body,
        name=name,
        grid=(s // tm,),
        in_specs=[pl.BlockSpec(p[1], p[2]) for p in parts]
        + [pl.BlockSpec(w_t.shape, lambda i: (0, 0)), rows, rows, ROW_SPEC1, ROW_SPEC1, ROW_SPEC1]
        + [ANY_SPEC] * nd,
        out_specs=[rows, SUMS_SPEC1],
        out_shape=[jax.ShapeDtypeStruct((s, D_MODEL), F32), SUMS_SHAPE],
        compiler_params=_cparams(dimension_semantics=("arbitrary",)),
    )(*[p[0] for p in parts], w_t, x, dres, g, shift, scale, *deps)


def _dw_gu(dgu, h, name, deps=(), tk=DW_TK):
    s = h.shape[0]
    tk = min(tk, s)
    nk = s // tk
    half = D_FF // FFN_TF

    def body(a_ref, b_ref, *rest):
        o_ref = rest[len(deps)]
        kk = pl.program_id(1)
        p = lax.dot_general(a_ref[...], b_ref[...], _DIMS["tn"], preferred_element_type=F32)
        if nk == 1:
            o_ref[...] = p.astype(o_ref.dtype)
            return
        acc = rest[len(deps) + 1]

        @pl.when(kk == 0)
        def _():
            acc[...] = p

        @pl.when(kk > 0)
        def _():
            acc[...] += p

        @pl.when(kk == nk - 1)
        def _():
            o_ref[...] = acc[...].astype(o_ref.dtype)

    return pl.pallas_call(
        body,
        name=name,
        grid=(2 * half, nk),
        in_specs=[pl.BlockSpec((None, tk, FFN_TF), lambda i, kk: (i // half, kk, i % half)),
                  pl.BlockSpec((tk, D_MODEL), lambda i, kk: (kk, 0))] + [ANY_SPEC] * len(deps),
        out_specs=pl.BlockSpec((FFN_TF, D_MODEL), lambda i, kk: (i, 0)),
        out_shape=jax.ShapeDtypeStruct((2 * D_FF, D_MODEL), BF16),
        scratch_shapes=[pltpu.VMEM((FFN_TF, D_MODEL), F32)] if nk > 1 else [],
        compiler_params=_cparams(dimension_semantics=("parallel", "arbitrary")),
    )(dgu, h, *deps)


def _shift_down(ext, j, rows):
    return pltpu.roll(ext, j, 0)[V7X_SUBLANES : V7X_SUBLANES + rows]


def _shift_up(ext, j, rows):
    return pltpu.roll(ext, ext.shape[0] - j, 0)[:rows] if j else ext[:rows]


LRU_SLAB = 256
N_SLABS = LRU_WIDTH // LRU_SLAB


def _slab_weights(wa, wx):
    per = LRU_SLAB // LRU_BLOCK
    eye = jnp.eye(per, dtype=wa.dtype)

    def diag(w):
        w4 = w.reshape(N_SLABS, per, LRU_BLOCK, LRU_BLOCK)
        return jnp.einsum("sbkj,bc->sbkcj", w4, eye).reshape(N_SLABS, LRU_SLAB, LRU_SLAB)

    return jnp.concatenate([diag(wa), diag(wx)], axis=2).reshape(LRU_WIDTH, 2 * LRU_SLAB).astype(BF16)


def _slab_cols(v, s):
    lo = s * LRU_SLAB
    return jnp.concatenate([v[:, lo : lo + LRU_SLAB], v[:, LRU_WIDTH + lo : LRU_WIDTH + lo + LRU_SLAB]], axis=1)


def _lru_front(proj, w8, b, w_slab, ba, bx, lam, name):
    def fn(i, steps, w8, b, w_slab, ba, bx, lam, x, halo):
        halo = jnp.where(i > 0, halo, 0.0)
        ext = jnp.concatenate([halo, x], axis=0)
        xc = b + w8[3:4] * x
        for j in (1, 2, 3):
            xc = xc + w8[3 - j : 4 - j] * _shift_down(ext, j, x.shape[0])
        xcb = xc.astype(BF16)
        prods = []
        for s in range(N_SLABS):
            rows = slice(s * LRU_SLAB, (s + 1) * LRU_SLAB)
            prods.append(jnp.dot(xcb[:, rows], w_slab[rows], preferred_element_type=F32))
        pre = jnp.concatenate([p[:, :LRU_SLAB] for p in prods] + [p[:, LRU_SLAB:] for p in prods], axis=1)
        a, u = _gates_fn(ba, bx, lam, pre, xc)
        return xc, pre, a, u

    tiles = [(proj, LRU_WIDTH, 0), (proj, LRU_WIDTH, 0, "prev")]
    outs = [(LRU_WIDTH, F32), (2 * LRU_WIDTH, F32), (LRU_WIDTH, F32), (LRU_WIDTH, F32)]
    return _rowwise(fn, name, [w8, b, w_slab, ba, bx, lam], tiles, outs, with_index=True)


def _lru_back(pre, xc, w_slab, ba, bx, lam, g, h_prev, name, deps=()):
    def fn(w_slab, ba, bx, lam, pre, xc, g, h_prev):
        _, vjp = jax.vjp(_gates_fn, ba, bx, lam, pre, xc)
        dba, dbx, dlam, dpre, dxc = vjp((g * h_prev, g))
        dpre = dpre.astype(BF16)
        back = []
        for s in range(N_SLABS):
            rows = slice(s * LRU_SLAB, (s + 1) * LRU_SLAB)
            back.append(lax.dot_general(_slab_cols(dpre, s), w_slab[rows], _DIMS["nt"], preferred_element_type=F32))
        return dpre, dxc + jnp.concatenate(back, axis=1), _sum_rows(dba, dbx, dlam)

    return _rowwise(fn, name, [w_slab, ba, bx, lam], [pre, xc, g, h_prev],
                    [(2 * LRU_WIDTH, BF16), (LRU_WIDTH, F32)], [(V7X_SUBLANES, LRU_WIDTH)], deps=deps)


def _lru_dw(xc, dpre, name):
    s = xc.shape[0]
    ts = min(512, s)
    steps = s // ts
    per = LRU_SLAB // LRU_BLOCK

    def body(x_ref, d_ref, o_ref, acc):
        i = pl.program_id(0)
        xcb = x_ref[...].astype(BF16)
        d = d_ref[...]
        for sl in range(N_SLABS):
            rows = slice(sl * LRU_SLAB, (sl + 1) * LRU_SLAB)
            p = lax.dot_general(xcb[:, rows], _slab_cols(d, sl), _DIMS["tn"], preferred_element_type=F32)

            @pl.when(i == 0)
            def _(p=p, rows=rows):
                acc[rows, :] = p

            @pl.when(i > 0)
            def _(p=p, rows=rows):
                acc[rows, :] += p

        @pl.when(i == steps - 1)
        def _():
            for half in range(2):
                for n in range(LRU_BLOCKS):
                    r0 = n * LRU_BLOCK
                    c0 = half * LRU_SLAB + (n % per) * LRU_BLOCK
                    o_ref[half, r0 : r0 + LRU_BLOCK, :] = acc[r0 : r0 + LRU_BLOCK, c0 : c0 + LRU_BLOCK]

    return pl.pallas_call(
        body,
        name=name,
        grid=(steps,),
        in_specs=[pl.BlockSpec((ts, LRU_WIDTH), lambda i: (i, 0)), pl.BlockSpec((ts, 2 * LRU_WIDTH), lambda i: (i, 0))],
        out_specs=pl.BlockSpec((2, LRU_WIDTH, LRU_BLOCK), lambda i: (0, 0, 0)),
        out_shape=jax.ShapeDtypeStruct((2, LRU_WIDTH, LRU_BLOCK), F32),
        scratch_shapes=[pltpu.VMEM((LRU_WIDTH, 2 * LRU_SLAB), F32)],
        compiler_params=_cparams(dimension_semantics=("arbitrary",)),
    )(xc, dpre)


def _conv_bwd(proj, w8, d1, name):
    def fn(i, steps, w8, x, halo, d, d1n):
        rows = x.shape[0]
        dn = jnp.where(i < steps - 1, d1n, 0.0)
        halo = jnp.where(i > 0, halo, 0.0)
        dext = jnp.concatenate([d, dn], axis=0)
        xext = jnp.concatenate([halo, x], axis=0)
        dx = w8[3:4] * d
        dw = [None] * 4
        dw[3] = _rowsum(d * x)
        for k in (1, 2, 3):
            dx = dx + w8[3 - k : 4 - k] * _shift_up(dext, k, rows)
            dw[3 - k] = _rowsum(d * _shift_down(xext, k, rows))
        return dx, _sum_rows(*dw, _rowsum(d))

    tiles = [(proj, LRU_WIDTH, 0), (proj, LRU_WIDTH, 0, "prev"), d1, (d1, LRU_WIDTH, 0, "next")]
    return _rowwise(fn, name, [w8], tiles, [(LRU_WIDTH, BF16)], [(V7X_SUBLANES, LRU_WIDTH)], with_index=True)


SCAN_ROWS = 512


def _block_scan(a, b, row, reverse):
    for d in (1, 2, 4):
        if reverse:
            shift, keep = V7X_SUBLANES - d, row < V7X_SUBLANES - d
        else:
            shift, keep = d, row >= d
        a_s = pltpu.roll(a, shift, 0)
        b_s = pltpu.roll(b, shift, 0)
        b = jnp.where(keep, a * b_s + b, b)
        a = jnp.where(keep, a * a_s, a)
    return a, b


def _scan_fwd(a, u, proj, name):
    s, w = a.shape
    ts = min(SCAN_ROWS, s)
    sub = ts // V7X_SUBLANES

    def body(a_ref, u_ref, yr_ref, h_ref, hp_ref, rec_ref, carry):
        @pl.when(pl.program_id(0) == 0)
        def _():
            carry[...] = jnp.zeros_like(carry)

        row = lax.broadcasted_iota(jnp.int32, (V7X_SUBLANES, w), 0)

        def step(j, c):
            rows = pl.ds(pl.multiple_of(j * V7X_SUBLANES, V7X_SUBLANES), V7X_SUBLANES)
            pa, pb = _block_scan(a_ref[rows, :], u_ref[rows, :], row, False)
            h = pb + pa * c
            h_ref[rows, :] = h
            hp_ref[rows, :] = jnp.where(row >= 1, pltpu.roll(h, 1, 0), c)
            return jnp.broadcast_to(h[V7X_SUBLANES - 1 :], (V7X_SUBLANES, w))

        carry[...] = lax.fori_loop(0, sub, step, carry[...])
        rec_ref[...] = _recin_fn(h_ref[...], yr_ref[...]).astype(rec_ref.dtype)

    spec = pl.BlockSpec((ts, w), lambda i: (i, 0))
    return pl.pallas_call(
        body,
        name=name,
        grid=(s // ts,),
        in_specs=[spec, spec, pl.BlockSpec((ts, w), lambda i: (i, 1))],
        out_specs=[spec, spec, spec],
        out_shape=[jax.ShapeDtypeStruct((s, w), F32)] * 2 + [jax.ShapeDtypeStruct((s, w), BF16)],
        scratch_shapes=[pltpu.VMEM((V7X_SUBLANES, w), F32)],
        compiler_params=_cparams(dimension_semantics=("arbitrary",)),
    )(a, u, proj)


def _scan_bwd(a, dh, name):
    s, w = a.shape
    ts = min(SCAN_ROWS, s)
    sub = ts // V7X_SUBLANES
    steps = s // ts

    def body(a_ref, d_ref, g_ref, carry):
        @pl.when(pl.program_id(0) == 0)
        def _():
            carry[...] = jnp.zeros_like(carry)

        row = lax.broadcasted_iota(jnp.int32, (V7X_SUBLANES, w), 0)

        def step(jj, c):
            j = sub - 1 - jj
            rows = pl.ds(pl.multiple_of(j * V7X_SUBLANES, V7X_SUBLANES), V7X_SUBLANES)
            av, dv = a_ref[rows, :], d_ref[rows, :]
            pa, pb = _block_scan(av, av * dv, row, True)
            big = pb + pa * c
            g_ref[rows, :] = dv + jnp.where(row < V7X_SUBLANES - 1, pltpu.roll(big, V7X_SUBLANES - 1, 0), c)
            return jnp.broadcast_to(big[:1], (V7X_SUBLANES, w))

        carry[...] = lax.fori_loop(0, sub, step, carry[...])

    spec = pl.BlockSpec((ts, w), lambda i: (steps - 1 - i, 0))
    return pl.pallas_call(
        body,
        name=name,
        grid=(steps,),
        in_specs=[spec, spec],
        out_specs=spec,
        out_shape=jax.ShapeDtypeStruct((s, w), F32),
        scratch_shapes=[pltpu.VMEM((V7X_SUBLANES, w), F32)],
        compiler_params=_cparams(dimension_semantics=("arbitrary",)),
    )(a, dh)


SKEW = 4 * ATT_TQ


def _skew_onehot():
    t = np.arange(SKEW)
    diag = np.where(t < 3 * ATT_TQ, -t, SKEW - t)
    idx = np.clip(diag + LEFT_CHUNKS * CHUNK, -MAX_REL, MAX_REL) + MAX_REL
    hit = (idx[:, None] == np.arange(2 * MAX_REL + 1)[None, :]) & (t[:, None] != 3 * ATT_TQ)
    return hit.astype(np.float32)


def _bias_tile(rel_bias, name):
    per_t = jnp.dot(rel_bias, jnp.asarray(_skew_onehot()).T, precision=lax.Precision.HIGHEST)
    win = 3 * ATT_TQ

    def body(t_ref, o_ref):
        tile = pltpu.roll(jnp.broadcast_to(t_ref[0], (ATT_TQ, SKEW)), 0, 1, stride=1, stride_axis=0)[:, :win]
        qc = lax.broadcasted_iota(jnp.int32, (ATT_TQ, win), 0) // CHUNK
        kpos = lax.broadcasted_iota(jnp.int32, (ATT_TQ, win), 1)
        band = (kpos // CHUNK >= qc) & (kpos // CHUNK <= qc + LEFT_CHUNKS)
        for v in range(3):
            o_ref[v, 0] = jnp.where(band & (kpos >= (2 - v) * ATT_TQ), tile, NEG)

    return pl.pallas_call(
        body,
        name=name,
        grid=(ATT_HEADS,),
        in_specs=[pl.BlockSpec((1, 1, SKEW), lambda h: (h, 0, 0))],
        out_specs=pl.BlockSpec((3, 1, ATT_TQ, win), lambda h: (0, h, 0, 0)),
        out_shape=jax.ShapeDtypeStruct((3, ATT_HEADS, ATT_TQ, win), F32),
        compiler_params=_cparams(dimension_semantics=("parallel",)),
    )(per_t.reshape(ATT_HEADS, 1, SKEW))


def _bias_grad(dbias, name):
    win = 3 * ATT_TQ

    def body(d_ref, o_ref):
        d = jnp.concatenate([d_ref[0], jnp.zeros((ATT_TQ, SKEW - win), F32)], axis=1)
        r = lax.broadcasted_iota(jnp.int32, (ATT_TQ, ATT_TQ), 0)
        c = lax.broadcasted_iota(jnp.int32, (ATT_TQ, ATT_TQ), 1)
        flip = (r + c == ATT_TQ - 1).astype(F32)
        d = jnp.dot(flip, d, preferred_element_type=F32, precision=lax.Precision.HIGHEST)
        o_ref[0] = jnp.sum(pltpu.roll(d, SKEW - (ATT_TQ - 1), 1, stride=1, stride_axis=0), axis=0, keepdims=True)

    per_t = pl.pallas_call(
        body,
        name=name,
        grid=(ATT_HEADS,),
        in_specs=[pl.BlockSpec((1, ATT_TQ, win), lambda h: (h, 0, 0))],
        out_specs=pl.BlockSpec((1, 1, SKEW), lambda h: (h, 0, 0)),
        out_shape=jax.ShapeDtypeStruct((ATT_HEADS, 1, SKEW), F32),
        compiler_params=_cparams(dimension_semantics=("parallel",)),
    )(dbias)
    return jnp.dot(per_t.reshape(ATT_HEADS, SKEW), jnp.asarray(_skew_onehot()), precision=lax.Precision.HIGHEST)


ATT_STEP_HEADS = ATT_HEADS
ATT_STEP_COLS = ATT_STEP_HEADS * ATT_HEAD_DIM


def _attn_specs(nt):
    qb, kb, vb = OFF_Q // ATT_STEP_COLS, OFF_K // ATT_STEP_COLS, OFF_V // ATT_STEP_COLS
    blk = (ATT_TQ, ATT_STEP_COLS)

    def qmap(base):
        return lambda hp, m: (jnp.minimum(m, nt - 1), base + hp)

    def wmap(base, back):
        return lambda hp, m: (jnp.clip(m - back, 0, nt - 1), base + hp)

    specs = [pl.BlockSpec(blk, qmap(qb))]
    specs += [pl.BlockSpec(blk, wmap(kb, back)) for back in (2, 1, 0)]
    specs += [pl.BlockSpec(blk, wmap(vb, back)) for back in (2, 1, 0)]
    return specs


ATT_SCALE = ATT_HEAD_DIM**-0.5


def _attn_exp(qh, kh, bias):
    s = lax.dot_general(qh, kh, _DIMS["nt"], preferred_element_type=F32) + bias
    e = jnp.exp(s - jnp.max(s, axis=-1, keepdims=True))
    return e, jnp.sum(e, axis=-1, keepdims=True)


def _attn_window(k0, k1, k2, v0, v1, v2):
    k = jnp.concatenate([k0[...], k1[...], k2[...]], axis=0).astype(BF16)
    v = jnp.concatenate([v0[...], v1[...], v2[...]], axis=0).astype(BF16)
    return k, v


def _bias_spec():
    return pl.BlockSpec((1, ATT_STEP_HEADS, ATT_TQ, 3 * ATT_TQ), lambda hp, m: (jnp.minimum(m, 2), hp, 0, 0))


def _attn_fwd(proj, bias, name):
    s = proj.shape[0]
    nt = s // ATT_TQ

    def body(q_ref, k0, k1, k2, v0, v1, v2, b_ref, o_ref):
        k, v = _attn_window(k0, k1, k2, v0, v1, v2)
        q = (q_ref[...] * ATT_SCALE).astype(BF16)
        for hh in range(ATT_STEP_HEADS):
            cols = slice(hh * ATT_HEAD_DIM, (hh + 1) * ATT_HEAD_DIM)
            e, total = _attn_exp(q[:, cols], k[:, cols], b_ref[0, hh])
            o = jnp.dot(e.astype(BF16), v[:, cols], preferred_element_type=F32) / total
            o_ref[:, cols] = o.astype(o_ref.dtype)

    specs = _attn_specs(nt) + [_bias_spec()]
    return pl.pallas_call(
        body,
        name=name,
        grid=(ATT_HEADS // ATT_STEP_HEADS, nt),
        in_specs=specs,
        out_specs=pl.BlockSpec((ATT_TQ, ATT_STEP_COLS), lambda hp, m: (m, hp)),
        out_shape=jax.ShapeDtypeStruct((s, ATT_WIDTH), BF16),
        compiler_params=_cparams(dimension_semantics=("parallel", "arbitrary")),
    )(proj, proj, proj, proj, proj, proj, proj, bias)


def _attn_bwd(proj, bias, do, name):
    s = proj.shape[0]
    nt = s // ATT_TQ
    win = 3 * ATT_TQ

    def body(q_ref, k0, k1, k2, v0, v1, v2, do_ref, b_ref, dq_ref, dk_ref, dv_ref, db_ref, dk_acc, dv_acc):
        m = pl.program_id(1)

        @pl.when(m == 0)
        def _():
            dk_acc[...] = jnp.zeros_like(dk_acc)
            dv_acc[...] = jnp.zeros_like(dv_acc)
            db_ref[...] = jnp.zeros_like(db_ref)

        @pl.when(m < nt)
        def _():
            k, v = _attn_window(k0, k1, k2, v0, v1, v2)
            q = (q_ref[...] * ATT_SCALE).astype(BF16)
            dout = do_ref[...]
            for hh in range(ATT_STEP_HEADS):
                cols = slice(hh * ATT_HEAD_DIM, (hh + 1) * ATT_HEAD_DIM)
                qh, kh, vh, doh = q[:, cols], k[:, cols], v[:, cols], dout[:, cols]
                e, total = _attn_exp(qh, kh, b_ref[0, hh])
                p = e / total
                dvh = lax.dot_general(p.astype(BF16), doh, _DIMS["tn"], preferred_element_type=F32)
                dp = lax.dot_general(doh, vh, _DIMS["nt"], preferred_element_type=F32)
                ds = p * (dp - jnp.sum(dp * p, axis=-1, keepdims=True))
                db_ref[hh] += ds
                dsb = ds.astype(BF16)
                dqh = jnp.dot(dsb, kh, preferred_element_type=F32) * ATT_SCALE
                dkh = lax.dot_general(dsb, qh, _DIMS["tn"], preferred_element_type=F32)
                dq_ref[:, cols] = dqh.astype(dq_ref.dtype)
                dk_acc[:, cols] += dkh
                dv_acc[:, cols] += dvh

        dk_ref[...] = dk_acc[:ATT_TQ].astype(dk_ref.dtype)
        dv_ref[...] = dv_acc[:ATT_TQ].astype(dv_ref.dtype)
        for acc in (dk_acc, dv_acc):
            rest = acc[ATT_TQ:]
            acc[: win - ATT_TQ] = rest
            acc[win - ATT_TQ :] = jnp.zeros((ATT_TQ, ATT_STEP_COLS), F32)

    blk = (ATT_TQ, ATT_STEP_COLS)
    specs = _attn_specs(nt)
    specs.append(pl.BlockSpec(blk, lambda hp, m: (jnp.minimum(m, nt - 1), hp)))
    specs.append(_bias_spec())
    done = lambda hp, m: (jnp.maximum(m - 2, 0), hp)
    out_specs = [
        pl.BlockSpec(blk, lambda hp, m: (jnp.minimum(m, nt - 1), hp)),
        pl.BlockSpec(blk, done),
        pl.BlockSpec(blk, done),
        pl.BlockSpec((ATT_STEP_HEADS, ATT_TQ, win), lambda hp, m: (hp, 0, 0)),
    ]
    out_shape = [jax.ShapeDtypeStruct((s, ATT_WIDTH), BF16)] * 3
    out_shape.append(jax.ShapeDtypeStruct((ATT_HEADS, ATT_TQ, win), F32))
    return pl.pallas_call(
        body,
        name=name,
        grid=(ATT_HEADS // ATT_STEP_HEADS, nt + 2),
        in_specs=specs,
        out_specs=out_specs,
        out_shape=out_shape,
        scratch_shapes=[pltpu.VMEM((win, ATT_STEP_COLS), F32), pltpu.VMEM((win, ATT_STEP_COLS), F32)],
        compiler_params=_cparams(dimension_semantics=("arbitrary", "arbitrary")),
    )(proj, proj, proj, proj, proj, proj, proj, do, bias)


def _ada_fwd(c_all, w, name):
    def body(c_ref, w_ref, o_ref):
        act = _silu(c_ref[...]).astype(BF16)
        o_ref[...] = jnp.dot(act, w_ref[...].astype(BF16), preferred_element_type=F32)

    return pl.pallas_call(
        body, name=name, out_shape=jax.ShapeDtypeStruct((c_all.shape[0], w.shape[1]), F32), compiler_params=_cparams()
    )(c_all, w)


def _ada_bwd(c_all, dmod, name):
    def body(c_ref, d_ref, o_ref):
        act = _silu(c_ref[...])
        o_ref[...] = lax.dot_general(act, d_ref[...], _DIMS["tn"], preferred_element_type=F32,
                                     precision=lax.Precision.HIGHEST)

    return pl.pallas_call(
        body, name=name, out_shape=jax.ShapeDtypeStruct((c_all.shape[1], dmod.shape[1]), F32), compiler_params=_cparams()
    )(c_all, dmod)


def _adamw_parts(landed, sent, me, w, m, v, name, rows=256):
    r, c = w.shape
    tr = _pick(r, rows, 16)

    def body(me_ref, g_ref, own_ref, w_ref, m_ref, v_ref, go_ref, d_ref, mo_ref, vo_ref):
        mine = me_ref[0]
        grad = jnp.zeros((tr, c), F32)
        for d in range(N_DEV):
            grad = grad + jnp.where(mine == d, own_ref[0], g_ref[d]).astype(F32)
        _adamw_update(grad, w_ref, m_ref, v_ref, go_ref, d_ref, mo_ref, vo_ref)

    spec = pl.BlockSpec((tr, c), lambda i, me_ref: (i, 0))
    return pl.pallas_call(
        body,
        name=name,
        grid_spec=pltpu.PrefetchScalarGridSpec(
            num_scalar_prefetch=1,
            grid=(r // tr,),
            in_specs=[pl.BlockSpec((N_DEV, tr, c), lambda i, me_ref: (0, i, 0)),
                      pl.BlockSpec((1, tr, c), lambda i, me_ref: (me_ref[0], i, 0)), spec, spec, spec],
            out_specs=[spec] * 4,
        ),
        out_shape=[jax.ShapeDtypeStruct((r, c), F32)] * 4,
        compiler_params=_cparams(dimension_semantics=("parallel",)),
    )(me.reshape(1).astype(jnp.int32), landed, sent, w, m, v)


def _adamw_update(grad, w_ref, m_ref, v_ref, go_ref, d_ref, mo_ref, vo_ref):
    m2 = ADAM_B1 * m_ref[...] + (1.0 - ADAM_B1) * grad
    v2 = ADAM_B2 * v_ref[...] + (1.0 - ADAM_B2) * (grad * grad)
    m_hat = m2 / (1.0 - ADAM_B1**ADAM_STEP)
    v_hat = v2 / (1.0 - ADAM_B2**ADAM_STEP)
    go_ref[...] = grad
    d_ref[...] = -ADAM_LR * (m_hat / (jnp.sqrt(v_hat) + ADAM_EPS) + ADAM_WD * w_ref[...])
    mo_ref[...] = m2
    vo_ref[...] = v2


def _adamw(g, w, m, v, name, rows=256):
    r, c = w.shape
    tr = _pick(r, rows, 16)

    def body(g_ref, w_ref, m_ref, v_ref, go_ref, d_ref, mo_ref, vo_ref):
        _adamw_update(g_ref[...], w_ref, m_ref, v_ref, go_ref, d_ref, mo_ref, vo_ref)

    spec = pl.BlockSpec((tr, c), lambda i: (i, 0))
    return pl.pallas_call(
        body,
        name=name,
        grid=(r // tr,),
        in_specs=[spec, spec, spec, spec],
        out_specs=[spec] * 4,
        out_shape=[jax.ShapeDtypeStruct((r, c), F32)] * 4,
        compiler_params=_cparams(dimension_semantics=("parallel",)),
    )(g, w, m, v)


def _sum_parts(parts, name):
    def body(p_ref, o_ref):
        acc = p_ref[0]
        for d in range(1, N_DEV):
            acc = acc + p_ref[d]
        o_ref[...] = acc

    return pl.pallas_call(
        body, name=name, out_shape=jax.ShapeDtypeStruct(parts.shape[1:], F32), compiler_params=_cparams()
    )(parts)


def _place():
    x, y, c = lax.axis_index("x"), lax.axis_index("y"), lax.axis_index("c")
    return x, y, c


def _dev_index(p):
    return 4 * p[0] + 2 * p[1] + p[2]


def _allgather_vmem(shard, name):
    m_per, n = shard.shape

    def body(x_ref, out_ref, send_sems, recv_sems, local_sem):
        x, y, c = _place()
        me, sibling = (x, y, c), (x, y, 1 - c)
        chips = [(1 - x, y), (x, 1 - y), (1 - x, 1 - y)]

        def rows(p):
            return out_ref.at[pl.ds(_dev_index(p) * m_per, m_per), :]

        def copy(k, block, to, src=None):
            return pltpu.make_async_remote_copy(
                src_ref=rows(block) if src is None else src, dst_ref=rows(block),
                send_sem=send_sems.at[k], recv_sem=recv_sems.at[k], device_id=to, device_id_type=MESH)

        mine = pltpu.make_async_copy(x_ref, rows(me), local_sem)
        mine.start()
        first = [copy(0, me, sibling, src=x_ref)]
        first += [copy(1 + j, me, (*chip, c), src=x_ref) for j, chip in enumerate(chips)]
        for cp in first:
            cp.start()
        passed = [copy(4 + j, (*chip, c), sibling) for j, chip in enumerate(chips)]
        for j, chip in enumerate(chips):
            copy(1 + j, (*chip, c), me).wait_recv()
            passed[j].start()
        copy(0, sibling, me).wait_recv()
        for j, chip in enumerate(chips):
            copy(4 + j, (*chip, 1 - c), me).wait_recv()
        for cp in first + passed:
            cp.wait_send()
        mine.wait()

    return pl.pallas_call(
        body,
        name=name,
        out_shape=jax.ShapeDtypeStruct((N_DEV * m_per, n), shard.dtype),
        in_specs=[pl.BlockSpec(memory_space=pltpu.VMEM)],
        out_specs=pl.BlockSpec(memory_space=pltpu.VMEM),
        scratch_shapes=[pltpu.SemaphoreType.DMA((7,)), pltpu.SemaphoreType.DMA((7,)), pltpu.SemaphoreType.DMA],
        compiler_params=_cparams(),
    )(shard)


def _allgather_hbm(shards, name):
    n = len(shards)

    def body(*refs):
        ins, outs = refs[:n], refs[n : 2 * n]
        send_sems, recv_sems, local_sems = refs[2 * n :]
        x, y, c = _place()
        me, sibling = (x, y, c), (x, y, 1 - c)
        chips = [(1 - x, y), (x, 1 - y), (1 - x, 1 - y)]

        def copy(a, k, block, to, src=None):
            dst = outs[a].at[_dev_index(block)]
            return pltpu.make_async_remote_copy(
                src_ref=dst if src is None else src, dst_ref=dst,
                send_sem=send_sems.at[a * 7 + k], recv_sem=recv_sems.at[a * 7 + k], device_id=to, device_id_type=MESH)

        mine = [pltpu.make_async_copy(ins[a], outs[a].at[_dev_index(me)], local_sems.at[a]) for a in range(n)]
        for cp in mine:
            cp.start()
        first = []
        for a in range(n):
            first.append(copy(a, 0, me, sibling, src=ins[a]))
            first += [copy(a, 1 + j, me, (*chip, c), src=ins[a]) for j, chip in enumerate(chips)]
        for cp in first:
            cp.start()
        passed = []
        for j, chip in enumerate(chips):
            for a in range(n):
                copy(a, 1 + j, (*chip, c), me).wait_recv()
                cp = copy(a, 4 + j, (*chip, c), sibling)
                cp.start()
                passed.append(cp)
        for a in range(n):
            copy(a, 0, sibling, me).wait_recv()
        for j, chip in enumerate(chips):
            for a in range(n):
                copy(a, 4 + j, (*chip, 1 - c), me).wait_recv()
        for cp in first + passed:
            cp.wait_send()
        for cp in mine:
            cp.wait()

    any_spec = pl.BlockSpec(memory_space=pl.ANY)
    return pl.pallas_call(
        body,
        name=name,
        out_shape=[jax.ShapeDtypeStruct((N_DEV, *s.shape), s.dtype) for s in shards],
        in_specs=[any_spec] * n,
        out_specs=[any_spec] * n,
        scratch_shapes=[pltpu.SemaphoreType.DMA((7 * n,)), pltpu.SemaphoreType.DMA((7 * n,)),
                        pltpu.SemaphoreType.DMA((n,))],
        compiler_params=_cparams(),
    )(*shards)


HBM_SPEC = pl.BlockSpec(memory_space=pltpu.HBM)
SEM_SPEC = pl.BlockSpec(memory_space=pltpu.SEMAPHORE)
EFFECT = pltpu.SideEffectType.DATAFLOW_SIDE_EFFECTING


def _peers(x, y, c):
    return [(1 - x if k & 4 else x, 1 - y if k & 2 else y, 1 - c if k & 1 else c) for k in range(1, N_DEV)]


def _push_peers(mode, x, y, c):
    if mode == "all":
        return _peers(x, y, c)
    return [(x, y, 1 - c), (1 - x, y, c), (x, 1 - y, c), (1 - x, 1 - y, c)]


def _push_start(groups, sliced, name, after=(), modes=None):
    flat = [b for g in groups for b in g]
    n, ng = len(flat), len(groups)
    sizes = [len(g) for g in groups]
    modes = modes or ["all"] * ng
    fan = [len(_push_peers(m, 0, 0, 0)) for m in modes]
    per = 2 if sliced else 3
    lands = [lax.empty(b.shape if sliced else (N_DEV, *b.shape), b.dtype) for b in flat]

    def body(*refs):
        ins, lnd = refs[:n], refs[n : 2 * n]
        sems = refs[2 * n + len(after) : 2 * n + len(after) + per * ng]
        token = refs[-1]
        x, y, c = _place()
        me = _dev_index((x, y, c))
        if not sliced:
            first = 0
            for gi, size in enumerate(sizes):
                for j in range(first, first + size):
                    pltpu.make_async_copy(ins[j], lnd[j].at[me], sems[per * gi + 2].at[j - first]).start()
                first += size
        first = 0
        for gi, size in enumerate(sizes):
            for k, peer in enumerate(_push_peers(modes[gi], x, y, c)):
                for j in range(first, first + size):
                    sem = (j - first) * fan[gi] + k
                    pltpu.make_async_remote_copy(
                        src_ref=ins[j].at[_dev_index(peer)] if sliced else ins[j], dst_ref=lnd[j].at[me],
                        send_sem=sems[per * gi].at[sem], recv_sem=sems[per * gi + 1].at[sem],
                        device_id=peer, device_id_type=MESH).start()
            first += size
        token[...] = jnp.zeros_like(token)

    out_shape = []
    for size, width in zip(sizes, fan):
        out_shape += [pltpu.SemaphoreType.DMA((width * size,)), pltpu.SemaphoreType.DMA((width * size,))]
        out_shape += [] if sliced else [pltpu.SemaphoreType.DMA((size,))]
    out_shape += [pltpu.HBM(b.shape, b.dtype) for b in flat + lands]
    out_shape.append(jax.ShapeDtypeStruct((V7X_SUBLANES, V7X_LANES), F32))
    res = pl.pallas_call(
        body,
        name=name,
        out_shape=tuple(out_shape),
        in_specs=[HBM_SPEC] * (2 * n) + [ANY_SPEC] * len(after),
        out_specs=tuple([SEM_SPEC] * (per * ng) + [HBM_SPEC] * (2 * n) + [pl.BlockSpec(memory_space=pltpu.VMEM)]),
        input_output_aliases={i: per * ng + i for i in range(2 * n)},
        compiler_params=pltpu.CompilerParams(has_side_effects=EFFECT),
    )(*[pltpu.with_memory_space_constraint(b, pltpu.HBM) for b in flat + lands], *after)
    sems, thru, token = res[: per * ng], res[per * ng : per * ng + 2 * n], res[-1]
    out, first = [], 0
    for gi, size in enumerate(sizes):
        out.append((sems[per * gi], sems[per * gi + 1], list(thru[first : first + size]),
                    list(thru[n + first : n + first + size]), None if sliced else sems[per * gi + 2]))
        first += size
    return out, token


def _push_wait(started, sliced, after, name, mode="all"):
    send_sems, recv_sems, bufs, lands, own_sems = started
    n = len(bufs)
    fan = len(_push_peers(mode, 0, 0, 0))
    own = [] if own_sems is None else [own_sems]

    def body(*refs):
        ins, lnd = refs[:n], refs[n : 2 * n]
        send_ref, recv_ref = refs[2 * n], refs[2 * n + 1]
        x, y, c = _place()
        for k, peer in enumerate(_push_peers(mode, x, y, c)):
            for j in range(n):
                cp = pltpu.make_async_remote_copy(
                    src_ref=ins[j].at[_dev_index(peer)] if sliced else ins[j], dst_ref=lnd[j].at[_dev_index(peer)],
                    send_sem=send_ref.at[j * fan + k], recv_sem=recv_ref.at[j * fan + k],
                    device_id=peer, device_id_type=MESH)
                cp.wait_send()
                cp.wait_recv()
        if own:
            for j in range(n):
                pltpu.make_async_copy(ins[j], lnd[j].at[_dev_index((x, y, c))], refs[2 * n + 2].at[j]).wait()

    res = pl.pallas_call(
        body,
        name=name,
        out_shape=tuple(pltpu.HBM(b.shape, b.dtype) for b in bufs + lands),
        in_specs=[HBM_SPEC] * (2 * n) + [SEM_SPEC] * (2 + len(own)) + [pl.BlockSpec(memory_space=pl.ANY)],
        out_specs=tuple([HBM_SPEC] * (2 * n)),
        input_output_aliases={i: i for i in range(2 * n)},
        compiler_params=pltpu.CompilerParams(has_side_effects=EFFECT),
    )(*bufs, *lands, send_sems, recv_sems, *own, after)
    return list(res[:n]), list(res[n:])


def _forward_copies(lnd, send_ref, recv_ref, incoming):
    x, y, c = _place()
    copies = []
    for k, chip in enumerate([(1 - x, y), (x, 1 - y), (1 - x, 1 - y)]):
        mine, theirs = _dev_index((*chip, c)), _dev_index((*chip, 1 - c))
        for j, ref in enumerate(lnd):
            copies.append(pltpu.make_async_remote_copy(
                src_ref=ref.at[mine], dst_ref=ref.at[theirs if incoming else mine],
                send_sem=send_ref.at[j * 3 + k], recv_sem=recv_ref.at[j * 3 + k],
                device_id=(x, y, 1 - c), device_id_type=MESH))
    return copies


def _forward_start(lands, name):
    n = len(lands)

    def body(*refs):
        for cp in _forward_copies(refs[:n], refs[n], refs[n + 1], False):
            cp.start()

    res = pl.pallas_call(
        body,
        name=name,
        out_shape=(pltpu.SemaphoreType.DMA((3 * n,)), pltpu.SemaphoreType.DMA((3 * n,)),
                   *[pltpu.HBM(b.shape, b.dtype) for b in lands]),
        in_specs=[HBM_SPEC] * n,
        out_specs=(SEM_SPEC, SEM_SPEC, *[HBM_SPEC] * n),
        input_output_aliases={i: 2 + i for i in range(n)},
        compiler_params=pltpu.CompilerParams(has_side_effects=EFFECT),
    )(*[pltpu.with_memory_space_constraint(b, pltpu.HBM) for b in lands])
    return res[0], res[1], list(res[2:])


def _forward_wait(started, after, name):
    send_sems, recv_sems, lands = started
    n = len(lands)

    def body(*refs):
        for cp in _forward_copies(refs[:n], refs[n], refs[n + 1], True):
            cp.wait_send()
            cp.wait_recv()

    res = pl.pallas_call(
        body,
        name=name,
        out_shape=tuple(pltpu.HBM(b.shape, b.dtype) for b in lands),
        in_specs=[HBM_SPEC] * n + [SEM_SPEC, SEM_SPEC, pl.BlockSpec(memory_space=pl.ANY)],
        out_specs=tuple([HBM_SPEC] * n),
        input_output_aliases={i: i for i in range(n)},
        compiler_params=pltpu.CompilerParams(has_side_effects=EFFECT),
    )(*lands, send_sems, recv_sems, after)
    return list(res)


def _cols_full(g):
    return jnp.transpose(g, (1, 0, 2)).reshape(g.shape[1], -1)


def _rows_full(g):
    return g.reshape(-1, g.shape[2])


def _cols_parts(full, n=N_DEV):
    r = full.shape[0]
    return jnp.transpose(full.reshape(r, n, -1), (1, 0, 2)).astype(BF16)


def _rows_parts(full):
    return full.reshape(N_DEV, -1, full.shape[1]).astype(BF16)


def _pad_rows(v, rows):
    flat = v.reshape(-1)
    return jnp.pad(flat, (0, rows * D_MODEL - flat.shape[0])).reshape(rows, D_MODEL)


def _my_cols(full, me, width):
    return lax.dynamic_slice_in_dim(full, me * width, width, axis=full.ndim - 1)


def kernel(x, c, w_ada, b_ada, norm_pre, norm_post, ffn1_w_gu, ffn1_w_down, w_in, rel_bias, conv_w, conv_b, lru_wa, lru_ba, lru_wx, lru_bx, lru_lambda, w_att_o, w_rec_o, w_out, ffn2_w_gu, ffn2_w_down, loss_target, m_w_ada, m_b_ada, m_norm_pre, m_norm_post, m_ffn1_w_gu, m_ffn1_w_down, m_w_in, m_rel_bias, m_conv_w, m_conv_b, m_lru_wa, m_lru_ba, m_lru_wx, m_lru_bx, m_lru_lambda, m_w_att_o, m_w_rec_o, m_w_out, m_ffn2_w_gu, m_ffn2_w_down, v_w_ada, v_b_ada, v_norm_pre, v_norm_post, v_ffn1_w_gu, v_ffn1_w_down, v_w_in, v_rel_bias, v_conv_w, v_conv_b, v_lru_wa, v_lru_ba, v_lru_wx, v_lru_bx, v_lru_lambda, v_w_att_o, v_w_rec_o, v_w_out, v_ffn2_w_gu, v_ffn2_w_down):
    weights = dict(w_ada=w_ada, b_ada=b_ada, norm_pre=norm_pre, norm_post=norm_post, ffn1_w_gu=ffn1_w_gu,
                   ffn1_w_down=ffn1_w_down, w_in=w_in, rel_bias=rel_bias, conv_w=conv_w, conv_b=conv_b,
                   lru_wa=lru_wa, lru_ba=lru_ba, lru_wx=lru_wx, lru_bx=lru_bx, lru_lambda=lru_lambda,
                   w_att_o=w_att_o, w_rec_o=w_rec_o, w_out=w_out, ffn2_w_gu=ffn2_w_gu, ffn2_w_down=ffn2_w_down)
    mom1 = dict(w_ada=m_w_ada, b_ada=m_b_ada, norm_pre=m_norm_pre, norm_post=m_norm_post, ffn1_w_gu=m_ffn1_w_gu,
                ffn1_w_down=m_ffn1_w_down, w_in=m_w_in, rel_bias=m_rel_bias, conv_w=m_conv_w, conv_b=m_conv_b,
                lru_wa=m_lru_wa, lru_ba=m_lru_ba, lru_wx=m_lru_wx, lru_bx=m_lru_bx, lru_lambda=m_lru_lambda,
                w_att_o=m_w_att_o, w_rec_o=m_w_rec_o, w_out=m_w_out, ffn2_w_gu=m_ffn2_w_gu, ffn2_w_down=m_ffn2_w_down)
    mom2 = dict(w_ada=v_w_ada, b_ada=v_b_ada, norm_pre=v_norm_pre, norm_post=v_norm_post, ffn1_w_gu=v_ffn1_w_gu,
                ffn1_w_down=v_ffn1_w_down, w_in=v_w_in, rel_bias=v_rel_bias, conv_w=v_conv_w, conv_b=v_conv_b,
                lru_wa=v_lru_wa, lru_ba=v_lru_ba, lru_wx=v_lru_wx, lru_bx=v_lru_bx, lru_lambda=v_lru_lambda,
                w_att_o=v_w_att_o, w_rec_o=v_w_rec_o, w_out=v_w_out, ffn2_w_gu=v_ffn2_w_gu, ffn2_w_down=v_ffn2_w_down)
    order = list(weights)
    big = ["ffn1_w_gu", "ffn1_w_down", "w_in", "w_att_o", "w_rec_o", "w_out", "ffn2_w_gu", "ffn2_w_down"]
    col_sharded = {"ffn1_w_gu", "w_in", "w_att_o", "ffn2_w_gu"}
    small = ["b_ada", "norm_pre", "norm_post", "rel_bias", "conv_w", "conv_b", "lru_wa", "lru_ba", "lru_wx",
             "lru_bx", "lru_lambda"]

    xi, yi, ci = _place()
    me = _dev_index((xi, yi, ci))
    x0 = x[0]
    target = loss_target[0]
    fuse_tm = min(FUSE_TM, x0.shape[0])

    transposed = {"ffn1_w_gu", "w_in", "ffn2_w_gu"}
    local = lambda n, arr: jnp.transpose(arr[0]) if n in transposed else arr[0]
    shards = {n: local(n, weights[n]).astype(BF16) for n in big}
    full_of = lambda n, g: _cols_full(g) if n == "w_att_o" else _rows_full(g)

    pack = jnp.concatenate([c.reshape(-1), norm_pre.reshape(-1), norm_post.reshape(-1), conv_w.reshape(-1)])
    pack = jnp.pad(pack, (0, 3072 - pack.shape[0])).reshape(8, 384)
    got = _allgather_vmem(pack, "gather_small_inputs").reshape(N_DEV, 3072)
    c_all = got[:, :1024]
    unshard = lambda blk, rows: jnp.transpose(blk.reshape(N_DEV, rows, 128), (1, 0, 2)).reshape(rows, D_MODEL)
    g_pre = unshard(got[:, 1024:1408], 3)
    g_post = unshard(got[:, 1408:1792], 3)
    conv_taps = unshard(got[:, 1792:2304], 4)
    conv_w8 = jnp.concatenate([conv_taps, jnp.zeros((4, LRU_WIDTH), F32)], axis=0)

    mod_cols = _ada_fwd(c_all, w_ada[0], "ada_fwd")
    mod_all = _allgather_vmem(mod_cols, "gather_mod").reshape(N_DEV, N_DEV, 1152)
    mod = lax.dynamic_index_in_dim(mod_all, me, axis=1, keepdims=False).reshape(1, -1) + b_ada
    mod = mod.reshape(3, 3, 1, D_MODEL)

    w_slab = _slab_weights(lru_wa[0], lru_wx[0])
    bias = _bias_tile(rel_bias[0], "bias_tile")

    res_w = (0.5, 1.0, 0.5)
    row = lambda v: v.reshape(1, -1)

    (w1_gu,) = _allgather_hbm([shards["ffn1_w_gu"]], "gather_ffn1_w_gu")
    weight_groups = [["ffn1_w_down"], ["w_in"], ["w_att_o", "w_rec_o", "w_out"], ["ffn2_w_gu", "ffn2_w_down"]]
    weight_modes = ["all", "chip", "all", "all"]
    weights_started, started = _push_start([[shards[n] for n in g] for g in weight_groups], False,
                                           "gather_weights_start", after=(mod, w1_gu), modes=weight_modes)
    full = {"ffn1_w_gu": _rows_full(w1_gu)}

    def gathered_group(gi, after):
        sent, lands = _push_wait(weights_started[gi], False, after, f"gather_weights_wait{gi}", mode=weight_modes[gi])
        if weight_modes[gi] == "chip":
            lands = _forward_wait(_forward_start(lands, f"gather_weights_forward{gi}"), sent[0],
                                  f"gather_weights_forward_wait{gi}")
        for n, land in zip(weight_groups[gi], lands):
            full[n] = full_of(n, land)

    def ffn_fwd(xin, k, gi, tag, deps=(), target=None):
        tm = FFN_TM // 2 if k == 2 else FFN_TM
        h, a, g, u = _pre_up(xin, row(g_pre[k]), mod[k, 0], mod[k, 1], full[f"{tag}_w_gu"], f"{tag}_up", deps=deps,
                             tm=tm)
        if f"{tag}_w_down" not in full:
            gathered_group(gi, a)
        f, *out = _matmul_post(a, full[f"{tag}_w_down"], xin, row(g_post[k]), mod[k, 2], res_w[k], f"{tag}_down",
                               target=target, tm=tm)
        return (out[0] if target is None else out), (h, g, u, a, f)

    x1, saved1 = ffn_fwd(x0, 0, 0, "ffn1", deps=(started,))

    gathered_group(1, x1)
    h2, proj = _pre_matmul(x1, row(g_pre[1]), mod[1, 0], mod[1, 1], full["w_in"], "mix_in",
                           b_shift=3 * ATT_WIDTH // 512)
    att_o = _attn_fwd(proj, bias, "attn_fwd")
    gathered_group(2, att_o)
    xc, pre, a_t, u_t = _lru_front(proj, conv_w8, conv_b, w_slab, lru_ba, lru_bx, lru_lambda, "lru_front")
    hs, h_prev, rec_in = _scan_fwd(a_t, u_t, proj, "lru_scan")
    att = _matmul(att_o, full["w_att_o"], "nn", F32, "att_out")
    rec = _matmul(rec_in, full["w_rec_o"], "nn", F32, "rec_out")
    merged, f2, x2 = _merge_matmul_post(att, rec, proj, full["w_out"], x1, row(g_post[1]), mod[1, 2], res_w[1],
                                        "mix_out")

    gathered_group(3, x2)
    (dy, sq), saved3 = ffn_fwd(x2, 2, 2, "ffn2", target=target)
    loss = lax.psum(0.5 * jnp.sum(sq) / D_MODEL, ("x", "y", "c"))

    grads = {}
    norm_sums = [None] * 6

    pending = []

    def exchange_start(names, tag, after=()):
        send = [(_cols_parts if n == "w_att_o" else _rows_parts)(grads[n]) for n in names]
        (group,), token = _push_start([send], True, f"exchange_{tag}_start", after=after)
        pending.append((names, send, group, tag))
        return token

    def exchange_finish(names, send, group, tag, after):
        sent, lands = _push_wait(group, True, after, f"exchange_{tag}_wait")
        res = None
        for n, land, mine in zip(names, lands, sent):
            res = _adamw_parts(land, mine, me, local(n, weights[n]), local(n, mom1[n]), local(n, mom2[n]),
                               f"adamw_{n}")
            back = (lambda r: jnp.transpose(r)) if n in transposed else (lambda r: r)
            out_g[n], out_d[n], out_m[n], out_v[n] = [back(r).reshape(weights[n].shape) for r in res]
        return res[0]

    out_g, out_d, out_m, out_v = {}, {}, {}, {}

    def ffn_bwd(xin, k, saved, dout, tag):
        h, g, u, a, f = saved
        w_gu, w_down = f"{tag}_w_gu", f"{tag}_w_down"
        df, dgu, norm_sums[2 * k + 1] = _post_bwd_up_bwd(f, dout, row(g_post[k]), mod[k, 2], res_w[k], full[w_down],
                                                          g, u, f"{tag}_up_bwd")
        dw_tk = 2 * DW_TK if k == 2 else DW_TK
        grads[w_down] = _matmul(a, df, "tn", BF16, f"{tag}_dw_down", tm=1408, tn=1024, tk=dw_tk)
        started = exchange_start([w_down], w_down)
        grads[w_gu] = _dw_gu(dgu, h, f"{tag}_dw_gu", deps=(started,), tk=dw_tk)
        started = exchange_start([w_gu], w_gu)
        halves = [(dgu, (None, fuse_tm, D_FF), lambda i, half=half: (half, i, 0), (half * D_FF, (half + 1) * D_FF))
                  for half in range(2)]
        dx, norm_sums[2 * k] = _matmul_pre_bwd(halves, full[w_gu], xin, dout, row(g_pre[k]),
                                                                mod[k, 0], mod[k, 1], f"{tag}_dh", deps=(started,))
        return dx

    dx2 = ffn_bwd(x2, 2, saved3, dy, "ffn2")

    df2, datt, drec, dg_att, dg_rec, norm_sums[3] = _post_bwd_merge_bwd(
        f2, dx2, row(g_post[1]), mod[1, 2], res_w[1], full["w_out"], att, rec, proj, "mix_dmerged")
    grads["w_out"] = _matmul(merged, df2, "tn", BF16, "mix_dw_out", tm=1024, tn=1024, tk=DW_TK)
    datt_o = _matmul(datt, full["w_att_o"], "nt", BF16, "att_out_bwd")
    grads["w_att_o"] = _matmul(att_o, datt, "tn", BF16, "dw_att_o", tm=512, tn=1024, tk=DW_TK)
    grads["w_rec_o"] = _matmul(rec_in, drec, "tn", BF16, "dw_rec_o", tm=1024, tn=1024, tk=DW_TK)
    started = exchange_start(["w_out", "w_att_o", "w_rec_o"], "mix_out")
    dhs, dyr = _matmul_recin_bwd(drec, full["w_rec_o"], hs, proj, "rec_out_bwd", deps=(started,))
    g_t = _scan_bwd(a_t, dhs, "lru_scan_bwd")
    dpre, dxc, lru_sums = _lru_back(pre, xc, w_slab, lru_ba, lru_bx, lru_lambda, g_t, h_prev, "lru_back")
    dxr, conv_sums = _conv_bwd(proj, conv_w8, dxc, "conv_bwd")
    dq, dk, dv, dbias = _attn_bwd(proj, bias, datt_o, "attn_bwd")
    dproj = jnp.concatenate([dq, dk, dv, dxr, dyr, dg_att, dg_rec], axis=1)
    grads["w_in"] = _matmul(dproj, h2, "tn", BF16, "mix_dw_in", tm=1408, tn=1024, tk=DW_TK)
    pack_mix = jnp.concatenate([conv_sums, lru_sums, _pad_rows(_bias_grad(dbias, "bias_grad"), V7X_SUBLANES),
                                _lru_dw(xc, dpre, "lru_dw").reshape(128, D_MODEL)], axis=0)
    (mix_started,), started = _push_start([[pack_mix]], False, "small_grads_mix_start")
    started = exchange_start(["w_in"], "w_in", after=(started,))
    whole = [(dproj, (fuse_tm, PROJ_WIDTH), lambda i: (i, 0), (0, PROJ_WIDTH))]
    dx1, norm_sums[2] = _matmul_pre_bwd(whole, full["w_in"], x1, dx2, row(g_pre[1]), mod[1, 0],
                                                             mod[1, 1], "mix_dh", deps=(started,))

    dx0 = ffn_bwd(x0, 0, saved1, dx1, "ffn1")

    pack_norm = jnp.concatenate(norm_sums, axis=0)
    (norm_started,), _ = _push_start([[pack_norm]], False, "small_grads_norm_start")

    def summed(started, pack, after, tag):
        _, (parts,) = _push_wait(started, False, after, f"small_grads_{tag}_wait")
        return parts, _sum_parts(parts, f"small_grads_{tag}_sum")

    done = dx0
    last = pending[-1:]
    for names, send, group, tag in pending[:-1]:
        done = exchange_finish(names, send, group, tag, done)

    _, total = summed(mix_started, pack_mix, done, "mix")
    grads["conv_w"] = _my_cols(total[0:4], me, 128)
    grads["conv_b"] = total[4:5]
    grads["lru_ba"] = total[8:9]
    grads["lru_bx"] = total[9:10]
    grads["lru_lambda"] = total[10:11]
    grads["rel_bias"] = total[16:19].reshape(-1)[: ATT_HEADS * (2 * MAX_REL + 1)].reshape(ATT_HEADS, -1)
    grads["lru_wa"] = total[24:88].reshape(LRU_BLOCKS, LRU_BLOCK, LRU_BLOCK)
    grads["lru_wx"] = total[88:152].reshape(LRU_BLOCKS, LRU_BLOCK, LRU_BLOCK)
    parts, total = summed(norm_started, pack_norm, total, "norm")
    by_sandwich = lambda v: v.reshape(*v.shape[:-2], 3, 2 * V7X_SUBLANES, D_MODEL)
    dmod_of = lambda v: jnp.concatenate([by_sandwich(v)[..., 1:3, :], by_sandwich(v)[..., 9:10, :]], axis=-2)
    grads["b_ada"] = dmod_of(total).reshape(1, -1)
    grads["norm_pre"] = _my_cols(by_sandwich(total)[:, 0, :], me, 128)
    grads["norm_post"] = _my_cols(by_sandwich(total)[:, V7X_SUBLANES, :], me, 128)
    dmod_all = dmod_of(parts).reshape(N_DEV, 9 * D_MODEL)
    grads["w_ada"] = _ada_bwd(c_all, _my_cols(dmod_all, me, 1152), "ada_bwd")

    res = _adamw(grads["w_ada"], w_ada[0], m_w_ada[0], v_w_ada[0], "adamw_w_ada")
    out_g["w_ada"], out_d["w_ada"], out_m["w_ada"], out_v["w_ada"] = [r.reshape(w_ada.shape) for r in res]

    sizes = [int(np.prod(weights[n].shape)) for n in small]
    tot = sum(sizes)
    rows_small = -(-tot // (16 * D_MODEL)) * 16
    flat = lambda arrs: jnp.pad(jnp.concatenate([a.reshape(-1) for a in arrs]),
                                (0, rows_small * D_MODEL - tot)).reshape(rows_small, D_MODEL)
    res = _adamw(flat([grads[n] for n in small]), flat([weights[n] for n in small]),
                 flat([mom1[n] for n in small]), flat([mom2[n] for n in small]), "adamw_small", rows=rows_small)
    offs = np.cumsum([0] + sizes)
    for dst, r in zip((out_g, out_d, out_m, out_v), res):
        rf = r.reshape(-1)
        for i, n in enumerate(small):
            dst[n] = rf[offs[i] : offs[i + 1]].reshape(weights[n].shape)

    done = res[0]
    for names, send, group, tag in last:
        done = exchange_finish(names, send, group, tag, done)

    return (loss, dx0[None], *[out_g[n] for n in order], *[out_d[n] for n in order],
            *[out_m[n] for n in order], *[out_v[n] for n in order])
```

```python
import jax
import jax.numpy as jnp
import numpy as np
from jax import lax
from jax.experimental import pallas as pl
from jax.experimental.pallas import tpu as pltpu

D_MODEL = 1024
D_FF = 2816
ATT_HEADS = 8
ATT_HEAD_DIM = 64
ATT_WIDTH = 512
CHUNK = 64
LEFT_CHUNKS = 8
MAX_REL = 128
LRU_WIDTH = 1024
LRU_BLOCKS = 16
LRU_BLOCK = 64
LRU_C = 8.0
EPS = 1e-6
PROJ_WIDTH = 5632
N_DEV = 8

ADAM_LR = 0.001
ADAM_B1 = 0.9
ADAM_B2 = 0.999
ADAM_EPS = 1e-08
ADAM_WD = 0.01
ADAM_STEP = 10

V7X_LANES = 128
V7X_SUBLANES = 8
V7X_VMEM_BYTES = 64 * 1024 * 1024
VMEM_LIMIT = V7X_VMEM_BYTES - 8 * 1024 * 1024

ATT_TQ = 256
NEG = -1e30
BF16 = jnp.bfloat16
F32 = jnp.float32
MESH = pl.DeviceIdType.MESH

OFF_Q = 4 * LRU_WIDTH
OFF_K = OFF_Q + ATT_WIDTH
OFF_V = OFF_K + ATT_WIDTH


def _cparams(**kw):
    return pltpu.CompilerParams(vmem_limit_bytes=VMEM_LIMIT, **kw)


def _pick(n, target, unit=V7X_LANES):
    best = None
    for t in range(unit, min(n, target) + 1, unit):
        if n % t == 0:
            best = t
    return n if best is None else best


_DIMS = {
    "nn": (((1,), (0,)), ((), ())),
    "nt": (((1,), (1,)), ((), ())),
    "tn": (((0,), (0,)), ((), ())),
}


ANY_SPEC = pl.BlockSpec(memory_space=pl.ANY)


def _matmul(a, b, mode, out_dtype, name, tm=1024, tn=512, tk=1408, deps=(), b_shift=0):
    n_deps = len(deps)
    if mode == "nn":
        (m, k), (k2, n) = a.shape, b.shape
    elif mode == "nt":
        (m, k), (n, k2) = a.shape, b.shape
    else:
        (k, m), (k2, n) = a.shape, b.shape
    assert k == k2, (a.shape, b.shape, mode)
    tm, tn, tk = _pick(m, tm), _pick(n, tn), _pick(k, tk)
    nk = k // tk
    dims = _DIMS[mode]

    def body(a_ref, b_ref, *rest):
        o_ref, scratch = rest[n_deps], rest[n_deps + 1 :]
        p = lax.dot_general(a_ref[...], b_ref[...], dims, preferred_element_type=F32)
        if nk == 1:
            o_ref[...] = p.astype(o_ref.dtype)
        else:
            acc = scratch[0]
            kk = pl.program_id(2)

            @pl.when(kk == 0)
            def _():
                acc[...] = p

            @pl.when(kk > 0)
            def _():
                acc[...] += p

            @pl.when(kk == nk - 1)
            def _():
                o_ref[...] = acc[...].astype(o_ref.dtype)

    if mode == "nn":
        a_spec = pl.BlockSpec((tm, tk), lambda i, j, kk: (i, kk))
        b_spec = pl.BlockSpec((tk, tn), lambda i, j, kk: (kk, j))
    elif mode == "nt":
        a_spec = pl.BlockSpec((tm, tk), lambda i, j, kk: (i, kk))
        b_spec = pl.BlockSpec((tn, tk), lambda i, j, kk: ((j + b_shift) % (n // tn), kk))
    else:
        a_spec = pl.BlockSpec((tk, tm), lambda i, j, kk: (kk, i))
        b_spec = pl.BlockSpec((tk, tn), lambda i, j, kk: (kk, j))
    return pl.pallas_call(
        body,
        name=name,
        grid=(m // tm, n // tn, nk),
        in_specs=[a_spec, b_spec] + [ANY_SPEC] * n_deps,
        out_specs=pl.BlockSpec((tm, tn), lambda i, j, kk: (i, j)),
        out_shape=jax.ShapeDtypeStruct((m, n), out_dtype),
        scratch_shapes=[pltpu.VMEM((tm, tn), F32)] if nk > 1 else [],
        compiler_params=_cparams(dimension_semantics=("parallel", "parallel", "arbitrary")),
    )(a, b, *deps)


def _rowwise(fn, name, params, tiles, outs, accs=(), ts=256, with_index=False, deps=()):
    norm = []
    for t in tiles:
        if not isinstance(t, tuple):
            t = (t, t.shape[1], 0)
        norm.append(t if len(t) == 4 else (*t, None))
    s = norm[0][0].shape[0]
    ts = min(ts, s)
    assert s % ts == 0 and ts % V7X_SUBLANES == 0
    steps = s // ts
    halo_blocks = ts // V7X_SUBLANES
    n_p, n_t, n_o = len(params), len(norm), len(outs)

    def body(*refs):
        i = pl.program_id(0)
        vals = [r[...] for r in refs[: n_p + n_t]]
        res = fn(i, steps, *vals) if with_index else fn(*vals)
        if not isinstance(res, (tuple, list)):
            res = (res,)
        first_out = n_p + n_t + len(deps)
        o_refs = refs[first_out : first_out + n_o]
        a_refs = refs[first_out + n_o :]
        for r, v in zip(o_refs, res[:n_o]):
            r[...] = v.astype(r.dtype)
        for r, v in zip(a_refs, res[n_o:]):
            _accumulate(r, v, i)

    in_specs = [pl.BlockSpec(p.shape, lambda i: (0, 0)) for p in params]
    for arr, w, cb, halo in norm:
        if halo is None:
            in_specs.append(pl.BlockSpec((ts, w), lambda i, cb=cb: (i, cb)))
        elif halo == "prev":
            in_specs.append(
                pl.BlockSpec((V7X_SUBLANES, w), lambda i, cb=cb: (jnp.maximum(i * halo_blocks - 1, 0), cb))
            )
        else:
            last = s // V7X_SUBLANES - 1
            in_specs.append(
                pl.BlockSpec((V7X_SUBLANES, w), lambda i, cb=cb: (jnp.minimum((i + 1) * halo_blocks, last), cb))
            )
    in_specs += [ANY_SPEC] * len(deps)
    out_specs = [pl.BlockSpec((ts, w), lambda i: (i, 0)) for w, _ in outs]
    out_specs += [pl.BlockSpec(shape, lambda i: (0, 0)) for shape in accs]
    out_shape = [jax.ShapeDtypeStruct((s, w), dt) for w, dt in outs]
    out_shape += [jax.ShapeDtypeStruct(shape, F32) for shape in accs]
    res = pl.pallas_call(
        body,
        name=name,
        grid=(steps,),
        in_specs=in_specs,
        out_specs=out_specs,
        out_shape=out_shape,
        compiler_params=_cparams(dimension_semantics=("arbitrary",)),
    )(*params, *[t[0] for t in norm], *deps)
    return res


def _accumulate(ref, val, step):
    @pl.when(step == 0)
    def _():
        ref[...] = val

    @pl.when(step > 0)
    def _():
        ref[...] += val


def _sigmoid(z):
    return jax.nn.sigmoid(z)


def _silu(z):
    return z * _sigmoid(z)


def _gelu(z):
    return 0.5 * z * (1.0 + jnp.tanh(0.7978845608028654 * (z + 0.044715 * (z * z * z))))


def _pre_fn(g, shift, scale, x):
    r = lax.rsqrt(jnp.mean(x * x, axis=-1, keepdims=True) + EPS)
    return ((x * r) * g) * (1.0 + scale) + shift


def _post_fn(res_w, g, gate, f, x):
    r = lax.rsqrt(jnp.mean(f * f, axis=-1, keepdims=True) + EPS)
    return x + (res_w * gate) * ((f * r) * g)


def _gates_fn(ba, bx, lam, pre, xc):
    ra = _sigmoid(pre[:, :LRU_WIDTH] + ba)
    ia = _sigmoid(pre[:, LRU_WIDTH:] + bx)
    softplus = jnp.maximum(-lam, 0.0) + jnp.log1p(jnp.exp(-jnp.abs(lam)))
    log_a = (-LRU_C) * ra * softplus
    a = jnp.exp(log_a)
    mult = jnp.sqrt(-jnp.tanh(log_a) * (a * a + 1.0))
    return a, mult * (ia * xc)


def _recin_fn(hs, yr):
    return hs * _gelu(yr)


def _merge_fn(att, rec, g_att, g_rec):
    return _sigmoid(g_att) * att + _sigmoid(g_rec) * rec


def _rowsum(v):
    return jnp.sum(v, axis=0, keepdims=True)


FFN_TM = 512
FFN_TF = 1408


def _glu_fn(g, u):
    return _silu(g) * u


FUSE_TM = 256
DW_TK = 4096
ROW_SPEC2 = pl.BlockSpec((1, D_MODEL), lambda i, j: (0, 0))
ROW_SPEC1 = pl.BlockSpec((1, D_MODEL), lambda i: (0, 0))
SUMS_SPEC1 = pl.BlockSpec((V7X_SUBLANES, D_MODEL), lambda i: (0, 0))
SUMS_SPEC2 = pl.BlockSpec((V7X_SUBLANES, D_MODEL), lambda i, j: (0, 0))
SUMS_SHAPE = jax.ShapeDtypeStruct((V7X_SUBLANES, D_MODEL), F32)


def _sum_rows(*rows):
    pad = jnp.zeros((V7X_SUBLANES - len(rows), rows[0].shape[1]), F32)
    return jnp.concatenate([*rows, pad], axis=0)


def _pre_up(x, g, shift, scale, w_gu_t, name, deps=()):
    s = x.shape[0]
    tm = min(FFN_TM, s)
    nf = D_FF // FFN_TF
    nd = len(deps)

    def body(x_ref, g_ref, sh_ref, sc_ref, wg_ref, wu_ref, *rest):
        h_ref, a_ref, gg_ref, u_ref, h_s = rest[nd:]

        @pl.when(pl.program_id(1) == 0)
        def _():
            h = _pre_fn(g_ref[...], sh_ref[...], sc_ref[...], x_ref[...]).astype(BF16)
            h_s[...] = h
            h_ref[...] = h

        hv = h_s[...]
        gv = lax.dot_general(hv, wg_ref[...], _DIMS["nt"], preferred_element_type=F32)
        uv = lax.dot_general(hv, wu_ref[...], _DIMS["nt"], preferred_element_type=F32)
        a_ref[...] = _glu_fn(gv, uv).astype(a_ref.dtype)
        gg_ref[...] = gv.astype(gg_ref.dtype)
        u_ref[...] = uv.astype(u_ref.dtype)

    rows = pl.BlockSpec((tm, D_MODEL), lambda i, j: (i, 0))
    out = pl.BlockSpec((tm, FFN_TF), lambda i, j: (i, j))
    return pl.pallas_call(
        body,
        name=name,
        grid=(s // tm, nf),
        in_specs=[rows, ROW_SPEC2, ROW_SPEC2, ROW_SPEC2,
                  pl.BlockSpec((FFN_TF, D_MODEL), lambda i, j: (j, 0)),
                  pl.BlockSpec((FFN_TF, D_MODEL), lambda i, j: (nf + j, 0))] + [ANY_SPEC] * nd,
        out_specs=[rows, out, out, out],
        out_shape=[jax.ShapeDtypeStruct((s, D_MODEL), BF16)] + [jax.ShapeDtypeStruct((s, D_FF), BF16)] * 3,
        scratch_shapes=[pltpu.VMEM((tm, D_MODEL), BF16)],
        compiler_params=_cparams(dimension_semantics=("parallel", "arbitrary")),
    )(x, g, shift, scale, w_gu_t, w_gu_t, *deps)


def _pre_matmul(x, g, shift, scale, w_t, name, b_shift=0, tn=512):
    s = x.shape[0]
    n = w_t.shape[0]
    tm = min(2 * FFN_TM, s)

    def body(x_ref, g_ref, sh_ref, sc_ref, w_ref, h_ref, o_ref, h_s):
        @pl.when(pl.program_id(1) == 0)
        def _():
            h = _pre_fn(g_ref[...], sh_ref[...], sc_ref[...], x_ref[...]).astype(BF16)
            h_s[...] = h
            h_ref[...] = h

        o_ref[...] = lax.dot_general(h_s[...], w_ref[...], _DIMS["nt"], preferred_element_type=F32)

    rows = pl.BlockSpec((tm, D_MODEL), lambda i, j: (i, 0))
    return pl.pallas_call(
        body,
        name=name,
        grid=(s // tm, n // tn),
        in_specs=[rows, ROW_SPEC2, ROW_SPEC2, ROW_SPEC2,
                  pl.BlockSpec((tn, D_MODEL), lambda i, j: ((j + b_shift) % (n // tn), 0))],
        out_specs=[rows, pl.BlockSpec((tm, tn), lambda i, j: (i, j))],
        out_shape=[jax.ShapeDtypeStruct((s, D_MODEL), BF16), jax.ShapeDtypeStruct((s, n), F32)],
        scratch_shapes=[pltpu.VMEM((tm, D_MODEL), BF16)],
        compiler_params=_cparams(dimension_semantics=("parallel", "arbitrary")),
    )(x, g, shift, scale, w_t)


def _matmul_post(a, w, x, g_post, gate, res_w, name, target=None):
    s, k = a.shape
    tm = min(FFN_TM, s)
    extra = [] if target is None else [target]

    def body(a_ref, w_ref, x_ref, g_ref, gate_ref, *rest):
        f = jnp.dot(a_ref[...], w_ref[...], preferred_element_type=F32)
        y = _post_fn(res_w, g_ref[...], gate_ref[...], f, x_ref[...])
        if target is None:
            f_ref, y_ref = rest
            y_ref[...] = y
        else:
            t_ref, f_ref, dy_ref, sq_ref = rest
            diff = y - t_ref[...]
            dy_ref[...] = diff * (1.0 / D_MODEL)
            _accumulate(sq_ref, _rowsum(diff * diff), pl.program_id(0))
        f_ref[...] = f

    rows = pl.BlockSpec((tm, D_MODEL), lambda i: (i, 0))
    out_specs, out_shape = [rows, rows], [jax.ShapeDtypeStruct((s, D_MODEL), F32)] * 2
    if target is not None:
        out_specs.append(ROW_SPEC1)
        out_shape.append(jax.ShapeDtypeStruct((1, D_MODEL), F32))
    return pl.pallas_call(
        body,
        name=name,
        grid=(s // tm,),
        in_specs=[pl.BlockSpec((tm, k), lambda i: (i, 0)), pl.BlockSpec((k, D_MODEL), lambda i: (0, 0)), rows,
                  ROW_SPEC1, ROW_SPEC1] + [rows] * len(extra),
        out_specs=out_specs,
        out_shape=out_shape,
        compiler_params=_cparams(dimension_semantics=("arbitrary",)),
    )(a, w, x, g_post, gate, *extra)


def _merge_matmul_post(att, rec, proj, w, x, g_post, gate, res_w, name):
    s = att.shape[0]
    tm = min(FUSE_TM, s)

    def body(att_ref, rec_ref, ga_ref, gr_ref, w_ref, x_ref, g_ref, gate_ref, m_ref, f_ref, y_ref):
        merged = _merge_fn(att_ref[...], rec_ref[...], ga_ref[...], gr_ref[...]).astype(BF16)
        m_ref[...] = merged
        f = jnp.dot(merged, w_ref[...], preferred_element_type=F32)
        f_ref[...] = f
        y_ref[...] = _post_fn(res_w, g_ref[...], gate_ref[...], f, x_ref[...])

    rows = pl.BlockSpec((tm, D_MODEL), lambda i: (i, 0))
    return pl.pallas_call(
        body,
        name=name,
        grid=(s // tm,),
        in_specs=[rows, rows, pl.BlockSpec((tm, D_MODEL), lambda i: (i, 2)), pl.BlockSpec((tm, D_MODEL), lambda i: (i, 3)),
                  pl.BlockSpec(w.shape, lambda i: (0, 0)), rows, ROW_SPEC1, ROW_SPEC1],
        out_specs=[rows, rows, rows],
        out_shape=[jax.ShapeDtypeStruct((s, D_MODEL), BF16)] + [jax.ShapeDtypeStruct((s, D_MODEL), F32)] * 2,
        compiler_params=_cparams(dimension_semantics=("parallel",)),
    )(att, rec, proj, proj, w, x, g_post, gate)


def _post_bwd_merge_bwd(f, dy, g_post, gate, res_w, w, att, rec, proj, name):
    s = f.shape[0]
    tm = min(FUSE_TM, s)

    def body(f_ref, dy_ref, gp_ref, gate_ref, w_ref, att_ref, rec_ref, ga_ref, gr_ref,
             df_ref, datt_ref, drec_ref, dga_ref, dgr_ref, sums_ref):
        i = pl.program_id(0)
        dgp, dgate, df = _post_vjp(res_w, gp_ref[...], gate_ref[...], f_ref[...], dy_ref[...])
        dfb = df.astype(BF16)
        df_ref[...] = dfb
        _accumulate(sums_ref, _sum_rows(dgp, dgate), i)
        dmerged = lax.dot_general(dfb, w_ref[...], _DIMS["nt"], preferred_element_type=F32)
        _, vjp = jax.vjp(_merge_fn, att_ref[...], rec_ref[...], ga_ref[...], gr_ref[...])
        for ref, val in zip((datt_ref, drec_ref, dga_ref, dgr_ref), vjp(dmerged)):
            ref[...] = val.astype(ref.dtype)

    rows = pl.BlockSpec((tm, D_MODEL), lambda i: (i, 0))
    return pl.pallas_call(
        body,
        name=name,
        grid=(s // tm,),
        in_specs=[rows, rows, ROW_SPEC1, ROW_SPEC1, pl.BlockSpec(w.shape, lambda i: (0, 0)), rows, rows,
                  pl.BlockSpec((tm, D_MODEL), lambda i: (i, 2)), pl.BlockSpec((tm, D_MODEL), lambda i: (i, 3))],
        out_specs=[rows] * 5 + [SUMS_SPEC1],
        out_shape=[jax.ShapeDtypeStruct((s, D_MODEL), BF16)] * 5 + [SUMS_SHAPE],
        compiler_params=_cparams(dimension_semantics=("arbitrary",)),
    )(f, dy, g_post, gate, w, att, rec, proj, proj)


def _matmul_recin_bwd(drec, w, hs, proj, name, deps=()):
    s = drec.shape[0]
    tm = min(FUSE_TM, s)
    nd = len(deps)

    def body(d_ref, w_ref, hs_ref, yr_ref, *rest):
        dhs_ref, dyr_ref = rest[nd:]
        d = lax.dot_general(d_ref[...], w_ref[...], _DIMS["nt"], preferred_element_type=F32)
        _, vjp = jax.vjp(_recin_fn, hs_ref[...], yr_ref[...])
        dhs, dyr = vjp(d)
        dhs_ref[...] = dhs
        dyr_ref[...] = dyr.astype(dyr_ref.dtype)

    rows = pl.BlockSpec((tm, D_MODEL), lambda i: (i, 0))
    return pl.pallas_call(
        body,
        name=name,
        grid=(s // tm,),
        in_specs=[rows, pl.BlockSpec(w.shape, lambda i: (0, 0)), rows,
                  pl.BlockSpec((tm, D_MODEL), lambda i: (i, 1))] + [ANY_SPEC] * nd,
        out_specs=[rows, rows],
        out_shape=[jax.ShapeDtypeStruct((s, D_MODEL), F32), jax.ShapeDtypeStruct((s, D_MODEL), BF16)],
        compiler_params=_cparams(dimension_semantics=("parallel",)),
    )(drec, w, hs, proj, *deps)


def _post_vjp(res_w, g, gate, f, dy):
    _, vjp = jax.vjp(lambda g, gate, f: _post_fn(res_w, g, gate, f, 0.0), g, gate, f)
    return vjp(dy)


def _post_bwd_up_bwd(f, dy, g_post, gate, res_w, w_down, g, u, name, deps=()):
    s = f.shape[0]
    tm = min(FFN_TM, s)
    nd = len(deps)

    def body(f_ref, dy_ref, gp_ref, gate_ref, wd_ref, g_ref, u_ref, *rest):
        df_ref, dgu_ref, sums_ref, df_s = rest[nd:]
        i = pl.program_id(0)

        @pl.when(pl.program_id(1) == 0)
        def _():
            dgp, dgate, df = _post_vjp(res_w, gp_ref[...], gate_ref[...], f_ref[...], dy_ref[...])
            df_s[...] = df.astype(BF16)
            df_ref[...] = df_s[...]
            _accumulate(sums_ref, _sum_rows(dgp, dgate), i)

        da = lax.dot_general(df_s[...], wd_ref[...], _DIMS["nt"], preferred_element_type=F32)
        _, vjp = jax.vjp(_glu_fn, g_ref[...].astype(F32), u_ref[...].astype(F32))
        dg, du = vjp(da)
        dgu_ref[0] = dg.astype(dgu_ref.dtype)
        dgu_ref[1] = du.astype(dgu_ref.dtype)

    rows = pl.BlockSpec((tm, D_MODEL), lambda i, j: (i, 0))
    blk = pl.BlockSpec((tm, FFN_TF), lambda i, j: (i, j))
    return pl.pallas_call(
        body,
        name=name,
        grid=(s // tm, D_FF // FFN_TF),
        in_specs=[rows, rows, ROW_SPEC2, ROW_SPEC2, pl.BlockSpec((FFN_TF, D_MODEL), lambda i, j: (j, 0)), blk,
                  blk] + [ANY_SPEC] * nd,
        out_specs=[rows, pl.BlockSpec((2, tm, FFN_TF), lambda i, j: (0, i, j)), SUMS_SPEC2],
        out_shape=[jax.ShapeDtypeStruct((s, D_MODEL), BF16), jax.ShapeDtypeStruct((2, s, D_FF), BF16), SUMS_SHAPE],
        scratch_shapes=[pltpu.VMEM((tm, D_MODEL), BF16)],
        compiler_params=_cparams(dimension_semantics=("arbitrary", "arbitrary")),
    )(f, dy, g_post, gate, w_down, g, u, *deps)


def _matmul_pre_bwd(parts, w_t, x, dres, g, shift, scale, name, deps=()):
    s = x.shape[0]
    na, nd = len(parts), len(deps)
    ranges = [p[3] for p in parts]

    def body(*refs):
        a_refs = refs[:na]
        w_ref, x_ref, dres_ref, g_ref, sh_ref, sc_ref = refs[na : na + 6]
        dx_ref, sums_ref = refs[na + 6 + nd :]
        i = pl.program_id(0)
        dh = None
        for a_ref, (r0, r1) in zip(a_refs, ranges):
            p = jnp.dot(a_ref[...], w_ref[r0:r1, :], preferred_element_type=F32)
            dh = p if dh is None else dh + p
        _, vjp = jax.vjp(_pre_fn, g_ref[...], sh_ref[...], sc_ref[...], x_ref[...])
        dg, dsh, dsc, dx = vjp(dh)
        dx_ref[...] = dx + dres_ref[...]
        _accumulate(sums_ref, _sum_rows(dg, dsh, dsc), i)

    tm = parts[0][1][-2]
    rows = pl.BlockSpec((tm, D_MODEL), lambda i: (i, 0))
    return pl.pallas_call(
        body,
        name=name,
        grid=(s // tm,),
        in_specs=[pl.BlockSpec(p[1], p[2]) for p in parts]
        + [pl.BlockSpec(w_t.shape, lambda i: (0, 0)), rows, rows, ROW_SPEC1, ROW_SPEC1, ROW_SPEC1]
        + [ANY_SPEC] * nd,
        out_specs=[rows, SUMS_SPEC1],
        out_shape=[jax.ShapeDtypeStruct((s, D_MODEL), F32), SUMS_SHAPE],
        compiler_params=_cparams(dimension_semantics=("arbitrary",)),
    )(*[p[0] for p in parts], w_t, x, dres, g, shift, scale, *deps)


def _dw_gu(dgu, h, name, deps=(), tk=DW_TK):
    s = h.shape[0]
    tk = min(tk, s)
    nk = s // tk
    half = D_FF // FFN_TF

    def body(a_ref, b_ref, *rest):
        o_ref = rest[len(deps)]
        kk = pl.program_id(1)
        p = lax.dot_general(a_ref[...], b_ref[...], _DIMS["tn"], preferred_element_type=F32)
        if nk == 1:
            o_ref[...] = p.astype(o_ref.dtype)
            return
        acc = rest[len(deps) + 1]

        @pl.when(kk == 0)
        def _():
            acc[...] = p

        @pl.when(kk > 0)
        def _():
            acc[...] += p

        @pl.when(kk == nk - 1)
        def _():
            o_ref[...] = acc[...].astype(o_ref.dtype)

    return pl.pallas_call(
        body,
        name=name,
        grid=(2 * half, nk),
        in_specs=[pl.BlockSpec((None, tk, FFN_TF), lambda i, kk: (i // half, kk, i % half)),
                  pl.BlockSpec((tk, D_MODEL), lambda i, kk: (kk, 0))] + [ANY_SPEC] * len(deps),
        out_specs=pl.BlockSpec((FFN_TF, D_MODEL), lambda i, kk: (i, 0)),
        out_shape=jax.ShapeDtypeStruct((2 * D_FF, D_MODEL), BF16),
        scratch_shapes=[pltpu.VMEM((FFN_TF, D_MODEL), F32)] if nk > 1 else [],
        compiler_params=_cparams(dimension_semantics=("parallel", "arbitrary")),
    )(dgu, h, *deps)


def _shift_down(ext, j, rows):
    return pltpu.roll(ext, j, 0)[V7X_SUBLANES : V7X_SUBLANES + rows]


def _shift_up(ext, j, rows):
    return pltpu.roll(ext, ext.shape[0] - j, 0)[:rows] if j else ext[:rows]


LRU_SLAB = 256
N_SLABS = LRU_WIDTH // LRU_SLAB


def _slab_weights(wa, wx):
    per = LRU_SLAB // LRU_BLOCK
    eye = jnp.eye(per, dtype=wa.dtype)

    def diag(w):
        w4 = w.reshape(N_SLABS, per, LRU_BLOCK, LRU_BLOCK)
        return jnp.einsum("sbkj,bc->sbkcj", w4, eye).reshape(N_SLABS, LRU_SLAB, LRU_SLAB)

    return jnp.concatenate([diag(wa), diag(wx)], axis=2).reshape(LRU_WIDTH, 2 * LRU_SLAB).astype(BF16)


def _slab_cols(v, s):
    lo = s * LRU_SLAB
    return jnp.concatenate([v[:, lo : lo + LRU_SLAB], v[:, LRU_WIDTH + lo : LRU_WIDTH + lo + LRU_SLAB]], axis=1)


def _lru_front(proj, w8, b, w_slab, ba, bx, lam, name):
    def fn(i, steps, w8, b, w_slab, ba, bx, lam, x, halo):
        halo = jnp.where(i > 0, halo, 0.0)
        ext = jnp.concatenate([halo, x], axis=0)
        xc = b + w8[3:4] * x
        for j in (1, 2, 3):
            xc = xc + w8[3 - j : 4 - j] * _shift_down(ext, j, x.shape[0])
        xcb = xc.astype(BF16)
        prods = []
        for s in range(N_SLABS):
            rows = slice(s * LRU_SLAB, (s + 1) * LRU_SLAB)
            prods.append(jnp.dot(xcb[:, rows], w_slab[rows], preferred_element_type=F32))
        pre = jnp.concatenate([p[:, :LRU_SLAB] for p in prods] + [p[:, LRU_SLAB:] for p in prods], axis=1)
        a, u = _gates_fn(ba, bx, lam, pre, xc)
        return xc, pre, a, u

    tiles = [(proj, LRU_WIDTH, 0), (proj, LRU_WIDTH, 0, "prev")]
    outs = [(LRU_WIDTH, F32), (2 * LRU_WIDTH, F32), (LRU_WIDTH, F32), (LRU_WIDTH, F32)]
    return _rowwise(fn, name, [w8, b, w_slab, ba, bx, lam], tiles, outs, with_index=True)


def _lru_back(pre, xc, w_slab, ba, bx, lam, g, h_prev, name, deps=()):
    def fn(w_slab, ba, bx, lam, pre, xc, g, h_prev):
        _, vjp = jax.vjp(_gates_fn, ba, bx, lam, pre, xc)
        dba, dbx, dlam, dpre, dxc = vjp((g * h_prev, g))
        dpre = dpre.astype(BF16)
        back = []
        for s in range(N_SLABS):
            rows = slice(s * LRU_SLAB, (s + 1) * LRU_SLAB)
            back.append(lax.dot_general(_slab_cols(dpre, s), w_slab[rows], _DIMS["nt"], preferred_element_type=F32))
        return dpre, dxc + jnp.concatenate(back, axis=1), _sum_rows(dba, dbx, dlam)

    return _rowwise(fn, name, [w_slab, ba, bx, lam], [pre, xc, g, h_prev],
                    [(2 * LRU_WIDTH, BF16), (LRU_WIDTH, F32)], [(V7X_SUBLANES, LRU_WIDTH)], deps=deps)


def _lru_dw(xc, dpre, name):
    s = xc.shape[0]
    ts = min(512, s)
    steps = s // ts
    per = LRU_SLAB // LRU_BLOCK

    def body(x_ref, d_ref, o_ref, acc):
        i = pl.program_id(0)
        xcb = x_ref[...].astype(BF16)
        d = d_ref[...]
        for sl in range(N_SLABS):
            rows = slice(sl * LRU_SLAB, (sl + 1) * LRU_SLAB)
            p = lax.dot_general(xcb[:, rows], _slab_cols(d, sl), _DIMS["tn"], preferred_element_type=F32)

            @pl.when(i == 0)
            def _(p=p, rows=rows):
                acc[rows, :] = p

            @pl.when(i > 0)
            def _(p=p, rows=rows):
                acc[rows, :] += p

        @pl.when(i == steps - 1)
        def _():
            for half in range(2):
                for n in range(LRU_BLOCKS):
                    r0 = n * LRU_BLOCK
                    c0 = half * LRU_SLAB + (n % per) * LRU_BLOCK
                    o_ref[half, r0 : r0 + LRU_BLOCK, :] = acc[r0 : r0 + LRU_BLOCK, c0 : c0 + LRU_BLOCK]

    return pl.pallas_call(
        body,
        name=name,
        grid=(steps,),
        in_specs=[pl.BlockSpec((ts, LRU_WIDTH), lambda i: (i, 0)), pl.BlockSpec((ts, 2 * LRU_WIDTH), lambda i: (i, 0))],
        out_specs=pl.BlockSpec((2, LRU_WIDTH, LRU_BLOCK), lambda i: (0, 0, 0)),
        out_shape=jax.ShapeDtypeStruct((2, LRU_WIDTH, LRU_BLOCK), F32),
        scratch_shapes=[pltpu.VMEM((LRU_WIDTH, 2 * LRU_SLAB), F32)],
        compiler_params=_cparams(dimension_semantics=("arbitrary",)),
    )(xc, dpre)


def _conv_bwd(proj, w8, d1, name):
    def fn(i, steps, w8, x, halo, d, d1n):
        rows = x.shape[0]
        dn = jnp.where(i < steps - 1, d1n, 0.0)
        halo = jnp.where(i > 0, halo, 0.0)
        dext = jnp.concatenate([d, dn], axis=0)
        xext = jnp.concatenate([halo, x], axis=0)
        dx = w8[3:4] * d
        dw = [None] * 4
        dw[3] = _rowsum(d * x)
        for k in (1, 2, 3):
            dx = dx + w8[3 - k : 4 - k] * _shift_up(dext, k, rows)
            dw[3 - k] = _rowsum(d * _shift_down(xext, k, rows))
        return dx, _sum_rows(*dw, _rowsum(d))

    tiles = [(proj, LRU_WIDTH, 0), (proj, LRU_WIDTH, 0, "prev"), d1, (d1, LRU_WIDTH, 0, "next")]
    return _rowwise(fn, name, [w8], tiles, [(LRU_WIDTH, BF16)], [(V7X_SUBLANES, LRU_WIDTH)], with_index=True)


SCAN_ROWS = 512


def _block_scan(a, b, row, reverse):
    for d in (1, 2, 4):
        if reverse:
            shift, keep = V7X_SUBLANES - d, row < V7X_SUBLANES - d
        else:
            shift, keep = d, row >= d
        a_s = pltpu.roll(a, shift, 0)
        b_s = pltpu.roll(b, shift, 0)
        b = jnp.where(keep, a * b_s + b, b)
        a = jnp.where(keep, a * a_s, a)
    return a, b


def _scan_fwd(a, u, proj, name):
    s, w = a.shape
    ts = min(SCAN_ROWS, s)
    sub = ts // V7X_SUBLANES

    def body(a_ref, u_ref, yr_ref, h_ref, hp_ref, rec_ref, carry):
        @pl.when(pl.program_id(0) == 0)
        def _():
            carry[...] = jnp.zeros_like(carry)

        row = lax.broadcasted_iota(jnp.int32, (V7X_SUBLANES, w), 0)

        def step(j, c):
            rows = pl.ds(pl.multiple_of(j * V7X_SUBLANES, V7X_SUBLANES), V7X_SUBLANES)
            pa, pb = _block_scan(a_ref[rows, :], u_ref[rows, :], row, False)
            h = pb + pa * c
            h_ref[rows, :] = h
            hp_ref[rows, :] = jnp.where(row >= 1, pltpu.roll(h, 1, 0), c)
            return jnp.broadcast_to(h[V7X_SUBLANES - 1 :], (V7X_SUBLANES, w))

        carry[...] = lax.fori_loop(0, sub, step, carry[...])
        rec_ref[...] = _recin_fn(h_ref[...], yr_ref[...]).astype(rec_ref.dtype)

    spec = pl.BlockSpec((ts, w), lambda i: (i, 0))
    return pl.pallas_call(
        body,
        name=name,
        grid=(s // ts,),
        in_specs=[spec, spec, pl.BlockSpec((ts, w), lambda i: (i, 1))],
        out_specs=[spec, spec, spec],
        out_shape=[jax.ShapeDtypeStruct((s, w), F32)] * 2 + [jax.ShapeDtypeStruct((s, w), BF16)],
        scratch_shapes=[pltpu.VMEM((V7X_SUBLANES, w), F32)],
        compiler_params=_cparams(dimension_semantics=("arbitrary",)),
    )(a, u, proj)


def _scan_bwd(a, dh, name):
    s, w = a.shape
    ts = min(SCAN_ROWS, s)
    sub = ts // V7X_SUBLANES
    steps = s // ts

    def body(a_ref, d_ref, g_ref, carry):
        @pl.when(pl.program_id(0) == 0)
        def _():
            carry[...] = jnp.zeros_like(carry)

        row = lax.broadcasted_iota(jnp.int32, (V7X_SUBLANES, w), 0)

        def step(jj, c):
            j = sub - 1 - jj
            rows = pl.ds(pl.multiple_of(j * V7X_SUBLANES, V7X_SUBLANES), V7X_SUBLANES)
            av, dv = a_ref[rows, :], d_ref[rows, :]
            pa, pb = _block_scan(av, av * dv, row, True)
            big = pb + pa * c
            g_ref[rows, :] = dv + jnp.where(row < V7X_SUBLANES - 1, pltpu.roll(big, V7X_SUBLANES - 1, 0), c)
            return jnp.broadcast_to(big[:1], (V7X_SUBLANES, w))

        carry[...] = lax.fori_loop(0, sub, step, carry[...])

    spec = pl.BlockSpec((ts, w), lambda i: (steps - 1 - i, 0))
    return pl.pallas_call(
        body,
        name=name,
        grid=(steps,),
        in_specs=[spec, spec],
        out_specs=spec,
        out_shape=jax.ShapeDtypeStruct((s, w), F32),
        scratch_shapes=[pltpu.VMEM((V7X_SUBLANES, w), F32)],
        compiler_params=_cparams(dimension_semantics=("arbitrary",)),
    )(a, dh)


SKEW = 4 * ATT_TQ


def _skew_onehot():
    t = np.arange(SKEW)
    diag = np.where(t < 3 * ATT_TQ, -t, SKEW - t)
    idx = np.clip(diag + LEFT_CHUNKS * CHUNK, -MAX_REL, MAX_REL) + MAX_REL
    hit = (idx[:, None] == np.arange(2 * MAX_REL + 1)[None, :]) & (t[:, None] != 3 * ATT_TQ)
    return hit.astype(np.float32)


def _bias_tile(rel_bias, name):
    per_t = jnp.dot(rel_bias, jnp.asarray(_skew_onehot()).T, precision=lax.Precision.HIGHEST)
    win = 3 * ATT_TQ

    def body(t_ref, o_ref):
        tile = pltpu.roll(jnp.broadcast_to(t_ref[0], (ATT_TQ, SKEW)), 0, 1, stride=1, stride_axis=0)[:, :win]
        qc = lax.broadcasted_iota(jnp.int32, (ATT_TQ, win), 0) // CHUNK
        kpos = lax.broadcasted_iota(jnp.int32, (ATT_TQ, win), 1)
        band = (kpos // CHUNK >= qc) & (kpos // CHUNK <= qc + LEFT_CHUNKS)
        for v in range(3):
            o_ref[v, 0] = jnp.where(band & (kpos >= (2 - v) * ATT_TQ), tile, NEG)

    return pl.pallas_call(
        body,
        name=name,
        grid=(ATT_HEADS,),
        in_specs=[pl.BlockSpec((1, 1, SKEW), lambda h: (h, 0, 0))],
        out_specs=pl.BlockSpec((3, 1, ATT_TQ, win), lambda h: (0, h, 0, 0)),
        out_shape=jax.ShapeDtypeStruct((3, ATT_HEADS, ATT_TQ, win), F32),
        compiler_params=_cparams(dimension_semantics=("parallel",)),
    )(per_t.reshape(ATT_HEADS, 1, SKEW))


def _bias_grad(dbias, name):
    win = 3 * ATT_TQ

    def body(d_ref, o_ref):
        d = jnp.concatenate([d_ref[0], jnp.zeros((ATT_TQ, SKEW - win), F32)], axis=1)
        r = lax.broadcasted_iota(jnp.int32, (ATT_TQ, ATT_TQ), 0)
        c = lax.broadcasted_iota(jnp.int32, (ATT_TQ, ATT_TQ), 1)
        flip = (r + c == ATT_TQ - 1).astype(F32)
        d = jnp.dot(flip, d, preferred_element_type=F32, precision=lax.Precision.HIGHEST)
        o_ref[0] = jnp.sum(pltpu.roll(d, SKEW - (ATT_TQ - 1), 1, stride=1, stride_axis=0), axis=0, keepdims=True)

    per_t = pl.pallas_call(
        body,
        name=name,
        grid=(ATT_HEADS,),
        in_specs=[pl.BlockSpec((1, ATT_TQ, win), lambda h: (h, 0, 0))],
        out_specs=pl.BlockSpec((1, 1, SKEW), lambda h: (h, 0, 0)),
        out_shape=jax.ShapeDtypeStruct((ATT_HEADS, 1, SKEW), F32),
        compiler_params=_cparams(dimension_semantics=("parallel",)),
    )(dbias)
    return jnp.dot(per_t.reshape(ATT_HEADS, SKEW), jnp.asarray(_skew_onehot()), precision=lax.Precision.HIGHEST)


ATT_STEP_HEADS = ATT_HEADS
ATT_STEP_COLS = ATT_STEP_HEADS * ATT_HEAD_DIM


def _attn_specs(nt):
    qb, kb, vb = OFF_Q // ATT_STEP_COLS, OFF_K // ATT_STEP_COLS, OFF_V // ATT_STEP_COLS
    blk = (ATT_TQ, ATT_STEP_COLS)

    def qmap(base):
        return lambda hp, m: (jnp.minimum(m, nt - 1), base + hp)

    def wmap(base, back):
        return lambda hp, m: (jnp.clip(m - back, 0, nt - 1), base + hp)

    specs = [pl.BlockSpec(blk, qmap(qb))]
    specs += [pl.BlockSpec(blk, wmap(kb, back)) for back in (2, 1, 0)]
    specs += [pl.BlockSpec(blk, wmap(vb, back)) for back in (2, 1, 0)]
    return specs


ATT_SCALE = ATT_HEAD_DIM**-0.5


def _attn_exp(qh, kh, bias):
    s = lax.dot_general(qh, kh, _DIMS["nt"], preferred_element_type=F32) + bias
    e = jnp.exp(s - jnp.max(s, axis=-1, keepdims=True))
    return e, jnp.sum(e, axis=-1, keepdims=True)


def _attn_window(k0, k1, k2, v0, v1, v2):
    k = jnp.concatenate([k0[...], k1[...], k2[...]], axis=0).astype(BF16)
    v = jnp.concatenate([v0[...], v1[...], v2[...]], axis=0).astype(BF16)
    return k, v


def _bias_spec():
    return pl.BlockSpec((1, ATT_STEP_HEADS, ATT_TQ, 3 * ATT_TQ), lambda hp, m: (jnp.minimum(m, 2), hp, 0, 0))


def _attn_fwd(proj, bias, name):
    s = proj.shape[0]
    nt = s // ATT_TQ

    def body(q_ref, k0, k1, k2, v0, v1, v2, b_ref, o_ref):
        k, v = _attn_window(k0, k1, k2, v0, v1, v2)
        q = (q_ref[...] * ATT_SCALE).astype(BF16)
        for hh in range(ATT_STEP_HEADS):
            cols = slice(hh * ATT_HEAD_DIM, (hh + 1) * ATT_HEAD_DIM)
            e, total = _attn_exp(q[:, cols], k[:, cols], b_ref[0, hh])
            o = jnp.dot(e.astype(BF16), v[:, cols], preferred_element_type=F32) / total
            o_ref[:, cols] = o.astype(o_ref.dtype)

    specs = _attn_specs(nt) + [_bias_spec()]
    return pl.pallas_call(
        body,
        name=name,
        grid=(ATT_HEADS // ATT_STEP_HEADS, nt),
        in_specs=specs,
        out_specs=pl.BlockSpec((ATT_TQ, ATT_STEP_COLS), lambda hp, m: (m, hp)),
        out_shape=jax.ShapeDtypeStruct((s, ATT_WIDTH), BF16),
        compiler_params=_cparams(dimension_semantics=("parallel", "arbitrary")),
    )(proj, proj, proj, proj, proj, proj, proj, bias)


def _attn_bwd(proj, bias, do, name):
    s = proj.shape[0]
    nt = s // ATT_TQ
    win = 3 * ATT_TQ

    def body(q_ref, k0, k1, k2, v0, v1, v2, do_ref, b_ref, dq_ref, dk_ref, dv_ref, db_ref, dk_acc, dv_acc):
        m = pl.program_id(1)

        @pl.when(m == 0)
        def _():
            dk_acc[...] = jnp.zeros_like(dk_acc)
            dv_acc[...] = jnp.zeros_like(dv_acc)
            db_ref[...] = jnp.zeros_like(db_ref)

        @pl.when(m < nt)
        def _():
            k, v = _attn_window(k0, k1, k2, v0, v1, v2)
            q = (q_ref[...] * ATT_SCALE).astype(BF16)
            dout = do_ref[...]
            for hh in range(ATT_STEP_HEADS):
                cols = slice(hh * ATT_HEAD_DIM, (hh + 1) * ATT_HEAD_DIM)
                qh, kh, vh, doh = q[:, cols], k[:, cols], v[:, cols], dout[:, cols]
                e, total = _attn_exp(qh, kh, b_ref[0, hh])
                p = e / total
                dvh = lax.dot_general(p.astype(BF16), doh, _DIMS["tn"], preferred_element_type=F32)
                dp = lax.dot_general(doh, vh, _DIMS["nt"], preferred_element_type=F32)
                ds = p * (dp - jnp.sum(dp * p, axis=-1, keepdims=True))
                db_ref[hh] += ds
                dsb = ds.astype(BF16)
                dqh = jnp.dot(dsb, kh, preferred_element_type=F32) * ATT_SCALE
                dkh = lax.dot_general(dsb, qh, _DIMS["tn"], preferred_element_type=F32)
                dq_ref[:, cols] = dqh.astype(dq_ref.dtype)
                dk_acc[:, cols] += dkh
                dv_acc[:, cols] += dvh

        dk_ref[...] = dk_acc[:ATT_TQ].astype(dk_ref.dtype)
        dv_ref[...] = dv_acc[:ATT_TQ].astype(dv_ref.dtype)
        for acc in (dk_acc, dv_acc):
            rest = acc[ATT_TQ:]
            acc[: win - ATT_TQ] = rest
            acc[win - ATT_TQ :] = jnp.zeros((ATT_TQ, ATT_STEP_COLS), F32)

    blk = (ATT_TQ, ATT_STEP_COLS)
    specs = _attn_specs(nt)
    specs.append(pl.BlockSpec(blk, lambda hp, m: (jnp.minimum(m, nt - 1), hp)))
    specs.append(_bias_spec())
    done = lambda hp, m: (jnp.maximum(m - 2, 0), hp)
    out_specs = [
        pl.BlockSpec(blk, lambda hp, m: (jnp.minimum(m, nt - 1), hp)),
        pl.BlockSpec(blk, done),
        pl.BlockSpec(blk, done),
        pl.BlockSpec((ATT_STEP_HEADS, ATT_TQ, win), lambda hp, m: (hp, 0, 0)),
    ]
    out_shape = [jax.ShapeDtypeStruct((s, ATT_WIDTH), BF16)] * 3
    out_shape.append(jax.ShapeDtypeStruct((ATT_HEADS, ATT_TQ, win), F32))
    return pl.pallas_call(
        body,
        name=name,
        grid=(ATT_HEADS // ATT_STEP_HEADS, nt + 2),
        in_specs=specs,
        out_specs=out_specs,
        out_shape=out_shape,
        scratch_shapes=[pltpu.VMEM((win, ATT_STEP_COLS), F32), pltpu.VMEM((win, ATT_STEP_COLS), F32)],
        compiler_params=_cparams(dimension_semantics=("arbitrary", "arbitrary")),
    )(proj, proj, proj, proj, proj, proj, proj, do, bias)


def _ada_fwd(c_all, w, name):
    def body(c_ref, w_ref, o_ref):
        act = _silu(c_ref[...]).astype(BF16)
        o_ref[...] = jnp.dot(act, w_ref[...].astype(BF16), preferred_element_type=F32)

    return pl.pallas_call(
        body, name=name, out_shape=jax.ShapeDtypeStruct((c_all.shape[0], w.shape[1]), F32), compiler_params=_cparams()
    )(c_all, w)


def _ada_bwd(c_all, dmod, name):
    def body(c_ref, d_ref, o_ref):
        act = _silu(c_ref[...])
        o_ref[...] = lax.dot_general(act, d_ref[...], _DIMS["tn"], preferred_element_type=F32,
                                     precision=lax.Precision.HIGHEST)

    return pl.pallas_call(
        body, name=name, out_shape=jax.ShapeDtypeStruct((c_all.shape[1], dmod.shape[1]), F32), compiler_params=_cparams()
    )(c_all, dmod)


def _adamw_parts(landed, sent, me, w, m, v, name, rows=256):
    r, c = w.shape
    tr = _pick(r, rows, 16)

    def body(me_ref, g_ref, own_ref, w_ref, m_ref, v_ref, go_ref, d_ref, mo_ref, vo_ref):
        mine = me_ref[0]
        grad = jnp.zeros((tr, c), F32)
        for d in range(N_DEV):
            grad = grad + jnp.where(mine == d, own_ref[0], g_ref[d]).astype(F32)
        _adamw_update(grad, w_ref, m_ref, v_ref, go_ref, d_ref, mo_ref, vo_ref)

    spec = pl.BlockSpec((tr, c), lambda i, me_ref: (i, 0))
    return pl.pallas_call(
        body,
        name=name,
        grid_spec=pltpu.PrefetchScalarGridSpec(
            num_scalar_prefetch=1,
            grid=(r // tr,),
            in_specs=[pl.BlockSpec((N_DEV, tr, c), lambda i, me_ref: (0, i, 0)),
                      pl.BlockSpec((1, tr, c), lambda i, me_ref: (me_ref[0], i, 0)), spec, spec, spec],
            out_specs=[spec] * 4,
        ),
        out_shape=[jax.ShapeDtypeStruct((r, c), F32)] * 4,
        compiler_params=_cparams(dimension_semantics=("parallel",)),
    )(me.reshape(1).astype(jnp.int32), landed, sent, w, m, v)


def _adamw_update(grad, w_ref, m_ref, v_ref, go_ref, d_ref, mo_ref, vo_ref):
    m2 = ADAM_B1 * m_ref[...] + (1.0 - ADAM_B1) * grad
    v2 = ADAM_B2 * v_ref[...] + (1.0 - ADAM_B2) * (grad * grad)
    m_hat = m2 / (1.0 - ADAM_B1**ADAM_STEP)
    v_hat = v2 / (1.0 - ADAM_B2**ADAM_STEP)
    go_ref[...] = grad
    d_ref[...] = -ADAM_LR * (m_hat / (jnp.sqrt(v_hat) + ADAM_EPS) + ADAM_WD * w_ref[...])
    mo_ref[...] = m2
    vo_ref[...] = v2


def _adamw(g, w, m, v, name, rows=256):
    r, c = w.shape
    tr = _pick(r, rows, 16)

    def body(g_ref, w_ref, m_ref, v_ref, go_ref, d_ref, mo_ref, vo_ref):
        _adamw_update(g_ref[...], w_ref, m_ref, v_ref, go_ref, d_ref, mo_ref, vo_ref)

    spec = pl.BlockSpec((tr, c), lambda i: (i, 0))
    return pl.pallas_call(
        body,
        name=name,
        grid=(r // tr,),
        in_specs=[spec, spec, spec, spec],
        out_specs=[spec] * 4,
        out_shape=[jax.ShapeDtypeStruct((r, c), F32)] * 4,
        compiler_params=_cparams(dimension_semantics=("parallel",)),
    )(g, w, m, v)


def _sum_parts(parts, name):
    def body(p_ref, o_ref):
        acc = p_ref[0]
        for d in range(1, N_DEV):
            acc = acc + p_ref[d]
        o_ref[...] = acc

    return pl.pallas_call(
        body, name=name, out_shape=jax.ShapeDtypeStruct(parts.shape[1:], F32), compiler_params=_cparams()
    )(parts)


def _place():
    x, y, c = lax.axis_index("x"), lax.axis_index("y"), lax.axis_index("c")
    return x, y, c


def _dev_index(p):
    return 4 * p[0] + 2 * p[1] + p[2]


def _allgather_vmem(shard, name):
    m_per, n = shard.shape

    def body(x_ref, out_ref, send_sems, recv_sems, local_sem):
        x, y, c = _place()
        me, sibling = (x, y, c), (x, y, 1 - c)
        chips = [(1 - x, y), (x, 1 - y), (1 - x, 1 - y)]

        def rows(p):
            return out_ref.at[pl.ds(_dev_index(p) * m_per, m_per), :]

        def copy(k, block, to, src=None):
            return pltpu.make_async_remote_copy(
                src_ref=rows(block) if src is None else src, dst_ref=rows(block),
                send_sem=send_sems.at[k], recv_sem=recv_sems.at[k], device_id=to, device_id_type=MESH)

        mine = pltpu.make_async_copy(x_ref, rows(me), local_sem)
        mine.start()
        first = [copy(0, me, sibling, src=x_ref)]
        first += [copy(1 + j, me, (*chip, c), src=x_ref) for j, chip in enumerate(chips)]
        for cp in first:
            cp.start()
        passed = [copy(4 + j, (*chip, c), sibling) for j, chip in enumerate(chips)]
        for j, chip in enumerate(chips):
            copy(1 + j, (*chip, c), me).wait_recv()
            passed[j].start()
        copy(0, sibling, me).wait_recv()
        for j, chip in enumerate(chips):
            copy(4 + j, (*chip, 1 - c), me).wait_recv()
        for cp in first + passed:
            cp.wait_send()
        mine.wait()

    return pl.pallas_call(
        body,
        name=name,
        out_shape=jax.ShapeDtypeStruct((N_DEV * m_per, n), shard.dtype),
        in_specs=[pl.BlockSpec(memory_space=pltpu.VMEM)],
        out_specs=pl.BlockSpec(memory_space=pltpu.VMEM),
        scratch_shapes=[pltpu.SemaphoreType.DMA((7,)), pltpu.SemaphoreType.DMA((7,)), pltpu.SemaphoreType.DMA],
        compiler_params=_cparams(),
    )(shard)


def _allgather_hbm(shards, name):
    n = len(shards)

    def body(*refs):
        ins, outs = refs[:n], refs[n : 2 * n]
        send_sems, recv_sems, local_sems = refs[2 * n :]
        x, y, c = _place()
        me, sibling = (x, y, c), (x, y, 1 - c)
        chips = [(1 - x, y), (x, 1 - y), (1 - x, 1 - y)]

        def copy(a, k, block, to, src=None):
            dst = outs[a].at[_dev_index(block)]
            return pltpu.make_async_remote_copy(
                src_ref=dst if src is None else src, dst_ref=dst,
                send_sem=send_sems.at[a * 7 + k], recv_sem=recv_sems.at[a * 7 + k], device_id=to, device_id_type=MESH)

        mine = [pltpu.make_async_copy(ins[a], outs[a].at[_dev_index(me)], local_sems.at[a]) for a in range(n)]
        for cp in mine:
            cp.start()
        first = []
        for a in range(n):
            first.append(copy(a, 0, me, sibling, src=ins[a]))
            first += [copy(a, 1 + j, me, (*chip, c), src=ins[a]) for j, chip in enumerate(chips)]
        for cp in first:
            cp.start()
        passed = []
        for j, chip in enumerate(chips):
            for a in range(n):
                copy(a, 1 + j, (*chip, c), me).wait_recv()
                cp = copy(a, 4 + j, (*chip, c), sibling)
                cp.start()
                passed.append(cp)
        for a in range(n):
            copy(a, 0, sibling, me).wait_recv()
        for j, chip in enumerate(chips):
            for a in range(n):
                copy(a, 4 + j, (*chip, 1 - c), me).wait_recv()
        for cp in first + passed:
            cp.wait_send()
        for cp in mine:
            cp.wait()

    any_spec = pl.BlockSpec(memory_space=pl.ANY)
    return pl.pallas_call(
        body,
        name=name,
        out_shape=[jax.ShapeDtypeStruct((N_DEV, *s.shape), s.dtype) for s in shards],
        in_specs=[any_spec] * n,
        out_specs=[any_spec] * n,
        scratch_shapes=[pltpu.SemaphoreType.DMA((7 * n,)), pltpu.SemaphoreType.DMA((7 * n,)),
                        pltpu.SemaphoreType.DMA((n,))],
        compiler_params=_cparams(),
    )(*shards)


HBM_SPEC = pl.BlockSpec(memory_space=pltpu.HBM)
SEM_SPEC = pl.BlockSpec(memory_space=pltpu.SEMAPHORE)
EFFECT = pltpu.SideEffectType.DATAFLOW_SIDE_EFFECTING


def _peers(x, y, c):
    return [(1 - x if k & 4 else x, 1 - y if k & 2 else y, 1 - c if k & 1 else c) for k in range(1, N_DEV)]


def _push_peers(mode, x, y, c):
    if mode == "all":
        return _peers(x, y, c)
    return [(x, y, 1 - c), (1 - x, y, c), (x, 1 - y, c), (1 - x, 1 - y, c)]


def _push_start(groups, sliced, name, after=(), modes=None):
    flat = [b for g in groups for b in g]
    n, ng = len(flat), len(groups)
    sizes = [len(g) for g in groups]
    modes = modes or ["all"] * ng
    fan = [len(_push_peers(m, 0, 0, 0)) for m in modes]
    per = 2 if sliced else 3
    lands = [lax.empty(b.shape if sliced else (N_DEV, *b.shape), b.dtype) for b in flat]

    def body(*refs):
        ins, lnd = refs[:n], refs[n : 2 * n]
        sems = refs[2 * n + len(after) : 2 * n + len(after) + per * ng]
        token = refs[-1]
        x, y, c = _place()
        me = _dev_index((x, y, c))
        if not sliced:
            first = 0
            for gi, size in enumerate(sizes):
                for j in range(first, first + size):
                    pltpu.make_async_copy(ins[j], lnd[j].at[me], sems[per * gi + 2].at[j - first]).start()
                first += size
        first = 0
        for gi, size in enumerate(sizes):
            for k, peer in enumerate(_push_peers(modes[gi], x, y, c)):
                for j in range(first, first + size):
                    sem = (j - first) * fan[gi] + k
                    pltpu.make_async_remote_copy(
                        src_ref=ins[j].at[_dev_index(peer)] if sliced else ins[j], dst_ref=lnd[j].at[me],
                        send_sem=sems[per * gi].at[sem], recv_sem=sems[per * gi + 1].at[sem],
                        device_id=peer, device_id_type=MESH).start()
            first += size
        token[...] = jnp.zeros_like(token)

    out_shape = []
    for size, width in zip(sizes, fan):
        out_shape += [pltpu.SemaphoreType.DMA((width * size,)), pltpu.SemaphoreType.DMA((width * size,))]
        out_shape += [] if sliced else [pltpu.SemaphoreType.DMA((size,))]
    out_shape += [pltpu.HBM(b.shape, b.dtype) for b in flat + lands]
    out_shape.append(jax.ShapeDtypeStruct((V7X_SUBLANES, V7X_LANES), F32))
    res = pl.pallas_call(
        body,
        name=name,
        out_shape=tuple(out_shape),
        in_specs=[HBM_SPEC] * (2 * n) + [ANY_SPEC] * len(after),
        out_specs=tuple([SEM_SPEC] * (per * ng) + [HBM_SPEC] * (2 * n) + [pl.BlockSpec(memory_space=pltpu.VMEM)]),
        input_output_aliases={i: per * ng + i for i in range(2 * n)},
        compiler_params=pltpu.CompilerParams(has_side_effects=EFFECT),
    )(*[pltpu.with_memory_space_constraint(b, pltpu.HBM) for b in flat + lands], *after)
    sems, thru, token = res[: per * ng], res[per * ng : per * ng + 2 * n], res[-1]
    out, first = [], 0
    for gi, size in enumerate(sizes):
        out.append((sems[per * gi], sems[per * gi + 1], list(thru[first : first + size]),
                    list(thru[n + first : n + first + size]), None if sliced else sems[per * gi + 2]))
        first += size
    return out, token


def _push_wait(started, sliced, after, name, mode="all"):
    send_sems, recv_sems, bufs, lands, own_sems = started
    n = len(bufs)
    fan = len(_push_peers(mode, 0, 0, 0))
    own = [] if own_sems is None else [own_sems]

    def body(*refs):
        ins, lnd = refs[:n], refs[n : 2 * n]
        send_ref, recv_ref = refs[2 * n], refs[2 * n + 1]
        x, y, c = _place()
        for k, peer in enumerate(_push_peers(mode, x, y, c)):
            for j in range(n):
                cp = pltpu.make_async_remote_copy(
                    src_ref=ins[j].at[_dev_index(peer)] if sliced else ins[j], dst_ref=lnd[j].at[_dev_index(peer)],
                    send_sem=send_ref.at[j * fan + k], recv_sem=recv_ref.at[j * fan + k],
                    device_id=peer, device_id_type=MESH)
                cp.wait_send()
                cp.wait_recv()
        if own:
            for j in range(n):
                pltpu.make_async_copy(ins[j], lnd[j].at[_dev_index((x, y, c))], refs[2 * n + 2].at[j]).wait()

    res = pl.pallas_call(
        body,
        name=name,
        out_shape=tuple(pltpu.HBM(b.shape, b.dtype) for b in bufs + lands),
        in_specs=[HBM_SPEC] * (2 * n) + [SEM_SPEC] * (2 + len(own)) + [pl.BlockSpec(memory_space=pl.ANY)],
        out_specs=tuple([HBM_SPEC] * (2 * n)),
        input_output_aliases={i: i for i in range(2 * n)},
        compiler_params=pltpu.CompilerParams(has_side_effects=EFFECT),
    )(*bufs, *lands, send_sems, recv_sems, *own, after)
    return list(res[:n]), list(res[n:])


def _forward_copies(lnd, send_ref, recv_ref, incoming):
    x, y, c = _place()
    copies = []
    for k, chip in enumerate([(1 - x, y), (x, 1 - y), (1 - x, 1 - y)]):
        mine, theirs = _dev_index((*chip, c)), _dev_index((*chip, 1 - c))
        for j, ref in enumerate(lnd):
            copies.append(pltpu.make_async_remote_copy(
                src_ref=ref.at[mine], dst_ref=ref.at[theirs if incoming else mine],
                send_sem=send_ref.at[j * 3 + k], recv_sem=recv_ref.at[j * 3 + k],
                device_id=(x, y, 1 - c), device_id_type=MESH))
    return copies


def _forward_start(lands, name):
    n = len(lands)

    def body(*refs):
        for cp in _forward_copies(refs[:n], refs[n], refs[n + 1], False):
            cp.start()

    res = pl.pallas_call(
        body,
        name=name,
        out_shape=(pltpu.SemaphoreType.DMA((3 * n,)), pltpu.SemaphoreType.DMA((3 * n,)),
                   *[pltpu.HBM(b.shape, b.dtype) for b in lands]),
        in_specs=[HBM_SPEC] * n,
        out_specs=(SEM_SPEC, SEM_SPEC, *[HBM_SPEC] * n),
        input_output_aliases={i: 2 + i for i in range(n)},
        compiler_params=pltpu.CompilerParams(has_side_effects=EFFECT),
    )(*[pltpu.with_memory_space_constraint(b, pltpu.HBM) for b in lands])
    return res[0], res[1], list(res[2:])


def _forward_wait(started, after, name):
    send_sems, recv_sems, lands = started
    n = len(lands)

    def body(*refs):
        for cp in _forward_copies(refs[:n], refs[n], refs[n + 1], True):
            cp.wait_send()
            cp.wait_recv()

    res = pl.pallas_call(
        body,
        name=name,
        out_shape=tuple(pltpu.HBM(b.shape, b.dtype) for b in lands),
        in_specs=[HBM_SPEC] * n + [SEM_SPEC, SEM_SPEC, pl.BlockSpec(memory_space=pl.ANY)],
        out_specs=tuple([HBM_SPEC] * n),
        input_output_aliases={i: i for i in range(n)},
        compiler_params=pltpu.CompilerParams(has_side_effects=EFFECT),
    )(*lands, send_sems, recv_sems, after)
    return list(res)


def _cols_full(g):
    return jnp.transpose(g, (1, 0, 2)).reshape(g.shape[1], -1)


def _rows_full(g):
    return g.reshape(-1, g.shape[2])


def _cols_parts(full, n=N_DEV):
    r = full.shape[0]
    return jnp.transpose(full.reshape(r, n, -1), (1, 0, 2)).astype(BF16)


def _rows_parts(full):
    return full.reshape(N_DEV, -1, full.shape[1]).astype(BF16)


def _pad_rows(v, rows):
    flat = v.reshape(-1)
    return jnp.pad(flat, (0, rows * D_MODEL - flat.shape[0])).reshape(rows, D_MODEL)


def _my_cols(full, me, width):
    return lax.dynamic_slice_in_dim(full, me * width, width, axis=full.ndim - 1)


def kernel(x, c, w_ada, b_ada, norm_pre, norm_post, ffn1_w_gu, ffn1_w_down, w_in, rel_bias, conv_w, conv_b, lru_wa, lru_ba, lru_wx, lru_bx, lru_lambda, w_att_o, w_rec_o, w_out, ffn2_w_gu, ffn2_w_down, loss_target, m_w_ada, m_b_ada, m_norm_pre, m_norm_post, m_ffn1_w_gu, m_ffn1_w_down, m_w_in, m_rel_bias, m_conv_w, m_conv_b, m_lru_wa, m_lru_ba, m_lru_wx, m_lru_bx, m_lru_lambda, m_w_att_o, m_w_rec_o, m_w_out, m_ffn2_w_gu, m_ffn2_w_down, v_w_ada, v_b_ada, v_norm_pre, v_norm_post, v_ffn1_w_gu, v_ffn1_w_down, v_w_in, v_rel_bias, v_conv_w, v_conv_b, v_lru_wa, v_lru_ba, v_lru_wx, v_lru_bx, v_lru_lambda, v_w_att_o, v_w_rec_o, v_w_out, v_ffn2_w_gu, v_ffn2_w_down):
    weights = dict(w_ada=w_ada, b_ada=b_ada, norm_pre=norm_pre, norm_post=norm_post, ffn1_w_gu=ffn1_w_gu,
                   ffn1_w_down=ffn1_w_down, w_in=w_in, rel_bias=rel_bias, conv_w=conv_w, conv_b=conv_b,
                   lru_wa=lru_wa, lru_ba=lru_ba, lru_wx=lru_wx, lru_bx=lru_bx, lru_lambda=lru_lambda,
                   w_att_o=w_att_o, w_rec_o=w_rec_o, w_out=w_out, ffn2_w_gu=ffn2_w_gu, ffn2_w_down=ffn2_w_down)
    mom1 = dict(w_ada=m_w_ada, b_ada=m_b_ada, norm_pre=m_norm_pre, norm_post=m_norm_post, ffn1_w_gu=m_ffn1_w_gu,
                ffn1_w_down=m_ffn1_w_down, w_in=m_w_in, rel_bias=m_rel_bias, conv_w=m_conv_w, conv_b=m_conv_b,
                lru_wa=m_lru_wa, lru_ba=m_lru_ba, lru_wx=m_lru_wx, lru_bx=m_lru_bx, lru_lambda=m_lru_lambda,
                w_att_o=m_w_att_o, w_rec_o=m_w_rec_o, w_out=m_w_out, ffn2_w_gu=m_ffn2_w_gu, ffn2_w_down=m_ffn2_w_down)
    mom2 = dict(w_ada=v_w_ada, b_ada=v_b_ada, norm_pre=v_norm_pre, norm_post=v_norm_post, ffn1_w_gu=v_ffn1_w_gu,
                ffn1_w_down=v_ffn1_w_down, w_in=v_w_in, rel_bias=v_rel_bias, conv_w=v_conv_w, conv_b=v_conv_b,
                lru_wa=v_lru_wa, lru_ba=v_lru_ba, lru_wx=v_lru_wx, lru_bx=v_lru_bx, lru_lambda=v_lru_lambda,
                w_att_o=v_w_att_o, w_rec_o=v_w_rec_o, w_out=v_w_out, ffn2_w_gu=v_ffn2_w_gu, ffn2_w_down=v_ffn2_w_down)
    order = list(weights)
    big = ["ffn1_w_gu", "ffn1_w_down", "w_in", "w_att_o", "w_rec_o", "w_out", "ffn2_w_gu", "ffn2_w_down"]
    col_sharded = {"ffn1_w_gu", "w_in", "w_att_o", "ffn2_w_gu"}
    small = ["b_ada", "norm_pre", "norm_post", "rel_bias", "conv_w", "conv_b", "lru_wa", "lru_ba", "lru_wx",
             "lru_bx", "lru_lambda"]

    xi, yi, ci = _place()
    me = _dev_index((xi, yi, ci))
    x0 = x[0]
    target = loss_target[0]
    fuse_tm = min(FUSE_TM, x0.shape[0])

    transposed = {"ffn1_w_gu", "w_in", "ffn2_w_gu"}
    local = lambda n, arr: jnp.transpose(arr[0]) if n in transposed else arr[0]
    shards = {n: local(n, weights[n]).astype(BF16) for n in big}
    full_of = lambda n, g: _cols_full(g) if n == "w_att_o" else _rows_full(g)

    pack = jnp.concatenate([c.reshape(-1), norm_pre.reshape(-1), norm_post.reshape(-1), conv_w.reshape(-1)])
    pack = jnp.pad(pack, (0, 3072 - pack.shape[0])).reshape(8, 384)
    got = _allgather_vmem(pack, "gather_small_inputs").reshape(N_DEV, 3072)
    c_all = got[:, :1024]
    unshard = lambda blk, rows: jnp.transpose(blk.reshape(N_DEV, rows, 128), (1, 0, 2)).reshape(rows, D_MODEL)
    g_pre = unshard(got[:, 1024:1408], 3)
    g_post = unshard(got[:, 1408:1792], 3)
    conv_taps = unshard(got[:, 1792:2304], 4)
    conv_w8 = jnp.concatenate([conv_taps, jnp.zeros((4, LRU_WIDTH), F32)], axis=0)

    mod_cols = _ada_fwd(c_all, w_ada[0], "ada_fwd")
    mod_all = _allgather_vmem(mod_cols, "gather_mod").reshape(N_DEV, N_DEV, 1152)
    mod = lax.dynamic_index_in_dim(mod_all, me, axis=1, keepdims=False).reshape(1, -1) + b_ada
    mod = mod.reshape(3, 3, 1, D_MODEL)

    w_slab = _slab_weights(lru_wa[0], lru_wx[0])
    bias = _bias_tile(rel_bias[0], "bias_tile")

    res_w = (0.5, 1.0, 0.5)
    row = lambda v: v.reshape(1, -1)

    (w1_gu,) = _allgather_hbm([shards["ffn1_w_gu"]], "gather_ffn1_w_gu")
    weight_groups = [["ffn1_w_down"], ["w_in"], ["w_att_o", "w_rec_o", "w_out"], ["ffn2_w_gu", "ffn2_w_down"]]
    weight_modes = ["all", "chip", "all", "all"]
    weights_started, started = _push_start([[shards[n] for n in g] for g in weight_groups], False,
                                           "gather_weights_start", after=(mod, w1_gu), modes=weight_modes)
    full = {"ffn1_w_gu": _rows_full(w1_gu)}

    def gathered_group(gi, after):
        sent, lands = _push_wait(weights_started[gi], False, after, f"gather_weights_wait{gi}", mode=weight_modes[gi])
        if weight_modes[gi] == "chip":
            lands = _forward_wait(_forward_start(lands, f"gather_weights_forward{gi}"), sent[0],
                                  f"gather_weights_forward_wait{gi}")
        for n, land in zip(weight_groups[gi], lands):
            full[n] = full_of(n, land)

    def ffn_fwd(xin, k, gi, tag, deps=(), target=None):
        h, a, g, u = _pre_up(xin, row(g_pre[k]), mod[k, 0], mod[k, 1], full[f"{tag}_w_gu"], f"{tag}_up", deps=deps)
        if f"{tag}_w_down" not in full:
            gathered_group(gi, a)
        f, *out = _matmul_post(a, full[f"{tag}_w_down"], xin, row(g_post[k]), mod[k, 2], res_w[k], f"{tag}_down",
                               target=target)
        return (out[0] if target is None else out), (h, g, u, a, f)

    x1, saved1 = ffn_fwd(x0, 0, 0, "ffn1", deps=(started,))

    gathered_group(1, x1)
    h2, proj = _pre_matmul(x1, row(g_pre[1]), mod[1, 0], mod[1, 1], full["w_in"], "mix_in",
                           b_shift=3 * ATT_WIDTH // 512)
    att_o = _attn_fwd(proj, bias, "attn_fwd")
    gathered_group(2, att_o)
    xc, pre, a_t, u_t = _lru_front(proj, conv_w8, conv_b, w_slab, lru_ba, lru_bx, lru_lambda, "lru_front")
    hs, h_prev, rec_in = _scan_fwd(a_t, u_t, proj, "lru_scan")
    att = _matmul(att_o, full["w_att_o"], "nn", F32, "att_out")
    rec = _matmul(rec_in, full["w_rec_o"], "nn", F32, "rec_out")
    merged, f2, x2 = _merge_matmul_post(att, rec, proj, full["w_out"], x1, row(g_post[1]), mod[1, 2], res_w[1],
                                        "mix_out")

    gathered_group(3, x2)
    (dy, sq), saved3 = ffn_fwd(x2, 2, 2, "ffn2", target=target)
    loss = lax.psum(0.5 * jnp.sum(sq) / D_MODEL, ("x", "y", "c"))

    grads = {}
    norm_sums = [None] * 6

    pending = []

    def exchange_start(names, tag, after=()):
        send = [(_cols_parts if n == "w_att_o" else _rows_parts)(grads[n]) for n in names]
        (group,), token = _push_start([send], True, f"exchange_{tag}_start", after=after)
        pending.append((names, send, group, tag))
        return token

    def exchange_finish(names, send, group, tag, after):
        sent, lands = _push_wait(group, True, after, f"exchange_{tag}_wait")
        res = None
        for n, land, mine in zip(names, lands, sent):
            res = _adamw_parts(land, mine, me, local(n, weights[n]), local(n, mom1[n]), local(n, mom2[n]),
                               f"adamw_{n}")
            back = (lambda r: jnp.transpose(r)) if n in transposed else (lambda r: r)
            out_g[n], out_d[n], out_m[n], out_v[n] = [back(r).reshape(weights[n].shape) for r in res]
        return res[0]

    out_g, out_d, out_m, out_v = {}, {}, {}, {}

    def ffn_bwd(xin, k, saved, dout, tag):
        h, g, u, a, f = saved
        w_gu, w_down = f"{tag}_w_gu", f"{tag}_w_down"
        df, dgu, norm_sums[2 * k + 1] = _post_bwd_up_bwd(f, dout, row(g_post[k]), mod[k, 2], res_w[k], full[w_down],
                                                          g, u, f"{tag}_up_bwd")
        grads[w_down] = _matmul(a, df, "tn", BF16, f"{tag}_dw_down", tm=1408, tn=1024, tk=DW_TK)
        started = exchange_start([w_down], w_down)
        grads[w_gu] = _dw_gu(dgu, h, f"{tag}_dw_gu", deps=(started,))
        started = exchange_start([w_gu], w_gu)
        halves = [(dgu, (None, fuse_tm, D_FF), lambda i, half=half: (half, i, 0), (half * D_FF, (half + 1) * D_FF))
                  for half in range(2)]
        dx, norm_sums[2 * k] = _matmul_pre_bwd(halves, full[w_gu], xin, dout, row(g_pre[k]),
                                                                mod[k, 0], mod[k, 1], f"{tag}_dh", deps=(started,))
        return dx

    dx2 = ffn_bwd(x2, 2, saved3, dy, "ffn2")

    df2, datt, drec, dg_att, dg_rec, norm_sums[3] = _post_bwd_merge_bwd(
        f2, dx2, row(g_post[1]), mod[1, 2], res_w[1], full["w_out"], att, rec, proj, "mix_dmerged")
    grads["w_out"] = _matmul(merged, df2, "tn", BF16, "mix_dw_out", tm=1024, tn=1024, tk=DW_TK)
    datt_o = _matmul(datt, full["w_att_o"], "nt", BF16, "att_out_bwd")
    grads["w_att_o"] = _matmul(att_o, datt, "tn", BF16, "dw_att_o", tm=512, tn=1024, tk=DW_TK)
    grads["w_rec_o"] = _matmul(rec_in, drec, "tn", BF16, "dw_rec_o", tm=1024, tn=1024, tk=DW_TK)
    started = exchange_start(["w_out", "w_att_o", "w_rec_o"], "mix_out")
    dhs, dyr = _matmul_recin_bwd(drec, full["w_rec_o"], hs, proj, "rec_out_bwd", deps=(started,))
    g_t = _scan_bwd(a_t, dhs, "lru_scan_bwd")
    dpre, dxc, lru_sums = _lru_back(pre, xc, w_slab, lru_ba, lru_bx, lru_lambda, g_t, h_prev, "lru_back")
    dxr, conv_sums = _conv_bwd(proj, conv_w8, dxc, "conv_bwd")
    dq, dk, dv, dbias = _attn_bwd(proj, bias, datt_o, "attn_bwd")
    dproj = jnp.concatenate([dq, dk, dv, dxr, dyr, dg_att, dg_rec], axis=1)
    grads["w_in"] = _matmul(dproj, h2, "tn", BF16, "mix_dw_in", tm=1408, tn=1024, tk=DW_TK)
    pack_mix = jnp.concatenate([conv_sums, lru_sums, _pad_rows(_bias_grad(dbias, "bias_grad"), V7X_SUBLANES),
                                _lru_dw(xc, dpre, "lru_dw").reshape(128, D_MODEL)], axis=0)
    (mix_started,), started = _push_start([[pack_mix]], False, "small_grads_mix_start")
    started = exchange_start(["w_in"], "w_in", after=(started,))
    whole = [(dproj, (fuse_tm, PROJ_WIDTH), lambda i: (i, 0), (0, PROJ_WIDTH))]
    dx1, norm_sums[2] = _matmul_pre_bwd(whole, full["w_in"], x1, dx2, row(g_pre[1]), mod[1, 0],
                                                             mod[1, 1], "mix_dh", deps=(started,))

    dx0 = ffn_bwd(x0, 0, saved1, dx1, "ffn1")

    pack_norm = jnp.concatenate(norm_sums, axis=0)
    (norm_started,), _ = _push_start([[pack_norm]], False, "small_grads_norm_start")

    def summed(started, pack, after, tag):
        _, (parts,) = _push_wait(started, False, after, f"small_grads_{tag}_wait")
        return parts, _sum_parts(parts, f"small_grads_{tag}_sum")

    done = dx0
    last = pending[-1:]
    for names, send, group, tag in pending[:-1]:
        done = exchange_finish(names, send, group, tag, done)

    _, total = summed(mix_started, pack_mix, done, "mix")
    grads["conv_w"] = _my_cols(total[0:4], me, 128)
    grads["conv_b"] = total[4:5]
    grads["lru_ba"] = total[8:9]
    grads["lru_bx"] = total[9:10]
    grads["lru_lambda"] = total[10:11]
    grads["rel_bias"] = total[16:19].reshape(-1)[: ATT_HEADS * (2 * MAX_REL + 1)].reshape(ATT_HEADS, -1)
    grads["lru_wa"] = total[24:88].reshape(LRU_BLOCKS, LRU_BLOCK, LRU_BLOCK)
    grads["lru_wx"] = total[88:152].reshape(LRU_BLOCKS, LRU_BLOCK, LRU_BLOCK)
    parts, total = summed(norm_started, pack_norm, total, "norm")
    by_sandwich = lambda v: v.reshape(*v.shape[:-2], 3, 2 * V7X_SUBLANES, D_MODEL)
    dmod_of = lambda v: jnp.concatenate([by_sandwich(v)[..., 1:3, :], by_sandwich(v)[..., 9:10, :]], axis=-2)
    grads["b_ada"] = dmod_of(total).reshape(1, -1)
    grads["norm_pre"] = _my_cols(by_sandwich(total)[:, 0, :], me, 128)
    grads["norm_post"] = _my_cols(by_sandwich(total)[:, V7X_SUBLANES, :], me, 128)
    dmod_all = dmod_of(parts).reshape(N_DEV, 9 * D_MODEL)
    grads["w_ada"] = _ada_bwd(c_all, _my_cols(dmod_all, me, 1152), "ada_bwd")

    res = _adamw(grads["w_ada"], w_ada[0], m_w_ada[0], v_w_ada[0], "adamw_w_ada")
    out_g["w_ada"], out_d["w_ada"], out_m["w_ada"], out_v["w_ada"] = [r.reshape(w_ada.shape) for r in res]

    sizes = [int(np.prod(weights[n].shape)) for n in small]
    tot = sum(sizes)
    rows_small = -(-tot // (16 * D_MODEL)) * 16
    flat = lambda arrs: jnp.pad(jnp.concatenate([a.reshape(-1) for a in arrs]),
                                (0, rows_small * D_MODEL - tot)).reshape(rows_small, D_MODEL)
    res = _adamw(flat([grads[n] for n in small]), flat([weights[n] for n in small]),
                 flat([mom1[n] for n in small]), flat([mom2[n] for n in small]), "adamw_small", rows=rows_small)
    offs = np.cumsum([0] + sizes)
    for dst, r in zip((out_g, out_d, out_m, out_v), res):
        rf = r.reshape(-1)
        for i, n in enumerate(small):
            dst[n] = rf[offs[i] : offs[i + 1]].reshape(weights[n].shape)

    done = res[0]
    for names, send, group, tag in last:
        done = exchange_finish(names, send, group, tag, done)

    return (loss, dx0[None], *[out_g[n] for n in order], *[out_d[n] for n in order],
            *[out_m[n] for n in order], *[out_v[n] for n in order])
```

```python
import jax
import jax.numpy as jnp
import numpy as np
from jax import lax
from jax.experimental import pallas as pl
from jax.experimental.pallas import tpu as pltpu

D_MODEL = 1024
D_FF = 2816
ATT_HEADS = 8
ATT_HEAD_DIM = 64
ATT_WIDTH = 512
CHUNK = 64
LEFT_CHUNKS = 8
MAX_REL = 128
LRU_WIDTH = 1024
LRU_BLOCKS = 16
LRU_BLOCK = 64
LRU_C = 8.0
EPS = 1e-6
PROJ_WIDTH = 5632
N_DEV = 8

ADAM_LR = 0.001
ADAM_B1 = 0.9
ADAM_B2 = 0.999
ADAM_EPS = 1e-08
ADAM_WD = 0.01
ADAM_STEP = 10

V7X_LANES = 128
V7X_SUBLANES = 8
V7X_VMEM_BYTES = 64 * 1024 * 1024
VMEM_LIMIT = V7X_VMEM_BYTES - 8 * 1024 * 1024

ATT_TQ = 256
NEG = -1e30
BF16 = jnp.bfloat16
F32 = jnp.float32
MESH = pl.DeviceIdType.MESH

OFF_Q = 4 * LRU_WIDTH
OFF_K = OFF_Q + ATT_WIDTH
OFF_V = OFF_K + ATT_WIDTH


def _cparams(**kw):
    return pltpu.CompilerParams(vmem_limit_bytes=VMEM_LIMIT, **kw)


def _pick(n, target, unit=V7X_LANES):
    best = None
    for t in range(unit, min(n, target) + 1, unit):
        if n % t == 0:
            best = t
    return n if best is None else best


_DIMS = {
    "nn": (((1,), (0,)), ((), ())),
    "nt": (((1,), (1,)), ((), ())),
    "tn": (((0,), (0,)), ((), ())),
}


ANY_SPEC = pl.BlockSpec(memory_space=pl.ANY)


def _matmul(a, b, mode, out_dtype, name, tm=1024, tn=512, tk=1408, deps=(), b_shift=0):
    n_deps = len(deps)
    if mode == "nn":
        (m, k), (k2, n) = a.shape, b.shape
    elif mode == "nt":
        (m, k), (n, k2) = a.shape, b.shape
    else:
        (k, m), (k2, n) = a.shape, b.shape
    assert k == k2, (a.shape, b.shape, mode)
    tm, tn, tk = _pick(m, tm), _pick(n, tn), _pick(k, tk)
    nk = k // tk
    dims = _DIMS[mode]

    def body(a_ref, b_ref, *rest):
        o_ref, scratch = rest[n_deps], rest[n_deps + 1 :]
        p = lax.dot_general(a_ref[...], b_ref[...], dims, preferred_element_type=F32)
        if nk == 1:
            o_ref[...] = p.astype(o_ref.dtype)
        else:
            acc = scratch[0]
            kk = pl.program_id(2)

            @pl.when(kk == 0)
            def _():
                acc[...] = p

            @pl.when(kk > 0)
            def _():
                acc[...] += p

            @pl.when(kk == nk - 1)
            def _():
                o_ref[...] = acc[...].astype(o_ref.dtype)

    if mode == "nn":
        a_spec = pl.BlockSpec((tm, tk), lambda i, j, kk: (i, kk))
        b_spec = pl.BlockSpec((tk, tn), lambda i, j, kk: (kk, j))
    elif mode == "nt":
        a_spec = pl.BlockSpec((tm, tk), lambda i, j, kk: (i, kk))
        b_spec = pl.BlockSpec((tn, tk), lambda i, j, kk: ((j + b_shift) % (n // tn), kk))
    else:
        a_spec = pl.BlockSpec((tk, tm), lambda i, j, kk: (kk, i))
        b_spec = pl.BlockSpec((tk, tn), lambda i, j, kk: (kk, j))
    return pl.pallas_call(
        body,
        name=name,
        grid=(m // tm, n // tn, nk),
        in_specs=[a_spec, b_spec] + [ANY_SPEC] * n_deps,
        out_specs=pl.BlockSpec((tm, tn), lambda i, j, kk: (i, j)),
        out_shape=jax.ShapeDtypeStruct((m, n), out_dtype),
        scratch_shapes=[pltpu.VMEM((tm, tn), F32)] if nk > 1 else [],
        compiler_params=_cparams(dimension_semantics=("parallel", "parallel", "arbitrary")),
    )(a, b, *deps)


def _rowwise(fn, name, params, tiles, outs, accs=(), ts=256, with_index=False, deps=()):
    norm = []
    for t in tiles:
        if not isinstance(t, tuple):
            t = (t, t.shape[1], 0)
        norm.append(t if len(t) == 4 else (*t, None))
    s = norm[0][0].shape[0]
    ts = min(ts, s)
    assert s % ts == 0 and ts % V7X_SUBLANES == 0
    steps = s // ts
    halo_blocks = ts // V7X_SUBLANES
    n_p, n_t, n_o = len(params), len(norm), len(outs)

    def body(*refs):
        i = pl.program_id(0)
        vals = [r[...] for r in refs[: n_p + n_t]]
        res = fn(i, steps, *vals) if with_index else fn(*vals)
        if not isinstance(res, (tuple, list)):
            res = (res,)
        first_out = n_p + n_t + len(deps)
        o_refs = refs[first_out : first_out + n_o]
        a_refs = refs[first_out + n_o :]
        for r, v in zip(o_refs, res[:n_o]):
            r[...] = v.astype(r.dtype)
        for r, v in zip(a_refs, res[n_o:]):
            _accumulate(r, v, i)

    in_specs = [pl.BlockSpec(p.shape, lambda i: (0, 0)) for p in params]
    for arr, w, cb, halo in norm:
        if halo is None:
            in_specs.append(pl.BlockSpec((ts, w), lambda i, cb=cb: (i, cb)))
        elif halo == "prev":
            in_specs.append(
                pl.BlockSpec((V7X_SUBLANES, w), lambda i, cb=cb: (jnp.maximum(i * halo_blocks - 1, 0), cb))
            )
        else:
            last = s // V7X_SUBLANES - 1
            in_specs.append(
                pl.BlockSpec((V7X_SUBLANES, w), lambda i, cb=cb: (jnp.minimum((i + 1) * halo_blocks, last), cb))
            )
    in_specs += [ANY_SPEC] * len(deps)
    out_specs = [pl.BlockSpec((ts, w), lambda i: (i, 0)) for w, _ in outs]
    out_specs += [pl.BlockSpec(shape, lambda i: (0, 0)) for shape in accs]
    out_shape = [jax.ShapeDtypeStruct((s, w), dt) for w, dt in outs]
    out_shape += [jax.ShapeDtypeStruct(shape, F32) for shape in accs]
    res = pl.pallas_call(
        body,
        name=name,
        grid=(steps,),
        in_specs=in_specs,
        out_specs=out_specs,
        out_shape=out_shape,
        compiler_params=_cparams(dimension_semantics=("arbitrary",)),
    )(*params, *[t[0] for t in norm], *deps)
    return res


def _accumulate(ref, val, step):
    @pl.when(step == 0)
    def _():
        ref[...] = val

    @pl.when(step > 0)
    def _():
        ref[...] += val


def _sigmoid(z):
    return jax.nn.sigmoid(z)


def _silu(z):
    return z * _sigmoid(z)


def _gelu(z):
    return 0.5 * z * (1.0 + jnp.tanh(0.7978845608028654 * (z + 0.044715 * (z * z * z))))


def _pre_fn(g, shift, scale, x):
    r = lax.rsqrt(jnp.mean(x * x, axis=-1, keepdims=True) + EPS)
    return ((x * r) * g) * (1.0 + scale) + shift


def _post_fn(res_w, g, gate, f, x):
    r = lax.rsqrt(jnp.mean(f * f, axis=-1, keepdims=True) + EPS)
    return x + (res_w * gate) * ((f * r) * g)


def _gates_fn(ba, bx, lam, pre, xc):
    ra = _sigmoid(pre[:, :LRU_WIDTH] + ba)
    ia = _sigmoid(pre[:, LRU_WIDTH:] + bx)
    softplus = jnp.maximum(-lam, 0.0) + jnp.log1p(jnp.exp(-jnp.abs(lam)))
    log_a = (-LRU_C) * ra * softplus
    a = jnp.exp(log_a)
    mult = jnp.sqrt(-jnp.tanh(log_a) * (a * a + 1.0))
    return a, mult * (ia * xc)


def _recin_fn(hs, yr):
    return hs * _gelu(yr)


def _merge_fn(att, rec, g_att, g_rec):
    return _sigmoid(g_att) * att + _sigmoid(g_rec) * rec


def _rowsum(v):
    return jnp.sum(v, axis=0, keepdims=True)


FFN_TM = 512
FFN_TF = 1408


def _glu_fn(g, u):
    return _silu(g) * u


FUSE_TM = 256
DW_TK = 4096
ROW_SPEC2 = pl.BlockSpec((1, D_MODEL), lambda i, j: (0, 0))
ROW_SPEC1 = pl.BlockSpec((1, D_MODEL), lambda i: (0, 0))
SUMS_SPEC1 = pl.BlockSpec((V7X_SUBLANES, D_MODEL), lambda i: (0, 0))
SUMS_SPEC2 = pl.BlockSpec((V7X_SUBLANES, D_MODEL), lambda i, j: (0, 0))
SUMS_SHAPE = jax.ShapeDtypeStruct((V7X_SUBLANES, D_MODEL), F32)


def _sum_rows(*rows):
    pad = jnp.zeros((V7X_SUBLANES - len(rows), rows[0].shape[1]), F32)
    return jnp.concatenate([*rows, pad], axis=0)


def _pre_up(x, g, shift, scale, w_gu_t, name, deps=()):
    s = x.shape[0]
    tm = min(FFN_TM, s)
    nf = D_FF // FFN_TF
    nd = len(deps)

    def body(x_ref, g_ref, sh_ref, sc_ref, wg_ref, wu_ref, *rest):
        h_ref, a_ref, gg_ref, u_ref, h_s = rest[nd:]

        @pl.when(pl.program_id(1) == 0)
        def _():
            h = _pre_fn(g_ref[...], sh_ref[...], sc_ref[...], x_ref[...]).astype(BF16)
            h_s[...] = h
            h_ref[...] = h

        hv = h_s[...]
        gv = lax.dot_general(hv, wg_ref[...], _DIMS["nt"], preferred_element_type=F32)
        uv = lax.dot_general(hv, wu_ref[...], _DIMS["nt"], preferred_element_type=F32)
        a_ref[...] = _glu_fn(gv, uv).astype(a_ref.dtype)
        gg_ref[...] = gv.astype(gg_ref.dtype)
        u_ref[...] = uv.astype(u_ref.dtype)

    rows = pl.BlockSpec((tm, D_MODEL), lambda i, j: (i, 0))
    out = pl.BlockSpec((tm, FFN_TF), lambda i, j: (i, j))
    return pl.pallas_call(
        body,
        name=name,
        grid=(s // tm, nf),
        in_specs=[rows, ROW_SPEC2, ROW_SPEC2, ROW_SPEC2,
                  pl.BlockSpec((FFN_TF, D_MODEL), lambda i, j: (j, 0)),
                  pl.BlockSpec((FFN_TF, D_MODEL), lambda i, j: (nf + j, 0))] + [ANY_SPEC] * nd,
        out_specs=[rows, out, out, out],
        out_shape=[jax.ShapeDtypeStruct((s, D_MODEL), BF16)] + [jax.ShapeDtypeStruct((s, D_FF), BF16)] * 3,
        scratch_shapes=[pltpu.VMEM((tm, D_MODEL), BF16)],
        compiler_params=_cparams(dimension_semantics=("parallel", "arbitrary")),
    )(x, g, shift, scale, w_gu_t, w_gu_t, *deps)


def _pre_matmul(x, g, shift, scale, w_t, name, b_shift=0, tn=512):
    s = x.shape[0]
    n = w_t.shape[0]
    tm = min(2 * FFN_TM, s)

    def body(x_ref, g_ref, sh_ref, sc_ref, w_ref, h_ref, o_ref, h_s):
        @pl.when(pl.program_id(1) == 0)
        def _():
            h = _pre_fn(g_ref[...], sh_ref[...], sc_ref[...], x_ref[...]).astype(BF16)
            h_s[...] = h
            h_ref[...] = h

        o_ref[...] = lax.dot_general(h_s[...], w_ref[...], _DIMS["nt"], preferred_element_type=F32)

    rows = pl.BlockSpec((tm, D_MODEL), lambda i, j: (i, 0))
    return pl.pallas_call(
        body,
        name=name,
        grid=(s // tm, n // tn),
        in_specs=[rows, ROW_SPEC2, ROW_SPEC2, ROW_SPEC2,
                  pl.BlockSpec((tn, D_MODEL), lambda i, j: ((j + b_shift) % (n // tn), 0))],
        out_specs=[rows, pl.BlockSpec((tm, tn), lambda i, j: (i, j))],
        out_shape=[jax.ShapeDtypeStruct((s, D_MODEL), BF16), jax.ShapeDtypeStruct((s, n), F32)],
        scratch_shapes=[pltpu.VMEM((tm, D_MODEL), BF16)],
        compiler_params=_cparams(dimension_semantics=("parallel", "arbitrary")),
    )(x, g, shift, scale, w_t)


def _matmul_post(a, w, x, g_post, gate, res_w, name, target=None):
    s, k = a.shape
    tm = min(FFN_TM, s)
    extra = [] if target is None else [target]

    def body(a_ref, w_ref, x_ref, g_ref, gate_ref, *rest):
        f = jnp.dot(a_ref[...], w_ref[...], preferred_element_type=F32)
        y = _post_fn(res_w, g_ref[...], gate_ref[...], f, x_ref[...])
        if target is None:
            f_ref, y_ref = rest
            y_ref[...] = y
        else:
            t_ref, f_ref, dy_ref, sq_ref = rest
            diff = y - t_ref[...]
            dy_ref[...] = diff * (1.0 / D_MODEL)
            _accumulate(sq_ref, _rowsum(diff * diff), pl.program_id(0))
        f_ref[...] = f

    rows = pl.BlockSpec((tm, D_MODEL), lambda i: (i, 0))
    out_specs, out_shape = [rows, rows], [jax.ShapeDtypeStruct((s, D_MODEL), F32)] * 2
    if target is not None:
        out_specs.append(ROW_SPEC1)
        out_shape.append(jax.ShapeDtypeStruct((1, D_MODEL), F32))
    return pl.pallas_call(
        body,
        name=name,
        grid=(s // tm,),
        in_specs=[pl.BlockSpec((tm, k), lambda i: (i, 0)), pl.BlockSpec((k, D_MODEL), lambda i: (0, 0)), rows,
                  ROW_SPEC1, ROW_SPEC1] + [rows] * len(extra),
        out_specs=out_specs,
        out_shape=out_shape,
        compiler_params=_cparams(dimension_semantics=("arbitrary",)),
    )(a, w, x, g_post, gate, *extra)


def _merge_matmul_post(att, rec, proj, w, x, g_post, gate, res_w, name):
    s = att.shape[0]
    tm = min(FUSE_TM, s)

    def body(att_ref, rec_ref, ga_ref, gr_ref, w_ref, x_ref, g_ref, gate_ref, m_ref, f_ref, y_ref):
        merged = _merge_fn(att_ref[...], rec_ref[...], ga_ref[...], gr_ref[...]).astype(BF16)
        m_ref[...] = merged
        f = jnp.dot(merged, w_ref[...], preferred_element_type=F32)
        f_ref[...] = f
        y_ref[...] = _post_fn(res_w, g_ref[...], gate_ref[...], f, x_ref[...])

    rows = pl.BlockSpec((tm, D_MODEL), lambda i: (i, 0))
    return pl.pallas_call(
        body,
        name=name,
        grid=(s // tm,),
        in_specs=[rows, rows, pl.BlockSpec((tm, D_MODEL), lambda i: (i, 2)), pl.BlockSpec((tm, D_MODEL), lambda i: (i, 3)),
                  pl.BlockSpec(w.shape, lambda i: (0, 0)), rows, ROW_SPEC1, ROW_SPEC1],
        out_specs=[rows, rows, rows],
        out_shape=[jax.ShapeDtypeStruct((s, D_MODEL), BF16)] + [jax.ShapeDtypeStruct((s, D_MODEL), F32)] * 2,
        compiler_params=_cparams(dimension_semantics=("parallel",)),
    )(att, rec, proj, proj, w, x, g_post, gate)


def _post_bwd_merge_bwd(f, dy, g_post, gate, res_w, w, att, rec, proj, name):
    s = f.shape[0]
    tm = min(FUSE_TM, s)

    def body(f_ref, dy_ref, gp_ref, gate_ref, w_ref, att_ref, rec_ref, ga_ref, gr_ref,
             df_ref, datt_ref, drec_ref, dga_ref, dgr_ref, sums_ref):
        i = pl.program_id(0)
        dgp, dgate, df = _post_vjp(res_w, gp_ref[...], gate_ref[...], f_ref[...], dy_ref[...])
        dfb = df.astype(BF16)
        df_ref[...] = dfb
        _accumulate(sums_ref, _sum_rows(dgp, dgate), i)
        dmerged = lax.dot_general(dfb, w_ref[...], _DIMS["nt"], preferred_element_type=F32)
        _, vjp = jax.vjp(_merge_fn, att_ref[...], rec_ref[...], ga_ref[...], gr_ref[...])
        for ref, val in zip((datt_ref, drec_ref, dga_ref, dgr_ref), vjp(dmerged)):
            ref[...] = val.astype(ref.dtype)

    rows = pl.BlockSpec((tm, D_MODEL), lambda i: (i, 0))
    return pl.pallas_call(
        body,
        name=name,
        grid=(s // tm,),
        in_specs=[rows, rows, ROW_SPEC1, ROW_SPEC1, pl.BlockSpec(w.shape, lambda i: (0, 0)), rows, rows,
                  pl.BlockSpec((tm, D_MODEL), lambda i: (i, 2)), pl.BlockSpec((tm, D_MODEL), lambda i: (i, 3))],
        out_specs=[rows] * 5 + [SUMS_SPEC1],
        out_shape=[jax.ShapeDtypeStruct((s, D_MODEL), BF16)] * 5 + [SUMS_SHAPE],
        compiler_params=_cparams(dimension_semantics=("arbitrary",)),
    )(f, dy, g_post, gate, w, att, rec, proj, proj)


def _matmul_recin_bwd(drec, w, hs, proj, name, deps=()):
    s = drec.shape[0]
    tm = min(FUSE_TM, s)
    nd = len(deps)

    def body(d_ref, w_ref, hs_ref, yr_ref, *rest):
        dhs_ref, dyr_ref = rest[nd:]
        d = lax.dot_general(d_ref[...], w_ref[...], _DIMS["nt"], preferred_element_type=F32)
        _, vjp = jax.vjp(_recin_fn, hs_ref[...], yr_ref[...])
        dhs, dyr = vjp(d)
        dhs_ref[...] = dhs
        dyr_ref[...] = dyr.astype(dyr_ref.dtype)

    rows = pl.BlockSpec((tm, D_MODEL), lambda i: (i, 0))
    return pl.pallas_call(
        body,
        name=name,
        grid=(s // tm,),
        in_specs=[rows, pl.BlockSpec(w.shape, lambda i: (0, 0)), rows,
                  pl.BlockSpec((tm, D_MODEL), lambda i: (i, 1))] + [ANY_SPEC] * nd,
        out_specs=[rows, rows],
        out_shape=[jax.ShapeDtypeStruct((s, D_MODEL), F32), jax.ShapeDtypeStruct((s, D_MODEL), BF16)],
        compiler_params=_cparams(dimension_semantics=("parallel",)),
    )(drec, w, hs, proj, *deps)


def _post_vjp(res_w, g, gate, f, dy):
    _, vjp = jax.vjp(lambda g, gate, f: _post_fn(res_w, g, gate, f, 0.0), g, gate, f)
    return vjp(dy)


def _post_bwd_up_bwd(f, dy, g_post, gate, res_w, w_down, g, u, name, deps=()):
    s = f.shape[0]
    tm = min(FFN_TM, s)
    nd = len(deps)

    def body(f_ref, dy_ref, gp_ref, gate_ref, wd_ref, g_ref, u_ref, *rest):
        df_ref, dgu_ref, sums_ref, df_s = rest[nd:]
        i = pl.program_id(0)

        @pl.when(pl.program_id(1) == 0)
        def _():
            dgp, dgate, df = _post_vjp(res_w, gp_ref[...], gate_ref[...], f_ref[...], dy_ref[...])
            df_s[...] = df.astype(BF16)
            df_ref[...] = df_s[...]
            _accumulate(sums_ref, _sum_rows(dgp, dgate), i)

        da = lax.dot_general(df_s[...], wd_ref[...], _DIMS["nt"], preferred_element_type=F32)
        _, vjp = jax.vjp(_glu_fn, g_ref[...].astype(F32), u_ref[...].astype(F32))
        dg, du = vjp(da)
        dgu_ref[0] = dg.astype(dgu_ref.dtype)
        dgu_ref[1] = du.astype(dgu_ref.dtype)

    rows = pl.BlockSpec((tm, D_MODEL), lambda i, j: (i, 0))
    blk = pl.BlockSpec((tm, FFN_TF), lambda i, j: (i, j))
    return pl.pallas_call(
        body,
        name=name,
        grid=(s // tm, D_FF // FFN_TF),
        in_specs=[rows, rows, ROW_SPEC2, ROW_SPEC2, pl.BlockSpec((FFN_TF, D_MODEL), lambda i, j: (j, 0)), blk,
                  blk] + [ANY_SPEC] * nd,
        out_specs=[rows, pl.BlockSpec((2, tm, FFN_TF), lambda i, j: (0, i, j)), SUMS_SPEC2],
        out_shape=[jax.ShapeDtypeStruct((s, D_MODEL), BF16), jax.ShapeDtypeStruct((2, s, D_FF), BF16), SUMS_SHAPE],
        scratch_shapes=[pltpu.VMEM((tm, D_MODEL), BF16)],
        compiler_params=_cparams(dimension_semantics=("arbitrary", "arbitrary")),
    )(f, dy, g_post, gate, w_down, g, u, *deps)


def _matmul_pre_bwd(parts, w_t, x, dres, g, shift, scale, name, deps=()):
    s = x.shape[0]
    na, nd = len(parts), len(deps)
    ranges = [p[3] for p in parts]

    def body(*refs):
        a_refs = refs[:na]
        w_ref, x_ref, dres_ref, g_ref, sh_ref, sc_ref = refs[na : na + 6]
        dx_ref, sums_ref = refs[na + 6 + nd :]
        i = pl.program_id(0)
        dh = None
        for a_ref, (r0, r1) in zip(a_refs, ranges):
            p = jnp.dot(a_ref[...], w_ref[r0:r1, :], preferred_element_type=F32)
            dh = p if dh is None else dh + p
        _, vjp = jax.vjp(_pre_fn, g_ref[...], sh_ref[...], sc_ref[...], x_ref[...])
        dg, dsh, dsc, dx = vjp(dh)
        dx_ref[...] = dx + dres_ref[...]
        _accumulate(sums_ref, _sum_rows(dg, dsh, dsc), i)

    tm = parts[0][1][-2]
    rows = pl.BlockSpec((tm, D_MODEL), lambda i: (i, 0))
    return pl.pallas_call(
        body,
        name=name,
        grid=(s // tm,),
        in_specs=[pl.BlockSpec(p[1], p[2]) for p in parts]
        + [pl.BlockSpec(w_t.shape, lambda i: (0, 0)), rows, rows, ROW_SPEC1, ROW_SPEC1, ROW_SPEC1]
        + [ANY_SPEC] * nd,
        out_specs=[rows, SUMS_SPEC1],
        out_shape=[jax.ShapeDtypeStruct((s, D_MODEL), F32), SUMS_SHAPE],
        compiler_params=_cparams(dimension_semantics=("arbitrary",)),
    )(*[p[0] for p in parts], w_t, x, dres, g, shift, scale, *deps)


def _dw_gu(dgu, h, name, deps=(), tk=DW_TK):
    s = h.shape[0]
    tk = min(tk, s)
    nk = s // tk
    half = D_FF // FFN_TF

    def body(a_ref, b_ref, *rest):
        o_ref = rest[len(deps)]
        kk = pl.program_id(1)
        p = lax.dot_general(a_ref[...], b_ref[...], _DIMS["tn"], preferred_element_type=F32)
        if nk == 1:
            o_ref[...] = p.astype(o_ref.dtype)
            return
        acc = rest[len(deps) + 1]

        @pl.when(kk == 0)
        def _():
            acc[...] = p

        @pl.when(kk > 0)
        def _():
            acc[...] += p

        @pl.when(kk == nk - 1)
        def _():
            o_ref[...] = acc[...].astype(o_ref.dtype)

    return pl.pallas_call(
        body,
        name=name,
        grid=(2 * half, nk),
        in_specs=[pl.BlockSpec((None, tk, FFN_TF), lambda i, kk: (i // half, kk, i % half)),
                  pl.BlockSpec((tk, D_MODEL), lambda i, kk: (kk, 0))] + [ANY_SPEC] * len(deps),
        out_specs=pl.BlockSpec((FFN_TF, D_MODEL), lambda i, kk: (i, 0)),
        out_shape=jax.ShapeDtypeStruct((2 * D_FF, D_MODEL), BF16),
        scratch_shapes=[pltpu.VMEM((FFN_TF, D_MODEL), F32)] if nk > 1 else [],
        compiler_params=_cparams(dimension_semantics=("parallel", "arbitrary")),
    )(dgu, h, *deps)


def _shift_down(ext, j, rows):
    return pltpu.roll(ext, j, 0)[V7X_SUBLANES : V7X_SUBLANES + rows]


def _shift_up(ext, j, rows):
    return pltpu.roll(ext, ext.shape[0] - j, 0)[:rows] if j else ext[:rows]


LRU_SLAB = 256
N_SLABS = LRU_WIDTH // LRU_SLAB


def _slab_weights(wa, wx):
    per = LRU_SLAB // LRU_BLOCK
    eye = jnp.eye(per, dtype=wa.dtype)

    def diag(w):
        w4 = w.reshape(N_SLABS, per, LRU_BLOCK, LRU_BLOCK)
        return jnp.einsum("sbkj,bc->sbkcj", w4, eye).reshape(N_SLABS, LRU_SLAB, LRU_SLAB)

    return jnp.concatenate([diag(wa), diag(wx)], axis=2).reshape(LRU_WIDTH, 2 * LRU_SLAB).astype(BF16)


def _slab_cols(v, s):
    lo = s * LRU_SLAB
    return jnp.concatenate([v[:, lo : lo + LRU_SLAB], v[:, LRU_WIDTH + lo : LRU_WIDTH + lo + LRU_SLAB]], axis=1)


def _lru_front(proj, w8, b, w_slab, ba, bx, lam, name):
    def fn(i, steps, w8, b, w_slab, ba, bx, lam, x, halo):
        halo = jnp.where(i > 0, halo, 0.0)
        ext = jnp.concatenate([halo, x], axis=0)
        xc = b + w8[3:4] * x
        for j in (1, 2, 3):
            xc = xc + w8[3 - j : 4 - j] * _shift_down(ext, j, x.shape[0])
        xcb = xc.astype(BF16)
        prods = []
        for s in range(N_SLABS):
            rows = slice(s * LRU_SLAB, (s + 1) * LRU_SLAB)
            prods.append(jnp.dot(xcb[:, rows], w_slab[rows], preferred_element_type=F32))
        pre = jnp.concatenate([p[:, :LRU_SLAB] for p in prods] + [p[:, LRU_SLAB:] for p in prods], axis=1)
        a, u = _gates_fn(ba, bx, lam, pre, xc)
        return xc, pre, a, u

    tiles = [(proj, LRU_WIDTH, 0), (proj, LRU_WIDTH, 0, "prev")]
    outs = [(LRU_WIDTH, F32), (2 * LRU_WIDTH, F32), (LRU_WIDTH, F32), (LRU_WIDTH, F32)]
    return _rowwise(fn, name, [w8, b, w_slab, ba, bx, lam], tiles, outs, with_index=True)


def _lru_back(pre, xc, w_slab, ba, bx, lam, g, h_prev, name, deps=()):
    def fn(w_slab, ba, bx, lam, pre, xc, g, h_prev):
        _, vjp = jax.vjp(_gates_fn, ba, bx, lam, pre, xc)
        dba, dbx, dlam, dpre, dxc = vjp((g * h_prev, g))
        dpre = dpre.astype(BF16)
        back = []
        for s in range(N_SLABS):
            rows = slice(s * LRU_SLAB, (s + 1) * LRU_SLAB)
            back.append(lax.dot_general(_slab_cols(dpre, s), w_slab[rows], _DIMS["nt"], preferred_element_type=F32))
        return dpre, dxc + jnp.concatenate(back, axis=1), _sum_rows(dba, dbx, dlam)

    return _rowwise(fn, name, [w_slab, ba, bx, lam], [pre, xc, g, h_prev],
                    [(2 * LRU_WIDTH, BF16), (LRU_WIDTH, F32)], [(V7X_SUBLANES, LRU_WIDTH)], deps=deps)


def _lru_dw(xc, dpre, name):
    s = xc.shape[0]
    ts = min(512, s)
    steps = s // ts
    per = LRU_SLAB // LRU_BLOCK

    def body(x_ref, d_ref, o_ref, acc):
        i = pl.program_id(0)
        xcb = x_ref[...].astype(BF16)
        d = d_ref[...]
        for sl in range(N_SLABS):
            rows = slice(sl * LRU_SLAB, (sl + 1) * LRU_SLAB)
            p = lax.dot_general(xcb[:, rows], _slab_cols(d, sl), _DIMS["tn"], preferred_element_type=F32)

            @pl.when(i == 0)
            def _(p=p, rows=rows):
                acc[rows, :] = p

            @pl.when(i > 0)
            def _(p=p, rows=rows):
                acc[rows, :] += p

        @pl.when(i == steps - 1)
        def _():
            for half in range(2):
                for n in range(LRU_BLOCKS):
                    r0 = n * LRU_BLOCK
                    c0 = half * LRU_SLAB + (n % per) * LRU_BLOCK
                    o_ref[half, r0 : r0 + LRU_BLOCK, :] = acc[r0 : r0 + LRU_BLOCK, c0 : c0 + LRU_BLOCK]

    return pl.pallas_call(
        body,
        name=name,
        grid=(steps,),
        in_specs=[pl.BlockSpec((ts, LRU_WIDTH), lambda i: (i, 0)), pl.BlockSpec((ts, 2 * LRU_WIDTH), lambda i: (i, 0))],
        out_specs=pl.BlockSpec((2, LRU_WIDTH, LRU_BLOCK), lambda i: (0, 0, 0)),
        out_shape=jax.ShapeDtypeStruct((2, LRU_WIDTH, LRU_BLOCK), F32),
        scratch_shapes=[pltpu.VMEM((LRU_WIDTH, 2 * LRU_SLAB), F32)],
        compiler_params=_cparams(dimension_semantics=("arbitrary",)),
    )(xc, dpre)


def _conv_bwd(proj, w8, d1, name):
    def fn(i, steps, w8, x, halo, d, d1n):
        rows = x.shape[0]
        dn = jnp.where(i < steps - 1, d1n, 0.0)
        halo = jnp.where(i > 0, halo, 0.0)
        dext = jnp.concatenate([d, dn], axis=0)
        xext = jnp.concatenate([halo, x], axis=0)
        dx = w8[3:4] * d
        dw = [None] * 4
        dw[3] = _rowsum(d * x)
        for k in (1, 2, 3):
            dx = dx + w8[3 - k : 4 - k] * _shift_up(dext, k, rows)
            dw[3 - k] = _rowsum(d * _shift_down(xext, k, rows))
        return dx, _sum_rows(*dw, _rowsum(d))

    tiles = [(proj, LRU_WIDTH, 0), (proj, LRU_WIDTH, 0, "prev"), d1, (d1, LRU_WIDTH, 0, "next")]
    return _rowwise(fn, name, [w8], tiles, [(LRU_WIDTH, BF16)], [(V7X_SUBLANES, LRU_WIDTH)], with_index=True)


SCAN_ROWS = 512


def _block_scan(a, b, row, reverse):
    for d in (1, 2, 4):
        if reverse:
            shift, keep = V7X_SUBLANES - d, row < V7X_SUBLANES - d
        else:
            shift, keep = d, row >= d
        a_s = pltpu.roll(a, shift, 0)
        b_s = pltpu.roll(b, shift, 0)
        b = jnp.where(keep, a * b_s + b, b)
        a = jnp.where(keep, a * a_s, a)
    return a, b


def _scan_fwd(a, u, proj, name):
    s, w = a.shape
    ts = min(SCAN_ROWS, s)
    sub = ts // V7X_SUBLANES

    def body(a_ref, u_ref, yr_ref, h_ref, hp_ref, rec_ref, carry):
        @pl.when(pl.program_id(0) == 0)
        def _():
            carry[...] = jnp.zeros_like(carry)

        row = lax.broadcasted_iota(jnp.int32, (V7X_SUBLANES, w), 0)

        def step(j, c):
            rows = pl.ds(pl.multiple_of(j * V7X_SUBLANES, V7X_SUBLANES), V7X_SUBLANES)
            pa, pb = _block_scan(a_ref[rows, :], u_ref[rows, :], row, False)
            h = pb + pa * c
            h_ref[rows, :] = h
            hp_ref[rows, :] = jnp.where(row >= 1, pltpu.roll(h, 1, 0), c)
            return jnp.broadcast_to(h[V7X_SUBLANES - 1 :], (V7X_SUBLANES, w))

        carry[...] = lax.fori_loop(0, sub, step, carry[...])
        rec_ref[...] = _recin_fn(h_ref[...], yr_ref[...]).astype(rec_ref.dtype)

    spec = pl.BlockSpec((ts, w), lambda i: (i, 0))
    return pl.pallas_call(
        body,
        name=name,
        grid=(s // ts,),
        in_specs=[spec, spec, pl.BlockSpec((ts, w), lambda i: (i, 1))],
        out_specs=[spec, spec, spec],
        out_shape=[jax.ShapeDtypeStruct((s, w), F32)] * 2 + [jax.ShapeDtypeStruct((s, w), BF16)],
        scratch_shapes=[pltpu.VMEM((V7X_SUBLANES, w), F32)],
        compiler_params=_cparams(dimension_semantics=("arbitrary",)),
    )(a, u, proj)


def _scan_bwd(a, dh, name):
    s, w = a.shape
    ts = min(SCAN_ROWS, s)
    sub = ts // V7X_SUBLANES
    steps = s // ts

    def body(a_ref, d_ref, g_ref, carry):
        @pl.when(pl.program_id(0) == 0)
        def _():
            carry[...] = jnp.zeros_like(carry)

        row = lax.broadcasted_iota(jnp.int32, (V7X_SUBLANES, w), 0)

        def step(jj, c):
            j = sub - 1 - jj
            rows = pl.ds(pl.multiple_of(j * V7X_SUBLANES, V7X_SUBLANES), V7X_SUBLANES)
            av, dv = a_ref[rows, :], d_ref[rows, :]
            pa, pb = _block_scan(av, av * dv, row, True)
            big = pb + pa * c
            g_ref[rows, :] = dv + jnp.where(row < V7X_SUBLANES - 1, pltpu.roll(big, V7X_SUBLANES - 1, 0), c)
            return jnp.broadcast_to(big[:1], (V7X_SUBLANES, w))

        carry[...] = lax.fori_loop(0, sub, step, carry[...])

    spec = pl.BlockSpec((ts, w), lambda i: (steps - 1 - i, 0))
    return pl.pallas_call(
        body,
        name=name,
        grid=(steps,),
        in_specs=[spec, spec],
        out_specs=spec,
        out_shape=jax.ShapeDtypeStruct((s, w), F32),
        scratch_shapes=[pltpu.VMEM((V7X_SUBLANES, w), F32)],
        compiler_params=_cparams(dimension_semantics=("arbitrary",)),
    )(a, dh)


SKEW = 4 * ATT_TQ


def _skew_onehot():
    t = np.arange(SKEW)
    diag = np.where(t < 3 * ATT_TQ, -t, SKEW - t)
    idx = np.clip(diag + LEFT_CHUNKS * CHUNK, -MAX_REL, MAX_REL) + MAX_REL
    hit = (idx[:, None] == np.arange(2 * MAX_REL + 1)[None, :]) & (t[:, None] != 3 * ATT_TQ)
    return hit.astype(np.float32)


def _bias_tile(rel_bias, name):
    per_t = jnp.dot(rel_bias, jnp.asarray(_skew_onehot()).T, precision=lax.Precision.HIGHEST)
    win = 3 * ATT_TQ

    def body(t_ref, o_ref):
        tile = pltpu.roll(jnp.broadcast_to(t_ref[0], (ATT_TQ, SKEW)), 0, 1, stride=1, stride_axis=0)[:, :win]
        qc = lax.broadcasted_iota(jnp.int32, (ATT_TQ, win), 0) // CHUNK
        kpos = lax.broadcasted_iota(jnp.int32, (ATT_TQ, win), 1)
        band = (kpos // CHUNK >= qc) & (kpos // CHUNK <= qc + LEFT_CHUNKS)
        for v in range(3):
            o_ref[v, 0] = jnp.where(band & (kpos >= (2 - v) * ATT_TQ), tile, NEG)

    return pl.pallas_call(
        body,
        name=name,
        grid=(ATT_HEADS,),
        in_specs=[pl.BlockSpec((1, 1, SKEW), lambda h: (h, 0, 0))],
        out_specs=pl.BlockSpec((3, 1, ATT_TQ, win), lambda h: (0, h, 0, 0)),
        out_shape=jax.ShapeDtypeStruct((3, ATT_HEADS, ATT_TQ, win), F32),
        compiler_params=_cparams(dimension_semantics=("parallel",)),
    )(per_t.reshape(ATT_HEADS, 1, SKEW))


def _bias_grad(dbias, name):
    win = 3 * ATT_TQ

    def body(d_ref, o_ref):
        d = jnp.concatenate([d_ref[0], jnp.zeros((ATT_TQ, SKEW - win), F32)], axis=1)
        r = lax.broadcasted_iota(jnp.int32, (ATT_TQ, ATT_TQ), 0)
        c = lax.broadcasted_iota(jnp.int32, (ATT_TQ, ATT_TQ), 1)
        flip = (r + c == ATT_TQ - 1).astype(F32)
        d = jnp.dot(flip, d, preferred_element_type=F32, precision=lax.Precision.HIGHEST)
        o_ref[0] = jnp.sum(pltpu.roll(d, SKEW - (ATT_TQ - 1), 1, stride=1, stride_axis=0), axis=0, keepdims=True)

    per_t = pl.pallas_call(
        body,
        name=name,
        grid=(ATT_HEADS,),
        in_specs=[pl.BlockSpec((1, ATT_TQ, win), lambda h: (h, 0, 0))],
        out_specs=pl.BlockSpec((1, 1, SKEW), lambda h: (h, 0, 0)),
        out_shape=jax.ShapeDtypeStruct((ATT_HEADS, 1, SKEW), F32),
        compiler_params=_cparams(dimension_semantics=("parallel",)),
    )(dbias)
    return jnp.dot(per_t.reshape(ATT_HEADS, SKEW), jnp.asarray(_skew_onehot()), precision=lax.Precision.HIGHEST)


ATT_STEP_HEADS = ATT_HEADS
ATT_STEP_COLS = ATT_STEP_HEADS * ATT_HEAD_DIM


def _attn_specs(nt):
    qb, kb, vb = OFF_Q // ATT_STEP_COLS, OFF_K // ATT_STEP_COLS, OFF_V // ATT_STEP_COLS
    blk = (ATT_TQ, ATT_STEP_COLS)

    def qmap(base):
        return lambda hp, m: (jnp.minimum(m, nt - 1), base + hp)

    def wmap(base, back):
        return lambda hp, m: (jnp.clip(m - back, 0, nt - 1), base + hp)

    specs = [pl.BlockSpec(blk, qmap(qb))]
    specs += [pl.BlockSpec(blk, wmap(kb, back)) for back in (2, 1, 0)]
    specs += [pl.BlockSpec(blk, wmap(vb, back)) for back in (2, 1, 0)]
    return specs


ATT_SCALE = ATT_HEAD_DIM**-0.5


def _attn_exp(qh, kh, bias):
    s = lax.dot_general(qh, kh, _DIMS["nt"], preferred_element_type=F32) + bias
    e = jnp.exp(s - jnp.max(s, axis=-1, keepdims=True))
    return e, jnp.sum(e, axis=-1, keepdims=True)


def _attn_window(k0, k1, k2, v0, v1, v2):
    k = jnp.concatenate([k0[...], k1[...], k2[...]], axis=0).astype(BF16)
    v = jnp.concatenate([v0[...], v1[...], v2[...]], axis=0).astype(BF16)
    return k, v


def _bias_spec():
    return pl.BlockSpec((1, ATT_STEP_HEADS, ATT_TQ, 3 * ATT_TQ), lambda hp, m: (jnp.minimum(m, 2), hp, 0, 0))


def _attn_fwd(proj, bias, name):
    s = proj.shape[0]
    nt = s // ATT_TQ

    def body(q_ref, k0, k1, k2, v0, v1, v2, b_ref, o_ref):
        k, v = _attn_window(k0, k1, k2, v0, v1, v2)
        q = (q_ref[...] * ATT_SCALE).astype(BF16)
        for hh in range(ATT_STEP_HEADS):
            cols = slice(hh * ATT_HEAD_DIM, (hh + 1) * ATT_HEAD_DIM)
            e, total = _attn_exp(q[:, cols], k[:, cols], b_ref[0, hh])
            o = jnp.dot(e.astype(BF16), v[:, cols], preferred_element_type=F32) / total
            o_ref[:, cols] = o.astype(o_ref.dtype)

    specs = _attn_specs(nt) + [_bias_spec()]
    return pl.pallas_call(
        body,
        name=name,
        grid=(ATT_HEADS // ATT_STEP_HEADS, nt),
        in_specs=specs,
        out_specs=pl.BlockSpec((ATT_TQ, ATT_STEP_COLS), lambda hp, m: (m, hp)),
        out_shape=jax.ShapeDtypeStruct((s, ATT_WIDTH), BF16),
        compiler_params=_cparams(dimension_semantics=("parallel", "arbitrary")),
    )(proj, proj, proj, proj, proj, proj, proj, bias)


def _attn_bwd(proj, bias, do, name):
    s = proj.shape[0]
    nt = s // ATT_TQ
    win = 3 * ATT_TQ

    def body(q_ref, k0, k1, k2, v0, v1, v2, do_ref, b_ref, dq_ref, dk_ref, dv_ref, db_ref, dk_acc, dv_acc):
        m = pl.program_id(1)

        @pl.when(m == 0)
        def _():
            dk_acc[...] = jnp.zeros_like(dk_acc)
            dv_acc[...] = jnp.zeros_like(dv_acc)
            db_ref[...] = jnp.zeros_like(db_ref)

        @pl.when(m < nt)
        def _():
            k, v = _attn_window(k0, k1, k2, v0, v1, v2)
            q = (q_ref[...] * ATT_SCALE).astype(BF16)
            dout = do_ref[...]
            for hh in range(ATT_STEP_HEADS):
                cols = slice(hh * ATT_HEAD_DIM, (hh + 1) * ATT_HEAD_DIM)
                qh, kh, vh, doh = q[:, cols], k[:, cols], v[:, cols], dout[:, cols]
                e, total = _attn_exp(qh, kh, b_ref[0, hh])
                p = e / total
                dvh = lax.dot_general(p.astype(BF16), doh, _DIMS["tn"], preferred_element_type=F32)
                dp = lax.dot_general(doh, vh, _DIMS["nt"], preferred_element_type=F32)
                ds = p * (dp - jnp.sum(dp * p, axis=-1, keepdims=True))
                db_ref[hh] += ds
                dsb = ds.astype(BF16)
                dqh = jnp.dot(dsb, kh, preferred_element_type=F32) * ATT_SCALE
                dkh = lax.dot_general(dsb, qh, _DIMS["tn"], preferred_element_type=F32)
                dq_ref[:, cols] = dqh.astype(dq_ref.dtype)
                dk_acc[:, cols] += dkh
                dv_acc[:, cols] += dvh

        dk_ref[...] = dk_acc[:ATT_TQ].astype(dk_ref.dtype)
        dv_ref[...] = dv_acc[:ATT_TQ].astype(dv_ref.dtype)
        for acc in (dk_acc, dv_acc):
            rest = acc[ATT_TQ:]
            acc[: win - ATT_TQ] = rest
            acc[win - ATT_TQ :] = jnp.zeros((ATT_TQ, ATT_STEP_COLS), F32)

    blk = (ATT_TQ, ATT_STEP_COLS)
    specs = _attn_specs(nt)
    specs.append(pl.BlockSpec(blk, lambda hp, m: (jnp.minimum(m, nt - 1), hp)))
    specs.append(_bias_spec())
    done = lambda hp, m: (jnp.maximum(m - 2, 0), hp)
    out_specs = [
        pl.BlockSpec(blk, lambda hp, m: (jnp.minimum(m, nt - 1), hp)),
        pl.BlockSpec(blk, done),
        pl.BlockSpec(blk, done),
        pl.BlockSpec((ATT_STEP_HEADS, ATT_TQ, win), lambda hp, m: (hp, 0, 0)),
    ]
    out_shape = [jax.ShapeDtypeStruct((s, ATT_WIDTH), BF16)] * 3
    out_shape.append(jax.ShapeDtypeStruct((ATT_HEADS, ATT_TQ, win), F32))
    return pl.pallas_call(
        body,
        name=name,
        grid=(ATT_HEADS // ATT_STEP_HEADS, nt + 2),
        in_specs=specs,
        out_specs=out_specs,
        out_shape=out_shape,
        scratch_shapes=[pltpu.VMEM((win, ATT_STEP_COLS), F32), pltpu.VMEM((win, ATT_STEP_COLS), F32)],
        compiler_params=_cparams(dimension_semantics=("arbitrary", "arbitrary")),
    )(proj, proj, proj, proj, proj, proj, proj, do, bias)


def _ada_fwd(c_all, w, name):
    def body(c_ref, w_ref, o_ref):
        act = _silu(c_ref[...]).astype(BF16)
        o_ref[...] = jnp.dot(act, w_ref[...].astype(BF16), preferred_element_type=F32)

    return pl.pallas_call(
        body, name=name, out_shape=jax.ShapeDtypeStruct((c_all.shape[0], w.shape[1]), F32), compiler_params=_cparams()
    )(c_all, w)


def _ada_bwd(c_all, dmod, name):
    def body(c_ref, d_ref, o_ref):
        act = _silu(c_ref[...])
        o_ref[...] = lax.dot_general(act, d_ref[...], _DIMS["tn"], preferred_element_type=F32,
                                     precision=lax.Precision.HIGHEST)

    return pl.pallas_call(
        body, name=name, out_shape=jax.ShapeDtypeStruct((c_all.shape[1], dmod.shape[1]), F32), compiler_params=_cparams()
    )(c_all, dmod)


def _adamw_parts(landed, sent, me, w, m, v, name, rows=256):
    r, c = w.shape
    tr = _pick(r, rows, 16)

    def body(me_ref, g_ref, own_ref, w_ref, m_ref, v_ref, go_ref, d_ref, mo_ref, vo_ref):
        mine = me_ref[0]
        grad = jnp.zeros((tr, c), F32)
        for d in range(N_DEV):
            grad = grad + jnp.where(mine == d, own_ref[0], g_ref[d]).astype(F32)
        _adamw_update(grad, w_ref, m_ref, v_ref, go_ref, d_ref, mo_ref, vo_ref)

    spec = pl.BlockSpec((tr, c), lambda i, me_ref: (i, 0))
    return pl.pallas_call(
        body,
        name=name,
        grid_spec=pltpu.PrefetchScalarGridSpec(
            num_scalar_prefetch=1,
            grid=(r // tr,),
            in_specs=[pl.BlockSpec((N_DEV, tr, c), lambda i, me_ref: (0, i, 0)),
                      pl.BlockSpec((1, tr, c), lambda i, me_ref: (me_ref[0], i, 0)), spec, spec, spec],
            out_specs=[spec] * 4,
        ),
        out_shape=[jax.ShapeDtypeStruct((r, c), F32)] * 4,
        compiler_params=_cparams(dimension_semantics=("parallel",)),
    )(me.reshape(1).astype(jnp.int32), landed, sent, w, m, v)


def _adamw_update(grad, w_ref, m_ref, v_ref, go_ref, d_ref, mo_ref, vo_ref):
    m2 = ADAM_B1 * m_ref[...] + (1.0 - ADAM_B1) * grad
    v2 = ADAM_B2 * v_ref[...] + (1.0 - ADAM_B2) * (grad * grad)
    m_hat = m2 / (1.0 - ADAM_B1**ADAM_STEP)
    v_hat = v2 / (1.0 - ADAM_B2**ADAM_STEP)
    go_ref[...] = grad
    d_ref[...] = -ADAM_LR * (m_hat / (jnp.sqrt(v_hat) + ADAM_EPS) + ADAM_WD * w_ref[...])
    mo_ref[...] = m2
    vo_ref[...] = v2


def _adamw(g, w, m, v, name, rows=256):
    r, c = w.shape
    tr = _pick(r, rows, 16)

    def body(g_ref, w_ref, m_ref, v_ref, go_ref, d_ref, mo_ref, vo_ref):
        _adamw_update(g_ref[...], w_ref, m_ref, v_ref, go_ref, d_ref, mo_ref, vo_ref)

    spec = pl.BlockSpec((tr, c), lambda i: (i, 0))
    return pl.pallas_call(
        body,
        name=name,
        grid=(r // tr,),
        in_specs=[spec, spec, spec, spec],
        out_specs=[spec] * 4,
        out_shape=[jax.ShapeDtypeStruct((r, c), F32)] * 4,
        compiler_params=_cparams(dimension_semantics=("parallel",)),
    )(g, w, m, v)


def _sum_parts(parts, name):
    def body(p_ref, o_ref):
        acc = p_ref[0]
        for d in range(1, N_DEV):
            acc = acc + p_ref[d]
        o_ref[...] = acc

    return pl.pallas_call(
        body, name=name, out_shape=jax.ShapeDtypeStruct(parts.shape[1:], F32), compiler_params=_cparams()
    )(parts)


def _place():
    x, y, c = lax.axis_index("x"), lax.axis_index("y"), lax.axis_index("c")
    return x, y, c


def _dev_index(p):
    return 4 * p[0] + 2 * p[1] + p[2]


def _allgather_vmem(shard, name):
    m_per, n = shard.shape

    def body(x_ref, out_ref, send_sems, recv_sems, local_sem):
        x, y, c = _place()
        me, sibling = (x, y, c), (x, y, 1 - c)
        chips = [(1 - x, y), (x, 1 - y), (1 - x, 1 - y)]

        def rows(p):
            return out_ref.at[pl.ds(_dev_index(p) * m_per, m_per), :]

        def copy(k, block, to, src=None):
            return pltpu.make_async_remote_copy(
                src_ref=rows(block) if src is None else src, dst_ref=rows(block),
                send_sem=send_sems.at[k], recv_sem=recv_sems.at[k], device_id=to, device_id_type=MESH)

        mine = pltpu.make_async_copy(x_ref, rows(me), local_sem)
        mine.start()
        first = [copy(0, me, sibling, src=x_ref)]
        first += [copy(1 + j, me, (*chip, c), src=x_ref) for j, chip in enumerate(chips)]
        for cp in first:
            cp.start()
        passed = [copy(4 + j, (*chip, c), sibling) for j, chip in enumerate(chips)]
        for j, chip in enumerate(chips):
            copy(1 + j, (*chip, c), me).wait_recv()
            passed[j].start()
        copy(0, sibling, me).wait_recv()
        for j, chip in enumerate(chips):
            copy(4 + j, (*chip, 1 - c), me).wait_recv()
        for cp in first + passed:
            cp.wait_send()
        mine.wait()

    return pl.pallas_call(
        body,
        name=name,
        out_shape=jax.ShapeDtypeStruct((N_DEV * m_per, n), shard.dtype),
        in_specs=[pl.BlockSpec(memory_space=pltpu.VMEM)],
        out_specs=pl.BlockSpec(memory_space=pltpu.VMEM),
        scratch_shapes=[pltpu.SemaphoreType.DMA((7,)), pltpu.SemaphoreType.DMA((7,)), pltpu.SemaphoreType.DMA],
        compiler_params=_cparams(),
    )(shard)


def _allgather_hbm(shards, name):
    n = len(shards)

    def body(*refs):
        ins, outs = refs[:n], refs[n : 2 * n]
        send_sems, recv_sems, local_sems = refs[2 * n :]
        x, y, c = _place()
        me, sibling = (x, y, c), (x, y, 1 - c)
        chips = [(1 - x, y), (x, 1 - y), (1 - x, 1 - y)]

        def copy(a, k, block, to, src=None):
            dst = outs[a].at[_dev_index(block)]
            return pltpu.make_async_remote_copy(
                src_ref=dst if src is None else src, dst_ref=dst,
                send_sem=send_sems.at[a * 7 + k], recv_sem=recv_sems.at[a * 7 + k], device_id=to, device_id_type=MESH)

        mine = [pltpu.make_async_copy(ins[a], outs[a].at[_dev_index(me)], local_sems.at[a]) for a in range(n)]
        for cp in mine:
            cp.start()
        first = []
        for a in range(n):
            first.append(copy(a, 0, me, sibling, src=ins[a]))
            first += [copy(a, 1 + j, me, (*chip, c), src=ins[a]) for j, chip in enumerate(chips)]
        for cp in first:
            cp.start()
        passed = []
        for j, chip in enumerate(chips):
            for a in range(n):
                copy(a, 1 + j, (*chip, c), me).wait_recv()
                cp = copy(a, 4 + j, (*chip, c), sibling)
                cp.start()
                passed.append(cp)
        for a in range(n):
            copy(a, 0, sibling, me).wait_recv()
        for j, chip in enumerate(chips):
            for a in range(n):
                copy(a, 4 + j, (*chip, 1 - c), me).wait_recv()
        for cp in first + passed:
            cp.wait_send()
        for cp in mine:
            cp.wait()

    any_spec = pl.BlockSpec(memory_space=pl.ANY)
    return pl.pallas_call(
        body,
        name=name,
        out_shape=[jax.ShapeDtypeStruct((N_DEV, *s.shape), s.dtype) for s in shards],
        in_specs=[any_spec] * n,
        out_specs=[any_spec] * n,
        scratch_shapes=[pltpu.SemaphoreType.DMA((7 * n,)), pltpu.SemaphoreType.DMA((7 * n,)),
                        pltpu.SemaphoreType.DMA((n,))],
        compiler_params=_cparams(),
    )(*shards)


HBM_SPEC = pl.BlockSpec(memory_space=pltpu.HBM)
SEM_SPEC = pl.BlockSpec(memory_space=pltpu.SEMAPHORE)
EFFECT = pltpu.SideEffectType.DATAFLOW_SIDE_EFFECTING


def _peers(x, y, c):
    return [(1 - x if k & 4 else x, 1 - y if k & 2 else y, 1 - c if k & 1 else c) for k in range(1, N_DEV)]


def _push_peers(mode, x, y, c):
    if mode == "all":
        return _peers(x, y, c)
    return [(x, y, 1 - c), (1 - x, y, c), (x, 1 - y, c), (1 - x, 1 - y, c)]


def _push_start(groups, sliced, name, after=(), modes=None):
    flat = [b for g in groups for b in g]
    n, ng = len(flat), len(groups)
    sizes = [len(g) for g in groups]
    modes = modes or ["all"] * ng
    fan = [len(_push_peers(m, 0, 0, 0)) for m in modes]
    per = 2 if sliced else 3
    lands = [lax.empty(b.shape if sliced else (N_DEV, *b.shape), b.dtype) for b in flat]

    def body(*refs):
        ins, lnd = refs[:n], refs[n : 2 * n]
        sems = refs[2 * n + len(after) : 2 * n + len(after) + per * ng]
        token = refs[-1]
        x, y, c = _place()
        me = _dev_index((x, y, c))
        if not sliced:
            first = 0
            for gi, size in enumerate(sizes):
                for j in range(first, first + size):
                    pltpu.make_async_copy(ins[j], lnd[j].at[me], sems[per * gi + 2].at[j - first]).start()
                first += size
        first = 0
        for gi, size in enumerate(sizes):
            for k, peer in enumerate(_push_peers(modes[gi], x, y, c)):
                for j in range(first, first + size):
                    sem = (j - first) * fan[gi] + k
                    pltpu.make_async_remote_copy(
                        src_ref=ins[j].at[_dev_index(peer)] if sliced else ins[j], dst_ref=lnd[j].at[me],
                        send_sem=sems[per * gi].at[sem], recv_sem=sems[per * gi + 1].at[sem],
                        device_id=peer, device_id_type=MESH).start()
            first += size
        token[...] = jnp.zeros_like(token)

    out_shape = []
    for size, width in zip(sizes, fan):
        out_shape += [pltpu.SemaphoreType.DMA((width * size,)), pltpu.SemaphoreType.DMA((width * size,))]
        out_shape += [] if sliced else [pltpu.SemaphoreType.DMA((size,))]
    out_shape += [pltpu.HBM(b.shape, b.dtype) for b in flat + lands]
    out_shape.append(jax.ShapeDtypeStruct((V7X_SUBLANES, V7X_LANES), F32))
    res = pl.pallas_call(
        body,
        name=name,
        out_shape=tuple(out_shape),
        in_specs=[HBM_SPEC] * (2 * n) + [ANY_SPEC] * len(after),
        out_specs=tuple([SEM_SPEC] * (per * ng) + [HBM_SPEC] * (2 * n) + [pl.BlockSpec(memory_space=pltpu.VMEM)]),
        input_output_aliases={i: per * ng + i for i in range(2 * n)},
        compiler_params=pltpu.CompilerParams(has_side_effects=EFFECT),
    )(*[pltpu.with_memory_space_constraint(b, pltpu.HBM) for b in flat + lands], *after)
    sems, thru, token = res[: per * ng], res[per * ng : per * ng + 2 * n], res[-1]
    out, first = [], 0
    for gi, size in enumerate(sizes):
        out.append((sems[per * gi], sems[per * gi + 1], list(thru[first : first + size]),
                    list(thru[n + first : n + first + size]), None if sliced else sems[per * gi + 2]))
        first += size
    return out, token


def _push_wait(started, sliced, after, name, mode="all"):
    send_sems, recv_sems, bufs, lands, own_sems = started
    n = len(bufs)
    fan = len(_push_peers(mode, 0, 0, 0))
    own = [] if own_sems is None else [own_sems]

    def body(*refs):
        ins, lnd = refs[:n], refs[n : 2 * n]
        send_ref, recv_ref = refs[2 * n], refs[2 * n + 1]
        x, y, c = _place()
        for k, peer in enumerate(_push_peers(mode, x, y, c)):
            for j in range(n):
                cp = pltpu.make_async_remote_copy(
                    src_ref=ins[j].at[_dev_index(peer)] if sliced else ins[j], dst_ref=lnd[j].at[_dev_index(peer)],
                    send_sem=send_ref.at[j * fan + k], recv_sem=recv_ref.at[j * fan + k],
                    device_id=peer, device_id_type=MESH)
                cp.wait_send()
                cp.wait_recv()
        if own:
            for j in range(n):
                pltpu.make_async_copy(ins[j], lnd[j].at[_dev_index((x, y, c))], refs[2 * n + 2].at[j]).wait()

    res = pl.pallas_call(
        body,
        name=name,
        out_shape=tuple(pltpu.HBM(b.shape, b.dtype) for b in bufs + lands),
        in_specs=[HBM_SPEC] * (2 * n) + [SEM_SPEC] * (2 + len(own)) + [pl.BlockSpec(memory_space=pl.ANY)],
        out_specs=tuple([HBM_SPEC] * (2 * n)),
        input_output_aliases={i: i for i in range(2 * n)},
        compiler_params=pltpu.CompilerParams(has_side_effects=EFFECT),
    )(*bufs, *lands, send_sems, recv_sems, *own, after)
    return list(res[:n]), list(res[n:])


def _forward_copies(lnd, send_ref, recv_ref, incoming):
    x, y, c = _place()
    copies = []
    for k, chip in enumerate([(1 - x, y), (x, 1 - y), (1 - x, 1 - y)]):
        mine, theirs = _dev_index((*chip, c)), _dev_index((*chip, 1 - c))
        for j, ref in enumerate(lnd):
            copies.append(pltpu.make_async_remote_copy(
                src_ref=ref.at[mine], dst_ref=ref.at[theirs if incoming else mine],
                send_sem=send_ref.at[j * 3 + k], recv_sem=recv_ref.at[j * 3 + k],
                device_id=(x, y, 1 - c), device_id_type=MESH))
    return copies


def _forward_start(lands, name):
    n = len(lands)

    def body(*refs):
        for cp in _forward_copies(refs[:n], refs[n], refs[n + 1], False):
            cp.start()

    res = pl.pallas_call(
        body,
        name=name,
        out_shape=(pltpu.SemaphoreType.DMA((3 * n,)), pltpu.SemaphoreType.DMA((3 * n,)),
                   *[pltpu.HBM(b.shape, b.dtype) for b in lands]),
        in_specs=[HBM_SPEC] * n,
        out_specs=(SEM_SPEC, SEM_SPEC, *[HBM_SPEC] * n),
        input_output_aliases={i: 2 + i for i in range(n)},
        compiler_params=pltpu.CompilerParams(has_side_effects=EFFECT),
    )(*[pltpu.with_memory_space_constraint(b, pltpu.HBM) for b in lands])
    return res[0], res[1], list(res[2:])


def _forward_wait(started, after, name):
    send_sems, recv_sems, lands = started
    n = len(lands)

    def body(*refs):
        for cp in _forward_copies(refs[:n], refs[n], refs[n + 1], True):
            cp.wait_send()
            cp.wait_recv()

    res = pl.pallas_call(
        body,
        name=name,
        out_shape=tuple(pltpu.HBM(b.shape, b.dtype) for b in lands),
        in_specs=[HBM_SPEC] * n + [SEM_SPEC, SEM_SPEC, pl.BlockSpec(memory_space=pl.ANY)],
        out_specs=tuple([HBM_SPEC] * n),
        input_output_aliases={i: i for i in range(n)},
        compiler_params=pltpu.CompilerParams(has_side_effects=EFFECT),
    )(*lands, send_sems, recv_sems, after)
    return list(res)


def _cols_full(g):
    return jnp.transpose(g, (1, 0, 2)).reshape(g.shape[1], -1)


def _rows_full(g):
    return g.reshape(-1, g.shape[2])


def _cols_parts(full, n=N_DEV):
    r = full.shape[0]
    return jnp.transpose(full.reshape(r, n, -1), (1, 0, 2)).astype(BF16)


def _rows_parts(full):
    return full.reshape(N_DEV, -1, full.shape[1]).astype(BF16)


def _pad_rows(v, rows):
    flat = v.reshape(-1)
    return jnp.pad(flat, (0, rows * D_MODEL - flat.shape[0])).reshape(rows, D_MODEL)


def _my_cols(full, me, width):
    return lax.dynamic_slice_in_dim(full, me * width, width, axis=full.ndim - 1)


def kernel(x, c, w_ada, b_ada, norm_pre, norm_post, ffn1_w_gu, ffn1_w_down, w_in, rel_bias, conv_w, conv_b, lru_wa, lru_ba, lru_wx, lru_bx, lru_lambda, w_att_o, w_rec_o, w_out, ffn2_w_gu, ffn2_w_down, loss_target, m_w_ada, m_b_ada, m_norm_pre, m_norm_post, m_ffn1_w_gu, m_ffn1_w_down, m_w_in, m_rel_bias, m_conv_w, m_conv_b, m_lru_wa, m_lru_ba, m_lru_wx, m_lru_bx, m_lru_lambda, m_w_att_o, m_w_rec_o, m_w_out, m_ffn2_w_gu, m_ffn2_w_down, v_w_ada, v_b_ada, v_norm_pre, v_norm_post, v_ffn1_w_gu, v_ffn1_w_down, v_w_in, v_rel_bias, v_conv_w, v_conv_b, v_lru_wa, v_lru_ba, v_lru_wx, v_lru_bx, v_lru_lambda, v_w_att_o, v_w_rec_o, v_w_out, v_ffn2_w_gu, v_ffn2_w_down):
    weights = dict(w_ada=w_ada, b_ada=b_ada, norm_pre=norm_pre, norm_post=norm_post, ffn1_w_gu=ffn1_w_gu,
                   ffn1_w_down=ffn1_w_down, w_in=w_in, rel_bias=rel_bias, conv_w=conv_w, conv_b=conv_b,
                   lru_wa=lru_wa, lru_ba=lru_ba, lru_wx=lru_wx, lru_bx=lru_bx, lru_lambda=lru_lambda,
                   w_att_o=w_att_o, w_rec_o=w_rec_o, w_out=w_out, ffn2_w_gu=ffn2_w_gu, ffn2_w_down=ffn2_w_down)
    mom1 = dict(w_ada=m_w_ada, b_ada=m_b_ada, norm_pre=m_norm_pre, norm_post=m_norm_post, ffn1_w_gu=m_ffn1_w_gu,
                ffn1_w_down=m_ffn1_w_down, w_in=m_w_in, rel_bias=m_rel_bias, conv_w=m_conv_w, conv_b=m_conv_b,
                lru_wa=m_lru_wa, lru_ba=m_lru_ba, lru_wx=m_lru_wx, lru_bx=m_lru_bx, lru_lambda=m_lru_lambda,
                w_att_o=m_w_att_o, w_rec_o=m_w_rec_o, w_out=m_w_out, ffn2_w_gu=m_ffn2_w_gu, ffn2_w_down=m_ffn2_w_down)
    mom2 = dict(w_ada=v_w_ada, b_ada=v_b_ada, norm_pre=v_norm_pre, norm_post=v_norm_post, ffn1_w_gu=v_ffn1_w_gu,
                ffn1_w_down=v_ffn1_w_down, w_in=v_w_in, rel_bias=v_rel_bias, conv_w=v_conv_w, conv_b=v_conv_b,
                lru_wa=v_lru_wa, lru_ba=v_lru_ba, lru_wx=v_lru_wx, lru_bx=v_lru_bx, lru_lambda=v_lru_lambda,
                w_att_o=v_w_att_o, w_rec_o=v_w_rec_o, w_out=v_w_out, ffn2_w_gu=v_ffn2_w_gu, ffn2_w_down=v_ffn2_w_down)
    order = list(weights)
    big = ["ffn1_w_gu", "ffn1_w_down", "w_in", "w_att_o", "w_rec_o", "w_out", "ffn2_w_gu", "ffn2_w_down"]
    col_sharded = {"ffn1_w_gu", "w_in", "w_att_o", "ffn2_w_gu"}
    small = ["b_ada", "norm_pre", "norm_post", "rel_bias", "conv_w", "conv_b", "lru_wa", "lru_ba", "lru_wx",
             "lru_bx", "lru_lambda"]

    xi, yi, ci = _place()
    me = _dev_index((xi, yi, ci))
    x0 = x[0]
    target = loss_target[0]
    fuse_tm = min(FUSE_TM, x0.shape[0])

    transposed = {"ffn1_w_gu", "w_in", "ffn2_w_gu"}
    local = lambda n, arr: jnp.transpose(arr[0]) if n in transposed else arr[0]
    shards = {n: local(n, weights[n]).astype(BF16) for n in big}
    full_of = lambda n, g: _cols_full(g) if n == "w_att_o" else _rows_full(g)

    pack = jnp.concatenate([c.reshape(-1), norm_pre.reshape(-1), norm_post.reshape(-1), conv_w.reshape(-1)])
    pack = jnp.pad(pack, (0, 3072 - pack.shape[0])).reshape(8, 384)
    got = _allgather_vmem(pack, "gather_small_inputs").reshape(N_DEV, 3072)
    c_all = got[:, :1024]
    unshard = lambda blk, rows: jnp.transpose(blk.reshape(N_DEV, rows, 128), (1, 0, 2)).reshape(rows, D_MODEL)
    g_pre = unshard(got[:, 1024:1408], 3)
    g_post = unshard(got[:, 1408:1792], 3)
    conv_taps = unshard(got[:, 1792:2304], 4)
    conv_w8 = jnp.concatenate([conv_taps, jnp.zeros((4, LRU_WIDTH), F32)], axis=0)

    mod_cols = _ada_fwd(c_all, w_ada[0], "ada_fwd")
    mod_all = _allgather_vmem(mod_cols, "gather_mod").reshape(N_DEV, N_DEV, 1152)
    mod = lax.dynamic_index_in_dim(mod_all, me, axis=1, keepdims=False).reshape(1, -1) + b_ada
    mod = mod.reshape(3, 3, 1, D_MODEL)

    w_slab = _slab_weights(lru_wa[0], lru_wx[0])
    bias = _bias_tile(rel_bias[0], "bias_tile")

    res_w = (0.5, 1.0, 0.5)
    row = lambda v: v.reshape(1, -1)

    (w1_gu,) = _allgather_hbm([shards["ffn1_w_gu"]], "gather_ffn1_w_gu")
    weight_groups = [["ffn1_w_down"], ["w_in"], ["w_att_o", "w_rec_o", "w_out"], ["ffn2_w_gu", "ffn2_w_down"]]
    weight_modes = ["all", "chip", "all", "all"]
    weights_started, started = _push_start([[shards[n] for n in g] for g in weight_groups], False,
                                           "gather_weights_start", after=(mod, w1_gu), modes=weight_modes)
    full = {"ffn1_w_gu": _rows_full(w1_gu)}

    def gathered_group(gi, after):
        sent, lands = _push_wait(weights_started[gi], False, after, f"gather_weights_wait{gi}", mode=weight_modes[gi])
        if weight_modes[gi] == "chip":
            lands = _forward_wait(_forward_start(lands, f"gather_weights_forward{gi}"), sent[0],
                                  f"gather_weights_forward_wait{gi}")
        for n, land in zip(weight_groups[gi], lands):
            full[n] = full_of(n, land)

    def ffn_fwd(xin, k, gi, tag, deps=(), target=None):
        h, a, g, u = _pre_up(xin, row(g_pre[k]), mod[k, 0], mod[k, 1], full[f"{tag}_w_gu"], f"{tag}_up", deps=deps)
        if f"{tag}_w_down" not in full:
            gathered_group(gi, a)
        f, *out = _matmul_post(a, full[f"{tag}_w_down"], xin, row(g_post[k]), mod[k, 2], res_w[k], f"{tag}_down",
                               target=target)
        return (out[0] if target is None else out), (h, g, u, a, f)

    x1, saved1 = ffn_fwd(x0, 0, 0, "ffn1", deps=(started,))

    gathered_group(1, x1)
    h2, proj = _pre_matmul(x1, row(g_pre[1]), mod[1, 0], mod[1, 1], full["w_in"], "mix_in",
                           b_shift=3 * ATT_WIDTH // 512)
    att_o = _attn_fwd(proj, bias, "attn_fwd")
    gathered_group(2, att_o)
    xc, pre, a_t, u_t = _lru_front(proj, conv_w8, conv_b, w_slab, lru_ba, lru_bx, lru_lambda, "lru_front")
    hs, h_prev, rec_in = _scan_fwd(a_t, u_t, proj, "lru_scan")
    att = _matmul(att_o, full["w_att_o"], "nn", F32, "att_out")
    rec = _matmul(rec_in, full["w_rec_o"], "nn", F32, "rec_out")
    merged, f2, x2 = _merge_matmul_post(att, rec, proj, full["w_out"], x1, row(g_post[1]), mod[1, 2], res_w[1],
                                        "mix_out")

    gathered_group(3, x2)
    (dy, sq), saved3 = ffn_fwd(x2, 2, 2, "ffn2", target=target)

    grads = {}
    norm_sums = [None] * 6

    pending = []

    def exchange_start(names, tag, after=()):
        send = [(_cols_parts if n == "w_att_o" else _rows_parts)(grads[n]) for n in names]
        (group,), token = _push_start([send], True, f"exchange_{tag}_start", after=after)
        pending.append((names, send, group, tag))
        return token

    def exchange_finish(names, send, group, tag, after):
        sent, lands = _push_wait(group, True, after, f"exchange_{tag}_wait")
        res = None
        for n, land, mine in zip(names, lands, sent):
            res = _adamw_parts(land, mine, me, local(n, weights[n]), local(n, mom1[n]), local(n, mom2[n]),
                               f"adamw_{n}")
            back = (lambda r: jnp.transpose(r)) if n in transposed else (lambda r: r)
            out_g[n], out_d[n], out_m[n], out_v[n] = [back(r).reshape(weights[n].shape) for r in res]
        return res[0]

    out_g, out_d, out_m, out_v = {}, {}, {}, {}

    def ffn_bwd(xin, k, saved, dout, tag):
        h, g, u, a, f = saved
        w_gu, w_down = f"{tag}_w_gu", f"{tag}_w_down"
        df, dgu, norm_sums[2 * k + 1] = _post_bwd_up_bwd(f, dout, row(g_post[k]), mod[k, 2], res_w[k], full[w_down],
                                                          g, u, f"{tag}_up_bwd")
        grads[w_down] = _matmul(a, df, "tn", BF16, f"{tag}_dw_down", tm=1408, tn=1024, tk=DW_TK)
        started = exchange_start([w_down], w_down)
        grads[w_gu] = _dw_gu(dgu, h, f"{tag}_dw_gu", deps=(started,))
        started = exchange_start([w_gu], w_gu)
        halves = [(dgu, (None, fuse_tm, D_FF), lambda i, half=half: (half, i, 0), (half * D_FF, (half + 1) * D_FF))
                  for half in range(2)]
        dx, norm_sums[2 * k] = _matmul_pre_bwd(halves, full[w_gu], xin, dout, row(g_pre[k]),
                                                                mod[k, 0], mod[k, 1], f"{tag}_dh", deps=(started,))
        return dx

    dx2 = ffn_bwd(x2, 2, saved3, dy, "ffn2")

    df2, datt, drec, dg_att, dg_rec, norm_sums[3] = _post_bwd_merge_bwd(
        f2, dx2, row(g_post[1]), mod[1, 2], res_w[1], full["w_out"], att, rec, proj, "mix_dmerged")
    grads["w_out"] = _matmul(merged, df2, "tn", BF16, "mix_dw_out", tm=1024, tn=1024, tk=DW_TK)
    datt_o = _matmul(datt, full["w_att_o"], "nt", BF16, "att_out_bwd")
    grads["w_att_o"] = _matmul(att_o, datt, "tn", BF16, "dw_att_o", tm=512, tn=1024, tk=DW_TK)
    grads["w_rec_o"] = _matmul(rec_in, drec, "tn", BF16, "dw_rec_o", tm=1024, tn=1024, tk=DW_TK)
    started = exchange_start(["w_out", "w_att_o", "w_rec_o"], "mix_out")
    dhs, dyr = _matmul_recin_bwd(drec, full["w_rec_o"], hs, proj, "rec_out_bwd", deps=(started,))
    g_t = _scan_bwd(a_t, dhs, "lru_scan_bwd")
    dpre, dxc, lru_sums = _lru_back(pre, xc, w_slab, lru_ba, lru_bx, lru_lambda, g_t, h_prev, "lru_back")
    dxr, conv_sums = _conv_bwd(proj, conv_w8, dxc, "conv_bwd")
    dq, dk, dv, dbias = _attn_bwd(proj, bias, datt_o, "attn_bwd")
    dproj = jnp.concatenate([dq, dk, dv, dxr, dyr, dg_att, dg_rec], axis=1)
    grads["w_in"] = _matmul(dproj, h2, "tn", BF16, "mix_dw_in", tm=1408, tn=1024, tk=DW_TK)
    pack_mix = jnp.concatenate([conv_sums, lru_sums, _pad_rows(_bias_grad(dbias, "bias_grad"), V7X_SUBLANES),
                                _lru_dw(xc, dpre, "lru_dw").reshape(128, D_MODEL)], axis=0)
    (mix_started,), started = _push_start([[pack_mix]], False, "small_grads_mix_start")
    started = exchange_start(["w_in"], "w_in", after=(started,))
    whole = [(dproj, (fuse_tm, PROJ_WIDTH), lambda i: (i, 0), (0, PROJ_WIDTH))]
    dx1, norm_sums[2] = _matmul_pre_bwd(whole, full["w_in"], x1, dx2, row(g_pre[1]), mod[1, 0],
                                                             mod[1, 1], "mix_dh", deps=(started,))

    dx0 = ffn_bwd(x0, 0, saved1, dx1, "ffn1")

    loss_rows = jnp.pad(sq, ((0, V7X_SUBLANES - 1), (0, 0)))
    pack_norm = jnp.concatenate([*norm_sums, loss_rows], axis=0)
    (norm_started,), _ = _push_start([[pack_norm]], False, "small_grads_norm_start")

    def summed(started, pack, after, tag):
        _, (parts,) = _push_wait(started, False, after, f"small_grads_{tag}_wait")
        return parts, _sum_parts(parts, f"small_grads_{tag}_sum")

    done = dx0
    last = pending[-1:]
    for names, send, group, tag in pending[:-1]:
        done = exchange_finish(names, send, group, tag, done)

    _, total = summed(mix_started, pack_mix, done, "mix")
    grads["conv_w"] = _my_cols(total[0:4], me, 128)
    grads["conv_b"] = total[4:5]
    grads["lru_ba"] = total[8:9]
    grads["lru_bx"] = total[9:10]
    grads["lru_lambda"] = total[10:11]
    grads["rel_bias"] = total[16:19].reshape(-1)[: ATT_HEADS * (2 * MAX_REL + 1)].reshape(ATT_HEADS, -1)
    grads["lru_wa"] = total[24:88].reshape(LRU_BLOCKS, LRU_BLOCK, LRU_BLOCK)
    grads["lru_wx"] = total[88:152].reshape(LRU_BLOCKS, LRU_BLOCK, LRU_BLOCK)
    parts, total = summed(norm_started, pack_norm, total, "norm")
    loss = 0.5 * jnp.sum(total[6 * V7X_SUBLANES]) / D_MODEL
    by_sandwich = lambda v: v[..., : 6 * V7X_SUBLANES, :].reshape(*v.shape[:-2], 3, 2 * V7X_SUBLANES, D_MODEL)
    dmod_of = lambda v: jnp.concatenate([by_sandwich(v)[..., 1:3, :], by_sandwich(v)[..., 9:10, :]], axis=-2)
    grads["b_ada"] = dmod_of(total).reshape(1, -1)
    grads["norm_pre"] = _my_cols(by_sandwich(total)[:, 0, :], me, 128)
    grads["norm_post"] = _my_cols(by_sandwich(total)[:, V7X_SUBLANES, :], me, 128)
    dmod_all = dmod_of(parts).reshape(N_DEV, 9 * D_MODEL)
    grads["w_ada"] = _ada_bwd(c_all, _my_cols(dmod_all, me, 1152), "ada_bwd")

    res = _adamw(grads["w_ada"], w_ada[0], m_w_ada[0], v_w_ada[0], "adamw_w_ada")
    out_g["w_ada"], out_d["w_ada"], out_m["w_ada"], out_v["w_ada"] = [r.reshape(w_ada.shape) for r in res]

    sizes = [int(np.prod(weights[n].shape)) for n in small]
    tot = sum(sizes)
    rows_small = -(-tot // (16 * D_MODEL)) * 16
    flat = lambda arrs: jnp.pad(jnp.concatenate([a.reshape(-1) for a in arrs]),
                                (0, rows_small * D_MODEL - tot)).reshape(rows_small, D_MODEL)
    res = _adamw(flat([grads[n] for n in small]), flat([weights[n] for n in small]),
                 flat([mom1[n] for n in small]), flat([mom2[n] for n in small]), "adamw_small", rows=rows_small)
    offs = np.cumsum([0] + sizes)
    for dst, r in zip((out_g, out_d, out_m, out_v), res):
        rf = r.reshape(-1)
        for i, n in enumerate(small):
            dst[n] = rf[offs[i] : offs[i + 1]].reshape(weights[n].shape)

    done = res[0]
    for names, send, group, tag in last:
        done = exchange_finish(names, send, group, tag, done)

    return (loss, dx0[None], *[out_g[n] for n in order], *[out_d[n] for n in order],
            *[out_m[n] for n in order], *[out_v[n] for n in order])
```

```python
import jax
import jax.numpy as jnp
import numpy as np
from jax import lax
from jax.experimental import pallas as pl
from jax.experimental.pallas import tpu as pltpu

D_MODEL = 1024
D_FF = 2816
ATT_HEADS = 8
ATT_HEAD_DIM = 64
ATT_WIDTH = 512
CHUNK = 64
LEFT_CHUNKS = 8
MAX_REL = 128
LRU_WIDTH = 1024
LRU_BLOCKS = 16
LRU_BLOCK = 64
LRU_C = 8.0
EPS = 1e-6
PROJ_WIDTH = 5632
N_DEV = 8

ADAM_LR = 0.001
ADAM_B1 = 0.9
ADAM_B2 = 0.999
ADAM_EPS = 1e-08
ADAM_WD = 0.01
ADAM_STEP = 10

V7X_LANES = 128
V7X_SUBLANES = 8
V7X_VMEM_BYTES = 64 * 1024 * 1024
VMEM_LIMIT = V7X_VMEM_BYTES - 8 * 1024 * 1024

ATT_TQ = 256
NEG = -1e30
BF16 = jnp.bfloat16
F32 = jnp.float32
MESH = pl.DeviceIdType.MESH

OFF_Q = 4 * LRU_WIDTH
OFF_K = OFF_Q + ATT_WIDTH
OFF_V = OFF_K + ATT_WIDTH


def _cparams(**kw):
    return pltpu.CompilerParams(vmem_limit_bytes=VMEM_LIMIT, **kw)


def _pick(n, target, unit=V7X_LANES):
    best = None
    for t in range(unit, min(n, target) + 1, unit):
        if n % t == 0:
            best = t
    return n if best is None else best


_DIMS = {
    "nn": (((1,), (0,)), ((), ())),
    "nt": (((1,), (1,)), ((), ())),
    "tn": (((0,), (0,)), ((), ())),
}


ANY_SPEC = pl.BlockSpec(memory_space=pl.ANY)


def _matmul(a, b, mode, out_dtype, name, tm=1024, tn=512, tk=1408, deps=(), b_shift=0):
    n_deps = len(deps)
    if mode == "nn":
        (m, k), (k2, n) = a.shape, b.shape
    elif mode == "nt":
        (m, k), (n, k2) = a.shape, b.shape
    else:
        (k, m), (k2, n) = a.shape, b.shape
    assert k == k2, (a.shape, b.shape, mode)
    tm, tn, tk = _pick(m, tm), _pick(n, tn), _pick(k, tk)
    nk = k // tk
    dims = _DIMS[mode]

    def body(a_ref, b_ref, *rest):
        o_ref, scratch = rest[n_deps], rest[n_deps + 1 :]
        p = lax.dot_general(a_ref[...], b_ref[...], dims, preferred_element_type=F32)
        if nk == 1:
            o_ref[...] = p.astype(o_ref.dtype)
        else:
            acc = scratch[0]
            kk = pl.program_id(2)

            @pl.when(kk == 0)
            def _():
                acc[...] = p

            @pl.when(kk > 0)
            def _():
                acc[...] += p

            @pl.when(kk == nk - 1)
            def _():
                o_ref[...] = acc[...].astype(o_ref.dtype)

    if mode == "nn":
        a_spec = pl.BlockSpec((tm, tk), lambda i, j, kk: (i, kk))
        b_spec = pl.BlockSpec((tk, tn), lambda i, j, kk: (kk, j))
    elif mode == "nt":
        a_spec = pl.BlockSpec((tm, tk), lambda i, j, kk: (i, kk))
        b_spec = pl.BlockSpec((tn, tk), lambda i, j, kk: ((j + b_shift) % (n // tn), kk))
    else:
        a_spec = pl.BlockSpec((tk, tm), lambda i, j, kk: (kk, i))
        b_spec = pl.BlockSpec((tk, tn), lambda i, j, kk: (kk, j))
    return pl.pallas_call(
        body,
        name=name,
        grid=(m // tm, n // tn, nk),
        in_specs=[a_spec, b_spec] + [ANY_SPEC] * n_deps,
        out_specs=pl.BlockSpec((tm, tn), lambda i, j, kk: (i, j)),
        out_shape=jax.ShapeDtypeStruct((m, n), out_dtype),
        scratch_shapes=[pltpu.VMEM((tm, tn), F32)] if nk > 1 else [],
        compiler_params=_cparams(dimension_semantics=("parallel", "parallel", "arbitrary")),
    )(a, b, *deps)


def _rowwise(fn, name, params, tiles, outs, accs=(), ts=256, with_index=False, deps=()):
    norm = []
    for t in tiles:
        if not isinstance(t, tuple):
            t = (t, t.shape[1], 0)
        norm.append(t if len(t) == 4 else (*t, None))
    s = norm[0][0].shape[0]
    ts = min(ts, s)
    assert s % ts == 0 and ts % V7X_SUBLANES == 0
    steps = s // ts
    halo_blocks = ts // V7X_SUBLANES
    n_p, n_t, n_o = len(params), len(norm), len(outs)

    def body(*refs):
        i = pl.program_id(0)
        vals = [r[...] for r in refs[: n_p + n_t]]
        res = fn(i, steps, *vals) if with_index else fn(*vals)
        if not isinstance(res, (tuple, list)):
            res = (res,)
        first_out = n_p + n_t + len(deps)
        o_refs = refs[first_out : first_out + n_o]
        a_refs = refs[first_out + n_o :]
        for r, v in zip(o_refs, res[:n_o]):
            r[...] = v.astype(r.dtype)
        for r, v in zip(a_refs, res[n_o:]):
            _accumulate(r, v, i)

    in_specs = [pl.BlockSpec(p.shape, lambda i: (0, 0)) for p in params]
    for arr, w, cb, halo in norm:
        if halo is None:
            in_specs.append(pl.BlockSpec((ts, w), lambda i, cb=cb: (i, cb)))
        elif halo == "prev":
            in_specs.append(
                pl.BlockSpec((V7X_SUBLANES, w), lambda i, cb=cb: (jnp.maximum(i * halo_blocks - 1, 0), cb))
            )
        else:
            last = s // V7X_SUBLANES - 1
            in_specs.append(
                pl.BlockSpec((V7X_SUBLANES, w), lambda i, cb=cb: (jnp.minimum((i + 1) * halo_blocks, last), cb))
            )
    in_specs += [ANY_SPEC] * len(deps)
    out_specs = [pl.BlockSpec((ts, w), lambda i: (i, 0)) for w, _ in outs]
    out_specs += [pl.BlockSpec(shape, lambda i: (0, 0)) for shape in accs]
    out_shape = [jax.ShapeDtypeStruct((s, w), dt) for w, dt in outs]
    out_shape += [jax.ShapeDtypeStruct(shape, F32) for shape in accs]
    res = pl.pallas_call(
        body,
        name=name,
        grid=(steps,),
        in_specs=in_specs,
        out_specs=out_specs,
        out_shape=out_shape,
        compiler_params=_cparams(dimension_semantics=("arbitrary",)),
    )(*params, *[t[0] for t in norm], *deps)
    return res


def _accumulate(ref, val, step):
    @pl.when(step == 0)
    def _():
        ref[...] = val

    @pl.when(step > 0)
    def _():
        ref[...] += val


def _sigmoid(z):
    return jax.nn.sigmoid(z)


def _silu(z):
    return z * _sigmoid(z)


def _gelu(z):
    return 0.5 * z * (1.0 + jnp.tanh(0.7978845608028654 * (z + 0.044715 * (z * z * z))))


def _pre_fn(g, shift, scale, x):
    r = lax.rsqrt(jnp.mean(x * x, axis=-1, keepdims=True) + EPS)
    return ((x * r) * g) * (1.0 + scale) + shift


def _post_fn(res_w, g, gate, f, x):
    r = lax.rsqrt(jnp.mean(f * f, axis=-1, keepdims=True) + EPS)
    return x + (res_w * gate) * ((f * r) * g)


def _gates_fn(ba, bx, lam, pre, xc):
    ra = _sigmoid(pre[:, :LRU_WIDTH] + ba)
    ia = _sigmoid(pre[:, LRU_WIDTH:] + bx)
    softplus = jnp.maximum(-lam, 0.0) + jnp.log1p(jnp.exp(-jnp.abs(lam)))
    log_a = (-LRU_C) * ra * softplus
    a = jnp.exp(log_a)
    mult = jnp.sqrt(-jnp.tanh(log_a) * (a * a + 1.0))
    return a, mult * (ia * xc)


def _recin_fn(hs, yr):
    return hs * _gelu(yr)


def _merge_fn(att, rec, g_att, g_rec):
    return _sigmoid(g_att) * att + _sigmoid(g_rec) * rec


def _rowsum(v):
    return jnp.sum(v, axis=0, keepdims=True)


FFN_TM = 512
FFN_TF = 1408


def _glu_fn(g, u):
    return _silu(g) * u


FUSE_TM = 256
DW_TK = 4096
ROW_SPEC2 = pl.BlockSpec((1, D_MODEL), lambda i, j: (0, 0))
ROW_SPEC1 = pl.BlockSpec((1, D_MODEL), lambda i: (0, 0))
SUMS_SPEC1 = pl.BlockSpec((V7X_SUBLANES, D_MODEL), lambda i: (0, 0))
SUMS_SPEC2 = pl.BlockSpec((V7X_SUBLANES, D_MODEL), lambda i, j: (0, 0))
SUMS_SHAPE = jax.ShapeDtypeStruct((V7X_SUBLANES, D_MODEL), F32)


def _sum_rows(*rows):
    pad = jnp.zeros((V7X_SUBLANES - len(rows), rows[0].shape[1]), F32)
    return jnp.concatenate([*rows, pad], axis=0)


def _pre_up(x, g, shift, scale, w_gu_t, name, deps=()):
    s = x.shape[0]
    tm = min(FFN_TM, s)
    nf = D_FF // FFN_TF
    nd = len(deps)

    def body(x_ref, g_ref, sh_ref, sc_ref, wg_ref, wu_ref, *rest):
        h_ref, a_ref, gg_ref, u_ref, h_s = rest[nd:]

        @pl.when(pl.program_id(1) == 0)
        def _():
            h = _pre_fn(g_ref[...], sh_ref[...], sc_ref[...], x_ref[...]).astype(BF16)
            h_s[...] = h
            h_ref[...] = h

        hv = h_s[...]
        gv = lax.dot_general(hv, wg_ref[...], _DIMS["nt"], preferred_element_type=F32)
        uv = lax.dot_general(hv, wu_ref[...], _DIMS["nt"], preferred_element_type=F32)
        a_ref[...] = _glu_fn(gv, uv).astype(a_ref.dtype)
        gg_ref[...] = gv.astype(gg_ref.dtype)
        u_ref[...] = uv.astype(u_ref.dtype)

    rows = pl.BlockSpec((tm, D_MODEL), lambda i, j: (i, 0))
    out = pl.BlockSpec((tm, FFN_TF), lambda i, j: (i, j))
    return pl.pallas_call(
        body,
        name=name,
        grid=(s // tm, nf),
        in_specs=[rows, ROW_SPEC2, ROW_SPEC2, ROW_SPEC2,
                  pl.BlockSpec((FFN_TF, D_MODEL), lambda i, j: (j, 0)),
                  pl.BlockSpec((FFN_TF, D_MODEL), lambda i, j: (nf + j, 0))] + [ANY_SPEC] * nd,
        out_specs=[rows, out, out, out],
        out_shape=[jax.ShapeDtypeStruct((s, D_MODEL), BF16)] + [jax.ShapeDtypeStruct((s, D_FF), BF16)] * 3,
        scratch_shapes=[pltpu.VMEM((tm, D_MODEL), BF16)],
        compiler_params=_cparams(dimension_semantics=("parallel", "arbitrary")),
    )(x, g, shift, scale, w_gu_t, w_gu_t, *deps)


def _pre_matmul(x, g, shift, scale, w_t, name, b_shift=0, tn=512):
    s = x.shape[0]
    n = w_t.shape[0]
    tm = min(2 * FFN_TM, s)

    def body(x_ref, g_ref, sh_ref, sc_ref, w_ref, h_ref, o_ref, h_s):
        @pl.when(pl.program_id(1) == 0)
        def _():
            h = _pre_fn(g_ref[...], sh_ref[...], sc_ref[...], x_ref[...]).astype(BF16)
            h_s[...] = h
            h_ref[...] = h

        o_ref[...] = lax.dot_general(h_s[...], w_ref[...], _DIMS["nt"], preferred_element_type=F32)

    rows = pl.BlockSpec((tm, D_MODEL), lambda i, j: (i, 0))
    return pl.pallas_call(
        body,
        name=name,
        grid=(s // tm, n // tn),
        in_specs=[rows, ROW_SPEC2, ROW_SPEC2, ROW_SPEC2,
                  pl.BlockSpec((tn, D_MODEL), lambda i, j: ((j + b_shift) % (n // tn), 0))],
        out_specs=[rows, pl.BlockSpec((tm, tn), lambda i, j: (i, j))],
        out_shape=[jax.ShapeDtypeStruct((s, D_MODEL), BF16), jax.ShapeDtypeStruct((s, n), F32)],
        scratch_shapes=[pltpu.VMEM((tm, D_MODEL), BF16)],
        compiler_params=_cparams(dimension_semantics=("parallel", "arbitrary")),
    )(x, g, shift, scale, w_t)


def _matmul_post(a, w, x, g_post, gate, res_w, name, target=None):
    s, k = a.shape
    tm = min(FFN_TM, s)
    extra = [] if target is None else [target]

    def body(a_ref, w_ref, x_ref, g_ref, gate_ref, *rest):
        f = jnp.dot(a_ref[...], w_ref[...], preferred_element_type=F32)
        y = _post_fn(res_w, g_ref[...], gate_ref[...], f, x_ref[...])
        if target is None:
            f_ref, y_ref = rest
            y_ref[...] = y
        else:
            t_ref, f_ref, dy_ref, sq_ref = rest
            diff = y - t_ref[...]
            dy_ref[...] = diff * (1.0 / D_MODEL)
            _accumulate(sq_ref, _rowsum(diff * diff), pl.program_id(0))
        f_ref[...] = f

    rows = pl.BlockSpec((tm, D_MODEL), lambda i: (i, 0))
    out_specs, out_shape = [rows, rows], [jax.ShapeDtypeStruct((s, D_MODEL), F32)] * 2
    if target is not None:
        out_specs.append(ROW_SPEC1)
        out_shape.append(jax.ShapeDtypeStruct((1, D_MODEL), F32))
    return pl.pallas_call(
        body,
        name=name,
        grid=(s // tm,),
        in_specs=[pl.BlockSpec((tm, k), lambda i: (i, 0)), pl.BlockSpec((k, D_MODEL), lambda i: (0, 0)), rows,
                  ROW_SPEC1, ROW_SPEC1] + [rows] * len(extra),
        out_specs=out_specs,
        out_shape=out_shape,
        compiler_params=_cparams(dimension_semantics=("arbitrary",)),
    )(a, w, x, g_post, gate, *extra)


def _merge_matmul_post(att, rec, proj, w, x, g_post, gate, res_w, name):
    s = att.shape[0]
    tm = min(FUSE_TM, s)

    def body(att_ref, rec_ref, ga_ref, gr_ref, w_ref, x_ref, g_ref, gate_ref, m_ref, f_ref, y_ref):
        merged = _merge_fn(att_ref[...], rec_ref[...], ga_ref[...], gr_ref[...]).astype(BF16)
        m_ref[...] = merged
        f = jnp.dot(merged, w_ref[...], preferred_element_type=F32)
        f_ref[...] = f
        y_ref[...] = _post_fn(res_w, g_ref[...], gate_ref[...], f, x_ref[...])

    rows = pl.BlockSpec((tm, D_MODEL), lambda i: (i, 0))
    return pl.pallas_call(
        body,
        name=name,
        grid=(s // tm,),
        in_specs=[rows, rows, pl.BlockSpec((tm, D_MODEL), lambda i: (i, 2)), pl.BlockSpec((tm, D_MODEL), lambda i: (i, 3)),
                  pl.BlockSpec(w.shape, lambda i: (0, 0)), rows, ROW_SPEC1, ROW_SPEC1],
        out_specs=[rows, rows, rows],
        out_shape=[jax.ShapeDtypeStruct((s, D_MODEL), BF16)] + [jax.ShapeDtypeStruct((s, D_MODEL), F32)] * 2,
        compiler_params=_cparams(dimension_semantics=("parallel",)),
    )(att, rec, proj, proj, w, x, g_post, gate)


def _post_bwd_merge_bwd(f, dy, g_post, gate, res_w, w, att, rec, proj, name):
    s = f.shape[0]
    tm = min(FUSE_TM, s)

    def body(f_ref, dy_ref, gp_ref, gate_ref, w_ref, att_ref, rec_ref, ga_ref, gr_ref,
             df_ref, datt_ref, drec_ref, dga_ref, dgr_ref, sums_ref):
        i = pl.program_id(0)
        dgp, dgate, df = _post_vjp(res_w, gp_ref[...], gate_ref[...], f_ref[...], dy_ref[...])
        dfb = df.astype(BF16)
        df_ref[...] = dfb
        _accumulate(sums_ref, _sum_rows(dgp, dgate), i)
        dmerged = lax.dot_general(dfb, w_ref[...], _DIMS["nt"], preferred_element_type=F32)
        _, vjp = jax.vjp(_merge_fn, att_ref[...], rec_ref[...], ga_ref[...], gr_ref[...])
        for ref, val in zip((datt_ref, drec_ref, dga_ref, dgr_ref), vjp(dmerged)):
            ref[...] = val.astype(ref.dtype)

    rows = pl.BlockSpec((tm, D_MODEL), lambda i: (i, 0))
    return pl.pallas_call(
        body,
        name=name,
        grid=(s // tm,),
        in_specs=[rows, rows, ROW_SPEC1, ROW_SPEC1, pl.BlockSpec(w.shape, lambda i: (0, 0)), rows, rows,
                  pl.BlockSpec((tm, D_MODEL), lambda i: (i, 2)), pl.BlockSpec((tm, D_MODEL), lambda i: (i, 3))],
        out_specs=[rows] * 5 + [SUMS_SPEC1],
        out_shape=[jax.ShapeDtypeStruct((s, D_MODEL), BF16)] * 5 + [SUMS_SHAPE],
        compiler_params=_cparams(dimension_semantics=("arbitrary",)),
    )(f, dy, g_post, gate, w, att, rec, proj, proj)


def _matmul_recin_bwd(drec, w, hs, proj, name, deps=()):
    s = drec.shape[0]
    tm = min(FUSE_TM, s)
    nd = len(deps)

    def body(d_ref, w_ref, hs_ref, yr_ref, *rest):
        dhs_ref, dyr_ref = rest[nd:]
        d = lax.dot_general(d_ref[...], w_ref[...], _DIMS["nt"], preferred_element_type=F32)
        _, vjp = jax.vjp(_recin_fn, hs_ref[...], yr_ref[...])
        dhs, dyr = vjp(d)
        dhs_ref[...] = dhs
        dyr_ref[...] = dyr.astype(dyr_ref.dtype)

    rows = pl.BlockSpec((tm, D_MODEL), lambda i: (i, 0))
    return pl.pallas_call(
        body,
        name=name,
        grid=(s // tm,),
        in_specs=[rows, pl.BlockSpec(w.shape, lambda i: (0, 0)), rows,
                  pl.BlockSpec((tm, D_MODEL), lambda i: (i, 1))] + [ANY_SPEC] * nd,
        out_specs=[rows, rows],
        out_shape=[jax.ShapeDtypeStruct((s, D_MODEL), F32), jax.ShapeDtypeStruct((s, D_MODEL), BF16)],
        compiler_params=_cparams(dimension_semantics=("parallel",)),
    )(drec, w, hs, proj, *deps)


def _post_vjp(res_w, g, gate, f, dy):
    _, vjp = jax.vjp(lambda g, gate, f: _post_fn(res_w, g, gate, f, 0.0), g, gate, f)
    return vjp(dy)


def _post_bwd_up_bwd(f, dy, g_post, gate, res_w, w_down, g, u, name, deps=()):
    s = f.shape[0]
    tm = min(FFN_TM, s)
    nd = len(deps)

    def body(f_ref, dy_ref, gp_ref, gate_ref, wd_ref, g_ref, u_ref, *rest):
        df_ref, dgu_ref, sums_ref, df_s = rest[nd:]
        i = pl.program_id(0)

        @pl.when(pl.program_id(1) == 0)
        def _():
            dgp, dgate, df = _post_vjp(res_w, gp_ref[...], gate_ref[...], f_ref[...], dy_ref[...])
            df_s[...] = df.astype(BF16)
            df_ref[...] = df_s[...]
            _accumulate(sums_ref, _sum_rows(dgp, dgate), i)

        da = lax.dot_general(df_s[...], wd_ref[...], _DIMS["nt"], preferred_element_type=F32)
        _, vjp = jax.vjp(_glu_fn, g_ref[...].astype(F32), u_ref[...].astype(F32))
        dg, du = vjp(da)
        dgu_ref[0] = dg.astype(dgu_ref.dtype)
        dgu_ref[1] = du.astype(dgu_ref.dtype)

    rows = pl.BlockSpec((tm, D_MODEL), lambda i, j: (i, 0))
    blk = pl.BlockSpec((tm, FFN_TF), lambda i, j: (i, j))
    return pl.pallas_call(
        body,
        name=name,
        grid=(s // tm, D_FF // FFN_TF),
        in_specs=[rows, rows, ROW_SPEC2, ROW_SPEC2, pl.BlockSpec((FFN_TF, D_MODEL), lambda i, j: (j, 0)), blk,
                  blk] + [ANY_SPEC] * nd,
        out_specs=[rows, pl.BlockSpec((2, tm, FFN_TF), lambda i, j: (0, i, j)), SUMS_SPEC2],
        out_shape=[jax.ShapeDtypeStruct((s, D_MODEL), BF16), jax.ShapeDtypeStruct((2, s, D_FF), BF16), SUMS_SHAPE],
        scratch_shapes=[pltpu.VMEM((tm, D_MODEL), BF16)],
        compiler_params=_cparams(dimension_semantics=("arbitrary", "arbitrary")),
    )(f, dy, g_post, gate, w_down, g, u, *deps)


def _matmul_pre_bwd(parts, w_t, x, dres, g, shift, scale, name, deps=()):
    s = x.shape[0]
    na, nd = len(parts), len(deps)
    ranges = [p[3] for p in parts]

    def body(*refs):
        a_refs = refs[:na]
        w_ref, x_ref, dres_ref, g_ref, sh_ref, sc_ref = refs[na : na + 6]
        dx_ref, sums_ref = refs[na + 6 + nd :]
        i = pl.program_id(0)
        dh = None
        for a_ref, (r0, r1) in zip(a_refs, ranges):
            p = jnp.dot(a_ref[...], w_ref[r0:r1, :], preferred_element_type=F32)
            dh = p if dh is None else dh + p
        _, vjp = jax.vjp(_pre_fn, g_ref[...], sh_ref[...], sc_ref[...], x_ref[...])
        dg, dsh, dsc, dx = vjp(dh)
        dx_ref[...] = dx + dres_ref[...]
        _accumulate(sums_ref, _sum_rows(dg, dsh, dsc), i)

    tm = parts[0][1][-2]
    rows = pl.BlockSpec((tm, D_MODEL), lambda i: (i, 0))
    return pl.pallas_call(
        body,
        name=name,
        grid=(s // tm,),
        in_specs=[pl.BlockSpec(p[1], p[2]) for p in parts]
        + [pl.BlockSpec(w_t.shape, lambda i: (0, 0)), rows, rows, ROW_SPEC1, ROW_SPEC1, ROW_SPEC1]
        + [ANY_SPEC] * nd,
        out_specs=[rows, SUMS_SPEC1],
        out_shape=[jax.ShapeDtypeStruct((s, D_MODEL), F32), SUMS_SHAPE],
        compiler_params=_cparams(dimension_semantics=("arbitrary",)),
    )(*[p[0] for p in parts], w_t, x, dres, g, shift, scale, *deps)


def _dw_gu(dgu, h, name, deps=(), tk=DW_TK):
    s = h.shape[0]
    tk = min(tk, s)
    nk = s // tk
    half = D_FF // FFN_TF

    def body(a_ref, b_ref, *rest):
        o_ref = rest[len(deps)]
        kk = pl.program_id(1)
        p = lax.dot_general(a_ref[...], b_ref[...], _DIMS["tn"], preferred_element_type=F32)
        if nk == 1:
            o_ref[...] = p.astype(o_ref.dtype)
            return
        acc = rest[len(deps) + 1]

        @pl.when(kk == 0)
        def _():
            acc[...] = p

        @pl.when(kk > 0)
        def _():
            acc[...] += p

        @pl.when(kk == nk - 1)
        def _():
            o_ref[...] = acc[...].astype(o_ref.dtype)

    return pl.pallas_call(
        body,
        name=name,
        grid=(2 * half, nk),
        in_specs=[pl.BlockSpec((None, tk, FFN_TF), lambda i, kk: (i // half, kk, i % half)),
                  pl.BlockSpec((tk, D_MODEL), lambda i, kk: (kk, 0))] + [ANY_SPEC] * len(deps),
        out_specs=pl.BlockSpec((FFN_TF, D_MODEL), lambda i, kk: (i, 0)),
        out_shape=jax.ShapeDtypeStruct((2 * D_FF, D_MODEL), BF16),
        scratch_shapes=[pltpu.VMEM((FFN_TF, D_MODEL), F32)] if nk > 1 else [],
        compiler_params=_cparams(dimension_semantics=("parallel", "arbitrary")),
    )(dgu, h, *deps)


def _shift_down(ext, j, rows):
    return pltpu.roll(ext, j, 0)[V7X_SUBLANES : V7X_SUBLANES + rows]


def _shift_up(ext, j, rows):
    return pltpu.roll(ext, ext.shape[0] - j, 0)[:rows] if j else ext[:rows]


LRU_SLAB = 256
N_SLABS = LRU_WIDTH // LRU_SLAB


def _slab_weights(wa, wx):
    per = LRU_SLAB // LRU_BLOCK
    eye = jnp.eye(per, dtype=wa.dtype)

    def diag(w):
        w4 = w.reshape(N_SLABS, per, LRU_BLOCK, LRU_BLOCK)
        return jnp.einsum("sbkj,bc->sbkcj", w4, eye).reshape(N_SLABS, LRU_SLAB, LRU_SLAB)

    return jnp.concatenate([diag(wa), diag(wx)], axis=2).reshape(LRU_WIDTH, 2 * LRU_SLAB).astype(BF16)


def _slab_cols(v, s):
    lo = s * LRU_SLAB
    return jnp.concatenate([v[:, lo : lo + LRU_SLAB], v[:, LRU_WIDTH + lo : LRU_WIDTH + lo + LRU_SLAB]], axis=1)


def _lru_front(proj, w8, b, w_slab, ba, bx, lam, name):
    def fn(i, steps, w8, b, w_slab, ba, bx, lam, x, halo):
        halo = jnp.where(i > 0, halo, 0.0)
        ext = jnp.concatenate([halo, x], axis=0)
        xc = b + w8[3:4] * x
        for j in (1, 2, 3):
            xc = xc + w8[3 - j : 4 - j] * _shift_down(ext, j, x.shape[0])
        xcb = xc.astype(BF16)
        prods = []
        for s in range(N_SLABS):
            rows = slice(s * LRU_SLAB, (s + 1) * LRU_SLAB)
            prods.append(jnp.dot(xcb[:, rows], w_slab[rows], preferred_element_type=F32))
        pre = jnp.concatenate([p[:, :LRU_SLAB] for p in prods] + [p[:, LRU_SLAB:] for p in prods], axis=1)
        a, u = _gates_fn(ba, bx, lam, pre, xc)
        return xc, pre, a, u

    tiles = [(proj, LRU_WIDTH, 0), (proj, LRU_WIDTH, 0, "prev")]
    outs = [(LRU_WIDTH, F32), (2 * LRU_WIDTH, F32), (LRU_WIDTH, F32), (LRU_WIDTH, F32)]
    return _rowwise(fn, name, [w8, b, w_slab, ba, bx, lam], tiles, outs, with_index=True)


def _lru_back(pre, xc, w_slab, ba, bx, lam, g, h_prev, name, deps=()):
    def fn(w_slab, ba, bx, lam, pre, xc, g, h_prev):
        _, vjp = jax.vjp(_gates_fn, ba, bx, lam, pre, xc)
        dba, dbx, dlam, dpre, dxc = vjp((g * h_prev, g))
        dpre = dpre.astype(BF16)
        back = []
        for s in range(N_SLABS):
            rows = slice(s * LRU_SLAB, (s + 1) * LRU_SLAB)
            back.append(lax.dot_general(_slab_cols(dpre, s), w_slab[rows], _DIMS["nt"], preferred_element_type=F32))
        return dpre, dxc + jnp.concatenate(back, axis=1), _sum_rows(dba, dbx, dlam)

    return _rowwise(fn, name, [w_slab, ba, bx, lam], [pre, xc, g, h_prev],
                    [(2 * LRU_WIDTH, BF16), (LRU_WIDTH, F32)], [(V7X_SUBLANES, LRU_WIDTH)], deps=deps)


def _lru_dw(xc, dpre, name):
    s = xc.shape[0]
    ts = min(512, s)
    steps = s // ts
    per = LRU_SLAB // LRU_BLOCK

    def body(x_ref, d_ref, o_ref, acc):
        i = pl.program_id(0)
        xcb = x_ref[...].astype(BF16)
        d = d_ref[...]
        for sl in range(N_SLABS):
            rows = slice(sl * LRU_SLAB, (sl + 1) * LRU_SLAB)
            p = lax.dot_general(xcb[:, rows], _slab_cols(d, sl), _DIMS["tn"], preferred_element_type=F32)

            @pl.when(i == 0)
            def _(p=p, rows=rows):
                acc[rows, :] = p

            @pl.when(i > 0)
            def _(p=p, rows=rows):
                acc[rows, :] += p

        @pl.when(i == steps - 1)
        def _():
            for half in range(2):
                for n in range(LRU_BLOCKS):
                    r0 = n * LRU_BLOCK
                    c0 = half * LRU_SLAB + (n % per) * LRU_BLOCK
                    o_ref[half, r0 : r0 + LRU_BLOCK, :] = acc[r0 : r0 + LRU_BLOCK, c0 : c0 + LRU_BLOCK]

    return pl.pallas_call(
        body,
        name=name,
        grid=(steps,),
        in_specs=[pl.BlockSpec((ts, LRU_WIDTH), lambda i: (i, 0)), pl.BlockSpec((ts, 2 * LRU_WIDTH), lambda i: (i, 0))],
        out_specs=pl.BlockSpec((2, LRU_WIDTH, LRU_BLOCK), lambda i: (0, 0, 0)),
        out_shape=jax.ShapeDtypeStruct((2, LRU_WIDTH, LRU_BLOCK), F32),
        scratch_shapes=[pltpu.VMEM((LRU_WIDTH, 2 * LRU_SLAB), F32)],
        compiler_params=_cparams(dimension_semantics=("arbitrary",)),
    )(xc, dpre)


def _conv_bwd(proj, w8, d1, name):
    def fn(i, steps, w8, x, halo, d, d1n):
        rows = x.shape[0]
        dn = jnp.where(i < steps - 1, d1n, 0.0)
        halo = jnp.where(i > 0, halo, 0.0)
        dext = jnp.concatenate([d, dn], axis=0)
        xext = jnp.concatenate([halo, x], axis=0)
        dx = w8[3:4] * d
        dw = [None] * 4
        dw[3] = _rowsum(d * x)
        for k in (1, 2, 3):
            dx = dx + w8[3 - k : 4 - k] * _shift_up(dext, k, rows)
            dw[3 - k] = _rowsum(d * _shift_down(xext, k, rows))
        return dx, _sum_rows(*dw, _rowsum(d))

    tiles = [(proj, LRU_WIDTH, 0), (proj, LRU_WIDTH, 0, "prev"), d1, (d1, LRU_WIDTH, 0, "next")]
    return _rowwise(fn, name, [w8], tiles, [(LRU_WIDTH, BF16)], [(V7X_SUBLANES, LRU_WIDTH)], with_index=True)


SCAN_ROWS = 512


def _block_scan(a, b, row, reverse):
    for d in (1, 2, 4):
        if reverse:
            shift, keep = V7X_SUBLANES - d, row < V7X_SUBLANES - d
        else:
            shift, keep = d, row >= d
        a_s = pltpu.roll(a, shift, 0)
        b_s = pltpu.roll(b, shift, 0)
        b = jnp.where(keep, a * b_s + b, b)
        a = jnp.where(keep, a * a_s, a)
    return a, b


def _scan_fwd(a, u, proj, name):
    s, w = a.shape
    ts = min(SCAN_ROWS, s)
    sub = ts // V7X_SUBLANES

    def body(a_ref, u_ref, yr_ref, h_ref, hp_ref, rec_ref, carry):
        @pl.when(pl.program_id(0) == 0)
        def _():
            carry[...] = jnp.zeros_like(carry)

        row = lax.broadcasted_iota(jnp.int32, (V7X_SUBLANES, w), 0)

        def step(j, c):
            rows = pl.ds(pl.multiple_of(j * V7X_SUBLANES, V7X_SUBLANES), V7X_SUBLANES)
            pa, pb = _block_scan(a_ref[rows, :], u_ref[rows, :], row, False)
            h = pb + pa * c
            h_ref[rows, :] = h
            hp_ref[rows, :] = jnp.where(row >= 1, pltpu.roll(h, 1, 0), c)
            return jnp.broadcast_to(h[V7X_SUBLANES - 1 :], (V7X_SUBLANES, w))

        carry[...] = lax.fori_loop(0, sub, step, carry[...])
        rec_ref[...] = _recin_fn(h_ref[...], yr_ref[...]).astype(rec_ref.dtype)

    spec = pl.BlockSpec((ts, w), lambda i: (i, 0))
    return pl.pallas_call(
        body,
        name=name,
        grid=(s // ts,),
        in_specs=[spec, spec, pl.BlockSpec((ts, w), lambda i: (i, 1))],
        out_specs=[spec, spec, spec],
        out_shape=[jax.ShapeDtypeStruct((s, w), F32)] * 2 + [jax.ShapeDtypeStruct((s, w), BF16)],
        scratch_shapes=[pltpu.VMEM((V7X_SUBLANES, w), F32)],
        compiler_params=_cparams(dimension_semantics=("arbitrary",)),
    )(a, u, proj)


def _scan_bwd(a, dh, name):
    s, w = a.shape
    ts = min(SCAN_ROWS, s)
    sub = ts // V7X_SUBLANES
    steps = s // ts

    def body(a_ref, d_ref, g_ref, carry):
        @pl.when(pl.program_id(0) == 0)
        def _():
            carry[...] = jnp.zeros_like(carry)

        row = lax.broadcasted_iota(jnp.int32, (V7X_SUBLANES, w), 0)

        def step(jj, c):
            j = sub - 1 - jj
            rows = pl.ds(pl.multiple_of(j * V7X_SUBLANES, V7X_SUBLANES), V7X_SUBLANES)
            av, dv = a_ref[rows, :], d_ref[rows, :]
            pa, pb = _block_scan(av, av * dv, row, True)
            big = pb + pa * c
            g_ref[rows, :] = dv + jnp.where(row < V7X_SUBLANES - 1, pltpu.roll(big, V7X_SUBLANES - 1, 0), c)
            return jnp.broadcast_to(big[:1], (V7X_SUBLANES, w))

        carry[...] = lax.fori_loop(0, sub, step, carry[...])

    spec = pl.BlockSpec((ts, w), lambda i: (steps - 1 - i, 0))
    return pl.pallas_call(
        body,
        name=name,
        grid=(steps,),
        in_specs=[spec, spec],
        out_specs=spec,
        out_shape=jax.ShapeDtypeStruct((s, w), F32),
        scratch_shapes=[pltpu.VMEM((V7X_SUBLANES, w), F32)],
        compiler_params=_cparams(dimension_semantics=("arbitrary",)),
    )(a, dh)


SKEW = 4 * ATT_TQ


def _skew_onehot():
    t = np.arange(SKEW)
    diag = np.where(t < 3 * ATT_TQ, -t, SKEW - t)
    idx = np.clip(diag + LEFT_CHUNKS * CHUNK, -MAX_REL, MAX_REL) + MAX_REL
    hit = (idx[:, None] == np.arange(2 * MAX_REL + 1)[None, :]) & (t[:, None] != 3 * ATT_TQ)
    return hit.astype(np.float32)


def _bias_tile(rel_bias, name):
    per_t = jnp.dot(rel_bias, jnp.asarray(_skew_onehot()).T, precision=lax.Precision.HIGHEST)
    win = 3 * ATT_TQ

    def body(t_ref, o_ref):
        tile = pltpu.roll(jnp.broadcast_to(t_ref[0], (ATT_TQ, SKEW)), 0, 1, stride=1, stride_axis=0)[:, :win]
        qc = lax.broadcasted_iota(jnp.int32, (ATT_TQ, win), 0) // CHUNK
        kpos = lax.broadcasted_iota(jnp.int32, (ATT_TQ, win), 1)
        band = (kpos // CHUNK >= qc) & (kpos // CHUNK <= qc + LEFT_CHUNKS)
        for v in range(3):
            o_ref[v, 0] = jnp.where(band & (kpos >= (2 - v) * ATT_TQ), tile, NEG)

    return pl.pallas_call(
        body,
        name=name,
        grid=(ATT_HEADS,),
        in_specs=[pl.BlockSpec((1, 1, SKEW), lambda h: (h, 0, 0))],
        out_specs=pl.BlockSpec((3, 1, ATT_TQ, win), lambda h: (0, h, 0, 0)),
        out_shape=jax.ShapeDtypeStruct((3, ATT_HEADS, ATT_TQ, win), F32),
        compiler_params=_cparams(dimension_semantics=("parallel",)),
    )(per_t.reshape(ATT_HEADS, 1, SKEW))


def _bias_grad(dbias, name):
    win = 3 * ATT_TQ

    def body(d_ref, o_ref):
        d = jnp.concatenate([d_ref[0], jnp.zeros((ATT_TQ, SKEW - win), F32)], axis=1)
        r = lax.broadcasted_iota(jnp.int32, (ATT_TQ, ATT_TQ), 0)
        c = lax.broadcasted_iota(jnp.int32, (ATT_TQ, ATT_TQ), 1)
        flip = (r + c == ATT_TQ - 1).astype(F32)
        d = jnp.dot(flip, d, preferred_element_type=F32, precision=lax.Precision.HIGHEST)
        o_ref[0] = jnp.sum(pltpu.roll(d, SKEW - (ATT_TQ - 1), 1, stride=1, stride_axis=0), axis=0, keepdims=True)

    per_t = pl.pallas_call(
        body,
        name=name,
        grid=(ATT_HEADS,),
        in_specs=[pl.BlockSpec((1, ATT_TQ, win), lambda h: (h, 0, 0))],
        out_specs=pl.BlockSpec((1, 1, SKEW), lambda h: (h, 0, 0)),
        out_shape=jax.ShapeDtypeStruct((ATT_HEADS, 1, SKEW), F32),
        compiler_params=_cparams(dimension_semantics=("parallel",)),
    )(dbias)
    return jnp.dot(per_t.reshape(ATT_HEADS, SKEW), jnp.asarray(_skew_onehot()), precision=lax.Precision.HIGHEST)


ATT_STEP_HEADS = ATT_HEADS
ATT_STEP_COLS = ATT_STEP_HEADS * ATT_HEAD_DIM


def _attn_specs(nt):
    qb, kb, vb = OFF_Q // ATT_STEP_COLS, OFF_K // ATT_STEP_COLS, OFF_V // ATT_STEP_COLS
    blk = (ATT_TQ, ATT_STEP_COLS)

    def qmap(base):
        return lambda hp, m: (jnp.minimum(m, nt - 1), base + hp)

    def wmap(base, back):
        return lambda hp, m: (jnp.clip(m - back, 0, nt - 1), base + hp)

    specs = [pl.BlockSpec(blk, qmap(qb))]
    specs += [pl.BlockSpec(blk, wmap(kb, back)) for back in (2, 1, 0)]
    specs += [pl.BlockSpec(blk, wmap(vb, back)) for back in (2, 1, 0)]
    return specs


ATT_SCALE = ATT_HEAD_DIM**-0.5


def _attn_exp(qh, kh, bias):
    s = lax.dot_general(qh, kh, _DIMS["nt"], preferred_element_type=F32) + bias
    e = jnp.exp(s - jnp.max(s, axis=-1, keepdims=True))
    return e, jnp.sum(e, axis=-1, keepdims=True)


def _attn_window(k0, k1, k2, v0, v1, v2):
    k = jnp.concatenate([k0[...], k1[...], k2[...]], axis=0).astype(BF16)
    v = jnp.concatenate([v0[...], v1[...], v2[...]], axis=0).astype(BF16)
    return k, v


def _bias_spec():
    return pl.BlockSpec((1, ATT_STEP_HEADS, ATT_TQ, 3 * ATT_TQ), lambda hp, m: (jnp.minimum(m, 2), hp, 0, 0))


def _attn_fwd(proj, bias, name):
    s = proj.shape[0]
    nt = s // ATT_TQ

    def body(q_ref, k0, k1, k2, v0, v1, v2, b_ref, o_ref):
        k, v = _attn_window(k0, k1, k2, v0, v1, v2)
        q = (q_ref[...] * ATT_SCALE).astype(BF16)
        for hh in range(ATT_STEP_HEADS):
            cols = slice(hh * ATT_HEAD_DIM, (hh + 1) * ATT_HEAD_DIM)
            e, total = _attn_exp(q[:, cols], k[:, cols], b_ref[0, hh])
            o = jnp.dot(e.astype(BF16), v[:, cols], preferred_element_type=F32) / total
            o_ref[:, cols] = o.astype(o_ref.dtype)

    specs = _attn_specs(nt) + [_bias_spec()]
    return pl.pallas_call(
        body,
        name=name,
        grid=(ATT_HEADS // ATT_STEP_HEADS, nt),
        in_specs=specs,
        out_specs=pl.BlockSpec((ATT_TQ, ATT_STEP_COLS), lambda hp, m: (m, hp)),
        out_shape=jax.ShapeDtypeStruct((s, ATT_WIDTH), BF16),
        compiler_params=_cparams(dimension_semantics=("parallel", "arbitrary")),
    )(proj, proj, proj, proj, proj, proj, proj, bias)


def _attn_bwd(proj, bias, do, name):
    s = proj.shape[0]
    nt = s // ATT_TQ
    win = 3 * ATT_TQ

    def body(q_ref, k0, k1, k2, v0, v1, v2, do_ref, b_ref, dq_ref, dk_ref, dv_ref, db_ref, dk_acc, dv_acc):
        m = pl.program_id(1)

        @pl.when(m == 0)
        def _():
            dk_acc[...] = jnp.zeros_like(dk_acc)
            dv_acc[...] = jnp.zeros_like(dv_acc)
            db_ref[...] = jnp.zeros_like(db_ref)

        @pl.when(m < nt)
        def _():
            k, v = _attn_window(k0, k1, k2, v0, v1, v2)
            q = (q_ref[...] * ATT_SCALE).astype(BF16)
            dout = do_ref[...]
            for hh in range(ATT_STEP_HEADS):
                cols = slice(hh * ATT_HEAD_DIM, (hh + 1) * ATT_HEAD_DIM)
                qh, kh, vh, doh = q[:, cols], k[:, cols], v[:, cols], dout[:, cols]
                e, total = _attn_exp(qh, kh, b_ref[0, hh])
                p = e / total
                dvh = lax.dot_general(p.astype(BF16), doh, _DIMS["tn"], preferred_element_type=F32)
                dp = lax.dot_general(doh, vh, _DIMS["nt"], preferred_element_type=F32)
                ds = p * (dp - jnp.sum(dp * p, axis=-1, keepdims=True))
                db_ref[hh] += ds
                dsb = ds.astype(BF16)
                dqh = jnp.dot(dsb, kh, preferred_element_type=F32) * ATT_SCALE
                dkh = lax.dot_general(dsb, qh, _DIMS["tn"], preferred_element_type=F32)
                dq_ref[:, cols] = dqh.astype(dq_ref.dtype)
                dk_acc[:, cols] += dkh
                dv_acc[:, cols] += dvh

        dk_ref[...] = dk_acc[:ATT_TQ].astype(dk_ref.dtype)
        dv_ref[...] = dv_acc[:ATT_TQ].astype(dv_ref.dtype)
        for acc in (dk_acc, dv_acc):
            rest = acc[ATT_TQ:]
            acc[: win - ATT_TQ] = rest
            acc[win - ATT_TQ :] = jnp.zeros((ATT_TQ, ATT_STEP_COLS), F32)

    blk = (ATT_TQ, ATT_STEP_COLS)
    specs = _attn_specs(nt)
    specs.append(pl.BlockSpec(blk, lambda hp, m: (jnp.minimum(m, nt - 1), hp)))
    specs.append(_bias_spec())
    done = lambda hp, m: (jnp.maximum(m - 2, 0), hp)
    out_specs = [
        pl.BlockSpec(blk, lambda hp, m: (jnp.minimum(m, nt - 1), hp)),
        pl.BlockSpec(blk, done),
        pl.BlockSpec(blk, done),
        pl.BlockSpec((ATT_STEP_HEADS, ATT_TQ, win), lambda hp, m: (hp, 0, 0)),
    ]
    out_shape = [jax.ShapeDtypeStruct((s, ATT_WIDTH), BF16)] * 3
    out_shape.append(jax.ShapeDtypeStruct((ATT_HEADS, ATT_TQ, win), F32))
    return pl.pallas_call(
        body,
        name=name,
        grid=(ATT_HEADS // ATT_STEP_HEADS, nt + 2),
        in_specs=specs,
        out_specs=out_specs,
        out_shape=out_shape,
        scratch_shapes=[pltpu.VMEM((win, ATT_STEP_COLS), F32), pltpu.VMEM((win, ATT_STEP_COLS), F32)],
        compiler_params=_cparams(dimension_semantics=("arbitrary", "arbitrary")),
    )(proj, proj, proj, proj, proj, proj, proj, do, bias)


def _ada_fwd(c_all, w, name):
    def body(c_ref, w_ref, o_ref):
        act = _silu(c_ref[...]).astype(BF16)
        o_ref[...] = jnp.dot(act, w_ref[...].astype(BF16), preferred_element_type=F32)

    return pl.pallas_call(
        body, name=name, out_shape=jax.ShapeDtypeStruct((c_all.shape[0], w.shape[1]), F32), compiler_params=_cparams()
    )(c_all, w)


def _ada_bwd(c_all, dmod, name):
    def body(c_ref, d_ref, o_ref):
        act = _silu(c_ref[...])
        o_ref[...] = lax.dot_general(act, d_ref[...], _DIMS["tn"], preferred_element_type=F32,
                                     precision=lax.Precision.HIGHEST)

    return pl.pallas_call(
        body, name=name, out_shape=jax.ShapeDtypeStruct((c_all.shape[1], dmod.shape[1]), F32), compiler_params=_cparams()
    )(c_all, dmod)


def _adamw_parts(landed, sent, me, w, m, v, name, rows=256):
    r, c = w.shape
    tr = _pick(r, rows, 16)

    def body(me_ref, g_ref, own_ref, w_ref, m_ref, v_ref, go_ref, d_ref, mo_ref, vo_ref):
        mine = me_ref[0]
        grad = jnp.zeros((tr, c), F32)
        for d in range(N_DEV):
            grad = grad + jnp.where(mine == d, own_ref[0], g_ref[d]).astype(F32)
        _adamw_update(grad, w_ref, m_ref, v_ref, go_ref, d_ref, mo_ref, vo_ref)

    spec = pl.BlockSpec((tr, c), lambda i, me_ref: (i, 0))
    return pl.pallas_call(
        body,
        name=name,
        grid_spec=pltpu.PrefetchScalarGridSpec(
            num_scalar_prefetch=1,
            grid=(r // tr,),
            in_specs=[pl.BlockSpec((N_DEV, tr, c), lambda i, me_ref: (0, i, 0)),
                      pl.BlockSpec((1, tr, c), lambda i, me_ref: (me_ref[0], i, 0)), spec, spec, spec],
            out_specs=[spec] * 4,
        ),
        out_shape=[jax.ShapeDtypeStruct((r, c), F32)] * 4,
        compiler_params=_cparams(dimension_semantics=("parallel",)),
    )(me.reshape(1).astype(jnp.int32), landed, sent, w, m, v)


def _adamw_update(grad, w_ref, m_ref, v_ref, go_ref, d_ref, mo_ref, vo_ref):
    m2 = ADAM_B1 * m_ref[...] + (1.0 - ADAM_B1) * grad
    v2 = ADAM_B2 * v_ref[...] + (1.0 - ADAM_B2) * (grad * grad)
    m_hat = m2 / (1.0 - ADAM_B1**ADAM_STEP)
    v_hat = v2 / (1.0 - ADAM_B2**ADAM_STEP)
    go_ref[...] = grad
    d_ref[...] = -ADAM_LR * (m_hat / (jnp.sqrt(v_hat) + ADAM_EPS) + ADAM_WD * w_ref[...])
    mo_ref[...] = m2
    vo_ref[...] = v2


def _adamw(g, w, m, v, name, rows=256):
    r, c = w.shape
    tr = _pick(r, rows, 16)

    def body(g_ref, w_ref, m_ref, v_ref, go_ref, d_ref, mo_ref, vo_ref):
        _adamw_update(g_ref[...], w_ref, m_ref, v_ref, go_ref, d_ref, mo_ref, vo_ref)

    spec = pl.BlockSpec((tr, c), lambda i: (i, 0))
    return pl.pallas_call(
        body,
        name=name,
        grid=(r // tr,),
        in_specs=[spec, spec, spec, spec],
        out_specs=[spec] * 4,
        out_shape=[jax.ShapeDtypeStruct((r, c), F32)] * 4,
        compiler_params=_cparams(dimension_semantics=("parallel",)),
    )(g, w, m, v)


def _sum_parts(parts, name):
    def body(p_ref, o_ref):
        acc = p_ref[0]
        for d in range(1, N_DEV):
            acc = acc + p_ref[d]
        o_ref[...] = acc

    return pl.pallas_call(
        body, name=name, out_shape=jax.ShapeDtypeStruct(parts.shape[1:], F32), compiler_params=_cparams()
    )(parts)


def _place():
    x, y, c = lax.axis_index("x"), lax.axis_index("y"), lax.axis_index("c")
    return x, y, c


def _dev_index(p):
    return 4 * p[0] + 2 * p[1] + p[2]


def _allgather_vmem(shard, name):
    m_per, n = shard.shape

    def body(x_ref, out_ref, send_sems, recv_sems, local_sem):
        x, y, c = _place()
        me, sibling = (x, y, c), (x, y, 1 - c)
        chips = [(1 - x, y), (x, 1 - y), (1 - x, 1 - y)]

        def rows(p):
            return out_ref.at[pl.ds(_dev_index(p) * m_per, m_per), :]

        def copy(k, block, to, src=None):
            return pltpu.make_async_remote_copy(
                src_ref=rows(block) if src is None else src, dst_ref=rows(block),
                send_sem=send_sems.at[k], recv_sem=recv_sems.at[k], device_id=to, device_id_type=MESH)

        mine = pltpu.make_async_copy(x_ref, rows(me), local_sem)
        mine.start()
        first = [copy(0, me, sibling, src=x_ref)]
        first += [copy(1 + j, me, (*chip, c), src=x_ref) for j, chip in enumerate(chips)]
        for cp in first:
            cp.start()
        passed = [copy(4 + j, (*chip, c), sibling) for j, chip in enumerate(chips)]
        for j, chip in enumerate(chips):
            copy(1 + j, (*chip, c), me).wait_recv()
            passed[j].start()
        copy(0, sibling, me).wait_recv()
        for j, chip in enumerate(chips):
            copy(4 + j, (*chip, 1 - c), me).wait_recv()
        for cp in first + passed:
            cp.wait_send()
        mine.wait()

    return pl.pallas_call(
        body,
        name=name,
        out_shape=jax.ShapeDtypeStruct((N_DEV * m_per, n), shard.dtype),
        in_specs=[pl.BlockSpec(memory_space=pltpu.VMEM)],
        out_specs=pl.BlockSpec(memory_space=pltpu.VMEM),
        scratch_shapes=[pltpu.SemaphoreType.DMA((7,)), pltpu.SemaphoreType.DMA((7,)), pltpu.SemaphoreType.DMA],
        compiler_params=_cparams(),
    )(shard)


def _allgather_hbm(shards, name):
    n = len(shards)

    def body(*refs):
        ins, outs = refs[:n], refs[n : 2 * n]
        send_sems, recv_sems, local_sems = refs[2 * n :]
        x, y, c = _place()
        me, sibling = (x, y, c), (x, y, 1 - c)
        chips = [(1 - x, y), (x, 1 - y), (1 - x, 1 - y)]

        def copy(a, k, block, to, src=None):
            dst = outs[a].at[_dev_index(block)]
            return pltpu.make_async_remote_copy(
                src_ref=dst if src is None else src, dst_ref=dst,
                send_sem=send_sems.at[a * 7 + k], recv_sem=recv_sems.at[a * 7 + k], device_id=to, device_id_type=MESH)

        mine = [pltpu.make_async_copy(ins[a], outs[a].at[_dev_index(me)], local_sems.at[a]) for a in range(n)]
        for cp in mine:
            cp.start()
        first = []
        for a in range(n):
            first.append(copy(a, 0, me, sibling, src=ins[a]))
            first += [copy(a, 1 + j, me, (*chip, c), src=ins[a]) for j, chip in enumerate(chips)]
        for cp in first:
            cp.start()
        passed = []
        for j, chip in enumerate(chips):
            for a in range(n):
                copy(a, 1 + j, (*chip, c), me).wait_recv()
                cp = copy(a, 4 + j, (*chip, c), sibling)
                cp.start()
                passed.append(cp)
        for a in range(n):
            copy(a, 0, sibling, me).wait_recv()
        for j, chip in enumerate(chips):
            for a in range(n):
                copy(a, 4 + j, (*chip, 1 - c), me).wait_recv()
        for cp in first + passed:
            cp.wait_send()
        for cp in mine:
            cp.wait()

    any_spec = pl.BlockSpec(memory_space=pl.ANY)
    return pl.pallas_call(
        body,
        name=name,
        out_shape=[jax.ShapeDtypeStruct((N_DEV, *s.shape), s.dtype) for s in shards],
        in_specs=[any_spec] * n,
        out_specs=[any_spec] * n,
        scratch_shapes=[pltpu.SemaphoreType.DMA((7 * n,)), pltpu.SemaphoreType.DMA((7 * n,)),
                        pltpu.SemaphoreType.DMA((n,))],
        compiler_params=_cparams(),
    )(*shards)


HBM_SPEC = pl.BlockSpec(memory_space=pltpu.HBM)
SEM_SPEC = pl.BlockSpec(memory_space=pltpu.SEMAPHORE)
EFFECT = pltpu.SideEffectType.DATAFLOW_SIDE_EFFECTING


def _peers(x, y, c):
    return [(1 - x if k & 4 else x, 1 - y if k & 2 else y, 1 - c if k & 1 else c) for k in range(1, N_DEV)]


def _push_peers(mode, x, y, c):
    if mode == "all":
        return _peers(x, y, c)
    return [(x, y, 1 - c), (1 - x, y, c), (x, 1 - y, c), (1 - x, 1 - y, c)]


def _push_start(groups, sliced, name, after=(), modes=None):
    flat = [b for g in groups for b in g]
    n, ng = len(flat), len(groups)
    sizes = [len(g) for g in groups]
    modes = modes or ["all"] * ng
    fan = [len(_push_peers(m, 0, 0, 0)) for m in modes]
    per = 2 if sliced else 3
    lands = [lax.empty(b.shape if sliced else (N_DEV, *b.shape), b.dtype) for b in flat]

    def body(*refs):
        ins, lnd = refs[:n], refs[n : 2 * n]
        sems = refs[2 * n + len(after) : 2 * n + len(after) + per * ng]
        token = refs[-1]
        x, y, c = _place()
        me = _dev_index((x, y, c))
        if not sliced:
            first = 0
            for gi, size in enumerate(sizes):
                for j in range(first, first + size):
                    pltpu.make_async_copy(ins[j], lnd[j].at[me], sems[per * gi + 2].at[j - first]).start()
                first += size
        first = 0
        for gi, size in enumerate(sizes):
            for k, peer in enumerate(_push_peers(modes[gi], x, y, c)):
                for j in range(first, first + size):
                    sem = (j - first) * fan[gi] + k
                    pltpu.make_async_remote_copy(
                        src_ref=ins[j].at[_dev_index(peer)] if sliced else ins[j], dst_ref=lnd[j].at[me],
                        send_sem=sems[per * gi].at[sem], recv_sem=sems[per * gi + 1].at[sem],
                        device_id=peer, device_id_type=MESH).start()
            first += size
        token[...] = jnp.zeros_like(token)

    out_shape = []
    for size, width in zip(sizes, fan):
        out_shape += [pltpu.SemaphoreType.DMA((width * size,)), pltpu.SemaphoreType.DMA((width * size,))]
        out_shape += [] if sliced else [pltpu.SemaphoreType.DMA((size,))]
    out_shape += [pltpu.HBM(b.shape, b.dtype) for b in flat + lands]
    out_shape.append(jax.ShapeDtypeStruct((V7X_SUBLANES, V7X_LANES), F32))
    res = pl.pallas_call(
        body,
        name=name,
        out_shape=tuple(out_shape),
        in_specs=[HBM_SPEC] * (2 * n) + [ANY_SPEC] * len(after),
        out_specs=tuple([SEM_SPEC] * (per * ng) + [HBM_SPEC] * (2 * n) + [pl.BlockSpec(memory_space=pltpu.VMEM)]),
        input_output_aliases={i: per * ng + i for i in range(2 * n)},
        compiler_params=pltpu.CompilerParams(has_side_effects=EFFECT),
    )(*[pltpu.with_memory_space_constraint(b, pltpu.HBM) for b in flat + lands], *after)
    sems, thru, token = res[: per * ng], res[per * ng : per * ng + 2 * n], res[-1]
    out, first = [], 0
    for gi, size in enumerate(sizes):
        out.append((sems[per * gi], sems[per * gi + 1], list(thru[first : first + size]),
                    list(thru[n + first : n + first + size]), None if sliced else sems[per * gi + 2]))
        first += size
    return out, token


def _push_wait(started, sliced, after, name, mode="all"):
    send_sems, recv_sems, bufs, lands, own_sems = started
    n = len(bufs)
    fan = len(_push_peers(mode, 0, 0, 0))
    own = [] if own_sems is None else [own_sems]

    def body(*refs):
        ins, lnd = refs[:n], refs[n : 2 * n]
        send_ref, recv_ref = refs[2 * n], refs[2 * n + 1]
        x, y, c = _place()
        for k, peer in enumerate(_push_peers(mode, x, y, c)):
            for j in range(n):
                cp = pltpu.make_async_remote_copy(
                    src_ref=ins[j].at[_dev_index(peer)] if sliced else ins[j], dst_ref=lnd[j].at[_dev_index(peer)],
                    send_sem=send_ref.at[j * fan + k], recv_sem=recv_ref.at[j * fan + k],
                    device_id=peer, device_id_type=MESH)
                cp.wait_send()
                cp.wait_recv()
        if own:
            for j in range(n):
                pltpu.make_async_copy(ins[j], lnd[j].at[_dev_index((x, y, c))], refs[2 * n + 2].at[j]).wait()

    res = pl.pallas_call(
        body,
        name=name,
        out_shape=tuple(pltpu.HBM(b.shape, b.dtype) for b in bufs + lands),
        in_specs=[HBM_SPEC] * (2 * n) + [SEM_SPEC] * (2 + len(own)) + [pl.BlockSpec(memory_space=pl.ANY)],
        out_specs=tuple([HBM_SPEC] * (2 * n)),
        input_output_aliases={i: i for i in range(2 * n)},
        compiler_params=pltpu.CompilerParams(has_side_effects=EFFECT),
    )(*bufs, *lands, send_sems, recv_sems, *own, after)
    return list(res[:n]), list(res[n:])


def _forward_copies(lnd, send_ref, recv_ref, incoming):
    x, y, c = _place()
    copies = []
    for k, chip in enumerate([(1 - x, y), (x, 1 - y), (1 - x, 1 - y)]):
        mine, theirs = _dev_index((*chip, c)), _dev_index((*chip, 1 - c))
        for j, ref in enumerate(lnd):
            copies.append(pltpu.make_async_remote_copy(
                src_ref=ref.at[mine], dst_ref=ref.at[theirs if incoming else mine],
                send_sem=send_ref.at[j * 3 + k], recv_sem=recv_ref.at[j * 3 + k],
                device_id=(x, y, 1 - c), device_id_type=MESH))
    return copies


def _forward_start(lands, name):
    n = len(lands)

    def body(*refs):
        for cp in _forward_copies(refs[:n], refs[n], refs[n + 1], False):
            cp.start()

    res = pl.pallas_call(
        body,
        name=name,
        out_shape=(pltpu.SemaphoreType.DMA((3 * n,)), pltpu.SemaphoreType.DMA((3 * n,)),
                   *[pltpu.HBM(b.shape, b.dtype) for b in lands]),
        in_specs=[HBM_SPEC] * n,
        out_specs=(SEM_SPEC, SEM_SPEC, *[HBM_SPEC] * n),
        input_output_aliases={i: 2 + i for i in range(n)},
        compiler_params=pltpu.CompilerParams(has_side_effects=EFFECT),
    )(*[pltpu.with_memory_space_constraint(b, pltpu.HBM) for b in lands])
    return res[0], res[1], list(res[2:])


def _forward_wait(started, after, name):
    send_sems, recv_sems, lands = started
    n = len(lands)

    def body(*refs):
        for cp in _forward_copies(refs[:n], refs[n], refs[n + 1], True):
            cp.wait_send()
            cp.wait_recv()

    res = pl.pallas_call(
        body,
        name=name,
        out_shape=tuple(pltpu.HBM(b.shape, b.dtype) for b in lands),
        in_specs=[HBM_SPEC] * n + [SEM_SPEC, SEM_SPEC, pl.BlockSpec(memory_space=pl.ANY)],
        out_specs=tuple([HBM_SPEC] * n),
        input_output_aliases={i: i for i in range(n)},
        compiler_params=pltpu.CompilerParams(has_side_effects=EFFECT),
    )(*lands, send_sems, recv_sems, after)
    return list(res)


def _cols_full(g):
    return jnp.transpose(g, (1, 0, 2)).reshape(g.shape[1], -1)


def _rows_full(g):
    return g.reshape(-1, g.shape[2])


def _cols_parts(full, n=N_DEV):
    r = full.shape[0]
    return jnp.transpose(full.reshape(r, n, -1), (1, 0, 2)).astype(BF16)


def _rows_parts(full):
    return full.reshape(N_DEV, -1, full.shape[1]).astype(BF16)


def _pad_rows(v, rows):
    flat = v.reshape(-1)
    return jnp.pad(flat, (0, rows * D_MODEL - flat.shape[0])).reshape(rows, D_MODEL)


def _my_cols(full, me, width):
    return lax.dynamic_slice_in_dim(full, me * width, width, axis=full.ndim - 1)


def kernel(x, c, w_ada, b_ada, norm_pre, norm_post, ffn1_w_gu, ffn1_w_down, w_in, rel_bias, conv_w, conv_b, lru_wa, lru_ba, lru_wx, lru_bx, lru_lambda, w_att_o, w_rec_o, w_out, ffn2_w_gu, ffn2_w_down, loss_target, m_w_ada, m_b_ada, m_norm_pre, m_norm_post, m_ffn1_w_gu, m_ffn1_w_down, m_w_in, m_rel_bias, m_conv_w, m_conv_b, m_lru_wa, m_lru_ba, m_lru_wx, m_lru_bx, m_lru_lambda, m_w_att_o, m_w_rec_o, m_w_out, m_ffn2_w_gu, m_ffn2_w_down, v_w_ada, v_b_ada, v_norm_pre, v_norm_post, v_ffn1_w_gu, v_ffn1_w_down, v_w_in, v_rel_bias, v_conv_w, v_conv_b, v_lru_wa, v_lru_ba, v_lru_wx, v_lru_bx, v_lru_lambda, v_w_att_o, v_w_rec_o, v_w_out, v_ffn2_w_gu, v_ffn2_w_down):
    weights = dict(w_ada=w_ada, b_ada=b_ada, norm_pre=norm_pre, norm_post=norm_post, ffn1_w_gu=ffn1_w_gu,
                   ffn1_w_down=ffn1_w_down, w_in=w_in, rel_bias=rel_bias, conv_w=conv_w, conv_b=conv_b,
                   lru_wa=lru_wa, lru_ba=lru_ba, lru_wx=lru_wx, lru_bx=lru_bx, lru_lambda=lru_lambda,
                   w_att_o=w_att_o, w_rec_o=w_rec_o, w_out=w_out, ffn2_w_gu=ffn2_w_gu, ffn2_w_down=ffn2_w_down)
    mom1 = dict(w_ada=m_w_ada, b_ada=m_b_ada, norm_pre=m_norm_pre, norm_post=m_norm_post, ffn1_w_gu=m_ffn1_w_gu,
                ffn1_w_down=m_ffn1_w_down, w_in=m_w_in, rel_bias=m_rel_bias, conv_w=m_conv_w, conv_b=m_conv_b,
                lru_wa=m_lru_wa, lru_ba=m_lru_ba, lru_wx=m_lru_wx, lru_bx=m_lru_bx, lru_lambda=m_lru_lambda,
                w_att_o=m_w_att_o, w_rec_o=m_w_rec_o, w_out=m_w_out, ffn2_w_gu=m_ffn2_w_gu, ffn2_w_down=m_ffn2_w_down)
    mom2 = dict(w_ada=v_w_ada, b_ada=v_b_ada, norm_pre=v_norm_pre, norm_post=v_norm_post, ffn1_w_gu=v_ffn1_w_gu,
                ffn1_w_down=v_ffn1_w_down, w_in=v_w_in, rel_bias=v_rel_bias, conv_w=v_conv_w, conv_b=v_conv_b,
                lru_wa=v_lru_wa, lru_ba=v_lru_ba, lru_wx=v_lru_wx, lru_bx=v_lru_bx, lru_lambda=v_lru_lambda,
                w_att_o=v_w_att_o, w_rec_o=v_w_rec_o, w_out=v_w_out, ffn2_w_gu=v_ffn2_w_gu, ffn2_w_down=v_ffn2_w_down)
    order = list(weights)
    big = ["ffn1_w_gu", "ffn1_w_down", "w_in", "w_att_o", "w_rec_o", "w_out", "ffn2_w_gu", "ffn2_w_down"]
    col_sharded = {"ffn1_w_gu", "w_in", "w_att_o", "ffn2_w_gu"}
    small = ["b_ada", "norm_pre", "norm_post", "rel_bias", "conv_w", "conv_b", "lru_wa", "lru_ba", "lru_wx",
             "lru_bx", "lru_lambda"]

    xi, yi, ci = _place()
    me = _dev_index((xi, yi, ci))
    x0 = x[0]
    target = loss_target[0]
    fuse_tm = min(FUSE_TM, x0.shape[0])

    transposed = {"ffn1_w_gu", "w_in", "ffn2_w_gu"}
    local = lambda n, arr: jnp.transpose(arr[0]) if n in transposed else arr[0]
    shards = {n: local(n, weights[n]).astype(BF16) for n in big}
    full_of = lambda n, g: _cols_full(g) if n == "w_att_o" else _rows_full(g)

    pack = jnp.concatenate([c.reshape(-1), norm_pre.reshape(-1), norm_post.reshape(-1), conv_w.reshape(-1)])
    pack = jnp.pad(pack, (0, 3072 - pack.shape[0])).reshape(8, 384)
    got, w1_gu = _allgather_hbm([pack, shards["ffn1_w_gu"]], "gather_first")
    got = got.reshape(N_DEV, 3072)
    c_all = got[:, :1024]
    unshard = lambda blk, rows: jnp.transpose(blk.reshape(N_DEV, rows, 128), (1, 0, 2)).reshape(rows, D_MODEL)
    g_pre = unshard(got[:, 1024:1408], 3)
    g_post = unshard(got[:, 1408:1792], 3)
    conv_taps = unshard(got[:, 1792:2304], 4)
    conv_w8 = jnp.concatenate([conv_taps, jnp.zeros((4, LRU_WIDTH), F32)], axis=0)

    mod_cols = _ada_fwd(c_all, w_ada[0], "ada_fwd")
    mod_all = _allgather_vmem(mod_cols, "gather_mod").reshape(N_DEV, N_DEV, 1152)
    mod = lax.dynamic_index_in_dim(mod_all, me, axis=1, keepdims=False).reshape(1, -1) + b_ada
    mod = mod.reshape(3, 3, 1, D_MODEL)

    w_slab = _slab_weights(lru_wa[0], lru_wx[0])
    bias = _bias_tile(rel_bias[0], "bias_tile")

    res_w = (0.5, 1.0, 0.5)
    row = lambda v: v.reshape(1, -1)

    weight_groups = [["ffn1_w_down"], ["w_in"], ["w_att_o", "w_rec_o", "w_out"], ["ffn2_w_gu", "ffn2_w_down"]]
    weight_modes = ["all", "chip", "all", "all"]
    weights_started, started = _push_start([[shards[n] for n in g] for g in weight_groups], False,
                                           "gather_weights_start", after=(mod, w1_gu), modes=weight_modes)
    full = {"ffn1_w_gu": _rows_full(w1_gu)}

    def gathered_group(gi, after):
        sent, lands = _push_wait(weights_started[gi], False, after, f"gather_weights_wait{gi}", mode=weight_modes[gi])
        if weight_modes[gi] == "chip":
            lands = _forward_wait(_forward_start(lands, f"gather_weights_forward{gi}"), sent[0],
                                  f"gather_weights_forward_wait{gi}")
        for n, land in zip(weight_groups[gi], lands):
            full[n] = full_of(n, land)

    def ffn_fwd(xin, k, gi, tag, deps=(), target=None):
        h, a, g, u = _pre_up(xin, row(g_pre[k]), mod[k, 0], mod[k, 1], full[f"{tag}_w_gu"], f"{tag}_up", deps=deps)
        if f"{tag}_w_down" not in full:
            gathered_group(gi, a)
        f, *out = _matmul_post(a, full[f"{tag}_w_down"], xin, row(g_post[k]), mod[k, 2], res_w[k], f"{tag}_down",
                               target=target)
        return (out[0] if target is None else out), (h, g, u, a, f)

    x1, saved1 = ffn_fwd(x0, 0, 0, "ffn1", deps=(started,))

    gathered_group(1, x1)
    h2, proj = _pre_matmul(x1, row(g_pre[1]), mod[1, 0], mod[1, 1], full["w_in"], "mix_in",
                           b_shift=3 * ATT_WIDTH // 512)
    att_o = _attn_fwd(proj, bias, "attn_fwd")
    gathered_group(2, att_o)
    xc, pre, a_t, u_t = _lru_front(proj, conv_w8, conv_b, w_slab, lru_ba, lru_bx, lru_lambda, "lru_front")
    hs, h_prev, rec_in = _scan_fwd(a_t, u_t, proj, "lru_scan")
    att = _matmul(att_o, full["w_att_o"], "nn", F32, "att_out")
    rec = _matmul(rec_in, full["w_rec_o"], "nn", F32, "rec_out")
    merged, f2, x2 = _merge_matmul_post(att, rec, proj, full["w_out"], x1, row(g_post[1]), mod[1, 2], res_w[1],
                                        "mix_out")

    gathered_group(3, x2)
    (dy, sq), saved3 = ffn_fwd(x2, 2, 2, "ffn2", target=target)
    loss = lax.psum(0.5 * jnp.sum(sq) / D_MODEL, ("x", "y", "c"))

    grads = {}
    norm_sums = [None] * 6

    pending = []

    def exchange_start(names, tag, after=()):
        send = [(_cols_parts if n == "w_att_o" else _rows_parts)(grads[n]) for n in names]
        (group,), token = _push_start([send], True, f"exchange_{tag}_start", after=after)
        pending.append((names, send, group, tag))
        return token

    def exchange_finish(names, send, group, tag, after):
        sent, lands = _push_wait(group, True, after, f"exchange_{tag}_wait")
        res = None
        for n, land, mine in zip(names, lands, sent):
            res = _adamw_parts(land, mine, me, local(n, weights[n]), local(n, mom1[n]), local(n, mom2[n]),
                               f"adamw_{n}")
            back = (lambda r: jnp.transpose(r)) if n in transposed else (lambda r: r)
            out_g[n], out_d[n], out_m[n], out_v[n] = [back(r).reshape(weights[n].shape) for r in res]
        return res[0]

    out_g, out_d, out_m, out_v = {}, {}, {}, {}

    def ffn_bwd(xin, k, saved, dout, tag):
        h, g, u, a, f = saved
        w_gu, w_down = f"{tag}_w_gu", f"{tag}_w_down"
        df, dgu, norm_sums[2 * k + 1] = _post_bwd_up_bwd(f, dout, row(g_post[k]), mod[k, 2], res_w[k], full[w_down],
                                                          g, u, f"{tag}_up_bwd")
        grads[w_down] = _matmul(a, df, "tn", BF16, f"{tag}_dw_down", tm=1408, tn=1024, tk=DW_TK)
        started = exchange_start([w_down], w_down)
        grads[w_gu] = _dw_gu(dgu, h, f"{tag}_dw_gu", deps=(started,))
        started = exchange_start([w_gu], w_gu)
        halves = [(dgu, (None, fuse_tm, D_FF), lambda i, half=half: (half, i, 0), (half * D_FF, (half + 1) * D_FF))
                  for half in range(2)]
        dx, norm_sums[2 * k] = _matmul_pre_bwd(halves, full[w_gu], xin, dout, row(g_pre[k]),
                                                                mod[k, 0], mod[k, 1], f"{tag}_dh", deps=(started,))
        return dx

    dx2 = ffn_bwd(x2, 2, saved3, dy, "ffn2")

    df2, datt, drec, dg_att, dg_rec, norm_sums[3] = _post_bwd_merge_bwd(
        f2, dx2, row(g_post[1]), mod[1, 2], res_w[1], full["w_out"], att, rec, proj, "mix_dmerged")
    grads["w_out"] = _matmul(merged, df2, "tn", BF16, "mix_dw_out", tm=1024, tn=1024, tk=DW_TK)
    datt_o = _matmul(datt, full["w_att_o"], "nt", BF16, "att_out_bwd")
    grads["w_att_o"] = _matmul(att_o, datt, "tn", BF16, "dw_att_o", tm=512, tn=1024, tk=DW_TK)
    grads["w_rec_o"] = _matmul(rec_in, drec, "tn", BF16, "dw_rec_o", tm=1024, tn=1024, tk=DW_TK)
    started = exchange_start(["w_out", "w_att_o", "w_rec_o"], "mix_out")
    dhs, dyr = _matmul_recin_bwd(drec, full["w_rec_o"], hs, proj, "rec_out_bwd", deps=(started,))
    g_t = _scan_bwd(a_t, dhs, "lru_scan_bwd")
    dpre, dxc, lru_sums = _lru_back(pre, xc, w_slab, lru_ba, lru_bx, lru_lambda, g_t, h_prev, "lru_back")
    dxr, conv_sums = _conv_bwd(proj, conv_w8, dxc, "conv_bwd")
    dq, dk, dv, dbias = _attn_bwd(proj, bias, datt_o, "attn_bwd")
    dproj = jnp.concatenate([dq, dk, dv, dxr, dyr, dg_att, dg_rec], axis=1)
    grads["w_in"] = _matmul(dproj, h2, "tn", BF16, "mix_dw_in", tm=1408, tn=1024, tk=DW_TK)
    pack_mix = jnp.concatenate([conv_sums, lru_sums, _pad_rows(_bias_grad(dbias, "bias_grad"), V7X_SUBLANES),
                                _lru_dw(xc, dpre, "lru_dw").reshape(128, D_MODEL)], axis=0)
    (mix_started,), started = _push_start([[pack_mix]], False, "small_grads_mix_start")
    started = exchange_start(["w_in"], "w_in", after=(started,))
    whole = [(dproj, (fuse_tm, PROJ_WIDTH), lambda i: (i, 0), (0, PROJ_WIDTH))]
    dx1, norm_sums[2] = _matmul_pre_bwd(whole, full["w_in"], x1, dx2, row(g_pre[1]), mod[1, 0],
                                                             mod[1, 1], "mix_dh", deps=(started,))

    dx0 = ffn_bwd(x0, 0, saved1, dx1, "ffn1")

    pack_norm = jnp.concatenate(norm_sums, axis=0)
    (norm_started,), _ = _push_start([[pack_norm]], False, "small_grads_norm_start")

    def summed(started, pack, after, tag):
        _, (parts,) = _push_wait(started, False, after, f"small_grads_{tag}_wait")
        return parts, _sum_parts(parts, f"small_grads_{tag}_sum")

    done = dx0
    last = pending[-1:]
    for names, send, group, tag in pending[:-1]:
        done = exchange_finish(names, send, group, tag, done)

    _, total = summed(mix_started, pack_mix, done, "mix")
    grads["conv_w"] = _my_cols(total[0:4], me, 128)
    grads["conv_b"] = total[4:5]
    grads["lru_ba"] = total[8:9]
    grads["lru_bx"] = total[9:10]
    grads["lru_lambda"] = total[10:11]
    grads["rel_bias"] = total[16:19].reshape(-1)[: ATT_HEADS * (2 * MAX_REL + 1)].reshape(ATT_HEADS, -1)
    grads["lru_wa"] = total[24:88].reshape(LRU_BLOCKS, LRU_BLOCK, LRU_BLOCK)
    grads["lru_wx"] = total[88:152].reshape(LRU_BLOCKS, LRU_BLOCK, LRU_BLOCK)
    parts, total = summed(norm_started, pack_norm, total, "norm")
    by_sandwich = lambda v: v.reshape(*v.shape[:-2], 3, 2 * V7X_SUBLANES, D_MODEL)
    dmod_of = lambda v: jnp.concatenate([by_sandwich(v)[..., 1:3, :], by_sandwich(v)[..., 9:10, :]], axis=-2)
    grads["b_ada"] = dmod_of(total).reshape(1, -1)
    grads["norm_pre"] = _my_cols(by_sandwich(total)[:, 0, :], me, 128)
    grads["norm_post"] = _my_cols(by_sandwich(total)[:, V7X_SUBLANES, :], me, 128)
    dmod_all = dmod_of(parts).reshape(N_DEV, 9 * D_MODEL)
    grads["w_ada"] = _ada_bwd(c_all, _my_cols(dmod_all, me, 1152), "ada_bwd")

    res = _adamw(grads["w_ada"], w_ada[0], m_w_ada[0], v_w_ada[0], "adamw_w_ada")
    out_g["w_ada"], out_d["w_ada"], out_m["w_ada"], out_v["w_ada"] = [r.reshape(w_ada.shape) for r in res]

    sizes = [int(np.prod(weights[n].shape)) for n in small]
    tot = sum(sizes)
    rows_small = -(-tot // (16 * D_MODEL)) * 16
    flat = lambda arrs: jnp.pad(jnp.concatenate([a.reshape(-1) for a in arrs]),
                                (0, rows_small * D_MODEL - tot)).reshape(rows_small, D_MODEL)
    res = _adamw(flat([grads[n] for n in small]), flat([weights[n] for n in small]),
                 flat([mom1[n] for n in small]), flat([mom2[n] for n in small]), "adamw_small", rows=rows_small)
    offs = np.cumsum([0] + sizes)
    for dst, r in zip((out_g, out_d, out_m, out_v), res):
        rf = r.reshape(-1)
        for i, n in enumerate(small):
            dst[n] = rf[offs[i] : offs[i + 1]].reshape(weights[n].shape)

    done = res[0]
    for names, send, group, tag in last:
        done = exchange_finish(names, send, group, tag, done)

    return (loss, dx0[None], *[out_g[n] for n in order], *[out_d[n] for n in order],
            *[out_m[n] for n in order], *[out_v[n] for n in order])
```

```python
import jax
import jax.numpy as jnp
import numpy as np
from jax import lax
from jax.experimental import pallas as pl
from jax.experimental.pallas import tpu as pltpu

D_MODEL = 1024
D_FF = 2816
ATT_HEADS = 8
ATT_HEAD_DIM = 64
ATT_WIDTH = 512
CHUNK = 64
LEFT_CHUNKS = 8
MAX_REL = 128
LRU_WIDTH = 1024
LRU_BLOCKS = 16
LRU_BLOCK = 64
LRU_C = 8.0
EPS = 1e-6
PROJ_WIDTH = 5632
N_DEV = 8

ADAM_LR = 0.001
ADAM_B1 = 0.9
ADAM_B2 = 0.999
ADAM_EPS = 1e-08
ADAM_WD = 0.01
ADAM_STEP = 10

V7X_LANES = 128
V7X_SUBLANES = 8
V7X_VMEM_BYTES = 64 * 1024 * 1024
VMEM_LIMIT = V7X_VMEM_BYTES - 8 * 1024 * 1024

ATT_TQ = 256
NEG = -1e30
BF16 = jnp.bfloat16
F32 = jnp.float32
MESH = pl.DeviceIdType.MESH

OFF_Q = 4 * LRU_WIDTH
OFF_K = OFF_Q + ATT_WIDTH
OFF_V = OFF_K + ATT_WIDTH


def _cparams(**kw):
    return pltpu.CompilerParams(vmem_limit_bytes=VMEM_LIMIT, **kw)


def _pick(n, target, unit=V7X_LANES):
    best = None
    for t in range(unit, min(n, target) + 1, unit):
        if n % t == 0:
            best = t
    return n if best is None else best


_DIMS = {
    "nn": (((1,), (0,)), ((), ())),
    "nt": (((1,), (1,)), ((), ())),
    "tn": (((0,), (0,)), ((), ())),
}


ANY_SPEC = pl.BlockSpec(memory_space=pl.ANY)


def _matmul(a, b, mode, out_dtype, name, tm=1024, tn=512, tk=1408, deps=(), b_shift=0):
    n_deps = len(deps)
    if mode == "nn":
        (m, k), (k2, n) = a.shape, b.shape
    elif mode == "nt":
        (m, k), (n, k2) = a.shape, b.shape
    else:
        (k, m), (k2, n) = a.shape, b.shape
    assert k == k2, (a.shape, b.shape, mode)
    tm, tn, tk = _pick(m, tm), _pick(n, tn), _pick(k, tk)
    nk = k // tk
    dims = _DIMS[mode]

    def body(a_ref, b_ref, *rest):
        o_ref, scratch = rest[n_deps], rest[n_deps + 1 :]
        p = lax.dot_general(a_ref[...], b_ref[...], dims, preferred_element_type=F32)
        if nk == 1:
            o_ref[...] = p.astype(o_ref.dtype)
        else:
            acc = scratch[0]
            kk = pl.program_id(2)

            @pl.when(kk == 0)
            def _():
                acc[...] = p

            @pl.when(kk > 0)
            def _():
                acc[...] += p

            @pl.when(kk == nk - 1)
            def _():
                o_ref[...] = acc[...].astype(o_ref.dtype)

    if mode == "nn":
        a_spec = pl.BlockSpec((tm, tk), lambda i, j, kk: (i, kk))
        b_spec = pl.BlockSpec((tk, tn), lambda i, j, kk: (kk, j))
    elif mode == "nt":
        a_spec = pl.BlockSpec((tm, tk), lambda i, j, kk: (i, kk))
        b_spec = pl.BlockSpec((tn, tk), lambda i, j, kk: ((j + b_shift) % (n // tn), kk))
    else:
        a_spec = pl.BlockSpec((tk, tm), lambda i, j, kk: (kk, i))
        b_spec = pl.BlockSpec((tk, tn), lambda i, j, kk: (kk, j))
    return pl.pallas_call(
        body,
        name=name,
        grid=(m // tm, n // tn, nk),
        in_specs=[a_spec, b_spec] + [ANY_SPEC] * n_deps,
        out_specs=pl.BlockSpec((tm, tn), lambda i, j, kk: (i, j)),
        out_shape=jax.ShapeDtypeStruct((m, n), out_dtype),
        scratch_shapes=[pltpu.VMEM((tm, tn), F32)] if nk > 1 else [],
        compiler_params=_cparams(dimension_semantics=("parallel", "parallel", "arbitrary")),
    )(a, b, *deps)


def _rowwise(fn, name, params, tiles, outs, accs=(), ts=256, with_index=False, deps=()):
    norm = []
    for t in tiles:
        if not isinstance(t, tuple):
            t = (t, t.shape[1], 0)
        norm.append(t if len(t) == 4 else (*t, None))
    s = norm[0][0].shape[0]
    ts = min(ts, s)
    assert s % ts == 0 and ts % V7X_SUBLANES == 0
    steps = s // ts
    halo_blocks = ts // V7X_SUBLANES
    n_p, n_t, n_o = len(params), len(norm), len(outs)

    def body(*refs):
        i = pl.program_id(0)
        vals = [r[...] for r in refs[: n_p + n_t]]
        res = fn(i, steps, *vals) if with_index else fn(*vals)
        if not isinstance(res, (tuple, list)):
            res = (res,)
        first_out = n_p + n_t + len(deps)
        o_refs = refs[first_out : first_out + n_o]
        a_refs = refs[first_out + n_o :]
        for r, v in zip(o_refs, res[:n_o]):
            r[...] = v.astype(r.dtype)
        for r, v in zip(a_refs, res[n_o:]):
            _accumulate(r, v, i)

    in_specs = [pl.BlockSpec(p.shape, lambda i: (0, 0)) for p in params]
    for arr, w, cb, halo in norm:
        if halo is None:
            in_specs.append(pl.BlockSpec((ts, w), lambda i, cb=cb: (i, cb)))
        elif halo == "prev":
            in_specs.append(
                pl.BlockSpec((V7X_SUBLANES, w), lambda i, cb=cb: (jnp.maximum(i * halo_blocks - 1, 0), cb))
            )
        else:
            last = s // V7X_SUBLANES - 1
            in_specs.append(
                pl.BlockSpec((V7X_SUBLANES, w), lambda i, cb=cb: (jnp.minimum((i + 1) * halo_blocks, last), cb))
            )
    in_specs += [ANY_SPEC] * len(deps)
    out_specs = [pl.BlockSpec((ts, w), lambda i: (i, 0)) for w, _ in outs]
    out_specs += [pl.BlockSpec(shape, lambda i: (0, 0)) for shape in accs]
    out_shape = [jax.ShapeDtypeStruct((s, w), dt) for w, dt in outs]
    out_shape += [jax.ShapeDtypeStruct(shape, F32) for shape in accs]
    res = pl.pallas_call(
        body,
        name=name,
        grid=(steps,),
        in_specs=in_specs,
        out_specs=out_specs,
        out_shape=out_shape,
        compiler_params=_cparams(dimension_semantics=("arbitrary",)),
    )(*params, *[t[0] for t in norm], *deps)
    return res


def _accumulate(ref, val, step):
    @pl.when(step == 0)
    def _():
        ref[...] = val

    @pl.when(step > 0)
    def _():
        ref[...] += val


def _sigmoid(z):
    return jax.nn.sigmoid(z)


def _silu(z):
    return z * _sigmoid(z)


def _gelu(z):
    return 0.5 * z * (1.0 + jnp.tanh(0.7978845608028654 * (z + 0.044715 * (z * z * z))))


def _pre_fn(g, shift, scale, x):
    r = lax.rsqrt(jnp.mean(x * x, axis=-1, keepdims=True) + EPS)
    return ((x * r) * g) * (1.0 + scale) + shift


def _post_fn(res_w, g, gate, f, x):
    r = lax.rsqrt(jnp.mean(f * f, axis=-1, keepdims=True) + EPS)
    return x + (res_w * gate) * ((f * r) * g)


def _gates_fn(ba, bx, lam, pre, xc):
    ra = _sigmoid(pre[:, :LRU_WIDTH] + ba)
    ia = _sigmoid(pre[:, LRU_WIDTH:] + bx)
    softplus = jnp.maximum(-lam, 0.0) + jnp.log1p(jnp.exp(-jnp.abs(lam)))
    log_a = (-LRU_C) * ra * softplus
    a = jnp.exp(log_a)
    mult = jnp.sqrt(-jnp.tanh(log_a) * (a * a + 1.0))
    return a, mult * (ia * xc)


def _recin_fn(hs, yr):
    return hs * _gelu(yr)


def _merge_fn(att, rec, g_att, g_rec):
    return _sigmoid(g_att) * att + _sigmoid(g_rec) * rec


def _rowsum(v):
    return jnp.sum(v, axis=0, keepdims=True)


FFN_TM = 512
FFN_TF = 1408


def _glu_fn(g, u):
    return _silu(g) * u


FUSE_TM = 256
DW_TK = 4096
ROW_SPEC2 = pl.BlockSpec((1, D_MODEL), lambda i, j: (0, 0))
ROW_SPEC1 = pl.BlockSpec((1, D_MODEL), lambda i: (0, 0))
SUMS_SPEC1 = pl.BlockSpec((V7X_SUBLANES, D_MODEL), lambda i: (0, 0))
SUMS_SPEC2 = pl.BlockSpec((V7X_SUBLANES, D_MODEL), lambda i, j: (0, 0))
SUMS_SHAPE = jax.ShapeDtypeStruct((V7X_SUBLANES, D_MODEL), F32)


def _sum_rows(*rows):
    pad = jnp.zeros((V7X_SUBLANES - len(rows), rows[0].shape[1]), F32)
    return jnp.concatenate([*rows, pad], axis=0)


def _pre_then_up(x, g, shift, scale, w_gu_t, name, deps=()):
    s = x.shape[0]
    tm = min(FFN_TM, s)
    nf = D_FF // FFN_TF
    (h,) = _rowwise(_pre_fn, name + "_pre", [g, shift, scale], [x], [(D_MODEL, BF16)], deps=deps)

    def body(h_ref, wg_ref, wu_ref, a_ref, gg_ref, u_ref):
        hv = h_ref[...]
        gv = lax.dot_general(hv, wg_ref[...], _DIMS["nt"], preferred_element_type=F32)
        uv = lax.dot_general(hv, wu_ref[...], _DIMS["nt"], preferred_element_type=F32)
        a_ref[...] = _glu_fn(gv, uv).astype(a_ref.dtype)
        gg_ref[...] = gv.astype(gg_ref.dtype)
        u_ref[...] = uv.astype(u_ref.dtype)

    out = pl.BlockSpec((tm, FFN_TF), lambda j, i: (i, j))
    a, gg, u = pl.pallas_call(
        body,
        name=name,
        grid=(nf, s // tm),
        in_specs=[pl.BlockSpec((tm, D_MODEL), lambda j, i: (i, 0)),
                  pl.BlockSpec((FFN_TF, D_MODEL), lambda j, i: (j, 0)),
                  pl.BlockSpec((FFN_TF, D_MODEL), lambda j, i: (nf + j, 0))],
        out_specs=[out, out, out],
        out_shape=[jax.ShapeDtypeStruct((s, D_FF), BF16)] * 3,
        compiler_params=_cparams(dimension_semantics=("arbitrary", "parallel")),
    )(h, w_gu_t, w_gu_t)
    return h, a, gg, u


def _pre_up(x, g, shift, scale, w_gu_t, name, deps=()):
    s = x.shape[0]
    tm = min(FFN_TM, s)
    nf = D_FF // FFN_TF
    nd = len(deps)

    def body(x_ref, g_ref, sh_ref, sc_ref, wg_ref, wu_ref, *rest):
        h_ref, a_ref, gg_ref, u_ref, h_s = rest[nd:]

        @pl.when(pl.program_id(1) == 0)
        def _():
            h = _pre_fn(g_ref[...], sh_ref[...], sc_ref[...], x_ref[...]).astype(BF16)
            h_s[...] = h
            h_ref[...] = h

        hv = h_s[...]
        gv = lax.dot_general(hv, wg_ref[...], _DIMS["nt"], preferred_element_type=F32)
        uv = lax.dot_general(hv, wu_ref[...], _DIMS["nt"], preferred_element_type=F32)
        a_ref[...] = _glu_fn(gv, uv).astype(a_ref.dtype)
        gg_ref[...] = gv.astype(gg_ref.dtype)
        u_ref[...] = uv.astype(u_ref.dtype)

    rows = pl.BlockSpec((tm, D_MODEL), lambda i, j: (i, 0))
    out = pl.BlockSpec((tm, FFN_TF), lambda i, j: (i, j))
    return pl.pallas_call(
        body,
        name=name,
        grid=(s // tm, nf),
        in_specs=[rows, ROW_SPEC2, ROW_SPEC2, ROW_SPEC2,
                  pl.BlockSpec((FFN_TF, D_MODEL), lambda i, j: (j, 0)),
                  pl.BlockSpec((FFN_TF, D_MODEL), lambda i, j: (nf + j, 0))] + [ANY_SPEC] * nd,
        out_specs=[rows, out, out, out],
        out_shape=[jax.ShapeDtypeStruct((s, D_MODEL), BF16)] + [jax.ShapeDtypeStruct((s, D_FF), BF16)] * 3,
        scratch_shapes=[pltpu.VMEM((tm, D_MODEL), BF16)],
        compiler_params=_cparams(dimension_semantics=("parallel", "arbitrary")),
    )(x, g, shift, scale, w_gu_t, w_gu_t, *deps)


def _pre_matmul(x, g, shift, scale, w_t, name, b_shift=0, tn=512):
    s = x.shape[0]
    n = w_t.shape[0]
    tm = min(2 * FFN_TM, s)

    def body(x_ref, g_ref, sh_ref, sc_ref, w_ref, h_ref, o_ref, h_s):
        @pl.when(pl.program_id(1) == 0)
        def _():
            h = _pre_fn(g_ref[...], sh_ref[...], sc_ref[...], x_ref[...]).astype(BF16)
            h_s[...] = h
            h_ref[...] = h

        o_ref[...] = lax.dot_general(h_s[...], w_ref[...], _DIMS["nt"], preferred_element_type=F32)

    rows = pl.BlockSpec((tm, D_MODEL), lambda i, j: (i, 0))
    return pl.pallas_call(
        body,
        name=name,
        grid=(s // tm, n // tn),
        in_specs=[rows, ROW_SPEC2, ROW_SPEC2, ROW_SPEC2,
                  pl.BlockSpec((tn, D_MODEL), lambda i, j: ((j + b_shift) % (n // tn), 0))],
        out_specs=[rows, pl.BlockSpec((tm, tn), lambda i, j: (i, j))],
        out_shape=[jax.ShapeDtypeStruct((s, D_MODEL), BF16), jax.ShapeDtypeStruct((s, n), F32)],
        scratch_shapes=[pltpu.VMEM((tm, D_MODEL), BF16)],
        compiler_params=_cparams(dimension_semantics=("parallel", "arbitrary")),
    )(x, g, shift, scale, w_t)


def _matmul_post(a, w, x, g_post, gate, res_w, name, target=None):
    s, k = a.shape
    tm = min(FFN_TM, s)
    extra = [] if target is None else [target]

    def body(a_ref, w_ref, x_ref, g_ref, gate_ref, *rest):
        f = jnp.dot(a_ref[...], w_ref[...], preferred_element_type=F32)
        y = _post_fn(res_w, g_ref[...], gate_ref[...], f, x_ref[...])
        if target is None:
            f_ref, y_ref = rest
            y_ref[...] = y
        else:
            t_ref, f_ref, dy_ref, sq_ref = rest
            diff = y - t_ref[...]
            dy_ref[...] = diff * (1.0 / D_MODEL)
            _accumulate(sq_ref, _rowsum(diff * diff), pl.program_id(0))
        f_ref[...] = f

    rows = pl.BlockSpec((tm, D_MODEL), lambda i: (i, 0))
    out_specs, out_shape = [rows, rows], [jax.ShapeDtypeStruct((s, D_MODEL), F32)] * 2
    if target is not None:
        out_specs.append(ROW_SPEC1)
        out_shape.append(jax.ShapeDtypeStruct((1, D_MODEL), F32))
    return pl.pallas_call(
        body,
        name=name,
        grid=(s // tm,),
        in_specs=[pl.BlockSpec((tm, k), lambda i: (i, 0)), pl.BlockSpec((k, D_MODEL), lambda i: (0, 0)), rows,
                  ROW_SPEC1, ROW_SPEC1] + [rows] * len(extra),
        out_specs=out_specs,
        out_shape=out_shape,
        compiler_params=_cparams(dimension_semantics=("arbitrary",)),
    )(a, w, x, g_post, gate, *extra)


def _merge_matmul_post(att, rec, proj, w, x, g_post, gate, res_w, name):
    s = att.shape[0]
    tm = min(FUSE_TM, s)

    def body(att_ref, rec_ref, ga_ref, gr_ref, w_ref, x_ref, g_ref, gate_ref, m_ref, f_ref, y_ref):
        merged = _merge_fn(att_ref[...], rec_ref[...], ga_ref[...], gr_ref[...]).astype(BF16)
        m_ref[...] = merged
        f = jnp.dot(merged, w_ref[...], preferred_element_type=F32)
        f_ref[...] = f
        y_ref[...] = _post_fn(res_w, g_ref[...], gate_ref[...], f, x_ref[...])

    rows = pl.BlockSpec((tm, D_MODEL), lambda i: (i, 0))
    return pl.pallas_call(
        body,
        name=name,
        grid=(s // tm,),
        in_specs=[rows, rows, pl.BlockSpec((tm, D_MODEL), lambda i: (i, 2)), pl.BlockSpec((tm, D_MODEL), lambda i: (i, 3)),
                  pl.BlockSpec(w.shape, lambda i: (0, 0)), rows, ROW_SPEC1, ROW_SPEC1],
        out_specs=[rows, rows, rows],
        out_shape=[jax.ShapeDtypeStruct((s, D_MODEL), BF16)] + [jax.ShapeDtypeStruct((s, D_MODEL), F32)] * 2,
        compiler_params=_cparams(dimension_semantics=("parallel",)),
    )(att, rec, proj, proj, w, x, g_post, gate)


def _post_bwd_merge_bwd(f, dy, g_post, gate, res_w, w, att, rec, proj, name):
    s = f.shape[0]
    tm = min(FUSE_TM, s)

    def body(f_ref, dy_ref, gp_ref, gate_ref, w_ref, att_ref, rec_ref, ga_ref, gr_ref,
             df_ref, datt_ref, drec_ref, dga_ref, dgr_ref, sums_ref):
        i = pl.program_id(0)
        dgp, dgate, df = _post_vjp(res_w, gp_ref[...], gate_ref[...], f_ref[...], dy_ref[...])
        dfb = df.astype(BF16)
        df_ref[...] = dfb
        _accumulate(sums_ref, _sum_rows(dgp, dgate), i)
        dmerged = lax.dot_general(dfb, w_ref[...], _DIMS["nt"], preferred_element_type=F32)
        _, vjp = jax.vjp(_merge_fn, att_ref[...], rec_ref[...], ga_ref[...], gr_ref[...])
        for ref, val in zip((datt_ref, drec_ref, dga_ref, dgr_ref), vjp(dmerged)):
            ref[...] = val.astype(ref.dtype)

    rows = pl.BlockSpec((tm, D_MODEL), lambda i: (i, 0))
    return pl.pallas_call(
        body,
        name=name,
        grid=(s // tm,),
        in_specs=[rows, rows, ROW_SPEC1, ROW_SPEC1, pl.BlockSpec(w.shape, lambda i: (0, 0)), rows, rows,
                  pl.BlockSpec((tm, D_MODEL), lambda i: (i, 2)), pl.BlockSpec((tm, D_MODEL), lambda i: (i, 3))],
        out_specs=[rows] * 5 + [SUMS_SPEC1],
        out_shape=[jax.ShapeDtypeStruct((s, D_MODEL), BF16)] * 5 + [SUMS_SHAPE],
        compiler_params=_cparams(dimension_semantics=("arbitrary",)),
    )(f, dy, g_post, gate, w, att, rec, proj, proj)


def _matmul_recin_bwd(drec, w, hs, proj, name, deps=()):
    s = drec.shape[0]
    tm = min(FUSE_TM, s)
    nd = len(deps)

    def body(d_ref, w_ref, hs_ref, yr_ref, *rest):
        dhs_ref, dyr_ref = rest[nd:]
        d = lax.dot_general(d_ref[...], w_ref[...], _DIMS["nt"], preferred_element_type=F32)
        _, vjp = jax.vjp(_recin_fn, hs_ref[...], yr_ref[...])
        dhs, dyr = vjp(d)
        dhs_ref[...] = dhs
        dyr_ref[...] = dyr.astype(dyr_ref.dtype)

    rows = pl.BlockSpec((tm, D_MODEL), lambda i: (i, 0))
    return pl.pallas_call(
        body,
        name=name,
        grid=(s // tm,),
        in_specs=[rows, pl.BlockSpec(w.shape, lambda i: (0, 0)), rows,
                  pl.BlockSpec((tm, D_MODEL), lambda i: (i, 1))] + [ANY_SPEC] * nd,
        out_specs=[rows, rows],
        out_shape=[jax.ShapeDtypeStruct((s, D_MODEL), F32), jax.ShapeDtypeStruct((s, D_MODEL), BF16)],
        compiler_params=_cparams(dimension_semantics=("parallel",)),
    )(drec, w, hs, proj, *deps)


def _post_vjp(res_w, g, gate, f, dy):
    _, vjp = jax.vjp(lambda g, gate, f: _post_fn(res_w, g, gate, f, 0.0), g, gate, f)
    return vjp(dy)


def _post_bwd_up_bwd(f, dy, g_post, gate, res_w, w_down, g, u, name, deps=()):
    s = f.shape[0]
    tm = min(FFN_TM, s)
    nd = len(deps)

    def body(f_ref, dy_ref, gp_ref, gate_ref, wd_ref, g_ref, u_ref, *rest):
        df_ref, dgu_ref, sums_ref, df_s = rest[nd:]
        i = pl.program_id(0)

        @pl.when(pl.program_id(1) == 0)
        def _():
            dgp, dgate, df = _post_vjp(res_w, gp_ref[...], gate_ref[...], f_ref[...], dy_ref[...])
            df_s[...] = df.astype(BF16)
            df_ref[...] = df_s[...]
            _accumulate(sums_ref, _sum_rows(dgp, dgate), i)

        da = lax.dot_general(df_s[...], wd_ref[...], _DIMS["nt"], preferred_element_type=F32)
        _, vjp = jax.vjp(_glu_fn, g_ref[...].astype(F32), u_ref[...].astype(F32))
        dg, du = vjp(da)
        dgu_ref[0] = dg.astype(dgu_ref.dtype)
        dgu_ref[1] = du.astype(dgu_ref.dtype)

    rows = pl.BlockSpec((tm, D_MODEL), lambda i, j: (i, 0))
    blk = pl.BlockSpec((tm, FFN_TF), lambda i, j: (i, j))
    return pl.pallas_call(
        body,
        name=name,
        grid=(s // tm, D_FF // FFN_TF),
        in_specs=[rows, rows, ROW_SPEC2, ROW_SPEC2, pl.BlockSpec((FFN_TF, D_MODEL), lambda i, j: (j, 0)), blk,
                  blk] + [ANY_SPEC] * nd,
        out_specs=[rows, pl.BlockSpec((2, tm, FFN_TF), lambda i, j: (0, i, j)), SUMS_SPEC2],
        out_shape=[jax.ShapeDtypeStruct((s, D_MODEL), BF16), jax.ShapeDtypeStruct((2, s, D_FF), BF16), SUMS_SHAPE],
        scratch_shapes=[pltpu.VMEM((tm, D_MODEL), BF16)],
        compiler_params=_cparams(dimension_semantics=("arbitrary", "arbitrary")),
    )(f, dy, g_post, gate, w_down, g, u, *deps)


def _matmul_pre_bwd(parts, w_t, x, dres, g, shift, scale, name, deps=()):
    s = x.shape[0]
    na, nd = len(parts), len(deps)
    ranges = [p[3] for p in parts]

    def body(*refs):
        a_refs = refs[:na]
        w_ref, x_ref, dres_ref, g_ref, sh_ref, sc_ref = refs[na : na + 6]
        dx_ref, sums_ref = refs[na + 6 + nd :]
        i = pl.program_id(0)
        dh = None
        for a_ref, (r0, r1) in zip(a_refs, ranges):
            p = jnp.dot(a_ref[...], w_ref[r0:r1, :], preferred_element_type=F32)
            dh = p if dh is None else dh + p
        _, vjp = jax.vjp(_pre_fn, g_ref[...], sh_ref[...], sc_ref[...], x_ref[...])
        dg, dsh, dsc, dx = vjp(dh)
        dx_ref[...] = dx + dres_ref[...]
        _accumulate(sums_ref, _sum_rows(dg, dsh, dsc), i)

    tm = parts[0][1][-2]
    rows = pl.BlockSpec((tm, D_MODEL), lambda i: (i, 0))
    return pl.pallas_call(
        body,
        name=name,
        grid=(s // tm,),
        in_specs=[pl.BlockSpec(p[1], p[2]) for p in parts]
        + [pl.BlockSpec(w_t.shape, lambda i: (0, 0)), rows, rows, ROW_SPEC1, ROW_SPEC1, ROW_SPEC1]
        + [ANY_SPEC] * nd,
        out_specs=[rows, SUMS_SPEC1],
        out_shape=[jax.ShapeDtypeStruct((s, D_MODEL), F32), SUMS_SHAPE],
        compiler_params=_cparams(dimension_semantics=("arbitrary",)),
    )(*[p[0] for p in parts], w_t, x, dres, g, shift, scale, *deps)


def _dw_gu(dgu, h, name, deps=(), tk=DW_TK):
    s = h.shape[0]
    tk = min(tk, s)
    nk = s // tk
    half = D_FF // FFN_TF

    def body(a_ref, b_ref, *rest):
        o_ref = rest[len(deps)]
        kk = pl.program_id(1)
        p = lax.dot_general(a_ref[...], b_ref[...], _DIMS["tn"], preferred_element_type=F32)
        if nk == 1:
            o_ref[...] = p.astype(o_ref.dtype)
            return
        acc = rest[len(deps) + 1]

        @pl.when(kk == 0)
        def _():
            acc[...] = p

        @pl.when(kk > 0)
        def _():
            acc[...] += p

        @pl.when(kk == nk - 1)
        def _():
            o_ref[...] = acc[...].astype(o_ref.dtype)

    return pl.pallas_call(
        body,
        name=name,
        grid=(2 * half, nk),
        in_specs=[pl.BlockSpec((None, tk, FFN_TF), lambda i, kk: (i // half, kk, i % half)),
                  pl.BlockSpec((tk, D_MODEL), lambda i, kk: (kk, 0))] + [ANY_SPEC] * len(deps),
        out_specs=pl.BlockSpec((FFN_TF, D_MODEL), lambda i, kk: (i, 0)),
        out_shape=jax.ShapeDtypeStruct((2 * D_FF, D_MODEL), BF16),
        scratch_shapes=[pltpu.VMEM((FFN_TF, D_MODEL), F32)] if nk > 1 else [],
        compiler_params=_cparams(dimension_semantics=("parallel", "arbitrary")),
    )(dgu, h, *deps)


def _shift_down(ext, j, rows):
    return pltpu.roll(ext, j, 0)[V7X_SUBLANES : V7X_SUBLANES + rows]


def _shift_up(ext, j, rows):
    return pltpu.roll(ext, ext.shape[0] - j, 0)[:rows] if j else ext[:rows]


LRU_SLAB = 256
N_SLABS = LRU_WIDTH // LRU_SLAB


def _slab_weights(wa, wx):
    per = LRU_SLAB // LRU_BLOCK
    eye = jnp.eye(per, dtype=wa.dtype)

    def diag(w):
        w4 = w.reshape(N_SLABS, per, LRU_BLOCK, LRU_BLOCK)
        return jnp.einsum("sbkj,bc->sbkcj", w4, eye).reshape(N_SLABS, LRU_SLAB, LRU_SLAB)

    return jnp.concatenate([diag(wa), diag(wx)], axis=2).reshape(LRU_WIDTH, 2 * LRU_SLAB).astype(BF16)


def _slab_cols(v, s):
    lo = s * LRU_SLAB
    return jnp.concatenate([v[:, lo : lo + LRU_SLAB], v[:, LRU_WIDTH + lo : LRU_WIDTH + lo + LRU_SLAB]], axis=1)


def _lru_front(proj, w8, b, w_slab, ba, bx, lam, name):
    def fn(i, steps, w8, b, w_slab, ba, bx, lam, x, halo):
        halo = jnp.where(i > 0, halo, 0.0)
        ext = jnp.concatenate([halo, x], axis=0)
        xc = b + w8[3:4] * x
        for j in (1, 2, 3):
            xc = xc + w8[3 - j : 4 - j] * _shift_down(ext, j, x.shape[0])
        xcb = xc.astype(BF16)
        prods = []
        for s in range(N_SLABS):
            rows = slice(s * LRU_SLAB, (s + 1) * LRU_SLAB)
            prods.append(jnp.dot(xcb[:, rows], w_slab[rows], preferred_element_type=F32))
        pre = jnp.concatenate([p[:, :LRU_SLAB] for p in prods] + [p[:, LRU_SLAB:] for p in prods], axis=1)
        a, u = _gates_fn(ba, bx, lam, pre, xc)
        return xc, pre, a, u

    tiles = [(proj, LRU_WIDTH, 0), (proj, LRU_WIDTH, 0, "prev")]
    outs = [(LRU_WIDTH, F32), (2 * LRU_WIDTH, F32), (LRU_WIDTH, F32), (LRU_WIDTH, F32)]
    return _rowwise(fn, name, [w8, b, w_slab, ba, bx, lam], tiles, outs, with_index=True)


def _lru_back(pre, xc, w_slab, ba, bx, lam, g, h_prev, name, deps=()):
    def fn(w_slab, ba, bx, lam, pre, xc, g, h_prev):
        _, vjp = jax.vjp(_gates_fn, ba, bx, lam, pre, xc)
        dba, dbx, dlam, dpre, dxc = vjp((g * h_prev, g))
        dpre = dpre.astype(BF16)
        back = []
        for s in range(N_SLABS):
            rows = slice(s * LRU_SLAB, (s + 1) * LRU_SLAB)
            back.append(lax.dot_general(_slab_cols(dpre, s), w_slab[rows], _DIMS["nt"], preferred_element_type=F32))
        return dpre, dxc + jnp.concatenate(back, axis=1), _sum_rows(dba, dbx, dlam)

    return _rowwise(fn, name, [w_slab, ba, bx, lam], [pre, xc, g, h_prev],
                    [(2 * LRU_WIDTH, BF16), (LRU_WIDTH, F32)], [(V7X_SUBLANES, LRU_WIDTH)], deps=deps)


def _lru_dw(xc, dpre, name):
    s = xc.shape[0]
    ts = min(512, s)
    steps = s // ts
    per = LRU_SLAB // LRU_BLOCK

    def body(x_ref, d_ref, o_ref, acc):
        i = pl.program_id(0)
        xcb = x_ref[...].astype(BF16)
        d = d_ref[...]
        for sl in range(N_SLABS):
            rows = slice(sl * LRU_SLAB, (sl + 1) * LRU_SLAB)
            p = lax.dot_general(xcb[:, rows], _slab_cols(d, sl), _DIMS["tn"], preferred_element_type=F32)

            @pl.when(i == 0)
            def _(p=p, rows=rows):
                acc[rows, :] = p

            @pl.when(i > 0)
            def _(p=p, rows=rows):
                acc[rows, :] += p

        @pl.when(i == steps - 1)
        def _():
            for half in range(2):
                for n in range(LRU_BLOCKS):
                    r0 = n * LRU_BLOCK
                    c0 = half * LRU_SLAB + (n % per) * LRU_BLOCK
                    o_ref[half, r0 : r0 + LRU_BLOCK, :] = acc[r0 : r0 + LRU_BLOCK, c0 : c0 + LRU_BLOCK]

    return pl.pallas_call(
        body,
        name=name,
        grid=(steps,),
        in_specs=[pl.BlockSpec((ts, LRU_WIDTH), lambda i: (i, 0)), pl.BlockSpec((ts, 2 * LRU_WIDTH), lambda i: (i, 0))],
        out_specs=pl.BlockSpec((2, LRU_WIDTH, LRU_BLOCK), lambda i: (0, 0, 0)),
        out_shape=jax.ShapeDtypeStruct((2, LRU_WIDTH, LRU_BLOCK), F32),
        scratch_shapes=[pltpu.VMEM((LRU_WIDTH, 2 * LRU_SLAB), F32)],
        compiler_params=_cparams(dimension_semantics=("arbitrary",)),
    )(xc, dpre)


def _conv_bwd(proj, w8, d1, name):
    def fn(i, steps, w8, x, halo, d, d1n):
        rows = x.shape[0]
        dn = jnp.where(i < steps - 1, d1n, 0.0)
        halo = jnp.where(i > 0, halo, 0.0)
        dext = jnp.concatenate([d, dn], axis=0)
        xext = jnp.concatenate([halo, x], axis=0)
        dx = w8[3:4] * d
        dw = [None] * 4
        dw[3] = _rowsum(d * x)
        for k in (1, 2, 3):
            dx = dx + w8[3 - k : 4 - k] * _shift_up(dext, k, rows)
            dw[3 - k] = _rowsum(d * _shift_down(xext, k, rows))
        return dx, _sum_rows(*dw, _rowsum(d))

    tiles = [(proj, LRU_WIDTH, 0), (proj, LRU_WIDTH, 0, "prev"), d1, (d1, LRU_WIDTH, 0, "next")]
    return _rowwise(fn, name, [w8], tiles, [(LRU_WIDTH, BF16)], [(V7X_SUBLANES, LRU_WIDTH)], with_index=True)


SCAN_ROWS = 512


def _block_scan(a, b, row, reverse):
    for d in (1, 2, 4):
        if reverse:
            shift, keep = V7X_SUBLANES - d, row < V7X_SUBLANES - d
        else:
            shift, keep = d, row >= d
        a_s = pltpu.roll(a, shift, 0)
        b_s = pltpu.roll(b, shift, 0)
        b = jnp.where(keep, a * b_s + b, b)
        a = jnp.where(keep, a * a_s, a)
    return a, b


def _scan_fwd(a, u, proj, name):
    s, w = a.shape
    ts = min(SCAN_ROWS, s)
    sub = ts // V7X_SUBLANES

    def body(a_ref, u_ref, yr_ref, h_ref, hp_ref, rec_ref, carry):
        @pl.when(pl.program_id(0) == 0)
        def _():
            carry[...] = jnp.zeros_like(carry)

        row = lax.broadcasted_iota(jnp.int32, (V7X_SUBLANES, w), 0)

        def step(j, c):
            rows = pl.ds(pl.multiple_of(j * V7X_SUBLANES, V7X_SUBLANES), V7X_SUBLANES)
            pa, pb = _block_scan(a_ref[rows, :], u_ref[rows, :], row, False)
            h = pb + pa * c
            h_ref[rows, :] = h
            hp_ref[rows, :] = jnp.where(row >= 1, pltpu.roll(h, 1, 0), c)
            return jnp.broadcast_to(h[V7X_SUBLANES - 1 :], (V7X_SUBLANES, w))

        carry[...] = lax.fori_loop(0, sub, step, carry[...])
        rec_ref[...] = _recin_fn(h_ref[...], yr_ref[...]).astype(rec_ref.dtype)

    spec = pl.BlockSpec((ts, w), lambda i: (i, 0))
    return pl.pallas_call(
        body,
        name=name,
        grid=(s // ts,),
        in_specs=[spec, spec, pl.BlockSpec((ts, w), lambda i: (i, 1))],
        out_specs=[spec, spec, spec],
        out_shape=[jax.ShapeDtypeStruct((s, w), F32)] * 2 + [jax.ShapeDtypeStruct((s, w), BF16)],
        scratch_shapes=[pltpu.VMEM((V7X_SUBLANES, w), F32)],
        compiler_params=_cparams(dimension_semantics=("arbitrary",)),
    )(a, u, proj)


def _scan_bwd(a, dh, name):
    s, w = a.shape
    ts = min(SCAN_ROWS, s)
    sub = ts // V7X_SUBLANES
    steps = s // ts

    def body(a_ref, d_ref, g_ref, carry):
        @pl.when(pl.program_id(0) == 0)
        def _():
            carry[...] = jnp.zeros_like(carry)

        row = lax.broadcasted_iota(jnp.int32, (V7X_SUBLANES, w), 0)

        def step(jj, c):
            j = sub - 1 - jj
            rows = pl.ds(pl.multiple_of(j * V7X_SUBLANES, V7X_SUBLANES), V7X_SUBLANES)
            av, dv = a_ref[rows, :], d_ref[rows, :]
            pa, pb = _block_scan(av, av * dv, row, True)
            big = pb + pa * c
            g_ref[rows, :] = dv + jnp.where(row < V7X_SUBLANES - 1, pltpu.roll(big, V7X_SUBLANES - 1, 0), c)
            return jnp.broadcast_to(big[:1], (V7X_SUBLANES, w))

        carry[...] = lax.fori_loop(0, sub, step, carry[...])

    spec = pl.BlockSpec((ts, w), lambda i: (steps - 1 - i, 0))
    return pl.pallas_call(
        body,
        name=name,
        grid=(steps,),
        in_specs=[spec, spec],
        out_specs=spec,
        out_shape=jax.ShapeDtypeStruct((s, w), F32),
        scratch_shapes=[pltpu.VMEM((V7X_SUBLANES, w), F32)],
        compiler_params=_cparams(dimension_semantics=("arbitrary",)),
    )(a, dh)


SKEW = 4 * ATT_TQ


def _skew_onehot():
    t = np.arange(SKEW)
    diag = np.where(t < 3 * ATT_TQ, -t, SKEW - t)
    idx = np.clip(diag + LEFT_CHUNKS * CHUNK, -MAX_REL, MAX_REL) + MAX_REL
    hit = (idx[:, None] == np.arange(2 * MAX_REL + 1)[None, :]) & (t[:, None] != 3 * ATT_TQ)
    return hit.astype(np.float32)


def _bias_tile(rel_bias, name):
    per_t = jnp.dot(rel_bias, jnp.asarray(_skew_onehot()).T, precision=lax.Precision.HIGHEST)
    win = 3 * ATT_TQ

    def body(t_ref, o_ref):
        tile = pltpu.roll(jnp.broadcast_to(t_ref[0], (ATT_TQ, SKEW)), 0, 1, stride=1, stride_axis=0)[:, :win]
        qc = lax.broadcasted_iota(jnp.int32, (ATT_TQ, win), 0) // CHUNK
        kpos = lax.broadcasted_iota(jnp.int32, (ATT_TQ, win), 1)
        band = (kpos // CHUNK >= qc) & (kpos // CHUNK <= qc + LEFT_CHUNKS)
        for v in range(3):
            o_ref[v, 0] = jnp.where(band & (kpos >= (2 - v) * ATT_TQ), tile, NEG)

    return pl.pallas_call(
        body,
        name=name,
        grid=(ATT_HEADS,),
        in_specs=[pl.BlockSpec((1, 1, SKEW), lambda h: (h, 0, 0))],
        out_specs=pl.BlockSpec((3, 1, ATT_TQ, win), lambda h: (0, h, 0, 0)),
        out_shape=jax.ShapeDtypeStruct((3, ATT_HEADS, ATT_TQ, win), F32),
        compiler_params=_cparams(dimension_semantics=("parallel",)),
    )(per_t.reshape(ATT_HEADS, 1, SKEW))


def _bias_grad(dbias, name):
    win = 3 * ATT_TQ

    def body(d_ref, o_ref):
        d = jnp.concatenate([d_ref[0], jnp.zeros((ATT_TQ, SKEW - win), F32)], axis=1)
        r = lax.broadcasted_iota(jnp.int32, (ATT_TQ, ATT_TQ), 0)
        c = lax.broadcasted_iota(jnp.int32, (ATT_TQ, ATT_TQ), 1)
        flip = (r + c == ATT_TQ - 1).astype(F32)
        d = jnp.dot(flip, d, preferred_element_type=F32, precision=lax.Precision.HIGHEST)
        o_ref[0] = jnp.sum(pltpu.roll(d, SKEW - (ATT_TQ - 1), 1, stride=1, stride_axis=0), axis=0, keepdims=True)

    per_t = pl.pallas_call(
        body,
        name=name,
        grid=(ATT_HEADS,),
        in_specs=[pl.BlockSpec((1, ATT_TQ, win), lambda h: (h, 0, 0))],
        out_specs=pl.BlockSpec((1, 1, SKEW), lambda h: (h, 0, 0)),
        out_shape=jax.ShapeDtypeStruct((ATT_HEADS, 1, SKEW), F32),
        compiler_params=_cparams(dimension_semantics=("parallel",)),
    )(dbias)
    return jnp.dot(per_t.reshape(ATT_HEADS, SKEW), jnp.asarray(_skew_onehot()), precision=lax.Precision.HIGHEST)


ATT_STEP_HEADS = ATT_HEADS
ATT_STEP_COLS = ATT_STEP_HEADS * ATT_HEAD_DIM


def _attn_specs(nt):
    qb, kb, vb = OFF_Q // ATT_STEP_COLS, OFF_K // ATT_STEP_COLS, OFF_V // ATT_STEP_COLS
    blk = (ATT_TQ, ATT_STEP_COLS)

    def qmap(base):
        return lambda hp, m: (jnp.minimum(m, nt - 1), base + hp)

    def wmap(base, back):
        return lambda hp, m: (jnp.clip(m - back, 0, nt - 1), base + hp)

    specs = [pl.BlockSpec(blk, qmap(qb))]
    specs += [pl.BlockSpec(blk, wmap(kb, back)) for back in (2, 1, 0)]
    specs += [pl.BlockSpec(blk, wmap(vb, back)) for back in (2, 1, 0)]
    return specs


ATT_SCALE = ATT_HEAD_DIM**-0.5


def _attn_exp(qh, kh, bias):
    s = lax.dot_general(qh, kh, _DIMS["nt"], preferred_element_type=F32) + bias
    e = jnp.exp(s - jnp.max(s, axis=-1, keepdims=True))
    return e, jnp.sum(e, axis=-1, keepdims=True)


def _attn_window(k0, k1, k2, v0, v1, v2):
    k = jnp.concatenate([k0[...], k1[...], k2[...]], axis=0).astype(BF16)
    v = jnp.concatenate([v0[...], v1[...], v2[...]], axis=0).astype(BF16)
    return k, v


def _bias_spec():
    return pl.BlockSpec((1, ATT_STEP_HEADS, ATT_TQ, 3 * ATT_TQ), lambda hp, m: (jnp.minimum(m, 2), hp, 0, 0))


def _attn_fwd(proj, bias, name):
    s = proj.shape[0]
    nt = s // ATT_TQ

    def body(q_ref, k0, k1, k2, v0, v1, v2, b_ref, o_ref):
        k, v = _attn_window(k0, k1, k2, v0, v1, v2)
        q = (q_ref[...] * ATT_SCALE).astype(BF16)
        for hh in range(ATT_STEP_HEADS):
            cols = slice(hh * ATT_HEAD_DIM, (hh + 1) * ATT_HEAD_DIM)
            e, total = _attn_exp(q[:, cols], k[:, cols], b_ref[0, hh])
            o = jnp.dot(e.astype(BF16), v[:, cols], preferred_element_type=F32) / total
            o_ref[:, cols] = o.astype(o_ref.dtype)

    specs = _attn_specs(nt) + [_bias_spec()]
    return pl.pallas_call(
        body,
        name=name,
        grid=(ATT_HEADS // ATT_STEP_HEADS, nt),
        in_specs=specs,
        out_specs=pl.BlockSpec((ATT_TQ, ATT_STEP_COLS), lambda hp, m: (m, hp)),
        out_shape=jax.ShapeDtypeStruct((s, ATT_WIDTH), BF16),
        compiler_params=_cparams(dimension_semantics=("parallel", "arbitrary")),
    )(proj, proj, proj, proj, proj, proj, proj, bias)


def _attn_bwd(proj, bias, do, name):
    s = proj.shape[0]
    nt = s // ATT_TQ
    win = 3 * ATT_TQ

    def body(q_ref, k0, k1, k2, v0, v1, v2, do_ref, b_ref, dq_ref, dk_ref, dv_ref, db_ref, dk_acc, dv_acc):
        m = pl.program_id(1)

        @pl.when(m == 0)
        def _():
            dk_acc[...] = jnp.zeros_like(dk_acc)
            dv_acc[...] = jnp.zeros_like(dv_acc)
            db_ref[...] = jnp.zeros_like(db_ref)

        @pl.when(m < nt)
        def _():
            k, v = _attn_window(k0, k1, k2, v0, v1, v2)
            q = (q_ref[...] * ATT_SCALE).astype(BF16)
            dout = do_ref[...]
            for hh in range(ATT_STEP_HEADS):
                cols = slice(hh * ATT_HEAD_DIM, (hh + 1) * ATT_HEAD_DIM)
                qh, kh, vh, doh = q[:, cols], k[:, cols], v[:, cols], dout[:, cols]
                e, total = _attn_exp(qh, kh, b_ref[0, hh])
                p = e / total
                dvh = lax.dot_general(p.astype(BF16), doh, _DIMS["tn"], preferred_element_type=F32)
                dp = lax.dot_general(doh, vh, _DIMS["nt"], preferred_element_type=F32)
                ds = p * (dp - jnp.sum(dp * p, axis=-1, keepdims=True))
                db_ref[hh] += ds
                dsb = ds.astype(BF16)
                dqh = jnp.dot(dsb, kh, preferred_element_type=F32) * ATT_SCALE
                dkh = lax.dot_general(dsb, qh, _DIMS["tn"], preferred_element_type=F32)
                dq_ref[:, cols] = dqh.astype(dq_ref.dtype)
                dk_acc[:, cols] += dkh
                dv_acc[:, cols] += dvh

        dk_ref[...] = dk_acc[:ATT_TQ].astype(dk_ref.dtype)
        dv_ref[...] = dv_acc[:ATT_TQ].astype(dv_ref.dtype)
        for acc in (dk_acc, dv_acc):
            rest = acc[ATT_TQ:]
            acc[: win - ATT_TQ] = rest
            acc[win - ATT_TQ :] = jnp.zeros((ATT_TQ, ATT_STEP_COLS), F32)

    blk = (ATT_TQ, ATT_STEP_COLS)
    specs = _attn_specs(nt)
    specs.append(pl.BlockSpec(blk, lambda hp, m: (jnp.minimum(m, nt - 1), hp)))
    specs.append(_bias_spec())
    done = lambda hp, m: (jnp.maximum(m - 2, 0), hp)
    out_specs = [
        pl.BlockSpec(blk, lambda hp, m: (jnp.minimum(m, nt - 1), hp)),
        pl.BlockSpec(blk, done),
        pl.BlockSpec(blk, done),
        pl.BlockSpec((ATT_STEP_HEADS, ATT_TQ, win), lambda hp, m: (hp, 0, 0)),
    ]
    out_shape = [jax.ShapeDtypeStruct((s, ATT_WIDTH), BF16)] * 3
    out_shape.append(jax.ShapeDtypeStruct((ATT_HEADS, ATT_TQ, win), F32))
    return pl.pallas_call(
        body,
        name=name,
        grid=(ATT_HEADS // ATT_STEP_HEADS, nt + 2),
        in_specs=specs,
        out_specs=out_specs,
        out_shape=out_shape,
        scratch_shapes=[pltpu.VMEM((win, ATT_STEP_COLS), F32), pltpu.VMEM((win, ATT_STEP_COLS), F32)],
        compiler_params=_cparams(dimension_semantics=("arbitrary", "arbitrary")),
    )(proj, proj, proj, proj, proj, proj, proj, do, bias)


def _ada_fwd(c_all, w, name):
    def body(c_ref, w_ref, o_ref):
        act = _silu(c_ref[...]).astype(BF16)
        o_ref[...] = jnp.dot(act, w_ref[...].astype(BF16), preferred_element_type=F32)

    return pl.pallas_call(
        body, name=name, out_shape=jax.ShapeDtypeStruct((c_all.shape[0], w.shape[1]), F32), compiler_params=_cparams()
    )(c_all, w)


def _ada_bwd(c_all, dmod, name):
    def body(c_ref, d_ref, o_ref):
        act = _silu(c_ref[...])
        o_ref[...] = lax.dot_general(act, d_ref[...], _DIMS["tn"], preferred_element_type=F32,
                                     precision=lax.Precision.HIGHEST)

    return pl.pallas_call(
        body, name=name, out_shape=jax.ShapeDtypeStruct((c_all.shape[1], dmod.shape[1]), F32), compiler_params=_cparams()
    )(c_all, dmod)


def _adamw_parts(landed, sent, me, w, m, v, name, rows=256):
    r, c = w.shape
    tr = _pick(r, rows, 16)

    def body(me_ref, g_ref, own_ref, w_ref, m_ref, v_ref, go_ref, d_ref, mo_ref, vo_ref):
        mine = me_ref[0]
        grad = jnp.zeros((tr, c), F32)
        for d in range(N_DEV):
            grad = grad + jnp.where(mine == d, own_ref[0], g_ref[d]).astype(F32)
        _adamw_update(grad, w_ref, m_ref, v_ref, go_ref, d_ref, mo_ref, vo_ref)

    spec = pl.BlockSpec((tr, c), lambda i, me_ref: (i, 0))
    return pl.pallas_call(
        body,
        name=name,
        grid_spec=pltpu.PrefetchScalarGridSpec(
            num_scalar_prefetch=1,
            grid=(r // tr,),
            in_specs=[pl.BlockSpec((N_DEV, tr, c), lambda i, me_ref: (0, i, 0)),
                      pl.BlockSpec((1, tr, c), lambda i, me_ref: (me_ref[0], i, 0)), spec, spec, spec],
            out_specs=[spec] * 4,
        ),
        out_shape=[jax.ShapeDtypeStruct((r, c), F32)] * 4,
        compiler_params=_cparams(dimension_semantics=("parallel",)),
    )(me.reshape(1).astype(jnp.int32), landed, sent, w, m, v)


def _adamw_update(grad, w_ref, m_ref, v_ref, go_ref, d_ref, mo_ref, vo_ref):
    m2 = ADAM_B1 * m_ref[...] + (1.0 - ADAM_B1) * grad
    v2 = ADAM_B2 * v_ref[...] + (1.0 - ADAM_B2) * (grad * grad)
    m_hat = m2 / (1.0 - ADAM_B1**ADAM_STEP)
    v_hat = v2 / (1.0 - ADAM_B2**ADAM_STEP)
    go_ref[...] = grad
    d_ref[...] = -ADAM_LR * (m_hat / (jnp.sqrt(v_hat) + ADAM_EPS) + ADAM_WD * w_ref[...])
    mo_ref[...] = m2
    vo_ref[...] = v2


def _adamw(g, w, m, v, name, rows=256):
    r, c = w.shape
    tr = _pick(r, rows, 16)

    def body(g_ref, w_ref, m_ref, v_ref, go_ref, d_ref, mo_ref, vo_ref):
        _adamw_update(g_ref[...], w_ref, m_ref, v_ref, go_ref, d_ref, mo_ref, vo_ref)

    spec = pl.BlockSpec((tr, c), lambda i: (i, 0))
    return pl.pallas_call(
        body,
        name=name,
        grid=(r // tr,),
        in_specs=[spec, spec, spec, spec],
        out_specs=[spec] * 4,
        out_shape=[jax.ShapeDtypeStruct((r, c), F32)] * 4,
        compiler_params=_cparams(dimension_semantics=("parallel",)),
    )(g, w, m, v)


def _sum_parts(parts, name):
    def body(p_ref, o_ref):
        acc = p_ref[0]
        for d in range(1, N_DEV):
            acc = acc + p_ref[d]
        o_ref[...] = acc

    return pl.pallas_call(
        body, name=name, out_shape=jax.ShapeDtypeStruct(parts.shape[1:], F32), compiler_params=_cparams()
    )(parts)


def _place():
    x, y, c = lax.axis_index("x"), lax.axis_index("y"), lax.axis_index("c")
    return x, y, c


def _dev_index(p):
    return 4 * p[0] + 2 * p[1] + p[2]


def _allgather_vmem(shard, name):
    m_per, n = shard.shape

    def body(x_ref, out_ref, send_sems, recv_sems, local_sem):
        x, y, c = _place()
        me, sibling = (x, y, c), (x, y, 1 - c)
        chips = [(1 - x, y), (x, 1 - y), (1 - x, 1 - y)]

        def rows(p):
            return out_ref.at[pl.ds(_dev_index(p) * m_per, m_per), :]

        def copy(k, block, to, src=None):
            return pltpu.make_async_remote_copy(
                src_ref=rows(block) if src is None else src, dst_ref=rows(block),
                send_sem=send_sems.at[k], recv_sem=recv_sems.at[k], device_id=to, device_id_type=MESH)

        mine = pltpu.make_async_copy(x_ref, rows(me), local_sem)
        mine.start()
        first = [copy(0, me, sibling, src=x_ref)]
        first += [copy(1 + j, me, (*chip, c), src=x_ref) for j, chip in enumerate(chips)]
        for cp in first:
            cp.start()
        passed = [copy(4 + j, (*chip, c), sibling) for j, chip in enumerate(chips)]
        for j, chip in enumerate(chips):
            copy(1 + j, (*chip, c), me).wait_recv()
            passed[j].start()
        copy(0, sibling, me).wait_recv()
        for j, chip in enumerate(chips):
            copy(4 + j, (*chip, 1 - c), me).wait_recv()
        for cp in first + passed:
            cp.wait_send()
        mine.wait()

    return pl.pallas_call(
        body,
        name=name,
        out_shape=jax.ShapeDtypeStruct((N_DEV * m_per, n), shard.dtype),
        in_specs=[pl.BlockSpec(memory_space=pltpu.VMEM)],
        out_specs=pl.BlockSpec(memory_space=pltpu.VMEM),
        scratch_shapes=[pltpu.SemaphoreType.DMA((7,)), pltpu.SemaphoreType.DMA((7,)), pltpu.SemaphoreType.DMA],
        compiler_params=_cparams(),
    )(shard)


def _allgather_hbm(shards, name):
    n = len(shards)

    def body(*refs):
        ins, outs = refs[:n], refs[n : 2 * n]
        send_sems, recv_sems, local_sems = refs[2 * n :]
        x, y, c = _place()
        me, sibling = (x, y, c), (x, y, 1 - c)
        chips = [(1 - x, y), (x, 1 - y), (1 - x, 1 - y)]

        def copy(a, k, block, to, src=None):
            dst = outs[a].at[_dev_index(block)]
            return pltpu.make_async_remote_copy(
                src_ref=dst if src is None else src, dst_ref=dst,
                send_sem=send_sems.at[a * 7 + k], recv_sem=recv_sems.at[a * 7 + k], device_id=to, device_id_type=MESH)

        mine = [pltpu.make_async_copy(ins[a], outs[a].at[_dev_index(me)], local_sems.at[a]) for a in range(n)]
        for cp in mine:
            cp.start()
        first = []
        for a in range(n):
            first.append(copy(a, 0, me, sibling, src=ins[a]))
            first += [copy(a, 1 + j, me, (*chip, c), src=ins[a]) for j, chip in enumerate(chips)]
        for cp in first:
            cp.start()
        passed = []
        for j, chip in enumerate(chips):
            for a in range(n):
                copy(a, 1 + j, (*chip, c), me).wait_recv()
                cp = copy(a, 4 + j, (*chip, c), sibling)
                cp.start()
                passed.append(cp)
        for a in range(n):
            copy(a, 0, sibling, me).wait_recv()
        for j, chip in enumerate(chips):
            for a in range(n):
                copy(a, 4 + j, (*chip, 1 - c), me).wait_recv()
        for cp in first + passed:
            cp.wait_send()
        for cp in mine:
            cp.wait()

    any_spec = pl.BlockSpec(memory_space=pl.ANY)
    return pl.pallas_call(
        body,
        name=name,
        out_shape=[jax.ShapeDtypeStruct((N_DEV, *s.shape), s.dtype) for s in shards],
        in_specs=[any_spec] * n,
        out_specs=[any_spec] * n,
        scratch_shapes=[pltpu.SemaphoreType.DMA((7 * n,)), pltpu.SemaphoreType.DMA((7 * n,)),
                        pltpu.SemaphoreType.DMA((n,))],
        compiler_params=_cparams(),
    )(*shards)


HBM_SPEC = pl.BlockSpec(memory_space=pltpu.HBM)
SEM_SPEC = pl.BlockSpec(memory_space=pltpu.SEMAPHORE)
EFFECT = pltpu.SideEffectType.DATAFLOW_SIDE_EFFECTING


def _peers(x, y, c):
    return [(1 - x if k & 4 else x, 1 - y if k & 2 else y, 1 - c if k & 1 else c) for k in range(1, N_DEV)]


def _push_peers(mode, x, y, c):
    if mode == "all":
        return _peers(x, y, c)
    return [(x, y, 1 - c), (1 - x, y, c), (x, 1 - y, c), (1 - x, 1 - y, c)]


def _push_start(groups, sliced, name, after=(), modes=None):
    flat = [b for g in groups for b in g]
    n, ng = len(flat), len(groups)
    sizes = [len(g) for g in groups]
    modes = modes or ["all"] * ng
    fan = [len(_push_peers(m, 0, 0, 0)) for m in modes]
    per = 2 if sliced else 3
    lands = [lax.empty(b.shape if sliced else (N_DEV, *b.shape), b.dtype) for b in flat]

    def body(*refs):
        ins, lnd = refs[:n], refs[n : 2 * n]
        sems = refs[2 * n + len(after) : 2 * n + len(after) + per * ng]
        token = refs[-1]
        x, y, c = _place()
        me = _dev_index((x, y, c))
        if not sliced:
            first = 0
            for gi, size in enumerate(sizes):
                for j in range(first, first + size):
                    pltpu.make_async_copy(ins[j], lnd[j].at[me], sems[per * gi + 2].at[j - first]).start()
                first += size
        first = 0
        for gi, size in enumerate(sizes):
            for k, peer in enumerate(_push_peers(modes[gi], x, y, c)):
                for j in range(first, first + size):
                    sem = (j - first) * fan[gi] + k
                    pltpu.make_async_remote_copy(
                        src_ref=ins[j].at[_dev_index(peer)] if sliced else ins[j], dst_ref=lnd[j].at[me],
                        send_sem=sems[per * gi].at[sem], recv_sem=sems[per * gi + 1].at[sem],
                        device_id=peer, device_id_type=MESH).start()
            first += size
        token[...] = jnp.zeros_like(token)

    out_shape = []
    for size, width in zip(sizes, fan):
        out_shape += [pltpu.SemaphoreType.DMA((width * size,)), pltpu.SemaphoreType.DMA((width * size,))]
        out_shape += [] if sliced else [pltpu.SemaphoreType.DMA((size,))]
    out_shape += [pltpu.HBM(b.shape, b.dtype) for b in flat + lands]
    out_shape.append(jax.ShapeDtypeStruct((V7X_SUBLANES, V7X_LANES), F32))
    res = pl.pallas_call(
        body,
        name=name,
        out_shape=tuple(out_shape),
        in_specs=[HBM_SPEC] * (2 * n) + [ANY_SPEC] * len(after),
        out_specs=tuple([SEM_SPEC] * (per * ng) + [HBM_SPEC] * (2 * n) + [pl.BlockSpec(memory_space=pltpu.VMEM)]),
        input_output_aliases={i: per * ng + i for i in range(2 * n)},
        compiler_params=pltpu.CompilerParams(has_side_effects=EFFECT),
    )(*[pltpu.with_memory_space_constraint(b, pltpu.HBM) for b in flat + lands], *after)
    sems, thru, token = res[: per * ng], res[per * ng : per * ng + 2 * n], res[-1]
    out, first = [], 0
    for gi, size in enumerate(sizes):
        out.append((sems[per * gi], sems[per * gi + 1], list(thru[first : first + size]),
                    list(thru[n + first : n + first + size]), None if sliced else sems[per * gi + 2]))
        first += size
    return out, token


def _push_wait(started, sliced, after, name, mode="all"):
    send_sems, recv_sems, bufs, lands, own_sems = started
    n = len(bufs)
    fan = len(_push_peers(mode, 0, 0, 0))
    own = [] if own_sems is None else [own_sems]

    def body(*refs):
        ins, lnd = refs[:n], refs[n : 2 * n]
        send_ref, recv_ref = refs[2 * n], refs[2 * n + 1]
        x, y, c = _place()
        for k, peer in enumerate(_push_peers(mode, x, y, c)):
            for j in range(n):
                cp = pltpu.make_async_remote_copy(
                    src_ref=ins[j].at[_dev_index(peer)] if sliced else ins[j], dst_ref=lnd[j].at[_dev_index(peer)],
                    send_sem=send_ref.at[j * fan + k], recv_sem=recv_ref.at[j * fan + k],
                    device_id=peer, device_id_type=MESH)
                cp.wait_send()
                cp.wait_recv()
        if own:
            for j in range(n):
                pltpu.make_async_copy(ins[j], lnd[j].at[_dev_index((x, y, c))], refs[2 * n + 2].at[j]).wait()

    res = pl.pallas_call(
        body,
        name=name,
        out_shape=tuple(pltpu.HBM(b.shape, b.dtype) for b in bufs + lands),
        in_specs=[HBM_SPEC] * (2 * n) + [SEM_SPEC] * (2 + len(own)) + [pl.BlockSpec(memory_space=pl.ANY)],
        out_specs=tuple([HBM_SPEC] * (2 * n)),
        input_output_aliases={i: i for i in range(2 * n)},
        compiler_params=pltpu.CompilerParams(has_side_effects=EFFECT),
    )(*bufs, *lands, send_sems, recv_sems, *own, after)
    return list(res[:n]), list(res[n:])


def _forward_copies(lnd, send_ref, recv_ref, incoming):
    x, y, c = _place()
    copies = []
    for k, chip in enumerate([(1 - x, y), (x, 1 - y), (1 - x, 1 - y)]):
        mine, theirs = _dev_index((*chip, c)), _dev_index((*chip, 1 - c))
        for j, ref in enumerate(lnd):
            copies.append(pltpu.make_async_remote_copy(
                src_ref=ref.at[mine], dst_ref=ref.at[theirs if incoming else mine],
                send_sem=send_ref.at[j * 3 + k], recv_sem=recv_ref.at[j * 3 + k],
                device_id=(x, y, 1 - c), device_id_type=MESH))
    return copies


def _forward_start(lands, name):
    n = len(lands)

    def body(*refs):
        for cp in _forward_copies(refs[:n], refs[n], refs[n + 1], False):
            cp.start()

    res = pl.pallas_call(
        body,
        name=name,
        out_shape=(pltpu.SemaphoreType.DMA((3 * n,)), pltpu.SemaphoreType.DMA((3 * n,)),
                   *[pltpu.HBM(b.shape, b.dtype) for b in lands]),
        in_specs=[HBM_SPEC] * n,
        out_specs=(SEM_SPEC, SEM_SPEC, *[HBM_SPEC] * n),
        input_output_aliases={i: 2 + i for i in range(n)},
        compiler_params=pltpu.CompilerParams(has_side_effects=EFFECT),
    )(*[pltpu.with_memory_space_constraint(b, pltpu.HBM) for b in lands])
    return res[0], res[1], list(res[2:])


def _forward_wait(started, after, name):
    send_sems, recv_sems, lands = started
    n = len(lands)

    def body(*refs):
        for cp in _forward_copies(refs[:n], refs[n], refs[n + 1], True):
            cp.wait_send()
            cp.wait_recv()

    res = pl.pallas_call(
        body,
        name=name,
        out_shape=tuple(pltpu.HBM(b.shape, b.dtype) for b in lands),
        in_specs=[HBM_SPEC] * n + [SEM_SPEC, SEM_SPEC, pl.BlockSpec(memory_space=pl.ANY)],
        out_specs=tuple([HBM_SPEC] * n),
        input_output_aliases={i: i for i in range(n)},
        compiler_params=pltpu.CompilerParams(has_side_effects=EFFECT),
    )(*lands, send_sems, recv_sems, after)
    return list(res)


def _cols_full(g):
    return jnp.transpose(g, (1, 0, 2)).reshape(g.shape[1], -1)


def _rows_full(g):
    return g.reshape(-1, g.shape[2])


def _cols_parts(full, n=N_DEV):
    r = full.shape[0]
    return jnp.transpose(full.reshape(r, n, -1), (1, 0, 2)).astype(BF16)


def _rows_parts(full):
    return full.reshape(N_DEV, -1, full.shape[1]).astype(BF16)


def _pad_rows(v, rows):
    flat = v.reshape(-1)
    return jnp.pad(flat, (0, rows * D_MODEL - flat.shape[0])).reshape(rows, D_MODEL)


def _my_cols(full, me, width):
    return lax.dynamic_slice_in_dim(full, me * width, width, axis=full.ndim - 1)


def kernel(x, c, w_ada, b_ada, norm_pre, norm_post, ffn1_w_gu, ffn1_w_down, w_in, rel_bias, conv_w, conv_b, lru_wa, lru_ba, lru_wx, lru_bx, lru_lambda, w_att_o, w_rec_o, w_out, ffn2_w_gu, ffn2_w_down, loss_target, m_w_ada, m_b_ada, m_norm_pre, m_norm_post, m_ffn1_w_gu, m_ffn1_w_down, m_w_in, m_rel_bias, m_conv_w, m_conv_b, m_lru_wa, m_lru_ba, m_lru_wx, m_lru_bx, m_lru_lambda, m_w_att_o, m_w_rec_o, m_w_out, m_ffn2_w_gu, m_ffn2_w_down, v_w_ada, v_b_ada, v_norm_pre, v_norm_post, v_ffn1_w_gu, v_ffn1_w_down, v_w_in, v_rel_bias, v_conv_w, v_conv_b, v_lru_wa, v_lru_ba, v_lru_wx, v_lru_bx, v_lru_lambda, v_w_att_o, v_w_rec_o, v_w_out, v_ffn2_w_gu, v_ffn2_w_down):
    weights = dict(w_ada=w_ada, b_ada=b_ada, norm_pre=norm_pre, norm_post=norm_post, ffn1_w_gu=ffn1_w_gu,
                   ffn1_w_down=ffn1_w_down, w_in=w_in, rel_bias=rel_bias, conv_w=conv_w, conv_b=conv_b,
                   lru_wa=lru_wa, lru_ba=lru_ba, lru_wx=lru_wx, lru_bx=lru_bx, lru_lambda=lru_lambda,
                   w_att_o=w_att_o, w_rec_o=w_rec_o, w_out=w_out, ffn2_w_gu=ffn2_w_gu, ffn2_w_down=ffn2_w_down)
    mom1 = dict(w_ada=m_w_ada, b_ada=m_b_ada, norm_pre=m_norm_pre, norm_post=m_norm_post, ffn1_w_gu=m_ffn1_w_gu,
                ffn1_w_down=m_ffn1_w_down, w_in=m_w_in, rel_bias=m_rel_bias, conv_w=m_conv_w, conv_b=m_conv_b,
                lru_wa=m_lru_wa, lru_ba=m_lru_ba, lru_wx=m_lru_wx, lru_bx=m_lru_bx, lru_lambda=m_lru_lambda,
                w_att_o=m_w_att_o, w_rec_o=m_w_rec_o, w_out=m_w_out, ffn2_w_gu=m_ffn2_w_gu, ffn2_w_down=m_ffn2_w_down)
    mom2 = dict(w_ada=v_w_ada, b_ada=v_b_ada, norm_pre=v_norm_pre, norm_post=v_norm_post, ffn1_w_gu=v_ffn1_w_gu,
                ffn1_w_down=v_ffn1_w_down, w_in=v_w_in, rel_bias=v_rel_bias, conv_w=v_conv_w, conv_b=v_conv_b,
                lru_wa=v_lru_wa, lru_ba=v_lru_ba, lru_wx=v_lru_wx, lru_bx=v_lru_bx, lru_lambda=v_lru_lambda,
                w_att_o=v_w_att_o, w_rec_o=v_w_rec_o, w_out=v_w_out, ffn2_w_gu=v_ffn2_w_gu, ffn2_w_down=v_ffn2_w_down)
    order = list(weights)
    big = ["ffn1_w_gu", "ffn1_w_down", "w_in", "w_att_o", "w_rec_o", "w_out", "ffn2_w_gu", "ffn2_w_down"]
    col_sharded = {"ffn1_w_gu", "w_in", "w_att_o", "ffn2_w_gu"}
    small = ["b_ada", "norm_pre", "norm_post", "rel_bias", "conv_w", "conv_b", "lru_wa", "lru_ba", "lru_wx",
             "lru_bx", "lru_lambda"]

    xi, yi, ci = _place()
    me = _dev_index((xi, yi, ci))
    x0 = x[0]
    target = loss_target[0]
    fuse_tm = min(FUSE_TM, x0.shape[0])

    transposed = {"ffn1_w_gu", "w_in", "ffn2_w_gu"}
    local = lambda n, arr: jnp.transpose(arr[0]) if n in transposed else arr[0]
    shards = {n: local(n, weights[n]).astype(BF16) for n in big}
    full_of = lambda n, g: _cols_full(g) if n == "w_att_o" else _rows_full(g)

    pack = jnp.concatenate([c.reshape(-1), norm_pre.reshape(-1), norm_post.reshape(-1), conv_w.reshape(-1)])
    pack = jnp.pad(pack, (0, 3072 - pack.shape[0])).reshape(8, 384)
    got, w1_gu = _allgather_hbm([pack, shards["ffn1_w_gu"]], "gather_first")
    got = got.reshape(N_DEV, 3072)
    c_all = got[:, :1024]
    unshard = lambda blk, rows: jnp.transpose(blk.reshape(N_DEV, rows, 128), (1, 0, 2)).reshape(rows, D_MODEL)
    g_pre = unshard(got[:, 1024:1408], 3)
    g_post = unshard(got[:, 1408:1792], 3)
    conv_taps = unshard(got[:, 1792:2304], 4)
    conv_w8 = jnp.concatenate([conv_taps, jnp.zeros((4, LRU_WIDTH), F32)], axis=0)

    mod_cols = _ada_fwd(c_all, w_ada[0], "ada_fwd")
    mod_all = _allgather_vmem(mod_cols, "gather_mod").reshape(N_DEV, N_DEV, 1152)
    mod = lax.dynamic_index_in_dim(mod_all, me, axis=1, keepdims=False).reshape(1, -1) + b_ada
    mod = mod.reshape(3, 3, 1, D_MODEL)

    w_slab = _slab_weights(lru_wa[0], lru_wx[0])
    bias = _bias_tile(rel_bias[0], "bias_tile")

    res_w = (0.5, 1.0, 0.5)
    row = lambda v: v.reshape(1, -1)

    weight_groups = [["ffn1_w_down"], ["w_in"], ["w_att_o", "w_rec_o", "w_out"], ["ffn2_w_gu", "ffn2_w_down"]]
    weight_modes = ["all", "chip", "all", "all"]
    weights_started, started = _push_start([[shards[n] for n in g] for g in weight_groups], False,
                                           "gather_weights_start", after=(mod, w1_gu), modes=weight_modes)
    full = {"ffn1_w_gu": _rows_full(w1_gu)}

    def gathered_group(gi, after):
        sent, lands = _push_wait(weights_started[gi], False, after, f"gather_weights_wait{gi}", mode=weight_modes[gi])
        if weight_modes[gi] == "chip":
            lands = _forward_wait(_forward_start(lands, f"gather_weights_forward{gi}"), sent[0],
                                  f"gather_weights_forward_wait{gi}")
        for n, land in zip(weight_groups[gi], lands):
            full[n] = full_of(n, land)

    def ffn_fwd(xin, k, gi, tag, deps=(), target=None):
        up = _pre_then_up if k == 2 else _pre_up
        h, a, g, u = up(xin, row(g_pre[k]), mod[k, 0], mod[k, 1], full[f"{tag}_w_gu"], f"{tag}_up", deps=deps)
        if f"{tag}_w_down" not in full:
            gathered_group(gi, a)
        f, *out = _matmul_post(a, full[f"{tag}_w_down"], xin, row(g_post[k]), mod[k, 2], res_w[k], f"{tag}_down",
                               target=target)
        return (out[0] if target is None else out), (h, g, u, a, f)

    x1, saved1 = ffn_fwd(x0, 0, 0, "ffn1", deps=(started,))

    gathered_group(1, x1)
    h2, proj = _pre_matmul(x1, row(g_pre[1]), mod[1, 0], mod[1, 1], full["w_in"], "mix_in",
                           b_shift=3 * ATT_WIDTH // 512)
    att_o = _attn_fwd(proj, bias, "attn_fwd")
    gathered_group(2, att_o)
    xc, pre, a_t, u_t = _lru_front(proj, conv_w8, conv_b, w_slab, lru_ba, lru_bx, lru_lambda, "lru_front")
    hs, h_prev, rec_in = _scan_fwd(a_t, u_t, proj, "lru_scan")
    att = _matmul(att_o, full["w_att_o"], "nn", F32, "att_out")
    rec = _matmul(rec_in, full["w_rec_o"], "nn", F32, "rec_out")
    merged, f2, x2 = _merge_matmul_post(att, rec, proj, full["w_out"], x1, row(g_post[1]), mod[1, 2], res_w[1],
                                        "mix_out")

    gathered_group(3, x2)
    (dy, sq), saved3 = ffn_fwd(x2, 2, 2, "ffn2", target=target)
    loss = lax.psum(0.5 * jnp.sum(sq) / D_MODEL, ("x", "y", "c"))

    grads = {}
    norm_sums = [None] * 6

    pending = []

    def exchange_start(names, tag, after=()):
        send = [(_cols_parts if n == "w_att_o" else _rows_parts)(grads[n]) for n in names]
        (group,), token = _push_start([send], True, f"exchange_{tag}_start", after=after)
        pending.append((names, send, group, tag))
        return token

    def exchange_finish(names, send, group, tag, after):
        sent, lands = _push_wait(group, True, after, f"exchange_{tag}_wait")
        res = None
        for n, land, mine in zip(names, lands, sent):
            res = _adamw_parts(land, mine, me, local(n, weights[n]), local(n, mom1[n]), local(n, mom2[n]),
                               f"adamw_{n}")
            back = (lambda r: jnp.transpose(r)) if n in transposed else (lambda r: r)
            out_g[n], out_d[n], out_m[n], out_v[n] = [back(r).reshape(weights[n].shape) for r in res]
        return res[0]

    out_g, out_d, out_m, out_v = {}, {}, {}, {}

    def ffn_bwd(xin, k, saved, dout, tag):
        h, g, u, a, f = saved
        w_gu, w_down = f"{tag}_w_gu", f"{tag}_w_down"
        df, dgu, norm_sums[2 * k + 1] = _post_bwd_up_bwd(f, dout, row(g_post[k]), mod[k, 2], res_w[k], full[w_down],
                                                          g, u, f"{tag}_up_bwd")
        grads[w_down] = _matmul(a, df, "tn", BF16, f"{tag}_dw_down", tm=1408, tn=1024, tk=DW_TK)
        started = exchange_start([w_down], w_down)
        grads[w_gu] = _dw_gu(dgu, h, f"{tag}_dw_gu", deps=(started,))
        started = exchange_start([w_gu], w_gu)
        halves = [(dgu, (None, fuse_tm, D_FF), lambda i, half=half: (half, i, 0), (half * D_FF, (half + 1) * D_FF))
                  for half in range(2)]
        dx, norm_sums[2 * k] = _matmul_pre_bwd(halves, full[w_gu], xin, dout, row(g_pre[k]),
                                                                mod[k, 0], mod[k, 1], f"{tag}_dh", deps=(started,))
        return dx

    dx2 = ffn_bwd(x2, 2, saved3, dy, "ffn2")

    df2, datt, drec, dg_att, dg_rec, norm_sums[3] = _post_bwd_merge_bwd(
        f2, dx2, row(g_post[1]), mod[1, 2], res_w[1], full["w_out"], att, rec, proj, "mix_dmerged")
    grads["w_out"] = _matmul(merged, df2, "tn", BF16, "mix_dw_out", tm=1024, tn=1024, tk=DW_TK)
    datt_o = _matmul(datt, full["w_att_o"], "nt", BF16, "att_out_bwd")
    grads["w_att_o"] = _matmul(att_o, datt, "tn", BF16, "dw_att_o", tm=512, tn=1024, tk=DW_TK)
    grads["w_rec_o"] = _matmul(rec_in, drec, "tn", BF16, "dw_rec_o", tm=1024, tn=1024, tk=DW_TK)
    started = exchange_start(["w_out", "w_att_o", "w_rec_o"], "mix_out")
    dhs, dyr = _matmul_recin_bwd(drec, full["w_rec_o"], hs, proj, "rec_out_bwd", deps=(started,))
    g_t = _scan_bwd(a_t, dhs, "lru_scan_bwd")
    dpre, dxc, lru_sums = _lru_back(pre, xc, w_slab, lru_ba, lru_bx, lru_lambda, g_t, h_prev, "lru_back")
    dxr, conv_sums = _conv_bwd(proj, conv_w8, dxc, "conv_bwd")
    dq, dk, dv, dbias = _attn_bwd(proj, bias, datt_o, "attn_bwd")
    dproj = jnp.concatenate([dq, dk, dv, dxr, dyr, dg_att, dg_rec], axis=1)
    grads["w_in"] = _matmul(dproj, h2, "tn", BF16, "mix_dw_in", tm=1408, tn=1024, tk=DW_TK)
    pack_mix = jnp.concatenate([conv_sums, lru_sums, _pad_rows(_bias_grad(dbias, "bias_grad"), V7X_SUBLANES),
                                _lru_dw(xc, dpre, "lru_dw").reshape(128, D_MODEL)], axis=0)
    (mix_started,), started = _push_start([[pack_mix]], False, "small_grads_mix_start")
    started = exchange_start(["w_in"], "w_in", after=(started,))
    whole = [(dproj, (fuse_tm, PROJ_WIDTH), lambda i: (i, 0), (0, PROJ_WIDTH))]
    dx1, norm_sums[2] = _matmul_pre_bwd(whole, full["w_in"], x1, dx2, row(g_pre[1]), mod[1, 0],
                                                             mod[1, 1], "mix_dh", deps=(started,))

    dx0 = ffn_bwd(x0, 0, saved1, dx1, "ffn1")

    pack_norm = jnp.concatenate(norm_sums, axis=0)
    (norm_started,), _ = _push_start([[pack_norm]], False, "small_grads_norm_start")

    def summed(started, pack, after, tag):
        _, (parts,) = _push_wait(started, False, after, f"small_grads_{tag}_wait")
        return parts, _sum_parts(parts, f"small_grads_{tag}_sum")

    done = dx0
    last = pending[-1:]
    for names, send, group, tag in pending[:-1]:
        done = exchange_finish(names, send, group, tag, done)

    _, total = summed(mix_started, pack_mix, done, "mix")
    grads["conv_w"] = _my_cols(total[0:4], me, 128)
    grads["conv_b"] = total[4:5]
    grads["lru_ba"] = total[8:9]
    grads["lru_bx"] = total[9:10]
    grads["lru_lambda"] = total[10:11]
    grads["rel_bias"] = total[16:19].reshape(-1)[: ATT_HEADS * (2 * MAX_REL + 1)].reshape(ATT_HEADS, -1)
    grads["lru_wa"] = total[24:88].reshape(LRU_BLOCKS, LRU_BLOCK, LRU_BLOCK)
    grads["lru_wx"] = total[88:152].reshape(LRU_BLOCKS, LRU_BLOCK, LRU_BLOCK)
    parts, total = summed(norm_started, pack_norm, total, "norm")
    by_sandwich = lambda v: v.reshape(*v.shape[:-2], 3, 2 * V7X_SUBLANES, D_MODEL)
    dmod_of = lambda v: jnp.concatenate([by_sandwich(v)[..., 1:3, :], by_sandwich(v)[..., 9:10, :]], axis=-2)
    grads["b_ada"] = dmod_of(total).reshape(1, -1)
    grads["norm_pre"] = _my_cols(by_sandwich(total)[:, 0, :], me, 128)
    grads["norm_post"] = _my_cols(by_sandwich(total)[:, V7X_SUBLANES, :], me, 128)
    dmod_all = dmod_of(parts).reshape(N_DEV, 9 * D_MODEL)
    grads["w_ada"] = _ada_bwd(c_all, _my_cols(dmod_all, me, 1152), "ada_bwd")

    res = _adamw(grads["w_ada"], w_ada[0], m_w_ada[0], v_w_ada[0], "adamw_w_ada")
    out_g["w_ada"], out_d["w_ada"], out_m["w_ada"], out_v["w_ada"] = [r.reshape(w_ada.shape) for r in res]

    sizes = [int(np.prod(weights[n].shape)) for n in small]
    tot = sum(sizes)
    rows_small = -(-tot // (16 * D_MODEL)) * 16
    flat = lambda arrs: jnp.pad(jnp.concatenate([a.reshape(-1) for a in arrs]),
                                (0, rows_small * D_MODEL - tot)).reshape(rows_small, D_MODEL)
    res = _adamw(flat([grads[n] for n in small]), flat([weights[n] for n in small]),
                 flat([mom1[n] for n in small]), flat([mom2[n] for n in small]), "adamw_small", rows=rows_small)
    offs = np.cumsum([0] + sizes)
    for dst, r in zip((out_g, out_d, out_m, out_v), res):
        rf = r.reshape(-1)
        for i, n in enumerate(small):
            dst[n] = rf[offs[i] : offs[i + 1]].reshape(weights[n].shape)

    done = res[0]
    for names, send, group, tag in last:
        done = exchange_finish(names, send, group, tag, done)

    return (loss, dx0[None], *[out_g[n] for n in order], *[out_d[n] for n in order],
            *[out_m[n] for n in order], *[out_v[n] for n in order])
```

```python
import jax
import jax.numpy as jnp
import numpy as np
from jax import lax
from jax.experimental import pallas as pl
from jax.experimental.pallas import tpu as pltpu

D_MODEL = 1024
D_FF = 2816
ATT_HEADS = 8
ATT_HEAD_DIM = 64
ATT_WIDTH = 512
CHUNK = 64
LEFT_CHUNKS = 8
MAX_REL = 128
LRU_WIDTH = 1024
LRU_BLOCKS = 16
LRU_BLOCK = 64
LRU_C = 8.0
EPS = 1e-6
PROJ_WIDTH = 5632
N_DEV = 8

ADAM_LR = 0.001
ADAM_B1 = 0.9
ADAM_B2 = 0.999
ADAM_EPS = 1e-08
ADAM_WD = 0.01
ADAM_STEP = 10

V7X_LANES = 128
V7X_SUBLANES = 8
V7X_VMEM_BYTES = 64 * 1024 * 1024
VMEM_LIMIT = V7X_VMEM_BYTES - 8 * 1024 * 1024

ATT_TQ = 256
NEG = -1e30
BF16 = jnp.bfloat16
F32 = jnp.float32
MESH = pl.DeviceIdType.MESH

OFF_Q = 4 * LRU_WIDTH
OFF_K = OFF_Q + ATT_WIDTH
OFF_V = OFF_K + ATT_WIDTH


def _cparams(**kw):
    return pltpu.CompilerParams(vmem_limit_bytes=VMEM_LIMIT, **kw)


def _pick(n, target, unit=V7X_LANES):
    best = None
    for t in range(unit, min(n, target) + 1, unit):
        if n % t == 0:
            best = t
    return n if best is None else best


_DIMS = {
    "nn": (((1,), (0,)), ((), ())),
    "nt": (((1,), (1,)), ((), ())),
    "tn": (((0,), (0,)), ((), ())),
}


ANY_SPEC = pl.BlockSpec(memory_space=pl.ANY)


def _matmul(a, b, mode, out_dtype, name, tm=1024, tn=512, tk=1408, deps=(), b_shift=0):
    n_deps = len(deps)
    if mode == "nn":
        (m, k), (k2, n) = a.shape, b.shape
    elif mode == "nt":
        (m, k), (n, k2) = a.shape, b.shape
    else:
        (k, m), (k2, n) = a.shape, b.shape
    assert k == k2, (a.shape, b.shape, mode)
    tm, tn, tk = _pick(m, tm), _pick(n, tn), _pick(k, tk)
    nk = k // tk
    dims = _DIMS[mode]

    def body(a_ref, b_ref, *rest):
        o_ref, scratch = rest[n_deps], rest[n_deps + 1 :]
        p = lax.dot_general(a_ref[...], b_ref[...], dims, preferred_element_type=F32)
        if nk == 1:
            o_ref[...] = p.astype(o_ref.dtype)
        else:
            acc = scratch[0]
            kk = pl.program_id(2)

            @pl.when(kk == 0)
            def _():
                acc[...] = p

            @pl.when(kk > 0)
            def _():
                acc[...] += p

            @pl.when(kk == nk - 1)
            def _():
                o_ref[...] = acc[...].astype(o_ref.dtype)

    if mode == "nn":
        a_spec = pl.BlockSpec((tm, tk), lambda i, j, kk: (i, kk))
        b_spec = pl.BlockSpec((tk, tn), lambda i, j, kk: (kk, j))
    elif mode == "nt":
        a_spec = pl.BlockSpec((tm, tk), lambda i, j, kk: (i, kk))
        b_spec = pl.BlockSpec((tn, tk), lambda i, j, kk: ((j + b_shift) % (n // tn), kk))
    else:
        a_spec = pl.BlockSpec((tk, tm), lambda i, j, kk: (kk, i))
        b_spec = pl.BlockSpec((tk, tn), lambda i, j, kk: (kk, j))
    return pl.pallas_call(
        body,
        name=name,
        grid=(m // tm, n // tn, nk),
        in_specs=[a_spec, b_spec] + [ANY_SPEC] * n_deps,
        out_specs=pl.BlockSpec((tm, tn), lambda i, j, kk: (i, j)),
        out_shape=jax.ShapeDtypeStruct((m, n), out_dtype),
        scratch_shapes=[pltpu.VMEM((tm, tn), F32)] if nk > 1 else [],
        compiler_params=_cparams(dimension_semantics=("parallel", "parallel", "arbitrary")),
    )(a, b, *deps)


def _rowwise(fn, name, params, tiles, outs, accs=(), ts=256, with_index=False, deps=()):
    norm = []
    for t in tiles:
        if not isinstance(t, tuple):
            t = (t, t.shape[1], 0)
        norm.append(t if len(t) == 4 else (*t, None))
    s = norm[0][0].shape[0]
    ts = min(ts, s)
    assert s % ts == 0 and ts % V7X_SUBLANES == 0
    steps = s // ts
    halo_blocks = ts // V7X_SUBLANES
    n_p, n_t, n_o = len(params), len(norm), len(outs)

    def body(*refs):
        i = pl.program_id(0)
        vals = [r[...] for r in refs[: n_p + n_t]]
        res = fn(i, steps, *vals) if with_index else fn(*vals)
        if not isinstance(res, (tuple, list)):
            res = (res,)
        first_out = n_p + n_t + len(deps)
        o_refs = refs[first_out : first_out + n_o]
        a_refs = refs[first_out + n_o :]
        for r, v in zip(o_refs, res[:n_o]):
            r[...] = v.astype(r.dtype)
        for r, v in zip(a_refs, res[n_o:]):
            _accumulate(r, v, i)

    in_specs = [pl.BlockSpec(p.shape, lambda i: (0, 0)) for p in params]
    for arr, w, cb, halo in norm:
        if halo is None:
            in_specs.append(pl.BlockSpec((ts, w), lambda i, cb=cb: (i, cb)))
        elif halo == "prev":
            in_specs.append(
                pl.BlockSpec((V7X_SUBLANES, w), lambda i, cb=cb: (jnp.maximum(i * halo_blocks - 1, 0), cb))
            )
        else:
            last = s // V7X_SUBLANES - 1
            in_specs.append(
                pl.BlockSpec((V7X_SUBLANES, w), lambda i, cb=cb: (jnp.minimum((i + 1) * halo_blocks, last), cb))
            )
    in_specs += [ANY_SPEC] * len(deps)
    out_specs = [pl.BlockSpec((ts, w), lambda i: (i, 0)) for w, _ in outs]
    out_specs += [pl.BlockSpec(shape, lambda i: (0, 0)) for shape in accs]
    out_shape = [jax.ShapeDtypeStruct((s, w), dt) for w, dt in outs]
    out_shape += [jax.ShapeDtypeStruct(shape, F32) for shape in accs]
    res = pl.pallas_call(
        body,
        name=name,
        grid=(steps,),
        in_specs=in_specs,
        out_specs=out_specs,
        out_shape=out_shape,
        compiler_params=_cparams(dimension_semantics=("arbitrary",)),
    )(*params, *[t[0] for t in norm], *deps)
    return res


def _accumulate(ref, val, step):
    @pl.when(step == 0)
    def _():
        ref[...] = val

    @pl.when(step > 0)
    def _():
        ref[...] += val


def _sigmoid(z):
    return jax.nn.sigmoid(z)


def _silu(z):
    return z * _sigmoid(z)


def _gelu(z):
    return 0.5 * z * (1.0 + jnp.tanh(0.7978845608028654 * (z + 0.044715 * (z * z * z))))


def _pre_fn(g, shift, scale, x):
    r = lax.rsqrt(jnp.mean(x * x, axis=-1, keepdims=True) + EPS)
    return ((x * r) * g) * (1.0 + scale) + shift


def _post_fn(res_w, g, gate, f, x):
    r = lax.rsqrt(jnp.mean(f * f, axis=-1, keepdims=True) + EPS)
    return x + (res_w * gate) * ((f * r) * g)


def _gates_fn(ba, bx, lam, pre, xc):
    ra = _sigmoid(pre[:, :LRU_WIDTH] + ba)
    ia = _sigmoid(pre[:, LRU_WIDTH:] + bx)
    softplus = jnp.maximum(-lam, 0.0) + jnp.log1p(jnp.exp(-jnp.abs(lam)))
    log_a = (-LRU_C) * ra * softplus
    a = jnp.exp(log_a)
    mult = jnp.sqrt(-jnp.tanh(log_a) * (a * a + 1.0))
    return a, mult * (ia * xc)


def _recin_fn(hs, yr):
    return hs * _gelu(yr)


def _merge_fn(att, rec, g_att, g_rec):
    return _sigmoid(g_att) * att + _sigmoid(g_rec) * rec


def _rowsum(v):
    return jnp.sum(v, axis=0, keepdims=True)


FFN_TM = 512
FFN_TF = 1408


def _glu_fn(g, u):
    return _silu(g) * u


FUSE_TM = 256
DW_TK = 4096
ROW_SPEC2 = pl.BlockSpec((1, D_MODEL), lambda i, j: (0, 0))
ROW_SPEC1 = pl.BlockSpec((1, D_MODEL), lambda i: (0, 0))
SUMS_SPEC1 = pl.BlockSpec((V7X_SUBLANES, D_MODEL), lambda i: (0, 0))
SUMS_SPEC2 = pl.BlockSpec((V7X_SUBLANES, D_MODEL), lambda i, j: (0, 0))
SUMS_SHAPE = jax.ShapeDtypeStruct((V7X_SUBLANES, D_MODEL), F32)


def _sum_rows(*rows):
    pad = jnp.zeros((V7X_SUBLANES - len(rows), rows[0].shape[1]), F32)
    return jnp.concatenate([*rows, pad], axis=0)


def _pre_up(x, g, shift, scale, w_gu_t, name, deps=()):
    s = x.shape[0]
    tm = min(FFN_TM, s)
    nf = D_FF // FFN_TF
    nd = len(deps)

    def body(x_ref, g_ref, sh_ref, sc_ref, wg_ref, wu_ref, *rest):
        h_ref, a_ref, gg_ref, u_ref, h_s = rest[nd:]

        @pl.when(pl.program_id(1) == 0)
        def _():
            h = _pre_fn(g_ref[...], sh_ref[...], sc_ref[...], x_ref[...]).astype(BF16)
            h_s[...] = h
            h_ref[...] = h

        hv = h_s[...]
        gv = lax.dot_general(hv, wg_ref[...], _DIMS["nt"], preferred_element_type=F32)
        uv = lax.dot_general(hv, wu_ref[...], _DIMS["nt"], preferred_element_type=F32)
        a_ref[...] = _glu_fn(gv, uv).astype(a_ref.dtype)
        gg_ref[...] = gv.astype(gg_ref.dtype)
        u_ref[...] = uv.astype(u_ref.dtype)

    rows = pl.BlockSpec((tm, D_MODEL), lambda i, j: (i, 0))
    out = pl.BlockSpec((tm, FFN_TF), lambda i, j: (i, j))
    return pl.pallas_call(
        body,
        name=name,
        grid=(s // tm, nf),
        in_specs=[rows, ROW_SPEC2, ROW_SPEC2, ROW_SPEC2,
                  pl.BlockSpec((FFN_TF, D_MODEL), lambda i, j: (j, 0)),
                  pl.BlockSpec((FFN_TF, D_MODEL), lambda i, j: (nf + j, 0))] + [ANY_SPEC] * nd,
        out_specs=[rows, out, out, out],
        out_shape=[jax.ShapeDtypeStruct((s, D_MODEL), BF16)] + [jax.ShapeDtypeStruct((s, D_FF), BF16)] * 3,
        scratch_shapes=[pltpu.VMEM((tm, D_MODEL), BF16)],
        compiler_params=_cparams(dimension_semantics=("parallel", "arbitrary")),
    )(x, g, shift, scale, w_gu_t, w_gu_t, *deps)


def _pre_matmul(x, g, shift, scale, w_t, name, b_shift=0, tn=512):
    s = x.shape[0]
    n = w_t.shape[0]
    tm = min(2 * FFN_TM, s)

    def body(x_ref, g_ref, sh_ref, sc_ref, w_ref, h_ref, o_ref, h_s):
        @pl.when(pl.program_id(1) == 0)
        def _():
            h = _pre_fn(g_ref[...], sh_ref[...], sc_ref[...], x_ref[...]).astype(BF16)
            h_s[...] = h
            h_ref[...] = h

        o_ref[...] = lax.dot_general(h_s[...], w_ref[...], _DIMS["nt"], preferred_element_type=F32)

    rows = pl.BlockSpec((tm, D_MODEL), lambda i, j: (i, 0))
    return pl.pallas_call(
        body,
        name=name,
        grid=(s // tm, n // tn),
        in_specs=[rows, ROW_SPEC2, ROW_SPEC2, ROW_SPEC2,
                  pl.BlockSpec((tn, D_MODEL), lambda i, j: ((j + b_shift) % (n // tn), 0))],
        out_specs=[rows, pl.BlockSpec((tm, tn), lambda i, j: (i, j))],
        out_shape=[jax.ShapeDtypeStruct((s, D_MODEL), BF16), jax.ShapeDtypeStruct((s, n), F32)],
        scratch_shapes=[pltpu.VMEM((tm, D_MODEL), BF16)],
        compiler_params=_cparams(dimension_semantics=("parallel", "arbitrary")),
    )(x, g, shift, scale, w_t)


def _matmul_post(a, w, x, g_post, gate, res_w, name, target=None):
    s, k = a.shape
    tm = min(FFN_TM, s)
    extra = [] if target is None else [target]

    def body(a_ref, w_ref, x_ref, g_ref, gate_ref, *rest):
        f = jnp.dot(a_ref[...], w_ref[...], preferred_element_type=F32)
        y = _post_fn(res_w, g_ref[...], gate_ref[...], f, x_ref[...])
        if target is None:
            f_ref, y_ref = rest
            y_ref[...] = y
        else:
            t_ref, f_ref, dy_ref, sq_ref = rest
            diff = y - t_ref[...]
            dy_ref[...] = diff * (1.0 / D_MODEL)
            _accumulate(sq_ref, _rowsum(diff * diff), pl.program_id(0))
        f_ref[...] = f

    rows = pl.BlockSpec((tm, D_MODEL), lambda i: (i, 0))
    out_specs, out_shape = [rows, rows], [jax.ShapeDtypeStruct((s, D_MODEL), F32)] * 2
    if target is not None:
        out_specs.append(ROW_SPEC1)
        out_shape.append(jax.ShapeDtypeStruct((1, D_MODEL), F32))
    return pl.pallas_call(
        body,
        name=name,
        grid=(s // tm,),
        in_specs=[pl.BlockSpec((tm, k), lambda i: (i, 0)), pl.BlockSpec((k, D_MODEL), lambda i: (0, 0)), rows,
                  ROW_SPEC1, ROW_SPEC1] + [rows] * len(extra),
        out_specs=out_specs,
        out_shape=out_shape,
        compiler_params=_cparams(dimension_semantics=("arbitrary",)),
    )(a, w, x, g_post, gate, *extra)


def _merge_matmul_post(att, rec, proj, w, x, g_post, gate, res_w, name):
    s = att.shape[0]
    tm = min(FUSE_TM, s)

    def body(att_ref, rec_ref, ga_ref, gr_ref, w_ref, x_ref, g_ref, gate_ref, m_ref, f_ref, y_ref):
        merged = _merge_fn(att_ref[...], rec_ref[...], ga_ref[...], gr_ref[...]).astype(BF16)
        m_ref[...] = merged
        f = jnp.dot(merged, w_ref[...], preferred_element_type=F32)
        f_ref[...] = f
        y_ref[...] = _post_fn(res_w, g_ref[...], gate_ref[...], f, x_ref[...])

    rows = pl.BlockSpec((tm, D_MODEL), lambda i: (i, 0))
    return pl.pallas_call(
        body,
        name=name,
        grid=(s // tm,),
        in_specs=[rows, rows, pl.BlockSpec((tm, D_MODEL), lambda i: (i, 2)), pl.BlockSpec((tm, D_MODEL), lambda i: (i, 3)),
                  pl.BlockSpec(w.shape, lambda i: (0, 0)), rows, ROW_SPEC1, ROW_SPEC1],
        out_specs=[rows, rows, rows],
        out_shape=[jax.ShapeDtypeStruct((s, D_MODEL), BF16)] + [jax.ShapeDtypeStruct((s, D_MODEL), F32)] * 2,
        compiler_params=_cparams(dimension_semantics=("parallel",)),
    )(att, rec, proj, proj, w, x, g_post, gate)


def _post_bwd_merge_bwd(f, dy, g_post, gate, res_w, w, att, rec, proj, name):
    s = f.shape[0]
    tm = min(FUSE_TM, s)

    def body(f_ref, dy_ref, gp_ref, gate_ref, w_ref, att_ref, rec_ref, ga_ref, gr_ref,
             df_ref, datt_ref, drec_ref, dga_ref, dgr_ref, sums_ref):
        i = pl.program_id(0)
        dgp, dgate, df = _post_vjp(res_w, gp_ref[...], gate_ref[...], f_ref[...], dy_ref[...])
        dfb = df.astype(BF16)
        df_ref[...] = dfb
        _accumulate(sums_ref, _sum_rows(dgp, dgate), i)
        dmerged = lax.dot_general(dfb, w_ref[...], _DIMS["nt"], preferred_element_type=F32)
        _, vjp = jax.vjp(_merge_fn, att_ref[...], rec_ref[...], ga_ref[...], gr_ref[...])
        for ref, val in zip((datt_ref, drec_ref, dga_ref, dgr_ref), vjp(dmerged)):
            ref[...] = val.astype(ref.dtype)

    rows = pl.BlockSpec((tm, D_MODEL), lambda i: (i, 0))
    return pl.pallas_call(
        body,
        name=name,
        grid=(s // tm,),
        in_specs=[rows, rows, ROW_SPEC1, ROW_SPEC1, pl.BlockSpec(w.shape, lambda i: (0, 0)), rows, rows,
                  pl.BlockSpec((tm, D_MODEL), lambda i: (i, 2)), pl.BlockSpec((tm, D_MODEL), lambda i: (i, 3))],
        out_specs=[rows] * 5 + [SUMS_SPEC1],
        out_shape=[jax.ShapeDtypeStruct((s, D_MODEL), BF16)] * 5 + [SUMS_SHAPE],
        compiler_params=_cparams(dimension_semantics=("arbitrary",)),
    )(f, dy, g_post, gate, w, att, rec, proj, proj)


def _matmul_recin_bwd(drec, w, hs, proj, name, deps=()):
    s = drec.shape[0]
    tm = min(FUSE_TM, s)
    nd = len(deps)

    def body(d_ref, w_ref, hs_ref, yr_ref, *rest):
        dhs_ref, dyr_ref = rest[nd:]
        d = lax.dot_general(d_ref[...], w_ref[...], _DIMS["nt"], preferred_element_type=F32)
        _, vjp = jax.vjp(_recin_fn, hs_ref[...], yr_ref[...])
        dhs, dyr = vjp(d)
        dhs_ref[...] = dhs
        dyr_ref[...] = dyr.astype(dyr_ref.dtype)

    rows = pl.BlockSpec((tm, D_MODEL), lambda i: (i, 0))
    return pl.pallas_call(
        body,
        name=name,
        grid=(s // tm,),
        in_specs=[rows, pl.BlockSpec(w.shape, lambda i: (0, 0)), rows,
                  pl.BlockSpec((tm, D_MODEL), lambda i: (i, 1))] + [ANY_SPEC] * nd,
        out_specs=[rows, rows],
        out_shape=[jax.ShapeDtypeStruct((s, D_MODEL), F32), jax.ShapeDtypeStruct((s, D_MODEL), BF16)],
        compiler_params=_cparams(dimension_semantics=("parallel",)),
    )(drec, w, hs, proj, *deps)


def _post_vjp(res_w, g, gate, f, dy):
    _, vjp = jax.vjp(lambda g, gate, f: _post_fn(res_w, g, gate, f, 0.0), g, gate, f)
    return vjp(dy)


def _post_bwd_up_bwd(f, dy, g_post, gate, res_w, w_down, g, u, name, deps=()):
    s = f.shape[0]
    tm = min(FFN_TM, s)
    nd = len(deps)

    def body(f_ref, dy_ref, gp_ref, gate_ref, wd_ref, g_ref, u_ref, *rest):
        df_ref, dgu_ref, sums_ref, df_s = rest[nd:]
        i = pl.program_id(0)

        @pl.when(pl.program_id(1) == 0)
        def _():
            dgp, dgate, df = _post_vjp(res_w, gp_ref[...], gate_ref[...], f_ref[...], dy_ref[...])
            df_s[...] = df.astype(BF16)
            df_ref[...] = df_s[...]
            _accumulate(sums_ref, _sum_rows(dgp, dgate), i)

        da = lax.dot_general(df_s[...], wd_ref[...], _DIMS["nt"], preferred_element_type=F32)
        _, vjp = jax.vjp(_glu_fn, g_ref[...].astype(F32), u_ref[...].astype(F32))
        dg, du = vjp(da)
        dgu_ref[0] = dg.astype(dgu_ref.dtype)
        dgu_ref[1] = du.astype(dgu_ref.dtype)

    rows = pl.BlockSpec((tm, D_MODEL), lambda i, j: (i, 0))
    blk = pl.BlockSpec((tm, FFN_TF), lambda i, j: (i, j))
    return pl.pallas_call(
        body,
        name=name,
        grid=(s // tm, D_FF // FFN_TF),
        in_specs=[rows, rows, ROW_SPEC2, ROW_SPEC2, pl.BlockSpec((FFN_TF, D_MODEL), lambda i, j: (j, 0)), blk,
                  blk] + [ANY_SPEC] * nd,
        out_specs=[rows, pl.BlockSpec((2, tm, FFN_TF), lambda i, j: (0, i, j)), SUMS_SPEC2],
        out_shape=[jax.ShapeDtypeStruct((s, D_MODEL), BF16), jax.ShapeDtypeStruct((2, s, D_FF), BF16), SUMS_SHAPE],
        scratch_shapes=[pltpu.VMEM((tm, D_MODEL), BF16)],
        compiler_params=_cparams(dimension_semantics=("arbitrary", "arbitrary")),
    )(f, dy, g_post, gate, w_down, g, u, *deps)


def _matmul_pre_bwd(parts, w_t, x, dres, g, shift, scale, name, deps=()):
    s = x.shape[0]
    na, nd = len(parts), len(deps)
    ranges = [p[3] for p in parts]

    def body(*refs):
        a_refs = refs[:na]
        w_ref, x_ref, dres_ref, g_ref, sh_ref, sc_ref = refs[na : na + 6]
        dx_ref, sums_ref = refs[na + 6 + nd :]
        i = pl.program_id(0)
        dh = None
        for a_ref, (r0, r1) in zip(a_refs, ranges):
            p = jnp.dot(a_ref[...], w_ref[r0:r1, :], preferred_element_type=F32)
            dh = p if dh is None else dh + p
        _, vjp = jax.vjp(_pre_fn, g_ref[...], sh_ref[...], sc_ref[...], x_ref[...])
        dg, dsh, dsc, dx = vjp(dh)
        dx_ref[...] = dx + dres_ref[...]
        _accumulate(sums_ref, _sum_rows(dg, dsh, dsc), i)

    tm = parts[0][1][-2]
    rows = pl.BlockSpec((tm, D_MODEL), lambda i: (i, 0))
    return pl.pallas_call(
        body,
        name=name,
        grid=(s // tm,),
        in_specs=[pl.BlockSpec(p[1], p[2]) for p in parts]
        + [pl.BlockSpec(w_t.shape, lambda i: (0, 0)), rows, rows, ROW_SPEC1, ROW_SPEC1, ROW_SPEC1]
        + [ANY_SPEC] * nd,
        out_specs=[rows, SUMS_SPEC1],
        out_shape=[jax.ShapeDtypeStruct((s, D_MODEL), F32), SUMS_SHAPE],
        compiler_params=_cparams(dimension_semantics=("arbitrary",)),
    )(*[p[0] for p in parts], w_t, x, dres, g, shift, scale, *deps)


def _dw_gu(dgu, h, name, deps=(), tk=DW_TK):
    s = h.shape[0]
    tk = min(tk, s)
    nk = s // tk
    half = D_FF // FFN_TF

    def body(a_ref, b_ref, *rest):
        o_ref = rest[len(deps)]
        kk = pl.program_id(1)
        p = lax.dot_general(a_ref[...], b_ref[...], _DIMS["tn"], preferred_element_type=F32)
        if nk == 1:
            o_ref[...] = p.astype(o_ref.dtype)
            return
        acc = rest[len(deps) + 1]

        @pl.when(kk == 0)
        def _():
            acc[...] = p

        @pl.when(kk > 0)
        def _():
            acc[...] += p

        @pl.when(kk == nk - 1)
        def _():
            o_ref[...] = acc[...].astype(o_ref.dtype)

    return pl.pallas_call(
        body,
        name=name,
        grid=(2 * half, nk),
        in_specs=[pl.BlockSpec((None, tk, FFN_TF), lambda i, kk: (i // half, kk, i % half)),
                  pl.BlockSpec((tk, D_MODEL), lambda i, kk: (kk, 0))] + [ANY_SPEC] * len(deps),
        out_specs=pl.BlockSpec((FFN_TF, D_MODEL), lambda i, kk: (i, 0)),
        out_shape=jax.ShapeDtypeStruct((2 * D_FF, D_MODEL), BF16),
        scratch_shapes=[pltpu.VMEM((FFN_TF, D_MODEL), F32)] if nk > 1 else [],
        compiler_params=_cparams(dimension_semantics=("parallel", "arbitrary")),
    )(dgu, h, *deps)


def _shift_down(ext, j, rows):
    return pltpu.roll(ext, j, 0)[V7X_SUBLANES : V7X_SUBLANES + rows]


def _shift_up(ext, j, rows):
    return pltpu.roll(ext, ext.shape[0] - j, 0)[:rows] if j else ext[:rows]


LRU_SLAB = 256
N_SLABS = LRU_WIDTH // LRU_SLAB


def _slab_weights(wa, wx):
    per = LRU_SLAB // LRU_BLOCK
    eye = jnp.eye(per, dtype=wa.dtype)

    def diag(w):
        w4 = w.reshape(N_SLABS, per, LRU_BLOCK, LRU_BLOCK)
        return jnp.einsum("sbkj,bc->sbkcj", w4, eye).reshape(N_SLABS, LRU_SLAB, LRU_SLAB)

    return jnp.concatenate([diag(wa), diag(wx)], axis=2).reshape(LRU_WIDTH, 2 * LRU_SLAB).astype(BF16)


def _slab_cols(v, s):
    lo = s * LRU_SLAB
    return jnp.concatenate([v[:, lo : lo + LRU_SLAB], v[:, LRU_WIDTH + lo : LRU_WIDTH + lo + LRU_SLAB]], axis=1)


def _lru_front(proj, w8, b, w_slab, ba, bx, lam, name):
    def fn(i, steps, w8, b, w_slab, ba, bx, lam, x, halo):
        halo = jnp.where(i > 0, halo, 0.0)
        ext = jnp.concatenate([halo, x], axis=0)
        xc = b + w8[3:4] * x
        for j in (1, 2, 3):
            xc = xc + w8[3 - j : 4 - j] * _shift_down(ext, j, x.shape[0])
        xcb = xc.astype(BF16)
        prods = []
        for s in range(N_SLABS):
            rows = slice(s * LRU_SLAB, (s + 1) * LRU_SLAB)
            prods.append(jnp.dot(xcb[:, rows], w_slab[rows], preferred_element_type=F32))
        pre = jnp.concatenate([p[:, :LRU_SLAB] for p in prods] + [p[:, LRU_SLAB:] for p in prods], axis=1)
        a, u = _gates_fn(ba, bx, lam, pre, xc)
        return xc, pre, a, u

    tiles = [(proj, LRU_WIDTH, 0), (proj, LRU_WIDTH, 0, "prev")]
    outs = [(LRU_WIDTH, F32), (2 * LRU_WIDTH, F32), (LRU_WIDTH, F32), (LRU_WIDTH, F32)]
    return _rowwise(fn, name, [w8, b, w_slab, ba, bx, lam], tiles, outs, with_index=True)


def _lru_back(pre, xc, w_slab, ba, bx, lam, g, h_prev, name, deps=()):
    def fn(w_slab, ba, bx, lam, pre, xc, g, h_prev):
        _, vjp = jax.vjp(_gates_fn, ba, bx, lam, pre, xc)
        dba, dbx, dlam, dpre, dxc = vjp((g * h_prev, g))
        dpre = dpre.astype(BF16)
        back = []
        for s in range(N_SLABS):
            rows = slice(s * LRU_SLAB, (s + 1) * LRU_SLAB)
            back.append(lax.dot_general(_slab_cols(dpre, s), w_slab[rows], _DIMS["nt"], preferred_element_type=F32))
        return dpre, dxc + jnp.concatenate(back, axis=1), _sum_rows(dba, dbx, dlam)

    return _rowwise(fn, name, [w_slab, ba, bx, lam], [pre, xc, g, h_prev],
                    [(2 * LRU_WIDTH, BF16), (LRU_WIDTH, F32)], [(V7X_SUBLANES, LRU_WIDTH)], deps=deps)


def _lru_dw(xc, dpre, name):
    s = xc.shape[0]
    ts = min(512, s)
    steps = s // ts
    per = LRU_SLAB // LRU_BLOCK

    def body(x_ref, d_ref, o_ref, acc):
        i = pl.program_id(0)
        xcb = x_ref[...].astype(BF16)
        d = d_ref[...]
        for sl in range(N_SLABS):
            rows = slice(sl * LRU_SLAB, (sl + 1) * LRU_SLAB)
            p = lax.dot_general(xcb[:, rows], _slab_cols(d, sl), _DIMS["tn"], preferred_element_type=F32)

            @pl.when(i == 0)
            def _(p=p, rows=rows):
                acc[rows, :] = p

            @pl.when(i > 0)
            def _(p=p, rows=rows):
                acc[rows, :] += p

        @pl.when(i == steps - 1)
        def _():
            for half in range(2):
                for n in range(LRU_BLOCKS):
                    r0 = n * LRU_BLOCK
                    c0 = half * LRU_SLAB + (n % per) * LRU_BLOCK
                    o_ref[half, r0 : r0 + LRU_BLOCK, :] = acc[r0 : r0 + LRU_BLOCK, c0 : c0 + LRU_BLOCK]

    return pl.pallas_call(
        body,
        name=name,
        grid=(steps,),
        in_specs=[pl.BlockSpec((ts, LRU_WIDTH), lambda i: (i, 0)), pl.BlockSpec((ts, 2 * LRU_WIDTH), lambda i: (i, 0))],
        out_specs=pl.BlockSpec((2, LRU_WIDTH, LRU_BLOCK), lambda i: (0, 0, 0)),
        out_shape=jax.ShapeDtypeStruct((2, LRU_WIDTH, LRU_BLOCK), F32),
        scratch_shapes=[pltpu.VMEM((LRU_WIDTH, 2 * LRU_SLAB), F32)],
        compiler_params=_cparams(dimension_semantics=("arbitrary",)),
    )(xc, dpre)


def _conv_bwd(proj, w8, d1, name):
    def fn(i, steps, w8, x, halo, d, d1n):
        rows = x.shape[0]
        dn = jnp.where(i < steps - 1, d1n, 0.0)
        halo = jnp.where(i > 0, halo, 0.0)
        dext = jnp.concatenate([d, dn], axis=0)
        xext = jnp.concatenate([halo, x], axis=0)
        dx = w8[3:4] * d
        dw = [None] * 4
        dw[3] = _rowsum(d * x)
        for k in (1, 2, 3):
            dx = dx + w8[3 - k : 4 - k] * _shift_up(dext, k, rows)
            dw[3 - k] = _rowsum(d * _shift_down(xext, k, rows))
        return dx, _sum_rows(*dw, _rowsum(d))

    tiles = [(proj, LRU_WIDTH, 0), (proj, LRU_WIDTH, 0, "prev"), d1, (d1, LRU_WIDTH, 0, "next")]
    return _rowwise(fn, name, [w8], tiles, [(LRU_WIDTH, BF16)], [(V7X_SUBLANES, LRU_WIDTH)], with_index=True)


SCAN_ROWS = 512


def _block_scan(a, b, row, reverse):
    for d in (1, 2, 4):
        if reverse:
            shift, keep = V7X_SUBLANES - d, row < V7X_SUBLANES - d
        else:
            shift, keep = d, row >= d
        a_s = pltpu.roll(a, shift, 0)
        b_s = pltpu.roll(b, shift, 0)
        b = jnp.where(keep, a * b_s + b, b)
        a = jnp.where(keep, a * a_s, a)
    return a, b


def _scan_fwd(a, u, proj, name):
    s, w = a.shape
    ts = min(SCAN_ROWS, s)
    sub = ts // V7X_SUBLANES

    def body(a_ref, u_ref, yr_ref, h_ref, hp_ref, rec_ref, carry):
        @pl.when(pl.program_id(0) == 0)
        def _():
            carry[...] = jnp.zeros_like(carry)

        row = lax.broadcasted_iota(jnp.int32, (V7X_SUBLANES, w), 0)

        def step(j, c):
            rows = pl.ds(pl.multiple_of(j * V7X_SUBLANES, V7X_SUBLANES), V7X_SUBLANES)
            pa, pb = _block_scan(a_ref[rows, :], u_ref[rows, :], row, False)
            h = pb + pa * c
            h_ref[rows, :] = h
            hp_ref[rows, :] = jnp.where(row >= 1, pltpu.roll(h, 1, 0), c)
            return jnp.broadcast_to(h[V7X_SUBLANES - 1 :], (V7X_SUBLANES, w))

        carry[...] = lax.fori_loop(0, sub, step, carry[...])
        rec_ref[...] = _recin_fn(h_ref[...], yr_ref[...]).astype(rec_ref.dtype)

    spec = pl.BlockSpec((ts, w), lambda i: (i, 0))
    return pl.pallas_call(
        body,
        name=name,
        grid=(s // ts,),
        in_specs=[spec, spec, pl.BlockSpec((ts, w), lambda i: (i, 1))],
        out_specs=[spec, spec, spec],
        out_shape=[jax.ShapeDtypeStruct((s, w), F32)] * 2 + [jax.ShapeDtypeStruct((s, w), BF16)],
        scratch_shapes=[pltpu.VMEM((V7X_SUBLANES, w), F32)],
        compiler_params=_cparams(dimension_semantics=("arbitrary",)),
    )(a, u, proj)


def _scan_bwd(a, dh, name):
    s, w = a.shape
    ts = min(SCAN_ROWS, s)
    sub = ts // V7X_SUBLANES
    steps = s // ts

    def body(a_ref, d_ref, g_ref, carry):
        @pl.when(pl.program_id(0) == 0)
        def _():
            carry[...] = jnp.zeros_like(carry)

        row = lax.broadcasted_iota(jnp.int32, (V7X_SUBLANES, w), 0)

        def step(jj, c):
            j = sub - 1 - jj
            rows = pl.ds(pl.multiple_of(j * V7X_SUBLANES, V7X_SUBLANES), V7X_SUBLANES)
            av, dv = a_ref[rows, :], d_ref[rows, :]
            pa, pb = _block_scan(av, av * dv, row, True)
            big = pb + pa * c
            g_ref[rows, :] = dv + jnp.where(row < V7X_SUBLANES - 1, pltpu.roll(big, V7X_SUBLANES - 1, 0), c)
            return jnp.broadcast_to(big[:1], (V7X_SUBLANES, w))

        carry[...] = lax.fori_loop(0, sub, step, carry[...])

    spec = pl.BlockSpec((ts, w), lambda i: (steps - 1 - i, 0))
    return pl.pallas_call(
        body,
        name=name,
        grid=(steps,),
        in_specs=[spec, spec],
        out_specs=spec,
        out_shape=jax.ShapeDtypeStruct((s, w), F32),
        scratch_shapes=[pltpu.VMEM((V7X_SUBLANES, w), F32)],
        compiler_params=_cparams(dimension_semantics=("arbitrary",)),
    )(a, dh)


SKEW = 4 * ATT_TQ


def _skew_onehot():
    t = np.arange(SKEW)
    diag = np.where(t < 3 * ATT_TQ, -t, SKEW - t)
    idx = np.clip(diag + LEFT_CHUNKS * CHUNK, -MAX_REL, MAX_REL) + MAX_REL
    hit = (idx[:, None] == np.arange(2 * MAX_REL + 1)[None, :]) & (t[:, None] != 3 * ATT_TQ)
    return hit.astype(np.float32)


def _bias_tile(rel_bias, name):
    per_t = jnp.dot(rel_bias, jnp.asarray(_skew_onehot()).T, precision=lax.Precision.HIGHEST)
    win = 3 * ATT_TQ

    def body(t_ref, o_ref):
        tile = pltpu.roll(jnp.broadcast_to(t_ref[0], (ATT_TQ, SKEW)), 0, 1, stride=1, stride_axis=0)[:, :win]
        qc = lax.broadcasted_iota(jnp.int32, (ATT_TQ, win), 0) // CHUNK
        kpos = lax.broadcasted_iota(jnp.int32, (ATT_TQ, win), 1)
        band = (kpos // CHUNK >= qc) & (kpos // CHUNK <= qc + LEFT_CHUNKS)
        for v in range(3):
            o_ref[v, 0] = jnp.where(band & (kpos >= (2 - v) * ATT_TQ), tile, NEG)

    return pl.pallas_call(
        body,
        name=name,
        grid=(ATT_HEADS,),
        in_specs=[pl.BlockSpec((1, 1, SKEW), lambda h: (h, 0, 0))],
        out_specs=pl.BlockSpec((3, 1, ATT_TQ, win), lambda h: (0, h, 0, 0)),
        out_shape=jax.ShapeDtypeStruct((3, ATT_HEADS, ATT_TQ, win), F32),
        compiler_params=_cparams(dimension_semantics=("parallel",)),
    )(per_t.reshape(ATT_HEADS, 1, SKEW))


def _bias_grad(dbias, name):
    win = 3 * ATT_TQ

    def body(d_ref, o_ref):
        d = jnp.concatenate([d_ref[0], jnp.zeros((ATT_TQ, SKEW - win), F32)], axis=1)
        r = lax.broadcasted_iota(jnp.int32, (ATT_TQ, ATT_TQ), 0)
        c = lax.broadcasted_iota(jnp.int32, (ATT_TQ, ATT_TQ), 1)
        flip = (r + c == ATT_TQ - 1).astype(F32)
        d = jnp.dot(flip, d, preferred_element_type=F32, precision=lax.Precision.HIGHEST)
        o_ref[0] = jnp.sum(pltpu.roll(d, SKEW - (ATT_TQ - 1), 1, stride=1, stride_axis=0), axis=0, keepdims=True)

    per_t = pl.pallas_call(
        body,
        name=name,
        grid=(ATT_HEADS,),
        in_specs=[pl.BlockSpec((1, ATT_TQ, win), lambda h: (h, 0, 0))],
        out_specs=pl.BlockSpec((1, 1, SKEW), lambda h: (h, 0, 0)),
        out_shape=jax.ShapeDtypeStruct((ATT_HEADS, 1, SKEW), F32),
        compiler_params=_cparams(dimension_semantics=("parallel",)),
    )(dbias)
    return jnp.dot(per_t.reshape(ATT_HEADS, SKEW), jnp.asarray(_skew_onehot()), precision=lax.Precision.HIGHEST)


ATT_STEP_HEADS = ATT_HEADS
ATT_STEP_COLS = ATT_STEP_HEADS * ATT_HEAD_DIM


def _attn_specs(nt):
    qb, kb, vb = OFF_Q // ATT_STEP_COLS, OFF_K // ATT_STEP_COLS, OFF_V // ATT_STEP_COLS
    blk = (ATT_TQ, ATT_STEP_COLS)

    def qmap(base):
        return lambda hp, m: (jnp.minimum(m, nt - 1), base + hp)

    def wmap(base, back):
        return lambda hp, m: (jnp.clip(m - back, 0, nt - 1), base + hp)

    specs = [pl.BlockSpec(blk, qmap(qb))]
    specs += [pl.BlockSpec(blk, wmap(kb, back)) for back in (2, 1, 0)]
    specs += [pl.BlockSpec(blk, wmap(vb, back)) for back in (2, 1, 0)]
    return specs


ATT_SCALE = ATT_HEAD_DIM**-0.5


def _attn_exp(qh, kh, bias):
    s = lax.dot_general(qh, kh, _DIMS["nt"], preferred_element_type=F32) + bias
    e = jnp.exp(s - jnp.max(s, axis=-1, keepdims=True))
    return e, jnp.sum(e, axis=-1, keepdims=True)


def _attn_window(k0, k1, k2, v0, v1, v2):
    k = jnp.concatenate([k0[...], k1[...], k2[...]], axis=0).astype(BF16)
    v = jnp.concatenate([v0[...], v1[...], v2[...]], axis=0).astype(BF16)
    return k, v


def _bias_spec():
    return pl.BlockSpec((1, ATT_STEP_HEADS, ATT_TQ, 3 * ATT_TQ), lambda hp, m: (jnp.minimum(m, 2), hp, 0, 0))


def _attn_fwd(proj, bias, name):
    s = proj.shape[0]
    nt = s // ATT_TQ

    def body(q_ref, k0, k1, k2, v0, v1, v2, b_ref, o_ref):
        k, v = _attn_window(k0, k1, k2, v0, v1, v2)
        q = (q_ref[...] * ATT_SCALE).astype(BF16)
        for hh in range(ATT_STEP_HEADS):
            cols = slice(hh * ATT_HEAD_DIM, (hh + 1) * ATT_HEAD_DIM)
            e, total = _attn_exp(q[:, cols], k[:, cols], b_ref[0, hh])
            o = jnp.dot(e.astype(BF16), v[:, cols], preferred_element_type=F32) / total
            o_ref[:, cols] = o.astype(o_ref.dtype)

    specs = _attn_specs(nt) + [_bias_spec()]
    return pl.pallas_call(
        body,
        name=name,
        grid=(ATT_HEADS // ATT_STEP_HEADS, nt),
        in_specs=specs,
        out_specs=pl.BlockSpec((ATT_TQ, ATT_STEP_COLS), lambda hp, m: (m, hp)),
        out_shape=jax.ShapeDtypeStruct((s, ATT_WIDTH), BF16),
        compiler_params=_cparams(dimension_semantics=("parallel", "arbitrary")),
    )(proj, proj, proj, proj, proj, proj, proj, bias)


def _attn_bwd(proj, bias, do, name):
    s = proj.shape[0]
    nt = s // ATT_TQ
    win = 3 * ATT_TQ

    def body(q_ref, k0, k1, k2, v0, v1, v2, do_ref, b_ref, dq_ref, dk_ref, dv_ref, db_ref, dk_acc, dv_acc):
        m = pl.program_id(1)

        @pl.when(m == 0)
        def _():
            dk_acc[...] = jnp.zeros_like(dk_acc)
            dv_acc[...] = jnp.zeros_like(dv_acc)
            db_ref[...] = jnp.zeros_like(db_ref)

        @pl.when(m < nt)
        def _():
            k, v = _attn_window(k0, k1, k2, v0, v1, v2)
            q = (q_ref[...] * ATT_SCALE).astype(BF16)
            dout = do_ref[...]
            for hh in range(ATT_STEP_HEADS):
                cols = slice(hh * ATT_HEAD_DIM, (hh + 1) * ATT_HEAD_DIM)
                qh, kh, vh, doh = q[:, cols], k[:, cols], v[:, cols], dout[:, cols]
                e, total = _attn_exp(qh, kh, b_ref[0, hh])
                p = e / total
                dvh = lax.dot_general(p.astype(BF16), doh, _DIMS["tn"], preferred_element_type=F32)
                dp = lax.dot_general(doh, vh, _DIMS["nt"], preferred_element_type=F32)
                ds = p * (dp - jnp.sum(dp * p, axis=-1, keepdims=True))
                db_ref[hh] += ds
                dsb = ds.astype(BF16)
                dqh = jnp.dot(dsb, kh, preferred_element_type=F32) * ATT_SCALE
                dkh = lax.dot_general(dsb, qh, _DIMS["tn"], preferred_element_type=F32)
                dq_ref[:, cols] = dqh.astype(dq_ref.dtype)
                dk_acc[:, cols] += dkh
                dv_acc[:, cols] += dvh

        dk_ref[...] = dk_acc[:ATT_TQ].astype(dk_ref.dtype)
        dv_ref[...] = dv_acc[:ATT_TQ].astype(dv_ref.dtype)
        for acc in (dk_acc, dv_acc):
            rest = acc[ATT_TQ:]
            acc[: win - ATT_TQ] = rest
            acc[win - ATT_TQ :] = jnp.zeros((ATT_TQ, ATT_STEP_COLS), F32)

    blk = (ATT_TQ, ATT_STEP_COLS)
    specs = _attn_specs(nt)
    specs.append(pl.BlockSpec(blk, lambda hp, m: (jnp.minimum(m, nt - 1), hp)))
    specs.append(_bias_spec())
    done = lambda hp, m: (jnp.maximum(m - 2, 0), hp)
    out_specs = [
        pl.BlockSpec(blk, lambda hp, m: (jnp.minimum(m, nt - 1), hp)),
        pl.BlockSpec(blk, done),
        pl.BlockSpec(blk, done),
        pl.BlockSpec((ATT_STEP_HEADS, ATT_TQ, win), lambda hp, m: (hp, 0, 0)),
    ]
    out_shape = [jax.ShapeDtypeStruct((s, ATT_WIDTH), BF16)] * 3
    out_shape.append(jax.ShapeDtypeStruct((ATT_HEADS, ATT_TQ, win), F32))
    return pl.pallas_call(
        body,
        name=name,
        grid=(ATT_HEADS // ATT_STEP_HEADS, nt + 2),
        in_specs=specs,
        out_specs=out_specs,
        out_shape=out_shape,
        scratch_shapes=[pltpu.VMEM((win, ATT_STEP_COLS), F32), pltpu.VMEM((win, ATT_STEP_COLS), F32)],
        compiler_params=_cparams(dimension_semantics=("arbitrary", "arbitrary")),
    )(proj, proj, proj, proj, proj, proj, proj, do, bias)


def _ada_fwd(c_all, w, name):
    def body(c_ref, w_ref, o_ref):
        act = _silu(c_ref[...]).astype(BF16)
        o_ref[...] = jnp.dot(act, w_ref[...].astype(BF16), preferred_element_type=F32)

    return pl.pallas_call(
        body, name=name, out_shape=jax.ShapeDtypeStruct((c_all.shape[0], w.shape[1]), F32), compiler_params=_cparams()
    )(c_all, w)


def _ada_bwd(c_all, dmod, name):
    def body(c_ref, d_ref, o_ref):
        act = _silu(c_ref[...])
        o_ref[...] = lax.dot_general(act, d_ref[...], _DIMS["tn"], preferred_element_type=F32,
                                     precision=lax.Precision.HIGHEST)

    return pl.pallas_call(
        body, name=name, out_shape=jax.ShapeDtypeStruct((c_all.shape[1], dmod.shape[1]), F32), compiler_params=_cparams()
    )(c_all, dmod)


def _adamw_parts(landed, sent, me, w, m, v, name, rows=256):
    r, c = w.shape
    tr = _pick(r, rows, 16)

    def body(me_ref, g_ref, own_ref, w_ref, m_ref, v_ref, go_ref, d_ref, mo_ref, vo_ref):
        mine = me_ref[0]
        grad = jnp.zeros((tr, c), F32)
        for d in range(N_DEV):
            grad = grad + jnp.where(mine == d, own_ref[0], g_ref[d]).astype(F32)
        _adamw_update(grad, w_ref, m_ref, v_ref, go_ref, d_ref, mo_ref, vo_ref)

    spec = pl.BlockSpec((tr, c), lambda i, me_ref: (i, 0))
    return pl.pallas_call(
        body,
        name=name,
        grid_spec=pltpu.PrefetchScalarGridSpec(
            num_scalar_prefetch=1,
            grid=(r // tr,),
            in_specs=[pl.BlockSpec((N_DEV, tr, c), lambda i, me_ref: (0, i, 0)),
                      pl.BlockSpec((1, tr, c), lambda i, me_ref: (me_ref[0], i, 0)), spec, spec, spec],
            out_specs=[spec] * 4,
        ),
        out_shape=[jax.ShapeDtypeStruct((r, c), F32)] * 4,
        compiler_params=_cparams(dimension_semantics=("parallel",)),
    )(me.reshape(1).astype(jnp.int32), landed, sent, w, m, v)


def _adamw_update(grad, w_ref, m_ref, v_ref, go_ref, d_ref, mo_ref, vo_ref):
    m2 = ADAM_B1 * m_ref[...] + (1.0 - ADAM_B1) * grad
    v2 = ADAM_B2 * v_ref[...] + (1.0 - ADAM_B2) * (grad * grad)
    m_hat = m2 / (1.0 - ADAM_B1**ADAM_STEP)
    v_hat = v2 / (1.0 - ADAM_B2**ADAM_STEP)
    go_ref[...] = grad
    d_ref[...] = -ADAM_LR * (m_hat / (jnp.sqrt(v_hat) + ADAM_EPS) + ADAM_WD * w_ref[...])
    mo_ref[...] = m2
    vo_ref[...] = v2


def _adamw(g, w, m, v, name, rows=256):
    r, c = w.shape
    tr = _pick(r, rows, 16)

    def body(g_ref, w_ref, m_ref, v_ref, go_ref, d_ref, mo_ref, vo_ref):
        _adamw_update(g_ref[...], w_ref, m_ref, v_ref, go_ref, d_ref, mo_ref, vo_ref)

    spec = pl.BlockSpec((tr, c), lambda i: (i, 0))
    return pl.pallas_call(
        body,
        name=name,
        grid=(r // tr,),
        in_specs=[spec, spec, spec, spec],
        out_specs=[spec] * 4,
        out_shape=[jax.ShapeDtypeStruct((r, c), F32)] * 4,
        compiler_params=_cparams(dimension_semantics=("parallel",)),
    )(g, w, m, v)


def _sum_parts(parts, name):
    def body(p_ref, o_ref):
        acc = p_ref[0]
        for d in range(1, N_DEV):
            acc = acc + p_ref[d]
        o_ref[...] = acc

    return pl.pallas_call(
        body, name=name, out_shape=jax.ShapeDtypeStruct(parts.shape[1:], F32), compiler_params=_cparams()
    )(parts)


def _place():
    x, y, c = lax.axis_index("x"), lax.axis_index("y"), lax.axis_index("c")
    return x, y, c


def _dev_index(p):
    return 4 * p[0] + 2 * p[1] + p[2]


def _allgather_vmem(shard, name):
    m_per, n = shard.shape

    def body(x_ref, out_ref, send_sems, recv_sems, local_sem):
        x, y, c = _place()
        me, sibling = (x, y, c), (x, y, 1 - c)
        chips = [(1 - x, y), (x, 1 - y), (1 - x, 1 - y)]

        def rows(p):
            return out_ref.at[pl.ds(_dev_index(p) * m_per, m_per), :]

        def copy(k, block, to, src=None):
            return pltpu.make_async_remote_copy(
                src_ref=rows(block) if src is None else src, dst_ref=rows(block),
                send_sem=send_sems.at[k], recv_sem=recv_sems.at[k], device_id=to, device_id_type=MESH)

        mine = pltpu.make_async_copy(x_ref, rows(me), local_sem)
        mine.start()
        first = [copy(0, me, sibling, src=x_ref)]
        first += [copy(1 + j, me, (*chip, c), src=x_ref) for j, chip in enumerate(chips)]
        for cp in first:
            cp.start()
        passed = [copy(4 + j, (*chip, c), sibling) for j, chip in enumerate(chips)]
        for j, chip in enumerate(chips):
            copy(1 + j, (*chip, c), me).wait_recv()
            passed[j].start()
        copy(0, sibling, me).wait_recv()
        for j, chip in enumerate(chips):
            copy(4 + j, (*chip, 1 - c), me).wait_recv()
        for cp in first + passed:
            cp.wait_send()
        mine.wait()

    return pl.pallas_call(
        body,
        name=name,
        out_shape=jax.ShapeDtypeStruct((N_DEV * m_per, n), shard.dtype),
        in_specs=[pl.BlockSpec(memory_space=pltpu.VMEM)],
        out_specs=pl.BlockSpec(memory_space=pltpu.VMEM),
        scratch_shapes=[pltpu.SemaphoreType.DMA((7,)), pltpu.SemaphoreType.DMA((7,)), pltpu.SemaphoreType.DMA],
        compiler_params=_cparams(),
    )(shard)


def _allgather_hbm(shards, name):
    n = len(shards)

    def body(*refs):
        ins, outs = refs[:n], refs[n : 2 * n]
        send_sems, recv_sems, local_sems = refs[2 * n :]
        x, y, c = _place()
        me, sibling = (x, y, c), (x, y, 1 - c)
        chips = [(1 - x, y), (x, 1 - y), (1 - x, 1 - y)]

        def copy(a, k, block, to, src=None):
            dst = outs[a].at[_dev_index(block)]
            return pltpu.make_async_remote_copy(
                src_ref=dst if src is None else src, dst_ref=dst,
                send_sem=send_sems.at[a * 7 + k], recv_sem=recv_sems.at[a * 7 + k], device_id=to, device_id_type=MESH)

        mine = [pltpu.make_async_copy(ins[a], outs[a].at[_dev_index(me)], local_sems.at[a]) for a in range(n)]
        for cp in mine:
            cp.start()
        first = []
        for a in range(n):
            first.append(copy(a, 0, me, sibling, src=ins[a]))
            first += [copy(a, 1 + j, me, (*chip, c), src=ins[a]) for j, chip in enumerate(chips)]
        for cp in first:
            cp.start()
        passed = []
        for j, chip in enumerate(chips):
            for a in range(n):
                copy(a, 1 + j, (*chip, c), me).wait_recv()
                cp = copy(a, 4 + j, (*chip, c), sibling)
                cp.start()
                passed.append(cp)
        for a in range(n):
            copy(a, 0, sibling, me).wait_recv()
        for j, chip in enumerate(chips):
            for a in range(n):
                copy(a, 4 + j, (*chip, 1 - c), me).wait_recv()
        for cp in first + passed:
            cp.wait_send()
        for cp in mine:
            cp.wait()

    any_spec = pl.BlockSpec(memory_space=pl.ANY)
    return pl.pallas_call(
        body,
        name=name,
        out_shape=[jax.ShapeDtypeStruct((N_DEV, *s.shape), s.dtype) for s in shards],
        in_specs=[any_spec] * n,
        out_specs=[any_spec] * n,
        scratch_shapes=[pltpu.SemaphoreType.DMA((7 * n,)), pltpu.SemaphoreType.DMA((7 * n,)),
                        pltpu.SemaphoreType.DMA((n,))],
        compiler_params=_cparams(),
    )(*shards)


HBM_SPEC = pl.BlockSpec(memory_space=pltpu.HBM)
SEM_SPEC = pl.BlockSpec(memory_space=pltpu.SEMAPHORE)
EFFECT = pltpu.SideEffectType.DATAFLOW_SIDE_EFFECTING


def _peers(x, y, c):
    return [(1 - x if k & 4 else x, 1 - y if k & 2 else y, 1 - c if k & 1 else c) for k in range(1, N_DEV)]


def _push_peers(mode, x, y, c):
    if mode == "all":
        return _peers(x, y, c)
    return [(x, y, 1 - c), (1 - x, y, c), (x, 1 - y, c), (1 - x, 1 - y, c)]


def _push_start(groups, sliced, name, after=(), modes=None):
    flat = [b for g in groups for b in g]
    n, ng = len(flat), len(groups)
    sizes = [len(g) for g in groups]
    modes = modes or ["all"] * ng
    fan = [len(_push_peers(m, 0, 0, 0)) for m in modes]
    per = 2 if sliced else 3
    lands = [lax.empty(b.shape if sliced else (N_DEV, *b.shape), b.dtype) for b in flat]

    def body(*refs):
        ins, lnd = refs[:n], refs[n : 2 * n]
        sems = refs[2 * n + len(after) : 2 * n + len(after) + per * ng]
        token = refs[-1]
        x, y, c = _place()
        me = _dev_index((x, y, c))
        if not sliced:
            first = 0
            for gi, size in enumerate(sizes):
                for j in range(first, first + size):
                    pltpu.make_async_copy(ins[j], lnd[j].at[me], sems[per * gi + 2].at[j - first]).start()
                first += size
        first = 0
        for gi, size in enumerate(sizes):
            for k, peer in enumerate(_push_peers(modes[gi], x, y, c)):
                for j in range(first, first + size):
                    sem = (j - first) * fan[gi] + k
                    pltpu.make_async_remote_copy(
                        src_ref=ins[j].at[_dev_index(peer)] if sliced else ins[j], dst_ref=lnd[j].at[me],
                        send_sem=sems[per * gi].at[sem], recv_sem=sems[per * gi + 1].at[sem],
                        device_id=peer, device_id_type=MESH).start()
            first += size
        token[...] = jnp.zeros_like(token)

    out_shape = []
    for size, width in zip(sizes, fan):
        out_shape += [pltpu.SemaphoreType.DMA((width * size,)), pltpu.SemaphoreType.DMA((width * size,))]
        out_shape += [] if sliced else [pltpu.SemaphoreType.DMA((size,))]
    out_shape += [pltpu.HBM(b.shape, b.dtype) for b in flat + lands]
    out_shape.append(jax.ShapeDtypeStruct((V7X_SUBLANES, V7X_LANES), F32))
    res = pl.pallas_call(
        body,
        name=name,
        out_shape=tuple(out_shape),
        in_specs=[HBM_SPEC] * (2 * n) + [ANY_SPEC] * len(after),
        out_specs=tuple([SEM_SPEC] * (per * ng) + [HBM_SPEC] * (2 * n) + [pl.BlockSpec(memory_space=pltpu.VMEM)]),
        input_output_aliases={i: per * ng + i for i in range(2 * n)},
        compiler_params=pltpu.CompilerParams(has_side_effects=EFFECT),
    )(*[pltpu.with_memory_space_constraint(b, pltpu.HBM) for b in flat + lands], *after)
    sems, thru, token = res[: per * ng], res[per * ng : per * ng + 2 * n], res[-1]
    out, first = [], 0
    for gi, size in enumerate(sizes):
        out.append((sems[per * gi], sems[per * gi + 1], list(thru[first : first + size]),
                    list(thru[n + first : n + first + size]), None if sliced else sems[per * gi + 2]))
        first += size
    return out, token


def _push_wait(started, sliced, after, name, mode="all"):
    send_sems, recv_sems, bufs, lands, own_sems = started
    n = len(bufs)
    fan = len(_push_peers(mode, 0, 0, 0))
    own = [] if own_sems is None else [own_sems]

    def body(*refs):
        ins, lnd = refs[:n], refs[n : 2 * n]
        send_ref, recv_ref = refs[2 * n], refs[2 * n + 1]
        x, y, c = _place()
        for k, peer in enumerate(_push_peers(mode, x, y, c)):
            for j in range(n):
                cp = pltpu.make_async_remote_copy(
                    src_ref=ins[j].at[_dev_index(peer)] if sliced else ins[j], dst_ref=lnd[j].at[_dev_index(peer)],
                    send_sem=send_ref.at[j * fan + k], recv_sem=recv_ref.at[j * fan + k],
                    device_id=peer, device_id_type=MESH)
                cp.wait_send()
                cp.wait_recv()
        if own:
            for j in range(n):
                pltpu.make_async_copy(ins[j], lnd[j].at[_dev_index((x, y, c))], refs[2 * n + 2].at[j]).wait()

    res = pl.pallas_call(
        body,
        name=name,
        out_shape=tuple(pltpu.HBM(b.shape, b.dtype) for b in bufs + lands),
        in_specs=[HBM_SPEC] * (2 * n) + [SEM_SPEC] * (2 + len(own)) + [pl.BlockSpec(memory_space=pl.ANY)],
        out_specs=tuple([HBM_SPEC] * (2 * n)),
        input_output_aliases={i: i for i in range(2 * n)},
        compiler_params=pltpu.CompilerParams(has_side_effects=EFFECT),
    )(*bufs, *lands, send_sems, recv_sems, *own, after)
    return list(res[:n]), list(res[n:])


def _forward_copies(lnd, send_ref, recv_ref, incoming):
    x, y, c = _place()
    copies = []
    for k, chip in enumerate([(1 - x, y), (x, 1 - y), (1 - x, 1 - y)]):
        mine, theirs = _dev_index((*chip, c)), _dev_index((*chip, 1 - c))
        for j, ref in enumerate(lnd):
            copies.append(pltpu.make_async_remote_copy(
                src_ref=ref.at[mine], dst_ref=ref.at[theirs if incoming else mine],
                send_sem=send_ref.at[j * 3 + k], recv_sem=recv_ref.at[j * 3 + k],
                device_id=(x, y, 1 - c), device_id_type=MESH))
    return copies


def _forward_start(lands, name):
    n = len(lands)

    def body(*refs):
        for cp in _forward_copies(refs[:n], refs[n], refs[n + 1], False):
            cp.start()

    res = pl.pallas_call(
        body,
        name=name,
        out_shape=(pltpu.SemaphoreType.DMA((3 * n,)), pltpu.SemaphoreType.DMA((3 * n,)),
                   *[pltpu.HBM(b.shape, b.dtype) for b in lands]),
        in_specs=[HBM_SPEC] * n,
        out_specs=(SEM_SPEC, SEM_SPEC, *[HBM_SPEC] * n),
        input_output_aliases={i: 2 + i for i in range(n)},
        compiler_params=pltpu.CompilerParams(has_side_effects=EFFECT),
    )(*[pltpu.with_memory_space_constraint(b, pltpu.HBM) for b in lands])
    return res[0], res[1], list(res[2:])


def _forward_wait(started, after, name):
    send_sems, recv_sems, lands = started
    n = len(lands)

    def body(*refs):
        for cp in _forward_copies(refs[:n], refs[n], refs[n + 1], True):
            cp.wait_send()
            cp.wait_recv()

    res = pl.pallas_call(
        body,
        name=name,
        out_shape=tuple(pltpu.HBM(b.shape, b.dtype) for b in lands),
        in_specs=[HBM_SPEC] * n + [SEM_SPEC, SEM_SPEC, pl.BlockSpec(memory_space=pl.ANY)],
        out_specs=tuple([HBM_SPEC] * n),
        input_output_aliases={i: i for i in range(n)},
        compiler_params=pltpu.CompilerParams(has_side_effects=EFFECT),
    )(*lands, send_sems, recv_sems, after)
    return list(res)


def _cols_full(g):
    return jnp.transpose(g, (1, 0, 2)).reshape(g.shape[1], -1)


def _rows_full(g):
    return g.reshape(-1, g.shape[2])


def _cols_parts(full, n=N_DEV):
    r = full.shape[0]
    return jnp.transpose(full.reshape(r, n, -1), (1, 0, 2)).astype(BF16)


def _rows_parts(full):
    return full.reshape(N_DEV, -1, full.shape[1]).astype(BF16)


def _pad_rows(v, rows):
    flat = v.reshape(-1)
    return jnp.pad(flat, (0, rows * D_MODEL - flat.shape[0])).reshape(rows, D_MODEL)


def _my_cols(full, me, width):
    return lax.dynamic_slice_in_dim(full, me * width, width, axis=full.ndim - 1)


def kernel(x, c, w_ada, b_ada, norm_pre, norm_post, ffn1_w_gu, ffn1_w_down, w_in, rel_bias, conv_w, conv_b, lru_wa, lru_ba, lru_wx, lru_bx, lru_lambda, w_att_o, w_rec_o, w_out, ffn2_w_gu, ffn2_w_down, loss_target, m_w_ada, m_b_ada, m_norm_pre, m_norm_post, m_ffn1_w_gu, m_ffn1_w_down, m_w_in, m_rel_bias, m_conv_w, m_conv_b, m_lru_wa, m_lru_ba, m_lru_wx, m_lru_bx, m_lru_lambda, m_w_att_o, m_w_rec_o, m_w_out, m_ffn2_w_gu, m_ffn2_w_down, v_w_ada, v_b_ada, v_norm_pre, v_norm_post, v_ffn1_w_gu, v_ffn1_w_down, v_w_in, v_rel_bias, v_conv_w, v_conv_b, v_lru_wa, v_lru_ba, v_lru_wx, v_lru_bx, v_lru_lambda, v_w_att_o, v_w_rec_o, v_w_out, v_ffn2_w_gu, v_ffn2_w_down):
    weights = dict(w_ada=w_ada, b_ada=b_ada, norm_pre=norm_pre, norm_post=norm_post, ffn1_w_gu=ffn1_w_gu,
                   ffn1_w_down=ffn1_w_down, w_in=w_in, rel_bias=rel_bias, conv_w=conv_w, conv_b=conv_b,
                   lru_wa=lru_wa, lru_ba=lru_ba, lru_wx=lru_wx, lru_bx=lru_bx, lru_lambda=lru_lambda,
                   w_att_o=w_att_o, w_rec_o=w_rec_o, w_out=w_out, ffn2_w_gu=ffn2_w_gu, ffn2_w_down=ffn2_w_down)
    mom1 = dict(w_ada=m_w_ada, b_ada=m_b_ada, norm_pre=m_norm_pre, norm_post=m_norm_post, ffn1_w_gu=m_ffn1_w_gu,
                ffn1_w_down=m_ffn1_w_down, w_in=m_w_in, rel_bias=m_rel_bias, conv_w=m_conv_w, conv_b=m_conv_b,
                lru_wa=m_lru_wa, lru_ba=m_lru_ba, lru_wx=m_lru_wx, lru_bx=m_lru_bx, lru_lambda=m_lru_lambda,
                w_att_o=m_w_att_o, w_rec_o=m_w_rec_o, w_out=m_w_out, ffn2_w_gu=m_ffn2_w_gu, ffn2_w_down=m_ffn2_w_down)
    mom2 = dict(w_ada=v_w_ada, b_ada=v_b_ada, norm_pre=v_norm_pre, norm_post=v_norm_post, ffn1_w_gu=v_ffn1_w_gu,
                ffn1_w_down=v_ffn1_w_down, w_in=v_w_in, rel_bias=v_rel_bias, conv_w=v_conv_w, conv_b=v_conv_b,
                lru_wa=v_lru_wa, lru_ba=v_lru_ba, lru_wx=v_lru_wx, lru_bx=v_lru_bx, lru_lambda=v_lru_lambda,
                w_att_o=v_w_att_o, w_rec_o=v_w_rec_o, w_out=v_w_out, ffn2_w_gu=v_ffn2_w_gu, ffn2_w_down=v_ffn2_w_down)
    order = list(weights)
    big = ["ffn1_w_gu", "ffn1_w_down", "w_in", "w_att_o", "w_rec_o", "w_out", "ffn2_w_gu", "ffn2_w_down"]
    small = ["b_ada", "norm_pre", "norm_post", "rel_bias", "conv_w", "conv_b", "lru_wa", "lru_ba", "lru_wx",
             "lru_bx", "lru_lambda"]

    xi, yi, ci = _place()
    me = _dev_index((xi, yi, ci))
    x0 = x[0]
    target = loss_target[0]
    fuse_tm = min(FUSE_TM, x0.shape[0])

    transposed = {"ffn1_w_gu", "w_in", "ffn2_w_gu"}
    local = lambda n, arr: jnp.transpose(arr[0]) if n in transposed else arr[0]
    shards = {n: local(n, weights[n]).astype(BF16) for n in big}
    full_of = lambda n, g: _cols_full(g) if n == "w_att_o" else _rows_full(g)

    pack = jnp.concatenate([c.reshape(-1), norm_pre.reshape(-1), norm_post.reshape(-1), conv_w.reshape(-1)])
    pack = jnp.pad(pack, (0, 3072 - pack.shape[0])).reshape(8, 384)
    got, w1_gu = _allgather_hbm([pack, shards["ffn1_w_gu"]], "gather_first")
    got = got.reshape(N_DEV, 3072)
    c_all = got[:, :1024]
    unshard = lambda blk, rows: jnp.transpose(blk.reshape(N_DEV, rows, 128), (1, 0, 2)).reshape(rows, D_MODEL)
    g_pre = unshard(got[:, 1024:1408], 3)
    g_post = unshard(got[:, 1408:1792], 3)
    conv_taps = unshard(got[:, 1792:2304], 4)
    conv_w8 = jnp.concatenate([conv_taps, jnp.zeros((4, LRU_WIDTH), F32)], axis=0)

    mod_cols = _ada_fwd(c_all, w_ada[0], "ada_fwd")
    mod_all = _allgather_vmem(mod_cols, "gather_mod").reshape(N_DEV, N_DEV, 1152)
    mod = lax.dynamic_index_in_dim(mod_all, me, axis=1, keepdims=False).reshape(1, -1) + b_ada
    mod = mod.reshape(3, 3, 1, D_MODEL)

    w_slab = _slab_weights(lru_wa[0], lru_wx[0])
    bias = _bias_tile(rel_bias[0], "bias_tile")

    res_w = (0.5, 1.0, 0.5)
    row = lambda v: v.reshape(1, -1)

    weight_groups = [["ffn1_w_down"], ["w_in"], ["w_att_o", "w_rec_o", "w_out"], ["ffn2_w_gu", "ffn2_w_down"]]
    weight_modes = ["all", "chip", "all", "all"]
    weights_started, started = _push_start([[shards[n] for n in g] for g in weight_groups], False,
                                           "gather_weights_start", after=(mod, w1_gu), modes=weight_modes)
    full = {"ffn1_w_gu": _rows_full(w1_gu)}

    def gathered_group(gi, after):
        sent, lands = _push_wait(weights_started[gi], False, after, f"gather_weights_wait{gi}", mode=weight_modes[gi])
        if weight_modes[gi] == "chip":
            lands = _forward_wait(_forward_start(lands, f"gather_weights_forward{gi}"), sent[0],
                                  f"gather_weights_forward_wait{gi}")
        for n, land in zip(weight_groups[gi], lands):
            full[n] = full_of(n, land)

    def ffn_fwd(xin, k, gi, tag, deps=(), target=None):
        h, a, g, u = _pre_up(xin, row(g_pre[k]), mod[k, 0], mod[k, 1], full[f"{tag}_w_gu"], f"{tag}_up", deps=deps)
        if f"{tag}_w_down" not in full:
            gathered_group(gi, a)
        f, *out = _matmul_post(a, full[f"{tag}_w_down"], xin, row(g_post[k]), mod[k, 2], res_w[k], f"{tag}_down",
                               target=target)
        return (out[0] if target is None else out), (h, g, u, a, f)

    x1, saved1 = ffn_fwd(x0, 0, 0, "ffn1", deps=(started,))

    gathered_group(1, x1)
    h2, proj = _pre_matmul(x1, row(g_pre[1]), mod[1, 0], mod[1, 1], full["w_in"], "mix_in",
                           b_shift=3 * ATT_WIDTH // 512)
    att_o = _attn_fwd(proj, bias, "attn_fwd")
    gathered_group(2, att_o)
    xc, pre, a_t, u_t = _lru_front(proj, conv_w8, conv_b, w_slab, lru_ba, lru_bx, lru_lambda, "lru_front")
    hs, h_prev, rec_in = _scan_fwd(a_t, u_t, proj, "lru_scan")
    att = _matmul(att_o, full["w_att_o"], "nn", F32, "att_out")
    rec = _matmul(rec_in, full["w_rec_o"], "nn", F32, "rec_out")
    merged, f2, x2 = _merge_matmul_post(att, rec, proj, full["w_out"], x1, row(g_post[1]), mod[1, 2], res_w[1],
                                        "mix_out")

    gathered_group(3, x2)
    (dy, sq), saved3 = ffn_fwd(x2, 2, 2, "ffn2", target=target)
    loss = lax.psum(0.5 * jnp.sum(sq) / D_MODEL, ("x", "y", "c"))

    grads = {}
    norm_sums = [None] * 6

    pending = []

    def exchange_start(names, tag, after=()):
        send = [(_cols_parts if n == "w_att_o" else _rows_parts)(grads[n]) for n in names]
        (group,), token = _push_start([send], True, f"exchange_{tag}_start", after=after)
        pending.append((names, send, group, tag))
        return token

    def exchange_finish(names, send, group, tag, after):
        sent, lands = _push_wait(group, True, after, f"exchange_{tag}_wait")
        res = None
        for n, land, mine in zip(names, lands, sent):
            res = _adamw_parts(land, mine, me, local(n, weights[n]), local(n, mom1[n]), local(n, mom2[n]),
                               f"adamw_{n}")
            back = (lambda r: jnp.transpose(r)) if n in transposed else (lambda r: r)
            out_g[n], out_d[n], out_m[n], out_v[n] = [back(r).reshape(weights[n].shape) for r in res]
        return res[0]

    out_g, out_d, out_m, out_v = {}, {}, {}, {}

    def ffn_bwd(xin, k, saved, dout, tag):
        h, g, u, a, f = saved
        w_gu, w_down = f"{tag}_w_gu", f"{tag}_w_down"
        df, dgu, norm_sums[2 * k + 1] = _post_bwd_up_bwd(f, dout, row(g_post[k]), mod[k, 2], res_w[k], full[w_down],
                                                          g, u, f"{tag}_up_bwd")
        grads[w_down] = _matmul(a, df, "tn", BF16, f"{tag}_dw_down", tm=1408, tn=1024, tk=DW_TK)
        if k == 0:
            started = exchange_start([w_down], w_down)
            grads[w_gu] = _dw_gu(dgu, h, f"{tag}_dw_gu", deps=(started,))
            started = exchange_start([w_gu], w_gu)
        else:
            grads[w_gu] = _dw_gu(dgu, h, f"{tag}_dw_gu")
            started = exchange_start([w_down, w_gu], tag)
        halves = [(dgu, (None, fuse_tm, D_FF), lambda i, half=half: (half, i, 0), (half * D_FF, (half + 1) * D_FF))
                  for half in range(2)]
        dx, norm_sums[2 * k] = _matmul_pre_bwd(halves, full[w_gu], xin, dout, row(g_pre[k]),
                                                                mod[k, 0], mod[k, 1], f"{tag}_dh", deps=(started,))
        return dx

    dx2 = ffn_bwd(x2, 2, saved3, dy, "ffn2")

    df2, datt, drec, dg_att, dg_rec, norm_sums[3] = _post_bwd_merge_bwd(
        f2, dx2, row(g_post[1]), mod[1, 2], res_w[1], full["w_out"], att, rec, proj, "mix_dmerged")
    grads["w_out"] = _matmul(merged, df2, "tn", BF16, "mix_dw_out", tm=1024, tn=1024, tk=DW_TK)
    datt_o = _matmul(datt, full["w_att_o"], "nt", BF16, "att_out_bwd")
    grads["w_att_o"] = _matmul(att_o, datt, "tn", BF16, "dw_att_o", tm=512, tn=1024, tk=DW_TK)
    grads["w_rec_o"] = _matmul(rec_in, drec, "tn", BF16, "dw_rec_o", tm=1024, tn=1024, tk=DW_TK)
    dhs, dyr = _matmul_recin_bwd(drec, full["w_rec_o"], hs, proj, "rec_out_bwd")
    g_t = _scan_bwd(a_t, dhs, "lru_scan_bwd")
    dpre, dxc, lru_sums = _lru_back(pre, xc, w_slab, lru_ba, lru_bx, lru_lambda, g_t, h_prev, "lru_back")
    dxr, conv_sums = _conv_bwd(proj, conv_w8, dxc, "conv_bwd")
    dq, dk, dv, dbias = _attn_bwd(proj, bias, datt_o, "attn_bwd")
    dproj = jnp.concatenate([dq, dk, dv, dxr, dyr, dg_att, dg_rec], axis=1)
    grads["w_in"] = _matmul(dproj, h2, "tn", BF16, "mix_dw_in", tm=1408, tn=1024, tk=DW_TK)
    pack_mix = jnp.concatenate([conv_sums, lru_sums, _pad_rows(_bias_grad(dbias, "bias_grad"), V7X_SUBLANES),
                                _lru_dw(xc, dpre, "lru_dw").reshape(128, D_MODEL)], axis=0)
    (mix_started,), started = _push_start([[pack_mix]], False, "small_grads_mix_start")
    started = exchange_start(["w_in", "w_out", "w_att_o", "w_rec_o"], "mix", after=(started,))
    whole = [(dproj, (fuse_tm, PROJ_WIDTH), lambda i: (i, 0), (0, PROJ_WIDTH))]
    dx1, norm_sums[2] = _matmul_pre_bwd(whole, full["w_in"], x1, dx2, row(g_pre[1]), mod[1, 0],
                                                             mod[1, 1], "mix_dh", deps=(started,))

    dx0 = ffn_bwd(x0, 0, saved1, dx1, "ffn1")

    pack_norm = jnp.concatenate(norm_sums, axis=0)
    (norm_started,), _ = _push_start([[pack_norm]], False, "small_grads_norm_start")

    def summed(started, pack, after, tag):
        _, (parts,) = _push_wait(started, False, after, f"small_grads_{tag}_wait")
        return parts, _sum_parts(parts, f"small_grads_{tag}_sum")

    done = dx0
    last = pending[-1:]
    for names, send, group, tag in pending[:-1]:
        done = exchange_finish(names, send, group, tag, done)

    _, total = summed(mix_started, pack_mix, done, "mix")
    grads["conv_w"] = _my_cols(total[0:4], me, 128)
    grads["conv_b"] = total[4:5]
    grads["lru_ba"] = total[8:9]
    grads["lru_bx"] = total[9:10]
    grads["lru_lambda"] = total[10:11]
    grads["rel_bias"] = total[16:19].reshape(-1)[: ATT_HEADS * (2 * MAX_REL + 1)].reshape(ATT_HEADS, -1)
    grads["lru_wa"] = total[24:88].reshape(LRU_BLOCKS, LRU_BLOCK, LRU_BLOCK)
    grads["lru_wx"] = total[88:152].reshape(LRU_BLOCKS, LRU_BLOCK, LRU_BLOCK)
    parts, total = summed(norm_started, pack_norm, total, "norm")
    by_sandwich = lambda v: v.reshape(*v.shape[:-2], 3, 2 * V7X_SUBLANES, D_MODEL)
    dmod_of = lambda v: jnp.concatenate([by_sandwich(v)[..., 1:3, :], by_sandwich(v)[..., 9:10, :]], axis=-2)
    grads["b_ada"] = dmod_of(total).reshape(1, -1)
    grads["norm_pre"] = _my_cols(by_sandwich(total)[:, 0, :], me, 128)
    grads["norm_post"] = _my_cols(by_sandwich(total)[:, V7X_SUBLANES, :], me, 128)
    dmod_all = dmod_of(parts).reshape(N_DEV, 9 * D_MODEL)
    grads["w_ada"] = _ada_bwd(c_all, _my_cols(dmod_all, me, 1152), "ada_bwd")

    res = _adamw(grads["w_ada"], w_ada[0], m_w_ada[0], v_w_ada[0], "adamw_w_ada")
    out_g["w_ada"], out_d["w_ada"], out_m["w_ada"], out_v["w_ada"] = [r.reshape(w_ada.shape) for r in res]

    sizes = [int(np.prod(weights[n].shape)) for n in small]
    tot = sum(sizes)
    rows_small = -(-tot // (16 * D_MODEL)) * 16
    flat = lambda arrs: jnp.pad(jnp.concatenate([a.reshape(-1) for a in arrs]),
                                (0, rows_small * D_MODEL - tot)).reshape(rows_small, D_MODEL)
    res = _adamw(flat([grads[n] for n in small]), flat([weights[n] for n in small]),
                 flat([mom1[n] for n in small]), flat([mom2[n] for n in small]), "adamw_small", rows=rows_small)
    offs = np.cumsum([0] + sizes)
    for dst, r in zip((out_g, out_d, out_m, out_v), res):
        rf = r.reshape(-1)
        for i, n in enumerate(small):
            dst[n] = rf[offs[i] : offs[i + 1]].reshape(weights[n].shape)

    done = res[0]
    for names, send, group, tag in last:
        done = exchange_finish(names, send, group, tag, done)

    return (loss, dx0[None], *[out_g[n] for n in order], *[out_d[n] for n in order],
            *[out_m[n] for n in order], *[out_v[n] for n in order])
```

```python
import jax
import jax.numpy as jnp
import numpy as np
from jax import lax
from jax.experimental import pallas as pl
from jax.experimental.pallas import tpu as pltpu

D_MODEL = 1024
D_FF = 2816
ATT_HEADS = 8
ATT_HEAD_DIM = 64
ATT_WIDTH = 512
CHUNK = 64
LEFT_CHUNKS = 8
MAX_REL = 128
LRU_WIDTH = 1024
LRU_BLOCKS = 16
LRU_BLOCK = 64
LRU_C = 8.0
EPS = 1e-6
PROJ_WIDTH = 5632
N_DEV = 8

ADAM_LR = 0.001
ADAM_B1 = 0.9
ADAM_B2 = 0.999
ADAM_EPS = 1e-08
ADAM_WD = 0.01
ADAM_STEP = 10

V7X_LANES = 128
V7X_SUBLANES = 8
V7X_VMEM_BYTES = 64 * 1024 * 1024
VMEM_LIMIT = V7X_VMEM_BYTES - 8 * 1024 * 1024

ATT_TQ = 256
NEG = -1e30
BF16 = jnp.bfloat16
F32 = jnp.float32
MESH = pl.DeviceIdType.MESH

OFF_Q = 4 * LRU_WIDTH
OFF_K = OFF_Q + ATT_WIDTH
OFF_V = OFF_K + ATT_WIDTH


def _cparams(**kw):
    return pltpu.CompilerParams(vmem_limit_bytes=VMEM_LIMIT, **kw)


def _pick(n, target, unit=V7X_LANES):
    best = None
    for t in range(unit, min(n, target) + 1, unit):
        if n % t == 0:
            best = t
    return n if best is None else best


_DIMS = {
    "nn": (((1,), (0,)), ((), ())),
    "nt": (((1,), (1,)), ((), ())),
    "tn": (((0,), (0,)), ((), ())),
}


ANY_SPEC = pl.BlockSpec(memory_space=pl.ANY)


def _matmul(a, b, mode, out_dtype, name, tm=1024, tn=512, tk=1408, deps=(), b_shift=0):
    n_deps = len(deps)
    if mode == "nn":
        (m, k), (k2, n) = a.shape, b.shape
    elif mode == "nt":
        (m, k), (n, k2) = a.shape, b.shape
    else:
        (k, m), (k2, n) = a.shape, b.shape
    assert k == k2, (a.shape, b.shape, mode)
    tm, tn, tk = _pick(m, tm), _pick(n, tn), _pick(k, tk)
    nk = k // tk
    dims = _DIMS[mode]

    def body(a_ref, b_ref, *rest):
        o_ref, scratch = rest[n_deps], rest[n_deps + 1 :]
        p = lax.dot_general(a_ref[...], b_ref[...], dims, preferred_element_type=F32)
        if nk == 1:
            o_ref[...] = p.astype(o_ref.dtype)
        else:
            acc = scratch[0]
            kk = pl.program_id(2)

            @pl.when(kk == 0)
            def _():
                acc[...] = p

            @pl.when(kk > 0)
            def _():
                acc[...] += p

            @pl.when(kk == nk - 1)
            def _():
                o_ref[...] = acc[...].astype(o_ref.dtype)

    if mode == "nn":
        a_spec = pl.BlockSpec((tm, tk), lambda i, j, kk: (i, kk))
        b_spec = pl.BlockSpec((tk, tn), lambda i, j, kk: (kk, j))
    elif mode == "nt":
        a_spec = pl.BlockSpec((tm, tk), lambda i, j, kk: (i, kk))
        b_spec = pl.BlockSpec((tn, tk), lambda i, j, kk: ((j + b_shift) % (n // tn), kk))
    else:
        a_spec = pl.BlockSpec((tk, tm), lambda i, j, kk: (kk, i))
        b_spec = pl.BlockSpec((tk, tn), lambda i, j, kk: (kk, j))
    return pl.pallas_call(
        body,
        name=name,
        grid=(m // tm, n // tn, nk),
        in_specs=[a_spec, b_spec] + [ANY_SPEC] * n_deps,
        out_specs=pl.BlockSpec((tm, tn), lambda i, j, kk: (i, j)),
        out_shape=jax.ShapeDtypeStruct((m, n), out_dtype),
        scratch_shapes=[pltpu.VMEM((tm, tn), F32)] if nk > 1 else [],
        compiler_params=_cparams(dimension_semantics=("parallel", "parallel", "arbitrary")),
    )(a, b, *deps)


def _rowwise(fn, name, params, tiles, outs, accs=(), ts=256, with_index=False, deps=()):
    norm = []
    for t in tiles:
        if not isinstance(t, tuple):
            t = (t, t.shape[1], 0)
        norm.append(t if len(t) == 4 else (*t, None))
    s = norm[0][0].shape[0]
    ts = min(ts, s)
    assert s % ts == 0 and ts % V7X_SUBLANES == 0
    steps = s // ts
    halo_blocks = ts // V7X_SUBLANES
    n_p, n_t, n_o = len(params), len(norm), len(outs)

    def body(*refs):
        i = pl.program_id(0)
        vals = [r[...] for r in refs[: n_p + n_t]]
        res = fn(i, steps, *vals) if with_index else fn(*vals)
        if not isinstance(res, (tuple, list)):
            res = (res,)
        first_out = n_p + n_t + len(deps)
        o_refs = refs[first_out : first_out + n_o]
        a_refs = refs[first_out + n_o :]
        for r, v in zip(o_refs, res[:n_o]):
            r[...] = v.astype(r.dtype)
        for r, v in zip(a_refs, res[n_o:]):
            _accumulate(r, v, i)

    in_specs = [pl.BlockSpec(p.shape, lambda i: (0, 0)) for p in params]
    for arr, w, cb, halo in norm:
        if halo is None:
            in_specs.append(pl.BlockSpec((ts, w), lambda i, cb=cb: (i, cb)))
        elif halo == "prev":
            in_specs.append(
                pl.BlockSpec((V7X_SUBLANES, w), lambda i, cb=cb: (jnp.maximum(i * halo_blocks - 1, 0), cb))
            )
        else:
            last = s // V7X_SUBLANES - 1
            in_specs.append(
                pl.BlockSpec((V7X_SUBLANES, w), lambda i, cb=cb: (jnp.minimum((i + 1) * halo_blocks, last), cb))
            )
    in_specs += [ANY_SPEC] * len(deps)
    out_specs = [pl.BlockSpec((ts, w), lambda i: (i, 0)) for w, _ in outs]
    out_specs += [pl.BlockSpec(shape, lambda i: (0, 0)) for shape in accs]
    out_shape = [jax.ShapeDtypeStruct((s, w), dt) for w, dt in outs]
    out_shape += [jax.ShapeDtypeStruct(shape, F32) for shape in accs]
    res = pl.pallas_call(
        body,
        name=name,
        grid=(steps,),
        in_specs=in_specs,
        out_specs=out_specs,
        out_shape=out_shape,
        compiler_params=_cparams(dimension_semantics=("arbitrary",)),
    )(*params, *[t[0] for t in norm], *deps)
    return res


def _accumulate(ref, val, step):
    @pl.when(step == 0)
    def _():
        ref[...] = val

    @pl.when(step > 0)
    def _():
        ref[...] += val


def _sigmoid(z):
    return jax.nn.sigmoid(z)


def _silu(z):
    return z * _sigmoid(z)


def _gelu(z):
    return 0.5 * z * (1.0 + jnp.tanh(0.7978845608028654 * (z + 0.044715 * (z * z * z))))


def _pre_fn(g, shift, scale, x):
    r = lax.rsqrt(jnp.mean(x * x, axis=-1, keepdims=True) + EPS)
    return ((x * r) * g) * (1.0 + scale) + shift


def _post_fn(res_w, g, gate, f, x):
    r = lax.rsqrt(jnp.mean(f * f, axis=-1, keepdims=True) + EPS)
    return x + (res_w * gate) * ((f * r) * g)


def _gates_fn(ba, bx, lam, pre, xc):
    ra = _sigmoid(pre[:, :LRU_WIDTH] + ba)
    ia = _sigmoid(pre[:, LRU_WIDTH:] + bx)
    softplus = jnp.maximum(-lam, 0.0) + jnp.log1p(jnp.exp(-jnp.abs(lam)))
    log_a = (-LRU_C) * ra * softplus
    a = jnp.exp(log_a)
    mult = jnp.sqrt(-jnp.tanh(log_a) * (a * a + 1.0))
    return a, mult * (ia * xc)


def _recin_fn(hs, yr):
    return hs * _gelu(yr)


def _merge_fn(att, rec, g_att, g_rec):
    return _sigmoid(g_att) * att + _sigmoid(g_rec) * rec


def _rowsum(v):
    return jnp.sum(v, axis=0, keepdims=True)


FFN_TM = 512
FFN_TF = 1408


def _glu_fn(g, u):
    return _silu(g) * u


FUSE_TM = 256
DW_TK = 4096
ROW_SPEC2 = pl.BlockSpec((1, D_MODEL), lambda i, j: (0, 0))
ROW_SPEC1 = pl.BlockSpec((1, D_MODEL), lambda i: (0, 0))
SUMS_SPEC1 = pl.BlockSpec((V7X_SUBLANES, D_MODEL), lambda i: (0, 0))
SUMS_SPEC2 = pl.BlockSpec((V7X_SUBLANES, D_MODEL), lambda i, j: (0, 0))
SUMS_SHAPE = jax.ShapeDtypeStruct((V7X_SUBLANES, D_MODEL), F32)


def _sum_rows(*rows):
    pad = jnp.zeros((V7X_SUBLANES - len(rows), rows[0].shape[1]), F32)
    return jnp.concatenate([*rows, pad], axis=0)


def _pre_up(x, g, shift, scale, w_gu_t, name, deps=()):
    s = x.shape[0]
    tm = min(FFN_TM, s)
    nf = D_FF // FFN_TF
    nd = len(deps)

    def body(x_ref, g_ref, sh_ref, sc_ref, wg_ref, wu_ref, *rest):
        h_ref, a_ref, gg_ref, u_ref, h_s = rest[nd:]

        @pl.when(pl.program_id(1) == 0)
        def _():
            h = _pre_fn(g_ref[...], sh_ref[...], sc_ref[...], x_ref[...]).astype(BF16)
            h_s[...] = h
            h_ref[...] = h

        hv = h_s[...]
        gv = lax.dot_general(hv, wg_ref[...], _DIMS["nt"], preferred_element_type=F32)
        uv = lax.dot_general(hv, wu_ref[...], _DIMS["nt"], preferred_element_type=F32)
        a_ref[...] = _glu_fn(gv, uv).astype(a_ref.dtype)
        gg_ref[...] = gv.astype(gg_ref.dtype)
        u_ref[...] = uv.astype(u_ref.dtype)

    rows = pl.BlockSpec((tm, D_MODEL), lambda i, j: (i, 0))
    out = pl.BlockSpec((tm, FFN_TF), lambda i, j: (i, j))
    return pl.pallas_call(
        body,
        name=name,
        grid=(s // tm, nf),
        in_specs=[rows, ROW_SPEC2, ROW_SPEC2, ROW_SPEC2,
                  pl.BlockSpec((FFN_TF, D_MODEL), lambda i, j: (j, 0)),
                  pl.BlockSpec((FFN_TF, D_MODEL), lambda i, j: (nf + j, 0))] + [ANY_SPEC] * nd,
        out_specs=[rows, out, out, out],
        out_shape=[jax.ShapeDtypeStruct((s, D_MODEL), BF16)] + [jax.ShapeDtypeStruct((s, D_FF), BF16)] * 3,
        scratch_shapes=[pltpu.VMEM((tm, D_MODEL), BF16)],
        compiler_params=_cparams(dimension_semantics=("parallel", "arbitrary")),
    )(x, g, shift, scale, w_gu_t, w_gu_t, *deps)


def _pre_matmul(x, g, shift, scale, w_t, name, b_shift=0, tn=512):
    s = x.shape[0]
    n = w_t.shape[0]
    tm = min(2 * FFN_TM, s)

    def body(x_ref, g_ref, sh_ref, sc_ref, w_ref, h_ref, o_ref, h_s):
        @pl.when(pl.program_id(1) == 0)
        def _():
            h = _pre_fn(g_ref[...], sh_ref[...], sc_ref[...], x_ref[...]).astype(BF16)
            h_s[...] = h
            h_ref[...] = h

        o_ref[...] = lax.dot_general(h_s[...], w_ref[...], _DIMS["nt"], preferred_element_type=F32)

    rows = pl.BlockSpec((tm, D_MODEL), lambda i, j: (i, 0))
    return pl.pallas_call(
        body,
        name=name,
        grid=(s // tm, n // tn),
        in_specs=[rows, ROW_SPEC2, ROW_SPEC2, ROW_SPEC2,
                  pl.BlockSpec((tn, D_MODEL), lambda i, j: ((j + b_shift) % (n // tn), 0))],
        out_specs=[rows, pl.BlockSpec((tm, tn), lambda i, j: (i, j))],
        out_shape=[jax.ShapeDtypeStruct((s, D_MODEL), BF16), jax.ShapeDtypeStruct((s, n), F32)],
        scratch_shapes=[pltpu.VMEM((tm, D_MODEL), BF16)],
        compiler_params=_cparams(dimension_semantics=("parallel", "arbitrary")),
    )(x, g, shift, scale, w_t)


def _matmul_post(a, w, x, g_post, gate, res_w, name, target=None):
    s, k = a.shape
    tm = min(FFN_TM, s)
    extra = [] if target is None else [target]

    def body(a_ref, w_ref, x_ref, g_ref, gate_ref, *rest):
        f = jnp.dot(a_ref[...], w_ref[...], preferred_element_type=F32)
        y = _post_fn(res_w, g_ref[...], gate_ref[...], f, x_ref[...])
        if target is None:
            f_ref, y_ref = rest
            y_ref[...] = y
        else:
            t_ref, f_ref, dy_ref, sq_ref = rest
            diff = y - t_ref[...]
            dy_ref[...] = diff * (1.0 / D_MODEL)
            _accumulate(sq_ref, _rowsum(diff * diff), pl.program_id(0))
        f_ref[...] = f

    rows = pl.BlockSpec((tm, D_MODEL), lambda i: (i, 0))
    out_specs, out_shape = [rows, rows], [jax.ShapeDtypeStruct((s, D_MODEL), F32)] * 2
    if target is not None:
        out_specs.append(ROW_SPEC1)
        out_shape.append(jax.ShapeDtypeStruct((1, D_MODEL), F32))
    return pl.pallas_call(
        body,
        name=name,
        grid=(s // tm,),
        in_specs=[pl.BlockSpec((tm, k), lambda i: (i, 0)), pl.BlockSpec((k, D_MODEL), lambda i: (0, 0)), rows,
                  ROW_SPEC1, ROW_SPEC1] + [rows] * len(extra),
        out_specs=out_specs,
        out_shape=out_shape,
        compiler_params=_cparams(dimension_semantics=("arbitrary",)),
    )(a, w, x, g_post, gate, *extra)


def _merge_matmul_post(att, rec, proj, w, x, g_post, gate, res_w, name):
    s = att.shape[0]
    tm = min(FUSE_TM, s)

    def body(att_ref, rec_ref, ga_ref, gr_ref, w_ref, x_ref, g_ref, gate_ref, m_ref, f_ref, y_ref):
        merged = _merge_fn(att_ref[...], rec_ref[...], ga_ref[...], gr_ref[...]).astype(BF16)
        m_ref[...] = merged
        f = jnp.dot(merged, w_ref[...], preferred_element_type=F32)
        f_ref[...] = f
        y_ref[...] = _post_fn(res_w, g_ref[...], gate_ref[...], f, x_ref[...])

    rows = pl.BlockSpec((tm, D_MODEL), lambda i: (i, 0))
    return pl.pallas_call(
        body,
        name=name,
        grid=(s // tm,),
        in_specs=[rows, rows, pl.BlockSpec((tm, D_MODEL), lambda i: (i, 2)), pl.BlockSpec((tm, D_MODEL), lambda i: (i, 3)),
                  pl.BlockSpec(w.shape, lambda i: (0, 0)), rows, ROW_SPEC1, ROW_SPEC1],
        out_specs=[rows, rows, rows],
        out_shape=[jax.ShapeDtypeStruct((s, D_MODEL), BF16)] + [jax.ShapeDtypeStruct((s, D_MODEL), F32)] * 2,
        compiler_params=_cparams(dimension_semantics=("parallel",)),
    )(att, rec, proj, proj, w, x, g_post, gate)


def _post_bwd_merge_bwd(f, dy, g_post, gate, res_w, w, att, rec, proj, name):
    s = f.shape[0]
    tm = min(FUSE_TM, s)

    def body(f_ref, dy_ref, gp_ref, gate_ref, w_ref, att_ref, rec_ref, ga_ref, gr_ref,
             df_ref, datt_ref, drec_ref, dga_ref, dgr_ref, sums_ref):
        i = pl.program_id(0)
        dgp, dgate, df = _post_vjp(res_w, gp_ref[...], gate_ref[...], f_ref[...], dy_ref[...])
        dfb = df.astype(BF16)
        df_ref[...] = dfb
        _accumulate(sums_ref, _sum_rows(dgp, dgate), i)
        dmerged = lax.dot_general(dfb, w_ref[...], _DIMS["nt"], preferred_element_type=F32)
        _, vjp = jax.vjp(_merge_fn, att_ref[...], rec_ref[...], ga_ref[...], gr_ref[...])
        for ref, val in zip((datt_ref, drec_ref, dga_ref, dgr_ref), vjp(dmerged)):
            ref[...] = val.astype(ref.dtype)

    rows = pl.BlockSpec((tm, D_MODEL), lambda i: (i, 0))
    return pl.pallas_call(
        body,
        name=name,
        grid=(s // tm,),
        in_specs=[rows, rows, ROW_SPEC1, ROW_SPEC1, pl.BlockSpec(w.shape, lambda i: (0, 0)), rows, rows,
                  pl.BlockSpec((tm, D_MODEL), lambda i: (i, 2)), pl.BlockSpec((tm, D_MODEL), lambda i: (i, 3))],
        out_specs=[rows] * 5 + [SUMS_SPEC1],
        out_shape=[jax.ShapeDtypeStruct((s, D_MODEL), BF16)] * 5 + [SUMS_SHAPE],
        compiler_params=_cparams(dimension_semantics=("arbitrary",)),
    )(f, dy, g_post, gate, w, att, rec, proj, proj)


def _matmul_recin_bwd(drec, w, hs, proj, name, deps=()):
    s = drec.shape[0]
    tm = min(FUSE_TM, s)
    nd = len(deps)

    def body(d_ref, w_ref, hs_ref, yr_ref, *rest):
        dhs_ref, dyr_ref = rest[nd:]
        d = lax.dot_general(d_ref[...], w_ref[...], _DIMS["nt"], preferred_element_type=F32)
        _, vjp = jax.vjp(_recin_fn, hs_ref[...], yr_ref[...])
        dhs, dyr = vjp(d)
        dhs_ref[...] = dhs
        dyr_ref[...] = dyr.astype(dyr_ref.dtype)

    rows = pl.BlockSpec((tm, D_MODEL), lambda i: (i, 0))
    return pl.pallas_call(
        body,
        name=name,
        grid=(s // tm,),
        in_specs=[rows, pl.BlockSpec(w.shape, lambda i: (0, 0)), rows,
                  pl.BlockSpec((tm, D_MODEL), lambda i: (i, 1))] + [ANY_SPEC] * nd,
        out_specs=[rows, rows],
        out_shape=[jax.ShapeDtypeStruct((s, D_MODEL), F32), jax.ShapeDtypeStruct((s, D_MODEL), BF16)],
        compiler_params=_cparams(dimension_semantics=("parallel",)),
    )(drec, w, hs, proj, *deps)


def _post_vjp(res_w, g, gate, f, dy):
    _, vjp = jax.vjp(lambda g, gate, f: _post_fn(res_w, g, gate, f, 0.0), g, gate, f)
    return vjp(dy)


def _post_bwd_up_bwd(f, dy, g_post, gate, res_w, w_down, g, u, name, deps=()):
    s = f.shape[0]
    tm = min(FFN_TM, s)
    nd = len(deps)

    def body(f_ref, dy_ref, gp_ref, gate_ref, wd_ref, g_ref, u_ref, *rest):
        df_ref, dgu_ref, sums_ref, df_s = rest[nd:]
        i = pl.program_id(0)

        @pl.when(pl.program_id(1) == 0)
        def _():
            dgp, dgate, df = _post_vjp(res_w, gp_ref[...], gate_ref[...], f_ref[...], dy_ref[...])
            df_s[...] = df.astype(BF16)
            df_ref[...] = df_s[...]
            _accumulate(sums_ref, _sum_rows(dgp, dgate), i)

        da = lax.dot_general(df_s[...], wd_ref[...], _DIMS["nt"], preferred_element_type=F32)
        _, vjp = jax.vjp(_glu_fn, g_ref[...].astype(F32), u_ref[...].astype(F32))
        dg, du = vjp(da)
        dgu_ref[0] = dg.astype(dgu_ref.dtype)
        dgu_ref[1] = du.astype(dgu_ref.dtype)

    rows = pl.BlockSpec((tm, D_MODEL), lambda i, j: (i, 0))
    blk = pl.BlockSpec((tm, FFN_TF), lambda i, j: (i, j))
    return pl.pallas_call(
        body,
        name=name,
        grid=(s // tm, D_FF // FFN_TF),
        in_specs=[rows, rows, ROW_SPEC2, ROW_SPEC2, pl.BlockSpec((FFN_TF, D_MODEL), lambda i, j: (j, 0)), blk,
                  blk] + [ANY_SPEC] * nd,
        out_specs=[rows, pl.BlockSpec((2, tm, FFN_TF), lambda i, j: (0, i, j)), SUMS_SPEC2],
        out_shape=[jax.ShapeDtypeStruct((s, D_MODEL), BF16), jax.ShapeDtypeStruct((2, s, D_FF), BF16), SUMS_SHAPE],
        scratch_shapes=[pltpu.VMEM((tm, D_MODEL), BF16)],
        compiler_params=_cparams(dimension_semantics=("arbitrary", "arbitrary")),
    )(f, dy, g_post, gate, w_down, g, u, *deps)


def _matmul_pre_bwd(parts, w_t, x, dres, g, shift, scale, name, deps=()):
    s = x.shape[0]
    na, nd = len(parts), len(deps)
    ranges = [p[3] for p in parts]

    def body(*refs):
        a_refs = refs[:na]
        w_ref, x_ref, dres_ref, g_ref, sh_ref, sc_ref = refs[na : na + 6]
        dx_ref, sums_ref = refs[na + 6 + nd :]
        i = pl.program_id(0)
        dh = None
        for a_ref, (r0, r1) in zip(a_refs, ranges):
            p = jnp.dot(a_ref[...], w_ref[r0:r1, :], preferred_element_type=F32)
            dh = p if dh is None else dh + p
        _, vjp = jax.vjp(_pre_fn, g_ref[...], sh_ref[...], sc_ref[...], x_ref[...])
        dg, dsh, dsc, dx = vjp(dh)
        dx_ref[...] = dx + dres_ref[...]
        _accumulate(sums_ref, _sum_rows(dg, dsh, dsc), i)

    tm = parts[0][1][-2]
    rows = pl.BlockSpec((tm, D_MODEL), lambda i: (i, 0))
    return pl.pallas_call(
        body,
        name=name,
        grid=(s // tm,),
        in_specs=[pl.BlockSpec(p[1], p[2]) for p in parts]
        + [pl.BlockSpec(w_t.shape, lambda i: (0, 0)), rows, rows, ROW_SPEC1, ROW_SPEC1, ROW_SPEC1]
        + [ANY_SPEC] * nd,
        out_specs=[rows, SUMS_SPEC1],
        out_shape=[jax.ShapeDtypeStruct((s, D_MODEL), F32), SUMS_SHAPE],
        compiler_params=_cparams(dimension_semantics=("arbitrary",)),
    )(*[p[0] for p in parts], w_t, x, dres, g, shift, scale, *deps)


def _dw_gu(dgu, h, name, deps=(), tk=DW_TK):
    s = h.shape[0]
    tk = min(tk, s)
    nk = s // tk
    half = D_FF // FFN_TF

    def body(a_ref, b_ref, *rest):
        o_ref = rest[len(deps)]
        kk = pl.program_id(1)
        p = lax.dot_general(a_ref[...], b_ref[...], _DIMS["tn"], preferred_element_type=F32)
        if nk == 1:
            o_ref[...] = p.astype(o_ref.dtype)
            return
        acc = rest[len(deps) + 1]

        @pl.when(kk == 0)
        def _():
            acc[...] = p

        @pl.when(kk > 0)
        def _():
            acc[...] += p

        @pl.when(kk == nk - 1)
        def _():
            o_ref[...] = acc[...].astype(o_ref.dtype)

    return pl.pallas_call(
        body,
        name=name,
        grid=(2 * half, nk),
        in_specs=[pl.BlockSpec((None, tk, FFN_TF), lambda i, kk: (i // half, kk, i % half)),
                  pl.BlockSpec((tk, D_MODEL), lambda i, kk: (kk, 0))] + [ANY_SPEC] * len(deps),
        out_specs=pl.BlockSpec((FFN_TF, D_MODEL), lambda i, kk: (i, 0)),
        out_shape=jax.ShapeDtypeStruct((2 * D_FF, D_MODEL), BF16),
        scratch_shapes=[pltpu.VMEM((FFN_TF, D_MODEL), F32)] if nk > 1 else [],
        compiler_params=_cparams(dimension_semantics=("parallel", "arbitrary")),
    )(dgu, h, *deps)


def _shift_down(ext, j, rows):
    return pltpu.roll(ext, j, 0)[V7X_SUBLANES : V7X_SUBLANES + rows]


def _shift_up(ext, j, rows):
    return pltpu.roll(ext, ext.shape[0] - j, 0)[:rows] if j else ext[:rows]


LRU_SLAB = 256
N_SLABS = LRU_WIDTH // LRU_SLAB


def _slab_weights(wa, wx):
    per = LRU_SLAB // LRU_BLOCK
    eye = jnp.eye(per, dtype=wa.dtype)

    def diag(w):
        w4 = w.reshape(N_SLABS, per, LRU_BLOCK, LRU_BLOCK)
        return jnp.einsum("sbkj,bc->sbkcj", w4, eye).reshape(N_SLABS, LRU_SLAB, LRU_SLAB)

    return jnp.concatenate([diag(wa), diag(wx)], axis=2).reshape(LRU_WIDTH, 2 * LRU_SLAB).astype(BF16)


def _slab_cols(v, s):
    lo = s * LRU_SLAB
    return jnp.concatenate([v[:, lo : lo + LRU_SLAB], v[:, LRU_WIDTH + lo : LRU_WIDTH + lo + LRU_SLAB]], axis=1)


def _lru_front(proj, w8, b, w_slab, ba, bx, lam, name):
    def fn(i, steps, w8, b, w_slab, ba, bx, lam, x, halo):
        halo = jnp.where(i > 0, halo, 0.0)
        ext = jnp.concatenate([halo, x], axis=0)
        xc = b + w8[3:4] * x
        for j in (1, 2, 3):
            xc = xc + w8[3 - j : 4 - j] * _shift_down(ext, j, x.shape[0])
        xcb = xc.astype(BF16)
        prods = []
        for s in range(N_SLABS):
            rows = slice(s * LRU_SLAB, (s + 1) * LRU_SLAB)
            prods.append(jnp.dot(xcb[:, rows], w_slab[rows], preferred_element_type=F32))
        pre = jnp.concatenate([p[:, :LRU_SLAB] for p in prods] + [p[:, LRU_SLAB:] for p in prods], axis=1)
        a, u = _gates_fn(ba, bx, lam, pre, xc)
        return xc, pre, a, u

    tiles = [(proj, LRU_WIDTH, 0), (proj, LRU_WIDTH, 0, "prev")]
    outs = [(LRU_WIDTH, F32), (2 * LRU_WIDTH, F32), (LRU_WIDTH, F32), (LRU_WIDTH, F32)]
    return _rowwise(fn, name, [w8, b, w_slab, ba, bx, lam], tiles, outs, with_index=True)


def _lru_back(pre, xc, w_slab, ba, bx, lam, g, h_prev, name, deps=()):
    def fn(w_slab, ba, bx, lam, pre, xc, g, h_prev):
        _, vjp = jax.vjp(_gates_fn, ba, bx, lam, pre, xc)
        dba, dbx, dlam, dpre, dxc = vjp((g * h_prev, g))
        dpre = dpre.astype(BF16)
        back = []
        for s in range(N_SLABS):
            rows = slice(s * LRU_SLAB, (s + 1) * LRU_SLAB)
            back.append(lax.dot_general(_slab_cols(dpre, s), w_slab[rows], _DIMS["nt"], preferred_element_type=F32))
        return dpre, dxc + jnp.concatenate(back, axis=1), _sum_rows(dba, dbx, dlam)

    return _rowwise(fn, name, [w_slab, ba, bx, lam], [pre, xc, g, h_prev],
                    [(2 * LRU_WIDTH, BF16), (LRU_WIDTH, F32)], [(V7X_SUBLANES, LRU_WIDTH)], deps=deps)


def _lru_dw(xc, dpre, name):
    s = xc.shape[0]
    ts = min(512, s)
    steps = s // ts
    per = LRU_SLAB // LRU_BLOCK

    def body(x_ref, d_ref, o_ref, acc):
        i = pl.program_id(0)
        xcb = x_ref[...].astype(BF16)
        d = d_ref[...]
        for sl in range(N_SLABS):
            rows = slice(sl * LRU_SLAB, (sl + 1) * LRU_SLAB)
            p = lax.dot_general(xcb[:, rows], _slab_cols(d, sl), _DIMS["tn"], preferred_element_type=F32)

            @pl.when(i == 0)
            def _(p=p, rows=rows):
                acc[rows, :] = p

            @pl.when(i > 0)
            def _(p=p, rows=rows):
                acc[rows, :] += p

        @pl.when(i == steps - 1)
        def _():
            for half in range(2):
                for n in range(LRU_BLOCKS):
                    r0 = n * LRU_BLOCK
                    c0 = half * LRU_SLAB + (n % per) * LRU_BLOCK
                    o_ref[half, r0 : r0 + LRU_BLOCK, :] = acc[r0 : r0 + LRU_BLOCK, c0 : c0 + LRU_BLOCK]

    return pl.pallas_call(
        body,
        name=name,
        grid=(steps,),
        in_specs=[pl.BlockSpec((ts, LRU_WIDTH), lambda i: (i, 0)), pl.BlockSpec((ts, 2 * LRU_WIDTH), lambda i: (i, 0))],
        out_specs=pl.BlockSpec((2, LRU_WIDTH, LRU_BLOCK), lambda i: (0, 0, 0)),
        out_shape=jax.ShapeDtypeStruct((2, LRU_WIDTH, LRU_BLOCK), F32),
        scratch_shapes=[pltpu.VMEM((LRU_WIDTH, 2 * LRU_SLAB), F32)],
        compiler_params=_cparams(dimension_semantics=("arbitrary",)),
    )(xc, dpre)


def _conv_bwd(proj, w8, d1, name):
    def fn(i, steps, w8, x, halo, d, d1n):
        rows = x.shape[0]
        dn = jnp.where(i < steps - 1, d1n, 0.0)
        halo = jnp.where(i > 0, halo, 0.0)
        dext = jnp.concatenate([d, dn], axis=0)
        xext = jnp.concatenate([halo, x], axis=0)
        dx = w8[3:4] * d
        dw = [None] * 4
        dw[3] = _rowsum(d * x)
        for k in (1, 2, 3):
            dx = dx + w8[3 - k : 4 - k] * _shift_up(dext, k, rows)
            dw[3 - k] = _rowsum(d * _shift_down(xext, k, rows))
        return dx, _sum_rows(*dw, _rowsum(d))

    tiles = [(proj, LRU_WIDTH, 0), (proj, LRU_WIDTH, 0, "prev"), d1, (d1, LRU_WIDTH, 0, "next")]
    return _rowwise(fn, name, [w8], tiles, [(LRU_WIDTH, BF16)], [(V7X_SUBLANES, LRU_WIDTH)], with_index=True)


SCAN_ROWS = 512


def _block_scan(a, b, row, reverse):
    for d in (1, 2, 4):
        if reverse:
            shift, keep = V7X_SUBLANES - d, row < V7X_SUBLANES - d
        else:
            shift, keep = d, row >= d
        a_s = pltpu.roll(a, shift, 0)
        b_s = pltpu.roll(b, shift, 0)
        b = jnp.where(keep, a * b_s + b, b)
        a = jnp.where(keep, a * a_s, a)
    return a, b


def _scan_fwd(a, u, proj, name):
    s, w = a.shape
    ts = min(SCAN_ROWS, s)
    sub = ts // V7X_SUBLANES

    def body(a_ref, u_ref, yr_ref, h_ref, hp_ref, rec_ref, carry):
        @pl.when(pl.program_id(0) == 0)
        def _():
            carry[...] = jnp.zeros_like(carry)

        row = lax.broadcasted_iota(jnp.int32, (V7X_SUBLANES, w), 0)

        def step(j, c):
            rows = pl.ds(pl.multiple_of(j * V7X_SUBLANES, V7X_SUBLANES), V7X_SUBLANES)
            pa, pb = _block_scan(a_ref[rows, :], u_ref[rows, :], row, False)
            h = pb + pa * c
            h_ref[rows, :] = h
            hp_ref[rows, :] = jnp.where(row >= 1, pltpu.roll(h, 1, 0), c)
            return jnp.broadcast_to(h[V7X_SUBLANES - 1 :], (V7X_SUBLANES, w))

        carry[...] = lax.fori_loop(0, sub, step, carry[...])
        rec_ref[...] = _recin_fn(h_ref[...], yr_ref[...]).astype(rec_ref.dtype)

    spec = pl.BlockSpec((ts, w), lambda i: (i, 0))
    return pl.pallas_call(
        body,
        name=name,
        grid=(s // ts,),
        in_specs=[spec, spec, pl.BlockSpec((ts, w), lambda i: (i, 1))],
        out_specs=[spec, spec, spec],
        out_shape=[jax.ShapeDtypeStruct((s, w), F32)] * 2 + [jax.ShapeDtypeStruct((s, w), BF16)],
        scratch_shapes=[pltpu.VMEM((V7X_SUBLANES, w), F32)],
        compiler_params=_cparams(dimension_semantics=("arbitrary",)),
    )(a, u, proj)


def _scan_bwd(a, dh, name):
    s, w = a.shape
    ts = min(SCAN_ROWS, s)
    sub = ts // V7X_SUBLANES
    steps = s // ts

    def body(a_ref, d_ref, g_ref, carry):
        @pl.when(pl.program_id(0) == 0)
        def _():
            carry[...] = jnp.zeros_like(carry)

        row = lax.broadcasted_iota(jnp.int32, (V7X_SUBLANES, w), 0)

        def step(jj, c):
            j = sub - 1 - jj
            rows = pl.ds(pl.multiple_of(j * V7X_SUBLANES, V7X_SUBLANES), V7X_SUBLANES)
            av, dv = a_ref[rows, :], d_ref[rows, :]
            pa, pb = _block_scan(av, av * dv, row, True)
            big = pb + pa * c
            g_ref[rows, :] = dv + jnp.where(row < V7X_SUBLANES - 1, pltpu.roll(big, V7X_SUBLANES - 1, 0), c)
            return jnp.broadcast_to(big[:1], (V7X_SUBLANES, w))

        carry[...] = lax.fori_loop(0, sub, step, carry[...])

    spec = pl.BlockSpec((ts, w), lambda i: (steps - 1 - i, 0))
    return pl.pallas_call(
        body,
        name=name,
        grid=(steps,),
        in_specs=[spec, spec],
        out_specs=spec,
        out_shape=jax.ShapeDtypeStruct((s, w), F32),
        scratch_shapes=[pltpu.VMEM((V7X_SUBLANES, w), F32)],
        compiler_params=_cparams(dimension_semantics=("arbitrary",)),
    )(a, dh)


SKEW = 4 * ATT_TQ


def _skew_onehot():
    t = np.arange(SKEW)
    diag = np.where(t < 3 * ATT_TQ, -t, SKEW - t)
    idx = np.clip(diag + LEFT_CHUNKS * CHUNK, -MAX_REL, MAX_REL) + MAX_REL
    hit = (idx[:, None] == np.arange(2 * MAX_REL + 1)[None, :]) & (t[:, None] != 3 * ATT_TQ)
    return hit.astype(np.float32)


def _bias_tile(rel_bias, name):
    per_t = jnp.dot(rel_bias, jnp.asarray(_skew_onehot()).T, precision=lax.Precision.HIGHEST)
    win = 3 * ATT_TQ

    def body(t_ref, o_ref):
        tile = pltpu.roll(jnp.broadcast_to(t_ref[0], (ATT_TQ, SKEW)), 0, 1, stride=1, stride_axis=0)[:, :win]
        qc = lax.broadcasted_iota(jnp.int32, (ATT_TQ, win), 0) // CHUNK
        kpos = lax.broadcasted_iota(jnp.int32, (ATT_TQ, win), 1)
        band = (kpos // CHUNK >= qc) & (kpos // CHUNK <= qc + LEFT_CHUNKS)
        for v in range(3):
            o_ref[v, 0] = jnp.where(band & (kpos >= (2 - v) * ATT_TQ), tile, NEG)

    return pl.pallas_call(
        body,
        name=name,
        grid=(ATT_HEADS,),
        in_specs=[pl.BlockSpec((1, 1, SKEW), lambda h: (h, 0, 0))],
        out_specs=pl.BlockSpec((3, 1, ATT_TQ, win), lambda h: (0, h, 0, 0)),
        out_shape=jax.ShapeDtypeStruct((3, ATT_HEADS, ATT_TQ, win), F32),
        compiler_params=_cparams(dimension_semantics=("parallel",)),
    )(per_t.reshape(ATT_HEADS, 1, SKEW))


def _bias_grad(dbias, name):
    win = 3 * ATT_TQ

    def body(d_ref, o_ref):
        d = jnp.concatenate([d_ref[0], jnp.zeros((ATT_TQ, SKEW - win), F32)], axis=1)
        r = lax.broadcasted_iota(jnp.int32, (ATT_TQ, ATT_TQ), 0)
        c = lax.broadcasted_iota(jnp.int32, (ATT_TQ, ATT_TQ), 1)
        flip = (r + c == ATT_TQ - 1).astype(F32)
        d = jnp.dot(flip, d, preferred_element_type=F32, precision=lax.Precision.HIGHEST)
        o_ref[0] = jnp.sum(pltpu.roll(d, SKEW - (ATT_TQ - 1), 1, stride=1, stride_axis=0), axis=0, keepdims=True)

    per_t = pl.pallas_call(
        body,
        name=name,
        grid=(ATT_HEADS,),
        in_specs=[pl.BlockSpec((1, ATT_TQ, win), lambda h: (h, 0, 0))],
        out_specs=pl.BlockSpec((1, 1, SKEW), lambda h: (h, 0, 0)),
        out_shape=jax.ShapeDtypeStruct((ATT_HEADS, 1, SKEW), F32),
        compiler_params=_cparams(dimension_semantics=("parallel",)),
    )(dbias)
    return jnp.dot(per_t.reshape(ATT_HEADS, SKEW), jnp.asarray(_skew_onehot()), precision=lax.Precision.HIGHEST)


ATT_STEP_HEADS = ATT_HEADS
ATT_STEP_COLS = ATT_STEP_HEADS * ATT_HEAD_DIM


def _attn_specs(nt):
    qb, kb, vb = OFF_Q // ATT_STEP_COLS, OFF_K // ATT_STEP_COLS, OFF_V // ATT_STEP_COLS
    blk = (ATT_TQ, ATT_STEP_COLS)

    def qmap(base):
        return lambda hp, m: (jnp.minimum(m, nt - 1), base + hp)

    def wmap(base, back):
        return lambda hp, m: (jnp.clip(m - back, 0, nt - 1), base + hp)

    specs = [pl.BlockSpec(blk, qmap(qb))]
    specs += [pl.BlockSpec(blk, wmap(kb, back)) for back in (2, 1, 0)]
    specs += [pl.BlockSpec(blk, wmap(vb, back)) for back in (2, 1, 0)]
    return specs


ATT_SCALE = ATT_HEAD_DIM**-0.5


def _attn_exp(qh, kh, bias):
    s = lax.dot_general(qh, kh, _DIMS["nt"], preferred_element_type=F32) + bias
    e = jnp.exp(s - jnp.max(s, axis=-1, keepdims=True))
    return e, jnp.sum(e, axis=-1, keepdims=True)


def _attn_window(k0, k1, k2, v0, v1, v2):
    k = jnp.concatenate([k0[...], k1[...], k2[...]], axis=0).astype(BF16)
    v = jnp.concatenate([v0[...], v1[...], v2[...]], axis=0).astype(BF16)
    return k, v


def _bias_spec():
    return pl.BlockSpec((1, ATT_STEP_HEADS, ATT_TQ, 3 * ATT_TQ), lambda hp, m: (jnp.minimum(m, 2), hp, 0, 0))


def _attn_fwd(proj, bias, name):
    s = proj.shape[0]
    nt = s // ATT_TQ

    def body(q_ref, k0, k1, k2, v0, v1, v2, b_ref, o_ref):
        k, v = _attn_window(k0, k1, k2, v0, v1, v2)
        q = (q_ref[...] * ATT_SCALE).astype(BF16)
        for hh in range(ATT_STEP_HEADS):
            cols = slice(hh * ATT_HEAD_DIM, (hh + 1) * ATT_HEAD_DIM)
            e, total = _attn_exp(q[:, cols], k[:, cols], b_ref[0, hh])
            o = jnp.dot(e.astype(BF16), v[:, cols], preferred_element_type=F32) / total
            o_ref[:, cols] = o.astype(o_ref.dtype)

    specs = _attn_specs(nt) + [_bias_spec()]
    return pl.pallas_call(
        body,
        name=name,
        grid=(ATT_HEADS // ATT_STEP_HEADS, nt),
        in_specs=specs,
        out_specs=pl.BlockSpec((ATT_TQ, ATT_STEP_COLS), lambda hp, m: (m, hp)),
        out_shape=jax.ShapeDtypeStruct((s, ATT_WIDTH), BF16),
        compiler_params=_cparams(dimension_semantics=("parallel", "arbitrary")),
    )(proj, proj, proj, proj, proj, proj, proj, bias)


def _attn_bwd(proj, bias, do, name):
    s = proj.shape[0]
    nt = s // ATT_TQ
    win = 3 * ATT_TQ

    def body(q_ref, k0, k1, k2, v0, v1, v2, do_ref, b_ref, dq_ref, dk_ref, dv_ref, db_ref, dk_acc, dv_acc):
        m = pl.program_id(1)

        @pl.when(m == 0)
        def _():
            dk_acc[...] = jnp.zeros_like(dk_acc)
            dv_acc[...] = jnp.zeros_like(dv_acc)
            db_ref[...] = jnp.zeros_like(db_ref)

        @pl.when(m < nt)
        def _():
            k, v = _attn_window(k0, k1, k2, v0, v1, v2)
            q = (q_ref[...] * ATT_SCALE).astype(BF16)
            dout = do_ref[...]
            for hh in range(ATT_STEP_HEADS):
                cols = slice(hh * ATT_HEAD_DIM, (hh + 1) * ATT_HEAD_DIM)
                qh, kh, vh, doh = q[:, cols], k[:, cols], v[:, cols], dout[:, cols]
                e, total = _attn_exp(qh, kh, b_ref[0, hh])
                p = e / total
                dvh = lax.dot_general(p.astype(BF16), doh, _DIMS["tn"], preferred_element_type=F32)
                dp = lax.dot_general(doh, vh, _DIMS["nt"], preferred_element_type=F32)
                ds = p * (dp - jnp.sum(dp * p, axis=-1, keepdims=True))
                db_ref[hh] += ds
                dsb = ds.astype(BF16)
                dqh = jnp.dot(dsb, kh, preferred_element_type=F32) * ATT_SCALE
                dkh = lax.dot_general(dsb, qh, _DIMS["tn"], preferred_element_type=F32)
                dq_ref[:, cols] = dqh.astype(dq_ref.dtype)
                dk_acc[:, cols] += dkh
                dv_acc[:, cols] += dvh

        dk_ref[...] = dk_acc[:ATT_TQ].astype(dk_ref.dtype)
        dv_ref[...] = dv_acc[:ATT_TQ].astype(dv_ref.dtype)
        for acc in (dk_acc, dv_acc):
            rest = acc[ATT_TQ:]
            acc[: win - ATT_TQ] = rest
            acc[win - ATT_TQ :] = jnp.zeros((ATT_TQ, ATT_STEP_COLS), F32)

    blk = (ATT_TQ, ATT_STEP_COLS)
    specs = _attn_specs(nt)
    specs.append(pl.BlockSpec(blk, lambda hp, m: (jnp.minimum(m, nt - 1), hp)))
    specs.append(_bias_spec())
    done = lambda hp, m: (jnp.maximum(m - 2, 0), hp)
    out_specs = [
        pl.BlockSpec(blk, lambda hp, m: (jnp.minimum(m, nt - 1), hp)),
        pl.BlockSpec(blk, done),
        pl.BlockSpec(blk, done),
        pl.BlockSpec((ATT_STEP_HEADS, ATT_TQ, win), lambda hp, m: (hp, 0, 0)),
    ]
    out_shape = [jax.ShapeDtypeStruct((s, ATT_WIDTH), BF16)] * 3
    out_shape.append(jax.ShapeDtypeStruct((ATT_HEADS, ATT_TQ, win), F32))
    return pl.pallas_call(
        body,
        name=name,
        grid=(ATT_HEADS // ATT_STEP_HEADS, nt + 2),
        in_specs=specs,
        out_specs=out_specs,
        out_shape=out_shape,
        scratch_shapes=[pltpu.VMEM((win, ATT_STEP_COLS), F32), pltpu.VMEM((win, ATT_STEP_COLS), F32)],
        compiler_params=_cparams(dimension_semantics=("arbitrary", "arbitrary")),
    )(proj, proj, proj, proj, proj, proj, proj, do, bias)


def _ada_fwd(c_all, w, name):
    def body(c_ref, w_ref, o_ref):
        act = _silu(c_ref[...]).astype(BF16)
        o_ref[...] = jnp.dot(act, w_ref[...].astype(BF16), preferred_element_type=F32)

    return pl.pallas_call(
        body, name=name, out_shape=jax.ShapeDtypeStruct((c_all.shape[0], w.shape[1]), F32), compiler_params=_cparams()
    )(c_all, w)


def _ada_bwd(c_all, dmod, name):
    def body(c_ref, d_ref, o_ref):
        act = _silu(c_ref[...])
        o_ref[...] = lax.dot_general(act, d_ref[...], _DIMS["tn"], preferred_element_type=F32,
                                     precision=lax.Precision.HIGHEST)

    return pl.pallas_call(
        body, name=name, out_shape=jax.ShapeDtypeStruct((c_all.shape[1], dmod.shape[1]), F32), compiler_params=_cparams()
    )(c_all, dmod)


def _adamw_parts(landed, sent, me, w, m, v, name, rows=256):
    r, c = w.shape
    tr = _pick(r, rows, 16)

    def body(me_ref, g_ref, own_ref, w_ref, m_ref, v_ref, go_ref, d_ref, mo_ref, vo_ref):
        mine = me_ref[0]
        grad = jnp.zeros((tr, c), F32)
        for d in range(N_DEV):
            grad = grad + jnp.where(mine == d, own_ref[0], g_ref[d]).astype(F32)
        _adamw_update(grad, w_ref, m_ref, v_ref, go_ref, d_ref, mo_ref, vo_ref)

    spec = pl.BlockSpec((tr, c), lambda i, me_ref: (i, 0))
    return pl.pallas_call(
        body,
        name=name,
        grid_spec=pltpu.PrefetchScalarGridSpec(
            num_scalar_prefetch=1,
            grid=(r // tr,),
            in_specs=[pl.BlockSpec((N_DEV, tr, c), lambda i, me_ref: (0, i, 0)),
                      pl.BlockSpec((1, tr, c), lambda i, me_ref: (me_ref[0], i, 0)), spec, spec, spec],
            out_specs=[spec] * 4,
        ),
        out_shape=[jax.ShapeDtypeStruct((r, c), F32)] * 4,
        compiler_params=_cparams(dimension_semantics=("parallel",)),
    )(me.reshape(1).astype(jnp.int32), landed, sent, w, m, v)


def _adamw_update(grad, w_ref, m_ref, v_ref, go_ref, d_ref, mo_ref, vo_ref):
    m2 = ADAM_B1 * m_ref[...] + (1.0 - ADAM_B1) * grad
    v2 = ADAM_B2 * v_ref[...] + (1.0 - ADAM_B2) * (grad * grad)
    m_hat = m2 / (1.0 - ADAM_B1**ADAM_STEP)
    v_hat = v2 / (1.0 - ADAM_B2**ADAM_STEP)
    go_ref[...] = grad
    d_ref[...] = -ADAM_LR * (m_hat / (jnp.sqrt(v_hat) + ADAM_EPS) + ADAM_WD * w_ref[...])
    mo_ref[...] = m2
    vo_ref[...] = v2


def _adamw(g, w, m, v, name, rows=256):
    r, c = w.shape
    tr = _pick(r, rows, 16)

    def body(g_ref, w_ref, m_ref, v_ref, go_ref, d_ref, mo_ref, vo_ref):
        _adamw_update(g_ref[...], w_ref, m_ref, v_ref, go_ref, d_ref, mo_ref, vo_ref)

    spec = pl.BlockSpec((tr, c), lambda i: (i, 0))
    return pl.pallas_call(
        body,
        name=name,
        grid=(r // tr,),
        in_specs=[spec, spec, spec, spec],
        out_specs=[spec] * 4,
        out_shape=[jax.ShapeDtypeStruct((r, c), F32)] * 4,
        compiler_params=_cparams(dimension_semantics=("parallel",)),
    )(g, w, m, v)


def _sum_parts(parts, name):
    def body(p_ref, o_ref):
        acc = p_ref[0]
        for d in range(1, N_DEV):
            acc = acc + p_ref[d]
        o_ref[...] = acc

    return pl.pallas_call(
        body, name=name, out_shape=jax.ShapeDtypeStruct(parts.shape[1:], F32), compiler_params=_cparams()
    )(parts)


def _place():
    x, y, c = lax.axis_index("x"), lax.axis_index("y"), lax.axis_index("c")
    return x, y, c


def _dev_index(p):
    return 4 * p[0] + 2 * p[1] + p[2]


def _allgather_vmem(shard, name):
    m_per, n = shard.shape

    def body(x_ref, out_ref, send_sems, recv_sems, local_sem):
        x, y, c = _place()
        me, sibling = (x, y, c), (x, y, 1 - c)
        chips = [(1 - x, y), (x, 1 - y), (1 - x, 1 - y)]

        def rows(p):
            return out_ref.at[pl.ds(_dev_index(p) * m_per, m_per), :]

        def copy(k, block, to, src=None):
            return pltpu.make_async_remote_copy(
                src_ref=rows(block) if src is None else src, dst_ref=rows(block),
                send_sem=send_sems.at[k], recv_sem=recv_sems.at[k], device_id=to, device_id_type=MESH)

        mine = pltpu.make_async_copy(x_ref, rows(me), local_sem)
        mine.start()
        first = [copy(0, me, sibling, src=x_ref)]
        first += [copy(1 + j, me, (*chip, c), src=x_ref) for j, chip in enumerate(chips)]
        for cp in first:
            cp.start()
        passed = [copy(4 + j, (*chip, c), sibling) for j, chip in enumerate(chips)]
        for j, chip in enumerate(chips):
            copy(1 + j, (*chip, c), me).wait_recv()
            passed[j].start()
        copy(0, sibling, me).wait_recv()
        for j, chip in enumerate(chips):
            copy(4 + j, (*chip, 1 - c), me).wait_recv()
        for cp in first + passed:
            cp.wait_send()
        mine.wait()

    return pl.pallas_call(
        body,
        name=name,
        out_shape=jax.ShapeDtypeStruct((N_DEV * m_per, n), shard.dtype),
        in_specs=[pl.BlockSpec(memory_space=pltpu.VMEM)],
        out_specs=pl.BlockSpec(memory_space=pltpu.VMEM),
        scratch_shapes=[pltpu.SemaphoreType.DMA((7,)), pltpu.SemaphoreType.DMA((7,)), pltpu.SemaphoreType.DMA],
        compiler_params=_cparams(),
    )(shard)


def _allgather_hbm(shards, name):
    n = len(shards)

    def body(*refs):
        ins, outs = refs[:n], refs[n : 2 * n]
        send_sems, recv_sems, local_sems = refs[2 * n :]
        x, y, c = _place()
        me, sibling = (x, y, c), (x, y, 1 - c)
        chips = [(1 - x, y), (x, 1 - y), (1 - x, 1 - y)]

        def copy(a, k, block, to, src=None):
            dst = outs[a].at[_dev_index(block)]
            return pltpu.make_async_remote_copy(
                src_ref=dst if src is None else src, dst_ref=dst,
                send_sem=send_sems.at[a * 7 + k], recv_sem=recv_sems.at[a * 7 + k], device_id=to, device_id_type=MESH)

        mine = [pltpu.make_async_copy(ins[a], outs[a].at[_dev_index(me)], local_sems.at[a]) for a in range(n)]
        for cp in mine:
            cp.start()
        first = []
        for a in range(n):
            first.append(copy(a, 0, me, sibling, src=ins[a]))
            first += [copy(a, 1 + j, me, (*chip, c), src=ins[a]) for j, chip in enumerate(chips)]
        for cp in first:
            cp.start()
        passed = []
        for j, chip in enumerate(chips):
            for a in range(n):
                copy(a, 1 + j, (*chip, c), me).wait_recv()
                cp = copy(a, 4 + j, (*chip, c), sibling)
                cp.start()
                passed.append(cp)
        for a in range(n):
            copy(a, 0, sibling, me).wait_recv()
        for j, chip in enumerate(chips):
            for a in range(n):
                copy(a, 4 + j, (*chip, 1 - c), me).wait_recv()
        for cp in first + passed:
            cp.wait_send()
        for cp in mine:
            cp.wait()

    any_spec = pl.BlockSpec(memory_space=pl.ANY)
    return pl.pallas_call(
        body,
        name=name,
        out_shape=[jax.ShapeDtypeStruct((N_DEV, *s.shape), s.dtype) for s in shards],
        in_specs=[any_spec] * n,
        out_specs=[any_spec] * n,
        scratch_shapes=[pltpu.SemaphoreType.DMA((7 * n,)), pltpu.SemaphoreType.DMA((7 * n,)),
                        pltpu.SemaphoreType.DMA((n,))],
        compiler_params=_cparams(),
    )(*shards)


HBM_SPEC = pl.BlockSpec(memory_space=pltpu.HBM)
SEM_SPEC = pl.BlockSpec(memory_space=pltpu.SEMAPHORE)
EFFECT = pltpu.SideEffectType.DATAFLOW_SIDE_EFFECTING


def _peers(x, y, c):
    return [(1 - x if k & 4 else x, 1 - y if k & 2 else y, 1 - c if k & 1 else c) for k in range(1, N_DEV)]


def _push_peers(mode, x, y, c):
    if mode == "all":
        return _peers(x, y, c)
    return [(x, y, 1 - c), (1 - x, y, c), (x, 1 - y, c), (1 - x, 1 - y, c)]


def _push_start(groups, sliced, name, after=(), modes=None):
    flat = [b for g in groups for b in g]
    n, ng = len(flat), len(groups)
    sizes = [len(g) for g in groups]
    modes = modes or ["all"] * ng
    fan = [len(_push_peers(m, 0, 0, 0)) for m in modes]
    per = 2 if sliced else 3
    lands = [lax.empty(b.shape if sliced else (N_DEV, *b.shape), b.dtype) for b in flat]

    def body(*refs):
        ins, lnd = refs[:n], refs[n : 2 * n]
        sems = refs[2 * n + len(after) : 2 * n + len(after) + per * ng]
        token = refs[-1]
        x, y, c = _place()
        me = _dev_index((x, y, c))
        if not sliced:
            first = 0
            for gi, size in enumerate(sizes):
                for j in range(first, first + size):
                    pltpu.make_async_copy(ins[j], lnd[j].at[me], sems[per * gi + 2].at[j - first]).start()
                first += size
        first = 0
        for gi, size in enumerate(sizes):
            for k, peer in enumerate(_push_peers(modes[gi], x, y, c)):
                for j in range(first, first + size):
                    sem = (j - first) * fan[gi] + k
                    pltpu.make_async_remote_copy(
                        src_ref=ins[j].at[_dev_index(peer)] if sliced else ins[j], dst_ref=lnd[j].at[me],
                        send_sem=sems[per * gi].at[sem], recv_sem=sems[per * gi + 1].at[sem],
                        device_id=peer, device_id_type=MESH).start()
            first += size
        token[...] = jnp.zeros_like(token)

    out_shape = []
    for size, width in zip(sizes, fan):
        out_shape += [pltpu.SemaphoreType.DMA((width * size,)), pltpu.SemaphoreType.DMA((width * size,))]
        out_shape += [] if sliced else [pltpu.SemaphoreType.DMA((size,))]
    out_shape += [pltpu.HBM(b.shape, b.dtype) for b in flat + lands]
    out_shape.append(jax.ShapeDtypeStruct((V7X_SUBLANES, V7X_LANES), F32))
    res = pl.pallas_call(
        body,
        name=name,
        out_shape=tuple(out_shape),
        in_specs=[HBM_SPEC] * (2 * n) + [ANY_SPEC] * len(after),
        out_specs=tuple([SEM_SPEC] * (per * ng) + [HBM_SPEC] * (2 * n) + [pl.BlockSpec(memory_space=pltpu.VMEM)]),
        input_output_aliases={i: per * ng + i for i in range(2 * n)},
        compiler_params=pltpu.CompilerParams(has_side_effects=EFFECT),
    )(*[pltpu.with_memory_space_constraint(b, pltpu.HBM) for b in flat + lands], *after)
    sems, thru, token = res[: per * ng], res[per * ng : per * ng + 2 * n], res[-1]
    out, first = [], 0
    for gi, size in enumerate(sizes):
        out.append((sems[per * gi], sems[per * gi + 1], list(thru[first : first + size]),
                    list(thru[n + first : n + first + size]), None if sliced else sems[per * gi + 2]))
        first += size
    return out, token


def _push_wait(started, sliced, after, name, mode="all"):
    send_sems, recv_sems, bufs, lands, own_sems = started
    n = len(bufs)
    fan = len(_push_peers(mode, 0, 0, 0))
    own = [] if own_sems is None else [own_sems]

    def body(*refs):
        ins, lnd = refs[:n], refs[n : 2 * n]
        send_ref, recv_ref = refs[2 * n], refs[2 * n + 1]
        x, y, c = _place()
        for k, peer in enumerate(_push_peers(mode, x, y, c)):
            for j in range(n):
                cp = pltpu.make_async_remote_copy(
                    src_ref=ins[j].at[_dev_index(peer)] if sliced else ins[j], dst_ref=lnd[j].at[_dev_index(peer)],
                    send_sem=send_ref.at[j * fan + k], recv_sem=recv_ref.at[j * fan + k],
                    device_id=peer, device_id_type=MESH)
                cp.wait_send()
                cp.wait_recv()
        if own:
            for j in range(n):
                pltpu.make_async_copy(ins[j], lnd[j].at[_dev_index((x, y, c))], refs[2 * n + 2].at[j]).wait()

    res = pl.pallas_call(
        body,
        name=name,
        out_shape=tuple(pltpu.HBM(b.shape, b.dtype) for b in bufs + lands),
        in_specs=[HBM_SPEC] * (2 * n) + [SEM_SPEC] * (2 + len(own)) + [pl.BlockSpec(memory_space=pl.ANY)],
        out_specs=tuple([HBM_SPEC] * (2 * n)),
        input_output_aliases={i: i for i in range(2 * n)},
        compiler_params=pltpu.CompilerParams(has_side_effects=EFFECT),
    )(*bufs, *lands, send_sems, recv_sems, *own, after)
    return list(res[:n]), list(res[n:])


def _forward_copies(lnd, send_ref, recv_ref, incoming):
    x, y, c = _place()
    copies = []
    for k, chip in enumerate([(1 - x, y), (x, 1 - y), (1 - x, 1 - y)]):
        mine, theirs = _dev_index((*chip, c)), _dev_index((*chip, 1 - c))
        for j, ref in enumerate(lnd):
            copies.append(pltpu.make_async_remote_copy(
                src_ref=ref.at[mine], dst_ref=ref.at[theirs if incoming else mine],
                send_sem=send_ref.at[j * 3 + k], recv_sem=recv_ref.at[j * 3 + k],
                device_id=(x, y, 1 - c), device_id_type=MESH))
    return copies


def _forward_start(lands, name):
    n = len(lands)

    def body(*refs):
        for cp in _forward_copies(refs[:n], refs[n], refs[n + 1], False):
            cp.start()

    res = pl.pallas_call(
        body,
        name=name,
        out_shape=(pltpu.SemaphoreType.DMA((3 * n,)), pltpu.SemaphoreType.DMA((3 * n,)),
                   *[pltpu.HBM(b.shape, b.dtype) for b in lands]),
        in_specs=[HBM_SPEC] * n,
        out_specs=(SEM_SPEC, SEM_SPEC, *[HBM_SPEC] * n),
        input_output_aliases={i: 2 + i for i in range(n)},
        compiler_params=pltpu.CompilerParams(has_side_effects=EFFECT),
    )(*[pltpu.with_memory_space_constraint(b, pltpu.HBM) for b in lands])
    return res[0], res[1], list(res[2:])


def _forward_wait(started, after, name):
    send_sems, recv_sems, lands = started
    n = len(lands)

    def body(*refs):
        for cp in _forward_copies(refs[:n], refs[n], refs[n + 1], True):
            cp.wait_send()
            cp.wait_recv()

    res = pl.pallas_call(
        body,
        name=name,
        out_shape=tuple(pltpu.HBM(b.shape, b.dtype) for b in lands),
        in_specs=[HBM_SPEC] * n + [SEM_SPEC, SEM_SPEC, pl.BlockSpec(memory_space=pl.ANY)],
        out_specs=tuple([HBM_SPEC] * n),
        input_output_aliases={i: i for i in range(n)},
        compiler_params=pltpu.CompilerParams(has_side_effects=EFFECT),
    )(*lands, send_sems, recv_sems, after)
    return list(res)


def _cols_full(g):
    return jnp.transpose(g, (1, 0, 2)).reshape(g.shape[1], -1)


def _rows_full(g):
    return g.reshape(-1, g.shape[2])


def _cols_parts(full, n=N_DEV):
    r = full.shape[0]
    return jnp.transpose(full.reshape(r, n, -1), (1, 0, 2)).astype(BF16)


def _rows_parts(full):
    return full.reshape(N_DEV, -1, full.shape[1]).astype(BF16)


def _pad_rows(v, rows):
    flat = v.reshape(-1)
    return jnp.pad(flat, (0, rows * D_MODEL - flat.shape[0])).reshape(rows, D_MODEL)


def _my_cols(full, me, width):
    return lax.dynamic_slice_in_dim(full, me * width, width, axis=full.ndim - 1)


def kernel(x, c, w_ada, b_ada, norm_pre, norm_post, ffn1_w_gu, ffn1_w_down, w_in, rel_bias, conv_w, conv_b, lru_wa, lru_ba, lru_wx, lru_bx, lru_lambda, w_att_o, w_rec_o, w_out, ffn2_w_gu, ffn2_w_down, loss_target, m_w_ada, m_b_ada, m_norm_pre, m_norm_post, m_ffn1_w_gu, m_ffn1_w_down, m_w_in, m_rel_bias, m_conv_w, m_conv_b, m_lru_wa, m_lru_ba, m_lru_wx, m_lru_bx, m_lru_lambda, m_w_att_o, m_w_rec_o, m_w_out, m_ffn2_w_gu, m_ffn2_w_down, v_w_ada, v_b_ada, v_norm_pre, v_norm_post, v_ffn1_w_gu, v_ffn1_w_down, v_w_in, v_rel_bias, v_conv_w, v_conv_b, v_lru_wa, v_lru_ba, v_lru_wx, v_lru_bx, v_lru_lambda, v_w_att_o, v_w_rec_o, v_w_out, v_ffn2_w_gu, v_ffn2_w_down):
    weights = dict(w_ada=w_ada, b_ada=b_ada, norm_pre=norm_pre, norm_post=norm_post, ffn1_w_gu=ffn1_w_gu,
                   ffn1_w_down=ffn1_w_down, w_in=w_in, rel_bias=rel_bias, conv_w=conv_w, conv_b=conv_b,
                   lru_wa=lru_wa, lru_ba=lru_ba, lru_wx=lru_wx, lru_bx=lru_bx, lru_lambda=lru_lambda,
                   w_att_o=w_att_o, w_rec_o=w_rec_o, w_out=w_out, ffn2_w_gu=ffn2_w_gu, ffn2_w_down=ffn2_w_down)
    mom1 = dict(w_ada=m_w_ada, b_ada=m_b_ada, norm_pre=m_norm_pre, norm_post=m_norm_post, ffn1_w_gu=m_ffn1_w_gu,
                ffn1_w_down=m_ffn1_w_down, w_in=m_w_in, rel_bias=m_rel_bias, conv_w=m_conv_w, conv_b=m_conv_b,
                lru_wa=m_lru_wa, lru_ba=m_lru_ba, lru_wx=m_lru_wx, lru_bx=m_lru_bx, lru_lambda=m_lru_lambda,
                w_att_o=m_w_att_o, w_rec_o=m_w_rec_o, w_out=m_w_out, ffn2_w_gu=m_ffn2_w_gu, ffn2_w_down=m_ffn2_w_down)
    mom2 = dict(w_ada=v_w_ada, b_ada=v_b_ada, norm_pre=v_norm_pre, norm_post=v_norm_post, ffn1_w_gu=v_ffn1_w_gu,
                ffn1_w_down=v_ffn1_w_down, w_in=v_w_in, rel_bias=v_rel_bias, conv_w=v_conv_w, conv_b=v_conv_b,
                lru_wa=v_lru_wa, lru_ba=v_lru_ba, lru_wx=v_lru_wx, lru_bx=v_lru_bx, lru_lambda=v_lru_lambda,
                w_att_o=v_w_att_o, w_rec_o=v_w_rec_o, w_out=v_w_out, ffn2_w_gu=v_ffn2_w_gu, ffn2_w_down=v_ffn2_w_down)
    order = list(weights)
    big = ["ffn1_w_gu", "ffn1_w_down", "w_in", "w_att_o", "w_rec_o", "w_out", "ffn2_w_gu", "ffn2_w_down"]
    small = ["b_ada", "norm_pre", "norm_post", "rel_bias", "conv_w", "conv_b", "lru_wa", "lru_ba", "lru_wx",
             "lru_bx", "lru_lambda"]

    xi, yi, ci = _place()
    me = _dev_index((xi, yi, ci))
    x0 = x[0]
    target = loss_target[0]
    fuse_tm = min(FUSE_TM, x0.shape[0])

    transposed = {"ffn1_w_gu", "w_in", "ffn2_w_gu"}
    local = lambda n, arr: jnp.transpose(arr[0]) if n in transposed else arr[0]
    shards = {n: local(n, weights[n]).astype(BF16) for n in big}
    full_of = lambda n, g: _cols_full(g) if n == "w_att_o" else _rows_full(g)

    pack = jnp.concatenate([c.reshape(-1), norm_pre.reshape(-1), norm_post.reshape(-1), conv_w.reshape(-1)])
    pack = jnp.pad(pack, (0, 3072 - pack.shape[0])).reshape(8, 384)
    got, w1_gu = _allgather_hbm([pack, shards["ffn1_w_gu"]], "gather_first")
    got = got.reshape(N_DEV, 3072)
    c_all = got[:, :1024]
    unshard = lambda blk, rows: jnp.transpose(blk.reshape(N_DEV, rows, 128), (1, 0, 2)).reshape(rows, D_MODEL)
    g_pre = unshard(got[:, 1024:1408], 3)
    g_post = unshard(got[:, 1408:1792], 3)
    conv_taps = unshard(got[:, 1792:2304], 4)
    conv_w8 = jnp.concatenate([conv_taps, jnp.zeros((4, LRU_WIDTH), F32)], axis=0)

    mod_cols = _ada_fwd(c_all, w_ada[0], "ada_fwd")
    mod_all = _allgather_vmem(mod_cols, "gather_mod").reshape(N_DEV, N_DEV, 1152)
    mod = lax.dynamic_index_in_dim(mod_all, me, axis=1, keepdims=False).reshape(1, -1) + b_ada
    mod = mod.reshape(3, 3, 1, D_MODEL)

    w_slab = _slab_weights(lru_wa[0], lru_wx[0])
    bias = _bias_tile(rel_bias[0], "bias_tile")

    res_w = (0.5, 1.0, 0.5)
    row = lambda v: v.reshape(1, -1)

    weight_groups = [["ffn1_w_down"], ["w_in"], ["w_att_o", "w_rec_o", "w_out"], ["ffn2_w_gu", "ffn2_w_down"]]
    weight_modes = ["all", "chip", "all", "all"]
    weights_started, started = _push_start([[shards[n] for n in g] for g in weight_groups], False,
                                           "gather_weights_start", after=(mod, w1_gu), modes=weight_modes)
    full = {"ffn1_w_gu": _rows_full(w1_gu)}

    def gathered_group(gi, after):
        sent, lands = _push_wait(weights_started[gi], False, after, f"gather_weights_wait{gi}", mode=weight_modes[gi])
        if weight_modes[gi] == "chip":
            lands = _forward_wait(_forward_start(lands, f"gather_weights_forward{gi}"), sent[0],
                                  f"gather_weights_forward_wait{gi}")
        for n, land in zip(weight_groups[gi], lands):
            full[n] = full_of(n, land)

    def ffn_fwd(xin, k, gi, tag, deps=(), target=None):
        h, a, g, u = _pre_up(xin, row(g_pre[k]), mod[k, 0], mod[k, 1], full[f"{tag}_w_gu"], f"{tag}_up", deps=deps)
        if f"{tag}_w_down" not in full:
            gathered_group(gi, a)
        f, *out = _matmul_post(a, full[f"{tag}_w_down"], xin, row(g_post[k]), mod[k, 2], res_w[k], f"{tag}_down",
                               target=target)
        return (out[0] if target is None else out), (h, g, u, a, f)

    x1, saved1 = ffn_fwd(x0, 0, 0, "ffn1", deps=(started,))

    gathered_group(1, x1)
    h2, proj = _pre_matmul(x1, row(g_pre[1]), mod[1, 0], mod[1, 1], full["w_in"], "mix_in",
                           b_shift=3 * ATT_WIDTH // 512)
    att_o = _attn_fwd(proj, bias, "attn_fwd")
    gathered_group(2, att_o)
    xc, pre, a_t, u_t = _lru_front(proj, conv_w8, conv_b, w_slab, lru_ba, lru_bx, lru_lambda, "lru_front")
    hs, h_prev, rec_in = _scan_fwd(a_t, u_t, proj, "lru_scan")
    att = _matmul(att_o, full["w_att_o"], "nn", F32, "att_out")
    rec = _matmul(rec_in, full["w_rec_o"], "nn", F32, "rec_out")
    merged, f2, x2 = _merge_matmul_post(att, rec, proj, full["w_out"], x1, row(g_post[1]), mod[1, 2], res_w[1],
                                        "mix_out")

    gathered_group(3, x2)
    (dy, sq), saved3 = ffn_fwd(x2, 2, 2, "ffn2", target=target)
    loss = lax.psum(0.5 * jnp.sum(sq) / D_MODEL, ("x", "y", "c"))

    grads = {}
    norm_sums = [None] * 6

    pending = []

    def exchange_start(names, tag, after=()):
        send = [(_cols_parts if n == "w_att_o" else _rows_parts)(grads[n]) for n in names]
        (group,), token = _push_start([send], True, f"exchange_{tag}_start", after=after)
        pending.append((names, send, group, tag))
        return token

    def exchange_finish(names, send, group, tag, after):
        sent, lands = _push_wait(group, True, after, f"exchange_{tag}_wait")
        res = None
        for n, land, mine in zip(names, lands, sent):
            res = _adamw_parts(land, mine, me, local(n, weights[n]), local(n, mom1[n]), local(n, mom2[n]),
                               f"adamw_{n}")
            back = (lambda r: jnp.transpose(r)) if n in transposed else (lambda r: r)
            out_g[n], out_d[n], out_m[n], out_v[n] = [back(r).reshape(weights[n].shape) for r in res]
        return res[0]

    out_g, out_d, out_m, out_v = {}, {}, {}, {}

    def ffn_bwd(xin, k, saved, dout, tag):
        h, g, u, a, f = saved
        w_gu, w_down = f"{tag}_w_gu", f"{tag}_w_down"
        df, dgu, norm_sums[2 * k + 1] = _post_bwd_up_bwd(f, dout, row(g_post[k]), mod[k, 2], res_w[k], full[w_down],
                                                          g, u, f"{tag}_up_bwd")
        grads[w_down] = _matmul(a, df, "tn", BF16, f"{tag}_dw_down", tm=1408, tn=1024, tk=DW_TK)
        started = exchange_start([w_down], w_down)
        grads[w_gu] = _dw_gu(dgu, h, f"{tag}_dw_gu", deps=(started,))
        started = exchange_start([w_gu], w_gu)
        halves = [(dgu, (None, fuse_tm, D_FF), lambda i, half=half: (half, i, 0), (half * D_FF, (half + 1) * D_FF))
                  for half in range(2)]
        dx, norm_sums[2 * k] = _matmul_pre_bwd(halves, full[w_gu], xin, dout, row(g_pre[k]),
                                                                mod[k, 0], mod[k, 1], f"{tag}_dh", deps=(started,))
        return dx

    dx2 = ffn_bwd(x2, 2, saved3, dy, "ffn2")

    df2, datt, drec, dg_att, dg_rec, norm_sums[3] = _post_bwd_merge_bwd(
        f2, dx2, row(g_post[1]), mod[1, 2], res_w[1], full["w_out"], att, rec, proj, "mix_dmerged")
    grads["w_out"] = _matmul(merged, df2, "tn", BF16, "mix_dw_out", tm=1024, tn=1024, tk=DW_TK)
    datt_o = _matmul(datt, full["w_att_o"], "nt", BF16, "att_out_bwd")
    grads["w_att_o"] = _matmul(att_o, datt, "tn", BF16, "dw_att_o", tm=512, tn=1024, tk=DW_TK)
    grads["w_rec_o"] = _matmul(rec_in, drec, "tn", BF16, "dw_rec_o", tm=1024, tn=1024, tk=DW_TK)
    started = exchange_start(["w_out", "w_att_o", "w_rec_o"], "mix_out")
    dhs, dyr = _matmul_recin_bwd(drec, full["w_rec_o"], hs, proj, "rec_out_bwd", deps=(started,))
    g_t = _scan_bwd(a_t, dhs, "lru_scan_bwd")
    dpre, dxc, lru_sums = _lru_back(pre, xc, w_slab, lru_ba, lru_bx, lru_lambda, g_t, h_prev, "lru_back")
    dxr, conv_sums = _conv_bwd(proj, conv_w8, dxc, "conv_bwd")
    dq, dk, dv, dbias = _attn_bwd(proj, bias, datt_o, "attn_bwd")
    dproj = jnp.concatenate([dq, dk, dv, dxr, dyr, dg_att, dg_rec], axis=1)
    grads["w_in"] = _matmul(dproj, h2, "tn", BF16, "mix_dw_in", tm=1408, tn=1024, tk=DW_TK)
    pack_mix = jnp.concatenate([conv_sums, lru_sums, _pad_rows(_bias_grad(dbias, "bias_grad"), V7X_SUBLANES),
                                _lru_dw(xc, dpre, "lru_dw").reshape(128, D_MODEL)], axis=0)
    (mix_started,), started = _push_start([[pack_mix]], False, "small_grads_mix_start")
    started = exchange_start(["w_in"], "w_in", after=(started,))
    whole = [(dproj, (fuse_tm, PROJ_WIDTH), lambda i: (i, 0), (0, PROJ_WIDTH))]
    dx1, norm_sums[2] = _matmul_pre_bwd(whole, full["w_in"], x1, dx2, row(g_pre[1]), mod[1, 0],
                                                             mod[1, 1], "mix_dh", deps=(started,))

    dx0 = ffn_bwd(x0, 0, saved1, dx1, "ffn1")

    pack_norm = jnp.concatenate(norm_sums, axis=0)
    (norm_started,), _ = _push_start([[pack_norm]], False, "small_grads_norm_start")

    def summed(started, pack, after, tag):
        _, (parts,) = _push_wait(started, False, after, f"small_grads_{tag}_wait")
        return parts, _sum_parts(parts, f"small_grads_{tag}_sum")

    done = dx0
    last = []
    for names, send, group, tag in pending:
        done = exchange_finish(names, send, group, tag, done)

    _, total = summed(mix_started, pack_mix, done, "mix")
    grads["conv_w"] = _my_cols(total[0:4], me, 128)
    grads["conv_b"] = total[4:5]
    grads["lru_ba"] = total[8:9]
    grads["lru_bx"] = total[9:10]
    grads["lru_lambda"] = total[10:11]
    grads["rel_bias"] = total[16:19].reshape(-1)[: ATT_HEADS * (2 * MAX_REL + 1)].reshape(ATT_HEADS, -1)
    grads["lru_wa"] = total[24:88].reshape(LRU_BLOCKS, LRU_BLOCK, LRU_BLOCK)
    grads["lru_wx"] = total[88:152].reshape(LRU_BLOCKS, LRU_BLOCK, LRU_BLOCK)
    parts, total = summed(norm_started, pack_norm, total, "norm")
    by_sandwich = lambda v: v.reshape(*v.shape[:-2], 3, 2 * V7X_SUBLANES, D_MODEL)
    dmod_of = lambda v: jnp.concatenate([by_sandwich(v)[..., 1:3, :], by_sandwich(v)[..., 9:10, :]], axis=-2)
    grads["b_ada"] = dmod_of(total).reshape(1, -1)
    grads["norm_pre"] = _my_cols(by_sandwich(total)[:, 0, :], me, 128)
    grads["norm_post"] = _my_cols(by_sandwich(total)[:, V7X_SUBLANES, :], me, 128)
    dmod_all = dmod_of(parts).reshape(N_DEV, 9 * D_MODEL)
    grads["w_ada"] = _ada_bwd(c_all, _my_cols(dmod_all, me, 1152), "ada_bwd")

    res = _adamw(grads["w_ada"], w_ada[0], m_w_ada[0], v_w_ada[0], "adamw_w_ada")
    out_g["w_ada"], out_d["w_ada"], out_m["w_ada"], out_v["w_ada"] = [r.reshape(w_ada.shape) for r in res]

    sizes = [int(np.prod(weights[n].shape)) for n in small]
    tot = sum(sizes)
    rows_small = -(-tot // (16 * D_MODEL)) * 16
    flat = lambda arrs: jnp.pad(jnp.concatenate([a.reshape(-1) for a in arrs]),
                                (0, rows_small * D_MODEL - tot)).reshape(rows_small, D_MODEL)
    res = _adamw(flat([grads[n] for n in small]), flat([weights[n] for n in small]),
                 flat([mom1[n] for n in small]), flat([mom2[n] for n in small]), "adamw_small", rows=rows_small)
    offs = np.cumsum([0] + sizes)
    for dst, r in zip((out_g, out_d, out_m, out_v), res):
        rf = r.reshape(-1)
        for i, n in enumerate(small):
            dst[n] = rf[offs[i] : offs[i + 1]].reshape(weights[n].shape)

    done = res[0]
    for names, send, group, tag in last:
        done = exchange_finish(names, send, group, tag, done)

    return (loss, dx0[None], *[out_g[n] for n in order], *[out_d[n] for n in order],
            *[out_m[n] for n in order], *[out_v[n] for n in order])
```

```python
import jax
import jax.numpy as jnp
import numpy as np
from jax import lax
from jax.experimental import pallas as pl
from jax.experimental.pallas import tpu as pltpu

D_MODEL = 1024
D_FF = 2816
ATT_HEADS = 8
ATT_HEAD_DIM = 64
ATT_WIDTH = 512
CHUNK = 64
LEFT_CHUNKS = 8
MAX_REL = 128
LRU_WIDTH = 1024
LRU_BLOCKS = 16
LRU_BLOCK = 64
LRU_C = 8.0
EPS = 1e-6
PROJ_WIDTH = 5632
N_DEV = 8

ADAM_LR = 0.001
ADAM_B1 = 0.9
ADAM_B2 = 0.999
ADAM_EPS = 1e-08
ADAM_WD = 0.01
ADAM_STEP = 10

V7X_LANES = 128
V7X_SUBLANES = 8
V7X_VMEM_BYTES = 64 * 1024 * 1024
VMEM_LIMIT = V7X_VMEM_BYTES - 8 * 1024 * 1024

ATT_TQ = 256
NEG = -1e30
BF16 = jnp.bfloat16
F32 = jnp.float32
MESH = pl.DeviceIdType.MESH

OFF_Q = 4 * LRU_WIDTH
OFF_K = OFF_Q + ATT_WIDTH
OFF_V = OFF_K + ATT_WIDTH


def _cparams(**kw):
    return pltpu.CompilerParams(vmem_limit_bytes=VMEM_LIMIT, **kw)


def _pick(n, target, unit=V7X_LANES):
    best = None
    for t in range(unit, min(n, target) + 1, unit):
        if n % t == 0:
            best = t
    return n if best is None else best


_DIMS = {
    "nn": (((1,), (0,)), ((), ())),
    "nt": (((1,), (1,)), ((), ())),
    "tn": (((0,), (0,)), ((), ())),
}


ANY_SPEC = pl.BlockSpec(memory_space=pl.ANY)


def _matmul(a, b, mode, out_dtype, name, tm=1024, tn=512, tk=1408, deps=(), b_shift=0):
    n_deps = len(deps)
    if mode == "nn":
        (m, k), (k2, n) = a.shape, b.shape
    elif mode == "nt":
        (m, k), (n, k2) = a.shape, b.shape
    else:
        (k, m), (k2, n) = a.shape, b.shape
    assert k == k2, (a.shape, b.shape, mode)
    tm, tn, tk = _pick(m, tm), _pick(n, tn), _pick(k, tk)
    nk = k // tk
    dims = _DIMS[mode]

    def body(a_ref, b_ref, *rest):
        o_ref, scratch = rest[n_deps], rest[n_deps + 1 :]
        p = lax.dot_general(a_ref[...], b_ref[...], dims, preferred_element_type=F32)
        if nk == 1:
            o_ref[...] = p.astype(o_ref.dtype)
        else:
            acc = scratch[0]
            kk = pl.program_id(2)

            @pl.when(kk == 0)
            def _():
                acc[...] = p

            @pl.when(kk > 0)
            def _():
                acc[...] += p

            @pl.when(kk == nk - 1)
            def _():
                o_ref[...] = acc[...].astype(o_ref.dtype)

    if mode == "nn":
        a_spec = pl.BlockSpec((tm, tk), lambda i, j, kk: (i, kk))
        b_spec = pl.BlockSpec((tk, tn), lambda i, j, kk: (kk, j))
    elif mode == "nt":
        a_spec = pl.BlockSpec((tm, tk), lambda i, j, kk: (i, kk))
        b_spec = pl.BlockSpec((tn, tk), lambda i, j, kk: ((j + b_shift) % (n // tn), kk))
    else:
        a_spec = pl.BlockSpec((tk, tm), lambda i, j, kk: (kk, i))
        b_spec = pl.BlockSpec((tk, tn), lambda i, j, kk: (kk, j))
    return pl.pallas_call(
        body,
        name=name,
        grid=(m // tm, n // tn, nk),
        in_specs=[a_spec, b_spec] + [ANY_SPEC] * n_deps,
        out_specs=pl.BlockSpec((tm, tn), lambda i, j, kk: (i, j)),
        out_shape=jax.ShapeDtypeStruct((m, n), out_dtype),
        scratch_shapes=[pltpu.VMEM((tm, tn), F32)] if nk > 1 else [],
        compiler_params=_cparams(dimension_semantics=("parallel", "parallel", "arbitrary")),
    )(a, b, *deps)


def _rowwise(fn, name, params, tiles, outs, accs=(), ts=256, with_index=False, deps=()):
    norm = []
    for t in tiles:
        if not isinstance(t, tuple):
            t = (t, t.shape[1], 0)
        norm.append(t if len(t) == 4 else (*t, None))
    s = norm[0][0].shape[0]
    ts = min(ts, s)
    assert s % ts == 0 and ts % V7X_SUBLANES == 0
    steps = s // ts
    halo_blocks = ts // V7X_SUBLANES
    n_p, n_t, n_o = len(params), len(norm), len(outs)

    def body(*refs):
        i = pl.program_id(0)
        vals = [r[...] for r in refs[: n_p + n_t]]
        res = fn(i, steps, *vals) if with_index else fn(*vals)
        if not isinstance(res, (tuple, list)):
            res = (res,)
        first_out = n_p + n_t + len(deps)
        o_refs = refs[first_out : first_out + n_o]
        a_refs = refs[first_out + n_o :]
        for r, v in zip(o_refs, res[:n_o]):
            r[...] = v.astype(r.dtype)
        for r, v in zip(a_refs, res[n_o:]):
            _accumulate(r, v, i)

    in_specs = [pl.BlockSpec(p.shape, lambda i: (0, 0)) for p in params]
    for arr, w, cb, halo in norm:
        if halo is None:
            in_specs.append(pl.BlockSpec((ts, w), lambda i, cb=cb: (i, cb)))
        elif halo == "prev":
            in_specs.append(
                pl.BlockSpec((V7X_SUBLANES, w), lambda i, cb=cb: (jnp.maximum(i * halo_blocks - 1, 0), cb))
            )
        else:
            last = s // V7X_SUBLANES - 1
            in_specs.append(
                pl.BlockSpec((V7X_SUBLANES, w), lambda i, cb=cb: (jnp.minimum((i + 1) * halo_blocks, last), cb))
            )
    in_specs += [ANY_SPEC] * len(deps)
    out_specs = [pl.BlockSpec((ts, w), lambda i: (i, 0)) for w, _ in outs]
    out_specs += [pl.BlockSpec(shape, lambda i: (0, 0)) for shape in accs]
    out_shape = [jax.ShapeDtypeStruct((s, w), dt) for w, dt in outs]
    out_shape += [jax.ShapeDtypeStruct(shape, F32) for shape in accs]
    res = pl.pallas_call(
        body,
        name=name,
        grid=(steps,),
        in_specs=in_specs,
        out_specs=out_specs,
        out_shape=out_shape,
        compiler_params=_cparams(dimension_semantics=("arbitrary",)),
    )(*params, *[t[0] for t in norm], *deps)
    return res


def _accumulate(ref, val, step):
    @pl.when(step == 0)
    def _():
        ref[...] = val

    @pl.when(step > 0)
    def _():
        ref[...] += val


def _sigmoid(z):
    return jax.nn.sigmoid(z)


def _silu(z):
    return z * _sigmoid(z)


def _gelu(z):
    return 0.5 * z * (1.0 + jnp.tanh(0.7978845608028654 * (z + 0.044715 * (z * z * z))))


def _pre_fn(g, shift, scale, x):
    r = lax.rsqrt(jnp.mean(x * x, axis=-1, keepdims=True) + EPS)
    return ((x * r) * g) * (1.0 + scale) + shift


def _post_fn(res_w, g, gate, f, x):
    r = lax.rsqrt(jnp.mean(f * f, axis=-1, keepdims=True) + EPS)
    return x + (res_w * gate) * ((f * r) * g)


def _gates_fn(ba, bx, lam, pre, xc):
    ra = _sigmoid(pre[:, :LRU_WIDTH] + ba)
    ia = _sigmoid(pre[:, LRU_WIDTH:] + bx)
    softplus = jnp.maximum(-lam, 0.0) + jnp.log1p(jnp.exp(-jnp.abs(lam)))
    log_a = (-LRU_C) * ra * softplus
    a = jnp.exp(log_a)
    mult = jnp.sqrt(-jnp.tanh(log_a) * (a * a + 1.0))
    return a, mult * (ia * xc)


def _recin_fn(hs, yr):
    return hs * _gelu(yr)


def _merge_fn(att, rec, g_att, g_rec):
    return _sigmoid(g_att) * att + _sigmoid(g_rec) * rec


def _rowsum(v):
    return jnp.sum(v, axis=0, keepdims=True)


FFN_TM = 512
FFN_TF = 1408


def _glu_fn(g, u):
    return _silu(g) * u


FUSE_TM = 256
DW_TK = 4096
ROW_SPEC2 = pl.BlockSpec((1, D_MODEL), lambda i, j: (0, 0))
ROW_SPEC1 = pl.BlockSpec((1, D_MODEL), lambda i: (0, 0))
SUMS_SPEC1 = pl.BlockSpec((V7X_SUBLANES, D_MODEL), lambda i: (0, 0))
SUMS_SPEC2 = pl.BlockSpec((V7X_SUBLANES, D_MODEL), lambda i, j: (0, 0))
SUMS_SHAPE = jax.ShapeDtypeStruct((V7X_SUBLANES, D_MODEL), F32)


def _sum_rows(*rows):
    pad = jnp.zeros((V7X_SUBLANES - len(rows), rows[0].shape[1]), F32)
    return jnp.concatenate([*rows, pad], axis=0)


def _pre_up(x, g, shift, scale, w_gu_t, name, deps=()):
    s = x.shape[0]
    tm = min(FFN_TM, s)
    nf = D_FF // FFN_TF
    nd = len(deps)

    def body(x_ref, g_ref, sh_ref, sc_ref, wg_ref, wu_ref, *rest):
        h_ref, a_ref, gg_ref, u_ref, h_s = rest[nd:]

        @pl.when(pl.program_id(1) == 0)
        def _():
            h = _pre_fn(g_ref[...], sh_ref[...], sc_ref[...], x_ref[...]).astype(BF16)
            h_s[...] = h
            h_ref[...] = h

        hv = h_s[...]
        gv = lax.dot_general(hv, wg_ref[...], _DIMS["nt"], preferred_element_type=F32)
        uv = lax.dot_general(hv, wu_ref[...], _DIMS["nt"], preferred_element_type=F32)
        a_ref[...] = _glu_fn(gv, uv).astype(a_ref.dtype)
        gg_ref[...] = gv.astype(gg_ref.dtype)
        u_ref[...] = uv.astype(u_ref.dtype)

    rows = pl.BlockSpec((tm, D_MODEL), lambda i, j: (i, 0))
    out = pl.BlockSpec((tm, FFN_TF), lambda i, j: (i, j))
    return pl.pallas_call(
        body,
        name=name,
        grid=(s // tm, nf),
        in_specs=[rows, ROW_SPEC2, ROW_SPEC2, ROW_SPEC2,
                  pl.BlockSpec((FFN_TF, D_MODEL), lambda i, j: (j, 0)),
                  pl.BlockSpec((FFN_TF, D_MODEL), lambda i, j: (nf + j, 0))] + [ANY_SPEC] * nd,
        out_specs=[rows, out, out, out],
        out_shape=[jax.ShapeDtypeStruct((s, D_MODEL), BF16)] + [jax.ShapeDtypeStruct((s, D_FF), BF16)] * 3,
        scratch_shapes=[pltpu.VMEM((tm, D_MODEL), BF16)],
        compiler_params=_cparams(dimension_semantics=("parallel", "arbitrary")),
    )(x, g, shift, scale, w_gu_t, w_gu_t, *deps)


def _pre_matmul(x, g, shift, scale, w_t, name, b_shift=0, tn=512):
    s = x.shape[0]
    n = w_t.shape[0]
    tm = min(2 * FFN_TM, s)

    def body(x_ref, g_ref, sh_ref, sc_ref, w_ref, h_ref, o_ref, h_s):
        @pl.when(pl.program_id(1) == 0)
        def _():
            h = _pre_fn(g_ref[...], sh_ref[...], sc_ref[...], x_ref[...]).astype(BF16)
            h_s[...] = h
            h_ref[...] = h

        o_ref[...] = lax.dot_general(h_s[...], w_ref[...], _DIMS["nt"], preferred_element_type=F32)

    rows = pl.BlockSpec((tm, D_MODEL), lambda i, j: (i, 0))
    return pl.pallas_call(
        body,
        name=name,
        grid=(s // tm, n // tn),
        in_specs=[rows, ROW_SPEC2, ROW_SPEC2, ROW_SPEC2,
                  pl.BlockSpec((tn, D_MODEL), lambda i, j: ((j + b_shift) % (n // tn), 0))],
        out_specs=[rows, pl.BlockSpec((tm, tn), lambda i, j: (i, j))],
        out_shape=[jax.ShapeDtypeStruct((s, D_MODEL), BF16), jax.ShapeDtypeStruct((s, n), F32)],
        scratch_shapes=[pltpu.VMEM((tm, D_MODEL), BF16)],
        compiler_params=_cparams(dimension_semantics=("parallel", "arbitrary")),
    )(x, g, shift, scale, w_t)


def _matmul_post(a, w, x, g_post, gate, res_w, name, target=None):
    s, k = a.shape
    tm = min(FFN_TM, s)
    extra = [] if target is None else [target]

    def body(a_ref, w_ref, x_ref, g_ref, gate_ref, *rest):
        f = jnp.dot(a_ref[...], w_ref[...], preferred_element_type=F32)
        y = _post_fn(res_w, g_ref[...], gate_ref[...], f, x_ref[...])
        if target is None:
            f_ref, y_ref = rest
            y_ref[...] = y
        else:
            t_ref, f_ref, dy_ref, sq_ref = rest
            diff = y - t_ref[...]
            dy_ref[...] = diff * (1.0 / D_MODEL)
            _accumulate(sq_ref, _rowsum(diff * diff), pl.program_id(0))
        f_ref[...] = f

    rows = pl.BlockSpec((tm, D_MODEL), lambda i: (i, 0))
    out_specs, out_shape = [rows, rows], [jax.ShapeDtypeStruct((s, D_MODEL), F32)] * 2
    if target is not None:
        out_specs.append(ROW_SPEC1)
        out_shape.append(jax.ShapeDtypeStruct((1, D_MODEL), F32))
    return pl.pallas_call(
        body,
        name=name,
        grid=(s // tm,),
        in_specs=[pl.BlockSpec((tm, k), lambda i: (i, 0)), pl.BlockSpec((k, D_MODEL), lambda i: (0, 0)), rows,
                  ROW_SPEC1, ROW_SPEC1] + [rows] * len(extra),
        out_specs=out_specs,
        out_shape=out_shape,
        compiler_params=_cparams(dimension_semantics=("arbitrary",)),
    )(a, w, x, g_post, gate, *extra)


def _merge_matmul_post(att, rec, proj, w, x, g_post, gate, res_w, name):
    s = att.shape[0]
    tm = min(FUSE_TM, s)

    def body(att_ref, rec_ref, ga_ref, gr_ref, w_ref, x_ref, g_ref, gate_ref, m_ref, f_ref, y_ref):
        merged = _merge_fn(att_ref[...], rec_ref[...], ga_ref[...], gr_ref[...]).astype(BF16)
        m_ref[...] = merged
        f = jnp.dot(merged, w_ref[...], preferred_element_type=F32)
        f_ref[...] = f
        y_ref[...] = _post_fn(res_w, g_ref[...], gate_ref[...], f, x_ref[...])

    rows = pl.BlockSpec((tm, D_MODEL), lambda i: (i, 0))
    return pl.pallas_call(
        body,
        name=name,
        grid=(s // tm,),
        in_specs=[rows, rows, pl.BlockSpec((tm, D_MODEL), lambda i: (i, 2)), pl.BlockSpec((tm, D_MODEL), lambda i: (i, 3)),
                  pl.BlockSpec(w.shape, lambda i: (0, 0)), rows, ROW_SPEC1, ROW_SPEC1],
        out_specs=[rows, rows, rows],
        out_shape=[jax.ShapeDtypeStruct((s, D_MODEL), BF16)] + [jax.ShapeDtypeStruct((s, D_MODEL), F32)] * 2,
        compiler_params=_cparams(dimension_semantics=("parallel",)),
    )(att, rec, proj, proj, w, x, g_post, gate)


def _post_bwd_merge_bwd(f, dy, g_post, gate, res_w, w, att, rec, proj, name):
    s = f.shape[0]
    tm = min(FUSE_TM, s)

    def body(f_ref, dy_ref, gp_ref, gate_ref, w_ref, att_ref, rec_ref, ga_ref, gr_ref,
             df_ref, datt_ref, drec_ref, dga_ref, dgr_ref, sums_ref):
        i = pl.program_id(0)
        dgp, dgate, df = _post_vjp(res_w, gp_ref[...], gate_ref[...], f_ref[...], dy_ref[...])
        dfb = df.astype(BF16)
        df_ref[...] = dfb
        _accumulate(sums_ref, _sum_rows(dgp, dgate), i)
        dmerged = lax.dot_general(dfb, w_ref[...], _DIMS["nt"], preferred_element_type=F32)
        _, vjp = jax.vjp(_merge_fn, att_ref[...], rec_ref[...], ga_ref[...], gr_ref[...])
        for ref, val in zip((datt_ref, drec_ref, dga_ref, dgr_ref), vjp(dmerged)):
            ref[...] = val.astype(ref.dtype)

    rows = pl.BlockSpec((tm, D_MODEL), lambda i: (i, 0))
    return pl.pallas_call(
        body,
        name=name,
        grid=(s // tm,),
        in_specs=[rows, rows, ROW_SPEC1, ROW_SPEC1, pl.BlockSpec(w.shape, lambda i: (0, 0)), rows, rows,
                  pl.BlockSpec((tm, D_MODEL), lambda i: (i, 2)), pl.BlockSpec((tm, D_MODEL), lambda i: (i, 3))],
        out_specs=[rows] * 5 + [SUMS_SPEC1],
        out_shape=[jax.ShapeDtypeStruct((s, D_MODEL), BF16)] * 5 + [SUMS_SHAPE],
        compiler_params=_cparams(dimension_semantics=("arbitrary",)),
    )(f, dy, g_post, gate, w, att, rec, proj, proj)


def _matmul_recin_bwd(drec, w, hs, proj, name, deps=()):
    s = drec.shape[0]
    tm = min(FUSE_TM, s)
    nd = len(deps)

    def body(d_ref, w_ref, hs_ref, yr_ref, *rest):
        dhs_ref, dyr_ref = rest[nd:]
        d = lax.dot_general(d_ref[...], w_ref[...], _DIMS["nt"], preferred_element_type=F32)
        _, vjp = jax.vjp(_recin_fn, hs_ref[...], yr_ref[...])
        dhs, dyr = vjp(d)
        dhs_ref[...] = dhs
        dyr_ref[...] = dyr.astype(dyr_ref.dtype)

    rows = pl.BlockSpec((tm, D_MODEL), lambda i: (i, 0))
    return pl.pallas_call(
        body,
        name=name,
        grid=(s // tm,),
        in_specs=[rows, pl.BlockSpec(w.shape, lambda i: (0, 0)), rows,
                  pl.BlockSpec((tm, D_MODEL), lambda i: (i, 1))] + [ANY_SPEC] * nd,
        out_specs=[rows, rows],
        out_shape=[jax.ShapeDtypeStruct((s, D_MODEL), F32), jax.ShapeDtypeStruct((s, D_MODEL), BF16)],
        compiler_params=_cparams(dimension_semantics=("parallel",)),
    )(drec, w, hs, proj, *deps)


def _post_vjp(res_w, g, gate, f, dy):
    _, vjp = jax.vjp(lambda g, gate, f: _post_fn(res_w, g, gate, f, 0.0), g, gate, f)
    return vjp(dy)


def _post_bwd_up_bwd(f, dy, g_post, gate, res_w, w_down, g, u, name, deps=()):
    s = f.shape[0]
    tm = min(FFN_TM, s)
    nd = len(deps)

    def body(f_ref, dy_ref, gp_ref, gate_ref, wd_ref, g_ref, u_ref, *rest):
        df_ref, dgu_ref, sums_ref, df_s = rest[nd:]
        i = pl.program_id(0)

        @pl.when(pl.program_id(1) == 0)
        def _():
            dgp, dgate, df = _post_vjp(res_w, gp_ref[...], gate_ref[...], f_ref[...], dy_ref[...])
            df_s[...] = df.astype(BF16)
            df_ref[...] = df_s[...]
            _accumulate(sums_ref, _sum_rows(dgp, dgate), i)

        da = lax.dot_general(df_s[...], wd_ref[...], _DIMS["nt"], preferred_element_type=F32)
        _, vjp = jax.vjp(_glu_fn, g_ref[...].astype(F32), u_ref[...].astype(F32))
        dg, du = vjp(da)
        dgu_ref[0] = dg.astype(dgu_ref.dtype)
        dgu_ref[1] = du.astype(dgu_ref.dtype)

    rows = pl.BlockSpec((tm, D_MODEL), lambda i, j: (i, 0))
    blk = pl.BlockSpec((tm, FFN_TF), lambda i, j: (i, j))
    return pl.pallas_call(
        body,
        name=name,
        grid=(s // tm, D_FF // FFN_TF),
        in_specs=[rows, rows, ROW_SPEC2, ROW_SPEC2, pl.BlockSpec((FFN_TF, D_MODEL), lambda i, j: (j, 0)), blk,
                  blk] + [ANY_SPEC] * nd,
        out_specs=[rows, pl.BlockSpec((2, tm, FFN_TF), lambda i, j: (0, i, j)), SUMS_SPEC2],
        out_shape=[jax.ShapeDtypeStruct((s, D_MODEL), BF16), jax.ShapeDtypeStruct((2, s, D_FF), BF16), SUMS_SHAPE],
        scratch_shapes=[pltpu.VMEM((tm, D_MODEL), BF16)],
        compiler_params=_cparams(dimension_semantics=("arbitrary", "arbitrary")),
    )(f, dy, g_post, gate, w_down, g, u, *deps)


def _matmul_pre_bwd(parts, w_t, x, dres, g, shift, scale, name, deps=()):
    s = x.shape[0]
    na, nd = len(parts), len(deps)
    ranges = [p[3] for p in parts]

    def body(*refs):
        a_refs = refs[:na]
        w_ref, x_ref, dres_ref, g_ref, sh_ref, sc_ref = refs[na : na + 6]
        dx_ref, sums_ref = refs[na + 6 + nd :]
        i = pl.program_id(0)
        dh = None
        for a_ref, (r0, r1) in zip(a_refs, ranges):
            p = jnp.dot(a_ref[...], w_ref[r0:r1, :], preferred_element_type=F32)
            dh = p if dh is None else dh + p
        _, vjp = jax.vjp(_pre_fn, g_ref[...], sh_ref[...], sc_ref[...], x_ref[...])
        dg, dsh, dsc, dx = vjp(dh)
        dx_ref[...] = dx + dres_ref[...]
        _accumulate(sums_ref, _sum_rows(dg, dsh, dsc), i)

    tm = parts[0][1][-2]
    rows = pl.BlockSpec((tm, D_MODEL), lambda i: (i, 0))
    return pl.pallas_call(
        body,
        name=name,
        grid=(s // tm,),
        in_specs=[pl.BlockSpec(p[1], p[2]) for p in parts]
        + [pl.BlockSpec(w_t.shape, lambda i: (0, 0)), rows, rows, ROW_SPEC1, ROW_SPEC1, ROW_SPEC1]
        + [ANY_SPEC] * nd,
        out_specs=[rows, SUMS_SPEC1],
        out_shape=[jax.ShapeDtypeStruct((s, D_MODEL), F32), SUMS_SHAPE],
        compiler_params=_cparams(dimension_semantics=("arbitrary",)),
    )(*[p[0] for p in parts], w_t, x, dres, g, shift, scale, *deps)


def _dw_gu(dgu, h, name, deps=(), tk=DW_TK):
    s = h.shape[0]
    tk = min(tk, s)
    nk = s // tk
    half = D_FF // FFN_TF

    def body(a_ref, b_ref, *rest):
        o_ref = rest[len(deps)]
        kk = pl.program_id(1)
        p = lax.dot_general(a_ref[...], b_ref[...], _DIMS["tn"], preferred_element_type=F32)
        if nk == 1:
            o_ref[...] = p.astype(o_ref.dtype)
            return
        acc = rest[len(deps) + 1]

        @pl.when(kk == 0)
        def _():
            acc[...] = p

        @pl.when(kk > 0)
        def _():
            acc[...] += p

        @pl.when(kk == nk - 1)
        def _():
            o_ref[...] = acc[...].astype(o_ref.dtype)

    return pl.pallas_call(
        body,
        name=name,
        grid=(2 * half, nk),
        in_specs=[pl.BlockSpec((None, tk, FFN_TF), lambda i, kk: (i // half, kk, i % half)),
                  pl.BlockSpec((tk, D_MODEL), lambda i, kk: (kk, 0))] + [ANY_SPEC] * len(deps),
        out_specs=pl.BlockSpec((FFN_TF, D_MODEL), lambda i, kk: (i, 0)),
        out_shape=jax.ShapeDtypeStruct((2 * D_FF, D_MODEL), BF16),
        scratch_shapes=[pltpu.VMEM((FFN_TF, D_MODEL), F32)] if nk > 1 else [],
        compiler_params=_cparams(dimension_semantics=("parallel", "arbitrary")),
    )(dgu, h, *deps)


def _shift_down(ext, j, rows):
    return pltpu.roll(ext, j, 0)[V7X_SUBLANES : V7X_SUBLANES + rows]


def _shift_up(ext, j, rows):
    return pltpu.roll(ext, ext.shape[0] - j, 0)[:rows] if j else ext[:rows]


LRU_SLAB = 256
N_SLABS = LRU_WIDTH // LRU_SLAB


def _slab_weights(wa, wx):
    per = LRU_SLAB // LRU_BLOCK
    eye = jnp.eye(per, dtype=wa.dtype)

    def diag(w):
        w4 = w.reshape(N_SLABS, per, LRU_BLOCK, LRU_BLOCK)
        return jnp.einsum("sbkj,bc->sbkcj", w4, eye).reshape(N_SLABS, LRU_SLAB, LRU_SLAB)

    return jnp.concatenate([diag(wa), diag(wx)], axis=2).reshape(LRU_WIDTH, 2 * LRU_SLAB).astype(BF16)


def _slab_cols(v, s):
    lo = s * LRU_SLAB
    return jnp.concatenate([v[:, lo : lo + LRU_SLAB], v[:, LRU_WIDTH + lo : LRU_WIDTH + lo + LRU_SLAB]], axis=1)


def _lru_front(proj, w8, b, w_slab, ba, bx, lam, name):
    def fn(i, steps, w8, b, w_slab, ba, bx, lam, x, halo):
        halo = jnp.where(i > 0, halo, 0.0)
        ext = jnp.concatenate([halo, x], axis=0)
        xc = b + w8[3:4] * x
        for j in (1, 2, 3):
            xc = xc + w8[3 - j : 4 - j] * _shift_down(ext, j, x.shape[0])
        xcb = xc.astype(BF16)
        prods = []
        for s in range(N_SLABS):
            rows = slice(s * LRU_SLAB, (s + 1) * LRU_SLAB)
            prods.append(jnp.dot(xcb[:, rows], w_slab[rows], preferred_element_type=F32))
        pre = jnp.concatenate([p[:, :LRU_SLAB] for p in prods] + [p[:, LRU_SLAB:] for p in prods], axis=1)
        a, u = _gates_fn(ba, bx, lam, pre, xc)
        return xc, pre, a, u

    tiles = [(proj, LRU_WIDTH, 0), (proj, LRU_WIDTH, 0, "prev")]
    outs = [(LRU_WIDTH, F32), (2 * LRU_WIDTH, F32), (LRU_WIDTH, F32), (LRU_WIDTH, F32)]
    return _rowwise(fn, name, [w8, b, w_slab, ba, bx, lam], tiles, outs, with_index=True)


def _lru_back(pre, xc, w_slab, ba, bx, lam, g, h_prev, name, deps=()):
    def fn(w_slab, ba, bx, lam, pre, xc, g, h_prev):
        _, vjp = jax.vjp(_gates_fn, ba, bx, lam, pre, xc)
        dba, dbx, dlam, dpre, dxc = vjp((g * h_prev, g))
        dpre = dpre.astype(BF16)
        back = []
        for s in range(N_SLABS):
            rows = slice(s * LRU_SLAB, (s + 1) * LRU_SLAB)
            back.append(lax.dot_general(_slab_cols(dpre, s), w_slab[rows], _DIMS["nt"], preferred_element_type=F32))
        return dpre, dxc + jnp.concatenate(back, axis=1), _sum_rows(dba, dbx, dlam)

    return _rowwise(fn, name, [w_slab, ba, bx, lam], [pre, xc, g, h_prev],
                    [(2 * LRU_WIDTH, BF16), (LRU_WIDTH, F32)], [(V7X_SUBLANES, LRU_WIDTH)], deps=deps)


def _lru_dw(xc, dpre, name):
    s = xc.shape[0]
    ts = min(512, s)
    steps = s // ts
    per = LRU_SLAB // LRU_BLOCK

    def body(x_ref, d_ref, o_ref, acc):
        i = pl.program_id(0)
        xcb = x_ref[...].astype(BF16)
        d = d_ref[...]
        for sl in range(N_SLABS):
            rows = slice(sl * LRU_SLAB, (sl + 1) * LRU_SLAB)
            p = lax.dot_general(xcb[:, rows], _slab_cols(d, sl), _DIMS["tn"], preferred_element_type=F32)

            @pl.when(i == 0)
            def _(p=p, rows=rows):
                acc[rows, :] = p

            @pl.when(i > 0)
            def _(p=p, rows=rows):
                acc[rows, :] += p

        @pl.when(i == steps - 1)
        def _():
            for half in range(2):
                for n in range(LRU_BLOCKS):
                    r0 = n * LRU_BLOCK
                    c0 = half * LRU_SLAB + (n % per) * LRU_BLOCK
                    o_ref[half, r0 : r0 + LRU_BLOCK, :] = acc[r0 : r0 + LRU_BLOCK, c0 : c0 + LRU_BLOCK]

    return pl.pallas_call(
        body,
        name=name,
        grid=(steps,),
        in_specs=[pl.BlockSpec((ts, LRU_WIDTH), lambda i: (i, 0)), pl.BlockSpec((ts, 2 * LRU_WIDTH), lambda i: (i, 0))],
        out_specs=pl.BlockSpec((2, LRU_WIDTH, LRU_BLOCK), lambda i: (0, 0, 0)),
        out_shape=jax.ShapeDtypeStruct((2, LRU_WIDTH, LRU_BLOCK), F32),
        scratch_shapes=[pltpu.VMEM((LRU_WIDTH, 2 * LRU_SLAB), F32)],
        compiler_params=_cparams(dimension_semantics=("arbitrary",)),
    )(xc, dpre)


def _conv_bwd(proj, w8, d1, name):
    def fn(i, steps, w8, x, halo, d, d1n):
        rows = x.shape[0]
        dn = jnp.where(i < steps - 1, d1n, 0.0)
        halo = jnp.where(i > 0, halo, 0.0)
        dext = jnp.concatenate([d, dn], axis=0)
        xext = jnp.concatenate([halo, x], axis=0)
        dx = w8[3:4] * d
        dw = [None] * 4
        dw[3] = _rowsum(d * x)
        for k in (1, 2, 3):
            dx = dx + w8[3 - k : 4 - k] * _shift_up(dext, k, rows)
            dw[3 - k] = _rowsum(d * _shift_down(xext, k, rows))
        return dx, _sum_rows(*dw, _rowsum(d))

    tiles = [(proj, LRU_WIDTH, 0), (proj, LRU_WIDTH, 0, "prev"), d1, (d1, LRU_WIDTH, 0, "next")]
    return _rowwise(fn, name, [w8], tiles, [(LRU_WIDTH, BF16)], [(V7X_SUBLANES, LRU_WIDTH)], with_index=True)


SCAN_ROWS = 512


def _block_scan(a, b, row, reverse):
    for d in (1, 2, 4):
        if reverse:
            shift, keep = V7X_SUBLANES - d, row < V7X_SUBLANES - d
        else:
            shift, keep = d, row >= d
        a_s = pltpu.roll(a, shift, 0)
        b_s = pltpu.roll(b, shift, 0)
        b = jnp.where(keep, a * b_s + b, b)
        a = jnp.where(keep, a * a_s, a)
    return a, b


def _scan_fwd(a, u, proj, name):
    s, w = a.shape
    ts = min(SCAN_ROWS, s)
    sub = ts // V7X_SUBLANES

    def body(a_ref, u_ref, yr_ref, h_ref, hp_ref, rec_ref, carry):
        @pl.when(pl.program_id(0) == 0)
        def _():
            carry[...] = jnp.zeros_like(carry)

        row = lax.broadcasted_iota(jnp.int32, (V7X_SUBLANES, w), 0)

        def step(j, c):
            rows = pl.ds(pl.multiple_of(j * V7X_SUBLANES, V7X_SUBLANES), V7X_SUBLANES)
            pa, pb = _block_scan(a_ref[rows, :], u_ref[rows, :], row, False)
            h = pb + pa * c
            h_ref[rows, :] = h
            hp_ref[rows, :] = jnp.where(row >= 1, pltpu.roll(h, 1, 0), c)
            return jnp.broadcast_to(h[V7X_SUBLANES - 1 :], (V7X_SUBLANES, w))

        carry[...] = lax.fori_loop(0, sub, step, carry[...])
        rec_ref[...] = _recin_fn(h_ref[...], yr_ref[...]).astype(rec_ref.dtype)

    spec = pl.BlockSpec((ts, w), lambda i: (i, 0))
    return pl.pallas_call(
        body,
        name=name,
        grid=(s // ts,),
        in_specs=[spec, spec, pl.BlockSpec((ts, w), lambda i: (i, 1))],
        out_specs=[spec, spec, spec],
        out_shape=[jax.ShapeDtypeStruct((s, w), F32)] * 2 + [jax.ShapeDtypeStruct((s, w), BF16)],
        scratch_shapes=[pltpu.VMEM((V7X_SUBLANES, w), F32)],
        compiler_params=_cparams(dimension_semantics=("arbitrary",)),
    )(a, u, proj)


def _scan_bwd(a, dh, name):
    s, w = a.shape
    ts = min(SCAN_ROWS, s)
    sub = ts // V7X_SUBLANES
    steps = s // ts

    def body(a_ref, d_ref, g_ref, carry):
        @pl.when(pl.program_id(0) == 0)
        def _():
            carry[...] = jnp.zeros_like(carry)

        row = lax.broadcasted_iota(jnp.int32, (V7X_SUBLANES, w), 0)

        def step(jj, c):
            j = sub - 1 - jj
            rows = pl.ds(pl.multiple_of(j * V7X_SUBLANES, V7X_SUBLANES), V7X_SUBLANES)
            av, dv = a_ref[rows, :], d_ref[rows, :]
            pa, pb = _block_scan(av, av * dv, row, True)
            big = pb + pa * c
            g_ref[rows, :] = dv + jnp.where(row < V7X_SUBLANES - 1, pltpu.roll(big, V7X_SUBLANES - 1, 0), c)
            return jnp.broadcast_to(big[:1], (V7X_SUBLANES, w))

        carry[...] = lax.fori_loop(0, sub, step, carry[...])

    spec = pl.BlockSpec((ts, w), lambda i: (steps - 1 - i, 0))
    return pl.pallas_call(
        body,
        name=name,
        grid=(steps,),
        in_specs=[spec, spec],
        out_specs=spec,
        out_shape=jax.ShapeDtypeStruct((s, w), F32),
        scratch_shapes=[pltpu.VMEM((V7X_SUBLANES, w), F32)],
        compiler_params=_cparams(dimension_semantics=("arbitrary",)),
    )(a, dh)


SKEW = 4 * ATT_TQ


def _skew_onehot():
    t = np.arange(SKEW)
    diag = np.where(t < 3 * ATT_TQ, -t, SKEW - t)
    idx = np.clip(diag + LEFT_CHUNKS * CHUNK, -MAX_REL, MAX_REL) + MAX_REL
    hit = (idx[:, None] == np.arange(2 * MAX_REL + 1)[None, :]) & (t[:, None] != 3 * ATT_TQ)
    return hit.astype(np.float32)


def _bias_tile(rel_bias, name):
    per_t = jnp.dot(rel_bias, jnp.asarray(_skew_onehot()).T, precision=lax.Precision.HIGHEST)
    win = 3 * ATT_TQ

    def body(t_ref, o_ref):
        tile = pltpu.roll(jnp.broadcast_to(t_ref[0], (ATT_TQ, SKEW)), 0, 1, stride=1, stride_axis=0)[:, :win]
        qc = lax.broadcasted_iota(jnp.int32, (ATT_TQ, win), 0) // CHUNK
        kpos = lax.broadcasted_iota(jnp.int32, (ATT_TQ, win), 1)
        band = (kpos // CHUNK >= qc) & (kpos // CHUNK <= qc + LEFT_CHUNKS)
        for v in range(3):
            o_ref[v, 0] = jnp.where(band & (kpos >= (2 - v) * ATT_TQ), tile, NEG)

    return pl.pallas_call(
        body,
        name=name,
        grid=(ATT_HEADS,),
        in_specs=[pl.BlockSpec((1, 1, SKEW), lambda h: (h, 0, 0))],
        out_specs=pl.BlockSpec((3, 1, ATT_TQ, win), lambda h: (0, h, 0, 0)),
        out_shape=jax.ShapeDtypeStruct((3, ATT_HEADS, ATT_TQ, win), F32),
        compiler_params=_cparams(dimension_semantics=("parallel",)),
    )(per_t.reshape(ATT_HEADS, 1, SKEW))


def _bias_grad(dbias, name):
    win = 3 * ATT_TQ

    def body(d_ref, o_ref):
        d = jnp.concatenate([d_ref[0], jnp.zeros((ATT_TQ, SKEW - win), F32)], axis=1)
        r = lax.broadcasted_iota(jnp.int32, (ATT_TQ, ATT_TQ), 0)
        c = lax.broadcasted_iota(jnp.int32, (ATT_TQ, ATT_TQ), 1)
        flip = (r + c == ATT_TQ - 1).astype(F32)
        d = jnp.dot(flip, d, preferred_element_type=F32, precision=lax.Precision.HIGHEST)
        o_ref[0] = jnp.sum(pltpu.roll(d, SKEW - (ATT_TQ - 1), 1, stride=1, stride_axis=0), axis=0, keepdims=True)

    per_t = pl.pallas_call(
        body,
        name=name,
        grid=(ATT_HEADS,),
        in_specs=[pl.BlockSpec((1, ATT_TQ, win), lambda h: (h, 0, 0))],
        out_specs=pl.BlockSpec((1, 1, SKEW), lambda h: (h, 0, 0)),
        out_shape=jax.ShapeDtypeStruct((ATT_HEADS, 1, SKEW), F32),
        compiler_params=_cparams(dimension_semantics=("parallel",)),
    )(dbias)
    return jnp.dot(per_t.reshape(ATT_HEADS, SKEW), jnp.asarray(_skew_onehot()), precision=lax.Precision.HIGHEST)


ATT_STEP_HEADS = ATT_HEADS
ATT_HALF = ATT_TQ // 2
ATT_HALF_KEYS = ATT_HALF + LEFT_CHUNKS * CHUNK
ATT_STEP_COLS = ATT_STEP_HEADS * ATT_HEAD_DIM


def _attn_specs(nt):
    qb, kb, vb = OFF_Q // ATT_STEP_COLS, OFF_K // ATT_STEP_COLS, OFF_V // ATT_STEP_COLS
    blk = (ATT_TQ, ATT_STEP_COLS)

    def qmap(base):
        return lambda hp, m: (jnp.minimum(m, nt - 1), base + hp)

    def wmap(base, back):
        return lambda hp, m: (jnp.clip(m - back, 0, nt - 1), base + hp)

    specs = [pl.BlockSpec(blk, qmap(qb))]
    specs += [pl.BlockSpec(blk, wmap(kb, back)) for back in (2, 1, 0)]
    specs += [pl.BlockSpec(blk, wmap(vb, back)) for back in (2, 1, 0)]
    return specs


ATT_SCALE = ATT_HEAD_DIM**-0.5


def _attn_exp(qh, kh, bias):
    s = lax.dot_general(qh, kh, _DIMS["nt"], preferred_element_type=F32) + bias
    e = jnp.exp(s - jnp.max(s, axis=-1, keepdims=True))
    return e, jnp.sum(e, axis=-1, keepdims=True)


def _attn_window(k0, k1, k2, v0, v1, v2):
    k = jnp.concatenate([k0[...], k1[...], k2[...]], axis=0).astype(BF16)
    v = jnp.concatenate([v0[...], v1[...], v2[...]], axis=0).astype(BF16)
    return k, v


def _bias_spec():
    return pl.BlockSpec((1, ATT_STEP_HEADS, ATT_TQ, 3 * ATT_TQ), lambda hp, m: (jnp.minimum(m, 2), hp, 0, 0))


def _attn_fwd(proj, bias, name):
    s = proj.shape[0]
    nt = s // ATT_TQ

    def body(q_ref, k0, k1, k2, v0, v1, v2, b_ref, o_ref):
        k, v = _attn_window(k0, k1, k2, v0, v1, v2)
        q = (q_ref[...] * ATT_SCALE).astype(BF16)
        for hh in range(ATT_STEP_HEADS):
            cols = slice(hh * ATT_HEAD_DIM, (hh + 1) * ATT_HEAD_DIM)
            e, total = _attn_exp(q[:, cols], k[:, cols], b_ref[0, hh])
            o = jnp.dot(e.astype(BF16), v[:, cols], preferred_element_type=F32) / total
            o_ref[:, cols] = o.astype(o_ref.dtype)

    specs = _attn_specs(nt) + [_bias_spec()]
    return pl.pallas_call(
        body,
        name=name,
        grid=(ATT_HEADS // ATT_STEP_HEADS, nt),
        in_specs=specs,
        out_specs=pl.BlockSpec((ATT_TQ, ATT_STEP_COLS), lambda hp, m: (m, hp)),
        out_shape=jax.ShapeDtypeStruct((s, ATT_WIDTH), BF16),
        compiler_params=_cparams(dimension_semantics=("parallel", "arbitrary")),
    )(proj, proj, proj, proj, proj, proj, proj, bias)


def _attn_bwd(proj, bias, do, name):
    s = proj.shape[0]
    nt = s // ATT_TQ
    win = 3 * ATT_TQ

    def body(q_ref, k0, k1, k2, v0, v1, v2, do_ref, b_ref, dq_ref, dk_ref, dv_ref, db_ref, dk_acc, dv_acc):
        m = pl.program_id(1)

        @pl.when(m == 0)
        def _():
            dk_acc[...] = jnp.zeros_like(dk_acc)
            dv_acc[...] = jnp.zeros_like(dv_acc)
            db_ref[...] = jnp.zeros_like(db_ref)

        @pl.when(m < nt)
        def _():
            k, v = _attn_window(k0, k1, k2, v0, v1, v2)
            q = (q_ref[...] * ATT_SCALE).astype(BF16)
            dout = do_ref[...]
            for hh in range(ATT_STEP_HEADS):
                cols = slice(hh * ATT_HEAD_DIM, (hh + 1) * ATT_HEAD_DIM)
                for half in range(2):
                    rows = slice(half * ATT_HALF, (half + 1) * ATT_HALF)
                    keys = slice(half * ATT_HALF, half * ATT_HALF + ATT_HALF_KEYS)
                    qh, doh = q[rows, cols], dout[rows, cols]
                    kh, vh = k[keys, cols], v[keys, cols]
                    e, total = _attn_exp(qh, kh, b_ref[0, hh, rows, keys])
                    p = e / total
                    dvh = lax.dot_general(p.astype(BF16), doh, _DIMS["tn"], preferred_element_type=F32)
                    dp = lax.dot_general(doh, vh, _DIMS["nt"], preferred_element_type=F32)
                    ds = p * (dp - jnp.sum(dp * p, axis=-1, keepdims=True))
                    db_ref[hh, rows, keys] += ds
                    dsb = ds.astype(BF16)
                    dqh = jnp.dot(dsb, kh, preferred_element_type=F32) * ATT_SCALE
                    dkh = lax.dot_general(dsb, qh, _DIMS["tn"], preferred_element_type=F32)
                    dq_ref[rows, cols] = dqh.astype(dq_ref.dtype)
                    dk_acc[keys, cols] += dkh
                    dv_acc[keys, cols] += dvh

        dk_ref[...] = dk_acc[:ATT_TQ].astype(dk_ref.dtype)
        dv_ref[...] = dv_acc[:ATT_TQ].astype(dv_ref.dtype)
        for acc in (dk_acc, dv_acc):
            rest = acc[ATT_TQ:]
            acc[: win - ATT_TQ] = rest
            acc[win - ATT_TQ :] = jnp.zeros((ATT_TQ, ATT_STEP_COLS), F32)

    blk = (ATT_TQ, ATT_STEP_COLS)
    specs = _attn_specs(nt)
    specs.append(pl.BlockSpec(blk, lambda hp, m: (jnp.minimum(m, nt - 1), hp)))
    specs.append(_bias_spec())
    done = lambda hp, m: (jnp.maximum(m - 2, 0), hp)
    out_specs = [
        pl.BlockSpec(blk, lambda hp, m: (jnp.minimum(m, nt - 1), hp)),
        pl.BlockSpec(blk, done),
        pl.BlockSpec(blk, done),
        pl.BlockSpec((ATT_STEP_HEADS, ATT_TQ, win), lambda hp, m: (hp, 0, 0)),
    ]
    out_shape = [jax.ShapeDtypeStruct((s, ATT_WIDTH), BF16)] * 3
    out_shape.append(jax.ShapeDtypeStruct((ATT_HEADS, ATT_TQ, win), F32))
    return pl.pallas_call(
        body,
        name=name,
        grid=(ATT_HEADS // ATT_STEP_HEADS, nt + 2),
        in_specs=specs,
        out_specs=out_specs,
        out_shape=out_shape,
        scratch_shapes=[pltpu.VMEM((win, ATT_STEP_COLS), F32), pltpu.VMEM((win, ATT_STEP_COLS), F32)],
        compiler_params=_cparams(dimension_semantics=("arbitrary", "arbitrary")),
    )(proj, proj, proj, proj, proj, proj, proj, do, bias)


def _ada_fwd(c_all, w, name):
    def body(c_ref, w_ref, o_ref):
        act = _silu(c_ref[...]).astype(BF16)
        o_ref[...] = jnp.dot(act, w_ref[...].astype(BF16), preferred_element_type=F32)

    return pl.pallas_call(
        body, name=name, out_shape=jax.ShapeDtypeStruct((c_all.shape[0], w.shape[1]), F32), compiler_params=_cparams()
    )(c_all, w)


def _ada_bwd(c_all, dmod, name):
    def body(c_ref, d_ref, o_ref):
        act = _silu(c_ref[...])
        o_ref[...] = lax.dot_general(act, d_ref[...], _DIMS["tn"], preferred_element_type=F32,
                                     precision=lax.Precision.HIGHEST)

    return pl.pallas_call(
        body, name=name, out_shape=jax.ShapeDtypeStruct((c_all.shape[1], dmod.shape[1]), F32), compiler_params=_cparams()
    )(c_all, dmod)


def _adamw_parts(landed, sent, me, w, m, v, name, rows=256):
    r, c = w.shape
    tr = _pick(r, rows, 16)

    def body(me_ref, g_ref, own_ref, w_ref, m_ref, v_ref, go_ref, d_ref, mo_ref, vo_ref):
        mine = me_ref[0]
        grad = jnp.zeros((tr, c), F32)
        for d in range(N_DEV):
            grad = grad + jnp.where(mine == d, own_ref[0], g_ref[d]).astype(F32)
        _adamw_update(grad, w_ref, m_ref, v_ref, go_ref, d_ref, mo_ref, vo_ref)

    spec = pl.BlockSpec((tr, c), lambda i, me_ref: (i, 0))
    return pl.pallas_call(
        body,
        name=name,
        grid_spec=pltpu.PrefetchScalarGridSpec(
            num_scalar_prefetch=1,
            grid=(r // tr,),
            in_specs=[pl.BlockSpec((N_DEV, tr, c), lambda i, me_ref: (0, i, 0)),
                      pl.BlockSpec((1, tr, c), lambda i, me_ref: (me_ref[0], i, 0)), spec, spec, spec],
            out_specs=[spec] * 4,
        ),
        out_shape=[jax.ShapeDtypeStruct((r, c), F32)] * 4,
        compiler_params=_cparams(dimension_semantics=("parallel",)),
    )(me.reshape(1).astype(jnp.int32), landed, sent, w, m, v)


def _adamw_update(grad, w_ref, m_ref, v_ref, go_ref, d_ref, mo_ref, vo_ref):
    m2 = ADAM_B1 * m_ref[...] + (1.0 - ADAM_B1) * grad
    v2 = ADAM_B2 * v_ref[...] + (1.0 - ADAM_B2) * (grad * grad)
    m_hat = m2 / (1.0 - ADAM_B1**ADAM_STEP)
    v_hat = v2 / (1.0 - ADAM_B2**ADAM_STEP)
    go_ref[...] = grad
    d_ref[...] = -ADAM_LR * (m_hat / (jnp.sqrt(v_hat) + ADAM_EPS) + ADAM_WD * w_ref[...])
    mo_ref[...] = m2
    vo_ref[...] = v2


def _adamw(g, w, m, v, name, rows=256):
    r, c = w.shape
    tr = _pick(r, rows, 16)

    def body(g_ref, w_ref, m_ref, v_ref, go_ref, d_ref, mo_ref, vo_ref):
        _adamw_update(g_ref[...], w_ref, m_ref, v_ref, go_ref, d_ref, mo_ref, vo_ref)

    spec = pl.BlockSpec((tr, c), lambda i: (i, 0))
    return pl.pallas_call(
        body,
        name=name,
        grid=(r // tr,),
        in_specs=[spec, spec, spec, spec],
        out_specs=[spec] * 4,
        out_shape=[jax.ShapeDtypeStruct((r, c), F32)] * 4,
        compiler_params=_cparams(dimension_semantics=("parallel",)),
    )(g, w, m, v)


def _sum_parts(parts, name):
    def body(p_ref, o_ref):
        acc = p_ref[0]
        for d in range(1, N_DEV):
            acc = acc + p_ref[d]
        o_ref[...] = acc

    return pl.pallas_call(
        body, name=name, out_shape=jax.ShapeDtypeStruct(parts.shape[1:], F32), compiler_params=_cparams()
    )(parts)


def _place():
    x, y, c = lax.axis_index("x"), lax.axis_index("y"), lax.axis_index("c")
    return x, y, c


def _dev_index(p):
    return 4 * p[0] + 2 * p[1] + p[2]


def _allgather_vmem(shard, name):
    m_per, n = shard.shape

    def body(x_ref, out_ref, send_sems, recv_sems, local_sem):
        x, y, c = _place()
        me, sibling = (x, y, c), (x, y, 1 - c)
        chips = [(1 - x, y), (x, 1 - y), (1 - x, 1 - y)]

        def rows(p):
            return out_ref.at[pl.ds(_dev_index(p) * m_per, m_per), :]

        def copy(k, block, to, src=None):
            return pltpu.make_async_remote_copy(
                src_ref=rows(block) if src is None else src, dst_ref=rows(block),
                send_sem=send_sems.at[k], recv_sem=recv_sems.at[k], device_id=to, device_id_type=MESH)

        mine = pltpu.make_async_copy(x_ref, rows(me), local_sem)
        mine.start()
        first = [copy(0, me, sibling, src=x_ref)]
        first += [copy(1 + j, me, (*chip, c), src=x_ref) for j, chip in enumerate(chips)]
        for cp in first:
            cp.start()
        passed = [copy(4 + j, (*chip, c), sibling) for j, chip in enumerate(chips)]
        for j, chip in enumerate(chips):
            copy(1 + j, (*chip, c), me).wait_recv()
            passed[j].start()
        copy(0, sibling, me).wait_recv()
        for j, chip in enumerate(chips):
            copy(4 + j, (*chip, 1 - c), me).wait_recv()
        for cp in first + passed:
            cp.wait_send()
        mine.wait()

    return pl.pallas_call(
        body,
        name=name,
        out_shape=jax.ShapeDtypeStruct((N_DEV * m_per, n), shard.dtype),
        in_specs=[pl.BlockSpec(memory_space=pltpu.VMEM)],
        out_specs=pl.BlockSpec(memory_space=pltpu.VMEM),
        scratch_shapes=[pltpu.SemaphoreType.DMA((7,)), pltpu.SemaphoreType.DMA((7,)), pltpu.SemaphoreType.DMA],
        compiler_params=_cparams(),
    )(shard)


def _allgather_hbm(shards, name):
    n = len(shards)

    def body(*refs):
        ins, outs = refs[:n], refs[n : 2 * n]
        send_sems, recv_sems, local_sems = refs[2 * n :]
        x, y, c = _place()
        me, sibling = (x, y, c), (x, y, 1 - c)
        chips = [(1 - x, y), (x, 1 - y), (1 - x, 1 - y)]

        def copy(a, k, block, to, src=None):
            dst = outs[a].at[_dev_index(block)]
            return pltpu.make_async_remote_copy(
                src_ref=dst if src is None else src, dst_ref=dst,
                send_sem=send_sems.at[a * 7 + k], recv_sem=recv_sems.at[a * 7 + k], device_id=to, device_id_type=MESH)

        mine = [pltpu.make_async_copy(ins[a], outs[a].at[_dev_index(me)], local_sems.at[a]) for a in range(n)]
        for cp in mine:
            cp.start()
        first = []
        for a in range(n):
            first.append(copy(a, 0, me, sibling, src=ins[a]))
            first += [copy(a, 1 + j, me, (*chip, c), src=ins[a]) for j, chip in enumerate(chips)]
        for cp in first:
            cp.start()
        passed = []
        for j, chip in enumerate(chips):
            for a in range(n):
                copy(a, 1 + j, (*chip, c), me).wait_recv()
                cp = copy(a, 4 + j, (*chip, c), sibling)
                cp.start()
                passed.append(cp)
        for a in range(n):
            copy(a, 0, sibling, me).wait_recv()
        for j, chip in enumerate(chips):
            for a in range(n):
                copy(a, 4 + j, (*chip, 1 - c), me).wait_recv()
        for cp in first + passed:
            cp.wait_send()
        for cp in mine:
            cp.wait()

    any_spec = pl.BlockSpec(memory_space=pl.ANY)
    return pl.pallas_call(
        body,
        name=name,
        out_shape=[jax.ShapeDtypeStruct((N_DEV, *s.shape), s.dtype) for s in shards],
        in_specs=[any_spec] * n,
        out_specs=[any_spec] * n,
        scratch_shapes=[pltpu.SemaphoreType.DMA((7 * n,)), pltpu.SemaphoreType.DMA((7 * n,)),
                        pltpu.SemaphoreType.DMA((n,))],
        compiler_params=_cparams(),
    )(*shards)


HBM_SPEC = pl.BlockSpec(memory_space=pltpu.HBM)
SEM_SPEC = pl.BlockSpec(memory_space=pltpu.SEMAPHORE)
EFFECT = pltpu.SideEffectType.DATAFLOW_SIDE_EFFECTING


def _peers(x, y, c):
    return [(1 - x if k & 4 else x, 1 - y if k & 2 else y, 1 - c if k & 1 else c) for k in range(1, N_DEV)]


def _push_peers(mode, x, y, c):
    if mode == "all":
        return _peers(x, y, c)
    return [(x, y, 1 - c), (1 - x, y, c), (x, 1 - y, c), (1 - x, 1 - y, c)]


def _push_start(groups, sliced, name, after=(), modes=None):
    flat = [b for g in groups for b in g]
    n, ng = len(flat), len(groups)
    sizes = [len(g) for g in groups]
    modes = modes or ["all"] * ng
    fan = [len(_push_peers(m, 0, 0, 0)) for m in modes]
    per = 2 if sliced else 3
    lands = [lax.empty(b.shape if sliced else (N_DEV, *b.shape), b.dtype) for b in flat]

    def body(*refs):
        ins, lnd = refs[:n], refs[n : 2 * n]
        sems = refs[2 * n + len(after) : 2 * n + len(after) + per * ng]
        token = refs[-1]
        x, y, c = _place()
        me = _dev_index((x, y, c))
        if not sliced:
            first = 0
            for gi, size in enumerate(sizes):
                for j in range(first, first + size):
                    pltpu.make_async_copy(ins[j], lnd[j].at[me], sems[per * gi + 2].at[j - first]).start()
                first += size
        first = 0
        for gi, size in enumerate(sizes):
            for k, peer in enumerate(_push_peers(modes[gi], x, y, c)):
                for j in range(first, first + size):
                    sem = (j - first) * fan[gi] + k
                    pltpu.make_async_remote_copy(
                        src_ref=ins[j].at[_dev_index(peer)] if sliced else ins[j], dst_ref=lnd[j].at[me],
                        send_sem=sems[per * gi].at[sem], recv_sem=sems[per * gi + 1].at[sem],
                        device_id=peer, device_id_type=MESH).start()
            first += size
        token[...] = jnp.zeros_like(token)

    out_shape = []
    for size, width in zip(sizes, fan):
        out_shape += [pltpu.SemaphoreType.DMA((width * size,)), pltpu.SemaphoreType.DMA((width * size,))]
        out_shape += [] if sliced else [pltpu.SemaphoreType.DMA((size,))]
    out_shape += [pltpu.HBM(b.shape, b.dtype) for b in flat + lands]
    out_shape.append(jax.ShapeDtypeStruct((V7X_SUBLANES, V7X_LANES), F32))
    res = pl.pallas_call(
        body,
        name=name,
        out_shape=tuple(out_shape),
        in_specs=[HBM_SPEC] * (2 * n) + [ANY_SPEC] * len(after),
        out_specs=tuple([SEM_SPEC] * (per * ng) + [HBM_SPEC] * (2 * n) + [pl.BlockSpec(memory_space=pltpu.VMEM)]),
        input_output_aliases={i: per * ng + i for i in range(2 * n)},
        compiler_params=pltpu.CompilerParams(has_side_effects=EFFECT),
    )(*[pltpu.with_memory_space_constraint(b, pltpu.HBM) for b in flat + lands], *after)
    sems, thru, token = res[: per * ng], res[per * ng : per * ng + 2 * n], res[-1]
    out, first = [], 0
    for gi, size in enumerate(sizes):
        out.append((sems[per * gi], sems[per * gi + 1], list(thru[first : first + size]),
                    list(thru[n + first : n + first + size]), None if sliced else sems[per * gi + 2]))
        first += size
    return out, token


def _push_wait(started, sliced, after, name, mode="all"):
    send_sems, recv_sems, bufs, lands, own_sems = started
    n = len(bufs)
    fan = len(_push_peers(mode, 0, 0, 0))
    own = [] if own_sems is None else [own_sems]

    def body(*refs):
        ins, lnd = refs[:n], refs[n : 2 * n]
        send_ref, recv_ref = refs[2 * n], refs[2 * n + 1]
        x, y, c = _place()
        for k, peer in enumerate(_push_peers(mode, x, y, c)):
            for j in range(n):
                cp = pltpu.make_async_remote_copy(
                    src_ref=ins[j].at[_dev_index(peer)] if sliced else ins[j], dst_ref=lnd[j].at[_dev_index(peer)],
                    send_sem=send_ref.at[j * fan + k], recv_sem=recv_ref.at[j * fan + k],
                    device_id=peer, device_id_type=MESH)
                cp.wait_send()
                cp.wait_recv()
        if own:
            for j in range(n):
                pltpu.make_async_copy(ins[j], lnd[j].at[_dev_index((x, y, c))], refs[2 * n + 2].at[j]).wait()

    res = pl.pallas_call(
        body,
        name=name,
        out_shape=tuple(pltpu.HBM(b.shape, b.dtype) for b in bufs + lands),
        in_specs=[HBM_SPEC] * (2 * n) + [SEM_SPEC] * (2 + len(own)) + [pl.BlockSpec(memory_space=pl.ANY)],
        out_specs=tuple([HBM_SPEC] * (2 * n)),
        input_output_aliases={i: i for i in range(2 * n)},
        compiler_params=pltpu.CompilerParams(has_side_effects=EFFECT),
    )(*bufs, *lands, send_sems, recv_sems, *own, after)
    return list(res[:n]), list(res[n:])


def _forward_copies(lnd, send_ref, recv_ref, incoming):
    x, y, c = _place()
    copies = []
    for k, chip in enumerate([(1 - x, y), (x, 1 - y), (1 - x, 1 - y)]):
        mine, theirs = _dev_index((*chip, c)), _dev_index((*chip, 1 - c))
        for j, ref in enumerate(lnd):
            copies.append(pltpu.make_async_remote_copy(
                src_ref=ref.at[mine], dst_ref=ref.at[theirs if incoming else mine],
                send_sem=send_ref.at[j * 3 + k], recv_sem=recv_ref.at[j * 3 + k],
                device_id=(x, y, 1 - c), device_id_type=MESH))
    return copies


def _forward_start(lands, name):
    n = len(lands)

    def body(*refs):
        for cp in _forward_copies(refs[:n], refs[n], refs[n + 1], False):
            cp.start()

    res = pl.pallas_call(
        body,
        name=name,
        out_shape=(pltpu.SemaphoreType.DMA((3 * n,)), pltpu.SemaphoreType.DMA((3 * n,)),
                   *[pltpu.HBM(b.shape, b.dtype) for b in lands]),
        in_specs=[HBM_SPEC] * n,
        out_specs=(SEM_SPEC, SEM_SPEC, *[HBM_SPEC] * n),
        input_output_aliases={i: 2 + i for i in range(n)},
        compiler_params=pltpu.CompilerParams(has_side_effects=EFFECT),
    )(*[pltpu.with_memory_space_constraint(b, pltpu.HBM) for b in lands])
    return res[0], res[1], list(res[2:])


def _forward_wait(started, after, name):
    send_sems, recv_sems, lands = started
    n = len(lands)

    def body(*refs):
        for cp in _forward_copies(refs[:n], refs[n], refs[n + 1], True):
            cp.wait_send()
            cp.wait_recv()

    res = pl.pallas_call(
        body,
        name=name,
        out_shape=tuple(pltpu.HBM(b.shape, b.dtype) for b in lands),
        in_specs=[HBM_SPEC] * n + [SEM_SPEC, SEM_SPEC, pl.BlockSpec(memory_space=pl.ANY)],
        out_specs=tuple([HBM_SPEC] * n),
        input_output_aliases={i: i for i in range(n)},
        compiler_params=pltpu.CompilerParams(has_side_effects=EFFECT),
    )(*lands, send_sems, recv_sems, after)
    return list(res)


def _cols_full(g):
    return jnp.transpose(g, (1, 0, 2)).reshape(g.shape[1], -1)


def _rows_full(g):
    return g.reshape(-1, g.shape[2])


def _cols_parts(full, n=N_DEV):
    r = full.shape[0]
    return jnp.transpose(full.reshape(r, n, -1), (1, 0, 2)).astype(BF16)


def _rows_parts(full):
    return full.reshape(N_DEV, -1, full.shape[1]).astype(BF16)


def _pad_rows(v, rows):
    flat = v.reshape(-1)
    return jnp.pad(flat, (0, rows * D_MODEL - flat.shape[0])).reshape(rows, D_MODEL)


def _my_cols(full, me, width):
    return lax.dynamic_slice_in_dim(full, me * width, width, axis=full.ndim - 1)


def kernel(x, c, w_ada, b_ada, norm_pre, norm_post, ffn1_w_gu, ffn1_w_down, w_in, rel_bias, conv_w, conv_b, lru_wa, lru_ba, lru_wx, lru_bx, lru_lambda, w_att_o, w_rec_o, w_out, ffn2_w_gu, ffn2_w_down, loss_target, m_w_ada, m_b_ada, m_norm_pre, m_norm_post, m_ffn1_w_gu, m_ffn1_w_down, m_w_in, m_rel_bias, m_conv_w, m_conv_b, m_lru_wa, m_lru_ba, m_lru_wx, m_lru_bx, m_lru_lambda, m_w_att_o, m_w_rec_o, m_w_out, m_ffn2_w_gu, m_ffn2_w_down, v_w_ada, v_b_ada, v_norm_pre, v_norm_post, v_ffn1_w_gu, v_ffn1_w_down, v_w_in, v_rel_bias, v_conv_w, v_conv_b, v_lru_wa, v_lru_ba, v_lru_wx, v_lru_bx, v_lru_lambda, v_w_att_o, v_w_rec_o, v_w_out, v_ffn2_w_gu, v_ffn2_w_down):
    weights = dict(w_ada=w_ada, b_ada=b_ada, norm_pre=norm_pre, norm_post=norm_post, ffn1_w_gu=ffn1_w_gu,
                   ffn1_w_down=ffn1_w_down, w_in=w_in, rel_bias=rel_bias, conv_w=conv_w, conv_b=conv_b,
                   lru_wa=lru_wa, lru_ba=lru_ba, lru_wx=lru_wx, lru_bx=lru_bx, lru_lambda=lru_lambda,
                   w_att_o=w_att_o, w_rec_o=w_rec_o, w_out=w_out, ffn2_w_gu=ffn2_w_gu, ffn2_w_down=ffn2_w_down)
    mom1 = dict(w_ada=m_w_ada, b_ada=m_b_ada, norm_pre=m_norm_pre, norm_post=m_norm_post, ffn1_w_gu=m_ffn1_w_gu,
                ffn1_w_down=m_ffn1_w_down, w_in=m_w_in, rel_bias=m_rel_bias, conv_w=m_conv_w, conv_b=m_conv_b,
                lru_wa=m_lru_wa, lru_ba=m_lru_ba, lru_wx=m_lru_wx, lru_bx=m_lru_bx, lru_lambda=m_lru_lambda,
                w_att_o=m_w_att_o, w_rec_o=m_w_rec_o, w_out=m_w_out, ffn2_w_gu=m_ffn2_w_gu, ffn2_w_down=m_ffn2_w_down)
    mom2 = dict(w_ada=v_w_ada, b_ada=v_b_ada, norm_pre=v_norm_pre, norm_post=v_norm_post, ffn1_w_gu=v_ffn1_w_gu,
                ffn1_w_down=v_ffn1_w_down, w_in=v_w_in, rel_bias=v_rel_bias, conv_w=v_conv_w, conv_b=v_conv_b,
                lru_wa=v_lru_wa, lru_ba=v_lru_ba, lru_wx=v_lru_wx, lru_bx=v_lru_bx, lru_lambda=v_lru_lambda,
                w_att_o=v_w_att_o, w_rec_o=v_w_rec_o, w_out=v_w_out, ffn2_w_gu=v_ffn2_w_gu, ffn2_w_down=v_ffn2_w_down)
    order = list(weights)
    big = ["ffn1_w_gu", "ffn1_w_down", "w_in", "w_att_o", "w_rec_o", "w_out", "ffn2_w_gu", "ffn2_w_down"]
    small = ["b_ada", "norm_pre", "norm_post", "rel_bias", "conv_w", "conv_b", "lru_wa", "lru_ba", "lru_wx",
             "lru_bx", "lru_lambda"]

    xi, yi, ci = _place()
    me = _dev_index((xi, yi, ci))
    x0 = x[0]
    target = loss_target[0]
    fuse_tm = min(FUSE_TM, x0.shape[0])

    transposed = {"ffn1_w_gu", "w_in", "ffn2_w_gu"}
    local = lambda n, arr: jnp.transpose(arr[0]) if n in transposed else arr[0]
    shards = {n: local(n, weights[n]).astype(BF16) for n in big}
    full_of = lambda n, g: _cols_full(g) if n == "w_att_o" else _rows_full(g)

    pack = jnp.concatenate([c.reshape(-1), norm_pre.reshape(-1), norm_post.reshape(-1), conv_w.reshape(-1)])
    pack = jnp.pad(pack, (0, 3072 - pack.shape[0])).reshape(8, 384)
    got, w1_gu = _allgather_hbm([pack, shards["ffn1_w_gu"]], "gather_first")
    got = got.reshape(N_DEV, 3072)
    c_all = got[:, :1024]
    unshard = lambda blk, rows: jnp.transpose(blk.reshape(N_DEV, rows, 128), (1, 0, 2)).reshape(rows, D_MODEL)
    g_pre = unshard(got[:, 1024:1408], 3)
    g_post = unshard(got[:, 1408:1792], 3)
    conv_taps = unshard(got[:, 1792:2304], 4)
    conv_w8 = jnp.concatenate([conv_taps, jnp.zeros((4, LRU_WIDTH), F32)], axis=0)

    mod_cols = _ada_fwd(c_all, w_ada[0], "ada_fwd")
    mod_all = _allgather_vmem(mod_cols, "gather_mod").reshape(N_DEV, N_DEV, 1152)
    mod = lax.dynamic_index_in_dim(mod_all, me, axis=1, keepdims=False).reshape(1, -1) + b_ada
    mod = mod.reshape(3, 3, 1, D_MODEL)

    w_slab = _slab_weights(lru_wa[0], lru_wx[0])
    bias = _bias_tile(rel_bias[0], "bias_tile")

    res_w = (0.5, 1.0, 0.5)
    row = lambda v: v.reshape(1, -1)

    weight_groups = [["ffn1_w_down"], ["w_in"], ["w_att_o", "w_rec_o", "w_out"], ["ffn2_w_gu", "ffn2_w_down"]]
    weight_modes = ["all", "chip", "all", "all"]
    weights_started, started = _push_start([[shards[n] for n in g] for g in weight_groups], False,
                                           "gather_weights_start", after=(mod, w1_gu), modes=weight_modes)
    full = {"ffn1_w_gu": _rows_full(w1_gu)}

    def gathered_group(gi, after):
        sent, lands = _push_wait(weights_started[gi], False, after, f"gather_weights_wait{gi}", mode=weight_modes[gi])
        if weight_modes[gi] == "chip":
            lands = _forward_wait(_forward_start(lands, f"gather_weights_forward{gi}"), sent[0],
                                  f"gather_weights_forward_wait{gi}")
        for n, land in zip(weight_groups[gi], lands):
            full[n] = full_of(n, land)

    def ffn_fwd(xin, k, gi, tag, deps=(), target=None):
        h, a, g, u = _pre_up(xin, row(g_pre[k]), mod[k, 0], mod[k, 1], full[f"{tag}_w_gu"], f"{tag}_up", deps=deps)
        if f"{tag}_w_down" not in full:
            gathered_group(gi, a)
        f, *out = _matmul_post(a, full[f"{tag}_w_down"], xin, row(g_post[k]), mod[k, 2], res_w[k], f"{tag}_down",
                               target=target)
        return (out[0] if target is None else out), (h, g, u, a, f)

    x1, saved1 = ffn_fwd(x0, 0, 0, "ffn1", deps=(started,))

    gathered_group(1, x1)
    h2, proj = _pre_matmul(x1, row(g_pre[1]), mod[1, 0], mod[1, 1], full["w_in"], "mix_in",
                           b_shift=3 * ATT_WIDTH // 512)
    att_o = _attn_fwd(proj, bias, "attn_fwd")
    gathered_group(2, att_o)
    xc, pre, a_t, u_t = _lru_front(proj, conv_w8, conv_b, w_slab, lru_ba, lru_bx, lru_lambda, "lru_front")
    hs, h_prev, rec_in = _scan_fwd(a_t, u_t, proj, "lru_scan")
    att = _matmul(att_o, full["w_att_o"], "nn", F32, "att_out")
    rec = _matmul(rec_in, full["w_rec_o"], "nn", F32, "rec_out")
    merged, f2, x2 = _merge_matmul_post(att, rec, proj, full["w_out"], x1, row(g_post[1]), mod[1, 2], res_w[1],
                                        "mix_out")

    gathered_group(3, x2)
    (dy, sq), saved3 = ffn_fwd(x2, 2, 2, "ffn2", target=target)
    loss = lax.psum(0.5 * jnp.sum(sq) / D_MODEL, ("x", "y", "c"))

    grads = {}
    norm_sums = [None] * 6

    pending = []

    def exchange_start(names, tag, after=()):
        send = [(_cols_parts if n == "w_att_o" else _rows_parts)(grads[n]) for n in names]
        (group,), token = _push_start([send], True, f"exchange_{tag}_start", after=after)
        pending.append((names, send, group, tag))
        return token

    def exchange_finish(names, send, group, tag, after):
        sent, lands = _push_wait(group, True, after, f"exchange_{tag}_wait")
        res = None
        for n, land, mine in zip(names, lands, sent):
            res = _adamw_parts(land, mine, me, local(n, weights[n]), local(n, mom1[n]), local(n, mom2[n]),
                               f"adamw_{n}")
            back = (lambda r: jnp.transpose(r)) if n in transposed else (lambda r: r)
            out_g[n], out_d[n], out_m[n], out_v[n] = [back(r).reshape(weights[n].shape) for r in res]
        return res[0]

    out_g, out_d, out_m, out_v = {}, {}, {}, {}

    def ffn_bwd(xin, k, saved, dout, tag):
        h, g, u, a, f = saved
        w_gu, w_down = f"{tag}_w_gu", f"{tag}_w_down"
        df, dgu, norm_sums[2 * k + 1] = _post_bwd_up_bwd(f, dout, row(g_post[k]), mod[k, 2], res_w[k], full[w_down],
                                                          g, u, f"{tag}_up_bwd")
        grads[w_down] = _matmul(a, df, "tn", BF16, f"{tag}_dw_down", tm=1408, tn=1024, tk=DW_TK)
        started = exchange_start([w_down], w_down)
        grads[w_gu] = _dw_gu(dgu, h, f"{tag}_dw_gu", deps=(started,))
        started = exchange_start([w_gu], w_gu)
        halves = [(dgu, (None, fuse_tm, D_FF), lambda i, half=half: (half, i, 0), (half * D_FF, (half + 1) * D_FF))
                  for half in range(2)]
        dx, norm_sums[2 * k] = _matmul_pre_bwd(halves, full[w_gu], xin, dout, row(g_pre[k]),
                                                                mod[k, 0], mod[k, 1], f"{tag}_dh", deps=(started,))
        return dx

    dx2 = ffn_bwd(x2, 2, saved3, dy, "ffn2")

    df2, datt, drec, dg_att, dg_rec, norm_sums[3] = _post_bwd_merge_bwd(
        f2, dx2, row(g_post[1]), mod[1, 2], res_w[1], full["w_out"], att, rec, proj, "mix_dmerged")
    grads["w_out"] = _matmul(merged, df2, "tn", BF16, "mix_dw_out", tm=1024, tn=1024, tk=DW_TK)
    datt_o = _matmul(datt, full["w_att_o"], "nt", BF16, "att_out_bwd")
    grads["w_att_o"] = _matmul(att_o, datt, "tn", BF16, "dw_att_o", tm=512, tn=1024, tk=DW_TK)
    grads["w_rec_o"] = _matmul(rec_in, drec, "tn", BF16, "dw_rec_o", tm=1024, tn=1024, tk=DW_TK)
    started = exchange_start(["w_out", "w_att_o", "w_rec_o"], "mix_out")
    dhs, dyr = _matmul_recin_bwd(drec, full["w_rec_o"], hs, proj, "rec_out_bwd", deps=(started,))
    g_t = _scan_bwd(a_t, dhs, "lru_scan_bwd")
    dpre, dxc, lru_sums = _lru_back(pre, xc, w_slab, lru_ba, lru_bx, lru_lambda, g_t, h_prev, "lru_back")
    dxr, conv_sums = _conv_bwd(proj, conv_w8, dxc, "conv_bwd")
    dq, dk, dv, dbias = _attn_bwd(proj, bias, datt_o, "attn_bwd")
    dproj = jnp.concatenate([dq, dk, dv, dxr, dyr, dg_att, dg_rec], axis=1)
    grads["w_in"] = _matmul(dproj, h2, "tn", BF16, "mix_dw_in", tm=1408, tn=1024, tk=DW_TK)
    pack_mix = jnp.concatenate([conv_sums, lru_sums, _pad_rows(_bias_grad(dbias, "bias_grad"), V7X_SUBLANES),
                                _lru_dw(xc, dpre, "lru_dw").reshape(128, D_MODEL)], axis=0)
    (mix_started,), started = _push_start([[pack_mix]], False, "small_grads_mix_start")
    started = exchange_start(["w_in"], "w_in", after=(started,))
    whole = [(dproj, (fuse_tm, PROJ_WIDTH), lambda i: (i, 0), (0, PROJ_WIDTH))]
    dx1, norm_sums[2] = _matmul_pre_bwd(whole, full["w_in"], x1, dx2, row(g_pre[1]), mod[1, 0],
                                                             mod[1, 1], "mix_dh", deps=(started,))

    dx0 = ffn_bwd(x0, 0, saved1, dx1, "ffn1")

    pack_norm = jnp.concatenate(norm_sums, axis=0)
    (norm_started,), _ = _push_start([[pack_norm]], False, "small_grads_norm_start")

    def summed(started, pack, after, tag):
        _, (parts,) = _push_wait(started, False, after, f"small_grads_{tag}_wait")
        return parts, _sum_parts(parts, f"small_grads_{tag}_sum")

    done = dx0
    for names, send, group, tag in pending:
        done = exchange_finish(names, send, group, tag, done)

    _, total = summed(mix_started, pack_mix, done, "mix")
    grads["conv_w"] = _my_cols(total[0:4], me, 128)
    grads["conv_b"] = total[4:5]
    grads["lru_ba"] = total[8:9]
    grads["lru_bx"] = total[9:10]
    grads["lru_lambda"] = total[10:11]
    grads["rel_bias"] = total[16:19].reshape(-1)[: ATT_HEADS * (2 * MAX_REL + 1)].reshape(ATT_HEADS, -1)
    grads["lru_wa"] = total[24:88].reshape(LRU_BLOCKS, LRU_BLOCK, LRU_BLOCK)
    grads["lru_wx"] = total[88:152].reshape(LRU_BLOCKS, LRU_BLOCK, LRU_BLOCK)
    parts, total = summed(norm_started, pack_norm, total, "norm")
    by_sandwich = lambda v: v.reshape(*v.shape[:-2], 3, 2 * V7X_SUBLANES, D_MODEL)
    dmod_of = lambda v: jnp.concatenate([by_sandwich(v)[..., 1:3, :], by_sandwich(v)[..., 9:10, :]], axis=-2)
    grads["b_ada"] = dmod_of(total).reshape(1, -1)
    grads["norm_pre"] = _my_cols(by_sandwich(total)[:, 0, :], me, 128)
    grads["norm_post"] = _my_cols(by_sandwich(total)[:, V7X_SUBLANES, :], me, 128)
    dmod_all = dmod_of(parts).reshape(N_DEV, 9 * D_MODEL)
    grads["w_ada"] = _ada_bwd(c_all, _my_cols(dmod_all, me, 1152), "ada_bwd")

    res = _adamw(grads["w_ada"], w_ada[0], m_w_ada[0], v_w_ada[0], "adamw_w_ada")
    out_g["w_ada"], out_d["w_ada"], out_m["w_ada"], out_v["w_ada"] = [r.reshape(w_ada.shape) for r in res]

    sizes = [int(np.prod(weights[n].shape)) for n in small]
    tot = sum(sizes)
    rows_small = -(-tot // (16 * D_MODEL)) * 16
    flat = lambda arrs: jnp.pad(jnp.concatenate([a.reshape(-1) for a in arrs]),
                                (0, rows_small * D_MODEL - tot)).reshape(rows_small, D_MODEL)
    res = _adamw(flat([grads[n] for n in small]), flat([weights[n] for n in small]),
                 flat([mom1[n] for n in small]), flat([mom2[n] for n in small]), "adamw_small", rows=rows_small)
    offs = np.cumsum([0] + sizes)
    for dst, r in zip((out_g, out_d, out_m, out_v), res):
        rf = r.reshape(-1)
        for i, n in enumerate(small):
            dst[n] = rf[offs[i] : offs[i + 1]].reshape(weights[n].shape)

    return (loss, dx0[None], *[out_g[n] for n in order], *[out_d[n] for n in order],
            *[out_m[n] for n in order], *[out_v[n] for n in order])
```
